```python
import jax, jax.numpy as jnp
from jax import lax
import numpy as np

D_MODEL = 2048
BATCH = 8
SEQ = 4096
DEPTH = 1

D_MIX = D_MODEL
D_A = D_MIX // 2
CHUNK = 128
A_GROUPS = 8
A_GROUP_W = D_A // A_GROUPS
D_B = D_MIX - D_A
HEAD_DIM = 64
N_Q_HEADS = D_B // HEAD_DIM
N_KV_HEADS = 4
Q_PER_KV = N_Q_HEADS // N_KV_HEADS
D_KV = N_KV_HEADS * HEAD_DIM
WINDOW = 128
BLOCK = WINDOW
ROPE_THETA = 10000.0
NORM_EPS = 1e-5
SPLIT_SIZES = (D_A, D_A, D_A, D_B, D_KV, D_KV, D_B)
D_IN = sum(SPLIT_SIZES)

kernel_name = "hymba_gmlp_swa_sink_adaln"


def rms_norm(x, g):
    xf = x.astype(jnp.float32)
    y = xf * lax.rsqrt(jnp.mean(xf * xf, axis=-1, keepdims=True) + NORM_EPS)
    return (y * g.astype(jnp.float32)).astype(x.dtype)


def layer_norm(x, g, b):
    xf = x.astype(jnp.float32)
    mu = jnp.mean(xf, axis=-1, keepdims=True)
    xc = xf - mu
    var = jnp.mean(xc * xc, axis=-1, keepdims=True)
    y = xc * lax.rsqrt(var + NORM_EPS)
    return (y * g.astype(jnp.float32) + b.astype(jnp.float32)).astype(x.dtype)


def modulate(h, shift, scale):
    return h * (1.0 + scale[:, None, :]) + shift[:, None, :]


def rope_tables(seq, dtype):
    inv_freq = ROPE_THETA ** (-jnp.arange(0, HEAD_DIM, 2, dtype=jnp.float32) / HEAD_DIM)
    ang = jnp.arange(seq, dtype=jnp.float32)[:, None] * inv_freq[None, :]
    return jnp.cos(ang).astype(dtype), jnp.sin(ang).astype(dtype)


def apply_rope(x, cos, sin):
    x1, x2 = jnp.split(x, 2, axis=-1)
    c = cos[None, :, None, :]
    s = sin[None, :, None, :]
    return jnp.concatenate([x1 * c - x2 * s, x2 * c + x1 * s], axis=-1)


def chunked_spatial_gating(u, v, ln_g, ln_b, w_s, b_s):
    bsz, seq, _ = v.shape
    n_chunks = seq // CHUNK
    v = layer_norm(v, ln_g, ln_b)
    vg = v.reshape(bsz, n_chunks, CHUNK, A_GROUPS, A_GROUP_W)
    w = w_s * jnp.tril(jnp.ones((CHUNK, CHUNK), w_s.dtype))[None]
    s = jnp.einsum('gts,bcsgd->bctgd', w, vg) + b_s.T[None, None, :, :, None]
    return u * s.reshape(bsz, seq, D_A)


def sliding_window_sink_attention(q, k, v, sinks):
    bsz, seq = q.shape[0], q.shape[1]
    nb = seq // BLOCK
    qb = q.reshape(bsz, nb, BLOCK, N_KV_HEADS, Q_PER_KV, HEAD_DIM)
    kb = k.reshape(bsz, nb, BLOCK, N_KV_HEADS, HEAD_DIM)
    vb = v.reshape(bsz, nb, BLOCK, N_KV_HEADS, HEAD_DIM)
    pad = ((0, 0), (1, 0), (0, 0), (0, 0), (0, 0))
    k_band = jnp.concatenate([jnp.pad(kb, pad)[:, :-1], kb], axis=2)
    v_band = jnp.concatenate([jnp.pad(vb, pad)[:, :-1], vb], axis=2)
    scale = HEAD_DIM ** -0.5
    scores = jnp.einsum('bnqkgd,bnjkd->bnkgqj', qb, k_band).astype(jnp.float32) * scale
    blk = jnp.arange(nb)[:, None]
    qpos = blk * BLOCK + jnp.arange(BLOCK)[None, :]
    kpos = (blk - 1) * BLOCK + jnp.arange(2 * BLOCK)[None, :]
    rel = qpos[:, :, None] - kpos[:, None, :]
    valid = (rel >= 0) & (rel < WINDOW) & (kpos[:, None, :] >= 0)
    scores = jnp.where(valid[None, :, None, None, :, :], scores, -jnp.inf)
    sink = sinks.astype(jnp.float32).reshape(N_KV_HEADS, Q_PER_KV)[None, None, :, :, None, None]
    m = jnp.maximum(jnp.max(scores, axis=-1, keepdims=True), sink)
    p = jnp.exp(scores - m)
    denom = jnp.sum(p, axis=-1, keepdims=True) + jnp.exp(sink - m)
    probs = (p / denom).astype(v.dtype)
    out = jnp.einsum('bnkgqj,bnjkd->bnqkgd', probs, v_band)
    return out.reshape(bsz, seq, N_Q_HEADS * HEAD_DIM)


def _fwd_setup_inputs(seed: int = 0) -> dict:
    key = jax.random.key(seed)
    ks = jax.random.split(key, 16)
    f32 = jnp.float32
    ada_std = 0.2 * D_MODEL ** -0.5
    return {
        "x": jax.random.normal(ks[0], (BATCH, SEQ, D_MODEL), f32),
        "c": jax.random.normal(ks[1], (BATCH, D_MODEL), f32),
        "w_ada": jax.random.normal(ks[2], (DEPTH, D_MODEL, 3 * D_MODEL), f32) * ada_std,
        "b_ada": jax.random.normal(ks[3], (DEPTH, 3 * D_MODEL), f32) * 0.02,
        "norm_g": 1.0 + 0.02 * jax.random.normal(ks[4], (DEPTH, D_MODEL), f32),
        "w_in": jax.random.normal(ks[5], (DEPTH, D_MODEL, D_IN), f32) * D_MODEL ** -0.5,
        "ln_v_g": 1.0 + 0.02 * jax.random.normal(ks[6], (DEPTH, D_A), f32),
        "ln_v_b": 0.02 * jax.random.normal(ks[7], (DEPTH, D_A), f32),
        "w_spatial": jax.random.normal(ks[8], (DEPTH, A_GROUPS, CHUNK, CHUNK), f32) * (0.5 * CHUNK ** -0.5),
        "b_spatial": 1.0 + 0.02 * jax.random.normal(ks[9], (DEPTH, A_GROUPS, CHUNK), f32),
        "sinks": 0.5 * jax.random.normal(ks[10], (DEPTH, N_Q_HEADS), f32),
        "w_out": jax.random.normal(ks[11], (DEPTH, D_MIX, D_MODEL), f32) * D_MIX ** -0.5,
        "w_ada_final": jax.random.normal(ks[12], (D_MODEL, 2 * D_MODEL), f32) * ada_std,
        "b_ada_final": jax.random.normal(ks[13], (2 * D_MODEL,), f32) * 0.02,
        "final_norm_g": 1.0 + 0.02 * jax.random.normal(ks[14], (D_MODEL,), f32),
    }


def _fwd_reference(x, c, w_ada, b_ada, norm_g, w_in, ln_v_g, ln_v_b, w_spatial, b_spatial,
              sinks, w_out, w_ada_final, b_ada_final, final_norm_g):
    bsz, seq, _ = x.shape
    c_act = jax.nn.silu(c)
    cos, sin = rope_tables(seq, x.dtype)
    offs = np.cumsum((0,) + SPLIT_SIZES)[:-1].tolist()[1:]
    for l in range(DEPTH):
        mod = c_act @ w_ada[l] + b_ada[l]
        shift, scale, gate = jnp.split(mod, 3, axis=-1)
        h = modulate(rms_norm(x, norm_g[l]), shift, scale)
        proj = h @ w_in[l]
        u_a, v_a, z_a, q, k, v, z_b = jnp.split(proj, offs, axis=-1)
        y_a = chunked_spatial_gating(u_a, v_a, ln_v_g[l], ln_v_b[l], w_spatial[l], b_spatial[l])
        q = apply_rope(q.reshape(bsz, seq, N_Q_HEADS, HEAD_DIM), cos, sin)
        k = apply_rope(k.reshape(bsz, seq, N_KV_HEADS, HEAD_DIM), cos, sin)
        v = v.reshape(bsz, seq, N_KV_HEADS, HEAD_DIM)
        y_b = sliding_window_sink_attention(q, k, v, sinks[l])
        y = jnp.concatenate([y_a * jax.nn.silu(z_a), y_b * jax.nn.silu(z_b)], axis=-1)
        x = x + gate[:, None, :] * (y @ w_out[l])
    mod_f = c_act @ w_ada_final + b_ada_final
    shift_f, scale_f = jnp.split(mod_f, 2, axis=-1)
    return modulate(rms_norm(x, final_norm_g), shift_f, scale_f)


import jax as _jax
import jax.numpy as _jnp

TWIN_FORMAT = 'train_step'
FWD_PARAMS = ['x', 'c', 'w_ada', 'b_ada', 'norm_g', 'w_in', 'ln_v_g', 'ln_v_b', 'w_spatial', 'b_spatial', 'sinks', 'w_out', 'w_ada_final', 'b_ada_final', 'final_norm_g']
TWIN_WEIGHTS = ['w_ada', 'b_ada', 'norm_g', 'w_in', 'ln_v_g', 'ln_v_b', 'w_spatial', 'b_spatial', 'sinks', 'w_out', 'w_ada_final', 'b_ada_final', 'final_norm_g']
TWIN_DIFF_INPUT = 'x'
TWIN_INPUTS = ['x', 'c', 'w_ada', 'b_ada', 'norm_g', 'w_in', 'ln_v_g', 'ln_v_b', 'w_spatial', 'b_spatial', 'sinks', 'w_out', 'w_ada_final', 'b_ada_final', 'final_norm_g', 'loss_target', 'm_w_ada', 'm_b_ada', 'm_norm_g', 'm_w_in', 'm_ln_v_g', 'm_ln_v_b', 'm_w_spatial', 'm_b_spatial', 'm_sinks', 'm_w_out', 'm_w_ada_final', 'm_b_ada_final', 'm_final_norm_g', 'v_w_ada', 'v_b_ada', 'v_norm_g', 'v_w_in', 'v_ln_v_g', 'v_ln_v_b', 'v_w_spatial', 'v_b_spatial', 'v_sinks', 'v_w_out', 'v_w_ada_final', 'v_b_ada_final', 'v_final_norm_g']
TWIN_OUTPUTS = ['loss', 'grad_x', 'grad_w_ada', 'grad_b_ada', 'grad_norm_g', 'grad_w_in', 'grad_ln_v_g', 'grad_ln_v_b', 'grad_w_spatial', 'grad_b_spatial', 'grad_sinks', 'grad_w_out', 'grad_w_ada_final', 'grad_b_ada_final', 'grad_final_norm_g', 'delta_w_ada', 'delta_b_ada', 'delta_norm_g', 'delta_w_in', 'delta_ln_v_g', 'delta_ln_v_b', 'delta_w_spatial', 'delta_b_spatial', 'delta_sinks', 'delta_w_out', 'delta_w_ada_final', 'delta_b_ada_final', 'delta_final_norm_g', 'new_m_w_ada', 'new_m_b_ada', 'new_m_norm_g', 'new_m_w_in', 'new_m_ln_v_g', 'new_m_ln_v_b', 'new_m_w_spatial', 'new_m_b_spatial', 'new_m_sinks', 'new_m_w_out', 'new_m_w_ada_final', 'new_m_b_ada_final', 'new_m_final_norm_g', 'new_v_w_ada', 'new_v_b_ada', 'new_v_norm_g', 'new_v_w_in', 'new_v_ln_v_g', 'new_v_ln_v_b', 'new_v_w_spatial', 'new_v_b_spatial', 'new_v_sinks', 'new_v_w_out', 'new_v_w_ada_final', 'new_v_b_ada_final', 'new_v_final_norm_g']
TWIN_LEAF_KINDS = {'loss': 'loss', 'grad_x': 'grad_x', 'grad_w_ada': 'grad_w', 'grad_b_ada': 'grad_w', 'grad_norm_g': 'grad_w', 'grad_w_in': 'grad_w', 'grad_ln_v_g': 'grad_w', 'grad_ln_v_b': 'grad_w', 'grad_w_spatial': 'grad_w', 'grad_b_spatial': 'grad_w', 'grad_sinks': 'grad_w', 'grad_w_out': 'grad_w', 'grad_w_ada_final': 'grad_w', 'grad_b_ada_final': 'grad_w', 'grad_final_norm_g': 'grad_w', 'delta_w_ada': 'delta_w', 'delta_b_ada': 'delta_w', 'delta_norm_g': 'delta_w', 'delta_w_in': 'delta_w', 'delta_ln_v_g': 'delta_w', 'delta_ln_v_b': 'delta_w', 'delta_w_spatial': 'delta_w', 'delta_b_spatial': 'delta_w', 'delta_sinks': 'delta_w', 'delta_w_out': 'delta_w', 'delta_w_ada_final': 'delta_w', 'delta_b_ada_final': 'delta_w', 'delta_final_norm_g': 'delta_w', 'new_m_w_ada': 'new_m', 'new_m_b_ada': 'new_m', 'new_m_norm_g': 'new_m', 'new_m_w_in': 'new_m', 'new_m_ln_v_g': 'new_m', 'new_m_ln_v_b': 'new_m', 'new_m_w_spatial': 'new_m', 'new_m_b_spatial': 'new_m', 'new_m_sinks': 'new_m', 'new_m_w_out': 'new_m', 'new_m_w_ada_final': 'new_m', 'new_m_b_ada_final': 'new_m', 'new_m_final_norm_g': 'new_m', 'new_v_w_ada': 'new_v', 'new_v_b_ada': 'new_v', 'new_v_norm_g': 'new_v', 'new_v_w_in': 'new_v', 'new_v_ln_v_g': 'new_v', 'new_v_ln_v_b': 'new_v', 'new_v_w_spatial': 'new_v', 'new_v_b_spatial': 'new_v', 'new_v_sinks': 'new_v', 'new_v_w_out': 'new_v', 'new_v_w_ada_final': 'new_v', 'new_v_b_ada_final': 'new_v', 'new_v_final_norm_g': 'new_v'}


def _forward(args):
    return _fwd_reference(*[args[k] for k in FWD_PARAMS])


def _output_shape():
    def fwd():
        inp = _fwd_setup_inputs(0)
        return _fwd_reference(*[inp[k] for k in FWD_PARAMS])
    out = _jax.eval_shape(fwd)
    return out.shape, out.dtype

N_MICROBATCH = 1
ADAM_LR = 0.001
ADAM_B1 = 0.9
ADAM_B2 = 0.999
ADAM_EPS = 1e-08
ADAM_WD = 0.01
ADAM_STEP = 10
PER_EXAMPLE_BATCH_AXIS = {'x': 0, 'c': 0, 'loss_target': 0}
SHARED_INPUTS = []
_WEIGHT_DTYPES = {'w_ada': _jnp.float32, 'b_ada': _jnp.float32, 'norm_g': _jnp.float32, 'w_in': _jnp.float32, 'ln_v_g': _jnp.float32, 'ln_v_b': _jnp.float32, 'w_spatial': _jnp.float32, 'b_spatial': _jnp.float32, 'sinks': _jnp.float32, 'w_out': _jnp.float32, 'w_ada_final': _jnp.float32, 'b_ada_final': _jnp.float32, 'final_norm_g': _jnp.float32}
MOMENT_SCALE = {'w_ada': 1.907238e-02, 'b_ada': 3.265742e-02, 'norm_g': 8.311569e-03, 'w_in': 4.808122e-03, 'ln_v_g': 2.354448e-03, 'ln_v_b': 2.451191e-03, 'w_spatial': 4.939088e-03, 'b_spatial': 7.335506e-03, 'sinks': 1.035193e-03, 'w_out': 5.501334e-03, 'w_ada_final': 3.231550e+00, 'b_ada_final': 1.135009e+01, 'final_norm_g': 1.634372e+01}


def _to_microbatches(a, axis):
    t = _jnp.moveaxis(a, axis, 0)
    t = t.reshape((N_MICROBATCH, t.shape[0] // N_MICROBATCH) + t.shape[1:])
    return _jnp.moveaxis(t, 1, axis + 1)


def setup_inputs(seed: int = 0) -> dict:
    inp = _fwd_setup_inputs(seed)
    key = _jax.random.fold_in(_jax.random.key(seed), 7919)
    shape, _ = _output_shape()
    out = dict(inp)
    out["loss_target"] = _jax.random.normal(_jax.random.fold_in(key, 0), shape, _jnp.float32)
    for i, name in enumerate(TWIN_WEIGHTS):
        w = inp[name].astype(_jnp.float32)
        if MOMENT_SCALE is None:
            s = _jnp.sqrt(_jnp.mean(_jnp.square(w)) + 1e-30)
        else:
            s = MOMENT_SCALE[name]
        km, kv = _jax.random.split(_jax.random.fold_in(key, i + 1))
        out[name] = w
        out["m_" + name] = s * _jax.random.normal(km, w.shape, _jnp.float32)
        out["v_" + name] = (s * s) * _jax.random.uniform(kv, w.shape, _jnp.float32, 0.5, 1.5)
    if N_MICROBATCH > 1:
        for name, axis in PER_EXAMPLE_BATCH_AXIS.items():
            out[name] = _to_microbatches(out[name], axis)
    return {'x': out['x'], 'c': out['c'], 'w_ada': out['w_ada'], 'b_ada': out['b_ada'], 'norm_g': out['norm_g'], 'w_in': out['w_in'], 'ln_v_g': out['ln_v_g'], 'ln_v_b': out['ln_v_b'], 'w_spatial': out['w_spatial'], 'b_spatial': out['b_spatial'], 'sinks': out['sinks'], 'w_out': out['w_out'], 'w_ada_final': out['w_ada_final'], 'b_ada_final': out['b_ada_final'], 'final_norm_g': out['final_norm_g'], 'loss_target': out['loss_target'], 'm_w_ada': out['m_w_ada'], 'm_b_ada': out['m_b_ada'], 'm_norm_g': out['m_norm_g'], 'm_w_in': out['m_w_in'], 'm_ln_v_g': out['m_ln_v_g'], 'm_ln_v_b': out['m_ln_v_b'], 'm_w_spatial': out['m_w_spatial'], 'm_b_spatial': out['m_b_spatial'], 'm_sinks': out['m_sinks'], 'm_w_out': out['m_w_out'], 'm_w_ada_final': out['m_w_ada_final'], 'm_b_ada_final': out['m_b_ada_final'], 'm_final_norm_g': out['m_final_norm_g'], 'v_w_ada': out['v_w_ada'], 'v_b_ada': out['v_b_ada'], 'v_norm_g': out['v_norm_g'], 'v_w_in': out['v_w_in'], 'v_ln_v_g': out['v_ln_v_g'], 'v_ln_v_b': out['v_ln_v_b'], 'v_w_spatial': out['v_w_spatial'], 'v_b_spatial': out['v_b_spatial'], 'v_sinks': out['v_sinks'], 'v_w_out': out['v_w_out'], 'v_w_ada_final': out['v_w_ada_final'], 'v_b_ada_final': out['v_b_ada_final'], 'v_final_norm_g': out['v_final_norm_g']}


def _loss(weights, diff, rest, loss_target):
    with _jax.named_scope("forward"):
        args = {**rest, TWIN_DIFF_INPUT: diff, **{k: w.astype(_WEIGHT_DTYPES[k]) for k, w in weights.items()}}
        y = _forward(args)
    with _jax.named_scope("loss_head"):
        err = _jnp.square(y.astype(_jnp.float32) - loss_target)
        return 0.5 * _jnp.sum(_jnp.mean(err, axis=-1)) if err.ndim else 0.5 * err


def _adamw(w, g, m, v):
    m = ADAM_B1 * m + (1.0 - ADAM_B1) * g
    v = ADAM_B2 * v + (1.0 - ADAM_B2) * _jnp.square(g)
    m_hat = m / (1.0 - ADAM_B1 ** ADAM_STEP)
    v_hat = v / (1.0 - ADAM_B2 ** ADAM_STEP)
    delta = -ADAM_LR * (m_hat / (_jnp.sqrt(v_hat) + ADAM_EPS) + ADAM_WD * w)
    return delta, m, v


def reference(x, c, w_ada, b_ada, norm_g, w_in, ln_v_g, ln_v_b, w_spatial, b_spatial, sinks, w_out, w_ada_final, b_ada_final, final_norm_g, loss_target, m_w_ada, m_b_ada, m_norm_g, m_w_in, m_ln_v_g, m_ln_v_b, m_w_spatial, m_b_spatial, m_sinks, m_w_out, m_w_ada_final, m_b_ada_final, m_final_norm_g, v_w_ada, v_b_ada, v_norm_g, v_w_in, v_ln_v_g, v_ln_v_b, v_w_spatial, v_b_spatial, v_sinks, v_w_out, v_w_ada_final, v_b_ada_final, v_final_norm_g):
    given = dict(x=x, c=c, w_ada=w_ada, b_ada=b_ada, norm_g=norm_g, w_in=w_in, ln_v_g=ln_v_g, ln_v_b=ln_v_b, w_spatial=w_spatial, b_spatial=b_spatial, sinks=sinks, w_out=w_out, w_ada_final=w_ada_final, b_ada_final=b_ada_final, final_norm_g=final_norm_g, loss_target=loss_target, m_w_ada=m_w_ada, m_b_ada=m_b_ada, m_norm_g=m_norm_g, m_w_in=m_w_in, m_ln_v_g=m_ln_v_g, m_ln_v_b=m_ln_v_b, m_w_spatial=m_w_spatial, m_b_spatial=m_b_spatial, m_sinks=m_sinks, m_w_out=m_w_out, m_w_ada_final=m_w_ada_final, m_b_ada_final=m_b_ada_final, m_final_norm_g=m_final_norm_g, v_w_ada=v_w_ada, v_b_ada=v_b_ada, v_norm_g=v_norm_g, v_w_in=v_w_in, v_ln_v_g=v_ln_v_g, v_ln_v_b=v_ln_v_b, v_w_spatial=v_w_spatial, v_b_spatial=v_b_spatial, v_sinks=v_sinks, v_w_out=v_w_out, v_w_ada_final=v_w_ada_final, v_b_ada_final=v_b_ada_final, v_final_norm_g=v_final_norm_g)
    weights = {n: given[n] for n in TWIN_WEIGHTS}
    shared = {n: given[n] for n in SHARED_INPUTS}
    per_example = {n: given[n] for n in ['x', 'c']}
    grad_fn = _jax.value_and_grad(_loss, argnums=(0, 1))

    def one_microbatch(ex, loss_target):
        ex = dict(ex)
        diff = ex.pop(TWIN_DIFF_INPUT)
        return grad_fn(weights, diff, {**shared, **ex}, loss_target)

    if N_MICROBATCH == 1:
        loss, (grad_w, grad_x) = one_microbatch(per_example, given["loss_target"])
    else:
        def body(carry, xs):
            loss_sum, grad_sum = carry
            l_k, (gw_k, gx_k) = one_microbatch(xs[0], xs[1])
            with _jax.named_scope("update"):
                return (loss_sum + l_k, _jax.tree.map(_jnp.add, grad_sum, gw_k)), gx_k

        init = (_jnp.zeros((), _jnp.float32), _jax.tree.map(_jnp.zeros_like, weights))
        (loss, grad_w), grad_x = _jax.lax.scan(body, init, (per_example, given["loss_target"]))
    with _jax.named_scope("update"):
        delta_w, new_m, new_v = {}, {}, {}
        for n in TWIN_WEIGHTS:
            delta_w[n], new_m[n], new_v[n] = _adamw(weights[n], grad_w[n], given["m_" + n], given["v_" + n])
    return (loss, grad_x, *[grad_w[n] for n in TWIN_WEIGHTS], *[delta_w[n] for n in TWIN_WEIGHTS],
            *[new_m[n] for n in TWIN_WEIGHTS], *[new_v[n] for n in TWIN_WEIGHTS])
```

```python
import functools

import jax
import jax.numpy as jnp
from jax import lax
from jax.experimental import pallas as pl
from jax.experimental.pallas import tpu as pltpu

D_MODEL = 2048
D_IN = 5632
D_A = 1024
CHUNK = 128
A_GROUPS = 8
HEAD_DIM = 64
N_KV_HEADS = 4
N_DEV = 8
ROPE_THETA = 10000.0
NORM_EPS = 1e-5
ATTN_SCALE = HEAD_DIM ** -0.5

ADAM_LR = 0.001
ADAM_B1 = 0.9
ADAM_B2 = 0.999
ADAM_EPS = 1e-08
ADAM_WD = 0.01
ADAM_STEP = 10

OFF_U, OFF_VA, OFF_ZA, OFF_Q, OFF_K, OFF_V, OFF_ZB = 0, 1024, 2048, 3072, 4096, 4352, 4608

SMALL_ROWS = 80
ROW_WSP, ROW_NG, ROW_LN, ROW_BSP, ROW_FNG, ROW_BADA, ROW_BADAF = 0, 64, 65, 66, 67, 68, 71

V7X_VMEM_LIMIT_BYTES = 56 * 1024 * 1024

F32 = jnp.float32
BF16 = jnp.bfloat16
MESH = pl.DeviceIdType.MESH
SDS = jax.ShapeDtypeStruct
NT = (((1,), (1,)), ((), ()))
TN = (((0,), (0,)), ((), ()))


def _params(*semantics):
    return pltpu.CompilerParams(dimension_semantics=semantics or None, vmem_limit_bytes=V7X_VMEM_LIMIT_BYTES)


def _mesh_pos():
    return lax.axis_index("x"), lax.axis_index("y"), lax.axis_index("c")


def _sigmoid(z):
    return 1.0 / (1.0 + jnp.exp(-z))


def _adamw(w, g, m, v):
    m = ADAM_B1 * m + (1.0 - ADAM_B1) * g
    v = ADAM_B2 * v + (1.0 - ADAM_B2) * (g * g)
    m_hat = m / (1.0 - ADAM_B1 ** ADAM_STEP)
    v_hat = v / (1.0 - ADAM_B2 ** ADAM_STEP)
    delta = -ADAM_LR * (m_hat / (jnp.sqrt(v_hat) + ADAM_EPS) + ADAM_WD * w)
    return delta, m, v


def _all_gather(name, blocks, memory_space):
    n_arr = len(blocks)

    def body(*refs):
        ins, outs = refs[:n_arr], refs[n_arr:2 * n_arr]
        send_sems, recv_sems, local_sems = refs[2 * n_arr:]
        x, y, c = _mesh_pos()
        me, sibling = (x, y, c), (x, y, 1 - c)
        chips = [(1 - x, y), (x, 1 - y), (1 - x, 1 - y)]

        def slot(p):
            return 4 * p[0] + 2 * p[1] + p[2]

        def copy(a, k, block, to, src=None):
            dst = outs[a].at[slot(block)]
            return pltpu.make_async_remote_copy(
                src_ref=dst if src is None else src, dst_ref=dst,
                send_sem=send_sems.at[a, k], recv_sem=recv_sems.at[a, k],
                device_id=to, device_id_type=MESH)

        mine = [pltpu.make_async_copy(ins[a], outs[a].at[slot(me)], local_sems.at[a]) for a in range(n_arr)]
        for cp in mine:
            cp.start()
        first = []
        for a in range(n_arr):
            first.append(copy(a, 0, me, sibling, src=ins[a]))
            first += [copy(a, 1 + j, me, (*chip, c), src=ins[a]) for j, chip in enumerate(chips)]
        for cp in first:
            cp.start()
        passed = []
        for j, chip in enumerate(chips):
            for a in range(n_arr):
                copy(a, 1 + j, (*chip, c), me).wait_recv()
                fwd = copy(a, 4 + j, (*chip, c), sibling)
                fwd.start()
                passed.append(fwd)
        for a in range(n_arr):
            copy(a, 0, sibling, me).wait_recv()
            for j, chip in enumerate(chips):
                copy(a, 4 + j, (*chip, 1 - c), me).wait_recv()
        for cp in first + passed:
            cp.wait_send()
        for cp in mine:
            cp.wait()

    spec = pl.BlockSpec(memory_space=memory_space)
    return pl.pallas_call(
        body, name=name,
        out_shape=[SDS((N_DEV,) + b.shape, b.dtype) for b in blocks],
        in_specs=[spec] * n_arr, out_specs=[spec] * n_arr,
        scratch_shapes=[pltpu.SemaphoreType.DMA((n_arr, 7)), pltpu.SemaphoreType.DMA((n_arr, 7)),
                        pltpu.SemaphoreType.DMA((n_arr,))],
        compiler_params=_params(),
    )(*blocks)


def _ada_exchange(c, w_ada, b_ada8, w_ada_f, b_ada_f8):
    n1, n2 = w_ada.shape[1], w_ada_f.shape[1]

    def body(c_ref, w1_ref, b1_ref, w2_ref, b2_ref, cact_ref, mod_ref, modf_ref,
             cact_buf, res1, res2, send1, send2, sems_s, sems_r):
        x, y, c_pos = _mesh_pos()
        me = 4 * x + 2 * y + c_pos
        flips = [(k >> 2 & 1, k >> 1 & 1, k & 1) for k in range(1, N_DEV)]

        def peer(f):
            return (1 - x if f[0] else x, 1 - y if f[1] else y, 1 - c_pos if f[2] else c_pos)

        cv = c_ref[...]
        cact = cv * _sigmoid(cv)
        cact_buf[...] = cact
        cact_ref[me] = cact

        def rdma(phase, k, src, dst, f):
            return pltpu.make_async_remote_copy(src_ref=src, dst_ref=dst, send_sem=sems_s.at[phase, k],
                                                recv_sem=sems_r.at[phase, k], device_id=peer(f), device_id_type=MESH)

        gather = [rdma(0, k, cact_buf, cact_ref.at[me], f) for k, f in enumerate(flips)]
        for cp in gather:
            cp.start()
        for cp in gather:
            cp.wait_recv()
        for cp in gather:
            cp.wait_send()

        rid = lax.broadcasted_iota(jnp.int32, (N_DEV, D_MODEL), 0)
        rows = jnp.zeros((N_DEV, D_MODEL), F32)
        for j in range(N_DEV):
            rows = jnp.where(rid == j, jnp.broadcast_to(cact_ref[j], (N_DEV, D_MODEL)), rows)
        rows = rows.astype(BF16)
        res1[...] = jnp.dot(rows, w1_ref[...].astype(BF16), preferred_element_type=F32) + b1_ref[pl.ds(me, 1), :]
        res2[...] = jnp.dot(rows, w2_ref[...].astype(BF16), preferred_element_type=F32) + b2_ref[pl.ds(me, 1), :]
        for j in range(N_DEV):
            send1[j] = res1[pl.ds(j, 1), :]
            send2[j] = res2[pl.ds(j, 1), :]
        mod_ref[me] = send1[me]
        modf_ref[me] = send2[me]
        scatter = []
        for k, f in enumerate(flips):
            to = me ^ (k + 1)
            scatter.append(rdma(1, k, send1.at[to], mod_ref.at[me], f))
            scatter.append(rdma(2, k, send2.at[to], modf_ref.at[me], f))
        for cp in scatter:
            cp.start()
        for cp in scatter:
            cp.wait_recv()
        for cp in scatter:
            cp.wait_send()

    vmem = pl.BlockSpec(memory_space=pltpu.VMEM)
    return pl.pallas_call(
        body, name="ada_exchange",
        out_shape=[SDS((N_DEV, 1, D_MODEL), F32), SDS((N_DEV, 1, n1), F32), SDS((N_DEV, 1, n2), F32)],
        in_specs=[vmem] * 5, out_specs=[vmem] * 3,
        scratch_shapes=[pltpu.VMEM((1, D_MODEL), F32), pltpu.VMEM((N_DEV, n1), F32), pltpu.VMEM((N_DEV, n2), F32),
                        pltpu.VMEM((N_DEV, 1, n1), F32), pltpu.VMEM((N_DEV, 1, n2), F32),
                        pltpu.SemaphoreType.DMA((3, 7)), pltpu.SemaphoreType.DMA((3, 7))],
        compiler_params=_params(),
    )(c, w_ada, b_ada8, w_ada_f, b_ada_f8)


def _rs_pair(grads):
    n_arr = len(grads)

    def body(*refs):
        ins = refs[:n_arr]
        kept, got = refs[n_arr:2 * n_arr], refs[2 * n_arr:3 * n_arr]
        send_sems, recv_sems, local_sems = refs[3 * n_arr:]
        x, y, c = _mesh_pos()
        local, remote = [], []
        for a in range(n_arr):
            for q in range(4):
                local.append(pltpu.make_async_copy(ins[a].at[2 * q + c], kept[a].at[q], local_sems.at[a, q]))
                remote.append(pltpu.make_async_remote_copy(
                    src_ref=ins[a].at[2 * q + 1 - c], dst_ref=got[a].at[q],
                    send_sem=send_sems.at[a, q], recv_sem=recv_sems.at[a, q],
                    device_id=(x, y, 1 - c), device_id_type=MESH))
        for cp in remote + local:
            cp.start()
        for cp in remote:
            cp.wait_recv()
        for cp in remote:
            cp.wait_send()
        for cp in local:
            cp.wait()

    hbm = pl.BlockSpec(memory_space=pl.ANY)
    quarter = [SDS((4,) + g.shape[1:], g.dtype) for g in grads]
    outs = pl.pallas_call(
        body, name="rs_pair", out_shape=quarter + quarter,
        in_specs=[hbm] * n_arr, out_specs=[hbm] * (2 * n_arr),
        scratch_shapes=[pltpu.SemaphoreType.DMA((n_arr, 4)), pltpu.SemaphoreType.DMA((n_arr, 4)),
                        pltpu.SemaphoreType.DMA((n_arr, 4))],
        compiler_params=_params(),
    )(*grads)
    return outs[:n_arr], outs[n_arr:]


def _rs_chips(pairs):
    n_arr = len(pairs)

    def body(*refs):
        ins, outs = refs[:n_arr], refs[n_arr:2 * n_arr]
        send_sems, recv_sems, local_sems = refs[2 * n_arr:]
        x, y, c = _mesh_pos()
        my_chip = 2 * x + y
        chips = [(1 - x, y), (x, 1 - y), (1 - x, 1 - y)]
        local = [pltpu.make_async_copy(ins[a].at[my_chip], outs[a].at[my_chip], local_sems.at[a])
                 for a in range(n_arr)]
        remote = []
        for a in range(n_arr):
            for j, chip in enumerate(chips):
                remote.append(pltpu.make_async_remote_copy(
                    src_ref=ins[a].at[2 * chip[0] + chip[1]], dst_ref=outs[a].at[my_chip],
                    send_sem=send_sems.at[a, j], recv_sem=recv_sems.at[a, j],
                    device_id=(*chip, c), device_id_type=MESH))
        for cp in remote + local:
            cp.start()
        for cp in remote:
            cp.wait_recv()
        for cp in remote:
            cp.wait_send()
        for cp in local:
            cp.wait()

    hbm = pl.BlockSpec(memory_space=pl.ANY)
    return pl.pallas_call(
        body, name="rs_chips", out_shape=[SDS(p.shape, p.dtype) for p in pairs],
        in_specs=[hbm] * n_arr, out_specs=[hbm] * n_arr,
        scratch_shapes=[pltpu.SemaphoreType.DMA((n_arr, 3)), pltpu.SemaphoreType.DMA((n_arr, 3)),
                        pltpu.SemaphoreType.DMA((n_arr,))],
        compiler_params=_params(),
    )(*pairs)


def _prep_weights(w_in, w_out):
    d, n = w_in.shape
    steps = 4

    def body(wi_ref, wo_ref, wt_ref, wob_ref):
        wt_ref[...] = wi_ref[...].T.astype(BF16)
        wob_ref[...] = wo_ref[...].astype(BF16)

    return pl.pallas_call(
        body, name="prep_weights", grid=(steps,),
        in_specs=[pl.BlockSpec((d // steps, n), lambda i: (i, 0)),
                  pl.BlockSpec((w_out.shape[0] // steps, d), lambda i: (i, 0))],
        out_specs=[pl.BlockSpec((n, d // steps), lambda i: (0, i)),
                   pl.BlockSpec((w_out.shape[0] // steps, d), lambda i: (i, 0))],
        out_shape=[SDS((n, d), BF16), SDS(w_out.shape, BF16)],
        compiler_params=_params("parallel"),
    )(w_in, w_out)


def _in_proj(x, shift, scale, norm_g, wt):
    s = x.shape[0]
    tm, tn = 512, 512

    def body(x_ref, shift_ref, scale_ref, g_ref, wt_ref, h_ref, proj_ref, h_scr):
        @pl.when(pl.program_id(1) == 0)
        def _():
            xv = x_ref[...]
            r = lax.rsqrt(jnp.mean(xv * xv, axis=-1, keepdims=True) + NORM_EPS)
            h = ((xv * r) * g_ref[...]) * (1.0 + scale_ref[...]) + shift_ref[...]
            hb = h.astype(BF16)
            h_scr[...] = hb
            h_ref[...] = hb

        proj_ref[...] = lax.dot_general(h_scr[...], wt_ref[...], NT, preferred_element_type=F32).astype(BF16)

    row = pl.BlockSpec((1, D_MODEL), lambda i, j: (0, 0))
    return pl.pallas_call(
        body, name="in_proj", grid=(s // tm, D_IN // tn),
        in_specs=[pl.BlockSpec((tm, D_MODEL), lambda i, j: (i, 0)), row, row, row,
                  pl.BlockSpec((tn, D_MODEL), lambda i, j: (j, 0))],
        out_specs=[pl.BlockSpec((tm, D_MODEL), lambda i, j: (i, 0)), pl.BlockSpec((tm, tn), lambda i, j: (i, j))],
        out_shape=[SDS((s, D_MODEL), BF16), SDS((s, D_IN), BF16)],
        scratch_shapes=[pltpu.VMEM((tm, D_MODEL), BF16)],
        compiler_params=_params("parallel", "arbitrary"),
    )(x, shift, scale, norm_g, wt)


def _rope_tables(seq):
    inv_freq = ROPE_THETA ** (-jnp.arange(0, HEAD_DIM, 2, dtype=F32) / HEAD_DIM)
    ang = jnp.arange(seq, dtype=F32)[:, None] * inv_freq[None, :]
    cos, sin, zero = jnp.cos(ang), jnp.sin(ang), jnp.zeros_like(ang)
    return (jnp.concatenate([cos] * 4, axis=1), jnp.concatenate([-sin, zero, -sin, zero], axis=1),
            jnp.concatenate([zero, sin, zero, sin], axis=1))


def _rope(v, cos, sin_lo, sin_hi):
    width = v.shape[1]
    rep = (1, width // 128)
    return (v * jnp.tile(cos, rep) + pltpu.roll(v, width - 32, 1) * jnp.tile(sin_lo, rep)
            + pltpu.roll(v, 32, 1) * jnp.tile(sin_hi, rep))


def _rope_bwd(d, cos, sin_lo, sin_hi):
    width = d.shape[1]
    rep = (1, width // 128)
    return (d * jnp.tile(cos, rep) + pltpu.roll(d * jnp.tile(sin_lo, rep), 32, 1)
            + pltpu.roll(d * jnp.tile(sin_hi, rep), width - 32, 1))


def _layer_norm(v, g, b):
    mu = jnp.mean(v, axis=-1, keepdims=True)
    vc = v - mu
    rstd = lax.rsqrt(jnp.mean(vc * vc, axis=-1, keepdims=True) + NORM_EPS)
    vhat = vc * rstd
    return vhat * g + b, vhat, rstd


def _tril_bf16(w_ref, g):
    t = lax.broadcasted_iota(jnp.int32, (CHUNK, CHUNK), 0)
    tp = lax.broadcasted_iota(jnp.int32, (CHUNK, CHUNK), 1)
    return jnp.where(tp <= t, w_ref[g], 0.0).astype(BF16)


def _bias_columns(b_ref, out_ref):
    for g in range(A_GROUPS):
        out_ref[g] = jnp.broadcast_to(b_ref[pl.ds(g, 1), :], (CHUNK, CHUNK)).T


def _band_valid(first_block):
    qi = lax.broadcasted_iota(jnp.int32, (4 * CHUNK, 2 * CHUNK), 0) & (CHUNK - 1)
    kj = lax.broadcasted_iota(jnp.int32, (4 * CHUNK, 2 * CHUNK), 1)
    rel = qi + CHUNK - kj
    return (rel >= 0) & (rel < CHUNK) & ((kj >= CHUNK) | jnp.logical_not(first_block))


def _low_lanes():
    return lax.broadcasted_iota(jnp.int32, (1, 128), 1) < HEAD_DIM


def _stack_heads(pair_a, pair_b):
    lo = _low_lanes()
    return jnp.concatenate([jnp.where(lo, pair_a, 0.0), jnp.where(lo, 0.0, pair_a),
                            jnp.where(lo, pair_b, 0.0), jnp.where(lo, 0.0, pair_b)], axis=0).astype(BF16)


def _unstack_heads(st):
    lo = _low_lanes()
    return (jnp.where(lo, st[0:128], st[128:256]), jnp.where(lo, st[256:384], st[384:512]))


def _dup_kv_head(band, gk):
    pair = band[:, (gk // 2) * 128:(gk // 2 + 1) * 128]
    lo = _low_lanes()
    one = jnp.where(lo if gk % 2 == 0 else jnp.logical_not(lo), pair, 0.0)
    return (one + pltpu.roll(one, HEAD_DIM, 1)).astype(BF16)


def _fold_kv_head(dup_grad, gk):
    both = dup_grad + pltpu.roll(dup_grad, HEAD_DIM, 1)
    lo = _low_lanes()
    return jnp.where(lo if gk % 2 == 0 else jnp.logical_not(lo), both, 0.0)


def _attn_probs(q_st, k_dup, sink_col, valid):
    s = lax.dot_general(q_st, k_dup, NT, preferred_element_type=F32) * ATTN_SCALE
    s = jnp.where(valid, s, -jnp.inf)
    m = jnp.maximum(jnp.max(s, axis=-1, keepdims=True), sink_col)
    p = jnp.exp(s - m)
    e_sink = jnp.exp(sink_col - m)
    inv = 1.0 / (jnp.sum(p, axis=-1, keepdims=True) + e_sink)
    return p * inv, e_sink * inv


def _sink_column(sinks_ref, gk):
    return jnp.concatenate([jnp.full((CHUNK, 1), sinks_ref[4 * gk + r], F32) for r in range(4)], axis=0)


def _mixer_specs(nb, rev):
    def blk(i):
        return nb - 1 - i if rev else i

    def prev(i):
        return jnp.maximum(blk(i) - 1, 0)

    tab = pl.BlockSpec((CHUNK, 128), lambda i: (blk(i), 0))
    tab_prev = pl.BlockSpec((CHUNK, 128), lambda i: (prev(i), 0))
    return dict(
        cur=pl.BlockSpec((CHUNK, D_IN), lambda i: (blk(i), 0)),
        prev_kv=pl.BlockSpec((CHUNK, 2 * 256), lambda i: (prev(i), OFF_K // 512)),
        tabs=[tab] * 3 + [tab_prev] * 3,
        vec=pl.BlockSpec((1, D_A), lambda i: (0, 0)),
        wsp=pl.BlockSpec((A_GROUPS, CHUNK, CHUNK), lambda i: (0, 0, 0)),
        bsp=pl.BlockSpec((A_GROUPS, CHUNK), lambda i: (0, 0)),
        smem=pl.BlockSpec(memory_space=pltpu.SMEM),
        blk=blk,
    )


def _mixer_fwd(proj, tabs, ln_g, ln_b, w_sp, b_sp, sinks):
    s = proj.shape[0]
    nb = s // CHUNK
    sp = _mixer_specs(nb, rev=False)

    def body(cur_ref, pkv_ref, c_ref, s1_ref, s2_ref, cp_ref, s1p_ref, s2p_ref, lg_ref, lb_ref, w_ref, b_ref,
             sinks_ref, y_ref, bcol):
        i = pl.program_id(0)

        @pl.when(i == 0)
        def _():
            _bias_columns(b_ref, bcol)

        vln, _, _ = _layer_norm(cur_ref[:, OFF_VA:OFF_ZA].astype(F32), lg_ref[...], lb_ref[...])
        vln = vln.astype(BF16)
        for g in range(A_GROUPS):
            cols = slice(g * 128, (g + 1) * 128)
            sg = jnp.dot(_tril_bf16(w_ref, g), vln[:, cols], preferred_element_type=F32) + bcol[g]
            u = cur_ref[:, OFF_U + g * 128:OFF_U + (g + 1) * 128].astype(F32)
            z = cur_ref[:, OFF_ZA + g * 128:OFF_ZA + (g + 1) * 128].astype(F32)
            y_ref[:, cols] = (u * sg * (z * _sigmoid(z))).astype(BF16)

        cur_t = (c_ref[...], s1_ref[...], s2_ref[...])
        prev_t = (cp_ref[...], s1p_ref[...], s2p_ref[...])
        qr = _rope(cur_ref[:, OFF_Q:OFF_K].astype(F32), *cur_t)
        kr = jnp.concatenate([_rope(pkv_ref[:, 0:256].astype(F32), *prev_t),
                              _rope(cur_ref[:, OFF_K:OFF_V].astype(F32), *cur_t)], axis=0)
        vb = jnp.concatenate([pkv_ref[:, 256:512], cur_ref[:, OFF_V:OFF_ZB]], axis=0).astype(F32)
        valid = _band_valid(i == 0)
        for gk in range(N_KV_HEADS):
            q_st = _stack_heads(qr[:, (2 * gk) * 128:(2 * gk + 1) * 128], qr[:, (2 * gk + 1) * 128:(2 * gk + 2) * 128])
            probs, _ = _attn_probs(q_st, _dup_kv_head(kr, gk), _sink_column(sinks_ref, gk), valid)
            out = jnp.dot(probs.astype(BF16), _dup_kv_head(vb, gk), preferred_element_type=F32)
            pair_a, pair_b = _unstack_heads(out)
            cols = slice(gk * 256, (gk + 1) * 256)
            zb = cur_ref[:, OFF_ZB + gk * 256:OFF_ZB + (gk + 1) * 256].astype(F32)
            yb = jnp.concatenate([pair_a, pair_b], axis=1) * (zb * _sigmoid(zb))
            y_ref[:, D_A + gk * 256:D_A + (gk + 1) * 256] = yb.astype(BF16)

    return pl.pallas_call(
        body, name="mixer_fwd", grid=(nb,),
        in_specs=[sp["cur"], sp["prev_kv"], *sp["tabs"], sp["vec"], sp["vec"], sp["wsp"], sp["bsp"], sp["smem"]],
        out_specs=pl.BlockSpec((CHUNK, D_MODEL), lambda i: (i, 0)),
        out_shape=SDS((s, D_MODEL), BF16),
        scratch_shapes=[pltpu.VMEM((A_GROUPS, CHUNK, CHUNK), F32)],
        compiler_params=_params("arbitrary"),
    )(proj, proj, *tabs, *tabs, ln_g, ln_b, w_sp, b_sp, sinks)


def _out_proj_loss(y, x, target, wo, gate, shift_f, scale_f, fng):
    s = y.shape[0]
    tm = 256

    def body(y_ref, x_ref, t_ref, wo_ref, gate_ref, sh_ref, sc_ref, g_ref, dx1_ref, do_ref, dy_ref, sums_ref):
        @pl.when(pl.program_id(0) == 0)
        def _():
            sums_ref[...] = jnp.zeros_like(sums_ref)

        o = jnp.dot(y_ref[...], wo_ref[...], preferred_element_type=F32)
        gate = gate_ref[...]
        x1 = x_ref[...] + gate * o
        rf = lax.rsqrt(jnp.mean(x1 * x1, axis=-1, keepdims=True) + NORM_EPS)
        x1n = x1 * rf
        hn = x1n * g_ref[...]
        one_sc = 1.0 + sc_ref[...]
        diff = hn * one_sc + sh_ref[...] - t_ref[...]
        dout = diff * (1.0 / D_MODEL)
        d_hn = dout * one_sc
        d_x1n = d_hn * g_ref[...]
        dx1 = rf * (d_x1n - x1n * jnp.mean(d_x1n * x1n, axis=-1, keepdims=True))
        dx1_ref[...] = dx1
        do = (dx1 * gate).astype(BF16)
        do_ref[...] = do
        dy_ref[...] = lax.dot_general(do, wo_ref[...], NT, preferred_element_type=F32).astype(BF16)

        def rowsum(v):
            return jnp.sum(v, axis=0, keepdims=True)

        sums_ref[0:1, :] += rowsum(dx1 * o)
        sums_ref[1:2, :] += rowsum(dout)
        sums_ref[2:3, :] += rowsum(dout * hn)
        sums_ref[3:4, :] += rowsum(d_hn * x1n)
        sums_ref[4:5, :] += rowsum(diff * diff)

    tile = pl.BlockSpec((tm, D_MODEL), lambda i: (i, 0))
    row = pl.BlockSpec((1, D_MODEL), lambda i: (0, 0))
    return pl.pallas_call(
        body, name="out_proj_loss", grid=(s // tm,),
        in_specs=[tile, tile, tile, pl.BlockSpec((D_MODEL, D_MODEL), lambda i: (0, 0)), row, row, row, row],
        out_specs=[tile, tile, tile, pl.BlockSpec((8, D_MODEL), lambda i: (0, 0))],
        out_shape=[SDS((s, D_MODEL), F32), SDS((s, D_MODEL), BF16), SDS((s, D_MODEL), BF16), SDS((8, D_MODEL), F32)],
        compiler_params=_params("arbitrary"),
    )(y, x, target, wo, gate, shift_f, scale_f, fng)


def _mixer_bwd(proj, dy, tabs, ln_g, ln_b, w_sp, b_sp, sinks):
    s = proj.shape[0]
    nb = s // CHUNK
    sp = _mixer_specs(nb, rev=True)

    def body(cur_ref, pkv_ref, dy_ref, c_ref, s1_ref, s2_ref, cp_ref, s1p_ref, s2p_ref, lg_ref, lb_ref, w_ref, b_ref,
             sinks_ref, dproj_ref, dln_ref, dw_ref, db_ref, dsink_ref, bcol, dbcol, carry):
        i = pl.program_id(0)
        block = nb - 1 - i

        @pl.when(i == 0)
        def _():
            _bias_columns(b_ref, bcol)
            dbcol[...] = jnp.zeros_like(dbcol)
            carry[...] = jnp.zeros_like(carry)
            dln_ref[...] = jnp.zeros_like(dln_ref)
            dw_ref[...] = jnp.zeros_like(dw_ref)
            dsink_ref[...] = jnp.zeros_like(dsink_ref)

        vln, vhat, rstd = _layer_norm(cur_ref[:, OFF_VA:OFF_ZA].astype(F32), lg_ref[...], lb_ref[...])
        vln = vln.astype(BF16)
        d_vln = []
        for g in range(A_GROUPS):
            cols = slice(g * 128, (g + 1) * 128)
            w_g = _tril_bf16(w_ref, g)
            sg = jnp.dot(w_g, vln[:, cols], preferred_element_type=F32) + bcol[g]
            u = cur_ref[:, OFF_U + g * 128:OFF_U + (g + 1) * 128].astype(F32)
            z = cur_ref[:, OFF_ZA + g * 128:OFF_ZA + (g + 1) * 128].astype(F32)
            dya = dy_ref[:, cols].astype(F32)
            sig = _sigmoid(z)
            d_ya = dya * (z * sig)
            dproj_ref[:, OFF_ZA + g * 128:OFF_ZA + (g + 1) * 128] = (
                dya * (u * sg) * (sig * (1.0 + z * (1.0 - sig)))).astype(BF16)
            dproj_ref[:, OFF_U + g * 128:OFF_U + (g + 1) * 128] = (d_ya * sg).astype(BF16)
            d_s = d_ya * u
            dbcol[g] += d_s
            d_sb = d_s.astype(BF16)
            dw_ref[g] += lax.dot_general(d_sb, vln[:, cols], NT, preferred_element_type=F32)
            d_vln.append(lax.dot_general(w_g, d_sb, TN, preferred_element_type=F32))
        d_vln = jnp.concatenate(d_vln, axis=1)
        dln_ref[0:1, :] += jnp.sum(d_vln * vhat, axis=0, keepdims=True)
        dln_ref[1:2, :] += jnp.sum(d_vln, axis=0, keepdims=True)
        d_vhat = d_vln * lg_ref[...]
        d_va = rstd * (d_vhat - jnp.mean(d_vhat, axis=-1, keepdims=True)
                       - vhat * jnp.mean(d_vhat * vhat, axis=-1, keepdims=True))
        dproj_ref[:, OFF_VA:OFF_ZA] = d_va.astype(BF16)

        cur_t = (c_ref[...], s1_ref[...], s2_ref[...])
        prev_t = (cp_ref[...], s1p_ref[...], s2p_ref[...])
        band_t = tuple(jnp.concatenate([p, c], axis=0) for p, c in zip(prev_t, cur_t))
        qr = _rope(cur_ref[:, OFF_Q:OFF_K].astype(F32), *cur_t)
        kr = jnp.concatenate([_rope(pkv_ref[:, 0:256].astype(F32), *prev_t),
                              _rope(cur_ref[:, OFF_K:OFF_V].astype(F32), *cur_t)], axis=0)
        vb = jnp.concatenate([pkv_ref[:, 256:512], cur_ref[:, OFF_V:OFF_ZB]], axis=0).astype(F32)
        valid = _band_valid(block == 0)
        dq_pairs = []
        dk_pairs = [jnp.zeros((2 * CHUNK, 128), F32) for _ in range(2)]
        dv_pairs = [jnp.zeros((2 * CHUNK, 128), F32) for _ in range(2)]
        for gk in range(N_KV_HEADS):
            q_st = _stack_heads(qr[:, (2 * gk) * 128:(2 * gk + 1) * 128], qr[:, (2 * gk + 1) * 128:(2 * gk + 2) * 128])
            k_dup, v_dup = _dup_kv_head(kr, gk), _dup_kv_head(vb, gk)
            probs, p_sink = _attn_probs(q_st, k_dup, _sink_column(sinks_ref, gk), valid)
            probs_b = probs.astype(BF16)
            out = jnp.dot(probs_b, v_dup, preferred_element_type=F32)
            yb = jnp.concatenate(_unstack_heads(out), axis=1)
            zb = cur_ref[:, OFF_ZB + gk * 256:OFF_ZB + (gk + 1) * 256].astype(F32)
            dyb = dy_ref[:, D_A + gk * 256:D_A + (gk + 1) * 256].astype(F32)
            sig = _sigmoid(zb)
            dproj_ref[:, OFF_ZB + gk * 256:OFF_ZB + (gk + 1) * 256] = (
                dyb * yb * (sig * (1.0 + zb * (1.0 - sig)))).astype(BF16)
            d_yb = dyb * (zb * sig)
            do_st = _stack_heads(d_yb[:, 0:128], d_yb[:, 128:256])
            dp = lax.dot_general(do_st, v_dup, NT, preferred_element_type=F32)
            delta = jnp.sum(probs * dp, axis=-1, keepdims=True)
            ds = (probs * (dp - delta) * ATTN_SCALE).astype(BF16)
            d_sink = -p_sink * delta
            for r in range(4):
                dsink_ref[4 * gk + r:4 * gk + r + 1, :] += jnp.broadcast_to(
                    jnp.sum(d_sink[r * CHUNK:(r + 1) * CHUNK], axis=0, keepdims=True), (1, 128))
            dq_pairs += list(_unstack_heads(jnp.dot(ds, k_dup, preferred_element_type=F32)))
            dk_pairs[gk // 2] += _fold_kv_head(lax.dot_general(ds, q_st, TN, preferred_element_type=F32), gk)
            dv_pairs[gk // 2] += _fold_kv_head(lax.dot_general(probs_b, do_st, TN, preferred_element_type=F32), gk)
        dproj_ref[:, OFF_Q:OFF_K] = _rope_bwd(jnp.concatenate(dq_pairs, axis=1), *cur_t).astype(BF16)
        dk_band = _rope_bwd(jnp.concatenate(dk_pairs, axis=1), *band_t)
        dv_band = jnp.concatenate(dv_pairs, axis=1)
        dproj_ref[:, OFF_K:OFF_V] = (dk_band[CHUNK:] + carry[:, 0:256]).astype(BF16)
        dproj_ref[:, OFF_V:OFF_ZB] = (dv_band[CHUNK:] + carry[:, 256:512]).astype(BF16)
        carry[:, 0:256] = dk_band[:CHUNK]
        carry[:, 256:512] = dv_band[:CHUNK]

        @pl.when(i == nb - 1)
        def _():
            t = lax.broadcasted_iota(jnp.int32, (CHUNK, CHUNK), 0)
            tp = lax.broadcasted_iota(jnp.int32, (CHUNK, CHUNK), 1)
            for g in range(A_GROUPS):
                dw_ref[g] = jnp.where(tp <= t, dw_ref[g], 0.0)
                db_ref[pl.ds(g, 1), :] = jnp.sum(dbcol[g].T, axis=0, keepdims=True)

    blk = sp["blk"]
    return pl.pallas_call(
        body, name="mixer_bwd", grid=(nb,),
        in_specs=[sp["cur"], sp["prev_kv"], pl.BlockSpec((CHUNK, D_MODEL), lambda i: (blk(i), 0)), *sp["tabs"],
                  sp["vec"], sp["vec"], sp["wsp"], sp["bsp"], sp["smem"]],
        out_specs=[pl.BlockSpec((CHUNK, D_IN), lambda i: (blk(i), 0)),
                   pl.BlockSpec((8, D_A), lambda i: (0, 0)),
                   pl.BlockSpec((A_GROUPS, CHUNK, CHUNK), lambda i: (0, 0, 0)),
                   pl.BlockSpec((A_GROUPS, CHUNK), lambda i: (0, 0)),
                   pl.BlockSpec((16, 128), lambda i: (0, 0))],
        out_shape=[SDS((s, D_IN), BF16), SDS((8, D_A), F32), SDS((A_GROUPS, CHUNK, CHUNK), F32),
                   SDS((A_GROUPS, CHUNK), F32), SDS((16, 128), F32)],
        scratch_shapes=[pltpu.VMEM((A_GROUPS, CHUNK, CHUNK), F32), pltpu.VMEM((A_GROUPS, CHUNK, CHUNK), F32),
                        pltpu.VMEM((CHUNK, 512), F32)],
        compiler_params=_params("arbitrary"),
    )(proj, proj, dy, *tabs, *tabs, ln_g, ln_b, w_sp, b_sp, sinks)


def _wgrad(name, a, b):
    s, m = a.shape
    n = b.shape[1]
    bm, bt = 512, 512
    steps = s // bt

    def body(a_ref, b_ref, out_ref, acc):
        t = pl.program_id(1)

        @pl.when(t == 0)
        def _():
            acc[...] = jnp.zeros_like(acc)

        acc[...] += lax.dot_general(a_ref[...], b_ref[...], TN, preferred_element_type=F32)

        @pl.when(t == steps - 1)
        def _():
            out_ref[...] = acc[...].astype(out_ref.dtype)

    return pl.pallas_call(
        body, name=name, grid=(m // bm, steps),
        in_specs=[pl.BlockSpec((bt, bm), lambda i, t: (t, i)), pl.BlockSpec((bt, n), lambda i, t: (t, 0))],
        out_specs=pl.BlockSpec((bm, n), lambda i, t: (i, 0)),
        out_shape=SDS((m, n), BF16),
        scratch_shapes=[pltpu.VMEM((bm, n), F32)],
        compiler_params=_params("parallel", "arbitrary"),
    )(a, b)


def _in_proj_bwd(dproj, wt, x, dx1, scale, norm_g):
    s = x.shape[0]
    tm, tk = 512, 512
    ksteps = D_IN // tk

    def body(dp_ref, wt_ref, x_ref, dx1_ref, sc_ref, g_ref, gx_ref, sums_ref, acc):
        i, k = pl.program_id(0), pl.program_id(1)

        @pl.when((i == 0) & (k == 0))
        def _():
            sums_ref[...] = jnp.zeros_like(sums_ref)

        @pl.when(k == 0)
        def _():
            acc[...] = jnp.zeros_like(acc)

        acc[...] += jnp.dot(dp_ref[...], wt_ref[...], preferred_element_type=F32)

        @pl.when(k == ksteps - 1)
        def _():
            dh = acc[...]
            xv = x_ref[...]
            r = lax.rsqrt(jnp.mean(xv * xv, axis=-1, keepdims=True) + NORM_EPS)
            xn = xv * r
            hn = xn * g_ref[...]
            d_hn = dh * (1.0 + sc_ref[...])
            d_xn = d_hn * g_ref[...]
            gx_ref[...] = dx1_ref[...] + r * (d_xn - xn * jnp.mean(d_xn * xn, axis=-1, keepdims=True))
            sums_ref[0:1, :] += jnp.sum(dh, axis=0, keepdims=True)
            sums_ref[1:2, :] += jnp.sum(dh * hn, axis=0, keepdims=True)
            sums_ref[2:3, :] += jnp.sum(d_hn * xn, axis=0, keepdims=True)

    tile = pl.BlockSpec((tm, D_MODEL), lambda i, k: (i, 0))
    row = pl.BlockSpec((1, D_MODEL), lambda i, k: (0, 0))
    return pl.pallas_call(
        body, name="in_proj_bwd", grid=(s // tm, ksteps),
        in_specs=[pl.BlockSpec((tm, tk), lambda i, k: (i, k)), pl.BlockSpec((tk, D_MODEL), lambda i, k: (k, 0)),
                  tile, tile, row, row],
        out_specs=[tile, pl.BlockSpec((8, D_MODEL), lambda i, k: (0, 0))],
        out_shape=[SDS((s, D_MODEL), F32), SDS((8, D_MODEL), F32)],
        scratch_shapes=[pltpu.VMEM((tm, D_MODEL), F32)],
        compiler_params=_params("arbitrary", "arbitrary"),
    )(dproj, wt, x, dx1, scale, norm_g)


def _pair_sum(kept, got):
    _, m, n = kept.shape

    def body(a_ref, b_ref, out_ref):
        out_ref[...] = (a_ref[...].astype(F32) + b_ref[...].astype(F32)).astype(BF16)

    blk = pl.BlockSpec((1, m, n), lambda q: (q, 0, 0))
    return pl.pallas_call(
        body, name=f"pair_sum_{m}", grid=(4,), in_specs=[blk, blk], out_specs=blk,
        out_shape=SDS(kept.shape, BF16), compiler_params=_params("parallel"),
    )(kept, got)


def _sum_chips(ref):
    return ((ref[0].astype(F32) + ref[1].astype(F32)) + ref[2].astype(F32)) + ref[3].astype(F32)


def _adam_w_in(parts, w, m, v):
    n = w.shape[1]
    tc = 256

    def body(p_ref, w_ref, m_ref, v_ref, g_ref, d_ref, nm_ref, nv_ref):
        g = _sum_chips(p_ref).T
        g_ref[...] = g
        d_ref[...], nm_ref[...], nv_ref[...] = _adamw(w_ref[...], g, m_ref[...], v_ref[...])

    blk = pl.BlockSpec((tc, n), lambda j: (j, 0))
    return pl.pallas_call(
        body, name="adam_w_in", grid=(D_MODEL // tc,),
        in_specs=[pl.BlockSpec((4, n, tc), lambda j: (0, 0, j)), blk, blk, blk],
        out_specs=[blk] * 4, out_shape=[SDS(w.shape, F32)] * 4,
        compiler_params=_params("parallel"),
    )(parts, w, m, v)


def _adam_w_out(parts, w, m, v):
    rows = w.shape[0]
    tr = 64

    def body(p_ref, w_ref, m_ref, v_ref, g_ref, d_ref, nm_ref, nv_ref):
        g = _sum_chips(p_ref)
        g_ref[...] = g
        d_ref[...], nm_ref[...], nv_ref[...] = _adamw(w_ref[...], g, m_ref[...], v_ref[...])

    blk = pl.BlockSpec((tr, D_MODEL), lambda j: (j, 0))
    return pl.pallas_call(
        body, name="adam_w_out", grid=(rows // tr,),
        in_specs=[pl.BlockSpec((4, tr, D_MODEL), lambda j: (0, j, 0)), blk, blk, blk],
        out_specs=[blk] * 4, out_shape=[SDS(w.shape, F32)] * 4,
        compiler_params=_params("parallel"),
    )(parts, w, m, v)


def _adam_ada(name, cact, dmod, w, m, v):
    n = w.shape[1]
    tr = 512

    def body(c_ref, dm_ref, w_ref, m_ref, v_ref, g_ref, d_ref, nm_ref, nv_ref):
        pad_c = jnp.concatenate([c_ref[...], jnp.zeros_like(c_ref)], axis=0).astype(BF16)
        pad_d = jnp.concatenate([dm_ref[...], jnp.zeros_like(dm_ref)], axis=0).astype(BF16)
        g = lax.dot_general(pad_c, pad_d, TN, preferred_element_type=F32)
        g_ref[...] = g
        d_ref[...], nm_ref[...], nv_ref[...] = _adamw(w_ref[...], g, m_ref[...], v_ref[...])

    blk = pl.BlockSpec((tr, n), lambda j: (j, 0))
    return pl.pallas_call(
        body, name=name, grid=(D_MODEL // tr,),
        in_specs=[pl.BlockSpec((N_DEV, tr), lambda j: (0, j)), pl.BlockSpec((N_DEV, n), lambda j: (0, 0)),
                  blk, blk, blk],
        out_specs=[blk] * 4, out_shape=[SDS(w.shape, F32)] * 4,
        compiler_params=_params("parallel"),
    )(cact, dmod, w, m, v)


def _adam_small(parts, w, m, v):
    def body(p_ref, w_ref, m_ref, v_ref, g_ref, d_ref, nm_ref, nv_ref):
        g = p_ref[0]
        for j in range(1, N_DEV):
            g = g + p_ref[j]
        g_ref[...] = g
        d_ref[...], nm_ref[...], nv_ref[...] = _adamw(w_ref[...], g, m_ref[...], v_ref[...])

    vmem = pl.BlockSpec(memory_space=pltpu.VMEM)
    return pl.pallas_call(
        body, name="adam_small", in_specs=[vmem] * 4, out_specs=[vmem] * 4,
        out_shape=[SDS(w.shape, F32)] * 4, compiler_params=_params(),
    )(parts, w, m, v)


def _pack_small(w_sp, norm_g, ln_g, ln_b, b_sp, sinks, fng, b_ada, b_ada_f):
    row_bsp = jnp.concatenate([b_sp.reshape(1, 1024), sinks.reshape(1, 16), jnp.zeros((1, 1008), F32)], axis=1)
    return jnp.concatenate([
        w_sp.reshape(64, D_MODEL), norm_g.reshape(1, D_MODEL),
        jnp.concatenate([ln_g.reshape(1, D_A), ln_b.reshape(1, D_A)], axis=1), row_bsp, fng.reshape(1, D_MODEL),
        b_ada.reshape(3, D_MODEL), b_ada_f.reshape(2, D_MODEL),
        jnp.zeros((SMALL_ROWS - 73, D_MODEL), F32)], axis=0)


def _unpack_small(p):
    return dict(
        w_spatial=p[ROW_WSP:ROW_WSP + 64].reshape(1, A_GROUPS, CHUNK, CHUNK),
        norm_g=p[ROW_NG].reshape(1, D_MODEL),
        ln_v_g=p[ROW_LN, :D_A].reshape(1, D_A), ln_v_b=p[ROW_LN, D_A:].reshape(1, D_A),
        b_spatial=p[ROW_BSP, :1024].reshape(1, A_GROUPS, CHUNK), sinks=p[ROW_BSP, 1024:1040].reshape(1, 16),
        final_norm_g=p[ROW_FNG].reshape(D_MODEL),
        b_ada=p[ROW_BADA:ROW_BADA + 3].reshape(1, 3 * D_MODEL), b_ada_final=p[ROW_BADAF:ROW_BADAF + 2].reshape(2 * D_MODEL),
    )


def kernel(x, c, w_ada, b_ada, norm_g, w_in, ln_v_g, ln_v_b, w_spatial, b_spatial, sinks, w_out, w_ada_final, b_ada_final, final_norm_g, loss_target, m_w_ada, m_b_ada, m_norm_g, m_w_in, m_ln_v_g, m_ln_v_b, m_w_spatial, m_b_spatial, m_sinks, m_w_out, m_w_ada_final, m_b_ada_final, m_final_norm_g, v_w_ada, v_b_ada, v_norm_g, v_w_in, v_ln_v_g, v_ln_v_b, v_w_spatial, v_b_spatial, v_sinks, v_w_out, v_w_ada_final, v_b_ada_final, v_final_norm_g):
    seq = x.shape[1]
    me = 4 * lax.axis_index("x") + 2 * lax.axis_index("y") + lax.axis_index("c")
    x2, tgt = x[0], loss_target[0]
    fng = final_norm_g.reshape(1, D_MODEL)

    n_ada, n_ada_f = w_ada.shape[2], w_ada_final.shape[1]
    cact, mod, mod_f = _ada_exchange(c, w_ada[0], b_ada.reshape(N_DEV, n_ada), w_ada_final,
                                     b_ada_final.reshape(N_DEV, n_ada_f))
    cact = cact.reshape(N_DEV, D_MODEL)
    mod, mod_f = mod.reshape(1, 3 * D_MODEL), mod_f.reshape(1, 2 * D_MODEL)
    shift, scale, gate = mod[:, :D_MODEL], mod[:, D_MODEL:2 * D_MODEL], mod[:, 2 * D_MODEL:]
    shift_f, scale_f = mod_f[:, :D_MODEL], mod_f[:, D_MODEL:]

    wt_shard, wo_shard = _prep_weights(w_in[0], w_out[0])
    wt, wo = _all_gather("gather_weights", [wt_shard, wo_shard], pl.ANY)
    wt, wo = wt.reshape(D_IN, D_MODEL), wo.reshape(D_MODEL, D_MODEL)

    tabs = _rope_tables(seq)
    sinks_v = sinks.reshape(16)
    h, proj = _in_proj(x2, shift, scale, norm_g, wt)
    y = _mixer_fwd(proj, tabs, ln_v_g, ln_v_b, w_spatial[0], b_spatial[0], sinks_v)
    dx1, do, dy, sums_o = _out_proj_loss(y, x2, tgt, wo, gate, shift_f, scale_f, fng)
    loss = lax.psum(0.5 * jnp.sum(sums_o[4]) / D_MODEL, ("x", "y", "c"))

    dproj, d_ln, d_wsp, d_bsp, d_sinks = _mixer_bwd(proj, dy, tabs, ln_v_g, ln_v_b, w_spatial[0], b_spatial[0], sinks_v)
    g_wo = _wgrad("wgrad_out", y, do)
    g_wt = _wgrad("wgrad_in", dproj, h)
    grad_x, sums_i = _in_proj_bwd(dproj, wt, x2, dx1, scale, norm_g)

    kept, got = _rs_pair([g_wt.reshape(N_DEV, D_IN // N_DEV, D_MODEL), g_wo.reshape(N_DEV, D_MODEL // N_DEV, D_MODEL)])
    parts_in, parts_out = _rs_chips([_pair_sum(k, g) for k, g in zip(kept, got)])
    g_w_in, d_w_in, nm_w_in, nv_w_in = _adam_w_in(parts_in, w_in[0], m_w_in[0], v_w_in[0])
    g_w_out, d_w_out, nm_w_out, nv_w_out = _adam_w_out(parts_out, w_out[0], m_w_out[0], v_w_out[0])

    dmod = jnp.concatenate([sums_i[0], sums_i[1], sums_o[0]])
    dmod_f = jnp.concatenate([sums_o[1], sums_o[2]])
    small = _pack_small(d_wsp, sums_i[2], d_ln[0], d_ln[1], d_bsp, d_sinks[:, 0], sums_o[3], dmod, dmod_f)
    (small_all,) = _all_gather("gather_small", [small], pltpu.VMEM)
    packed = [_pack_small(*t) for t in (
        (w_spatial, norm_g, ln_v_g, ln_v_b, b_spatial, sinks, final_norm_g, b_ada, b_ada_final),
        (m_w_spatial, m_norm_g, m_ln_v_g, m_ln_v_b, m_b_spatial, m_sinks, m_final_norm_g, m_b_ada, m_b_ada_final),
        (v_w_spatial, v_norm_g, v_ln_v_g, v_ln_v_b, v_b_spatial, v_sinks, v_final_norm_g, v_b_ada, v_b_ada_final))]
    g_s, d_s, nm_s, nv_s = [_unpack_small(p) for p in _adam_small(small_all, *packed)]

    dmod_all = small_all[:, ROW_BADA:ROW_BADA + 3].reshape(N_DEV, 3 * D_MODEL)
    dmod_f_all = small_all[:, ROW_BADAF:ROW_BADAF + 2].reshape(N_DEV, 2 * D_MODEL)
    dmod_mine = lax.dynamic_slice_in_dim(dmod_all, me * n_ada, n_ada, axis=1)
    dmod_f_mine = lax.dynamic_slice_in_dim(dmod_f_all, me * n_ada_f, n_ada_f, axis=1)
    ada = _adam_ada("adam_w_ada", cact, dmod_mine, w_ada[0], m_w_ada[0], v_w_ada[0])
    ada_f = _adam_ada("adam_w_ada_final", cact, dmod_f_mine, w_ada_final, m_w_ada_final, v_w_ada_final)

    def leaves(k):
        small_k = (g_s, d_s, nm_s, nv_s)[k]
        return (ada[k][None], small_k["b_ada"], small_k["norm_g"], (g_w_in, d_w_in, nm_w_in, nv_w_in)[k][None],
                small_k["ln_v_g"], small_k["ln_v_b"], small_k["w_spatial"], small_k["b_spatial"], small_k["sinks"],
                (g_w_out, d_w_out, nm_w_out, nv_w_out)[k][None], ada_f[k], small_k["b_ada_final"],
                small_k["final_norm_g"])

    return (loss, grad_x[None], *leaves(0), *leaves(1), *leaves(2), *leaves(3))
```

```python
import functools

import jax
import jax.numpy as jnp
from jax import lax
from jax.experimental import pallas as pl
from jax.experimental.pallas import tpu as pltpu

D_MODEL = 2048
D_IN = 5632
D_A = 1024
CHUNK = 128
A_GROUPS = 8
HEAD_DIM = 64
N_KV_HEADS = 4
N_DEV = 8
ROPE_THETA = 10000.0
NORM_EPS = 1e-5
ATTN_SCALE = HEAD_DIM ** -0.5

ADAM_LR = 0.001
ADAM_B1 = 0.9
ADAM_B2 = 0.999
ADAM_EPS = 1e-08
ADAM_WD = 0.01
ADAM_STEP = 10

OFF_U, OFF_VA, OFF_ZA, OFF_Q, OFF_K, OFF_V, OFF_ZB = 0, 1024, 2048, 3072, 4096, 4352, 4608

SMALL_ROWS = 80
ROW_WSP, ROW_NG, ROW_LN, ROW_BSP, ROW_FNG, ROW_BADA, ROW_BADAF = 0, 64, 65, 66, 67, 68, 71

V7X_VMEM_LIMIT_BYTES = 56 * 1024 * 1024

F32 = jnp.float32
BF16 = jnp.bfloat16
MESH = pl.DeviceIdType.MESH
SDS = jax.ShapeDtypeStruct
NT = (((1,), (1,)), ((), ()))
TN = (((0,), (0,)), ((), ()))


def _params(*semantics):
    return pltpu.CompilerParams(dimension_semantics=semantics or None, vmem_limit_bytes=V7X_VMEM_LIMIT_BYTES)


def _mesh_pos():
    return lax.axis_index("x"), lax.axis_index("y"), lax.axis_index("c")


def _sigmoid(z):
    return 1.0 / (1.0 + jnp.exp(-z))


def _adamw(w, g, m, v):
    m = ADAM_B1 * m + (1.0 - ADAM_B1) * g
    v = ADAM_B2 * v + (1.0 - ADAM_B2) * (g * g)
    m_hat = m / (1.0 - ADAM_B1 ** ADAM_STEP)
    v_hat = v / (1.0 - ADAM_B2 ** ADAM_STEP)
    delta = -ADAM_LR * (m_hat / (jnp.sqrt(v_hat) + ADAM_EPS) + ADAM_WD * w)
    return delta, m, v


def _all_gather(name, blocks, memory_space):
    n_arr = len(blocks)

    def body(*refs):
        ins, outs = refs[:n_arr], refs[n_arr:2 * n_arr]
        send_sems, recv_sems, local_sems = refs[2 * n_arr:]
        x, y, c = _mesh_pos()
        me, sibling = (x, y, c), (x, y, 1 - c)
        chips = [(1 - x, y), (x, 1 - y), (1 - x, 1 - y)]

        def slot(p):
            return 4 * p[0] + 2 * p[1] + p[2]

        def copy(a, k, block, to, src=None):
            dst = outs[a].at[slot(block)]
            return pltpu.make_async_remote_copy(
                src_ref=dst if src is None else src, dst_ref=dst,
                send_sem=send_sems.at[a, k], recv_sem=recv_sems.at[a, k],
                device_id=to, device_id_type=MESH)

        mine = [pltpu.make_async_copy(ins[a], outs[a].at[slot(me)], local_sems.at[a]) for a in range(n_arr)]
        for cp in mine:
            cp.start()
        first = []
        for a in range(n_arr):
            first.append(copy(a, 0, me, sibling, src=ins[a]))
            first += [copy(a, 1 + j, me, (*chip, c), src=ins[a]) for j, chip in enumerate(chips)]
        for cp in first:
            cp.start()
        passed = []
        for j, chip in enumerate(chips):
            for a in range(n_arr):
                copy(a, 1 + j, (*chip, c), me).wait_recv()
                fwd = copy(a, 4 + j, (*chip, c), sibling)
                fwd.start()
                passed.append(fwd)
        for a in range(n_arr):
            copy(a, 0, sibling, me).wait_recv()
            for j, chip in enumerate(chips):
                copy(a, 4 + j, (*chip, 1 - c), me).wait_recv()
        for cp in first + passed:
            cp.wait_send()
        for cp in mine:
            cp.wait()

    spec = pl.BlockSpec(memory_space=memory_space)
    return pl.pallas_call(
        body, name=name,
        out_shape=[SDS((N_DEV,) + b.shape, b.dtype) for b in blocks],
        in_specs=[spec] * n_arr, out_specs=[spec] * n_arr,
        scratch_shapes=[pltpu.SemaphoreType.DMA((n_arr, 7)), pltpu.SemaphoreType.DMA((n_arr, 7)),
                        pltpu.SemaphoreType.DMA((n_arr,))],
        compiler_params=_params(),
    )(*blocks)


def _ada_exchange(c, w_ada, b_ada8, w_ada_f, b_ada_f8):
    n1, n2 = w_ada.shape[1], w_ada_f.shape[1]

    def body(c_ref, w1_ref, b1_ref, w2_ref, b2_ref, cact_ref, mod_ref, modf_ref,
             cact_buf, res1, res2, send1, send2, sems_s, sems_r):
        x, y, c_pos = _mesh_pos()
        me = 4 * x + 2 * y + c_pos
        flips = [(k >> 2 & 1, k >> 1 & 1, k & 1) for k in range(1, N_DEV)]

        def peer(f):
            return (1 - x if f[0] else x, 1 - y if f[1] else y, 1 - c_pos if f[2] else c_pos)

        cv = c_ref[...]
        cact = cv * _sigmoid(cv)
        cact_buf[...] = cact
        cact_ref[me] = cact

        def rdma(phase, k, src, dst, f):
            return pltpu.make_async_remote_copy(src_ref=src, dst_ref=dst, send_sem=sems_s.at[phase, k],
                                                recv_sem=sems_r.at[phase, k], device_id=peer(f), device_id_type=MESH)

        gather = [rdma(0, k, cact_buf, cact_ref.at[me], f) for k, f in enumerate(flips)]
        for cp in gather:
            cp.start()
        for cp in gather:
            cp.wait_recv()
        for cp in gather:
            cp.wait_send()

        rid = lax.broadcasted_iota(jnp.int32, (N_DEV, D_MODEL), 0)
        rows = jnp.zeros((N_DEV, D_MODEL), F32)
        for j in range(N_DEV):
            rows = jnp.where(rid == j, jnp.broadcast_to(cact_ref[j], (N_DEV, D_MODEL)), rows)
        rows = rows.astype(BF16)
        res1[...] = jnp.dot(rows, w1_ref[...].astype(BF16), preferred_element_type=F32) + b1_ref[pl.ds(me, 1), :]
        res2[...] = jnp.dot(rows, w2_ref[...].astype(BF16), preferred_element_type=F32) + b2_ref[pl.ds(me, 1), :]
        for j in range(N_DEV):
            send1[j] = res1[pl.ds(j, 1), :]
            send2[j] = res2[pl.ds(j, 1), :]
        mod_ref[me] = send1[me]
        modf_ref[me] = send2[me]
        scatter = []
        for k, f in enumerate(flips):
            to = me ^ (k + 1)
            scatter.append(rdma(1, k, send1.at[to], mod_ref.at[me], f))
            scatter.append(rdma(2, k, send2.at[to], modf_ref.at[me], f))
        for cp in scatter:
            cp.start()
        for cp in scatter:
            cp.wait_recv()
        for cp in scatter:
            cp.wait_send()

    vmem = pl.BlockSpec(memory_space=pltpu.VMEM)
    return pl.pallas_call(
        body, name="ada_exchange",
        out_shape=[SDS((N_DEV, 1, D_MODEL), F32), SDS((N_DEV, 1, n1), F32), SDS((N_DEV, 1, n2), F32)],
        in_specs=[vmem] * 5, out_specs=[vmem] * 3,
        scratch_shapes=[pltpu.VMEM((1, D_MODEL), F32), pltpu.VMEM((N_DEV, n1), F32), pltpu.VMEM((N_DEV, n2), F32),
                        pltpu.VMEM((N_DEV, 1, n1), F32), pltpu.VMEM((N_DEV, 1, n2), F32),
                        pltpu.SemaphoreType.DMA((3, 7)), pltpu.SemaphoreType.DMA((3, 7))],
        compiler_params=_params(),
    )(c, w_ada, b_ada8, w_ada_f, b_ada_f8)


def _rs_pair(name, grad):
    def body(g_ref, got_ref, send_sems, recv_sems):
        x, y, c = _mesh_pos()
        copies = [pltpu.make_async_remote_copy(
            src_ref=g_ref.at[2 * q + 1 - c], dst_ref=got_ref.at[q], send_sem=send_sems.at[q], recv_sem=recv_sems.at[q],
            device_id=(x, y, 1 - c), device_id_type=MESH) for q in range(4)]
        for cp in copies:
            cp.start()
        for cp in copies:
            cp.wait_recv()
        for cp in copies:
            cp.wait_send()

    hbm = pl.BlockSpec(memory_space=pl.ANY)
    return pl.pallas_call(
        body, name=name, out_shape=SDS((4,) + grad.shape[1:], grad.dtype), in_specs=[hbm], out_specs=hbm,
        scratch_shapes=[pltpu.SemaphoreType.DMA((4,)), pltpu.SemaphoreType.DMA((4,))],
        compiler_params=_params(),
    )(grad)


def _chip_scatter(pair_ref, parts_ref, send_sems, recv_sems):
    x, y, c = _mesh_pos()
    chips = [(1 - x, y), (x, 1 - y), (1 - x, 1 - y)]
    return [pltpu.make_async_remote_copy(
        src_ref=pair_ref.at[2 * cx + cy], dst_ref=parts_ref.at[j], send_sem=send_sems.at[j], recv_sem=recv_sems.at[j],
        device_id=(cx, cy, c), device_id_type=MESH) for j, (cx, cy) in enumerate(chips)]


def _scatter_scratch():
    return [pltpu.SemaphoreType.DMA((3,)), pltpu.SemaphoreType.DMA((3,))]


def _prep_weights(w_in, w_out):
    d, n = w_in.shape
    steps = 4

    def body(wi_ref, wo_ref, wt_ref, wob_ref):
        wt_ref[...] = wi_ref[...].T.astype(BF16)
        wob_ref[...] = wo_ref[...].astype(BF16)

    return pl.pallas_call(
        body, name="prep_weights", grid=(steps,),
        in_specs=[pl.BlockSpec((d // steps, n), lambda i: (i, 0)),
                  pl.BlockSpec((w_out.shape[0] // steps, d), lambda i: (i, 0))],
        out_specs=[pl.BlockSpec((n, d // steps), lambda i: (0, i)),
                   pl.BlockSpec((w_out.shape[0] // steps, d), lambda i: (i, 0))],
        out_shape=[SDS((n, d), BF16), SDS(w_out.shape, BF16)],
        compiler_params=_params("parallel"),
    )(w_in, w_out)


def _in_proj(x, shift, scale, norm_g, wt):
    s = x.shape[0]
    tm, tn = 512, 512

    def body(x_ref, shift_ref, scale_ref, g_ref, wt_ref, h_ref, proj_ref, h_scr):
        @pl.when(pl.program_id(1) == 0)
        def _():
            xv = x_ref[...]
            r = lax.rsqrt(jnp.mean(xv * xv, axis=-1, keepdims=True) + NORM_EPS)
            h = ((xv * r) * g_ref[...]) * (1.0 + scale_ref[...]) + shift_ref[...]
            hb = h.astype(BF16)
            h_scr[...] = hb
            h_ref[...] = hb

        proj_ref[...] = lax.dot_general(h_scr[...], wt_ref[...], NT, preferred_element_type=F32).astype(BF16)

    row = pl.BlockSpec((1, D_MODEL), lambda i, j: (0, 0))
    return pl.pallas_call(
        body, name="in_proj", grid=(s // tm, D_IN // tn),
        in_specs=[pl.BlockSpec((tm, D_MODEL), lambda i, j: (i, 0)), row, row, row,
                  pl.BlockSpec((tn, D_MODEL), lambda i, j: (j, 0))],
        out_specs=[pl.BlockSpec((tm, D_MODEL), lambda i, j: (i, 0)), pl.BlockSpec((tm, tn), lambda i, j: (i, j))],
        out_shape=[SDS((s, D_MODEL), BF16), SDS((s, D_IN), BF16)],
        scratch_shapes=[pltpu.VMEM((tm, D_MODEL), BF16)],
        compiler_params=_params("parallel", "arbitrary"),
    )(x, shift, scale, norm_g, wt)


def _rope_tables(seq):
    inv_freq = ROPE_THETA ** (-jnp.arange(0, HEAD_DIM, 2, dtype=F32) / HEAD_DIM)
    ang = jnp.arange(seq, dtype=F32)[:, None] * inv_freq[None, :]
    cos, sin, zero = jnp.cos(ang), jnp.sin(ang), jnp.zeros_like(ang)
    return (jnp.concatenate([cos] * 4, axis=1), jnp.concatenate([-sin, zero, -sin, zero], axis=1),
            jnp.concatenate([zero, sin, zero, sin], axis=1))


def _rope(v, cos, sin_lo, sin_hi):
    width = v.shape[1]
    rep = (1, width // 128)
    return (v * jnp.tile(cos, rep) + pltpu.roll(v, width - 32, 1) * jnp.tile(sin_lo, rep)
            + pltpu.roll(v, 32, 1) * jnp.tile(sin_hi, rep))


def _rope_bwd(d, cos, sin_lo, sin_hi):
    width = d.shape[1]
    rep = (1, width // 128)
    return (d * jnp.tile(cos, rep) + pltpu.roll(d * jnp.tile(sin_lo, rep), 32, 1)
            + pltpu.roll(d * jnp.tile(sin_hi, rep), width - 32, 1))


def _layer_norm(v, g, b):
    mu = jnp.mean(v, axis=-1, keepdims=True)
    vc = v - mu
    rstd = lax.rsqrt(jnp.mean(vc * vc, axis=-1, keepdims=True) + NORM_EPS)
    vhat = vc * rstd
    return vhat * g + b, vhat, rstd


def _tril_bf16(w_ref, g):
    t = lax.broadcasted_iota(jnp.int32, (CHUNK, CHUNK), 0)
    tp = lax.broadcasted_iota(jnp.int32, (CHUNK, CHUNK), 1)
    return jnp.where(tp <= t, w_ref[g], 0.0).astype(BF16)


def _bias_columns(b_ref, out_ref):
    for g in range(A_GROUPS):
        out_ref[g] = jnp.broadcast_to(b_ref[pl.ds(g, 1), :], (CHUNK, CHUNK)).T


def _band_valid(first_block):
    qi = lax.broadcasted_iota(jnp.int32, (4 * CHUNK, 2 * CHUNK), 0) & (CHUNK - 1)
    kj = lax.broadcasted_iota(jnp.int32, (4 * CHUNK, 2 * CHUNK), 1)
    rel = qi + CHUNK - kj
    return (rel >= 0) & (rel < CHUNK) & ((kj >= CHUNK) | jnp.logical_not(first_block))


def _low_lanes():
    return lax.broadcasted_iota(jnp.int32, (1, 128), 1) < HEAD_DIM


def _stack_heads(pair_a, pair_b):
    lo = _low_lanes()
    return jnp.concatenate([jnp.where(lo, pair_a, 0.0), jnp.where(lo, 0.0, pair_a),
                            jnp.where(lo, pair_b, 0.0), jnp.where(lo, 0.0, pair_b)], axis=0).astype(BF16)


def _unstack_heads(st):
    lo = _low_lanes()
    return (jnp.where(lo, st[0:128], st[128:256]), jnp.where(lo, st[256:384], st[384:512]))


def _dup_kv_head(band, gk):
    pair = band[:, (gk // 2) * 128:(gk // 2 + 1) * 128]
    lo = _low_lanes()
    one = jnp.where(lo if gk % 2 == 0 else jnp.logical_not(lo), pair, 0.0)
    return (one + pltpu.roll(one, HEAD_DIM, 1)).astype(BF16)


def _fold_kv_head(dup_grad, gk):
    both = dup_grad + pltpu.roll(dup_grad, HEAD_DIM, 1)
    lo = _low_lanes()
    return jnp.where(lo if gk % 2 == 0 else jnp.logical_not(lo), both, 0.0)


def _attn_probs(q_st, k_dup, sink_col, valid):
    s = lax.dot_general(q_st, k_dup, NT, preferred_element_type=F32) * ATTN_SCALE
    s = jnp.where(valid, s, -jnp.inf)
    m = jnp.maximum(jnp.max(s, axis=-1, keepdims=True), sink_col)
    p = jnp.exp(s - m)
    e_sink = jnp.exp(sink_col - m)
    inv = 1.0 / (jnp.sum(p, axis=-1, keepdims=True) + e_sink)
    return p * inv, e_sink * inv


def _sink_column(sinks_ref, gk):
    return jnp.concatenate([jnp.full((CHUNK, 1), sinks_ref[4 * gk + r], F32) for r in range(4)], axis=0)


def _mixer_specs(nb, rev):
    def blk(i):
        return nb - 1 - i if rev else i

    def prev(i):
        return jnp.maximum(blk(i) - 1, 0)

    tab = pl.BlockSpec((CHUNK, 128), lambda i: (blk(i), 0))
    tab_prev = pl.BlockSpec((CHUNK, 128), lambda i: (prev(i), 0))
    return dict(
        cur=pl.BlockSpec((CHUNK, D_IN), lambda i: (blk(i), 0)),
        prev_kv=pl.BlockSpec((CHUNK, 2 * 256), lambda i: (prev(i), OFF_K // 512)),
        tabs=[tab] * 3 + [tab_prev] * 3,
        vec=pl.BlockSpec((1, D_A), lambda i: (0, 0)),
        wsp=pl.BlockSpec((A_GROUPS, CHUNK, CHUNK), lambda i: (0, 0, 0)),
        bsp=pl.BlockSpec((A_GROUPS, CHUNK), lambda i: (0, 0)),
        smem=pl.BlockSpec(memory_space=pltpu.SMEM),
        blk=blk,
    )


def _mixer_fwd(proj, tabs, ln_g, ln_b, w_sp, b_sp, sinks):
    s = proj.shape[0]
    nb = s // CHUNK
    sp = _mixer_specs(nb, rev=False)

    def body(cur_ref, pkv_ref, c_ref, s1_ref, s2_ref, cp_ref, s1p_ref, s2p_ref, lg_ref, lb_ref, w_ref, b_ref,
             sinks_ref, y_ref, bcol):
        i = pl.program_id(0)

        @pl.when(i == 0)
        def _():
            _bias_columns(b_ref, bcol)

        vln, _, _ = _layer_norm(cur_ref[:, OFF_VA:OFF_ZA].astype(F32), lg_ref[...], lb_ref[...])
        vln = vln.astype(BF16)
        for g in range(A_GROUPS):
            cols = slice(g * 128, (g + 1) * 128)
            sg = jnp.dot(_tril_bf16(w_ref, g), vln[:, cols], preferred_element_type=F32) + bcol[g]
            u = cur_ref[:, OFF_U + g * 128:OFF_U + (g + 1) * 128].astype(F32)
            z = cur_ref[:, OFF_ZA + g * 128:OFF_ZA + (g + 1) * 128].astype(F32)
            y_ref[:, cols] = (u * sg * (z * _sigmoid(z))).astype(BF16)

        cur_t = (c_ref[...], s1_ref[...], s2_ref[...])
        prev_t = (cp_ref[...], s1p_ref[...], s2p_ref[...])
        qr = _rope(cur_ref[:, OFF_Q:OFF_K].astype(F32), *cur_t)
        kr = jnp.concatenate([_rope(pkv_ref[:, 0:256].astype(F32), *prev_t),
                              _rope(cur_ref[:, OFF_K:OFF_V].astype(F32), *cur_t)], axis=0)
        vb = jnp.concatenate([pkv_ref[:, 256:512], cur_ref[:, OFF_V:OFF_ZB]], axis=0).astype(F32)
        valid = _band_valid(i == 0)
        for gk in range(N_KV_HEADS):
            q_st = _stack_heads(qr[:, (2 * gk) * 128:(2 * gk + 1) * 128], qr[:, (2 * gk + 1) * 128:(2 * gk + 2) * 128])
            probs, _ = _attn_probs(q_st, _dup_kv_head(kr, gk), _sink_column(sinks_ref, gk), valid)
            out = jnp.dot(probs.astype(BF16), _dup_kv_head(vb, gk), preferred_element_type=F32)
            pair_a, pair_b = _unstack_heads(out)
            cols = slice(gk * 256, (gk + 1) * 256)
            zb = cur_ref[:, OFF_ZB + gk * 256:OFF_ZB + (gk + 1) * 256].astype(F32)
            yb = jnp.concatenate([pair_a, pair_b], axis=1) * (zb * _sigmoid(zb))
            y_ref[:, D_A + gk * 256:D_A + (gk + 1) * 256] = yb.astype(BF16)

    return pl.pallas_call(
        body, name="mixer_fwd", grid=(nb,),
        in_specs=[sp["cur"], sp["prev_kv"], *sp["tabs"], sp["vec"], sp["vec"], sp["wsp"], sp["bsp"], sp["smem"]],
        out_specs=pl.BlockSpec((CHUNK, D_MODEL), lambda i: (i, 0)),
        out_shape=SDS((s, D_MODEL), BF16),
        scratch_shapes=[pltpu.VMEM((A_GROUPS, CHUNK, CHUNK), F32)],
        compiler_params=_params("arbitrary"),
    )(proj, proj, *tabs, *tabs, ln_g, ln_b, w_sp, b_sp, sinks)


def _out_proj_loss(y, x, target, wo, gate, shift_f, scale_f, fng):
    s = y.shape[0]
    tm = 256

    def body(y_ref, x_ref, t_ref, wo_ref, gate_ref, sh_ref, sc_ref, g_ref, dx1_ref, do_ref, dy_ref, sums_ref):
        @pl.when(pl.program_id(0) == 0)
        def _():
            sums_ref[...] = jnp.zeros_like(sums_ref)

        o = jnp.dot(y_ref[...], wo_ref[...], preferred_element_type=F32)
        gate = gate_ref[...]
        x1 = x_ref[...] + gate * o
        rf = lax.rsqrt(jnp.mean(x1 * x1, axis=-1, keepdims=True) + NORM_EPS)
        x1n = x1 * rf
        hn = x1n * g_ref[...]
        one_sc = 1.0 + sc_ref[...]
        diff = hn * one_sc + sh_ref[...] - t_ref[...]
        dout = diff * (1.0 / D_MODEL)
        d_hn = dout * one_sc
        d_x1n = d_hn * g_ref[...]
        dx1 = rf * (d_x1n - x1n * jnp.mean(d_x1n * x1n, axis=-1, keepdims=True))
        dx1_ref[...] = dx1
        do = (dx1 * gate).astype(BF16)
        do_ref[...] = do
        dy_ref[...] = lax.dot_general(do, wo_ref[...], NT, preferred_element_type=F32).astype(BF16)

        def rowsum(v):
            return jnp.sum(v, axis=0, keepdims=True)

        sums_ref[0:1, :] += rowsum(dx1 * o)
        sums_ref[1:2, :] += rowsum(dout)
        sums_ref[2:3, :] += rowsum(dout * hn)
        sums_ref[3:4, :] += rowsum(d_hn * x1n)
        sums_ref[4:5, :] += rowsum(diff * diff)

    tile = pl.BlockSpec((tm, D_MODEL), lambda i: (i, 0))
    row = pl.BlockSpec((1, D_MODEL), lambda i: (0, 0))
    return pl.pallas_call(
        body, name="out_proj_loss", grid=(s // tm,),
        in_specs=[tile, tile, tile, pl.BlockSpec((D_MODEL, D_MODEL), lambda i: (0, 0)), row, row, row, row],
        out_specs=[tile, tile, tile, pl.BlockSpec((8, D_MODEL), lambda i: (0, 0))],
        out_shape=[SDS((s, D_MODEL), F32), SDS((s, D_MODEL), BF16), SDS((s, D_MODEL), BF16), SDS((8, D_MODEL), F32)],
        compiler_params=_params("arbitrary"),
    )(y, x, target, wo, gate, shift_f, scale_f, fng)


def _mixer_bwd(proj, dy, tabs, ln_g, ln_b, w_sp, b_sp, sinks, pair):
    s = proj.shape[0]
    nb = s // CHUNK
    sp = _mixer_specs(nb, rev=True)

    def body(cur_ref, pkv_ref, dy_ref, c_ref, s1_ref, s2_ref, cp_ref, s1p_ref, s2p_ref, lg_ref, lb_ref, w_ref, b_ref,
             sinks_ref, pair_ref, dproj_ref, dln_ref, dw_ref, db_ref, dsink_ref, parts_ref, bcol, dbcol, carry,
             send_sems, recv_sems):
        i = pl.program_id(0)
        block = nb - 1 - i

        @pl.when(i == 0)
        def _():
            for cp in _chip_scatter(pair_ref, parts_ref, send_sems, recv_sems):
                cp.start()
            _bias_columns(b_ref, bcol)
            dbcol[...] = jnp.zeros_like(dbcol)
            carry[...] = jnp.zeros_like(carry)
            dln_ref[...] = jnp.zeros_like(dln_ref)
            dw_ref[...] = jnp.zeros_like(dw_ref)
            dsink_ref[...] = jnp.zeros_like(dsink_ref)

        vln, vhat, rstd = _layer_norm(cur_ref[:, OFF_VA:OFF_ZA].astype(F32), lg_ref[...], lb_ref[...])
        vln = vln.astype(BF16)
        d_vln = []
        for g in range(A_GROUPS):
            cols = slice(g * 128, (g + 1) * 128)
            w_g = _tril_bf16(w_ref, g)
            sg = jnp.dot(w_g, vln[:, cols], preferred_element_type=F32) + bcol[g]
            u = cur_ref[:, OFF_U + g * 128:OFF_U + (g + 1) * 128].astype(F32)
            z = cur_ref[:, OFF_ZA + g * 128:OFF_ZA + (g + 1) * 128].astype(F32)
            dya = dy_ref[:, cols].astype(F32)
            sig = _sigmoid(z)
            d_ya = dya * (z * sig)
            dproj_ref[:, OFF_ZA + g * 128:OFF_ZA + (g + 1) * 128] = (
                dya * (u * sg) * (sig * (1.0 + z * (1.0 - sig)))).astype(BF16)
            dproj_ref[:, OFF_U + g * 128:OFF_U + (g + 1) * 128] = (d_ya * sg).astype(BF16)
            d_s = d_ya * u
            dbcol[g] += d_s
            d_sb = d_s.astype(BF16)
            dw_ref[g] += lax.dot_general(d_sb, vln[:, cols], NT, preferred_element_type=F32)
            d_vln.append(lax.dot_general(w_g, d_sb, TN, preferred_element_type=F32))
        d_vln = jnp.concatenate(d_vln, axis=1)
        dln_ref[0:1, :] += jnp.sum(d_vln * vhat, axis=0, keepdims=True)
        dln_ref[1:2, :] += jnp.sum(d_vln, axis=0, keepdims=True)
        d_vhat = d_vln * lg_ref[...]
        d_va = rstd * (d_vhat - jnp.mean(d_vhat, axis=-1, keepdims=True)
                       - vhat * jnp.mean(d_vhat * vhat, axis=-1, keepdims=True))
        dproj_ref[:, OFF_VA:OFF_ZA] = d_va.astype(BF16)

        cur_t = (c_ref[...], s1_ref[...], s2_ref[...])
        prev_t = (cp_ref[...], s1p_ref[...], s2p_ref[...])
        band_t = tuple(jnp.concatenate([p, c], axis=0) for p, c in zip(prev_t, cur_t))
        qr = _rope(cur_ref[:, OFF_Q:OFF_K].astype(F32), *cur_t)
        kr = jnp.concatenate([_rope(pkv_ref[:, 0:256].astype(F32), *prev_t),
                              _rope(cur_ref[:, OFF_K:OFF_V].astype(F32), *cur_t)], axis=0)
        vb = jnp.concatenate([pkv_ref[:, 256:512], cur_ref[:, OFF_V:OFF_ZB]], axis=0).astype(F32)
        valid = _band_valid(block == 0)
        dq_pairs = []
        dk_pairs = [jnp.zeros((2 * CHUNK, 128), F32) for _ in range(2)]
        dv_pairs = [jnp.zeros((2 * CHUNK, 128), F32) for _ in range(2)]
        for gk in range(N_KV_HEADS):
            q_st = _stack_heads(qr[:, (2 * gk) * 128:(2 * gk + 1) * 128], qr[:, (2 * gk + 1) * 128:(2 * gk + 2) * 128])
            k_dup, v_dup = _dup_kv_head(kr, gk), _dup_kv_head(vb, gk)
            probs, p_sink = _attn_probs(q_st, k_dup, _sink_column(sinks_ref, gk), valid)
            probs_b = probs.astype(BF16)
            out = jnp.dot(probs_b, v_dup, preferred_element_type=F32)
            yb = jnp.concatenate(_unstack_heads(out), axis=1)
            zb = cur_ref[:, OFF_ZB + gk * 256:OFF_ZB + (gk + 1) * 256].astype(F32)
            dyb = dy_ref[:, D_A + gk * 256:D_A + (gk + 1) * 256].astype(F32)
            sig = _sigmoid(zb)
            dproj_ref[:, OFF_ZB + gk * 256:OFF_ZB + (gk + 1) * 256] = (
                dyb * yb * (sig * (1.0 + zb * (1.0 - sig)))).astype(BF16)
            d_yb = dyb * (zb * sig)
            do_st = _stack_heads(d_yb[:, 0:128], d_yb[:, 128:256])
            dp = lax.dot_general(do_st, v_dup, NT, preferred_element_type=F32)
            delta = jnp.sum(probs * dp, axis=-1, keepdims=True)
            ds = (probs * (dp - delta) * ATTN_SCALE).astype(BF16)
            d_sink = -p_sink * delta
            for r in range(4):
                dsink_ref[4 * gk + r:4 * gk + r + 1, :] += jnp.broadcast_to(
                    jnp.sum(d_sink[r * CHUNK:(r + 1) * CHUNK], axis=0, keepdims=True), (1, 128))
            dq_pairs += list(_unstack_heads(jnp.dot(ds, k_dup, preferred_element_type=F32)))
            dk_pairs[gk // 2] += _fold_kv_head(lax.dot_general(ds, q_st, TN, preferred_element_type=F32), gk)
            dv_pairs[gk // 2] += _fold_kv_head(lax.dot_general(probs_b, do_st, TN, preferred_element_type=F32), gk)
        dproj_ref[:, OFF_Q:OFF_K] = _rope_bwd(jnp.concatenate(dq_pairs, axis=1), *cur_t).astype(BF16)
        dk_band = _rope_bwd(jnp.concatenate(dk_pairs, axis=1), *band_t)
        dv_band = jnp.concatenate(dv_pairs, axis=1)
        dproj_ref[:, OFF_K:OFF_V] = (dk_band[CHUNK:] + carry[:, 0:256]).astype(BF16)
        dproj_ref[:, OFF_V:OFF_ZB] = (dv_band[CHUNK:] + carry[:, 256:512]).astype(BF16)
        carry[:, 0:256] = dk_band[:CHUNK]
        carry[:, 256:512] = dv_band[:CHUNK]

        @pl.when(i == nb - 1)
        def _():
            t = lax.broadcasted_iota(jnp.int32, (CHUNK, CHUNK), 0)
            tp = lax.broadcasted_iota(jnp.int32, (CHUNK, CHUNK), 1)
            for g in range(A_GROUPS):
                dw_ref[g] = jnp.where(tp <= t, dw_ref[g], 0.0)
                db_ref[pl.ds(g, 1), :] = jnp.sum(dbcol[g].T, axis=0, keepdims=True)
            scatter = _chip_scatter(pair_ref, parts_ref, send_sems, recv_sems)
            for cp in scatter:
                cp.wait_recv()
            for cp in scatter:
                cp.wait_send()

    blk = sp["blk"]
    hbm = pl.BlockSpec(memory_space=pl.ANY)
    return pl.pallas_call(
        body, name="mixer_bwd", grid=(nb,),
        in_specs=[sp["cur"], sp["prev_kv"], pl.BlockSpec((CHUNK, D_MODEL), lambda i: (blk(i), 0)), *sp["tabs"],
                  sp["vec"], sp["vec"], sp["wsp"], sp["bsp"], sp["smem"], hbm],
        out_specs=[pl.BlockSpec((CHUNK, D_IN), lambda i: (blk(i), 0)),
                   pl.BlockSpec((8, D_A), lambda i: (0, 0)),
                   pl.BlockSpec((A_GROUPS, CHUNK, CHUNK), lambda i: (0, 0, 0)),
                   pl.BlockSpec((A_GROUPS, CHUNK), lambda i: (0, 0)),
                   pl.BlockSpec((16, 128), lambda i: (0, 0)), hbm],
        out_shape=[SDS((s, D_IN), BF16), SDS((8, D_A), F32), SDS((A_GROUPS, CHUNK, CHUNK), F32),
                   SDS((A_GROUPS, CHUNK), F32), SDS((16, 128), F32), SDS((3,) + pair.shape[1:], pair.dtype)],
        scratch_shapes=[pltpu.VMEM((A_GROUPS, CHUNK, CHUNK), F32), pltpu.VMEM((A_GROUPS, CHUNK, CHUNK), F32),
                        pltpu.VMEM((CHUNK, 512), F32), *_scatter_scratch()],
        compiler_params=_params("arbitrary"),
    )(proj, proj, dy, *tabs, *tabs, ln_g, ln_b, w_sp, b_sp, sinks, pair)


def _wgrad(name, a, b):
    s, m = a.shape
    n = b.shape[1]
    bm, bt = 512, 512
    steps = s // bt

    def body(a_ref, b_ref, out_ref, acc):
        t = pl.program_id(1)

        @pl.when(t == 0)
        def _():
            acc[...] = jnp.zeros_like(acc)

        acc[...] += lax.dot_general(a_ref[...], b_ref[...], TN, preferred_element_type=F32)

        @pl.when(t == steps - 1)
        def _():
            out_ref[...] = acc[...].astype(out_ref.dtype)

    return pl.pallas_call(
        body, name=name, grid=(m // bm, steps),
        in_specs=[pl.BlockSpec((bt, bm), lambda i, t: (t, i)), pl.BlockSpec((bt, n), lambda i, t: (t, 0))],
        out_specs=pl.BlockSpec((bm, n), lambda i, t: (i, 0)),
        out_shape=SDS((m, n), BF16),
        scratch_shapes=[pltpu.VMEM((bm, n), F32)],
        compiler_params=_params("parallel", "arbitrary"),
    )(a, b)


def _in_proj_bwd(dproj, wt, x, dx1, scale, norm_g, pair):
    s = x.shape[0]
    tm, tk = 512, 512
    ksteps = D_IN // tk

    def body(dp_ref, wt_ref, x_ref, dx1_ref, sc_ref, g_ref, pair_ref, gx_ref, sums_ref, parts_ref, acc,
             send_sems, recv_sems):
        i, k = pl.program_id(0), pl.program_id(1)

        @pl.when((i == 0) & (k == 0))
        def _():
            for cp in _chip_scatter(pair_ref, parts_ref, send_sems, recv_sems):
                cp.start()
            sums_ref[...] = jnp.zeros_like(sums_ref)

        @pl.when(k == 0)
        def _():
            acc[...] = jnp.zeros_like(acc)

        acc[...] += jnp.dot(dp_ref[...], wt_ref[...], preferred_element_type=F32)

        @pl.when(k == ksteps - 1)
        def _():
            dh = acc[...]
            xv = x_ref[...]
            r = lax.rsqrt(jnp.mean(xv * xv, axis=-1, keepdims=True) + NORM_EPS)
            xn = xv * r
            hn = xn * g_ref[...]
            d_hn = dh * (1.0 + sc_ref[...])
            d_xn = d_hn * g_ref[...]
            gx_ref[...] = dx1_ref[...] + r * (d_xn - xn * jnp.mean(d_xn * xn, axis=-1, keepdims=True))
            sums_ref[0:1, :] += jnp.sum(dh, axis=0, keepdims=True)
            sums_ref[1:2, :] += jnp.sum(dh * hn, axis=0, keepdims=True)
            sums_ref[2:3, :] += jnp.sum(d_hn * xn, axis=0, keepdims=True)

        @pl.when((i == s // tm - 1) & (k == ksteps - 1))
        def _():
            scatter = _chip_scatter(pair_ref, parts_ref, send_sems, recv_sems)
            for cp in scatter:
                cp.wait_recv()
            for cp in scatter:
                cp.wait_send()

    tile = pl.BlockSpec((tm, D_MODEL), lambda i, k: (i, 0))
    row = pl.BlockSpec((1, D_MODEL), lambda i, k: (0, 0))
    hbm = pl.BlockSpec(memory_space=pl.ANY)
    return pl.pallas_call(
        body, name="in_proj_bwd", grid=(s // tm, ksteps),
        in_specs=[pl.BlockSpec((tm, tk), lambda i, k: (i, k)), pl.BlockSpec((tk, D_MODEL), lambda i, k: (k, 0)),
                  tile, tile, row, row, hbm],
        out_specs=[tile, pl.BlockSpec((8, D_MODEL), lambda i, k: (0, 0)), hbm],
        out_shape=[SDS((s, D_MODEL), F32), SDS((8, D_MODEL), F32), SDS((3,) + pair.shape[1:], pair.dtype)],
        scratch_shapes=[pltpu.VMEM((tm, D_MODEL), F32), *_scatter_scratch()],
        compiler_params=_params("arbitrary", "arbitrary"),
    )(dproj, wt, x, dx1, scale, norm_g, pair)


def _pair_sum(core, grad, got):
    _, m, n = got.shape

    def body(core_ref, a_ref, b_ref, out_ref):
        out_ref[...] = (a_ref[...].astype(F32) + b_ref[...].astype(F32)).astype(BF16)

    blk = pl.BlockSpec((1, m, n), lambda q, core_ref: (q, 0, 0))
    return pl.pallas_call(
        body, name=f"pair_sum_{m}",
        grid_spec=pltpu.PrefetchScalarGridSpec(
            num_scalar_prefetch=1, grid=(4,),
            in_specs=[pl.BlockSpec((1, m, n), lambda q, core_ref: (2 * q + core_ref[0], 0, 0)), blk], out_specs=blk),
        out_shape=SDS(got.shape, BF16), compiler_params=_params("parallel"),
    )(core, grad, got)


def _sum_chips(own_ref, parts_ref):
    return ((own_ref[0].astype(F32) + parts_ref[0].astype(F32)) + parts_ref[1].astype(F32)) + parts_ref[2].astype(F32)


def _adam_w_in(chip, pair, parts, w, m, v):
    n = w.shape[1]
    tc = 256

    def body(chip_ref, own_ref, p_ref, w_ref, m_ref, v_ref, g_ref, d_ref, nm_ref, nv_ref):
        g = _sum_chips(own_ref, p_ref).T
        g_ref[...] = g
        d_ref[...], nm_ref[...], nv_ref[...] = _adamw(w_ref[...], g, m_ref[...], v_ref[...])

    blk = pl.BlockSpec((tc, n), lambda j, chip_ref: (j, 0))
    return pl.pallas_call(
        body, name="adam_w_in",
        grid_spec=pltpu.PrefetchScalarGridSpec(
            num_scalar_prefetch=1, grid=(D_MODEL // tc,),
            in_specs=[pl.BlockSpec((1, n, tc), lambda j, chip_ref: (chip_ref[0], 0, j)),
                      pl.BlockSpec((3, n, tc), lambda j, chip_ref: (0, 0, j)), blk, blk, blk],
            out_specs=[blk] * 4),
        out_shape=[SDS(w.shape, F32)] * 4, compiler_params=_params("parallel"),
    )(chip, pair, parts, w, m, v)


def _adam_w_out(chip, pair, parts, w, m, v):
    rows = w.shape[0]
    tr = 64

    def body(chip_ref, own_ref, p_ref, w_ref, m_ref, v_ref, g_ref, d_ref, nm_ref, nv_ref):
        g = _sum_chips(own_ref, p_ref)
        g_ref[...] = g
        d_ref[...], nm_ref[...], nv_ref[...] = _adamw(w_ref[...], g, m_ref[...], v_ref[...])

    blk = pl.BlockSpec((tr, D_MODEL), lambda j, chip_ref: (j, 0))
    return pl.pallas_call(
        body, name="adam_w_out",
        grid_spec=pltpu.PrefetchScalarGridSpec(
            num_scalar_prefetch=1, grid=(rows // tr,),
            in_specs=[pl.BlockSpec((1, tr, D_MODEL), lambda j, chip_ref: (chip_ref[0], j, 0)),
                      pl.BlockSpec((3, tr, D_MODEL), lambda j, chip_ref: (0, j, 0)), blk, blk, blk],
            out_specs=[blk] * 4),
        out_shape=[SDS(w.shape, F32)] * 4, compiler_params=_params("parallel"),
    )(chip, pair, parts, w, m, v)


def _adam_ada(name, cact, dmod, w, m, v):
    n = w.shape[1]
    tr = 512

    def body(c_ref, dm_ref, w_ref, m_ref, v_ref, g_ref, d_ref, nm_ref, nv_ref):
        pad_c = jnp.concatenate([c_ref[...], jnp.zeros_like(c_ref)], axis=0).astype(BF16)
        pad_d = jnp.concatenate([dm_ref[...], jnp.zeros_like(dm_ref)], axis=0).astype(BF16)
        g = lax.dot_general(pad_c, pad_d, TN, preferred_element_type=F32)
        g_ref[...] = g
        d_ref[...], nm_ref[...], nv_ref[...] = _adamw(w_ref[...], g, m_ref[...], v_ref[...])

    blk = pl.BlockSpec((tr, n), lambda j: (j, 0))
    return pl.pallas_call(
        body, name=name, grid=(D_MODEL // tr,),
        in_specs=[pl.BlockSpec((N_DEV, tr), lambda j: (0, j)), pl.BlockSpec((N_DEV, n), lambda j: (0, 0)),
                  blk, blk, blk],
        out_specs=[blk] * 4, out_shape=[SDS(w.shape, F32)] * 4,
        compiler_params=_params("parallel"),
    )(cact, dmod, w, m, v)


def _adam_small(parts, w, m, v):
    def body(p_ref, w_ref, m_ref, v_ref, g_ref, d_ref, nm_ref, nv_ref):
        g = p_ref[0]
        for j in range(1, N_DEV):
            g = g + p_ref[j]
        g_ref[...] = g
        d_ref[...], nm_ref[...], nv_ref[...] = _adamw(w_ref[...], g, m_ref[...], v_ref[...])

    vmem = pl.BlockSpec(memory_space=pltpu.VMEM)
    return pl.pallas_call(
        body, name="adam_small", in_specs=[vmem] * 4, out_specs=[vmem] * 4,
        out_shape=[SDS(w.shape, F32)] * 4, compiler_params=_params(),
    )(parts, w, m, v)


def _pack_small(w_sp, norm_g, ln_g, ln_b, b_sp, sinks, fng, b_ada, b_ada_f):
    row_bsp = jnp.concatenate([b_sp.reshape(1, 1024), sinks.reshape(1, 16), jnp.zeros((1, 1008), F32)], axis=1)
    return jnp.concatenate([
        w_sp.reshape(64, D_MODEL), norm_g.reshape(1, D_MODEL),
        jnp.concatenate([ln_g.reshape(1, D_A), ln_b.reshape(1, D_A)], axis=1), row_bsp, fng.reshape(1, D_MODEL),
        b_ada.reshape(3, D_MODEL), b_ada_f.reshape(2, D_MODEL),
        jnp.zeros((SMALL_ROWS - 73, D_MODEL), F32)], axis=0)


def _unpack_small(p):
    return dict(
        w_spatial=p[ROW_WSP:ROW_WSP + 64].reshape(1, A_GROUPS, CHUNK, CHUNK),
        norm_g=p[ROW_NG].reshape(1, D_MODEL),
        ln_v_g=p[ROW_LN, :D_A].reshape(1, D_A), ln_v_b=p[ROW_LN, D_A:].reshape(1, D_A),
        b_spatial=p[ROW_BSP, :1024].reshape(1, A_GROUPS, CHUNK), sinks=p[ROW_BSP, 1024:1040].reshape(1, 16),
        final_norm_g=p[ROW_FNG].reshape(D_MODEL),
        b_ada=p[ROW_BADA:ROW_BADA + 3].reshape(1, 3 * D_MODEL), b_ada_final=p[ROW_BADAF:ROW_BADAF + 2].reshape(2 * D_MODEL),
    )


def kernel(x, c, w_ada, b_ada, norm_g, w_in, ln_v_g, ln_v_b, w_spatial, b_spatial, sinks, w_out, w_ada_final, b_ada_final, final_norm_g, loss_target, m_w_ada, m_b_ada, m_norm_g, m_w_in, m_ln_v_g, m_ln_v_b, m_w_spatial, m_b_spatial, m_sinks, m_w_out, m_w_ada_final, m_b_ada_final, m_final_norm_g, v_w_ada, v_b_ada, v_norm_g, v_w_in, v_ln_v_g, v_ln_v_b, v_w_spatial, v_b_spatial, v_sinks, v_w_out, v_w_ada_final, v_b_ada_final, v_final_norm_g):
    seq = x.shape[1]
    me = 4 * lax.axis_index("x") + 2 * lax.axis_index("y") + lax.axis_index("c")
    x2, tgt = x[0], loss_target[0]
    fng = final_norm_g.reshape(1, D_MODEL)

    n_ada, n_ada_f = w_ada.shape[2], w_ada_final.shape[1]
    cact, mod, mod_f = _ada_exchange(c, w_ada[0], b_ada.reshape(N_DEV, n_ada), w_ada_final,
                                     b_ada_final.reshape(N_DEV, n_ada_f))
    cact = cact.reshape(N_DEV, D_MODEL)
    mod, mod_f = mod.reshape(1, 3 * D_MODEL), mod_f.reshape(1, 2 * D_MODEL)
    shift, scale, gate = mod[:, :D_MODEL], mod[:, D_MODEL:2 * D_MODEL], mod[:, 2 * D_MODEL:]
    shift_f, scale_f = mod_f[:, :D_MODEL], mod_f[:, D_MODEL:]

    wt_shard, wo_shard = _prep_weights(w_in[0], w_out[0])
    wt, wo = _all_gather("gather_weights", [wt_shard, wo_shard], pl.ANY)
    wt, wo = wt.reshape(D_IN, D_MODEL), wo.reshape(D_MODEL, D_MODEL)

    tabs = _rope_tables(seq)
    sinks_v = sinks.reshape(16)
    h, proj = _in_proj(x2, shift, scale, norm_g, wt)
    y = _mixer_fwd(proj, tabs, ln_v_g, ln_v_b, w_spatial[0], b_spatial[0], sinks_v)
    dx1, do, dy, sums_o = _out_proj_loss(y, x2, tgt, wo, gate, shift_f, scale_f, fng)
    loss = lax.psum(0.5 * jnp.sum(sums_o[4]) / D_MODEL, ("x", "y", "c"))

    core = lax.axis_index("c").reshape(1)
    chip = (2 * lax.axis_index("x") + lax.axis_index("y")).reshape(1)
    g_wo = _wgrad("wgrad_out", y, do).reshape(N_DEV, D_MODEL // N_DEV, D_MODEL)
    pair_out = _pair_sum(core, g_wo, _rs_pair("rs_pair_out", g_wo))
    dproj, d_ln, d_wsp, d_bsp, d_sinks, parts_out = _mixer_bwd(
        proj, dy, tabs, ln_v_g, ln_v_b, w_spatial[0], b_spatial[0], sinks_v, pair_out)
    g_wt = _wgrad("wgrad_in", dproj, h).reshape(N_DEV, D_IN // N_DEV, D_MODEL)
    pair_in = _pair_sum(core, g_wt, _rs_pair("rs_pair_in", g_wt))
    grad_x, sums_i, parts_in = _in_proj_bwd(dproj, wt, x2, dx1, scale, norm_g, pair_in)
    g_w_in, d_w_in, nm_w_in, nv_w_in = _adam_w_in(chip, pair_in, parts_in, w_in[0], m_w_in[0], v_w_in[0])
    g_w_out, d_w_out, nm_w_out, nv_w_out = _adam_w_out(chip, pair_out, parts_out, w_out[0], m_w_out[0], v_w_out[0])

    dmod = jnp.concatenate([sums_i[0], sums_i[1], sums_o[0]])
    dmod_f = jnp.concatenate([sums_o[1], sums_o[2]])
    small = _pack_small(d_wsp, sums_i[2], d_ln[0], d_ln[1], d_bsp, d_sinks[:, 0], sums_o[3], dmod, dmod_f)
    (small_all,) = _all_gather("gather_small", [small], pltpu.VMEM)
    packed = [_pack_small(*t) for t in (
        (w_spatial, norm_g, ln_v_g, ln_v_b, b_spatial, sinks, final_norm_g, b_ada, b_ada_final),
        (m_w_spatial, m_norm_g, m_ln_v_g, m_ln_v_b, m_b_spatial, m_sinks, m_final_norm_g, m_b_ada, m_b_ada_final),
        (v_w_spatial, v_norm_g, v_ln_v_g, v_ln_v_b, v_b_spatial, v_sinks, v_final_norm_g, v_b_ada, v_b_ada_final))]
    g_s, d_s, nm_s, nv_s = [_unpack_small(p) for p in _adam_small(small_all, *packed)]

    dmod_all = small_all[:, ROW_BADA:ROW_BADA + 3].reshape(N_DEV, 3 * D_MODEL)
    dmod_f_all = small_all[:, ROW_BADAF:ROW_BADAF + 2].reshape(N_DEV, 2 * D_MODEL)
    dmod_mine = lax.dynamic_slice_in_dim(dmod_all, me * n_ada, n_ada, axis=1)
    dmod_f_mine = lax.dynamic_slice_in_dim(dmod_f_all, me * n_ada_f, n_ada_f, axis=1)
    ada = _adam_ada("adam_w_ada", cact, dmod_mine, w_ada[0], m_w_ada[0], v_w_ada[0])
    ada_f = _adam_ada("adam_w_ada_final", cact, dmod_f_mine, w_ada_final, m_w_ada_final, v_w_ada_final)

    def leaves(k):
        small_k = (g_s, d_s, nm_s, nv_s)[k]
        return (ada[k][None], small_k["b_ada"], small_k["norm_g"], (g_w_in, d_w_in, nm_w_in, nv_w_in)[k][None],
                small_k["ln_v_g"], small_k["ln_v_b"], small_k["w_spatial"], small_k["b_spatial"], small_k["sinks"],
                (g_w_out, d_w_out, nm_w_out, nv_w_out)[k][None], ada_f[k], small_k["b_ada_final"],
                small_k["final_norm_g"])

    return (loss, grad_x[None], *leaves(0), *leaves(1), *leaves(2), *leaves(3))
```

```python
import functools

import jax
import jax.numpy as jnp
from jax import lax
from jax.experimental import pallas as pl
from jax.experimental.pallas import tpu as pltpu

D_MODEL = 2048
D_IN = 5632
D_A = 1024
CHUNK = 128
A_GROUPS = 8
HEAD_DIM = 64
N_KV_HEADS = 4
N_DEV = 8
ROPE_THETA = 10000.0
NORM_EPS = 1e-5
ATTN_SCALE = HEAD_DIM ** -0.5

ADAM_LR = 0.001
ADAM_B1 = 0.9
ADAM_B2 = 0.999
ADAM_EPS = 1e-08
ADAM_WD = 0.01
ADAM_STEP = 10

OFF_U, OFF_VA, OFF_ZA, OFF_Q, OFF_K, OFF_V, OFF_ZB = 0, 1024, 2048, 3072, 4096, 4352, 4608

SMALL_ROWS = 80
ROW_WSP, ROW_NG, ROW_LN, ROW_BSP, ROW_FNG, ROW_BADA, ROW_BADAF = 0, 64, 65, 66, 67, 68, 71

V7X_VMEM_LIMIT_BYTES = 56 * 1024 * 1024

F32 = jnp.float32
BF16 = jnp.bfloat16
MESH = pl.DeviceIdType.MESH
SDS = jax.ShapeDtypeStruct
NT = (((1,), (1,)), ((), ()))
TN = (((0,), (0,)), ((), ()))


def _params(*semantics):
    return pltpu.CompilerParams(dimension_semantics=semantics or None, vmem_limit_bytes=V7X_VMEM_LIMIT_BYTES)


def _mesh_pos():
    return lax.axis_index("x"), lax.axis_index("y"), lax.axis_index("c")


def _sigmoid(z):
    return 1.0 / (1.0 + jnp.exp(-z))


def _adamw(w, g, m, v):
    m = ADAM_B1 * m + (1.0 - ADAM_B1) * g
    v = ADAM_B2 * v + (1.0 - ADAM_B2) * (g * g)
    m_hat = m / (1.0 - ADAM_B1 ** ADAM_STEP)
    v_hat = v / (1.0 - ADAM_B2 ** ADAM_STEP)
    delta = -ADAM_LR * (m_hat / (jnp.sqrt(v_hat) + ADAM_EPS) + ADAM_WD * w)
    return delta, m, v


def _all_gather(name, blocks, memory_space):
    n_arr = len(blocks)

    def body(*refs):
        ins, outs = refs[:n_arr], refs[n_arr:2 * n_arr]
        send_sems, recv_sems, local_sems = refs[2 * n_arr:]
        x, y, c = _mesh_pos()
        me, sibling = (x, y, c), (x, y, 1 - c)
        chips = [(1 - x, y), (x, 1 - y), (1 - x, 1 - y)]

        def slot(p):
            return 4 * p[0] + 2 * p[1] + p[2]

        def copy(a, k, block, to, src=None):
            dst = outs[a].at[slot(block)]
            return pltpu.make_async_remote_copy(
                src_ref=dst if src is None else src, dst_ref=dst,
                send_sem=send_sems.at[a, k], recv_sem=recv_sems.at[a, k],
                device_id=to, device_id_type=MESH)

        mine = [pltpu.make_async_copy(ins[a], outs[a].at[slot(me)], local_sems.at[a]) for a in range(n_arr)]
        for cp in mine:
            cp.start()
        first = []
        for a in range(n_arr):
            first.append(copy(a, 0, me, sibling, src=ins[a]))
            first += [copy(a, 1 + j, me, (*chip, c), src=ins[a]) for j, chip in enumerate(chips)]
        for cp in first:
            cp.start()
        passed = []
        for j, chip in enumerate(chips):
            for a in range(n_arr):
                copy(a, 1 + j, (*chip, c), me).wait_recv()
                fwd = copy(a, 4 + j, (*chip, c), sibling)
                fwd.start()
                passed.append(fwd)
        for a in range(n_arr):
            copy(a, 0, sibling, me).wait_recv()
            for j, chip in enumerate(chips):
                copy(a, 4 + j, (*chip, 1 - c), me).wait_recv()
        for cp in first + passed:
            cp.wait_send()
        for cp in mine:
            cp.wait()

    spec = pl.BlockSpec(memory_space=memory_space)
    return pl.pallas_call(
        body, name=name,
        out_shape=[SDS((N_DEV,) + b.shape, b.dtype) for b in blocks],
        in_specs=[spec] * n_arr, out_specs=[spec] * n_arr,
        scratch_shapes=[pltpu.SemaphoreType.DMA((n_arr, 7)), pltpu.SemaphoreType.DMA((n_arr, 7)),
                        pltpu.SemaphoreType.DMA((n_arr,))],
        compiler_params=_params(),
    )(*blocks)


def _ada_exchange(c, w_ada, b_ada8, w_ada_f, b_ada_f8):
    n1, n2 = w_ada.shape[1], w_ada_f.shape[1]

    def body(c_ref, w1_ref, b1_ref, w2_ref, b2_ref, cact_ref, mod_ref, modf_ref,
             cact_buf, res1, res2, send1, send2, sems_s, sems_r):
        x, y, c_pos = _mesh_pos()
        me = 4 * x + 2 * y + c_pos
        flips = [(k >> 2 & 1, k >> 1 & 1, k & 1) for k in range(1, N_DEV)]

        def peer(f):
            return (1 - x if f[0] else x, 1 - y if f[1] else y, 1 - c_pos if f[2] else c_pos)

        cv = c_ref[...]
        cact = cv * _sigmoid(cv)
        cact_buf[...] = cact
        cact_ref[me] = cact

        def rdma(phase, k, src, dst, f):
            return pltpu.make_async_remote_copy(src_ref=src, dst_ref=dst, send_sem=sems_s.at[phase, k],
                                                recv_sem=sems_r.at[phase, k], device_id=peer(f), device_id_type=MESH)

        gather = [rdma(0, k, cact_buf, cact_ref.at[me], f) for k, f in enumerate(flips)]
        for cp in gather:
            cp.start()
        for cp in gather:
            cp.wait_recv()
        for cp in gather:
            cp.wait_send()

        rid = lax.broadcasted_iota(jnp.int32, (N_DEV, D_MODEL), 0)
        rows = jnp.zeros((N_DEV, D_MODEL), F32)
        for j in range(N_DEV):
            rows = jnp.where(rid == j, jnp.broadcast_to(cact_ref[j], (N_DEV, D_MODEL)), rows)
        rows = rows.astype(BF16)
        res1[...] = jnp.dot(rows, w1_ref[...].astype(BF16), preferred_element_type=F32) + b1_ref[pl.ds(me, 1), :]
        res2[...] = jnp.dot(rows, w2_ref[...].astype(BF16), preferred_element_type=F32) + b2_ref[pl.ds(me, 1), :]
        for j in range(N_DEV):
            send1[j] = res1[pl.ds(j, 1), :]
            send2[j] = res2[pl.ds(j, 1), :]
        mod_ref[me] = send1[me]
        modf_ref[me] = send2[me]
        scatter = []
        for k, f in enumerate(flips):
            to = me ^ (k + 1)
            scatter.append(rdma(1, k, send1.at[to], mod_ref.at[me], f))
            scatter.append(rdma(2, k, send2.at[to], modf_ref.at[me], f))
        for cp in scatter:
            cp.start()
        for cp in scatter:
            cp.wait_recv()
        for cp in scatter:
            cp.wait_send()

    vmem = pl.BlockSpec(memory_space=pltpu.VMEM)
    return pl.pallas_call(
        body, name="ada_exchange",
        out_shape=[SDS((N_DEV, 1, D_MODEL), F32), SDS((N_DEV, 1, n1), F32), SDS((N_DEV, 1, n2), F32)],
        in_specs=[vmem] * 5, out_specs=[vmem] * 3,
        scratch_shapes=[pltpu.VMEM((1, D_MODEL), F32), pltpu.VMEM((N_DEV, n1), F32), pltpu.VMEM((N_DEV, n2), F32),
                        pltpu.VMEM((N_DEV, 1, n1), F32), pltpu.VMEM((N_DEV, 1, n2), F32),
                        pltpu.SemaphoreType.DMA((3, 7)), pltpu.SemaphoreType.DMA((3, 7))],
        compiler_params=_params(),
    )(c, w_ada, b_ada8, w_ada_f, b_ada_f8)


def _rs_pair(name, grad):
    def body(g_ref, got_ref, send_sems, recv_sems):
        x, y, c = _mesh_pos()
        copies = [pltpu.make_async_remote_copy(
            src_ref=g_ref.at[2 * q + 1 - c], dst_ref=got_ref.at[q], send_sem=send_sems.at[q], recv_sem=recv_sems.at[q],
            device_id=(x, y, 1 - c), device_id_type=MESH) for q in range(4)]
        for cp in copies:
            cp.start()
        for cp in copies:
            cp.wait_recv()
        for cp in copies:
            cp.wait_send()

    hbm = pl.BlockSpec(memory_space=pl.ANY)
    return pl.pallas_call(
        body, name=name, out_shape=SDS((4,) + grad.shape[1:], grad.dtype), in_specs=[hbm], out_specs=hbm,
        scratch_shapes=[pltpu.SemaphoreType.DMA((4,)), pltpu.SemaphoreType.DMA((4,))],
        compiler_params=_params(),
    )(grad)


def _chip_scatter(pair_ref, parts_ref, send_sems, recv_sems):
    x, y, c = _mesh_pos()
    chips = [(1 - x, y), (x, 1 - y), (1 - x, 1 - y)]
    return [pltpu.make_async_remote_copy(
        src_ref=pair_ref.at[2 * cx + cy], dst_ref=parts_ref.at[j], send_sem=send_sems.at[j], recv_sem=recv_sems.at[j],
        device_id=(cx, cy, c), device_id_type=MESH) for j, (cx, cy) in enumerate(chips)]


def _scatter_scratch():
    return [pltpu.SemaphoreType.DMA((3,)), pltpu.SemaphoreType.DMA((3,))]


def _prep_weights(w_in, w_out):
    d, n = w_in.shape
    steps = 4

    def body(wi_ref, wo_ref, wt_ref, wob_ref):
        wt_ref[...] = wi_ref[...].T.astype(BF16)
        wob_ref[...] = wo_ref[...].astype(BF16)

    return pl.pallas_call(
        body, name="prep_weights", grid=(steps,),
        in_specs=[pl.BlockSpec((d // steps, n), lambda i: (i, 0)),
                  pl.BlockSpec((w_out.shape[0] // steps, d), lambda i: (i, 0))],
        out_specs=[pl.BlockSpec((n, d // steps), lambda i: (0, i)),
                   pl.BlockSpec((w_out.shape[0] // steps, d), lambda i: (i, 0))],
        out_shape=[SDS((n, d), BF16), SDS(w_out.shape, BF16)],
        compiler_params=_params("parallel"),
    )(w_in, w_out)


def _in_proj(x, shift, scale, norm_g, wt):
    s = x.shape[0]
    tm, tn = 512, 512

    def body(x_ref, shift_ref, scale_ref, g_ref, wt_ref, h_ref, proj_ref, h_scr):
        @pl.when(pl.program_id(1) == 0)
        def _():
            xv = x_ref[...]
            r = lax.rsqrt(jnp.mean(xv * xv, axis=-1, keepdims=True) + NORM_EPS)
            h = ((xv * r) * g_ref[...]) * (1.0 + scale_ref[...]) + shift_ref[...]
            hb = h.astype(BF16)
            h_scr[...] = hb
            h_ref[...] = hb

        proj_ref[...] = lax.dot_general(h_scr[...], wt_ref[...], NT, preferred_element_type=F32).astype(BF16)

    row = pl.BlockSpec((1, D_MODEL), lambda i, j: (0, 0))
    return pl.pallas_call(
        body, name="in_proj", grid=(s // tm, D_IN // tn),
        in_specs=[pl.BlockSpec((tm, D_MODEL), lambda i, j: (i, 0)), row, row, row,
                  pl.BlockSpec((tn, D_MODEL), lambda i, j: (j, 0))],
        out_specs=[pl.BlockSpec((tm, D_MODEL), lambda i, j: (i, 0)), pl.BlockSpec((tm, tn), lambda i, j: (i, j))],
        out_shape=[SDS((s, D_MODEL), BF16), SDS((s, D_IN), BF16)],
        scratch_shapes=[pltpu.VMEM((tm, D_MODEL), BF16)],
        compiler_params=_params("parallel", "arbitrary"),
    )(x, shift, scale, norm_g, wt)


def _rope_tables(seq):
    inv_freq = ROPE_THETA ** (-jnp.arange(0, HEAD_DIM, 2, dtype=F32) / HEAD_DIM)
    ang = jnp.arange(seq, dtype=F32)[:, None] * inv_freq[None, :]
    cos, sin, zero = jnp.cos(ang), jnp.sin(ang), jnp.zeros_like(ang)
    return (jnp.concatenate([cos] * 4, axis=1), jnp.concatenate([-sin, zero, -sin, zero], axis=1),
            jnp.concatenate([zero, sin, zero, sin], axis=1))


def _rope(v, cos, sin_lo, sin_hi):
    width = v.shape[1]
    rep = (1, width // 128)
    return (v * jnp.tile(cos, rep) + pltpu.roll(v, width - 32, 1) * jnp.tile(sin_lo, rep)
            + pltpu.roll(v, 32, 1) * jnp.tile(sin_hi, rep))


def _rope_bwd(d, cos, sin_lo, sin_hi):
    width = d.shape[1]
    rep = (1, width // 128)
    return (d * jnp.tile(cos, rep) + pltpu.roll(d * jnp.tile(sin_lo, rep), 32, 1)
            + pltpu.roll(d * jnp.tile(sin_hi, rep), width - 32, 1))


def _layer_norm(v, g, b):
    mu = jnp.mean(v, axis=-1, keepdims=True)
    vc = v - mu
    rstd = lax.rsqrt(jnp.mean(vc * vc, axis=-1, keepdims=True) + NORM_EPS)
    vhat = vc * rstd
    return vhat * g + b, vhat, rstd


def _tril_bf16(w_ref, g):
    t = lax.broadcasted_iota(jnp.int32, (CHUNK, CHUNK), 0)
    tp = lax.broadcasted_iota(jnp.int32, (CHUNK, CHUNK), 1)
    return jnp.where(tp <= t, w_ref[g], 0.0).astype(BF16)


def _bias_columns(b_ref, out_ref):
    for g in range(A_GROUPS):
        out_ref[g] = jnp.broadcast_to(b_ref[pl.ds(g, 1), :], (CHUNK, CHUNK)).T


def _band_mask():
    kj = lax.broadcasted_iota(jnp.int32, (2 * CHUNK, 4 * CHUNK), 0)
    qi = lax.broadcasted_iota(jnp.int32, (2 * CHUNK, 4 * CHUNK), 1) & (CHUNK - 1)
    rel = qi + CHUNK - kj
    return jnp.where((rel >= 0) & (rel < CHUNK), 0.0, -jnp.inf)


def _low_lanes():
    return lax.broadcasted_iota(jnp.int32, (1, 128), 1) < HEAD_DIM


def _stack_heads(pair_a, pair_b):
    lo = _low_lanes()
    return jnp.concatenate([jnp.where(lo, pair_a, 0.0), jnp.where(lo, 0.0, pair_a),
                            jnp.where(lo, pair_b, 0.0), jnp.where(lo, 0.0, pair_b)], axis=0).astype(BF16)


def _heads_to_lanes(per_group):
    rows = [t[:, r * CHUNK:(r + 1) * CHUNK] for t in per_group for r in range(4)]
    return jnp.concatenate(rows, axis=0).T


def _dup_kv_head(band, gk):
    pair = band[:, (gk // 2) * 128:(gk // 2 + 1) * 128]
    lo = _low_lanes()
    one = jnp.where(lo if gk % 2 == 0 else jnp.logical_not(lo), pair, 0.0)
    return (one + pltpu.roll(one, HEAD_DIM, 1)).astype(BF16)


def _fold_kv_head(dup_grad, gk):
    both = dup_grad + pltpu.roll(dup_grad, HEAD_DIM, 1)
    lo = _low_lanes()
    return jnp.where(lo if gk % 2 == 0 else jnp.logical_not(lo), both, 0.0)


def _attn_probs(q_st, k_dup, sink_row, mask, first_block):
    s = lax.dot_general(k_dup, q_st, NT, preferred_element_type=F32) * ATTN_SCALE + mask
    s = jnp.concatenate([jnp.where(first_block, -jnp.inf, s[:CHUNK]), s[CHUNK:]], axis=0)
    m = jnp.maximum(jnp.max(s, axis=0, keepdims=True), sink_row)
    p = jnp.exp(s - m)
    e_sink = jnp.exp(sink_row - m)
    inv = 1.0 / (jnp.sum(p, axis=0, keepdims=True) + e_sink)
    return p * inv, e_sink * inv


def _sink_row(sinks_ref, gk):
    return jnp.concatenate([jnp.full((1, CHUNK), sinks_ref[4 * gk + r], F32) for r in range(4)], axis=1)


def _mixer_specs(nb, rev):
    def blk(i):
        return nb - 1 - i if rev else i

    def prev(i):
        return jnp.maximum(blk(i) - 1, 0)

    tab = pl.BlockSpec((CHUNK, 128), lambda i: (blk(i), 0))
    tab_prev = pl.BlockSpec((CHUNK, 128), lambda i: (prev(i), 0))
    return dict(
        cur=pl.BlockSpec((CHUNK, D_IN), lambda i: (blk(i), 0)),
        prev_kv=pl.BlockSpec((CHUNK, 2 * 256), lambda i: (prev(i), OFF_K // 512)),
        tabs=[tab] * 3 + [tab_prev] * 3,
        vec=pl.BlockSpec((1, D_A), lambda i: (0, 0)),
        wsp=pl.BlockSpec((A_GROUPS, CHUNK, CHUNK), lambda i: (0, 0, 0)),
        bsp=pl.BlockSpec((A_GROUPS, CHUNK), lambda i: (0, 0)),
        smem=pl.BlockSpec(memory_space=pltpu.SMEM),
        blk=blk,
    )


def _mixer_fwd(proj, tabs, ln_g, ln_b, w_sp, b_sp, sinks):
    s = proj.shape[0]
    nb = s // CHUNK
    sp = _mixer_specs(nb, rev=False)

    def body(cur_ref, pkv_ref, c_ref, s1_ref, s2_ref, cp_ref, s1p_ref, s2p_ref, lg_ref, lb_ref, w_ref, b_ref,
             sinks_ref, y_ref, bcol, mask):
        i = pl.program_id(0)

        @pl.when(i == 0)
        def _():
            _bias_columns(b_ref, bcol)
            mask[...] = _band_mask()

        vln, _, _ = _layer_norm(cur_ref[:, OFF_VA:OFF_ZA].astype(F32), lg_ref[...], lb_ref[...])
        vln = vln.astype(BF16)
        for g in range(A_GROUPS):
            cols = slice(g * 128, (g + 1) * 128)
            sg = jnp.dot(_tril_bf16(w_ref, g), vln[:, cols], preferred_element_type=F32) + bcol[g]
            u = cur_ref[:, OFF_U + g * 128:OFF_U + (g + 1) * 128].astype(F32)
            z = cur_ref[:, OFF_ZA + g * 128:OFF_ZA + (g + 1) * 128].astype(F32)
            y_ref[:, cols] = (u * sg * (z * _sigmoid(z))).astype(BF16)

        cur_t = (c_ref[...], s1_ref[...], s2_ref[...])
        prev_t = (cp_ref[...], s1p_ref[...], s2p_ref[...])
        qr = _rope(cur_ref[:, OFF_Q:OFF_K].astype(F32), *cur_t)
        kr = jnp.concatenate([_rope(pkv_ref[:, 0:256].astype(F32), *prev_t),
                              _rope(cur_ref[:, OFF_K:OFF_V].astype(F32), *cur_t)], axis=0)
        v_t = jnp.concatenate([pkv_ref[:, 256:512], cur_ref[:, OFF_V:OFF_ZB]], axis=0).astype(F32).T.astype(BF16)
        outs = []
        for gk in range(N_KV_HEADS):
            q_st = _stack_heads(qr[:, (2 * gk) * 128:(2 * gk + 1) * 128], qr[:, (2 * gk + 1) * 128:(2 * gk + 2) * 128])
            probs, _ = _attn_probs(q_st, _dup_kv_head(kr, gk), _sink_row(sinks_ref, gk), mask[...], i == 0)
            outs.append(jnp.dot(v_t[gk * HEAD_DIM:(gk + 1) * HEAD_DIM], probs.astype(BF16),
                                preferred_element_type=F32))
        zb = cur_ref[:, OFF_ZB:D_IN].astype(F32)
        y_ref[:, D_A:D_MODEL] = (_heads_to_lanes(outs) * (zb * _sigmoid(zb))).astype(BF16)

    return pl.pallas_call(
        body, name="mixer_fwd", grid=(nb,),
        in_specs=[sp["cur"], sp["prev_kv"], *sp["tabs"], sp["vec"], sp["vec"], sp["wsp"], sp["bsp"], sp["smem"]],
        out_specs=pl.BlockSpec((CHUNK, D_MODEL), lambda i: (i, 0)),
        out_shape=SDS((s, D_MODEL), BF16),
        scratch_shapes=[pltpu.VMEM((A_GROUPS, CHUNK, CHUNK), F32), pltpu.VMEM((2 * CHUNK, 4 * CHUNK), F32)],
        compiler_params=_params("arbitrary"),
    )(proj, proj, *tabs, *tabs, ln_g, ln_b, w_sp, b_sp, sinks)


def _out_proj_loss(y, x, target, wo, gate, shift_f, scale_f, fng):
    s = y.shape[0]
    tm = 256

    def body(y_ref, x_ref, t_ref, wo_ref, gate_ref, sh_ref, sc_ref, g_ref, dx1_ref, do_ref, dy_ref, sums_ref):
        @pl.when(pl.program_id(0) == 0)
        def _():
            sums_ref[...] = jnp.zeros_like(sums_ref)

        o = jnp.dot(y_ref[...], wo_ref[...], preferred_element_type=F32)
        gate = gate_ref[...]
        x1 = x_ref[...] + gate * o
        rf = lax.rsqrt(jnp.mean(x1 * x1, axis=-1, keepdims=True) + NORM_EPS)
        x1n = x1 * rf
        hn = x1n * g_ref[...]
        one_sc = 1.0 + sc_ref[...]
        diff = hn * one_sc + sh_ref[...] - t_ref[...]
        dout = diff * (1.0 / D_MODEL)
        d_hn = dout * one_sc
        d_x1n = d_hn * g_ref[...]
        dx1 = rf * (d_x1n - x1n * jnp.mean(d_x1n * x1n, axis=-1, keepdims=True))
        dx1_ref[...] = dx1
        do = (dx1 * gate).astype(BF16)
        do_ref[...] = do
        dy_ref[...] = lax.dot_general(do, wo_ref[...], NT, preferred_element_type=F32).astype(BF16)

        def rowsum(v):
            return jnp.sum(v, axis=0, keepdims=True)

        sums_ref[0:1, :] += rowsum(dx1 * o)
        sums_ref[1:2, :] += rowsum(dout)
        sums_ref[2:3, :] += rowsum(dout * hn)
        sums_ref[3:4, :] += rowsum(d_hn * x1n)
        sums_ref[4:5, :] += rowsum(diff * diff)

    tile = pl.BlockSpec((tm, D_MODEL), lambda i: (i, 0))
    row = pl.BlockSpec((1, D_MODEL), lambda i: (0, 0))
    return pl.pallas_call(
        body, name="out_proj_loss", grid=(s // tm,),
        in_specs=[tile, tile, tile, pl.BlockSpec((D_MODEL, D_MODEL), lambda i: (0, 0)), row, row, row, row],
        out_specs=[tile, tile, tile, pl.BlockSpec((8, D_MODEL), lambda i: (0, 0))],
        out_shape=[SDS((s, D_MODEL), F32), SDS((s, D_MODEL), BF16), SDS((s, D_MODEL), BF16), SDS((8, D_MODEL), F32)],
        compiler_params=_params("arbitrary"),
    )(y, x, target, wo, gate, shift_f, scale_f, fng)


def _mixer_bwd(proj, dy, tabs, ln_g, ln_b, w_sp, b_sp, sinks, pair):
    s = proj.shape[0]
    nb = s // CHUNK
    sp = _mixer_specs(nb, rev=True)

    def body(cur_ref, pkv_ref, dy_ref, c_ref, s1_ref, s2_ref, cp_ref, s1p_ref, s2p_ref, lg_ref, lb_ref, w_ref, b_ref,
             sinks_ref, pair_ref, dproj_ref, dln_ref, dw_ref, db_ref, dsink_ref, parts_ref, bcol, dbcol, carry, mask,
             send_sems, recv_sems):
        i = pl.program_id(0)
        block = nb - 1 - i

        @pl.when(i == 0)
        def _():
            for cp in _chip_scatter(pair_ref, parts_ref, send_sems, recv_sems):
                cp.start()
            _bias_columns(b_ref, bcol)
            mask[...] = _band_mask()
            dbcol[...] = jnp.zeros_like(dbcol)
            carry[...] = jnp.zeros_like(carry)
            dln_ref[...] = jnp.zeros_like(dln_ref)
            dw_ref[...] = jnp.zeros_like(dw_ref)
            dsink_ref[...] = jnp.zeros_like(dsink_ref)

        vln, vhat, rstd = _layer_norm(cur_ref[:, OFF_VA:OFF_ZA].astype(F32), lg_ref[...], lb_ref[...])
        vln = vln.astype(BF16)
        d_vln = []
        for g in range(A_GROUPS):
            cols = slice(g * 128, (g + 1) * 128)
            w_g = _tril_bf16(w_ref, g)
            sg = jnp.dot(w_g, vln[:, cols], preferred_element_type=F32) + bcol[g]
            u = cur_ref[:, OFF_U + g * 128:OFF_U + (g + 1) * 128].astype(F32)
            z = cur_ref[:, OFF_ZA + g * 128:OFF_ZA + (g + 1) * 128].astype(F32)
            dya = dy_ref[:, cols].astype(F32)
            sig = _sigmoid(z)
            d_ya = dya * (z * sig)
            dproj_ref[:, OFF_ZA + g * 128:OFF_ZA + (g + 1) * 128] = (
                dya * (u * sg) * (sig * (1.0 + z * (1.0 - sig)))).astype(BF16)
            dproj_ref[:, OFF_U + g * 128:OFF_U + (g + 1) * 128] = (d_ya * sg).astype(BF16)
            d_s = d_ya * u
            dbcol[g] += d_s
            d_sb = d_s.astype(BF16)
            dw_ref[g] += lax.dot_general(d_sb, vln[:, cols], NT, preferred_element_type=F32)
            d_vln.append(lax.dot_general(w_g, d_sb, TN, preferred_element_type=F32))
        d_vln = jnp.concatenate(d_vln, axis=1)
        dln_ref[0:1, :] += jnp.sum(d_vln * vhat, axis=0, keepdims=True)
        dln_ref[1:2, :] += jnp.sum(d_vln, axis=0, keepdims=True)
        d_vhat = d_vln * lg_ref[...]
        d_va = rstd * (d_vhat - jnp.mean(d_vhat, axis=-1, keepdims=True)
                       - vhat * jnp.mean(d_vhat * vhat, axis=-1, keepdims=True))
        dproj_ref[:, OFF_VA:OFF_ZA] = d_va.astype(BF16)

        cur_t = (c_ref[...], s1_ref[...], s2_ref[...])
        prev_t = (cp_ref[...], s1p_ref[...], s2p_ref[...])
        band_t = tuple(jnp.concatenate([p, c], axis=0) for p, c in zip(prev_t, cur_t))
        qr = _rope(cur_ref[:, OFF_Q:OFF_K].astype(F32), *cur_t)
        kr = jnp.concatenate([_rope(pkv_ref[:, 0:256].astype(F32), *prev_t),
                              _rope(cur_ref[:, OFF_K:OFF_V].astype(F32), *cur_t)], axis=0)
        vb = jnp.concatenate([pkv_ref[:, 256:512], cur_ref[:, OFF_V:OFF_ZB]], axis=0).astype(F32)
        k_t, v_t = kr.T.astype(BF16), vb.T.astype(BF16)
        zb = cur_ref[:, OFF_ZB:D_IN].astype(F32)
        dyb = dy_ref[:, D_A:D_MODEL].astype(F32)
        sig = _sigmoid(zb)
        d_yb = dyb * (zb * sig)
        outs, dqs = [], []
        dk_pairs = [jnp.zeros((2 * CHUNK, 128), F32) for _ in range(2)]
        dv_pairs = [jnp.zeros((2 * CHUNK, 128), F32) for _ in range(2)]
        for gk in range(N_KV_HEADS):
            heads = slice(gk * HEAD_DIM, (gk + 1) * HEAD_DIM)
            q_st = _stack_heads(qr[:, (2 * gk) * 128:(2 * gk + 1) * 128], qr[:, (2 * gk + 1) * 128:(2 * gk + 2) * 128])
            k_dup, v_dup = _dup_kv_head(kr, gk), _dup_kv_head(vb, gk)
            probs, p_sink = _attn_probs(q_st, k_dup, _sink_row(sinks_ref, gk), mask[...], block == 0)
            probs_b = probs.astype(BF16)
            outs.append(jnp.dot(v_t[heads], probs_b, preferred_element_type=F32))
            do_st = _stack_heads(d_yb[:, (2 * gk) * 128:(2 * gk + 1) * 128], d_yb[:, (2 * gk + 1) * 128:(2 * gk + 2) * 128])
            dp = lax.dot_general(v_dup, do_st, NT, preferred_element_type=F32)
            delta = jnp.sum(probs * dp, axis=0, keepdims=True)
            ds = (probs * (dp - delta) * ATTN_SCALE).astype(BF16)
            d_sink = -p_sink * delta
            for r in range(4):
                dsink_ref[4 * gk + r:4 * gk + r + 1, :] += jnp.broadcast_to(
                    jnp.sum(d_sink[:, r * CHUNK:(r + 1) * CHUNK], axis=1, keepdims=True), (1, 128))
            dqs.append(jnp.dot(k_t[heads], ds, preferred_element_type=F32))
            dk_pairs[gk // 2] += _fold_kv_head(jnp.dot(ds, q_st, preferred_element_type=F32), gk)
            dv_pairs[gk // 2] += _fold_kv_head(jnp.dot(probs_b, do_st, preferred_element_type=F32), gk)
        dproj_ref[:, OFF_ZB:D_IN] = (dyb * _heads_to_lanes(outs) * (sig * (1.0 + zb * (1.0 - sig)))).astype(BF16)
        dproj_ref[:, OFF_Q:OFF_K] = _rope_bwd(_heads_to_lanes(dqs), *cur_t).astype(BF16)
        dk_band = _rope_bwd(jnp.concatenate(dk_pairs, axis=1), *band_t)
        dv_band = jnp.concatenate(dv_pairs, axis=1)
        dproj_ref[:, OFF_K:OFF_V] = (dk_band[CHUNK:] + carry[:, 0:256]).astype(BF16)
        dproj_ref[:, OFF_V:OFF_ZB] = (dv_band[CHUNK:] + carry[:, 256:512]).astype(BF16)
        carry[:, 0:256] = dk_band[:CHUNK]
        carry[:, 256:512] = dv_band[:CHUNK]

        @pl.when(i == nb - 1)
        def _():
            t = lax.broadcasted_iota(jnp.int32, (CHUNK, CHUNK), 0)
            tp = lax.broadcasted_iota(jnp.int32, (CHUNK, CHUNK), 1)
            for g in range(A_GROUPS):
                dw_ref[g] = jnp.where(tp <= t, dw_ref[g], 0.0)
                db_ref[pl.ds(g, 1), :] = jnp.sum(dbcol[g].T, axis=0, keepdims=True)
            scatter = _chip_scatter(pair_ref, parts_ref, send_sems, recv_sems)
            for cp in scatter:
                cp.wait_recv()
            for cp in scatter:
                cp.wait_send()

    blk = sp["blk"]
    hbm = pl.BlockSpec(memory_space=pl.ANY)
    return pl.pallas_call(
        body, name="mixer_bwd", grid=(nb,),
        in_specs=[sp["cur"], sp["prev_kv"], pl.BlockSpec((CHUNK, D_MODEL), lambda i: (blk(i), 0)), *sp["tabs"],
                  sp["vec"], sp["vec"], sp["wsp"], sp["bsp"], sp["smem"], hbm],
        out_specs=[pl.BlockSpec((CHUNK, D_IN), lambda i: (blk(i), 0)),
                   pl.BlockSpec((8, D_A), lambda i: (0, 0)),
                   pl.BlockSpec((A_GROUPS, CHUNK, CHUNK), lambda i: (0, 0, 0)),
                   pl.BlockSpec((A_GROUPS, CHUNK), lambda i: (0, 0)),
                   pl.BlockSpec((16, 128), lambda i: (0, 0)), hbm],
        out_shape=[SDS((s, D_IN), BF16), SDS((8, D_A), F32), SDS((A_GROUPS, CHUNK, CHUNK), F32),
                   SDS((A_GROUPS, CHUNK), F32), SDS((16, 128), F32), SDS((3,) + pair.shape[1:], pair.dtype)],
        scratch_shapes=[pltpu.VMEM((A_GROUPS, CHUNK, CHUNK), F32), pltpu.VMEM((A_GROUPS, CHUNK, CHUNK), F32),
                        pltpu.VMEM((CHUNK, 512), F32), pltpu.VMEM((2 * CHUNK, 4 * CHUNK), F32), *_scatter_scratch()],
        compiler_params=_params("arbitrary"),
    )(proj, proj, dy, *tabs, *tabs, ln_g, ln_b, w_sp, b_sp, sinks, pair)


def _wgrad(name, a, b):
    s, m = a.shape
    n = b.shape[1]
    bm, bt = 512, 512
    steps = s // bt

    def body(a_ref, b_ref, out_ref, acc):
        t = pl.program_id(1)

        @pl.when(t == 0)
        def _():
            acc[...] = jnp.zeros_like(acc)

        acc[...] += lax.dot_general(a_ref[...], b_ref[...], TN, preferred_element_type=F32)

        @pl.when(t == steps - 1)
        def _():
            out_ref[...] = acc[...].astype(out_ref.dtype)

    return pl.pallas_call(
        body, name=name, grid=(m // bm, steps),
        in_specs=[pl.BlockSpec((bt, bm), lambda i, t: (t, i)), pl.BlockSpec((bt, n), lambda i, t: (t, 0))],
        out_specs=pl.BlockSpec((bm, n), lambda i, t: (i, 0)),
        out_shape=SDS((m, n), BF16),
        scratch_shapes=[pltpu.VMEM((bm, n), F32)],
        compiler_params=_params("parallel", "arbitrary"),
    )(a, b)


def _in_proj_bwd(dproj, wt, x, dx1, scale, norm_g, pair):
    s = x.shape[0]
    tm, tk = 512, 512
    ksteps = D_IN // tk

    def body(dp_ref, wt_ref, x_ref, dx1_ref, sc_ref, g_ref, pair_ref, gx_ref, sums_ref, parts_ref, acc,
             send_sems, recv_sems):
        i, k = pl.program_id(0), pl.program_id(1)

        @pl.when((i == 0) & (k == 0))
        def _():
            for cp in _chip_scatter(pair_ref, parts_ref, send_sems, recv_sems):
                cp.start()
            sums_ref[...] = jnp.zeros_like(sums_ref)

        @pl.when(k == 0)
        def _():
            acc[...] = jnp.zeros_like(acc)

        acc[...] += jnp.dot(dp_ref[...], wt_ref[...], preferred_element_type=F32)

        @pl.when(k == ksteps - 1)
        def _():
            dh = acc[...]
            xv = x_ref[...]
            r = lax.rsqrt(jnp.mean(xv * xv, axis=-1, keepdims=True) + NORM_EPS)
            xn = xv * r
            hn = xn * g_ref[...]
            d_hn = dh * (1.0 + sc_ref[...])
            d_xn = d_hn * g_ref[...]
            gx_ref[...] = dx1_ref[...] + r * (d_xn - xn * jnp.mean(d_xn * xn, axis=-1, keepdims=True))
            sums_ref[0:1, :] += jnp.sum(dh, axis=0, keepdims=True)
            sums_ref[1:2, :] += jnp.sum(dh * hn, axis=0, keepdims=True)
            sums_ref[2:3, :] += jnp.sum(d_hn * xn, axis=0, keepdims=True)

        @pl.when((i == s // tm - 1) & (k == ksteps - 1))
        def _():
            scatter = _chip_scatter(pair_ref, parts_ref, send_sems, recv_sems)
            for cp in scatter:
                cp.wait_recv()
            for cp in scatter:
                cp.wait_send()

    tile = pl.BlockSpec((tm, D_MODEL), lambda i, k: (i, 0))
    row = pl.BlockSpec((1, D_MODEL), lambda i, k: (0, 0))
    hbm = pl.BlockSpec(memory_space=pl.ANY)
    return pl.pallas_call(
        body, name="in_proj_bwd", grid=(s // tm, ksteps),
        in_specs=[pl.BlockSpec((tm, tk), lambda i, k: (i, k)), pl.BlockSpec((tk, D_MODEL), lambda i, k: (k, 0)),
                  tile, tile, row, row, hbm],
        out_specs=[tile, pl.BlockSpec((8, D_MODEL), lambda i, k: (0, 0)), hbm],
        out_shape=[SDS((s, D_MODEL), F32), SDS((8, D_MODEL), F32), SDS((3,) + pair.shape[1:], pair.dtype)],
        scratch_shapes=[pltpu.VMEM((tm, D_MODEL), F32), *_scatter_scratch()],
        compiler_params=_params("arbitrary", "arbitrary"),
    )(dproj, wt, x, dx1, scale, norm_g, pair)


def _pair_sum(core, grad, got):
    _, m, n = got.shape

    def body(core_ref, a_ref, b_ref, out_ref):
        out_ref[...] = (a_ref[...].astype(F32) + b_ref[...].astype(F32)).astype(BF16)

    blk = pl.BlockSpec((1, m, n), lambda q, core_ref: (q, 0, 0))
    return pl.pallas_call(
        body, name=f"pair_sum_{m}",
        grid_spec=pltpu.PrefetchScalarGridSpec(
            num_scalar_prefetch=1, grid=(4,),
            in_specs=[pl.BlockSpec((1, m, n), lambda q, core_ref: (2 * q + core_ref[0], 0, 0)), blk], out_specs=blk),
        out_shape=SDS(got.shape, BF16), compiler_params=_params("parallel"),
    )(core, grad, got)


def _sum_chips(own_ref, parts_ref):
    return ((own_ref[0].astype(F32) + parts_ref[0].astype(F32)) + parts_ref[1].astype(F32)) + parts_ref[2].astype(F32)


def _adam_w_in(chip, pair, parts, w, m, v):
    n = w.shape[1]
    tc = 256

    def body(chip_ref, own_ref, p_ref, w_ref, m_ref, v_ref, g_ref, d_ref, nm_ref, nv_ref):
        g = _sum_chips(own_ref, p_ref).T
        g_ref[...] = g
        d_ref[...], nm_ref[...], nv_ref[...] = _adamw(w_ref[...], g, m_ref[...], v_ref[...])

    blk = pl.BlockSpec((tc, n), lambda j, chip_ref: (j, 0))
    return pl.pallas_call(
        body, name="adam_w_in",
        grid_spec=pltpu.PrefetchScalarGridSpec(
            num_scalar_prefetch=1, grid=(D_MODEL // tc,),
            in_specs=[pl.BlockSpec((1, n, tc), lambda j, chip_ref: (chip_ref[0], 0, j)),
                      pl.BlockSpec((3, n, tc), lambda j, chip_ref: (0, 0, j)), blk, blk, blk],
            out_specs=[blk] * 4),
        out_shape=[SDS(w.shape, F32)] * 4, compiler_params=_params("parallel"),
    )(chip, pair, parts, w, m, v)


def _adam_w_out(chip, pair, parts, w, m, v):
    rows = w.shape[0]
    tr = 64

    def body(chip_ref, own_ref, p_ref, w_ref, m_ref, v_ref, g_ref, d_ref, nm_ref, nv_ref):
        g = _sum_chips(own_ref, p_ref)
        g_ref[...] = g
        d_ref[...], nm_ref[...], nv_ref[...] = _adamw(w_ref[...], g, m_ref[...], v_ref[...])

    blk = pl.BlockSpec((tr, D_MODEL), lambda j, chip_ref: (j, 0))
    return pl.pallas_call(
        body, name="adam_w_out",
        grid_spec=pltpu.PrefetchScalarGridSpec(
            num_scalar_prefetch=1, grid=(rows // tr,),
            in_specs=[pl.BlockSpec((1, tr, D_MODEL), lambda j, chip_ref: (chip_ref[0], j, 0)),
                      pl.BlockSpec((3, tr, D_MODEL), lambda j, chip_ref: (0, j, 0)), blk, blk, blk],
            out_specs=[blk] * 4),
        out_shape=[SDS(w.shape, F32)] * 4, compiler_params=_params("parallel"),
    )(chip, pair, parts, w, m, v)


def _adam_ada(name, cact, dmod, w, m, v):
    n = w.shape[1]
    tr = 512

    def body(c_ref, dm_ref, w_ref, m_ref, v_ref, g_ref, d_ref, nm_ref, nv_ref):
        pad_c = jnp.concatenate([c_ref[...], jnp.zeros_like(c_ref)], axis=0).astype(BF16)
        pad_d = jnp.concatenate([dm_ref[...], jnp.zeros_like(dm_ref)], axis=0).astype(BF16)
        g = lax.dot_general(pad_c, pad_d, TN, preferred_element_type=F32)
        g_ref[...] = g
        d_ref[...], nm_ref[...], nv_ref[...] = _adamw(w_ref[...], g, m_ref[...], v_ref[...])

    blk = pl.BlockSpec((tr, n), lambda j: (j, 0))
    return pl.pallas_call(
        body, name=name, grid=(D_MODEL // tr,),
        in_specs=[pl.BlockSpec((N_DEV, tr), lambda j: (0, j)), pl.BlockSpec((N_DEV, n), lambda j: (0, 0)),
                  blk, blk, blk],
        out_specs=[blk] * 4, out_shape=[SDS(w.shape, F32)] * 4,
        compiler_params=_params("parallel"),
    )(cact, dmod, w, m, v)


def _adam_small(parts, w, m, v):
    def body(p_ref, w_ref, m_ref, v_ref, g_ref, d_ref, nm_ref, nv_ref):
        g = p_ref[0]
        for j in range(1, N_DEV):
            g = g + p_ref[j]
        g_ref[...] = g
        d_ref[...], nm_ref[...], nv_ref[...] = _adamw(w_ref[...], g, m_ref[...], v_ref[...])

    vmem = pl.BlockSpec(memory_space=pltpu.VMEM)
    return pl.pallas_call(
        body, name="adam_small", in_specs=[vmem] * 4, out_specs=[vmem] * 4,
        out_shape=[SDS(w.shape, F32)] * 4, compiler_params=_params(),
    )(parts, w, m, v)


def _pack_small(w_sp, norm_g, ln_g, ln_b, b_sp, sinks, fng, b_ada, b_ada_f):
    row_bsp = jnp.concatenate([b_sp.reshape(1, 1024), sinks.reshape(1, 16), jnp.zeros((1, 1008), F32)], axis=1)
    return jnp.concatenate([
        w_sp.reshape(64, D_MODEL), norm_g.reshape(1, D_MODEL),
        jnp.concatenate([ln_g.reshape(1, D_A), ln_b.reshape(1, D_A)], axis=1), row_bsp, fng.reshape(1, D_MODEL),
        b_ada.reshape(3, D_MODEL), b_ada_f.reshape(2, D_MODEL),
        jnp.zeros((SMALL_ROWS - 73, D_MODEL), F32)], axis=0)


def _unpack_small(p):
    return dict(
        w_spatial=p[ROW_WSP:ROW_WSP + 64].reshape(1, A_GROUPS, CHUNK, CHUNK),
        norm_g=p[ROW_NG].reshape(1, D_MODEL),
        ln_v_g=p[ROW_LN, :D_A].reshape(1, D_A), ln_v_b=p[ROW_LN, D_A:].reshape(1, D_A),
        b_spatial=p[ROW_BSP, :1024].reshape(1, A_GROUPS, CHUNK), sinks=p[ROW_BSP, 1024:1040].reshape(1, 16),
        final_norm_g=p[ROW_FNG].reshape(D_MODEL),
        b_ada=p[ROW_BADA:ROW_BADA + 3].reshape(1, 3 * D_MODEL), b_ada_final=p[ROW_BADAF:ROW_BADAF + 2].reshape(2 * D_MODEL),
    )


def kernel(x, c, w_ada, b_ada, norm_g, w_in, ln_v_g, ln_v_b, w_spatial, b_spatial, sinks, w_out, w_ada_final, b_ada_final, final_norm_g, loss_target, m_w_ada, m_b_ada, m_norm_g, m_w_in, m_ln_v_g, m_ln_v_b, m_w_spatial, m_b_spatial, m_sinks, m_w_out, m_w_ada_final, m_b_ada_final, m_final_norm_g, v_w_ada, v_b_ada, v_norm_g, v_w_in, v_ln_v_g, v_ln_v_b, v_w_spatial, v_b_spatial, v_sinks, v_w_out, v_w_ada_final, v_b_ada_final, v_final_norm_g):
    seq = x.shape[1]
    me = 4 * lax.axis_index("x") + 2 * lax.axis_index("y") + lax.axis_index("c")
    x2, tgt = x[0], loss_target[0]
    fng = final_norm_g.reshape(1, D_MODEL)

    n_ada, n_ada_f = w_ada.shape[2], w_ada_final.shape[1]
    cact, mod, mod_f = _ada_exchange(c, w_ada[0], b_ada.reshape(N_DEV, n_ada), w_ada_final,
                                     b_ada_final.reshape(N_DEV, n_ada_f))
    cact = cact.reshape(N_DEV, D_MODEL)
    mod, mod_f = mod.reshape(1, 3 * D_MODEL), mod_f.reshape(1, 2 * D_MODEL)
    shift, scale, gate = mod[:, :D_MODEL], mod[:, D_MODEL:2 * D_MODEL], mod[:, 2 * D_MODEL:]
    shift_f, scale_f = mod_f[:, :D_MODEL], mod_f[:, D_MODEL:]

    wt_shard, wo_shard = _prep_weights(w_in[0], w_out[0])
    wt, wo = _all_gather("gather_weights", [wt_shard, wo_shard], pl.ANY)
    wt, wo = wt.reshape(D_IN, D_MODEL), wo.reshape(D_MODEL, D_MODEL)

    tabs = _rope_tables(seq)
    sinks_v = sinks.reshape(16)
    h, proj = _in_proj(x2, shift, scale, norm_g, wt)
    y = _mixer_fwd(proj, tabs, ln_v_g, ln_v_b, w_spatial[0], b_spatial[0], sinks_v)
    dx1, do, dy, sums_o = _out_proj_loss(y, x2, tgt, wo, gate, shift_f, scale_f, fng)
    loss = lax.psum(0.5 * jnp.sum(sums_o[4]) / D_MODEL, ("x", "y", "c"))

    core = lax.axis_index("c").reshape(1)
    chip = (2 * lax.axis_index("x") + lax.axis_index("y")).reshape(1)
    g_wo = _wgrad("wgrad_out", y, do).reshape(N_DEV, D_MODEL // N_DEV, D_MODEL)
    pair_out = _pair_sum(core, g_wo, _rs_pair("rs_pair_out", g_wo))
    dproj, d_ln, d_wsp, d_bsp, d_sinks, parts_out = _mixer_bwd(
        proj, dy, tabs, ln_v_g, ln_v_b, w_spatial[0], b_spatial[0], sinks_v, pair_out)
    g_wt = _wgrad("wgrad_in", dproj, h).reshape(N_DEV, D_IN // N_DEV, D_MODEL)
    pair_in = _pair_sum(core, g_wt, _rs_pair("rs_pair_in", g_wt))
    grad_x, sums_i, parts_in = _in_proj_bwd(dproj, wt, x2, dx1, scale, norm_g, pair_in)
    g_w_in, d_w_in, nm_w_in, nv_w_in = _adam_w_in(chip, pair_in, parts_in, w_in[0], m_w_in[0], v_w_in[0])
    g_w_out, d_w_out, nm_w_out, nv_w_out = _adam_w_out(chip, pair_out, parts_out, w_out[0], m_w_out[0], v_w_out[0])

    dmod = jnp.concatenate([sums_i[0], sums_i[1], sums_o[0]])
    dmod_f = jnp.concatenate([sums_o[1], sums_o[2]])
    small = _pack_small(d_wsp, sums_i[2], d_ln[0], d_ln[1], d_bsp, d_sinks[:, 0], sums_o[3], dmod, dmod_f)
    (small_all,) = _all_gather("gather_small", [small], pltpu.VMEM)
    packed = [_pack_small(*t) for t in (
        (w_spatial, norm_g, ln_v_g, ln_v_b, b_spatial, sinks, final_norm_g, b_ada, b_ada_final),
        (m_w_spatial, m_norm_g, m_ln_v_g, m_ln_v_b, m_b_spatial, m_sinks, m_final_norm_g, m_b_ada, m_b_ada_final),
        (v_w_spatial, v_norm_g, v_ln_v_g, v_ln_v_b, v_b_spatial, v_sinks, v_final_norm_g, v_b_ada, v_b_ada_final))]
    g_s, d_s, nm_s, nv_s = [_unpack_small(p) for p in _adam_small(small_all, *packed)]

    dmod_all = small_all[:, ROW_BADA:ROW_BADA + 3].reshape(N_DEV, 3 * D_MODEL)
    dmod_f_all = small_all[:, ROW_BADAF:ROW_BADAF + 2].reshape(N_DEV, 2 * D_MODEL)
    dmod_mine = lax.dynamic_slice_in_dim(dmod_all, me * n_ada, n_ada, axis=1)
    dmod_f_mine = lax.dynamic_slice_in_dim(dmod_f_all, me * n_ada_f, n_ada_f, axis=1)
    ada = _adam_ada("adam_w_ada", cact, dmod_mine, w_ada[0], m_w_ada[0], v_w_ada[0])
    ada_f = _adam_ada("adam_w_ada_final", cact, dmod_f_mine, w_ada_final, m_w_ada_final, v_w_ada_final)

    def leaves(k):
        small_k = (g_s, d_s, nm_s, nv_s)[k]
        return (ada[k][None], small_k["b_ada"], small_k["norm_g"], (g_w_in, d_w_in, nm_w_in, nv_w_in)[k][None],
                small_k["ln_v_g"], small_k["ln_v_b"], small_k["w_spatial"], small_k["b_spatial"], small_k["sinks"],
                (g_w_out, d_w_out, nm_w_out, nv_w_out)[k][None], ada_f[k], small_k["b_ada_final"],
                small_k["final_norm_g"])

    return (loss, grad_x[None], *leaves(0), *leaves(1), *leaves(2), *leaves(3))
```

```python
import functools

import jax
import jax.numpy as jnp
from jax import lax
from jax.experimental import pallas as pl
from jax.experimental.pallas import tpu as pltpu

D_MODEL = 2048
D_IN = 5632
D_A = 1024
CHUNK = 128
A_GROUPS = 8
HEAD_DIM = 64
N_KV_HEADS = 4
N_DEV = 8
ROPE_THETA = 10000.0
NORM_EPS = 1e-5
ATTN_SCALE = HEAD_DIM ** -0.5

ADAM_LR = 0.001
ADAM_B1 = 0.9
ADAM_B2 = 0.999
ADAM_EPS = 1e-08
ADAM_WD = 0.01
ADAM_STEP = 10

OFF_U, OFF_VA, OFF_ZA, OFF_Q, OFF_K, OFF_V, OFF_ZB = 0, 1024, 2048, 3072, 4096, 4352, 4608

SMALL_ROWS = 80
ROW_WSP, ROW_NG, ROW_LN, ROW_BSP, ROW_FNG, ROW_BADA, ROW_BADAF = 0, 64, 65, 66, 67, 68, 71

V7X_VMEM_LIMIT_BYTES = 56 * 1024 * 1024

F32 = jnp.float32
BF16 = jnp.bfloat16
MESH = pl.DeviceIdType.MESH
SDS = jax.ShapeDtypeStruct
NT = (((1,), (1,)), ((), ()))
TN = (((0,), (0,)), ((), ()))


def _params(*semantics):
    return pltpu.CompilerParams(dimension_semantics=semantics or None, vmem_limit_bytes=V7X_VMEM_LIMIT_BYTES)


def _mesh_pos():
    return lax.axis_index("x"), lax.axis_index("y"), lax.axis_index("c")


def _sigmoid(z):
    return 1.0 / (1.0 + jnp.exp(-z))


def _adamw(w, g, m, v):
    m = ADAM_B1 * m + (1.0 - ADAM_B1) * g
    v = ADAM_B2 * v + (1.0 - ADAM_B2) * (g * g)
    m_hat = m / (1.0 - ADAM_B1 ** ADAM_STEP)
    v_hat = v / (1.0 - ADAM_B2 ** ADAM_STEP)
    delta = -ADAM_LR * (m_hat / (jnp.sqrt(v_hat) + ADAM_EPS) + ADAM_WD * w)
    return delta, m, v


def _all_gather(name, blocks, memory_space):
    n_arr = len(blocks)

    def body(*refs):
        ins, outs = refs[:n_arr], refs[n_arr:2 * n_arr]
        send_sems, recv_sems, local_sems = refs[2 * n_arr:]
        x, y, c = _mesh_pos()
        me, sibling = (x, y, c), (x, y, 1 - c)
        chips = [(1 - x, y), (x, 1 - y), (1 - x, 1 - y)]

        def slot(p):
            return 4 * p[0] + 2 * p[1] + p[2]

        def copy(a, k, block, to, src=None):
            dst = outs[a].at[slot(block)]
            return pltpu.make_async_remote_copy(
                src_ref=dst if src is None else src, dst_ref=dst,
                send_sem=send_sems.at[a, k], recv_sem=recv_sems.at[a, k],
                device_id=to, device_id_type=MESH)

        mine = [pltpu.make_async_copy(ins[a], outs[a].at[slot(me)], local_sems.at[a]) for a in range(n_arr)]
        for cp in mine:
            cp.start()
        first = []
        for a in range(n_arr):
            first.append(copy(a, 0, me, sibling, src=ins[a]))
            first += [copy(a, 1 + j, me, (*chip, c), src=ins[a]) for j, chip in enumerate(chips)]
        for cp in first:
            cp.start()
        passed = []
        for j, chip in enumerate(chips):
            for a in range(n_arr):
                copy(a, 1 + j, (*chip, c), me).wait_recv()
                fwd = copy(a, 4 + j, (*chip, c), sibling)
                fwd.start()
                passed.append(fwd)
        for a in range(n_arr):
            copy(a, 0, sibling, me).wait_recv()
            for j, chip in enumerate(chips):
                copy(a, 4 + j, (*chip, 1 - c), me).wait_recv()
        for cp in first + passed:
            cp.wait_send()
        for cp in mine:
            cp.wait()

    spec = pl.BlockSpec(memory_space=memory_space)
    return pl.pallas_call(
        body, name=name,
        out_shape=[SDS((N_DEV,) + b.shape, b.dtype) for b in blocks],
        in_specs=[spec] * n_arr, out_specs=[spec] * n_arr,
        scratch_shapes=[pltpu.SemaphoreType.DMA((n_arr, 7)), pltpu.SemaphoreType.DMA((n_arr, 7)),
                        pltpu.SemaphoreType.DMA((n_arr,))],
        compiler_params=_params(),
    )(*blocks)


def _ada_exchange(c, w_ada, b_ada8, w_ada_f, b_ada_f8):
    n1, n2 = w_ada.shape[1], w_ada_f.shape[1]

    def body(c_ref, w1_ref, b1_ref, w2_ref, b2_ref, cact_ref, mod_ref, modf_ref,
             cact_buf, res1, res2, send1, send2, sems_s, sems_r):
        x, y, c_pos = _mesh_pos()
        me = 4 * x + 2 * y + c_pos
        flips = [(k >> 2 & 1, k >> 1 & 1, k & 1) for k in range(1, N_DEV)]

        def peer(f):
            return (1 - x if f[0] else x, 1 - y if f[1] else y, 1 - c_pos if f[2] else c_pos)

        cv = c_ref[...]
        cact = cv * _sigmoid(cv)
        cact_buf[...] = cact
        cact_ref[me] = cact

        def rdma(phase, k, src, dst, f):
            return pltpu.make_async_remote_copy(src_ref=src, dst_ref=dst, send_sem=sems_s.at[phase, k],
                                                recv_sem=sems_r.at[phase, k], device_id=peer(f), device_id_type=MESH)

        gather = [rdma(0, k, cact_buf, cact_ref.at[me], f) for k, f in enumerate(flips)]
        for cp in gather:
            cp.start()
        for cp in gather:
            cp.wait_recv()
        for cp in gather:
            cp.wait_send()

        rid = lax.broadcasted_iota(jnp.int32, (N_DEV, D_MODEL), 0)
        rows = jnp.zeros((N_DEV, D_MODEL), F32)
        for j in range(N_DEV):
            rows = jnp.where(rid == j, jnp.broadcast_to(cact_ref[j], (N_DEV, D_MODEL)), rows)
        rows = rows.astype(BF16)
        res1[...] = jnp.dot(rows, w1_ref[...].astype(BF16), preferred_element_type=F32) + b1_ref[pl.ds(me, 1), :]
        res2[...] = jnp.dot(rows, w2_ref[...].astype(BF16), preferred_element_type=F32) + b2_ref[pl.ds(me, 1), :]
        for j in range(N_DEV):
            send1[j] = res1[pl.ds(j, 1), :]
            send2[j] = res2[pl.ds(j, 1), :]
        mod_ref[me] = send1[me]
        modf_ref[me] = send2[me]
        scatter = []
        for k, f in enumerate(flips):
            to = me ^ (k + 1)
            scatter.append(rdma(1, k, send1.at[to], mod_ref.at[me], f))
            scatter.append(rdma(2, k, send2.at[to], modf_ref.at[me], f))
        for cp in scatter:
            cp.start()
        for cp in scatter:
            cp.wait_recv()
        for cp in scatter:
            cp.wait_send()

    vmem = pl.BlockSpec(memory_space=pltpu.VMEM)
    return pl.pallas_call(
        body, name="ada_exchange",
        out_shape=[SDS((N_DEV, 1, D_MODEL), F32), SDS((N_DEV, 1, n1), F32), SDS((N_DEV, 1, n2), F32)],
        in_specs=[vmem] * 5, out_specs=[vmem] * 3,
        scratch_shapes=[pltpu.VMEM((1, D_MODEL), F32), pltpu.VMEM((N_DEV, n1), F32), pltpu.VMEM((N_DEV, n2), F32),
                        pltpu.VMEM((N_DEV, 1, n1), F32), pltpu.VMEM((N_DEV, 1, n2), F32),
                        pltpu.SemaphoreType.DMA((3, 7)), pltpu.SemaphoreType.DMA((3, 7))],
        compiler_params=_params(),
    )(c, w_ada, b_ada8, w_ada_f, b_ada_f8)


def _rs_pair(name, grad):
    def body(g_ref, got_ref, send_sems, recv_sems):
        x, y, c = _mesh_pos()
        copies = [pltpu.make_async_remote_copy(
            src_ref=g_ref.at[2 * q + 1 - c], dst_ref=got_ref.at[q], send_sem=send_sems.at[q], recv_sem=recv_sems.at[q],
            device_id=(x, y, 1 - c), device_id_type=MESH) for q in range(4)]
        for cp in copies:
            cp.start()
        for cp in copies:
            cp.wait_recv()
        for cp in copies:
            cp.wait_send()

    hbm = pl.BlockSpec(memory_space=pl.ANY)
    return pl.pallas_call(
        body, name=name, out_shape=SDS((4,) + grad.shape[1:], grad.dtype), in_specs=[hbm], out_specs=hbm,
        scratch_shapes=[pltpu.SemaphoreType.DMA((4,)), pltpu.SemaphoreType.DMA((4,))],
        compiler_params=_params(),
    )(grad)


def _chip_scatter(pair_ref, parts_ref, send_sems, recv_sems):
    x, y, c = _mesh_pos()
    chips = [(1 - x, y), (x, 1 - y), (1 - x, 1 - y)]
    return [pltpu.make_async_remote_copy(
        src_ref=pair_ref.at[2 * cx + cy], dst_ref=parts_ref.at[j], send_sem=send_sems.at[j], recv_sem=recv_sems.at[j],
        device_id=(cx, cy, c), device_id_type=MESH) for j, (cx, cy) in enumerate(chips)]


def _scatter_scratch():
    return [pltpu.SemaphoreType.DMA((3,)), pltpu.SemaphoreType.DMA((3,))]


def _prep_weights(wt, w_out):
    steps = 4

    def body(wt_ref, wo_ref, wtb_ref, wob_ref):
        wtb_ref[...] = wt_ref[...].astype(BF16)
        wob_ref[...] = wo_ref[...].astype(BF16)

    def rows(a):
        return pl.BlockSpec((a.shape[0] // steps, a.shape[1]), lambda i: (i, 0))

    return pl.pallas_call(
        body, name="prep_weights", grid=(steps,),
        in_specs=[rows(wt), rows(w_out)], out_specs=[rows(wt), rows(w_out)],
        out_shape=[SDS(wt.shape, BF16), SDS(w_out.shape, BF16)],
        compiler_params=_params("parallel"),
    )(wt, w_out)


def _in_proj(x, shift, scale, norm_g, wt):
    s = x.shape[0]
    tm, tn = min(1024, s), 512

    def body(x_ref, shift_ref, scale_ref, g_ref, wt_ref, h_ref, proj_ref):
        @pl.when(pl.program_id(1) == 0)
        def _():
            xv = x_ref[...]
            r = lax.rsqrt(jnp.mean(xv * xv, axis=-1, keepdims=True) + NORM_EPS)
            h = ((xv * r) * g_ref[...]) * (1.0 + scale_ref[...]) + shift_ref[...]
            h_ref[...] = h.astype(BF16)

        proj_ref[...] = lax.dot_general(h_ref[...], wt_ref[...], NT, preferred_element_type=F32).astype(BF16)

    row = pl.BlockSpec((1, D_MODEL), lambda i, j: (0, 0))
    return pl.pallas_call(
        body, name="in_proj", grid=(s // tm, D_IN // tn),
        in_specs=[pl.BlockSpec((tm, D_MODEL), lambda i, j: (i, 0)), row, row, row,
                  pl.BlockSpec((tn, D_MODEL), lambda i, j: (j, 0))],
        out_specs=[pl.BlockSpec((tm, D_MODEL), lambda i, j: (i, 0)), pl.BlockSpec((tm, tn), lambda i, j: (i, j))],
        out_shape=[SDS((s, D_MODEL), BF16), SDS((s, D_IN), BF16)],
        compiler_params=_params("parallel", "arbitrary"),
    )(x, shift, scale, norm_g, wt)


def _rope_tables(seq):
    inv_freq = ROPE_THETA ** (-jnp.arange(0, HEAD_DIM, 2, dtype=F32) / HEAD_DIM)
    ang = jnp.arange(seq, dtype=F32)[:, None] * inv_freq[None, :]
    cos, sin, zero = jnp.cos(ang), jnp.sin(ang), jnp.zeros_like(ang)
    return (jnp.concatenate([cos] * 4, axis=1), jnp.concatenate([-sin, zero, -sin, zero], axis=1),
            jnp.concatenate([zero, sin, zero, sin], axis=1))


def _rope(v, cos, sin_lo, sin_hi):
    width = v.shape[1]
    rep = (1, width // 128)
    return (v * jnp.tile(cos, rep) + pltpu.roll(v, width - 32, 1) * jnp.tile(sin_lo, rep)
            + pltpu.roll(v, 32, 1) * jnp.tile(sin_hi, rep))


def _rope_bwd(d, cos, sin_lo, sin_hi):
    width = d.shape[1]
    rep = (1, width // 128)
    return (d * jnp.tile(cos, rep) + pltpu.roll(d * jnp.tile(sin_lo, rep), 32, 1)
            + pltpu.roll(d * jnp.tile(sin_hi, rep), width - 32, 1))


def _layer_norm(v, g, b):
    mu = jnp.mean(v, axis=-1, keepdims=True)
    vc = v - mu
    rstd = lax.rsqrt(jnp.mean(vc * vc, axis=-1, keepdims=True) + NORM_EPS)
    vhat = vc * rstd
    return vhat * g + b, vhat, rstd


def _tril_bf16(w_ref, g):
    t = lax.broadcasted_iota(jnp.int32, (CHUNK, CHUNK), 0)
    tp = lax.broadcasted_iota(jnp.int32, (CHUNK, CHUNK), 1)
    return jnp.where(tp <= t, w_ref[g], 0.0).astype(BF16)


def _bias_columns(b_ref, out_ref):
    for g in range(A_GROUPS):
        out_ref[g] = jnp.broadcast_to(b_ref[pl.ds(g, 1), :], (CHUNK, CHUNK)).T


def _band_mask():
    kj = lax.broadcasted_iota(jnp.int32, (2 * CHUNK, 4 * CHUNK), 0)
    qi = lax.broadcasted_iota(jnp.int32, (2 * CHUNK, 4 * CHUNK), 1) & (CHUNK - 1)
    rel = qi + CHUNK - kj
    return jnp.where((rel >= 0) & (rel < CHUNK), 0.0, -jnp.inf)


def _low_lanes():
    return lax.broadcasted_iota(jnp.int32, (1, 128), 1) < HEAD_DIM


def _stack_heads(pair_a, pair_b):
    lo = _low_lanes()
    return jnp.concatenate([jnp.where(lo, pair_a, 0.0), jnp.where(lo, 0.0, pair_a),
                            jnp.where(lo, pair_b, 0.0), jnp.where(lo, 0.0, pair_b)], axis=0).astype(BF16)


def _heads_to_lanes(per_group):
    rows = [t[:, r * CHUNK:(r + 1) * CHUNK] for t in per_group for r in range(4)]
    return jnp.concatenate(rows, axis=0).T


def _dup_kv_head(band, gk):
    pair = band[:, (gk // 2) * 128:(gk // 2 + 1) * 128]
    lo = _low_lanes()
    one = jnp.where(lo if gk % 2 == 0 else jnp.logical_not(lo), pair, 0.0)
    return (one + pltpu.roll(one, HEAD_DIM, 1)).astype(BF16)


def _fold_kv_head(dup_grad, gk):
    both = dup_grad + pltpu.roll(dup_grad, HEAD_DIM, 1)
    lo = _low_lanes()
    return jnp.where(lo if gk % 2 == 0 else jnp.logical_not(lo), both, 0.0)


def _attn_probs(q_st, k_dup, sink_row, mask, first_block):
    s = lax.dot_general(k_dup, q_st, NT, preferred_element_type=F32) * ATTN_SCALE + mask
    s = jnp.concatenate([jnp.where(first_block, -jnp.inf, s[:CHUNK]), s[CHUNK:]], axis=0)
    m = jnp.maximum(jnp.max(s, axis=0, keepdims=True), sink_row)
    p = jnp.exp(s - m)
    e_sink = jnp.exp(sink_row - m)
    inv = 1.0 / (jnp.sum(p, axis=0, keepdims=True) + e_sink)
    return p * inv, e_sink * inv


def _sink_row(sinks_ref, gk):
    return jnp.concatenate([jnp.full((1, CHUNK), sinks_ref[4 * gk + r], F32) for r in range(4)], axis=1)


def _mixer_specs(nb, rev):
    def blk(i):
        return nb - 1 - i if rev else i

    def prev(i):
        return jnp.maximum(blk(i) - 1, 0)

    tab = pl.BlockSpec((CHUNK, 128), lambda i: (blk(i), 0))
    tab_prev = pl.BlockSpec((CHUNK, 128), lambda i: (prev(i), 0))
    return dict(
        cur=pl.BlockSpec((CHUNK, D_IN), lambda i: (blk(i), 0)),
        prev_kv=pl.BlockSpec((CHUNK, 2 * 256), lambda i: (prev(i), OFF_K // 512)),
        tabs=[tab] * 3 + [tab_prev] * 3,
        vec=pl.BlockSpec((1, D_A), lambda i: (0, 0)),
        wsp=pl.BlockSpec((A_GROUPS, CHUNK, CHUNK), lambda i: (0, 0, 0)),
        bsp=pl.BlockSpec((A_GROUPS, CHUNK), lambda i: (0, 0)),
        smem=pl.BlockSpec(memory_space=pltpu.SMEM),
        blk=blk,
    )


def _mixer_fwd(proj, tabs, ln_g, ln_b, w_sp, b_sp, sinks):
    s = proj.shape[0]
    nb = s // CHUNK
    sp = _mixer_specs(nb, rev=False)

    def body(cur_ref, pkv_ref, c_ref, s1_ref, s2_ref, cp_ref, s1p_ref, s2p_ref, lg_ref, lb_ref, w_ref, b_ref,
             sinks_ref, y_ref, bcol, mask):
        i = pl.program_id(0)

        @pl.when(i == 0)
        def _():
            _bias_columns(b_ref, bcol)
            mask[...] = _band_mask()

        vln, _, _ = _layer_norm(cur_ref[:, OFF_VA:OFF_ZA].astype(F32), lg_ref[...], lb_ref[...])
        vln = vln.astype(BF16)
        for g in range(A_GROUPS):
            cols = slice(g * 128, (g + 1) * 128)
            sg = jnp.dot(_tril_bf16(w_ref, g), vln[:, cols], preferred_element_type=F32) + bcol[g]
            u = cur_ref[:, OFF_U + g * 128:OFF_U + (g + 1) * 128].astype(F32)
            z = cur_ref[:, OFF_ZA + g * 128:OFF_ZA + (g + 1) * 128].astype(F32)
            y_ref[:, cols] = (u * sg * (z * _sigmoid(z))).astype(BF16)

        cur_t = (c_ref[...], s1_ref[...], s2_ref[...])
        prev_t = (cp_ref[...], s1p_ref[...], s2p_ref[...])
        qr = _rope(cur_ref[:, OFF_Q:OFF_K].astype(F32), *cur_t)
        kr = jnp.concatenate([_rope(pkv_ref[:, 0:256].astype(F32), *prev_t),
                              _rope(cur_ref[:, OFF_K:OFF_V].astype(F32), *cur_t)], axis=0)
        v_t = jnp.concatenate([pkv_ref[:, 256:512], cur_ref[:, OFF_V:OFF_ZB]], axis=0).astype(F32).T.astype(BF16)
        outs = []
        for gk in range(N_KV_HEADS):
            q_st = _stack_heads(qr[:, (2 * gk) * 128:(2 * gk + 1) * 128], qr[:, (2 * gk + 1) * 128:(2 * gk + 2) * 128])
            probs, _ = _attn_probs(q_st, _dup_kv_head(kr, gk), _sink_row(sinks_ref, gk), mask[...], i == 0)
            outs.append(jnp.dot(v_t[gk * HEAD_DIM:(gk + 1) * HEAD_DIM], probs.astype(BF16),
                                preferred_element_type=F32))
        zb = cur_ref[:, OFF_ZB:D_IN].astype(F32)
        y_ref[:, D_A:D_MODEL] = (_heads_to_lanes(outs) * (zb * _sigmoid(zb))).astype(BF16)

    return pl.pallas_call(
        body, name="mixer_fwd", grid=(nb,),
        in_specs=[sp["cur"], sp["prev_kv"], *sp["tabs"], sp["vec"], sp["vec"], sp["wsp"], sp["bsp"], sp["smem"]],
        out_specs=pl.BlockSpec((CHUNK, D_MODEL), lambda i: (i, 0)),
        out_shape=SDS((s, D_MODEL), BF16),
        scratch_shapes=[pltpu.VMEM((A_GROUPS, CHUNK, CHUNK), F32), pltpu.VMEM((2 * CHUNK, 4 * CHUNK), F32)],
        compiler_params=_params("arbitrary"),
    )(proj, proj, *tabs, *tabs, ln_g, ln_b, w_sp, b_sp, sinks)


def _out_proj_loss(y, x, target, wo, gate, shift_f, scale_f, fng):
    s = y.shape[0]
    tm = 256

    def body(y_ref, x_ref, t_ref, wo_ref, gate_ref, sh_ref, sc_ref, g_ref, dx1_ref, do_ref, dy_ref, sums_ref):
        @pl.when(pl.program_id(0) == 0)
        def _():
            sums_ref[...] = jnp.zeros_like(sums_ref)

        o = jnp.dot(y_ref[...], wo_ref[...], preferred_element_type=F32)
        gate = gate_ref[...]
        x1 = x_ref[...] + gate * o
        rf = lax.rsqrt(jnp.mean(x1 * x1, axis=-1, keepdims=True) + NORM_EPS)
        x1n = x1 * rf
        hn = x1n * g_ref[...]
        one_sc = 1.0 + sc_ref[...]
        diff = hn * one_sc + sh_ref[...] - t_ref[...]
        dout = diff * (1.0 / D_MODEL)
        d_hn = dout * one_sc
        d_x1n = d_hn * g_ref[...]
        dx1 = rf * (d_x1n - x1n * jnp.mean(d_x1n * x1n, axis=-1, keepdims=True))
        dx1_ref[...] = dx1
        do = (dx1 * gate).astype(BF16)
        do_ref[...] = do
        dy_ref[...] = lax.dot_general(do, wo_ref[...], NT, preferred_element_type=F32).astype(BF16)

        def rowsum(v):
            return jnp.sum(v, axis=0, keepdims=True)

        sums_ref[0:1, :] += rowsum(dx1 * o)
        sums_ref[1:2, :] += rowsum(dout)
        sums_ref[2:3, :] += rowsum(dout * hn)
        sums_ref[3:4, :] += rowsum(d_hn * x1n)
        sums_ref[4:5, :] += rowsum(diff * diff)

    tile = pl.BlockSpec((tm, D_MODEL), lambda i: (i, 0))
    row = pl.BlockSpec((1, D_MODEL), lambda i: (0, 0))
    return pl.pallas_call(
        body, name="out_proj_loss", grid=(s // tm,),
        in_specs=[tile, tile, tile, pl.BlockSpec((D_MODEL, D_MODEL), lambda i: (0, 0)), row, row, row, row],
        out_specs=[tile, tile, tile, pl.BlockSpec((8, D_MODEL), lambda i: (0, 0))],
        out_shape=[SDS((s, D_MODEL), F32), SDS((s, D_MODEL), BF16), SDS((s, D_MODEL), BF16), SDS((8, D_MODEL), F32)],
        compiler_params=_params("arbitrary"),
    )(y, x, target, wo, gate, shift_f, scale_f, fng)


def _mixer_bwd(proj, dy, tabs, ln_g, ln_b, w_sp, b_sp, sinks, pair):
    s = proj.shape[0]
    nb = s // CHUNK
    sp = _mixer_specs(nb, rev=True)

    def body(cur_ref, pkv_ref, dy_ref, c_ref, s1_ref, s2_ref, cp_ref, s1p_ref, s2p_ref, lg_ref, lb_ref, w_ref, b_ref,
             sinks_ref, pair_ref, dproj_ref, dln_ref, dw_ref, db_ref, dsink_ref, parts_ref, bcol, dbcol, carry, mask,
             send_sems, recv_sems):
        i = pl.program_id(0)
        block = nb - 1 - i

        @pl.when(i == 0)
        def _():
            for cp in _chip_scatter(pair_ref, parts_ref, send_sems, recv_sems):
                cp.start()
            _bias_columns(b_ref, bcol)
            mask[...] = _band_mask()
            dbcol[...] = jnp.zeros_like(dbcol)
            carry[...] = jnp.zeros_like(carry)
            dln_ref[...] = jnp.zeros_like(dln_ref)
            dw_ref[...] = jnp.zeros_like(dw_ref)
            dsink_ref[...] = jnp.zeros_like(dsink_ref)

        vln, vhat, rstd = _layer_norm(cur_ref[:, OFF_VA:OFF_ZA].astype(F32), lg_ref[...], lb_ref[...])
        vln = vln.astype(BF16)
        d_vln = []
        for g in range(A_GROUPS):
            cols = slice(g * 128, (g + 1) * 128)
            w_g = _tril_bf16(w_ref, g)
            sg = jnp.dot(w_g, vln[:, cols], preferred_element_type=F32) + bcol[g]
            u = cur_ref[:, OFF_U + g * 128:OFF_U + (g + 1) * 128].astype(F32)
            z = cur_ref[:, OFF_ZA + g * 128:OFF_ZA + (g + 1) * 128].astype(F32)
            dya = dy_ref[:, cols].astype(F32)
            sig = _sigmoid(z)
            d_ya = dya * (z * sig)
            dproj_ref[:, OFF_ZA + g * 128:OFF_ZA + (g + 1) * 128] = (
                dya * (u * sg) * (sig * (1.0 + z * (1.0 - sig)))).astype(BF16)
            dproj_ref[:, OFF_U + g * 128:OFF_U + (g + 1) * 128] = (d_ya * sg).astype(BF16)
            d_s = d_ya * u
            dbcol[g] += d_s
            d_sb = d_s.astype(BF16)
            dw_ref[g] += lax.dot_general(d_sb, vln[:, cols], NT, preferred_element_type=F32)
            d_vln.append(lax.dot_general(w_g, d_sb, TN, preferred_element_type=F32))
        d_vln = jnp.concatenate(d_vln, axis=1)
        dln_ref[0:1, :] += jnp.sum(d_vln * vhat, axis=0, keepdims=True)
        dln_ref[1:2, :] += jnp.sum(d_vln, axis=0, keepdims=True)
        d_vhat = d_vln * lg_ref[...]
        d_va = rstd * (d_vhat - jnp.mean(d_vhat, axis=-1, keepdims=True)
                       - vhat * jnp.mean(d_vhat * vhat, axis=-1, keepdims=True))
        dproj_ref[:, OFF_VA:OFF_ZA] = d_va.astype(BF16)

        cur_t = (c_ref[...], s1_ref[...], s2_ref[...])
        prev_t = (cp_ref[...], s1p_ref[...], s2p_ref[...])
        band_t = tuple(jnp.concatenate([p, c], axis=0) for p, c in zip(prev_t, cur_t))
        qr = _rope(cur_ref[:, OFF_Q:OFF_K].astype(F32), *cur_t)
        kr = jnp.concatenate([_rope(pkv_ref[:, 0:256].astype(F32), *prev_t),
                              _rope(cur_ref[:, OFF_K:OFF_V].astype(F32), *cur_t)], axis=0)
        vb = jnp.concatenate([pkv_ref[:, 256:512], cur_ref[:, OFF_V:OFF_ZB]], axis=0).astype(F32)
        k_t, v_t = kr.T.astype(BF16), vb.T.astype(BF16)
        zb = cur_ref[:, OFF_ZB:D_IN].astype(F32)
        dyb = dy_ref[:, D_A:D_MODEL].astype(F32)
        sig = _sigmoid(zb)
        d_yb = dyb * (zb * sig)
        outs, dqs = [], []
        dk_pairs = [jnp.zeros((2 * CHUNK, 128), F32) for _ in range(2)]
        dv_pairs = [jnp.zeros((2 * CHUNK, 128), F32) for _ in range(2)]
        for gk in range(N_KV_HEADS):
            heads = slice(gk * HEAD_DIM, (gk + 1) * HEAD_DIM)
            q_st = _stack_heads(qr[:, (2 * gk) * 128:(2 * gk + 1) * 128], qr[:, (2 * gk + 1) * 128:(2 * gk + 2) * 128])
            k_dup, v_dup = _dup_kv_head(kr, gk), _dup_kv_head(vb, gk)
            probs, p_sink = _attn_probs(q_st, k_dup, _sink_row(sinks_ref, gk), mask[...], block == 0)
            probs_b = probs.astype(BF16)
            outs.append(jnp.dot(v_t[heads], probs_b, preferred_element_type=F32))
            do_st = _stack_heads(d_yb[:, (2 * gk) * 128:(2 * gk + 1) * 128], d_yb[:, (2 * gk + 1) * 128:(2 * gk + 2) * 128])
            dp = lax.dot_general(v_dup, do_st, NT, preferred_element_type=F32)
            delta = jnp.sum(probs * dp, axis=0, keepdims=True)
            ds = (probs * (dp - delta) * ATTN_SCALE).astype(BF16)
            d_sink = -p_sink * delta
            for r in range(4):
                dsink_ref[4 * gk + r:4 * gk + r + 1, :] += jnp.broadcast_to(
                    jnp.sum(d_sink[:, r * CHUNK:(r + 1) * CHUNK], axis=1, keepdims=True), (1, 128))
            dqs.append(jnp.dot(k_t[heads], ds, preferred_element_type=F32))
            dk_pairs[gk // 2] += _fold_kv_head(jnp.dot(ds, q_st, preferred_element_type=F32), gk)
            dv_pairs[gk // 2] += _fold_kv_head(jnp.dot(probs_b, do_st, preferred_element_type=F32), gk)
        dproj_ref[:, OFF_ZB:D_IN] = (dyb * _heads_to_lanes(outs) * (sig * (1.0 + zb * (1.0 - sig)))).astype(BF16)
        dproj_ref[:, OFF_Q:OFF_K] = _rope_bwd(_heads_to_lanes(dqs), *cur_t).astype(BF16)
        dk_band = _rope_bwd(jnp.concatenate(dk_pairs, axis=1), *band_t)
        dv_band = jnp.concatenate(dv_pairs, axis=1)
        dproj_ref[:, OFF_K:OFF_V] = (dk_band[CHUNK:] + carry[:, 0:256]).astype(BF16)
        dproj_ref[:, OFF_V:OFF_ZB] = (dv_band[CHUNK:] + carry[:, 256:512]).astype(BF16)
        carry[:, 0:256] = dk_band[:CHUNK]
        carry[:, 256:512] = dv_band[:CHUNK]

        @pl.when(i == nb - 1)
        def _():
            t = lax.broadcasted_iota(jnp.int32, (CHUNK, CHUNK), 0)
            tp = lax.broadcasted_iota(jnp.int32, (CHUNK, CHUNK), 1)
            for g in range(A_GROUPS):
                dw_ref[g] = jnp.where(tp <= t, dw_ref[g], 0.0)
                db_ref[pl.ds(g, 1), :] = jnp.sum(dbcol[g].T, axis=0, keepdims=True)
            scatter = _chip_scatter(pair_ref, parts_ref, send_sems, recv_sems)
            for cp in scatter:
                cp.wait_recv()
            for cp in scatter:
                cp.wait_send()

    blk = sp["blk"]
    hbm = pl.BlockSpec(memory_space=pl.ANY)
    return pl.pallas_call(
        body, name="mixer_bwd", grid=(nb,),
        in_specs=[sp["cur"], sp["prev_kv"], pl.BlockSpec((CHUNK, D_MODEL), lambda i: (blk(i), 0)), *sp["tabs"],
                  sp["vec"], sp["vec"], sp["wsp"], sp["bsp"], sp["smem"], hbm],
        out_specs=[pl.BlockSpec((CHUNK, D_IN), lambda i: (blk(i), 0)),
                   pl.BlockSpec((8, D_A), lambda i: (0, 0)),
                   pl.BlockSpec((A_GROUPS, CHUNK, CHUNK), lambda i: (0, 0, 0)),
                   pl.BlockSpec((A_GROUPS, CHUNK), lambda i: (0, 0)),
                   pl.BlockSpec((16, 128), lambda i: (0, 0)), hbm],
        out_shape=[SDS((s, D_IN), BF16), SDS((8, D_A), F32), SDS((A_GROUPS, CHUNK, CHUNK), F32),
                   SDS((A_GROUPS, CHUNK), F32), SDS((16, 128), F32), SDS((3,) + pair.shape[1:], pair.dtype)],
        scratch_shapes=[pltpu.VMEM((A_GROUPS, CHUNK, CHUNK), F32), pltpu.VMEM((A_GROUPS, CHUNK, CHUNK), F32),
                        pltpu.VMEM((CHUNK, 512), F32), pltpu.VMEM((2 * CHUNK, 4 * CHUNK), F32), *_scatter_scratch()],
        compiler_params=_params("arbitrary"),
    )(proj, proj, dy, *tabs, *tabs, ln_g, ln_b, w_sp, b_sp, sinks, pair)


def _wgrad(name, a, b, bm):
    s, m = a.shape
    n = b.shape[1]
    bt = 512
    steps = s // bt

    def body(a_ref, b_ref, out_ref, acc):
        t = pl.program_id(1)

        @pl.when(t == 0)
        def _():
            acc[...] = jnp.zeros_like(acc)

        acc[...] += lax.dot_general(a_ref[...], b_ref[...], TN, preferred_element_type=F32)

        @pl.when(t == steps - 1)
        def _():
            out_ref[...] = acc[...].astype(out_ref.dtype)

    return pl.pallas_call(
        body, name=name, grid=(m // bm, steps),
        in_specs=[pl.BlockSpec((bt, bm), lambda i, t: (t, i)), pl.BlockSpec((bt, n), lambda i, t: (t, 0))],
        out_specs=pl.BlockSpec((bm, n), lambda i, t: (i, 0)),
        out_shape=SDS((m, n), BF16),
        scratch_shapes=[pltpu.VMEM((bm, n), F32)],
        compiler_params=_params("parallel", "arbitrary"),
    )(a, b)


def _in_proj_bwd(dproj, wt, x, dx1, scale, norm_g, pair):
    s = x.shape[0]
    tm, tk, tr = min(1024, s), 512, 256
    ksteps = D_IN // tk

    def body(dp_ref, wt_ref, x_hbm, dx1_hbm, sc_ref, g_ref, pair_ref, gx_ref, sums_ref, parts_ref, x_buf, dx1_buf,
             tile_sems, send_sems, recv_sems):
        i, k = pl.program_id(0), pl.program_id(1)

        def tile_copies():
            rows = pl.ds(pl.multiple_of(i * tm, tm), tm)
            return (pltpu.make_async_copy(x_hbm.at[rows], x_buf, tile_sems.at[0]),
                    pltpu.make_async_copy(dx1_hbm.at[rows], dx1_buf, tile_sems.at[1]))

        @pl.when((i == 0) & (k == 0))
        def _():
            for cp in _chip_scatter(pair_ref, parts_ref, send_sems, recv_sems):
                cp.start()
            sums_ref[...] = jnp.zeros_like(sums_ref)

        @pl.when(k == 0)
        def _():
            for cp in tile_copies():
                cp.start()
            gx_ref[...] = jnp.dot(dp_ref[...], wt_ref[...], preferred_element_type=F32)

        @pl.when(k > 0)
        def _():
            gx_ref[...] += jnp.dot(dp_ref[...], wt_ref[...], preferred_element_type=F32)

        @pl.when(k == ksteps - 1)
        def _():
            for cp in tile_copies():
                cp.wait()
            one_sc, g = 1.0 + sc_ref[...], g_ref[...]

            def chunk(j, sums):
                rows = pl.ds(pl.multiple_of(j * tr, tr), tr)
                dh, xv = gx_ref[rows, :], x_buf[rows, :]
                r = lax.rsqrt(jnp.mean(xv * xv, axis=-1, keepdims=True) + NORM_EPS)
                xn = xv * r
                d_hn = dh * one_sc
                d_xn = d_hn * g
                gx_ref[rows, :] = dx1_buf[rows, :] + r * (d_xn - xn * jnp.mean(d_xn * xn, axis=-1, keepdims=True))
                return (sums[0] + jnp.sum(dh, axis=0, keepdims=True),
                        sums[1] + jnp.sum(dh * (xn * g), axis=0, keepdims=True),
                        sums[2] + jnp.sum(d_hn * xn, axis=0, keepdims=True))

            zero = jnp.zeros((1, D_MODEL), F32)
            sums = lax.fori_loop(0, tm // tr, chunk, (zero, zero, zero))
            sums_ref[0:1, :] += sums[0]
            sums_ref[1:2, :] += sums[1]
            sums_ref[2:3, :] += sums[2]

        @pl.when((i == s // tm - 1) & (k == ksteps - 1))
        def _():
            scatter = _chip_scatter(pair_ref, parts_ref, send_sems, recv_sems)
            for cp in scatter:
                cp.wait_recv()
            for cp in scatter:
                cp.wait_send()

    row = pl.BlockSpec((1, D_MODEL), lambda i, k: (0, 0))
    hbm = pl.BlockSpec(memory_space=pl.ANY)
    return pl.pallas_call(
        body, name="in_proj_bwd", grid=(s // tm, ksteps),
        in_specs=[pl.BlockSpec((tm, tk), lambda i, k: (i, k)), pl.BlockSpec((tk, D_MODEL), lambda i, k: (k, 0)),
                  hbm, hbm, row, row, hbm],
        out_specs=[pl.BlockSpec((tm, D_MODEL), lambda i, k: (i, 0)), pl.BlockSpec((8, D_MODEL), lambda i, k: (0, 0)),
                   hbm],
        out_shape=[SDS((s, D_MODEL), F32), SDS((8, D_MODEL), F32), SDS((3,) + pair.shape[1:], pair.dtype)],
        scratch_shapes=[pltpu.VMEM((tm, D_MODEL), F32), pltpu.VMEM((tm, D_MODEL), F32),
                        pltpu.SemaphoreType.DMA((2,)), *_scatter_scratch()],
        compiler_params=_params("arbitrary", "arbitrary"),
    )(dproj, wt, x, dx1, scale, norm_g, pair)


def _pair_sum(core, grad, got):
    _, m, n = got.shape

    def body(core_ref, a_ref, b_ref, out_ref):
        out_ref[...] = (a_ref[...].astype(F32) + b_ref[...].astype(F32)).astype(BF16)

    blk = pl.BlockSpec((1, m, n), lambda q, core_ref: (q, 0, 0))
    return pl.pallas_call(
        body, name=f"pair_sum_{m}",
        grid_spec=pltpu.PrefetchScalarGridSpec(
            num_scalar_prefetch=1, grid=(4,),
            in_specs=[pl.BlockSpec((1, m, n), lambda q, core_ref: (2 * q + core_ref[0], 0, 0)), blk], out_specs=blk),
        out_shape=SDS(got.shape, BF16), compiler_params=_params("parallel"),
    )(core, grad, got)


def _sum_chips(own_ref, parts_ref):
    return ((own_ref[0].astype(F32) + parts_ref[0].astype(F32)) + parts_ref[1].astype(F32)) + parts_ref[2].astype(F32)


def _adam_rows(name, chip, pair, parts, w, m, v):
    rows = w.shape[0]
    tr = rows // 4

    def body(chip_ref, own_ref, p_ref, w_ref, m_ref, v_ref, g_ref, d_ref, nm_ref, nv_ref):
        g = _sum_chips(own_ref, p_ref)
        g_ref[...] = g
        d_ref[...], nm_ref[...], nv_ref[...] = _adamw(w_ref[...], g, m_ref[...], v_ref[...])

    blk = pl.BlockSpec((tr, D_MODEL), lambda j, chip_ref: (j, 0))
    return pl.pallas_call(
        body, name=name,
        grid_spec=pltpu.PrefetchScalarGridSpec(
            num_scalar_prefetch=1, grid=(rows // tr,),
            in_specs=[pl.BlockSpec((1, tr, D_MODEL), lambda j, chip_ref: (chip_ref[0], j, 0)),
                      pl.BlockSpec((3, tr, D_MODEL), lambda j, chip_ref: (0, j, 0)), blk, blk, blk],
            out_specs=[blk] * 4),
        out_shape=[SDS(w.shape, F32)] * 4, compiler_params=_params("parallel"),
    )(chip, pair, parts, w, m, v)


def _adam_ada(name, cact, dmod, w, m, v):
    n = w.shape[1]
    tr = 512

    def body(c_ref, dm_ref, w_ref, m_ref, v_ref, g_ref, d_ref, nm_ref, nv_ref):
        pad_c = jnp.concatenate([c_ref[...], jnp.zeros_like(c_ref)], axis=0).astype(BF16)
        pad_d = jnp.concatenate([dm_ref[...], jnp.zeros_like(dm_ref)], axis=0).astype(BF16)
        g = lax.dot_general(pad_c, pad_d, TN, preferred_element_type=F32)
        g_ref[...] = g
        d_ref[...], nm_ref[...], nv_ref[...] = _adamw(w_ref[...], g, m_ref[...], v_ref[...])

    blk = pl.BlockSpec((tr, n), lambda j: (j, 0))
    return pl.pallas_call(
        body, name=name, grid=(D_MODEL // tr,),
        in_specs=[pl.BlockSpec((N_DEV, tr), lambda j: (0, j)), pl.BlockSpec((N_DEV, n), lambda j: (0, 0)),
                  blk, blk, blk],
        out_specs=[blk] * 4, out_shape=[SDS(w.shape, F32)] * 4,
        compiler_params=_params("parallel"),
    )(cact, dmod, w, m, v)


def _adam_small(parts, w, m, v):
    def body(p_ref, w_ref, m_ref, v_ref, g_ref, d_ref, nm_ref, nv_ref):
        g = p_ref[0]
        for j in range(1, N_DEV):
            g = g + p_ref[j]
        g_ref[...] = g
        d_ref[...], nm_ref[...], nv_ref[...] = _adamw(w_ref[...], g, m_ref[...], v_ref[...])

    vmem = pl.BlockSpec(memory_space=pltpu.VMEM)
    return pl.pallas_call(
        body, name="adam_small", in_specs=[vmem] * 4, out_specs=[vmem] * 4,
        out_shape=[SDS(w.shape, F32)] * 4, compiler_params=_params(),
    )(parts, w, m, v)


def _pack_small(w_sp, norm_g, ln_g, ln_b, b_sp, sinks, fng, b_ada, b_ada_f):
    row_bsp = jnp.concatenate([b_sp.reshape(1, 1024), sinks.reshape(1, 16), jnp.zeros((1, 1008), F32)], axis=1)
    return jnp.concatenate([
        w_sp.reshape(64, D_MODEL), norm_g.reshape(1, D_MODEL),
        jnp.concatenate([ln_g.reshape(1, D_A), ln_b.reshape(1, D_A)], axis=1), row_bsp, fng.reshape(1, D_MODEL),
        b_ada.reshape(3, D_MODEL), b_ada_f.reshape(2, D_MODEL),
        jnp.zeros((SMALL_ROWS - 73, D_MODEL), F32)], axis=0)


def _unpack_small(p):
    return dict(
        w_spatial=p[ROW_WSP:ROW_WSP + 64].reshape(1, A_GROUPS, CHUNK, CHUNK),
        norm_g=p[ROW_NG].reshape(1, D_MODEL),
        ln_v_g=p[ROW_LN, :D_A].reshape(1, D_A), ln_v_b=p[ROW_LN, D_A:].reshape(1, D_A),
        b_spatial=p[ROW_BSP, :1024].reshape(1, A_GROUPS, CHUNK), sinks=p[ROW_BSP, 1024:1040].reshape(1, 16),
        final_norm_g=p[ROW_FNG].reshape(D_MODEL),
        b_ada=p[ROW_BADA:ROW_BADA + 3].reshape(1, 3 * D_MODEL), b_ada_final=p[ROW_BADAF:ROW_BADAF + 2].reshape(2 * D_MODEL),
    )


def kernel(x, c, w_ada, b_ada, norm_g, w_in, ln_v_g, ln_v_b, w_spatial, b_spatial, sinks, w_out, w_ada_final, b_ada_final, final_norm_g, loss_target, m_w_ada, m_b_ada, m_norm_g, m_w_in, m_ln_v_g, m_ln_v_b, m_w_spatial, m_b_spatial, m_sinks, m_w_out, m_w_ada_final, m_b_ada_final, m_final_norm_g, v_w_ada, v_b_ada, v_norm_g, v_w_in, v_ln_v_g, v_ln_v_b, v_w_spatial, v_b_spatial, v_sinks, v_w_out, v_w_ada_final, v_b_ada_final, v_final_norm_g):
    seq = x.shape[1]
    me = 4 * lax.axis_index("x") + 2 * lax.axis_index("y") + lax.axis_index("c")
    x2, tgt = x[0], loss_target[0]
    fng = final_norm_g.reshape(1, D_MODEL)

    n_ada, n_ada_f = w_ada.shape[2], w_ada_final.shape[1]
    cact, mod, mod_f = _ada_exchange(c, w_ada[0], b_ada.reshape(N_DEV, n_ada), w_ada_final,
                                     b_ada_final.reshape(N_DEV, n_ada_f))
    cact = cact.reshape(N_DEV, D_MODEL)
    mod, mod_f = mod.reshape(1, 3 * D_MODEL), mod_f.reshape(1, 2 * D_MODEL)
    shift, scale, gate = mod[:, :D_MODEL], mod[:, D_MODEL:2 * D_MODEL], mod[:, 2 * D_MODEL:]
    shift_f, scale_f = mod_f[:, :D_MODEL], mod_f[:, D_MODEL:]

    wt_f32, m_wt, v_wt = (jnp.swapaxes(a, 1, 2)[0] for a in (w_in, m_w_in, v_w_in))
    wt_shard, wo_shard = _prep_weights(wt_f32, w_out[0])
    wt, wo = _all_gather("gather_weights", [wt_shard, wo_shard], pl.ANY)
    wt, wo = wt.reshape(D_IN, D_MODEL), wo.reshape(D_MODEL, D_MODEL)

    tabs = _rope_tables(seq)
    sinks_v = sinks.reshape(16)
    h, proj = _in_proj(x2, shift, scale, norm_g, wt)
    y = _mixer_fwd(proj, tabs, ln_v_g, ln_v_b, w_spatial[0], b_spatial[0], sinks_v)
    dx1, do, dy, sums_o = _out_proj_loss(y, x2, tgt, wo, gate, shift_f, scale_f, fng)
    loss = lax.psum(0.5 * jnp.sum(sums_o[4]) / D_MODEL, ("x", "y", "c"))

    core = lax.axis_index("c").reshape(1)
    chip = (2 * lax.axis_index("x") + lax.axis_index("y")).reshape(1)
    g_wo = _wgrad("wgrad_out", y, do, 1024).reshape(N_DEV, D_MODEL // N_DEV, D_MODEL)
    pair_out = _pair_sum(core, g_wo, _rs_pair("rs_pair_out", g_wo))
    dproj, d_ln, d_wsp, d_bsp, d_sinks, parts_out = _mixer_bwd(
        proj, dy, tabs, ln_v_g, ln_v_b, w_spatial[0], b_spatial[0], sinks_v, pair_out)
    g_wt = _wgrad("wgrad_in", dproj, h, 1408).reshape(N_DEV, D_IN // N_DEV, D_MODEL)
    pair_in = _pair_sum(core, g_wt, _rs_pair("rs_pair_in", g_wt))
    grad_x, sums_i, parts_in = _in_proj_bwd(dproj, wt, x2, dx1, scale, norm_g, pair_in)
    wt_leaves = [jnp.swapaxes(a[None], 1, 2) for a in _adam_rows("adam_w_in", chip, pair_in, parts_in, wt_f32, m_wt, v_wt)]
    g_w_in, d_w_in, nm_w_in, nv_w_in = wt_leaves
    g_w_out, d_w_out, nm_w_out, nv_w_out = (
        a[None] for a in _adam_rows("adam_w_out", chip, pair_out, parts_out, w_out[0], m_w_out[0], v_w_out[0]))

    dmod = jnp.concatenate([sums_i[0], sums_i[1], sums_o[0]])
    dmod_f = jnp.concatenate([sums_o[1], sums_o[2]])
    small = _pack_small(d_wsp, sums_i[2], d_ln[0], d_ln[1], d_bsp, d_sinks[:, 0], sums_o[3], dmod, dmod_f)
    (small_all,) = _all_gather("gather_small", [small], pltpu.VMEM)
    packed = [_pack_small(*t) for t in (
        (w_spatial, norm_g, ln_v_g, ln_v_b, b_spatial, sinks, final_norm_g, b_ada, b_ada_final),
        (m_w_spatial, m_norm_g, m_ln_v_g, m_ln_v_b, m_b_spatial, m_sinks, m_final_norm_g, m_b_ada, m_b_ada_final),
        (v_w_spatial, v_norm_g, v_ln_v_g, v_ln_v_b, v_b_spatial, v_sinks, v_final_norm_g, v_b_ada, v_b_ada_final))]
    g_s, d_s, nm_s, nv_s = [_unpack_small(p) for p in _adam_small(small_all, *packed)]

    dmod_all = small_all[:, ROW_BADA:ROW_BADA + 3].reshape(N_DEV, 3 * D_MODEL)
    dmod_f_all = small_all[:, ROW_BADAF:ROW_BADAF + 2].reshape(N_DEV, 2 * D_MODEL)
    dmod_mine = lax.dynamic_slice_in_dim(dmod_all, me * n_ada, n_ada, axis=1)
    dmod_f_mine = lax.dynamic_slice_in_dim(dmod_f_all, me * n_ada_f, n_ada_f, axis=1)
    ada = _adam_ada("adam_w_ada", cact, dmod_mine, w_ada[0], m_w_ada[0], v_w_ada[0])
    ada_f = _adam_ada("adam_w_ada_final", cact, dmod_f_mine, w_ada_final, m_w_ada_final, v_w_ada_final)

    def leaves(k):
        small_k = (g_s, d_s, nm_s, nv_s)[k]
        return (ada[k][None], small_k["b_ada"], small_k["norm_g"], (g_w_in, d_w_in, nm_w_in, nv_w_in)[k],
                small_k["ln_v_g"], small_k["ln_v_b"], small_k["w_spatial"], small_k["b_spatial"], small_k["sinks"],
                (g_w_out, d_w_out, nm_w_out, nv_w_out)[k], ada_f[k], small_k["b_ada_final"],
                small_k["final_norm_g"])

    return (loss, grad_x[None], *leaves(0), *leaves(1), *leaves(2), *leaves(3))
```

```python
import functools

import jax
import jax.numpy as jnp
from jax import lax
from jax.experimental import pallas as pl
from jax.experimental.pallas import tpu as pltpu

D_MODEL = 2048
D_IN = 5632
D_A = 1024
CHUNK = 128
A_GROUPS = 8
HEAD_DIM = 64
N_KV_HEADS = 4
N_DEV = 8
ROPE_THETA = 10000.0
NORM_EPS = 1e-5
ATTN_SCALE = HEAD_DIM ** -0.5

ADAM_LR = 0.001
ADAM_B1 = 0.9
ADAM_B2 = 0.999
ADAM_EPS = 1e-08
ADAM_WD = 0.01
ADAM_STEP = 10

OFF_U, OFF_VA, OFF_ZA, OFF_Q, OFF_K, OFF_V, OFF_ZB = 0, 1024, 2048, 3072, 4096, 4352, 4608

SMALL_ROWS = 80
ROW_WSP, ROW_NG, ROW_LN, ROW_BSP, ROW_FNG, ROW_BADA, ROW_BADAF = 0, 64, 65, 66, 67, 68, 71

V7X_VMEM_LIMIT_BYTES = 56 * 1024 * 1024

F32 = jnp.float32
BF16 = jnp.bfloat16
MESH = pl.DeviceIdType.MESH
SDS = jax.ShapeDtypeStruct
NT = (((1,), (1,)), ((), ()))
TN = (((0,), (0,)), ((), ()))


def _params(*semantics):
    return pltpu.CompilerParams(dimension_semantics=semantics or None, vmem_limit_bytes=V7X_VMEM_LIMIT_BYTES)


def _mesh_pos():
    return lax.axis_index("x"), lax.axis_index("y"), lax.axis_index("c")


def _sigmoid(z):
    return 1.0 / (1.0 + jnp.exp(-z))


def _adamw(w, g, m, v):
    m = ADAM_B1 * m + (1.0 - ADAM_B1) * g
    v = ADAM_B2 * v + (1.0 - ADAM_B2) * (g * g)
    m_hat = m / (1.0 - ADAM_B1 ** ADAM_STEP)
    v_hat = v / (1.0 - ADAM_B2 ** ADAM_STEP)
    delta = -ADAM_LR * (m_hat / (jnp.sqrt(v_hat) + ADAM_EPS) + ADAM_WD * w)
    return delta, m, v


def _all_gather(name, blocks, memory_space):
    n_arr = len(blocks)

    def body(*refs):
        ins, outs = refs[:n_arr], refs[n_arr:2 * n_arr]
        send_sems, recv_sems, local_sems = refs[2 * n_arr:]
        x, y, c = _mesh_pos()
        me, sibling = (x, y, c), (x, y, 1 - c)
        chips = [(1 - x, y), (x, 1 - y), (1 - x, 1 - y)]

        def slot(p):
            return 4 * p[0] + 2 * p[1] + p[2]

        def copy(a, k, block, to, src=None):
            dst = outs[a].at[slot(block)]
            return pltpu.make_async_remote_copy(
                src_ref=dst if src is None else src, dst_ref=dst,
                send_sem=send_sems.at[a, k], recv_sem=recv_sems.at[a, k],
                device_id=to, device_id_type=MESH)

        mine = [pltpu.make_async_copy(ins[a], outs[a].at[slot(me)], local_sems.at[a]) for a in range(n_arr)]
        for cp in mine:
            cp.start()
        first = []
        for a in range(n_arr):
            first.append(copy(a, 0, me, sibling, src=ins[a]))
            first += [copy(a, 1 + j, me, (*chip, c), src=ins[a]) for j, chip in enumerate(chips)]
        for cp in first:
            cp.start()
        passed = []
        for j, chip in enumerate(chips):
            for a in range(n_arr):
                copy(a, 1 + j, (*chip, c), me).wait_recv()
                fwd = copy(a, 4 + j, (*chip, c), sibling)
                fwd.start()
                passed.append(fwd)
        for a in range(n_arr):
            copy(a, 0, sibling, me).wait_recv()
            for j, chip in enumerate(chips):
                copy(a, 4 + j, (*chip, 1 - c), me).wait_recv()
        for cp in first + passed:
            cp.wait_send()
        for cp in mine:
            cp.wait()

    spec = pl.BlockSpec(memory_space=memory_space)
    return pl.pallas_call(
        body, name=name,
        out_shape=[SDS((N_DEV,) + b.shape, b.dtype) for b in blocks],
        in_specs=[spec] * n_arr, out_specs=[spec] * n_arr,
        scratch_shapes=[pltpu.SemaphoreType.DMA((n_arr, 7)), pltpu.SemaphoreType.DMA((n_arr, 7)),
                        pltpu.SemaphoreType.DMA((n_arr,))],
        compiler_params=_params(),
    )(*blocks)


def _ada_exchange(c, w_ada, b_ada8, w_ada_f, b_ada_f8):
    n1, n2 = w_ada.shape[1], w_ada_f.shape[1]

    def body(c_ref, w1_ref, b1_ref, w2_ref, b2_ref, cact_ref, mod_ref, modf_ref,
             cact_buf, res1, res2, send1, send2, sems_s, sems_r):
        x, y, c_pos = _mesh_pos()
        me = 4 * x + 2 * y + c_pos
        flips = [(k >> 2 & 1, k >> 1 & 1, k & 1) for k in range(1, N_DEV)]

        def peer(f):
            return (1 - x if f[0] else x, 1 - y if f[1] else y, 1 - c_pos if f[2] else c_pos)

        cv = c_ref[...]
        cact = cv * _sigmoid(cv)
        cact_buf[...] = cact
        cact_ref[me] = cact

        def rdma(phase, k, src, dst, f):
            return pltpu.make_async_remote_copy(src_ref=src, dst_ref=dst, send_sem=sems_s.at[phase, k],
                                                recv_sem=sems_r.at[phase, k], device_id=peer(f), device_id_type=MESH)

        gather = [rdma(0, k, cact_buf, cact_ref.at[me], f) for k, f in enumerate(flips)]
        for cp in gather:
            cp.start()
        for cp in gather:
            cp.wait_recv()
        for cp in gather:
            cp.wait_send()

        rid = lax.broadcasted_iota(jnp.int32, (N_DEV, D_MODEL), 0)
        rows = jnp.zeros((N_DEV, D_MODEL), F32)
        for j in range(N_DEV):
            rows = jnp.where(rid == j, jnp.broadcast_to(cact_ref[j], (N_DEV, D_MODEL)), rows)
        rows = rows.astype(BF16)
        res1[...] = jnp.dot(rows, w1_ref[...].astype(BF16), preferred_element_type=F32) + b1_ref[pl.ds(me, 1), :]
        res2[...] = jnp.dot(rows, w2_ref[...].astype(BF16), preferred_element_type=F32) + b2_ref[pl.ds(me, 1), :]
        for j in range(N_DEV):
            send1[j] = res1[pl.ds(j, 1), :]
            send2[j] = res2[pl.ds(j, 1), :]
        mod_ref[me] = send1[me]
        modf_ref[me] = send2[me]
        scatter = []
        for k, f in enumerate(flips):
            to = me ^ (k + 1)
            scatter.append(rdma(1, k, send1.at[to], mod_ref.at[me], f))
            scatter.append(rdma(2, k, send2.at[to], modf_ref.at[me], f))
        for cp in scatter:
            cp.start()
        for cp in scatter:
            cp.wait_recv()
        for cp in scatter:
            cp.wait_send()

    vmem = pl.BlockSpec(memory_space=pltpu.VMEM)
    return pl.pallas_call(
        body, name="ada_exchange",
        out_shape=[SDS((N_DEV, 1, D_MODEL), F32), SDS((N_DEV, 1, n1), F32), SDS((N_DEV, 1, n2), F32)],
        in_specs=[vmem] * 5, out_specs=[vmem] * 3,
        scratch_shapes=[pltpu.VMEM((1, D_MODEL), F32), pltpu.VMEM((N_DEV, n1), F32), pltpu.VMEM((N_DEV, n2), F32),
                        pltpu.VMEM((N_DEV, 1, n1), F32), pltpu.VMEM((N_DEV, 1, n2), F32),
                        pltpu.SemaphoreType.DMA((3, 7)), pltpu.SemaphoreType.DMA((3, 7))],
        compiler_params=_params(),
    )(c, w_ada, b_ada8, w_ada_f, b_ada_f8)


def _rs_pair(name, grad):
    def body(g_ref, got_ref, send_sems, recv_sems):
        x, y, c = _mesh_pos()
        copies = [pltpu.make_async_remote_copy(
            src_ref=g_ref.at[2 * q + 1 - c], dst_ref=got_ref.at[q], send_sem=send_sems.at[q], recv_sem=recv_sems.at[q],
            device_id=(x, y, 1 - c), device_id_type=MESH) for q in range(4)]
        for cp in copies:
            cp.start()
        for cp in copies:
            cp.wait_recv()
        for cp in copies:
            cp.wait_send()

    hbm = pl.BlockSpec(memory_space=pl.ANY)
    return pl.pallas_call(
        body, name=name, out_shape=SDS((4,) + grad.shape[1:], grad.dtype), in_specs=[hbm], out_specs=hbm,
        scratch_shapes=[pltpu.SemaphoreType.DMA((4,)), pltpu.SemaphoreType.DMA((4,))],
        compiler_params=_params(),
    )(grad)


def _chip_scatter(pair_ref, parts_ref, send_sems, recv_sems):
    x, y, c = _mesh_pos()
    chips = [(1 - x, y), (x, 1 - y), (1 - x, 1 - y)]
    return [pltpu.make_async_remote_copy(
        src_ref=pair_ref.at[2 * cx + cy], dst_ref=parts_ref.at[j], send_sem=send_sems.at[j], recv_sem=recv_sems.at[j],
        device_id=(cx, cy, c), device_id_type=MESH) for j, (cx, cy) in enumerate(chips)]


def _scatter_scratch():
    return [pltpu.SemaphoreType.DMA((3,)), pltpu.SemaphoreType.DMA((3,))]


def _prep_weights(me, wt, w_out):
    steps = 4

    def body(me_ref, wt_ref, wo_ref, wtb_ref, wob_ref):
        wtb_ref[...] = wt_ref[...].astype(BF16)
        wob_ref[...] = wo_ref[...].astype(BF16)

    def rows(a, mine):
        blk = (a.shape[0] // steps, a.shape[1])
        return pl.BlockSpec(blk, (lambda i, me_ref: (steps * me_ref[0] + i, 0)) if mine else (lambda i, me_ref: (i, 0)))

    return pl.pallas_call(
        body, name="prep_weights",
        grid_spec=pltpu.PrefetchScalarGridSpec(
            num_scalar_prefetch=1, grid=(steps,),
            in_specs=[rows(wt, False), rows(w_out, False)], out_specs=[rows(wt, True), rows(w_out, True)]),
        out_shape=[SDS((N_DEV * wt.shape[0], D_MODEL), BF16), SDS((N_DEV * w_out.shape[0], D_MODEL), BF16)],
        compiler_params=_params("parallel"),
    )(me, wt, w_out)


def _gather_in_proj(order, x, shift, scale, norm_g, wt_all, wo_all):
    s = x.shape[0]
    th = tm = min(512, s)
    nh, ni = s // th, s // tm
    tn = D_IN // 4
    steps = nh + 4 * ni
    block_rows = (D_IN // N_DEV, D_MODEL // N_DEV)

    def body(order_ref, x_ref, shift_ref, scale_ref, g_ref, wt_in, wo_in, h_ref, proj_ref, wt_ref, wo_ref,
             h_scr, w_buf, load_sems, send_sems, recv_sems):
        g = pl.program_id(0)
        mx, my, mc = _mesh_pos()
        me, sibling = (mx, my, mc), (mx, my, 1 - mc)
        chips = [(1 - mx, my), (mx, 1 - my), (1 - mx, 1 - my)]
        gathered = (wt_ref, wo_ref)

        def copy(a, k, block, to):
            n = block_rows[a]
            rows = gathered[a].at[pl.ds(pl.multiple_of((4 * block[0] + 2 * block[1] + block[2]) * n, n), n)]
            return pltpu.make_async_remote_copy(src_ref=rows, dst_ref=rows, send_sem=send_sems.at[a, k],
                                                recv_sem=recv_sems.at[a, k], device_id=to, device_id_type=MESH)

        def tile_load(slot, chip):
            return pltpu.make_async_copy(wt_ref.at[pl.ds(pl.multiple_of(chip * tn, tn), tn)], w_buf.at[slot],
                                         load_sems.at[slot])

        @pl.when(g == 0)
        def _():
            for a in range(2):
                copy(a, 0, me, sibling).start()
                for j, chip in enumerate(chips):
                    copy(a, 1 + j, me, (*chip, mc)).start()

        @pl.when(g < nh)
        def _():
            xv = x_ref[...]
            r = lax.rsqrt(jnp.mean(xv * xv, axis=-1, keepdims=True) + NORM_EPS)
            hb = (((xv * r) * g_ref[...]) * (1.0 + scale_ref[...]) + shift_ref[...]).astype(BF16)
            h_ref[...] = hb
            h_scr[pl.ds(pl.multiple_of(g * th, th), th), :] = hb

        @pl.when(g == nh - 1)
        def _():
            copy(0, 0, sibling, me).wait_recv()
            tile_load(0, order_ref[0]).start()

        @pl.when(g >= nh)
        def _():
            t, i = (g - nh) // ni, (g - nh) % ni

            @pl.when(i == 0)
            def _():
                tile_load(t % 2, order_ref[t]).wait()

            for j, chip in enumerate(chips):
                @pl.when((i == ni - 1) & (t == j))
                def _():
                    copy(0, 1 + j, (*chip, mc), me).wait_recv()
                    copy(0, 4 + j, (*chip, mc), sibling).start()
                    copy(0, 4 + j, (*chip, 1 - mc), me).wait_recv()
                    tile_load((j + 1) % 2, order_ref[j + 1]).start()

            lhs = h_scr[pl.ds(pl.multiple_of(i * tm, tm), tm), :]
            proj_ref[...] = lax.dot_general(lhs, w_buf[t % 2], NT, preferred_element_type=F32).astype(BF16)

        @pl.when(g == steps - 1)
        def _():
            for j, chip in enumerate(chips):
                copy(1, 1 + j, (*chip, mc), me).wait_recv()
                copy(1, 4 + j, (*chip, mc), sibling).start()
            copy(1, 0, sibling, me).wait_recv()
            for j, chip in enumerate(chips):
                copy(1, 4 + j, (*chip, 1 - mc), me).wait_recv()
            for a in range(2):
                for k in range(7):
                    copy(a, k, me, sibling).wait_send()

    def h_tile(g, order_ref):
        return (jnp.minimum(g, nh - 1), 0)

    def proj_tile(g, order_ref):
        mm = jnp.maximum(g - nh, 0)
        return (mm % ni, order_ref[mm // ni])

    row = pl.BlockSpec((1, D_MODEL), lambda g, order_ref: (0, 0))
    hbm = pl.BlockSpec(memory_space=pl.ANY)
    return pl.pallas_call(
        body, name="gather_in_proj",
        grid_spec=pltpu.PrefetchScalarGridSpec(
            num_scalar_prefetch=1, grid=(steps,),
            in_specs=[pl.BlockSpec((th, D_MODEL), h_tile), row, row, row, hbm, hbm],
            out_specs=[pl.BlockSpec((th, D_MODEL), h_tile), pl.BlockSpec((tm, tn), proj_tile), hbm, hbm],
            scratch_shapes=[pltpu.VMEM((s, D_MODEL), BF16), pltpu.VMEM((2, tn, D_MODEL), BF16),
                            pltpu.SemaphoreType.DMA((2,)), pltpu.SemaphoreType.DMA((2, 7)),
                            pltpu.SemaphoreType.DMA((2, 7))]),
        out_shape=[SDS((s, D_MODEL), BF16), SDS((s, D_IN), BF16), SDS(wt_all.shape, BF16), SDS(wo_all.shape, BF16)],
        input_output_aliases={5: 2, 6: 3},
        compiler_params=_params("arbitrary"),
    )(order, x, shift, scale, norm_g, wt_all, wo_all)


def _rope_tables(seq):
    inv_freq = ROPE_THETA ** (-jnp.arange(0, HEAD_DIM, 2, dtype=F32) / HEAD_DIM)
    ang = jnp.arange(seq, dtype=F32)[:, None] * inv_freq[None, :]
    cos, sin, zero = jnp.cos(ang), jnp.sin(ang), jnp.zeros_like(ang)
    return (jnp.concatenate([cos] * 4, axis=1), jnp.concatenate([-sin, zero, -sin, zero], axis=1),
            jnp.concatenate([zero, sin, zero, sin], axis=1))


def _rope(v, cos, sin_lo, sin_hi):
    width = v.shape[1]
    rep = (1, width // 128)
    return (v * jnp.tile(cos, rep) + pltpu.roll(v, width - 32, 1) * jnp.tile(sin_lo, rep)
            + pltpu.roll(v, 32, 1) * jnp.tile(sin_hi, rep))


def _rope_bwd(d, cos, sin_lo, sin_hi):
    width = d.shape[1]
    rep = (1, width // 128)
    return (d * jnp.tile(cos, rep) + pltpu.roll(d * jnp.tile(sin_lo, rep), 32, 1)
            + pltpu.roll(d * jnp.tile(sin_hi, rep), width - 32, 1))


def _layer_norm(v, g, b):
    mu = jnp.mean(v, axis=-1, keepdims=True)
    vc = v - mu
    rstd = lax.rsqrt(jnp.mean(vc * vc, axis=-1, keepdims=True) + NORM_EPS)
    vhat = vc * rstd
    return vhat * g + b, vhat, rstd


def _tril_bf16(w_ref, g):
    t = lax.broadcasted_iota(jnp.int32, (CHUNK, CHUNK), 0)
    tp = lax.broadcasted_iota(jnp.int32, (CHUNK, CHUNK), 1)
    return jnp.where(tp <= t, w_ref[g], 0.0).astype(BF16)


def _bias_columns(b_ref, out_ref):
    for g in range(A_GROUPS):
        out_ref[g] = jnp.broadcast_to(b_ref[pl.ds(g, 1), :], (CHUNK, CHUNK)).T


def _band_mask():
    kj = lax.broadcasted_iota(jnp.int32, (2 * CHUNK, 4 * CHUNK), 0)
    qi = lax.broadcasted_iota(jnp.int32, (2 * CHUNK, 4 * CHUNK), 1) & (CHUNK - 1)
    rel = qi + CHUNK - kj
    return jnp.where((rel >= 0) & (rel < CHUNK), 0.0, -jnp.inf)


def _low_lanes():
    return lax.broadcasted_iota(jnp.int32, (1, 128), 1) < HEAD_DIM


def _stack_heads(pair_a, pair_b):
    lo = _low_lanes()
    return jnp.concatenate([jnp.where(lo, pair_a, 0.0), jnp.where(lo, 0.0, pair_a),
                            jnp.where(lo, pair_b, 0.0), jnp.where(lo, 0.0, pair_b)], axis=0).astype(BF16)


def _heads_to_lanes(per_group):
    rows = [t[:, r * CHUNK:(r + 1) * CHUNK] for t in per_group for r in range(4)]
    return jnp.concatenate(rows, axis=0).T


def _dup_kv_head(band, gk):
    pair = band[:, (gk // 2) * 128:(gk // 2 + 1) * 128]
    lo = _low_lanes()
    one = jnp.where(lo if gk % 2 == 0 else jnp.logical_not(lo), pair, 0.0)
    return (one + pltpu.roll(one, HEAD_DIM, 1)).astype(BF16)


def _fold_kv_head(dup_grad, gk):
    both = dup_grad + pltpu.roll(dup_grad, HEAD_DIM, 1)
    lo = _low_lanes()
    return jnp.where(lo if gk % 2 == 0 else jnp.logical_not(lo), both, 0.0)


def _attn_probs(q_st, k_dup, sink_row, mask, first_block):
    s = lax.dot_general(k_dup, q_st, NT, preferred_element_type=F32) * ATTN_SCALE + mask
    s = jnp.concatenate([jnp.where(first_block, -jnp.inf, s[:CHUNK]), s[CHUNK:]], axis=0)
    m = jnp.maximum(jnp.max(s, axis=0, keepdims=True), sink_row)
    p = jnp.exp(s - m)
    e_sink = jnp.exp(sink_row - m)
    inv = 1.0 / (jnp.sum(p, axis=0, keepdims=True) + e_sink)
    return p * inv, e_sink * inv


def _sink_row(sinks_ref, gk):
    return jnp.concatenate([jnp.full((1, CHUNK), sinks_ref[4 * gk + r], F32) for r in range(4)], axis=1)


def _mixer_specs(nb, rev):
    def blk(i):
        return nb - 1 - i if rev else i

    def prev(i):
        return jnp.maximum(blk(i) - 1, 0)

    tab = pl.BlockSpec((CHUNK, 128), lambda i: (blk(i), 0))
    tab_prev = pl.BlockSpec((CHUNK, 128), lambda i: (prev(i), 0))
    return dict(
        cur=pl.BlockSpec((CHUNK, D_IN), lambda i: (blk(i), 0)),
        prev_kv=pl.BlockSpec((CHUNK, 2 * 256), lambda i: (prev(i), OFF_K // 512)),
        tabs=[tab] * 3 + [tab_prev] * 3,
        vec=pl.BlockSpec((1, D_A), lambda i: (0, 0)),
        wsp=pl.BlockSpec((A_GROUPS, CHUNK, CHUNK), lambda i: (0, 0, 0)),
        bsp=pl.BlockSpec((A_GROUPS, CHUNK), lambda i: (0, 0)),
        smem=pl.BlockSpec(memory_space=pltpu.SMEM),
        blk=blk,
    )


def _mixer_fwd(proj, tabs, ln_g, ln_b, w_sp, b_sp, sinks):
    s = proj.shape[0]
    nb = s // CHUNK
    sp = _mixer_specs(nb, rev=False)

    def body(cur_ref, pkv_ref, c_ref, s1_ref, s2_ref, cp_ref, s1p_ref, s2p_ref, lg_ref, lb_ref, w_ref, b_ref,
             sinks_ref, y_ref, bcol, mask):
        i = pl.program_id(0)

        @pl.when(i == 0)
        def _():
            _bias_columns(b_ref, bcol)
            mask[...] = _band_mask()

        vln, _, _ = _layer_norm(cur_ref[:, OFF_VA:OFF_ZA].astype(F32), lg_ref[...], lb_ref[...])
        vln = vln.astype(BF16)
        for g in range(A_GROUPS):
            cols = slice(g * 128, (g + 1) * 128)
            sg = jnp.dot(_tril_bf16(w_ref, g), vln[:, cols], preferred_element_type=F32) + bcol[g]
            u = cur_ref[:, OFF_U + g * 128:OFF_U + (g + 1) * 128].astype(F32)
            z = cur_ref[:, OFF_ZA + g * 128:OFF_ZA + (g + 1) * 128].astype(F32)
            y_ref[:, cols] = (u * sg * (z * _sigmoid(z))).astype(BF16)

        cur_t = (c_ref[...], s1_ref[...], s2_ref[...])
        prev_t = (cp_ref[...], s1p_ref[...], s2p_ref[...])
        qr = _rope(cur_ref[:, OFF_Q:OFF_K].astype(F32), *cur_t)
        kr = jnp.concatenate([_rope(pkv_ref[:, 0:256].astype(F32), *prev_t),
                              _rope(cur_ref[:, OFF_K:OFF_V].astype(F32), *cur_t)], axis=0)
        v_t = jnp.concatenate([pkv_ref[:, 256:512], cur_ref[:, OFF_V:OFF_ZB]], axis=0).astype(F32).T.astype(BF16)
        outs = []
        for gk in range(N_KV_HEADS):
            q_st = _stack_heads(qr[:, (2 * gk) * 128:(2 * gk + 1) * 128], qr[:, (2 * gk + 1) * 128:(2 * gk + 2) * 128])
            probs, _ = _attn_probs(q_st, _dup_kv_head(kr, gk), _sink_row(sinks_ref, gk), mask[...], i == 0)
            outs.append(jnp.dot(v_t[gk * HEAD_DIM:(gk + 1) * HEAD_DIM], probs.astype(BF16),
                                preferred_element_type=F32))
        zb = cur_ref[:, OFF_ZB:D_IN].astype(F32)
        y_ref[:, D_A:D_MODEL] = (_heads_to_lanes(outs) * (zb * _sigmoid(zb))).astype(BF16)

    return pl.pallas_call(
        body, name="mixer_fwd", grid=(nb,),
        in_specs=[sp["cur"], sp["prev_kv"], *sp["tabs"], sp["vec"], sp["vec"], sp["wsp"], sp["bsp"], sp["smem"]],
        out_specs=pl.BlockSpec((CHUNK, D_MODEL), lambda i: (i, 0)),
        out_shape=SDS((s, D_MODEL), BF16),
        scratch_shapes=[pltpu.VMEM((A_GROUPS, CHUNK, CHUNK), F32), pltpu.VMEM((2 * CHUNK, 4 * CHUNK), F32)],
        compiler_params=_params("arbitrary"),
    )(proj, proj, *tabs, *tabs, ln_g, ln_b, w_sp, b_sp, sinks)


def _out_proj_loss(y, x, target, wo, gate, shift_f, scale_f, fng):
    s = y.shape[0]
    tm = 256

    def body(y_ref, x_ref, t_ref, wo_ref, gate_ref, sh_ref, sc_ref, g_ref, dx1_ref, do_ref, dy_ref, sums_ref):
        @pl.when(pl.program_id(0) == 0)
        def _():
            sums_ref[...] = jnp.zeros_like(sums_ref)

        o = jnp.dot(y_ref[...], wo_ref[...], preferred_element_type=F32)
        gate = gate_ref[...]
        x1 = x_ref[...] + gate * o
        rf = lax.rsqrt(jnp.mean(x1 * x1, axis=-1, keepdims=True) + NORM_EPS)
        x1n = x1 * rf
        hn = x1n * g_ref[...]
        one_sc = 1.0 + sc_ref[...]
        diff = hn * one_sc + sh_ref[...] - t_ref[...]
        dout = diff * (1.0 / D_MODEL)
        d_hn = dout * one_sc
        d_x1n = d_hn * g_ref[...]
        dx1 = rf * (d_x1n - x1n * jnp.mean(d_x1n * x1n, axis=-1, keepdims=True))
        dx1_ref[...] = dx1
        do = (dx1 * gate).astype(BF16)
        do_ref[...] = do
        dy_ref[...] = lax.dot_general(do, wo_ref[...], NT, preferred_element_type=F32).astype(BF16)

        def rowsum(v):
            return jnp.sum(v, axis=0, keepdims=True)

        sums_ref[0:1, :] += rowsum(dx1 * o)
        sums_ref[1:2, :] += rowsum(dout)
        sums_ref[2:3, :] += rowsum(dout * hn)
        sums_ref[3:4, :] += rowsum(d_hn * x1n)
        sums_ref[4:5, :] += rowsum(diff * diff)

    tile = pl.BlockSpec((tm, D_MODEL), lambda i: (i, 0))
    row = pl.BlockSpec((1, D_MODEL), lambda i: (0, 0))
    return pl.pallas_call(
        body, name="out_proj_loss", grid=(s // tm,),
        in_specs=[tile, tile, tile, pl.BlockSpec((D_MODEL, D_MODEL), lambda i: (0, 0)), row, row, row, row],
        out_specs=[tile, tile, tile, pl.BlockSpec((8, D_MODEL), lambda i: (0, 0))],
        out_shape=[SDS((s, D_MODEL), F32), SDS((s, D_MODEL), BF16), SDS((s, D_MODEL), BF16), SDS((8, D_MODEL), F32)],
        compiler_params=_params("arbitrary"),
    )(y, x, target, wo, gate, shift_f, scale_f, fng)


def _mixer_bwd(proj, dy, tabs, ln_g, ln_b, w_sp, b_sp, sinks, pair):
    s = proj.shape[0]
    nb = s // CHUNK
    sp = _mixer_specs(nb, rev=True)

    def body(cur_ref, pkv_ref, dy_ref, c_ref, s1_ref, s2_ref, cp_ref, s1p_ref, s2p_ref, lg_ref, lb_ref, w_ref, b_ref,
             sinks_ref, pair_ref, dproj_ref, dln_ref, dw_ref, db_ref, dsink_ref, parts_ref, bcol, dbcol, carry, mask,
             send_sems, recv_sems):
        i = pl.program_id(0)
        block = nb - 1 - i

        @pl.when(i == 0)
        def _():
            for cp in _chip_scatter(pair_ref, parts_ref, send_sems, recv_sems):
                cp.start()
            _bias_columns(b_ref, bcol)
            mask[...] = _band_mask()
            dbcol[...] = jnp.zeros_like(dbcol)
            carry[...] = jnp.zeros_like(carry)
            dln_ref[...] = jnp.zeros_like(dln_ref)
            dw_ref[...] = jnp.zeros_like(dw_ref)
            dsink_ref[...] = jnp.zeros_like(dsink_ref)

        vln, vhat, rstd = _layer_norm(cur_ref[:, OFF_VA:OFF_ZA].astype(F32), lg_ref[...], lb_ref[...])
        vln = vln.astype(BF16)
        d_vln = []
        for g in range(A_GROUPS):
            cols = slice(g * 128, (g + 1) * 128)
            w_g = _tril_bf16(w_ref, g)
            sg = jnp.dot(w_g, vln[:, cols], preferred_element_type=F32) + bcol[g]
            u = cur_ref[:, OFF_U + g * 128:OFF_U + (g + 1) * 128].astype(F32)
            z = cur_ref[:, OFF_ZA + g * 128:OFF_ZA + (g + 1) * 128].astype(F32)
            dya = dy_ref[:, cols].astype(F32)
            sig = _sigmoid(z)
            d_ya = dya * (z * sig)
            dproj_ref[:, OFF_ZA + g * 128:OFF_ZA + (g + 1) * 128] = (
                dya * (u * sg) * (sig * (1.0 + z * (1.0 - sig)))).astype(BF16)
            dproj_ref[:, OFF_U + g * 128:OFF_U + (g + 1) * 128] = (d_ya * sg).astype(BF16)
            d_s = d_ya * u
            dbcol[g] += d_s
            d_sb = d_s.astype(BF16)
            dw_ref[g] += lax.dot_general(d_sb, vln[:, cols], NT, preferred_element_type=F32)
            d_vln.append(lax.dot_general(w_g, d_sb, TN, preferred_element_type=F32))
        d_vln = jnp.concatenate(d_vln, axis=1)
        dln_ref[0:1, :] += jnp.sum(d_vln * vhat, axis=0, keepdims=True)
        dln_ref[1:2, :] += jnp.sum(d_vln, axis=0, keepdims=True)
        d_vhat = d_vln * lg_ref[...]
        d_va = rstd * (d_vhat - jnp.mean(d_vhat, axis=-1, keepdims=True)
                       - vhat * jnp.mean(d_vhat * vhat, axis=-1, keepdims=True))
        dproj_ref[:, OFF_VA:OFF_ZA] = d_va.astype(BF16)

        cur_t = (c_ref[...], s1_ref[...], s2_ref[...])
        prev_t = (cp_ref[...], s1p_ref[...], s2p_ref[...])
        band_t = tuple(jnp.concatenate([p, c], axis=0) for p, c in zip(prev_t, cur_t))
        qr = _rope(cur_ref[:, OFF_Q:OFF_K].astype(F32), *cur_t)
        kr = jnp.concatenate([_rope(pkv_ref[:, 0:256].astype(F32), *prev_t),
                              _rope(cur_ref[:, OFF_K:OFF_V].astype(F32), *cur_t)], axis=0)
        vb = jnp.concatenate([pkv_ref[:, 256:512], cur_ref[:, OFF_V:OFF_ZB]], axis=0).astype(F32)
        k_t, v_t = kr.T.astype(BF16), vb.T.astype(BF16)
        zb = cur_ref[:, OFF_ZB:D_IN].astype(F32)
        dyb = dy_ref[:, D_A:D_MODEL].astype(F32)
        sig = _sigmoid(zb)
        d_yb = dyb * (zb * sig)
        outs, dqs = [], []
        dk_pairs = [jnp.zeros((2 * CHUNK, 128), F32) for _ in range(2)]
        dv_pairs = [jnp.zeros((2 * CHUNK, 128), F32) for _ in range(2)]
        for gk in range(N_KV_HEADS):
            heads = slice(gk * HEAD_DIM, (gk + 1) * HEAD_DIM)
            q_st = _stack_heads(qr[:, (2 * gk) * 128:(2 * gk + 1) * 128], qr[:, (2 * gk + 1) * 128:(2 * gk + 2) * 128])
            k_dup, v_dup = _dup_kv_head(kr, gk), _dup_kv_head(vb, gk)
            probs, p_sink = _attn_probs(q_st, k_dup, _sink_row(sinks_ref, gk), mask[...], block == 0)
            probs_b = probs.astype(BF16)
            outs.append(jnp.dot(v_t[heads], probs_b, preferred_element_type=F32))
            do_st = _stack_heads(d_yb[:, (2 * gk) * 128:(2 * gk + 1) * 128], d_yb[:, (2 * gk + 1) * 128:(2 * gk + 2) * 128])
            dp = lax.dot_general(v_dup, do_st, NT, preferred_element_type=F32)
            delta = jnp.sum(probs * dp, axis=0, keepdims=True)
            ds = (probs * (dp - delta) * ATTN_SCALE).astype(BF16)
            d_sink = -p_sink * delta
            for r in range(4):
                dsink_ref[4 * gk + r:4 * gk + r + 1, :] += jnp.broadcast_to(
                    jnp.sum(d_sink[:, r * CHUNK:(r + 1) * CHUNK], axis=1, keepdims=True), (1, 128))
            dqs.append(jnp.dot(k_t[heads], ds, preferred_element_type=F32))
            dk_pairs[gk // 2] += _fold_kv_head(jnp.dot(ds, q_st, preferred_element_type=F32), gk)
            dv_pairs[gk // 2] += _fold_kv_head(jnp.dot(probs_b, do_st, preferred_element_type=F32), gk)
        dproj_ref[:, OFF_ZB:D_IN] = (dyb * _heads_to_lanes(outs) * (sig * (1.0 + zb * (1.0 - sig)))).astype(BF16)
        dproj_ref[:, OFF_Q:OFF_K] = _rope_bwd(_heads_to_lanes(dqs), *cur_t).astype(BF16)
        dk_band = _rope_bwd(jnp.concatenate(dk_pairs, axis=1), *band_t)
        dv_band = jnp.concatenate(dv_pairs, axis=1)
        dproj_ref[:, OFF_K:OFF_V] = (dk_band[CHUNK:] + carry[:, 0:256]).astype(BF16)
        dproj_ref[:, OFF_V:OFF_ZB] = (dv_band[CHUNK:] + carry[:, 256:512]).astype(BF16)
        carry[:, 0:256] = dk_band[:CHUNK]
        carry[:, 256:512] = dv_band[:CHUNK]

        @pl.when(i == nb - 1)
        def _():
            t = lax.broadcasted_iota(jnp.int32, (CHUNK, CHUNK), 0)
            tp = lax.broadcasted_iota(jnp.int32, (CHUNK, CHUNK), 1)
            for g in range(A_GROUPS):
                dw_ref[g] = jnp.where(tp <= t, dw_ref[g], 0.0)
                db_ref[pl.ds(g, 1), :] = jnp.sum(dbcol[g].T, axis=0, keepdims=True)
            scatter = _chip_scatter(pair_ref, parts_ref, send_sems, recv_sems)
            for cp in scatter:
                cp.wait_recv()
            for cp in scatter:
                cp.wait_send()

    blk = sp["blk"]
    hbm = pl.BlockSpec(memory_space=pl.ANY)
    return pl.pallas_call(
        body, name="mixer_bwd", grid=(nb,),
        in_specs=[sp["cur"], sp["prev_kv"], pl.BlockSpec((CHUNK, D_MODEL), lambda i: (blk(i), 0)), *sp["tabs"],
                  sp["vec"], sp["vec"], sp["wsp"], sp["bsp"], sp["smem"], hbm],
        out_specs=[pl.BlockSpec((CHUNK, D_IN), lambda i: (blk(i), 0)),
                   pl.BlockSpec((8, D_A), lambda i: (0, 0)),
                   pl.BlockSpec((A_GROUPS, CHUNK, CHUNK), lambda i: (0, 0, 0)),
                   pl.BlockSpec((A_GROUPS, CHUNK), lambda i: (0, 0)),
                   pl.BlockSpec((16, 128), lambda i: (0, 0)), hbm],
        out_shape=[SDS((s, D_IN), BF16), SDS((8, D_A), F32), SDS((A_GROUPS, CHUNK, CHUNK), F32),
                   SDS((A_GROUPS, CHUNK), F32), SDS((16, 128), F32), SDS((3,) + pair.shape[1:], pair.dtype)],
        scratch_shapes=[pltpu.VMEM((A_GROUPS, CHUNK, CHUNK), F32), pltpu.VMEM((A_GROUPS, CHUNK, CHUNK), F32),
                        pltpu.VMEM((CHUNK, 512), F32), pltpu.VMEM((2 * CHUNK, 4 * CHUNK), F32), *_scatter_scratch()],
        compiler_params=_params("arbitrary"),
    )(proj, proj, dy, *tabs, *tabs, ln_g, ln_b, w_sp, b_sp, sinks, pair)


def _wgrad(name, a, b, bm):
    s, m = a.shape
    n = b.shape[1]
    bt = 512
    steps = s // bt

    def body(a_ref, b_ref, out_ref, acc):
        t = pl.program_id(1)

        @pl.when(t == 0)
        def _():
            acc[...] = jnp.zeros_like(acc)

        acc[...] += lax.dot_general(a_ref[...], b_ref[...], TN, preferred_element_type=F32)

        @pl.when(t == steps - 1)
        def _():
            out_ref[...] = acc[...].astype(out_ref.dtype)

    return pl.pallas_call(
        body, name=name, grid=(m // bm, steps),
        in_specs=[pl.BlockSpec((bt, bm), lambda i, t: (t, i)), pl.BlockSpec((bt, n), lambda i, t: (t, 0))],
        out_specs=pl.BlockSpec((bm, n), lambda i, t: (i, 0)),
        out_shape=SDS((m, n), BF16),
        scratch_shapes=[pltpu.VMEM((bm, n), F32)],
        compiler_params=_params("parallel", "arbitrary"),
    )(a, b)


def _in_proj_bwd(dproj, wt, x, dx1, scale, norm_g, pair):
    s = x.shape[0]
    tm, tk, tr = min(1024, s), 512, 256
    ksteps = D_IN // tk

    def body(dp_ref, wt_ref, x_hbm, dx1_hbm, sc_ref, g_ref, pair_ref, gx_ref, sums_ref, parts_ref, x_buf, dx1_buf,
             tile_sems, send_sems, recv_sems):
        i, k = pl.program_id(0), pl.program_id(1)

        def tile_copies():
            rows = pl.ds(pl.multiple_of(i * tm, tm), tm)
            return (pltpu.make_async_copy(x_hbm.at[rows], x_buf, tile_sems.at[0]),
                    pltpu.make_async_copy(dx1_hbm.at[rows], dx1_buf, tile_sems.at[1]))

        @pl.when((i == 0) & (k == 0))
        def _():
            for cp in _chip_scatter(pair_ref, parts_ref, send_sems, recv_sems):
                cp.start()
            sums_ref[...] = jnp.zeros_like(sums_ref)

        @pl.when(k == 0)
        def _():
            for cp in tile_copies():
                cp.start()
            gx_ref[...] = jnp.dot(dp_ref[...], wt_ref[...], preferred_element_type=F32)

        @pl.when(k > 0)
        def _():
            gx_ref[...] += jnp.dot(dp_ref[...], wt_ref[...], preferred_element_type=F32)

        @pl.when(k == ksteps - 1)
        def _():
            for cp in tile_copies():
                cp.wait()
            one_sc, g = 1.0 + sc_ref[...], g_ref[...]

            def chunk(j, sums):
                rows = pl.ds(pl.multiple_of(j * tr, tr), tr)
                dh, xv = gx_ref[rows, :], x_buf[rows, :]
                r = lax.rsqrt(jnp.mean(xv * xv, axis=-1, keepdims=True) + NORM_EPS)
                xn = xv * r
                d_hn = dh * one_sc
                d_xn = d_hn * g
                gx_ref[rows, :] = dx1_buf[rows, :] + r * (d_xn - xn * jnp.mean(d_xn * xn, axis=-1, keepdims=True))
                return (sums[0] + jnp.sum(dh, axis=0, keepdims=True),
                        sums[1] + jnp.sum(dh * (xn * g), axis=0, keepdims=True),
                        sums[2] + jnp.sum(d_hn * xn, axis=0, keepdims=True))

            zero = jnp.zeros((1, D_MODEL), F32)
            sums = lax.fori_loop(0, tm // tr, chunk, (zero, zero, zero))
            sums_ref[0:1, :] += sums[0]
            sums_ref[1:2, :] += sums[1]
            sums_ref[2:3, :] += sums[2]

        @pl.when((i == s // tm - 1) & (k == ksteps - 1))
        def _():
            scatter = _chip_scatter(pair_ref, parts_ref, send_sems, recv_sems)
            for cp in scatter:
                cp.wait_recv()
            for cp in scatter:
                cp.wait_send()

    row = pl.BlockSpec((1, D_MODEL), lambda i, k: (0, 0))
    hbm = pl.BlockSpec(memory_space=pl.ANY)
    return pl.pallas_call(
        body, name="in_proj_bwd", grid=(s // tm, ksteps),
        in_specs=[pl.BlockSpec((tm, tk), lambda i, k: (i, k)), pl.BlockSpec((tk, D_MODEL), lambda i, k: (k, 0)),
                  hbm, hbm, row, row, hbm],
        out_specs=[pl.BlockSpec((tm, D_MODEL), lambda i, k: (i, 0)), pl.BlockSpec((8, D_MODEL), lambda i, k: (0, 0)),
                   hbm],
        out_shape=[SDS((s, D_MODEL), F32), SDS((8, D_MODEL), F32), SDS((3,) + pair.shape[1:], pair.dtype)],
        scratch_shapes=[pltpu.VMEM((tm, D_MODEL), F32), pltpu.VMEM((tm, D_MODEL), F32),
                        pltpu.SemaphoreType.DMA((2,)), *_scatter_scratch()],
        compiler_params=_params("arbitrary", "arbitrary"),
    )(dproj, wt, x, dx1, scale, norm_g, pair)


def _pair_sum(core, grad, got):
    _, m, n = got.shape

    def body(core_ref, a_ref, b_ref, out_ref):
        out_ref[...] = (a_ref[...].astype(F32) + b_ref[...].astype(F32)).astype(BF16)

    blk = pl.BlockSpec((1, m, n), lambda q, core_ref: (q, 0, 0))
    return pl.pallas_call(
        body, name=f"pair_sum_{m}",
        grid_spec=pltpu.PrefetchScalarGridSpec(
            num_scalar_prefetch=1, grid=(4,),
            in_specs=[pl.BlockSpec((1, m, n), lambda q, core_ref: (2 * q + core_ref[0], 0, 0)), blk], out_specs=blk),
        out_shape=SDS(got.shape, BF16), compiler_params=_params("parallel"),
    )(core, grad, got)


def _sum_chips(own_ref, parts_ref):
    return ((own_ref[0].astype(F32) + parts_ref[0].astype(F32)) + parts_ref[1].astype(F32)) + parts_ref[2].astype(F32)


def _adam_rows(name, chip, pair, parts, w, m, v):
    rows = w.shape[0]
    tr = rows // 4

    def body(chip_ref, own_ref, p_ref, w_ref, m_ref, v_ref, g_ref, d_ref, nm_ref, nv_ref):
        g = _sum_chips(own_ref, p_ref)
        g_ref[...] = g
        d_ref[...], nm_ref[...], nv_ref[...] = _adamw(w_ref[...], g, m_ref[...], v_ref[...])

    blk = pl.BlockSpec((tr, D_MODEL), lambda j, chip_ref: (j, 0))
    return pl.pallas_call(
        body, name=name,
        grid_spec=pltpu.PrefetchScalarGridSpec(
            num_scalar_prefetch=1, grid=(rows // tr,),
            in_specs=[pl.BlockSpec((1, tr, D_MODEL), lambda j, chip_ref: (chip_ref[0], j, 0)),
                      pl.BlockSpec((3, tr, D_MODEL), lambda j, chip_ref: (0, j, 0)), blk, blk, blk],
            out_specs=[blk] * 4),
        out_shape=[SDS(w.shape, F32)] * 4, compiler_params=_params("parallel"),
    )(chip, pair, parts, w, m, v)


def _adam_ada(name, cact, dmod, w, m, v):
    n = w.shape[1]
    tr = 512

    def body(c_ref, dm_ref, w_ref, m_ref, v_ref, g_ref, d_ref, nm_ref, nv_ref):
        pad_c = jnp.concatenate([c_ref[...], jnp.zeros_like(c_ref)], axis=0).astype(BF16)
        pad_d = jnp.concatenate([dm_ref[...], jnp.zeros_like(dm_ref)], axis=0).astype(BF16)
        g = lax.dot_general(pad_c, pad_d, TN, preferred_element_type=F32)
        g_ref[...] = g
        d_ref[...], nm_ref[...], nv_ref[...] = _adamw(w_ref[...], g, m_ref[...], v_ref[...])

    blk = pl.BlockSpec((tr, n), lambda j: (j, 0))
    return pl.pallas_call(
        body, name=name, grid=(D_MODEL // tr,),
        in_specs=[pl.BlockSpec((N_DEV, tr), lambda j: (0, j)), pl.BlockSpec((N_DEV, n), lambda j: (0, 0)),
                  blk, blk, blk],
        out_specs=[blk] * 4, out_shape=[SDS(w.shape, F32)] * 4,
        compiler_params=_params("parallel"),
    )(cact, dmod, w, m, v)


def _adam_small(parts, w, m, v):
    def body(p_ref, w_ref, m_ref, v_ref, g_ref, d_ref, nm_ref, nv_ref):
        g = p_ref[0]
        for j in range(1, N_DEV):
            g = g + p_ref[j]
        g_ref[...] = g
        d_ref[...], nm_ref[...], nv_ref[...] = _adamw(w_ref[...], g, m_ref[...], v_ref[...])

    vmem = pl.BlockSpec(memory_space=pltpu.VMEM)
    return pl.pallas_call(
        body, name="adam_small", in_specs=[vmem] * 4, out_specs=[vmem] * 4,
        out_shape=[SDS(w.shape, F32)] * 4, compiler_params=_params(),
    )(parts, w, m, v)


def _pack_small(w_sp, norm_g, ln_g, ln_b, b_sp, sinks, fng, b_ada, b_ada_f):
    row_bsp = jnp.concatenate([b_sp.reshape(1, 1024), sinks.reshape(1, 16), jnp.zeros((1, 1008), F32)], axis=1)
    return jnp.concatenate([
        w_sp.reshape(64, D_MODEL), norm_g.reshape(1, D_MODEL),
        jnp.concatenate([ln_g.reshape(1, D_A), ln_b.reshape(1, D_A)], axis=1), row_bsp, fng.reshape(1, D_MODEL),
        b_ada.reshape(3, D_MODEL), b_ada_f.reshape(2, D_MODEL),
        jnp.zeros((SMALL_ROWS - 73, D_MODEL), F32)], axis=0)


def _unpack_small(p):
    return dict(
        w_spatial=p[ROW_WSP:ROW_WSP + 64].reshape(1, A_GROUPS, CHUNK, CHUNK),
        norm_g=p[ROW_NG].reshape(1, D_MODEL),
        ln_v_g=p[ROW_LN, :D_A].reshape(1, D_A), ln_v_b=p[ROW_LN, D_A:].reshape(1, D_A),
        b_spatial=p[ROW_BSP, :1024].reshape(1, A_GROUPS, CHUNK), sinks=p[ROW_BSP, 1024:1040].reshape(1, 16),
        final_norm_g=p[ROW_FNG].reshape(D_MODEL),
        b_ada=p[ROW_BADA:ROW_BADA + 3].reshape(1, 3 * D_MODEL), b_ada_final=p[ROW_BADAF:ROW_BADAF + 2].reshape(2 * D_MODEL),
    )


def kernel(x, c, w_ada, b_ada, norm_g, w_in, ln_v_g, ln_v_b, w_spatial, b_spatial, sinks, w_out, w_ada_final, b_ada_final, final_norm_g, loss_target, m_w_ada, m_b_ada, m_norm_g, m_w_in, m_ln_v_g, m_ln_v_b, m_w_spatial, m_b_spatial, m_sinks, m_w_out, m_w_ada_final, m_b_ada_final, m_final_norm_g, v_w_ada, v_b_ada, v_norm_g, v_w_in, v_ln_v_g, v_ln_v_b, v_w_spatial, v_b_spatial, v_sinks, v_w_out, v_w_ada_final, v_b_ada_final, v_final_norm_g):
    seq = x.shape[1]
    me = 4 * lax.axis_index("x") + 2 * lax.axis_index("y") + lax.axis_index("c")
    x2, tgt = x[0], loss_target[0]
    fng = final_norm_g.reshape(1, D_MODEL)

    n_ada, n_ada_f = w_ada.shape[2], w_ada_final.shape[1]
    cact, mod, mod_f = _ada_exchange(c, w_ada[0], b_ada.reshape(N_DEV, n_ada), w_ada_final,
                                     b_ada_final.reshape(N_DEV, n_ada_f))
    cact = cact.reshape(N_DEV, D_MODEL)
    mod, mod_f = mod.reshape(1, 3 * D_MODEL), mod_f.reshape(1, 2 * D_MODEL)
    shift, scale, gate = mod[:, :D_MODEL], mod[:, D_MODEL:2 * D_MODEL], mod[:, 2 * D_MODEL:]
    shift_f, scale_f = mod_f[:, :D_MODEL], mod_f[:, D_MODEL:]

    wt_f32, m_wt, v_wt = (jnp.swapaxes(a, 1, 2)[0] for a in (w_in, m_w_in, v_w_in))
    xi, yi = lax.axis_index("x"), lax.axis_index("y")
    chip_order = jnp.stack([2 * xi + yi, 2 * (1 - xi) + yi, 2 * xi + 1 - yi, 2 * (1 - xi) + 1 - yi]).astype(jnp.int32)
    wt_mine, wo_mine = _prep_weights(me.reshape(1), wt_f32, w_out[0])

    tabs = _rope_tables(seq)
    sinks_v = sinks.reshape(16)
    h, proj, wt, wo = _gather_in_proj(chip_order, x2, shift, scale, norm_g, wt_mine, wo_mine)
    y = _mixer_fwd(proj, tabs, ln_v_g, ln_v_b, w_spatial[0], b_spatial[0], sinks_v)
    dx1, do, dy, sums_o = _out_proj_loss(y, x2, tgt, wo, gate, shift_f, scale_f, fng)
    loss = lax.psum(0.5 * jnp.sum(sums_o[4]) / D_MODEL, ("x", "y", "c"))

    core = lax.axis_index("c").reshape(1)
    chip = (2 * lax.axis_index("x") + lax.axis_index("y")).reshape(1)
    g_wo = _wgrad("wgrad_out", y, do, 1024).reshape(N_DEV, D_MODEL // N_DEV, D_MODEL)
    pair_out = _pair_sum(core, g_wo, _rs_pair("rs_pair_out", g_wo))
    dproj, d_ln, d_wsp, d_bsp, d_sinks, parts_out = _mixer_bwd(
        proj, dy, tabs, ln_v_g, ln_v_b, w_spatial[0], b_spatial[0], sinks_v, pair_out)
    g_wt = _wgrad("wgrad_in", dproj, h, 1408).reshape(N_DEV, D_IN // N_DEV, D_MODEL)
    pair_in = _pair_sum(core, g_wt, _rs_pair("rs_pair_in", g_wt))
    grad_x, sums_i, parts_in = _in_proj_bwd(dproj, wt, x2, dx1, scale, norm_g, pair_in)
    wt_leaves = [jnp.swapaxes(a[None], 1, 2) for a in _adam_rows("adam_w_in", chip, pair_in, parts_in, wt_f32, m_wt, v_wt)]
    g_w_in, d_w_in, nm_w_in, nv_w_in = wt_leaves
    g_w_out, d_w_out, nm_w_out, nv_w_out = (
        a[None] for a in _adam_rows("adam_w_out", chip, pair_out, parts_out, w_out[0], m_w_out[0], v_w_out[0]))

    dmod = jnp.concatenate([sums_i[0], sums_i[1], sums_o[0]])
    dmod_f = jnp.concatenate([sums_o[1], sums_o[2]])
    small = _pack_small(d_wsp, sums_i[2], d_ln[0], d_ln[1], d_bsp, d_sinks[:, 0], sums_o[3], dmod, dmod_f)
    (small_all,) = _all_gather("gather_small", [small], pltpu.VMEM)
    packed = [_pack_small(*t) for t in (
        (w_spatial, norm_g, ln_v_g, ln_v_b, b_spatial, sinks, final_norm_g, b_ada, b_ada_final),
        (m_w_spatial, m_norm_g, m_ln_v_g, m_ln_v_b, m_b_spatial, m_sinks, m_final_norm_g, m_b_ada, m_b_ada_final),
        (v_w_spatial, v_norm_g, v_ln_v_g, v_ln_v_b, v_b_spatial, v_sinks, v_final_norm_g, v_b_ada, v_b_ada_final))]
    g_s, d_s, nm_s, nv_s = [_unpack_small(p) for p in _adam_small(small_all, *packed)]

    dmod_all = small_all[:, ROW_BADA:ROW_BADA + 3].reshape(N_DEV, 3 * D_MODEL)
    dmod_f_all = small_all[:, ROW_BADAF:ROW_BADAF + 2].reshape(N_DEV, 2 * D_MODEL)
    dmod_mine = lax.dynamic_slice_in_dim(dmod_all, me * n_ada, n_ada, axis=1)
    dmod_f_mine = lax.dynamic_slice_in_dim(dmod_f_all, me * n_ada_f, n_ada_f, axis=1)
    ada = _adam_ada("adam_w_ada", cact, dmod_mine, w_ada[0], m_w_ada[0], v_w_ada[0])
    ada_f = _adam_ada("adam_w_ada_final", cact, dmod_f_mine, w_ada_final, m_w_ada_final, v_w_ada_final)

    def leaves(k):
        small_k = (g_s, d_s, nm_s, nv_s)[k]
        return (ada[k][None], small_k["b_ada"], small_k["norm_g"], (g_w_in, d_w_in, nm_w_in, nv_w_in)[k],
                small_k["ln_v_g"], small_k["ln_v_b"], small_k["w_spatial"], small_k["b_spatial"], small_k["sinks"],
                (g_w_out, d_w_out, nm_w_out, nv_w_out)[k], ada_f[k], small_k["b_ada_final"],
                small_k["final_norm_g"])

    return (loss, grad_x[None], *leaves(0), *leaves(1), *leaves(2), *leaves(3))
```

```python
import functools

import jax
import jax.numpy as jnp
from jax import lax
from jax.experimental import pallas as pl
from jax.experimental.pallas import tpu as pltpu

D_MODEL = 2048
D_IN = 5632
D_A = 1024
CHUNK = 128
A_GROUPS = 8
HEAD_DIM = 64
N_KV_HEADS = 4
N_DEV = 8
ROPE_THETA = 10000.0
NORM_EPS = 1e-5
ATTN_SCALE = HEAD_DIM ** -0.5

ADAM_LR = 0.001
ADAM_B1 = 0.9
ADAM_B2 = 0.999
ADAM_EPS = 1e-08
ADAM_WD = 0.01
ADAM_STEP = 10

OFF_U, OFF_VA, OFF_ZA, OFF_Q, OFF_K, OFF_V, OFF_ZB = 0, 1024, 2048, 3072, 4096, 4352, 4608

SMALL_ROWS = 80
ROW_WSP, ROW_NG, ROW_LN, ROW_BSP, ROW_FNG, ROW_BADA, ROW_BADAF = 0, 64, 65, 66, 67, 68, 71

V7X_VMEM_LIMIT_BYTES = 56 * 1024 * 1024

F32 = jnp.float32
BF16 = jnp.bfloat16
MESH = pl.DeviceIdType.MESH
SDS = jax.ShapeDtypeStruct
NT = (((1,), (1,)), ((), ()))
TN = (((0,), (0,)), ((), ()))


def _params(*semantics):
    return pltpu.CompilerParams(dimension_semantics=semantics or None, vmem_limit_bytes=V7X_VMEM_LIMIT_BYTES)


def _mesh_pos():
    return lax.axis_index("x"), lax.axis_index("y"), lax.axis_index("c")


def _sigmoid(z):
    return 1.0 / (1.0 + jnp.exp(-z))


def _adamw(w, g, m, v):
    m = ADAM_B1 * m + (1.0 - ADAM_B1) * g
    v = ADAM_B2 * v + (1.0 - ADAM_B2) * (g * g)
    m_hat = m / (1.0 - ADAM_B1 ** ADAM_STEP)
    v_hat = v / (1.0 - ADAM_B2 ** ADAM_STEP)
    delta = -ADAM_LR * (m_hat / (jnp.sqrt(v_hat) + ADAM_EPS) + ADAM_WD * w)
    return delta, m, v


def _all_gather(name, blocks, memory_space):
    n_arr = len(blocks)

    def body(*refs):
        ins, outs = refs[:n_arr], refs[n_arr:2 * n_arr]
        send_sems, recv_sems, local_sems = refs[2 * n_arr:]
        x, y, c = _mesh_pos()
        me, sibling = (x, y, c), (x, y, 1 - c)
        chips = [(1 - x, y), (x, 1 - y), (1 - x, 1 - y)]

        def slot(p):
            return 4 * p[0] + 2 * p[1] + p[2]

        def copy(a, k, block, to, src=None):
            dst = outs[a].at[slot(block)]
            return pltpu.make_async_remote_copy(
                src_ref=dst if src is None else src, dst_ref=dst,
                send_sem=send_sems.at[a, k], recv_sem=recv_sems.at[a, k],
                device_id=to, device_id_type=MESH)

        mine = [pltpu.make_async_copy(ins[a], outs[a].at[slot(me)], local_sems.at[a]) for a in range(n_arr)]
        for cp in mine:
            cp.start()
        first = []
        for a in range(n_arr):
            first.append(copy(a, 0, me, sibling, src=ins[a]))
            first += [copy(a, 1 + j, me, (*chip, c), src=ins[a]) for j, chip in enumerate(chips)]
        for cp in first:
            cp.start()
        passed = []
        for j, chip in enumerate(chips):
            for a in range(n_arr):
                copy(a, 1 + j, (*chip, c), me).wait_recv()
                fwd = copy(a, 4 + j, (*chip, c), sibling)
                fwd.start()
                passed.append(fwd)
        for a in range(n_arr):
            copy(a, 0, sibling, me).wait_recv()
            for j, chip in enumerate(chips):
                copy(a, 4 + j, (*chip, 1 - c), me).wait_recv()
        for cp in first + passed:
            cp.wait_send()
        for cp in mine:
            cp.wait()

    spec = pl.BlockSpec(memory_space=memory_space)
    return pl.pallas_call(
        body, name=name,
        out_shape=[SDS((N_DEV,) + b.shape, b.dtype) for b in blocks],
        in_specs=[spec] * n_arr, out_specs=[spec] * n_arr,
        scratch_shapes=[pltpu.SemaphoreType.DMA((n_arr, 7)), pltpu.SemaphoreType.DMA((n_arr, 7)),
                        pltpu.SemaphoreType.DMA((n_arr,))],
        compiler_params=_params(),
    )(*blocks)


def _ada_exchange(c, w_ada, b_ada8, w_ada_f, b_ada_f8):
    n1, n2 = w_ada.shape[1], w_ada_f.shape[1]

    def body(c_ref, w1_ref, b1_ref, w2_ref, b2_ref, cact_ref, mod_ref, modf_ref,
             cact_buf, res1, res2, send1, send2, sems_s, sems_r):
        x, y, c_pos = _mesh_pos()
        me = 4 * x + 2 * y + c_pos
        flips = [(k >> 2 & 1, k >> 1 & 1, k & 1) for k in range(1, N_DEV)]

        def peer(f):
            return (1 - x if f[0] else x, 1 - y if f[1] else y, 1 - c_pos if f[2] else c_pos)

        cv = c_ref[...]
        cact = cv * _sigmoid(cv)
        cact_buf[...] = cact
        cact_ref[me] = cact

        def rdma(phase, k, src, dst, f):
            return pltpu.make_async_remote_copy(src_ref=src, dst_ref=dst, send_sem=sems_s.at[phase, k],
                                                recv_sem=sems_r.at[phase, k], device_id=peer(f), device_id_type=MESH)

        gather = [rdma(0, k, cact_buf, cact_ref.at[me], f) for k, f in enumerate(flips)]
        for cp in gather:
            cp.start()
        for cp in gather:
            cp.wait_recv()
        for cp in gather:
            cp.wait_send()

        rid = lax.broadcasted_iota(jnp.int32, (N_DEV, D_MODEL), 0)
        rows = jnp.zeros((N_DEV, D_MODEL), F32)
        for j in range(N_DEV):
            rows = jnp.where(rid == j, jnp.broadcast_to(cact_ref[j], (N_DEV, D_MODEL)), rows)
        rows = rows.astype(BF16)
        res1[...] = jnp.dot(rows, w1_ref[...].astype(BF16), preferred_element_type=F32) + b1_ref[pl.ds(me, 1), :]
        res2[...] = jnp.dot(rows, w2_ref[...].astype(BF16), preferred_element_type=F32) + b2_ref[pl.ds(me, 1), :]
        for j in range(N_DEV):
            send1[j] = res1[pl.ds(j, 1), :]
            send2[j] = res2[pl.ds(j, 1), :]
        mod_ref[me] = send1[me]
        modf_ref[me] = send2[me]
        scatter = []
        for k, f in enumerate(flips):
            to = me ^ (k + 1)
            scatter.append(rdma(1, k, send1.at[to], mod_ref.at[me], f))
            scatter.append(rdma(2, k, send2.at[to], modf_ref.at[me], f))
        for cp in scatter:
            cp.start()
        for cp in scatter:
            cp.wait_recv()
        for cp in scatter:
            cp.wait_send()

    vmem = pl.BlockSpec(memory_space=pltpu.VMEM)
    return pl.pallas_call(
        body, name="ada_exchange",
        out_shape=[SDS((N_DEV, 1, D_MODEL), F32), SDS((N_DEV, 1, n1), F32), SDS((N_DEV, 1, n2), F32)],
        in_specs=[vmem] * 5, out_specs=[vmem] * 3,
        scratch_shapes=[pltpu.VMEM((1, D_MODEL), F32), pltpu.VMEM((N_DEV, n1), F32), pltpu.VMEM((N_DEV, n2), F32),
                        pltpu.VMEM((N_DEV, 1, n1), F32), pltpu.VMEM((N_DEV, 1, n2), F32),
                        pltpu.SemaphoreType.DMA((3, 7)), pltpu.SemaphoreType.DMA((3, 7))],
        compiler_params=_params(),
    )(c, w_ada, b_ada8, w_ada_f, b_ada_f8)


def _rs_pair(name, grad):
    def body(g_ref, got_ref, send_sems, recv_sems):
        x, y, c = _mesh_pos()
        copies = [pltpu.make_async_remote_copy(
            src_ref=g_ref.at[2 * q + 1 - c], dst_ref=got_ref.at[q], send_sem=send_sems.at[q], recv_sem=recv_sems.at[q],
            device_id=(x, y, 1 - c), device_id_type=MESH) for q in range(4)]
        for cp in copies:
            cp.start()
        for cp in copies:
            cp.wait_recv()
        for cp in copies:
            cp.wait_send()

    hbm = pl.BlockSpec(memory_space=pl.ANY)
    return pl.pallas_call(
        body, name=name, out_shape=SDS((4,) + grad.shape[1:], grad.dtype), in_specs=[hbm], out_specs=hbm,
        scratch_shapes=[pltpu.SemaphoreType.DMA((4,)), pltpu.SemaphoreType.DMA((4,))],
        compiler_params=_params(),
    )(grad)


def _chip_scatter(pair_ref, parts_ref, send_sems, recv_sems):
    x, y, c = _mesh_pos()
    chips = [(1 - x, y), (x, 1 - y), (1 - x, 1 - y)]
    return [pltpu.make_async_remote_copy(
        src_ref=pair_ref.at[2 * cx + cy], dst_ref=parts_ref.at[j], send_sem=send_sems.at[j], recv_sem=recv_sems.at[j],
        device_id=(cx, cy, c), device_id_type=MESH) for j, (cx, cy) in enumerate(chips)]


def _scatter_scratch():
    return [pltpu.SemaphoreType.DMA((3,)), pltpu.SemaphoreType.DMA((3,))]


def _prep_weights(me, wt, w_out):
    steps = 4

    def body(me_ref, wt_ref, wo_ref, wtb_ref, wob_ref):
        wtb_ref[...] = wt_ref[...].astype(BF16)
        wob_ref[...] = wo_ref[...].astype(BF16)

    def rows(a, mine):
        blk = (a.shape[0] // steps, a.shape[1])
        return pl.BlockSpec(blk, (lambda i, me_ref: (steps * me_ref[0] + i, 0)) if mine else (lambda i, me_ref: (i, 0)))

    return pl.pallas_call(
        body, name="prep_weights",
        grid_spec=pltpu.PrefetchScalarGridSpec(
            num_scalar_prefetch=1, grid=(steps,),
            in_specs=[rows(wt, False), rows(w_out, False)], out_specs=[rows(wt, True), rows(w_out, True)]),
        out_shape=[SDS((N_DEV * wt.shape[0], D_MODEL), BF16), SDS((N_DEV * w_out.shape[0], D_MODEL), BF16)],
        compiler_params=_params("parallel"),
    )(me, wt, w_out)


def _gather_in_proj(order, x, shift, scale, norm_g, wt_all, wo_all):
    s = x.shape[0]
    th = tm = min(512, s)
    nh, ni = s // th, s // tm
    tn = D_IN // 4
    steps = nh + 4 * ni
    block_rows = (D_IN // N_DEV, D_MODEL // N_DEV)

    def body(order_ref, x_ref, shift_ref, scale_ref, g_ref, wt_in, wo_in, h_ref, proj_ref, wt_ref, wo_ref,
             h_scr, w_buf, load_sems, send_sems, recv_sems):
        g = pl.program_id(0)
        mx, my, mc = _mesh_pos()
        me, sibling = (mx, my, mc), (mx, my, 1 - mc)
        chips = [(1 - mx, my), (mx, 1 - my), (1 - mx, 1 - my)]
        gathered = (wt_ref, wo_ref)

        def copy(a, k, block, to):
            n = block_rows[a]
            rows = gathered[a].at[pl.ds(pl.multiple_of((4 * block[0] + 2 * block[1] + block[2]) * n, n), n)]
            return pltpu.make_async_remote_copy(src_ref=rows, dst_ref=rows, send_sem=send_sems.at[a, k],
                                                recv_sem=recv_sems.at[a, k], device_id=to, device_id_type=MESH)

        def tile_load(slot, chip):
            return pltpu.make_async_copy(wt_ref.at[pl.ds(pl.multiple_of(chip * tn, tn), tn)], w_buf.at[slot],
                                         load_sems.at[slot])

        @pl.when(g == 0)
        def _():
            for a in range(2):
                copy(a, 0, me, sibling).start()
                for j, chip in enumerate(chips):
                    copy(a, 1 + j, me, (*chip, mc)).start()

        @pl.when(g < nh)
        def _():
            xv = x_ref[...]
            r = lax.rsqrt(jnp.mean(xv * xv, axis=-1, keepdims=True) + NORM_EPS)
            hb = (((xv * r) * g_ref[...]) * (1.0 + scale_ref[...]) + shift_ref[...]).astype(BF16)
            h_ref[...] = hb
            h_scr[pl.ds(pl.multiple_of(g * th, th), th), :] = hb

        @pl.when(g == nh - 1)
        def _():
            copy(0, 0, sibling, me).wait_recv()
            tile_load(0, order_ref[0]).start()

        @pl.when(g >= nh)
        def _():
            t, i = (g - nh) // ni, (g - nh) % ni

            @pl.when(i == 0)
            def _():
                tile_load(t % 2, order_ref[t]).wait()

            for j, chip in enumerate(chips):
                @pl.when((i == ni - 1) & (t == j))
                def _():
                    copy(0, 1 + j, (*chip, mc), me).wait_recv()
                    copy(0, 4 + j, (*chip, mc), sibling).start()
                    copy(0, 4 + j, (*chip, 1 - mc), me).wait_recv()
                    tile_load((j + 1) % 2, order_ref[j + 1]).start()

            lhs = h_scr[pl.ds(pl.multiple_of(i * tm, tm), tm), :]
            proj_ref[...] = lax.dot_general(lhs, w_buf[t % 2], NT, preferred_element_type=F32).astype(BF16)

        @pl.when(g == steps - 1)
        def _():
            for j, chip in enumerate(chips):
                copy(1, 1 + j, (*chip, mc), me).wait_recv()
                copy(1, 4 + j, (*chip, mc), sibling).start()
            copy(1, 0, sibling, me).wait_recv()
            for j, chip in enumerate(chips):
                copy(1, 4 + j, (*chip, 1 - mc), me).wait_recv()
            for a in range(2):
                for k in range(7):
                    copy(a, k, me, sibling).wait_send()

    def h_tile(g, order_ref):
        return (jnp.minimum(g, nh - 1), 0)

    def proj_tile(g, order_ref):
        mm = jnp.maximum(g - nh, 0)
        return (mm % ni, order_ref[mm // ni])

    row = pl.BlockSpec((1, D_MODEL), lambda g, order_ref: (0, 0))
    hbm = pl.BlockSpec(memory_space=pl.ANY)
    return pl.pallas_call(
        body, name="gather_in_proj",
        grid_spec=pltpu.PrefetchScalarGridSpec(
            num_scalar_prefetch=1, grid=(steps,),
            in_specs=[pl.BlockSpec((th, D_MODEL), h_tile), row, row, row, hbm, hbm],
            out_specs=[pl.BlockSpec((th, D_MODEL), h_tile), pl.BlockSpec((tm, tn), proj_tile), hbm, hbm],
            scratch_shapes=[pltpu.VMEM((s, D_MODEL), BF16), pltpu.VMEM((2, tn, D_MODEL), BF16),
                            pltpu.SemaphoreType.DMA((2,)), pltpu.SemaphoreType.DMA((2, 7)),
                            pltpu.SemaphoreType.DMA((2, 7))]),
        out_shape=[SDS((s, D_MODEL), BF16), SDS((s, D_IN), BF16), SDS(wt_all.shape, BF16), SDS(wo_all.shape, BF16)],
        input_output_aliases={5: 2, 6: 3},
        compiler_params=_params("arbitrary"),
    )(order, x, shift, scale, norm_g, wt_all, wo_all)


def _rope_tables(seq):
    inv_freq = ROPE_THETA ** (-jnp.arange(0, HEAD_DIM, 2, dtype=F32) / HEAD_DIM)
    ang = jnp.arange(seq, dtype=F32)[:, None] * inv_freq[None, :]
    cos, sin, zero = jnp.cos(ang), jnp.sin(ang), jnp.zeros_like(ang)
    return (jnp.concatenate([cos] * 4, axis=1), jnp.concatenate([-sin, zero, -sin, zero], axis=1),
            jnp.concatenate([zero, sin, zero, sin], axis=1))


def _rope(v, cos, sin_lo, sin_hi):
    width = v.shape[1]
    rep = (1, width // 128)
    return (v * jnp.tile(cos, rep) + pltpu.roll(v, width - 32, 1) * jnp.tile(sin_lo, rep)
            + pltpu.roll(v, 32, 1) * jnp.tile(sin_hi, rep))


def _rope_bwd(d, cos, sin_lo, sin_hi):
    width = d.shape[1]
    rep = (1, width // 128)
    return (d * jnp.tile(cos, rep) + pltpu.roll(d * jnp.tile(sin_lo, rep), 32, 1)
            + pltpu.roll(d * jnp.tile(sin_hi, rep), width - 32, 1))


def _layer_norm(v, g, b):
    mu = jnp.mean(v, axis=-1, keepdims=True)
    vc = v - mu
    rstd = lax.rsqrt(jnp.mean(vc * vc, axis=-1, keepdims=True) + NORM_EPS)
    vhat = vc * rstd
    return vhat * g + b, vhat, rstd


def _tril_bf16(w_ref, g):
    t = lax.broadcasted_iota(jnp.int32, (CHUNK, CHUNK), 0)
    tp = lax.broadcasted_iota(jnp.int32, (CHUNK, CHUNK), 1)
    return jnp.where(tp <= t, w_ref[g], 0.0).astype(BF16)


def _bias_columns(b_ref, out_ref):
    for g in range(A_GROUPS):
        out_ref[g] = jnp.broadcast_to(b_ref[pl.ds(g, 1), :], (CHUNK, CHUNK)).T


def _band_mask():
    kj = lax.broadcasted_iota(jnp.int32, (2 * CHUNK, 4 * CHUNK), 0)
    qi = lax.broadcasted_iota(jnp.int32, (2 * CHUNK, 4 * CHUNK), 1) & (CHUNK - 1)
    rel = qi + CHUNK - kj
    return jnp.where((rel >= 0) & (rel < CHUNK), 0.0, -jnp.inf)


def _low_lanes():
    return lax.broadcasted_iota(jnp.int32, (1, 128), 1) < HEAD_DIM


def _stack_heads(pair_a, pair_b):
    lo = _low_lanes()
    return jnp.concatenate([jnp.where(lo, pair_a, 0.0), jnp.where(lo, 0.0, pair_a),
                            jnp.where(lo, pair_b, 0.0), jnp.where(lo, 0.0, pair_b)], axis=0).astype(BF16)


def _heads_to_lanes(per_group):
    rows = [t[:, r * CHUNK:(r + 1) * CHUNK] for t in per_group for r in range(4)]
    return jnp.concatenate(rows, axis=0).T


def _dup_kv_head(band, gk):
    pair = band[:, (gk // 2) * 128:(gk // 2 + 1) * 128]
    lo = _low_lanes()
    one = jnp.where(lo if gk % 2 == 0 else jnp.logical_not(lo), pair, 0.0)
    return (one + pltpu.roll(one, HEAD_DIM, 1)).astype(BF16)


def _fold_kv_head(dup_grad, gk):
    both = dup_grad + pltpu.roll(dup_grad, HEAD_DIM, 1)
    lo = _low_lanes()
    return jnp.where(lo if gk % 2 == 0 else jnp.logical_not(lo), both, 0.0)


def _attn_probs(q_st, k_dup, sink_row, mask, first_block):
    s = lax.dot_general(k_dup, q_st, NT, preferred_element_type=F32) + mask
    s = jnp.concatenate([jnp.where(first_block, -jnp.inf, s[:CHUNK]), s[CHUNK:]], axis=0)
    m = jnp.maximum(jnp.max(s, axis=0, keepdims=True), sink_row)
    p = jnp.exp(s - m)
    e_sink = jnp.exp(sink_row - m)
    inv = 1.0 / (jnp.sum(p, axis=0, keepdims=True) + e_sink)
    return p * inv, e_sink * inv


def _sink_row(sinks_ref, gk):
    return jnp.concatenate([jnp.full((1, CHUNK), sinks_ref[4 * gk + r], F32) for r in range(4)], axis=1)


def _mixer_specs(nb, rev):
    def blk(i):
        return nb - 1 - i if rev else i

    def prev(i):
        return jnp.maximum(blk(i) - 1, 0)

    tab = pl.BlockSpec((CHUNK, 128), lambda i: (blk(i), 0))
    tab_prev = pl.BlockSpec((CHUNK, 128), lambda i: (prev(i), 0))
    return dict(
        cur=pl.BlockSpec((CHUNK, D_IN), lambda i: (blk(i), 0)),
        prev_kv=pl.BlockSpec((CHUNK, 2 * 256), lambda i: (prev(i), OFF_K // 512)),
        tabs=[tab] * 3 + [tab_prev] * 3,
        vec=pl.BlockSpec((1, D_A), lambda i: (0, 0)),
        wsp=pl.BlockSpec((A_GROUPS, CHUNK, CHUNK), lambda i: (0, 0, 0)),
        bsp=pl.BlockSpec((A_GROUPS, CHUNK), lambda i: (0, 0)),
        smem=pl.BlockSpec(memory_space=pltpu.SMEM),
        blk=blk,
    )


def _mixer_fwd(proj, tabs, ln_g, ln_b, w_sp, b_sp, sinks):
    s = proj.shape[0]
    nb = s // CHUNK
    sp = _mixer_specs(nb, rev=False)

    def body(cur_ref, pkv_ref, c_ref, s1_ref, s2_ref, cp_ref, s1p_ref, s2p_ref, lg_ref, lb_ref, w_ref, b_ref,
             sinks_ref, y_ref, bcol, mask):
        i = pl.program_id(0)

        @pl.when(i == 0)
        def _():
            _bias_columns(b_ref, bcol)
            mask[...] = _band_mask()

        vln, _, _ = _layer_norm(cur_ref[:, OFF_VA:OFF_ZA].astype(F32), lg_ref[...], lb_ref[...])
        vln = vln.astype(BF16)
        for g in range(A_GROUPS):
            cols = slice(g * 128, (g + 1) * 128)
            sg = jnp.dot(_tril_bf16(w_ref, g), vln[:, cols], preferred_element_type=F32) + bcol[g]
            u = cur_ref[:, OFF_U + g * 128:OFF_U + (g + 1) * 128].astype(F32)
            z = cur_ref[:, OFF_ZA + g * 128:OFF_ZA + (g + 1) * 128].astype(F32)
            y_ref[:, cols] = (u * sg * (z * _sigmoid(z))).astype(BF16)

        cur_t = (c_ref[...], s1_ref[...], s2_ref[...])
        prev_t = (cp_ref[...], s1p_ref[...], s2p_ref[...])
        qr = _rope(cur_ref[:, OFF_Q:OFF_K].astype(F32), *cur_t) * ATTN_SCALE
        kr = jnp.concatenate([_rope(pkv_ref[:, 0:256].astype(F32), *prev_t),
                              _rope(cur_ref[:, OFF_K:OFF_V].astype(F32), *cur_t)], axis=0)
        v_t = jnp.concatenate([pkv_ref[:, 256:512], cur_ref[:, OFF_V:OFF_ZB]], axis=0).astype(F32).T.astype(BF16)
        outs = []
        for gk in range(N_KV_HEADS):
            q_st = _stack_heads(qr[:, (2 * gk) * 128:(2 * gk + 1) * 128], qr[:, (2 * gk + 1) * 128:(2 * gk + 2) * 128])
            probs, _ = _attn_probs(q_st, _dup_kv_head(kr, gk), _sink_row(sinks_ref, gk), mask[...], i == 0)
            outs.append(jnp.dot(v_t[gk * HEAD_DIM:(gk + 1) * HEAD_DIM], probs.astype(BF16),
                                preferred_element_type=F32))
        zb = cur_ref[:, OFF_ZB:D_IN].astype(F32)
        y_ref[:, D_A:D_MODEL] = (_heads_to_lanes(outs) * (zb * _sigmoid(zb))).astype(BF16)

    return pl.pallas_call(
        body, name="mixer_fwd", grid=(nb,),
        in_specs=[sp["cur"], sp["prev_kv"], *sp["tabs"], sp["vec"], sp["vec"], sp["wsp"], sp["bsp"], sp["smem"]],
        out_specs=pl.BlockSpec((CHUNK, D_MODEL), lambda i: (i, 0)),
        out_shape=SDS((s, D_MODEL), BF16),
        scratch_shapes=[pltpu.VMEM((A_GROUPS, CHUNK, CHUNK), F32), pltpu.VMEM((2 * CHUNK, 4 * CHUNK), F32)],
        compiler_params=_params("arbitrary"),
    )(proj, proj, *tabs, *tabs, ln_g, ln_b, w_sp, b_sp, sinks)


def _out_proj_loss(y, x, target, wo, gate, shift_f, scale_f, fng):
    s = y.shape[0]
    tm, tr = 256, 128
    nt = s // tm

    def body(y_ref, x_ref, t_ref, wo_ref, gate_ref, sh_ref, sc_ref, g_ref, dx1_ref, do_ref, dy_ref, sums_ref,
             do_last, do_work):
        i = pl.program_id(0)

        @pl.when(i == 0)
        def _():
            sums_ref[...] = jnp.zeros_like(sums_ref)
            do_last[...] = jnp.zeros_like(do_last)

        do_work[...] = do_last[...]
        o = jnp.dot(y_ref[...], wo_ref[...], preferred_element_type=F32)
        gate, g, sh = gate_ref[...], g_ref[...], sh_ref[...]
        one_sc = 1.0 + sc_ref[...]
        cs, inv_d = g * one_sc, 1.0 / D_MODEL

        def rowsum(v):
            return jnp.sum(v, axis=0, keepdims=True)

        sums = [jnp.zeros((1, D_MODEL), F32) for _ in range(4)]
        for c in range(tm // tr):
            rows = slice(c * tr, (c + 1) * tr)
            oc = o[rows]
            x1 = x_ref[rows, :] + gate * oc
            r = lax.rsqrt(jnp.sum(x1 * x1, axis=-1, keepdims=True) * inv_d + NORM_EPS)
            x1n = x1 * r
            diff = x1n * cs + sh - t_ref[rows, :]
            w = diff * x1n
            lane_sum = jnp.sum(w * cs, axis=-1, keepdims=True)
            dx1 = (diff * cs) * (r * inv_d) - x1n * (r * lane_sum * (inv_d * inv_d))
            dx1_ref[rows, :] = dx1
            do = (dx1 * gate).astype(BF16)
            do_ref[rows, :] = do
            do_last[rows, :] = do
            for k, v in enumerate((dx1 * oc, diff, w, diff * diff)):
                sums[k] = sums[k] + rowsum(v)
        live = jnp.where(i < nt, 1.0, 0.0)
        sums_ref[0:1, :] += live * sums[0]
        sums_ref[1:2, :] += (live * inv_d) * sums[1]
        sums_ref[2:3, :] += (live * inv_d) * (sums[2] * g)
        sums_ref[3:4, :] += (live * inv_d) * (sums[2] * one_sc)
        sums_ref[4:5, :] += live * sums[3]
        dy_ref[...] = lax.dot_general(do_work[...], wo_ref[...], NT, preferred_element_type=F32).astype(BF16)

    tile = pl.BlockSpec((tm, D_MODEL), lambda i: (jnp.minimum(i, nt - 1), 0))
    row = pl.BlockSpec((1, D_MODEL), lambda i: (0, 0))
    return pl.pallas_call(
        body, name="out_proj_loss", grid=(nt + 1,),
        in_specs=[tile, tile, tile, pl.BlockSpec((D_MODEL, D_MODEL), lambda i: (0, 0)), row, row, row, row],
        out_specs=[tile, tile, pl.BlockSpec((tm, D_MODEL), lambda i: (jnp.maximum(i - 1, 0), 0)),
                   pl.BlockSpec((8, D_MODEL), lambda i: (0, 0))],
        out_shape=[SDS((s, D_MODEL), F32), SDS((s, D_MODEL), BF16), SDS((s, D_MODEL), BF16), SDS((8, D_MODEL), F32)],
        scratch_shapes=[pltpu.VMEM((tm, D_MODEL), BF16), pltpu.VMEM((tm, D_MODEL), BF16)],
        compiler_params=_params("arbitrary"),
    )(y, x, target, wo, gate, shift_f, scale_f, fng)


def _mixer_bwd(proj, dy, tabs, ln_g, ln_b, w_sp, b_sp, sinks, pair):
    s = proj.shape[0]
    nb = s // CHUNK
    sp = _mixer_specs(nb, rev=True)

    def body(cur_ref, pkv_ref, dy_ref, c_ref, s1_ref, s2_ref, cp_ref, s1p_ref, s2p_ref, lg_ref, lb_ref, w_ref, b_ref,
             sinks_ref, pair_ref, dproj_ref, dln_ref, dw_ref, db_ref, dsink_ref, parts_ref, bcol, dbcol, carry, mask,
             send_sems, recv_sems):
        i = pl.program_id(0)
        block = nb - 1 - i

        @pl.when(i == 0)
        def _():
            for cp in _chip_scatter(pair_ref, parts_ref, send_sems, recv_sems):
                cp.start()
            _bias_columns(b_ref, bcol)
            mask[...] = _band_mask()
            dbcol[...] = jnp.zeros_like(dbcol)
            carry[...] = jnp.zeros_like(carry)
            dln_ref[...] = jnp.zeros_like(dln_ref)
            dw_ref[...] = jnp.zeros_like(dw_ref)
            dsink_ref[...] = jnp.zeros_like(dsink_ref)

        vln, vhat, rstd = _layer_norm(cur_ref[:, OFF_VA:OFF_ZA].astype(F32), lg_ref[...], lb_ref[...])
        vln = vln.astype(BF16)
        d_vln = []
        for g in range(A_GROUPS):
            cols = slice(g * 128, (g + 1) * 128)
            w_g = _tril_bf16(w_ref, g)
            sg = jnp.dot(w_g, vln[:, cols], preferred_element_type=F32) + bcol[g]
            u = cur_ref[:, OFF_U + g * 128:OFF_U + (g + 1) * 128].astype(F32)
            z = cur_ref[:, OFF_ZA + g * 128:OFF_ZA + (g + 1) * 128].astype(F32)
            dya = dy_ref[:, cols].astype(F32)
            sig = _sigmoid(z)
            d_ya = dya * (z * sig)
            dproj_ref[:, OFF_ZA + g * 128:OFF_ZA + (g + 1) * 128] = (
                dya * (u * sg) * (sig * (1.0 + z * (1.0 - sig)))).astype(BF16)
            dproj_ref[:, OFF_U + g * 128:OFF_U + (g + 1) * 128] = (d_ya * sg).astype(BF16)
            d_s = d_ya * u
            dbcol[g] += d_s
            d_sb = d_s.astype(BF16)
            dw_ref[g] += lax.dot_general(d_sb, vln[:, cols], NT, preferred_element_type=F32)
            d_vln.append(lax.dot_general(w_g, d_sb, TN, preferred_element_type=F32))
        d_vln = jnp.concatenate(d_vln, axis=1)
        dln_ref[0:1, :] += jnp.sum(d_vln * vhat, axis=0, keepdims=True)
        dln_ref[1:2, :] += jnp.sum(d_vln, axis=0, keepdims=True)
        d_vhat = d_vln * lg_ref[...]
        d_va = rstd * (d_vhat - jnp.mean(d_vhat, axis=-1, keepdims=True)
                       - vhat * jnp.mean(d_vhat * vhat, axis=-1, keepdims=True))
        dproj_ref[:, OFF_VA:OFF_ZA] = d_va.astype(BF16)

        cur_t = (c_ref[...], s1_ref[...], s2_ref[...])
        prev_t = (cp_ref[...], s1p_ref[...], s2p_ref[...])
        band_t = tuple(jnp.concatenate([p, c], axis=0) for p, c in zip(prev_t, cur_t))
        qr = _rope(cur_ref[:, OFF_Q:OFF_K].astype(F32), *cur_t) * ATTN_SCALE
        kr = jnp.concatenate([_rope(pkv_ref[:, 0:256].astype(F32), *prev_t),
                              _rope(cur_ref[:, OFF_K:OFF_V].astype(F32), *cur_t)], axis=0)
        vb = jnp.concatenate([pkv_ref[:, 256:512], cur_ref[:, OFF_V:OFF_ZB]], axis=0).astype(F32)
        k_t, v_t = (kr.T * ATTN_SCALE).astype(BF16), vb.T.astype(BF16)
        zb = cur_ref[:, OFF_ZB:D_IN].astype(F32)
        dyb = dy_ref[:, D_A:D_MODEL].astype(F32)
        sig = _sigmoid(zb)
        d_yb = dyb * (zb * sig)
        outs, dqs = [], []
        dk_pairs = [jnp.zeros((2 * CHUNK, 128), F32) for _ in range(2)]
        dv_pairs = [jnp.zeros((2 * CHUNK, 128), F32) for _ in range(2)]
        for gk in range(N_KV_HEADS):
            heads = slice(gk * HEAD_DIM, (gk + 1) * HEAD_DIM)
            q_st = _stack_heads(qr[:, (2 * gk) * 128:(2 * gk + 1) * 128], qr[:, (2 * gk + 1) * 128:(2 * gk + 2) * 128])
            k_dup, v_dup = _dup_kv_head(kr, gk), _dup_kv_head(vb, gk)
            probs, p_sink = _attn_probs(q_st, k_dup, _sink_row(sinks_ref, gk), mask[...], block == 0)
            probs_b = probs.astype(BF16)
            outs.append(jnp.dot(v_t[heads], probs_b, preferred_element_type=F32))
            do_st = _stack_heads(d_yb[:, (2 * gk) * 128:(2 * gk + 1) * 128], d_yb[:, (2 * gk + 1) * 128:(2 * gk + 2) * 128])
            dp = lax.dot_general(v_dup, do_st, NT, preferred_element_type=F32)
            delta = jnp.sum(probs * dp, axis=0, keepdims=True)
            ds = (probs * (dp - delta)).astype(BF16)
            d_sink = -p_sink * delta
            for r in range(4):
                dsink_ref[4 * gk + r:4 * gk + r + 1, :] += jnp.broadcast_to(
                    jnp.sum(d_sink[:, r * CHUNK:(r + 1) * CHUNK], axis=1, keepdims=True), (1, 128))
            dqs.append(jnp.dot(k_t[heads], ds, preferred_element_type=F32))
            dk_pairs[gk // 2] += _fold_kv_head(jnp.dot(ds, q_st, preferred_element_type=F32), gk)
            dv_pairs[gk // 2] += _fold_kv_head(jnp.dot(probs_b, do_st, preferred_element_type=F32), gk)
        dproj_ref[:, OFF_ZB:D_IN] = (dyb * _heads_to_lanes(outs) * (sig * (1.0 + zb * (1.0 - sig)))).astype(BF16)
        dproj_ref[:, OFF_Q:OFF_K] = _rope_bwd(_heads_to_lanes(dqs), *cur_t).astype(BF16)
        dk_band = _rope_bwd(jnp.concatenate(dk_pairs, axis=1), *band_t)
        dv_band = jnp.concatenate(dv_pairs, axis=1)
        dproj_ref[:, OFF_K:OFF_V] = (dk_band[CHUNK:] + carry[:, 0:256]).astype(BF16)
        dproj_ref[:, OFF_V:OFF_ZB] = (dv_band[CHUNK:] + carry[:, 256:512]).astype(BF16)
        carry[:, 0:256] = dk_band[:CHUNK]
        carry[:, 256:512] = dv_band[:CHUNK]

        @pl.when(i == nb - 1)
        def _():
            t = lax.broadcasted_iota(jnp.int32, (CHUNK, CHUNK), 0)
            tp = lax.broadcasted_iota(jnp.int32, (CHUNK, CHUNK), 1)
            for g in range(A_GROUPS):
                dw_ref[g] = jnp.where(tp <= t, dw_ref[g], 0.0)
                db_ref[pl.ds(g, 1), :] = jnp.sum(dbcol[g].T, axis=0, keepdims=True)
            scatter = _chip_scatter(pair_ref, parts_ref, send_sems, recv_sems)
            for cp in scatter:
                cp.wait_recv()
            for cp in scatter:
                cp.wait_send()

    blk = sp["blk"]
    hbm = pl.BlockSpec(memory_space=pl.ANY)
    return pl.pallas_call(
        body, name="mixer_bwd", grid=(nb,),
        in_specs=[sp["cur"], sp["prev_kv"], pl.BlockSpec((CHUNK, D_MODEL), lambda i: (blk(i), 0)), *sp["tabs"],
                  sp["vec"], sp["vec"], sp["wsp"], sp["bsp"], sp["smem"], hbm],
        out_specs=[pl.BlockSpec((CHUNK, D_IN), lambda i: (blk(i), 0)),
                   pl.BlockSpec((8, D_A), lambda i: (0, 0)),
                   pl.BlockSpec((A_GROUPS, CHUNK, CHUNK), lambda i: (0, 0, 0)),
                   pl.BlockSpec((A_GROUPS, CHUNK), lambda i: (0, 0)),
                   pl.BlockSpec((16, 128), lambda i: (0, 0)), hbm],
        out_shape=[SDS((s, D_IN), BF16), SDS((8, D_A), F32), SDS((A_GROUPS, CHUNK, CHUNK), F32),
                   SDS((A_GROUPS, CHUNK), F32), SDS((16, 128), F32), SDS((3,) + pair.shape[1:], pair.dtype)],
        scratch_shapes=[pltpu.VMEM((A_GROUPS, CHUNK, CHUNK), F32), pltpu.VMEM((A_GROUPS, CHUNK, CHUNK), F32),
                        pltpu.VMEM((CHUNK, 512), F32), pltpu.VMEM((2 * CHUNK, 4 * CHUNK), F32), *_scatter_scratch()],
        compiler_params=_params("arbitrary"),
    )(proj, proj, dy, *tabs, *tabs, ln_g, ln_b, w_sp, b_sp, sinks, pair)


def _wgrad(name, a, b, bm):
    s, m = a.shape
    n = b.shape[1]
    bt = 512
    steps = s // bt

    def body(a_ref, b_ref, out_ref, acc):
        t = pl.program_id(1)

        @pl.when(t == 0)
        def _():
            acc[...] = jnp.zeros_like(acc)

        acc[...] += lax.dot_general(a_ref[...], b_ref[...], TN, preferred_element_type=F32)

        @pl.when(t == steps - 1)
        def _():
            out_ref[...] = acc[...].astype(out_ref.dtype)

    return pl.pallas_call(
        body, name=name, grid=(m // bm, steps),
        in_specs=[pl.BlockSpec((bt, bm), lambda i, t: (t, i)), pl.BlockSpec((bt, n), lambda i, t: (t, 0))],
        out_specs=pl.BlockSpec((bm, n), lambda i, t: (i, 0)),
        out_shape=SDS((m, n), BF16),
        scratch_shapes=[pltpu.VMEM((bm, n), F32)],
        compiler_params=_params("parallel", "arbitrary"),
    )(a, b)


def _in_proj_bwd(dproj, wt, x, dx1, scale, norm_g, pair):
    s = x.shape[0]
    tm, tk, tr = min(1024, s), D_IN // 4, 64
    ksteps = D_IN // tk

    def body(dp_ref, wt_ref, x_hbm, dx1_hbm, sc_ref, g_ref, pair_ref, gx_ref, sums_ref, parts_ref, x_buf, dx1_buf,
             tile_sems, send_sems, recv_sems):
        i, k = pl.program_id(0), pl.program_id(1)

        def tile_copies():
            rows = pl.ds(pl.multiple_of(i * tm, tm), tm)
            return (pltpu.make_async_copy(x_hbm.at[rows], x_buf, tile_sems.at[0]),
                    pltpu.make_async_copy(dx1_hbm.at[rows], dx1_buf, tile_sems.at[1]))

        @pl.when((i == 0) & (k == 0))
        def _():
            for cp in _chip_scatter(pair_ref, parts_ref, send_sems, recv_sems):
                cp.start()
            sums_ref[...] = jnp.zeros_like(sums_ref)

        @pl.when(k == 0)
        def _():
            for cp in tile_copies():
                cp.start()
            gx_ref[...] = jnp.dot(dp_ref[...], wt_ref[...], preferred_element_type=F32)

        @pl.when(k > 0)
        def _():
            gx_ref[...] += jnp.dot(dp_ref[...], wt_ref[...], preferred_element_type=F32)

        @pl.when(k == ksteps - 1)
        def _():
            for cp in tile_copies():
                cp.wait()
            one_sc, g = 1.0 + sc_ref[...], g_ref[...]
            cs = one_sc * g

            def chunk(j, sums):
                rows = pl.ds(pl.multiple_of(j * tr, tr), tr)
                dh, xv = gx_ref[rows, :], x_buf[rows, :]
                dhx = dh * xv
                r = lax.rsqrt(jnp.sum(xv * xv, axis=-1, keepdims=True) * (1.0 / D_MODEL) + NORM_EPS)
                coef = (r * r * r) * (jnp.sum(dhx * cs, axis=-1, keepdims=True) * (1.0 / D_MODEL))
                gx_ref[rows, :] = dx1_buf[rows, :] + r * (dh * cs) - xv * coef
                return (sums[0] + jnp.sum(dh, axis=0, keepdims=True), sums[1] + jnp.sum(dhx * r, axis=0, keepdims=True))

            zero = jnp.zeros((1, D_MODEL), F32)
            sums = lax.fori_loop(0, tm // tr, chunk, (zero, zero))
            sums_ref[0:1, :] += sums[0]
            sums_ref[1:2, :] += sums[1] * g
            sums_ref[2:3, :] += sums[1] * one_sc

        @pl.when((i == s // tm - 1) & (k == ksteps - 1))
        def _():
            scatter = _chip_scatter(pair_ref, parts_ref, send_sems, recv_sems)
            for cp in scatter:
                cp.wait_recv()
            for cp in scatter:
                cp.wait_send()

    row = pl.BlockSpec((1, D_MODEL), lambda i, k: (0, 0))
    hbm = pl.BlockSpec(memory_space=pl.ANY)
    return pl.pallas_call(
        body, name="in_proj_bwd", grid=(s // tm, ksteps),
        in_specs=[pl.BlockSpec((tm, tk), lambda i, k: (i, k)), pl.BlockSpec((tk, D_MODEL), lambda i, k: (k, 0)),
                  hbm, hbm, row, row, hbm],
        out_specs=[pl.BlockSpec((tm, D_MODEL), lambda i, k: (i, 0)), pl.BlockSpec((8, D_MODEL), lambda i, k: (0, 0)),
                   hbm],
        out_shape=[SDS((s, D_MODEL), F32), SDS((8, D_MODEL), F32), SDS((3,) + pair.shape[1:], pair.dtype)],
        scratch_shapes=[pltpu.VMEM((tm, D_MODEL), F32), pltpu.VMEM((tm, D_MODEL), F32),
                        pltpu.SemaphoreType.DMA((2,)), *_scatter_scratch()],
        compiler_params=_params("arbitrary", "arbitrary"),
    )(dproj, wt, x, dx1, scale, norm_g, pair)


def _pair_sum(core, grad, got):
    _, m, n = got.shape

    def body(core_ref, a_ref, b_ref, out_ref):
        out_ref[...] = (a_ref[...].astype(F32) + b_ref[...].astype(F32)).astype(BF16)

    blk = pl.BlockSpec((1, m, n), lambda q, core_ref: (q, 0, 0))
    return pl.pallas_call(
        body, name=f"pair_sum_{m}",
        grid_spec=pltpu.PrefetchScalarGridSpec(
            num_scalar_prefetch=1, grid=(4,),
            in_specs=[pl.BlockSpec((1, m, n), lambda q, core_ref: (2 * q + core_ref[0], 0, 0)), blk], out_specs=blk),
        out_shape=SDS(got.shape, BF16), compiler_params=_params("parallel"),
    )(core, grad, got)


def _sum_chips(own_ref, parts_ref):
    return ((own_ref[0].astype(F32) + parts_ref[0].astype(F32)) + parts_ref[1].astype(F32)) + parts_ref[2].astype(F32)


def _adam_rows(name, chip, pair, parts, w, m, v):
    rows = w.shape[0]
    tr = rows // 4

    def body(chip_ref, own_ref, p_ref, w_ref, m_ref, v_ref, g_ref, d_ref, nm_ref, nv_ref):
        g = _sum_chips(own_ref, p_ref)
        g_ref[...] = g
        d_ref[...], nm_ref[...], nv_ref[...] = _adamw(w_ref[...], g, m_ref[...], v_ref[...])

    blk = pl.BlockSpec((tr, D_MODEL), lambda j, chip_ref: (j, 0))
    return pl.pallas_call(
        body, name=name,
        grid_spec=pltpu.PrefetchScalarGridSpec(
            num_scalar_prefetch=1, grid=(rows // tr,),
            in_specs=[pl.BlockSpec((1, tr, D_MODEL), lambda j, chip_ref: (chip_ref[0], j, 0)),
                      pl.BlockSpec((3, tr, D_MODEL), lambda j, chip_ref: (0, j, 0)), blk, blk, blk],
            out_specs=[blk] * 4),
        out_shape=[SDS(w.shape, F32)] * 4, compiler_params=_params("parallel"),
    )(chip, pair, parts, w, m, v)


def _adam_ada(name, cact, dmod, w, m, v):
    n = w.shape[1]
    tr = 512

    def body(c_ref, dm_ref, w_ref, m_ref, v_ref, g_ref, d_ref, nm_ref, nv_ref):
        pad_c = jnp.concatenate([c_ref[...], jnp.zeros_like(c_ref)], axis=0).astype(BF16)
        pad_d = jnp.concatenate([dm_ref[...], jnp.zeros_like(dm_ref)], axis=0).astype(BF16)
        g = lax.dot_general(pad_c, pad_d, TN, preferred_element_type=F32)
        g_ref[...] = g
        d_ref[...], nm_ref[...], nv_ref[...] = _adamw(w_ref[...], g, m_ref[...], v_ref[...])

    blk = pl.BlockSpec((tr, n), lambda j: (j, 0))
    return pl.pallas_call(
        body, name=name, grid=(D_MODEL // tr,),
        in_specs=[pl.BlockSpec((N_DEV, tr), lambda j: (0, j)), pl.BlockSpec((N_DEV, n), lambda j: (0, 0)),
                  blk, blk, blk],
        out_specs=[blk] * 4, out_shape=[SDS(w.shape, F32)] * 4,
        compiler_params=_params("parallel"),
    )(cact, dmod, w, m, v)


def _adam_small(parts, w, m, v):
    def body(p_ref, w_ref, m_ref, v_ref, g_ref, d_ref, nm_ref, nv_ref):
        g = p_ref[0]
        for j in range(1, N_DEV):
            g = g + p_ref[j]
        g_ref[...] = g
        d_ref[...], nm_ref[...], nv_ref[...] = _adamw(w_ref[...], g, m_ref[...], v_ref[...])

    vmem = pl.BlockSpec(memory_space=pltpu.VMEM)
    return pl.pallas_call(
        body, name="adam_small", in_specs=[vmem] * 4, out_specs=[vmem] * 4,
        out_shape=[SDS(w.shape, F32)] * 4, compiler_params=_params(),
    )(parts, w, m, v)


def _pack_small(w_sp, norm_g, ln_g, ln_b, b_sp, sinks, fng, b_ada, b_ada_f):
    row_bsp = jnp.concatenate([b_sp.reshape(1, 1024), sinks.reshape(1, 16), jnp.zeros((1, 1008), F32)], axis=1)
    return jnp.concatenate([
        w_sp.reshape(64, D_MODEL), norm_g.reshape(1, D_MODEL),
        jnp.concatenate([ln_g.reshape(1, D_A), ln_b.reshape(1, D_A)], axis=1), row_bsp, fng.reshape(1, D_MODEL),
        b_ada.reshape(3, D_MODEL), b_ada_f.reshape(2, D_MODEL),
        jnp.zeros((SMALL_ROWS - 73, D_MODEL), F32)], axis=0)


def _unpack_small(p):
    return dict(
        w_spatial=p[ROW_WSP:ROW_WSP + 64].reshape(1, A_GROUPS, CHUNK, CHUNK),
        norm_g=p[ROW_NG].reshape(1, D_MODEL),
        ln_v_g=p[ROW_LN, :D_A].reshape(1, D_A), ln_v_b=p[ROW_LN, D_A:].reshape(1, D_A),
        b_spatial=p[ROW_BSP, :1024].reshape(1, A_GROUPS, CHUNK), sinks=p[ROW_BSP, 1024:1040].reshape(1, 16),
        final_norm_g=p[ROW_FNG].reshape(D_MODEL),
        b_ada=p[ROW_BADA:ROW_BADA + 3].reshape(1, 3 * D_MODEL), b_ada_final=p[ROW_BADAF:ROW_BADAF + 2].reshape(2 * D_MODEL),
    )


def kernel(x, c, w_ada, b_ada, norm_g, w_in, ln_v_g, ln_v_b, w_spatial, b_spatial, sinks, w_out, w_ada_final, b_ada_final, final_norm_g, loss_target, m_w_ada, m_b_ada, m_norm_g, m_w_in, m_ln_v_g, m_ln_v_b, m_w_spatial, m_b_spatial, m_sinks, m_w_out, m_w_ada_final, m_b_ada_final, m_final_norm_g, v_w_ada, v_b_ada, v_norm_g, v_w_in, v_ln_v_g, v_ln_v_b, v_w_spatial, v_b_spatial, v_sinks, v_w_out, v_w_ada_final, v_b_ada_final, v_final_norm_g):
    seq = x.shape[1]
    me = 4 * lax.axis_index("x") + 2 * lax.axis_index("y") + lax.axis_index("c")
    x2, tgt = x[0], loss_target[0]
    fng = final_norm_g.reshape(1, D_MODEL)

    n_ada, n_ada_f = w_ada.shape[2], w_ada_final.shape[1]
    cact, mod, mod_f = _ada_exchange(c, w_ada[0], b_ada.reshape(N_DEV, n_ada), w_ada_final,
                                     b_ada_final.reshape(N_DEV, n_ada_f))
    cact = cact.reshape(N_DEV, D_MODEL)
    mod, mod_f = mod.reshape(1, 3 * D_MODEL), mod_f.reshape(1, 2 * D_MODEL)
    shift, scale, gate = mod[:, :D_MODEL], mod[:, D_MODEL:2 * D_MODEL], mod[:, 2 * D_MODEL:]
    shift_f, scale_f = mod_f[:, :D_MODEL], mod_f[:, D_MODEL:]

    wt_f32, m_wt, v_wt = (jnp.swapaxes(a, 1, 2)[0] for a in (w_in, m_w_in, v_w_in))
    xi, yi = lax.axis_index("x"), lax.axis_index("y")
    chip_order = jnp.stack([2 * xi + yi, 2 * (1 - xi) + yi, 2 * xi + 1 - yi, 2 * (1 - xi) + 1 - yi]).astype(jnp.int32)
    wt_mine, wo_mine = _prep_weights(me.reshape(1), wt_f32, w_out[0])

    tabs = _rope_tables(seq)
    sinks_v = sinks.reshape(16)
    h, proj, wt, wo = _gather_in_proj(chip_order, x2, shift, scale, norm_g, wt_mine, wo_mine)
    y = _mixer_fwd(proj, tabs, ln_v_g, ln_v_b, w_spatial[0], b_spatial[0], sinks_v)
    dx1, do, dy, sums_o = _out_proj_loss(y, x2, tgt, wo, gate, shift_f, scale_f, fng)
    loss = lax.psum(0.5 * jnp.sum(sums_o[4]) / D_MODEL, ("x", "y", "c"))

    core = lax.axis_index("c").reshape(1)
    chip = (2 * lax.axis_index("x") + lax.axis_index("y")).reshape(1)
    g_wo = _wgrad("wgrad_out", y, do, 1024).reshape(N_DEV, D_MODEL // N_DEV, D_MODEL)
    pair_out = _pair_sum(core, g_wo, _rs_pair("rs_pair_out", g_wo))
    dproj, d_ln, d_wsp, d_bsp, d_sinks, parts_out = _mixer_bwd(
        proj, dy, tabs, ln_v_g, ln_v_b, w_spatial[0], b_spatial[0], sinks_v, pair_out)
    g_wt = _wgrad("wgrad_in", dproj, h, 1408).reshape(N_DEV, D_IN // N_DEV, D_MODEL)
    pair_in = _pair_sum(core, g_wt, _rs_pair("rs_pair_in", g_wt))
    grad_x, sums_i, parts_in = _in_proj_bwd(dproj, wt, x2, dx1, scale, norm_g, pair_in)
    wt_leaves = [jnp.swapaxes(a[None], 1, 2) for a in _adam_rows("adam_w_in", chip, pair_in, parts_in, wt_f32, m_wt, v_wt)]
    g_w_in, d_w_in, nm_w_in, nv_w_in = wt_leaves
    g_w_out, d_w_out, nm_w_out, nv_w_out = (
        a[None] for a in _adam_rows("adam_w_out", chip, pair_out, parts_out, w_out[0], m_w_out[0], v_w_out[0]))

    dmod = jnp.concatenate([sums_i[0], sums_i[1], sums_o[0]])
    dmod_f = jnp.concatenate([sums_o[1], sums_o[2]])
    small = _pack_small(d_wsp, sums_i[2], d_ln[0], d_ln[1], d_bsp, d_sinks[:, 0], sums_o[3], dmod, dmod_f)
    (small_all,) = _all_gather("gather_small", [small], pltpu.VMEM)
    packed = [_pack_small(*t) for t in (
        (w_spatial, norm_g, ln_v_g, ln_v_b, b_spatial, sinks, final_norm_g, b_ada, b_ada_final),
        (m_w_spatial, m_norm_g, m_ln_v_g, m_ln_v_b, m_b_spatial, m_sinks, m_final_norm_g, m_b_ada, m_b_ada_final),
        (v_w_spatial, v_norm_g, v_ln_v_g, v_ln_v_b, v_b_spatial, v_sinks, v_final_norm_g, v_b_ada, v_b_ada_final))]
    g_s, d_s, nm_s, nv_s = [_unpack_small(p) for p in _adam_small(small_all, *packed)]

    dmod_all = small_all[:, ROW_BADA:ROW_BADA + 3].reshape(N_DEV, 3 * D_MODEL)
    dmod_f_all = small_all[:, ROW_BADAF:ROW_BADAF + 2].reshape(N_DEV, 2 * D_MODEL)
    dmod_mine = lax.dynamic_slice_in_dim(dmod_all, me * n_ada, n_ada, axis=1)
    dmod_f_mine = lax.dynamic_slice_in_dim(dmod_f_all, me * n_ada_f, n_ada_f, axis=1)
    ada = _adam_ada("adam_w_ada", cact, dmod_mine, w_ada[0], m_w_ada[0], v_w_ada[0])
    ada_f = _adam_ada("adam_w_ada_final", cact, dmod_f_mine, w_ada_final, m_w_ada_final, v_w_ada_final)

    def leaves(k):
        small_k = (g_s, d_s, nm_s, nv_s)[k]
        return (ada[k][None], small_k["b_ada"], small_k["norm_g"], (g_w_in, d_w_in, nm_w_in, nv_w_in)[k],
                small_k["ln_v_g"], small_k["ln_v_b"], small_k["w_spatial"], small_k["b_spatial"], small_k["sinks"],
                (g_w_out, d_w_out, nm_w_out, nv_w_out)[k], ada_f[k], small_k["b_ada_final"],
                small_k["final_norm_g"])

    return (loss, grad_x[None], *leaves(0), *leaves(1), *leaves(2), *leaves(3))
```

```python
import functools

import jax
import jax.numpy as jnp
from jax import lax
from jax.experimental import pallas as pl
from jax.experimental.pallas import tpu as pltpu

D_MODEL = 2048
D_IN = 5632
D_A = 1024
CHUNK = 128
A_GROUPS = 8
HEAD_DIM = 64
N_KV_HEADS = 4
N_DEV = 8
ROPE_THETA = 10000.0
NORM_EPS = 1e-5
ATTN_SCALE = HEAD_DIM ** -0.5

ADAM_LR = 0.001
ADAM_B1 = 0.9
ADAM_B2 = 0.999
ADAM_EPS = 1e-08
ADAM_WD = 0.01
ADAM_STEP = 10

OFF_U, OFF_VA, OFF_ZA, OFF_Q, OFF_K, OFF_V, OFF_ZB = 0, 1024, 2048, 3072, 4096, 4352, 4608

V7X_VMEM_LIMIT_BYTES = 56 * 1024 * 1024

F32 = jnp.float32
BF16 = jnp.bfloat16
MESH = pl.DeviceIdType.MESH
SDS = jax.ShapeDtypeStruct
NT = (((1,), (1,)), ((), ()))
TN = (((0,), (0,)), ((), ()))


def _params(*semantics):
    return pltpu.CompilerParams(dimension_semantics=semantics or None, vmem_limit_bytes=V7X_VMEM_LIMIT_BYTES)


def _mesh_pos():
    return lax.axis_index("x"), lax.axis_index("y"), lax.axis_index("c")


def _sigmoid(z):
    return 1.0 / (1.0 + jnp.exp(-z))


def _adamw(w, g, m, v):
    m = ADAM_B1 * m + (1.0 - ADAM_B1) * g
    v = ADAM_B2 * v + (1.0 - ADAM_B2) * (g * g)
    m_hat = m / (1.0 - ADAM_B1 ** ADAM_STEP)
    v_hat = v / (1.0 - ADAM_B2 ** ADAM_STEP)
    delta = -ADAM_LR * (m_hat / (jnp.sqrt(v_hat) + ADAM_EPS) + ADAM_WD * w)
    return delta, m, v


def _all_gather(name, blocks, memory_space):
    n_arr = len(blocks)

    def body(*refs):
        ins, outs = refs[:n_arr], refs[n_arr:2 * n_arr]
        send_sems, recv_sems, local_sems = refs[2 * n_arr:]
        x, y, c = _mesh_pos()
        me, sibling = (x, y, c), (x, y, 1 - c)
        chips = [(1 - x, y), (x, 1 - y), (1 - x, 1 - y)]

        def slot(p):
            return 4 * p[0] + 2 * p[1] + p[2]

        def copy(a, k, block, to, src=None):
            dst = outs[a].at[slot(block)]
            return pltpu.make_async_remote_copy(
                src_ref=dst if src is None else src, dst_ref=dst,
                send_sem=send_sems.at[a, k], recv_sem=recv_sems.at[a, k],
                device_id=to, device_id_type=MESH)

        mine = [pltpu.make_async_copy(ins[a], outs[a].at[slot(me)], local_sems.at[a]) for a in range(n_arr)]
        for cp in mine:
            cp.start()
        first = []
        for a in range(n_arr):
            first.append(copy(a, 0, me, sibling, src=ins[a]))
            first += [copy(a, 1 + j, me, (*chip, c), src=ins[a]) for j, chip in enumerate(chips)]
        for cp in first:
            cp.start()
        passed = []
        for j, chip in enumerate(chips):
            for a in range(n_arr):
                copy(a, 1 + j, (*chip, c), me).wait_recv()
                fwd = copy(a, 4 + j, (*chip, c), sibling)
                fwd.start()
                passed.append(fwd)
        for a in range(n_arr):
            copy(a, 0, sibling, me).wait_recv()
            for j, chip in enumerate(chips):
                copy(a, 4 + j, (*chip, 1 - c), me).wait_recv()
        for cp in first + passed:
            cp.wait_send()
        for cp in mine:
            cp.wait()

    spec = pl.BlockSpec(memory_space=memory_space)
    return pl.pallas_call(
        body, name=name,
        out_shape=[SDS((N_DEV,) + b.shape, b.dtype) for b in blocks],
        in_specs=[spec] * n_arr, out_specs=[spec] * n_arr,
        scratch_shapes=[pltpu.SemaphoreType.DMA((n_arr, 7)), pltpu.SemaphoreType.DMA((n_arr, 7)),
                        pltpu.SemaphoreType.DMA((n_arr,))],
        compiler_params=_params(),
    )(*blocks)


def _ada_exchange(c, w_ada, b_ada8, w_ada_f, b_ada_f8):
    n1, n2 = w_ada.shape[1], w_ada_f.shape[1]

    def body(c_ref, w1_ref, b1_ref, w2_ref, b2_ref, cact_ref, mod_ref, modf_ref,
             cact_buf, res1, res2, send1, send2, sems_s, sems_r):
        x, y, c_pos = _mesh_pos()
        me = 4 * x + 2 * y + c_pos
        flips = [(k >> 2 & 1, k >> 1 & 1, k & 1) for k in range(1, N_DEV)]

        def peer(f):
            return (1 - x if f[0] else x, 1 - y if f[1] else y, 1 - c_pos if f[2] else c_pos)

        cv = c_ref[...]
        cact = cv * _sigmoid(cv)
        cact_buf[...] = cact
        cact_ref[me] = cact

        def rdma(phase, k, src, dst, f):
            return pltpu.make_async_remote_copy(src_ref=src, dst_ref=dst, send_sem=sems_s.at[phase, k],
                                                recv_sem=sems_r.at[phase, k], device_id=peer(f), device_id_type=MESH)

        gather = [rdma(0, k, cact_buf, cact_ref.at[me], f) for k, f in enumerate(flips)]
        for cp in gather:
            cp.start()
        for cp in gather:
            cp.wait_recv()
        for cp in gather:
            cp.wait_send()

        rid = lax.broadcasted_iota(jnp.int32, (N_DEV, D_MODEL), 0)
        rows = jnp.zeros((N_DEV, D_MODEL), F32)
        for j in range(N_DEV):
            rows = jnp.where(rid == j, jnp.broadcast_to(cact_ref[j], (N_DEV, D_MODEL)), rows)
        rows = rows.astype(BF16)
        res1[...] = jnp.dot(rows, w1_ref[...].astype(BF16), preferred_element_type=F32) + b1_ref[pl.ds(me, 1), :]
        res2[...] = jnp.dot(rows, w2_ref[...].astype(BF16), preferred_element_type=F32) + b2_ref[pl.ds(me, 1), :]
        for j in range(N_DEV):
            send1[j] = res1[pl.ds(j, 1), :]
            send2[j] = res2[pl.ds(j, 1), :]
        mod_ref[me] = send1[me]
        modf_ref[me] = send2[me]
        scatter = []
        for k, f in enumerate(flips):
            to = me ^ (k + 1)
            scatter.append(rdma(1, k, send1.at[to], mod_ref.at[me], f))
            scatter.append(rdma(2, k, send2.at[to], modf_ref.at[me], f))
        for cp in scatter:
            cp.start()
        for cp in scatter:
            cp.wait_recv()
        for cp in scatter:
            cp.wait_send()

    vmem = pl.BlockSpec(memory_space=pltpu.VMEM)
    return pl.pallas_call(
        body, name="ada_exchange",
        out_shape=[SDS((N_DEV, 1, D_MODEL), F32), SDS((N_DEV, 1, n1), F32), SDS((N_DEV, 1, n2), F32)],
        in_specs=[vmem] * 5, out_specs=[vmem] * 3,
        scratch_shapes=[pltpu.VMEM((1, D_MODEL), F32), pltpu.VMEM((N_DEV, n1), F32), pltpu.VMEM((N_DEV, n2), F32),
                        pltpu.VMEM((N_DEV, 1, n1), F32), pltpu.VMEM((N_DEV, 1, n2), F32),
                        pltpu.SemaphoreType.DMA((3, 7)), pltpu.SemaphoreType.DMA((3, 7))],
        compiler_params=_params(),
    )(c, w_ada, b_ada8, w_ada_f, b_ada_f8)


def _rs_pair(name, grad):
    def body(g_ref, got_ref, send_sems, recv_sems):
        x, y, c = _mesh_pos()
        copies = [pltpu.make_async_remote_copy(
            src_ref=g_ref.at[2 * q + 1 - c], dst_ref=got_ref.at[q], send_sem=send_sems.at[q], recv_sem=recv_sems.at[q],
            device_id=(x, y, 1 - c), device_id_type=MESH) for q in range(4)]
        for cp in copies:
            cp.start()
        for cp in copies:
            cp.wait_recv()
        for cp in copies:
            cp.wait_send()

    hbm = pl.BlockSpec(memory_space=pl.ANY)
    return pl.pallas_call(
        body, name=name, out_shape=SDS((4,) + grad.shape[1:], grad.dtype), in_specs=[hbm], out_specs=hbm,
        scratch_shapes=[pltpu.SemaphoreType.DMA((4,)), pltpu.SemaphoreType.DMA((4,))],
        compiler_params=_params(),
    )(grad)


def _chip_scatter(pair_ref, parts_ref, send_sems, recv_sems):
    x, y, c = _mesh_pos()
    chips = [(1 - x, y), (x, 1 - y), (1 - x, 1 - y)]
    return [pltpu.make_async_remote_copy(
        src_ref=pair_ref.at[2 * cx + cy], dst_ref=parts_ref.at[j], send_sem=send_sems.at[j], recv_sem=recv_sems.at[j],
        device_id=(cx, cy, c), device_id_type=MESH) for j, (cx, cy) in enumerate(chips)]


def _scatter_scratch():
    return [pltpu.SemaphoreType.DMA((3,)), pltpu.SemaphoreType.DMA((3,))]


def _prep_weights(me, wt, w_out):
    steps = 4

    def body(me_ref, wt_ref, wo_ref, wtb_ref, wob_ref):
        wtb_ref[...] = wt_ref[...].astype(BF16)
        wob_ref[...] = wo_ref[...].astype(BF16)

    def rows(a, mine):
        blk = (a.shape[0] // steps, a.shape[1])
        return pl.BlockSpec(blk, (lambda i, me_ref: (steps * me_ref[0] + i, 0)) if mine else (lambda i, me_ref: (i, 0)))

    return pl.pallas_call(
        body, name="prep_weights",
        grid_spec=pltpu.PrefetchScalarGridSpec(
            num_scalar_prefetch=1, grid=(steps,),
            in_specs=[rows(wt, False), rows(w_out, False)], out_specs=[rows(wt, True), rows(w_out, True)]),
        out_shape=[SDS((N_DEV * wt.shape[0], D_MODEL), BF16), SDS((N_DEV * w_out.shape[0], D_MODEL), BF16)],
        compiler_params=_params("parallel"),
    )(me, wt, w_out)


class _InPlaceGather:
    def __init__(self, buf_ref, send_sems, recv_sems):
        self.buf, self.send_sems, self.recv_sems = buf_ref, send_sems, recv_sems
        self.n = buf_ref.shape[0] // N_DEV
        x, y, c = _mesh_pos()
        self.me, self.sibling, self.core = (x, y, c), (x, y, 1 - c), c
        self.chips = [(1 - x, y), (x, 1 - y), (1 - x, 1 - y)]

    def copy(self, k, block, to):
        start = pl.multiple_of((4 * block[0] + 2 * block[1] + block[2]) * self.n, self.n)
        rows = self.buf.at[pl.ds(start, self.n)]
        return pltpu.make_async_remote_copy(src_ref=rows, dst_ref=rows, send_sem=self.send_sems.at[k],
                                            recv_sem=self.recv_sems.at[k], device_id=to, device_id_type=MESH)

    def start(self):
        self.copy(0, self.me, self.sibling).start()
        for j, chip in enumerate(self.chips):
            self.copy(1 + j, self.me, (*chip, self.core)).start()

    def pass_on(self, j):
        self.copy(1 + j, (*self.chips[j], self.core), self.me).wait_recv()
        self.copy(4 + j, (*self.chips[j], self.core), self.sibling).start()

    def wait_sibling(self, k):
        self.copy(k, self.sibling, self.me).wait_recv()

    def wait_sends(self):
        for k in range(7):
            self.copy(k, self.me, self.sibling).wait_send()


def _gather_scratch():
    return [pltpu.SemaphoreType.DMA((7,)), pltpu.SemaphoreType.DMA((7,))]


def _gather_in_proj(order, x, shift, scale, norm_g, wt_all):
    s = x.shape[0]
    th = tm = min(512, s)
    nh, ni = s // th, s // tm
    tn = D_IN // 4
    steps = nh + 4 * ni

    def body(order_ref, x_ref, shift_ref, scale_ref, g_ref, wt_in, h_ref, proj_ref, wt_ref,
             h_scr, w_buf, load_sems, send_sems, recv_sems):
        g = pl.program_id(0)
        gather = _InPlaceGather(wt_ref, send_sems, recv_sems)

        def tile_load(slot, chip):
            return pltpu.make_async_copy(wt_ref.at[pl.ds(pl.multiple_of(chip * tn, tn), tn)], w_buf.at[slot],
                                         load_sems.at[slot])

        @pl.when(g == 0)
        def _():
            gather.start()

        @pl.when(g < nh)
        def _():
            xv = x_ref[...]
            r = lax.rsqrt(jnp.mean(xv * xv, axis=-1, keepdims=True) + NORM_EPS)
            hb = (((xv * r) * g_ref[...]) * (1.0 + scale_ref[...]) + shift_ref[...]).astype(BF16)
            h_ref[...] = hb
            h_scr[pl.ds(pl.multiple_of(g * th, th), th), :] = hb

        @pl.when(g == nh - 1)
        def _():
            gather.wait_sibling(0)
            tile_load(0, order_ref[0]).start()

        @pl.when(g >= nh)
        def _():
            t, i = (g - nh) // ni, (g - nh) % ni

            @pl.when(i == 0)
            def _():
                tile_load(t % 2, order_ref[t]).wait()

            for j in range(3):
                @pl.when((i == ni - 1) & (t == j))
                def _():
                    gather.pass_on(j)
                    gather.wait_sibling(4 + j)
                    tile_load((j + 1) % 2, order_ref[j + 1]).start()

            lhs = h_scr[pl.ds(pl.multiple_of(i * tm, tm), tm), :]
            proj_ref[...] = lax.dot_general(lhs, w_buf[t % 2], NT, preferred_element_type=F32).astype(BF16)

        @pl.when(g == steps - 1)
        def _():
            gather.wait_sends()

    def h_tile(g, order_ref):
        return (jnp.minimum(g, nh - 1), 0)

    def proj_tile(g, order_ref):
        mm = jnp.maximum(g - nh, 0)
        return (mm % ni, order_ref[mm // ni])

    row = pl.BlockSpec((1, D_MODEL), lambda g, order_ref: (0, 0))
    hbm = pl.BlockSpec(memory_space=pl.ANY)
    return pl.pallas_call(
        body, name="gather_in_proj",
        grid_spec=pltpu.PrefetchScalarGridSpec(
            num_scalar_prefetch=1, grid=(steps,),
            in_specs=[pl.BlockSpec((th, D_MODEL), h_tile), row, row, row, hbm],
            out_specs=[pl.BlockSpec((th, D_MODEL), h_tile), pl.BlockSpec((tm, tn), proj_tile), hbm],
            scratch_shapes=[pltpu.VMEM((s, D_MODEL), BF16), pltpu.VMEM((2, tn, D_MODEL), BF16),
                            pltpu.SemaphoreType.DMA((2,)), *_gather_scratch()]),
        out_shape=[SDS((s, D_MODEL), BF16), SDS((s, D_IN), BF16), SDS(wt_all.shape, BF16)],
        input_output_aliases={5: 2},
        compiler_params=_params("arbitrary"),
    )(order, x, shift, scale, norm_g, wt_all)


def _rope_tables(seq):
    inv_freq = ROPE_THETA ** (-jnp.arange(0, HEAD_DIM, 2, dtype=F32) / HEAD_DIM)
    ang = jnp.arange(seq, dtype=F32)[:, None] * inv_freq[None, :]
    cos, sin, zero = jnp.cos(ang), jnp.sin(ang), jnp.zeros_like(ang)
    return (jnp.concatenate([cos] * 4, axis=1), jnp.concatenate([-sin, zero, -sin, zero], axis=1),
            jnp.concatenate([zero, sin, zero, sin], axis=1))


def _rope(v, cos, sin_lo, sin_hi):
    width = v.shape[1]
    rep = (1, width // 128)
    return (v * jnp.tile(cos, rep) + pltpu.roll(v, width - 32, 1) * jnp.tile(sin_lo, rep)
            + pltpu.roll(v, 32, 1) * jnp.tile(sin_hi, rep))


def _rope_bwd(d, cos, sin_lo, sin_hi):
    width = d.shape[1]
    rep = (1, width // 128)
    return (d * jnp.tile(cos, rep) + pltpu.roll(d * jnp.tile(sin_lo, rep), 32, 1)
            + pltpu.roll(d * jnp.tile(sin_hi, rep), width - 32, 1))


def _layer_norm(v, g, b):
    mu = jnp.mean(v, axis=-1, keepdims=True)
    vc = v - mu
    rstd = lax.rsqrt(jnp.mean(vc * vc, axis=-1, keepdims=True) + NORM_EPS)
    vhat = vc * rstd
    return vhat * g + b, vhat, rstd


def _tril_bf16(w_ref, g):
    t = lax.broadcasted_iota(jnp.int32, (CHUNK, CHUNK), 0)
    tp = lax.broadcasted_iota(jnp.int32, (CHUNK, CHUNK), 1)
    return jnp.where(tp <= t, w_ref[g], 0.0).astype(BF16)


def _bias_columns(b_ref, out_ref):
    for g in range(A_GROUPS):
        out_ref[g] = jnp.broadcast_to(b_ref[pl.ds(g, 1), :], (CHUNK, CHUNK)).T


def _band_mask():
    kj = lax.broadcasted_iota(jnp.int32, (2 * CHUNK, 4 * CHUNK), 0)
    qi = lax.broadcasted_iota(jnp.int32, (2 * CHUNK, 4 * CHUNK), 1) & (CHUNK - 1)
    rel = qi + CHUNK - kj
    return jnp.where((rel >= 0) & (rel < CHUNK), 0.0, -jnp.inf)


def _low_lanes():
    return lax.broadcasted_iota(jnp.int32, (1, 128), 1) < HEAD_DIM


def _stack_heads(pair_a, pair_b):
    lo = _low_lanes()
    return jnp.concatenate([jnp.where(lo, pair_a, 0.0), jnp.where(lo, 0.0, pair_a),
                            jnp.where(lo, pair_b, 0.0), jnp.where(lo, 0.0, pair_b)], axis=0).astype(BF16)


def _heads_to_lanes(per_group):
    rows = [t[:, r * CHUNK:(r + 1) * CHUNK] for t in per_group for r in range(4)]
    return jnp.concatenate(rows, axis=0).T


def _dup_kv_head(band, gk):
    pair = band[:, (gk // 2) * 128:(gk // 2 + 1) * 128]
    lo = _low_lanes()
    one = jnp.where(lo if gk % 2 == 0 else jnp.logical_not(lo), pair, 0.0)
    return (one + pltpu.roll(one, HEAD_DIM, 1)).astype(BF16)


def _fold_kv_head(dup_grad, gk):
    both = dup_grad + pltpu.roll(dup_grad, HEAD_DIM, 1)
    lo = _low_lanes()
    return jnp.where(lo if gk % 2 == 0 else jnp.logical_not(lo), both, 0.0)


def _attn_probs(q_st, k_dup, sink_row, mask, first_block):
    s = lax.dot_general(k_dup, q_st, NT, preferred_element_type=F32) + mask
    s = jnp.concatenate([jnp.where(first_block, -jnp.inf, s[:CHUNK]), s[CHUNK:]], axis=0)
    m = jnp.maximum(jnp.max(s, axis=0, keepdims=True), sink_row)
    p = jnp.exp(s - m)
    e_sink = jnp.exp(sink_row - m)
    inv = 1.0 / (jnp.sum(p, axis=0, keepdims=True) + e_sink)
    return p * inv, e_sink * inv


def _sink_row(sinks_ref, gk):
    return jnp.concatenate([jnp.full((1, CHUNK), sinks_ref[4 * gk + r], F32) for r in range(4)], axis=1)


def _mixer_specs(nb, rev):
    def blk(i):
        return nb - 1 - i if rev else i

    def prev(i):
        return jnp.maximum(blk(i) - 1, 0)

    tab = pl.BlockSpec((CHUNK, 128), lambda i, *_: (blk(i), 0))
    tab_prev = pl.BlockSpec((CHUNK, 128), lambda i, *_: (prev(i), 0))
    return dict(
        cur=pl.BlockSpec((CHUNK, D_IN), lambda i, *_: (blk(i), 0)),
        prev_kv=pl.BlockSpec((CHUNK, 2 * 256), lambda i, *_: (prev(i), OFF_K // 512)),
        tabs=[tab] * 3 + [tab_prev] * 3,
        vec=pl.BlockSpec((1, D_A), lambda i, *_: (0, 0)),
        wsp=pl.BlockSpec((A_GROUPS, CHUNK, CHUNK), lambda i, *_: (0, 0, 0)),
        bsp=pl.BlockSpec((A_GROUPS, CHUNK), lambda i, *_: (0, 0)),
        smem=pl.BlockSpec(memory_space=pltpu.SMEM),
        blk=blk,
    )


def _mixer_fwd(proj, tabs, ln_g, ln_b, w_sp, b_sp, sinks, wo_all):
    s = proj.shape[0]
    nb = s // CHUNK
    sp = _mixer_specs(nb, rev=False)

    def body(cur_ref, pkv_ref, c_ref, s1_ref, s2_ref, cp_ref, s1p_ref, s2p_ref, lg_ref, lb_ref, w_ref, b_ref,
             sinks_ref, wo_in, y_ref, wo_ref, bcol, mask, send_sems, recv_sems):
        i = pl.program_id(0)
        gather = _InPlaceGather(wo_ref, send_sems, recv_sems)

        @pl.when(i == 0)
        def _():
            gather.start()
            _bias_columns(b_ref, bcol)
            mask[...] = _band_mask()

        @pl.when(i == nb // 2)
        def _():
            for j in range(3):
                gather.pass_on(j)

        vln, _, _ = _layer_norm(cur_ref[:, OFF_VA:OFF_ZA].astype(F32), lg_ref[...], lb_ref[...])
        vln = vln.astype(BF16)
        for g in range(A_GROUPS):
            cols = slice(g * 128, (g + 1) * 128)
            sg = jnp.dot(_tril_bf16(w_ref, g), vln[:, cols], preferred_element_type=F32) + bcol[g]
            u = cur_ref[:, OFF_U + g * 128:OFF_U + (g + 1) * 128].astype(F32)
            z = cur_ref[:, OFF_ZA + g * 128:OFF_ZA + (g + 1) * 128].astype(F32)
            y_ref[:, cols] = (u * sg * (z * _sigmoid(z))).astype(BF16)

        cur_t = (c_ref[...], s1_ref[...], s2_ref[...])
        prev_t = (cp_ref[...], s1p_ref[...], s2p_ref[...])
        qr = _rope(cur_ref[:, OFF_Q:OFF_K].astype(F32), *cur_t) * ATTN_SCALE
        kr = jnp.concatenate([_rope(pkv_ref[:, 0:256].astype(F32), *prev_t),
                              _rope(cur_ref[:, OFF_K:OFF_V].astype(F32), *cur_t)], axis=0)
        v_t = jnp.concatenate([pkv_ref[:, 256:512], cur_ref[:, OFF_V:OFF_ZB]], axis=0).astype(F32).T.astype(BF16)
        outs = []
        for gk in range(N_KV_HEADS):
            q_st = _stack_heads(qr[:, (2 * gk) * 128:(2 * gk + 1) * 128], qr[:, (2 * gk + 1) * 128:(2 * gk + 2) * 128])
            probs, _ = _attn_probs(q_st, _dup_kv_head(kr, gk), _sink_row(sinks_ref, gk), mask[...], i == 0)
            outs.append(jnp.dot(v_t[gk * HEAD_DIM:(gk + 1) * HEAD_DIM], probs.astype(BF16),
                                preferred_element_type=F32))
        zb = cur_ref[:, OFF_ZB:D_IN].astype(F32)
        y_ref[:, D_A:D_MODEL] = (_heads_to_lanes(outs) * (zb * _sigmoid(zb))).astype(BF16)

        @pl.when(i == nb - 1)
        def _():
            gather.wait_sibling(0)
            for j in range(3):
                gather.wait_sibling(4 + j)
            gather.wait_sends()

    hbm = pl.BlockSpec(memory_space=pl.ANY)
    return pl.pallas_call(
        body, name="mixer_fwd", grid=(nb,),
        in_specs=[sp["cur"], sp["prev_kv"], *sp["tabs"], sp["vec"], sp["vec"], sp["wsp"], sp["bsp"], sp["smem"], hbm],
        out_specs=[pl.BlockSpec((CHUNK, D_MODEL), lambda i: (i, 0)), hbm],
        out_shape=[SDS((s, D_MODEL), BF16), SDS(wo_all.shape, wo_all.dtype)],
        scratch_shapes=[pltpu.VMEM((A_GROUPS, CHUNK, CHUNK), F32), pltpu.VMEM((2 * CHUNK, 4 * CHUNK), F32),
                        *_gather_scratch()],
        input_output_aliases={13: 1},
        compiler_params=_params("arbitrary"),
    )(proj, proj, *tabs, *tabs, ln_g, ln_b, w_sp, b_sp, sinks, wo_all)


def _out_proj_loss(y, x, target, wo, gate, shift_f, scale_f, fng):
    s = y.shape[0]
    tm, tr = 256, 128
    nt = s // tm

    def body(y_ref, x_ref, t_ref, wo_ref, gate_ref, sh_ref, sc_ref, g_ref, dx1_ref, do_ref, dy_ref, sums_ref,
             do_last, do_work):
        i = pl.program_id(0)

        @pl.when(i == 0)
        def _():
            sums_ref[...] = jnp.zeros_like(sums_ref)
            do_last[...] = jnp.zeros_like(do_last)

        do_work[...] = do_last[...]
        o = jnp.dot(y_ref[...], wo_ref[...], preferred_element_type=F32)
        gate, g, sh = gate_ref[...], g_ref[...], sh_ref[...]
        one_sc = 1.0 + sc_ref[...]
        cs, inv_d = g * one_sc, 1.0 / D_MODEL

        def rowsum(v):
            return jnp.sum(v, axis=0, keepdims=True)

        sums = [jnp.zeros((1, D_MODEL), F32) for _ in range(4)]
        for c in range(tm // tr):
            rows = slice(c * tr, (c + 1) * tr)
            oc = o[rows]
            x1 = x_ref[rows, :] + gate * oc
            r = lax.rsqrt(jnp.sum(x1 * x1, axis=-1, keepdims=True) * inv_d + NORM_EPS)
            x1n = x1 * r
            diff = x1n * cs + sh - t_ref[rows, :]
            w = diff * x1n
            lane_sum = jnp.sum(w * cs, axis=-1, keepdims=True)
            dx1 = (diff * cs) * (r * inv_d) - x1n * (r * lane_sum * (inv_d * inv_d))
            dx1_ref[rows, :] = dx1
            do = (dx1 * gate).astype(BF16)
            do_ref[rows, :] = do
            do_last[rows, :] = do
            for k, v in enumerate((dx1 * oc, diff, w, diff * diff)):
                sums[k] = sums[k] + rowsum(v)
        live = jnp.where(i < nt, 1.0, 0.0)
        sums_ref[0:1, :] += live * sums[0]
        sums_ref[1:2, :] += (live * inv_d) * sums[1]
        sums_ref[2:3, :] += (live * inv_d) * (sums[2] * g)
        sums_ref[3:4, :] += (live * inv_d) * (sums[2] * one_sc)
        sums_ref[4:5, :] += live * sums[3]
        dy_ref[...] = lax.dot_general(do_work[...], wo_ref[...], NT, preferred_element_type=F32).astype(BF16)

    tile = pl.BlockSpec((tm, D_MODEL), lambda i: (jnp.minimum(i, nt - 1), 0))
    row = pl.BlockSpec((1, D_MODEL), lambda i: (0, 0))
    return pl.pallas_call(
        body, name="out_proj_loss", grid=(nt + 1,),
        in_specs=[tile, tile, tile, pl.BlockSpec((D_MODEL, D_MODEL), lambda i: (0, 0)), row, row, row, row],
        out_specs=[tile, tile, pl.BlockSpec((tm, D_MODEL), lambda i: (jnp.maximum(i - 1, 0), 0)),
                   pl.BlockSpec((8, D_MODEL), lambda i: (0, 0))],
        out_shape=[SDS((s, D_MODEL), F32), SDS((s, D_MODEL), BF16), SDS((s, D_MODEL), BF16), SDS((8, D_MODEL), F32)],
        scratch_shapes=[pltpu.VMEM((tm, D_MODEL), BF16), pltpu.VMEM((tm, D_MODEL), BF16)],
        compiler_params=_params("arbitrary"),
    )(y, x, target, wo, gate, shift_f, scale_f, fng)


ROW_DBSP, ROW_DSINKS, MISC_ROWS = 0, 8, 32


def _mixer_bwd(me, proj, dy, tabs, ln_g, ln_b, w_sp, b_sp, sinks, pair):
    s = proj.shape[0]
    nb = s // CHUNK
    sp = _mixer_specs(nb, rev=True)

    def body(me_ref, cur_ref, pkv_ref, dy_ref, c_ref, s1_ref, s2_ref, cp_ref, s1p_ref, s2p_ref, lg_ref, lb_ref, w_ref,
             b_ref, sinks_ref, pair_ref, dproj_ref, dln_ref, dw_ref, misc_ref, parts_ref, bcol, dbcol, carry, mask,
             send_sems, recv_sems):
        i = pl.program_id(0)
        block = nb - 1 - i

        @pl.when(i == 0)
        def _():
            for cp in _chip_scatter(pair_ref, parts_ref, send_sems, recv_sems):
                cp.start()
            _bias_columns(b_ref, bcol)
            mask[...] = _band_mask()
            dbcol[...] = jnp.zeros_like(dbcol)
            carry[...] = jnp.zeros_like(carry)
            dln_ref[...] = jnp.zeros_like(dln_ref)
            dw_ref[...] = jnp.zeros_like(dw_ref)
            misc_ref[...] = jnp.zeros_like(misc_ref)

        vln, vhat, rstd = _layer_norm(cur_ref[:, OFF_VA:OFF_ZA].astype(F32), lg_ref[...], lb_ref[...])
        vln = vln.astype(BF16)
        d_vln = []
        for g in range(A_GROUPS):
            cols = slice(g * 128, (g + 1) * 128)
            w_g = _tril_bf16(w_ref, g)
            sg = jnp.dot(w_g, vln[:, cols], preferred_element_type=F32) + bcol[g]
            u = cur_ref[:, OFF_U + g * 128:OFF_U + (g + 1) * 128].astype(F32)
            z = cur_ref[:, OFF_ZA + g * 128:OFF_ZA + (g + 1) * 128].astype(F32)
            dya = dy_ref[:, cols].astype(F32)
            sig = _sigmoid(z)
            d_ya = dya * (z * sig)
            dproj_ref[:, OFF_ZA + g * 128:OFF_ZA + (g + 1) * 128] = (
                dya * (u * sg) * (sig * (1.0 + z * (1.0 - sig)))).astype(BF16)
            dproj_ref[:, OFF_U + g * 128:OFF_U + (g + 1) * 128] = (d_ya * sg).astype(BF16)
            d_s = d_ya * u
            dbcol[g] += d_s
            d_sb = d_s.astype(BF16)
            dw_ref[g] += lax.dot_general(d_sb, vln[:, cols], NT, preferred_element_type=F32)
            d_vln.append(lax.dot_general(w_g, d_sb, TN, preferred_element_type=F32))
        d_vln = jnp.concatenate(d_vln, axis=1)
        dln_ref[0:1, :] += jnp.sum(d_vln * vhat, axis=0, keepdims=True)
        dln_ref[1:2, :] += jnp.sum(d_vln, axis=0, keepdims=True)
        d_vhat = d_vln * lg_ref[...]
        d_va = rstd * (d_vhat - jnp.mean(d_vhat, axis=-1, keepdims=True)
                       - vhat * jnp.mean(d_vhat * vhat, axis=-1, keepdims=True))
        dproj_ref[:, OFF_VA:OFF_ZA] = d_va.astype(BF16)

        cur_t = (c_ref[...], s1_ref[...], s2_ref[...])
        prev_t = (cp_ref[...], s1p_ref[...], s2p_ref[...])
        band_t = tuple(jnp.concatenate([p, c], axis=0) for p, c in zip(prev_t, cur_t))
        qr = _rope(cur_ref[:, OFF_Q:OFF_K].astype(F32), *cur_t) * ATTN_SCALE
        kr = jnp.concatenate([_rope(pkv_ref[:, 0:256].astype(F32), *prev_t),
                              _rope(cur_ref[:, OFF_K:OFF_V].astype(F32), *cur_t)], axis=0)
        vb = jnp.concatenate([pkv_ref[:, 256:512], cur_ref[:, OFF_V:OFF_ZB]], axis=0).astype(F32)
        k_t, v_t = (kr.T * ATTN_SCALE).astype(BF16), vb.T.astype(BF16)
        zb = cur_ref[:, OFF_ZB:D_IN].astype(F32)
        dyb = dy_ref[:, D_A:D_MODEL].astype(F32)
        sig = _sigmoid(zb)
        d_yb = dyb * (zb * sig)
        outs, dqs = [], []
        dk_pairs = [jnp.zeros((2 * CHUNK, 128), F32) for _ in range(2)]
        dv_pairs = [jnp.zeros((2 * CHUNK, 128), F32) for _ in range(2)]
        for gk in range(N_KV_HEADS):
            heads = slice(gk * HEAD_DIM, (gk + 1) * HEAD_DIM)
            q_st = _stack_heads(qr[:, (2 * gk) * 128:(2 * gk + 1) * 128], qr[:, (2 * gk + 1) * 128:(2 * gk + 2) * 128])
            k_dup, v_dup = _dup_kv_head(kr, gk), _dup_kv_head(vb, gk)
            probs, p_sink = _attn_probs(q_st, k_dup, _sink_row(sinks_ref, gk), mask[...], block == 0)
            probs_b = probs.astype(BF16)
            outs.append(jnp.dot(v_t[heads], probs_b, preferred_element_type=F32))
            do_st = _stack_heads(d_yb[:, (2 * gk) * 128:(2 * gk + 1) * 128], d_yb[:, (2 * gk + 1) * 128:(2 * gk + 2) * 128])
            dp = lax.dot_general(v_dup, do_st, NT, preferred_element_type=F32)
            delta = jnp.sum(probs * dp, axis=0, keepdims=True)
            ds = (probs * (dp - delta)).astype(BF16)
            d_sink = -p_sink * delta
            for r in range(4):
                row = ROW_DSINKS + 4 * gk + r
                misc_ref[row:row + 1, :] += jnp.broadcast_to(
                    jnp.sum(d_sink[:, r * CHUNK:(r + 1) * CHUNK], axis=1, keepdims=True), (1, 128))
            dqs.append(jnp.dot(k_t[heads], ds, preferred_element_type=F32))
            dk_pairs[gk // 2] += _fold_kv_head(jnp.dot(ds, q_st, preferred_element_type=F32), gk)
            dv_pairs[gk // 2] += _fold_kv_head(jnp.dot(probs_b, do_st, preferred_element_type=F32), gk)
        dproj_ref[:, OFF_ZB:D_IN] = (dyb * _heads_to_lanes(outs) * (sig * (1.0 + zb * (1.0 - sig)))).astype(BF16)
        dproj_ref[:, OFF_Q:OFF_K] = _rope_bwd(_heads_to_lanes(dqs), *cur_t).astype(BF16)
        dk_band = _rope_bwd(jnp.concatenate(dk_pairs, axis=1), *band_t)
        dv_band = jnp.concatenate(dv_pairs, axis=1)
        dproj_ref[:, OFF_K:OFF_V] = (dk_band[CHUNK:] + carry[:, 0:256]).astype(BF16)
        dproj_ref[:, OFF_V:OFF_ZB] = (dv_band[CHUNK:] + carry[:, 256:512]).astype(BF16)
        carry[:, 0:256] = dk_band[:CHUNK]
        carry[:, 256:512] = dv_band[:CHUNK]

        @pl.when(i == nb - 1)
        def _():
            t = lax.broadcasted_iota(jnp.int32, (CHUNK, CHUNK), 0)
            tp = lax.broadcasted_iota(jnp.int32, (CHUNK, CHUNK), 1)
            for g in range(A_GROUPS):
                dw_ref[g] = jnp.where(tp <= t, dw_ref[g], 0.0)
                misc_ref[pl.ds(ROW_DBSP + g, 1), :] = jnp.sum(dbcol[g].T, axis=0, keepdims=True)
            scatter = _chip_scatter(pair_ref, parts_ref, send_sems, recv_sems)
            for cp in scatter:
                cp.wait_recv()
            for cp in scatter:
                cp.wait_send()

    blk = sp["blk"]
    hbm = pl.BlockSpec(memory_space=pl.ANY)
    return pl.pallas_call(
        body, name="mixer_bwd",
        grid_spec=pltpu.PrefetchScalarGridSpec(
            num_scalar_prefetch=1, grid=(nb,),
            in_specs=[sp["cur"], sp["prev_kv"], pl.BlockSpec((CHUNK, D_MODEL), lambda i, me_ref: (blk(i), 0)),
                      *sp["tabs"], sp["vec"], sp["vec"], sp["wsp"], sp["bsp"], sp["smem"], hbm],
            out_specs=[pl.BlockSpec((CHUNK, D_IN), lambda i, me_ref: (blk(i), 0)),
                       pl.BlockSpec((8, D_A), lambda i, me_ref: (me_ref[0], 0)),
                       pl.BlockSpec((A_GROUPS, CHUNK, CHUNK), lambda i, me_ref: (me_ref[0], 0, 0)),
                       pl.BlockSpec((MISC_ROWS, 128), lambda i, me_ref: (me_ref[0], 0)), hbm],
            scratch_shapes=[pltpu.VMEM((A_GROUPS, CHUNK, CHUNK), F32), pltpu.VMEM((A_GROUPS, CHUNK, CHUNK), F32),
                            pltpu.VMEM((CHUNK, 512), F32), pltpu.VMEM((2 * CHUNK, 4 * CHUNK), F32),
                            *_scatter_scratch()]),
        out_shape=[SDS((s, D_IN), BF16), SDS((N_DEV * 8, D_A), F32), SDS((N_DEV * A_GROUPS, CHUNK, CHUNK), F32),
                   SDS((N_DEV * MISC_ROWS, 128), F32), SDS((3,) + pair.shape[1:], pair.dtype)],
        compiler_params=_params("arbitrary"),
    )(me, proj, proj, dy, *tabs, *tabs, ln_g, ln_b, w_sp, b_sp, sinks, pair)


def _wgrad(name, a, b, bm, gathers=()):
    s, m = a.shape
    n = b.shape[1]
    bt = 512
    steps = s // bt
    n_g = len(gathers)

    def body(*refs):
        a_ref, b_ref = refs[:2]
        out_ref, bufs = refs[2 + n_g], refs[3 + n_g:3 + 2 * n_g]
        acc, sems = refs[3 + 2 * n_g], refs[4 + 2 * n_g:]
        i, t = pl.program_id(0), pl.program_id(1)
        jobs = [_InPlaceGather(bufs[k], sems[2 * k], sems[2 * k + 1]) for k in range(n_g)]

        @pl.when((i == 0) & (t == 0))
        def _():
            for job in jobs:
                job.start()

        @pl.when((i == (m // bm) // 2) & (t == 0))
        def _():
            for job in jobs:
                for j in range(3):
                    job.pass_on(j)

        @pl.when(t == 0)
        def _():
            acc[...] = jnp.zeros_like(acc)

        acc[...] += lax.dot_general(a_ref[...], b_ref[...], TN, preferred_element_type=F32)

        @pl.when(t == steps - 1)
        def _():
            out_ref[...] = acc[...].astype(out_ref.dtype)

        @pl.when((i == m // bm - 1) & (t == steps - 1))
        def _():
            for job in jobs:
                job.wait_sibling(0)
                for j in range(3):
                    job.wait_sibling(4 + j)
                job.wait_sends()

    hbm = pl.BlockSpec(memory_space=pl.ANY)
    outs = pl.pallas_call(
        body, name=name, grid=(m // bm, steps),
        in_specs=[pl.BlockSpec((bt, bm), lambda i, t: (t, i)), pl.BlockSpec((bt, n), lambda i, t: (t, 0))] + [hbm] * n_g,
        out_specs=[pl.BlockSpec((bm, n), lambda i, t: (i, 0))] + [hbm] * n_g,
        out_shape=[SDS((m, n), BF16)] + [SDS(g.shape, g.dtype) for g in gathers],
        scratch_shapes=[pltpu.VMEM((bm, n), F32)] + _gather_scratch() * n_g,
        input_output_aliases={2 + k: 1 + k for k in range(n_g)},
        compiler_params=_params("arbitrary", "arbitrary"),
    )(a, b, *gathers)
    return outs[0], outs[1:]


def _in_proj_bwd(dproj, wt, x, dx1, scale, norm_g, pair):
    s = x.shape[0]
    tm, tk, tr = min(1024, s), D_IN // 4, 64
    ksteps = D_IN // tk

    def body(dp_ref, wt_ref, x_hbm, dx1_hbm, sc_ref, g_ref, pair_ref, gx_ref, sums_ref, parts_ref, x_buf, dx1_buf,
             tile_sems, send_sems, recv_sems):
        i, k = pl.program_id(0), pl.program_id(1)

        def tile_copies():
            rows = pl.ds(pl.multiple_of(i * tm, tm), tm)
            return (pltpu.make_async_copy(x_hbm.at[rows], x_buf, tile_sems.at[0]),
                    pltpu.make_async_copy(dx1_hbm.at[rows], dx1_buf, tile_sems.at[1]))

        @pl.when((i == 0) & (k == 0))
        def _():
            for cp in _chip_scatter(pair_ref, parts_ref, send_sems, recv_sems):
                cp.start()
            sums_ref[...] = jnp.zeros_like(sums_ref)

        @pl.when(k == 0)
        def _():
            for cp in tile_copies():
                cp.start()
            gx_ref[...] = jnp.dot(dp_ref[...], wt_ref[...], preferred_element_type=F32)

        @pl.when(k > 0)
        def _():
            gx_ref[...] += jnp.dot(dp_ref[...], wt_ref[...], preferred_element_type=F32)

        @pl.when(k == ksteps - 1)
        def _():
            for cp in tile_copies():
                cp.wait()
            one_sc, g = 1.0 + sc_ref[...], g_ref[...]
            cs = one_sc * g

            def chunk(j, sums):
                rows = pl.ds(pl.multiple_of(j * tr, tr), tr)
                dh, xv = gx_ref[rows, :], x_buf[rows, :]
                dhx = dh * xv
                r = lax.rsqrt(jnp.sum(xv * xv, axis=-1, keepdims=True) * (1.0 / D_MODEL) + NORM_EPS)
                coef = (r * r * r) * (jnp.sum(dhx * cs, axis=-1, keepdims=True) * (1.0 / D_MODEL))
                gx_ref[rows, :] = dx1_buf[rows, :] + r * (dh * cs) - xv * coef
                return (sums[0] + jnp.sum(dh, axis=0, keepdims=True), sums[1] + jnp.sum(dhx * r, axis=0, keepdims=True))

            zero = jnp.zeros((1, D_MODEL), F32)
            sums = lax.fori_loop(0, tm // tr, chunk, (zero, zero))
            sums_ref[0:1, :] += sums[0]
            sums_ref[1:2, :] += sums[1] * g
            sums_ref[2:3, :] += sums[1] * one_sc

        @pl.when((i == s // tm - 1) & (k == ksteps - 1))
        def _():
            scatter = _chip_scatter(pair_ref, parts_ref, send_sems, recv_sems)
            for cp in scatter:
                cp.wait_recv()
            for cp in scatter:
                cp.wait_send()

    row = pl.BlockSpec((1, D_MODEL), lambda i, k: (0, 0))
    hbm = pl.BlockSpec(memory_space=pl.ANY)
    return pl.pallas_call(
        body, name="in_proj_bwd", grid=(s // tm, ksteps),
        in_specs=[pl.BlockSpec((tm, tk), lambda i, k: (i, k)), pl.BlockSpec((tk, D_MODEL), lambda i, k: (k, 0)),
                  hbm, hbm, row, row, hbm],
        out_specs=[pl.BlockSpec((tm, D_MODEL), lambda i, k: (i, 0)), pl.BlockSpec((8, D_MODEL), lambda i, k: (0, 0)),
                   hbm],
        out_shape=[SDS((s, D_MODEL), F32), SDS((8, D_MODEL), F32), SDS((3,) + pair.shape[1:], pair.dtype)],
        scratch_shapes=[pltpu.VMEM((tm, D_MODEL), F32), pltpu.VMEM((tm, D_MODEL), F32),
                        pltpu.SemaphoreType.DMA((2,)), *_scatter_scratch()],
        compiler_params=_params("arbitrary", "arbitrary"),
    )(dproj, wt, x, dx1, scale, norm_g, pair)


def _pair_sum(core, grad, got):
    _, m, n = got.shape

    def body(core_ref, a_ref, b_ref, out_ref):
        out_ref[...] = (a_ref[...].astype(F32) + b_ref[...].astype(F32)).astype(BF16)

    blk = pl.BlockSpec((1, m, n), lambda q, core_ref: (q, 0, 0))
    return pl.pallas_call(
        body, name=f"pair_sum_{m}",
        grid_spec=pltpu.PrefetchScalarGridSpec(
            num_scalar_prefetch=1, grid=(4,),
            in_specs=[pl.BlockSpec((1, m, n), lambda q, core_ref: (2 * q + core_ref[0], 0, 0)), blk], out_specs=blk),
        out_shape=SDS(got.shape, BF16), compiler_params=_params("parallel"),
    )(core, grad, got)


def _sum_chips(own_ref, parts_ref):
    return ((own_ref[0].astype(F32) + parts_ref[0].astype(F32)) + parts_ref[1].astype(F32)) + parts_ref[2].astype(F32)


def _adam_rows(name, chip, pair, parts, w, m, v):
    rows = w.shape[0]
    tr = rows // 4

    def body(chip_ref, own_ref, p_ref, w_ref, m_ref, v_ref, g_ref, d_ref, nm_ref, nv_ref):
        g = _sum_chips(own_ref, p_ref)
        g_ref[...] = g
        d_ref[...], nm_ref[...], nv_ref[...] = _adamw(w_ref[...], g, m_ref[...], v_ref[...])

    blk = pl.BlockSpec((tr, D_MODEL), lambda j, chip_ref: (j, 0))
    return pl.pallas_call(
        body, name=name,
        grid_spec=pltpu.PrefetchScalarGridSpec(
            num_scalar_prefetch=1, grid=(rows // tr,),
            in_specs=[pl.BlockSpec((1, tr, D_MODEL), lambda j, chip_ref: (chip_ref[0], j, 0)),
                      pl.BlockSpec((3, tr, D_MODEL), lambda j, chip_ref: (0, j, 0)), blk, blk, blk],
            out_specs=[blk] * 4),
        out_shape=[SDS(w.shape, F32)] * 4, compiler_params=_params("parallel"),
    )(chip, pair, parts, w, m, v)


def _adam_ada(name, cact, dmod, w, m, v):
    n = w.shape[1]
    tr = 512

    def body(c_ref, dm_ref, w_ref, m_ref, v_ref, g_ref, d_ref, nm_ref, nv_ref):
        pad_c = jnp.concatenate([c_ref[...], jnp.zeros_like(c_ref)], axis=0).astype(BF16)
        pad_d = jnp.concatenate([dm_ref[...], jnp.zeros_like(dm_ref)], axis=0).astype(BF16)
        g = lax.dot_general(pad_c, pad_d, TN, preferred_element_type=F32)
        g_ref[...] = g
        d_ref[...], nm_ref[...], nv_ref[...] = _adamw(w_ref[...], g, m_ref[...], v_ref[...])

    blk = pl.BlockSpec((tr, n), lambda j: (j, 0))
    return pl.pallas_call(
        body, name=name, grid=(D_MODEL // tr,),
        in_specs=[pl.BlockSpec((N_DEV, tr), lambda j: (0, j)), pl.BlockSpec((N_DEV, n), lambda j: (0, 0)),
                  blk, blk, blk],
        out_specs=[blk] * 4, out_shape=[SDS(w.shape, F32)] * 4,
        compiler_params=_params("parallel"),
    )(cact, dmod, w, m, v)


SMALL_PARAMS = ("w_spatial", "b_spatial", "sinks", "norm_g", "ln_v_g", "ln_v_b", "final_norm_g", "b_ada", "b_ada_final")


def _adam_small(d_wsp, misc, d_ln, sums_i, sums_o, params):
    n_p = len(SMALL_PARAMS)

    def body(*refs):
        wsp_ref, misc_ref, ln_ref, si_ref, so_ref = refs[:5]
        wmv = [refs[5 + 3 * k:8 + 3 * k] for k in range(n_p)]
        loss_ref = refs[5 + 3 * n_p]
        outs = [refs[6 + 3 * n_p + 4 * k:10 + 3 * n_p + 4 * k] for k in range(n_p)]

        def total(ref, rows=None):
            def part(j):
                return ref[j] if rows is None else ref[j, rows[0]:rows[1], :]
            acc = part(0)
            for j in range(1, N_DEV):
                acc = acc + part(j)
            return acc

        sink_rows = total(misc_ref, (ROW_DSINKS, ROW_DSINKS + 16))
        diag = (lax.broadcasted_iota(jnp.int32, (16, 128), 0) == lax.broadcasted_iota(jnp.int32, (16, 128), 1))
        grads = dict(
            w_spatial=total(wsp_ref), b_spatial=total(misc_ref, (ROW_DBSP, ROW_DBSP + A_GROUPS)),
            sinks=jnp.sum(jnp.where(diag, sink_rows, 0.0), axis=0, keepdims=True),
            norm_g=total(si_ref, (2, 3)), ln_v_g=total(ln_ref, (0, 1)), ln_v_b=total(ln_ref, (1, 2)),
            final_norm_g=total(so_ref, (3, 4)),
            b_ada=jnp.concatenate([total(si_ref, (0, 1)), total(si_ref, (1, 2)), total(so_ref, (0, 1))], axis=1),
            b_ada_final=jnp.concatenate([total(so_ref, (1, 2)), total(so_ref, (2, 3))], axis=1))
        sq_err = jnp.sum(total(so_ref, (4, 5)), axis=1, keepdims=True)
        loss_ref[...] = jnp.broadcast_to(sq_err * (0.5 / D_MODEL), (1, 128))
        for k, name in enumerate(SMALL_PARAMS):
            w_ref, m_ref, v_ref = wmv[k]
            g_ref, d_ref, nm_ref, nv_ref = outs[k]
            g_ref[...] = grads[name]
            d_ref[...], nm_ref[...], nv_ref[...] = _adamw(w_ref[...], grads[name], m_ref[...], v_ref[...])

    flat = [a for name in SMALL_PARAMS for a in params[name]]
    vmem = pl.BlockSpec(memory_space=pltpu.VMEM)
    out_shape = [SDS((1, 128), F32)] + [SDS(params[name][0].shape, F32) for name in SMALL_PARAMS for _ in range(4)]
    outs = pl.pallas_call(
        body, name="adam_small", in_specs=[vmem] * (5 + len(flat)), out_specs=[vmem] * len(out_shape),
        out_shape=out_shape, compiler_params=_params(),
    )(d_wsp, misc, d_ln, sums_i, sums_o, *flat)
    return outs[0], {name: outs[1 + 4 * k:5 + 4 * k] for k, name in enumerate(SMALL_PARAMS)}


def kernel(x, c, w_ada, b_ada, norm_g, w_in, ln_v_g, ln_v_b, w_spatial, b_spatial, sinks, w_out, w_ada_final, b_ada_final, final_norm_g, loss_target, m_w_ada, m_b_ada, m_norm_g, m_w_in, m_ln_v_g, m_ln_v_b, m_w_spatial, m_b_spatial, m_sinks, m_w_out, m_w_ada_final, m_b_ada_final, m_final_norm_g, v_w_ada, v_b_ada, v_norm_g, v_w_in, v_ln_v_g, v_ln_v_b, v_w_spatial, v_b_spatial, v_sinks, v_w_out, v_w_ada_final, v_b_ada_final, v_final_norm_g):
    seq = x.shape[1]
    me = 4 * lax.axis_index("x") + 2 * lax.axis_index("y") + lax.axis_index("c")
    x2, tgt = x[0], loss_target[0]
    fng = final_norm_g.reshape(1, D_MODEL)

    n_ada, n_ada_f = w_ada.shape[2], w_ada_final.shape[1]
    cact, mod, mod_f = _ada_exchange(c, w_ada[0], b_ada.reshape(N_DEV, n_ada), w_ada_final,
                                     b_ada_final.reshape(N_DEV, n_ada_f))
    cact = cact.reshape(N_DEV, D_MODEL)
    mod, mod_f = mod.reshape(1, 3 * D_MODEL), mod_f.reshape(1, 2 * D_MODEL)
    shift, scale, gate = mod[:, :D_MODEL], mod[:, D_MODEL:2 * D_MODEL], mod[:, 2 * D_MODEL:]
    shift_f, scale_f = mod_f[:, :D_MODEL], mod_f[:, D_MODEL:]

    wt_f32, m_wt, v_wt = (jnp.swapaxes(a, 1, 2)[0] for a in (w_in, m_w_in, v_w_in))
    xi, yi = lax.axis_index("x"), lax.axis_index("y")
    chip_order = jnp.stack([2 * xi + yi, 2 * (1 - xi) + yi, 2 * xi + 1 - yi, 2 * (1 - xi) + 1 - yi]).astype(jnp.int32)
    wt_mine, wo_mine = _prep_weights(me.reshape(1), wt_f32, w_out[0])

    tabs = _rope_tables(seq)
    sinks_v = sinks.reshape(16)
    h, proj, wt = _gather_in_proj(chip_order, x2, shift, scale, norm_g, wt_mine)
    y, wo = _mixer_fwd(proj, tabs, ln_v_g, ln_v_b, w_spatial[0], b_spatial[0], sinks_v, wo_mine)
    dx1, do, dy, sums_o = _out_proj_loss(y, x2, tgt, wo, gate, shift_f, scale_f, fng)

    core = lax.axis_index("c").reshape(1)
    chip = (2 * lax.axis_index("x") + lax.axis_index("y")).reshape(1)
    g_wo, _ = _wgrad("wgrad_out", y, do, 1024)
    g_wo = g_wo.reshape(N_DEV, D_MODEL // N_DEV, D_MODEL)
    pair_out = _pair_sum(core, g_wo, _rs_pair("rs_pair_out", g_wo))
    dproj, d_ln, d_wsp, misc, parts_out = _mixer_bwd(
        me.reshape(1), proj, dy, tabs, ln_v_g, ln_v_b, w_spatial[0], b_spatial[0], sinks_v, pair_out)
    g_wt, (d_ln, d_wsp, misc) = _wgrad("wgrad_in", dproj, h, 1408,
                                       gathers=(d_ln, d_wsp.reshape(N_DEV * A_GROUPS * CHUNK, CHUNK), misc))
    g_wt = g_wt.reshape(N_DEV, D_IN // N_DEV, D_MODEL)
    pair_in = _pair_sum(core, g_wt, _rs_pair("rs_pair_in", g_wt))
    grad_x, sums_i, parts_in = _in_proj_bwd(dproj, wt, x2, dx1, scale, norm_g, pair_in)
    wt_leaves = [jnp.swapaxes(a[None], 1, 2) for a in _adam_rows("adam_w_in", chip, pair_in, parts_in, wt_f32, m_wt, v_wt)]
    w_out_leaves = [a[None] for a in _adam_rows("adam_w_out", chip, pair_out, parts_out, w_out[0], m_w_out[0], v_w_out[0])]

    sums_i, sums_o = _all_gather("gather_sums", [sums_i, sums_o], pltpu.VMEM)
    natural = dict(w_spatial=(A_GROUPS * CHUNK, CHUNK), b_spatial=(A_GROUPS, CHUNK), sinks=(1, 16), norm_g=(1, D_MODEL),
                   ln_v_g=(1, D_A), ln_v_b=(1, D_A), final_norm_g=(1, D_MODEL), b_ada=(1, 3 * D_MODEL),
                   b_ada_final=(1, 2 * D_MODEL))
    given = dict(
        w_spatial=(w_spatial, m_w_spatial, v_w_spatial), b_spatial=(b_spatial, m_b_spatial, v_b_spatial),
        sinks=(sinks, m_sinks, v_sinks), norm_g=(norm_g, m_norm_g, v_norm_g), ln_v_g=(ln_v_g, m_ln_v_g, v_ln_v_g),
        ln_v_b=(ln_v_b, m_ln_v_b, v_ln_v_b), final_norm_g=(final_norm_g, m_final_norm_g, v_final_norm_g),
        b_ada=(b_ada, m_b_ada, v_b_ada), b_ada_final=(b_ada_final, m_b_ada_final, v_b_ada_final))
    params = {name: tuple(a.reshape(natural[name]) for a in given[name]) for name in SMALL_PARAMS}
    params["sinks"] = tuple(jnp.pad(a, ((0, 0), (0, 128 - 16))) for a in params["sinks"])
    loss, small = _adam_small(d_wsp.reshape(N_DEV, A_GROUPS * CHUNK, CHUNK), misc.reshape(N_DEV, MISC_ROWS, 128),
                              d_ln.reshape(N_DEV, 8, D_A), sums_i, sums_o, params)
    small["sinks"] = [a[:, :16] for a in small["sinks"]]
    small = {name: [a.reshape(given[name][0].shape) for a in small[name]] for name in SMALL_PARAMS}

    dmod_all = jnp.concatenate([sums_i[:, 0], sums_i[:, 1], sums_o[:, 0]], axis=1)
    dmod_f_all = jnp.concatenate([sums_o[:, 1], sums_o[:, 2]], axis=1)
    dmod_mine = lax.dynamic_slice_in_dim(dmod_all, me * n_ada, n_ada, axis=1)
    dmod_f_mine = lax.dynamic_slice_in_dim(dmod_f_all, me * n_ada_f, n_ada_f, axis=1)
    ada = _adam_ada("adam_w_ada", cact, dmod_mine, w_ada[0], m_w_ada[0], v_w_ada[0])
    ada_f = _adam_ada("adam_w_ada_final", cact, dmod_f_mine, w_ada_final, m_w_ada_final, v_w_ada_final)

    def leaves(k):
        return (ada[k][None], small["b_ada"][k], small["norm_g"][k], wt_leaves[k], small["ln_v_g"][k],
                small["ln_v_b"][k], small["w_spatial"][k], small["b_spatial"][k], small["sinks"][k], w_out_leaves[k],
                ada_f[k], small["b_ada_final"][k], small["final_norm_g"][k])

    return (loss[0, 0], grad_x[None], *leaves(0), *leaves(1), *leaves(2), *leaves(3))
```

```python
import functools

import jax
import jax.numpy as jnp
from jax import lax
from jax.experimental import pallas as pl
from jax.experimental.pallas import tpu as pltpu

D_MODEL = 2048
D_IN = 5632
D_A = 1024
CHUNK = 128
A_GROUPS = 8
HEAD_DIM = 64
N_KV_HEADS = 4
N_DEV = 8
ROPE_THETA = 10000.0
NORM_EPS = 1e-5
ATTN_SCALE = HEAD_DIM ** -0.5

ADAM_LR = 0.001
ADAM_B1 = 0.9
ADAM_B2 = 0.999
ADAM_EPS = 1e-08
ADAM_WD = 0.01
ADAM_STEP = 10

OFF_U, OFF_VA, OFF_ZA, OFF_Q, OFF_K, OFF_V, OFF_ZB = 0, 1024, 2048, 3072, 4096, 4352, 4608

V7X_VMEM_LIMIT_BYTES = 56 * 1024 * 1024

F32 = jnp.float32
BF16 = jnp.bfloat16
MESH = pl.DeviceIdType.MESH
SDS = jax.ShapeDtypeStruct
NT = (((1,), (1,)), ((), ()))
TN = (((0,), (0,)), ((), ()))


def _params(*semantics):
    return pltpu.CompilerParams(dimension_semantics=semantics or None, vmem_limit_bytes=V7X_VMEM_LIMIT_BYTES)


def _mesh_pos():
    return lax.axis_index("x"), lax.axis_index("y"), lax.axis_index("c")


def _sigmoid(z):
    return 1.0 / (1.0 + jnp.exp(-z))


def _adamw(w, g, m, v):
    m = ADAM_B1 * m + (1.0 - ADAM_B1) * g
    v = ADAM_B2 * v + (1.0 - ADAM_B2) * (g * g)
    m_hat = m / (1.0 - ADAM_B1 ** ADAM_STEP)
    v_hat = v / (1.0 - ADAM_B2 ** ADAM_STEP)
    delta = -ADAM_LR * (m_hat / (jnp.sqrt(v_hat) + ADAM_EPS) + ADAM_WD * w)
    return delta, m, v


def _all_gather(name, blocks, memory_space):
    n_arr = len(blocks)

    def body(*refs):
        ins, outs = refs[:n_arr], refs[n_arr:2 * n_arr]
        send_sems, recv_sems, local_sems = refs[2 * n_arr:]
        x, y, c = _mesh_pos()
        me, sibling = (x, y, c), (x, y, 1 - c)
        chips = [(1 - x, y), (x, 1 - y), (1 - x, 1 - y)]

        def slot(p):
            return 4 * p[0] + 2 * p[1] + p[2]

        def copy(a, k, block, to, src=None):
            dst = outs[a].at[slot(block)]
            return pltpu.make_async_remote_copy(
                src_ref=dst if src is None else src, dst_ref=dst,
                send_sem=send_sems.at[a, k], recv_sem=recv_sems.at[a, k],
                device_id=to, device_id_type=MESH)

        mine = [pltpu.make_async_copy(ins[a], outs[a].at[slot(me)], local_sems.at[a]) for a in range(n_arr)]
        for cp in mine:
            cp.start()
        first = []
        for a in range(n_arr):
            first.append(copy(a, 0, me, sibling, src=ins[a]))
            first += [copy(a, 1 + j, me, (*chip, c), src=ins[a]) for j, chip in enumerate(chips)]
        for cp in first:
            cp.start()
        passed = []
        for j, chip in enumerate(chips):
            for a in range(n_arr):
                copy(a, 1 + j, (*chip, c), me).wait_recv()
                fwd = copy(a, 4 + j, (*chip, c), sibling)
                fwd.start()
                passed.append(fwd)
        for a in range(n_arr):
            copy(a, 0, sibling, me).wait_recv()
            for j, chip in enumerate(chips):
                copy(a, 4 + j, (*chip, 1 - c), me).wait_recv()
        for cp in first + passed:
            cp.wait_send()
        for cp in mine:
            cp.wait()

    spec = pl.BlockSpec(memory_space=memory_space)
    return pl.pallas_call(
        body, name=name,
        out_shape=[SDS((N_DEV,) + b.shape, b.dtype) for b in blocks],
        in_specs=[spec] * n_arr, out_specs=[spec] * n_arr,
        scratch_shapes=[pltpu.SemaphoreType.DMA((n_arr, 7)), pltpu.SemaphoreType.DMA((n_arr, 7)),
                        pltpu.SemaphoreType.DMA((n_arr,))],
        compiler_params=_params(),
    )(*blocks)


def _ada_exchange(c, w_ada, b_ada8, w_ada_f, b_ada_f8):
    n1, n2 = w_ada.shape[1], w_ada_f.shape[1]

    def body(c_ref, w1_ref, b1_ref, w2_ref, b2_ref, cact_ref, mod_ref, modf_ref,
             cact_buf, res1, res2, send1, send2, sems_s, sems_r):
        x, y, c_pos = _mesh_pos()
        me = 4 * x + 2 * y + c_pos
        flips = [(k >> 2 & 1, k >> 1 & 1, k & 1) for k in range(1, N_DEV)]

        def peer(f):
            return (1 - x if f[0] else x, 1 - y if f[1] else y, 1 - c_pos if f[2] else c_pos)

        cv = c_ref[...]
        cact = cv * _sigmoid(cv)
        cact_buf[...] = cact
        cact_ref[me] = cact

        def rdma(phase, k, src, dst, f):
            return pltpu.make_async_remote_copy(src_ref=src, dst_ref=dst, send_sem=sems_s.at[phase, k],
                                                recv_sem=sems_r.at[phase, k], device_id=peer(f), device_id_type=MESH)

        gather = [rdma(0, k, cact_buf, cact_ref.at[me], f) for k, f in enumerate(flips)]
        for cp in gather:
            cp.start()
        for cp in gather:
            cp.wait_recv()
        for cp in gather:
            cp.wait_send()

        rid = lax.broadcasted_iota(jnp.int32, (N_DEV, D_MODEL), 0)
        rows = jnp.zeros((N_DEV, D_MODEL), F32)
        for j in range(N_DEV):
            rows = jnp.where(rid == j, jnp.broadcast_to(cact_ref[j], (N_DEV, D_MODEL)), rows)
        rows = rows.astype(BF16)
        res1[...] = jnp.dot(rows, w1_ref[...].astype(BF16), preferred_element_type=F32) + b1_ref[pl.ds(me, 1), :]
        res2[...] = jnp.dot(rows, w2_ref[...].astype(BF16), preferred_element_type=F32) + b2_ref[pl.ds(me, 1), :]
        for j in range(N_DEV):
            send1[j] = res1[pl.ds(j, 1), :]
            send2[j] = res2[pl.ds(j, 1), :]
        mod_ref[me] = send1[me]
        modf_ref[me] = send2[me]
        scatter = []
        for k, f in enumerate(flips):
            to = me ^ (k + 1)
            scatter.append(rdma(1, k, send1.at[to], mod_ref.at[me], f))
            scatter.append(rdma(2, k, send2.at[to], modf_ref.at[me], f))
        for cp in scatter:
            cp.start()
        for cp in scatter:
            cp.wait_recv()
        for cp in scatter:
            cp.wait_send()

    vmem = pl.BlockSpec(memory_space=pltpu.VMEM)
    return pl.pallas_call(
        body, name="ada_exchange",
        out_shape=[SDS((N_DEV, 1, D_MODEL), F32), SDS((N_DEV, 1, n1), F32), SDS((N_DEV, 1, n2), F32)],
        in_specs=[vmem] * 5, out_specs=[vmem] * 3,
        scratch_shapes=[pltpu.VMEM((1, D_MODEL), F32), pltpu.VMEM((N_DEV, n1), F32), pltpu.VMEM((N_DEV, n2), F32),
                        pltpu.VMEM((N_DEV, 1, n1), F32), pltpu.VMEM((N_DEV, 1, n2), F32),
                        pltpu.SemaphoreType.DMA((3, 7)), pltpu.SemaphoreType.DMA((3, 7))],
        compiler_params=_params(),
    )(c, w_ada, b_ada8, w_ada_f, b_ada_f8)


def _rs_pair(name, grad):
    def body(g_ref, got_ref, send_sems, recv_sems):
        x, y, c = _mesh_pos()
        copies = [pltpu.make_async_remote_copy(
            src_ref=g_ref.at[2 * q + 1 - c], dst_ref=got_ref.at[q], send_sem=send_sems.at[q], recv_sem=recv_sems.at[q],
            device_id=(x, y, 1 - c), device_id_type=MESH) for q in range(4)]
        for cp in copies:
            cp.start()
        for cp in copies:
            cp.wait_recv()
        for cp in copies:
            cp.wait_send()

    hbm = pl.BlockSpec(memory_space=pl.ANY)
    return pl.pallas_call(
        body, name=name, out_shape=SDS((4,) + grad.shape[1:], grad.dtype), in_specs=[hbm], out_specs=hbm,
        scratch_shapes=[pltpu.SemaphoreType.DMA((4,)), pltpu.SemaphoreType.DMA((4,))],
        compiler_params=_params(),
    )(grad)


def _chip_scatter(pair_ref, parts_ref, send_sems, recv_sems):
    x, y, c = _mesh_pos()
    chips = [(1 - x, y), (x, 1 - y), (1 - x, 1 - y)]
    return [pltpu.make_async_remote_copy(
        src_ref=pair_ref.at[2 * cx + cy], dst_ref=parts_ref.at[j], send_sem=send_sems.at[j], recv_sem=recv_sems.at[j],
        device_id=(cx, cy, c), device_id_type=MESH) for j, (cx, cy) in enumerate(chips)]


def _scatter_scratch():
    return [pltpu.SemaphoreType.DMA((3,)), pltpu.SemaphoreType.DMA((3,))]


def _prep_weights(me, wt, w_out):
    steps = 4

    def body(me_ref, wt_ref, wo_ref, wtb_ref, wob_ref):
        wtb_ref[...] = wt_ref[...].astype(BF16)
        wob_ref[...] = wo_ref[...].astype(BF16)

    def rows(a, mine):
        blk = (a.shape[0] // steps, a.shape[1])
        return pl.BlockSpec(blk, (lambda i, me_ref: (steps * me_ref[0] + i, 0)) if mine else (lambda i, me_ref: (i, 0)))

    return pl.pallas_call(
        body, name="prep_weights",
        grid_spec=pltpu.PrefetchScalarGridSpec(
            num_scalar_prefetch=1, grid=(steps,),
            in_specs=[rows(wt, False), rows(w_out, False)], out_specs=[rows(wt, True), rows(w_out, True)]),
        out_shape=[SDS((N_DEV * wt.shape[0], D_MODEL), BF16), SDS((N_DEV * w_out.shape[0], D_MODEL), BF16)],
        compiler_params=_params("parallel"),
    )(me, wt, w_out)


class _InPlaceGather:
    def __init__(self, buf_ref, send_sems, recv_sems, relay=False):
        self.buf, self.send_sems, self.recv_sems, self.relay = buf_ref, send_sems, recv_sems, relay
        self.n = buf_ref.shape[0] // N_DEV
        x, y, c = _mesh_pos()
        self.me, self.sibling, self.core = (x, y, c), (x, y, 1 - c), c
        self.chips = [(1 - x, y), (x, 1 - y), (1 - x, 1 - y)]
        self.relay_from = (jnp.where(c == 0, 1 - x, x), jnp.where(c == 0, y, 1 - y), c)
        self.relay_to = (jnp.where(c == 0, x, 1 - x), jnp.where(c == 0, 1 - y, y), c)

    def copy(self, k, block, to):
        start = pl.multiple_of((4 * block[0] + 2 * block[1] + block[2]) * self.n, self.n)
        rows = self.buf.at[pl.ds(start, self.n)]
        return pltpu.make_async_remote_copy(src_ref=rows, dst_ref=rows, send_sem=self.send_sems.at[k],
                                            recv_sem=self.recv_sems.at[k], device_id=to, device_id_type=MESH)

    def start(self):
        self.copy(0, self.me, self.sibling).start()
        for j, chip in enumerate(self.chips[:2] if self.relay else self.chips):
            self.copy(1 + j, self.me, (*chip, self.core)).start()

    def relay_diagonal(self):
        self.copy(3, self.relay_from, self.relay_to).start()

    def pass_on(self, j):
        self.copy(1 + j, (*self.chips[j], self.core), self.me).wait_recv()
        self.copy(4 + j, (*self.chips[j], self.core), self.sibling).start()

    def wait_sibling(self, k):
        self.copy(k, self.sibling, self.me).wait_recv()

    def wait_sends(self):
        for k in range(7):
            self.copy(k, self.me, self.sibling).wait_send()


def _gather_scratch():
    return [pltpu.SemaphoreType.DMA((7,)), pltpu.SemaphoreType.DMA((7,))]


def _gather_in_proj(order, x, shift, scale, norm_g, wt_all):
    s = x.shape[0]
    th = tm = min(512, s)
    nh, ni = s // th, s // tm
    tn = D_IN // 4
    steps = nh + 4 * ni

    def body(order_ref, x_ref, shift_ref, scale_ref, g_ref, wt_in, h_ref, proj_ref, wt_ref,
             h_scr, w_buf, load_sems, send_sems, recv_sems):
        g = pl.program_id(0)
        gather = _InPlaceGather(wt_ref, send_sems, recv_sems, relay=True)

        def tile_load(slot, chip):
            return pltpu.make_async_copy(wt_ref.at[pl.ds(pl.multiple_of(chip * tn, tn), tn)], w_buf.at[slot],
                                         load_sems.at[slot])

        @pl.when(g == 0)
        def _():
            gather.start()

        @pl.when(g < nh)
        def _():
            xv = x_ref[...]
            r = lax.rsqrt(jnp.mean(xv * xv, axis=-1, keepdims=True) + NORM_EPS)
            hb = (((xv * r) * g_ref[...]) * (1.0 + scale_ref[...]) + shift_ref[...]).astype(BF16)
            h_ref[...] = hb
            h_scr[pl.ds(pl.multiple_of(g * th, th), th), :] = hb

        @pl.when(g == nh - 1)
        def _():
            gather.wait_sibling(0)
            tile_load(0, order_ref[0]).start()

        @pl.when(g >= nh)
        def _():
            t, i = (g - nh) // ni, (g - nh) % ni

            @pl.when(i == 0)
            def _():
                tile_load(t % 2, order_ref[t]).wait()

            @pl.when((i == ni - 1) & (t == 0))
            def _():
                gather.pass_on(0)
                gather.pass_on(1)
                gather.relay_diagonal()

            @pl.when((i == ni - 1) & (t == 2))
            def _():
                gather.pass_on(2)

            for j in range(3):
                @pl.when((i == ni - 1) & (t == j))
                def _():
                    gather.wait_sibling(4 + j)
                    tile_load((j + 1) % 2, order_ref[j + 1]).start()

            lhs = h_scr[pl.ds(pl.multiple_of(i * tm, tm), tm), :]
            proj_ref[...] = lax.dot_general(lhs, w_buf[t % 2], NT, preferred_element_type=F32).astype(BF16)

        @pl.when(g == steps - 1)
        def _():
            gather.wait_sends()

    def h_tile(g, order_ref):
        return (jnp.minimum(g, nh - 1), 0)

    def proj_tile(g, order_ref):
        mm = jnp.maximum(g - nh, 0)
        return (mm % ni, order_ref[mm // ni])

    row = pl.BlockSpec((1, D_MODEL), lambda g, order_ref: (0, 0))
    hbm = pl.BlockSpec(memory_space=pl.ANY)
    return pl.pallas_call(
        body, name="gather_in_proj",
        grid_spec=pltpu.PrefetchScalarGridSpec(
            num_scalar_prefetch=1, grid=(steps,),
            in_specs=[pl.BlockSpec((th, D_MODEL), h_tile), row, row, row, hbm],
            out_specs=[pl.BlockSpec((th, D_MODEL), h_tile), pl.BlockSpec((tm, tn), proj_tile), hbm],
            scratch_shapes=[pltpu.VMEM((s, D_MODEL), BF16), pltpu.VMEM((2, tn, D_MODEL), BF16),
                            pltpu.SemaphoreType.DMA((2,)), *_gather_scratch()]),
        out_shape=[SDS((s, D_MODEL), BF16), SDS((s, D_IN), BF16), SDS(wt_all.shape, BF16)],
        input_output_aliases={5: 2},
        compiler_params=_params("arbitrary"),
    )(order, x, shift, scale, norm_g, wt_all)


def _rope_tables(seq):
    inv_freq = ROPE_THETA ** (-jnp.arange(0, HEAD_DIM, 2, dtype=F32) / HEAD_DIM)
    ang = jnp.arange(seq, dtype=F32)[:, None] * inv_freq[None, :]
    cos, sin, zero = jnp.cos(ang), jnp.sin(ang), jnp.zeros_like(ang)
    return (jnp.concatenate([cos] * 4, axis=1), jnp.concatenate([-sin, zero, -sin, zero], axis=1),
            jnp.concatenate([zero, sin, zero, sin], axis=1))


def _rope(v, cos, sin_lo, sin_hi):
    width = v.shape[1]
    rep = (1, width // 128)
    return (v * jnp.tile(cos, rep) + pltpu.roll(v, width - 32, 1) * jnp.tile(sin_lo, rep)
            + pltpu.roll(v, 32, 1) * jnp.tile(sin_hi, rep))


def _rope_bwd(d, cos, sin_lo, sin_hi):
    width = d.shape[1]
    rep = (1, width // 128)
    return (d * jnp.tile(cos, rep) + pltpu.roll(d * jnp.tile(sin_lo, rep), 32, 1)
            + pltpu.roll(d * jnp.tile(sin_hi, rep), width - 32, 1))


def _layer_norm(v, g, b):
    mu = jnp.mean(v, axis=-1, keepdims=True)
    vc = v - mu
    rstd = lax.rsqrt(jnp.mean(vc * vc, axis=-1, keepdims=True) + NORM_EPS)
    vhat = vc * rstd
    return vhat * g + b, vhat, rstd


def _tril_bf16(w_ref, g):
    t = lax.broadcasted_iota(jnp.int32, (CHUNK, CHUNK), 0)
    tp = lax.broadcasted_iota(jnp.int32, (CHUNK, CHUNK), 1)
    return jnp.where(tp <= t, w_ref[g], 0.0).astype(BF16)


def _bias_columns(b_ref, out_ref):
    for g in range(A_GROUPS):
        out_ref[g] = jnp.broadcast_to(b_ref[pl.ds(g, 1), :], (CHUNK, CHUNK)).T


def _band_mask():
    kj = lax.broadcasted_iota(jnp.int32, (2 * CHUNK, 4 * CHUNK), 0)
    qi = lax.broadcasted_iota(jnp.int32, (2 * CHUNK, 4 * CHUNK), 1) & (CHUNK - 1)
    rel = qi + CHUNK - kj
    return jnp.where((rel >= 0) & (rel < CHUNK), 0.0, -jnp.inf)


def _low_lanes():
    return lax.broadcasted_iota(jnp.int32, (1, 128), 1) < HEAD_DIM


def _stack_heads(pair_a, pair_b):
    lo = _low_lanes()
    return jnp.concatenate([jnp.where(lo, pair_a, 0.0), jnp.where(lo, 0.0, pair_a),
                            jnp.where(lo, pair_b, 0.0), jnp.where(lo, 0.0, pair_b)], axis=0).astype(BF16)


def _heads_to_lanes(per_group):
    rows = [t[:, r * CHUNK:(r + 1) * CHUNK] for t in per_group for r in range(4)]
    return jnp.concatenate(rows, axis=0).T


def _dup_kv_head(band, gk):
    pair = band[:, (gk // 2) * 128:(gk // 2 + 1) * 128]
    lo = _low_lanes()
    one = jnp.where(lo if gk % 2 == 0 else jnp.logical_not(lo), pair, 0.0)
    return (one + pltpu.roll(one, HEAD_DIM, 1)).astype(BF16)


def _fold_kv_head(dup_grad, gk):
    both = dup_grad + pltpu.roll(dup_grad, HEAD_DIM, 1)
    lo = _low_lanes()
    return jnp.where(lo if gk % 2 == 0 else jnp.logical_not(lo), both, 0.0)


def _attn_probs(q_st, k_dup, sink_row, mask, first_block):
    s = lax.dot_general(k_dup, q_st, NT, preferred_element_type=F32) + mask
    s = jnp.concatenate([jnp.where(first_block, -jnp.inf, s[:CHUNK]), s[CHUNK:]], axis=0)
    m = jnp.maximum(jnp.max(s, axis=0, keepdims=True), sink_row)
    p = jnp.exp(s - m)
    e_sink = jnp.exp(sink_row - m)
    inv = 1.0 / (jnp.sum(p, axis=0, keepdims=True) + e_sink)
    return p * inv, e_sink * inv


def _sink_row(sinks_ref, gk):
    return jnp.concatenate([jnp.full((1, CHUNK), sinks_ref[4 * gk + r], F32) for r in range(4)], axis=1)


def _mixer_specs(nb, rev):
    def blk(i):
        return nb - 1 - i if rev else i

    def prev(i):
        return jnp.maximum(blk(i) - 1, 0)

    tab = pl.BlockSpec((CHUNK, 128), lambda i, *_: (blk(i), 0))
    tab_prev = pl.BlockSpec((CHUNK, 128), lambda i, *_: (prev(i), 0))
    return dict(
        cur=pl.BlockSpec((CHUNK, D_IN), lambda i, *_: (blk(i), 0)),
        prev_kv=pl.BlockSpec((CHUNK, 2 * 256), lambda i, *_: (prev(i), OFF_K // 512)),
        tabs=[tab] * 3 + [tab_prev] * 3,
        vec=pl.BlockSpec((1, D_A), lambda i, *_: (0, 0)),
        wsp=pl.BlockSpec((A_GROUPS, CHUNK, CHUNK), lambda i, *_: (0, 0, 0)),
        bsp=pl.BlockSpec((A_GROUPS, CHUNK), lambda i, *_: (0, 0)),
        smem=pl.BlockSpec(memory_space=pltpu.SMEM),
        blk=blk,
    )


def _mixer_fwd(proj, tabs, ln_g, ln_b, w_sp, b_sp, sinks, wo_all):
    s = proj.shape[0]
    nb = s // CHUNK
    sp = _mixer_specs(nb, rev=False)

    def body(cur_ref, pkv_ref, c_ref, s1_ref, s2_ref, cp_ref, s1p_ref, s2p_ref, lg_ref, lb_ref, w_ref, b_ref,
             sinks_ref, wo_in, y_ref, wo_ref, bcol, mask, send_sems, recv_sems):
        i = pl.program_id(0)
        gather = _InPlaceGather(wo_ref, send_sems, recv_sems)

        @pl.when(i == 0)
        def _():
            gather.start()
            _bias_columns(b_ref, bcol)
            mask[...] = _band_mask()

        @pl.when(i == nb // 2)
        def _():
            for j in range(3):
                gather.pass_on(j)

        vln, _, _ = _layer_norm(cur_ref[:, OFF_VA:OFF_ZA].astype(F32), lg_ref[...], lb_ref[...])
        vln = vln.astype(BF16)
        for g in range(A_GROUPS):
            cols = slice(g * 128, (g + 1) * 128)
            sg = jnp.dot(_tril_bf16(w_ref, g), vln[:, cols], preferred_element_type=F32) + bcol[g]
            u = cur_ref[:, OFF_U + g * 128:OFF_U + (g + 1) * 128].astype(F32)
            z = cur_ref[:, OFF_ZA + g * 128:OFF_ZA + (g + 1) * 128].astype(F32)
            y_ref[:, cols] = (u * sg * (z * _sigmoid(z))).astype(BF16)

        cur_t = (c_ref[...], s1_ref[...], s2_ref[...])
        prev_t = (cp_ref[...], s1p_ref[...], s2p_ref[...])
        qr = _rope(cur_ref[:, OFF_Q:OFF_K].astype(F32), *cur_t) * ATTN_SCALE
        kr = jnp.concatenate([_rope(pkv_ref[:, 0:256].astype(F32), *prev_t),
                              _rope(cur_ref[:, OFF_K:OFF_V].astype(F32), *cur_t)], axis=0)
        v_t = jnp.concatenate([pkv_ref[:, 256:512], cur_ref[:, OFF_V:OFF_ZB]], axis=0).astype(F32).T.astype(BF16)
        outs = []
        for gk in range(N_KV_HEADS):
            q_st = _stack_heads(qr[:, (2 * gk) * 128:(2 * gk + 1) * 128], qr[:, (2 * gk + 1) * 128:(2 * gk + 2) * 128])
            probs, _ = _attn_probs(q_st, _dup_kv_head(kr, gk), _sink_row(sinks_ref, gk), mask[...], i == 0)
            outs.append(jnp.dot(v_t[gk * HEAD_DIM:(gk + 1) * HEAD_DIM], probs.astype(BF16),
                                preferred_element_type=F32))
        zb = cur_ref[:, OFF_ZB:D_IN].astype(F32)
        y_ref[:, D_A:D_MODEL] = (_heads_to_lanes(outs) * (zb * _sigmoid(zb))).astype(BF16)

        @pl.when(i == nb - 1)
        def _():
            gather.wait_sibling(0)
            for j in range(3):
                gather.wait_sibling(4 + j)
            gather.wait_sends()

    hbm = pl.BlockSpec(memory_space=pl.ANY)
    return pl.pallas_call(
        body, name="mixer_fwd", grid=(nb,),
        in_specs=[sp["cur"], sp["prev_kv"], *sp["tabs"], sp["vec"], sp["vec"], sp["wsp"], sp["bsp"], sp["smem"], hbm],
        out_specs=[pl.BlockSpec((CHUNK, D_MODEL), lambda i: (i, 0)), hbm],
        out_shape=[SDS((s, D_MODEL), BF16), SDS(wo_all.shape, wo_all.dtype)],
        scratch_shapes=[pltpu.VMEM((A_GROUPS, CHUNK, CHUNK), F32), pltpu.VMEM((2 * CHUNK, 4 * CHUNK), F32),
                        *_gather_scratch()],
        input_output_aliases={13: 1},
        compiler_params=_params("arbitrary"),
    )(proj, proj, *tabs, *tabs, ln_g, ln_b, w_sp, b_sp, sinks, wo_all)


def _out_proj_loss(y, x, target, wo, gate, shift_f, scale_f, fng):
    s = y.shape[0]
    tm, tr = 256, 128
    nt = s // tm

    def body(y_ref, x_ref, t_ref, wo_ref, gate_ref, sh_ref, sc_ref, g_ref, dx1_ref, do_ref, dy_ref, sums_ref,
             do_last, do_work):
        i = pl.program_id(0)

        @pl.when(i == 0)
        def _():
            sums_ref[...] = jnp.zeros_like(sums_ref)
            do_last[...] = jnp.zeros_like(do_last)

        do_work[...] = do_last[...]
        o = jnp.dot(y_ref[...], wo_ref[...], preferred_element_type=F32)
        gate, g, sh = gate_ref[...], g_ref[...], sh_ref[...]
        one_sc = 1.0 + sc_ref[...]
        cs, inv_d = g * one_sc, 1.0 / D_MODEL

        def rowsum(v):
            return jnp.sum(v, axis=0, keepdims=True)

        sums = [jnp.zeros((1, D_MODEL), F32) for _ in range(4)]
        for c in range(tm // tr):
            rows = slice(c * tr, (c + 1) * tr)
            oc = o[rows]
            x1 = x_ref[rows, :] + gate * oc
            r = lax.rsqrt(jnp.sum(x1 * x1, axis=-1, keepdims=True) * inv_d + NORM_EPS)
            x1n = x1 * r
            diff = x1n * cs + sh - t_ref[rows, :]
            w = diff * x1n
            lane_sum = jnp.sum(w * cs, axis=-1, keepdims=True)
            dx1 = (diff * cs) * (r * inv_d) - x1n * (r * lane_sum * (inv_d * inv_d))
            dx1_ref[rows, :] = dx1
            do = (dx1 * gate).astype(BF16)
            do_ref[rows, :] = do
            do_last[rows, :] = do
            for k, v in enumerate((dx1 * oc, diff, w, diff * diff)):
                sums[k] = sums[k] + rowsum(v)
        live = jnp.where(i < nt, 1.0, 0.0)
        sums_ref[0:1, :] += live * sums[0]
        sums_ref[1:2, :] += (live * inv_d) * sums[1]
        sums_ref[2:3, :] += (live * inv_d) * (sums[2] * g)
        sums_ref[3:4, :] += (live * inv_d) * (sums[2] * one_sc)
        sums_ref[4:5, :] += live * sums[3]
        dy_ref[...] = lax.dot_general(do_work[...], wo_ref[...], NT, preferred_element_type=F32).astype(BF16)

    tile = pl.BlockSpec((tm, D_MODEL), lambda i: (jnp.minimum(i, nt - 1), 0))
    row = pl.BlockSpec((1, D_MODEL), lambda i: (0, 0))
    return pl.pallas_call(
        body, name="out_proj_loss", grid=(nt + 1,),
        in_specs=[tile, tile, tile, pl.BlockSpec((D_MODEL, D_MODEL), lambda i: (0, 0)), row, row, row, row],
        out_specs=[tile, tile, pl.BlockSpec((tm, D_MODEL), lambda i: (jnp.maximum(i - 1, 0), 0)),
                   pl.BlockSpec((8, D_MODEL), lambda i: (0, 0))],
        out_shape=[SDS((s, D_MODEL), F32), SDS((s, D_MODEL), BF16), SDS((s, D_MODEL), BF16), SDS((8, D_MODEL), F32)],
        scratch_shapes=[pltpu.VMEM((tm, D_MODEL), BF16), pltpu.VMEM((tm, D_MODEL), BF16)],
        compiler_params=_params("arbitrary"),
    )(y, x, target, wo, gate, shift_f, scale_f, fng)


ROW_DBSP, ROW_DSINKS, MISC_ROWS = 0, 8, 32


def _mixer_bwd(me, proj, dy, tabs, ln_g, ln_b, w_sp, b_sp, sinks, pair):
    s = proj.shape[0]
    nb = s // CHUNK
    sp = _mixer_specs(nb, rev=True)

    def body(me_ref, cur_ref, pkv_ref, dy_ref, c_ref, s1_ref, s2_ref, cp_ref, s1p_ref, s2p_ref, lg_ref, lb_ref, w_ref,
             b_ref, sinks_ref, pair_ref, dproj_ref, dln_ref, dw_ref, misc_ref, parts_ref, bcol, dbcol, carry, mask,
             send_sems, recv_sems):
        i = pl.program_id(0)
        block = nb - 1 - i

        @pl.when(i == 0)
        def _():
            for cp in _chip_scatter(pair_ref, parts_ref, send_sems, recv_sems):
                cp.start()
            _bias_columns(b_ref, bcol)
            mask[...] = _band_mask()
            dbcol[...] = jnp.zeros_like(dbcol)
            carry[...] = jnp.zeros_like(carry)
            dln_ref[...] = jnp.zeros_like(dln_ref)
            dw_ref[...] = jnp.zeros_like(dw_ref)
            misc_ref[...] = jnp.zeros_like(misc_ref)

        vln, vhat, rstd = _layer_norm(cur_ref[:, OFF_VA:OFF_ZA].astype(F32), lg_ref[...], lb_ref[...])
        vln = vln.astype(BF16)
        d_vln = []
        for g in range(A_GROUPS):
            cols = slice(g * 128, (g + 1) * 128)
            w_g = _tril_bf16(w_ref, g)
            sg = jnp.dot(w_g, vln[:, cols], preferred_element_type=F32) + bcol[g]
            u = cur_ref[:, OFF_U + g * 128:OFF_U + (g + 1) * 128].astype(F32)
            z = cur_ref[:, OFF_ZA + g * 128:OFF_ZA + (g + 1) * 128].astype(F32)
            dya = dy_ref[:, cols].astype(F32)
            sig = _sigmoid(z)
            d_ya = dya * (z * sig)
            dproj_ref[:, OFF_ZA + g * 128:OFF_ZA + (g + 1) * 128] = (
                dya * (u * sg) * (sig * (1.0 + z * (1.0 - sig)))).astype(BF16)
            dproj_ref[:, OFF_U + g * 128:OFF_U + (g + 1) * 128] = (d_ya * sg).astype(BF16)
            d_s = d_ya * u
            dbcol[g] += d_s
            d_sb = d_s.astype(BF16)
            dw_ref[g] += lax.dot_general(d_sb, vln[:, cols], NT, preferred_element_type=F32)
            d_vln.append(lax.dot_general(w_g, d_sb, TN, preferred_element_type=F32))
        d_vln = jnp.concatenate(d_vln, axis=1)
        dln_ref[0:1, :] += jnp.sum(d_vln * vhat, axis=0, keepdims=True)
        dln_ref[1:2, :] += jnp.sum(d_vln, axis=0, keepdims=True)
        d_vhat = d_vln * lg_ref[...]
        d_va = rstd * (d_vhat - jnp.mean(d_vhat, axis=-1, keepdims=True)
                       - vhat * jnp.mean(d_vhat * vhat, axis=-1, keepdims=True))
        dproj_ref[:, OFF_VA:OFF_ZA] = d_va.astype(BF16)

        cur_t = (c_ref[...], s1_ref[...], s2_ref[...])
        prev_t = (cp_ref[...], s1p_ref[...], s2p_ref[...])
        band_t = tuple(jnp.concatenate([p, c], axis=0) for p, c in zip(prev_t, cur_t))
        qr = _rope(cur_ref[:, OFF_Q:OFF_K].astype(F32), *cur_t) * ATTN_SCALE
        kr = jnp.concatenate([_rope(pkv_ref[:, 0:256].astype(F32), *prev_t),
                              _rope(cur_ref[:, OFF_K:OFF_V].astype(F32), *cur_t)], axis=0)
        vb = jnp.concatenate([pkv_ref[:, 256:512], cur_ref[:, OFF_V:OFF_ZB]], axis=0).astype(F32)
        k_t, v_t = (kr.T * ATTN_SCALE).astype(BF16), vb.T.astype(BF16)
        zb = cur_ref[:, OFF_ZB:D_IN].astype(F32)
        dyb = dy_ref[:, D_A:D_MODEL].astype(F32)
        sig = _sigmoid(zb)
        d_yb = dyb * (zb * sig)
        outs, dqs = [], []
        dk_pairs = [jnp.zeros((2 * CHUNK, 128), F32) for _ in range(2)]
        dv_pairs = [jnp.zeros((2 * CHUNK, 128), F32) for _ in range(2)]
        for gk in range(N_KV_HEADS):
            heads = slice(gk * HEAD_DIM, (gk + 1) * HEAD_DIM)
            q_st = _stack_heads(qr[:, (2 * gk) * 128:(2 * gk + 1) * 128], qr[:, (2 * gk + 1) * 128:(2 * gk + 2) * 128])
            k_dup, v_dup = _dup_kv_head(kr, gk), _dup_kv_head(vb, gk)
            probs, p_sink = _attn_probs(q_st, k_dup, _sink_row(sinks_ref, gk), mask[...], block == 0)
            probs_b = probs.astype(BF16)
            outs.append(jnp.dot(v_t[heads], probs_b, preferred_element_type=F32))
            do_st = _stack_heads(d_yb[:, (2 * gk) * 128:(2 * gk + 1) * 128], d_yb[:, (2 * gk + 1) * 128:(2 * gk + 2) * 128])
            dp = lax.dot_general(v_dup, do_st, NT, preferred_element_type=F32)
            delta = jnp.sum(probs * dp, axis=0, keepdims=True)
            ds = (probs * (dp - delta)).astype(BF16)
            d_sink = -p_sink * delta
            for r in range(4):
                row = ROW_DSINKS + 4 * gk + r
                misc_ref[row:row + 1, :] += jnp.broadcast_to(
                    jnp.sum(d_sink[:, r * CHUNK:(r + 1) * CHUNK], axis=1, keepdims=True), (1, 128))
            dqs.append(jnp.dot(k_t[heads], ds, preferred_element_type=F32))
            dk_pairs[gk // 2] += _fold_kv_head(jnp.dot(ds, q_st, preferred_element_type=F32), gk)
            dv_pairs[gk // 2] += _fold_kv_head(jnp.dot(probs_b, do_st, preferred_element_type=F32), gk)
        dproj_ref[:, OFF_ZB:D_IN] = (dyb * _heads_to_lanes(outs) * (sig * (1.0 + zb * (1.0 - sig)))).astype(BF16)
        dproj_ref[:, OFF_Q:OFF_K] = _rope_bwd(_heads_to_lanes(dqs), *cur_t).astype(BF16)
        dk_band = _rope_bwd(jnp.concatenate(dk_pairs, axis=1), *band_t)
        dv_band = jnp.concatenate(dv_pairs, axis=1)
        dproj_ref[:, OFF_K:OFF_V] = (dk_band[CHUNK:] + carry[:, 0:256]).astype(BF16)
        dproj_ref[:, OFF_V:OFF_ZB] = (dv_band[CHUNK:] + carry[:, 256:512]).astype(BF16)
        carry[:, 0:256] = dk_band[:CHUNK]
        carry[:, 256:512] = dv_band[:CHUNK]

        @pl.when(i == nb - 1)
        def _():
            t = lax.broadcasted_iota(jnp.int32, (CHUNK, CHUNK), 0)
            tp = lax.broadcasted_iota(jnp.int32, (CHUNK, CHUNK), 1)
            for g in range(A_GROUPS):
                dw_ref[g] = jnp.where(tp <= t, dw_ref[g], 0.0)
                misc_ref[pl.ds(ROW_DBSP + g, 1), :] = jnp.sum(dbcol[g].T, axis=0, keepdims=True)
            scatter = _chip_scatter(pair_ref, parts_ref, send_sems, recv_sems)
            for cp in scatter:
                cp.wait_recv()
            for cp in scatter:
                cp.wait_send()

    blk = sp["blk"]
    hbm = pl.BlockSpec(memory_space=pl.ANY)
    return pl.pallas_call(
        body, name="mixer_bwd",
        grid_spec=pltpu.PrefetchScalarGridSpec(
            num_scalar_prefetch=1, grid=(nb,),
            in_specs=[sp["cur"], sp["prev_kv"], pl.BlockSpec((CHUNK, D_MODEL), lambda i, me_ref: (blk(i), 0)),
                      *sp["tabs"], sp["vec"], sp["vec"], sp["wsp"], sp["bsp"], sp["smem"], hbm],
            out_specs=[pl.BlockSpec((CHUNK, D_IN), lambda i, me_ref: (blk(i), 0)),
                       pl.BlockSpec((8, D_A), lambda i, me_ref: (me_ref[0], 0)),
                       pl.BlockSpec((A_GROUPS, CHUNK, CHUNK), lambda i, me_ref: (me_ref[0], 0, 0)),
                       pl.BlockSpec((MISC_ROWS, 128), lambda i, me_ref: (me_ref[0], 0)), hbm],
            scratch_shapes=[pltpu.VMEM((A_GROUPS, CHUNK, CHUNK), F32), pltpu.VMEM((A_GROUPS, CHUNK, CHUNK), F32),
                            pltpu.VMEM((CHUNK, 512), F32), pltpu.VMEM((2 * CHUNK, 4 * CHUNK), F32),
                            *_scatter_scratch()]),
        out_shape=[SDS((s, D_IN), BF16), SDS((N_DEV * 8, D_A), F32), SDS((N_DEV * A_GROUPS, CHUNK, CHUNK), F32),
                   SDS((N_DEV * MISC_ROWS, 128), F32), SDS((3,) + pair.shape[1:], pair.dtype)],
        compiler_params=_params("arbitrary"),
    )(me, proj, proj, dy, *tabs, *tabs, ln_g, ln_b, w_sp, b_sp, sinks, pair)


def _wgrad(name, a, b, bm, gathers=()):
    s, m = a.shape
    n = b.shape[1]
    bt = 512
    steps = s // bt
    n_g = len(gathers)

    def body(*refs):
        a_ref, b_ref = refs[:2]
        out_ref, bufs = refs[2 + n_g], refs[3 + n_g:3 + 2 * n_g]
        acc, sems = refs[3 + 2 * n_g], refs[4 + 2 * n_g:]
        i, t = pl.program_id(0), pl.program_id(1)
        jobs = [_InPlaceGather(bufs[k], sems[2 * k], sems[2 * k + 1]) for k in range(n_g)]

        @pl.when((i == 0) & (t == 0))
        def _():
            for job in jobs:
                job.start()

        @pl.when((i == (m // bm) // 2) & (t == 0))
        def _():
            for job in jobs:
                for j in range(3):
                    job.pass_on(j)

        @pl.when(t == 0)
        def _():
            acc[...] = jnp.zeros_like(acc)

        acc[...] += lax.dot_general(a_ref[...], b_ref[...], TN, preferred_element_type=F32)

        @pl.when(t == steps - 1)
        def _():
            out_ref[...] = acc[...].astype(out_ref.dtype)

        @pl.when((i == m // bm - 1) & (t == steps - 1))
        def _():
            for job in jobs:
                job.wait_sibling(0)
                for j in range(3):
                    job.wait_sibling(4 + j)
                job.wait_sends()

    hbm = pl.BlockSpec(memory_space=pl.ANY)
    outs = pl.pallas_call(
        body, name=name, grid=(m // bm, steps),
        in_specs=[pl.BlockSpec((bt, bm), lambda i, t: (t, i)), pl.BlockSpec((bt, n), lambda i, t: (t, 0))] + [hbm] * n_g,
        out_specs=[pl.BlockSpec((bm, n), lambda i, t: (i, 0))] + [hbm] * n_g,
        out_shape=[SDS((m, n), BF16)] + [SDS(g.shape, g.dtype) for g in gathers],
        scratch_shapes=[pltpu.VMEM((bm, n), F32)] + _gather_scratch() * n_g,
        input_output_aliases={2 + k: 1 + k for k in range(n_g)},
        compiler_params=_params("arbitrary", "arbitrary"),
    )(a, b, *gathers)
    return outs[0], outs[1:]


def _in_proj_bwd(dproj, wt, x, dx1, scale, norm_g, pair):
    s = x.shape[0]
    tm, tk, tr = min(1024, s), D_IN // 4, 64
    ksteps = D_IN // tk

    def body(dp_ref, wt_ref, x_hbm, dx1_hbm, sc_ref, g_ref, pair_ref, gx_ref, sums_ref, parts_ref, x_buf, dx1_buf,
             tile_sems, send_sems, recv_sems):
        i, k = pl.program_id(0), pl.program_id(1)

        def tile_copies():
            rows = pl.ds(pl.multiple_of(i * tm, tm), tm)
            return (pltpu.make_async_copy(x_hbm.at[rows], x_buf, tile_sems.at[0]),
                    pltpu.make_async_copy(dx1_hbm.at[rows], dx1_buf, tile_sems.at[1]))

        @pl.when((i == 0) & (k == 0))
        def _():
            for cp in _chip_scatter(pair_ref, parts_ref, send_sems, recv_sems):
                cp.start()
            sums_ref[...] = jnp.zeros_like(sums_ref)

        @pl.when(k == 0)
        def _():
            for cp in tile_copies():
                cp.start()
            gx_ref[...] = jnp.dot(dp_ref[...], wt_ref[...], preferred_element_type=F32)

        @pl.when(k > 0)
        def _():
            gx_ref[...] += jnp.dot(dp_ref[...], wt_ref[...], preferred_element_type=F32)

        @pl.when(k == ksteps - 1)
        def _():
            for cp in tile_copies():
                cp.wait()
            one_sc, g = 1.0 + sc_ref[...], g_ref[...]
            cs = one_sc * g

            def chunk(j, sums):
                rows = pl.ds(pl.multiple_of(j * tr, tr), tr)
                dh, xv = gx_ref[rows, :], x_buf[rows, :]
                dhx = dh * xv
                r = lax.rsqrt(jnp.sum(xv * xv, axis=-1, keepdims=True) * (1.0 / D_MODEL) + NORM_EPS)
                coef = (r * r * r) * (jnp.sum(dhx * cs, axis=-1, keepdims=True) * (1.0 / D_MODEL))
                gx_ref[rows, :] = dx1_buf[rows, :] + r * (dh * cs) - xv * coef
                return (sums[0] + jnp.sum(dh, axis=0, keepdims=True), sums[1] + jnp.sum(dhx * r, axis=0, keepdims=True))

            zero = jnp.zeros((1, D_MODEL), F32)
            sums = lax.fori_loop(0, tm // tr, chunk, (zero, zero))
            sums_ref[0:1, :] += sums[0]
            sums_ref[1:2, :] += sums[1] * g
            sums_ref[2:3, :] += sums[1] * one_sc

        @pl.when((i == s // tm - 1) & (k == ksteps - 1))
        def _():
            scatter = _chip_scatter(pair_ref, parts_ref, send_sems, recv_sems)
            for cp in scatter:
                cp.wait_recv()
            for cp in scatter:
                cp.wait_send()

    row = pl.BlockSpec((1, D_MODEL), lambda i, k: (0, 0))
    hbm = pl.BlockSpec(memory_space=pl.ANY)
    return pl.pallas_call(
        body, name="in_proj_bwd", grid=(s // tm, ksteps),
        in_specs=[pl.BlockSpec((tm, tk), lambda i, k: (i, k)), pl.BlockSpec((tk, D_MODEL), lambda i, k: (k, 0)),
                  hbm, hbm, row, row, hbm],
        out_specs=[pl.BlockSpec((tm, D_MODEL), lambda i, k: (i, 0)), pl.BlockSpec((8, D_MODEL), lambda i, k: (0, 0)),
                   hbm],
        out_shape=[SDS((s, D_MODEL), F32), SDS((8, D_MODEL), F32), SDS((3,) + pair.shape[1:], pair.dtype)],
        scratch_shapes=[pltpu.VMEM((tm, D_MODEL), F32), pltpu.VMEM((tm, D_MODEL), F32),
                        pltpu.SemaphoreType.DMA((2,)), *_scatter_scratch()],
        compiler_params=_params("arbitrary", "arbitrary"),
    )(dproj, wt, x, dx1, scale, norm_g, pair)


def _pair_sum(core, grad, got):
    _, m, n = got.shape

    def body(core_ref, a_ref, b_ref, out_ref):
        out_ref[...] = (a_ref[...].astype(F32) + b_ref[...].astype(F32)).astype(BF16)

    blk = pl.BlockSpec((1, m, n), lambda q, core_ref: (q, 0, 0))
    return pl.pallas_call(
        body, name=f"pair_sum_{m}",
        grid_spec=pltpu.PrefetchScalarGridSpec(
            num_scalar_prefetch=1, grid=(4,),
            in_specs=[pl.BlockSpec((1, m, n), lambda q, core_ref: (2 * q + core_ref[0], 0, 0)), blk], out_specs=blk),
        out_shape=SDS(got.shape, BF16), compiler_params=_params("parallel"),
    )(core, grad, got)


def _sum_chips(own_ref, parts_ref):
    return ((own_ref[0].astype(F32) + parts_ref[0].astype(F32)) + parts_ref[1].astype(F32)) + parts_ref[2].astype(F32)


def _adam_rows(name, chip, pair, parts, w, m, v):
    rows = w.shape[0]
    tr = rows // 4

    def body(chip_ref, own_ref, p_ref, w_ref, m_ref, v_ref, g_ref, d_ref, nm_ref, nv_ref):
        g = _sum_chips(own_ref, p_ref)
        g_ref[...] = g
        d_ref[...], nm_ref[...], nv_ref[...] = _adamw(w_ref[...], g, m_ref[...], v_ref[...])

    blk = pl.BlockSpec((tr, D_MODEL), lambda j, chip_ref: (j, 0))
    return pl.pallas_call(
        body, name=name,
        grid_spec=pltpu.PrefetchScalarGridSpec(
            num_scalar_prefetch=1, grid=(rows // tr,),
            in_specs=[pl.BlockSpec((1, tr, D_MODEL), lambda j, chip_ref: (chip_ref[0], j, 0)),
                      pl.BlockSpec((3, tr, D_MODEL), lambda j, chip_ref: (0, j, 0)), blk, blk, blk],
            out_specs=[blk] * 4),
        out_shape=[SDS(w.shape, F32)] * 4, compiler_params=_params("parallel"),
    )(chip, pair, parts, w, m, v)


def _adam_ada(name, cact, dmod, w, m, v):
    n = w.shape[1]
    tr = 512

    def body(c_ref, dm_ref, w_ref, m_ref, v_ref, g_ref, d_ref, nm_ref, nv_ref):
        pad_c = jnp.concatenate([c_ref[...], jnp.zeros_like(c_ref)], axis=0).astype(BF16)
        pad_d = jnp.concatenate([dm_ref[...], jnp.zeros_like(dm_ref)], axis=0).astype(BF16)
        g = lax.dot_general(pad_c, pad_d, TN, preferred_element_type=F32)
        g_ref[...] = g
        d_ref[...], nm_ref[...], nv_ref[...] = _adamw(w_ref[...], g, m_ref[...], v_ref[...])

    blk = pl.BlockSpec((tr, n), lambda j: (j, 0))
    return pl.pallas_call(
        body, name=name, grid=(D_MODEL // tr,),
        in_specs=[pl.BlockSpec((N_DEV, tr), lambda j: (0, j)), pl.BlockSpec((N_DEV, n), lambda j: (0, 0)),
                  blk, blk, blk],
        out_specs=[blk] * 4, out_shape=[SDS(w.shape, F32)] * 4,
        compiler_params=_params("parallel"),
    )(cact, dmod, w, m, v)


SMALL_PARAMS = ("w_spatial", "b_spatial", "sinks", "norm_g", "ln_v_g", "ln_v_b", "final_norm_g", "b_ada", "b_ada_final")


def _adam_small(d_wsp, misc, d_ln, sums_i, sums_o, params):
    n_p = len(SMALL_PARAMS)

    def body(*refs):
        wsp_ref, misc_ref, ln_ref, si_ref, so_ref = refs[:5]
        wmv = [refs[5 + 3 * k:8 + 3 * k] for k in range(n_p)]
        loss_ref = refs[5 + 3 * n_p]
        outs = [refs[6 + 3 * n_p + 4 * k:10 + 3 * n_p + 4 * k] for k in range(n_p)]

        def total(ref, rows=None):
            def part(j):
                return ref[j] if rows is None else ref[j, rows[0]:rows[1], :]
            acc = part(0)
            for j in range(1, N_DEV):
                acc = acc + part(j)
            return acc

        sink_rows = total(misc_ref, (ROW_DSINKS, ROW_DSINKS + 16))
        diag = (lax.broadcasted_iota(jnp.int32, (16, 128), 0) == lax.broadcasted_iota(jnp.int32, (16, 128), 1))
        grads = dict(
            w_spatial=total(wsp_ref), b_spatial=total(misc_ref, (ROW_DBSP, ROW_DBSP + A_GROUPS)),
            sinks=jnp.sum(jnp.where(diag, sink_rows, 0.0), axis=0, keepdims=True),
            norm_g=total(si_ref, (2, 3)), ln_v_g=total(ln_ref, (0, 1)), ln_v_b=total(ln_ref, (1, 2)),
            final_norm_g=total(so_ref, (3, 4)),
            b_ada=jnp.concatenate([total(si_ref, (0, 1)), total(si_ref, (1, 2)), total(so_ref, (0, 1))], axis=1),
            b_ada_final=jnp.concatenate([total(so_ref, (1, 2)), total(so_ref, (2, 3))], axis=1))
        sq_err = jnp.sum(total(so_ref, (4, 5)), axis=1, keepdims=True)
        loss_ref[...] = jnp.broadcast_to(sq_err * (0.5 / D_MODEL), (1, 128))
        for k, name in enumerate(SMALL_PARAMS):
            w_ref, m_ref, v_ref = wmv[k]
            g_ref, d_ref, nm_ref, nv_ref = outs[k]
            g_ref[...] = grads[name]
            d_ref[...], nm_ref[...], nv_ref[...] = _adamw(w_ref[...], grads[name], m_ref[...], v_ref[...])

    flat = [a for name in SMALL_PARAMS for a in params[name]]
    vmem = pl.BlockSpec(memory_space=pltpu.VMEM)
    out_shape = [SDS((1, 128), F32)] + [SDS(params[name][0].shape, F32) for name in SMALL_PARAMS for _ in range(4)]
    outs = pl.pallas_call(
        body, name="adam_small", in_specs=[vmem] * (5 + len(flat)), out_specs=[vmem] * len(out_shape),
        out_shape=out_shape, compiler_params=_params(),
    )(d_wsp, misc, d_ln, sums_i, sums_o, *flat)
    return outs[0], {name: outs[1 + 4 * k:5 + 4 * k] for k, name in enumerate(SMALL_PARAMS)}


def kernel(x, c, w_ada, b_ada, norm_g, w_in, ln_v_g, ln_v_b, w_spatial, b_spatial, sinks, w_out, w_ada_final, b_ada_final, final_norm_g, loss_target, m_w_ada, m_b_ada, m_norm_g, m_w_in, m_ln_v_g, m_ln_v_b, m_w_spatial, m_b_spatial, m_sinks, m_w_out, m_w_ada_final, m_b_ada_final, m_final_norm_g, v_w_ada, v_b_ada, v_norm_g, v_w_in, v_ln_v_g, v_ln_v_b, v_w_spatial, v_b_spatial, v_sinks, v_w_out, v_w_ada_final, v_b_ada_final, v_final_norm_g):
    seq = x.shape[1]
    me = 4 * lax.axis_index("x") + 2 * lax.axis_index("y") + lax.axis_index("c")
    x2, tgt = x[0], loss_target[0]
    fng = final_norm_g.reshape(1, D_MODEL)

    n_ada, n_ada_f = w_ada.shape[2], w_ada_final.shape[1]
    cact, mod, mod_f = _ada_exchange(c, w_ada[0], b_ada.reshape(N_DEV, n_ada), w_ada_final,
                                     b_ada_final.reshape(N_DEV, n_ada_f))
    cact = cact.reshape(N_DEV, D_MODEL)
    mod, mod_f = mod.reshape(1, 3 * D_MODEL), mod_f.reshape(1, 2 * D_MODEL)
    shift, scale, gate = mod[:, :D_MODEL], mod[:, D_MODEL:2 * D_MODEL], mod[:, 2 * D_MODEL:]
    shift_f, scale_f = mod_f[:, :D_MODEL], mod_f[:, D_MODEL:]

    wt_f32, m_wt, v_wt = (jnp.swapaxes(a, 1, 2)[0] for a in (w_in, m_w_in, v_w_in))
    xi, yi = lax.axis_index("x"), lax.axis_index("y")
    chip_order = jnp.stack([2 * xi + yi, 2 * (1 - xi) + yi, 2 * xi + 1 - yi, 2 * (1 - xi) + 1 - yi]).astype(jnp.int32)
    wt_mine, wo_mine = _prep_weights(me.reshape(1), wt_f32, w_out[0])

    tabs = _rope_tables(seq)
    sinks_v = sinks.reshape(16)
    h, proj, wt = _gather_in_proj(chip_order, x2, shift, scale, norm_g, wt_mine)
    y, wo = _mixer_fwd(proj, tabs, ln_v_g, ln_v_b, w_spatial[0], b_spatial[0], sinks_v, wo_mine)
    dx1, do, dy, sums_o = _out_proj_loss(y, x2, tgt, wo, gate, shift_f, scale_f, fng)

    core = lax.axis_index("c").reshape(1)
    chip = (2 * lax.axis_index("x") + lax.axis_index("y")).reshape(1)
    g_wo, _ = _wgrad("wgrad_out", y, do, 1024)
    g_wo = g_wo.reshape(N_DEV, D_MODEL // N_DEV, D_MODEL)
    pair_out = _pair_sum(core, g_wo, _rs_pair("rs_pair_out", g_wo))
    dproj, d_ln, d_wsp, misc, parts_out = _mixer_bwd(
        me.reshape(1), proj, dy, tabs, ln_v_g, ln_v_b, w_spatial[0], b_spatial[0], sinks_v, pair_out)
    g_wt, (d_ln, d_wsp, misc) = _wgrad("wgrad_in", dproj, h, 1408,
                                       gathers=(d_ln, d_wsp.reshape(N_DEV * A_GROUPS * CHUNK, CHUNK), misc))
    g_wt = g_wt.reshape(N_DEV, D_IN // N_DEV, D_MODEL)
    pair_in = _pair_sum(core, g_wt, _rs_pair("rs_pair_in", g_wt))
    grad_x, sums_i, parts_in = _in_proj_bwd(dproj, wt, x2, dx1, scale, norm_g, pair_in)
    wt_leaves = [jnp.swapaxes(a[None], 1, 2) for a in _adam_rows("adam_w_in", chip, pair_in, parts_in, wt_f32, m_wt, v_wt)]
    w_out_leaves = [a[None] for a in _adam_rows("adam_w_out", chip, pair_out, parts_out, w_out[0], m_w_out[0], v_w_out[0])]

    sums_i, sums_o = _all_gather("gather_sums", [sums_i, sums_o], pltpu.VMEM)
    natural = dict(w_spatial=(A_GROUPS * CHUNK, CHUNK), b_spatial=(A_GROUPS, CHUNK), sinks=(1, 16), norm_g=(1, D_MODEL),
                   ln_v_g=(1, D_A), ln_v_b=(1, D_A), final_norm_g=(1, D_MODEL), b_ada=(1, 3 * D_MODEL),
                   b_ada_final=(1, 2 * D_MODEL))
    given = dict(
        w_spatial=(w_spatial, m_w_spatial, v_w_spatial), b_spatial=(b_spatial, m_b_spatial, v_b_spatial),
        sinks=(sinks, m_sinks, v_sinks), norm_g=(norm_g, m_norm_g, v_norm_g), ln_v_g=(ln_v_g, m_ln_v_g, v_ln_v_g),
        ln_v_b=(ln_v_b, m_ln_v_b, v_ln_v_b), final_norm_g=(final_norm_g, m_final_norm_g, v_final_norm_g),
        b_ada=(b_ada, m_b_ada, v_b_ada), b_ada_final=(b_ada_final, m_b_ada_final, v_b_ada_final))
    params = {name: tuple(a.reshape(natural[name]) for a in given[name]) for name in SMALL_PARAMS}
    params["sinks"] = tuple(jnp.pad(a, ((0, 0), (0, 128 - 16))) for a in params["sinks"])
    loss, small = _adam_small(d_wsp.reshape(N_DEV, A_GROUPS * CHUNK, CHUNK), misc.reshape(N_DEV, MISC_ROWS, 128),
                              d_ln.reshape(N_DEV, 8, D_A), sums_i, sums_o, params)
    small["sinks"] = [a[:, :16] for a in small["sinks"]]
    small = {name: [a.reshape(given[name][0].shape) for a in small[name]] for name in SMALL_PARAMS}

    dmod_all = jnp.concatenate([sums_i[:, 0], sums_i[:, 1], sums_o[:, 0]], axis=1)
    dmod_f_all = jnp.concatenate([sums_o[:, 1], sums_o[:, 2]], axis=1)
    dmod_mine = lax.dynamic_slice_in_dim(dmod_all, me * n_ada, n_ada, axis=1)
    dmod_f_mine = lax.dynamic_slice_in_dim(dmod_f_all, me * n_ada_f, n_ada_f, axis=1)
    ada = _adam_ada("adam_w_ada", cact, dmod_mine, w_ada[0], m_w_ada[0], v_w_ada[0])
    ada_f = _adam_ada("adam_w_ada_final", cact, dmod_f_mine, w_ada_final, m_w_ada_final, v_w_ada_final)

    def leaves(k):
        return (ada[k][None], small["b_ada"][k], small["norm_g"][k], wt_leaves[k], small["ln_v_g"][k],
                small["ln_v_b"][k], small["w_spatial"][k], small["b_spatial"][k], small["sinks"][k], w_out_leaves[k],
                ada_f[k], small["b_ada_final"][k], small["final_norm_g"][k])

    return (loss[0, 0], grad_x[None], *leaves(0), *leaves(1), *leaves(2), *leaves(3))
```

```python
import functools

import jax
import jax.numpy as jnp
from jax import lax
from jax.experimental import pallas as pl
from jax.experimental.pallas import tpu as pltpu

D_MODEL = 2048
D_IN = 5632
D_A = 1024
CHUNK = 128
A_GROUPS = 8
HEAD_DIM = 64
N_KV_HEADS = 4
N_DEV = 8
ROPE_THETA = 10000.0
NORM_EPS = 1e-5
ATTN_SCALE = HEAD_DIM ** -0.5

ADAM_LR = 0.001
ADAM_B1 = 0.9
ADAM_B2 = 0.999
ADAM_EPS = 1e-08
ADAM_WD = 0.01
ADAM_STEP = 10

OFF_U, OFF_VA, OFF_ZA, OFF_Q, OFF_K, OFF_V, OFF_ZB = 0, 1024, 2048, 3072, 4096, 4352, 4608

V7X_VMEM_LIMIT_BYTES = 56 * 1024 * 1024

F32 = jnp.float32
BF16 = jnp.bfloat16
MESH = pl.DeviceIdType.MESH
SDS = jax.ShapeDtypeStruct
NT = (((1,), (1,)), ((), ()))
TN = (((0,), (0,)), ((), ()))


def _params(*semantics):
    return pltpu.CompilerParams(dimension_semantics=semantics or None, vmem_limit_bytes=V7X_VMEM_LIMIT_BYTES)


def _mesh_pos():
    return lax.axis_index("x"), lax.axis_index("y"), lax.axis_index("c")


def _sigmoid(z):
    return 1.0 / (1.0 + jnp.exp(-z))


def _adamw(w, g, m, v):
    m = ADAM_B1 * m + (1.0 - ADAM_B1) * g
    v = ADAM_B2 * v + (1.0 - ADAM_B2) * (g * g)
    m_hat = m / (1.0 - ADAM_B1 ** ADAM_STEP)
    v_hat = v / (1.0 - ADAM_B2 ** ADAM_STEP)
    delta = -ADAM_LR * (m_hat / (jnp.sqrt(v_hat) + ADAM_EPS) + ADAM_WD * w)
    return delta, m, v


def _all_gather(name, blocks, memory_space):
    n_arr = len(blocks)

    def body(*refs):
        ins, outs = refs[:n_arr], refs[n_arr:2 * n_arr]
        send_sems, recv_sems, local_sems = refs[2 * n_arr:]
        x, y, c = _mesh_pos()
        me, sibling = (x, y, c), (x, y, 1 - c)
        chips = [(1 - x, y), (x, 1 - y), (1 - x, 1 - y)]

        def slot(p):
            return 4 * p[0] + 2 * p[1] + p[2]

        def copy(a, k, block, to, src=None):
            dst = outs[a].at[slot(block)]
            return pltpu.make_async_remote_copy(
                src_ref=dst if src is None else src, dst_ref=dst,
                send_sem=send_sems.at[a, k], recv_sem=recv_sems.at[a, k],
                device_id=to, device_id_type=MESH)

        mine = [pltpu.make_async_copy(ins[a], outs[a].at[slot(me)], local_sems.at[a]) for a in range(n_arr)]
        for cp in mine:
            cp.start()
        first = []
        for a in range(n_arr):
            first.append(copy(a, 0, me, sibling, src=ins[a]))
            first += [copy(a, 1 + j, me, (*chip, c), src=ins[a]) for j, chip in enumerate(chips)]
        for cp in first:
            cp.start()
        passed = []
        for j, chip in enumerate(chips):
            for a in range(n_arr):
                copy(a, 1 + j, (*chip, c), me).wait_recv()
                fwd = copy(a, 4 + j, (*chip, c), sibling)
                fwd.start()
                passed.append(fwd)
        for a in range(n_arr):
            copy(a, 0, sibling, me).wait_recv()
            for j, chip in enumerate(chips):
                copy(a, 4 + j, (*chip, 1 - c), me).wait_recv()
        for cp in first + passed:
            cp.wait_send()
        for cp in mine:
            cp.wait()

    spec = pl.BlockSpec(memory_space=memory_space)
    return pl.pallas_call(
        body, name=name,
        out_shape=[SDS((N_DEV,) + b.shape, b.dtype) for b in blocks],
        in_specs=[spec] * n_arr, out_specs=[spec] * n_arr,
        scratch_shapes=[pltpu.SemaphoreType.DMA((n_arr, 7)), pltpu.SemaphoreType.DMA((n_arr, 7)),
                        pltpu.SemaphoreType.DMA((n_arr,))],
        compiler_params=_params(),
    )(*blocks)


def _ada_exchange(c, w_ada, b_ada8, w_ada_f, b_ada_f8):
    n1, n2 = w_ada.shape[1], w_ada_f.shape[1]

    def body(c_ref, w1_ref, b1_ref, w2_ref, b2_ref, cact_ref, mod_ref, modf_ref,
             cact_buf, res1, res2, send1, send2, sems_s, sems_r):
        x, y, c_pos = _mesh_pos()
        me = 4 * x + 2 * y + c_pos
        flips = [(k >> 2 & 1, k >> 1 & 1, k & 1) for k in range(1, N_DEV)]

        def peer(f):
            return (1 - x if f[0] else x, 1 - y if f[1] else y, 1 - c_pos if f[2] else c_pos)

        cv = c_ref[...]
        cact = cv * _sigmoid(cv)
        cact_buf[...] = cact
        cact_ref[me] = cact

        def rdma(phase, k, src, dst, f):
            return pltpu.make_async_remote_copy(src_ref=src, dst_ref=dst, send_sem=sems_s.at[phase, k],
                                                recv_sem=sems_r.at[phase, k], device_id=peer(f), device_id_type=MESH)

        gather = [rdma(0, k, cact_buf, cact_ref.at[me], f) for k, f in enumerate(flips)]
        for cp in gather:
            cp.start()
        for cp in gather:
            cp.wait_recv()
        for cp in gather:
            cp.wait_send()

        rid = lax.broadcasted_iota(jnp.int32, (N_DEV, D_MODEL), 0)
        rows = jnp.zeros((N_DEV, D_MODEL), F32)
        for j in range(N_DEV):
            rows = jnp.where(rid == j, jnp.broadcast_to(cact_ref[j], (N_DEV, D_MODEL)), rows)
        rows = rows.astype(BF16)
        res1[...] = jnp.dot(rows, w1_ref[...].astype(BF16), preferred_element_type=F32) + b1_ref[pl.ds(me, 1), :]
        res2[...] = jnp.dot(rows, w2_ref[...].astype(BF16), preferred_element_type=F32) + b2_ref[pl.ds(me, 1), :]
        for j in range(N_DEV):
            send1[j] = res1[pl.ds(j, 1), :]
            send2[j] = res2[pl.ds(j, 1), :]
        mod_ref[me] = send1[me]
        modf_ref[me] = send2[me]
        scatter = []
        for k, f in enumerate(flips):
            to = me ^ (k + 1)
            scatter.append(rdma(1, k, send1.at[to], mod_ref.at[me], f))
            scatter.append(rdma(2, k, send2.at[to], modf_ref.at[me], f))
        for cp in scatter:
            cp.start()
        for cp in scatter:
            cp.wait_recv()
        for cp in scatter:
            cp.wait_send()

    vmem = pl.BlockSpec(memory_space=pltpu.VMEM)
    return pl.pallas_call(
        body, name="ada_exchange",
        out_shape=[SDS((N_DEV, 1, D_MODEL), F32), SDS((N_DEV, 1, n1), F32), SDS((N_DEV, 1, n2), F32)],
        in_specs=[vmem] * 5, out_specs=[vmem] * 3,
        scratch_shapes=[pltpu.VMEM((1, D_MODEL), F32), pltpu.VMEM((N_DEV, n1), F32), pltpu.VMEM((N_DEV, n2), F32),
                        pltpu.VMEM((N_DEV, 1, n1), F32), pltpu.VMEM((N_DEV, 1, n2), F32),
                        pltpu.SemaphoreType.DMA((3, 7)), pltpu.SemaphoreType.DMA((3, 7))],
        compiler_params=_params(),
    )(c, w_ada, b_ada8, w_ada_f, b_ada_f8)


def _rs_pair(name, grad):
    def body(g_ref, got_ref, send_sems, recv_sems):
        x, y, c = _mesh_pos()
        copies = [pltpu.make_async_remote_copy(
            src_ref=g_ref.at[2 * q + 1 - c], dst_ref=got_ref.at[q], send_sem=send_sems.at[q], recv_sem=recv_sems.at[q],
            device_id=(x, y, 1 - c), device_id_type=MESH) for q in range(4)]
        for cp in copies:
            cp.start()
        for cp in copies:
            cp.wait_recv()
        for cp in copies:
            cp.wait_send()

    hbm = pl.BlockSpec(memory_space=pl.ANY)
    return pl.pallas_call(
        body, name=name, out_shape=SDS((4,) + grad.shape[1:], grad.dtype), in_specs=[hbm], out_specs=hbm,
        scratch_shapes=[pltpu.SemaphoreType.DMA((4,)), pltpu.SemaphoreType.DMA((4,))],
        compiler_params=_params(),
    )(grad)


def _chip_scatter(pair_ref, parts_ref, send_sems, recv_sems):
    x, y, c = _mesh_pos()
    chips = [(1 - x, y), (x, 1 - y), (1 - x, 1 - y)]
    return [pltpu.make_async_remote_copy(
        src_ref=pair_ref.at[2 * cx + cy], dst_ref=parts_ref.at[j], send_sem=send_sems.at[j], recv_sem=recv_sems.at[j],
        device_id=(cx, cy, c), device_id_type=MESH) for j, (cx, cy) in enumerate(chips)]


def _scatter_scratch():
    return [pltpu.SemaphoreType.DMA((3,)), pltpu.SemaphoreType.DMA((3,))]


def _prep_weights(me, wt, w_out):
    steps = 4

    def body(me_ref, wt_ref, wo_ref, wtb_ref, wob_ref):
        wtb_ref[...] = wt_ref[...].astype(BF16)
        wob_ref[...] = wo_ref[...].astype(BF16)

    def rows(a, mine):
        blk = (a.shape[0] // steps, a.shape[1])
        return pl.BlockSpec(blk, (lambda i, me_ref: (steps * me_ref[0] + i, 0)) if mine else (lambda i, me_ref: (i, 0)))

    return pl.pallas_call(
        body, name="prep_weights",
        grid_spec=pltpu.PrefetchScalarGridSpec(
            num_scalar_prefetch=1, grid=(steps,),
            in_specs=[rows(wt, False), rows(w_out, False)], out_specs=[rows(wt, True), rows(w_out, True)]),
        out_shape=[SDS((N_DEV * wt.shape[0], D_MODEL), BF16), SDS((N_DEV * w_out.shape[0], D_MODEL), BF16)],
        compiler_params=_params("parallel"),
    )(me, wt, w_out)


class _InPlaceGather:
    def __init__(self, buf_ref, send_sems, recv_sems, relay=False):
        self.buf, self.send_sems, self.recv_sems, self.relay = buf_ref, send_sems, recv_sems, relay
        self.n = buf_ref.shape[0] // N_DEV
        x, y, c = _mesh_pos()
        self.me, self.sibling, self.core = (x, y, c), (x, y, 1 - c), c
        self.chips = [(1 - x, y), (x, 1 - y), (1 - x, 1 - y)]
        self.relay_from = (jnp.where(c == 0, 1 - x, x), jnp.where(c == 0, y, 1 - y), c)
        self.relay_to = (jnp.where(c == 0, x, 1 - x), jnp.where(c == 0, 1 - y, y), c)

    def copy(self, k, block, to):
        start = pl.multiple_of((4 * block[0] + 2 * block[1] + block[2]) * self.n, self.n)
        rows = self.buf.at[pl.ds(start, self.n)]
        return pltpu.make_async_remote_copy(src_ref=rows, dst_ref=rows, send_sem=self.send_sems.at[k],
                                            recv_sem=self.recv_sems.at[k], device_id=to, device_id_type=MESH)

    def start(self):
        self.copy(0, self.me, self.sibling).start()
        for j, chip in enumerate(self.chips[:2] if self.relay else self.chips):
            self.copy(1 + j, self.me, (*chip, self.core)).start()

    def relay_diagonal(self):
        self.copy(3, self.relay_from, self.relay_to).start()

    def pass_on(self, j):
        self.copy(1 + j, (*self.chips[j], self.core), self.me).wait_recv()
        self.copy(4 + j, (*self.chips[j], self.core), self.sibling).start()

    def wait_sibling(self, k):
        self.copy(k, self.sibling, self.me).wait_recv()

    def wait_sends(self):
        for k in range(7):
            self.copy(k, self.me, self.sibling).wait_send()


def _gather_scratch():
    return [pltpu.SemaphoreType.DMA((7,)), pltpu.SemaphoreType.DMA((7,))]


def _gather_in_proj(order, x, shift, scale, norm_g, wt_all):
    s = x.shape[0]
    th = tm = min(512, s)
    nh, ni = s // th, s // tm
    tn = D_IN // 4
    steps = nh + 4 * ni

    def body(order_ref, x_ref, shift_ref, scale_ref, g_ref, wt_in, h_ref, proj_ref, wt_ref,
             h_scr, w_buf, load_sems, send_sems, recv_sems):
        g = pl.program_id(0)
        gather = _InPlaceGather(wt_ref, send_sems, recv_sems, relay=True)

        def tile_load(slot, chip):
            return pltpu.make_async_copy(wt_ref.at[pl.ds(pl.multiple_of(chip * tn, tn), tn)], w_buf.at[slot],
                                         load_sems.at[slot])

        @pl.when(g == 0)
        def _():
            gather.start()

        @pl.when(g < nh)
        def _():
            xv = x_ref[...]
            r = lax.rsqrt(jnp.mean(xv * xv, axis=-1, keepdims=True) + NORM_EPS)
            hb = (((xv * r) * g_ref[...]) * (1.0 + scale_ref[...]) + shift_ref[...]).astype(BF16)
            h_ref[...] = hb
            h_scr[pl.ds(pl.multiple_of(g * th, th), th), :] = hb

        @pl.when(g == nh - 1)
        def _():
            gather.wait_sibling(0)
            tile_load(0, order_ref[0]).start()

        @pl.when(g >= nh)
        def _():
            t, i = (g - nh) // ni, (g - nh) % ni

            @pl.when(i == 0)
            def _():
                tile_load(t % 2, order_ref[t]).wait()

            @pl.when((i == ni - 1) & (t == 0))
            def _():
                gather.pass_on(0)
                gather.pass_on(1)
                gather.relay_diagonal()

            @pl.when((i == ni - 1) & (t == 2))
            def _():
                gather.pass_on(2)

            for j in range(3):
                @pl.when((i == ni - 1) & (t == j))
                def _():
                    gather.wait_sibling(4 + j)
                    tile_load((j + 1) % 2, order_ref[j + 1]).start()

            lhs = h_scr[pl.ds(pl.multiple_of(i * tm, tm), tm), :]
            proj_ref[...] = lax.dot_general(lhs, w_buf[t % 2], NT, preferred_element_type=F32).astype(BF16)

        @pl.when(g == steps - 1)
        def _():
            gather.wait_sends()

    def h_tile(g, order_ref):
        return (jnp.minimum(g, nh - 1), 0)

    def proj_tile(g, order_ref):
        mm = jnp.maximum(g - nh, 0)
        return (mm % ni, order_ref[mm // ni])

    row = pl.BlockSpec((1, D_MODEL), lambda g, order_ref: (0, 0))
    hbm = pl.BlockSpec(memory_space=pl.ANY)
    return pl.pallas_call(
        body, name="gather_in_proj",
        grid_spec=pltpu.PrefetchScalarGridSpec(
            num_scalar_prefetch=1, grid=(steps,),
            in_specs=[pl.BlockSpec((th, D_MODEL), h_tile), row, row, row, hbm],
            out_specs=[pl.BlockSpec((th, D_MODEL), h_tile), pl.BlockSpec((tm, tn), proj_tile), hbm],
            scratch_shapes=[pltpu.VMEM((s, D_MODEL), BF16), pltpu.VMEM((2, tn, D_MODEL), BF16),
                            pltpu.SemaphoreType.DMA((2,)), *_gather_scratch()]),
        out_shape=[SDS((s, D_MODEL), BF16), SDS((s, D_IN), BF16), SDS(wt_all.shape, BF16)],
        input_output_aliases={5: 2},
        compiler_params=_params("arbitrary"),
    )(order, x, shift, scale, norm_g, wt_all)


def _rope_tables(seq):
    inv_freq = ROPE_THETA ** (-jnp.arange(0, HEAD_DIM, 2, dtype=F32) / HEAD_DIM)
    ang = jnp.arange(seq, dtype=F32)[:, None] * inv_freq[None, :]
    cos, sin, zero = jnp.cos(ang), jnp.sin(ang), jnp.zeros_like(ang)
    return (jnp.concatenate([cos] * 4, axis=1), jnp.concatenate([-sin, zero, -sin, zero], axis=1),
            jnp.concatenate([zero, sin, zero, sin], axis=1))


def _rope(v, cos, sin_lo, sin_hi):
    width = v.shape[1]
    rep = (1, width // 128)
    return (v * jnp.tile(cos, rep) + pltpu.roll(v, width - 32, 1) * jnp.tile(sin_lo, rep)
            + pltpu.roll(v, 32, 1) * jnp.tile(sin_hi, rep))


def _rope_bwd(d, cos, sin_lo, sin_hi):
    width = d.shape[1]
    rep = (1, width // 128)
    return (d * jnp.tile(cos, rep) + pltpu.roll(d * jnp.tile(sin_lo, rep), 32, 1)
            + pltpu.roll(d * jnp.tile(sin_hi, rep), width - 32, 1))


def _layer_norm(v, g, b):
    mu = jnp.mean(v, axis=-1, keepdims=True)
    vc = v - mu
    rstd = lax.rsqrt(jnp.mean(vc * vc, axis=-1, keepdims=True) + NORM_EPS)
    vhat = vc * rstd
    return vhat * g + b, vhat, rstd


def _tril_bf16(w_ref, g):
    t = lax.broadcasted_iota(jnp.int32, (CHUNK, CHUNK), 0)
    tp = lax.broadcasted_iota(jnp.int32, (CHUNK, CHUNK), 1)
    return jnp.where(tp <= t, w_ref[g], 0.0).astype(BF16)


def _bias_columns(b_ref, out_ref):
    for g in range(A_GROUPS):
        out_ref[g] = jnp.broadcast_to(b_ref[pl.ds(g, 1), :], (CHUNK, CHUNK)).T


def _band_mask():
    kj = lax.broadcasted_iota(jnp.int32, (2 * CHUNK, 4 * CHUNK), 0)
    qi = lax.broadcasted_iota(jnp.int32, (2 * CHUNK, 4 * CHUNK), 1) & (CHUNK - 1)
    rel = qi + CHUNK - kj
    return jnp.where((rel >= 0) & (rel < CHUNK), 0.0, -jnp.inf)


def _low_lanes():
    return lax.broadcasted_iota(jnp.int32, (1, 128), 1) < HEAD_DIM


def _stack_heads(pair_a, pair_b):
    lo = _low_lanes()
    return jnp.concatenate([jnp.where(lo, pair_a, 0.0), jnp.where(lo, 0.0, pair_a),
                            jnp.where(lo, pair_b, 0.0), jnp.where(lo, 0.0, pair_b)], axis=0).astype(BF16)


def _heads_to_lanes(per_group):
    rows = [t[:, r * CHUNK:(r + 1) * CHUNK] for t in per_group for r in range(4)]
    return jnp.concatenate(rows, axis=0).T


def _dup_kv_head(band, gk):
    pair = band[:, (gk // 2) * 128:(gk // 2 + 1) * 128]
    lo = _low_lanes()
    one = jnp.where(lo if gk % 2 == 0 else jnp.logical_not(lo), pair, 0.0)
    return (one + pltpu.roll(one, HEAD_DIM, 1)).astype(BF16)


def _fold_kv_head(dup_grad, gk):
    both = dup_grad + pltpu.roll(dup_grad, HEAD_DIM, 1)
    lo = _low_lanes()
    return jnp.where(lo if gk % 2 == 0 else jnp.logical_not(lo), both, 0.0)


def _attn_probs(q_st, k_dup, sink_row, mask, first_block):
    s = lax.dot_general(k_dup, q_st, NT, preferred_element_type=F32) + mask
    s = jnp.concatenate([jnp.where(first_block, -jnp.inf, s[:CHUNK]), s[CHUNK:]], axis=0)
    m = jnp.maximum(jnp.max(s, axis=0, keepdims=True), sink_row)
    p = jnp.exp(s - m)
    e_sink = jnp.exp(sink_row - m)
    inv = 1.0 / (jnp.sum(p, axis=0, keepdims=True) + e_sink)
    return p * inv, e_sink * inv


def _sink_row(sinks_ref, gk):
    return jnp.concatenate([jnp.full((1, CHUNK), sinks_ref[4 * gk + r], F32) for r in range(4)], axis=1)


def _mixer_specs(nb, rev):
    def blk(i):
        return nb - 1 - i if rev else i

    def prev(i):
        return jnp.maximum(blk(i) - 1, 0)

    tab = pl.BlockSpec((CHUNK, 128), lambda i, *_: (blk(i), 0))
    tab_prev = pl.BlockSpec((CHUNK, 128), lambda i, *_: (prev(i), 0))
    return dict(
        cur=pl.BlockSpec((CHUNK, D_IN), lambda i, *_: (blk(i), 0)),
        prev_kv=pl.BlockSpec((CHUNK, 2 * 256), lambda i, *_: (prev(i), OFF_K // 512)),
        tabs=[tab] * 3 + [tab_prev] * 3,
        vec=pl.BlockSpec((1, D_A), lambda i, *_: (0, 0)),
        wsp=pl.BlockSpec((A_GROUPS, CHUNK, CHUNK), lambda i, *_: (0, 0, 0)),
        bsp=pl.BlockSpec((A_GROUPS, CHUNK), lambda i, *_: (0, 0)),
        smem=pl.BlockSpec(memory_space=pltpu.SMEM),
        blk=blk,
    )


def _mixer_fwd(proj, tabs, ln_g, ln_b, w_sp, b_sp, sinks, wo_all):
    s = proj.shape[0]
    nb = s // CHUNK
    sp = _mixer_specs(nb, rev=False)

    def body(cur_ref, pkv_ref, c_ref, s1_ref, s2_ref, cp_ref, s1p_ref, s2p_ref, lg_ref, lb_ref, w_ref, b_ref,
             sinks_ref, wo_in, y_ref, wo_ref, bcol, mask, send_sems, recv_sems):
        i = pl.program_id(0)
        gather = _InPlaceGather(wo_ref, send_sems, recv_sems)

        @pl.when(i == 0)
        def _():
            gather.start()
            _bias_columns(b_ref, bcol)
            mask[...] = _band_mask()

        @pl.when(i == nb // 2)
        def _():
            for j in range(3):
                gather.pass_on(j)

        vln, _, _ = _layer_norm(cur_ref[:, OFF_VA:OFF_ZA].astype(F32), lg_ref[...], lb_ref[...])
        vln = vln.astype(BF16)
        for g in range(A_GROUPS):
            cols = slice(g * 128, (g + 1) * 128)
            sg = jnp.dot(_tril_bf16(w_ref, g), vln[:, cols], preferred_element_type=F32) + bcol[g]
            u = cur_ref[:, OFF_U + g * 128:OFF_U + (g + 1) * 128].astype(F32)
            z = cur_ref[:, OFF_ZA + g * 128:OFF_ZA + (g + 1) * 128].astype(F32)
            y_ref[:, cols] = (u * sg * (z * _sigmoid(z))).astype(BF16)

        cur_t = (c_ref[...], s1_ref[...], s2_ref[...])
        prev_t = (cp_ref[...], s1p_ref[...], s2p_ref[...])
        qr = _rope(cur_ref[:, OFF_Q:OFF_K].astype(F32), *cur_t) * ATTN_SCALE
        kr = jnp.concatenate([_rope(pkv_ref[:, 0:256].astype(F32), *prev_t),
                              _rope(cur_ref[:, OFF_K:OFF_V].astype(F32), *cur_t)], axis=0)
        v_t = jnp.concatenate([pkv_ref[:, 256:512], cur_ref[:, OFF_V:OFF_ZB]], axis=0).astype(F32).T.astype(BF16)
        outs = []
        for gk in range(N_KV_HEADS):
            q_st = _stack_heads(qr[:, (2 * gk) * 128:(2 * gk + 1) * 128], qr[:, (2 * gk + 1) * 128:(2 * gk + 2) * 128])
            probs, _ = _attn_probs(q_st, _dup_kv_head(kr, gk), _sink_row(sinks_ref, gk), mask[...], i == 0)
            outs.append(jnp.dot(v_t[gk * HEAD_DIM:(gk + 1) * HEAD_DIM], probs.astype(BF16),
                                preferred_element_type=F32))
        zb = cur_ref[:, OFF_ZB:D_IN].astype(F32)
        y_ref[:, D_A:D_MODEL] = (_heads_to_lanes(outs) * (zb * _sigmoid(zb))).astype(BF16)

        @pl.when(i == nb - 1)
        def _():
            gather.wait_sibling(0)
            for j in range(3):
                gather.wait_sibling(4 + j)
            gather.wait_sends()

    hbm = pl.BlockSpec(memory_space=pl.ANY)
    return pl.pallas_call(
        body, name="mixer_fwd", grid=(nb,),
        in_specs=[sp["cur"], sp["prev_kv"], *sp["tabs"], sp["vec"], sp["vec"], sp["wsp"], sp["bsp"], sp["smem"], hbm],
        out_specs=[pl.BlockSpec((CHUNK, D_MODEL), lambda i: (i, 0)), hbm],
        out_shape=[SDS((s, D_MODEL), BF16), SDS(wo_all.shape, wo_all.dtype)],
        scratch_shapes=[pltpu.VMEM((A_GROUPS, CHUNK, CHUNK), F32), pltpu.VMEM((2 * CHUNK, 4 * CHUNK), F32),
                        *_gather_scratch()],
        input_output_aliases={13: 1},
        compiler_params=_params("arbitrary"),
    )(proj, proj, *tabs, *tabs, ln_g, ln_b, w_sp, b_sp, sinks, wo_all)


def _out_proj_loss(y, x, target, wo, gate, shift_f, scale_f, fng):
    s = y.shape[0]
    tm, tr = 256, 128
    nt = s // tm

    def body(y_ref, x_ref, t_ref, wo_ref, gate_ref, sh_ref, sc_ref, g_ref, dx1_ref, do_ref, dy_ref, sums_ref,
             do_last, do_work):
        i = pl.program_id(0)

        @pl.when(i == 0)
        def _():
            sums_ref[...] = jnp.zeros_like(sums_ref)
            do_last[...] = jnp.zeros_like(do_last)

        do_work[...] = do_last[...]
        o = jnp.dot(y_ref[...], wo_ref[...], preferred_element_type=F32)
        gate, g, sh = gate_ref[...], g_ref[...], sh_ref[...]
        one_sc = 1.0 + sc_ref[...]
        cs, inv_d = g * one_sc, 1.0 / D_MODEL

        def rowsum(v):
            return jnp.sum(v, axis=0, keepdims=True)

        sums = [jnp.zeros((1, D_MODEL), F32) for _ in range(4)]
        for c in range(tm // tr):
            rows = slice(c * tr, (c + 1) * tr)
            oc = o[rows]
            x1 = x_ref[rows, :] + gate * oc
            r = lax.rsqrt(jnp.sum(x1 * x1, axis=-1, keepdims=True) * inv_d + NORM_EPS)
            x1n = x1 * r
            diff = x1n * cs + sh - t_ref[rows, :]
            w = diff * x1n
            lane_sum = jnp.sum(w * cs, axis=-1, keepdims=True)
            dx1 = (diff * cs) * (r * inv_d) - x1n * (r * lane_sum * (inv_d * inv_d))
            dx1_ref[rows, :] = dx1
            do = (dx1 * gate).astype(BF16)
            do_ref[rows, :] = do
            do_last[rows, :] = do
            for k, v in enumerate((dx1 * oc, diff, w, diff * diff)):
                sums[k] = sums[k] + rowsum(v)
        live = jnp.where(i < nt, 1.0, 0.0)
        sums_ref[0:1, :] += live * sums[0]
        sums_ref[1:2, :] += (live * inv_d) * sums[1]
        sums_ref[2:3, :] += (live * inv_d) * (sums[2] * g)
        sums_ref[3:4, :] += (live * inv_d) * (sums[2] * one_sc)
        sums_ref[4:5, :] += live * sums[3]
        dy_ref[...] = lax.dot_general(do_work[...], wo_ref[...], NT, preferred_element_type=F32).astype(BF16)

    tile = pl.BlockSpec((tm, D_MODEL), lambda i: (jnp.minimum(i, nt - 1), 0))
    row = pl.BlockSpec((1, D_MODEL), lambda i: (0, 0))
    return pl.pallas_call(
        body, name="out_proj_loss", grid=(nt + 1,),
        in_specs=[tile, tile, tile, pl.BlockSpec((D_MODEL, D_MODEL), lambda i: (0, 0)), row, row, row, row],
        out_specs=[tile, tile, pl.BlockSpec((tm, D_MODEL), lambda i: (jnp.maximum(i - 1, 0), 0)),
                   pl.BlockSpec((8, D_MODEL), lambda i: (0, 0))],
        out_shape=[SDS((s, D_MODEL), F32), SDS((s, D_MODEL), BF16), SDS((s, D_MODEL), BF16), SDS((8, D_MODEL), F32)],
        scratch_shapes=[pltpu.VMEM((tm, D_MODEL), BF16), pltpu.VMEM((tm, D_MODEL), BF16)],
        compiler_params=_params("arbitrary"),
    )(y, x, target, wo, gate, shift_f, scale_f, fng)


ROW_DBSP, ROW_DSINKS, MISC_ROWS = 0, 8, 32


def _mixer_bwd(me, proj, dy, tabs, ln_g, ln_b, w_sp, b_sp, sinks, pair):
    s = proj.shape[0]
    nb = s // CHUNK
    sp = _mixer_specs(nb, rev=True)

    def body(me_ref, cur_ref, pkv_ref, dy_ref, c_ref, s1_ref, s2_ref, cp_ref, s1p_ref, s2p_ref, lg_ref, lb_ref, w_ref,
             b_ref, sinks_ref, pair_ref, dproj_ref, dln_ref, dw_ref, misc_ref, parts_ref, bcol, dbcol, carry, mask,
             send_sems, recv_sems):
        i = pl.program_id(0)
        block = nb - 1 - i

        @pl.when(i == 0)
        def _():
            for cp in _chip_scatter(pair_ref, parts_ref, send_sems, recv_sems):
                cp.start()
            _bias_columns(b_ref, bcol)
            mask[...] = _band_mask()
            dbcol[...] = jnp.zeros_like(dbcol)
            carry[...] = jnp.zeros_like(carry)
            dln_ref[...] = jnp.zeros_like(dln_ref)
            dw_ref[...] = jnp.zeros_like(dw_ref)
            misc_ref[...] = jnp.zeros_like(misc_ref)

        vln, vhat, rstd = _layer_norm(cur_ref[:, OFF_VA:OFF_ZA].astype(F32), lg_ref[...], lb_ref[...])
        vln = vln.astype(BF16)
        d_vln = []
        for g in range(A_GROUPS):
            cols = slice(g * 128, (g + 1) * 128)
            w_g = _tril_bf16(w_ref, g)
            sg = jnp.dot(w_g, vln[:, cols], preferred_element_type=F32) + bcol[g]
            u = cur_ref[:, OFF_U + g * 128:OFF_U + (g + 1) * 128].astype(F32)
            z = cur_ref[:, OFF_ZA + g * 128:OFF_ZA + (g + 1) * 128].astype(F32)
            dya = dy_ref[:, cols].astype(F32)
            sig = _sigmoid(z)
            d_ya = dya * (z * sig)
            dproj_ref[:, OFF_ZA + g * 128:OFF_ZA + (g + 1) * 128] = (
                dya * (u * sg) * (sig * (1.0 + z * (1.0 - sig)))).astype(BF16)
            dproj_ref[:, OFF_U + g * 128:OFF_U + (g + 1) * 128] = (d_ya * sg).astype(BF16)
            d_s = d_ya * u
            dbcol[g] += d_s
            d_sb = d_s.astype(BF16)
            dw_ref[g] += lax.dot_general(d_sb, vln[:, cols], NT, preferred_element_type=F32)
            d_vln.append(lax.dot_general(w_g, d_sb, TN, preferred_element_type=F32))
        d_vln = jnp.concatenate(d_vln, axis=1)
        dln_ref[0:1, :] += jnp.sum(d_vln * vhat, axis=0, keepdims=True)
        dln_ref[1:2, :] += jnp.sum(d_vln, axis=0, keepdims=True)
        d_vhat = d_vln * lg_ref[...]
        d_va = rstd * (d_vhat - jnp.mean(d_vhat, axis=-1, keepdims=True)
                       - vhat * jnp.mean(d_vhat * vhat, axis=-1, keepdims=True))
        dproj_ref[:, OFF_VA:OFF_ZA] = d_va.astype(BF16)

        cur_t = (c_ref[...], s1_ref[...], s2_ref[...])
        prev_t = (cp_ref[...], s1p_ref[...], s2p_ref[...])
        band_t = tuple(jnp.concatenate([p, c], axis=0) for p, c in zip(prev_t, cur_t))
        qr = _rope(cur_ref[:, OFF_Q:OFF_K].astype(F32), *cur_t) * ATTN_SCALE
        kr = jnp.concatenate([_rope(pkv_ref[:, 0:256].astype(F32), *prev_t),
                              _rope(cur_ref[:, OFF_K:OFF_V].astype(F32), *cur_t)], axis=0)
        vb = jnp.concatenate([pkv_ref[:, 256:512], cur_ref[:, OFF_V:OFF_ZB]], axis=0).astype(F32)
        k_t, v_t = (kr.T * ATTN_SCALE).astype(BF16), vb.T.astype(BF16)
        zb = cur_ref[:, OFF_ZB:D_IN].astype(F32)
        dyb = dy_ref[:, D_A:D_MODEL].astype(F32)
        sig = _sigmoid(zb)
        d_yb = dyb * (zb * sig)
        outs, dqs = [], []
        dk_pairs = [jnp.zeros((2 * CHUNK, 128), F32) for _ in range(2)]
        dv_pairs = [jnp.zeros((2 * CHUNK, 128), F32) for _ in range(2)]
        for gk in range(N_KV_HEADS):
            heads = slice(gk * HEAD_DIM, (gk + 1) * HEAD_DIM)
            q_st = _stack_heads(qr[:, (2 * gk) * 128:(2 * gk + 1) * 128], qr[:, (2 * gk + 1) * 128:(2 * gk + 2) * 128])
            k_dup, v_dup = _dup_kv_head(kr, gk), _dup_kv_head(vb, gk)
            probs, p_sink = _attn_probs(q_st, k_dup, _sink_row(sinks_ref, gk), mask[...], block == 0)
            probs_b = probs.astype(BF16)
            outs.append(jnp.dot(v_t[heads], probs_b, preferred_element_type=F32))
            do_st = _stack_heads(d_yb[:, (2 * gk) * 128:(2 * gk + 1) * 128], d_yb[:, (2 * gk + 1) * 128:(2 * gk + 2) * 128])
            dp = lax.dot_general(v_dup, do_st, NT, preferred_element_type=F32)
            delta = jnp.sum(probs * dp, axis=0, keepdims=True)
            ds = (probs * (dp - delta)).astype(BF16)
            d_sink = -p_sink * delta
            for r in range(4):
                row = ROW_DSINKS + 4 * gk + r
                misc_ref[row:row + 1, :] += jnp.broadcast_to(
                    jnp.sum(d_sink[:, r * CHUNK:(r + 1) * CHUNK], axis=1, keepdims=True), (1, 128))
            dqs.append(jnp.dot(k_t[heads], ds, preferred_element_type=F32))
            dk_pairs[gk // 2] += _fold_kv_head(jnp.dot(ds, q_st, preferred_element_type=F32), gk)
            dv_pairs[gk // 2] += _fold_kv_head(jnp.dot(probs_b, do_st, preferred_element_type=F32), gk)
        dproj_ref[:, OFF_ZB:D_IN] = (dyb * _heads_to_lanes(outs) * (sig * (1.0 + zb * (1.0 - sig)))).astype(BF16)
        dproj_ref[:, OFF_Q:OFF_K] = _rope_bwd(_heads_to_lanes(dqs), *cur_t).astype(BF16)
        dk_band = _rope_bwd(jnp.concatenate(dk_pairs, axis=1), *band_t)
        dv_band = jnp.concatenate(dv_pairs, axis=1)
        dproj_ref[:, OFF_K:OFF_V] = (dk_band[CHUNK:] + carry[:, 0:256]).astype(BF16)
        dproj_ref[:, OFF_V:OFF_ZB] = (dv_band[CHUNK:] + carry[:, 256:512]).astype(BF16)
        carry[:, 0:256] = dk_band[:CHUNK]
        carry[:, 256:512] = dv_band[:CHUNK]

        @pl.when(i == nb - 1)
        def _():
            t = lax.broadcasted_iota(jnp.int32, (CHUNK, CHUNK), 0)
            tp = lax.broadcasted_iota(jnp.int32, (CHUNK, CHUNK), 1)
            for g in range(A_GROUPS):
                dw_ref[g] = jnp.where(tp <= t, dw_ref[g], 0.0)
                misc_ref[pl.ds(ROW_DBSP + g, 1), :] = jnp.sum(dbcol[g].T, axis=0, keepdims=True)
            scatter = _chip_scatter(pair_ref, parts_ref, send_sems, recv_sems)
            for cp in scatter:
                cp.wait_recv()
            for cp in scatter:
                cp.wait_send()

    blk = sp["blk"]
    hbm = pl.BlockSpec(memory_space=pl.ANY)
    return pl.pallas_call(
        body, name="mixer_bwd",
        grid_spec=pltpu.PrefetchScalarGridSpec(
            num_scalar_prefetch=1, grid=(nb,),
            in_specs=[sp["cur"], sp["prev_kv"], pl.BlockSpec((CHUNK, D_MODEL), lambda i, me_ref: (blk(i), 0)),
                      *sp["tabs"], sp["vec"], sp["vec"], sp["wsp"], sp["bsp"], sp["smem"], hbm],
            out_specs=[pl.BlockSpec((CHUNK, D_IN), lambda i, me_ref: (blk(i), 0)),
                       pl.BlockSpec((8, D_A), lambda i, me_ref: (me_ref[0], 0)),
                       pl.BlockSpec((A_GROUPS, CHUNK, CHUNK), lambda i, me_ref: (me_ref[0], 0, 0)),
                       pl.BlockSpec((MISC_ROWS, 128), lambda i, me_ref: (me_ref[0], 0)), hbm],
            scratch_shapes=[pltpu.VMEM((A_GROUPS, CHUNK, CHUNK), F32), pltpu.VMEM((A_GROUPS, CHUNK, CHUNK), F32),
                            pltpu.VMEM((CHUNK, 512), F32), pltpu.VMEM((2 * CHUNK, 4 * CHUNK), F32),
                            *_scatter_scratch()]),
        out_shape=[SDS((s, D_IN), BF16), SDS((N_DEV * 8, D_A), F32), SDS((N_DEV * A_GROUPS, CHUNK, CHUNK), F32),
                   SDS((N_DEV * MISC_ROWS, 128), F32), SDS((3,) + pair.shape[1:], pair.dtype)],
        compiler_params=_params("arbitrary"),
    )(me, proj, proj, dy, *tabs, *tabs, ln_g, ln_b, w_sp, b_sp, sinks, pair)


def _wgrad(name, a, b, bm, gathers=()):
    s, m = a.shape
    n = b.shape[1]
    bt = min(1024, s)
    steps = s // bt
    n_g = len(gathers)

    def body(*refs):
        a_ref, b_ref = refs[:2]
        out_ref, bufs = refs[2 + n_g], refs[3 + n_g:3 + 2 * n_g]
        acc, sems = refs[3 + 2 * n_g], refs[4 + 2 * n_g:]
        i, t = pl.program_id(0), pl.program_id(1)
        jobs = [_InPlaceGather(bufs[k], sems[2 * k], sems[2 * k + 1]) for k in range(n_g)]

        @pl.when((i == 0) & (t == 0))
        def _():
            for job in jobs:
                job.start()

        @pl.when((i == (m // bm) // 2) & (t == 0))
        def _():
            for job in jobs:
                for j in range(3):
                    job.pass_on(j)

        @pl.when(t == 0)
        def _():
            acc[...] = jnp.zeros_like(acc)

        acc[...] += lax.dot_general(a_ref[...], b_ref[...], TN, preferred_element_type=F32)

        @pl.when(t == steps - 1)
        def _():
            out_ref[...] = acc[...].astype(out_ref.dtype)

        @pl.when((i == m // bm - 1) & (t == steps - 1))
        def _():
            for job in jobs:
                job.wait_sibling(0)
                for j in range(3):
                    job.wait_sibling(4 + j)
                job.wait_sends()

    hbm = pl.BlockSpec(memory_space=pl.ANY)
    outs = pl.pallas_call(
        body, name=name, grid=(m // bm, steps),
        in_specs=[pl.BlockSpec((bt, bm), lambda i, t: (t, i)), pl.BlockSpec((bt, n), lambda i, t: (t, 0))] + [hbm] * n_g,
        out_specs=[pl.BlockSpec((bm, n), lambda i, t: (i, 0))] + [hbm] * n_g,
        out_shape=[SDS((m, n), BF16)] + [SDS(g.shape, g.dtype) for g in gathers],
        scratch_shapes=[pltpu.VMEM((bm, n), F32)] + _gather_scratch() * n_g,
        input_output_aliases={2 + k: 1 + k for k in range(n_g)},
        compiler_params=_params("arbitrary", "arbitrary"),
    )(a, b, *gathers)
    return outs[0], outs[1:]


def _in_proj_bwd(dproj, wt, x, dx1, scale, norm_g, pair):
    s = x.shape[0]
    tm, tk, tr = min(1024, s), D_IN // 4, 64
    ksteps = D_IN // tk

    def body(dp_ref, wt_ref, x_hbm, dx1_hbm, sc_ref, g_ref, pair_ref, gx_ref, sums_ref, parts_ref, x_buf, dx1_buf,
             tile_sems, send_sems, recv_sems):
        i, k = pl.program_id(0), pl.program_id(1)

        def tile_copies():
            rows = pl.ds(pl.multiple_of(i * tm, tm), tm)
            return (pltpu.make_async_copy(x_hbm.at[rows], x_buf, tile_sems.at[0]),
                    pltpu.make_async_copy(dx1_hbm.at[rows], dx1_buf, tile_sems.at[1]))

        @pl.when((i == 0) & (k == 0))
        def _():
            for cp in _chip_scatter(pair_ref, parts_ref, send_sems, recv_sems):
                cp.start()
            sums_ref[...] = jnp.zeros_like(sums_ref)

        @pl.when(k == 0)
        def _():
            for cp in tile_copies():
                cp.start()
            gx_ref[...] = jnp.dot(dp_ref[...], wt_ref[...], preferred_element_type=F32)

        @pl.when(k > 0)
        def _():
            gx_ref[...] += jnp.dot(dp_ref[...], wt_ref[...], preferred_element_type=F32)

        @pl.when(k == ksteps - 1)
        def _():
            for cp in tile_copies():
                cp.wait()
            one_sc, g = 1.0 + sc_ref[...], g_ref[...]
            cs = one_sc * g

            def chunk(j, sums):
                rows = pl.ds(pl.multiple_of(j * tr, tr), tr)
                dh, xv = gx_ref[rows, :], x_buf[rows, :]
                dhx = dh * xv
                r = lax.rsqrt(jnp.sum(xv * xv, axis=-1, keepdims=True) * (1.0 / D_MODEL) + NORM_EPS)
                coef = (r * r * r) * (jnp.sum(dhx * cs, axis=-1, keepdims=True) * (1.0 / D_MODEL))
                gx_ref[rows, :] = dx1_buf[rows, :] + r * (dh * cs) - xv * coef
                return (sums[0] + jnp.sum(dh, axis=0, keepdims=True), sums[1] + jnp.sum(dhx * r, axis=0, keepdims=True))

            zero = jnp.zeros((1, D_MODEL), F32)
            sums = lax.fori_loop(0, tm // tr, chunk, (zero, zero))
            sums_ref[0:1, :] += sums[0]
            sums_ref[1:2, :] += sums[1] * g
            sums_ref[2:3, :] += sums[1] * one_sc

        @pl.when((i == s // tm - 1) & (k == ksteps - 1))
        def _():
            scatter = _chip_scatter(pair_ref, parts_ref, send_sems, recv_sems)
            for cp in scatter:
                cp.wait_recv()
            for cp in scatter:
                cp.wait_send()

    row = pl.BlockSpec((1, D_MODEL), lambda i, k: (0, 0))
    hbm = pl.BlockSpec(memory_space=pl.ANY)
    return pl.pallas_call(
        body, name="in_proj_bwd", grid=(s // tm, ksteps),
        in_specs=[pl.BlockSpec((tm, tk), lambda i, k: (i, k)), pl.BlockSpec((tk, D_MODEL), lambda i, k: (k, 0)),
                  hbm, hbm, row, row, hbm],
        out_specs=[pl.BlockSpec((tm, D_MODEL), lambda i, k: (i, 0)), pl.BlockSpec((8, D_MODEL), lambda i, k: (0, 0)),
                   hbm],
        out_shape=[SDS((s, D_MODEL), F32), SDS((8, D_MODEL), F32), SDS((3,) + pair.shape[1:], pair.dtype)],
        scratch_shapes=[pltpu.VMEM((tm, D_MODEL), F32), pltpu.VMEM((tm, D_MODEL), F32),
                        pltpu.SemaphoreType.DMA((2,)), *_scatter_scratch()],
        compiler_params=_params("arbitrary", "arbitrary"),
    )(dproj, wt, x, dx1, scale, norm_g, pair)


def _pair_sum(core, grad, got):
    _, m, n = got.shape

    def body(core_ref, a_ref, b_ref, out_ref):
        out_ref[...] = (a_ref[...].astype(F32) + b_ref[...].astype(F32)).astype(BF16)

    blk = pl.BlockSpec((1, m, n), lambda q, core_ref: (q, 0, 0))
    return pl.pallas_call(
        body, name=f"pair_sum_{m}",
        grid_spec=pltpu.PrefetchScalarGridSpec(
            num_scalar_prefetch=1, grid=(4,),
            in_specs=[pl.BlockSpec((1, m, n), lambda q, core_ref: (2 * q + core_ref[0], 0, 0)), blk], out_specs=blk),
        out_shape=SDS(got.shape, BF16), compiler_params=_params("parallel"),
    )(core, grad, got)


def _sum_chips(own_ref, parts_ref):
    return ((own_ref[0].astype(F32) + parts_ref[0].astype(F32)) + parts_ref[1].astype(F32)) + parts_ref[2].astype(F32)


def _adam_rows(name, chip, pair, parts, w, m, v):
    rows = w.shape[0]
    tr = rows // 4

    def body(chip_ref, own_ref, p_ref, w_ref, m_ref, v_ref, g_ref, d_ref, nm_ref, nv_ref):
        g = _sum_chips(own_ref, p_ref)
        g_ref[...] = g
        d_ref[...], nm_ref[...], nv_ref[...] = _adamw(w_ref[...], g, m_ref[...], v_ref[...])

    blk = pl.BlockSpec((tr, D_MODEL), lambda j, chip_ref: (j, 0))
    return pl.pallas_call(
        body, name=name,
        grid_spec=pltpu.PrefetchScalarGridSpec(
            num_scalar_prefetch=1, grid=(rows // tr,),
            in_specs=[pl.BlockSpec((1, tr, D_MODEL), lambda j, chip_ref: (chip_ref[0], j, 0)),
                      pl.BlockSpec((3, tr, D_MODEL), lambda j, chip_ref: (0, j, 0)), blk, blk, blk],
            out_specs=[blk] * 4),
        out_shape=[SDS(w.shape, F32)] * 4, compiler_params=_params("parallel"),
    )(chip, pair, parts, w, m, v)


def _adam_ada(name, cact, dmod, w, m, v):
    n = w.shape[1]
    tr = 512

    def body(c_ref, dm_ref, w_ref, m_ref, v_ref, g_ref, d_ref, nm_ref, nv_ref):
        pad_c = jnp.concatenate([c_ref[...], jnp.zeros_like(c_ref)], axis=0).astype(BF16)
        pad_d = jnp.concatenate([dm_ref[...], jnp.zeros_like(dm_ref)], axis=0).astype(BF16)
        g = lax.dot_general(pad_c, pad_d, TN, preferred_element_type=F32)
        g_ref[...] = g
        d_ref[...], nm_ref[...], nv_ref[...] = _adamw(w_ref[...], g, m_ref[...], v_ref[...])

    blk = pl.BlockSpec((tr, n), lambda j: (j, 0))
    return pl.pallas_call(
        body, name=name, grid=(D_MODEL // tr,),
        in_specs=[pl.BlockSpec((N_DEV, tr), lambda j: (0, j)), pl.BlockSpec((N_DEV, n), lambda j: (0, 0)),
                  blk, blk, blk],
        out_specs=[blk] * 4, out_shape=[SDS(w.shape, F32)] * 4,
        compiler_params=_params("parallel"),
    )(cact, dmod, w, m, v)


SMALL_PARAMS = ("w_spatial", "b_spatial", "sinks", "norm_g", "ln_v_g", "ln_v_b", "final_norm_g", "b_ada", "b_ada_final")


def _adam_small(d_wsp, misc, d_ln, sums_i, sums_o, params):
    n_p = len(SMALL_PARAMS)

    def body(*refs):
        wsp_ref, misc_ref, ln_ref, si_ref, so_ref = refs[:5]
        wmv = [refs[5 + 3 * k:8 + 3 * k] for k in range(n_p)]
        loss_ref = refs[5 + 3 * n_p]
        outs = [refs[6 + 3 * n_p + 4 * k:10 + 3 * n_p + 4 * k] for k in range(n_p)]

        def total(ref, rows=None):
            def part(j):
                return ref[j] if rows is None else ref[j, rows[0]:rows[1], :]
            acc = part(0)
            for j in range(1, N_DEV):
                acc = acc + part(j)
            return acc

        sink_rows = total(misc_ref, (ROW_DSINKS, ROW_DSINKS + 16))
        diag = (lax.broadcasted_iota(jnp.int32, (16, 128), 0) == lax.broadcasted_iota(jnp.int32, (16, 128), 1))
        grads = dict(
            w_spatial=total(wsp_ref), b_spatial=total(misc_ref, (ROW_DBSP, ROW_DBSP + A_GROUPS)),
            sinks=jnp.sum(jnp.where(diag, sink_rows, 0.0), axis=0, keepdims=True),
            norm_g=total(si_ref, (2, 3)), ln_v_g=total(ln_ref, (0, 1)), ln_v_b=total(ln_ref, (1, 2)),
            final_norm_g=total(so_ref, (3, 4)),
            b_ada=jnp.concatenate([total(si_ref, (0, 1)), total(si_ref, (1, 2)), total(so_ref, (0, 1))], axis=1),
            b_ada_final=jnp.concatenate([total(so_ref, (1, 2)), total(so_ref, (2, 3))], axis=1))
        sq_err = jnp.sum(total(so_ref, (4, 5)), axis=1, keepdims=True)
        loss_ref[...] = jnp.broadcast_to(sq_err * (0.5 / D_MODEL), (1, 128))
        for k, name in enumerate(SMALL_PARAMS):
            w_ref, m_ref, v_ref = wmv[k]
            g_ref, d_ref, nm_ref, nv_ref = outs[k]
            g_ref[...] = grads[name]
            d_ref[...], nm_ref[...], nv_ref[...] = _adamw(w_ref[...], grads[name], m_ref[...], v_ref[...])

    flat = [a for name in SMALL_PARAMS for a in params[name]]
    vmem = pl.BlockSpec(memory_space=pltpu.VMEM)
    out_shape = [SDS((1, 128), F32)] + [SDS(params[name][0].shape, F32) for name in SMALL_PARAMS for _ in range(4)]
    outs = pl.pallas_call(
        body, name="adam_small", in_specs=[vmem] * (5 + len(flat)), out_specs=[vmem] * len(out_shape),
        out_shape=out_shape, compiler_params=_params(),
    )(d_wsp, misc, d_ln, sums_i, sums_o, *flat)
    return outs[0], {name: outs[1 + 4 * k:5 + 4 * k] for k, name in enumerate(SMALL_PARAMS)}


def kernel(x, c, w_ada, b_ada, norm_g, w_in, ln_v_g, ln_v_b, w_spatial, b_spatial, sinks, w_out, w_ada_final, b_ada_final, final_norm_g, loss_target, m_w_ada, m_b_ada, m_norm_g, m_w_in, m_ln_v_g, m_ln_v_b, m_w_spatial, m_b_spatial, m_sinks, m_w_out, m_w_ada_final, m_b_ada_final, m_final_norm_g, v_w_ada, v_b_ada, v_norm_g, v_w_in, v_ln_v_g, v_ln_v_b, v_w_spatial, v_b_spatial, v_sinks, v_w_out, v_w_ada_final, v_b_ada_final, v_final_norm_g):
    seq = x.shape[1]
    me = 4 * lax.axis_index("x") + 2 * lax.axis_index("y") + lax.axis_index("c")
    x2, tgt = x[0], loss_target[0]
    fng = final_norm_g.reshape(1, D_MODEL)

    n_ada, n_ada_f = w_ada.shape[2], w_ada_final.shape[1]
    cact, mod, mod_f = _ada_exchange(c, w_ada[0], b_ada.reshape(N_DEV, n_ada), w_ada_final,
                                     b_ada_final.reshape(N_DEV, n_ada_f))
    cact = cact.reshape(N_DEV, D_MODEL)
    mod, mod_f = mod.reshape(1, 3 * D_MODEL), mod_f.reshape(1, 2 * D_MODEL)
    shift, scale, gate = mod[:, :D_MODEL], mod[:, D_MODEL:2 * D_MODEL], mod[:, 2 * D_MODEL:]
    shift_f, scale_f = mod_f[:, :D_MODEL], mod_f[:, D_MODEL:]

    wt_f32, m_wt, v_wt = (jnp.swapaxes(a, 1, 2)[0] for a in (w_in, m_w_in, v_w_in))
    xi, yi = lax.axis_index("x"), lax.axis_index("y")
    chip_order = jnp.stack([2 * xi + yi, 2 * (1 - xi) + yi, 2 * xi + 1 - yi, 2 * (1 - xi) + 1 - yi]).astype(jnp.int32)
    wt_mine, wo_mine = _prep_weights(me.reshape(1), wt_f32, w_out[0])

    tabs = _rope_tables(seq)
    sinks_v = sinks.reshape(16)
    h, proj, wt = _gather_in_proj(chip_order, x2, shift, scale, norm_g, wt_mine)
    y, wo = _mixer_fwd(proj, tabs, ln_v_g, ln_v_b, w_spatial[0], b_spatial[0], sinks_v, wo_mine)
    dx1, do, dy, sums_o = _out_proj_loss(y, x2, tgt, wo, gate, shift_f, scale_f, fng)

    core = lax.axis_index("c").reshape(1)
    chip = (2 * lax.axis_index("x") + lax.axis_index("y")).reshape(1)
    g_wo, _ = _wgrad("wgrad_out", y, do, 1024)
    g_wo = g_wo.reshape(N_DEV, D_MODEL // N_DEV, D_MODEL)
    pair_out = _pair_sum(core, g_wo, _rs_pair("rs_pair_out", g_wo))
    dproj, d_ln, d_wsp, misc, parts_out = _mixer_bwd(
        me.reshape(1), proj, dy, tabs, ln_v_g, ln_v_b, w_spatial[0], b_spatial[0], sinks_v, pair_out)
    g_wt, (d_ln, d_wsp, misc) = _wgrad("wgrad_in", dproj, h, 1408,
                                       gathers=(d_ln, d_wsp.reshape(N_DEV * A_GROUPS * CHUNK, CHUNK), misc))
    g_wt = g_wt.reshape(N_DEV, D_IN // N_DEV, D_MODEL)
    pair_in = _pair_sum(core, g_wt, _rs_pair("rs_pair_in", g_wt))
    grad_x, sums_i, parts_in = _in_proj_bwd(dproj, wt, x2, dx1, scale, norm_g, pair_in)
    wt_leaves = [jnp.swapaxes(a[None], 1, 2) for a in _adam_rows("adam_w_in", chip, pair_in, parts_in, wt_f32, m_wt, v_wt)]
    w_out_leaves = [a[None] for a in _adam_rows("adam_w_out", chip, pair_out, parts_out, w_out[0], m_w_out[0], v_w_out[0])]

    sums_i, sums_o = _all_gather("gather_sums", [sums_i, sums_o], pltpu.VMEM)
    natural = dict(w_spatial=(A_GROUPS * CHUNK, CHUNK), b_spatial=(A_GROUPS, CHUNK), sinks=(1, 16), norm_g=(1, D_MODEL),
                   ln_v_g=(1, D_A), ln_v_b=(1, D_A), final_norm_g=(1, D_MODEL), b_ada=(1, 3 * D_MODEL),
                   b_ada_final=(1, 2 * D_MODEL))
    given = dict(
        w_spatial=(w_spatial, m_w_spatial, v_w_spatial), b_spatial=(b_spatial, m_b_spatial, v_b_spatial),
        sinks=(sinks, m_sinks, v_sinks), norm_g=(norm_g, m_norm_g, v_norm_g), ln_v_g=(ln_v_g, m_ln_v_g, v_ln_v_g),
        ln_v_b=(ln_v_b, m_ln_v_b, v_ln_v_b), final_norm_g=(final_norm_g, m_final_norm_g, v_final_norm_g),
        b_ada=(b_ada, m_b_ada, v_b_ada), b_ada_final=(b_ada_final, m_b_ada_final, v_b_ada_final))
    params = {name: tuple(a.reshape(natural[name]) for a in given[name]) for name in SMALL_PARAMS}
    params["sinks"] = tuple(jnp.pad(a, ((0, 0), (0, 128 - 16))) for a in params["sinks"])
    loss, small = _adam_small(d_wsp.reshape(N_DEV, A_GROUPS * CHUNK, CHUNK), misc.reshape(N_DEV, MISC_ROWS, 128),
                              d_ln.reshape(N_DEV, 8, D_A), sums_i, sums_o, params)
    small["sinks"] = [a[:, :16] for a in small["sinks"]]
    small = {name: [a.reshape(given[name][0].shape) for a in small[name]] for name in SMALL_PARAMS}

    dmod_all = jnp.concatenate([sums_i[:, 0], sums_i[:, 1], sums_o[:, 0]], axis=1)
    dmod_f_all = jnp.concatenate([sums_o[:, 1], sums_o[:, 2]], axis=1)
    dmod_mine = lax.dynamic_slice_in_dim(dmod_all, me * n_ada, n_ada, axis=1)
    dmod_f_mine = lax.dynamic_slice_in_dim(dmod_f_all, me * n_ada_f, n_ada_f, axis=1)
    ada = _adam_ada("adam_w_ada", cact, dmod_mine, w_ada[0], m_w_ada[0], v_w_ada[0])
    ada_f = _adam_ada("adam_w_ada_final", cact, dmod_f_mine, w_ada_final, m_w_ada_final, v_w_ada_final)

    def leaves(k):
        return (ada[k][None], small["b_ada"][k], small["norm_g"][k], wt_leaves[k], small["ln_v_g"][k],
                small["ln_v_b"][k], small["w_spatial"][k], small["b_spatial"][k], small["sinks"][k], w_out_leaves[k],
                ada_f[k], small["b_ada_final"][k], small["final_norm_g"][k])

    return (loss[0, 0], grad_x[None], *leaves(0), *leaves(1), *leaves(2), *leaves(3))
```

```python
import functools

import jax
import jax.numpy as jnp
from jax import lax
from jax.experimental import pallas as pl
from jax.experimental.pallas import tpu as pltpu

D_MODEL = 2048
D_IN = 5632
D_A = 1024
CHUNK = 128
A_GROUPS = 8
HEAD_DIM = 64
N_KV_HEADS = 4
N_DEV = 8
ROPE_THETA = 10000.0
NORM_EPS = 1e-5
ATTN_SCALE = HEAD_DIM ** -0.5

ADAM_LR = 0.001
ADAM_B1 = 0.9
ADAM_B2 = 0.999
ADAM_EPS = 1e-08
ADAM_WD = 0.01
ADAM_STEP = 10

OFF_U, OFF_VA, OFF_ZA, OFF_Q, OFF_K, OFF_V, OFF_ZB = 0, 1024, 2048, 3072, 4096, 4352, 4608

V7X_VMEM_LIMIT_BYTES = 56 * 1024 * 1024

F32 = jnp.float32
BF16 = jnp.bfloat16
MESH = pl.DeviceIdType.MESH
SDS = jax.ShapeDtypeStruct
NT = (((1,), (1,)), ((), ()))
TN = (((0,), (0,)), ((), ()))


def _params(*semantics):
    return pltpu.CompilerParams(dimension_semantics=semantics or None, vmem_limit_bytes=V7X_VMEM_LIMIT_BYTES)


def _mesh_pos():
    return lax.axis_index("x"), lax.axis_index("y"), lax.axis_index("c")


def _sigmoid(z):
    return 1.0 / (1.0 + jnp.exp(-z))


def _adamw(w, g, m, v):
    m = ADAM_B1 * m + (1.0 - ADAM_B1) * g
    v = ADAM_B2 * v + (1.0 - ADAM_B2) * (g * g)
    m_hat = m / (1.0 - ADAM_B1 ** ADAM_STEP)
    v_hat = v / (1.0 - ADAM_B2 ** ADAM_STEP)
    delta = -ADAM_LR * (m_hat / (jnp.sqrt(v_hat) + ADAM_EPS) + ADAM_WD * w)
    return delta, m, v


def _all_gather(name, blocks, memory_space):
    n_arr = len(blocks)

    def body(*refs):
        ins, outs = refs[:n_arr], refs[n_arr:2 * n_arr]
        send_sems, recv_sems, local_sems = refs[2 * n_arr:]
        x, y, c = _mesh_pos()
        me, sibling = (x, y, c), (x, y, 1 - c)
        chips = [(1 - x, y), (x, 1 - y), (1 - x, 1 - y)]

        def slot(p):
            return 4 * p[0] + 2 * p[1] + p[2]

        def copy(a, k, block, to, src=None):
            dst = outs[a].at[slot(block)]
            return pltpu.make_async_remote_copy(
                src_ref=dst if src is None else src, dst_ref=dst,
                send_sem=send_sems.at[a, k], recv_sem=recv_sems.at[a, k],
                device_id=to, device_id_type=MESH)

        mine = [pltpu.make_async_copy(ins[a], outs[a].at[slot(me)], local_sems.at[a]) for a in range(n_arr)]
        for cp in mine:
            cp.start()
        first = []
        for a in range(n_arr):
            first.append(copy(a, 0, me, sibling, src=ins[a]))
            first += [copy(a, 1 + j, me, (*chip, c), src=ins[a]) for j, chip in enumerate(chips)]
        for cp in first:
            cp.start()
        passed = []
        for j, chip in enumerate(chips):
            for a in range(n_arr):
                copy(a, 1 + j, (*chip, c), me).wait_recv()
                fwd = copy(a, 4 + j, (*chip, c), sibling)
                fwd.start()
                passed.append(fwd)
        for a in range(n_arr):
            copy(a, 0, sibling, me).wait_recv()
            for j, chip in enumerate(chips):
                copy(a, 4 + j, (*chip, 1 - c), me).wait_recv()
        for cp in first + passed:
            cp.wait_send()
        for cp in mine:
            cp.wait()

    spec = pl.BlockSpec(memory_space=memory_space)
    return pl.pallas_call(
        body, name=name,
        out_shape=[SDS((N_DEV,) + b.shape, b.dtype) for b in blocks],
        in_specs=[spec] * n_arr, out_specs=[spec] * n_arr,
        scratch_shapes=[pltpu.SemaphoreType.DMA((n_arr, 7)), pltpu.SemaphoreType.DMA((n_arr, 7)),
                        pltpu.SemaphoreType.DMA((n_arr,))],
        compiler_params=_params(),
    )(*blocks)


def _ada_exchange(c, w_ada, b_ada8, w_ada_f, b_ada_f8):
    n1, n2 = w_ada.shape[1], w_ada_f.shape[1]

    def body(c_ref, w1_ref, b1_ref, w2_ref, b2_ref, cact_ref, mod_ref, modf_ref,
             cact_buf, res1, res2, send1, send2, sems_s, sems_r):
        x, y, c_pos = _mesh_pos()
        me = 4 * x + 2 * y + c_pos
        flips = [(k >> 2 & 1, k >> 1 & 1, k & 1) for k in range(1, N_DEV)]

        def peer(f):
            return (1 - x if f[0] else x, 1 - y if f[1] else y, 1 - c_pos if f[2] else c_pos)

        cv = c_ref[...]
        cact = cv * _sigmoid(cv)
        cact_buf[...] = cact
        cact_ref[me] = cact

        def rdma(phase, k, src, dst, f):
            return pltpu.make_async_remote_copy(src_ref=src, dst_ref=dst, send_sem=sems_s.at[phase, k],
                                                recv_sem=sems_r.at[phase, k], device_id=peer(f), device_id_type=MESH)

        gather = [rdma(0, k, cact_buf, cact_ref.at[me], f) for k, f in enumerate(flips)]
        for cp in gather:
            cp.start()
        for cp in gather:
            cp.wait_recv()
        for cp in gather:
            cp.wait_send()

        rid = lax.broadcasted_iota(jnp.int32, (N_DEV, D_MODEL), 0)
        rows = jnp.zeros((N_DEV, D_MODEL), F32)
        for j in range(N_DEV):
            rows = jnp.where(rid == j, jnp.broadcast_to(cact_ref[j], (N_DEV, D_MODEL)), rows)
        rows = rows.astype(BF16)
        res1[...] = jnp.dot(rows, w1_ref[...].astype(BF16), preferred_element_type=F32) + b1_ref[pl.ds(me, 1), :]
        res2[...] = jnp.dot(rows, w2_ref[...].astype(BF16), preferred_element_type=F32) + b2_ref[pl.ds(me, 1), :]
        for j in range(N_DEV):
            send1[j] = res1[pl.ds(j, 1), :]
            send2[j] = res2[pl.ds(j, 1), :]
        mod_ref[me] = send1[me]
        modf_ref[me] = send2[me]
        scatter = []
        for k, f in enumerate(flips):
            to = me ^ (k + 1)
            scatter.append(rdma(1, k, send1.at[to], mod_ref.at[me], f))
            scatter.append(rdma(2, k, send2.at[to], modf_ref.at[me], f))
        for cp in scatter:
            cp.start()
        for cp in scatter:
            cp.wait_recv()
        for cp in scatter:
            cp.wait_send()

    vmem = pl.BlockSpec(memory_space=pltpu.VMEM)
    return pl.pallas_call(
        body, name="ada_exchange",
        out_shape=[SDS((N_DEV, 1, D_MODEL), F32), SDS((N_DEV, 1, n1), F32), SDS((N_DEV, 1, n2), F32)],
        in_specs=[vmem] * 5, out_specs=[vmem] * 3,
        scratch_shapes=[pltpu.VMEM((1, D_MODEL), F32), pltpu.VMEM((N_DEV, n1), F32), pltpu.VMEM((N_DEV, n2), F32),
                        pltpu.VMEM((N_DEV, 1, n1), F32), pltpu.VMEM((N_DEV, 1, n2), F32),
                        pltpu.SemaphoreType.DMA((3, 7)), pltpu.SemaphoreType.DMA((3, 7))],
        compiler_params=_params(),
    )(c, w_ada, b_ada8, w_ada_f, b_ada_f8)


def _chip_scatter(pair_ref, parts_ref, send_sems, recv_sems):
    x, y, c = _mesh_pos()
    chips = [(1 - x, y), (x, 1 - y), (1 - x, 1 - y)]
    return [pltpu.make_async_remote_copy(
        src_ref=pair_ref.at[2 * cx + cy], dst_ref=parts_ref.at[j], send_sem=send_sems.at[j], recv_sem=recv_sems.at[j],
        device_id=(cx, cy, c), device_id_type=MESH) for j, (cx, cy) in enumerate(chips)]


def _scatter_scratch():
    return [pltpu.SemaphoreType.DMA((3,)), pltpu.SemaphoreType.DMA((3,))]


def _prep_weights(me, wt, w_out):
    steps = 4

    def body(me_ref, wt_ref, wo_ref, wtb_ref, wob_ref):
        wtb_ref[...] = wt_ref[...].astype(BF16)
        wob_ref[...] = wo_ref[...].astype(BF16)

    def rows(a, mine):
        blk = (a.shape[0] // steps, a.shape[1])
        return pl.BlockSpec(blk, (lambda i, me_ref: (steps * me_ref[0] + i, 0)) if mine else (lambda i, me_ref: (i, 0)))

    return pl.pallas_call(
        body, name="prep_weights",
        grid_spec=pltpu.PrefetchScalarGridSpec(
            num_scalar_prefetch=1, grid=(steps,),
            in_specs=[rows(wt, False), rows(w_out, False)], out_specs=[rows(wt, True), rows(w_out, True)]),
        out_shape=[SDS((N_DEV * wt.shape[0], D_MODEL), BF16), SDS((N_DEV * w_out.shape[0], D_MODEL), BF16)],
        compiler_params=_params("parallel"),
    )(me, wt, w_out)


class _InPlaceGather:
    def __init__(self, buf_ref, send_sems, recv_sems, relay=False):
        self.buf, self.send_sems, self.recv_sems, self.relay = buf_ref, send_sems, recv_sems, relay
        self.n = buf_ref.shape[0] // N_DEV
        x, y, c = _mesh_pos()
        self.me, self.sibling, self.core = (x, y, c), (x, y, 1 - c), c
        self.chips = [(1 - x, y), (x, 1 - y), (1 - x, 1 - y)]
        self.relay_from = (jnp.where(c == 0, 1 - x, x), jnp.where(c == 0, y, 1 - y), c)
        self.relay_to = (jnp.where(c == 0, x, 1 - x), jnp.where(c == 0, 1 - y, y), c)

    def copy(self, k, block, to):
        start = pl.multiple_of((4 * block[0] + 2 * block[1] + block[2]) * self.n, self.n)
        rows = self.buf.at[pl.ds(start, self.n)]
        return pltpu.make_async_remote_copy(src_ref=rows, dst_ref=rows, send_sem=self.send_sems.at[k],
                                            recv_sem=self.recv_sems.at[k], device_id=to, device_id_type=MESH)

    def start(self):
        self.copy(0, self.me, self.sibling).start()
        for j, chip in enumerate(self.chips[:2] if self.relay else self.chips):
            self.copy(1 + j, self.me, (*chip, self.core)).start()

    def relay_diagonal(self):
        self.copy(3, self.relay_from, self.relay_to).start()

    def pass_on(self, j):
        self.copy(1 + j, (*self.chips[j], self.core), self.me).wait_recv()
        self.copy(4 + j, (*self.chips[j], self.core), self.sibling).start()

    def wait_sibling(self, k):
        self.copy(k, self.sibling, self.me).wait_recv()

    def wait_sends(self):
        for k in range(7):
            self.copy(k, self.me, self.sibling).wait_send()


def _gather_scratch():
    return [pltpu.SemaphoreType.DMA((7,)), pltpu.SemaphoreType.DMA((7,))]


def _gather_in_proj(order, x, shift, scale, norm_g, wt_all):
    s = x.shape[0]
    th = tm = min(512, s)
    nh, ni = s // th, s // tm
    tn = D_IN // 4
    steps = nh + 4 * ni

    def body(order_ref, x_ref, shift_ref, scale_ref, g_ref, wt_in, h_ref, proj_ref, wt_ref,
             h_scr, w_buf, load_sems, send_sems, recv_sems):
        g = pl.program_id(0)
        gather = _InPlaceGather(wt_ref, send_sems, recv_sems, relay=True)

        def tile_load(slot, chip):
            return pltpu.make_async_copy(wt_ref.at[pl.ds(pl.multiple_of(chip * tn, tn), tn)], w_buf.at[slot],
                                         load_sems.at[slot])

        @pl.when(g == 0)
        def _():
            gather.start()

        @pl.when(g < nh)
        def _():
            xv = x_ref[...]
            r = lax.rsqrt(jnp.mean(xv * xv, axis=-1, keepdims=True) + NORM_EPS)
            hb = (((xv * r) * g_ref[...]) * (1.0 + scale_ref[...]) + shift_ref[...]).astype(BF16)
            h_ref[...] = hb
            h_scr[pl.ds(pl.multiple_of(g * th, th), th), :] = hb

        @pl.when(g == nh - 1)
        def _():
            gather.wait_sibling(0)
            tile_load(0, order_ref[0]).start()

        @pl.when(g >= nh)
        def _():
            t, i = (g - nh) // ni, (g - nh) % ni

            @pl.when(i == 0)
            def _():
                tile_load(t % 2, order_ref[t]).wait()

            @pl.when((i == ni - 1) & (t == 0))
            def _():
                gather.pass_on(0)
                gather.pass_on(1)
                gather.relay_diagonal()

            @pl.when((i == ni - 1) & (t == 2))
            def _():
                gather.pass_on(2)

            for j in range(3):
                @pl.when((i == ni - 1) & (t == j))
                def _():
                    gather.wait_sibling(4 + j)
                    tile_load((j + 1) % 2, order_ref[j + 1]).start()

            lhs = h_scr[pl.ds(pl.multiple_of(i * tm, tm), tm), :]
            proj_ref[...] = lax.dot_general(lhs, w_buf[t % 2], NT, preferred_element_type=F32).astype(BF16)

        @pl.when(g == steps - 1)
        def _():
            gather.wait_sends()

    def h_tile(g, order_ref):
        return (jnp.minimum(g, nh - 1), 0)

    def proj_tile(g, order_ref):
        mm = jnp.maximum(g - nh, 0)
        return (mm % ni, order_ref[mm // ni])

    row = pl.BlockSpec((1, D_MODEL), lambda g, order_ref: (0, 0))
    hbm = pl.BlockSpec(memory_space=pl.ANY)
    return pl.pallas_call(
        body, name="gather_in_proj",
        grid_spec=pltpu.PrefetchScalarGridSpec(
            num_scalar_prefetch=1, grid=(steps,),
            in_specs=[pl.BlockSpec((th, D_MODEL), h_tile), row, row, row, hbm],
            out_specs=[pl.BlockSpec((th, D_MODEL), h_tile), pl.BlockSpec((tm, tn), proj_tile), hbm],
            scratch_shapes=[pltpu.VMEM((s, D_MODEL), BF16), pltpu.VMEM((2, tn, D_MODEL), BF16),
                            pltpu.SemaphoreType.DMA((2,)), *_gather_scratch()]),
        out_shape=[SDS((s, D_MODEL), BF16), SDS((s, D_IN), BF16), SDS(wt_all.shape, BF16)],
        input_output_aliases={5: 2},
        compiler_params=_params("arbitrary"),
    )(order, x, shift, scale, norm_g, wt_all)


def _rope_tables(seq):
    inv_freq = ROPE_THETA ** (-jnp.arange(0, HEAD_DIM, 2, dtype=F32) / HEAD_DIM)
    ang = jnp.arange(seq, dtype=F32)[:, None] * inv_freq[None, :]
    cos, sin, zero = jnp.cos(ang), jnp.sin(ang), jnp.zeros_like(ang)
    return (jnp.concatenate([cos] * 4, axis=1), jnp.concatenate([-sin, zero, -sin, zero], axis=1),
            jnp.concatenate([zero, sin, zero, sin], axis=1))


def _rope(v, cos, sin_lo, sin_hi):
    width = v.shape[1]
    rep = (1, width // 128)
    return (v * jnp.tile(cos, rep) + pltpu.roll(v, width - 32, 1) * jnp.tile(sin_lo, rep)
            + pltpu.roll(v, 32, 1) * jnp.tile(sin_hi, rep))


def _rope_bwd(d, cos, sin_lo, sin_hi):
    width = d.shape[1]
    rep = (1, width // 128)
    return (d * jnp.tile(cos, rep) + pltpu.roll(d * jnp.tile(sin_lo, rep), 32, 1)
            + pltpu.roll(d * jnp.tile(sin_hi, rep), width - 32, 1))


def _layer_norm(v, g, b):
    mu = jnp.mean(v, axis=-1, keepdims=True)
    vc = v - mu
    rstd = lax.rsqrt(jnp.mean(vc * vc, axis=-1, keepdims=True) + NORM_EPS)
    vhat = vc * rstd
    return vhat * g + b, vhat, rstd


def _tril_bf16(w_ref, g):
    t = lax.broadcasted_iota(jnp.int32, (CHUNK, CHUNK), 0)
    tp = lax.broadcasted_iota(jnp.int32, (CHUNK, CHUNK), 1)
    return jnp.where(tp <= t, w_ref[g], 0.0).astype(BF16)


def _bias_columns(b_ref, out_ref):
    for g in range(A_GROUPS):
        out_ref[g] = jnp.broadcast_to(b_ref[pl.ds(g, 1), :], (CHUNK, CHUNK)).T


def _band_mask():
    kj = lax.broadcasted_iota(jnp.int32, (2 * CHUNK, 4 * CHUNK), 0)
    qi = lax.broadcasted_iota(jnp.int32, (2 * CHUNK, 4 * CHUNK), 1) & (CHUNK - 1)
    rel = qi + CHUNK - kj
    return jnp.where((rel >= 0) & (rel < CHUNK), 0.0, -jnp.inf)


def _low_lanes():
    return lax.broadcasted_iota(jnp.int32, (1, 128), 1) < HEAD_DIM


def _stack_heads(pair_a, pair_b):
    lo = _low_lanes()
    return jnp.concatenate([jnp.where(lo, pair_a, 0.0), jnp.where(lo, 0.0, pair_a),
                            jnp.where(lo, pair_b, 0.0), jnp.where(lo, 0.0, pair_b)], axis=0).astype(BF16)


def _heads_to_lanes(per_group):
    rows = [t[:, r * CHUNK:(r + 1) * CHUNK] for t in per_group for r in range(4)]
    return jnp.concatenate(rows, axis=0).T


def _dup_kv_head(band, gk):
    pair = band[:, (gk // 2) * 128:(gk // 2 + 1) * 128]
    lo = _low_lanes()
    one = jnp.where(lo if gk % 2 == 0 else jnp.logical_not(lo), pair, 0.0)
    return (one + pltpu.roll(one, HEAD_DIM, 1)).astype(BF16)


def _fold_kv_head(dup_grad, gk):
    both = dup_grad + pltpu.roll(dup_grad, HEAD_DIM, 1)
    lo = _low_lanes()
    return jnp.where(lo if gk % 2 == 0 else jnp.logical_not(lo), both, 0.0)


def _attn_probs(q_st, k_dup, sink_row, mask, first_block):
    s = lax.dot_general(k_dup, q_st, NT, preferred_element_type=F32) + mask
    s = jnp.concatenate([jnp.where(first_block, -jnp.inf, s[:CHUNK]), s[CHUNK:]], axis=0)
    m = jnp.maximum(jnp.max(s, axis=0, keepdims=True), sink_row)
    p = jnp.exp(s - m)
    e_sink = jnp.exp(sink_row - m)
    inv = 1.0 / (jnp.sum(p, axis=0, keepdims=True) + e_sink)
    return p * inv, e_sink * inv


def _sink_row(sinks_ref, gk):
    return jnp.concatenate([jnp.full((1, CHUNK), sinks_ref[4 * gk + r], F32) for r in range(4)], axis=1)


def _mixer_specs(nb, rev):
    def blk(i):
        return nb - 1 - i if rev else i

    def prev(i):
        return jnp.maximum(blk(i) - 1, 0)

    tab = pl.BlockSpec((CHUNK, 128), lambda i, *_: (blk(i), 0))
    tab_prev = pl.BlockSpec((CHUNK, 128), lambda i, *_: (prev(i), 0))
    return dict(
        cur=pl.BlockSpec((CHUNK, D_IN), lambda i, *_: (blk(i), 0)),
        prev_kv=pl.BlockSpec((CHUNK, 2 * 256), lambda i, *_: (prev(i), OFF_K // 512)),
        tabs=[tab] * 3 + [tab_prev] * 3,
        vec=pl.BlockSpec((1, D_A), lambda i, *_: (0, 0)),
        wsp=pl.BlockSpec((A_GROUPS, CHUNK, CHUNK), lambda i, *_: (0, 0, 0)),
        bsp=pl.BlockSpec((A_GROUPS, CHUNK), lambda i, *_: (0, 0)),
        smem=pl.BlockSpec(memory_space=pltpu.SMEM),
        blk=blk,
    )


def _mixer_fwd(proj, tabs, ln_g, ln_b, w_sp, b_sp, sinks, wo_all):
    s = proj.shape[0]
    nb = s // CHUNK
    sp = _mixer_specs(nb, rev=False)

    def body(cur_ref, pkv_ref, c_ref, s1_ref, s2_ref, cp_ref, s1p_ref, s2p_ref, lg_ref, lb_ref, w_ref, b_ref,
             sinks_ref, wo_in, y_ref, wo_ref, bcol, mask, send_sems, recv_sems):
        i = pl.program_id(0)
        gather = _InPlaceGather(wo_ref, send_sems, recv_sems)

        @pl.when(i == 0)
        def _():
            gather.start()
            _bias_columns(b_ref, bcol)
            mask[...] = _band_mask()

        @pl.when(i == nb // 2)
        def _():
            for j in range(3):
                gather.pass_on(j)

        vln, _, _ = _layer_norm(cur_ref[:, OFF_VA:OFF_ZA].astype(F32), lg_ref[...], lb_ref[...])
        vln = vln.astype(BF16)
        for g in range(A_GROUPS):
            cols = slice(g * 128, (g + 1) * 128)
            sg = jnp.dot(_tril_bf16(w_ref, g), vln[:, cols], preferred_element_type=F32) + bcol[g]
            u = cur_ref[:, OFF_U + g * 128:OFF_U + (g + 1) * 128].astype(F32)
            z = cur_ref[:, OFF_ZA + g * 128:OFF_ZA + (g + 1) * 128].astype(F32)
            y_ref[:, cols] = (u * sg * (z * _sigmoid(z))).astype(BF16)

        cur_t = (c_ref[...], s1_ref[...], s2_ref[...])
        prev_t = (cp_ref[...], s1p_ref[...], s2p_ref[...])
        qr = _rope(cur_ref[:, OFF_Q:OFF_K].astype(F32), *cur_t) * ATTN_SCALE
        kr = jnp.concatenate([_rope(pkv_ref[:, 0:256].astype(F32), *prev_t),
                              _rope(cur_ref[:, OFF_K:OFF_V].astype(F32), *cur_t)], axis=0)
        v_t = jnp.concatenate([pkv_ref[:, 256:512], cur_ref[:, OFF_V:OFF_ZB]], axis=0).astype(F32).T.astype(BF16)
        outs = []
        for gk in range(N_KV_HEADS):
            q_st = _stack_heads(qr[:, (2 * gk) * 128:(2 * gk + 1) * 128], qr[:, (2 * gk + 1) * 128:(2 * gk + 2) * 128])
            probs, _ = _attn_probs(q_st, _dup_kv_head(kr, gk), _sink_row(sinks_ref, gk), mask[...], i == 0)
            outs.append(jnp.dot(v_t[gk * HEAD_DIM:(gk + 1) * HEAD_DIM], probs.astype(BF16),
                                preferred_element_type=F32))
        zb = cur_ref[:, OFF_ZB:D_IN].astype(F32)
        y_ref[:, D_A:D_MODEL] = (_heads_to_lanes(outs) * (zb * _sigmoid(zb))).astype(BF16)

        @pl.when(i == nb - 1)
        def _():
            gather.wait_sibling(0)
            for j in range(3):
                gather.wait_sibling(4 + j)
            gather.wait_sends()

    hbm = pl.BlockSpec(memory_space=pl.ANY)
    return pl.pallas_call(
        body, name="mixer_fwd", grid=(nb,),
        in_specs=[sp["cur"], sp["prev_kv"], *sp["tabs"], sp["vec"], sp["vec"], sp["wsp"], sp["bsp"], sp["smem"], hbm],
        out_specs=[pl.BlockSpec((CHUNK, D_MODEL), lambda i: (i, 0)), hbm],
        out_shape=[SDS((s, D_MODEL), BF16), SDS(wo_all.shape, wo_all.dtype)],
        scratch_shapes=[pltpu.VMEM((A_GROUPS, CHUNK, CHUNK), F32), pltpu.VMEM((2 * CHUNK, 4 * CHUNK), F32),
                        *_gather_scratch()],
        input_output_aliases={13: 1},
        compiler_params=_params("arbitrary"),
    )(proj, proj, *tabs, *tabs, ln_g, ln_b, w_sp, b_sp, sinks, wo_all)


def _out_proj_loss(y, x, target, wo, gate, shift_f, scale_f, fng):
    s = y.shape[0]
    tm, tr = 256, 128
    nt = s // tm

    def body(y_ref, x_ref, t_ref, wo_ref, gate_ref, sh_ref, sc_ref, g_ref, dx1_ref, do_ref, dy_ref, sums_ref,
             do_last, do_work):
        i = pl.program_id(0)

        @pl.when(i == 0)
        def _():
            sums_ref[...] = jnp.zeros_like(sums_ref)
            do_last[...] = jnp.zeros_like(do_last)

        do_work[...] = do_last[...]
        o = jnp.dot(y_ref[...], wo_ref[...], preferred_element_type=F32)
        gate, g, sh = gate_ref[...], g_ref[...], sh_ref[...]
        one_sc = 1.0 + sc_ref[...]
        cs, inv_d = g * one_sc, 1.0 / D_MODEL

        def rowsum(v):
            return jnp.sum(v, axis=0, keepdims=True)

        sums = [jnp.zeros((1, D_MODEL), F32) for _ in range(4)]
        for c in range(tm // tr):
            rows = slice(c * tr, (c + 1) * tr)
            oc = o[rows]
            x1 = x_ref[rows, :] + gate * oc
            r = lax.rsqrt(jnp.sum(x1 * x1, axis=-1, keepdims=True) * inv_d + NORM_EPS)
            x1n = x1 * r
            diff = x1n * cs + sh - t_ref[rows, :]
            w = diff * x1n
            lane_sum = jnp.sum(w * cs, axis=-1, keepdims=True)
            dx1 = (diff * cs) * (r * inv_d) - x1n * (r * lane_sum * (inv_d * inv_d))
            dx1_ref[rows, :] = dx1
            do = (dx1 * gate).astype(BF16)
            do_ref[rows, :] = do
            do_last[rows, :] = do
            for k, v in enumerate((dx1 * oc, diff, w, diff * diff)):
                sums[k] = sums[k] + rowsum(v)
        live = jnp.where(i < nt, 1.0, 0.0)
        sums_ref[0:1, :] += live * sums[0]
        sums_ref[1:2, :] += (live * inv_d) * sums[1]
        sums_ref[2:3, :] += (live * inv_d) * (sums[2] * g)
        sums_ref[3:4, :] += (live * inv_d) * (sums[2] * one_sc)
        sums_ref[4:5, :] += live * sums[3]
        dy_ref[...] = lax.dot_general(do_work[...], wo_ref[...], NT, preferred_element_type=F32).astype(BF16)

    tile = pl.BlockSpec((tm, D_MODEL), lambda i: (jnp.minimum(i, nt - 1), 0))
    row = pl.BlockSpec((1, D_MODEL), lambda i: (0, 0))
    return pl.pallas_call(
        body, name="out_proj_loss", grid=(nt + 1,),
        in_specs=[tile, tile, tile, pl.BlockSpec((D_MODEL, D_MODEL), lambda i: (0, 0)), row, row, row, row],
        out_specs=[tile, tile, pl.BlockSpec((tm, D_MODEL), lambda i: (jnp.maximum(i - 1, 0), 0)),
                   pl.BlockSpec((8, D_MODEL), lambda i: (0, 0))],
        out_shape=[SDS((s, D_MODEL), F32), SDS((s, D_MODEL), BF16), SDS((s, D_MODEL), BF16), SDS((8, D_MODEL), F32)],
        scratch_shapes=[pltpu.VMEM((tm, D_MODEL), BF16), pltpu.VMEM((tm, D_MODEL), BF16)],
        compiler_params=_params("arbitrary"),
    )(y, x, target, wo, gate, shift_f, scale_f, fng)


ROW_DBSP, ROW_DSINKS, MISC_ROWS = 0, 8, 32


def _mixer_bwd(me, proj, dy, tabs, ln_g, ln_b, w_sp, b_sp, sinks, pair):
    s = proj.shape[0]
    nb = s // CHUNK
    sp = _mixer_specs(nb, rev=True)

    def body(me_ref, cur_ref, pkv_ref, dy_ref, c_ref, s1_ref, s2_ref, cp_ref, s1p_ref, s2p_ref, lg_ref, lb_ref, w_ref,
             b_ref, sinks_ref, pair_ref, dproj_ref, dln_ref, dw_ref, misc_ref, parts_ref, bcol, dbcol, carry, mask,
             send_sems, recv_sems):
        i = pl.program_id(0)
        block = nb - 1 - i

        @pl.when(i == 0)
        def _():
            for cp in _chip_scatter(pair_ref, parts_ref, send_sems, recv_sems):
                cp.start()
            _bias_columns(b_ref, bcol)
            mask[...] = _band_mask()
            dbcol[...] = jnp.zeros_like(dbcol)
            carry[...] = jnp.zeros_like(carry)
            dln_ref[...] = jnp.zeros_like(dln_ref)
            dw_ref[...] = jnp.zeros_like(dw_ref)
            misc_ref[...] = jnp.zeros_like(misc_ref)

        vln, vhat, rstd = _layer_norm(cur_ref[:, OFF_VA:OFF_ZA].astype(F32), lg_ref[...], lb_ref[...])
        vln = vln.astype(BF16)
        d_vln = []
        for g in range(A_GROUPS):
            cols = slice(g * 128, (g + 1) * 128)
            w_g = _tril_bf16(w_ref, g)
            sg = jnp.dot(w_g, vln[:, cols], preferred_element_type=F32) + bcol[g]
            u = cur_ref[:, OFF_U + g * 128:OFF_U + (g + 1) * 128].astype(F32)
            z = cur_ref[:, OFF_ZA + g * 128:OFF_ZA + (g + 1) * 128].astype(F32)
            dya = dy_ref[:, cols].astype(F32)
            sig = _sigmoid(z)
            d_ya = dya * (z * sig)
            dproj_ref[:, OFF_ZA + g * 128:OFF_ZA + (g + 1) * 128] = (
                dya * (u * sg) * (sig * (1.0 + z * (1.0 - sig)))).astype(BF16)
            dproj_ref[:, OFF_U + g * 128:OFF_U + (g + 1) * 128] = (d_ya * sg).astype(BF16)
            d_s = d_ya * u
            dbcol[g] += d_s
            d_sb = d_s.astype(BF16)
            dw_ref[g] += lax.dot_general(d_sb, vln[:, cols], NT, preferred_element_type=F32)
            d_vln.append(lax.dot_general(w_g, d_sb, TN, preferred_element_type=F32))
        d_vln = jnp.concatenate(d_vln, axis=1)
        dln_ref[0:1, :] += jnp.sum(d_vln * vhat, axis=0, keepdims=True)
        dln_ref[1:2, :] += jnp.sum(d_vln, axis=0, keepdims=True)
        d_vhat = d_vln * lg_ref[...]
        d_va = rstd * (d_vhat - jnp.mean(d_vhat, axis=-1, keepdims=True)
                       - vhat * jnp.mean(d_vhat * vhat, axis=-1, keepdims=True))
        dproj_ref[:, OFF_VA:OFF_ZA] = d_va.astype(BF16)

        cur_t = (c_ref[...], s1_ref[...], s2_ref[...])
        prev_t = (cp_ref[...], s1p_ref[...], s2p_ref[...])
        band_t = tuple(jnp.concatenate([p, c], axis=0) for p, c in zip(prev_t, cur_t))
        qr = _rope(cur_ref[:, OFF_Q:OFF_K].astype(F32), *cur_t) * ATTN_SCALE
        kr = jnp.concatenate([_rope(pkv_ref[:, 0:256].astype(F32), *prev_t),
                              _rope(cur_ref[:, OFF_K:OFF_V].astype(F32), *cur_t)], axis=0)
        vb = jnp.concatenate([pkv_ref[:, 256:512], cur_ref[:, OFF_V:OFF_ZB]], axis=0).astype(F32)
        k_t, v_t = (kr.T * ATTN_SCALE).astype(BF16), vb.T.astype(BF16)
        zb = cur_ref[:, OFF_ZB:D_IN].astype(F32)
        dyb = dy_ref[:, D_A:D_MODEL].astype(F32)
        sig = _sigmoid(zb)
        d_yb = dyb * (zb * sig)
        outs, dqs = [], []
        dk_pairs = [jnp.zeros((2 * CHUNK, 128), F32) for _ in range(2)]
        dv_pairs = [jnp.zeros((2 * CHUNK, 128), F32) for _ in range(2)]
        for gk in range(N_KV_HEADS):
            heads = slice(gk * HEAD_DIM, (gk + 1) * HEAD_DIM)
            q_st = _stack_heads(qr[:, (2 * gk) * 128:(2 * gk + 1) * 128], qr[:, (2 * gk + 1) * 128:(2 * gk + 2) * 128])
            k_dup, v_dup = _dup_kv_head(kr, gk), _dup_kv_head(vb, gk)
            probs, p_sink = _attn_probs(q_st, k_dup, _sink_row(sinks_ref, gk), mask[...], block == 0)
            probs_b = probs.astype(BF16)
            outs.append(jnp.dot(v_t[heads], probs_b, preferred_element_type=F32))
            do_st = _stack_heads(d_yb[:, (2 * gk) * 128:(2 * gk + 1) * 128], d_yb[:, (2 * gk + 1) * 128:(2 * gk + 2) * 128])
            dp = lax.dot_general(v_dup, do_st, NT, preferred_element_type=F32)
            delta = jnp.sum(probs * dp, axis=0, keepdims=True)
            ds = (probs * (dp - delta)).astype(BF16)
            d_sink = -p_sink * delta
            for r in range(4):
                row = ROW_DSINKS + 4 * gk + r
                misc_ref[row:row + 1, :] += jnp.broadcast_to(
                    jnp.sum(d_sink[:, r * CHUNK:(r + 1) * CHUNK], axis=1, keepdims=True), (1, 128))
            dqs.append(jnp.dot(k_t[heads], ds, preferred_element_type=F32))
            dk_pairs[gk // 2] += _fold_kv_head(jnp.dot(ds, q_st, preferred_element_type=F32), gk)
            dv_pairs[gk // 2] += _fold_kv_head(jnp.dot(probs_b, do_st, preferred_element_type=F32), gk)
        dproj_ref[:, OFF_ZB:D_IN] = (dyb * _heads_to_lanes(outs) * (sig * (1.0 + zb * (1.0 - sig)))).astype(BF16)
        dproj_ref[:, OFF_Q:OFF_K] = _rope_bwd(_heads_to_lanes(dqs), *cur_t).astype(BF16)
        dk_band = _rope_bwd(jnp.concatenate(dk_pairs, axis=1), *band_t)
        dv_band = jnp.concatenate(dv_pairs, axis=1)
        dproj_ref[:, OFF_K:OFF_V] = (dk_band[CHUNK:] + carry[:, 0:256]).astype(BF16)
        dproj_ref[:, OFF_V:OFF_ZB] = (dv_band[CHUNK:] + carry[:, 256:512]).astype(BF16)
        carry[:, 0:256] = dk_band[:CHUNK]
        carry[:, 256:512] = dv_band[:CHUNK]

        @pl.when(i == nb - 1)
        def _():
            t = lax.broadcasted_iota(jnp.int32, (CHUNK, CHUNK), 0)
            tp = lax.broadcasted_iota(jnp.int32, (CHUNK, CHUNK), 1)
            for g in range(A_GROUPS):
                dw_ref[g] = jnp.where(tp <= t, dw_ref[g], 0.0)
                misc_ref[pl.ds(ROW_DBSP + g, 1), :] = jnp.sum(dbcol[g].T, axis=0, keepdims=True)
            scatter = _chip_scatter(pair_ref, parts_ref, send_sems, recv_sems)
            for cp in scatter:
                cp.wait_recv()
            for cp in scatter:
                cp.wait_send()

    blk = sp["blk"]
    hbm = pl.BlockSpec(memory_space=pl.ANY)
    return pl.pallas_call(
        body, name="mixer_bwd",
        grid_spec=pltpu.PrefetchScalarGridSpec(
            num_scalar_prefetch=1, grid=(nb,),
            in_specs=[sp["cur"], sp["prev_kv"], pl.BlockSpec((CHUNK, D_MODEL), lambda i, me_ref: (blk(i), 0)),
                      *sp["tabs"], sp["vec"], sp["vec"], sp["wsp"], sp["bsp"], sp["smem"], hbm],
            out_specs=[pl.BlockSpec((CHUNK, D_IN), lambda i, me_ref: (blk(i), 0)),
                       pl.BlockSpec((8, D_A), lambda i, me_ref: (me_ref[0], 0)),
                       pl.BlockSpec((A_GROUPS, CHUNK, CHUNK), lambda i, me_ref: (me_ref[0], 0, 0)),
                       pl.BlockSpec((MISC_ROWS, 128), lambda i, me_ref: (me_ref[0], 0)), hbm],
            scratch_shapes=[pltpu.VMEM((A_GROUPS, CHUNK, CHUNK), F32), pltpu.VMEM((A_GROUPS, CHUNK, CHUNK), F32),
                            pltpu.VMEM((CHUNK, 512), F32), pltpu.VMEM((2 * CHUNK, 4 * CHUNK), F32),
                            *_scatter_scratch()]),
        out_shape=[SDS((s, D_IN), BF16), SDS((N_DEV * 8, D_A), F32), SDS((N_DEV * A_GROUPS, CHUNK, CHUNK), F32),
                   SDS((N_DEV * MISC_ROWS, 128), F32), SDS((3,) + pair.shape[1:], pair.dtype)],
        compiler_params=_params("arbitrary"),
    )(me, proj, proj, dy, *tabs, *tabs, ln_g, ln_b, w_sp, b_sp, sinks, pair)


def _wgrad_pair(name, a, b, gathers=()):
    s, m = a.shape
    n = b.shape[1]
    bm, half = m // 4, m // 8
    bt = min(1024, s)
    steps = s // bt
    last = 4 * steps
    n_g = len(gathers)

    def body(*refs):
        a_ref, b_ref = refs[:2]
        out_ref, bufs = refs[2 + n_g], refs[3 + n_g:3 + 2 * n_g]
        acc, kept, got, sent, send_sems, recv_sems = refs[3 + 2 * n_g:9 + 2 * n_g]
        sems = refs[9 + 2 * n_g:]
        g = pl.program_id(0)
        tile, t = g // steps, g % steps
        mx, my, mc = _mesh_pos()
        jobs = [_InPlaceGather(bufs[k], sems[2 * k], sems[2 * k + 1]) for k in range(n_g)]

        def exchange(q):
            return pltpu.make_async_remote_copy(src_ref=sent, dst_ref=got.at[q % 2], send_sem=send_sems.at[q],
                                                recv_sem=recv_sems.at[q], device_id=(mx, my, 1 - mc),
                                                device_id_type=MESH)

        @pl.when(g == 0)
        def _():
            for job in jobs:
                job.start()

        @pl.when(g == 2 * steps)
        def _():
            for job in jobs:
                for j in range(3):
                    job.pass_on(j)

        @pl.when((t == 0) & (g > 0))
        def _():
            q = tile - 1
            exchange(q).wait_recv()
            exchange(q).wait_send()
            out_ref[0] = (kept[q % 2].astype(F32) + got[q % 2].astype(F32)).astype(BF16)

        @pl.when(g < last)
        def _():
            prod = lax.dot_general(a_ref[...], b_ref[...], TN, preferred_element_type=F32)

            @pl.when(t == 0)
            def _():
                acc[...] = prod

            @pl.when(t > 0)
            def _():
                acc[...] += prod

            @pl.when(t == steps - 1)
            def _():
                kept[tile % 2] = acc[pl.ds(pl.multiple_of(mc * half, 8), half), :].astype(BF16)
                sent[...] = acc[pl.ds(pl.multiple_of((1 - mc) * half, 8), half), :].astype(BF16)
                exchange(tile).start()

        @pl.when(g == last)
        def _():
            for job in jobs:
                job.wait_sibling(0)
                for j in range(3):
                    job.wait_sibling(4 + j)
                job.wait_sends()

    def a_tile(g):
        gg = jnp.minimum(g, last - 1)
        return (gg % steps, gg // steps)

    def b_tile(g):
        return (jnp.minimum(g, last - 1) % steps, 0)

    hbm = pl.BlockSpec(memory_space=pl.ANY)
    outs = pl.pallas_call(
        body, name=name, grid=(last + 1,),
        in_specs=[pl.BlockSpec((bt, bm), a_tile), pl.BlockSpec((bt, n), b_tile)] + [hbm] * n_g,
        out_specs=[pl.BlockSpec((1, half, n), lambda g: (jnp.maximum(g - 1, 0) // steps, 0, 0))] + [hbm] * n_g,
        out_shape=[SDS((4, half, n), BF16)] + [SDS(gb.shape, gb.dtype) for gb in gathers],
        scratch_shapes=[pltpu.VMEM((bm, n), F32), pltpu.VMEM((2, half, n), BF16), pltpu.VMEM((2, half, n), BF16),
                        pltpu.VMEM((half, n), BF16), pltpu.SemaphoreType.DMA((4,)), pltpu.SemaphoreType.DMA((4,))]
        + _gather_scratch() * n_g,
        input_output_aliases={2 + k: 1 + k for k in range(n_g)},
        compiler_params=_params("arbitrary"),
    )(a, b, *gathers)
    return outs[0], outs[1:]


def _in_proj_bwd(dproj, wt, x, dx1, scale, norm_g, pair):
    s = x.shape[0]
    tm, tk, tr = min(1024, s), D_IN // 4, 64
    ksteps = D_IN // tk

    def body(dp_ref, wt_ref, x_hbm, dx1_hbm, sc_ref, g_ref, pair_ref, gx_ref, sums_ref, parts_ref, x_buf, dx1_buf,
             tile_sems, send_sems, recv_sems):
        i, k = pl.program_id(0), pl.program_id(1)

        def tile_copies():
            rows = pl.ds(pl.multiple_of(i * tm, tm), tm)
            return (pltpu.make_async_copy(x_hbm.at[rows], x_buf, tile_sems.at[0]),
                    pltpu.make_async_copy(dx1_hbm.at[rows], dx1_buf, tile_sems.at[1]))

        @pl.when((i == 0) & (k == 0))
        def _():
            for cp in _chip_scatter(pair_ref, parts_ref, send_sems, recv_sems):
                cp.start()
            sums_ref[...] = jnp.zeros_like(sums_ref)

        @pl.when(k == 0)
        def _():
            for cp in tile_copies():
                cp.start()
            gx_ref[...] = jnp.dot(dp_ref[...], wt_ref[...], preferred_element_type=F32)

        @pl.when(k > 0)
        def _():
            gx_ref[...] += jnp.dot(dp_ref[...], wt_ref[...], preferred_element_type=F32)

        @pl.when(k == ksteps - 1)
        def _():
            for cp in tile_copies():
                cp.wait()
            one_sc, g = 1.0 + sc_ref[...], g_ref[...]
            cs = one_sc * g

            def chunk(j, sums):
                rows = pl.ds(pl.multiple_of(j * tr, tr), tr)
                dh, xv = gx_ref[rows, :], x_buf[rows, :]
                dhx = dh * xv
                r = lax.rsqrt(jnp.sum(xv * xv, axis=-1, keepdims=True) * (1.0 / D_MODEL) + NORM_EPS)
                coef = (r * r * r) * (jnp.sum(dhx * cs, axis=-1, keepdims=True) * (1.0 / D_MODEL))
                gx_ref[rows, :] = dx1_buf[rows, :] + r * (dh * cs) - xv * coef
                return (sums[0] + jnp.sum(dh, axis=0, keepdims=True), sums[1] + jnp.sum(dhx * r, axis=0, keepdims=True))

            zero = jnp.zeros((1, D_MODEL), F32)
            sums = lax.fori_loop(0, tm // tr, chunk, (zero, zero))
            sums_ref[0:1, :] += sums[0]
            sums_ref[1:2, :] += sums[1] * g
            sums_ref[2:3, :] += sums[1] * one_sc

        @pl.when((i == s // tm - 1) & (k == ksteps - 1))
        def _():
            scatter = _chip_scatter(pair_ref, parts_ref, send_sems, recv_sems)
            for cp in scatter:
                cp.wait_recv()
            for cp in scatter:
                cp.wait_send()

    row = pl.BlockSpec((1, D_MODEL), lambda i, k: (0, 0))
    hbm = pl.BlockSpec(memory_space=pl.ANY)
    return pl.pallas_call(
        body, name="in_proj_bwd", grid=(s // tm, ksteps),
        in_specs=[pl.BlockSpec((tm, tk), lambda i, k: (i, k)), pl.BlockSpec((tk, D_MODEL), lambda i, k: (k, 0)),
                  hbm, hbm, row, row, hbm],
        out_specs=[pl.BlockSpec((tm, D_MODEL), lambda i, k: (i, 0)), pl.BlockSpec((8, D_MODEL), lambda i, k: (0, 0)),
                   hbm],
        out_shape=[SDS((s, D_MODEL), F32), SDS((8, D_MODEL), F32), SDS((3,) + pair.shape[1:], pair.dtype)],
        scratch_shapes=[pltpu.VMEM((tm, D_MODEL), F32), pltpu.VMEM((tm, D_MODEL), F32),
                        pltpu.SemaphoreType.DMA((2,)), *_scatter_scratch()],
        compiler_params=_params("arbitrary", "arbitrary"),
    )(dproj, wt, x, dx1, scale, norm_g, pair)


def _sum_chips(own_ref, parts_ref):
    return ((own_ref[0].astype(F32) + parts_ref[0].astype(F32)) + parts_ref[1].astype(F32)) + parts_ref[2].astype(F32)


def _adam_rows(name, chip, pair, parts, w, m, v):
    rows = w.shape[0]
    tr = rows // 4

    def body(chip_ref, own_ref, p_ref, w_ref, m_ref, v_ref, g_ref, d_ref, nm_ref, nv_ref):
        g = _sum_chips(own_ref, p_ref)
        g_ref[...] = g
        d_ref[...], nm_ref[...], nv_ref[...] = _adamw(w_ref[...], g, m_ref[...], v_ref[...])

    blk = pl.BlockSpec((tr, D_MODEL), lambda j, chip_ref: (j, 0))
    return pl.pallas_call(
        body, name=name,
        grid_spec=pltpu.PrefetchScalarGridSpec(
            num_scalar_prefetch=1, grid=(rows // tr,),
            in_specs=[pl.BlockSpec((1, tr, D_MODEL), lambda j, chip_ref: (chip_ref[0], j, 0)),
                      pl.BlockSpec((3, tr, D_MODEL), lambda j, chip_ref: (0, j, 0)), blk, blk, blk],
            out_specs=[blk] * 4),
        out_shape=[SDS(w.shape, F32)] * 4, compiler_params=_params("parallel"),
    )(chip, pair, parts, w, m, v)


def _adam_ada(name, cact, dmod, w, m, v):
    n = w.shape[1]
    tr = 512

    def body(c_ref, dm_ref, w_ref, m_ref, v_ref, g_ref, d_ref, nm_ref, nv_ref):
        pad_c = jnp.concatenate([c_ref[...], jnp.zeros_like(c_ref)], axis=0).astype(BF16)
        pad_d = jnp.concatenate([dm_ref[...], jnp.zeros_like(dm_ref)], axis=0).astype(BF16)
        g = lax.dot_general(pad_c, pad_d, TN, preferred_element_type=F32)
        g_ref[...] = g
        d_ref[...], nm_ref[...], nv_ref[...] = _adamw(w_ref[...], g, m_ref[...], v_ref[...])

    blk = pl.BlockSpec((tr, n), lambda j: (j, 0))
    return pl.pallas_call(
        body, name=name, grid=(D_MODEL // tr,),
        in_specs=[pl.BlockSpec((N_DEV, tr), lambda j: (0, j)), pl.BlockSpec((N_DEV, n), lambda j: (0, 0)),
                  blk, blk, blk],
        out_specs=[blk] * 4, out_shape=[SDS(w.shape, F32)] * 4,
        compiler_params=_params("parallel"),
    )(cact, dmod, w, m, v)


SMALL_PARAMS = ("w_spatial", "b_spatial", "sinks", "norm_g", "ln_v_g", "ln_v_b", "final_norm_g", "b_ada", "b_ada_final")


def _adam_small(d_wsp, misc, d_ln, sums_i, sums_o, params):
    n_p = len(SMALL_PARAMS)

    def body(*refs):
        wsp_ref, misc_ref, ln_ref, si_ref, so_ref = refs[:5]
        wmv = [refs[5 + 3 * k:8 + 3 * k] for k in range(n_p)]
        loss_ref = refs[5 + 3 * n_p]
        outs = [refs[6 + 3 * n_p + 4 * k:10 + 3 * n_p + 4 * k] for k in range(n_p)]

        def total(ref, rows=None):
            def part(j):
                return ref[j] if rows is None else ref[j, rows[0]:rows[1], :]
            acc = part(0)
            for j in range(1, N_DEV):
                acc = acc + part(j)
            return acc

        sink_rows = total(misc_ref, (ROW_DSINKS, ROW_DSINKS + 16))
        diag = (lax.broadcasted_iota(jnp.int32, (16, 128), 0) == lax.broadcasted_iota(jnp.int32, (16, 128), 1))
        grads = dict(
            w_spatial=total(wsp_ref), b_spatial=total(misc_ref, (ROW_DBSP, ROW_DBSP + A_GROUPS)),
            sinks=jnp.sum(jnp.where(diag, sink_rows, 0.0), axis=0, keepdims=True),
            norm_g=total(si_ref, (2, 3)), ln_v_g=total(ln_ref, (0, 1)), ln_v_b=total(ln_ref, (1, 2)),
            final_norm_g=total(so_ref, (3, 4)),
            b_ada=jnp.concatenate([total(si_ref, (0, 1)), total(si_ref, (1, 2)), total(so_ref, (0, 1))], axis=1),
            b_ada_final=jnp.concatenate([total(so_ref, (1, 2)), total(so_ref, (2, 3))], axis=1))
        sq_err = jnp.sum(total(so_ref, (4, 5)), axis=1, keepdims=True)
        loss_ref[...] = jnp.broadcast_to(sq_err * (0.5 / D_MODEL), (1, 128))
        for k, name in enumerate(SMALL_PARAMS):
            w_ref, m_ref, v_ref = wmv[k]
            g_ref, d_ref, nm_ref, nv_ref = outs[k]
            g_ref[...] = grads[name]
            d_ref[...], nm_ref[...], nv_ref[...] = _adamw(w_ref[...], grads[name], m_ref[...], v_ref[...])

    flat = [a for name in SMALL_PARAMS for a in params[name]]
    vmem = pl.BlockSpec(memory_space=pltpu.VMEM)
    out_shape = [SDS((1, 128), F32)] + [SDS(params[name][0].shape, F32) for name in SMALL_PARAMS for _ in range(4)]
    outs = pl.pallas_call(
        body, name="adam_small", in_specs=[vmem] * (5 + len(flat)), out_specs=[vmem] * len(out_shape),
        out_shape=out_shape, compiler_params=_params(),
    )(d_wsp, misc, d_ln, sums_i, sums_o, *flat)
    return outs[0], {name: outs[1 + 4 * k:5 + 4 * k] for k, name in enumerate(SMALL_PARAMS)}


def kernel(x, c, w_ada, b_ada, norm_g, w_in, ln_v_g, ln_v_b, w_spatial, b_spatial, sinks, w_out, w_ada_final, b_ada_final, final_norm_g, loss_target, m_w_ada, m_b_ada, m_norm_g, m_w_in, m_ln_v_g, m_ln_v_b, m_w_spatial, m_b_spatial, m_sinks, m_w_out, m_w_ada_final, m_b_ada_final, m_final_norm_g, v_w_ada, v_b_ada, v_norm_g, v_w_in, v_ln_v_g, v_ln_v_b, v_w_spatial, v_b_spatial, v_sinks, v_w_out, v_w_ada_final, v_b_ada_final, v_final_norm_g):
    seq = x.shape[1]
    me = 4 * lax.axis_index("x") + 2 * lax.axis_index("y") + lax.axis_index("c")
    x2, tgt = x[0], loss_target[0]
    fng = final_norm_g.reshape(1, D_MODEL)

    n_ada, n_ada_f = w_ada.shape[2], w_ada_final.shape[1]
    cact, mod, mod_f = _ada_exchange(c, w_ada[0], b_ada.reshape(N_DEV, n_ada), w_ada_final,
                                     b_ada_final.reshape(N_DEV, n_ada_f))
    cact = cact.reshape(N_DEV, D_MODEL)
    mod, mod_f = mod.reshape(1, 3 * D_MODEL), mod_f.reshape(1, 2 * D_MODEL)
    shift, scale, gate = mod[:, :D_MODEL], mod[:, D_MODEL:2 * D_MODEL], mod[:, 2 * D_MODEL:]
    shift_f, scale_f = mod_f[:, :D_MODEL], mod_f[:, D_MODEL:]

    wt_f32, m_wt, v_wt = (jnp.swapaxes(a, 1, 2)[0] for a in (w_in, m_w_in, v_w_in))
    xi, yi = lax.axis_index("x"), lax.axis_index("y")
    chip_order = jnp.stack([2 * xi + yi, 2 * (1 - xi) + yi, 2 * xi + 1 - yi, 2 * (1 - xi) + 1 - yi]).astype(jnp.int32)
    wt_mine, wo_mine = _prep_weights(me.reshape(1), wt_f32, w_out[0])

    tabs = _rope_tables(seq)
    sinks_v = sinks.reshape(16)
    h, proj, wt = _gather_in_proj(chip_order, x2, shift, scale, norm_g, wt_mine)
    y, wo = _mixer_fwd(proj, tabs, ln_v_g, ln_v_b, w_spatial[0], b_spatial[0], sinks_v, wo_mine)
    dx1, do, dy, sums_o = _out_proj_loss(y, x2, tgt, wo, gate, shift_f, scale_f, fng)

    chip = (2 * lax.axis_index("x") + lax.axis_index("y")).reshape(1)
    pair_out, _ = _wgrad_pair("wgrad_out", y, do)
    dproj, d_ln, d_wsp, misc, parts_out = _mixer_bwd(
        me.reshape(1), proj, dy, tabs, ln_v_g, ln_v_b, w_spatial[0], b_spatial[0], sinks_v, pair_out)
    pair_in, (d_ln, d_wsp, misc) = _wgrad_pair(
        "wgrad_in", dproj, h, gathers=(d_ln, d_wsp.reshape(N_DEV * A_GROUPS * CHUNK, CHUNK), misc))
    grad_x, sums_i, parts_in = _in_proj_bwd(dproj, wt, x2, dx1, scale, norm_g, pair_in)
    wt_leaves = [jnp.swapaxes(a[None], 1, 2) for a in _adam_rows("adam_w_in", chip, pair_in, parts_in, wt_f32, m_wt, v_wt)]
    w_out_leaves = [a[None] for a in _adam_rows("adam_w_out", chip, pair_out, parts_out, w_out[0], m_w_out[0], v_w_out[0])]

    sums_i, sums_o = _all_gather("gather_sums", [sums_i, sums_o], pltpu.VMEM)
    natural = dict(w_spatial=(A_GROUPS * CHUNK, CHUNK), b_spatial=(A_GROUPS, CHUNK), sinks=(1, 16), norm_g=(1, D_MODEL),
                   ln_v_g=(1, D_A), ln_v_b=(1, D_A), final_norm_g=(1, D_MODEL), b_ada=(1, 3 * D_MODEL),
                   b_ada_final=(1, 2 * D_MODEL))
    given = dict(
        w_spatial=(w_spatial, m_w_spatial, v_w_spatial), b_spatial=(b_spatial, m_b_spatial, v_b_spatial),
        sinks=(sinks, m_sinks, v_sinks), norm_g=(norm_g, m_norm_g, v_norm_g), ln_v_g=(ln_v_g, m_ln_v_g, v_ln_v_g),
        ln_v_b=(ln_v_b, m_ln_v_b, v_ln_v_b), final_norm_g=(final_norm_g, m_final_norm_g, v_final_norm_g),
        b_ada=(b_ada, m_b_ada, v_b_ada), b_ada_final=(b_ada_final, m_b_ada_final, v_b_ada_final))
    params = {name: tuple(a.reshape(natural[name]) for a in given[name]) for name in SMALL_PARAMS}
    params["sinks"] = tuple(jnp.pad(a, ((0, 0), (0, 128 - 16))) for a in params["sinks"])
    loss, small = _adam_small(d_wsp.reshape(N_DEV, A_GROUPS * CHUNK, CHUNK), misc.reshape(N_DEV, MISC_ROWS, 128),
                              d_ln.reshape(N_DEV, 8, D_A), sums_i, sums_o, params)
    small["sinks"] = [a[:, :16] for a in small["sinks"]]
    small = {name: [a.reshape(given[name][0].shape) for a in small[name]] for name in SMALL_PARAMS}

    dmod_all = jnp.concatenate([sums_i[:, 0], sums_i[:, 1], sums_o[:, 0]], axis=1)
    dmod_f_all = jnp.concatenate([sums_o[:, 1], sums_o[:, 2]], axis=1)
    dmod_mine = lax.dynamic_slice_in_dim(dmod_all, me * n_ada, n_ada, axis=1)
    dmod_f_mine = lax.dynamic_slice_in_dim(dmod_f_all, me * n_ada_f, n_ada_f, axis=1)
    ada = _adam_ada("adam_w_ada", cact, dmod_mine, w_ada[0], m_w_ada[0], v_w_ada[0])
    ada_f = _adam_ada("adam_w_ada_final", cact, dmod_f_mine, w_ada_final, m_w_ada_final, v_w_ada_final)

    def leaves(k):
        return (ada[k][None], small["b_ada"][k], small["norm_g"][k], wt_leaves[k], small["ln_v_g"][k],
                small["ln_v_b"][k], small["w_spatial"][k], small["b_spatial"][k], small["sinks"][k], w_out_leaves[k],
                ada_f[k], small["b_ada_final"][k], small["final_norm_g"][k])

    return (loss[0, 0], grad_x[None], *leaves(0), *leaves(1), *leaves(2), *leaves(3))
```

```python
import functools

import jax
import jax.numpy as jnp
from jax import lax
from jax.experimental import pallas as pl
from jax.experimental.pallas import tpu as pltpu

D_MODEL = 2048
D_IN = 5632
D_A = 1024
CHUNK = 128
A_GROUPS = 8
HEAD_DIM = 64
N_KV_HEADS = 4
N_DEV = 8
ROPE_THETA = 10000.0
NORM_EPS = 1e-5
ATTN_SCALE = HEAD_DIM ** -0.5

ADAM_LR = 0.001
ADAM_B1 = 0.9
ADAM_B2 = 0.999
ADAM_EPS = 1e-08
ADAM_WD = 0.01
ADAM_STEP = 10

OFF_U, OFF_VA, OFF_ZA, OFF_Q, OFF_K, OFF_V, OFF_ZB = 0, 1024, 2048, 3072, 4096, 4352, 4608

V7X_VMEM_LIMIT_BYTES = 56 * 1024 * 1024

F32 = jnp.float32
BF16 = jnp.bfloat16
MESH = pl.DeviceIdType.MESH
SDS = jax.ShapeDtypeStruct
NT = (((1,), (1,)), ((), ()))
TN = (((0,), (0,)), ((), ()))


def _params(*semantics):
    return pltpu.CompilerParams(dimension_semantics=semantics or None, vmem_limit_bytes=V7X_VMEM_LIMIT_BYTES)


def _mesh_pos():
    return lax.axis_index("x"), lax.axis_index("y"), lax.axis_index("c")


def _sigmoid(z):
    return 1.0 / (1.0 + jnp.exp(-z))


def _adamw(w, g, m, v):
    m = ADAM_B1 * m + (1.0 - ADAM_B1) * g
    v = ADAM_B2 * v + (1.0 - ADAM_B2) * (g * g)
    m_hat = m / (1.0 - ADAM_B1 ** ADAM_STEP)
    v_hat = v / (1.0 - ADAM_B2 ** ADAM_STEP)
    delta = -ADAM_LR * (m_hat / (jnp.sqrt(v_hat) + ADAM_EPS) + ADAM_WD * w)
    return delta, m, v


def _all_gather(name, blocks, memory_space):
    n_arr = len(blocks)

    def body(*refs):
        ins, outs = refs[:n_arr], refs[n_arr:2 * n_arr]
        send_sems, recv_sems, local_sems = refs[2 * n_arr:]
        x, y, c = _mesh_pos()
        me, sibling = (x, y, c), (x, y, 1 - c)
        chips = [(1 - x, y), (x, 1 - y), (1 - x, 1 - y)]

        def slot(p):
            return 4 * p[0] + 2 * p[1] + p[2]

        def copy(a, k, block, to, src=None):
            dst = outs[a].at[slot(block)]
            return pltpu.make_async_remote_copy(
                src_ref=dst if src is None else src, dst_ref=dst,
                send_sem=send_sems.at[a, k], recv_sem=recv_sems.at[a, k],
                device_id=to, device_id_type=MESH)

        mine = [pltpu.make_async_copy(ins[a], outs[a].at[slot(me)], local_sems.at[a]) for a in range(n_arr)]
        for cp in mine:
            cp.start()
        first = []
        for a in range(n_arr):
            first.append(copy(a, 0, me, sibling, src=ins[a]))
            first += [copy(a, 1 + j, me, (*chip, c), src=ins[a]) for j, chip in enumerate(chips)]
        for cp in first:
            cp.start()
        passed = []
        for j, chip in enumerate(chips):
            for a in range(n_arr):
                copy(a, 1 + j, (*chip, c), me).wait_recv()
                fwd = copy(a, 4 + j, (*chip, c), sibling)
                fwd.start()
                passed.append(fwd)
        for a in range(n_arr):
            copy(a, 0, sibling, me).wait_recv()
            for j, chip in enumerate(chips):
                copy(a, 4 + j, (*chip, 1 - c), me).wait_recv()
        for cp in first + passed:
            cp.wait_send()
        for cp in mine:
            cp.wait()

    spec = pl.BlockSpec(memory_space=memory_space)
    return pl.pallas_call(
        body, name=name,
        out_shape=[SDS((N_DEV,) + b.shape, b.dtype) for b in blocks],
        in_specs=[spec] * n_arr, out_specs=[spec] * n_arr,
        scratch_shapes=[pltpu.SemaphoreType.DMA((n_arr, 7)), pltpu.SemaphoreType.DMA((n_arr, 7)),
                        pltpu.SemaphoreType.DMA((n_arr,))],
        compiler_params=_params(),
    )(*blocks)


def _ada_exchange(c, w_ada, b_ada8, w_ada_f, b_ada_f8):
    n1, n2 = w_ada.shape[1], w_ada_f.shape[1]

    def body(c_ref, w1_ref, b1_ref, w2_ref, b2_ref, cact_ref, mod_ref, modf_ref,
             cact_buf, res1, res2, send1, send2, sems_s, sems_r):
        x, y, c_pos = _mesh_pos()
        me = 4 * x + 2 * y + c_pos
        flips = [(k >> 2 & 1, k >> 1 & 1, k & 1) for k in range(1, N_DEV)]

        def peer(f):
            return (1 - x if f[0] else x, 1 - y if f[1] else y, 1 - c_pos if f[2] else c_pos)

        cv = c_ref[...]
        cact = cv * _sigmoid(cv)
        cact_buf[...] = cact
        cact_ref[me] = cact

        def rdma(phase, k, src, dst, f):
            return pltpu.make_async_remote_copy(src_ref=src, dst_ref=dst, send_sem=sems_s.at[phase, k],
                                                recv_sem=sems_r.at[phase, k], device_id=peer(f), device_id_type=MESH)

        gather = [rdma(0, k, cact_buf, cact_ref.at[me], f) for k, f in enumerate(flips)]
        for cp in gather:
            cp.start()
        for cp in gather:
            cp.wait_recv()
        for cp in gather:
            cp.wait_send()

        rid = lax.broadcasted_iota(jnp.int32, (N_DEV, D_MODEL), 0)
        rows = jnp.zeros((N_DEV, D_MODEL), F32)
        for j in range(N_DEV):
            rows = jnp.where(rid == j, jnp.broadcast_to(cact_ref[j], (N_DEV, D_MODEL)), rows)
        rows = rows.astype(BF16)
        res1[...] = jnp.dot(rows, w1_ref[...].astype(BF16), preferred_element_type=F32) + b1_ref[pl.ds(me, 1), :]
        res2[...] = jnp.dot(rows, w2_ref[...].astype(BF16), preferred_element_type=F32) + b2_ref[pl.ds(me, 1), :]
        for j in range(N_DEV):
            send1[j] = res1[pl.ds(j, 1), :]
            send2[j] = res2[pl.ds(j, 1), :]
        mod_ref[me] = send1[me]
        modf_ref[me] = send2[me]
        scatter = []
        for k, f in enumerate(flips):
            to = me ^ (k + 1)
            scatter.append(rdma(1, k, send1.at[to], mod_ref.at[me], f))
            scatter.append(rdma(2, k, send2.at[to], modf_ref.at[me], f))
        for cp in scatter:
            cp.start()
        for cp in scatter:
            cp.wait_recv()
        for cp in scatter:
            cp.wait_send()

    vmem = pl.BlockSpec(memory_space=pltpu.VMEM)
    return pl.pallas_call(
        body, name="ada_exchange",
        out_shape=[SDS((N_DEV, 1, D_MODEL), F32), SDS((N_DEV, 1, n1), F32), SDS((N_DEV, 1, n2), F32)],
        in_specs=[vmem] * 5, out_specs=[vmem] * 3,
        scratch_shapes=[pltpu.VMEM((1, D_MODEL), F32), pltpu.VMEM((N_DEV, n1), F32), pltpu.VMEM((N_DEV, n2), F32),
                        pltpu.VMEM((N_DEV, 1, n1), F32), pltpu.VMEM((N_DEV, 1, n2), F32),
                        pltpu.SemaphoreType.DMA((3, 7)), pltpu.SemaphoreType.DMA((3, 7))],
        compiler_params=_params(),
    )(c, w_ada, b_ada8, w_ada_f, b_ada_f8)


def _chip_scatter(pair_ref, parts_ref, send_sems, recv_sems):
    x, y, c = _mesh_pos()
    chips = [(1 - x, y), (x, 1 - y), (1 - x, 1 - y)]
    return [pltpu.make_async_remote_copy(
        src_ref=pair_ref.at[2 * cx + cy], dst_ref=parts_ref.at[j], send_sem=send_sems.at[j], recv_sem=recv_sems.at[j],
        device_id=(cx, cy, c), device_id_type=MESH) for j, (cx, cy) in enumerate(chips)]


def _scatter_scratch():
    return [pltpu.SemaphoreType.DMA((3,)), pltpu.SemaphoreType.DMA((3,))]


def _prep_weights(me, wt, w_out):
    steps = 4

    def body(me_ref, wt_ref, wo_ref, wtb_ref, wob_ref):
        wtb_ref[...] = wt_ref[...].astype(BF16)
        wob_ref[...] = wo_ref[...].astype(BF16)

    def rows(a, mine):
        blk = (a.shape[0] // steps, a.shape[1])
        return pl.BlockSpec(blk, (lambda i, me_ref: (steps * me_ref[0] + i, 0)) if mine else (lambda i, me_ref: (i, 0)))

    return pl.pallas_call(
        body, name="prep_weights",
        grid_spec=pltpu.PrefetchScalarGridSpec(
            num_scalar_prefetch=1, grid=(steps,),
            in_specs=[rows(wt, False), rows(w_out, False)], out_specs=[rows(wt, True), rows(w_out, True)]),
        out_shape=[SDS((N_DEV * wt.shape[0], D_MODEL), BF16), SDS((N_DEV * w_out.shape[0], D_MODEL), BF16)],
        compiler_params=_params("parallel"),
    )(me, wt, w_out)


class _InPlaceGather:
    def __init__(self, buf_ref, send_sems, recv_sems, relay=False):
        self.buf, self.send_sems, self.recv_sems, self.relay = buf_ref, send_sems, recv_sems, relay
        self.n = buf_ref.shape[0] // N_DEV
        x, y, c = _mesh_pos()
        self.me, self.sibling, self.core = (x, y, c), (x, y, 1 - c), c
        self.chips = [(1 - x, y), (x, 1 - y), (1 - x, 1 - y)]
        self.relay_from = (jnp.where(c == 0, 1 - x, x), jnp.where(c == 0, y, 1 - y), c)
        self.relay_to = (jnp.where(c == 0, x, 1 - x), jnp.where(c == 0, 1 - y, y), c)

    def copy(self, k, block, to):
        start = pl.multiple_of((4 * block[0] + 2 * block[1] + block[2]) * self.n, self.n)
        rows = self.buf.at[pl.ds(start, self.n)]
        return pltpu.make_async_remote_copy(src_ref=rows, dst_ref=rows, send_sem=self.send_sems.at[k],
                                            recv_sem=self.recv_sems.at[k], device_id=to, device_id_type=MESH)

    def start(self):
        self.copy(0, self.me, self.sibling).start()
        for j, chip in enumerate(self.chips[:2] if self.relay else self.chips):
            self.copy(1 + j, self.me, (*chip, self.core)).start()

    def relay_diagonal(self):
        self.copy(3, self.relay_from, self.relay_to).start()

    def pass_on(self, j):
        self.copy(1 + j, (*self.chips[j], self.core), self.me).wait_recv()
        self.copy(4 + j, (*self.chips[j], self.core), self.sibling).start()

    def wait_sibling(self, k):
        self.copy(k, self.sibling, self.me).wait_recv()

    def wait_sends(self):
        for k in range(7):
            self.copy(k, self.me, self.sibling).wait_send()


def _gather_scratch():
    return [pltpu.SemaphoreType.DMA((7,)), pltpu.SemaphoreType.DMA((7,))]


def _gather_in_proj(order, x, shift, scale, norm_g, wt_all):
    s = x.shape[0]
    th = tm = min(512, s)
    nh, ni = s // th, s // tm
    tn = D_IN // 4
    steps = nh + 4 * ni

    def body(order_ref, x_ref, shift_ref, scale_ref, g_ref, wt_in, h_ref, proj_ref, wt_ref,
             h_scr, w_buf, load_sems, send_sems, recv_sems):
        g = pl.program_id(0)
        gather = _InPlaceGather(wt_ref, send_sems, recv_sems, relay=True)

        def tile_load(slot, chip):
            return pltpu.make_async_copy(wt_ref.at[pl.ds(pl.multiple_of(chip * tn, tn), tn)], w_buf.at[slot],
                                         load_sems.at[slot])

        @pl.when(g == 0)
        def _():
            gather.start()

        @pl.when(g < nh)
        def _():
            xv = x_ref[...]
            r = lax.rsqrt(jnp.mean(xv * xv, axis=-1, keepdims=True) + NORM_EPS)
            hb = (((xv * r) * g_ref[...]) * (1.0 + scale_ref[...]) + shift_ref[...]).astype(BF16)
            h_ref[...] = hb
            h_scr[pl.ds(pl.multiple_of(g * th, th), th), :] = hb

        @pl.when(g == nh - 1)
        def _():
            gather.wait_sibling(0)
            tile_load(0, order_ref[0]).start()

        @pl.when(g >= nh)
        def _():
            t, i = (g - nh) // ni, (g - nh) % ni

            @pl.when(i == 0)
            def _():
                tile_load(t % 2, order_ref[t]).wait()

            @pl.when((i == ni - 1) & (t == 0))
            def _():
                gather.pass_on(0)
                gather.pass_on(1)
                gather.relay_diagonal()

            @pl.when((i == ni - 1) & (t == 2))
            def _():
                gather.pass_on(2)

            for j in range(3):
                @pl.when((i == ni - 1) & (t == j))
                def _():
                    gather.wait_sibling(4 + j)
                    tile_load((j + 1) % 2, order_ref[j + 1]).start()

            lhs = h_scr[pl.ds(pl.multiple_of(i * tm, tm), tm), :]
            proj_ref[...] = lax.dot_general(lhs, w_buf[t % 2], NT, preferred_element_type=F32).astype(BF16)

        @pl.when(g == steps - 1)
        def _():
            gather.wait_sends()

    def h_tile(g, order_ref):
        return (jnp.minimum(g, nh - 1), 0)

    def proj_tile(g, order_ref):
        mm = jnp.maximum(g - nh, 0)
        return (mm % ni, order_ref[mm // ni])

    row = pl.BlockSpec((1, D_MODEL), lambda g, order_ref: (0, 0))
    hbm = pl.BlockSpec(memory_space=pl.ANY)
    return pl.pallas_call(
        body, name="gather_in_proj",
        grid_spec=pltpu.PrefetchScalarGridSpec(
            num_scalar_prefetch=1, grid=(steps,),
            in_specs=[pl.BlockSpec((th, D_MODEL), h_tile), row, row, row, hbm],
            out_specs=[pl.BlockSpec((th, D_MODEL), h_tile), pl.BlockSpec((tm, tn), proj_tile), hbm],
            scratch_shapes=[pltpu.VMEM((s, D_MODEL), BF16), pltpu.VMEM((2, tn, D_MODEL), BF16),
                            pltpu.SemaphoreType.DMA((2,)), *_gather_scratch()]),
        out_shape=[SDS((s, D_MODEL), BF16), SDS((s, D_IN), BF16), SDS(wt_all.shape, BF16)],
        input_output_aliases={5: 2},
        compiler_params=_params("arbitrary"),
    )(order, x, shift, scale, norm_g, wt_all)


def _rope_tables(seq):
    inv_freq = ROPE_THETA ** (-jnp.arange(0, HEAD_DIM, 2, dtype=F32) / HEAD_DIM)
    ang = jnp.arange(seq, dtype=F32)[:, None] * inv_freq[None, :]
    cos, sin, zero = jnp.cos(ang), jnp.sin(ang), jnp.zeros_like(ang)
    return (jnp.concatenate([cos] * 4, axis=1), jnp.concatenate([-sin, zero, -sin, zero], axis=1),
            jnp.concatenate([zero, sin, zero, sin], axis=1))


def _rope(v, cos, sin_lo, sin_hi):
    width = v.shape[1]
    rep = (1, width // 128)
    return (v * jnp.tile(cos, rep) + pltpu.roll(v, width - 32, 1) * jnp.tile(sin_lo, rep)
            + pltpu.roll(v, 32, 1) * jnp.tile(sin_hi, rep))


def _rope_bwd(d, cos, sin_lo, sin_hi):
    width = d.shape[1]
    rep = (1, width // 128)
    return (d * jnp.tile(cos, rep) + pltpu.roll(d * jnp.tile(sin_lo, rep), 32, 1)
            + pltpu.roll(d * jnp.tile(sin_hi, rep), width - 32, 1))


def _layer_norm(v, g, b):
    mu = jnp.mean(v, axis=-1, keepdims=True)
    vc = v - mu
    rstd = lax.rsqrt(jnp.mean(vc * vc, axis=-1, keepdims=True) + NORM_EPS)
    vhat = vc * rstd
    return vhat * g + b, vhat, rstd


def _tril_bf16(w_ref, g):
    t = lax.broadcasted_iota(jnp.int32, (CHUNK, CHUNK), 0)
    tp = lax.broadcasted_iota(jnp.int32, (CHUNK, CHUNK), 1)
    return jnp.where(tp <= t, w_ref[g], 0.0).astype(BF16)


def _bias_columns(b_ref, out_ref):
    for g in range(A_GROUPS):
        out_ref[g] = jnp.broadcast_to(b_ref[pl.ds(g, 1), :], (CHUNK, CHUNK)).T


def _band_mask():
    kj = lax.broadcasted_iota(jnp.int32, (2 * CHUNK, 4 * CHUNK), 0)
    qi = lax.broadcasted_iota(jnp.int32, (2 * CHUNK, 4 * CHUNK), 1) & (CHUNK - 1)
    rel = qi + CHUNK - kj
    return jnp.where((rel >= 0) & (rel < CHUNK), 0.0, -jnp.inf)


def _low_lanes():
    return lax.broadcasted_iota(jnp.int32, (1, 128), 1) < HEAD_DIM


def _stack_heads(pair_a, pair_b):
    lo = _low_lanes()
    return jnp.concatenate([jnp.where(lo, pair_a, 0.0), jnp.where(lo, 0.0, pair_a),
                            jnp.where(lo, pair_b, 0.0), jnp.where(lo, 0.0, pair_b)], axis=0).astype(BF16)


def _heads_to_lanes(per_group):
    rows = [t[:, r * CHUNK:(r + 1) * CHUNK] for t in per_group for r in range(4)]
    return jnp.concatenate(rows, axis=0).T


def _dup_kv_head(band, gk):
    pair = band[:, (gk // 2) * 128:(gk // 2 + 1) * 128]
    lo = _low_lanes()
    one = jnp.where(lo if gk % 2 == 0 else jnp.logical_not(lo), pair, 0.0)
    return (one + pltpu.roll(one, HEAD_DIM, 1)).astype(BF16)


def _fold_kv_head(dup_grad, gk):
    both = dup_grad + pltpu.roll(dup_grad, HEAD_DIM, 1)
    lo = _low_lanes()
    return jnp.where(lo if gk % 2 == 0 else jnp.logical_not(lo), both, 0.0)


def _attn_probs(q_st, k_dup, sink_row, mask, first_block):
    s = lax.dot_general(k_dup, q_st, NT, preferred_element_type=F32) + mask
    s = jnp.concatenate([jnp.where(first_block, -jnp.inf, s[:CHUNK]), s[CHUNK:]], axis=0)
    m = jnp.maximum(jnp.max(s, axis=0, keepdims=True), sink_row)
    p = jnp.exp(s - m)
    e_sink = jnp.exp(sink_row - m)
    inv = 1.0 / (jnp.sum(p, axis=0, keepdims=True) + e_sink)
    return p * inv, e_sink * inv


def _sink_row(sinks_ref, gk):
    return jnp.concatenate([jnp.full((1, CHUNK), sinks_ref[4 * gk + r], F32) for r in range(4)], axis=1)


def _mixer_specs(nb, rev):
    def blk(i):
        return nb - 1 - i if rev else i

    def prev(i):
        return jnp.maximum(blk(i) - 1, 0)

    tab = pl.BlockSpec((CHUNK, 128), lambda i, *_: (blk(i), 0))
    tab_prev = pl.BlockSpec((CHUNK, 128), lambda i, *_: (prev(i), 0))
    return dict(
        cur=pl.BlockSpec((CHUNK, D_IN), lambda i, *_: (blk(i), 0)),
        prev_kv=pl.BlockSpec((CHUNK, 2 * 256), lambda i, *_: (prev(i), OFF_K // 512)),
        tabs=[tab] * 3 + [tab_prev] * 3,
        vec=pl.BlockSpec((1, D_A), lambda i, *_: (0, 0)),
        wsp=pl.BlockSpec((A_GROUPS, CHUNK, CHUNK), lambda i, *_: (0, 0, 0)),
        bsp=pl.BlockSpec((A_GROUPS, CHUNK), lambda i, *_: (0, 0)),
        smem=pl.BlockSpec(memory_space=pltpu.SMEM),
        blk=blk,
    )


def _mixer_fwd(proj, tabs, ln_g, ln_b, w_sp, b_sp, sinks, wo_all):
    s = proj.shape[0]
    nb = s // CHUNK
    sp = _mixer_specs(nb, rev=False)

    def body(cur_ref, pkv_ref, c_ref, s1_ref, s2_ref, cp_ref, s1p_ref, s2p_ref, lg_ref, lb_ref, w_ref, b_ref,
             sinks_ref, wo_in, y_ref, wo_ref, bcol, mask, send_sems, recv_sems):
        i = pl.program_id(0)
        gather = _InPlaceGather(wo_ref, send_sems, recv_sems)

        @pl.when(i == 0)
        def _():
            gather.start()
            _bias_columns(b_ref, bcol)
            mask[...] = _band_mask()

        @pl.when(i == nb // 2)
        def _():
            for j in range(3):
                gather.pass_on(j)

        vln, _, _ = _layer_norm(cur_ref[:, OFF_VA:OFF_ZA].astype(F32), lg_ref[...], lb_ref[...])
        vln = vln.astype(BF16)
        for g in range(A_GROUPS):
            cols = slice(g * 128, (g + 1) * 128)
            sg = jnp.dot(_tril_bf16(w_ref, g), vln[:, cols], preferred_element_type=F32) + bcol[g]
            u = cur_ref[:, OFF_U + g * 128:OFF_U + (g + 1) * 128].astype(F32)
            z = cur_ref[:, OFF_ZA + g * 128:OFF_ZA + (g + 1) * 128].astype(F32)
            y_ref[:, cols] = (u * sg * (z * _sigmoid(z))).astype(BF16)

        cur_t = (c_ref[...], s1_ref[...], s2_ref[...])
        prev_t = (cp_ref[...], s1p_ref[...], s2p_ref[...])
        qr = _rope(cur_ref[:, OFF_Q:OFF_K].astype(F32), *cur_t) * ATTN_SCALE
        kr = jnp.concatenate([_rope(pkv_ref[:, 0:256].astype(F32), *prev_t),
                              _rope(cur_ref[:, OFF_K:OFF_V].astype(F32), *cur_t)], axis=0)
        v_t = jnp.concatenate([pkv_ref[:, 256:512], cur_ref[:, OFF_V:OFF_ZB]], axis=0).astype(F32).T.astype(BF16)
        outs = []
        for gk in range(N_KV_HEADS):
            q_st = _stack_heads(qr[:, (2 * gk) * 128:(2 * gk + 1) * 128], qr[:, (2 * gk + 1) * 128:(2 * gk + 2) * 128])
            probs, _ = _attn_probs(q_st, _dup_kv_head(kr, gk), _sink_row(sinks_ref, gk), mask[...], i == 0)
            outs.append(jnp.dot(v_t[gk * HEAD_DIM:(gk + 1) * HEAD_DIM], probs.astype(BF16),
                                preferred_element_type=F32))
        zb = cur_ref[:, OFF_ZB:D_IN].astype(F32)
        y_ref[:, D_A:D_MODEL] = (_heads_to_lanes(outs) * (zb * _sigmoid(zb))).astype(BF16)

        @pl.when(i == nb - 1)
        def _():
            gather.wait_sibling(0)
            for j in range(3):
                gather.wait_sibling(4 + j)
            gather.wait_sends()

    hbm = pl.BlockSpec(memory_space=pl.ANY)
    return pl.pallas_call(
        body, name="mixer_fwd", grid=(nb,),
        in_specs=[sp["cur"], sp["prev_kv"], *sp["tabs"], sp["vec"], sp["vec"], sp["wsp"], sp["bsp"], sp["smem"], hbm],
        out_specs=[pl.BlockSpec((CHUNK, D_MODEL), lambda i: (i, 0)), hbm],
        out_shape=[SDS((s, D_MODEL), BF16), SDS(wo_all.shape, wo_all.dtype)],
        scratch_shapes=[pltpu.VMEM((A_GROUPS, CHUNK, CHUNK), F32), pltpu.VMEM((2 * CHUNK, 4 * CHUNK), F32),
                        *_gather_scratch()],
        input_output_aliases={13: 1},
        compiler_params=_params("arbitrary"),
    )(proj, proj, *tabs, *tabs, ln_g, ln_b, w_sp, b_sp, sinks, wo_all)


def _out_proj_loss(y, x, target, wo, gate, shift_f, scale_f, fng):
    s = y.shape[0]
    tm, tr = 256, 128
    nt = s // tm

    def body(y_ref, x_ref, t_ref, wo_ref, gate_ref, sh_ref, sc_ref, g_ref, dx1_ref, do_ref, dy_ref, sums_ref,
             do_last, do_work):
        i = pl.program_id(0)

        @pl.when(i == 0)
        def _():
            sums_ref[...] = jnp.zeros_like(sums_ref)
            do_last[...] = jnp.zeros_like(do_last)

        do_work[...] = do_last[...]
        o = jnp.dot(y_ref[...], wo_ref[...], preferred_element_type=F32)
        gate, g, sh = gate_ref[...], g_ref[...], sh_ref[...]
        one_sc = 1.0 + sc_ref[...]
        cs, inv_d = g * one_sc, 1.0 / D_MODEL

        def rowsum(v):
            return jnp.sum(v, axis=0, keepdims=True)

        sums = [jnp.zeros((1, D_MODEL), F32) for _ in range(4)]
        for c in range(tm // tr):
            rows = slice(c * tr, (c + 1) * tr)
            oc = o[rows]
            x1 = x_ref[rows, :] + gate * oc
            r = lax.rsqrt(jnp.sum(x1 * x1, axis=-1, keepdims=True) * inv_d + NORM_EPS)
            x1n = x1 * r
            diff = x1n * cs + sh - t_ref[rows, :]
            w = diff * x1n
            lane_sum = jnp.sum(w * cs, axis=-1, keepdims=True)
            dx1 = (diff * cs) * (r * inv_d) - x1n * (r * lane_sum * (inv_d * inv_d))
            dx1_ref[rows, :] = dx1
            do = (dx1 * gate).astype(BF16)
            do_ref[rows, :] = do
            do_last[rows, :] = do
            for k, v in enumerate((dx1 * oc, diff, w, diff * diff)):
                sums[k] = sums[k] + rowsum(v)
        live = jnp.where(i < nt, 1.0, 0.0)
        sums_ref[0:1, :] += live * sums[0]
        sums_ref[1:2, :] += (live * inv_d) * sums[1]
        sums_ref[2:3, :] += (live * inv_d) * (sums[2] * g)
        sums_ref[3:4, :] += (live * inv_d) * (sums[2] * one_sc)
        sums_ref[4:5, :] += live * sums[3]
        dy_ref[...] = lax.dot_general(do_work[...], wo_ref[...], NT, preferred_element_type=F32).astype(BF16)

    tile = pl.BlockSpec((tm, D_MODEL), lambda i: (jnp.minimum(i, nt - 1), 0))
    row = pl.BlockSpec((1, D_MODEL), lambda i: (0, 0))
    return pl.pallas_call(
        body, name="out_proj_loss", grid=(nt + 1,),
        in_specs=[tile, tile, tile, pl.BlockSpec((D_MODEL, D_MODEL), lambda i: (0, 0)), row, row, row, row],
        out_specs=[tile, tile, pl.BlockSpec((tm, D_MODEL), lambda i: (jnp.maximum(i - 1, 0), 0)),
                   pl.BlockSpec((8, D_MODEL), lambda i: (0, 0))],
        out_shape=[SDS((s, D_MODEL), F32), SDS((s, D_MODEL), BF16), SDS((s, D_MODEL), BF16), SDS((8, D_MODEL), F32)],
        scratch_shapes=[pltpu.VMEM((tm, D_MODEL), BF16), pltpu.VMEM((tm, D_MODEL), BF16)],
        compiler_params=_params("arbitrary"),
    )(y, x, target, wo, gate, shift_f, scale_f, fng)


ROW_DBSP, ROW_DSINKS, MISC_ROWS = 0, 8, 32


def _mixer_bwd(me, proj, dy, tabs, ln_g, ln_b, w_sp, b_sp, sinks, pair):
    s = proj.shape[0]
    nb = s // CHUNK
    sp = _mixer_specs(nb, rev=True)

    def body(me_ref, cur_ref, pkv_ref, dy_ref, c_ref, s1_ref, s2_ref, cp_ref, s1p_ref, s2p_ref, lg_ref, lb_ref, w_ref,
             b_ref, sinks_ref, pair_ref, dproj_ref, dln_ref, dw_ref, misc_ref, parts_ref, bcol, dbcol, carry, mask,
             send_sems, recv_sems):
        i = pl.program_id(0)
        block = nb - 1 - i

        @pl.when(i == 0)
        def _():
            for cp in _chip_scatter(pair_ref, parts_ref, send_sems, recv_sems):
                cp.start()
            _bias_columns(b_ref, bcol)
            mask[...] = _band_mask()
            dbcol[...] = jnp.zeros_like(dbcol)
            carry[...] = jnp.zeros_like(carry)
            dln_ref[...] = jnp.zeros_like(dln_ref)
            dw_ref[...] = jnp.zeros_like(dw_ref)
            misc_ref[...] = jnp.zeros_like(misc_ref)

        vln, vhat, rstd = _layer_norm(cur_ref[:, OFF_VA:OFF_ZA].astype(F32), lg_ref[...], lb_ref[...])
        vln = vln.astype(BF16)
        d_vln = []
        for g in range(A_GROUPS):
            cols = slice(g * 128, (g + 1) * 128)
            w_g = _tril_bf16(w_ref, g)
            sg = jnp.dot(w_g, vln[:, cols], preferred_element_type=F32) + bcol[g]
            u = cur_ref[:, OFF_U + g * 128:OFF_U + (g + 1) * 128].astype(F32)
            z = cur_ref[:, OFF_ZA + g * 128:OFF_ZA + (g + 1) * 128].astype(F32)
            dya = dy_ref[:, cols].astype(F32)
            sig = _sigmoid(z)
            d_ya = dya * (z * sig)
            dproj_ref[:, OFF_ZA + g * 128:OFF_ZA + (g + 1) * 128] = (
                dya * (u * sg) * (sig * (1.0 + z * (1.0 - sig)))).astype(BF16)
            dproj_ref[:, OFF_U + g * 128:OFF_U + (g + 1) * 128] = (d_ya * sg).astype(BF16)
            d_s = d_ya * u
            dbcol[g] += d_s
            d_sb = d_s.astype(BF16)
            dw_ref[g] += lax.dot_general(d_sb, vln[:, cols], NT, preferred_element_type=F32)
            d_vln.append(lax.dot_general(w_g, d_sb, TN, preferred_element_type=F32))
        d_vln = jnp.concatenate(d_vln, axis=1)
        dln_ref[0:1, :] += jnp.sum(d_vln * vhat, axis=0, keepdims=True)
        dln_ref[1:2, :] += jnp.sum(d_vln, axis=0, keepdims=True)
        d_vhat = d_vln * lg_ref[...]
        d_va = rstd * (d_vhat - jnp.mean(d_vhat, axis=-1, keepdims=True)
                       - vhat * jnp.mean(d_vhat * vhat, axis=-1, keepdims=True))
        dproj_ref[:, OFF_VA:OFF_ZA] = d_va.astype(BF16)

        cur_t = (c_ref[...], s1_ref[...], s2_ref[...])
        prev_t = (cp_ref[...], s1p_ref[...], s2p_ref[...])
        band_t = tuple(jnp.concatenate([p, c], axis=0) for p, c in zip(prev_t, cur_t))
        qr = _rope(cur_ref[:, OFF_Q:OFF_K].astype(F32), *cur_t) * ATTN_SCALE
        kr = jnp.concatenate([_rope(pkv_ref[:, 0:256].astype(F32), *prev_t),
                              _rope(cur_ref[:, OFF_K:OFF_V].astype(F32), *cur_t)], axis=0)
        vb = jnp.concatenate([pkv_ref[:, 256:512], cur_ref[:, OFF_V:OFF_ZB]], axis=0).astype(F32)
        k_t, v_t = (kr.T * ATTN_SCALE).astype(BF16), vb.T.astype(BF16)
        zb = cur_ref[:, OFF_ZB:D_IN].astype(F32)
        dyb = dy_ref[:, D_A:D_MODEL].astype(F32)
        sig = _sigmoid(zb)
        d_yb = dyb * (zb * sig)
        outs, dqs = [], []
        dk_pairs = [jnp.zeros((2 * CHUNK, 128), F32) for _ in range(2)]
        dv_pairs = [jnp.zeros((2 * CHUNK, 128), F32) for _ in range(2)]
        for gk in range(N_KV_HEADS):
            heads = slice(gk * HEAD_DIM, (gk + 1) * HEAD_DIM)
            q_st = _stack_heads(qr[:, (2 * gk) * 128:(2 * gk + 1) * 128], qr[:, (2 * gk + 1) * 128:(2 * gk + 2) * 128])
            k_dup, v_dup = _dup_kv_head(kr, gk), _dup_kv_head(vb, gk)
            probs, p_sink = _attn_probs(q_st, k_dup, _sink_row(sinks_ref, gk), mask[...], block == 0)
            probs_b = probs.astype(BF16)
            outs.append(jnp.dot(v_t[heads], probs_b, preferred_element_type=F32))
            do_st = _stack_heads(d_yb[:, (2 * gk) * 128:(2 * gk + 1) * 128], d_yb[:, (2 * gk + 1) * 128:(2 * gk + 2) * 128])
            dp = lax.dot_general(v_dup, do_st, NT, preferred_element_type=F32)
            delta = jnp.sum(probs * dp, axis=0, keepdims=True)
            ds = (probs * (dp - delta)).astype(BF16)
            d_sink = -p_sink * delta
            for r in range(4):
                row = ROW_DSINKS + 4 * gk + r
                misc_ref[row:row + 1, :] += jnp.broadcast_to(
                    jnp.sum(d_sink[:, r * CHUNK:(r + 1) * CHUNK], axis=1, keepdims=True), (1, 128))
            dqs.append(jnp.dot(k_t[heads], ds, preferred_element_type=F32))
            dk_pairs[gk // 2] += _fold_kv_head(jnp.dot(ds, q_st, preferred_element_type=F32), gk)
            dv_pairs[gk // 2] += _fold_kv_head(jnp.dot(probs_b, do_st, preferred_element_type=F32), gk)
        dproj_ref[:, OFF_ZB:D_IN] = (dyb * _heads_to_lanes(outs) * (sig * (1.0 + zb * (1.0 - sig)))).astype(BF16)
        dproj_ref[:, OFF_Q:OFF_K] = _rope_bwd(_heads_to_lanes(dqs), *cur_t).astype(BF16)
        dk_band = _rope_bwd(jnp.concatenate(dk_pairs, axis=1), *band_t)
        dv_band = jnp.concatenate(dv_pairs, axis=1)
        dproj_ref[:, OFF_K:OFF_V] = (dk_band[CHUNK:] + carry[:, 0:256]).astype(BF16)
        dproj_ref[:, OFF_V:OFF_ZB] = (dv_band[CHUNK:] + carry[:, 256:512]).astype(BF16)
        carry[:, 0:256] = dk_band[:CHUNK]
        carry[:, 256:512] = dv_band[:CHUNK]

        @pl.when(i == nb - 1)
        def _():
            t = lax.broadcasted_iota(jnp.int32, (CHUNK, CHUNK), 0)
            tp = lax.broadcasted_iota(jnp.int32, (CHUNK, CHUNK), 1)
            for g in range(A_GROUPS):
                dw_ref[g] = jnp.where(tp <= t, dw_ref[g], 0.0)
                misc_ref[pl.ds(ROW_DBSP + g, 1), :] = jnp.sum(dbcol[g].T, axis=0, keepdims=True)
            scatter = _chip_scatter(pair_ref, parts_ref, send_sems, recv_sems)
            for cp in scatter:
                cp.wait_recv()
            for cp in scatter:
                cp.wait_send()

    blk = sp["blk"]
    hbm = pl.BlockSpec(memory_space=pl.ANY)
    return pl.pallas_call(
        body, name="mixer_bwd",
        grid_spec=pltpu.PrefetchScalarGridSpec(
            num_scalar_prefetch=1, grid=(nb,),
            in_specs=[sp["cur"], sp["prev_kv"], pl.BlockSpec((CHUNK, D_MODEL), lambda i, me_ref: (blk(i), 0)),
                      *sp["tabs"], sp["vec"], sp["vec"], sp["wsp"], sp["bsp"], sp["smem"], hbm],
            out_specs=[pl.BlockSpec((CHUNK, D_IN), lambda i, me_ref: (blk(i), 0)),
                       pl.BlockSpec((8, D_A), lambda i, me_ref: (me_ref[0], 0)),
                       pl.BlockSpec((A_GROUPS, CHUNK, CHUNK), lambda i, me_ref: (me_ref[0], 0, 0)),
                       pl.BlockSpec((MISC_ROWS, 128), lambda i, me_ref: (me_ref[0], 0)), hbm],
            scratch_shapes=[pltpu.VMEM((A_GROUPS, CHUNK, CHUNK), F32), pltpu.VMEM((A_GROUPS, CHUNK, CHUNK), F32),
                            pltpu.VMEM((CHUNK, 512), F32), pltpu.VMEM((2 * CHUNK, 4 * CHUNK), F32),
                            *_scatter_scratch()]),
        out_shape=[SDS((s, D_IN), BF16), SDS((N_DEV * 8, D_A), F32), SDS((N_DEV * A_GROUPS, CHUNK, CHUNK), F32),
                   SDS((N_DEV * MISC_ROWS, 128), F32), SDS((3,) + pair.shape[1:], pair.dtype)],
        compiler_params=_params("arbitrary"),
    )(me, proj, proj, dy, *tabs, *tabs, ln_g, ln_b, w_sp, b_sp, sinks, pair)


def _wgrad_pair(name, a, b, gathers=()):
    s, m = a.shape
    n = b.shape[1]
    bm, half = m // 4, m // 8
    bt = min(1024, s)
    steps = s // bt
    last = 4 * steps
    n_g = len(gathers)

    def body(*refs):
        a_ref, b_ref = refs[:2]
        out_ref, bufs = refs[2 + n_g], refs[3 + n_g:3 + 2 * n_g]
        acc, kept, got, sent, send_sems, recv_sems = refs[3 + 2 * n_g:9 + 2 * n_g]
        sems = refs[9 + 2 * n_g:]
        g = pl.program_id(0)
        tile, t = g // steps, g % steps
        mx, my, mc = _mesh_pos()
        jobs = [_InPlaceGather(bufs[k], sems[2 * k], sems[2 * k + 1]) for k in range(n_g)]

        def exchange(q):
            return pltpu.make_async_remote_copy(src_ref=sent, dst_ref=got.at[q % 2], send_sem=send_sems.at[q],
                                                recv_sem=recv_sems.at[q], device_id=(mx, my, 1 - mc),
                                                device_id_type=MESH)

        @pl.when(g == 0)
        def _():
            for job in jobs:
                job.start()

        @pl.when(g == 2 * steps)
        def _():
            for job in jobs:
                for j in range(3):
                    job.pass_on(j)

        @pl.when(g < last)
        def _():
            prod = lax.dot_general(a_ref[...], b_ref[...], TN, preferred_element_type=F32)

            @pl.when(t == 0)
            def _():
                acc[...] = prod

            @pl.when(t > 0)
            def _():
                acc[...] += prod

            @pl.when(t == steps - 1)
            def _():
                @pl.when(tile > 0)
                def _():
                    exchange(tile - 1).wait_send()

                kept[tile % 2] = acc[pl.ds(pl.multiple_of(mc * half, 8), half), :].astype(BF16)
                sent[...] = acc[pl.ds(pl.multiple_of((1 - mc) * half, 8), half), :].astype(BF16)
                exchange(tile).start()

        @pl.when((t == 0) & (g > 0))
        def _():
            q = tile - 1
            exchange(q).wait_recv()
            out_ref[0] = (kept[q % 2].astype(F32) + got[q % 2].astype(F32)).astype(BF16)

        @pl.when(g == last)
        def _():
            exchange(3).wait_send()
            for job in jobs:
                job.wait_sibling(0)
                for j in range(3):
                    job.wait_sibling(4 + j)
                job.wait_sends()

    def a_tile(g):
        gg = jnp.minimum(g, last - 1)
        return (gg % steps, gg // steps)

    def b_tile(g):
        return (jnp.minimum(g, last - 1) % steps, 0)

    hbm = pl.BlockSpec(memory_space=pl.ANY)
    outs = pl.pallas_call(
        body, name=name, grid=(last + 1,),
        in_specs=[pl.BlockSpec((bt, bm), a_tile), pl.BlockSpec((bt, n), b_tile)] + [hbm] * n_g,
        out_specs=[pl.BlockSpec((1, half, n), lambda g: (jnp.maximum(g - 1, 0) // steps, 0, 0))] + [hbm] * n_g,
        out_shape=[SDS((4, half, n), BF16)] + [SDS(gb.shape, gb.dtype) for gb in gathers],
        scratch_shapes=[pltpu.VMEM((bm, n), F32), pltpu.VMEM((2, half, n), BF16), pltpu.VMEM((2, half, n), BF16),
                        pltpu.VMEM((half, n), BF16), pltpu.SemaphoreType.DMA((4,)), pltpu.SemaphoreType.DMA((4,))]
        + _gather_scratch() * n_g,
        input_output_aliases={2 + k: 1 + k for k in range(n_g)},
        compiler_params=_params("arbitrary"),
    )(a, b, *gathers)
    return outs[0], outs[1:]


def _in_proj_bwd(dproj, wt, x, dx1, scale, norm_g, pair):
    s = x.shape[0]
    tm, tk, tr = min(1024, s), D_IN // 4, 64
    ksteps = D_IN // tk

    def body(dp_ref, wt_ref, x_hbm, dx1_hbm, sc_ref, g_ref, pair_ref, gx_ref, sums_ref, parts_ref, x_buf, dx1_buf,
             tile_sems, send_sems, recv_sems):
        i, k = pl.program_id(0), pl.program_id(1)

        def tile_copies():
            rows = pl.ds(pl.multiple_of(i * tm, tm), tm)
            return (pltpu.make_async_copy(x_hbm.at[rows], x_buf, tile_sems.at[0]),
                    pltpu.make_async_copy(dx1_hbm.at[rows], dx1_buf, tile_sems.at[1]))

        @pl.when((i == 0) & (k == 0))
        def _():
            for cp in _chip_scatter(pair_ref, parts_ref, send_sems, recv_sems):
                cp.start()
            sums_ref[...] = jnp.zeros_like(sums_ref)

        @pl.when(k == 0)
        def _():
            for cp in tile_copies():
                cp.start()
            gx_ref[...] = jnp.dot(dp_ref[...], wt_ref[...], preferred_element_type=F32)

        @pl.when(k > 0)
        def _():
            gx_ref[...] += jnp.dot(dp_ref[...], wt_ref[...], preferred_element_type=F32)

        @pl.when(k == ksteps - 1)
        def _():
            for cp in tile_copies():
                cp.wait()
            one_sc, g = 1.0 + sc_ref[...], g_ref[...]
            cs = one_sc * g

            def chunk(j, sums):
                rows = pl.ds(pl.multiple_of(j * tr, tr), tr)
                dh, xv = gx_ref[rows, :], x_buf[rows, :]
                dhx = dh * xv
                r = lax.rsqrt(jnp.sum(xv * xv, axis=-1, keepdims=True) * (1.0 / D_MODEL) + NORM_EPS)
                coef = (r * r * r) * (jnp.sum(dhx * cs, axis=-1, keepdims=True) * (1.0 / D_MODEL))
                gx_ref[rows, :] = dx1_buf[rows, :] + r * (dh * cs) - xv * coef
                return (sums[0] + jnp.sum(dh, axis=0, keepdims=True), sums[1] + jnp.sum(dhx * r, axis=0, keepdims=True))

            zero = jnp.zeros((1, D_MODEL), F32)
            sums = lax.fori_loop(0, tm // tr, chunk, (zero, zero))
            sums_ref[0:1, :] += sums[0]
            sums_ref[1:2, :] += sums[1] * g
            sums_ref[2:3, :] += sums[1] * one_sc

        @pl.when((i == s // tm - 1) & (k == ksteps - 1))
        def _():
            scatter = _chip_scatter(pair_ref, parts_ref, send_sems, recv_sems)
            for cp in scatter:
                cp.wait_recv()
            for cp in scatter:
                cp.wait_send()

    row = pl.BlockSpec((1, D_MODEL), lambda i, k: (0, 0))
    hbm = pl.BlockSpec(memory_space=pl.ANY)
    return pl.pallas_call(
        body, name="in_proj_bwd", grid=(s // tm, ksteps),
        in_specs=[pl.BlockSpec((tm, tk), lambda i, k: (i, k)), pl.BlockSpec((tk, D_MODEL), lambda i, k: (k, 0)),
                  hbm, hbm, row, row, hbm],
        out_specs=[pl.BlockSpec((tm, D_MODEL), lambda i, k: (i, 0)), pl.BlockSpec((8, D_MODEL), lambda i, k: (0, 0)),
                   hbm],
        out_shape=[SDS((s, D_MODEL), F32), SDS((8, D_MODEL), F32), SDS((3,) + pair.shape[1:], pair.dtype)],
        scratch_shapes=[pltpu.VMEM((tm, D_MODEL), F32), pltpu.VMEM((tm, D_MODEL), F32),
                        pltpu.SemaphoreType.DMA((2,)), *_scatter_scratch()],
        compiler_params=_params("arbitrary", "arbitrary"),
    )(dproj, wt, x, dx1, scale, norm_g, pair)


def _sum_chips(own_ref, parts_ref):
    return ((own_ref[0].astype(F32) + parts_ref[0].astype(F32)) + parts_ref[1].astype(F32)) + parts_ref[2].astype(F32)


def _adam_rows(name, chip, pair, parts, w, m, v):
    rows = w.shape[0]
    tr = rows // 4

    def body(chip_ref, own_ref, p_ref, w_ref, m_ref, v_ref, g_ref, d_ref, nm_ref, nv_ref):
        g = _sum_chips(own_ref, p_ref)
        g_ref[...] = g
        d_ref[...], nm_ref[...], nv_ref[...] = _adamw(w_ref[...], g, m_ref[...], v_ref[...])

    blk = pl.BlockSpec((tr, D_MODEL), lambda j, chip_ref: (j, 0))
    return pl.pallas_call(
        body, name=name,
        grid_spec=pltpu.PrefetchScalarGridSpec(
            num_scalar_prefetch=1, grid=(rows // tr,),
            in_specs=[pl.BlockSpec((1, tr, D_MODEL), lambda j, chip_ref: (chip_ref[0], j, 0)),
                      pl.BlockSpec((3, tr, D_MODEL), lambda j, chip_ref: (0, j, 0)), blk, blk, blk],
            out_specs=[blk] * 4),
        out_shape=[SDS(w.shape, F32)] * 4, compiler_params=_params("parallel"),
    )(chip, pair, parts, w, m, v)


def _adam_ada(name, cact, dmod, w, m, v):
    n = w.shape[1]
    tr = 512

    def body(c_ref, dm_ref, w_ref, m_ref, v_ref, g_ref, d_ref, nm_ref, nv_ref):
        pad_c = jnp.concatenate([c_ref[...], jnp.zeros_like(c_ref)], axis=0).astype(BF16)
        pad_d = jnp.concatenate([dm_ref[...], jnp.zeros_like(dm_ref)], axis=0).astype(BF16)
        g = lax.dot_general(pad_c, pad_d, TN, preferred_element_type=F32)
        g_ref[...] = g
        d_ref[...], nm_ref[...], nv_ref[...] = _adamw(w_ref[...], g, m_ref[...], v_ref[...])

    blk = pl.BlockSpec((tr, n), lambda j: (j, 0))
    return pl.pallas_call(
        body, name=name, grid=(D_MODEL // tr,),
        in_specs=[pl.BlockSpec((N_DEV, tr), lambda j: (0, j)), pl.BlockSpec((N_DEV, n), lambda j: (0, 0)),
                  blk, blk, blk],
        out_specs=[blk] * 4, out_shape=[SDS(w.shape, F32)] * 4,
        compiler_params=_params("parallel"),
    )(cact, dmod, w, m, v)


SMALL_PARAMS = ("w_spatial", "b_spatial", "sinks", "norm_g", "ln_v_g", "ln_v_b", "final_norm_g", "b_ada", "b_ada_final")


def _adam_small(d_wsp, misc, d_ln, sums_i, sums_o, params):
    n_p = len(SMALL_PARAMS)

    def body(*refs):
        wsp_ref, misc_ref, ln_ref, si_ref, so_ref = refs[:5]
        wmv = [refs[5 + 3 * k:8 + 3 * k] for k in range(n_p)]
        loss_ref = refs[5 + 3 * n_p]
        outs = [refs[6 + 3 * n_p + 4 * k:10 + 3 * n_p + 4 * k] for k in range(n_p)]

        def total(ref, rows=None):
            def part(j):
                return ref[j] if rows is None else ref[j, rows[0]:rows[1], :]
            acc = part(0)
            for j in range(1, N_DEV):
                acc = acc + part(j)
            return acc

        sink_rows = total(misc_ref, (ROW_DSINKS, ROW_DSINKS + 16))
        diag = (lax.broadcasted_iota(jnp.int32, (16, 128), 0) == lax.broadcasted_iota(jnp.int32, (16, 128), 1))
        grads = dict(
            w_spatial=total(wsp_ref), b_spatial=total(misc_ref, (ROW_DBSP, ROW_DBSP + A_GROUPS)),
            sinks=jnp.sum(jnp.where(diag, sink_rows, 0.0), axis=0, keepdims=True),
            norm_g=total(si_ref, (2, 3)), ln_v_g=total(ln_ref, (0, 1)), ln_v_b=total(ln_ref, (1, 2)),
            final_norm_g=total(so_ref, (3, 4)),
            b_ada=jnp.concatenate([total(si_ref, (0, 1)), total(si_ref, (1, 2)), total(so_ref, (0, 1))], axis=1),
            b_ada_final=jnp.concatenate([total(so_ref, (1, 2)), total(so_ref, (2, 3))], axis=1))
        sq_err = jnp.sum(total(so_ref, (4, 5)), axis=1, keepdims=True)
        loss_ref[...] = jnp.broadcast_to(sq_err * (0.5 / D_MODEL), (1, 128))
        for k, name in enumerate(SMALL_PARAMS):
            w_ref, m_ref, v_ref = wmv[k]
            g_ref, d_ref, nm_ref, nv_ref = outs[k]
            g_ref[...] = grads[name]
            d_ref[...], nm_ref[...], nv_ref[...] = _adamw(w_ref[...], grads[name], m_ref[...], v_ref[...])

    flat = [a for name in SMALL_PARAMS for a in params[name]]
    vmem = pl.BlockSpec(memory_space=pltpu.VMEM)
    out_shape = [SDS((1, 128), F32)] + [SDS(params[name][0].shape, F32) for name in SMALL_PARAMS for _ in range(4)]
    outs = pl.pallas_call(
        body, name="adam_small", in_specs=[vmem] * (5 + len(flat)), out_specs=[vmem] * len(out_shape),
        out_shape=out_shape, compiler_params=_params(),
    )(d_wsp, misc, d_ln, sums_i, sums_o, *flat)
    return outs[0], {name: outs[1 + 4 * k:5 + 4 * k] for k, name in enumerate(SMALL_PARAMS)}


def kernel(x, c, w_ada, b_ada, norm_g, w_in, ln_v_g, ln_v_b, w_spatial, b_spatial, sinks, w_out, w_ada_final, b_ada_final, final_norm_g, loss_target, m_w_ada, m_b_ada, m_norm_g, m_w_in, m_ln_v_g, m_ln_v_b, m_w_spatial, m_b_spatial, m_sinks, m_w_out, m_w_ada_final, m_b_ada_final, m_final_norm_g, v_w_ada, v_b_ada, v_norm_g, v_w_in, v_ln_v_g, v_ln_v_b, v_w_spatial, v_b_spatial, v_sinks, v_w_out, v_w_ada_final, v_b_ada_final, v_final_norm_g):
    seq = x.shape[1]
    me = 4 * lax.axis_index("x") + 2 * lax.axis_index("y") + lax.axis_index("c")
    x2, tgt = x[0], loss_target[0]
    fng = final_norm_g.reshape(1, D_MODEL)

    n_ada, n_ada_f = w_ada.shape[2], w_ada_final.shape[1]
    cact, mod, mod_f = _ada_exchange(c, w_ada[0], b_ada.reshape(N_DEV, n_ada), w_ada_final,
                                     b_ada_final.reshape(N_DEV, n_ada_f))
    cact = cact.reshape(N_DEV, D_MODEL)
    mod, mod_f = mod.reshape(1, 3 * D_MODEL), mod_f.reshape(1, 2 * D_MODEL)
    shift, scale, gate = mod[:, :D_MODEL], mod[:, D_MODEL:2 * D_MODEL], mod[:, 2 * D_MODEL:]
    shift_f, scale_f = mod_f[:, :D_MODEL], mod_f[:, D_MODEL:]

    wt_f32, m_wt, v_wt = (jnp.swapaxes(a, 1, 2)[0] for a in (w_in, m_w_in, v_w_in))
    xi, yi = lax.axis_index("x"), lax.axis_index("y")
    chip_order = jnp.stack([2 * xi + yi, 2 * (1 - xi) + yi, 2 * xi + 1 - yi, 2 * (1 - xi) + 1 - yi]).astype(jnp.int32)
    wt_mine, wo_mine = _prep_weights(me.reshape(1), wt_f32, w_out[0])

    tabs = _rope_tables(seq)
    sinks_v = sinks.reshape(16)
    h, proj, wt = _gather_in_proj(chip_order, x2, shift, scale, norm_g, wt_mine)
    y, wo = _mixer_fwd(proj, tabs, ln_v_g, ln_v_b, w_spatial[0], b_spatial[0], sinks_v, wo_mine)
    dx1, do, dy, sums_o = _out_proj_loss(y, x2, tgt, wo, gate, shift_f, scale_f, fng)

    chip = (2 * lax.axis_index("x") + lax.axis_index("y")).reshape(1)
    pair_out, _ = _wgrad_pair("wgrad_out", y, do)
    dproj, d_ln, d_wsp, misc, parts_out = _mixer_bwd(
        me.reshape(1), proj, dy, tabs, ln_v_g, ln_v_b, w_spatial[0], b_spatial[0], sinks_v, pair_out)
    pair_in, (d_ln, d_wsp, misc) = _wgrad_pair(
        "wgrad_in", dproj, h, gathers=(d_ln, d_wsp.reshape(N_DEV * A_GROUPS * CHUNK, CHUNK), misc))
    grad_x, sums_i, parts_in = _in_proj_bwd(dproj, wt, x2, dx1, scale, norm_g, pair_in)
    wt_leaves = [jnp.swapaxes(a[None], 1, 2) for a in _adam_rows("adam_w_in", chip, pair_in, parts_in, wt_f32, m_wt, v_wt)]
    w_out_leaves = [a[None] for a in _adam_rows("adam_w_out", chip, pair_out, parts_out, w_out[0], m_w_out[0], v_w_out[0])]

    sums_i, sums_o = _all_gather("gather_sums", [sums_i, sums_o], pltpu.VMEM)
    natural = dict(w_spatial=(A_GROUPS * CHUNK, CHUNK), b_spatial=(A_GROUPS, CHUNK), sinks=(1, 16), norm_g=(1, D_MODEL),
                   ln_v_g=(1, D_A), ln_v_b=(1, D_A), final_norm_g=(1, D_MODEL), b_ada=(1, 3 * D_MODEL),
                   b_ada_final=(1, 2 * D_MODEL))
    given = dict(
        w_spatial=(w_spatial, m_w_spatial, v_w_spatial), b_spatial=(b_spatial, m_b_spatial, v_b_spatial),
        sinks=(sinks, m_sinks, v_sinks), norm_g=(norm_g, m_norm_g, v_norm_g), ln_v_g=(ln_v_g, m_ln_v_g, v_ln_v_g),
        ln_v_b=(ln_v_b, m_ln_v_b, v_ln_v_b), final_norm_g=(final_norm_g, m_final_norm_g, v_final_norm_g),
        b_ada=(b_ada, m_b_ada, v_b_ada), b_ada_final=(b_ada_final, m_b_ada_final, v_b_ada_final))
    params = {name: tuple(a.reshape(natural[name]) for a in given[name]) for name in SMALL_PARAMS}
    params["sinks"] = tuple(jnp.pad(a, ((0, 0), (0, 128 - 16))) for a in params["sinks"])
    loss, small = _adam_small(d_wsp.reshape(N_DEV, A_GROUPS * CHUNK, CHUNK), misc.reshape(N_DEV, MISC_ROWS, 128),
                              d_ln.reshape(N_DEV, 8, D_A), sums_i, sums_o, params)
    small["sinks"] = [a[:, :16] for a in small["sinks"]]
    small = {name: [a.reshape(given[name][0].shape) for a in small[name]] for name in SMALL_PARAMS}

    dmod_all = jnp.concatenate([sums_i[:, 0], sums_i[:, 1], sums_o[:, 0]], axis=1)
    dmod_f_all = jnp.concatenate([sums_o[:, 1], sums_o[:, 2]], axis=1)
    dmod_mine = lax.dynamic_slice_in_dim(dmod_all, me * n_ada, n_ada, axis=1)
    dmod_f_mine = lax.dynamic_slice_in_dim(dmod_f_all, me * n_ada_f, n_ada_f, axis=1)
    ada = _adam_ada("adam_w_ada", cact, dmod_mine, w_ada[0], m_w_ada[0], v_w_ada[0])
    ada_f = _adam_ada("adam_w_ada_final", cact, dmod_f_mine, w_ada_final, m_w_ada_final, v_w_ada_final)

    def leaves(k):
        return (ada[k][None], small["b_ada"][k], small["norm_g"][k], wt_leaves[k], small["ln_v_g"][k],
                small["ln_v_b"][k], small["w_spatial"][k], small["b_spatial"][k], small["sinks"][k], w_out_leaves[k],
                ada_f[k], small["b_ada_final"][k], small["final_norm_g"][k])

    return (loss[0, 0], grad_x[None], *leaves(0), *leaves(1), *leaves(2), *leaves(3))
```

```python
import jax
import jax.numpy as jnp
from jax import lax
from jax.experimental import pallas as pl
from jax.experimental.pallas import tpu as pltpu

D_MODEL = 2048
D_IN = 5632
D_A = 1024
CHUNK = 128
A_GROUPS = 8
HEAD_DIM = 64
N_KV_HEADS = 4
N_DEV = 8
ROPE_THETA = 10000.0
NORM_EPS = 1e-5
ATTN_SCALE = HEAD_DIM ** -0.5

ADAM_LR = 0.001
ADAM_B1 = 0.9
ADAM_B2 = 0.999
ADAM_EPS = 1e-08
ADAM_WD = 0.01
ADAM_STEP = 10

OFF_U, OFF_VA, OFF_ZA, OFF_Q, OFF_K, OFF_V, OFF_ZB = 0, 1024, 2048, 3072, 4096, 4352, 4608

SUM_SHIFT, SUM_SCALE, SUM_NORM_G, SUM_GATE, SUM_SHIFT_F, SUM_SCALE_F, SUM_FNG, SUM_SQ_ERR = range(8)

V7X_VMEM_LIMIT_BYTES = 56 * 1024 * 1024

F32 = jnp.float32
BF16 = jnp.bfloat16
MESH = pl.DeviceIdType.MESH
SDS = jax.ShapeDtypeStruct
NT = (((1,), (1,)), ((), ()))
TN = (((0,), (0,)), ((), ()))


def _params(*semantics):
    return pltpu.CompilerParams(dimension_semantics=semantics or None, vmem_limit_bytes=V7X_VMEM_LIMIT_BYTES)


def _mesh_pos():
    return lax.axis_index("x"), lax.axis_index("y"), lax.axis_index("c")


def _sigmoid(z):
    return 1.0 / (1.0 + jnp.exp(-z))


def _adamw(w, g, m, v):
    m = ADAM_B1 * m + (1.0 - ADAM_B1) * g
    v = ADAM_B2 * v + (1.0 - ADAM_B2) * (g * g)
    m_hat = m / (1.0 - ADAM_B1 ** ADAM_STEP)
    v_hat = v / (1.0 - ADAM_B2 ** ADAM_STEP)
    delta = -ADAM_LR * (m_hat / (jnp.sqrt(v_hat) + ADAM_EPS) + ADAM_WD * w)
    return delta, m, v


def _all_gather(name, blocks, memory_space):
    n_arr = len(blocks)

    def body(*refs):
        ins, outs = refs[:n_arr], refs[n_arr:2 * n_arr]
        send_sems, recv_sems, local_sems = refs[2 * n_arr:]
        x, y, c = _mesh_pos()
        me, sibling = (x, y, c), (x, y, 1 - c)
        chips = [(1 - x, y), (x, 1 - y), (1 - x, 1 - y)]

        def slot(p):
            return 4 * p[0] + 2 * p[1] + p[2]

        def copy(a, k, block, to, src=None):
            dst = outs[a].at[slot(block)]
            return pltpu.make_async_remote_copy(
                src_ref=dst if src is None else src, dst_ref=dst,
                send_sem=send_sems.at[a, k], recv_sem=recv_sems.at[a, k],
                device_id=to, device_id_type=MESH)

        mine = [pltpu.make_async_copy(ins[a], outs[a].at[slot(me)], local_sems.at[a]) for a in range(n_arr)]
        for cp in mine:
            cp.start()
        first = []
        for a in range(n_arr):
            first.append(copy(a, 0, me, sibling, src=ins[a]))
            first += [copy(a, 1 + j, me, (*chip, c), src=ins[a]) for j, chip in enumerate(chips)]
        for cp in first:
            cp.start()
        passed = []
        for j, chip in enumerate(chips):
            for a in range(n_arr):
                copy(a, 1 + j, (*chip, c), me).wait_recv()
                fwd = copy(a, 4 + j, (*chip, c), sibling)
                fwd.start()
                passed.append(fwd)
        for a in range(n_arr):
            copy(a, 0, sibling, me).wait_recv()
            for j, chip in enumerate(chips):
                copy(a, 4 + j, (*chip, 1 - c), me).wait_recv()
        for cp in first + passed:
            cp.wait_send()
        for cp in mine:
            cp.wait()

    spec = pl.BlockSpec(memory_space=memory_space)
    return pl.pallas_call(
        body, name=name,
        out_shape=[SDS((N_DEV,) + b.shape, b.dtype) for b in blocks],
        in_specs=[spec] * n_arr, out_specs=[spec] * n_arr,
        scratch_shapes=[pltpu.SemaphoreType.DMA((n_arr, 7)), pltpu.SemaphoreType.DMA((n_arr, 7)),
                        pltpu.SemaphoreType.DMA((n_arr,))],
        compiler_params=_params(),
    )(*blocks)


def _ada_exchange(c, w_ada, b_ada8, w_ada_f, b_ada_f8):
    n1, n2 = w_ada.shape[1], w_ada_f.shape[1]

    def body(c_ref, w1_ref, b1_ref, w2_ref, b2_ref, cact_ref, mod_ref, modf_ref,
             cact_buf, res1, res2, send1, send2, sems_s, sems_r):
        x, y, c_pos = _mesh_pos()
        me = 4 * x + 2 * y + c_pos
        flips = [(k >> 2 & 1, k >> 1 & 1, k & 1) for k in range(1, N_DEV)]

        def peer(f):
            return (1 - x if f[0] else x, 1 - y if f[1] else y, 1 - c_pos if f[2] else c_pos)

        cv = c_ref[...]
        cact = cv * _sigmoid(cv)
        cact_buf[...] = cact
        cact_ref[me] = cact

        def rdma(phase, k, src, dst, f):
            return pltpu.make_async_remote_copy(src_ref=src, dst_ref=dst, send_sem=sems_s.at[phase, k],
                                                recv_sem=sems_r.at[phase, k], device_id=peer(f), device_id_type=MESH)

        gather = [rdma(0, k, cact_buf, cact_ref.at[me], f) for k, f in enumerate(flips)]
        for cp in gather:
            cp.start()
        for cp in gather:
            cp.wait_recv()
        for cp in gather:
            cp.wait_send()

        rid = lax.broadcasted_iota(jnp.int32, (N_DEV, D_MODEL), 0)
        rows = jnp.zeros((N_DEV, D_MODEL), F32)
        for j in range(N_DEV):
            rows = jnp.where(rid == j, jnp.broadcast_to(cact_ref[j], (N_DEV, D_MODEL)), rows)
        rows = rows.astype(BF16)
        res1[...] = jnp.dot(rows, w1_ref[...].astype(BF16), preferred_element_type=F32) + b1_ref[pl.ds(me, 1), :]
        res2[...] = jnp.dot(rows, w2_ref[...].astype(BF16), preferred_element_type=F32) + b2_ref[pl.ds(me, 1), :]
        for j in range(N_DEV):
            send1[j] = res1[pl.ds(j, 1), :]
            send2[j] = res2[pl.ds(j, 1), :]
        mod_ref[me] = send1[me]
        modf_ref[me] = send2[me]
        scatter = []
        for k, f in enumerate(flips):
            to = me ^ (k + 1)
            scatter.append(rdma(1, k, send1.at[to], mod_ref.at[me], f))
            scatter.append(rdma(2, k, send2.at[to], modf_ref.at[me], f))
        for cp in scatter:
            cp.start()
        for cp in scatter:
            cp.wait_recv()
        for cp in scatter:
            cp.wait_send()

    vmem = pl.BlockSpec(memory_space=pltpu.VMEM)
    return pl.pallas_call(
        body, name="ada_exchange",
        out_shape=[SDS((N_DEV, 1, D_MODEL), F32), SDS((N_DEV, 1, n1), F32), SDS((N_DEV, 1, n2), F32)],
        in_specs=[vmem] * 5, out_specs=[vmem] * 3,
        scratch_shapes=[pltpu.VMEM((1, D_MODEL), F32), pltpu.VMEM((N_DEV, n1), F32), pltpu.VMEM((N_DEV, n2), F32),
                        pltpu.VMEM((N_DEV, 1, n1), F32), pltpu.VMEM((N_DEV, 1, n2), F32),
                        pltpu.SemaphoreType.DMA((3, 7)), pltpu.SemaphoreType.DMA((3, 7))],
        compiler_params=_params(),
    )(c, w_ada, b_ada8, w_ada_f, b_ada_f8)


def _chip_scatter(pair_ref, parts_ref, send_sems, recv_sems):
    x, y, c = _mesh_pos()
    chips = [(1 - x, y), (x, 1 - y), (1 - x, 1 - y)]
    return [pltpu.make_async_remote_copy(
        src_ref=pair_ref.at[2 * cx + cy], dst_ref=parts_ref.at[j], send_sem=send_sems.at[j], recv_sem=recv_sems.at[j],
        device_id=(cx, cy, c), device_id_type=MESH) for j, (cx, cy) in enumerate(chips)]


def _scatter_scratch():
    return [pltpu.SemaphoreType.DMA((3,)), pltpu.SemaphoreType.DMA((3,))]


def _prep_weights(me, wt, w_out):
    steps = 4

    def body(me_ref, wt_ref, wo_ref, wtb_ref, wob_ref):
        wtb_ref[...] = wt_ref[...].astype(BF16)
        wob_ref[...] = wo_ref[...].astype(BF16)

    def rows(a, mine):
        blk = (a.shape[0] // steps, a.shape[1])
        return pl.BlockSpec(blk, (lambda i, me_ref: (steps * me_ref[0] + i, 0)) if mine else (lambda i, me_ref: (i, 0)))

    return pl.pallas_call(
        body, name="prep_weights",
        grid_spec=pltpu.PrefetchScalarGridSpec(
            num_scalar_prefetch=1, grid=(steps,),
            in_specs=[rows(wt, False), rows(w_out, False)], out_specs=[rows(wt, True), rows(w_out, True)]),
        out_shape=[SDS((N_DEV * wt.shape[0], D_MODEL), BF16), SDS((N_DEV * w_out.shape[0], D_MODEL), BF16)],
        compiler_params=_params("parallel"),
    )(me, wt, w_out)


class _InPlaceGather:
    def __init__(self, buf_ref, send_sems, recv_sems, relay=False):
        self.buf, self.send_sems, self.recv_sems, self.relay = buf_ref, send_sems, recv_sems, relay
        self.n = buf_ref.shape[0] // N_DEV
        x, y, c = _mesh_pos()
        self.me, self.sibling, self.core = (x, y, c), (x, y, 1 - c), c
        self.chips = [(1 - x, y), (x, 1 - y), (1 - x, 1 - y)]
        self.relay_from = (jnp.where(c == 0, 1 - x, x), jnp.where(c == 0, y, 1 - y), c)
        self.relay_to = (jnp.where(c == 0, x, 1 - x), jnp.where(c == 0, 1 - y, y), c)

    def copy(self, k, block, to):
        start = pl.multiple_of((4 * block[0] + 2 * block[1] + block[2]) * self.n, self.n)
        rows = self.buf.at[pl.ds(start, self.n)]
        return pltpu.make_async_remote_copy(src_ref=rows, dst_ref=rows, send_sem=self.send_sems.at[k],
                                            recv_sem=self.recv_sems.at[k], device_id=to, device_id_type=MESH)

    def start(self):
        self.copy(0, self.me, self.sibling).start()
        for j, chip in enumerate(self.chips[:2] if self.relay else self.chips):
            self.copy(1 + j, self.me, (*chip, self.core)).start()

    def relay_diagonal(self):
        self.copy(3, self.relay_from, self.relay_to).start()

    def pass_on(self, j):
        self.copy(1 + j, (*self.chips[j], self.core), self.me).wait_recv()
        self.copy(4 + j, (*self.chips[j], self.core), self.sibling).start()

    def wait_sibling(self, k):
        self.copy(k, self.sibling, self.me).wait_recv()

    def wait_sends(self):
        for k in range(7):
            self.copy(k, self.me, self.sibling).wait_send()


def _gather_scratch():
    return [pltpu.SemaphoreType.DMA((7,)), pltpu.SemaphoreType.DMA((7,))]


def _gather_in_proj(order, x, shift, scale, norm_g, wt_all):
    s = x.shape[0]
    th = tm = min(512, s)
    nh, ni = s // th, s // tm
    tn = D_IN // 4
    steps = nh + 4 * ni

    def body(order_ref, x_ref, shift_ref, scale_ref, g_ref, wt_in, h_ref, proj_ref, wt_ref,
             h_scr, w_buf, load_sems, send_sems, recv_sems):
        g = pl.program_id(0)
        gather = _InPlaceGather(wt_ref, send_sems, recv_sems, relay=True)

        def tile_load(slot, chip):
            return pltpu.make_async_copy(wt_ref.at[pl.ds(pl.multiple_of(chip * tn, tn), tn)], w_buf.at[slot],
                                         load_sems.at[slot])

        @pl.when(g == 0)
        def _():
            gather.start()

        @pl.when(g < nh)
        def _():
            xv = x_ref[...]
            r = lax.rsqrt(jnp.mean(xv * xv, axis=-1, keepdims=True) + NORM_EPS)
            hb = (((xv * r) * g_ref[...]) * (1.0 + scale_ref[...]) + shift_ref[...]).astype(BF16)
            h_ref[...] = hb
            h_scr[pl.ds(pl.multiple_of(g * th, th), th), :] = hb

        @pl.when(g == nh - 1)
        def _():
            gather.wait_sibling(0)
            tile_load(0, order_ref[0]).start()

        @pl.when(g >= nh)
        def _():
            t, i = (g - nh) // ni, (g - nh) % ni

            @pl.when(i == 0)
            def _():
                tile_load(t % 2, order_ref[t]).wait()

            @pl.when((i == ni - 1) & (t == 0))
            def _():
                gather.pass_on(0)
                gather.pass_on(1)
                gather.relay_diagonal()

            @pl.when((i == ni // 2) & (t == 2))
            def _():
                gather.pass_on(2)

            for j in range(3):
                @pl.when((i == ni - 1) & (t == j))
                def _():
                    gather.wait_sibling(4 + j)
                    tile_load((j + 1) % 2, order_ref[j + 1]).start()

            lhs = h_scr[pl.ds(pl.multiple_of(i * tm, tm), tm), :]
            proj_ref[...] = lax.dot_general(lhs, w_buf[t % 2], NT, preferred_element_type=F32).astype(BF16)

        @pl.when(g == steps - 1)
        def _():
            gather.wait_sends()

    def h_tile(g, order_ref):
        return (jnp.minimum(g, nh - 1), 0)

    def proj_tile(g, order_ref):
        mm = jnp.maximum(g - nh, 0)
        return (mm % ni, order_ref[mm // ni])

    row = pl.BlockSpec((1, D_MODEL), lambda g, order_ref: (0, 0))
    hbm = pl.BlockSpec(memory_space=pl.ANY)
    return pl.pallas_call(
        body, name="gather_in_proj",
        grid_spec=pltpu.PrefetchScalarGridSpec(
            num_scalar_prefetch=1, grid=(steps,),
            in_specs=[pl.BlockSpec((th, D_MODEL), h_tile), row, row, row, hbm],
            out_specs=[pl.BlockSpec((th, D_MODEL), h_tile), pl.BlockSpec((tm, tn), proj_tile), hbm],
            scratch_shapes=[pltpu.VMEM((s, D_MODEL), BF16), pltpu.VMEM((2, tn, D_MODEL), BF16),
                            pltpu.SemaphoreType.DMA((2,)), *_gather_scratch()]),
        out_shape=[SDS((s, D_MODEL), BF16), SDS((s, D_IN), BF16), SDS(wt_all.shape, BF16)],
        input_output_aliases={5: 2},
        compiler_params=_params("arbitrary"),
    )(order, x, shift, scale, norm_g, wt_all)


def _rope_tables(seq):
    inv_freq = ROPE_THETA ** (-jnp.arange(0, HEAD_DIM, 2, dtype=F32) / HEAD_DIM)
    ang = jnp.arange(seq, dtype=F32)[:, None] * inv_freq[None, :]
    cos, sin, zero = jnp.cos(ang), jnp.sin(ang), jnp.zeros_like(ang)
    return (jnp.concatenate([cos] * 4, axis=1), jnp.concatenate([-sin, zero, -sin, zero], axis=1),
            jnp.concatenate([zero, sin, zero, sin], axis=1))


def _rope(v, cos, sin_lo, sin_hi):
    width = v.shape[1]
    rep = (1, width // 128)
    return (v * jnp.tile(cos, rep) + pltpu.roll(v, width - 32, 1) * jnp.tile(sin_lo, rep)
            + pltpu.roll(v, 32, 1) * jnp.tile(sin_hi, rep))


def _rope_bwd(d, cos, sin_lo, sin_hi):
    width = d.shape[1]
    rep = (1, width // 128)
    return (d * jnp.tile(cos, rep) + pltpu.roll(d * jnp.tile(sin_lo, rep), 32, 1)
            + pltpu.roll(d * jnp.tile(sin_hi, rep), width - 32, 1))


def _layer_norm(v, g, b):
    mu = jnp.mean(v, axis=-1, keepdims=True)
    vc = v - mu
    rstd = lax.rsqrt(jnp.mean(vc * vc, axis=-1, keepdims=True) + NORM_EPS)
    vhat = vc * rstd
    return vhat * g + b, vhat, rstd


def _tril_bf16(w_ref, g):
    t = lax.broadcasted_iota(jnp.int32, (CHUNK, CHUNK), 0)
    tp = lax.broadcasted_iota(jnp.int32, (CHUNK, CHUNK), 1)
    return jnp.where(tp <= t, w_ref[g], 0.0).astype(BF16)


def _bias_columns(b_ref, out_ref):
    for g in range(A_GROUPS):
        out_ref[g] = jnp.broadcast_to(b_ref[pl.ds(g, 1), :], (CHUNK, CHUNK)).T


def _band_mask():
    kj = lax.broadcasted_iota(jnp.int32, (2 * CHUNK, 4 * CHUNK), 0)
    qi = lax.broadcasted_iota(jnp.int32, (2 * CHUNK, 4 * CHUNK), 1) & (CHUNK - 1)
    rel = qi + CHUNK - kj
    return jnp.where((rel >= 0) & (rel < CHUNK), 0.0, -jnp.inf)


def _low_lanes():
    return lax.broadcasted_iota(jnp.int32, (1, 128), 1) < HEAD_DIM


def _stack_heads(pair_a, pair_b):
    lo = _low_lanes()
    return jnp.concatenate([jnp.where(lo, pair_a, 0.0), jnp.where(lo, 0.0, pair_a),
                            jnp.where(lo, pair_b, 0.0), jnp.where(lo, 0.0, pair_b)], axis=0).astype(BF16)


def _heads_to_lanes(per_group):
    rows = [t[:, r * CHUNK:(r + 1) * CHUNK] for t in per_group for r in range(4)]
    return jnp.concatenate(rows, axis=0).T


def _dup_kv_head(band, gk):
    pair = band[:, (gk // 2) * 128:(gk // 2 + 1) * 128]
    lo = _low_lanes()
    one = jnp.where(lo if gk % 2 == 0 else jnp.logical_not(lo), pair, 0.0)
    return (one + pltpu.roll(one, HEAD_DIM, 1)).astype(BF16)


def _fold_kv_head(dup_grad, gk):
    both = dup_grad + pltpu.roll(dup_grad, HEAD_DIM, 1)
    lo = _low_lanes()
    return jnp.where(lo if gk % 2 == 0 else jnp.logical_not(lo), both, 0.0)


def _attn_probs(q_st, k_dup, sink_row, mask, first_block):
    s = lax.dot_general(k_dup, q_st, NT, preferred_element_type=F32) + mask
    s = jnp.concatenate([jnp.where(first_block, -jnp.inf, s[:CHUNK]), s[CHUNK:]], axis=0)
    m = jnp.maximum(jnp.max(s, axis=0, keepdims=True), sink_row)
    p = jnp.exp(s - m)
    e_sink = jnp.exp(sink_row - m)
    inv = 1.0 / (jnp.sum(p, axis=0, keepdims=True) + e_sink)
    return p * inv, e_sink * inv


def _sink_row(sinks_ref, gk):
    return jnp.concatenate([jnp.full((1, CHUNK), sinks_ref[4 * gk + r], F32) for r in range(4)], axis=1)


def _mixer_specs(nb, rev):
    def blk(i):
        return nb - 1 - i if rev else i

    def prev(i):
        return jnp.maximum(blk(i) - 1, 0)

    tab = pl.BlockSpec((CHUNK, 128), lambda i, *_: (blk(i), 0))
    tab_prev = pl.BlockSpec((CHUNK, 128), lambda i, *_: (prev(i), 0))
    return dict(
        cur=pl.BlockSpec((CHUNK, D_IN), lambda i, *_: (blk(i), 0)),
        prev_kv=pl.BlockSpec((CHUNK, 2 * 256), lambda i, *_: (prev(i), OFF_K // 512)),
        tabs=[tab] * 3 + [tab_prev] * 3,
        vec=pl.BlockSpec((1, D_A), lambda i, *_: (0, 0)),
        wsp=pl.BlockSpec((A_GROUPS, CHUNK, CHUNK), lambda i, *_: (0, 0, 0)),
        bsp=pl.BlockSpec((A_GROUPS, CHUNK), lambda i, *_: (0, 0)),
        smem=pl.BlockSpec(memory_space=pltpu.SMEM),
        blk=blk,
    )


def _mixer_fwd(proj, tabs, ln_g, ln_b, w_sp, b_sp, sinks, wo_all):
    s = proj.shape[0]
    nb = s // CHUNK
    sp = _mixer_specs(nb, rev=False)

    def body(cur_ref, pkv_ref, c_ref, s1_ref, s2_ref, cp_ref, s1p_ref, s2p_ref, lg_ref, lb_ref, w_ref, b_ref,
             sinks_ref, wo_in, y_ref, wo_ref, bcol, mask, send_sems, recv_sems):
        i = pl.program_id(0)
        gather = _InPlaceGather(wo_ref, send_sems, recv_sems)

        @pl.when(i == 0)
        def _():
            gather.start()
            _bias_columns(b_ref, bcol)
            mask[...] = _band_mask()

        @pl.when(i == nb // 2)
        def _():
            for j in range(3):
                gather.pass_on(j)

        vln, _, _ = _layer_norm(cur_ref[:, OFF_VA:OFF_ZA].astype(F32), lg_ref[...], lb_ref[...])
        vln = vln.astype(BF16)
        for g in range(A_GROUPS):
            cols = slice(g * 128, (g + 1) * 128)
            sg = jnp.dot(_tril_bf16(w_ref, g), vln[:, cols], preferred_element_type=F32) + bcol[g]
            u = cur_ref[:, OFF_U + g * 128:OFF_U + (g + 1) * 128].astype(F32)
            z = cur_ref[:, OFF_ZA + g * 128:OFF_ZA + (g + 1) * 128].astype(F32)
            y_ref[:, cols] = (u * sg * (z * _sigmoid(z))).astype(BF16)

        cur_t = (c_ref[...], s1_ref[...], s2_ref[...])
        prev_t = (cp_ref[...], s1p_ref[...], s2p_ref[...])
        qr = _rope(cur_ref[:, OFF_Q:OFF_K].astype(F32), *cur_t) * ATTN_SCALE
        kr = jnp.concatenate([_rope(pkv_ref[:, 0:256].astype(F32), *prev_t),
                              _rope(cur_ref[:, OFF_K:OFF_V].astype(F32), *cur_t)], axis=0)
        v_t = jnp.concatenate([pkv_ref[:, 256:512], cur_ref[:, OFF_V:OFF_ZB]], axis=0).astype(F32).T.astype(BF16)
        outs = []
        for gk in range(N_KV_HEADS):
            q_st = _stack_heads(qr[:, (2 * gk) * 128:(2 * gk + 1) * 128], qr[:, (2 * gk + 1) * 128:(2 * gk + 2) * 128])
            probs, _ = _attn_probs(q_st, _dup_kv_head(kr, gk), _sink_row(sinks_ref, gk), mask[...], i == 0)
            outs.append(jnp.dot(v_t[gk * HEAD_DIM:(gk + 1) * HEAD_DIM], probs.astype(BF16),
                                preferred_element_type=F32))
        zb = cur_ref[:, OFF_ZB:D_IN].astype(F32)
        y_ref[:, D_A:D_MODEL] = (_heads_to_lanes(outs) * (zb * _sigmoid(zb))).astype(BF16)

        @pl.when(i == nb - 1)
        def _():
            gather.wait_sibling(0)
            for j in range(3):
                gather.wait_sibling(4 + j)
            gather.wait_sends()

    hbm = pl.BlockSpec(memory_space=pl.ANY)
    return pl.pallas_call(
        body, name="mixer_fwd", grid=(nb,),
        in_specs=[sp["cur"], sp["prev_kv"], *sp["tabs"], sp["vec"], sp["vec"], sp["wsp"], sp["bsp"], sp["smem"], hbm],
        out_specs=[pl.BlockSpec((CHUNK, D_MODEL), lambda i: (i, 0)), hbm],
        out_shape=[SDS((s, D_MODEL), BF16), SDS(wo_all.shape, wo_all.dtype)],
        scratch_shapes=[pltpu.VMEM((A_GROUPS, CHUNK, CHUNK), F32), pltpu.VMEM((2 * CHUNK, 4 * CHUNK), F32),
                        *_gather_scratch()],
        input_output_aliases={13: 1},
        compiler_params=_params("arbitrary"),
    )(proj, proj, *tabs, *tabs, ln_g, ln_b, w_sp, b_sp, sinks, wo_all)


def _out_proj_loss(y, x, target, wo, gate, shift_f, scale_f, fng):
    s = y.shape[0]
    tm, tr = 256, 128
    nt = s // tm

    def body(y_ref, x_ref, t_ref, wo_ref, gate_ref, sh_ref, sc_ref, g_ref, dx1_ref, do_ref, dy_ref, sums_ref,
             do_last, do_work):
        i = pl.program_id(0)

        @pl.when(i == 0)
        def _():
            sums_ref[...] = jnp.zeros_like(sums_ref)
            do_last[...] = jnp.zeros_like(do_last)

        do_work[...] = do_last[...]
        o = jnp.dot(y_ref[...], wo_ref[...], preferred_element_type=F32)
        gate, g, sh = gate_ref[...], g_ref[...], sh_ref[...]
        one_sc = 1.0 + sc_ref[...]
        cs, inv_d = g * one_sc, 1.0 / D_MODEL

        def rowsum(v):
            return jnp.sum(v, axis=0, keepdims=True)

        sums = [jnp.zeros((1, D_MODEL), F32) for _ in range(4)]
        for c in range(tm // tr):
            rows = slice(c * tr, (c + 1) * tr)
            oc = o[rows]
            x1 = x_ref[rows, :] + gate * oc
            r = lax.rsqrt(jnp.sum(x1 * x1, axis=-1, keepdims=True) * inv_d + NORM_EPS)
            x1n = x1 * r
            diff = x1n * cs + sh - t_ref[rows, :]
            w = diff * x1n
            lane_sum = jnp.sum(w * cs, axis=-1, keepdims=True)
            dx1 = (diff * cs) * (r * inv_d) - x1n * (r * lane_sum * (inv_d * inv_d))
            dx1_ref[rows, :] = dx1
            do = (dx1 * gate).astype(BF16)
            do_ref[rows, :] = do
            do_last[rows, :] = do
            for k, v in enumerate((dx1 * oc, diff, w, diff * diff)):
                sums[k] = sums[k] + rowsum(v)
        live = jnp.where(i < nt, 1.0, 0.0)
        for row, v in ((SUM_GATE, sums[0]), (SUM_SHIFT_F, inv_d * sums[1]), (SUM_SCALE_F, inv_d * (sums[2] * g)),
                       (SUM_FNG, inv_d * (sums[2] * one_sc)), (SUM_SQ_ERR, sums[3])):
            sums_ref[row:row + 1, :] += live * v
        dy_ref[...] = lax.dot_general(do_work[...], wo_ref[...], NT, preferred_element_type=F32).astype(BF16)

    tile = pl.BlockSpec((tm, D_MODEL), lambda i: (jnp.minimum(i, nt - 1), 0))
    row = pl.BlockSpec((1, D_MODEL), lambda i: (0, 0))
    return pl.pallas_call(
        body, name="out_proj_loss", grid=(nt + 1,),
        in_specs=[tile, tile, tile, pl.BlockSpec((D_MODEL, D_MODEL), lambda i: (0, 0)), row, row, row, row],
        out_specs=[tile, tile, pl.BlockSpec((tm, D_MODEL), lambda i: (jnp.maximum(i - 1, 0), 0)),
                   pl.BlockSpec((8, D_MODEL), lambda i: (0, 0))],
        out_shape=[SDS((s, D_MODEL), F32), SDS((s, D_MODEL), BF16), SDS((s, D_MODEL), BF16), SDS((8, D_MODEL), F32)],
        scratch_shapes=[pltpu.VMEM((tm, D_MODEL), BF16), pltpu.VMEM((tm, D_MODEL), BF16)],
        compiler_params=_params("arbitrary"),
    )(y, x, target, wo, gate, shift_f, scale_f, fng)


ROW_DBSP, ROW_DSINKS, MISC_ROWS = 0, 8, 32


def _mixer_bwd(me, proj, dy, tabs, ln_g, ln_b, w_sp, b_sp, sinks, pair):
    s = proj.shape[0]
    nb = s // CHUNK
    sp = _mixer_specs(nb, rev=True)

    def body(me_ref, cur_ref, pkv_ref, dy_ref, c_ref, s1_ref, s2_ref, cp_ref, s1p_ref, s2p_ref, lg_ref, lb_ref, w_ref,
             b_ref, sinks_ref, pair_ref, dproj_ref, dln_ref, dw_ref, misc_ref, parts_ref, bcol, dbcol, carry, mask,
             send_sems, recv_sems):
        i = pl.program_id(0)
        block = nb - 1 - i

        @pl.when(i == 0)
        def _():
            for cp in _chip_scatter(pair_ref, parts_ref, send_sems, recv_sems):
                cp.start()
            _bias_columns(b_ref, bcol)
            mask[...] = _band_mask()
            dbcol[...] = jnp.zeros_like(dbcol)
            carry[...] = jnp.zeros_like(carry)
            dln_ref[...] = jnp.zeros_like(dln_ref)
            dw_ref[...] = jnp.zeros_like(dw_ref)
            misc_ref[...] = jnp.zeros_like(misc_ref)

        vln, vhat, rstd = _layer_norm(cur_ref[:, OFF_VA:OFF_ZA].astype(F32), lg_ref[...], lb_ref[...])
        vln = vln.astype(BF16)
        d_vln = []
        for g in range(A_GROUPS):
            cols = slice(g * 128, (g + 1) * 128)
            w_g = _tril_bf16(w_ref, g)
            sg = jnp.dot(w_g, vln[:, cols], preferred_element_type=F32) + bcol[g]
            u = cur_ref[:, OFF_U + g * 128:OFF_U + (g + 1) * 128].astype(F32)
            z = cur_ref[:, OFF_ZA + g * 128:OFF_ZA + (g + 1) * 128].astype(F32)
            dya = dy_ref[:, cols].astype(F32)
            sig = _sigmoid(z)
            d_ya = dya * (z * sig)
            dproj_ref[:, OFF_ZA + g * 128:OFF_ZA + (g + 1) * 128] = (
                dya * (u * sg) * (sig * (1.0 + z * (1.0 - sig)))).astype(BF16)
            dproj_ref[:, OFF_U + g * 128:OFF_U + (g + 1) * 128] = (d_ya * sg).astype(BF16)
            d_s = d_ya * u
            dbcol[g] += d_s
            d_sb = d_s.astype(BF16)
            dw_ref[g] += lax.dot_general(d_sb, vln[:, cols], NT, preferred_element_type=F32)
            d_vln.append(lax.dot_general(w_g, d_sb, TN, preferred_element_type=F32))
        d_vln = jnp.concatenate(d_vln, axis=1)
        dln_ref[0:1, :] += jnp.sum(d_vln * vhat, axis=0, keepdims=True)
        dln_ref[1:2, :] += jnp.sum(d_vln, axis=0, keepdims=True)
        d_vhat = d_vln * lg_ref[...]
        d_va = rstd * (d_vhat - jnp.mean(d_vhat, axis=-1, keepdims=True)
                       - vhat * jnp.mean(d_vhat * vhat, axis=-1, keepdims=True))
        dproj_ref[:, OFF_VA:OFF_ZA] = d_va.astype(BF16)

        cur_t = (c_ref[...], s1_ref[...], s2_ref[...])
        prev_t = (cp_ref[...], s1p_ref[...], s2p_ref[...])
        band_t = tuple(jnp.concatenate([p, c], axis=0) for p, c in zip(prev_t, cur_t))
        qr = _rope(cur_ref[:, OFF_Q:OFF_K].astype(F32), *cur_t) * ATTN_SCALE
        kr = jnp.concatenate([_rope(pkv_ref[:, 0:256].astype(F32), *prev_t),
                              _rope(cur_ref[:, OFF_K:OFF_V].astype(F32), *cur_t)], axis=0)
        vb = jnp.concatenate([pkv_ref[:, 256:512], cur_ref[:, OFF_V:OFF_ZB]], axis=0).astype(F32)
        k_t, v_t = (kr.T * ATTN_SCALE).astype(BF16), vb.T.astype(BF16)
        zb = cur_ref[:, OFF_ZB:D_IN].astype(F32)
        dyb = dy_ref[:, D_A:D_MODEL].astype(F32)
        sig = _sigmoid(zb)
        d_yb = dyb * (zb * sig)
        outs, dqs = [], []
        dk_pairs = [jnp.zeros((2 * CHUNK, 128), F32) for _ in range(2)]
        dv_pairs = [jnp.zeros((2 * CHUNK, 128), F32) for _ in range(2)]
        for gk in range(N_KV_HEADS):
            heads = slice(gk * HEAD_DIM, (gk + 1) * HEAD_DIM)
            q_st = _stack_heads(qr[:, (2 * gk) * 128:(2 * gk + 1) * 128], qr[:, (2 * gk + 1) * 128:(2 * gk + 2) * 128])
            k_dup, v_dup = _dup_kv_head(kr, gk), _dup_kv_head(vb, gk)
            probs, p_sink = _attn_probs(q_st, k_dup, _sink_row(sinks_ref, gk), mask[...], block == 0)
            probs_b = probs.astype(BF16)
            outs.append(jnp.dot(v_t[heads], probs_b, preferred_element_type=F32))
            do_st = _stack_heads(d_yb[:, (2 * gk) * 128:(2 * gk + 1) * 128], d_yb[:, (2 * gk + 1) * 128:(2 * gk + 2) * 128])
            dp = lax.dot_general(v_dup, do_st, NT, preferred_element_type=F32)
            delta = jnp.sum(probs * dp, axis=0, keepdims=True)
            ds = (probs * (dp - delta)).astype(BF16)
            d_sink = -p_sink * delta
            for r in range(4):
                row = ROW_DSINKS + 4 * gk + r
                misc_ref[row:row + 1, :] += jnp.broadcast_to(
                    jnp.sum(d_sink[:, r * CHUNK:(r + 1) * CHUNK], axis=1, keepdims=True), (1, 128))
            dqs.append(jnp.dot(k_t[heads], ds, preferred_element_type=F32))
            dk_pairs[gk // 2] += _fold_kv_head(jnp.dot(ds, q_st, preferred_element_type=F32), gk)
            dv_pairs[gk // 2] += _fold_kv_head(jnp.dot(probs_b, do_st, preferred_element_type=F32), gk)
        dproj_ref[:, OFF_ZB:D_IN] = (dyb * _heads_to_lanes(outs) * (sig * (1.0 + zb * (1.0 - sig)))).astype(BF16)
        dproj_ref[:, OFF_Q:OFF_K] = _rope_bwd(_heads_to_lanes(dqs), *cur_t).astype(BF16)
        dk_band = _rope_bwd(jnp.concatenate(dk_pairs, axis=1), *band_t)
        dv_band = jnp.concatenate(dv_pairs, axis=1)
        dproj_ref[:, OFF_K:OFF_V] = (dk_band[CHUNK:] + carry[:, 0:256]).astype(BF16)
        dproj_ref[:, OFF_V:OFF_ZB] = (dv_band[CHUNK:] + carry[:, 256:512]).astype(BF16)
        carry[:, 0:256] = dk_band[:CHUNK]
        carry[:, 256:512] = dv_band[:CHUNK]

        @pl.when(i == nb - 1)
        def _():
            t = lax.broadcasted_iota(jnp.int32, (CHUNK, CHUNK), 0)
            tp = lax.broadcasted_iota(jnp.int32, (CHUNK, CHUNK), 1)
            for g in range(A_GROUPS):
                dw_ref[g] = jnp.where(tp <= t, dw_ref[g], 0.0)
                misc_ref[pl.ds(ROW_DBSP + g, 1), :] = jnp.sum(dbcol[g].T, axis=0, keepdims=True)
            scatter = _chip_scatter(pair_ref, parts_ref, send_sems, recv_sems)
            for cp in scatter:
                cp.wait_recv()
            for cp in scatter:
                cp.wait_send()

    blk = sp["blk"]
    hbm = pl.BlockSpec(memory_space=pl.ANY)
    return pl.pallas_call(
        body, name="mixer_bwd",
        grid_spec=pltpu.PrefetchScalarGridSpec(
            num_scalar_prefetch=1, grid=(nb,),
            in_specs=[sp["cur"], sp["prev_kv"], pl.BlockSpec((CHUNK, D_MODEL), lambda i, me_ref: (blk(i), 0)),
                      *sp["tabs"], sp["vec"], sp["vec"], sp["wsp"], sp["bsp"], sp["smem"], hbm],
            out_specs=[pl.BlockSpec((CHUNK, D_IN), lambda i, me_ref: (blk(i), 0)),
                       pl.BlockSpec((8, D_A), lambda i, me_ref: (me_ref[0], 0)),
                       pl.BlockSpec((A_GROUPS, CHUNK, CHUNK), lambda i, me_ref: (me_ref[0], 0, 0)),
                       pl.BlockSpec((MISC_ROWS, 128), lambda i, me_ref: (me_ref[0], 0)), hbm],
            scratch_shapes=[pltpu.VMEM((A_GROUPS, CHUNK, CHUNK), F32), pltpu.VMEM((A_GROUPS, CHUNK, CHUNK), F32),
                            pltpu.VMEM((CHUNK, 512), F32), pltpu.VMEM((2 * CHUNK, 4 * CHUNK), F32),
                            *_scatter_scratch()]),
        out_shape=[SDS((s, D_IN), BF16), SDS((N_DEV * 8, D_A), F32), SDS((N_DEV * A_GROUPS, CHUNK, CHUNK), F32),
                   SDS((N_DEV * MISC_ROWS, 128), F32), SDS((3,) + pair.shape[1:], pair.dtype)],
        compiler_params=_params("arbitrary"),
    )(me, proj, proj, dy, *tabs, *tabs, ln_g, ln_b, w_sp, b_sp, sinks, pair)


def _wgrad_pair(name, a, b, gathers=()):
    s, m = a.shape
    n = b.shape[1]
    bm, half = m // 4, m // 8
    bt = min(1024, s)
    steps = s // bt
    last = 4 * steps
    n_g = len(gathers)

    def body(*refs):
        a_ref, b_ref = refs[:2]
        out_ref, bufs = refs[2 + n_g], refs[3 + n_g:3 + 2 * n_g]
        acc, kept, got, sent, send_sems, recv_sems = refs[3 + 2 * n_g:9 + 2 * n_g]
        sems = refs[9 + 2 * n_g:]
        g = pl.program_id(0)
        tile, t = g // steps, g % steps
        mx, my, mc = _mesh_pos()
        jobs = [_InPlaceGather(bufs[k], sems[2 * k], sems[2 * k + 1]) for k in range(n_g)]

        def exchange(q):
            return pltpu.make_async_remote_copy(src_ref=sent, dst_ref=got.at[q % 2], send_sem=send_sems.at[q],
                                                recv_sem=recv_sems.at[q], device_id=(mx, my, 1 - mc),
                                                device_id_type=MESH)

        @pl.when(g == 0)
        def _():
            for job in jobs:
                job.start()

        @pl.when(g == 2 * steps)
        def _():
            for job in jobs:
                for j in range(3):
                    job.pass_on(j)

        @pl.when(g < last)
        def _():
            prod = lax.dot_general(a_ref[...], b_ref[...], TN, preferred_element_type=F32)

            @pl.when(t == 0)
            def _():
                acc[...] = prod

            @pl.when(t > 0)
            def _():
                acc[...] += prod

            @pl.when(t == steps - 1)
            def _():
                @pl.when(tile > 0)
                def _():
                    exchange(tile - 1).wait_send()

                kept[tile % 2] = acc[pl.ds(pl.multiple_of(mc * half, 8), half), :].astype(BF16)
                sent[...] = acc[pl.ds(pl.multiple_of((1 - mc) * half, 8), half), :].astype(BF16)
                exchange(tile).start()

        @pl.when((t == 0) & (g > 0))
        def _():
            q = tile - 1
            exchange(q).wait_recv()
            out_ref[0] = (kept[q % 2].astype(F32) + got[q % 2].astype(F32)).astype(BF16)

        @pl.when(g == last)
        def _():
            exchange(3).wait_send()
            for job in jobs:
                job.wait_sibling(0)
                for j in range(3):
                    job.wait_sibling(4 + j)
                job.wait_sends()

    def a_tile(g):
        gg = jnp.minimum(g, last - 1)
        return (gg % steps, gg // steps)

    def b_tile(g):
        return (jnp.minimum(g, last - 1) % steps, 0)

    hbm = pl.BlockSpec(memory_space=pl.ANY)
    outs = pl.pallas_call(
        body, name=name, grid=(last + 1,),
        in_specs=[pl.BlockSpec((bt, bm), a_tile), pl.BlockSpec((bt, n), b_tile)] + [hbm] * n_g,
        out_specs=[pl.BlockSpec((1, half, n), lambda g: (jnp.maximum(g - 1, 0) // steps, 0, 0))] + [hbm] * n_g,
        out_shape=[SDS((4, half, n), BF16)] + [SDS(gb.shape, gb.dtype) for gb in gathers],
        scratch_shapes=[pltpu.VMEM((bm, n), F32), pltpu.VMEM((2, half, n), BF16), pltpu.VMEM((2, half, n), BF16),
                        pltpu.VMEM((half, n), BF16), pltpu.SemaphoreType.DMA((4,)), pltpu.SemaphoreType.DMA((4,))]
        + _gather_scratch() * n_g,
        input_output_aliases={2 + k: 1 + k for k in range(n_g)},
        compiler_params=_params("arbitrary"),
    )(a, b, *gathers)
    return outs[0], outs[1:]


def _in_proj_bwd(dproj, wt, x, dx1, scale, norm_g, sums_o, pair):
    s = x.shape[0]
    tm, tk, tr = min(1024, s), D_IN // 4, 64
    ksteps = D_IN // tk

    def body(dp_ref, wt_ref, x_hbm, dx1_hbm, sc_ref, g_ref, so_ref, pair_ref, gx_ref, sums_ref, parts_ref, x_buf,
             dx1_buf, tile_sems, send_sems, recv_sems):
        i, k = pl.program_id(0), pl.program_id(1)

        def tile_copies():
            rows = pl.ds(pl.multiple_of(i * tm, tm), tm)
            return (pltpu.make_async_copy(x_hbm.at[rows], x_buf, tile_sems.at[0]),
                    pltpu.make_async_copy(dx1_hbm.at[rows], dx1_buf, tile_sems.at[1]))

        @pl.when((i == 0) & (k == 0))
        def _():
            for cp in _chip_scatter(pair_ref, parts_ref, send_sems, recv_sems):
                cp.start()
            sums_ref[...] = so_ref[...]

        @pl.when(k == 0)
        def _():
            for cp in tile_copies():
                cp.start()
            gx_ref[...] = jnp.dot(dp_ref[...], wt_ref[...], preferred_element_type=F32)

        @pl.when(k > 0)
        def _():
            gx_ref[...] += jnp.dot(dp_ref[...], wt_ref[...], preferred_element_type=F32)

        @pl.when(k == ksteps - 1)
        def _():
            for cp in tile_copies():
                cp.wait()
            one_sc, g = 1.0 + sc_ref[...], g_ref[...]
            cs = one_sc * g

            def chunk(j, sums):
                rows = pl.ds(pl.multiple_of(j * tr, tr), tr)
                dh, xv = gx_ref[rows, :], x_buf[rows, :]
                dhx = dh * xv
                r = lax.rsqrt(jnp.sum(xv * xv, axis=-1, keepdims=True) * (1.0 / D_MODEL) + NORM_EPS)
                coef = (r * r * r) * (jnp.sum(dhx * cs, axis=-1, keepdims=True) * (1.0 / D_MODEL))
                gx_ref[rows, :] = dx1_buf[rows, :] + r * (dh * cs) - xv * coef
                return (sums[0] + jnp.sum(dh, axis=0, keepdims=True), sums[1] + jnp.sum(dhx * r, axis=0, keepdims=True))

            zero = jnp.zeros((1, D_MODEL), F32)
            sums = lax.fori_loop(0, tm // tr, chunk, (zero, zero))
            sums_ref[SUM_SHIFT:SUM_SHIFT + 1, :] += sums[0]
            sums_ref[SUM_SCALE:SUM_SCALE + 1, :] += sums[1] * g
            sums_ref[SUM_NORM_G:SUM_NORM_G + 1, :] += sums[1] * one_sc

        @pl.when((i == s // tm - 1) & (k == ksteps - 1))
        def _():
            scatter = _chip_scatter(pair_ref, parts_ref, send_sems, recv_sems)
            for cp in scatter:
                cp.wait_recv()
            for cp in scatter:
                cp.wait_send()

    row = pl.BlockSpec((1, D_MODEL), lambda i, k: (0, 0))
    hbm = pl.BlockSpec(memory_space=pl.ANY)
    return pl.pallas_call(
        body, name="in_proj_bwd", grid=(s // tm, ksteps),
        in_specs=[pl.BlockSpec((tm, tk), lambda i, k: (i, k)), pl.BlockSpec((tk, D_MODEL), lambda i, k: (k, 0)),
                  hbm, hbm, row, row, pl.BlockSpec((8, D_MODEL), lambda i, k: (0, 0)), hbm],
        out_specs=[pl.BlockSpec((tm, D_MODEL), lambda i, k: (i, 0)), pl.BlockSpec((8, D_MODEL), lambda i, k: (0, 0)),
                   hbm],
        out_shape=[SDS((s, D_MODEL), F32), SDS((8, D_MODEL), F32), SDS((3,) + pair.shape[1:], pair.dtype)],
        scratch_shapes=[pltpu.VMEM((tm, D_MODEL), F32), pltpu.VMEM((tm, D_MODEL), F32),
                        pltpu.SemaphoreType.DMA((2,)), *_scatter_scratch()],
        compiler_params=_params("arbitrary", "arbitrary"),
    )(dproj, wt, x, dx1, scale, norm_g, sums_o, pair)


def _sum_chips(own_ref, parts_ref):
    return ((own_ref[0].astype(F32) + parts_ref[0].astype(F32)) + parts_ref[1].astype(F32)) + parts_ref[2].astype(F32)


def _adam_rows(name, chip, pair, parts, w, m, v, tr):
    rows = w.shape[0]

    def body(chip_ref, own_ref, p_ref, w_ref, m_ref, v_ref, g_ref, d_ref, nm_ref, nv_ref):
        g = _sum_chips(own_ref, p_ref)
        g_ref[...] = g
        d_ref[...], nm_ref[...], nv_ref[...] = _adamw(w_ref[...], g, m_ref[...], v_ref[...])

    blk = pl.BlockSpec((tr, D_MODEL), lambda j, chip_ref: (j, 0))
    return pl.pallas_call(
        body, name=name,
        grid_spec=pltpu.PrefetchScalarGridSpec(
            num_scalar_prefetch=1, grid=(rows // tr,),
            in_specs=[pl.BlockSpec((1, tr, D_MODEL), lambda j, chip_ref: (chip_ref[0], j, 0)),
                      pl.BlockSpec((3, tr, D_MODEL), lambda j, chip_ref: (0, j, 0)), blk, blk, blk],
            out_specs=[blk] * 4),
        out_shape=[SDS(w.shape, F32)] * 4, compiler_params=_params("parallel"),
    )(chip, pair, parts, w, m, v)


def _adam_ada(name, cact, dmod, w, m, v):
    n = w.shape[1]
    tr = 128

    def body(c_ref, dm_ref, w_ref, m_ref, v_ref, g_ref, d_ref, nm_ref, nv_ref):
        pad_c = jnp.concatenate([c_ref[...], jnp.zeros_like(c_ref)], axis=0).astype(BF16)
        pad_d = jnp.concatenate([dm_ref[...], jnp.zeros_like(dm_ref)], axis=0).astype(BF16)
        g = lax.dot_general(pad_c, pad_d, TN, preferred_element_type=F32)
        g_ref[...] = g
        d_ref[...], nm_ref[...], nv_ref[...] = _adamw(w_ref[...], g, m_ref[...], v_ref[...])

    blk = pl.BlockSpec((tr, n), lambda j: (j, 0))
    return pl.pallas_call(
        body, name=name, grid=(D_MODEL // tr,),
        in_specs=[pl.BlockSpec((N_DEV, tr), lambda j: (0, j)), pl.BlockSpec((N_DEV, n), lambda j: (0, 0)),
                  blk, blk, blk],
        out_specs=[blk] * 4, out_shape=[SDS(w.shape, F32)] * 4,
        compiler_params=_params("parallel"),
    )(cact, dmod, w, m, v)


SMALL_PARAMS = ("w_spatial", "b_spatial", "sinks", "norm_g", "ln_v_g", "ln_v_b", "final_norm_g", "b_ada", "b_ada_final")


def _adam_small(d_wsp, misc, d_ln, sums, params):
    n_p = len(SMALL_PARAMS)

    def body(*refs):
        wsp_ref, misc_ref, ln_ref, sums_ref = refs[:4]
        wmv = [refs[4 + 3 * k:7 + 3 * k] for k in range(n_p)]
        loss_ref = refs[4 + 3 * n_p]
        outs = [refs[5 + 3 * n_p + 4 * k:9 + 3 * n_p + 4 * k] for k in range(n_p)]

        def column_sum(row):
            return total(sums_ref, (row, row + 1))

        def total(ref, rows=None):
            def part(j):
                return ref[j] if rows is None else ref[j, rows[0]:rows[1], :]
            acc = part(0)
            for j in range(1, N_DEV):
                acc = acc + part(j)
            return acc

        sink_rows = total(misc_ref, (ROW_DSINKS, ROW_DSINKS + 16))
        diag = (lax.broadcasted_iota(jnp.int32, (16, 128), 0) == lax.broadcasted_iota(jnp.int32, (16, 128), 1))
        grads = dict(
            w_spatial=total(wsp_ref), b_spatial=total(misc_ref, (ROW_DBSP, ROW_DBSP + A_GROUPS)),
            sinks=jnp.sum(jnp.where(diag, sink_rows, 0.0), axis=0, keepdims=True),
            norm_g=column_sum(SUM_NORM_G), ln_v_g=total(ln_ref, (0, 1)), ln_v_b=total(ln_ref, (1, 2)),
            final_norm_g=column_sum(SUM_FNG),
            b_ada=jnp.concatenate([column_sum(SUM_SHIFT), column_sum(SUM_SCALE), column_sum(SUM_GATE)], axis=1),
            b_ada_final=jnp.concatenate([column_sum(SUM_SHIFT_F), column_sum(SUM_SCALE_F)], axis=1))
        sq_err = jnp.sum(column_sum(SUM_SQ_ERR), axis=1, keepdims=True)
        loss_ref[...] = jnp.broadcast_to(sq_err * (0.5 / D_MODEL), (1, 128))
        for k, name in enumerate(SMALL_PARAMS):
            w_ref, m_ref, v_ref = wmv[k]
            g_ref, d_ref, nm_ref, nv_ref = outs[k]
            g_ref[...] = grads[name]
            d_ref[...], nm_ref[...], nv_ref[...] = _adamw(w_ref[...], grads[name], m_ref[...], v_ref[...])

    flat = [a for name in SMALL_PARAMS for a in params[name]]
    vmem = pl.BlockSpec(memory_space=pltpu.VMEM)
    out_shape = [SDS((1, 128), F32)] + [SDS(params[name][0].shape, F32) for name in SMALL_PARAMS for _ in range(4)]
    outs = pl.pallas_call(
        body, name="adam_small", in_specs=[vmem] * (4 + len(flat)), out_specs=[vmem] * len(out_shape),
        out_shape=out_shape, compiler_params=_params(),
    )(d_wsp, misc, d_ln, sums, *flat)
    return outs[0], {name: outs[1 + 4 * k:5 + 4 * k] for k, name in enumerate(SMALL_PARAMS)}


def kernel(x, c, w_ada, b_ada, norm_g, w_in, ln_v_g, ln_v_b, w_spatial, b_spatial, sinks, w_out, w_ada_final, b_ada_final, final_norm_g, loss_target, m_w_ada, m_b_ada, m_norm_g, m_w_in, m_ln_v_g, m_ln_v_b, m_w_spatial, m_b_spatial, m_sinks, m_w_out, m_w_ada_final, m_b_ada_final, m_final_norm_g, v_w_ada, v_b_ada, v_norm_g, v_w_in, v_ln_v_g, v_ln_v_b, v_w_spatial, v_b_spatial, v_sinks, v_w_out, v_w_ada_final, v_b_ada_final, v_final_norm_g):
    seq = x.shape[1]
    me = 4 * lax.axis_index("x") + 2 * lax.axis_index("y") + lax.axis_index("c")
    x2, tgt = x[0], loss_target[0]
    fng = final_norm_g.reshape(1, D_MODEL)

    n_ada, n_ada_f = w_ada.shape[2], w_ada_final.shape[1]
    cact, mod, mod_f = _ada_exchange(c, w_ada[0], b_ada.reshape(N_DEV, n_ada), w_ada_final,
                                     b_ada_final.reshape(N_DEV, n_ada_f))
    cact = cact.reshape(N_DEV, D_MODEL)
    mod, mod_f = mod.reshape(1, 3 * D_MODEL), mod_f.reshape(1, 2 * D_MODEL)
    shift, scale, gate = mod[:, :D_MODEL], mod[:, D_MODEL:2 * D_MODEL], mod[:, 2 * D_MODEL:]
    shift_f, scale_f = mod_f[:, :D_MODEL], mod_f[:, D_MODEL:]

    wt_f32, m_wt, v_wt = (jnp.swapaxes(a, 1, 2)[0] for a in (w_in, m_w_in, v_w_in))
    xi, yi = lax.axis_index("x"), lax.axis_index("y")
    chip_order = jnp.stack([2 * xi + yi, 2 * (1 - xi) + yi, 2 * xi + 1 - yi, 2 * (1 - xi) + 1 - yi]).astype(jnp.int32)
    wt_mine, wo_mine = _prep_weights(me.reshape(1), wt_f32, w_out[0])

    tabs = _rope_tables(seq)
    sinks_v = sinks.reshape(16)
    h, proj, wt = _gather_in_proj(chip_order, x2, shift, scale, norm_g, wt_mine)
    y, wo = _mixer_fwd(proj, tabs, ln_v_g, ln_v_b, w_spatial[0], b_spatial[0], sinks_v, wo_mine)
    dx1, do, dy, sums_o = _out_proj_loss(y, x2, tgt, wo, gate, shift_f, scale_f, fng)

    chip = (2 * lax.axis_index("x") + lax.axis_index("y")).reshape(1)
    pair_out, _ = _wgrad_pair("wgrad_out", y, do)
    dproj, d_ln, d_wsp, misc, parts_out = _mixer_bwd(
        me.reshape(1), proj, dy, tabs, ln_v_g, ln_v_b, w_spatial[0], b_spatial[0], sinks_v, pair_out)
    pair_in, (d_ln, d_wsp, misc) = _wgrad_pair(
        "wgrad_in", dproj, h, gathers=(d_ln, d_wsp.reshape(N_DEV * A_GROUPS * CHUNK, CHUNK), misc))
    grad_x, sums, parts_in = _in_proj_bwd(dproj, wt, x2, dx1, scale, norm_g, sums_o, pair_in)
    wt_leaves = [jnp.swapaxes(a[None], 1, 2)
                 for a in _adam_rows("adam_w_in", chip, pair_in, parts_in, wt_f32, m_wt, v_wt, 64)]
    w_out_leaves = [a[None] for a in _adam_rows("adam_w_out", chip, pair_out, parts_out, w_out[0], m_w_out[0], v_w_out[0], 32)]

    (sums,) = _all_gather("gather_sums", [sums], pltpu.VMEM)
    natural = dict(w_spatial=(A_GROUPS * CHUNK, CHUNK), b_spatial=(A_GROUPS, CHUNK), sinks=(1, 16), norm_g=(1, D_MODEL),
                   ln_v_g=(1, D_A), ln_v_b=(1, D_A), final_norm_g=(1, D_MODEL), b_ada=(1, 3 * D_MODEL),
                   b_ada_final=(1, 2 * D_MODEL))
    given = dict(
        w_spatial=(w_spatial, m_w_spatial, v_w_spatial), b_spatial=(b_spatial, m_b_spatial, v_b_spatial),
        sinks=(sinks, m_sinks, v_sinks), norm_g=(norm_g, m_norm_g, v_norm_g), ln_v_g=(ln_v_g, m_ln_v_g, v_ln_v_g),
        ln_v_b=(ln_v_b, m_ln_v_b, v_ln_v_b), final_norm_g=(final_norm_g, m_final_norm_g, v_final_norm_g),
        b_ada=(b_ada, m_b_ada, v_b_ada), b_ada_final=(b_ada_final, m_b_ada_final, v_b_ada_final))
    params = {name: tuple(a.reshape(natural[name]) for a in given[name]) for name in SMALL_PARAMS}
    params["sinks"] = tuple(jnp.pad(a, ((0, 0), (0, 128 - 16))) for a in params["sinks"])
    loss, small = _adam_small(d_wsp.reshape(N_DEV, A_GROUPS * CHUNK, CHUNK), misc.reshape(N_DEV, MISC_ROWS, 128),
                              d_ln.reshape(N_DEV, 8, D_A), sums, params)
    small["sinks"] = [a[:, :16] for a in small["sinks"]]
    small = {name: [a.reshape(given[name][0].shape) for a in small[name]] for name in SMALL_PARAMS}

    dmod_all = jnp.concatenate([sums[:, SUM_SHIFT], sums[:, SUM_SCALE], sums[:, SUM_GATE]], axis=1)
    dmod_f_all = jnp.concatenate([sums[:, SUM_SHIFT_F], sums[:, SUM_SCALE_F]], axis=1)
    dmod_mine = lax.dynamic_slice_in_dim(dmod_all, me * n_ada, n_ada, axis=1)
    dmod_f_mine = lax.dynamic_slice_in_dim(dmod_f_all, me * n_ada_f, n_ada_f, axis=1)
    ada = _adam_ada("adam_w_ada", cact, dmod_mine, w_ada[0], m_w_ada[0], v_w_ada[0])
    ada_f = _adam_ada("adam_w_ada_final", cact, dmod_f_mine, w_ada_final, m_w_ada_final, v_w_ada_final)

    def leaves(k):
        return (ada[k][None], small["b_ada"][k], small["norm_g"][k], wt_leaves[k], small["ln_v_g"][k],
                small["ln_v_b"][k], small["w_spatial"][k], small["b_spatial"][k], small["sinks"][k], w_out_leaves[k],
                ada_f[k], small["b_ada_final"][k], small["final_norm_g"][k])

    return (loss[0, 0], grad_x[None], *leaves(0), *leaves(1), *leaves(2), *leaves(3))
```

```python
import jax
import jax.numpy as jnp
from jax import lax
from jax.experimental import pallas as pl
from jax.experimental.pallas import tpu as pltpu

D_MODEL = 2048
D_IN = 5632
D_A = 1024
CHUNK = 128
A_GROUPS = 8
HEAD_DIM = 64
N_KV_HEADS = 4
N_DEV = 8
ROPE_THETA = 10000.0
NORM_EPS = 1e-5
ATTN_SCALE = HEAD_DIM ** -0.5

ADAM_LR = 0.001
ADAM_B1 = 0.9
ADAM_B2 = 0.999
ADAM_EPS = 1e-08
ADAM_WD = 0.01
ADAM_STEP = 10

OFF_U, OFF_VA, OFF_ZA, OFF_Q, OFF_K, OFF_V, OFF_ZB = 0, 1024, 2048, 3072, 4096, 4352, 4608

SUM_SHIFT, SUM_SCALE, SUM_NORM_G, SUM_GATE, SUM_SHIFT_F, SUM_SCALE_F, SUM_FNG, SUM_SQ_ERR = range(8)

V7X_VMEM_LIMIT_BYTES = 56 * 1024 * 1024

F32 = jnp.float32
BF16 = jnp.bfloat16
MESH = pl.DeviceIdType.MESH
SDS = jax.ShapeDtypeStruct
NT = (((1,), (1,)), ((), ()))
TN = (((0,), (0,)), ((), ()))


def _params(*semantics):
    return pltpu.CompilerParams(dimension_semantics=semantics or None, vmem_limit_bytes=V7X_VMEM_LIMIT_BYTES)


def _mesh_pos():
    return lax.axis_index("x"), lax.axis_index("y"), lax.axis_index("c")


def _sigmoid(z):
    return 1.0 / (1.0 + jnp.exp(-z))


def _adamw(w, g, m, v):
    m = ADAM_B1 * m + (1.0 - ADAM_B1) * g
    v = ADAM_B2 * v + (1.0 - ADAM_B2) * (g * g)
    m_hat = m / (1.0 - ADAM_B1 ** ADAM_STEP)
    v_hat = v / (1.0 - ADAM_B2 ** ADAM_STEP)
    delta = -ADAM_LR * (m_hat / (jnp.sqrt(v_hat) + ADAM_EPS) + ADAM_WD * w)
    return delta, m, v


def _all_gather(name, blocks, memory_space):
    n_arr = len(blocks)

    def body(*refs):
        ins, outs = refs[:n_arr], refs[n_arr:2 * n_arr]
        send_sems, recv_sems, local_sems = refs[2 * n_arr:]
        x, y, c = _mesh_pos()
        me, sibling = (x, y, c), (x, y, 1 - c)
        chips = [(1 - x, y), (x, 1 - y), (1 - x, 1 - y)]

        def slot(p):
            return 4 * p[0] + 2 * p[1] + p[2]

        def copy(a, k, block, to, src=None):
            dst = outs[a].at[slot(block)]
            return pltpu.make_async_remote_copy(
                src_ref=dst if src is None else src, dst_ref=dst,
                send_sem=send_sems.at[a, k], recv_sem=recv_sems.at[a, k],
                device_id=to, device_id_type=MESH)

        mine = [pltpu.make_async_copy(ins[a], outs[a].at[slot(me)], local_sems.at[a]) for a in range(n_arr)]
        for cp in mine:
            cp.start()
        first = []
        for a in range(n_arr):
            first.append(copy(a, 0, me, sibling, src=ins[a]))
            first += [copy(a, 1 + j, me, (*chip, c), src=ins[a]) for j, chip in enumerate(chips)]
        for cp in first:
            cp.start()
        passed = []
        for j, chip in enumerate(chips):
            for a in range(n_arr):
                copy(a, 1 + j, (*chip, c), me).wait_recv()
                fwd = copy(a, 4 + j, (*chip, c), sibling)
                fwd.start()
                passed.append(fwd)
        for a in range(n_arr):
            copy(a, 0, sibling, me).wait_recv()
            for j, chip in enumerate(chips):
                copy(a, 4 + j, (*chip, 1 - c), me).wait_recv()
        for cp in first + passed:
            cp.wait_send()
        for cp in mine:
            cp.wait()

    spec = pl.BlockSpec(memory_space=memory_space)
    return pl.pallas_call(
        body, name=name,
        out_shape=[SDS((N_DEV,) + b.shape, b.dtype) for b in blocks],
        in_specs=[spec] * n_arr, out_specs=[spec] * n_arr,
        scratch_shapes=[pltpu.SemaphoreType.DMA((n_arr, 7)), pltpu.SemaphoreType.DMA((n_arr, 7)),
                        pltpu.SemaphoreType.DMA((n_arr,))],
        compiler_params=_params(),
    )(*blocks)


def _ada_exchange(c, w_ada, b_ada8, w_ada_f, b_ada_f8):
    n1, n2 = w_ada.shape[1], w_ada_f.shape[1]

    def body(c_ref, w1_ref, b1_ref, w2_ref, b2_ref, cact_ref, mod_ref, modf_ref,
             cact_buf, res1, res2, send1, send2, sems_s, sems_r):
        x, y, c_pos = _mesh_pos()
        me = 4 * x + 2 * y + c_pos
        flips = [(k >> 2 & 1, k >> 1 & 1, k & 1) for k in range(1, N_DEV)]

        def peer(f):
            return (1 - x if f[0] else x, 1 - y if f[1] else y, 1 - c_pos if f[2] else c_pos)

        cv = c_ref[...]
        cact = cv * _sigmoid(cv)
        cact_buf[...] = cact
        cact_ref[me] = cact

        def rdma(phase, k, src, dst, f):
            return pltpu.make_async_remote_copy(src_ref=src, dst_ref=dst, send_sem=sems_s.at[phase, k],
                                                recv_sem=sems_r.at[phase, k], device_id=peer(f), device_id_type=MESH)

        gather = [rdma(0, k, cact_buf, cact_ref.at[me], f) for k, f in enumerate(flips)]
        for cp in gather:
            cp.start()
        for cp in gather:
            cp.wait_recv()
        for cp in gather:
            cp.wait_send()

        rid = lax.broadcasted_iota(jnp.int32, (N_DEV, D_MODEL), 0)
        rows = jnp.zeros((N_DEV, D_MODEL), F32)
        for j in range(N_DEV):
            rows = jnp.where(rid == j, jnp.broadcast_to(cact_ref[j], (N_DEV, D_MODEL)), rows)
        rows = rows.astype(BF16)
        res1[...] = jnp.dot(rows, w1_ref[...].astype(BF16), preferred_element_type=F32) + b1_ref[pl.ds(me, 1), :]
        res2[...] = jnp.dot(rows, w2_ref[...].astype(BF16), preferred_element_type=F32) + b2_ref[pl.ds(me, 1), :]
        for j in range(N_DEV):
            send1[j] = res1[pl.ds(j, 1), :]
            send2[j] = res2[pl.ds(j, 1), :]
        mod_ref[me] = send1[me]
        modf_ref[me] = send2[me]
        scatter = []
        for k, f in enumerate(flips):
            to = me ^ (k + 1)
            scatter.append(rdma(1, k, send1.at[to], mod_ref.at[me], f))
            scatter.append(rdma(2, k, send2.at[to], modf_ref.at[me], f))
        for cp in scatter:
            cp.start()
        for cp in scatter:
            cp.wait_recv()
        for cp in scatter:
            cp.wait_send()

    vmem = pl.BlockSpec(memory_space=pltpu.VMEM)
    return pl.pallas_call(
        body, name="ada_exchange",
        out_shape=[SDS((N_DEV, 1, D_MODEL), F32), SDS((N_DEV, 1, n1), F32), SDS((N_DEV, 1, n2), F32)],
        in_specs=[vmem] * 5, out_specs=[vmem] * 3,
        scratch_shapes=[pltpu.VMEM((1, D_MODEL), F32), pltpu.VMEM((N_DEV, n1), F32), pltpu.VMEM((N_DEV, n2), F32),
                        pltpu.VMEM((N_DEV, 1, n1), F32), pltpu.VMEM((N_DEV, 1, n2), F32),
                        pltpu.SemaphoreType.DMA((3, 7)), pltpu.SemaphoreType.DMA((3, 7))],
        compiler_params=_params(),
    )(c, w_ada, b_ada8, w_ada_f, b_ada_f8)


def _chip_scatter(pair_ref, parts_ref, send_sems, recv_sems):
    x, y, c = _mesh_pos()
    chips = [(1 - x, y), (x, 1 - y), (1 - x, 1 - y)]
    return [pltpu.make_async_remote_copy(
        src_ref=pair_ref.at[2 * cx + cy], dst_ref=parts_ref.at[j], send_sem=send_sems.at[j], recv_sem=recv_sems.at[j],
        device_id=(cx, cy, c), device_id_type=MESH) for j, (cx, cy) in enumerate(chips)]


def _scatter_scratch():
    return [pltpu.SemaphoreType.DMA((3,)), pltpu.SemaphoreType.DMA((3,))]


def _prep_weights(me, wt, w_out):
    steps = 4

    def body(me_ref, wt_ref, wo_ref, wtb_ref, wob_ref):
        wtb_ref[...] = wt_ref[...].astype(BF16)
        wob_ref[...] = wo_ref[...].astype(BF16)

    def rows(a, mine):
        blk = (a.shape[0] // steps, a.shape[1])
        return pl.BlockSpec(blk, (lambda i, me_ref: (steps * me_ref[0] + i, 0)) if mine else (lambda i, me_ref: (i, 0)))

    return pl.pallas_call(
        body, name="prep_weights",
        grid_spec=pltpu.PrefetchScalarGridSpec(
            num_scalar_prefetch=1, grid=(steps,),
            in_specs=[rows(wt, False), rows(w_out, False)], out_specs=[rows(wt, True), rows(w_out, True)]),
        out_shape=[SDS((N_DEV * wt.shape[0], D_MODEL), BF16), SDS((N_DEV * w_out.shape[0], D_MODEL), BF16)],
        compiler_params=_params("parallel"),
    )(me, wt, w_out)


class _InPlaceGather:
    def __init__(self, buf_ref, send_sems, recv_sems, relay=False):
        self.buf, self.send_sems, self.recv_sems, self.relay = buf_ref, send_sems, recv_sems, relay
        self.n = buf_ref.shape[0] // N_DEV
        x, y, c = _mesh_pos()
        self.me, self.sibling, self.core = (x, y, c), (x, y, 1 - c), c
        self.chips = [(1 - x, y), (x, 1 - y), (1 - x, 1 - y)]
        self.relay_from = (jnp.where(c == 0, 1 - x, x), jnp.where(c == 0, y, 1 - y), c)
        self.relay_to = (jnp.where(c == 0, x, 1 - x), jnp.where(c == 0, 1 - y, y), c)

    def copy(self, k, block, to):
        start = pl.multiple_of((4 * block[0] + 2 * block[1] + block[2]) * self.n, self.n)
        rows = self.buf.at[pl.ds(start, self.n)]
        return pltpu.make_async_remote_copy(src_ref=rows, dst_ref=rows, send_sem=self.send_sems.at[k],
                                            recv_sem=self.recv_sems.at[k], device_id=to, device_id_type=MESH)

    def start(self):
        self.copy(0, self.me, self.sibling).start()
        for j, chip in enumerate(self.chips[:2] if self.relay else self.chips):
            self.copy(1 + j, self.me, (*chip, self.core)).start()

    def relay_diagonal(self):
        self.copy(3, self.relay_from, self.relay_to).start()

    def pass_on(self, j):
        self.copy(1 + j, (*self.chips[j], self.core), self.me).wait_recv()
        self.copy(4 + j, (*self.chips[j], self.core), self.sibling).start()

    def wait_sibling(self, k):
        self.copy(k, self.sibling, self.me).wait_recv()

    def wait_sends(self):
        for k in range(7):
            self.copy(k, self.me, self.sibling).wait_send()


def _gather_scratch():
    return [pltpu.SemaphoreType.DMA((7,)), pltpu.SemaphoreType.DMA((7,))]


def _gather_in_proj(order, x, shift, scale, norm_g, wt_all):
    s = x.shape[0]
    th = tm = min(512, s)
    nh, ni = s // th, s // tm
    tn = D_IN // 4
    steps = nh + 4 * ni

    def body(order_ref, x_ref, shift_ref, scale_ref, g_ref, wt_in, h_ref, proj_ref, wt_ref,
             h_scr, w_buf, load_sems, send_sems, recv_sems):
        g = pl.program_id(0)
        gather = _InPlaceGather(wt_ref, send_sems, recv_sems, relay=True)

        def tile_load(slot, chip):
            return pltpu.make_async_copy(wt_ref.at[pl.ds(pl.multiple_of(chip * tn, tn), tn)], w_buf.at[slot],
                                         load_sems.at[slot])

        @pl.when(g == 0)
        def _():
            gather.start()

        @pl.when(g < nh)
        def _():
            xv = x_ref[...]
            r = lax.rsqrt(jnp.mean(xv * xv, axis=-1, keepdims=True) + NORM_EPS)
            hb = (((xv * r) * g_ref[...]) * (1.0 + scale_ref[...]) + shift_ref[...]).astype(BF16)
            h_ref[...] = hb
            h_scr[pl.ds(pl.multiple_of(g * th, th), th), :] = hb

        @pl.when(g == nh - 1)
        def _():
            gather.wait_sibling(0)
            tile_load(0, order_ref[0]).start()

        @pl.when(g >= nh)
        def _():
            t, i = (g - nh) // ni, (g - nh) % ni

            @pl.when(i == 0)
            def _():
                tile_load(t % 2, order_ref[t]).wait()

            @pl.when((i == ni - 1) & (t == 0))
            def _():
                gather.pass_on(0)
                gather.pass_on(1)
                gather.relay_diagonal()

            @pl.when((i == ni // 2) & (t == 2))
            def _():
                gather.pass_on(2)

            for j in range(3):
                @pl.when((i == ni - 1) & (t == j))
                def _():
                    gather.wait_sibling(4 + j)
                    tile_load((j + 1) % 2, order_ref[j + 1]).start()

            lhs = h_scr[pl.ds(pl.multiple_of(i * tm, tm), tm), :]
            proj_ref[...] = lax.dot_general(lhs, w_buf[t % 2], NT, preferred_element_type=F32).astype(BF16)

        @pl.when(g == steps - 1)
        def _():
            gather.wait_sends()

    def h_tile(g, order_ref):
        return (jnp.minimum(g, nh - 1), 0)

    def proj_tile(g, order_ref):
        mm = jnp.maximum(g - nh, 0)
        return (mm % ni, order_ref[mm // ni])

    row = pl.BlockSpec((1, D_MODEL), lambda g, order_ref: (0, 0))
    hbm = pl.BlockSpec(memory_space=pl.ANY)
    return pl.pallas_call(
        body, name="gather_in_proj",
        grid_spec=pltpu.PrefetchScalarGridSpec(
            num_scalar_prefetch=1, grid=(steps,),
            in_specs=[pl.BlockSpec((th, D_MODEL), h_tile), row, row, row, hbm],
            out_specs=[pl.BlockSpec((th, D_MODEL), h_tile), pl.BlockSpec((tm, tn), proj_tile), hbm],
            scratch_shapes=[pltpu.VMEM((s, D_MODEL), BF16), pltpu.VMEM((2, tn, D_MODEL), BF16),
                            pltpu.SemaphoreType.DMA((2,)), *_gather_scratch()]),
        out_shape=[SDS((s, D_MODEL), BF16), SDS((s, D_IN), BF16), SDS(wt_all.shape, BF16)],
        input_output_aliases={5: 2},
        compiler_params=_params("arbitrary"),
    )(order, x, shift, scale, norm_g, wt_all)


def _rope_freqs():
    inv_freq = ROPE_THETA ** (-jnp.arange(0, HEAD_DIM, 2, dtype=F32) / HEAD_DIM)
    return jnp.tile(inv_freq, 4).reshape(1, 128)


class _RopeTables:
    def __init__(self, freq_ref, rows_ref, state_ref, last_ref):
        self.freq, self.rows, self.state, self.last = freq_ref, rows_ref, state_ref, last_ref

    def start(self, block, direction):
        ang = lax.broadcasted_iota(jnp.int32, (CHUNK, 128), 0).astype(F32) * self.freq[...]
        self.rows[0] = jnp.cos(ang)
        self.rows[1] = jnp.sin(ang)
        base = jnp.asarray(block * CHUNK, dtype=F32) * self.freq[...]
        turn = float(direction * CHUNK) * self.freq[...]
        self.state[0:1, :] = jnp.cos(base)
        self.state[1:2, :] = jnp.sin(base)
        self.state[2:3, :] = jnp.cos(turn)
        self.state[3:4, :] = jnp.sin(turn)

    def step(self):
        c, s, ct, st = (self.state[k:k + 1, :] for k in range(4))
        self.state[0:1, :] = c * ct - s * st
        self.state[1:2, :] = s * ct + c * st

    def tables(self):
        c, s = self.state[0:1, :], self.state[1:2, :]
        cos = c * self.rows[0] - s * self.rows[1]
        sin = s * self.rows[0] + c * self.rows[1]
        first_half = (lax.broadcasted_iota(jnp.int32, (1, 128), 1) & (HEAD_DIM - 1)) < HEAD_DIM // 2
        return cos, jnp.where(first_half, -sin, 0.0), jnp.where(first_half, 0.0, sin)

    def keep(self, tabs):
        for k in range(3):
            self.last[k] = tabs[k]

    def kept(self):
        return tuple(self.last[k] for k in range(3))


def _rope_scratch():
    return [pltpu.VMEM((2, CHUNK, 128), F32), pltpu.VMEM((8, 128), F32), pltpu.VMEM((3, CHUNK, 128), F32)]


def _rope(v, cos, sin_lo, sin_hi):
    width = v.shape[1]
    rep = (1, width // 128)
    return (v * jnp.tile(cos, rep) + pltpu.roll(v, width - 32, 1) * jnp.tile(sin_lo, rep)
            + pltpu.roll(v, 32, 1) * jnp.tile(sin_hi, rep))


def _rope_bwd(d, cos, sin_lo, sin_hi):
    width = d.shape[1]
    rep = (1, width // 128)
    return (d * jnp.tile(cos, rep) + pltpu.roll(d * jnp.tile(sin_lo, rep), 32, 1)
            + pltpu.roll(d * jnp.tile(sin_hi, rep), width - 32, 1))


def _layer_norm(v, g, b):
    mu = jnp.mean(v, axis=-1, keepdims=True)
    vc = v - mu
    rstd = lax.rsqrt(jnp.mean(vc * vc, axis=-1, keepdims=True) + NORM_EPS)
    vhat = vc * rstd
    return vhat * g + b, vhat, rstd


def _tril_bf16(w_ref, g):
    t = lax.broadcasted_iota(jnp.int32, (CHUNK, CHUNK), 0)
    tp = lax.broadcasted_iota(jnp.int32, (CHUNK, CHUNK), 1)
    return jnp.where(tp <= t, w_ref[g], 0.0).astype(BF16)


def _bias_columns(b_ref, out_ref):
    for g in range(A_GROUPS):
        out_ref[g] = jnp.broadcast_to(b_ref[pl.ds(g, 1), :], (CHUNK, CHUNK)).T


def _band_mask():
    kj = lax.broadcasted_iota(jnp.int32, (2 * CHUNK, 4 * CHUNK), 0)
    qi = lax.broadcasted_iota(jnp.int32, (2 * CHUNK, 4 * CHUNK), 1) & (CHUNK - 1)
    rel = qi + CHUNK - kj
    return jnp.where((rel >= 0) & (rel < CHUNK), 0.0, -jnp.inf)


def _low_lanes():
    return lax.broadcasted_iota(jnp.int32, (1, 128), 1) < HEAD_DIM


def _stack_heads(pair_a, pair_b):
    lo = _low_lanes()
    return jnp.concatenate([jnp.where(lo, pair_a, 0.0), jnp.where(lo, 0.0, pair_a),
                            jnp.where(lo, pair_b, 0.0), jnp.where(lo, 0.0, pair_b)], axis=0).astype(BF16)


def _heads_to_lanes(per_group):
    rows = [t[:, r * CHUNK:(r + 1) * CHUNK] for t in per_group for r in range(4)]
    return jnp.concatenate(rows, axis=0).T


def _dup_kv_head(band, gk):
    pair = band[:, (gk // 2) * 128:(gk // 2 + 1) * 128]
    lo = _low_lanes()
    one = jnp.where(lo if gk % 2 == 0 else jnp.logical_not(lo), pair, 0.0)
    return (one + pltpu.roll(one, HEAD_DIM, 1)).astype(BF16)


def _fold_kv_head(dup_grad, gk):
    both = dup_grad + pltpu.roll(dup_grad, HEAD_DIM, 1)
    lo = _low_lanes()
    return jnp.where(lo if gk % 2 == 0 else jnp.logical_not(lo), both, 0.0)


def _attn_probs(q_st, k_dup, sink_row, mask, first_block):
    s = lax.dot_general(k_dup, q_st, NT, preferred_element_type=F32) + mask
    s = jnp.concatenate([jnp.where(first_block, -jnp.inf, s[:CHUNK]), s[CHUNK:]], axis=0)
    m = jnp.maximum(jnp.max(s, axis=0, keepdims=True), sink_row)
    p = jnp.exp(s - m)
    e_sink = jnp.exp(sink_row - m)
    inv = 1.0 / (jnp.sum(p, axis=0, keepdims=True) + e_sink)
    return p * inv, e_sink * inv


def _sink_row(sinks_ref, gk):
    return jnp.concatenate([jnp.full((1, CHUNK), sinks_ref[4 * gk + r], F32) for r in range(4)], axis=1)


def _mixer_specs(nb, rev):
    def blk(i):
        return nb - 1 - i if rev else i

    def prev(i):
        return jnp.maximum(blk(i) - 1, 0)

    return dict(
        cur=pl.BlockSpec((CHUNK, D_IN), lambda i, *_: (blk(i), 0)),
        prev_kv=pl.BlockSpec((CHUNK, 2 * 256), lambda i, *_: (prev(i), OFF_K // 512)),
        freq=pl.BlockSpec((1, 128), lambda i, *_: (0, 0)),
        vec=pl.BlockSpec((1, D_A), lambda i, *_: (0, 0)),
        wsp=pl.BlockSpec((A_GROUPS, CHUNK, CHUNK), lambda i, *_: (0, 0, 0)),
        bsp=pl.BlockSpec((A_GROUPS, CHUNK), lambda i, *_: (0, 0)),
        smem=pl.BlockSpec(memory_space=pltpu.SMEM),
        blk=blk,
    )


def _mixer_fwd(proj, freqs, ln_g, ln_b, w_sp, b_sp, sinks, wo_all):
    s = proj.shape[0]
    nb = s // CHUNK
    sp = _mixer_specs(nb, rev=False)

    def body(cur_ref, pkv_ref, freq_ref, lg_ref, lb_ref, w_ref, b_ref, sinks_ref, wo_in, y_ref, wo_ref,
             bcol, mask, rope_rows, rope_state, rope_last, send_sems, recv_sems):
        i = pl.program_id(0)
        gather = _InPlaceGather(wo_ref, send_sems, recv_sems)
        rope = _RopeTables(freq_ref, rope_rows, rope_state, rope_last)

        @pl.when(i == 0)
        def _():
            gather.start()
            _bias_columns(b_ref, bcol)
            mask[...] = _band_mask()
            rope.start(-1, 1)
            rope_last[...] = jnp.zeros_like(rope_last)

        @pl.when(i == nb // 2)
        def _():
            for j in range(3):
                gather.pass_on(j)

        vln, _, _ = _layer_norm(cur_ref[:, OFF_VA:OFF_ZA].astype(F32), lg_ref[...], lb_ref[...])
        vln = vln.astype(BF16)
        for g in range(A_GROUPS):
            cols = slice(g * 128, (g + 1) * 128)
            sg = jnp.dot(_tril_bf16(w_ref, g), vln[:, cols], preferred_element_type=F32) + bcol[g]
            u = cur_ref[:, OFF_U + g * 128:OFF_U + (g + 1) * 128].astype(F32)
            z = cur_ref[:, OFF_ZA + g * 128:OFF_ZA + (g + 1) * 128].astype(F32)
            y_ref[:, cols] = (u * sg * (z * _sigmoid(z))).astype(BF16)

        rope.step()
        cur_t, prev_t = rope.tables(), rope.kept()
        rope.keep(cur_t)
        qr = _rope(cur_ref[:, OFF_Q:OFF_K].astype(F32), *cur_t) * ATTN_SCALE
        kr = jnp.concatenate([_rope(pkv_ref[:, 0:256].astype(F32), *prev_t),
                              _rope(cur_ref[:, OFF_K:OFF_V].astype(F32), *cur_t)], axis=0)
        v_t = jnp.concatenate([pkv_ref[:, 256:512], cur_ref[:, OFF_V:OFF_ZB]], axis=0).astype(F32).T.astype(BF16)
        outs = []
        for gk in range(N_KV_HEADS):
            q_st = _stack_heads(qr[:, (2 * gk) * 128:(2 * gk + 1) * 128], qr[:, (2 * gk + 1) * 128:(2 * gk + 2) * 128])
            probs, _ = _attn_probs(q_st, _dup_kv_head(kr, gk), _sink_row(sinks_ref, gk), mask[...], i == 0)
            outs.append(jnp.dot(v_t[gk * HEAD_DIM:(gk + 1) * HEAD_DIM], probs.astype(BF16),
                                preferred_element_type=F32))
        zb = cur_ref[:, OFF_ZB:D_IN].astype(F32)
        y_ref[:, D_A:D_MODEL] = (_heads_to_lanes(outs) * (zb * _sigmoid(zb))).astype(BF16)

        @pl.when(i == nb - 1)
        def _():
            gather.wait_sibling(0)
            for j in range(3):
                gather.wait_sibling(4 + j)
            gather.wait_sends()

    hbm = pl.BlockSpec(memory_space=pl.ANY)
    return pl.pallas_call(
        body, name="mixer_fwd", grid=(nb,),
        in_specs=[sp["cur"], sp["prev_kv"], sp["freq"], sp["vec"], sp["vec"], sp["wsp"], sp["bsp"], sp["smem"], hbm],
        out_specs=[pl.BlockSpec((CHUNK, D_MODEL), lambda i: (i, 0)), hbm],
        out_shape=[SDS((s, D_MODEL), BF16), SDS(wo_all.shape, wo_all.dtype)],
        scratch_shapes=[pltpu.VMEM((A_GROUPS, CHUNK, CHUNK), F32), pltpu.VMEM((2 * CHUNK, 4 * CHUNK), F32),
                        *_rope_scratch(), *_gather_scratch()],
        input_output_aliases={8: 1},
        compiler_params=_params("arbitrary"),
    )(proj, proj, freqs, ln_g, ln_b, w_sp, b_sp, sinks, wo_all)


def _out_proj_loss(y, x, target, wo, gate, shift_f, scale_f, fng):
    s = y.shape[0]
    tm, tr = 256, 128
    nt = s // tm

    def body(y_ref, x_ref, t_ref, wo_ref, gate_ref, sh_ref, sc_ref, g_ref, dx1_ref, do_ref, dy_ref, sums_ref,
             do_last, do_work):
        i = pl.program_id(0)

        @pl.when(i == 0)
        def _():
            sums_ref[...] = jnp.zeros_like(sums_ref)
            do_last[...] = jnp.zeros_like(do_last)

        do_work[...] = do_last[...]
        o = jnp.dot(y_ref[...], wo_ref[...], preferred_element_type=F32)
        gate, g, sh = gate_ref[...], g_ref[...], sh_ref[...]
        one_sc = 1.0 + sc_ref[...]
        cs, inv_d = g * one_sc, 1.0 / D_MODEL

        def rowsum(v):
            return jnp.sum(v, axis=0, keepdims=True)

        sums = [jnp.zeros((1, D_MODEL), F32) for _ in range(4)]
        for c in range(tm // tr):
            rows = slice(c * tr, (c + 1) * tr)
            oc = o[rows]
            x1 = x_ref[rows, :] + gate * oc
            r = lax.rsqrt(jnp.sum(x1 * x1, axis=-1, keepdims=True) * inv_d + NORM_EPS)
            x1n = x1 * r
            diff = x1n * cs + sh - t_ref[rows, :]
            w = diff * x1n
            lane_sum = jnp.sum(w * cs, axis=-1, keepdims=True)
            dx1 = (diff * cs) * (r * inv_d) - x1n * (r * lane_sum * (inv_d * inv_d))
            dx1_ref[rows, :] = dx1
            do = (dx1 * gate).astype(BF16)
            do_ref[rows, :] = do
            do_last[rows, :] = do
            for k, v in enumerate((dx1 * oc, diff, w, diff * diff)):
                sums[k] = sums[k] + rowsum(v)
        live = jnp.where(i < nt, 1.0, 0.0)
        for row, v in ((SUM_GATE, sums[0]), (SUM_SHIFT_F, inv_d * sums[1]), (SUM_SCALE_F, inv_d * (sums[2] * g)),
                       (SUM_FNG, inv_d * (sums[2] * one_sc)), (SUM_SQ_ERR, sums[3])):
            sums_ref[row:row + 1, :] += live * v
        dy_ref[...] = lax.dot_general(do_work[...], wo_ref[...], NT, preferred_element_type=F32).astype(BF16)

    tile = pl.BlockSpec((tm, D_MODEL), lambda i: (jnp.minimum(i, nt - 1), 0))
    row = pl.BlockSpec((1, D_MODEL), lambda i: (0, 0))
    return pl.pallas_call(
        body, name="out_proj_loss", grid=(nt + 1,),
        in_specs=[tile, tile, tile, pl.BlockSpec((D_MODEL, D_MODEL), lambda i: (0, 0)), row, row, row, row],
        out_specs=[tile, tile, pl.BlockSpec((tm, D_MODEL), lambda i: (jnp.maximum(i - 1, 0), 0)),
                   pl.BlockSpec((8, D_MODEL), lambda i: (0, 0))],
        out_shape=[SDS((s, D_MODEL), F32), SDS((s, D_MODEL), BF16), SDS((s, D_MODEL), BF16), SDS((8, D_MODEL), F32)],
        scratch_shapes=[pltpu.VMEM((tm, D_MODEL), BF16), pltpu.VMEM((tm, D_MODEL), BF16)],
        compiler_params=_params("arbitrary"),
    )(y, x, target, wo, gate, shift_f, scale_f, fng)


ROW_DBSP, ROW_DSINKS, MISC_ROWS = 0, 8, 32


def _mixer_bwd(me, proj, dy, freqs, ln_g, ln_b, w_sp, b_sp, sinks, pair):
    s = proj.shape[0]
    nb = s // CHUNK
    sp = _mixer_specs(nb, rev=True)

    def body(me_ref, cur_ref, pkv_ref, dy_ref, freq_ref, lg_ref, lb_ref, w_ref, b_ref, sinks_ref, pair_ref,
             dproj_ref, dln_ref, dw_ref, misc_ref, parts_ref, bcol, dbcol, carry, mask, rope_rows, rope_state,
             rope_last, send_sems, recv_sems):
        i = pl.program_id(0)
        block = nb - 1 - i
        rope = _RopeTables(freq_ref, rope_rows, rope_state, rope_last)

        @pl.when(i == 0)
        def _():
            for cp in _chip_scatter(pair_ref, parts_ref, send_sems, recv_sems):
                cp.start()
            _bias_columns(b_ref, bcol)
            mask[...] = _band_mask()
            rope.start(nb - 1, -1)
            rope.keep(rope.tables())
            dbcol[...] = jnp.zeros_like(dbcol)
            carry[...] = jnp.zeros_like(carry)
            dln_ref[...] = jnp.zeros_like(dln_ref)
            dw_ref[...] = jnp.zeros_like(dw_ref)
            misc_ref[...] = jnp.zeros_like(misc_ref)

        vln, vhat, rstd = _layer_norm(cur_ref[:, OFF_VA:OFF_ZA].astype(F32), lg_ref[...], lb_ref[...])
        vln = vln.astype(BF16)
        d_vln = []
        for g in range(A_GROUPS):
            cols = slice(g * 128, (g + 1) * 128)
            w_g = _tril_bf16(w_ref, g)
            sg = jnp.dot(w_g, vln[:, cols], preferred_element_type=F32) + bcol[g]
            u = cur_ref[:, OFF_U + g * 128:OFF_U + (g + 1) * 128].astype(F32)
            z = cur_ref[:, OFF_ZA + g * 128:OFF_ZA + (g + 1) * 128].astype(F32)
            dya = dy_ref[:, cols].astype(F32)
            sig = _sigmoid(z)
            d_ya = dya * (z * sig)
            dproj_ref[:, OFF_ZA + g * 128:OFF_ZA + (g + 1) * 128] = (
                dya * (u * sg) * (sig * (1.0 + z * (1.0 - sig)))).astype(BF16)
            dproj_ref[:, OFF_U + g * 128:OFF_U + (g + 1) * 128] = (d_ya * sg).astype(BF16)
            d_s = d_ya * u
            dbcol[g] += d_s
            d_sb = d_s.astype(BF16)
            dw_ref[g] += lax.dot_general(d_sb, vln[:, cols], NT, preferred_element_type=F32)
            d_vln.append(lax.dot_general(w_g, d_sb, TN, preferred_element_type=F32))
        d_vln = jnp.concatenate(d_vln, axis=1)
        dln_ref[0:1, :] += jnp.sum(d_vln * vhat, axis=0, keepdims=True)
        dln_ref[1:2, :] += jnp.sum(d_vln, axis=0, keepdims=True)
        d_vhat = d_vln * lg_ref[...]
        d_va = rstd * (d_vhat - jnp.mean(d_vhat, axis=-1, keepdims=True)
                       - vhat * jnp.mean(d_vhat * vhat, axis=-1, keepdims=True))
        dproj_ref[:, OFF_VA:OFF_ZA] = d_va.astype(BF16)

        cur_t = rope.kept()
        rope.step()
        prev_t = rope.tables()
        rope.keep(prev_t)
        band_t = tuple(jnp.concatenate([p, c], axis=0) for p, c in zip(prev_t, cur_t))
        qr = _rope(cur_ref[:, OFF_Q:OFF_K].astype(F32), *cur_t) * ATTN_SCALE
        kr = jnp.concatenate([_rope(pkv_ref[:, 0:256].astype(F32), *prev_t),
                              _rope(cur_ref[:, OFF_K:OFF_V].astype(F32), *cur_t)], axis=0)
        vb = jnp.concatenate([pkv_ref[:, 256:512], cur_ref[:, OFF_V:OFF_ZB]], axis=0).astype(F32)
        k_t, v_t = (kr.T * ATTN_SCALE).astype(BF16), vb.T.astype(BF16)
        zb = cur_ref[:, OFF_ZB:D_IN].astype(F32)
        dyb = dy_ref[:, D_A:D_MODEL].astype(F32)
        sig = _sigmoid(zb)
        d_yb = dyb * (zb * sig)
        outs, dqs = [], []
        dk_pairs = [jnp.zeros((2 * CHUNK, 128), F32) for _ in range(2)]
        dv_pairs = [jnp.zeros((2 * CHUNK, 128), F32) for _ in range(2)]
        for gk in range(N_KV_HEADS):
            heads = slice(gk * HEAD_DIM, (gk + 1) * HEAD_DIM)
            q_st = _stack_heads(qr[:, (2 * gk) * 128:(2 * gk + 1) * 128], qr[:, (2 * gk + 1) * 128:(2 * gk + 2) * 128])
            k_dup, v_dup = _dup_kv_head(kr, gk), _dup_kv_head(vb, gk)
            probs, p_sink = _attn_probs(q_st, k_dup, _sink_row(sinks_ref, gk), mask[...], block == 0)
            probs_b = probs.astype(BF16)
            outs.append(jnp.dot(v_t[heads], probs_b, preferred_element_type=F32))
            do_st = _stack_heads(d_yb[:, (2 * gk) * 128:(2 * gk + 1) * 128], d_yb[:, (2 * gk + 1) * 128:(2 * gk + 2) * 128])
            dp = lax.dot_general(v_dup, do_st, NT, preferred_element_type=F32)
            delta = jnp.sum(probs * dp, axis=0, keepdims=True)
            ds = (probs * (dp - delta)).astype(BF16)
            d_sink = -p_sink * delta
            for r in range(4):
                row = ROW_DSINKS + 4 * gk + r
                misc_ref[row:row + 1, :] += jnp.broadcast_to(
                    jnp.sum(d_sink[:, r * CHUNK:(r + 1) * CHUNK], axis=1, keepdims=True), (1, 128))
            dqs.append(jnp.dot(k_t[heads], ds, preferred_element_type=F32))
            dk_pairs[gk // 2] += _fold_kv_head(jnp.dot(ds, q_st, preferred_element_type=F32), gk)
            dv_pairs[gk // 2] += _fold_kv_head(jnp.dot(probs_b, do_st, preferred_element_type=F32), gk)
        dproj_ref[:, OFF_ZB:D_IN] = (dyb * _heads_to_lanes(outs) * (sig * (1.0 + zb * (1.0 - sig)))).astype(BF16)
        dproj_ref[:, OFF_Q:OFF_K] = _rope_bwd(_heads_to_lanes(dqs), *cur_t).astype(BF16)
        dk_band = _rope_bwd(jnp.concatenate(dk_pairs, axis=1), *band_t)
        dv_band = jnp.concatenate(dv_pairs, axis=1)
        dproj_ref[:, OFF_K:OFF_V] = (dk_band[CHUNK:] + carry[:, 0:256]).astype(BF16)
        dproj_ref[:, OFF_V:OFF_ZB] = (dv_band[CHUNK:] + carry[:, 256:512]).astype(BF16)
        carry[:, 0:256] = dk_band[:CHUNK]
        carry[:, 256:512] = dv_band[:CHUNK]

        @pl.when(i == nb - 1)
        def _():
            t = lax.broadcasted_iota(jnp.int32, (CHUNK, CHUNK), 0)
            tp = lax.broadcasted_iota(jnp.int32, (CHUNK, CHUNK), 1)
            for g in range(A_GROUPS):
                dw_ref[g] = jnp.where(tp <= t, dw_ref[g], 0.0)
                misc_ref[pl.ds(ROW_DBSP + g, 1), :] = jnp.sum(dbcol[g].T, axis=0, keepdims=True)
            scatter = _chip_scatter(pair_ref, parts_ref, send_sems, recv_sems)
            for cp in scatter:
                cp.wait_recv()
            for cp in scatter:
                cp.wait_send()

    blk = sp["blk"]
    hbm = pl.BlockSpec(memory_space=pl.ANY)
    return pl.pallas_call(
        body, name="mixer_bwd",
        grid_spec=pltpu.PrefetchScalarGridSpec(
            num_scalar_prefetch=1, grid=(nb,),
            in_specs=[sp["cur"], sp["prev_kv"], pl.BlockSpec((CHUNK, D_MODEL), lambda i, me_ref: (blk(i), 0)),
                      sp["freq"], sp["vec"], sp["vec"], sp["wsp"], sp["bsp"], sp["smem"], hbm],
            out_specs=[pl.BlockSpec((CHUNK, D_IN), lambda i, me_ref: (blk(i), 0)),
                       pl.BlockSpec((8, D_A), lambda i, me_ref: (me_ref[0], 0)),
                       pl.BlockSpec((A_GROUPS, CHUNK, CHUNK), lambda i, me_ref: (me_ref[0], 0, 0)),
                       pl.BlockSpec((MISC_ROWS, 128), lambda i, me_ref: (me_ref[0], 0)), hbm],
            scratch_shapes=[pltpu.VMEM((A_GROUPS, CHUNK, CHUNK), F32), pltpu.VMEM((A_GROUPS, CHUNK, CHUNK), F32),
                            pltpu.VMEM((CHUNK, 512), F32), pltpu.VMEM((2 * CHUNK, 4 * CHUNK), F32),
                            *_rope_scratch(), *_scatter_scratch()]),
        out_shape=[SDS((s, D_IN), BF16), SDS((N_DEV * 8, D_A), F32), SDS((N_DEV * A_GROUPS, CHUNK, CHUNK), F32),
                   SDS((N_DEV * MISC_ROWS, 128), F32), SDS((3,) + pair.shape[1:], pair.dtype)],
        compiler_params=_params("arbitrary"),
    )(me, proj, proj, dy, freqs, ln_g, ln_b, w_sp, b_sp, sinks, pair)


def _wgrad_pair(name, a, b, gathers=()):
    s, m = a.shape
    n = b.shape[1]
    bm, half = m // 4, m // 8
    bt = min(1024, s)
    steps = s // bt
    last = 4 * steps
    n_g = len(gathers)

    def body(*refs):
        a_ref, b_ref = refs[:2]
        out_ref, bufs = refs[2 + n_g], refs[3 + n_g:3 + 2 * n_g]
        acc, kept, got, sent, send_sems, recv_sems = refs[3 + 2 * n_g:9 + 2 * n_g]
        sems = refs[9 + 2 * n_g:]
        g = pl.program_id(0)
        tile, t = g // steps, g % steps
        mx, my, mc = _mesh_pos()
        jobs = [_InPlaceGather(bufs[k], sems[2 * k], sems[2 * k + 1]) for k in range(n_g)]

        def exchange(q):
            return pltpu.make_async_remote_copy(src_ref=sent, dst_ref=got.at[q % 2], send_sem=send_sems.at[q],
                                                recv_sem=recv_sems.at[q], device_id=(mx, my, 1 - mc),
                                                device_id_type=MESH)

        @pl.when(g == 0)
        def _():
            for job in jobs:
                job.start()

        @pl.when(g == 2 * steps)
        def _():
            for job in jobs:
                for j in range(3):
                    job.pass_on(j)

        @pl.when(g < last)
        def _():
            prod = lax.dot_general(a_ref[...], b_ref[...], TN, preferred_element_type=F32)

            @pl.when(t == 0)
            def _():
                acc[...] = prod

            @pl.when(t > 0)
            def _():
                acc[...] += prod

            @pl.when(t == steps - 1)
            def _():
                @pl.when(tile > 0)
                def _():
                    exchange(tile - 1).wait_send()

                kept[tile % 2] = acc[pl.ds(pl.multiple_of(mc * half, 8), half), :].astype(BF16)
                sent[...] = acc[pl.ds(pl.multiple_of((1 - mc) * half, 8), half), :].astype(BF16)
                exchange(tile).start()

        @pl.when((t == 0) & (g > 0))
        def _():
            q = tile - 1
            exchange(q).wait_recv()
            out_ref[0] = (kept[q % 2].astype(F32) + got[q % 2].astype(F32)).astype(BF16)

        @pl.when(g == last)
        def _():
            exchange(3).wait_send()
            for job in jobs:
                job.wait_sibling(0)
                for j in range(3):
                    job.wait_sibling(4 + j)
                job.wait_sends()

    def a_tile(g):
        gg = jnp.minimum(g, last - 1)
        return (gg % steps, gg // steps)

    def b_tile(g):
        return (jnp.minimum(g, last - 1) % steps, 0)

    hbm = pl.BlockSpec(memory_space=pl.ANY)
    outs = pl.pallas_call(
        body, name=name, grid=(last + 1,),
        in_specs=[pl.BlockSpec((bt, bm), a_tile), pl.BlockSpec((bt, n), b_tile)] + [hbm] * n_g,
        out_specs=[pl.BlockSpec((1, half, n), lambda g: (jnp.maximum(g - 1, 0) // steps, 0, 0))] + [hbm] * n_g,
        out_shape=[SDS((4, half, n), BF16)] + [SDS(gb.shape, gb.dtype) for gb in gathers],
        scratch_shapes=[pltpu.VMEM((bm, n), F32), pltpu.VMEM((2, half, n), BF16), pltpu.VMEM((2, half, n), BF16),
                        pltpu.VMEM((half, n), BF16), pltpu.SemaphoreType.DMA((4,)), pltpu.SemaphoreType.DMA((4,))]
        + _gather_scratch() * n_g,
        input_output_aliases={2 + k: 1 + k for k in range(n_g)},
        compiler_params=_params("arbitrary"),
    )(a, b, *gathers)
    return outs[0], outs[1:]


def _in_proj_bwd(dproj, wt, x, dx1, scale, norm_g, sums_o, pair):
    s = x.shape[0]
    tm, tk, tr = min(1024, s), D_IN // 4, 64
    ksteps = D_IN // tk

    def body(dp_ref, wt_ref, x_hbm, dx1_hbm, sc_ref, g_ref, so_ref, pair_ref, gx_ref, sums_ref, parts_ref, x_buf,
             dx1_buf, tile_sems, send_sems, recv_sems):
        i, k = pl.program_id(0), pl.program_id(1)

        def tile_copies():
            rows = pl.ds(pl.multiple_of(i * tm, tm), tm)
            return (pltpu.make_async_copy(x_hbm.at[rows], x_buf, tile_sems.at[0]),
                    pltpu.make_async_copy(dx1_hbm.at[rows], dx1_buf, tile_sems.at[1]))

        @pl.when((i == 0) & (k == 0))
        def _():
            for cp in _chip_scatter(pair_ref, parts_ref, send_sems, recv_sems):
                cp.start()
            sums_ref[...] = so_ref[...]

        @pl.when(k == 0)
        def _():
            for cp in tile_copies():
                cp.start()
            gx_ref[...] = jnp.dot(dp_ref[...], wt_ref[...], preferred_element_type=F32)

        @pl.when(k > 0)
        def _():
            gx_ref[...] += jnp.dot(dp_ref[...], wt_ref[...], preferred_element_type=F32)

        @pl.when(k == ksteps - 1)
        def _():
            for cp in tile_copies():
                cp.wait()
            one_sc, g = 1.0 + sc_ref[...], g_ref[...]
            cs = one_sc * g

            def chunk(j, sums):
                rows = pl.ds(pl.multiple_of(j * tr, tr), tr)
                dh, xv = gx_ref[rows, :], x_buf[rows, :]
                dhx = dh * xv
                r = lax.rsqrt(jnp.sum(xv * xv, axis=-1, keepdims=True) * (1.0 / D_MODEL) + NORM_EPS)
                coef = (r * r * r) * (jnp.sum(dhx * cs, axis=-1, keepdims=True) * (1.0 / D_MODEL))
                gx_ref[rows, :] = dx1_buf[rows, :] + r * (dh * cs) - xv * coef
                return (sums[0] + jnp.sum(dh, axis=0, keepdims=True), sums[1] + jnp.sum(dhx * r, axis=0, keepdims=True))

            zero = jnp.zeros((1, D_MODEL), F32)
            sums = lax.fori_loop(0, tm // tr, chunk, (zero, zero))
            sums_ref[SUM_SHIFT:SUM_SHIFT + 1, :] += sums[0]
            sums_ref[SUM_SCALE:SUM_SCALE + 1, :] += sums[1] * g
            sums_ref[SUM_NORM_G:SUM_NORM_G + 1, :] += sums[1] * one_sc

        @pl.when((i == s // tm - 1) & (k == ksteps - 1))
        def _():
            scatter = _chip_scatter(pair_ref, parts_ref, send_sems, recv_sems)
            for cp in scatter:
                cp.wait_recv()
            for cp in scatter:
                cp.wait_send()

    row = pl.BlockSpec((1, D_MODEL), lambda i, k: (0, 0))
    hbm = pl.BlockSpec(memory_space=pl.ANY)
    return pl.pallas_call(
        body, name="in_proj_bwd", grid=(s // tm, ksteps),
        in_specs=[pl.BlockSpec((tm, tk), lambda i, k: (i, k)), pl.BlockSpec((tk, D_MODEL), lambda i, k: (k, 0)),
                  hbm, hbm, row, row, pl.BlockSpec((8, D_MODEL), lambda i, k: (0, 0)), hbm],
        out_specs=[pl.BlockSpec((tm, D_MODEL), lambda i, k: (i, 0)), pl.BlockSpec((8, D_MODEL), lambda i, k: (0, 0)),
                   hbm],
        out_shape=[SDS((s, D_MODEL), F32), SDS((8, D_MODEL), F32), SDS((3,) + pair.shape[1:], pair.dtype)],
        scratch_shapes=[pltpu.VMEM((tm, D_MODEL), F32), pltpu.VMEM((tm, D_MODEL), F32),
                        pltpu.SemaphoreType.DMA((2,)), *_scatter_scratch()],
        compiler_params=_params("arbitrary", "arbitrary"),
    )(dproj, wt, x, dx1, scale, norm_g, sums_o, pair)


def _sum_chips(own_ref, parts_ref):
    return ((own_ref[0].astype(F32) + parts_ref[0].astype(F32)) + parts_ref[1].astype(F32)) + parts_ref[2].astype(F32)


def _adam_rows(name, chip, pair, parts, w, m, v, tr):
    rows = w.shape[0]

    def body(chip_ref, own_ref, p_ref, w_ref, m_ref, v_ref, g_ref, d_ref, nm_ref, nv_ref):
        g = _sum_chips(own_ref, p_ref)
        g_ref[...] = g
        d_ref[...], nm_ref[...], nv_ref[...] = _adamw(w_ref[...], g, m_ref[...], v_ref[...])

    blk = pl.BlockSpec((tr, D_MODEL), lambda j, chip_ref: (j, 0))
    return pl.pallas_call(
        body, name=name,
        grid_spec=pltpu.PrefetchScalarGridSpec(
            num_scalar_prefetch=1, grid=(rows // tr,),
            in_specs=[pl.BlockSpec((1, tr, D_MODEL), lambda j, chip_ref: (chip_ref[0], j, 0)),
                      pl.BlockSpec((3, tr, D_MODEL), lambda j, chip_ref: (0, j, 0)), blk, blk, blk],
            out_specs=[blk] * 4),
        out_shape=[SDS(w.shape, F32)] * 4, compiler_params=_params("parallel"),
    )(chip, pair, parts, w, m, v)


def _adam_ada(name, cact, dmod, w, m, v):
    n = w.shape[1]
    tr = 512

    def body(c_ref, dm_ref, w_ref, m_ref, v_ref, g_ref, d_ref, nm_ref, nv_ref):
        pad_c = jnp.concatenate([c_ref[...], jnp.zeros_like(c_ref)], axis=0).astype(BF16)
        pad_d = jnp.concatenate([dm_ref[...], jnp.zeros_like(dm_ref)], axis=0).astype(BF16)
        g = lax.dot_general(pad_c, pad_d, TN, preferred_element_type=F32)
        g_ref[...] = g
        d_ref[...], nm_ref[...], nv_ref[...] = _adamw(w_ref[...], g, m_ref[...], v_ref[...])

    blk = pl.BlockSpec((tr, n), lambda j: (j, 0))
    return pl.pallas_call(
        body, name=name, grid=(D_MODEL // tr,),
        in_specs=[pl.BlockSpec((N_DEV, tr), lambda j: (0, j)), pl.BlockSpec((N_DEV, n), lambda j: (0, 0)),
                  blk, blk, blk],
        out_specs=[blk] * 4, out_shape=[SDS(w.shape, F32)] * 4,
        compiler_params=_params("parallel"),
    )(cact, dmod, w, m, v)


SMALL_PARAMS = ("w_spatial", "b_spatial", "sinks", "norm_g", "ln_v_g", "ln_v_b", "final_norm_g", "b_ada", "b_ada_final")


def _adam_small(d_wsp, misc, d_ln, sums, params):
    n_p = len(SMALL_PARAMS)

    def body(*refs):
        wsp_ref, misc_ref, ln_ref, sums_ref = refs[:4]
        wmv = [refs[4 + 3 * k:7 + 3 * k] for k in range(n_p)]
        loss_ref = refs[4 + 3 * n_p]
        outs = [refs[5 + 3 * n_p + 4 * k:9 + 3 * n_p + 4 * k] for k in range(n_p)]

        def column_sum(row):
            return total(sums_ref, (row, row + 1))

        def total(ref, rows=None):
            def part(j):
                return ref[j] if rows is None else ref[j, rows[0]:rows[1], :]
            acc = part(0)
            for j in range(1, N_DEV):
                acc = acc + part(j)
            return acc

        sink_rows = total(misc_ref, (ROW_DSINKS, ROW_DSINKS + 16))
        diag = (lax.broadcasted_iota(jnp.int32, (16, 128), 0) == lax.broadcasted_iota(jnp.int32, (16, 128), 1))
        grads = dict(
            w_spatial=total(wsp_ref), b_spatial=total(misc_ref, (ROW_DBSP, ROW_DBSP + A_GROUPS)),
            sinks=jnp.sum(jnp.where(diag, sink_rows, 0.0), axis=0, keepdims=True),
            norm_g=column_sum(SUM_NORM_G), ln_v_g=total(ln_ref, (0, 1)), ln_v_b=total(ln_ref, (1, 2)),
            final_norm_g=column_sum(SUM_FNG),
            b_ada=jnp.concatenate([column_sum(SUM_SHIFT), column_sum(SUM_SCALE), column_sum(SUM_GATE)], axis=1),
            b_ada_final=jnp.concatenate([column_sum(SUM_SHIFT_F), column_sum(SUM_SCALE_F)], axis=1))
        sq_err = jnp.sum(column_sum(SUM_SQ_ERR), axis=1, keepdims=True)
        loss_ref[...] = jnp.broadcast_to(sq_err * (0.5 / D_MODEL), (1, 128))
        for k, name in enumerate(SMALL_PARAMS):
            w_ref, m_ref, v_ref = wmv[k]
            g_ref, d_ref, nm_ref, nv_ref = outs[k]
            g_ref[...] = grads[name]
            d_ref[...], nm_ref[...], nv_ref[...] = _adamw(w_ref[...], grads[name], m_ref[...], v_ref[...])

    flat = [a for name in SMALL_PARAMS for a in params[name]]
    vmem = pl.BlockSpec(memory_space=pltpu.VMEM)
    out_shape = [SDS((1, 128), F32)] + [SDS(params[name][0].shape, F32) for name in SMALL_PARAMS for _ in range(4)]
    outs = pl.pallas_call(
        body, name="adam_small", in_specs=[vmem] * (4 + len(flat)), out_specs=[vmem] * len(out_shape),
        out_shape=out_shape, compiler_params=_params(),
    )(d_wsp, misc, d_ln, sums, *flat)
    return outs[0], {name: outs[1 + 4 * k:5 + 4 * k] for k, name in enumerate(SMALL_PARAMS)}


def kernel(x, c, w_ada, b_ada, norm_g, w_in, ln_v_g, ln_v_b, w_spatial, b_spatial, sinks, w_out, w_ada_final, b_ada_final, final_norm_g, loss_target, m_w_ada, m_b_ada, m_norm_g, m_w_in, m_ln_v_g, m_ln_v_b, m_w_spatial, m_b_spatial, m_sinks, m_w_out, m_w_ada_final, m_b_ada_final, m_final_norm_g, v_w_ada, v_b_ada, v_norm_g, v_w_in, v_ln_v_g, v_ln_v_b, v_w_spatial, v_b_spatial, v_sinks, v_w_out, v_w_ada_final, v_b_ada_final, v_final_norm_g):
    seq = x.shape[1]
    me = 4 * lax.axis_index("x") + 2 * lax.axis_index("y") + lax.axis_index("c")
    x2, tgt = x[0], loss_target[0]
    fng = final_norm_g.reshape(1, D_MODEL)

    n_ada, n_ada_f = w_ada.shape[2], w_ada_final.shape[1]
    cact, mod, mod_f = _ada_exchange(c, w_ada[0], b_ada.reshape(N_DEV, n_ada), w_ada_final,
                                     b_ada_final.reshape(N_DEV, n_ada_f))
    cact = cact.reshape(N_DEV, D_MODEL)
    mod, mod_f = mod.reshape(1, 3 * D_MODEL), mod_f.reshape(1, 2 * D_MODEL)
    shift, scale, gate = mod[:, :D_MODEL], mod[:, D_MODEL:2 * D_MODEL], mod[:, 2 * D_MODEL:]
    shift_f, scale_f = mod_f[:, :D_MODEL], mod_f[:, D_MODEL:]

    wt_f32, m_wt, v_wt = (jnp.swapaxes(a, 1, 2)[0] for a in (w_in, m_w_in, v_w_in))
    xi, yi = lax.axis_index("x"), lax.axis_index("y")
    chip_order = jnp.stack([2 * xi + yi, 2 * (1 - xi) + yi, 2 * xi + 1 - yi, 2 * (1 - xi) + 1 - yi]).astype(jnp.int32)
    wt_mine, wo_mine = _prep_weights(me.reshape(1), wt_f32, w_out[0])

    freqs = _rope_freqs()
    sinks_v = sinks.reshape(16)
    h, proj, wt = _gather_in_proj(chip_order, x2, shift, scale, norm_g, wt_mine)
    y, wo = _mixer_fwd(proj, freqs, ln_v_g, ln_v_b, w_spatial[0], b_spatial[0], sinks_v, wo_mine)
    dx1, do, dy, sums_o = _out_proj_loss(y, x2, tgt, wo, gate, shift_f, scale_f, fng)

    chip = (2 * lax.axis_index("x") + lax.axis_index("y")).reshape(1)
    pair_out, _ = _wgrad_pair("wgrad_out", y, do)
    dproj, d_ln, d_wsp, misc, parts_out = _mixer_bwd(
        me.reshape(1), proj, dy, freqs, ln_v_g, ln_v_b, w_spatial[0], b_spatial[0], sinks_v, pair_out)
    pair_in, (d_ln, d_wsp, misc) = _wgrad_pair(
        "wgrad_in", dproj, h, gathers=(d_ln, d_wsp.reshape(N_DEV * A_GROUPS * CHUNK, CHUNK), misc))
    grad_x, sums, parts_in = _in_proj_bwd(dproj, wt, x2, dx1, scale, norm_g, sums_o, pair_in)
    wt_leaves = [jnp.swapaxes(a[None], 1, 2)
                 for a in _adam_rows("adam_w_in", chip, pair_in, parts_in, wt_f32, m_wt, v_wt, 176)]
    w_out_leaves = [a[None] for a in _adam_rows("adam_w_out", chip, pair_out, parts_out, w_out[0], m_w_out[0], v_w_out[0], 64)]

    (sums,) = _all_gather("gather_sums", [sums], pltpu.VMEM)
    natural = dict(w_spatial=(A_GROUPS * CHUNK, CHUNK), b_spatial=(A_GROUPS, CHUNK), sinks=(1, 16), norm_g=(1, D_MODEL),
                   ln_v_g=(1, D_A), ln_v_b=(1, D_A), final_norm_g=(1, D_MODEL), b_ada=(1, 3 * D_MODEL),
                   b_ada_final=(1, 2 * D_MODEL))
    given = dict(
        w_spatial=(w_spatial, m_w_spatial, v_w_spatial), b_spatial=(b_spatial, m_b_spatial, v_b_spatial),
        sinks=(sinks, m_sinks, v_sinks), norm_g=(norm_g, m_norm_g, v_norm_g), ln_v_g=(ln_v_g, m_ln_v_g, v_ln_v_g),
        ln_v_b=(ln_v_b, m_ln_v_b, v_ln_v_b), final_norm_g=(final_norm_g, m_final_norm_g, v_final_norm_g),
        b_ada=(b_ada, m_b_ada, v_b_ada), b_ada_final=(b_ada_final, m_b_ada_final, v_b_ada_final))
    params = {name: tuple(a.reshape(natural[name]) for a in given[name]) for name in SMALL_PARAMS}
    params["sinks"] = tuple(jnp.pad(a, ((0, 0), (0, 128 - 16))) for a in params["sinks"])
    loss, small = _adam_small(d_wsp.reshape(N_DEV, A_GROUPS * CHUNK, CHUNK), misc.reshape(N_DEV, MISC_ROWS, 128),
                              d_ln.reshape(N_DEV, 8, D_A), sums, params)
    small["sinks"] = [a[:, :16] for a in small["sinks"]]
    small = {name: [a.reshape(given[name][0].shape) for a in small[name]] for name in SMALL_PARAMS}

    dmod_all = jnp.concatenate([sums[:, SUM_SHIFT], sums[:, SUM_SCALE], sums[:, SUM_GATE]], axis=1)
    dmod_f_all = jnp.concatenate([sums[:, SUM_SHIFT_F], sums[:, SUM_SCALE_F]], axis=1)
    dmod_mine = lax.dynamic_slice_in_dim(dmod_all, me * n_ada, n_ada, axis=1)
    dmod_f_mine = lax.dynamic_slice_in_dim(dmod_f_all, me * n_ada_f, n_ada_f, axis=1)
    ada = _adam_ada("adam_w_ada", cact, dmod_mine, w_ada[0], m_w_ada[0], v_w_ada[0])
    ada_f = _adam_ada("adam_w_ada_final", cact, dmod_f_mine, w_ada_final, m_w_ada_final, v_w_ada_final)

    def leaves(k):
        return (ada[k][None], small["b_ada"][k], small["norm_g"][k], wt_leaves[k], small["ln_v_g"][k],
                small["ln_v_b"][k], small["w_spatial"][k], small["b_spatial"][k], small["sinks"][k], w_out_leaves[k],
                ada_f[k], small["b_ada_final"][k], small["final_norm_g"][k])

    return (loss[0, 0], grad_x[None], *leaves(0), *leaves(1), *leaves(2), *leaves(3))
```

```python
import jax
import jax.numpy as jnp
from jax import lax
from jax.experimental import pallas as pl
from jax.experimental.pallas import tpu as pltpu

D_MODEL = 2048
D_IN = 5632
D_A = 1024
CHUNK = 128
A_GROUPS = 8
HEAD_DIM = 64
N_KV_HEADS = 4
N_DEV = 8
ROPE_THETA = 10000.0
NORM_EPS = 1e-5
ATTN_SCALE = HEAD_DIM ** -0.5

ADAM_LR = 0.001
ADAM_B1 = 0.9
ADAM_B2 = 0.999
ADAM_EPS = 1e-08
ADAM_WD = 0.01
ADAM_STEP = 10

OFF_U, OFF_VA, OFF_ZA, OFF_Q, OFF_K, OFF_V, OFF_ZB = 0, 1024, 2048, 3072, 4096, 4352, 4608

SUM_SHIFT, SUM_SCALE, SUM_NORM_G, SUM_GATE, SUM_SHIFT_F, SUM_SCALE_F, SUM_FNG, SUM_SQ_ERR = range(8)

V7X_VMEM_LIMIT_BYTES = 56 * 1024 * 1024

F32 = jnp.float32
BF16 = jnp.bfloat16
MESH = pl.DeviceIdType.MESH
SDS = jax.ShapeDtypeStruct
NT = (((1,), (1,)), ((), ()))
TN = (((0,), (0,)), ((), ()))


def _params(*semantics):
    return pltpu.CompilerParams(dimension_semantics=semantics or None, vmem_limit_bytes=V7X_VMEM_LIMIT_BYTES)


def _mesh_pos():
    return lax.axis_index("x"), lax.axis_index("y"), lax.axis_index("c")


def _sigmoid(z):
    return 1.0 / (1.0 + jnp.exp(-z))


def _adamw(w, g, m, v):
    m = ADAM_B1 * m + (1.0 - ADAM_B1) * g
    v = ADAM_B2 * v + (1.0 - ADAM_B2) * (g * g)
    m_hat = m / (1.0 - ADAM_B1 ** ADAM_STEP)
    v_hat = v / (1.0 - ADAM_B2 ** ADAM_STEP)
    delta = -ADAM_LR * (m_hat / (jnp.sqrt(v_hat) + ADAM_EPS) + ADAM_WD * w)
    return delta, m, v


def _all_gather(name, blocks, memory_space):
    n_arr = len(blocks)

    def body(*refs):
        ins, outs = refs[:n_arr], refs[n_arr:2 * n_arr]
        send_sems, recv_sems, local_sems = refs[2 * n_arr:]
        x, y, c = _mesh_pos()
        me, sibling = (x, y, c), (x, y, 1 - c)
        chips = [(1 - x, y), (x, 1 - y), (1 - x, 1 - y)]

        def slot(p):
            return 4 * p[0] + 2 * p[1] + p[2]

        def copy(a, k, block, to, src=None):
            dst = outs[a].at[slot(block)]
            return pltpu.make_async_remote_copy(
                src_ref=dst if src is None else src, dst_ref=dst,
                send_sem=send_sems.at[a, k], recv_sem=recv_sems.at[a, k],
                device_id=to, device_id_type=MESH)

        mine = [pltpu.make_async_copy(ins[a], outs[a].at[slot(me)], local_sems.at[a]) for a in range(n_arr)]
        for cp in mine:
            cp.start()
        first = []
        for a in range(n_arr):
            first.append(copy(a, 0, me, sibling, src=ins[a]))
            first += [copy(a, 1 + j, me, (*chip, c), src=ins[a]) for j, chip in enumerate(chips)]
        for cp in first:
            cp.start()
        passed = []
        for j, chip in enumerate(chips):
            for a in range(n_arr):
                copy(a, 1 + j, (*chip, c), me).wait_recv()
                fwd = copy(a, 4 + j, (*chip, c), sibling)
                fwd.start()
                passed.append(fwd)
        for a in range(n_arr):
            copy(a, 0, sibling, me).wait_recv()
            for j, chip in enumerate(chips):
                copy(a, 4 + j, (*chip, 1 - c), me).wait_recv()
        for cp in first + passed:
            cp.wait_send()
        for cp in mine:
            cp.wait()

    spec = pl.BlockSpec(memory_space=memory_space)
    return pl.pallas_call(
        body, name=name,
        out_shape=[SDS((N_DEV,) + b.shape, b.dtype) for b in blocks],
        in_specs=[spec] * n_arr, out_specs=[spec] * n_arr,
        scratch_shapes=[pltpu.SemaphoreType.DMA((n_arr, 7)), pltpu.SemaphoreType.DMA((n_arr, 7)),
                        pltpu.SemaphoreType.DMA((n_arr,))],
        compiler_params=_params(),
    )(*blocks)


def _ada_exchange(c, w_ada, b_ada8, w_ada_f, b_ada_f8):
    n1, n2 = w_ada.shape[1], w_ada_f.shape[1]

    def body(c_ref, w1_ref, b1_ref, w2_ref, b2_ref, cact_ref, mod_ref, modf_ref,
             cact_buf, res1, res2, send1, send2, sems_s, sems_r):
        x, y, c_pos = _mesh_pos()
        me = 4 * x + 2 * y + c_pos
        flips = [(k >> 2 & 1, k >> 1 & 1, k & 1) for k in range(1, N_DEV)]

        def peer(f):
            return (1 - x if f[0] else x, 1 - y if f[1] else y, 1 - c_pos if f[2] else c_pos)

        cv = c_ref[...]
        cact = cv * _sigmoid(cv)
        cact_buf[...] = cact
        cact_ref[me] = cact

        def rdma(phase, k, src, dst, f):
            return pltpu.make_async_remote_copy(src_ref=src, dst_ref=dst, send_sem=sems_s.at[phase, k],
                                                recv_sem=sems_r.at[phase, k], device_id=peer(f), device_id_type=MESH)

        gather = [rdma(0, k, cact_buf, cact_ref.at[me], f) for k, f in enumerate(flips)]
        for cp in gather:
            cp.start()
        for cp in gather:
            cp.wait_recv()
        for cp in gather:
            cp.wait_send()

        rid = lax.broadcasted_iota(jnp.int32, (N_DEV, D_MODEL), 0)
        rows = jnp.zeros((N_DEV, D_MODEL), F32)
        for j in range(N_DEV):
            rows = jnp.where(rid == j, jnp.broadcast_to(cact_ref[j], (N_DEV, D_MODEL)), rows)
        rows = rows.astype(BF16)
        res1[...] = jnp.dot(rows, w1_ref[...].astype(BF16), preferred_element_type=F32) + b1_ref[pl.ds(me, 1), :]
        res2[...] = jnp.dot(rows, w2_ref[...].astype(BF16), preferred_element_type=F32) + b2_ref[pl.ds(me, 1), :]
        for j in range(N_DEV):
            send1[j] = res1[pl.ds(j, 1), :]
            send2[j] = res2[pl.ds(j, 1), :]
        mod_ref[me] = send1[me]
        modf_ref[me] = send2[me]
        scatter = []
        for k, f in enumerate(flips):
            to = me ^ (k + 1)
            scatter.append(rdma(1, k, send1.at[to], mod_ref.at[me], f))
            scatter.append(rdma(2, k, send2.at[to], modf_ref.at[me], f))
        for cp in scatter:
            cp.start()
        for cp in scatter:
            cp.wait_recv()
        for cp in scatter:
            cp.wait_send()

    vmem = pl.BlockSpec(memory_space=pltpu.VMEM)
    return pl.pallas_call(
        body, name="ada_exchange",
        out_shape=[SDS((N_DEV, 1, D_MODEL), F32), SDS((N_DEV, 1, n1), F32), SDS((N_DEV, 1, n2), F32)],
        in_specs=[vmem] * 5, out_specs=[vmem] * 3,
        scratch_shapes=[pltpu.VMEM((1, D_MODEL), F32), pltpu.VMEM((N_DEV, n1), F32), pltpu.VMEM((N_DEV, n2), F32),
                        pltpu.VMEM((N_DEV, 1, n1), F32), pltpu.VMEM((N_DEV, 1, n2), F32),
                        pltpu.SemaphoreType.DMA((3, 7)), pltpu.SemaphoreType.DMA((3, 7))],
        compiler_params=_params(),
    )(c, w_ada, b_ada8, w_ada_f, b_ada_f8)


def _chip_scatter(pair_ref, parts_ref, send_sems, recv_sems):
    x, y, c = _mesh_pos()
    chips = [(1 - x, y), (x, 1 - y), (1 - x, 1 - y)]
    return [pltpu.make_async_remote_copy(
        src_ref=pair_ref.at[2 * cx + cy], dst_ref=parts_ref.at[j], send_sem=send_sems.at[j], recv_sem=recv_sems.at[j],
        device_id=(cx, cy, c), device_id_type=MESH) for j, (cx, cy) in enumerate(chips)]


def _scatter_scratch():
    return [pltpu.SemaphoreType.DMA((3,)), pltpu.SemaphoreType.DMA((3,))]


def _prep_weights(me, wt, w_out):
    steps = 4

    def body(me_ref, wt_ref, wo_ref, wtb_ref, wob_ref):
        wtb_ref[...] = wt_ref[...].astype(BF16)
        wob_ref[...] = wo_ref[...].astype(BF16)

    def rows(a, mine):
        blk = (a.shape[0] // steps, a.shape[1])
        return pl.BlockSpec(blk, (lambda i, me_ref: (steps * me_ref[0] + i, 0)) if mine else (lambda i, me_ref: (i, 0)))

    return pl.pallas_call(
        body, name="prep_weights",
        grid_spec=pltpu.PrefetchScalarGridSpec(
            num_scalar_prefetch=1, grid=(steps,),
            in_specs=[rows(wt, False), rows(w_out, False)], out_specs=[rows(wt, True), rows(w_out, True)]),
        out_shape=[SDS((N_DEV * wt.shape[0], D_MODEL), BF16), SDS((N_DEV * w_out.shape[0], D_MODEL), BF16)],
        compiler_params=_params("parallel"),
    )(me, wt, w_out)


class _InPlaceGather:
    def __init__(self, buf_ref, send_sems, recv_sems, relay=False):
        self.buf, self.send_sems, self.recv_sems, self.relay = buf_ref, send_sems, recv_sems, relay
        self.n = buf_ref.shape[0] // N_DEV
        x, y, c = _mesh_pos()
        self.me, self.sibling, self.core = (x, y, c), (x, y, 1 - c), c
        self.chips = [(1 - x, y), (x, 1 - y), (1 - x, 1 - y)]
        self.relay_from = (jnp.where(c == 0, 1 - x, x), jnp.where(c == 0, y, 1 - y), c)
        self.relay_to = (jnp.where(c == 0, x, 1 - x), jnp.where(c == 0, 1 - y, y), c)

    def copy(self, k, block, to):
        start = pl.multiple_of((4 * block[0] + 2 * block[1] + block[2]) * self.n, self.n)
        rows = self.buf.at[pl.ds(start, self.n)]
        return pltpu.make_async_remote_copy(src_ref=rows, dst_ref=rows, send_sem=self.send_sems.at[k],
                                            recv_sem=self.recv_sems.at[k], device_id=to, device_id_type=MESH)

    def start(self):
        self.copy(0, self.me, self.sibling).start()
        for j, chip in enumerate(self.chips[:2] if self.relay else self.chips):
            self.copy(1 + j, self.me, (*chip, self.core)).start()

    def relay_diagonal(self):
        self.copy(3, self.relay_from, self.relay_to).start()

    def pass_on(self, j):
        self.copy(1 + j, (*self.chips[j], self.core), self.me).wait_recv()
        self.copy(4 + j, (*self.chips[j], self.core), self.sibling).start()

    def wait_sibling(self, k):
        self.copy(k, self.sibling, self.me).wait_recv()

    def wait_sends(self):
        for k in range(7):
            self.copy(k, self.me, self.sibling).wait_send()


def _gather_scratch():
    return [pltpu.SemaphoreType.DMA((7,)), pltpu.SemaphoreType.DMA((7,))]


def _gather_in_proj(order, x, shift, scale, norm_g, wt_all):
    s = x.shape[0]
    th = tm = min(512, s)
    nh, ni = s // th, s // tm
    tn = D_IN // 4
    steps = nh + 4 * ni

    def body(order_ref, x_ref, shift_ref, scale_ref, g_ref, wt_in, h_ref, proj_ref, wt_ref,
             h_scr, w_buf, load_sems, send_sems, recv_sems):
        g = pl.program_id(0)
        gather = _InPlaceGather(wt_ref, send_sems, recv_sems, relay=True)

        def tile_load(slot, chip):
            return pltpu.make_async_copy(wt_ref.at[pl.ds(pl.multiple_of(chip * tn, tn), tn)], w_buf.at[slot],
                                         load_sems.at[slot])

        @pl.when(g == 0)
        def _():
            gather.start()

        @pl.when(g < nh)
        def _():
            xv = x_ref[...]
            r = lax.rsqrt(jnp.mean(xv * xv, axis=-1, keepdims=True) + NORM_EPS)
            hb = (((xv * r) * g_ref[...]) * (1.0 + scale_ref[...]) + shift_ref[...]).astype(BF16)
            h_ref[...] = hb
            h_scr[pl.ds(pl.multiple_of(g * th, th), th), :] = hb

        @pl.when(g == nh - 1)
        def _():
            gather.wait_sibling(0)
            tile_load(0, order_ref[0]).start()

        @pl.when(g >= nh)
        def _():
            t, i = (g - nh) // ni, (g - nh) % ni

            @pl.when(i == 0)
            def _():
                tile_load(t % 2, order_ref[t]).wait()

            @pl.when((i == ni - 1) & (t == 0))
            def _():
                gather.pass_on(0)
                gather.pass_on(1)
                gather.relay_diagonal()

            @pl.when((i == ni // 2) & (t == 2))
            def _():
                gather.pass_on(2)

            for j in range(3):
                @pl.when((i == ni - 1) & (t == j))
                def _():
                    gather.wait_sibling(4 + j)
                    tile_load((j + 1) % 2, order_ref[j + 1]).start()

            lhs = h_scr[pl.ds(pl.multiple_of(i * tm, tm), tm), :]
            proj_ref[...] = lax.dot_general(lhs, w_buf[t % 2], NT, preferred_element_type=F32).astype(BF16)

        @pl.when(g == steps - 1)
        def _():
            gather.wait_sends()

    def h_tile(g, order_ref):
        return (jnp.minimum(g, nh - 1), 0)

    def proj_tile(g, order_ref):
        mm = jnp.maximum(g - nh, 0)
        return (mm % ni, order_ref[mm // ni])

    row = pl.BlockSpec((1, D_MODEL), lambda g, order_ref: (0, 0))
    hbm = pl.BlockSpec(memory_space=pl.ANY)
    return pl.pallas_call(
        body, name="gather_in_proj",
        grid_spec=pltpu.PrefetchScalarGridSpec(
            num_scalar_prefetch=1, grid=(steps,),
            in_specs=[pl.BlockSpec((th, D_MODEL), h_tile), row, row, row, hbm],
            out_specs=[pl.BlockSpec((th, D_MODEL), h_tile), pl.BlockSpec((tm, tn), proj_tile), hbm],
            scratch_shapes=[pltpu.VMEM((s, D_MODEL), BF16), pltpu.VMEM((2, tn, D_MODEL), BF16),
                            pltpu.SemaphoreType.DMA((2,)), *_gather_scratch()]),
        out_shape=[SDS((s, D_MODEL), BF16), SDS((s, D_IN), BF16), SDS(wt_all.shape, BF16)],
        input_output_aliases={5: 2},
        compiler_params=_params("arbitrary"),
    )(order, x, shift, scale, norm_g, wt_all)


def _rope_freqs():
    inv_freq = ROPE_THETA ** (-jnp.arange(0, HEAD_DIM, 2, dtype=F32) / HEAD_DIM)
    return jnp.tile(inv_freq, 4).reshape(1, 128)


class _RopeTables:
    def __init__(self, freq_ref, rows_ref, state_ref, last_ref):
        self.freq, self.rows, self.state, self.last = freq_ref, rows_ref, state_ref, last_ref

    def start(self, block, direction):
        ang = lax.broadcasted_iota(jnp.int32, (CHUNK, 128), 0).astype(F32) * self.freq[...]
        self.rows[0] = jnp.cos(ang)
        self.rows[1] = jnp.sin(ang)
        base = jnp.asarray(block * CHUNK, dtype=F32) * self.freq[...]
        turn = float(direction * CHUNK) * self.freq[...]
        self.state[0:1, :] = jnp.cos(base)
        self.state[1:2, :] = jnp.sin(base)
        self.state[2:3, :] = jnp.cos(turn)
        self.state[3:4, :] = jnp.sin(turn)

    def step(self):
        c, s, ct, st = (self.state[k:k + 1, :] for k in range(4))
        self.state[0:1, :] = c * ct - s * st
        self.state[1:2, :] = s * ct + c * st

    def tables(self):
        c, s = self.state[0:1, :], self.state[1:2, :]
        cos = c * self.rows[0] - s * self.rows[1]
        sin = s * self.rows[0] + c * self.rows[1]
        first_half = (lax.broadcasted_iota(jnp.int32, (1, 128), 1) & (HEAD_DIM - 1)) < HEAD_DIM // 2
        return cos, jnp.where(first_half, -sin, 0.0), jnp.where(first_half, 0.0, sin)

    def keep(self, tabs):
        for k in range(3):
            self.last[k] = tabs[k]

    def kept(self):
        return tuple(self.last[k] for k in range(3))


def _rope_scratch():
    return [pltpu.VMEM((2, CHUNK, 128), F32), pltpu.VMEM((8, 128), F32), pltpu.VMEM((3, CHUNK, 128), F32)]


def _rope(v, cos, sin_lo, sin_hi):
    width = v.shape[1]
    rep = (1, width // 128)
    return (v * jnp.tile(cos, rep) + pltpu.roll(v, width - 32, 1) * jnp.tile(sin_lo, rep)
            + pltpu.roll(v, 32, 1) * jnp.tile(sin_hi, rep))


def _rope_bwd(d, cos, sin_lo, sin_hi):
    width = d.shape[1]
    rep = (1, width // 128)
    return (d * jnp.tile(cos, rep) + pltpu.roll(d * jnp.tile(sin_lo, rep), 32, 1)
            + pltpu.roll(d * jnp.tile(sin_hi, rep), width - 32, 1))


def _layer_norm(v, g, b):
    mu = jnp.mean(v, axis=-1, keepdims=True)
    vc = v - mu
    rstd = lax.rsqrt(jnp.mean(vc * vc, axis=-1, keepdims=True) + NORM_EPS)
    vhat = vc * rstd
    return vhat * g + b, vhat, rstd


def _tril_bf16(w_ref, g):
    t = lax.broadcasted_iota(jnp.int32, (CHUNK, CHUNK), 0)
    tp = lax.broadcasted_iota(jnp.int32, (CHUNK, CHUNK), 1)
    return jnp.where(tp <= t, w_ref[g], 0.0).astype(BF16)


def _bias_columns(b_ref, out_ref):
    for g in range(A_GROUPS):
        out_ref[g] = jnp.broadcast_to(b_ref[pl.ds(g, 1), :], (CHUNK, CHUNK)).T


def _from_prev():
    r = lax.broadcasted_iota(jnp.int32, (CHUNK, 4 * CHUNK), 0)
    i = lax.broadcasted_iota(jnp.int32, (CHUNK, 4 * CHUNK), 1) & (CHUNK - 1)
    return r > i


def _set_unfold_masks(mask_ref):
    prev = _from_prev()
    mask_ref[0] = jnp.where(prev, 1.0, 0.0).astype(BF16)
    mask_ref[1] = jnp.where(prev, 0.0, 1.0).astype(BF16)


def _fold_band(t, from_prev):
    return jnp.where(from_prev, t[:CHUNK], t[CHUNK:])


def _unfold_band(t, mask_ref):
    return jnp.concatenate([t * mask_ref[0], t * mask_ref[1]], axis=0)


def _low_lanes():
    return lax.broadcasted_iota(jnp.int32, (1, 128), 1) < HEAD_DIM


def _stack_heads(pair_a, pair_b):
    lo = _low_lanes()
    return jnp.concatenate([jnp.where(lo, pair_a, 0.0), jnp.where(lo, 0.0, pair_a),
                            jnp.where(lo, pair_b, 0.0), jnp.where(lo, 0.0, pair_b)], axis=0).astype(BF16)


def _heads_to_lanes(per_group):
    rows = [t[:, r * CHUNK:(r + 1) * CHUNK] for t in per_group for r in range(4)]
    return jnp.concatenate(rows, axis=0).T


def _dup_kv_head(band, gk):
    pair = band[:, (gk // 2) * 128:(gk // 2 + 1) * 128]
    lo = _low_lanes()
    one = jnp.where(lo if gk % 2 == 0 else jnp.logical_not(lo), pair, 0.0)
    return (one + pltpu.roll(one, HEAD_DIM, 1)).astype(BF16)


def _fold_kv_head(dup_grad, gk):
    both = dup_grad + pltpu.roll(dup_grad, HEAD_DIM, 1)
    lo = _low_lanes()
    return jnp.where(lo if gk % 2 == 0 else jnp.logical_not(lo), both, 0.0)


def _attn_probs(q_st, k_dup, sink_row, from_prev, first_block):
    s = lax.dot_general(k_dup, q_st, NT, preferred_element_type=F32)
    no_prev = jnp.where(first_block, -jnp.inf, 0.0)
    s = jnp.where(from_prev, s[:CHUNK] + no_prev, s[CHUNK:])
    m = jnp.maximum(jnp.max(s, axis=0, keepdims=True), sink_row)
    p = jnp.exp(s - m)
    e_sink = jnp.exp(sink_row - m)
    inv = 1.0 / (jnp.sum(p, axis=0, keepdims=True) + e_sink)
    return p * inv, e_sink * inv


def _sink_row(sinks_ref, gk):
    return jnp.concatenate([jnp.full((1, CHUNK), sinks_ref[4 * gk + r], F32) for r in range(4)], axis=1)


def _mixer_specs(nb, rev):
    def blk(i):
        return nb - 1 - i if rev else i

    def prev(i):
        return jnp.maximum(blk(i) - 1, 0)

    return dict(
        cur=pl.BlockSpec((CHUNK, D_IN), lambda i, *_: (blk(i), 0)),
        prev_kv=pl.BlockSpec((CHUNK, 2 * 256), lambda i, *_: (prev(i), OFF_K // 512)),
        freq=pl.BlockSpec((1, 128), lambda i, *_: (0, 0)),
        vec=pl.BlockSpec((1, D_A), lambda i, *_: (0, 0)),
        wsp=pl.BlockSpec((A_GROUPS, CHUNK, CHUNK), lambda i, *_: (0, 0, 0)),
        bsp=pl.BlockSpec((A_GROUPS, CHUNK), lambda i, *_: (0, 0)),
        smem=pl.BlockSpec(memory_space=pltpu.SMEM),
        blk=blk,
    )


def _mixer_fwd(proj, freqs, ln_g, ln_b, w_sp, b_sp, sinks, wo_all):
    s = proj.shape[0]
    nb = s // CHUNK
    sp = _mixer_specs(nb, rev=False)

    def body(cur_ref, pkv_ref, freq_ref, lg_ref, lb_ref, w_ref, b_ref, sinks_ref, wo_in, y_ref, wo_ref,
             bcol, mask, rope_rows, rope_state, rope_last, send_sems, recv_sems):
        i = pl.program_id(0)
        gather = _InPlaceGather(wo_ref, send_sems, recv_sems)
        rope = _RopeTables(freq_ref, rope_rows, rope_state, rope_last)

        @pl.when(i == 0)
        def _():
            gather.start()
            _bias_columns(b_ref, bcol)
            _set_unfold_masks(mask)
            rope.start(-1, 1)
            rope_last[...] = jnp.zeros_like(rope_last)

        @pl.when(i == nb // 2)
        def _():
            for j in range(3):
                gather.pass_on(j)

        vln, _, _ = _layer_norm(cur_ref[:, OFF_VA:OFF_ZA].astype(F32), lg_ref[...], lb_ref[...])
        vln = vln.astype(BF16)
        for g in range(A_GROUPS):
            cols = slice(g * 128, (g + 1) * 128)
            sg = jnp.dot(_tril_bf16(w_ref, g), vln[:, cols], preferred_element_type=F32) + bcol[g]
            u = cur_ref[:, OFF_U + g * 128:OFF_U + (g + 1) * 128].astype(F32)
            z = cur_ref[:, OFF_ZA + g * 128:OFF_ZA + (g + 1) * 128].astype(F32)
            y_ref[:, cols] = (u * sg * (z * _sigmoid(z))).astype(BF16)

        rope.step()
        cur_t, prev_t = rope.tables(), rope.kept()
        rope.keep(cur_t)
        qr = _rope(cur_ref[:, OFF_Q:OFF_K].astype(F32), *cur_t) * ATTN_SCALE
        kr = jnp.concatenate([_rope(pkv_ref[:, 0:256].astype(F32), *prev_t),
                              _rope(cur_ref[:, OFF_K:OFF_V].astype(F32), *cur_t)], axis=0)
        v_t = jnp.concatenate([pkv_ref[:, 256:512], cur_ref[:, OFF_V:OFF_ZB]], axis=0).astype(F32).T.astype(BF16)
        outs = []
        from_prev = _from_prev()
        for gk in range(N_KV_HEADS):
            q_st = _stack_heads(qr[:, (2 * gk) * 128:(2 * gk + 1) * 128], qr[:, (2 * gk + 1) * 128:(2 * gk + 2) * 128])
            probs, _ = _attn_probs(q_st, _dup_kv_head(kr, gk), _sink_row(sinks_ref, gk), from_prev, i == 0)
            outs.append(jnp.dot(v_t[gk * HEAD_DIM:(gk + 1) * HEAD_DIM], _unfold_band(probs.astype(BF16), mask),
                                preferred_element_type=F32))
        zb = cur_ref[:, OFF_ZB:D_IN].astype(F32)
        y_ref[:, D_A:D_MODEL] = (_heads_to_lanes(outs) * (zb * _sigmoid(zb))).astype(BF16)

        @pl.when(i == nb - 1)
        def _():
            gather.wait_sibling(0)
            for j in range(3):
                gather.wait_sibling(4 + j)
            gather.wait_sends()

    hbm = pl.BlockSpec(memory_space=pl.ANY)
    return pl.pallas_call(
        body, name="mixer_fwd", grid=(nb,),
        in_specs=[sp["cur"], sp["prev_kv"], sp["freq"], sp["vec"], sp["vec"], sp["wsp"], sp["bsp"], sp["smem"], hbm],
        out_specs=[pl.BlockSpec((CHUNK, D_MODEL), lambda i: (i, 0)), hbm],
        out_shape=[SDS((s, D_MODEL), BF16), SDS(wo_all.shape, wo_all.dtype)],
        scratch_shapes=[pltpu.VMEM((A_GROUPS, CHUNK, CHUNK), F32), pltpu.VMEM((2, CHUNK, 4 * CHUNK), BF16),
                        *_rope_scratch(), *_gather_scratch()],
        input_output_aliases={8: 1},
        compiler_params=_params("arbitrary"),
    )(proj, proj, freqs, ln_g, ln_b, w_sp, b_sp, sinks, wo_all)


def _out_proj_loss(y, x, target, wo, gate, shift_f, scale_f, fng):
    s = y.shape[0]
    tm, tr = 256, 128
    nt = s // tm

    def body(y_ref, x_ref, t_ref, wo_ref, gate_ref, sh_ref, sc_ref, g_ref, dx1_ref, do_ref, dy_ref, sums_ref,
             do_last, do_work):
        i = pl.program_id(0)

        @pl.when(i == 0)
        def _():
            sums_ref[...] = jnp.zeros_like(sums_ref)
            do_last[...] = jnp.zeros_like(do_last)

        do_work[...] = do_last[...]
        o = jnp.dot(y_ref[...], wo_ref[...], preferred_element_type=F32)
        gate, g, sh = gate_ref[...], g_ref[...], sh_ref[...]
        one_sc = 1.0 + sc_ref[...]
        cs, inv_d = g * one_sc, 1.0 / D_MODEL

        def rowsum(v):
            return jnp.sum(v, axis=0, keepdims=True)

        sums = [jnp.zeros((1, D_MODEL), F32) for _ in range(4)]
        for c in range(tm // tr):
            rows = slice(c * tr, (c + 1) * tr)
            oc = o[rows]
            x1 = x_ref[rows, :] + gate * oc
            r = lax.rsqrt(jnp.sum(x1 * x1, axis=-1, keepdims=True) * inv_d + NORM_EPS)
            x1n = x1 * r
            diff = x1n * cs + sh - t_ref[rows, :]
            w = diff * x1n
            lane_sum = jnp.sum(w * cs, axis=-1, keepdims=True)
            dx1 = (diff * cs) * (r * inv_d) - x1n * (r * lane_sum * (inv_d * inv_d))
            dx1_ref[rows, :] = dx1
            do = (dx1 * gate).astype(BF16)
            do_ref[rows, :] = do
            do_last[rows, :] = do
            for k, v in enumerate((dx1 * oc, diff, w, diff * diff)):
                sums[k] = sums[k] + rowsum(v)
        live = jnp.where(i < nt, 1.0, 0.0)
        for row, v in ((SUM_GATE, sums[0]), (SUM_SHIFT_F, inv_d * sums[1]), (SUM_SCALE_F, inv_d * (sums[2] * g)),
                       (SUM_FNG, inv_d * (sums[2] * one_sc)), (SUM_SQ_ERR, sums[3])):
            sums_ref[row:row + 1, :] += live * v
        dy_ref[...] = lax.dot_general(do_work[...], wo_ref[...], NT, preferred_element_type=F32).astype(BF16)

    tile = pl.BlockSpec((tm, D_MODEL), lambda i: (jnp.minimum(i, nt - 1), 0))
    row = pl.BlockSpec((1, D_MODEL), lambda i: (0, 0))
    return pl.pallas_call(
        body, name="out_proj_loss", grid=(nt + 1,),
        in_specs=[tile, tile, tile, pl.BlockSpec((D_MODEL, D_MODEL), lambda i: (0, 0)), row, row, row, row],
        out_specs=[tile, tile, pl.BlockSpec((tm, D_MODEL), lambda i: (jnp.maximum(i - 1, 0), 0)),
                   pl.BlockSpec((8, D_MODEL), lambda i: (0, 0))],
        out_shape=[SDS((s, D_MODEL), F32), SDS((s, D_MODEL), BF16), SDS((s, D_MODEL), BF16), SDS((8, D_MODEL), F32)],
        scratch_shapes=[pltpu.VMEM((tm, D_MODEL), BF16), pltpu.VMEM((tm, D_MODEL), BF16)],
        compiler_params=_params("arbitrary"),
    )(y, x, target, wo, gate, shift_f, scale_f, fng)


ROW_DBSP, ROW_DSINKS, MISC_ROWS = 0, 8, 32


def _mixer_bwd(me, proj, dy, freqs, ln_g, ln_b, w_sp, b_sp, sinks, pair):
    s = proj.shape[0]
    nb = s // CHUNK
    sp = _mixer_specs(nb, rev=True)

    def body(me_ref, cur_ref, pkv_ref, dy_ref, freq_ref, lg_ref, lb_ref, w_ref, b_ref, sinks_ref, pair_ref,
             dproj_ref, dln_ref, dw_ref, misc_ref, parts_ref, bcol, dbcol, carry, mask, rope_rows, rope_state,
             rope_last, send_sems, recv_sems):
        i = pl.program_id(0)
        block = nb - 1 - i
        rope = _RopeTables(freq_ref, rope_rows, rope_state, rope_last)

        @pl.when(i == 0)
        def _():
            for cp in _chip_scatter(pair_ref, parts_ref, send_sems, recv_sems):
                cp.start()
            _bias_columns(b_ref, bcol)
            _set_unfold_masks(mask)
            rope.start(nb - 1, -1)
            rope.keep(rope.tables())
            dbcol[...] = jnp.zeros_like(dbcol)
            carry[...] = jnp.zeros_like(carry)
            dln_ref[...] = jnp.zeros_like(dln_ref)
            dw_ref[...] = jnp.zeros_like(dw_ref)
            misc_ref[...] = jnp.zeros_like(misc_ref)

        vln, vhat, rstd = _layer_norm(cur_ref[:, OFF_VA:OFF_ZA].astype(F32), lg_ref[...], lb_ref[...])
        vln = vln.astype(BF16)
        d_vln = []
        for g in range(A_GROUPS):
            cols = slice(g * 128, (g + 1) * 128)
            w_g = _tril_bf16(w_ref, g)
            sg = jnp.dot(w_g, vln[:, cols], preferred_element_type=F32) + bcol[g]
            u = cur_ref[:, OFF_U + g * 128:OFF_U + (g + 1) * 128].astype(F32)
            z = cur_ref[:, OFF_ZA + g * 128:OFF_ZA + (g + 1) * 128].astype(F32)
            dya = dy_ref[:, cols].astype(F32)
            sig = _sigmoid(z)
            d_ya = dya * (z * sig)
            dproj_ref[:, OFF_ZA + g * 128:OFF_ZA + (g + 1) * 128] = (
                dya * (u * sg) * (sig * (1.0 + z * (1.0 - sig)))).astype(BF16)
            dproj_ref[:, OFF_U + g * 128:OFF_U + (g + 1) * 128] = (d_ya * sg).astype(BF16)
            d_s = d_ya * u
            dbcol[g] += d_s
            d_sb = d_s.astype(BF16)
            dw_ref[g] += lax.dot_general(d_sb, vln[:, cols], NT, preferred_element_type=F32)
            d_vln.append(lax.dot_general(w_g, d_sb, TN, preferred_element_type=F32))
        d_vln = jnp.concatenate(d_vln, axis=1)
        dln_ref[0:1, :] += jnp.sum(d_vln * vhat, axis=0, keepdims=True)
        dln_ref[1:2, :] += jnp.sum(d_vln, axis=0, keepdims=True)
        d_vhat = d_vln * lg_ref[...]
        d_va = rstd * (d_vhat - jnp.mean(d_vhat, axis=-1, keepdims=True)
                       - vhat * jnp.mean(d_vhat * vhat, axis=-1, keepdims=True))
        dproj_ref[:, OFF_VA:OFF_ZA] = d_va.astype(BF16)

        cur_t = rope.kept()
        rope.step()
        prev_t = rope.tables()
        rope.keep(prev_t)
        band_t = tuple(jnp.concatenate([p, c], axis=0) for p, c in zip(prev_t, cur_t))
        qr = _rope(cur_ref[:, OFF_Q:OFF_K].astype(F32), *cur_t) * ATTN_SCALE
        kr = jnp.concatenate([_rope(pkv_ref[:, 0:256].astype(F32), *prev_t),
                              _rope(cur_ref[:, OFF_K:OFF_V].astype(F32), *cur_t)], axis=0)
        vb = jnp.concatenate([pkv_ref[:, 256:512], cur_ref[:, OFF_V:OFF_ZB]], axis=0).astype(F32)
        k_t, v_t = (kr.T * ATTN_SCALE).astype(BF16), vb.T.astype(BF16)
        zb = cur_ref[:, OFF_ZB:D_IN].astype(F32)
        dyb = dy_ref[:, D_A:D_MODEL].astype(F32)
        sig = _sigmoid(zb)
        d_yb = dyb * (zb * sig)
        outs, dqs = [], []
        dk_pairs = [jnp.zeros((2 * CHUNK, 128), F32) for _ in range(2)]
        dv_pairs = [jnp.zeros((2 * CHUNK, 128), F32) for _ in range(2)]
        from_prev = _from_prev()
        for gk in range(N_KV_HEADS):
            heads = slice(gk * HEAD_DIM, (gk + 1) * HEAD_DIM)
            q_st = _stack_heads(qr[:, (2 * gk) * 128:(2 * gk + 1) * 128], qr[:, (2 * gk + 1) * 128:(2 * gk + 2) * 128])
            k_dup, v_dup = _dup_kv_head(kr, gk), _dup_kv_head(vb, gk)
            probs, p_sink = _attn_probs(q_st, k_dup, _sink_row(sinks_ref, gk), from_prev, block == 0)
            probs_b = _unfold_band(probs.astype(BF16), mask)
            outs.append(jnp.dot(v_t[heads], probs_b, preferred_element_type=F32))
            do_st = _stack_heads(d_yb[:, (2 * gk) * 128:(2 * gk + 1) * 128], d_yb[:, (2 * gk + 1) * 128:(2 * gk + 2) * 128])
            dp = _fold_band(lax.dot_general(v_dup, do_st, NT, preferred_element_type=F32), from_prev)
            delta = jnp.sum(probs * dp, axis=0, keepdims=True)
            ds = _unfold_band((probs * (dp - delta)).astype(BF16), mask)
            d_sink = -p_sink * delta
            for r in range(4):
                row = ROW_DSINKS + 4 * gk + r
                misc_ref[row:row + 1, :] += jnp.broadcast_to(
                    jnp.sum(d_sink[:, r * CHUNK:(r + 1) * CHUNK], axis=1, keepdims=True), (1, 128))
            dqs.append(jnp.dot(k_t[heads], ds, preferred_element_type=F32))
            dk_pairs[gk // 2] += _fold_kv_head(jnp.dot(ds, q_st, preferred_element_type=F32), gk)
            dv_pairs[gk // 2] += _fold_kv_head(jnp.dot(probs_b, do_st, preferred_element_type=F32), gk)
        dproj_ref[:, OFF_ZB:D_IN] = (dyb * _heads_to_lanes(outs) * (sig * (1.0 + zb * (1.0 - sig)))).astype(BF16)
        dproj_ref[:, OFF_Q:OFF_K] = _rope_bwd(_heads_to_lanes(dqs), *cur_t).astype(BF16)
        dk_band = _rope_bwd(jnp.concatenate(dk_pairs, axis=1), *band_t)
        dv_band = jnp.concatenate(dv_pairs, axis=1)
        dproj_ref[:, OFF_K:OFF_V] = (dk_band[CHUNK:] + carry[:, 0:256]).astype(BF16)
        dproj_ref[:, OFF_V:OFF_ZB] = (dv_band[CHUNK:] + carry[:, 256:512]).astype(BF16)
        carry[:, 0:256] = dk_band[:CHUNK]
        carry[:, 256:512] = dv_band[:CHUNK]

        @pl.when(i == nb - 1)
        def _():
            t = lax.broadcasted_iota(jnp.int32, (CHUNK, CHUNK), 0)
            tp = lax.broadcasted_iota(jnp.int32, (CHUNK, CHUNK), 1)
            for g in range(A_GROUPS):
                dw_ref[g] = jnp.where(tp <= t, dw_ref[g], 0.0)
                misc_ref[pl.ds(ROW_DBSP + g, 1), :] = jnp.sum(dbcol[g].T, axis=0, keepdims=True)
            scatter = _chip_scatter(pair_ref, parts_ref, send_sems, recv_sems)
            for cp in scatter:
                cp.wait_recv()
            for cp in scatter:
                cp.wait_send()

    blk = sp["blk"]
    hbm = pl.BlockSpec(memory_space=pl.ANY)
    return pl.pallas_call(
        body, name="mixer_bwd",
        grid_spec=pltpu.PrefetchScalarGridSpec(
            num_scalar_prefetch=1, grid=(nb,),
            in_specs=[sp["cur"], sp["prev_kv"], pl.BlockSpec((CHUNK, D_MODEL), lambda i, me_ref: (blk(i), 0)),
                      sp["freq"], sp["vec"], sp["vec"], sp["wsp"], sp["bsp"], sp["smem"], hbm],
            out_specs=[pl.BlockSpec((CHUNK, D_IN), lambda i, me_ref: (blk(i), 0)),
                       pl.BlockSpec((8, D_A), lambda i, me_ref: (me_ref[0], 0)),
                       pl.BlockSpec((A_GROUPS, CHUNK, CHUNK), lambda i, me_ref: (me_ref[0], 0, 0)),
                       pl.BlockSpec((MISC_ROWS, 128), lambda i, me_ref: (me_ref[0], 0)), hbm],
            scratch_shapes=[pltpu.VMEM((A_GROUPS, CHUNK, CHUNK), F32), pltpu.VMEM((A_GROUPS, CHUNK, CHUNK), F32),
                            pltpu.VMEM((CHUNK, 512), F32), pltpu.VMEM((2, CHUNK, 4 * CHUNK), BF16),
                            *_rope_scratch(), *_scatter_scratch()]),
        out_shape=[SDS((s, D_IN), BF16), SDS((N_DEV * 8, D_A), F32), SDS((N_DEV * A_GROUPS, CHUNK, CHUNK), F32),
                   SDS((N_DEV * MISC_ROWS, 128), F32), SDS((3,) + pair.shape[1:], pair.dtype)],
        compiler_params=_params("arbitrary"),
    )(me, proj, proj, dy, freqs, ln_g, ln_b, w_sp, b_sp, sinks, pair)


def _wgrad_pair(name, a, b, gathers=()):
    s, m = a.shape
    n = b.shape[1]
    bm, half = m // 4, m // 8
    bt = min(1024, s)
    steps = s // bt
    last = 4 * steps
    n_g = len(gathers)

    def body(*refs):
        a_ref, b_ref = refs[:2]
        out_ref, bufs = refs[2 + n_g], refs[3 + n_g:3 + 2 * n_g]
        acc, kept, got, sent, send_sems, recv_sems = refs[3 + 2 * n_g:9 + 2 * n_g]
        sems = refs[9 + 2 * n_g:]
        g = pl.program_id(0)
        tile, t = g // steps, g % steps
        mx, my, mc = _mesh_pos()
        jobs = [_InPlaceGather(bufs[k], sems[2 * k], sems[2 * k + 1]) for k in range(n_g)]

        def exchange(q):
            return pltpu.make_async_remote_copy(src_ref=sent, dst_ref=got.at[q % 2], send_sem=send_sems.at[q],
                                                recv_sem=recv_sems.at[q], device_id=(mx, my, 1 - mc),
                                                device_id_type=MESH)

        @pl.when(g == 0)
        def _():
            for job in jobs:
                job.start()

        @pl.when(g == 2 * steps)
        def _():
            for job in jobs:
                for j in range(3):
                    job.pass_on(j)

        @pl.when(g < last)
        def _():
            prod = lax.dot_general(a_ref[...], b_ref[...], TN, preferred_element_type=F32)

            @pl.when(t == 0)
            def _():
                acc[...] = prod

            @pl.when(t > 0)
            def _():
                acc[...] += prod

            @pl.when(t == steps - 1)
            def _():
                @pl.when(tile > 0)
                def _():
                    exchange(tile - 1).wait_send()

                kept[tile % 2] = acc[pl.ds(pl.multiple_of(mc * half, 8), half), :].astype(BF16)
                sent[...] = acc[pl.ds(pl.multiple_of((1 - mc) * half, 8), half), :].astype(BF16)
                exchange(tile).start()

        @pl.when((t == 0) & (g > 0))
        def _():
            q = tile - 1
            exchange(q).wait_recv()
            out_ref[0] = (kept[q % 2].astype(F32) + got[q % 2].astype(F32)).astype(BF16)

        @pl.when(g == last)
        def _():
            exchange(3).wait_send()
            for job in jobs:
                job.wait_sibling(0)
                for j in range(3):
                    job.wait_sibling(4 + j)
                job.wait_sends()

    def a_tile(g):
        gg = jnp.minimum(g, last - 1)
        return (gg % steps, gg // steps)

    def b_tile(g):
        return (jnp.minimum(g, last - 1) % steps, 0)

    hbm = pl.BlockSpec(memory_space=pl.ANY)
    outs = pl.pallas_call(
        body, name=name, grid=(last + 1,),
        in_specs=[pl.BlockSpec((bt, bm), a_tile), pl.BlockSpec((bt, n), b_tile)] + [hbm] * n_g,
        out_specs=[pl.BlockSpec((1, half, n), lambda g: (jnp.maximum(g - 1, 0) // steps, 0, 0))] + [hbm] * n_g,
        out_shape=[SDS((4, half, n), BF16)] + [SDS(gb.shape, gb.dtype) for gb in gathers],
        scratch_shapes=[pltpu.VMEM((bm, n), F32), pltpu.VMEM((2, half, n), BF16), pltpu.VMEM((2, half, n), BF16),
                        pltpu.VMEM((half, n), BF16), pltpu.SemaphoreType.DMA((4,)), pltpu.SemaphoreType.DMA((4,))]
        + _gather_scratch() * n_g,
        input_output_aliases={2 + k: 1 + k for k in range(n_g)},
        compiler_params=_params("arbitrary"),
    )(a, b, *gathers)
    return outs[0], outs[1:]


def _in_proj_bwd(dproj, wt, x, dx1, scale, norm_g, sums_o, pair):
    s = x.shape[0]
    tm, tk, tr = min(1024, s), D_IN // 4, 64
    ksteps = D_IN // tk

    def body(dp_ref, wt_ref, x_hbm, dx1_hbm, sc_ref, g_ref, so_ref, pair_ref, gx_ref, sums_ref, parts_ref, x_buf,
             dx1_buf, tile_sems, send_sems, recv_sems):
        i, k = pl.program_id(0), pl.program_id(1)

        def tile_copies():
            rows = pl.ds(pl.multiple_of(i * tm, tm), tm)
            return (pltpu.make_async_copy(x_hbm.at[rows], x_buf, tile_sems.at[0]),
                    pltpu.make_async_copy(dx1_hbm.at[rows], dx1_buf, tile_sems.at[1]))

        @pl.when((i == 0) & (k == 0))
        def _():
            for cp in _chip_scatter(pair_ref, parts_ref, send_sems, recv_sems):
                cp.start()
            sums_ref[...] = so_ref[...]

        @pl.when(k == 0)
        def _():
            for cp in tile_copies():
                cp.start()
            gx_ref[...] = jnp.dot(dp_ref[...], wt_ref[...], preferred_element_type=F32)

        @pl.when(k > 0)
        def _():
            gx_ref[...] += jnp.dot(dp_ref[...], wt_ref[...], preferred_element_type=F32)

        @pl.when(k == ksteps - 1)
        def _():
            for cp in tile_copies():
                cp.wait()
            one_sc, g = 1.0 + sc_ref[...], g_ref[...]
            cs = one_sc * g

            def chunk(j, sums):
                rows = pl.ds(pl.multiple_of(j * tr, tr), tr)
                dh, xv = gx_ref[rows, :], x_buf[rows, :]
                dhx = dh * xv
                r = lax.rsqrt(jnp.sum(xv * xv, axis=-1, keepdims=True) * (1.0 / D_MODEL) + NORM_EPS)
                coef = (r * r * r) * (jnp.sum(dhx * cs, axis=-1, keepdims=True) * (1.0 / D_MODEL))
                gx_ref[rows, :] = dx1_buf[rows, :] + r * (dh * cs) - xv * coef
                return (sums[0] + jnp.sum(dh, axis=0, keepdims=True), sums[1] + jnp.sum(dhx * r, axis=0, keepdims=True))

            zero = jnp.zeros((1, D_MODEL), F32)
            sums = lax.fori_loop(0, tm // tr, chunk, (zero, zero))
            sums_ref[SUM_SHIFT:SUM_SHIFT + 1, :] += sums[0]
            sums_ref[SUM_SCALE:SUM_SCALE + 1, :] += sums[1] * g
            sums_ref[SUM_NORM_G:SUM_NORM_G + 1, :] += sums[1] * one_sc

        @pl.when((i == s // tm - 1) & (k == ksteps - 1))
        def _():
            scatter = _chip_scatter(pair_ref, parts_ref, send_sems, recv_sems)
            for cp in scatter:
                cp.wait_recv()
            for cp in scatter:
                cp.wait_send()

    row = pl.BlockSpec((1, D_MODEL), lambda i, k: (0, 0))
    hbm = pl.BlockSpec(memory_space=pl.ANY)
    return pl.pallas_call(
        body, name="in_proj_bwd", grid=(s // tm, ksteps),
        in_specs=[pl.BlockSpec((tm, tk), lambda i, k: (i, k)), pl.BlockSpec((tk, D_MODEL), lambda i, k: (k, 0)),
                  hbm, hbm, row, row, pl.BlockSpec((8, D_MODEL), lambda i, k: (0, 0)), hbm],
        out_specs=[pl.BlockSpec((tm, D_MODEL), lambda i, k: (i, 0)), pl.BlockSpec((8, D_MODEL), lambda i, k: (0, 0)),
                   hbm],
        out_shape=[SDS((s, D_MODEL), F32), SDS((8, D_MODEL), F32), SDS((3,) + pair.shape[1:], pair.dtype)],
        scratch_shapes=[pltpu.VMEM((tm, D_MODEL), F32), pltpu.VMEM((tm, D_MODEL), F32),
                        pltpu.SemaphoreType.DMA((2,)), *_scatter_scratch()],
        compiler_params=_params("arbitrary", "arbitrary"),
    )(dproj, wt, x, dx1, scale, norm_g, sums_o, pair)


def _sum_chips(own_ref, parts_ref):
    return ((own_ref[0].astype(F32) + parts_ref[0].astype(F32)) + parts_ref[1].astype(F32)) + parts_ref[2].astype(F32)


def _adam_rows(name, chip, pair, parts, w, m, v, tr):
    rows = w.shape[0]

    def body(chip_ref, own_ref, p_ref, w_ref, m_ref, v_ref, g_ref, d_ref, nm_ref, nv_ref):
        g = _sum_chips(own_ref, p_ref)
        g_ref[...] = g
        d_ref[...], nm_ref[...], nv_ref[...] = _adamw(w_ref[...], g, m_ref[...], v_ref[...])

    blk = pl.BlockSpec((tr, D_MODEL), lambda j, chip_ref: (j, 0))
    return pl.pallas_call(
        body, name=name,
        grid_spec=pltpu.PrefetchScalarGridSpec(
            num_scalar_prefetch=1, grid=(rows // tr,),
            in_specs=[pl.BlockSpec((1, tr, D_MODEL), lambda j, chip_ref: (chip_ref[0], j, 0)),
                      pl.BlockSpec((3, tr, D_MODEL), lambda j, chip_ref: (0, j, 0)), blk, blk, blk],
            out_specs=[blk] * 4),
        out_shape=[SDS(w.shape, F32)] * 4, compiler_params=_params("parallel"),
    )(chip, pair, parts, w, m, v)


def _adam_ada(name, cact, dmod, w, m, v):
    n = w.shape[1]
    tr = 512

    def body(c_ref, dm_ref, w_ref, m_ref, v_ref, g_ref, d_ref, nm_ref, nv_ref):
        pad_c = jnp.concatenate([c_ref[...], jnp.zeros_like(c_ref)], axis=0).astype(BF16)
        pad_d = jnp.concatenate([dm_ref[...], jnp.zeros_like(dm_ref)], axis=0).astype(BF16)
        g = lax.dot_general(pad_c, pad_d, TN, preferred_element_type=F32)
        g_ref[...] = g
        d_ref[...], nm_ref[...], nv_ref[...] = _adamw(w_ref[...], g, m_ref[...], v_ref[...])

    blk = pl.BlockSpec((tr, n), lambda j: (j, 0))
    return pl.pallas_call(
        body, name=name, grid=(D_MODEL // tr,),
        in_specs=[pl.BlockSpec((N_DEV, tr), lambda j: (0, j)), pl.BlockSpec((N_DEV, n), lambda j: (0, 0)),
                  blk, blk, blk],
        out_specs=[blk] * 4, out_shape=[SDS(w.shape, F32)] * 4,
        compiler_params=_params("parallel"),
    )(cact, dmod, w, m, v)


SMALL_PARAMS = ("w_spatial", "b_spatial", "sinks", "norm_g", "ln_v_g", "ln_v_b", "final_norm_g", "b_ada", "b_ada_final")


def _adam_small(d_wsp, misc, d_ln, sums, params):
    n_p = len(SMALL_PARAMS)

    def body(*refs):
        wsp_ref, misc_ref, ln_ref, sums_ref = refs[:4]
        wmv = [refs[4 + 3 * k:7 + 3 * k] for k in range(n_p)]
        loss_ref = refs[4 + 3 * n_p]
        outs = [refs[5 + 3 * n_p + 4 * k:9 + 3 * n_p + 4 * k] for k in range(n_p)]

        def column_sum(row):
            return total(sums_ref, (row, row + 1))

        def total(ref, rows=None):
            def part(j):
                return ref[j] if rows is None else ref[j, rows[0]:rows[1], :]
            acc = part(0)
            for j in range(1, N_DEV):
                acc = acc + part(j)
            return acc

        sink_rows = total(misc_ref, (ROW_DSINKS, ROW_DSINKS + 16))
        diag = (lax.broadcasted_iota(jnp.int32, (16, 128), 0) == lax.broadcasted_iota(jnp.int32, (16, 128), 1))
        grads = dict(
            w_spatial=total(wsp_ref), b_spatial=total(misc_ref, (ROW_DBSP, ROW_DBSP + A_GROUPS)),
            sinks=jnp.sum(jnp.where(diag, sink_rows, 0.0), axis=0, keepdims=True),
            norm_g=column_sum(SUM_NORM_G), ln_v_g=total(ln_ref, (0, 1)), ln_v_b=total(ln_ref, (1, 2)),
            final_norm_g=column_sum(SUM_FNG),
            b_ada=jnp.concatenate([column_sum(SUM_SHIFT), column_sum(SUM_SCALE), column_sum(SUM_GATE)], axis=1),
            b_ada_final=jnp.concatenate([column_sum(SUM_SHIFT_F), column_sum(SUM_SCALE_F)], axis=1))
        sq_err = jnp.sum(column_sum(SUM_SQ_ERR), axis=1, keepdims=True)
        loss_ref[...] = jnp.broadcast_to(sq_err * (0.5 / D_MODEL), (1, 128))
        for k, name in enumerate(SMALL_PARAMS):
            w_ref, m_ref, v_ref = wmv[k]
            g_ref, d_ref, nm_ref, nv_ref = outs[k]
            g_ref[...] = grads[name]
            d_ref[...], nm_ref[...], nv_ref[...] = _adamw(w_ref[...], grads[name], m_ref[...], v_ref[...])

    flat = [a for name in SMALL_PARAMS for a in params[name]]
    vmem = pl.BlockSpec(memory_space=pltpu.VMEM)
    out_shape = [SDS((1, 128), F32)] + [SDS(params[name][0].shape, F32) for name in SMALL_PARAMS for _ in range(4)]
    outs = pl.pallas_call(
        body, name="adam_small", in_specs=[vmem] * (4 + len(flat)), out_specs=[vmem] * len(out_shape),
        out_shape=out_shape, compiler_params=_params(),
    )(d_wsp, misc, d_ln, sums, *flat)
    return outs[0], {name: outs[1 + 4 * k:5 + 4 * k] for k, name in enumerate(SMALL_PARAMS)}


def kernel(x, c, w_ada, b_ada, norm_g, w_in, ln_v_g, ln_v_b, w_spatial, b_spatial, sinks, w_out, w_ada_final, b_ada_final, final_norm_g, loss_target, m_w_ada, m_b_ada, m_norm_g, m_w_in, m_ln_v_g, m_ln_v_b, m_w_spatial, m_b_spatial, m_sinks, m_w_out, m_w_ada_final, m_b_ada_final, m_final_norm_g, v_w_ada, v_b_ada, v_norm_g, v_w_in, v_ln_v_g, v_ln_v_b, v_w_spatial, v_b_spatial, v_sinks, v_w_out, v_w_ada_final, v_b_ada_final, v_final_norm_g):
    seq = x.shape[1]
    me = 4 * lax.axis_index("x") + 2 * lax.axis_index("y") + lax.axis_index("c")
    x2, tgt = x[0], loss_target[0]
    fng = final_norm_g.reshape(1, D_MODEL)

    n_ada, n_ada_f = w_ada.shape[2], w_ada_final.shape[1]
    cact, mod, mod_f = _ada_exchange(c, w_ada[0], b_ada.reshape(N_DEV, n_ada), w_ada_final,
                                     b_ada_final.reshape(N_DEV, n_ada_f))
    cact = cact.reshape(N_DEV, D_MODEL)
    mod, mod_f = mod.reshape(1, 3 * D_MODEL), mod_f.reshape(1, 2 * D_MODEL)
    shift, scale, gate = mod[:, :D_MODEL], mod[:, D_MODEL:2 * D_MODEL], mod[:, 2 * D_MODEL:]
    shift_f, scale_f = mod_f[:, :D_MODEL], mod_f[:, D_MODEL:]

    wt_f32, m_wt, v_wt = (jnp.swapaxes(a, 1, 2)[0] for a in (w_in, m_w_in, v_w_in))
    xi, yi = lax.axis_index("x"), lax.axis_index("y")
    chip_order = jnp.stack([2 * xi + yi, 2 * (1 - xi) + yi, 2 * xi + 1 - yi, 2 * (1 - xi) + 1 - yi]).astype(jnp.int32)
    wt_mine, wo_mine = _prep_weights(me.reshape(1), wt_f32, w_out[0])

    freqs = _rope_freqs()
    sinks_v = sinks.reshape(16)
    h, proj, wt = _gather_in_proj(chip_order, x2, shift, scale, norm_g, wt_mine)
    y, wo = _mixer_fwd(proj, freqs, ln_v_g, ln_v_b, w_spatial[0], b_spatial[0], sinks_v, wo_mine)
    dx1, do, dy, sums_o = _out_proj_loss(y, x2, tgt, wo, gate, shift_f, scale_f, fng)

    chip = (2 * lax.axis_index("x") + lax.axis_index("y")).reshape(1)
    pair_out, _ = _wgrad_pair("wgrad_out", y, do)
    dproj, d_ln, d_wsp, misc, parts_out = _mixer_bwd(
        me.reshape(1), proj, dy, freqs, ln_v_g, ln_v_b, w_spatial[0], b_spatial[0], sinks_v, pair_out)
    pair_in, (d_ln, d_wsp, misc) = _wgrad_pair(
        "wgrad_in", dproj, h, gathers=(d_ln, d_wsp.reshape(N_DEV * A_GROUPS * CHUNK, CHUNK), misc))
    grad_x, sums, parts_in = _in_proj_bwd(dproj, wt, x2, dx1, scale, norm_g, sums_o, pair_in)
    wt_leaves = [jnp.swapaxes(a[None], 1, 2)
                 for a in _adam_rows("adam_w_in", chip, pair_in, parts_in, wt_f32, m_wt, v_wt, 176)]
    w_out_leaves = [a[None] for a in _adam_rows("adam_w_out", chip, pair_out, parts_out, w_out[0], m_w_out[0], v_w_out[0], 64)]

    (sums,) = _all_gather("gather_sums", [sums], pltpu.VMEM)
    natural = dict(w_spatial=(A_GROUPS * CHUNK, CHUNK), b_spatial=(A_GROUPS, CHUNK), sinks=(1, 16), norm_g=(1, D_MODEL),
                   ln_v_g=(1, D_A), ln_v_b=(1, D_A), final_norm_g=(1, D_MODEL), b_ada=(1, 3 * D_MODEL),
                   b_ada_final=(1, 2 * D_MODEL))
    given = dict(
        w_spatial=(w_spatial, m_w_spatial, v_w_spatial), b_spatial=(b_spatial, m_b_spatial, v_b_spatial),
        sinks=(sinks, m_sinks, v_sinks), norm_g=(norm_g, m_norm_g, v_norm_g), ln_v_g=(ln_v_g, m_ln_v_g, v_ln_v_g),
        ln_v_b=(ln_v_b, m_ln_v_b, v_ln_v_b), final_norm_g=(final_norm_g, m_final_norm_g, v_final_norm_g),
        b_ada=(b_ada, m_b_ada, v_b_ada), b_ada_final=(b_ada_final, m_b_ada_final, v_b_ada_final))
    params = {name: tuple(a.reshape(natural[name]) for a in given[name]) for name in SMALL_PARAMS}
    params["sinks"] = tuple(jnp.pad(a, ((0, 0), (0, 128 - 16))) for a in params["sinks"])
    loss, small = _adam_small(d_wsp.reshape(N_DEV, A_GROUPS * CHUNK, CHUNK), misc.reshape(N_DEV, MISC_ROWS, 128),
                              d_ln.reshape(N_DEV, 8, D_A), sums, params)
    small["sinks"] = [a[:, :16] for a in small["sinks"]]
    small = {name: [a.reshape(given[name][0].shape) for a in small[name]] for name in SMALL_PARAMS}

    dmod_all = jnp.concatenate([sums[:, SUM_SHIFT], sums[:, SUM_SCALE], sums[:, SUM_GATE]], axis=1)
    dmod_f_all = jnp.concatenate([sums[:, SUM_SHIFT_F], sums[:, SUM_SCALE_F]], axis=1)
    dmod_mine = lax.dynamic_slice_in_dim(dmod_all, me * n_ada, n_ada, axis=1)
    dmod_f_mine = lax.dynamic_slice_in_dim(dmod_f_all, me * n_ada_f, n_ada_f, axis=1)
    ada = _adam_ada("adam_w_ada", cact, dmod_mine, w_ada[0], m_w_ada[0], v_w_ada[0])
    ada_f = _adam_ada("adam_w_ada_final", cact, dmod_f_mine, w_ada_final, m_w_ada_final, v_w_ada_final)

    def leaves(k):
        return (ada[k][None], small["b_ada"][k], small["norm_g"][k], wt_leaves[k], small["ln_v_g"][k],
                small["ln_v_b"][k], small["w_spatial"][k], small["b_spatial"][k], small["sinks"][k], w_out_leaves[k],
                ada_f[k], small["b_ada_final"][k], small["final_norm_g"][k])

    return (loss[0, 0], grad_x[None], *leaves(0), *leaves(1), *leaves(2), *leaves(3))
```

```python
import jax
import jax.numpy as jnp
from jax import lax
from jax.experimental import pallas as pl
from jax.experimental.pallas import tpu as pltpu

D_MODEL = 2048
D_IN = 5632
D_A = 1024
CHUNK = 128
A_GROUPS = 8
HEAD_DIM = 64
N_KV_HEADS = 4
N_DEV = 8
ROPE_THETA = 10000.0
NORM_EPS = 1e-5
ATTN_SCALE = HEAD_DIM ** -0.5

ADAM_LR = 0.001
ADAM_B1 = 0.9
ADAM_B2 = 0.999
ADAM_EPS = 1e-08
ADAM_WD = 0.01
ADAM_STEP = 10

OFF_U, OFF_VA, OFF_ZA, OFF_Q, OFF_K, OFF_V, OFF_ZB = 0, 1024, 2048, 3072, 4096, 4352, 4608

SUM_SHIFT, SUM_SCALE, SUM_NORM_G, SUM_GATE, SUM_SHIFT_F, SUM_SCALE_F, SUM_FNG, SUM_SQ_ERR = range(8)

V7X_VMEM_LIMIT_BYTES = 56 * 1024 * 1024

F32 = jnp.float32
BF16 = jnp.bfloat16
MESH = pl.DeviceIdType.MESH
SDS = jax.ShapeDtypeStruct
NT = (((1,), (1,)), ((), ()))
TN = (((0,), (0,)), ((), ()))


def _params(*semantics):
    return pltpu.CompilerParams(dimension_semantics=semantics or None, vmem_limit_bytes=V7X_VMEM_LIMIT_BYTES)


def _mesh_pos():
    return lax.axis_index("x"), lax.axis_index("y"), lax.axis_index("c")


def _sigmoid(z):
    return 1.0 / (1.0 + jnp.exp(-z))


def _adamw(w, g, m, v):
    m = ADAM_B1 * m + (1.0 - ADAM_B1) * g
    v = ADAM_B2 * v + (1.0 - ADAM_B2) * (g * g)
    m_hat = m / (1.0 - ADAM_B1 ** ADAM_STEP)
    v_hat = v / (1.0 - ADAM_B2 ** ADAM_STEP)
    delta = -ADAM_LR * (m_hat / (jnp.sqrt(v_hat) + ADAM_EPS) + ADAM_WD * w)
    return delta, m, v


def _all_gather(name, blocks, memory_space):
    n_arr = len(blocks)

    def body(*refs):
        ins, outs = refs[:n_arr], refs[n_arr:2 * n_arr]
        send_sems, recv_sems, local_sems = refs[2 * n_arr:]
        x, y, c = _mesh_pos()
        me, sibling = (x, y, c), (x, y, 1 - c)
        chips = [(1 - x, y), (x, 1 - y), (1 - x, 1 - y)]

        def slot(p):
            return 4 * p[0] + 2 * p[1] + p[2]

        def copy(a, k, block, to, src=None):
            dst = outs[a].at[slot(block)]
            return pltpu.make_async_remote_copy(
                src_ref=dst if src is None else src, dst_ref=dst,
                send_sem=send_sems.at[a, k], recv_sem=recv_sems.at[a, k],
                device_id=to, device_id_type=MESH)

        mine = [pltpu.make_async_copy(ins[a], outs[a].at[slot(me)], local_sems.at[a]) for a in range(n_arr)]
        for cp in mine:
            cp.start()
        first = []
        for a in range(n_arr):
            first.append(copy(a, 0, me, sibling, src=ins[a]))
            first += [copy(a, 1 + j, me, (*chip, c), src=ins[a]) for j, chip in enumerate(chips)]
        for cp in first:
            cp.start()
        passed = []
        for j, chip in enumerate(chips):
            for a in range(n_arr):
                copy(a, 1 + j, (*chip, c), me).wait_recv()
                fwd = copy(a, 4 + j, (*chip, c), sibling)
                fwd.start()
                passed.append(fwd)
        for a in range(n_arr):
            copy(a, 0, sibling, me).wait_recv()
            for j, chip in enumerate(chips):
                copy(a, 4 + j, (*chip, 1 - c), me).wait_recv()
        for cp in first + passed:
            cp.wait_send()
        for cp in mine:
            cp.wait()

    spec = pl.BlockSpec(memory_space=memory_space)
    return pl.pallas_call(
        body, name=name,
        out_shape=[SDS((N_DEV,) + b.shape, b.dtype) for b in blocks],
        in_specs=[spec] * n_arr, out_specs=[spec] * n_arr,
        scratch_shapes=[pltpu.SemaphoreType.DMA((n_arr, 7)), pltpu.SemaphoreType.DMA((n_arr, 7)),
                        pltpu.SemaphoreType.DMA((n_arr,))],
        compiler_params=_params(),
    )(*blocks)


def _ada_exchange(c, w_ada, b_ada8, w_ada_f, b_ada_f8):
    n1, n2 = w_ada.shape[1], w_ada_f.shape[1]

    def body(c_ref, w1_ref, b1_ref, w2_ref, b2_ref, cact_ref, mod_ref, modf_ref,
             cact_buf, res1, res2, send1, send2, sems_s, sems_r):
        x, y, c_pos = _mesh_pos()
        me = 4 * x + 2 * y + c_pos
        flips = [(k >> 2 & 1, k >> 1 & 1, k & 1) for k in range(1, N_DEV)]

        def peer(f):
            return (1 - x if f[0] else x, 1 - y if f[1] else y, 1 - c_pos if f[2] else c_pos)

        cv = c_ref[...]
        cact = cv * _sigmoid(cv)
        cact_buf[...] = cact
        cact_ref[me] = cact

        def rdma(phase, k, src, dst, f):
            return pltpu.make_async_remote_copy(src_ref=src, dst_ref=dst, send_sem=sems_s.at[phase, k],
                                                recv_sem=sems_r.at[phase, k], device_id=peer(f), device_id_type=MESH)

        gather = [rdma(0, k, cact_buf, cact_ref.at[me], f) for k, f in enumerate(flips)]
        for cp in gather:
            cp.start()
        for cp in gather:
            cp.wait_recv()
        for cp in gather:
            cp.wait_send()

        rid = lax.broadcasted_iota(jnp.int32, (N_DEV, D_MODEL), 0)
        rows = jnp.zeros((N_DEV, D_MODEL), F32)
        for j in range(N_DEV):
            rows = jnp.where(rid == j, jnp.broadcast_to(cact_ref[j], (N_DEV, D_MODEL)), rows)
        rows = rows.astype(BF16)
        res1[...] = jnp.dot(rows, w1_ref[...].astype(BF16), preferred_element_type=F32) + b1_ref[pl.ds(me, 1), :]
        res2[...] = jnp.dot(rows, w2_ref[...].astype(BF16), preferred_element_type=F32) + b2_ref[pl.ds(me, 1), :]
        for j in range(N_DEV):
            send1[j] = res1[pl.ds(j, 1), :]
            send2[j] = res2[pl.ds(j, 1), :]
        mod_ref[me] = send1[me]
        modf_ref[me] = send2[me]
        scatter = []
        for k, f in enumerate(flips):
            to = me ^ (k + 1)
            scatter.append(rdma(1, k, send1.at[to], mod_ref.at[me], f))
            scatter.append(rdma(2, k, send2.at[to], modf_ref.at[me], f))
        for cp in scatter:
            cp.start()
        for cp in scatter:
            cp.wait_recv()
        for cp in scatter:
            cp.wait_send()

    vmem = pl.BlockSpec(memory_space=pltpu.VMEM)
    return pl.pallas_call(
        body, name="ada_exchange",
        out_shape=[SDS((N_DEV, 1, D_MODEL), F32), SDS((N_DEV, 1, n1), F32), SDS((N_DEV, 1, n2), F32)],
        in_specs=[vmem] * 5, out_specs=[vmem] * 3,
        scratch_shapes=[pltpu.VMEM((1, D_MODEL), F32), pltpu.VMEM((N_DEV, n1), F32), pltpu.VMEM((N_DEV, n2), F32),
                        pltpu.VMEM((N_DEV, 1, n1), F32), pltpu.VMEM((N_DEV, 1, n2), F32),
                        pltpu.SemaphoreType.DMA((3, 7)), pltpu.SemaphoreType.DMA((3, 7))],
        compiler_params=_params(),
    )(c, w_ada, b_ada8, w_ada_f, b_ada_f8)


def _chip_scatter(pair_ref, parts_ref, send_sems, recv_sems):
    x, y, c = _mesh_pos()
    chips = [(1 - x, y), (x, 1 - y), (1 - x, 1 - y)]
    return [pltpu.make_async_remote_copy(
        src_ref=pair_ref.at[2 * cx + cy], dst_ref=parts_ref.at[j], send_sem=send_sems.at[j], recv_sem=recv_sems.at[j],
        device_id=(cx, cy, c), device_id_type=MESH) for j, (cx, cy) in enumerate(chips)]


def _scatter_scratch():
    return [pltpu.SemaphoreType.DMA((3,)), pltpu.SemaphoreType.DMA((3,))]


def _prep_weights(me, wt, w_out):
    steps = 4

    def body(me_ref, wt_ref, wo_ref, wtb_ref, wob_ref):
        wtb_ref[...] = wt_ref[...].astype(BF16)
        wob_ref[...] = wo_ref[...].astype(BF16)

    def rows(a, mine):
        blk = (a.shape[0] // steps, a.shape[1])
        return pl.BlockSpec(blk, (lambda i, me_ref: (steps * me_ref[0] + i, 0)) if mine else (lambda i, me_ref: (i, 0)))

    return pl.pallas_call(
        body, name="prep_weights",
        grid_spec=pltpu.PrefetchScalarGridSpec(
            num_scalar_prefetch=1, grid=(steps,),
            in_specs=[rows(wt, False), rows(w_out, False)], out_specs=[rows(wt, True), rows(w_out, True)]),
        out_shape=[SDS((N_DEV * wt.shape[0], D_MODEL), BF16), SDS((N_DEV * w_out.shape[0], D_MODEL), BF16)],
        compiler_params=_params("parallel"),
    )(me, wt, w_out)


class _InPlaceGather:
    def __init__(self, buf_ref, send_sems, recv_sems, relay=False):
        self.buf, self.send_sems, self.recv_sems, self.relay = buf_ref, send_sems, recv_sems, relay
        self.n = buf_ref.shape[0] // N_DEV
        x, y, c = _mesh_pos()
        self.me, self.sibling, self.core = (x, y, c), (x, y, 1 - c), c
        self.chips = [(1 - x, y), (x, 1 - y), (1 - x, 1 - y)]
        self.relay_from = (jnp.where(c == 0, 1 - x, x), jnp.where(c == 0, y, 1 - y), c)
        self.relay_to = (jnp.where(c == 0, x, 1 - x), jnp.where(c == 0, 1 - y, y), c)

    def copy(self, k, block, to):
        start = pl.multiple_of((4 * block[0] + 2 * block[1] + block[2]) * self.n, self.n)
        rows = self.buf.at[pl.ds(start, self.n)]
        return pltpu.make_async_remote_copy(src_ref=rows, dst_ref=rows, send_sem=self.send_sems.at[k],
                                            recv_sem=self.recv_sems.at[k], device_id=to, device_id_type=MESH)

    def start(self):
        self.copy(0, self.me, self.sibling).start()
        for j, chip in enumerate(self.chips[:2] if self.relay else self.chips):
            self.copy(1 + j, self.me, (*chip, self.core)).start()

    def relay_diagonal(self):
        self.copy(3, self.relay_from, self.relay_to).start()

    def pass_on(self, j):
        self.copy(1 + j, (*self.chips[j], self.core), self.me).wait_recv()
        self.copy(4 + j, (*self.chips[j], self.core), self.sibling).start()

    def wait_sibling(self, k):
        self.copy(k, self.sibling, self.me).wait_recv()

    def wait_sends(self):
        for k in range(7):
            self.copy(k, self.me, self.sibling).wait_send()


def _gather_scratch():
    return [pltpu.SemaphoreType.DMA((7,)), pltpu.SemaphoreType.DMA((7,))]


def _gather_in_proj(order, x, shift, scale, norm_g, wt_all):
    s = x.shape[0]
    th = tm = min(512, s)
    nh, ni = s // th, s // tm
    tn = D_IN // 4
    steps = nh + 4 * ni

    def body(order_ref, x_ref, shift_ref, scale_ref, g_ref, wt_in, h_ref, proj_ref, wt_ref,
             h_scr, w_buf, load_sems, send_sems, recv_sems):
        g = pl.program_id(0)
        gather = _InPlaceGather(wt_ref, send_sems, recv_sems, relay=True)

        def tile_load(slot, chip):
            return pltpu.make_async_copy(wt_ref.at[pl.ds(pl.multiple_of(chip * tn, tn), tn)], w_buf.at[slot],
                                         load_sems.at[slot])

        @pl.when(g == 0)
        def _():
            gather.start()

        @pl.when(g < nh)
        def _():
            xv = x_ref[...]
            r = lax.rsqrt(jnp.mean(xv * xv, axis=-1, keepdims=True) + NORM_EPS)
            hb = (((xv * r) * g_ref[...]) * (1.0 + scale_ref[...]) + shift_ref[...]).astype(BF16)
            h_ref[...] = hb
            h_scr[pl.ds(pl.multiple_of(g * th, th), th), :] = hb

        @pl.when(g == nh - 1)
        def _():
            gather.wait_sibling(0)
            tile_load(0, order_ref[0]).start()

        @pl.when(g >= nh)
        def _():
            t, i = (g - nh) // ni, (g - nh) % ni

            @pl.when(i == 0)
            def _():
                tile_load(t % 2, order_ref[t]).wait()

            @pl.when((i == ni - 1) & (t == 0))
            def _():
                gather.pass_on(0)
                gather.pass_on(1)
                gather.relay_diagonal()

            @pl.when((i == ni // 2) & (t == 2))
            def _():
                gather.pass_on(2)

            for j in range(3):
                @pl.when((i == ni - 1) & (t == j))
                def _():
                    gather.wait_sibling(4 + j)
                    tile_load((j + 1) % 2, order_ref[j + 1]).start()

            lhs = h_scr[pl.ds(pl.multiple_of(i * tm, tm), tm), :]
            proj_ref[...] = lax.dot_general(lhs, w_buf[t % 2], NT, preferred_element_type=F32).astype(BF16)

        @pl.when(g == steps - 1)
        def _():
            gather.wait_sends()

    def h_tile(g, order_ref):
        return (jnp.minimum(g, nh - 1), 0)

    def proj_tile(g, order_ref):
        mm = jnp.maximum(g - nh, 0)
        return (mm % ni, order_ref[mm // ni])

    row = pl.BlockSpec((1, D_MODEL), lambda g, order_ref: (0, 0))
    hbm = pl.BlockSpec(memory_space=pl.ANY)
    return pl.pallas_call(
        body, name="gather_in_proj",
        grid_spec=pltpu.PrefetchScalarGridSpec(
            num_scalar_prefetch=1, grid=(steps,),
            in_specs=[pl.BlockSpec((th, D_MODEL), h_tile), row, row, row, hbm],
            out_specs=[pl.BlockSpec((th, D_MODEL), h_tile), pl.BlockSpec((tm, tn), proj_tile), hbm],
            scratch_shapes=[pltpu.VMEM((s, D_MODEL), BF16), pltpu.VMEM((2, tn, D_MODEL), BF16),
                            pltpu.SemaphoreType.DMA((2,)), *_gather_scratch()]),
        out_shape=[SDS((s, D_MODEL), BF16), SDS((s, D_IN), BF16), SDS(wt_all.shape, BF16)],
        input_output_aliases={5: 2},
        compiler_params=_params("arbitrary"),
    )(order, x, shift, scale, norm_g, wt_all)


def _rope_freqs():
    inv_freq = ROPE_THETA ** (-jnp.arange(0, HEAD_DIM, 2, dtype=F32) / HEAD_DIM)
    return jnp.tile(inv_freq, 4).reshape(1, 128)


class _RopeTables:
    def __init__(self, freq_ref, rows_ref, state_ref, last_ref):
        self.freq, self.rows, self.state, self.last = freq_ref, rows_ref, state_ref, last_ref

    def start(self, block, direction):
        ang = lax.broadcasted_iota(jnp.int32, (CHUNK, 128), 0).astype(F32) * self.freq[...]
        self.rows[0] = jnp.cos(ang)
        self.rows[1] = jnp.sin(ang)
        base = jnp.asarray(block * CHUNK, dtype=F32) * self.freq[...]
        turn = float(direction * CHUNK) * self.freq[...]
        self.state[0:1, :] = jnp.cos(base)
        self.state[1:2, :] = jnp.sin(base)
        self.state[2:3, :] = jnp.cos(turn)
        self.state[3:4, :] = jnp.sin(turn)

    def step(self):
        c, s, ct, st = (self.state[k:k + 1, :] for k in range(4))
        self.state[0:1, :] = c * ct - s * st
        self.state[1:2, :] = s * ct + c * st

    def tables(self):
        c, s = self.state[0:1, :], self.state[1:2, :]
        cos = c * self.rows[0] - s * self.rows[1]
        sin = s * self.rows[0] + c * self.rows[1]
        first_half = (lax.broadcasted_iota(jnp.int32, (1, 128), 1) & (HEAD_DIM - 1)) < HEAD_DIM // 2
        return cos, jnp.where(first_half, -sin, 0.0), jnp.where(first_half, 0.0, sin)

    def keep(self, tabs):
        for k in range(3):
            self.last[k] = tabs[k]

    def kept(self):
        return tuple(self.last[k] for k in range(3))


def _rope_scratch():
    return [pltpu.VMEM((2, CHUNK, 128), F32), pltpu.VMEM((8, 128), F32), pltpu.VMEM((3, CHUNK, 128), F32)]


def _rope(v, cos, sin_lo, sin_hi):
    width = v.shape[1]
    rep = (1, width // 128)
    return (v * jnp.tile(cos, rep) + pltpu.roll(v, width - 32, 1) * jnp.tile(sin_lo, rep)
            + pltpu.roll(v, 32, 1) * jnp.tile(sin_hi, rep))


def _rope_bwd(d, cos, sin_lo, sin_hi):
    width = d.shape[1]
    rep = (1, width // 128)
    return (d * jnp.tile(cos, rep) + pltpu.roll(d * jnp.tile(sin_lo, rep), 32, 1)
            + pltpu.roll(d * jnp.tile(sin_hi, rep), width - 32, 1))


def _layer_norm(v, g, b):
    mu = jnp.mean(v, axis=-1, keepdims=True)
    vc = v - mu
    rstd = lax.rsqrt(jnp.mean(vc * vc, axis=-1, keepdims=True) + NORM_EPS)
    vhat = vc * rstd
    return vhat * g + b, vhat, rstd


def _tril_bf16(w_ref, g):
    t = lax.broadcasted_iota(jnp.int32, (CHUNK, CHUNK), 0)
    tp = lax.broadcasted_iota(jnp.int32, (CHUNK, CHUNK), 1)
    return jnp.where(tp <= t, w_ref[g], 0.0).astype(BF16)


def _bias_columns(b_ref, out_ref):
    for g in range(A_GROUPS):
        out_ref[g] = jnp.broadcast_to(b_ref[pl.ds(g, 1), :], (CHUNK, CHUNK)).T


def _from_prev():
    r = lax.broadcasted_iota(jnp.int32, (CHUNK, 4 * CHUNK), 0)
    i = lax.broadcasted_iota(jnp.int32, (CHUNK, 4 * CHUNK), 1) & (CHUNK - 1)
    return r > i


def _set_unfold_masks(mask_ref):
    prev = _from_prev()
    mask_ref[0] = jnp.where(prev, 1.0, 0.0).astype(BF16)
    mask_ref[1] = jnp.where(prev, 0.0, 1.0).astype(BF16)


def _fold_band(t, from_prev):
    return jnp.where(from_prev, t[:CHUNK], t[CHUNK:])


def _unfold_band(t, mask_ref):
    return jnp.concatenate([t * mask_ref[0], t * mask_ref[1]], axis=0)


def _low_lanes():
    return lax.broadcasted_iota(jnp.int32, (1, 128), 1) < HEAD_DIM


def _stack_heads(pair_a, pair_b):
    lo = _low_lanes()
    return jnp.concatenate([jnp.where(lo, pair_a, 0.0), jnp.where(lo, 0.0, pair_a),
                            jnp.where(lo, pair_b, 0.0), jnp.where(lo, 0.0, pair_b)], axis=0).astype(BF16)


def _heads_to_lanes(per_group):
    rows = [t[:, r * CHUNK:(r + 1) * CHUNK] for t in per_group for r in range(4)]
    return jnp.concatenate(rows, axis=0).T


def _dup_kv_head(band, gk):
    pair = band[:, (gk // 2) * 128:(gk // 2 + 1) * 128]
    lo = _low_lanes()
    one = jnp.where(lo if gk % 2 == 0 else jnp.logical_not(lo), pair, 0.0)
    return (one + pltpu.roll(one, HEAD_DIM, 1)).astype(BF16)


def _fold_kv_head(dup_grad, gk):
    both = dup_grad + pltpu.roll(dup_grad, HEAD_DIM, 1)
    lo = _low_lanes()
    return jnp.where(lo if gk % 2 == 0 else jnp.logical_not(lo), both, 0.0)


def _attn_probs(q_st, k_dup, sink_row, from_prev, first_block):
    s = lax.dot_general(k_dup, q_st, NT, preferred_element_type=F32)
    no_prev = jnp.where(first_block, -jnp.inf, 0.0)
    s = jnp.where(from_prev, s[:CHUNK] + no_prev, s[CHUNK:])
    m = jnp.maximum(jnp.max(s, axis=0, keepdims=True), sink_row)
    p = jnp.exp(s - m)
    e_sink = jnp.exp(sink_row - m)
    inv = 1.0 / (jnp.sum(p, axis=0, keepdims=True) + e_sink)
    return p * inv, e_sink * inv


def _sink_row(sinks_ref, gk):
    return jnp.concatenate([jnp.full((1, CHUNK), sinks_ref[4 * gk + r], F32) for r in range(4)], axis=1)


def _mixer_specs(nb, rev):
    def blk(i):
        return nb - 1 - i if rev else i

    def prev(i):
        return jnp.maximum(blk(i) - 1, 0)

    return dict(
        cur=pl.BlockSpec((CHUNK, D_IN), lambda i, *_: (blk(i), 0)),
        prev_kv=pl.BlockSpec((CHUNK, 2 * 256), lambda i, *_: (prev(i), OFF_K // 512)),
        freq=pl.BlockSpec((1, 128), lambda i, *_: (0, 0)),
        vec=pl.BlockSpec((1, D_A), lambda i, *_: (0, 0)),
        wsp=pl.BlockSpec((A_GROUPS, CHUNK, CHUNK), lambda i, *_: (0, 0, 0)),
        bsp=pl.BlockSpec((A_GROUPS, CHUNK), lambda i, *_: (0, 0)),
        smem=pl.BlockSpec(memory_space=pltpu.SMEM),
        blk=blk,
    )


def _mixer_fwd(proj, freqs, ln_g, ln_b, w_sp, b_sp, sinks, wo_all):
    s = proj.shape[0]
    nb = s // CHUNK
    sp = _mixer_specs(nb, rev=False)

    def body(cur_ref, pkv_ref, freq_ref, lg_ref, lb_ref, w_ref, b_ref, sinks_ref, wo_in, y_ref, wo_ref,
             bcol, mask, rope_rows, rope_state, rope_last, send_sems, recv_sems):
        i = pl.program_id(0)
        gather = _InPlaceGather(wo_ref, send_sems, recv_sems)
        rope = _RopeTables(freq_ref, rope_rows, rope_state, rope_last)

        @pl.when(i == 0)
        def _():
            gather.start()
            _bias_columns(b_ref, bcol)
            _set_unfold_masks(mask)
            rope.start(-1, 1)
            rope_last[...] = jnp.zeros_like(rope_last)

        @pl.when(i == (3 * nb) // 4)
        def _():
            for j in range(3):
                gather.pass_on(j)

        vln, _, _ = _layer_norm(cur_ref[:, OFF_VA:OFF_ZA].astype(F32), lg_ref[...], lb_ref[...])
        vln = vln.astype(BF16)
        for g in range(A_GROUPS):
            cols = slice(g * 128, (g + 1) * 128)
            sg = jnp.dot(_tril_bf16(w_ref, g), vln[:, cols], preferred_element_type=F32) + bcol[g]
            u = cur_ref[:, OFF_U + g * 128:OFF_U + (g + 1) * 128].astype(F32)
            z = cur_ref[:, OFF_ZA + g * 128:OFF_ZA + (g + 1) * 128].astype(F32)
            y_ref[:, cols] = (u * sg * (z * _sigmoid(z))).astype(BF16)

        rope.step()
        cur_t, prev_t = rope.tables(), rope.kept()
        rope.keep(cur_t)
        qr = _rope(cur_ref[:, OFF_Q:OFF_K].astype(F32), *cur_t) * ATTN_SCALE
        kr = jnp.concatenate([_rope(pkv_ref[:, 0:256].astype(F32), *prev_t),
                              _rope(cur_ref[:, OFF_K:OFF_V].astype(F32), *cur_t)], axis=0)
        v_t = jnp.concatenate([pkv_ref[:, 256:512], cur_ref[:, OFF_V:OFF_ZB]], axis=0).astype(F32).T.astype(BF16)
        outs = []
        from_prev = _from_prev()
        for gk in range(N_KV_HEADS):
            q_st = _stack_heads(qr[:, (2 * gk) * 128:(2 * gk + 1) * 128], qr[:, (2 * gk + 1) * 128:(2 * gk + 2) * 128])
            probs, _ = _attn_probs(q_st, _dup_kv_head(kr, gk), _sink_row(sinks_ref, gk), from_prev, i == 0)
            outs.append(jnp.dot(v_t[gk * HEAD_DIM:(gk + 1) * HEAD_DIM], _unfold_band(probs.astype(BF16), mask),
                                preferred_element_type=F32))
        zb = cur_ref[:, OFF_ZB:D_IN].astype(F32)
        y_ref[:, D_A:D_MODEL] = (_heads_to_lanes(outs) * (zb * _sigmoid(zb))).astype(BF16)

        @pl.when(i == nb - 1)
        def _():
            gather.wait_sibling(0)
            for j in range(3):
                gather.wait_sibling(4 + j)
            gather.wait_sends()

    hbm = pl.BlockSpec(memory_space=pl.ANY)
    return pl.pallas_call(
        body, name="mixer_fwd", grid=(nb,),
        in_specs=[sp["cur"], sp["prev_kv"], sp["freq"], sp["vec"], sp["vec"], sp["wsp"], sp["bsp"], sp["smem"], hbm],
        out_specs=[pl.BlockSpec((CHUNK, D_MODEL), lambda i: (i, 0)), hbm],
        out_shape=[SDS((s, D_MODEL), BF16), SDS(wo_all.shape, wo_all.dtype)],
        scratch_shapes=[pltpu.VMEM((A_GROUPS, CHUNK, CHUNK), F32), pltpu.VMEM((2, CHUNK, 4 * CHUNK), BF16),
                        *_rope_scratch(), *_gather_scratch()],
        input_output_aliases={8: 1},
        compiler_params=_params("arbitrary"),
    )(proj, proj, freqs, ln_g, ln_b, w_sp, b_sp, sinks, wo_all)


def _out_proj_loss(y, x, target, wo, gate, shift_f, scale_f, fng):
    s = y.shape[0]
    tm, tr = 256, 128
    nt = s // tm

    def body(y_ref, x_ref, t_ref, wo_ref, gate_ref, sh_ref, sc_ref, g_ref, dx1_ref, do_ref, dy_ref, sums_ref,
             do_last, do_work):
        i = pl.program_id(0)

        @pl.when(i == 0)
        def _():
            sums_ref[...] = jnp.zeros_like(sums_ref)
            do_last[...] = jnp.zeros_like(do_last)

        do_work[...] = do_last[...]
        o = jnp.dot(y_ref[...], wo_ref[...], preferred_element_type=F32)
        gate, g, sh = gate_ref[...], g_ref[...], sh_ref[...]
        one_sc = 1.0 + sc_ref[...]
        cs, inv_d = g * one_sc, 1.0 / D_MODEL

        def rowsum(v):
            return jnp.sum(v, axis=0, keepdims=True)

        sums = [jnp.zeros((1, D_MODEL), F32) for _ in range(4)]
        for c in range(tm // tr):
            rows = slice(c * tr, (c + 1) * tr)
            oc = o[rows]
            x1 = x_ref[rows, :] + gate * oc
            r = lax.rsqrt(jnp.sum(x1 * x1, axis=-1, keepdims=True) * inv_d + NORM_EPS)
            x1n = x1 * r
            diff = x1n * cs + sh - t_ref[rows, :]
            w = diff * x1n
            lane_sum = jnp.sum(w * cs, axis=-1, keepdims=True)
            dx1 = (diff * cs) * (r * inv_d) - x1n * (r * lane_sum * (inv_d * inv_d))
            dx1_ref[rows, :] = dx1
            do = (dx1 * gate).astype(BF16)
            do_ref[rows, :] = do
            do_last[rows, :] = do
            for k, v in enumerate((dx1 * oc, diff, w, diff * diff)):
                sums[k] = sums[k] + rowsum(v)
        live = jnp.where(i < nt, 1.0, 0.0)
        for row, v in ((SUM_GATE, sums[0]), (SUM_SHIFT_F, inv_d * sums[1]), (SUM_SCALE_F, inv_d * (sums[2] * g)),
                       (SUM_FNG, inv_d * (sums[2] * one_sc)), (SUM_SQ_ERR, sums[3])):
            sums_ref[row:row + 1, :] += live * v
        dy_ref[...] = lax.dot_general(do_work[...], wo_ref[...], NT, preferred_element_type=F32).astype(BF16)

    tile = pl.BlockSpec((tm, D_MODEL), lambda i: (jnp.minimum(i, nt - 1), 0))
    row = pl.BlockSpec((1, D_MODEL), lambda i: (0, 0))
    return pl.pallas_call(
        body, name="out_proj_loss", grid=(nt + 1,),
        in_specs=[tile, tile, tile, pl.BlockSpec((D_MODEL, D_MODEL), lambda i: (0, 0)), row, row, row, row],
        out_specs=[tile, tile, pl.BlockSpec((tm, D_MODEL), lambda i: (jnp.maximum(i - 1, 0), 0)),
                   pl.BlockSpec((8, D_MODEL), lambda i: (0, 0))],
        out_shape=[SDS((s, D_MODEL), F32), SDS((s, D_MODEL), BF16), SDS((s, D_MODEL), BF16), SDS((8, D_MODEL), F32)],
        scratch_shapes=[pltpu.VMEM((tm, D_MODEL), BF16), pltpu.VMEM((tm, D_MODEL), BF16)],
        compiler_params=_params("arbitrary"),
    )(y, x, target, wo, gate, shift_f, scale_f, fng)


ROW_DBSP, ROW_DSINKS, MISC_ROWS = 0, 8, 32


def _mixer_bwd(me, proj, dy, freqs, ln_g, ln_b, w_sp, b_sp, sinks, pair):
    s = proj.shape[0]
    nb = s // CHUNK
    sp = _mixer_specs(nb, rev=True)

    def body(me_ref, cur_ref, pkv_ref, dy_ref, freq_ref, lg_ref, lb_ref, w_ref, b_ref, sinks_ref, pair_ref,
             dproj_ref, dln_ref, dw_ref, misc_ref, parts_ref, bcol, dbcol, carry, mask, rope_rows, rope_state,
             rope_last, send_sems, recv_sems):
        i = pl.program_id(0)
        block = nb - 1 - i
        rope = _RopeTables(freq_ref, rope_rows, rope_state, rope_last)

        @pl.when(i == 0)
        def _():
            for cp in _chip_scatter(pair_ref, parts_ref, send_sems, recv_sems):
                cp.start()
            _bias_columns(b_ref, bcol)
            _set_unfold_masks(mask)
            rope.start(nb - 1, -1)
            rope.keep(rope.tables())
            dbcol[...] = jnp.zeros_like(dbcol)
            carry[...] = jnp.zeros_like(carry)
            dln_ref[...] = jnp.zeros_like(dln_ref)
            dw_ref[...] = jnp.zeros_like(dw_ref)
            misc_ref[...] = jnp.zeros_like(misc_ref)

        vln, vhat, rstd = _layer_norm(cur_ref[:, OFF_VA:OFF_ZA].astype(F32), lg_ref[...], lb_ref[...])
        vln = vln.astype(BF16)
        d_vln = []
        for g in range(A_GROUPS):
            cols = slice(g * 128, (g + 1) * 128)
            w_g = _tril_bf16(w_ref, g)
            sg = jnp.dot(w_g, vln[:, cols], preferred_element_type=F32) + bcol[g]
            u = cur_ref[:, OFF_U + g * 128:OFF_U + (g + 1) * 128].astype(F32)
            z = cur_ref[:, OFF_ZA + g * 128:OFF_ZA + (g + 1) * 128].astype(F32)
            dya = dy_ref[:, cols].astype(F32)
            sig = _sigmoid(z)
            d_ya = dya * (z * sig)
            dproj_ref[:, OFF_ZA + g * 128:OFF_ZA + (g + 1) * 128] = (
                dya * (u * sg) * (sig * (1.0 + z * (1.0 - sig)))).astype(BF16)
            dproj_ref[:, OFF_U + g * 128:OFF_U + (g + 1) * 128] = (d_ya * sg).astype(BF16)
            d_s = d_ya * u
            dbcol[g] += d_s
            d_sb = d_s.astype(BF16)
            dw_ref[g] += lax.dot_general(d_sb, vln[:, cols], NT, preferred_element_type=F32)
            d_vln.append(lax.dot_general(w_g, d_sb, TN, preferred_element_type=F32))
        d_vln = jnp.concatenate(d_vln, axis=1)
        dln_ref[0:1, :] += jnp.sum(d_vln * vhat, axis=0, keepdims=True)
        dln_ref[1:2, :] += jnp.sum(d_vln, axis=0, keepdims=True)
        d_vhat = d_vln * lg_ref[...]
        d_va = rstd * (d_vhat - jnp.mean(d_vhat, axis=-1, keepdims=True)
                       - vhat * jnp.mean(d_vhat * vhat, axis=-1, keepdims=True))
        dproj_ref[:, OFF_VA:OFF_ZA] = d_va.astype(BF16)

        cur_t = rope.kept()
        rope.step()
        prev_t = rope.tables()
        rope.keep(prev_t)
        band_t = tuple(jnp.concatenate([p, c], axis=0) for p, c in zip(prev_t, cur_t))
        qr = _rope(cur_ref[:, OFF_Q:OFF_K].astype(F32), *cur_t) * ATTN_SCALE
        kr = jnp.concatenate([_rope(pkv_ref[:, 0:256].astype(F32), *prev_t),
                              _rope(cur_ref[:, OFF_K:OFF_V].astype(F32), *cur_t)], axis=0)
        vb = jnp.concatenate([pkv_ref[:, 256:512], cur_ref[:, OFF_V:OFF_ZB]], axis=0).astype(F32)
        k_t, v_t = (kr.T * ATTN_SCALE).astype(BF16), vb.T.astype(BF16)
        zb = cur_ref[:, OFF_ZB:D_IN].astype(F32)
        dyb = dy_ref[:, D_A:D_MODEL].astype(F32)
        sig = _sigmoid(zb)
        d_yb = dyb * (zb * sig)
        outs, dqs = [], []
        dk_pairs = [jnp.zeros((2 * CHUNK, 128), F32) for _ in range(2)]
        dv_pairs = [jnp.zeros((2 * CHUNK, 128), F32) for _ in range(2)]
        from_prev = _from_prev()
        for gk in range(N_KV_HEADS):
            heads = slice(gk * HEAD_DIM, (gk + 1) * HEAD_DIM)
            q_st = _stack_heads(qr[:, (2 * gk) * 128:(2 * gk + 1) * 128], qr[:, (2 * gk + 1) * 128:(2 * gk + 2) * 128])
            k_dup, v_dup = _dup_kv_head(kr, gk), _dup_kv_head(vb, gk)
            probs, p_sink = _attn_probs(q_st, k_dup, _sink_row(sinks_ref, gk), from_prev, block == 0)
            probs_b = _unfold_band(probs.astype(BF16), mask)
            outs.append(jnp.dot(v_t[heads], probs_b, preferred_element_type=F32))
            do_st = _stack_heads(d_yb[:, (2 * gk) * 128:(2 * gk + 1) * 128], d_yb[:, (2 * gk + 1) * 128:(2 * gk + 2) * 128])
            dp = _fold_band(lax.dot_general(v_dup, do_st, NT, preferred_element_type=F32), from_prev)
            delta = jnp.sum(probs * dp, axis=0, keepdims=True)
            ds = _unfold_band((probs * (dp - delta)).astype(BF16), mask)
            d_sink = -p_sink * delta
            for r in range(4):
                row = ROW_DSINKS + 4 * gk + r
                misc_ref[row:row + 1, :] += jnp.broadcast_to(
                    jnp.sum(d_sink[:, r * CHUNK:(r + 1) * CHUNK], axis=1, keepdims=True), (1, 128))
            dqs.append(jnp.dot(k_t[heads], ds, preferred_element_type=F32))
            dk_pairs[gk // 2] += _fold_kv_head(jnp.dot(ds, q_st, preferred_element_type=F32), gk)
            dv_pairs[gk // 2] += _fold_kv_head(jnp.dot(probs_b, do_st, preferred_element_type=F32), gk)
        dproj_ref[:, OFF_ZB:D_IN] = (dyb * _heads_to_lanes(outs) * (sig * (1.0 + zb * (1.0 - sig)))).astype(BF16)
        dproj_ref[:, OFF_Q:OFF_K] = _rope_bwd(_heads_to_lanes(dqs), *cur_t).astype(BF16)
        dk_band = _rope_bwd(jnp.concatenate(dk_pairs, axis=1), *band_t)
        dv_band = jnp.concatenate(dv_pairs, axis=1)
        dproj_ref[:, OFF_K:OFF_V] = (dk_band[CHUNK:] + carry[:, 0:256]).astype(BF16)
        dproj_ref[:, OFF_V:OFF_ZB] = (dv_band[CHUNK:] + carry[:, 256:512]).astype(BF16)
        carry[:, 0:256] = dk_band[:CHUNK]
        carry[:, 256:512] = dv_band[:CHUNK]

        @pl.when(i == nb - 1)
        def _():
            t = lax.broadcasted_iota(jnp.int32, (CHUNK, CHUNK), 0)
            tp = lax.broadcasted_iota(jnp.int32, (CHUNK, CHUNK), 1)
            for g in range(A_GROUPS):
                dw_ref[g] = jnp.where(tp <= t, dw_ref[g], 0.0)
                misc_ref[pl.ds(ROW_DBSP + g, 1), :] = jnp.sum(dbcol[g].T, axis=0, keepdims=True)
            scatter = _chip_scatter(pair_ref, parts_ref, send_sems, recv_sems)
            for cp in scatter:
                cp.wait_recv()
            for cp in scatter:
                cp.wait_send()

    blk = sp["blk"]
    hbm = pl.BlockSpec(memory_space=pl.ANY)
    return pl.pallas_call(
        body, name="mixer_bwd",
        grid_spec=pltpu.PrefetchScalarGridSpec(
            num_scalar_prefetch=1, grid=(nb,),
            in_specs=[sp["cur"], sp["prev_kv"], pl.BlockSpec((CHUNK, D_MODEL), lambda i, me_ref: (blk(i), 0)),
                      sp["freq"], sp["vec"], sp["vec"], sp["wsp"], sp["bsp"], sp["smem"], hbm],
            out_specs=[pl.BlockSpec((CHUNK, D_IN), lambda i, me_ref: (blk(i), 0)),
                       pl.BlockSpec((8, D_A), lambda i, me_ref: (me_ref[0], 0)),
                       pl.BlockSpec((A_GROUPS, CHUNK, CHUNK), lambda i, me_ref: (me_ref[0], 0, 0)),
                       pl.BlockSpec((MISC_ROWS, 128), lambda i, me_ref: (me_ref[0], 0)), hbm],
            scratch_shapes=[pltpu.VMEM((A_GROUPS, CHUNK, CHUNK), F32), pltpu.VMEM((A_GROUPS, CHUNK, CHUNK), F32),
                            pltpu.VMEM((CHUNK, 512), F32), pltpu.VMEM((2, CHUNK, 4 * CHUNK), BF16),
                            *_rope_scratch(), *_scatter_scratch()]),
        out_shape=[SDS((s, D_IN), BF16), SDS((N_DEV * 8, D_A), F32), SDS((N_DEV * A_GROUPS, CHUNK, CHUNK), F32),
                   SDS((N_DEV * MISC_ROWS, 128), F32), SDS((3,) + pair.shape[1:], pair.dtype)],
        compiler_params=_params("arbitrary"),
    )(me, proj, proj, dy, freqs, ln_g, ln_b, w_sp, b_sp, sinks, pair)


def _wgrad_pair(name, a, b, gathers=()):
    s, m = a.shape
    n = b.shape[1]
    bm, half = m // 4, m // 8
    bt = min(1024, s)
    steps = s // bt
    last = 4 * steps
    n_g = len(gathers)

    def body(*refs):
        a_ref, b_ref = refs[:2]
        out_ref, bufs = refs[2 + n_g], refs[3 + n_g:3 + 2 * n_g]
        acc, kept, got, sent, send_sems, recv_sems = refs[3 + 2 * n_g:9 + 2 * n_g]
        sems = refs[9 + 2 * n_g:]
        g = pl.program_id(0)
        tile, t = g // steps, g % steps
        mx, my, mc = _mesh_pos()
        jobs = [_InPlaceGather(bufs[k], sems[2 * k], sems[2 * k + 1]) for k in range(n_g)]

        def exchange(q):
            return pltpu.make_async_remote_copy(src_ref=sent, dst_ref=got.at[q % 2], send_sem=send_sems.at[q],
                                                recv_sem=recv_sems.at[q], device_id=(mx, my, 1 - mc),
                                                device_id_type=MESH)

        @pl.when(g == 0)
        def _():
            for job in jobs:
                job.start()

        @pl.when(g == 2 * steps)
        def _():
            for job in jobs:
                for j in range(3):
                    job.pass_on(j)

        @pl.when(g < last)
        def _():
            prod = lax.dot_general(a_ref[...], b_ref[...], TN, preferred_element_type=F32)

            @pl.when(t == 0)
            def _():
                acc[...] = prod

            @pl.when(t > 0)
            def _():
                acc[...] += prod

            @pl.when(t == steps - 1)
            def _():
                @pl.when(tile > 0)
                def _():
                    exchange(tile - 1).wait_send()

                kept[tile % 2] = acc[pl.ds(pl.multiple_of(mc * half, 8), half), :].astype(BF16)
                sent[...] = acc[pl.ds(pl.multiple_of((1 - mc) * half, 8), half), :].astype(BF16)
                exchange(tile).start()

        @pl.when((t == 0) & (g > 0))
        def _():
            q = tile - 1
            exchange(q).wait_recv()
            out_ref[0] = (kept[q % 2].astype(F32) + got[q % 2].astype(F32)).astype(BF16)

        @pl.when(g == last)
        def _():
            exchange(3).wait_send()
            for job in jobs:
                job.wait_sibling(0)
                for j in range(3):
                    job.wait_sibling(4 + j)
                job.wait_sends()

    def a_tile(g):
        gg = jnp.minimum(g, last - 1)
        return (gg % steps, gg // steps)

    def b_tile(g):
        return (jnp.minimum(g, last - 1) % steps, 0)

    hbm = pl.BlockSpec(memory_space=pl.ANY)
    outs = pl.pallas_call(
        body, name=name, grid=(last + 1,),
        in_specs=[pl.BlockSpec((bt, bm), a_tile), pl.BlockSpec((bt, n), b_tile)] + [hbm] * n_g,
        out_specs=[pl.BlockSpec((1, half, n), lambda g: (jnp.maximum(g - 1, 0) // steps, 0, 0))] + [hbm] * n_g,
        out_shape=[SDS((4, half, n), BF16)] + [SDS(gb.shape, gb.dtype) for gb in gathers],
        scratch_shapes=[pltpu.VMEM((bm, n), F32), pltpu.VMEM((2, half, n), BF16), pltpu.VMEM((2, half, n), BF16),
                        pltpu.VMEM((half, n), BF16), pltpu.SemaphoreType.DMA((4,)), pltpu.SemaphoreType.DMA((4,))]
        + _gather_scratch() * n_g,
        input_output_aliases={2 + k: 1 + k for k in range(n_g)},
        compiler_params=_params("arbitrary"),
    )(a, b, *gathers)
    return outs[0], outs[1:]


def _in_proj_bwd(dproj, wt, x, dx1, scale, norm_g, sums_o, pair):
    s = x.shape[0]
    tm, tk, tr = min(1024, s), D_IN // 4, 64
    ksteps = D_IN // tk

    def body(dp_ref, wt_ref, x_hbm, dx1_hbm, sc_ref, g_ref, so_ref, pair_ref, gx_ref, sums_ref, parts_ref, x_buf,
             dx1_buf, tile_sems, send_sems, recv_sems):
        i, k = pl.program_id(0), pl.program_id(1)

        def tile_copies():
            rows = pl.ds(pl.multiple_of(i * tm, tm), tm)
            return (pltpu.make_async_copy(x_hbm.at[rows], x_buf, tile_sems.at[0]),
                    pltpu.make_async_copy(dx1_hbm.at[rows], dx1_buf, tile_sems.at[1]))

        @pl.when((i == 0) & (k == 0))
        def _():
            for cp in _chip_scatter(pair_ref, parts_ref, send_sems, recv_sems):
                cp.start()
            sums_ref[...] = so_ref[...]

        @pl.when(k == 0)
        def _():
            for cp in tile_copies():
                cp.start()
            gx_ref[...] = jnp.dot(dp_ref[...], wt_ref[...], preferred_element_type=F32)

        @pl.when(k > 0)
        def _():
            gx_ref[...] += jnp.dot(dp_ref[...], wt_ref[...], preferred_element_type=F32)

        @pl.when(k == ksteps - 1)
        def _():
            for cp in tile_copies():
                cp.wait()
            one_sc, g = 1.0 + sc_ref[...], g_ref[...]
            cs = one_sc * g

            def chunk(j, sums):
                rows = pl.ds(pl.multiple_of(j * tr, tr), tr)
                dh, xv = gx_ref[rows, :], x_buf[rows, :]
                dhx = dh * xv
                r = lax.rsqrt(jnp.sum(xv * xv, axis=-1, keepdims=True) * (1.0 / D_MODEL) + NORM_EPS)
                coef = (r * r * r) * (jnp.sum(dhx * cs, axis=-1, keepdims=True) * (1.0 / D_MODEL))
                gx_ref[rows, :] = dx1_buf[rows, :] + r * (dh * cs) - xv * coef
                return (sums[0] + jnp.sum(dh, axis=0, keepdims=True), sums[1] + jnp.sum(dhx * r, axis=0, keepdims=True))

            zero = jnp.zeros((1, D_MODEL), F32)
            sums = lax.fori_loop(0, tm // tr, chunk, (zero, zero))
            sums_ref[SUM_SHIFT:SUM_SHIFT + 1, :] += sums[0]
            sums_ref[SUM_SCALE:SUM_SCALE + 1, :] += sums[1] * g
            sums_ref[SUM_NORM_G:SUM_NORM_G + 1, :] += sums[1] * one_sc

        @pl.when((i == s // tm - 1) & (k == ksteps - 1))
        def _():
            scatter = _chip_scatter(pair_ref, parts_ref, send_sems, recv_sems)
            for cp in scatter:
                cp.wait_recv()
            for cp in scatter:
                cp.wait_send()

    row = pl.BlockSpec((1, D_MODEL), lambda i, k: (0, 0))
    hbm = pl.BlockSpec(memory_space=pl.ANY)
    return pl.pallas_call(
        body, name="in_proj_bwd", grid=(s // tm, ksteps),
        in_specs=[pl.BlockSpec((tm, tk), lambda i, k: (i, k)), pl.BlockSpec((tk, D_MODEL), lambda i, k: (k, 0)),
                  hbm, hbm, row, row, pl.BlockSpec((8, D_MODEL), lambda i, k: (0, 0)), hbm],
        out_specs=[pl.BlockSpec((tm, D_MODEL), lambda i, k: (i, 0)), pl.BlockSpec((8, D_MODEL), lambda i, k: (0, 0)),
                   hbm],
        out_shape=[SDS((s, D_MODEL), F32), SDS((8, D_MODEL), F32), SDS((3,) + pair.shape[1:], pair.dtype)],
        scratch_shapes=[pltpu.VMEM((tm, D_MODEL), F32), pltpu.VMEM((tm, D_MODEL), F32),
                        pltpu.SemaphoreType.DMA((2,)), *_scatter_scratch()],
        compiler_params=_params("arbitrary", "arbitrary"),
    )(dproj, wt, x, dx1, scale, norm_g, sums_o, pair)


def _sum_chips(own_ref, parts_ref):
    return ((own_ref[0].astype(F32) + parts_ref[0].astype(F32)) + parts_ref[1].astype(F32)) + parts_ref[2].astype(F32)


def _adam_rows(name, chip, pair, parts, w, m, v, tr):
    rows = w.shape[0]

    def body(chip_ref, own_ref, p_ref, w_ref, m_ref, v_ref, g_ref, d_ref, nm_ref, nv_ref):
        g = _sum_chips(own_ref, p_ref)
        g_ref[...] = g
        d_ref[...], nm_ref[...], nv_ref[...] = _adamw(w_ref[...], g, m_ref[...], v_ref[...])

    blk = pl.BlockSpec((tr, D_MODEL), lambda j, chip_ref: (j, 0))
    return pl.pallas_call(
        body, name=name,
        grid_spec=pltpu.PrefetchScalarGridSpec(
            num_scalar_prefetch=1, grid=(rows // tr,),
            in_specs=[pl.BlockSpec((1, tr, D_MODEL), lambda j, chip_ref: (chip_ref[0], j, 0)),
                      pl.BlockSpec((3, tr, D_MODEL), lambda j, chip_ref: (0, j, 0)), blk, blk, blk],
            out_specs=[blk] * 4),
        out_shape=[SDS(w.shape, F32)] * 4, compiler_params=_params("parallel"),
    )(chip, pair, parts, w, m, v)


def _adam_ada(name, cact, dmod, w, m, v):
    n = w.shape[1]
    tr = 512

    def body(c_ref, dm_ref, w_ref, m_ref, v_ref, g_ref, d_ref, nm_ref, nv_ref):
        pad_c = jnp.concatenate([c_ref[...], jnp.zeros_like(c_ref)], axis=0).astype(BF16)
        pad_d = jnp.concatenate([dm_ref[...], jnp.zeros_like(dm_ref)], axis=0).astype(BF16)
        g = lax.dot_general(pad_c, pad_d, TN, preferred_element_type=F32)
        g_ref[...] = g
        d_ref[...], nm_ref[...], nv_ref[...] = _adamw(w_ref[...], g, m_ref[...], v_ref[...])

    blk = pl.BlockSpec((tr, n), lambda j: (j, 0))
    return pl.pallas_call(
        body, name=name, grid=(D_MODEL // tr,),
        in_specs=[pl.BlockSpec((N_DEV, tr), lambda j: (0, j)), pl.BlockSpec((N_DEV, n), lambda j: (0, 0)),
                  blk, blk, blk],
        out_specs=[blk] * 4, out_shape=[SDS(w.shape, F32)] * 4,
        compiler_params=_params("parallel"),
    )(cact, dmod, w, m, v)


SMALL_PARAMS = ("w_spatial", "b_spatial", "sinks", "norm_g", "ln_v_g", "ln_v_b", "final_norm_g", "b_ada", "b_ada_final")


def _adam_small(d_wsp, misc, d_ln, sums, params):
    n_p = len(SMALL_PARAMS)

    def body(*refs):
        wsp_ref, misc_ref, ln_ref, sums_ref = refs[:4]
        wmv = [refs[4 + 3 * k:7 + 3 * k] for k in range(n_p)]
        loss_ref = refs[4 + 3 * n_p]
        outs = [refs[5 + 3 * n_p + 4 * k:9 + 3 * n_p + 4 * k] for k in range(n_p)]

        def column_sum(row):
            return total(sums_ref, (row, row + 1))

        def total(ref, rows=None):
            def part(j):
                return ref[j] if rows is None else ref[j, rows[0]:rows[1], :]
            acc = part(0)
            for j in range(1, N_DEV):
                acc = acc + part(j)
            return acc

        sink_rows = total(misc_ref, (ROW_DSINKS, ROW_DSINKS + 16))
        diag = (lax.broadcasted_iota(jnp.int32, (16, 128), 0) == lax.broadcasted_iota(jnp.int32, (16, 128), 1))
        grads = dict(
            w_spatial=total(wsp_ref), b_spatial=total(misc_ref, (ROW_DBSP, ROW_DBSP + A_GROUPS)),
            sinks=jnp.sum(jnp.where(diag, sink_rows, 0.0), axis=0, keepdims=True),
            norm_g=column_sum(SUM_NORM_G), ln_v_g=total(ln_ref, (0, 1)), ln_v_b=total(ln_ref, (1, 2)),
            final_norm_g=column_sum(SUM_FNG),
            b_ada=jnp.concatenate([column_sum(SUM_SHIFT), column_sum(SUM_SCALE), column_sum(SUM_GATE)], axis=1),
            b_ada_final=jnp.concatenate([column_sum(SUM_SHIFT_F), column_sum(SUM_SCALE_F)], axis=1))
        sq_err = jnp.sum(column_sum(SUM_SQ_ERR), axis=1, keepdims=True)
        loss_ref[...] = jnp.broadcast_to(sq_err * (0.5 / D_MODEL), (1, 128))
        for k, name in enumerate(SMALL_PARAMS):
            w_ref, m_ref, v_ref = wmv[k]
            g_ref, d_ref, nm_ref, nv_ref = outs[k]
            g_ref[...] = grads[name]
            d_ref[...], nm_ref[...], nv_ref[...] = _adamw(w_ref[...], grads[name], m_ref[...], v_ref[...])

    flat = [a for name in SMALL_PARAMS for a in params[name]]
    vmem = pl.BlockSpec(memory_space=pltpu.VMEM)
    out_shape = [SDS((1, 128), F32)] + [SDS(params[name][0].shape, F32) for name in SMALL_PARAMS for _ in range(4)]
    outs = pl.pallas_call(
        body, name="adam_small", in_specs=[vmem] * (4 + len(flat)), out_specs=[vmem] * len(out_shape),
        out_shape=out_shape, compiler_params=_params(),
    )(d_wsp, misc, d_ln, sums, *flat)
    return outs[0], {name: outs[1 + 4 * k:5 + 4 * k] for k, name in enumerate(SMALL_PARAMS)}


def kernel(x, c, w_ada, b_ada, norm_g, w_in, ln_v_g, ln_v_b, w_spatial, b_spatial, sinks, w_out, w_ada_final, b_ada_final, final_norm_g, loss_target, m_w_ada, m_b_ada, m_norm_g, m_w_in, m_ln_v_g, m_ln_v_b, m_w_spatial, m_b_spatial, m_sinks, m_w_out, m_w_ada_final, m_b_ada_final, m_final_norm_g, v_w_ada, v_b_ada, v_norm_g, v_w_in, v_ln_v_g, v_ln_v_b, v_w_spatial, v_b_spatial, v_sinks, v_w_out, v_w_ada_final, v_b_ada_final, v_final_norm_g):
    seq = x.shape[1]
    me = 4 * lax.axis_index("x") + 2 * lax.axis_index("y") + lax.axis_index("c")
    x2, tgt = x[0], loss_target[0]
    fng = final_norm_g.reshape(1, D_MODEL)

    n_ada, n_ada_f = w_ada.shape[2], w_ada_final.shape[1]
    cact, mod, mod_f = _ada_exchange(c, w_ada[0], b_ada.reshape(N_DEV, n_ada), w_ada_final,
                                     b_ada_final.reshape(N_DEV, n_ada_f))
    cact = cact.reshape(N_DEV, D_MODEL)
    mod, mod_f = mod.reshape(1, 3 * D_MODEL), mod_f.reshape(1, 2 * D_MODEL)
    shift, scale, gate = mod[:, :D_MODEL], mod[:, D_MODEL:2 * D_MODEL], mod[:, 2 * D_MODEL:]
    shift_f, scale_f = mod_f[:, :D_MODEL], mod_f[:, D_MODEL:]

    wt_f32, m_wt, v_wt = (jnp.swapaxes(a, 1, 2)[0] for a in (w_in, m_w_in, v_w_in))
    xi, yi = lax.axis_index("x"), lax.axis_index("y")
    chip_order = jnp.stack([2 * xi + yi, 2 * (1 - xi) + yi, 2 * xi + 1 - yi, 2 * (1 - xi) + 1 - yi]).astype(jnp.int32)
    wt_mine, wo_mine = _prep_weights(me.reshape(1), wt_f32, w_out[0])

    freqs = _rope_freqs()
    sinks_v = sinks.reshape(16)
    h, proj, wt = _gather_in_proj(chip_order, x2, shift, scale, norm_g, wt_mine)
    y, wo = _mixer_fwd(proj, freqs, ln_v_g, ln_v_b, w_spatial[0], b_spatial[0], sinks_v, wo_mine)
    dx1, do, dy, sums_o = _out_proj_loss(y, x2, tgt, wo, gate, shift_f, scale_f, fng)

    chip = (2 * lax.axis_index("x") + lax.axis_index("y")).reshape(1)
    pair_out, _ = _wgrad_pair("wgrad_out", y, do)
    dproj, d_ln, d_wsp, misc, parts_out = _mixer_bwd(
        me.reshape(1), proj, dy, freqs, ln_v_g, ln_v_b, w_spatial[0], b_spatial[0], sinks_v, pair_out)
    pair_in, (d_ln, d_wsp, misc) = _wgrad_pair(
        "wgrad_in", dproj, h, gathers=(d_ln, d_wsp.reshape(N_DEV * A_GROUPS * CHUNK, CHUNK), misc))
    grad_x, sums, parts_in = _in_proj_bwd(dproj, wt, x2, dx1, scale, norm_g, sums_o, pair_in)
    wt_leaves = [jnp.swapaxes(a[None], 1, 2)
                 for a in _adam_rows("adam_w_in", chip, pair_in, parts_in, wt_f32, m_wt, v_wt, 176)]
    w_out_leaves = [a[None] for a in _adam_rows("adam_w_out", chip, pair_out, parts_out, w_out[0], m_w_out[0], v_w_out[0], 64)]

    (sums,) = _all_gather("gather_sums", [sums], pltpu.VMEM)
    natural = dict(w_spatial=(A_GROUPS * CHUNK, CHUNK), b_spatial=(A_GROUPS, CHUNK), sinks=(1, 16), norm_g=(1, D_MODEL),
                   ln_v_g=(1, D_A), ln_v_b=(1, D_A), final_norm_g=(1, D_MODEL), b_ada=(1, 3 * D_MODEL),
                   b_ada_final=(1, 2 * D_MODEL))
    given = dict(
        w_spatial=(w_spatial, m_w_spatial, v_w_spatial), b_spatial=(b_spatial, m_b_spatial, v_b_spatial),
        sinks=(sinks, m_sinks, v_sinks), norm_g=(norm_g, m_norm_g, v_norm_g), ln_v_g=(ln_v_g, m_ln_v_g, v_ln_v_g),
        ln_v_b=(ln_v_b, m_ln_v_b, v_ln_v_b), final_norm_g=(final_norm_g, m_final_norm_g, v_final_norm_g),
        b_ada=(b_ada, m_b_ada, v_b_ada), b_ada_final=(b_ada_final, m_b_ada_final, v_b_ada_final))
    params = {name: tuple(a.reshape(natural[name]) for a in given[name]) for name in SMALL_PARAMS}
    params["sinks"] = tuple(jnp.pad(a, ((0, 0), (0, 128 - 16))) for a in params["sinks"])
    loss, small = _adam_small(d_wsp.reshape(N_DEV, A_GROUPS * CHUNK, CHUNK), misc.reshape(N_DEV, MISC_ROWS, 128),
                              d_ln.reshape(N_DEV, 8, D_A), sums, params)
    small["sinks"] = [a[:, :16] for a in small["sinks"]]
    small = {name: [a.reshape(given[name][0].shape) for a in small[name]] for name in SMALL_PARAMS}

    dmod_all = jnp.concatenate([sums[:, SUM_SHIFT], sums[:, SUM_SCALE], sums[:, SUM_GATE]], axis=1)
    dmod_f_all = jnp.concatenate([sums[:, SUM_SHIFT_F], sums[:, SUM_SCALE_F]], axis=1)
    dmod_mine = lax.dynamic_slice_in_dim(dmod_all, me * n_ada, n_ada, axis=1)
    dmod_f_mine = lax.dynamic_slice_in_dim(dmod_f_all, me * n_ada_f, n_ada_f, axis=1)
    ada = _adam_ada("adam_w_ada", cact, dmod_mine, w_ada[0], m_w_ada[0], v_w_ada[0])
    ada_f = _adam_ada("adam_w_ada_final", cact, dmod_f_mine, w_ada_final, m_w_ada_final, v_w_ada_final)

    def leaves(k):
        return (ada[k][None], small["b_ada"][k], small["norm_g"][k], wt_leaves[k], small["ln_v_g"][k],
                small["ln_v_b"][k], small["w_spatial"][k], small["b_spatial"][k], small["sinks"][k], w_out_leaves[k],
                ada_f[k], small["b_ada_final"][k], small["final_norm_g"][k])

    return (loss[0, 0], grad_x[None], *leaves(0), *leaves(1), *leaves(2), *leaves(3))
```

```python
import jax
import jax.numpy as jnp
from jax import lax
from jax.experimental import pallas as pl
from jax.experimental.pallas import tpu as pltpu

D_MODEL = 2048
D_IN = 5632
D_A = 1024
CHUNK = 128
A_GROUPS = 8
HEAD_DIM = 64
N_KV_HEADS = 4
N_DEV = 8
ROPE_THETA = 10000.0
NORM_EPS = 1e-5
ATTN_SCALE = HEAD_DIM ** -0.5

ADAM_LR = 0.001
ADAM_B1 = 0.9
ADAM_B2 = 0.999
ADAM_EPS = 1e-08
ADAM_WD = 0.01
ADAM_STEP = 10

OFF_U, OFF_VA, OFF_ZA, OFF_Q, OFF_K, OFF_V, OFF_ZB = 0, 1024, 2048, 3072, 4096, 4352, 4608

SUM_SHIFT, SUM_SCALE, SUM_NORM_G, SUM_GATE, SUM_SHIFT_F, SUM_SCALE_F, SUM_FNG, SUM_SQ_ERR = range(8)

V7X_VMEM_LIMIT_BYTES = 56 * 1024 * 1024

F32 = jnp.float32
BF16 = jnp.bfloat16
MESH = pl.DeviceIdType.MESH
SDS = jax.ShapeDtypeStruct
NT = (((1,), (1,)), ((), ()))
TN = (((0,), (0,)), ((), ()))


def _params(*semantics):
    return pltpu.CompilerParams(dimension_semantics=semantics or None, vmem_limit_bytes=V7X_VMEM_LIMIT_BYTES)


def _mesh_pos():
    return lax.axis_index("x"), lax.axis_index("y"), lax.axis_index("c")


def _sigmoid(z):
    return 1.0 / (1.0 + jnp.exp(-z))


def _adamw(w, g, m, v):
    m = ADAM_B1 * m + (1.0 - ADAM_B1) * g
    v = ADAM_B2 * v + (1.0 - ADAM_B2) * (g * g)
    m_hat = m / (1.0 - ADAM_B1 ** ADAM_STEP)
    v_hat = v / (1.0 - ADAM_B2 ** ADAM_STEP)
    delta = -ADAM_LR * (m_hat / (jnp.sqrt(v_hat) + ADAM_EPS) + ADAM_WD * w)
    return delta, m, v


def _all_gather(name, blocks, memory_space):
    n_arr = len(blocks)

    def body(*refs):
        ins, outs = refs[:n_arr], refs[n_arr:2 * n_arr]
        send_sems, recv_sems, local_sems = refs[2 * n_arr:]
        x, y, c = _mesh_pos()
        me, sibling = (x, y, c), (x, y, 1 - c)
        chips = [(1 - x, y), (x, 1 - y), (1 - x, 1 - y)]

        def slot(p):
            return 4 * p[0] + 2 * p[1] + p[2]

        def copy(a, k, block, to, src=None):
            dst = outs[a].at[slot(block)]
            return pltpu.make_async_remote_copy(
                src_ref=dst if src is None else src, dst_ref=dst,
                send_sem=send_sems.at[a, k], recv_sem=recv_sems.at[a, k],
                device_id=to, device_id_type=MESH)

        mine = [pltpu.make_async_copy(ins[a], outs[a].at[slot(me)], local_sems.at[a]) for a in range(n_arr)]
        for cp in mine:
            cp.start()
        first = []
        for a in range(n_arr):
            first.append(copy(a, 0, me, sibling, src=ins[a]))
            first += [copy(a, 1 + j, me, (*chip, c), src=ins[a]) for j, chip in enumerate(chips)]
        for cp in first:
            cp.start()
        passed = []
        for j, chip in enumerate(chips):
            for a in range(n_arr):
                copy(a, 1 + j, (*chip, c), me).wait_recv()
                fwd = copy(a, 4 + j, (*chip, c), sibling)
                fwd.start()
                passed.append(fwd)
        for a in range(n_arr):
            copy(a, 0, sibling, me).wait_recv()
            for j, chip in enumerate(chips):
                copy(a, 4 + j, (*chip, 1 - c), me).wait_recv()
        for cp in first + passed:
            cp.wait_send()
        for cp in mine:
            cp.wait()

    spec = pl.BlockSpec(memory_space=memory_space)
    return pl.pallas_call(
        body, name=name,
        out_shape=[SDS((N_DEV,) + b.shape, b.dtype) for b in blocks],
        in_specs=[spec] * n_arr, out_specs=[spec] * n_arr,
        scratch_shapes=[pltpu.SemaphoreType.DMA((n_arr, 7)), pltpu.SemaphoreType.DMA((n_arr, 7)),
                        pltpu.SemaphoreType.DMA((n_arr,))],
        compiler_params=_params(),
    )(*blocks)


def _ada_exchange(c, w_ada, b_ada8, w_ada_f, b_ada_f8):
    n1, n2 = w_ada.shape[1], w_ada_f.shape[1]

    def body(c_ref, w1_hbm, b1_ref, w2_hbm, b2_ref, cact_ref, mod_ref, modf_ref,
             cact_buf, res1, res2, send1, send2, w1_ref, w2_ref, load_sems, sems_s, sems_r):
        x, y, c_pos = _mesh_pos()
        me = 4 * x + 2 * y + c_pos
        flips = [(k >> 2 & 1, k >> 1 & 1, k & 1) for k in range(1, N_DEV)]

        def peer(f):
            return (1 - x if f[0] else x, 1 - y if f[1] else y, 1 - c_pos if f[2] else c_pos)

        loads = [pltpu.make_async_copy(w1_hbm, w1_ref, load_sems.at[0]),
                 pltpu.make_async_copy(w2_hbm, w2_ref, load_sems.at[1])]
        for cp in loads:
            cp.start()
        cv = c_ref[...]
        cact = cv * _sigmoid(cv)
        cact_buf[...] = cact
        cact_ref[me] = cact

        def rdma(phase, k, src, dst, f):
            return pltpu.make_async_remote_copy(src_ref=src, dst_ref=dst, send_sem=sems_s.at[phase, k],
                                                recv_sem=sems_r.at[phase, k], device_id=peer(f), device_id_type=MESH)

        gather = [rdma(0, k, cact_buf, cact_ref.at[me], f) for k, f in enumerate(flips)]
        for cp in gather:
            cp.start()
        for cp in gather:
            cp.wait_recv()
        for cp in gather:
            cp.wait_send()

        rid = lax.broadcasted_iota(jnp.int32, (N_DEV, D_MODEL), 0)
        rows = jnp.zeros((N_DEV, D_MODEL), F32)
        for j in range(N_DEV):
            rows = jnp.where(rid == j, jnp.broadcast_to(cact_ref[j], (N_DEV, D_MODEL)), rows)
        rows = rows.astype(BF16)
        for cp in loads:
            cp.wait()
        res1[...] = jnp.dot(rows, w1_ref[...].astype(BF16), preferred_element_type=F32) + b1_ref[pl.ds(me, 1), :]
        res2[...] = jnp.dot(rows, w2_ref[...].astype(BF16), preferred_element_type=F32) + b2_ref[pl.ds(me, 1), :]
        for j in range(N_DEV):
            send1[j] = res1[pl.ds(j, 1), :]
            send2[j] = res2[pl.ds(j, 1), :]
        mod_ref[me] = send1[me]
        modf_ref[me] = send2[me]
        scatter = []
        for k, f in enumerate(flips):
            to = me ^ (k + 1)
            scatter.append(rdma(1, k, send1.at[to], mod_ref.at[me], f))
            scatter.append(rdma(2, k, send2.at[to], modf_ref.at[me], f))
        for cp in scatter:
            cp.start()
        for cp in scatter:
            cp.wait_recv()
        for cp in scatter:
            cp.wait_send()

    vmem, hbm = pl.BlockSpec(memory_space=pltpu.VMEM), pl.BlockSpec(memory_space=pl.ANY)
    return pl.pallas_call(
        body, name="ada_exchange",
        out_shape=[SDS((N_DEV, 1, D_MODEL), F32), SDS((N_DEV, 1, n1), F32), SDS((N_DEV, 1, n2), F32)],
        in_specs=[vmem, hbm, vmem, hbm, vmem], out_specs=[vmem] * 3,
        scratch_shapes=[pltpu.VMEM((1, D_MODEL), F32), pltpu.VMEM((N_DEV, n1), F32), pltpu.VMEM((N_DEV, n2), F32),
                        pltpu.VMEM((N_DEV, 1, n1), F32), pltpu.VMEM((N_DEV, 1, n2), F32),
                        pltpu.VMEM(w_ada.shape, F32), pltpu.VMEM(w_ada_f.shape, F32), pltpu.SemaphoreType.DMA((2,)),
                        pltpu.SemaphoreType.DMA((3, 7)), pltpu.SemaphoreType.DMA((3, 7))],
        compiler_params=_params(),
    )(c, w_ada, b_ada8, w_ada_f, b_ada_f8)


def _chip_scatter(pair_ref, parts_ref, send_sems, recv_sems):
    x, y, c = _mesh_pos()
    chips = [(1 - x, y), (x, 1 - y), (1 - x, 1 - y)]
    return [pltpu.make_async_remote_copy(
        src_ref=pair_ref.at[2 * cx + cy], dst_ref=parts_ref.at[j], send_sem=send_sems.at[j], recv_sem=recv_sems.at[j],
        device_id=(cx, cy, c), device_id_type=MESH) for j, (cx, cy) in enumerate(chips)]


def _scatter_scratch():
    return [pltpu.SemaphoreType.DMA((3,)), pltpu.SemaphoreType.DMA((3,))]


def _prep_weights(me, wt, w_out):
    steps = 4

    def body(me_ref, wt_ref, wo_ref, wtb_ref, wob_ref):
        wtb_ref[...] = wt_ref[...].astype(BF16)
        wob_ref[...] = wo_ref[...].astype(BF16)

    def rows(a, mine):
        blk = (a.shape[0] // steps, a.shape[1])
        return pl.BlockSpec(blk, (lambda i, me_ref: (steps * me_ref[0] + i, 0)) if mine else (lambda i, me_ref: (i, 0)))

    return pl.pallas_call(
        body, name="prep_weights",
        grid_spec=pltpu.PrefetchScalarGridSpec(
            num_scalar_prefetch=1, grid=(steps,),
            in_specs=[rows(wt, False), rows(w_out, False)], out_specs=[rows(wt, True), rows(w_out, True)]),
        out_shape=[SDS((N_DEV * wt.shape[0], D_MODEL), BF16), SDS((N_DEV * w_out.shape[0], D_MODEL), BF16)],
        compiler_params=_params("parallel"),
    )(me, wt, w_out)


class _InPlaceGather:
    def __init__(self, buf_ref, send_sems, recv_sems, relay=False):
        self.buf, self.send_sems, self.recv_sems, self.relay = buf_ref, send_sems, recv_sems, relay
        self.n = buf_ref.shape[0] // N_DEV
        x, y, c = _mesh_pos()
        self.me, self.sibling, self.core = (x, y, c), (x, y, 1 - c), c
        self.chips = [(1 - x, y), (x, 1 - y), (1 - x, 1 - y)]
        self.relay_from = (jnp.where(c == 0, 1 - x, x), jnp.where(c == 0, y, 1 - y), c)
        self.relay_to = (jnp.where(c == 0, x, 1 - x), jnp.where(c == 0, 1 - y, y), c)

    def copy(self, k, block, to):
        start = pl.multiple_of((4 * block[0] + 2 * block[1] + block[2]) * self.n, self.n)
        rows = self.buf.at[pl.ds(start, self.n)]
        return pltpu.make_async_remote_copy(src_ref=rows, dst_ref=rows, send_sem=self.send_sems.at[k],
                                            recv_sem=self.recv_sems.at[k], device_id=to, device_id_type=MESH)

    def start(self):
        self.copy(0, self.me, self.sibling).start()
        for j, chip in enumerate(self.chips[:2] if self.relay else self.chips):
            self.copy(1 + j, self.me, (*chip, self.core)).start()

    def relay_diagonal(self):
        self.copy(3, self.relay_from, self.relay_to).start()

    def pass_on(self, j):
        self.copy(1 + j, (*self.chips[j], self.core), self.me).wait_recv()
        self.copy(4 + j, (*self.chips[j], self.core), self.sibling).start()

    def wait_sibling(self, k):
        self.copy(k, self.sibling, self.me).wait_recv()

    def wait_sends(self):
        for k in range(7):
            self.copy(k, self.me, self.sibling).wait_send()


def _gather_scratch():
    return [pltpu.SemaphoreType.DMA((7,)), pltpu.SemaphoreType.DMA((7,))]


def _gather_in_proj(order, x, shift, scale, norm_g, wt_all):
    s = x.shape[0]
    th = tm = min(512, s)
    nh, ni = s // th, s // tm
    tn = D_IN // 4
    steps = nh + 4 * ni

    def body(order_ref, x_ref, shift_ref, scale_ref, g_ref, wt_in, h_ref, proj_ref, wt_ref,
             h_scr, w_buf, load_sems, send_sems, recv_sems):
        g = pl.program_id(0)
        gather = _InPlaceGather(wt_ref, send_sems, recv_sems, relay=True)

        def tile_load(slot, chip):
            return pltpu.make_async_copy(wt_ref.at[pl.ds(pl.multiple_of(chip * tn, tn), tn)], w_buf.at[slot],
                                         load_sems.at[slot])

        @pl.when(g == 0)
        def _():
            gather.start()

        @pl.when(g < nh)
        def _():
            xv = x_ref[...]
            r = lax.rsqrt(jnp.mean(xv * xv, axis=-1, keepdims=True) + NORM_EPS)
            hb = (((xv * r) * g_ref[...]) * (1.0 + scale_ref[...]) + shift_ref[...]).astype(BF16)
            h_ref[...] = hb
            h_scr[pl.ds(pl.multiple_of(g * th, th), th), :] = hb

        @pl.when(g == nh - 1)
        def _():
            gather.wait_sibling(0)
            tile_load(0, order_ref[0]).start()

        @pl.when(g >= nh)
        def _():
            t, i = (g - nh) // ni, (g - nh) % ni

            @pl.when(i == 0)
            def _():
                tile_load(t % 2, order_ref[t]).wait()

            @pl.when((i == ni - 1) & (t == 0))
            def _():
                gather.pass_on(0)
                gather.pass_on(1)
                gather.relay_diagonal()

            @pl.when((i == ni // 2) & (t == 2))
            def _():
                gather.pass_on(2)

            for j in range(3):
                @pl.when((i == ni - 1) & (t == j))
                def _():
                    gather.wait_sibling(4 + j)
                    tile_load((j + 1) % 2, order_ref[j + 1]).start()

            lhs = h_scr[pl.ds(pl.multiple_of(i * tm, tm), tm), :]
            proj_ref[...] = lax.dot_general(lhs, w_buf[t % 2], NT, preferred_element_type=F32).astype(BF16)

        @pl.when(g == steps - 1)
        def _():
            gather.wait_sends()

    def h_tile(g, order_ref):
        return (jnp.minimum(g, nh - 1), 0)

    def proj_tile(g, order_ref):
        mm = jnp.maximum(g - nh, 0)
        return (mm % ni, order_ref[mm // ni])

    row = pl.BlockSpec((1, D_MODEL), lambda g, order_ref: (0, 0))
    hbm = pl.BlockSpec(memory_space=pl.ANY)
    return pl.pallas_call(
        body, name="gather_in_proj",
        grid_spec=pltpu.PrefetchScalarGridSpec(
            num_scalar_prefetch=1, grid=(steps,),
            in_specs=[pl.BlockSpec((th, D_MODEL), h_tile), row, row, row, hbm],
            out_specs=[pl.BlockSpec((th, D_MODEL), h_tile), pl.BlockSpec((tm, tn), proj_tile), hbm],
            scratch_shapes=[pltpu.VMEM((s, D_MODEL), BF16), pltpu.VMEM((2, tn, D_MODEL), BF16),
                            pltpu.SemaphoreType.DMA((2,)), *_gather_scratch()]),
        out_shape=[SDS((s, D_MODEL), BF16), SDS((s, D_IN), BF16), SDS(wt_all.shape, BF16)],
        input_output_aliases={5: 2},
        compiler_params=_params("arbitrary"),
    )(order, x, shift, scale, norm_g, wt_all)


def _rope_freqs():
    inv_freq = ROPE_THETA ** (-jnp.arange(0, HEAD_DIM, 2, dtype=F32) / HEAD_DIM)
    return jnp.tile(inv_freq, 4).reshape(1, 128)


class _RopeTables:
    def __init__(self, freq_ref, rows_ref, state_ref, last_ref):
        self.freq, self.rows, self.state, self.last = freq_ref, rows_ref, state_ref, last_ref

    def start(self, block, direction):
        ang = lax.broadcasted_iota(jnp.int32, (CHUNK, 128), 0).astype(F32) * self.freq[...]
        self.rows[0] = jnp.cos(ang)
        self.rows[1] = jnp.sin(ang)
        base = jnp.asarray(block * CHUNK, dtype=F32) * self.freq[...]
        turn = float(direction * CHUNK) * self.freq[...]
        self.state[0:1, :] = jnp.cos(base)
        self.state[1:2, :] = jnp.sin(base)
        self.state[2:3, :] = jnp.cos(turn)
        self.state[3:4, :] = jnp.sin(turn)

    def step(self):
        c, s, ct, st = (self.state[k:k + 1, :] for k in range(4))
        self.state[0:1, :] = c * ct - s * st
        self.state[1:2, :] = s * ct + c * st

    def tables(self):
        c, s = self.state[0:1, :], self.state[1:2, :]
        cos = c * self.rows[0] - s * self.rows[1]
        sin = s * self.rows[0] + c * self.rows[1]
        first_half = (lax.broadcasted_iota(jnp.int32, (1, 128), 1) & (HEAD_DIM - 1)) < HEAD_DIM // 2
        return cos, jnp.where(first_half, -sin, 0.0), jnp.where(first_half, 0.0, sin)

    def keep(self, tabs):
        for k in range(3):
            self.last[k] = tabs[k]

    def kept(self):
        return tuple(self.last[k] for k in range(3))


def _rope_scratch():
    return [pltpu.VMEM((2, CHUNK, 128), F32), pltpu.VMEM((8, 128), F32), pltpu.VMEM((3, CHUNK, 128), F32)]


def _rope(v, cos, sin_lo, sin_hi):
    width = v.shape[1]
    rep = (1, width // 128)
    return (v * jnp.tile(cos, rep) + pltpu.roll(v, width - 32, 1) * jnp.tile(sin_lo, rep)
            + pltpu.roll(v, 32, 1) * jnp.tile(sin_hi, rep))


def _rope_bwd(d, cos, sin_lo, sin_hi):
    width = d.shape[1]
    rep = (1, width // 128)
    return (d * jnp.tile(cos, rep) + pltpu.roll(d * jnp.tile(sin_lo, rep), 32, 1)
            + pltpu.roll(d * jnp.tile(sin_hi, rep), width - 32, 1))


def _layer_norm(v, g, b):
    mu = jnp.mean(v, axis=-1, keepdims=True)
    vc = v - mu
    rstd = lax.rsqrt(jnp.mean(vc * vc, axis=-1, keepdims=True) + NORM_EPS)
    vhat = vc * rstd
    return vhat * g + b, vhat, rstd


def _tril_bf16(w_ref, g):
    t = lax.broadcasted_iota(jnp.int32, (CHUNK, CHUNK), 0)
    tp = lax.broadcasted_iota(jnp.int32, (CHUNK, CHUNK), 1)
    return jnp.where(tp <= t, w_ref[g], 0.0).astype(BF16)


def _bias_columns(b_ref, out_ref):
    for g in range(A_GROUPS):
        out_ref[g] = jnp.broadcast_to(b_ref[pl.ds(g, 1), :], (CHUNK, CHUNK)).T


def _from_prev():
    r = lax.broadcasted_iota(jnp.int32, (CHUNK, 4 * CHUNK), 0)
    i = lax.broadcasted_iota(jnp.int32, (CHUNK, 4 * CHUNK), 1) & (CHUNK - 1)
    return r > i


def _set_unfold_masks(mask_ref):
    prev = _from_prev()
    mask_ref[0] = jnp.where(prev, 1.0, 0.0).astype(BF16)
    mask_ref[1] = jnp.where(prev, 0.0, 1.0).astype(BF16)


def _fold_band(t, from_prev):
    return jnp.where(from_prev, t[:CHUNK], t[CHUNK:])


def _unfold_band(t, mask_ref):
    return jnp.concatenate([t * mask_ref[0], t * mask_ref[1]], axis=0)


def _low_lanes():
    return lax.broadcasted_iota(jnp.int32, (1, 128), 1) < HEAD_DIM


def _stack_heads(pair_a, pair_b):
    lo = _low_lanes()
    return jnp.concatenate([jnp.where(lo, pair_a, 0.0), jnp.where(lo, 0.0, pair_a),
                            jnp.where(lo, pair_b, 0.0), jnp.where(lo, 0.0, pair_b)], axis=0).astype(BF16)


def _heads_to_lanes(per_group):
    rows = [t[:, r * CHUNK:(r + 1) * CHUNK] for t in per_group for r in range(4)]
    return jnp.concatenate(rows, axis=0).T


def _dup_kv_head(band, gk):
    pair = band[:, (gk // 2) * 128:(gk // 2 + 1) * 128]
    lo = _low_lanes()
    one = jnp.where(lo if gk % 2 == 0 else jnp.logical_not(lo), pair, 0.0)
    return (one + pltpu.roll(one, HEAD_DIM, 1)).astype(BF16)


def _fold_kv_head(dup_grad, gk):
    both = dup_grad + pltpu.roll(dup_grad, HEAD_DIM, 1)
    lo = _low_lanes()
    return jnp.where(lo if gk % 2 == 0 else jnp.logical_not(lo), both, 0.0)


def _attn_probs(q_st, k_dup, sink_row, from_prev, first_block):
    s = lax.dot_general(k_dup, q_st, NT, preferred_element_type=F32)
    no_prev = jnp.where(first_block, -jnp.inf, 0.0)
    s = jnp.where(from_prev, s[:CHUNK] + no_prev, s[CHUNK:])
    m = jnp.maximum(jnp.max(s, axis=0, keepdims=True), sink_row)
    p = jnp.exp(s - m)
    e_sink = jnp.exp(sink_row - m)
    inv = 1.0 / (jnp.sum(p, axis=0, keepdims=True) + e_sink)
    return p * inv, e_sink * inv


def _sink_row(sinks_ref, gk):
    return jnp.concatenate([jnp.full((1, CHUNK), sinks_ref[4 * gk + r], F32) for r in range(4)], axis=1)


def _mixer_specs(nb, rev):
    def blk(i):
        return nb - 1 - i if rev else i

    def prev(i):
        return jnp.maximum(blk(i) - 1, 0)

    return dict(
        cur=pl.BlockSpec((CHUNK, D_IN), lambda i, *_: (blk(i), 0)),
        prev_kv=pl.BlockSpec((CHUNK, 2 * 256), lambda i, *_: (prev(i), OFF_K // 512)),
        freq=pl.BlockSpec((1, 128), lambda i, *_: (0, 0)),
        vec=pl.BlockSpec((1, D_A), lambda i, *_: (0, 0)),
        wsp=pl.BlockSpec((A_GROUPS, CHUNK, CHUNK), lambda i, *_: (0, 0, 0)),
        bsp=pl.BlockSpec((A_GROUPS, CHUNK), lambda i, *_: (0, 0)),
        smem=pl.BlockSpec(memory_space=pltpu.SMEM),
        blk=blk,
    )


def _mixer_fwd(proj, freqs, ln_g, ln_b, w_sp, b_sp, sinks, wo_all):
    s = proj.shape[0]
    nb = s // CHUNK
    sp = _mixer_specs(nb, rev=False)

    def body(cur_ref, pkv_ref, freq_ref, lg_ref, lb_ref, w_ref, b_ref, sinks_ref, wo_in, y_ref, wo_ref,
             bcol, mask, rope_rows, rope_state, rope_last, send_sems, recv_sems):
        i = pl.program_id(0)
        gather = _InPlaceGather(wo_ref, send_sems, recv_sems)
        rope = _RopeTables(freq_ref, rope_rows, rope_state, rope_last)

        @pl.when(i == 0)
        def _():
            gather.start()
            _bias_columns(b_ref, bcol)
            _set_unfold_masks(mask)
            rope.start(-1, 1)
            rope_last[...] = jnp.zeros_like(rope_last)

        @pl.when(i == (3 * nb) // 4)
        def _():
            for j in range(3):
                gather.pass_on(j)

        vln, _, _ = _layer_norm(cur_ref[:, OFF_VA:OFF_ZA].astype(F32), lg_ref[...], lb_ref[...])
        vln = vln.astype(BF16)
        for g in range(A_GROUPS):
            cols = slice(g * 128, (g + 1) * 128)
            sg = jnp.dot(_tril_bf16(w_ref, g), vln[:, cols], preferred_element_type=F32) + bcol[g]
            u = cur_ref[:, OFF_U + g * 128:OFF_U + (g + 1) * 128].astype(F32)
            z = cur_ref[:, OFF_ZA + g * 128:OFF_ZA + (g + 1) * 128].astype(F32)
            y_ref[:, cols] = (u * sg * (z * _sigmoid(z))).astype(BF16)

        rope.step()
        cur_t, prev_t = rope.tables(), rope.kept()
        rope.keep(cur_t)
        qr = _rope(cur_ref[:, OFF_Q:OFF_K].astype(F32), *cur_t) * ATTN_SCALE
        kr = jnp.concatenate([_rope(pkv_ref[:, 0:256].astype(F32), *prev_t),
                              _rope(cur_ref[:, OFF_K:OFF_V].astype(F32), *cur_t)], axis=0)
        v_t = jnp.concatenate([pkv_ref[:, 256:512], cur_ref[:, OFF_V:OFF_ZB]], axis=0).astype(F32).T.astype(BF16)
        outs = []
        from_prev = _from_prev()
        for gk in range(N_KV_HEADS):
            q_st = _stack_heads(qr[:, (2 * gk) * 128:(2 * gk + 1) * 128], qr[:, (2 * gk + 1) * 128:(2 * gk + 2) * 128])
            probs, _ = _attn_probs(q_st, _dup_kv_head(kr, gk), _sink_row(sinks_ref, gk), from_prev, i == 0)
            outs.append(jnp.dot(v_t[gk * HEAD_DIM:(gk + 1) * HEAD_DIM], _unfold_band(probs.astype(BF16), mask),
                                preferred_element_type=F32))
        zb = cur_ref[:, OFF_ZB:D_IN].astype(F32)
        y_ref[:, D_A:D_MODEL] = (_heads_to_lanes(outs) * (zb * _sigmoid(zb))).astype(BF16)

        @pl.when(i == nb - 1)
        def _():
            gather.wait_sibling(0)
            for j in range(3):
                gather.wait_sibling(4 + j)
            gather.wait_sends()

    hbm = pl.BlockSpec(memory_space=pl.ANY)
    return pl.pallas_call(
        body, name="mixer_fwd", grid=(nb,),
        in_specs=[sp["cur"], sp["prev_kv"], sp["freq"], sp["vec"], sp["vec"], sp["wsp"], sp["bsp"], sp["smem"], hbm],
        out_specs=[pl.BlockSpec((CHUNK, D_MODEL), lambda i: (i, 0)), hbm],
        out_shape=[SDS((s, D_MODEL), BF16), SDS(wo_all.shape, wo_all.dtype)],
        scratch_shapes=[pltpu.VMEM((A_GROUPS, CHUNK, CHUNK), F32), pltpu.VMEM((2, CHUNK, 4 * CHUNK), BF16),
                        *_rope_scratch(), *_gather_scratch()],
        input_output_aliases={8: 1},
        compiler_params=_params("arbitrary"),
    )(proj, proj, freqs, ln_g, ln_b, w_sp, b_sp, sinks, wo_all)


def _out_proj_loss(y, x, target, wo, gate, shift_f, scale_f, fng):
    s = y.shape[0]
    tm, tr = 256, 128
    nt = s // tm

    def body(y_ref, x_ref, t_ref, wo_ref, gate_ref, sh_ref, sc_ref, g_ref, dx1_ref, do_ref, dy_ref, sums_ref,
             do_last, do_work):
        i = pl.program_id(0)

        @pl.when(i == 0)
        def _():
            sums_ref[...] = jnp.zeros_like(sums_ref)
            do_last[...] = jnp.zeros_like(do_last)

        do_work[...] = do_last[...]
        o = jnp.dot(y_ref[...], wo_ref[...], preferred_element_type=F32)
        gate, g, sh = gate_ref[...], g_ref[...], sh_ref[...]
        one_sc = 1.0 + sc_ref[...]
        cs, inv_d = g * one_sc, 1.0 / D_MODEL

        def rowsum(v):
            return jnp.sum(v, axis=0, keepdims=True)

        sums = [jnp.zeros((1, D_MODEL), F32) for _ in range(4)]
        for c in range(tm // tr):
            rows = slice(c * tr, (c + 1) * tr)
            oc = o[rows]
            x1 = x_ref[rows, :] + gate * oc
            r = lax.rsqrt(jnp.sum(x1 * x1, axis=-1, keepdims=True) * inv_d + NORM_EPS)
            x1n = x1 * r
            diff = x1n * cs + sh - t_ref[rows, :]
            w = diff * x1n
            lane_sum = jnp.sum(w * cs, axis=-1, keepdims=True)
            dx1 = (diff * cs) * (r * inv_d) - x1n * (r * lane_sum * (inv_d * inv_d))
            dx1_ref[rows, :] = dx1
            do = (dx1 * gate).astype(BF16)
            do_ref[rows, :] = do
            do_last[rows, :] = do
            for k, v in enumerate((dx1 * oc, diff, w, diff * diff)):
                sums[k] = sums[k] + rowsum(v)
        live = jnp.where(i < nt, 1.0, 0.0)
        for row, v in ((SUM_GATE, sums[0]), (SUM_SHIFT_F, inv_d * sums[1]), (SUM_SCALE_F, inv_d * (sums[2] * g)),
                       (SUM_FNG, inv_d * (sums[2] * one_sc)), (SUM_SQ_ERR, sums[3])):
            sums_ref[row:row + 1, :] += live * v
        dy_ref[...] = lax.dot_general(do_work[...], wo_ref[...], NT, preferred_element_type=F32).astype(BF16)

    tile = pl.BlockSpec((tm, D_MODEL), lambda i: (jnp.minimum(i, nt - 1), 0))
    row = pl.BlockSpec((1, D_MODEL), lambda i: (0, 0))
    return pl.pallas_call(
        body, name="out_proj_loss", grid=(nt + 1,),
        in_specs=[tile, tile, tile, pl.BlockSpec((D_MODEL, D_MODEL), lambda i: (0, 0)), row, row, row, row],
        out_specs=[tile, tile, pl.BlockSpec((tm, D_MODEL), lambda i: (jnp.maximum(i - 1, 0), 0)),
                   pl.BlockSpec((8, D_MODEL), lambda i: (0, 0))],
        out_shape=[SDS((s, D_MODEL), F32), SDS((s, D_MODEL), BF16), SDS((s, D_MODEL), BF16), SDS((8, D_MODEL), F32)],
        scratch_shapes=[pltpu.VMEM((tm, D_MODEL), BF16), pltpu.VMEM((tm, D_MODEL), BF16)],
        compiler_params=_params("arbitrary"),
    )(y, x, target, wo, gate, shift_f, scale_f, fng)


ROW_DBSP, ROW_DSINKS, MISC_ROWS = 0, 8, 32


def _mixer_bwd(me, proj, dy, freqs, ln_g, ln_b, w_sp, b_sp, sinks, pair):
    s = proj.shape[0]
    nb = s // CHUNK
    sp = _mixer_specs(nb, rev=True)

    def body(me_ref, cur_ref, pkv_ref, dy_ref, freq_ref, lg_ref, lb_ref, w_ref, b_ref, sinks_ref, pair_ref,
             dproj_ref, dln_ref, dw_ref, misc_ref, parts_ref, bcol, dbcol, carry, mask, rope_rows, rope_state,
             rope_last, send_sems, recv_sems):
        i = pl.program_id(0)
        block = nb - 1 - i
        rope = _RopeTables(freq_ref, rope_rows, rope_state, rope_last)

        @pl.when(i == 0)
        def _():
            for cp in _chip_scatter(pair_ref, parts_ref, send_sems, recv_sems):
                cp.start()
            _bias_columns(b_ref, bcol)
            _set_unfold_masks(mask)
            rope.start(nb - 1, -1)
            rope.keep(rope.tables())
            dbcol[...] = jnp.zeros_like(dbcol)
            carry[...] = jnp.zeros_like(carry)
            dln_ref[...] = jnp.zeros_like(dln_ref)
            dw_ref[...] = jnp.zeros_like(dw_ref)
            misc_ref[...] = jnp.zeros_like(misc_ref)

        vln, vhat, rstd = _layer_norm(cur_ref[:, OFF_VA:OFF_ZA].astype(F32), lg_ref[...], lb_ref[...])
        vln = vln.astype(BF16)
        d_vln = []
        for g in range(A_GROUPS):
            cols = slice(g * 128, (g + 1) * 128)
            w_g = _tril_bf16(w_ref, g)
            sg = jnp.dot(w_g, vln[:, cols], preferred_element_type=F32) + bcol[g]
            u = cur_ref[:, OFF_U + g * 128:OFF_U + (g + 1) * 128].astype(F32)
            z = cur_ref[:, OFF_ZA + g * 128:OFF_ZA + (g + 1) * 128].astype(F32)
            dya = dy_ref[:, cols].astype(F32)
            sig = _sigmoid(z)
            d_ya = dya * (z * sig)
            dproj_ref[:, OFF_ZA + g * 128:OFF_ZA + (g + 1) * 128] = (
                dya * (u * sg) * (sig * (1.0 + z * (1.0 - sig)))).astype(BF16)
            dproj_ref[:, OFF_U + g * 128:OFF_U + (g + 1) * 128] = (d_ya * sg).astype(BF16)
            d_s = d_ya * u
            dbcol[g] += d_s
            d_sb = d_s.astype(BF16)
            dw_ref[g] += lax.dot_general(d_sb, vln[:, cols], NT, preferred_element_type=F32)
            d_vln.append(lax.dot_general(w_g, d_sb, TN, preferred_element_type=F32))
        d_vln = jnp.concatenate(d_vln, axis=1)
        dln_ref[0:1, :] += jnp.sum(d_vln * vhat, axis=0, keepdims=True)
        dln_ref[1:2, :] += jnp.sum(d_vln, axis=0, keepdims=True)
        d_vhat = d_vln * lg_ref[...]
        d_va = rstd * (d_vhat - jnp.mean(d_vhat, axis=-1, keepdims=True)
                       - vhat * jnp.mean(d_vhat * vhat, axis=-1, keepdims=True))
        dproj_ref[:, OFF_VA:OFF_ZA] = d_va.astype(BF16)

        cur_t = rope.kept()
        rope.step()
        prev_t = rope.tables()
        rope.keep(prev_t)
        band_t = tuple(jnp.concatenate([p, c], axis=0) for p, c in zip(prev_t, cur_t))
        qr = _rope(cur_ref[:, OFF_Q:OFF_K].astype(F32), *cur_t) * ATTN_SCALE
        kr = jnp.concatenate([_rope(pkv_ref[:, 0:256].astype(F32), *prev_t),
                              _rope(cur_ref[:, OFF_K:OFF_V].astype(F32), *cur_t)], axis=0)
        vb = jnp.concatenate([pkv_ref[:, 256:512], cur_ref[:, OFF_V:OFF_ZB]], axis=0).astype(F32)
        k_t, v_t = (kr.T * ATTN_SCALE).astype(BF16), vb.T.astype(BF16)
        zb = cur_ref[:, OFF_ZB:D_IN].astype(F32)
        dyb = dy_ref[:, D_A:D_MODEL].astype(F32)
        sig = _sigmoid(zb)
        d_yb = dyb * (zb * sig)
        outs, dqs = [], []
        dk_pairs = [jnp.zeros((2 * CHUNK, 128), F32) for _ in range(2)]
        dv_pairs = [jnp.zeros((2 * CHUNK, 128), F32) for _ in range(2)]
        from_prev = _from_prev()
        for gk in range(N_KV_HEADS):
            heads = slice(gk * HEAD_DIM, (gk + 1) * HEAD_DIM)
            q_st = _stack_heads(qr[:, (2 * gk) * 128:(2 * gk + 1) * 128], qr[:, (2 * gk + 1) * 128:(2 * gk + 2) * 128])
            k_dup, v_dup = _dup_kv_head(kr, gk), _dup_kv_head(vb, gk)
            probs, p_sink = _attn_probs(q_st, k_dup, _sink_row(sinks_ref, gk), from_prev, block == 0)
            probs_b = _unfold_band(probs.astype(BF16), mask)
            outs.append(jnp.dot(v_t[heads], probs_b, preferred_element_type=F32))
            do_st = _stack_heads(d_yb[:, (2 * gk) * 128:(2 * gk + 1) * 128], d_yb[:, (2 * gk + 1) * 128:(2 * gk + 2) * 128])
            dp = _fold_band(lax.dot_general(v_dup, do_st, NT, preferred_element_type=F32), from_prev)
            delta = jnp.sum(probs * dp, axis=0, keepdims=True)
            ds = _unfold_band((probs * (dp - delta)).astype(BF16), mask)
            d_sink = -p_sink * delta
            for r in range(4):
                row = ROW_DSINKS + 4 * gk + r
                misc_ref[row:row + 1, :] += jnp.broadcast_to(
                    jnp.sum(d_sink[:, r * CHUNK:(r + 1) * CHUNK], axis=1, keepdims=True), (1, 128))
            dqs.append(jnp.dot(k_t[heads], ds, preferred_element_type=F32))
            dk_pairs[gk // 2] += _fold_kv_head(jnp.dot(ds, q_st, preferred_element_type=F32), gk)
            dv_pairs[gk // 2] += _fold_kv_head(jnp.dot(probs_b, do_st, preferred_element_type=F32), gk)
        dproj_ref[:, OFF_ZB:D_IN] = (dyb * _heads_to_lanes(outs) * (sig * (1.0 + zb * (1.0 - sig)))).astype(BF16)
        dproj_ref[:, OFF_Q:OFF_K] = _rope_bwd(_heads_to_lanes(dqs), *cur_t).astype(BF16)
        dk_band = _rope_bwd(jnp.concatenate(dk_pairs, axis=1), *band_t)
        dv_band = jnp.concatenate(dv_pairs, axis=1)
        dproj_ref[:, OFF_K:OFF_V] = (dk_band[CHUNK:] + carry[:, 0:256]).astype(BF16)
        dproj_ref[:, OFF_V:OFF_ZB] = (dv_band[CHUNK:] + carry[:, 256:512]).astype(BF16)
        carry[:, 0:256] = dk_band[:CHUNK]
        carry[:, 256:512] = dv_band[:CHUNK]

        @pl.when(i == nb - 1)
        def _():
            t = lax.broadcasted_iota(jnp.int32, (CHUNK, CHUNK), 0)
            tp = lax.broadcasted_iota(jnp.int32, (CHUNK, CHUNK), 1)
            for g in range(A_GROUPS):
                dw_ref[g] = jnp.where(tp <= t, dw_ref[g], 0.0)
                misc_ref[pl.ds(ROW_DBSP + g, 1), :] = jnp.sum(dbcol[g].T, axis=0, keepdims=True)
            scatter = _chip_scatter(pair_ref, parts_ref, send_sems, recv_sems)
            for cp in scatter:
                cp.wait_recv()
            for cp in scatter:
                cp.wait_send()

    blk = sp["blk"]
    hbm = pl.BlockSpec(memory_space=pl.ANY)
    return pl.pallas_call(
        body, name="mixer_bwd",
        grid_spec=pltpu.PrefetchScalarGridSpec(
            num_scalar_prefetch=1, grid=(nb,),
            in_specs=[sp["cur"], sp["prev_kv"], pl.BlockSpec((CHUNK, D_MODEL), lambda i, me_ref: (blk(i), 0)),
                      sp["freq"], sp["vec"], sp["vec"], sp["wsp"], sp["bsp"], sp["smem"], hbm],
            out_specs=[pl.BlockSpec((CHUNK, D_IN), lambda i, me_ref: (blk(i), 0)),
                       pl.BlockSpec((8, D_A), lambda i, me_ref: (me_ref[0], 0)),
                       pl.BlockSpec((A_GROUPS, CHUNK, CHUNK), lambda i, me_ref: (me_ref[0], 0, 0)),
                       pl.BlockSpec((MISC_ROWS, 128), lambda i, me_ref: (me_ref[0], 0)), hbm],
            scratch_shapes=[pltpu.VMEM((A_GROUPS, CHUNK, CHUNK), F32), pltpu.VMEM((A_GROUPS, CHUNK, CHUNK), F32),
                            pltpu.VMEM((CHUNK, 512), F32), pltpu.VMEM((2, CHUNK, 4 * CHUNK), BF16),
                            *_rope_scratch(), *_scatter_scratch()]),
        out_shape=[SDS((s, D_IN), BF16), SDS((N_DEV * 8, D_A), F32), SDS((N_DEV * A_GROUPS, CHUNK, CHUNK), F32),
                   SDS((N_DEV * MISC_ROWS, 128), F32), SDS((3,) + pair.shape[1:], pair.dtype)],
        compiler_params=_params("arbitrary"),
    )(me, proj, proj, dy, freqs, ln_g, ln_b, w_sp, b_sp, sinks, pair)


def _wgrad_pair(name, a, b, gathers=()):
    s, m = a.shape
    n = b.shape[1]
    bm, half = m // 4, m // 8
    bt = min(1024, s)
    steps = s // bt
    last = 4 * steps
    n_g = len(gathers)

    def body(*refs):
        a_ref, b_ref = refs[:2]
        out_ref, bufs = refs[2 + n_g], refs[3 + n_g:3 + 2 * n_g]
        acc, kept, got, sent, send_sems, recv_sems = refs[3 + 2 * n_g:9 + 2 * n_g]
        sems = refs[9 + 2 * n_g:]
        g = pl.program_id(0)
        tile, t = g // steps, g % steps
        mx, my, mc = _mesh_pos()
        jobs = [_InPlaceGather(bufs[k], sems[2 * k], sems[2 * k + 1]) for k in range(n_g)]

        def exchange(q):
            return pltpu.make_async_remote_copy(src_ref=sent, dst_ref=got.at[q % 2], send_sem=send_sems.at[q],
                                                recv_sem=recv_sems.at[q], device_id=(mx, my, 1 - mc),
                                                device_id_type=MESH)

        @pl.when(g == 0)
        def _():
            for job in jobs:
                job.start()

        @pl.when(g == 2 * steps)
        def _():
            for job in jobs:
                for j in range(3):
                    job.pass_on(j)

        @pl.when(g < last)
        def _():
            prod = lax.dot_general(a_ref[...], b_ref[...], TN, preferred_element_type=F32)

            @pl.when(t == 0)
            def _():
                acc[...] = prod

            @pl.when(t > 0)
            def _():
                acc[...] += prod

            @pl.when(t == steps - 1)
            def _():
                @pl.when(tile > 0)
                def _():
                    exchange(tile - 1).wait_send()

                kept[tile % 2] = acc[pl.ds(pl.multiple_of(mc * half, 8), half), :].astype(BF16)
                sent[...] = acc[pl.ds(pl.multiple_of((1 - mc) * half, 8), half), :].astype(BF16)
                exchange(tile).start()

        @pl.when((t == 0) & (g > 0))
        def _():
            q = tile - 1
            exchange(q).wait_recv()
            out_ref[0] = (kept[q % 2].astype(F32) + got[q % 2].astype(F32)).astype(BF16)

        @pl.when(g == last)
        def _():
            exchange(3).wait_send()
            for job in jobs:
                job.wait_sibling(0)
                for j in range(3):
                    job.wait_sibling(4 + j)
                job.wait_sends()

    def a_tile(g):
        gg = jnp.minimum(g, last - 1)
        return (gg % steps, gg // steps)

    def b_tile(g):
        return (jnp.minimum(g, last - 1) % steps, 0)

    hbm = pl.BlockSpec(memory_space=pl.ANY)
    outs = pl.pallas_call(
        body, name=name, grid=(last + 1,),
        in_specs=[pl.BlockSpec((bt, bm), a_tile), pl.BlockSpec((bt, n), b_tile)] + [hbm] * n_g,
        out_specs=[pl.BlockSpec((1, half, n), lambda g: (jnp.maximum(g - 1, 0) // steps, 0, 0))] + [hbm] * n_g,
        out_shape=[SDS((4, half, n), BF16)] + [SDS(gb.shape, gb.dtype) for gb in gathers],
        scratch_shapes=[pltpu.VMEM((bm, n), F32), pltpu.VMEM((2, half, n), BF16), pltpu.VMEM((2, half, n), BF16),
                        pltpu.VMEM((half, n), BF16), pltpu.SemaphoreType.DMA((4,)), pltpu.SemaphoreType.DMA((4,))]
        + _gather_scratch() * n_g,
        input_output_aliases={2 + k: 1 + k for k in range(n_g)},
        compiler_params=_params("arbitrary"),
    )(a, b, *gathers)
    return outs[0], outs[1:]


def _in_proj_bwd(dproj, wt, x, dx1, scale, norm_g, sums_o, pair):
    s = x.shape[0]
    tm, tk, tr = min(1024, s), D_IN // 4, 64
    ksteps = D_IN // tk

    def body(dp_ref, wt_ref, x_hbm, dx1_hbm, sc_ref, g_ref, so_ref, pair_ref, gx_ref, sums_ref, parts_ref, x_buf,
             dx1_buf, tile_sems, send_sems, recv_sems):
        i, k = pl.program_id(0), pl.program_id(1)

        def tile_copies():
            rows = pl.ds(pl.multiple_of(i * tm, tm), tm)
            return (pltpu.make_async_copy(x_hbm.at[rows], x_buf, tile_sems.at[0]),
                    pltpu.make_async_copy(dx1_hbm.at[rows], dx1_buf, tile_sems.at[1]))

        @pl.when((i == 0) & (k == 0))
        def _():
            for cp in _chip_scatter(pair_ref, parts_ref, send_sems, recv_sems):
                cp.start()
            sums_ref[...] = so_ref[...]

        @pl.when(k == 0)
        def _():
            for cp in tile_copies():
                cp.start()
            gx_ref[...] = jnp.dot(dp_ref[...], wt_ref[...], preferred_element_type=F32)

        @pl.when(k > 0)
        def _():
            gx_ref[...] += jnp.dot(dp_ref[...], wt_ref[...], preferred_element_type=F32)

        @pl.when(k == ksteps - 1)
        def _():
            for cp in tile_copies():
                cp.wait()
            one_sc, g = 1.0 + sc_ref[...], g_ref[...]
            cs = one_sc * g

            def chunk(j, sums):
                rows = pl.ds(pl.multiple_of(j * tr, tr), tr)
                dh, xv = gx_ref[rows, :], x_buf[rows, :]
                dhx = dh * xv
                r = lax.rsqrt(jnp.sum(xv * xv, axis=-1, keepdims=True) * (1.0 / D_MODEL) + NORM_EPS)
                coef = (r * r * r) * (jnp.sum(dhx * cs, axis=-1, keepdims=True) * (1.0 / D_MODEL))
                gx_ref[rows, :] = dx1_buf[rows, :] + r * (dh * cs) - xv * coef
                return (sums[0] + jnp.sum(dh, axis=0, keepdims=True), sums[1] + jnp.sum(dhx * r, axis=0, keepdims=True))

            zero = jnp.zeros((1, D_MODEL), F32)
            sums = lax.fori_loop(0, tm // tr, chunk, (zero, zero))
            sums_ref[SUM_SHIFT:SUM_SHIFT + 1, :] += sums[0]
            sums_ref[SUM_SCALE:SUM_SCALE + 1, :] += sums[1] * g
            sums_ref[SUM_NORM_G:SUM_NORM_G + 1, :] += sums[1] * one_sc

        @pl.when((i == s // tm - 1) & (k == ksteps - 1))
        def _():
            scatter = _chip_scatter(pair_ref, parts_ref, send_sems, recv_sems)
            for cp in scatter:
                cp.wait_recv()
            for cp in scatter:
                cp.wait_send()

    row = pl.BlockSpec((1, D_MODEL), lambda i, k: (0, 0))
    hbm = pl.BlockSpec(memory_space=pl.ANY)
    return pl.pallas_call(
        body, name="in_proj_bwd", grid=(s // tm, ksteps),
        in_specs=[pl.BlockSpec((tm, tk), lambda i, k: (i, k)), pl.BlockSpec((tk, D_MODEL), lambda i, k: (k, 0)),
                  hbm, hbm, row, row, pl.BlockSpec((8, D_MODEL), lambda i, k: (0, 0)), hbm],
        out_specs=[pl.BlockSpec((tm, D_MODEL), lambda i, k: (i, 0)), pl.BlockSpec((8, D_MODEL), lambda i, k: (0, 0)),
                   hbm],
        out_shape=[SDS((s, D_MODEL), F32), SDS((8, D_MODEL), F32), SDS((3,) + pair.shape[1:], pair.dtype)],
        scratch_shapes=[pltpu.VMEM((tm, D_MODEL), F32), pltpu.VMEM((tm, D_MODEL), F32),
                        pltpu.SemaphoreType.DMA((2,)), *_scatter_scratch()],
        compiler_params=_params("arbitrary", "arbitrary"),
    )(dproj, wt, x, dx1, scale, norm_g, sums_o, pair)


def _sum_chips(own_ref, parts_ref):
    return ((own_ref[0].astype(F32) + parts_ref[0].astype(F32)) + parts_ref[1].astype(F32)) + parts_ref[2].astype(F32)


def _adam_rows(name, chip, pair, parts, w, m, v, tr):
    rows = w.shape[0]

    def body(chip_ref, own_ref, p_ref, w_ref, m_ref, v_ref, g_ref, d_ref, nm_ref, nv_ref):
        g = _sum_chips(own_ref, p_ref)
        g_ref[...] = g
        d_ref[...], nm_ref[...], nv_ref[...] = _adamw(w_ref[...], g, m_ref[...], v_ref[...])

    blk = pl.BlockSpec((tr, D_MODEL), lambda j, chip_ref: (j, 0))
    return pl.pallas_call(
        body, name=name,
        grid_spec=pltpu.PrefetchScalarGridSpec(
            num_scalar_prefetch=1, grid=(rows // tr,),
            in_specs=[pl.BlockSpec((1, tr, D_MODEL), lambda j, chip_ref: (chip_ref[0], j, 0)),
                      pl.BlockSpec((3, tr, D_MODEL), lambda j, chip_ref: (0, j, 0)), blk, blk, blk],
            out_specs=[blk] * 4),
        out_shape=[SDS(w.shape, F32)] * 4, compiler_params=_params("parallel"),
    )(chip, pair, parts, w, m, v)


def _adam_ada(name, cact, dmod, w, m, v):
    n = w.shape[1]
    tr = 512

    def body(c_ref, dm_ref, w_ref, m_ref, v_ref, g_ref, d_ref, nm_ref, nv_ref):
        pad_c = jnp.concatenate([c_ref[...], jnp.zeros_like(c_ref)], axis=0).astype(BF16)
        pad_d = jnp.concatenate([dm_ref[...], jnp.zeros_like(dm_ref)], axis=0).astype(BF16)
        g = lax.dot_general(pad_c, pad_d, TN, preferred_element_type=F32)
        g_ref[...] = g
        d_ref[...], nm_ref[...], nv_ref[...] = _adamw(w_ref[...], g, m_ref[...], v_ref[...])

    blk = pl.BlockSpec((tr, n), lambda j: (j, 0))
    return pl.pallas_call(
        body, name=name, grid=(D_MODEL // tr,),
        in_specs=[pl.BlockSpec((N_DEV, tr), lambda j: (0, j)), pl.BlockSpec((N_DEV, n), lambda j: (0, 0)),
                  blk, blk, blk],
        out_specs=[blk] * 4, out_shape=[SDS(w.shape, F32)] * 4,
        compiler_params=_params("parallel"),
    )(cact, dmod, w, m, v)


SMALL_PARAMS = ("w_spatial", "b_spatial", "sinks", "norm_g", "ln_v_g", "ln_v_b", "final_norm_g", "b_ada", "b_ada_final")


def _adam_small(d_wsp, misc, d_ln, sums, params):
    n_p = len(SMALL_PARAMS)

    def body(*refs):
        wsp_ref, misc_ref, ln_ref, sums_ref = refs[:4]
        wmv = [refs[4 + 3 * k:7 + 3 * k] for k in range(n_p)]
        loss_ref = refs[4 + 3 * n_p]
        outs = [refs[5 + 3 * n_p + 4 * k:9 + 3 * n_p + 4 * k] for k in range(n_p)]

        def column_sum(row):
            return total(sums_ref, (row, row + 1))

        def total(ref, rows=None):
            def part(j):
                return ref[j] if rows is None else ref[j, rows[0]:rows[1], :]
            acc = part(0)
            for j in range(1, N_DEV):
                acc = acc + part(j)
            return acc

        sink_rows = total(misc_ref, (ROW_DSINKS, ROW_DSINKS + 16))
        diag = (lax.broadcasted_iota(jnp.int32, (16, 128), 0) == lax.broadcasted_iota(jnp.int32, (16, 128), 1))
        grads = dict(
            w_spatial=total(wsp_ref), b_spatial=total(misc_ref, (ROW_DBSP, ROW_DBSP + A_GROUPS)),
            sinks=jnp.sum(jnp.where(diag, sink_rows, 0.0), axis=0, keepdims=True),
            norm_g=column_sum(SUM_NORM_G), ln_v_g=total(ln_ref, (0, 1)), ln_v_b=total(ln_ref, (1, 2)),
            final_norm_g=column_sum(SUM_FNG),
            b_ada=jnp.concatenate([column_sum(SUM_SHIFT), column_sum(SUM_SCALE), column_sum(SUM_GATE)], axis=1),
            b_ada_final=jnp.concatenate([column_sum(SUM_SHIFT_F), column_sum(SUM_SCALE_F)], axis=1))
        sq_err = jnp.sum(column_sum(SUM_SQ_ERR), axis=1, keepdims=True)
        loss_ref[...] = jnp.broadcast_to(sq_err * (0.5 / D_MODEL), (1, 128))
        for k, name in enumerate(SMALL_PARAMS):
            w_ref, m_ref, v_ref = wmv[k]
            g_ref, d_ref, nm_ref, nv_ref = outs[k]
            g_ref[...] = grads[name]
            d_ref[...], nm_ref[...], nv_ref[...] = _adamw(w_ref[...], grads[name], m_ref[...], v_ref[...])

    flat = [a for name in SMALL_PARAMS for a in params[name]]
    vmem = pl.BlockSpec(memory_space=pltpu.VMEM)
    out_shape = [SDS((1, 128), F32)] + [SDS(params[name][0].shape, F32) for name in SMALL_PARAMS for _ in range(4)]
    outs = pl.pallas_call(
        body, name="adam_small", in_specs=[vmem] * (4 + len(flat)), out_specs=[vmem] * len(out_shape),
        out_shape=out_shape, compiler_params=_params(),
    )(d_wsp, misc, d_ln, sums, *flat)
    return outs[0], {name: outs[1 + 4 * k:5 + 4 * k] for k, name in enumerate(SMALL_PARAMS)}


def kernel(x, c, w_ada, b_ada, norm_g, w_in, ln_v_g, ln_v_b, w_spatial, b_spatial, sinks, w_out, w_ada_final, b_ada_final, final_norm_g, loss_target, m_w_ada, m_b_ada, m_norm_g, m_w_in, m_ln_v_g, m_ln_v_b, m_w_spatial, m_b_spatial, m_sinks, m_w_out, m_w_ada_final, m_b_ada_final, m_final_norm_g, v_w_ada, v_b_ada, v_norm_g, v_w_in, v_ln_v_g, v_ln_v_b, v_w_spatial, v_b_spatial, v_sinks, v_w_out, v_w_ada_final, v_b_ada_final, v_final_norm_g):
    me = 4 * lax.axis_index("x") + 2 * lax.axis_index("y") + lax.axis_index("c")
    x2, tgt = x[0], loss_target[0]
    fng = final_norm_g.reshape(1, D_MODEL)

    n_ada, n_ada_f = w_ada.shape[2], w_ada_final.shape[1]
    cact, mod, mod_f = _ada_exchange(c, w_ada[0], b_ada.reshape(N_DEV, n_ada), w_ada_final,
                                     b_ada_final.reshape(N_DEV, n_ada_f))
    cact = cact.reshape(N_DEV, D_MODEL)
    mod, mod_f = mod.reshape(1, 3 * D_MODEL), mod_f.reshape(1, 2 * D_MODEL)
    shift, scale, gate = mod[:, :D_MODEL], mod[:, D_MODEL:2 * D_MODEL], mod[:, 2 * D_MODEL:]
    shift_f, scale_f = mod_f[:, :D_MODEL], mod_f[:, D_MODEL:]

    wt_f32, m_wt, v_wt = (jnp.swapaxes(a, 1, 2)[0] for a in (w_in, m_w_in, v_w_in))
    xi, yi = lax.axis_index("x"), lax.axis_index("y")
    chip_order = jnp.stack([2 * xi + yi, 2 * (1 - xi) + yi, 2 * xi + 1 - yi, 2 * (1 - xi) + 1 - yi]).astype(jnp.int32)
    wt_mine, wo_mine = _prep_weights(me.reshape(1), wt_f32, w_out[0])

    freqs = _rope_freqs()
    sinks_v = sinks.reshape(16)
    h, proj, wt = _gather_in_proj(chip_order, x2, shift, scale, norm_g, wt_mine)
    y, wo = _mixer_fwd(proj, freqs, ln_v_g, ln_v_b, w_spatial[0], b_spatial[0], sinks_v, wo_mine)
    dx1, do, dy, sums_o = _out_proj_loss(y, x2, tgt, wo, gate, shift_f, scale_f, fng)

    chip = (2 * lax.axis_index("x") + lax.axis_index("y")).reshape(1)
    pair_out, _ = _wgrad_pair("wgrad_out", y, do)
    dproj, d_ln, d_wsp, misc, parts_out = _mixer_bwd(
        me.reshape(1), proj, dy, freqs, ln_v_g, ln_v_b, w_spatial[0], b_spatial[0], sinks_v, pair_out)
    pair_in, (d_ln, d_wsp, misc) = _wgrad_pair(
        "wgrad_in", dproj, h, gathers=(d_ln, d_wsp.reshape(N_DEV * A_GROUPS * CHUNK, CHUNK), misc))
    grad_x, sums, parts_in = _in_proj_bwd(dproj, wt, x2, dx1, scale, norm_g, sums_o, pair_in)
    wt_leaves = [jnp.swapaxes(a[None], 1, 2)
                 for a in _adam_rows("adam_w_in", chip, pair_in, parts_in, wt_f32, m_wt, v_wt, 176)]
    w_out_leaves = [a[None] for a in _adam_rows("adam_w_out", chip, pair_out, parts_out, w_out[0], m_w_out[0], v_w_out[0], 64)]

    (sums,) = _all_gather("gather_sums", [sums], pltpu.VMEM)
    natural = dict(w_spatial=(A_GROUPS * CHUNK, CHUNK), b_spatial=(A_GROUPS, CHUNK), sinks=(1, 16), norm_g=(1, D_MODEL),
                   ln_v_g=(1, D_A), ln_v_b=(1, D_A), final_norm_g=(1, D_MODEL), b_ada=(1, 3 * D_MODEL),
                   b_ada_final=(1, 2 * D_MODEL))
    given = dict(
        w_spatial=(w_spatial, m_w_spatial, v_w_spatial), b_spatial=(b_spatial, m_b_spatial, v_b_spatial),
        sinks=(sinks, m_sinks, v_sinks), norm_g=(norm_g, m_norm_g, v_norm_g), ln_v_g=(ln_v_g, m_ln_v_g, v_ln_v_g),
        ln_v_b=(ln_v_b, m_ln_v_b, v_ln_v_b), final_norm_g=(final_norm_g, m_final_norm_g, v_final_norm_g),
        b_ada=(b_ada, m_b_ada, v_b_ada), b_ada_final=(b_ada_final, m_b_ada_final, v_b_ada_final))
    params = {name: tuple(a.reshape(natural[name]) for a in given[name]) for name in SMALL_PARAMS}
    params["sinks"] = tuple(jnp.pad(a, ((0, 0), (0, 128 - 16))) for a in params["sinks"])
    loss, small = _adam_small(d_wsp.reshape(N_DEV, A_GROUPS * CHUNK, CHUNK), misc.reshape(N_DEV, MISC_ROWS, 128),
                              d_ln.reshape(N_DEV, 8, D_A), sums, params)
    small["sinks"] = [a[:, :16] for a in small["sinks"]]
    small = {name: [a.reshape(given[name][0].shape) for a in small[name]] for name in SMALL_PARAMS}

    dmod_all = jnp.concatenate([sums[:, SUM_SHIFT], sums[:, SUM_SCALE], sums[:, SUM_GATE]], axis=1)
    dmod_f_all = jnp.concatenate([sums[:, SUM_SHIFT_F], sums[:, SUM_SCALE_F]], axis=1)
    dmod_mine = lax.dynamic_slice_in_dim(dmod_all, me * n_ada, n_ada, axis=1)
    dmod_f_mine = lax.dynamic_slice_in_dim(dmod_f_all, me * n_ada_f, n_ada_f, axis=1)
    ada = _adam_ada("adam_w_ada", cact, dmod_mine, w_ada[0], m_w_ada[0], v_w_ada[0])
    ada_f = _adam_ada("adam_w_ada_final", cact, dmod_f_mine, w_ada_final, m_w_ada_final, v_w_ada_final)

    def leaves(k):
        return (ada[k][None], small["b_ada"][k], small["norm_g"][k], wt_leaves[k], small["ln_v_g"][k],
                small["ln_v_b"][k], small["w_spatial"][k], small["b_spatial"][k], small["sinks"][k], w_out_leaves[k],
                ada_f[k], small["b_ada_final"][k], small["final_norm_g"][k])

    return (loss[0, 0], grad_x[None], *leaves(0), *leaves(1), *leaves(2), *leaves(3))
```

```python
import jax
import jax.numpy as jnp
from jax import lax
from jax.experimental import pallas as pl
from jax.experimental.pallas import tpu as pltpu

D_MODEL = 2048
D_IN = 5632
D_A = 1024
CHUNK = 128
A_GROUPS = 8
HEAD_DIM = 64
N_KV_HEADS = 4
N_DEV = 8
ROPE_THETA = 10000.0
NORM_EPS = 1e-5
ATTN_SCALE = HEAD_DIM ** -0.5

ADAM_LR = 0.001
ADAM_B1 = 0.9
ADAM_B2 = 0.999
ADAM_EPS = 1e-08
ADAM_WD = 0.01
ADAM_STEP = 10

OFF_U, OFF_VA, OFF_ZA, OFF_Q, OFF_K, OFF_V, OFF_ZB = 0, 1024, 2048, 3072, 4096, 4352, 4608

SUM_SHIFT, SUM_SCALE, SUM_NORM_G, SUM_GATE, SUM_SHIFT_F, SUM_SCALE_F, SUM_FNG, SUM_SQ_ERR = range(8)

V7X_VMEM_LIMIT_BYTES = 56 * 1024 * 1024

F32 = jnp.float32
BF16 = jnp.bfloat16
MESH = pl.DeviceIdType.MESH
SDS = jax.ShapeDtypeStruct
NT = (((1,), (1,)), ((), ()))
TN = (((0,), (0,)), ((), ()))


def _params(*semantics):
    return pltpu.CompilerParams(dimension_semantics=semantics or None, vmem_limit_bytes=V7X_VMEM_LIMIT_BYTES)


def _mesh_pos():
    return lax.axis_index("x"), lax.axis_index("y"), lax.axis_index("c")


def _sigmoid(z):
    return 1.0 / (1.0 + jnp.exp(-z))


def _adamw(w, g, m, v):
    m = ADAM_B1 * m + (1.0 - ADAM_B1) * g
    v = ADAM_B2 * v + (1.0 - ADAM_B2) * (g * g)
    m_hat = m / (1.0 - ADAM_B1 ** ADAM_STEP)
    v_hat = v / (1.0 - ADAM_B2 ** ADAM_STEP)
    delta = -ADAM_LR * (m_hat / (jnp.sqrt(v_hat) + ADAM_EPS) + ADAM_WD * w)
    return delta, m, v


def _all_gather(name, blocks, memory_space):
    n_arr = len(blocks)

    def body(*refs):
        ins, outs = refs[:n_arr], refs[n_arr:2 * n_arr]
        send_sems, recv_sems, local_sems = refs[2 * n_arr:]
        x, y, c = _mesh_pos()
        me, sibling = (x, y, c), (x, y, 1 - c)
        chips = [(1 - x, y), (x, 1 - y), (1 - x, 1 - y)]

        def slot(p):
            return 4 * p[0] + 2 * p[1] + p[2]

        def copy(a, k, block, to, src=None):
            dst = outs[a].at[slot(block)]
            return pltpu.make_async_remote_copy(
                src_ref=dst if src is None else src, dst_ref=dst,
                send_sem=send_sems.at[a, k], recv_sem=recv_sems.at[a, k],
                device_id=to, device_id_type=MESH)

        mine = [pltpu.make_async_copy(ins[a], outs[a].at[slot(me)], local_sems.at[a]) for a in range(n_arr)]
        for cp in mine:
            cp.start()
        first = []
        for a in range(n_arr):
            first.append(copy(a, 0, me, sibling, src=ins[a]))
            first += [copy(a, 1 + j, me, (*chip, c), src=ins[a]) for j, chip in enumerate(chips)]
        for cp in first:
            cp.start()
        passed = []
        for j, chip in enumerate(chips):
            for a in range(n_arr):
                copy(a, 1 + j, (*chip, c), me).wait_recv()
                fwd = copy(a, 4 + j, (*chip, c), sibling)
                fwd.start()
                passed.append(fwd)
        for a in range(n_arr):
            copy(a, 0, sibling, me).wait_recv()
            for j, chip in enumerate(chips):
                copy(a, 4 + j, (*chip, 1 - c), me).wait_recv()
        for cp in first + passed:
            cp.wait_send()
        for cp in mine:
            cp.wait()

    spec = pl.BlockSpec(memory_space=memory_space)
    return pl.pallas_call(
        body, name=name,
        out_shape=[SDS((N_DEV,) + b.shape, b.dtype) for b in blocks],
        in_specs=[spec] * n_arr, out_specs=[spec] * n_arr,
        scratch_shapes=[pltpu.SemaphoreType.DMA((n_arr, 7)), pltpu.SemaphoreType.DMA((n_arr, 7)),
                        pltpu.SemaphoreType.DMA((n_arr,))],
        compiler_params=_params(),
    )(*blocks)


def _ada_exchange(c, w_ada, b_ada8, w_ada_f, b_ada_f8):
    n1, n2 = w_ada.shape[1], w_ada_f.shape[1]

    def body(c_ref, w1_ref, b1_ref, w2_ref, b2_ref, cact_ref, mod_ref, modf_ref,
             cact_buf, res1, res2, send1, send2, sems_s, sems_r):
        x, y, c_pos = _mesh_pos()
        me = 4 * x + 2 * y + c_pos
        flips = [(k >> 2 & 1, k >> 1 & 1, k & 1) for k in range(1, N_DEV)]

        def peer(f):
            return (1 - x if f[0] else x, 1 - y if f[1] else y, 1 - c_pos if f[2] else c_pos)

        cv = c_ref[...]
        cact = cv * _sigmoid(cv)
        cact_buf[...] = cact
        cact_ref[me] = cact

        def rdma(phase, k, src, dst, f):
            return pltpu.make_async_remote_copy(src_ref=src, dst_ref=dst, send_sem=sems_s.at[phase, k],
                                                recv_sem=sems_r.at[phase, k], device_id=peer(f), device_id_type=MESH)

        gather = [rdma(0, k, cact_buf, cact_ref.at[me], f) for k, f in enumerate(flips)]
        for cp in gather:
            cp.start()
        for cp in gather:
            cp.wait_recv()
        for cp in gather:
            cp.wait_send()

        rid = lax.broadcasted_iota(jnp.int32, (N_DEV, D_MODEL), 0)
        rows = jnp.zeros((N_DEV, D_MODEL), F32)
        for j in range(N_DEV):
            rows = jnp.where(rid == j, jnp.broadcast_to(cact_ref[j], (N_DEV, D_MODEL)), rows)
        rows = rows.astype(BF16)
        res1[...] = jnp.dot(rows, w1_ref[...].astype(BF16), preferred_element_type=F32) + b1_ref[pl.ds(me, 1), :]
        res2[...] = jnp.dot(rows, w2_ref[...].astype(BF16), preferred_element_type=F32) + b2_ref[pl.ds(me, 1), :]
        for j in range(N_DEV):
            send1[j] = res1[pl.ds(j, 1), :]
            send2[j] = res2[pl.ds(j, 1), :]
        mod_ref[me] = send1[me]
        modf_ref[me] = send2[me]
        scatter = []
        for k, f in enumerate(flips):
            to = me ^ (k + 1)
            scatter.append(rdma(1, k, send1.at[to], mod_ref.at[me], f))
            scatter.append(rdma(2, k, send2.at[to], modf_ref.at[me], f))
        for cp in scatter:
            cp.start()
        for cp in scatter:
            cp.wait_recv()
        for cp in scatter:
            cp.wait_send()

    vmem = pl.BlockSpec(memory_space=pltpu.VMEM)
    return pl.pallas_call(
        body, name="ada_exchange",
        out_shape=[SDS((N_DEV, 1, D_MODEL), F32), SDS((N_DEV, 1, n1), F32), SDS((N_DEV, 1, n2), F32)],
        in_specs=[vmem] * 5, out_specs=[vmem] * 3,
        scratch_shapes=[pltpu.VMEM((1, D_MODEL), F32), pltpu.VMEM((N_DEV, n1), F32), pltpu.VMEM((N_DEV, n2), F32),
                        pltpu.VMEM((N_DEV, 1, n1), F32), pltpu.VMEM((N_DEV, 1, n2), F32),
                        pltpu.SemaphoreType.DMA((3, 7)), pltpu.SemaphoreType.DMA((3, 7))],
        compiler_params=_params(),
    )(c, w_ada, b_ada8, w_ada_f, b_ada_f8)


def _chip_scatter(pair_ref, parts_ref, send_sems, recv_sems):
    x, y, c = _mesh_pos()
    chips = [(1 - x, y), (x, 1 - y), (1 - x, 1 - y)]
    return [pltpu.make_async_remote_copy(
        src_ref=pair_ref.at[2 * cx + cy], dst_ref=parts_ref.at[j], send_sem=send_sems.at[j], recv_sem=recv_sems.at[j],
        device_id=(cx, cy, c), device_id_type=MESH) for j, (cx, cy) in enumerate(chips)]


def _scatter_scratch():
    return [pltpu.SemaphoreType.DMA((3,)), pltpu.SemaphoreType.DMA((3,))]


def _prep_weights(me, wt, w_out):
    steps = 4

    def body(me_ref, wt_ref, wo_ref, wtb_ref, wob_ref):
        wtb_ref[...] = wt_ref[...].astype(BF16)
        wob_ref[...] = wo_ref[...].astype(BF16)

    def rows(a, mine):
        blk = (a.shape[0] // steps, a.shape[1])
        return pl.BlockSpec(blk, (lambda i, me_ref: (steps * me_ref[0] + i, 0)) if mine else (lambda i, me_ref: (i, 0)))

    return pl.pallas_call(
        body, name="prep_weights",
        grid_spec=pltpu.PrefetchScalarGridSpec(
            num_scalar_prefetch=1, grid=(steps,),
            in_specs=[rows(wt, False), rows(w_out, False)], out_specs=[rows(wt, True), rows(w_out, True)]),
        out_shape=[SDS((N_DEV * wt.shape[0], D_MODEL), BF16), SDS((N_DEV * w_out.shape[0], D_MODEL), BF16)],
        compiler_params=_params("parallel"),
    )(me, wt, w_out)


class _InPlaceGather:
    def __init__(self, buf_ref, send_sems, recv_sems, relay=False):
        self.buf, self.send_sems, self.recv_sems, self.relay = buf_ref, send_sems, recv_sems, relay
        self.n = buf_ref.shape[0] // N_DEV
        x, y, c = _mesh_pos()
        self.me, self.sibling, self.core = (x, y, c), (x, y, 1 - c), c
        self.chips = [(1 - x, y), (x, 1 - y), (1 - x, 1 - y)]
        self.relay_from = (jnp.where(c == 0, 1 - x, x), jnp.where(c == 0, y, 1 - y), c)
        self.relay_to = (jnp.where(c == 0, x, 1 - x), jnp.where(c == 0, 1 - y, y), c)

    def copy(self, k, block, to):
        start = pl.multiple_of((4 * block[0] + 2 * block[1] + block[2]) * self.n, self.n)
        rows = self.buf.at[pl.ds(start, self.n)]
        return pltpu.make_async_remote_copy(src_ref=rows, dst_ref=rows, send_sem=self.send_sems.at[k],
                                            recv_sem=self.recv_sems.at[k], device_id=to, device_id_type=MESH)

    def start(self):
        self.copy(0, self.me, self.sibling).start()
        for j, chip in enumerate(self.chips[:2] if self.relay else self.chips):
            self.copy(1 + j, self.me, (*chip, self.core)).start()

    def relay_diagonal(self):
        self.copy(3, self.relay_from, self.relay_to).start()

    def pass_on(self, j):
        self.copy(1 + j, (*self.chips[j], self.core), self.me).wait_recv()
        self.copy(4 + j, (*self.chips[j], self.core), self.sibling).start()

    def wait_sibling(self, k):
        self.copy(k, self.sibling, self.me).wait_recv()

    def wait_sends(self):
        for k in range(7):
            self.copy(k, self.me, self.sibling).wait_send()


def _gather_scratch():
    return [pltpu.SemaphoreType.DMA((7,)), pltpu.SemaphoreType.DMA((7,))]


def _gather_in_proj(order, x, shift, scale, norm_g, wt_all):
    s = x.shape[0]
    th, tm = min(512, s), min(1024, s)
    nh, ni = s // th, s // tm
    tn = D_IN // 4
    steps = nh + 4 * ni

    def body(order_ref, x_ref, shift_ref, scale_ref, g_ref, wt_in, h_ref, proj_ref, wt_ref,
             h_scr, w_buf, load_sems, send_sems, recv_sems):
        g = pl.program_id(0)
        gather = _InPlaceGather(wt_ref, send_sems, recv_sems, relay=True)

        def tile_load(slot, chip):
            return pltpu.make_async_copy(wt_ref.at[pl.ds(pl.multiple_of(chip * tn, tn), tn)], w_buf.at[slot],
                                         load_sems.at[slot])

        @pl.when(g == 0)
        def _():
            gather.start()

        @pl.when(g < nh)
        def _():
            xv = x_ref[...]
            r = lax.rsqrt(jnp.mean(xv * xv, axis=-1, keepdims=True) + NORM_EPS)
            hb = (((xv * r) * g_ref[...]) * (1.0 + scale_ref[...]) + shift_ref[...]).astype(BF16)
            h_ref[...] = hb
            h_scr[pl.ds(pl.multiple_of(g * th, th), th), :] = hb

        @pl.when(g == nh - 1)
        def _():
            gather.wait_sibling(0)
            tile_load(0, order_ref[0]).start()

        @pl.when(g >= nh)
        def _():
            t, i = (g - nh) // ni, (g - nh) % ni

            @pl.when(i == 0)
            def _():
                tile_load(t % 2, order_ref[t]).wait()

            @pl.when((i == ni - 1) & (t == 0))
            def _():
                gather.pass_on(0)
                gather.pass_on(1)
                gather.relay_diagonal()

            @pl.when((i == ni // 2) & (t == 2))
            def _():
                gather.pass_on(2)

            for j in range(3):
                @pl.when((i == ni - 1) & (t == j))
                def _():
                    gather.wait_sibling(4 + j)
                    tile_load((j + 1) % 2, order_ref[j + 1]).start()

            lhs = h_scr[pl.ds(pl.multiple_of(i * tm, tm), tm), :]
            proj_ref[...] = lax.dot_general(lhs, w_buf[t % 2], NT, preferred_element_type=F32).astype(BF16)

        @pl.when(g == steps - 1)
        def _():
            gather.wait_sends()

    def h_tile(g, order_ref):
        return (jnp.minimum(g, nh - 1), 0)

    def proj_tile(g, order_ref):
        mm = jnp.maximum(g - nh, 0)
        return (mm % ni, order_ref[mm // ni])

    row = pl.BlockSpec((1, D_MODEL), lambda g, order_ref: (0, 0))
    hbm = pl.BlockSpec(memory_space=pl.ANY)
    return pl.pallas_call(
        body, name="gather_in_proj",
        grid_spec=pltpu.PrefetchScalarGridSpec(
            num_scalar_prefetch=1, grid=(steps,),
            in_specs=[pl.BlockSpec((th, D_MODEL), h_tile), row, row, row, hbm],
            out_specs=[pl.BlockSpec((th, D_MODEL), h_tile), pl.BlockSpec((tm, tn), proj_tile), hbm],
            scratch_shapes=[pltpu.VMEM((s, D_MODEL), BF16), pltpu.VMEM((2, tn, D_MODEL), BF16),
                            pltpu.SemaphoreType.DMA((2,)), *_gather_scratch()]),
        out_shape=[SDS((s, D_MODEL), BF16), SDS((s, D_IN), BF16), SDS(wt_all.shape, BF16)],
        input_output_aliases={5: 2},
        compiler_params=_params("arbitrary"),
    )(order, x, shift, scale, norm_g, wt_all)


def _rope_freqs():
    inv_freq = ROPE_THETA ** (-jnp.arange(0, HEAD_DIM, 2, dtype=F32) / HEAD_DIM)
    return jnp.tile(inv_freq, 4).reshape(1, 128)


class _RopeTables:
    def __init__(self, freq_ref, rows_ref, state_ref, last_ref):
        self.freq, self.rows, self.state, self.last = freq_ref, rows_ref, state_ref, last_ref

    def start(self, block, direction):
        ang = lax.broadcasted_iota(jnp.int32, (CHUNK, 128), 0).astype(F32) * self.freq[...]
        self.rows[0] = jnp.cos(ang)
        self.rows[1] = jnp.sin(ang)
        base = jnp.asarray(block * CHUNK, dtype=F32) * self.freq[...]
        turn = float(direction * CHUNK) * self.freq[...]
        self.state[0:1, :] = jnp.cos(base)
        self.state[1:2, :] = jnp.sin(base)
        self.state[2:3, :] = jnp.cos(turn)
        self.state[3:4, :] = jnp.sin(turn)

    def step(self):
        c, s, ct, st = (self.state[k:k + 1, :] for k in range(4))
        self.state[0:1, :] = c * ct - s * st
        self.state[1:2, :] = s * ct + c * st

    def tables(self):
        c, s = self.state[0:1, :], self.state[1:2, :]
        cos = c * self.rows[0] - s * self.rows[1]
        sin = s * self.rows[0] + c * self.rows[1]
        first_half = (lax.broadcasted_iota(jnp.int32, (1, 128), 1) & (HEAD_DIM - 1)) < HEAD_DIM // 2
        return cos, jnp.where(first_half, -sin, 0.0), jnp.where(first_half, 0.0, sin)

    def keep(self, tabs):
        for k in range(3):
            self.last[k] = tabs[k]

    def kept(self):
        return tuple(self.last[k] for k in range(3))


def _rope_scratch():
    return [pltpu.VMEM((2, CHUNK, 128), F32), pltpu.VMEM((8, 128), F32), pltpu.VMEM((3, CHUNK, 128), F32)]


def _rope(v, cos, sin_lo, sin_hi):
    width = v.shape[1]
    rep = (1, width // 128)
    return (v * jnp.tile(cos, rep) + pltpu.roll(v, width - 32, 1) * jnp.tile(sin_lo, rep)
            + pltpu.roll(v, 32, 1) * jnp.tile(sin_hi, rep))


def _rope_bwd(d, cos, sin_lo, sin_hi):
    width = d.shape[1]
    rep = (1, width // 128)
    return (d * jnp.tile(cos, rep) + pltpu.roll(d * jnp.tile(sin_lo, rep), 32, 1)
            + pltpu.roll(d * jnp.tile(sin_hi, rep), width - 32, 1))


def _layer_norm(v, g, b):
    mu = jnp.mean(v, axis=-1, keepdims=True)
    vc = v - mu
    rstd = lax.rsqrt(jnp.mean(vc * vc, axis=-1, keepdims=True) + NORM_EPS)
    vhat = vc * rstd
    return vhat * g + b, vhat, rstd


def _tril_bf16(w_ref, g):
    t = lax.broadcasted_iota(jnp.int32, (CHUNK, CHUNK), 0)
    tp = lax.broadcasted_iota(jnp.int32, (CHUNK, CHUNK), 1)
    return jnp.where(tp <= t, w_ref[g], 0.0).astype(BF16)


def _bias_columns(b_ref, out_ref):
    for g in range(A_GROUPS):
        out_ref[g] = jnp.broadcast_to(b_ref[pl.ds(g, 1), :], (CHUNK, CHUNK)).T


def _from_prev():
    r = lax.broadcasted_iota(jnp.int32, (CHUNK, 4 * CHUNK), 0)
    i = lax.broadcasted_iota(jnp.int32, (CHUNK, 4 * CHUNK), 1) & (CHUNK - 1)
    return r > i


def _set_unfold_masks(mask_ref):
    prev = _from_prev()
    mask_ref[0] = jnp.where(prev, 1.0, 0.0).astype(BF16)
    mask_ref[1] = jnp.where(prev, 0.0, 1.0).astype(BF16)


def _fold_band(t, from_prev):
    return jnp.where(from_prev, t[:CHUNK], t[CHUNK:])


def _unfold_band(t, mask_ref):
    return jnp.concatenate([t * mask_ref[0], t * mask_ref[1]], axis=0)


def _low_lanes():
    return lax.broadcasted_iota(jnp.int32, (1, 128), 1) < HEAD_DIM


def _stack_heads(pair_a, pair_b):
    lo = _low_lanes()
    return jnp.concatenate([jnp.where(lo, pair_a, 0.0), jnp.where(lo, 0.0, pair_a),
                            jnp.where(lo, pair_b, 0.0), jnp.where(lo, 0.0, pair_b)], axis=0).astype(BF16)


def _heads_to_lanes(per_group):
    rows = [t[:, r * CHUNK:(r + 1) * CHUNK] for t in per_group for r in range(4)]
    return jnp.concatenate(rows, axis=0).T


def _dup_kv_head(band, gk):
    pair = band[:, (gk // 2) * 128:(gk // 2 + 1) * 128]
    lo = _low_lanes()
    one = jnp.where(lo if gk % 2 == 0 else jnp.logical_not(lo), pair, 0.0)
    return (one + pltpu.roll(one, HEAD_DIM, 1)).astype(BF16)


def _fold_kv_head(dup_grad, gk):
    both = dup_grad + pltpu.roll(dup_grad, HEAD_DIM, 1)
    lo = _low_lanes()
    return jnp.where(lo if gk % 2 == 0 else jnp.logical_not(lo), both, 0.0)


def _attn_probs(q_st, k_dup, sink_row, from_prev, first_block):
    s = lax.dot_general(k_dup, q_st, NT, preferred_element_type=F32)
    no_prev = jnp.where(first_block, -jnp.inf, 0.0)
    s = jnp.where(from_prev, s[:CHUNK] + no_prev, s[CHUNK:])
    m = jnp.maximum(jnp.max(s, axis=0, keepdims=True), sink_row)
    p = jnp.exp(s - m)
    e_sink = jnp.exp(sink_row - m)
    inv = 1.0 / (jnp.sum(p, axis=0, keepdims=True) + e_sink)
    return p * inv, e_sink * inv


def _sink_row(sinks_ref, gk):
    return jnp.concatenate([jnp.full((1, CHUNK), sinks_ref[4 * gk + r], F32) for r in range(4)], axis=1)


def _mixer_specs(nb, rev):
    def blk(i):
        return nb - 1 - i if rev else i

    def prev(i):
        return jnp.maximum(blk(i) - 1, 0)

    return dict(
        cur=pl.BlockSpec((CHUNK, D_IN), lambda i, *_: (blk(i), 0)),
        prev_kv=pl.BlockSpec((CHUNK, 2 * 256), lambda i, *_: (prev(i), OFF_K // 512)),
        freq=pl.BlockSpec((1, 128), lambda i, *_: (0, 0)),
        vec=pl.BlockSpec((1, D_A), lambda i, *_: (0, 0)),
        wsp=pl.BlockSpec((A_GROUPS, CHUNK, CHUNK), lambda i, *_: (0, 0, 0)),
        bsp=pl.BlockSpec((A_GROUPS, CHUNK), lambda i, *_: (0, 0)),
        smem=pl.BlockSpec(memory_space=pltpu.SMEM),
        blk=blk,
    )


def _mixer_fwd(proj, freqs, ln_g, ln_b, w_sp, b_sp, sinks, wo_all):
    s = proj.shape[0]
    nb = s // CHUNK
    sp = _mixer_specs(nb, rev=False)

    def body(cur_ref, pkv_ref, freq_ref, lg_ref, lb_ref, w_ref, b_ref, sinks_ref, wo_in, y_ref, wo_ref,
             bcol, mask, rope_rows, rope_state, rope_last, send_sems, recv_sems):
        i = pl.program_id(0)
        gather = _InPlaceGather(wo_ref, send_sems, recv_sems)
        rope = _RopeTables(freq_ref, rope_rows, rope_state, rope_last)

        @pl.when(i == 0)
        def _():
            gather.start()
            _bias_columns(b_ref, bcol)
            _set_unfold_masks(mask)
            rope.start(-1, 1)
            rope_last[...] = jnp.zeros_like(rope_last)

        @pl.when(i == (3 * nb) // 4)
        def _():
            for j in range(3):
                gather.pass_on(j)

        vln, _, _ = _layer_norm(cur_ref[:, OFF_VA:OFF_ZA].astype(F32), lg_ref[...], lb_ref[...])
        vln = vln.astype(BF16)
        for g in range(A_GROUPS):
            cols = slice(g * 128, (g + 1) * 128)
            sg = jnp.dot(_tril_bf16(w_ref, g), vln[:, cols], preferred_element_type=F32) + bcol[g]
            u = cur_ref[:, OFF_U + g * 128:OFF_U + (g + 1) * 128].astype(F32)
            z = cur_ref[:, OFF_ZA + g * 128:OFF_ZA + (g + 1) * 128].astype(F32)
            y_ref[:, cols] = (u * sg * (z * _sigmoid(z))).astype(BF16)

        rope.step()
        cur_t, prev_t = rope.tables(), rope.kept()
        rope.keep(cur_t)
        qr = _rope(cur_ref[:, OFF_Q:OFF_K].astype(F32), *cur_t) * ATTN_SCALE
        kr = jnp.concatenate([_rope(pkv_ref[:, 0:256].astype(F32), *prev_t),
                              _rope(cur_ref[:, OFF_K:OFF_V].astype(F32), *cur_t)], axis=0)
        v_t = jnp.concatenate([pkv_ref[:, 256:512], cur_ref[:, OFF_V:OFF_ZB]], axis=0).astype(F32).T.astype(BF16)
        outs = []
        from_prev = _from_prev()
        for gk in range(N_KV_HEADS):
            q_st = _stack_heads(qr[:, (2 * gk) * 128:(2 * gk + 1) * 128], qr[:, (2 * gk + 1) * 128:(2 * gk + 2) * 128])
            probs, _ = _attn_probs(q_st, _dup_kv_head(kr, gk), _sink_row(sinks_ref, gk), from_prev, i == 0)
            outs.append(jnp.dot(v_t[gk * HEAD_DIM:(gk + 1) * HEAD_DIM], _unfold_band(probs.astype(BF16), mask),
                                preferred_element_type=F32))
        zb = cur_ref[:, OFF_ZB:D_IN].astype(F32)
        y_ref[:, D_A:D_MODEL] = (_heads_to_lanes(outs) * (zb * _sigmoid(zb))).astype(BF16)

        @pl.when(i == nb - 1)
        def _():
            gather.wait_sibling(0)
            for j in range(3):
                gather.wait_sibling(4 + j)
            gather.wait_sends()

    hbm = pl.BlockSpec(memory_space=pl.ANY)
    return pl.pallas_call(
        body, name="mixer_fwd", grid=(nb,),
        in_specs=[sp["cur"], sp["prev_kv"], sp["freq"], sp["vec"], sp["vec"], sp["wsp"], sp["bsp"], sp["smem"], hbm],
        out_specs=[pl.BlockSpec((CHUNK, D_MODEL), lambda i: (i, 0)), hbm],
        out_shape=[SDS((s, D_MODEL), BF16), SDS(wo_all.shape, wo_all.dtype)],
        scratch_shapes=[pltpu.VMEM((A_GROUPS, CHUNK, CHUNK), F32), pltpu.VMEM((2, CHUNK, 4 * CHUNK), BF16),
                        *_rope_scratch(), *_gather_scratch()],
        input_output_aliases={8: 1},
        compiler_params=_params("arbitrary"),
    )(proj, proj, freqs, ln_g, ln_b, w_sp, b_sp, sinks, wo_all)


def _out_proj_loss(y, x, target, wo, gate, shift_f, scale_f, fng):
    s = y.shape[0]
    tm, tr = 256, 128
    nt = s // tm

    def body(y_ref, x_ref, t_ref, wo_ref, gate_ref, sh_ref, sc_ref, g_ref, dx1_ref, do_ref, dy_ref, sums_ref,
             do_last, do_work):
        i = pl.program_id(0)

        @pl.when(i == 0)
        def _():
            sums_ref[...] = jnp.zeros_like(sums_ref)
            do_last[...] = jnp.zeros_like(do_last)

        do_work[...] = do_last[...]
        o = jnp.dot(y_ref[...], wo_ref[...], preferred_element_type=F32)
        gate, g, sh = gate_ref[...], g_ref[...], sh_ref[...]
        one_sc = 1.0 + sc_ref[...]
        cs, inv_d = g * one_sc, 1.0 / D_MODEL

        def rowsum(v):
            return jnp.sum(v, axis=0, keepdims=True)

        sums = [jnp.zeros((1, D_MODEL), F32) for _ in range(4)]
        for c in range(tm // tr):
            rows = slice(c * tr, (c + 1) * tr)
            oc = o[rows]
            x1 = x_ref[rows, :] + gate * oc
            r = lax.rsqrt(jnp.sum(x1 * x1, axis=-1, keepdims=True) * inv_d + NORM_EPS)
            x1n = x1 * r
            diff = x1n * cs + sh - t_ref[rows, :]
            w = diff * x1n
            lane_sum = jnp.sum(w * cs, axis=-1, keepdims=True)
            dx1 = (diff * cs) * (r * inv_d) - x1n * (r * lane_sum * (inv_d * inv_d))
            dx1_ref[rows, :] = dx1
            do = (dx1 * gate).astype(BF16)
            do_ref[rows, :] = do
            do_last[rows, :] = do
            for k, v in enumerate((dx1 * oc, diff, w, diff * diff)):
                sums[k] = sums[k] + rowsum(v)
        live = jnp.where(i < nt, 1.0, 0.0)
        for row, v in ((SUM_GATE, sums[0]), (SUM_SHIFT_F, inv_d * sums[1]), (SUM_SCALE_F, inv_d * (sums[2] * g)),
                       (SUM_FNG, inv_d * (sums[2] * one_sc)), (SUM_SQ_ERR, sums[3])):
            sums_ref[row:row + 1, :] += live * v
        dy_ref[...] = lax.dot_general(do_work[...], wo_ref[...], NT, preferred_element_type=F32).astype(BF16)

    tile = pl.BlockSpec((tm, D_MODEL), lambda i: (jnp.minimum(i, nt - 1), 0))
    row = pl.BlockSpec((1, D_MODEL), lambda i: (0, 0))
    return pl.pallas_call(
        body, name="out_proj_loss", grid=(nt + 1,),
        in_specs=[tile, tile, tile, pl.BlockSpec((D_MODEL, D_MODEL), lambda i: (0, 0)), row, row, row, row],
        out_specs=[tile, tile, pl.BlockSpec((tm, D_MODEL), lambda i: (jnp.maximum(i - 1, 0), 0)),
                   pl.BlockSpec((8, D_MODEL), lambda i: (0, 0))],
        out_shape=[SDS((s, D_MODEL), F32), SDS((s, D_MODEL), BF16), SDS((s, D_MODEL), BF16), SDS((8, D_MODEL), F32)],
        scratch_shapes=[pltpu.VMEM((tm, D_MODEL), BF16), pltpu.VMEM((tm, D_MODEL), BF16)],
        compiler_params=_params("arbitrary"),
    )(y, x, target, wo, gate, shift_f, scale_f, fng)


ROW_DBSP, ROW_DSINKS, MISC_ROWS = 0, 8, 32


def _mixer_bwd(me, proj, dy, freqs, ln_g, ln_b, w_sp, b_sp, sinks, pair):
    s = proj.shape[0]
    nb = s // CHUNK
    sp = _mixer_specs(nb, rev=True)

    def body(me_ref, cur_ref, pkv_ref, dy_ref, freq_ref, lg_ref, lb_ref, w_ref, b_ref, sinks_ref, pair_ref,
             dproj_ref, dln_ref, dw_ref, misc_ref, parts_ref, bcol, dbcol, carry, mask, rope_rows, rope_state,
             rope_last, send_sems, recv_sems):
        i = pl.program_id(0)
        block = nb - 1 - i
        rope = _RopeTables(freq_ref, rope_rows, rope_state, rope_last)

        @pl.when(i == 0)
        def _():
            for cp in _chip_scatter(pair_ref, parts_ref, send_sems, recv_sems):
                cp.start()
            _bias_columns(b_ref, bcol)
            _set_unfold_masks(mask)
            rope.start(nb - 1, -1)
            rope.keep(rope.tables())
            dbcol[...] = jnp.zeros_like(dbcol)
            carry[...] = jnp.zeros_like(carry)
            dln_ref[...] = jnp.zeros_like(dln_ref)
            dw_ref[...] = jnp.zeros_like(dw_ref)
            misc_ref[...] = jnp.zeros_like(misc_ref)

        vln, vhat, rstd = _layer_norm(cur_ref[:, OFF_VA:OFF_ZA].astype(F32), lg_ref[...], lb_ref[...])
        vln = vln.astype(BF16)
        d_vln = []
        for g in range(A_GROUPS):
            cols = slice(g * 128, (g + 1) * 128)
            w_g = _tril_bf16(w_ref, g)
            sg = jnp.dot(w_g, vln[:, cols], preferred_element_type=F32) + bcol[g]
            u = cur_ref[:, OFF_U + g * 128:OFF_U + (g + 1) * 128].astype(F32)
            z = cur_ref[:, OFF_ZA + g * 128:OFF_ZA + (g + 1) * 128].astype(F32)
            dya = dy_ref[:, cols].astype(F32)
            sig = _sigmoid(z)
            d_ya = dya * (z * sig)
            dproj_ref[:, OFF_ZA + g * 128:OFF_ZA + (g + 1) * 128] = (
                dya * (u * sg) * (sig * (1.0 + z * (1.0 - sig)))).astype(BF16)
            dproj_ref[:, OFF_U + g * 128:OFF_U + (g + 1) * 128] = (d_ya * sg).astype(BF16)
            d_s = d_ya * u
            dbcol[g] += d_s
            d_sb = d_s.astype(BF16)
            dw_ref[g] += lax.dot_general(d_sb, vln[:, cols], NT, preferred_element_type=F32)
            d_vln.append(lax.dot_general(w_g, d_sb, TN, preferred_element_type=F32))
        d_vln = jnp.concatenate(d_vln, axis=1)
        dln_ref[0:1, :] += jnp.sum(d_vln * vhat, axis=0, keepdims=True)
        dln_ref[1:2, :] += jnp.sum(d_vln, axis=0, keepdims=True)
        d_vhat = d_vln * lg_ref[...]
        d_va = rstd * (d_vhat - jnp.mean(d_vhat, axis=-1, keepdims=True)
                       - vhat * jnp.mean(d_vhat * vhat, axis=-1, keepdims=True))
        dproj_ref[:, OFF_VA:OFF_ZA] = d_va.astype(BF16)

        cur_t = rope.kept()
        rope.step()
        prev_t = rope.tables()
        rope.keep(prev_t)
        band_t = tuple(jnp.concatenate([p, c], axis=0) for p, c in zip(prev_t, cur_t))
        qr = _rope(cur_ref[:, OFF_Q:OFF_K].astype(F32), *cur_t) * ATTN_SCALE
        kr = jnp.concatenate([_rope(pkv_ref[:, 0:256].astype(F32), *prev_t),
                              _rope(cur_ref[:, OFF_K:OFF_V].astype(F32), *cur_t)], axis=0)
        vb = jnp.concatenate([pkv_ref[:, 256:512], cur_ref[:, OFF_V:OFF_ZB]], axis=0).astype(F32)
        k_t, v_t = (kr.T * ATTN_SCALE).astype(BF16), vb.T.astype(BF16)
        zb = cur_ref[:, OFF_ZB:D_IN].astype(F32)
        dyb = dy_ref[:, D_A:D_MODEL].astype(F32)
        sig = _sigmoid(zb)
        d_yb = dyb * (zb * sig)
        outs, dqs = [], []
        dk_pairs = [jnp.zeros((2 * CHUNK, 128), F32) for _ in range(2)]
        dv_pairs = [jnp.zeros((2 * CHUNK, 128), F32) for _ in range(2)]
        from_prev = _from_prev()
        for gk in range(N_KV_HEADS):
            heads = slice(gk * HEAD_DIM, (gk + 1) * HEAD_DIM)
            q_st = _stack_heads(qr[:, (2 * gk) * 128:(2 * gk + 1) * 128], qr[:, (2 * gk + 1) * 128:(2 * gk + 2) * 128])
            k_dup, v_dup = _dup_kv_head(kr, gk), _dup_kv_head(vb, gk)
            probs, p_sink = _attn_probs(q_st, k_dup, _sink_row(sinks_ref, gk), from_prev, block == 0)
            probs_b = _unfold_band(probs.astype(BF16), mask)
            outs.append(jnp.dot(v_t[heads], probs_b, preferred_element_type=F32))
            do_st = _stack_heads(d_yb[:, (2 * gk) * 128:(2 * gk + 1) * 128], d_yb[:, (2 * gk + 1) * 128:(2 * gk + 2) * 128])
            dp = _fold_band(lax.dot_general(v_dup, do_st, NT, preferred_element_type=F32), from_prev)
            delta = jnp.sum(probs * dp, axis=0, keepdims=True)
            ds = _unfold_band((probs * (dp - delta)).astype(BF16), mask)
            d_sink = -p_sink * delta
            for r in range(4):
                row = ROW_DSINKS + 4 * gk + r
                misc_ref[row:row + 1, :] += jnp.broadcast_to(
                    jnp.sum(d_sink[:, r * CHUNK:(r + 1) * CHUNK], axis=1, keepdims=True), (1, 128))
            dqs.append(jnp.dot(k_t[heads], ds, preferred_element_type=F32))
            dk_pairs[gk // 2] += _fold_kv_head(jnp.dot(ds, q_st, preferred_element_type=F32), gk)
            dv_pairs[gk // 2] += _fold_kv_head(jnp.dot(probs_b, do_st, preferred_element_type=F32), gk)
        dproj_ref[:, OFF_ZB:D_IN] = (dyb * _heads_to_lanes(outs) * (sig * (1.0 + zb * (1.0 - sig)))).astype(BF16)
        dproj_ref[:, OFF_Q:OFF_K] = _rope_bwd(_heads_to_lanes(dqs), *cur_t).astype(BF16)
        dk_band = _rope_bwd(jnp.concatenate(dk_pairs, axis=1), *band_t)
        dv_band = jnp.concatenate(dv_pairs, axis=1)
        dproj_ref[:, OFF_K:OFF_V] = (dk_band[CHUNK:] + carry[:, 0:256]).astype(BF16)
        dproj_ref[:, OFF_V:OFF_ZB] = (dv_band[CHUNK:] + carry[:, 256:512]).astype(BF16)
        carry[:, 0:256] = dk_band[:CHUNK]
        carry[:, 256:512] = dv_band[:CHUNK]

        @pl.when(i == nb - 1)
        def _():
            t = lax.broadcasted_iota(jnp.int32, (CHUNK, CHUNK), 0)
            tp = lax.broadcasted_iota(jnp.int32, (CHUNK, CHUNK), 1)
            for g in range(A_GROUPS):
                dw_ref[g] = jnp.where(tp <= t, dw_ref[g], 0.0)
                misc_ref[pl.ds(ROW_DBSP + g, 1), :] = jnp.sum(dbcol[g].T, axis=0, keepdims=True)
            scatter = _chip_scatter(pair_ref, parts_ref, send_sems, recv_sems)
            for cp in scatter:
                cp.wait_recv()
            for cp in scatter:
                cp.wait_send()

    blk = sp["blk"]
    hbm = pl.BlockSpec(memory_space=pl.ANY)
    return pl.pallas_call(
        body, name="mixer_bwd",
        grid_spec=pltpu.PrefetchScalarGridSpec(
            num_scalar_prefetch=1, grid=(nb,),
            in_specs=[sp["cur"], sp["prev_kv"], pl.BlockSpec((CHUNK, D_MODEL), lambda i, me_ref: (blk(i), 0)),
                      sp["freq"], sp["vec"], sp["vec"], sp["wsp"], sp["bsp"], sp["smem"], hbm],
            out_specs=[pl.BlockSpec((CHUNK, D_IN), lambda i, me_ref: (blk(i), 0)),
                       pl.BlockSpec((8, D_A), lambda i, me_ref: (me_ref[0], 0)),
                       pl.BlockSpec((A_GROUPS, CHUNK, CHUNK), lambda i, me_ref: (me_ref[0], 0, 0)),
                       pl.BlockSpec((MISC_ROWS, 128), lambda i, me_ref: (me_ref[0], 0)), hbm],
            scratch_shapes=[pltpu.VMEM((A_GROUPS, CHUNK, CHUNK), F32), pltpu.VMEM((A_GROUPS, CHUNK, CHUNK), F32),
                            pltpu.VMEM((CHUNK, 512), F32), pltpu.VMEM((2, CHUNK, 4 * CHUNK), BF16),
                            *_rope_scratch(), *_scatter_scratch()]),
        out_shape=[SDS((s, D_IN), BF16), SDS((N_DEV * 8, D_A), F32), SDS((N_DEV * A_GROUPS, CHUNK, CHUNK), F32),
                   SDS((N_DEV * MISC_ROWS, 128), F32), SDS((3,) + pair.shape[1:], pair.dtype)],
        compiler_params=_params("arbitrary"),
    )(me, proj, proj, dy, freqs, ln_g, ln_b, w_sp, b_sp, sinks, pair)


def _wgrad_pair(name, a, b, gathers=()):
    s, m = a.shape
    n = b.shape[1]
    bm, half = m // 4, m // 8
    bt = min(1024, s)
    steps = s // bt
    last = 4 * steps
    n_g = len(gathers)

    def body(*refs):
        a_ref, b_ref = refs[:2]
        out_ref, bufs = refs[2 + n_g], refs[3 + n_g:3 + 2 * n_g]
        acc, kept, got, sent, send_sems, recv_sems = refs[3 + 2 * n_g:9 + 2 * n_g]
        sems = refs[9 + 2 * n_g:]
        g = pl.program_id(0)
        tile, t = g // steps, g % steps
        mx, my, mc = _mesh_pos()
        jobs = [_InPlaceGather(bufs[k], sems[2 * k], sems[2 * k + 1]) for k in range(n_g)]

        def exchange(q):
            return pltpu.make_async_remote_copy(src_ref=sent, dst_ref=got.at[q % 2], send_sem=send_sems.at[q],
                                                recv_sem=recv_sems.at[q], device_id=(mx, my, 1 - mc),
                                                device_id_type=MESH)

        @pl.when(g == 0)
        def _():
            for job in jobs:
                job.start()

        @pl.when(g == 2 * steps)
        def _():
            for job in jobs:
                for j in range(3):
                    job.pass_on(j)

        @pl.when(g < last)
        def _():
            prod = lax.dot_general(a_ref[...], b_ref[...], TN, preferred_element_type=F32)

            @pl.when(t == 0)
            def _():
                acc[...] = prod

            @pl.when(t > 0)
            def _():
                acc[...] += prod

            @pl.when(t == steps - 1)
            def _():
                @pl.when(tile > 0)
                def _():
                    exchange(tile - 1).wait_send()

                kept[tile % 2] = acc[pl.ds(pl.multiple_of(mc * half, 8), half), :].astype(BF16)
                sent[...] = acc[pl.ds(pl.multiple_of((1 - mc) * half, 8), half), :].astype(BF16)
                exchange(tile).start()

        @pl.when((t == 0) & (g > 0))
        def _():
            q = tile - 1
            exchange(q).wait_recv()
            out_ref[0] = (kept[q % 2].astype(F32) + got[q % 2].astype(F32)).astype(BF16)

        @pl.when(g == last)
        def _():
            exchange(3).wait_send()
            for job in jobs:
                job.wait_sibling(0)
                for j in range(3):
                    job.wait_sibling(4 + j)
                job.wait_sends()

    def a_tile(g):
        gg = jnp.minimum(g, last - 1)
        return (gg % steps, gg // steps)

    def b_tile(g):
        return (jnp.minimum(g, last - 1) % steps, 0)

    hbm = pl.BlockSpec(memory_space=pl.ANY)
    outs = pl.pallas_call(
        body, name=name, grid=(last + 1,),
        in_specs=[pl.BlockSpec((bt, bm), a_tile), pl.BlockSpec((bt, n), b_tile)] + [hbm] * n_g,
        out_specs=[pl.BlockSpec((1, half, n), lambda g: (jnp.maximum(g - 1, 0) // steps, 0, 0))] + [hbm] * n_g,
        out_shape=[SDS((4, half, n), BF16)] + [SDS(gb.shape, gb.dtype) for gb in gathers],
        scratch_shapes=[pltpu.VMEM((bm, n), F32), pltpu.VMEM((2, half, n), BF16), pltpu.VMEM((2, half, n), BF16),
                        pltpu.VMEM((half, n), BF16), pltpu.SemaphoreType.DMA((4,)), pltpu.SemaphoreType.DMA((4,))]
        + _gather_scratch() * n_g,
        input_output_aliases={2 + k: 1 + k for k in range(n_g)},
        compiler_params=_params("arbitrary"),
    )(a, b, *gathers)
    return outs[0], outs[1:]


def _in_proj_bwd(dproj, wt, x, dx1, scale, norm_g, sums_o, pair):
    s = x.shape[0]
    tm, tk, tr = min(1024, s), D_IN // 4, 64
    ksteps = D_IN // tk

    def body(dp_ref, wt_ref, x_hbm, dx1_hbm, sc_ref, g_ref, so_ref, pair_ref, gx_ref, sums_ref, parts_ref, x_buf,
             dx1_buf, tile_sems, send_sems, recv_sems):
        i, k = pl.program_id(0), pl.program_id(1)

        def tile_copies():
            rows = pl.ds(pl.multiple_of(i * tm, tm), tm)
            return (pltpu.make_async_copy(x_hbm.at[rows], x_buf, tile_sems.at[0]),
                    pltpu.make_async_copy(dx1_hbm.at[rows], dx1_buf, tile_sems.at[1]))

        @pl.when((i == 0) & (k == 0))
        def _():
            for cp in _chip_scatter(pair_ref, parts_ref, send_sems, recv_sems):
                cp.start()
            sums_ref[...] = so_ref[...]

        @pl.when(k == 0)
        def _():
            for cp in tile_copies():
                cp.start()
            gx_ref[...] = jnp.dot(dp_ref[...], wt_ref[...], preferred_element_type=F32)

        @pl.when(k > 0)
        def _():
            gx_ref[...] += jnp.dot(dp_ref[...], wt_ref[...], preferred_element_type=F32)

        @pl.when(k == ksteps - 1)
        def _():
            for cp in tile_copies():
                cp.wait()
            one_sc, g = 1.0 + sc_ref[...], g_ref[...]
            cs = one_sc * g

            def chunk(j, sums):
                rows = pl.ds(pl.multiple_of(j * tr, tr), tr)
                dh, xv = gx_ref[rows, :], x_buf[rows, :]
                dhx = dh * xv
                r = lax.rsqrt(jnp.sum(xv * xv, axis=-1, keepdims=True) * (1.0 / D_MODEL) + NORM_EPS)
                coef = (r * r * r) * (jnp.sum(dhx * cs, axis=-1, keepdims=True) * (1.0 / D_MODEL))
                gx_ref[rows, :] = dx1_buf[rows, :] + r * (dh * cs) - xv * coef
                return (sums[0] + jnp.sum(dh, axis=0, keepdims=True), sums[1] + jnp.sum(dhx * r, axis=0, keepdims=True))

            zero = jnp.zeros((1, D_MODEL), F32)
            sums = lax.fori_loop(0, tm // tr, chunk, (zero, zero))
            sums_ref[SUM_SHIFT:SUM_SHIFT + 1, :] += sums[0]
            sums_ref[SUM_SCALE:SUM_SCALE + 1, :] += sums[1] * g
            sums_ref[SUM_NORM_G:SUM_NORM_G + 1, :] += sums[1] * one_sc

        @pl.when((i == s // tm - 1) & (k == ksteps - 1))
        def _():
            scatter = _chip_scatter(pair_ref, parts_ref, send_sems, recv_sems)
            for cp in scatter:
                cp.wait_recv()
            for cp in scatter:
                cp.wait_send()

    row = pl.BlockSpec((1, D_MODEL), lambda i, k: (0, 0))
    hbm = pl.BlockSpec(memory_space=pl.ANY)
    return pl.pallas_call(
        body, name="in_proj_bwd", grid=(s // tm, ksteps),
        in_specs=[pl.BlockSpec((tm, tk), lambda i, k: (i, k)), pl.BlockSpec((tk, D_MODEL), lambda i, k: (k, 0)),
                  hbm, hbm, row, row, pl.BlockSpec((8, D_MODEL), lambda i, k: (0, 0)), hbm],
        out_specs=[pl.BlockSpec((tm, D_MODEL), lambda i, k: (i, 0)), pl.BlockSpec((8, D_MODEL), lambda i, k: (0, 0)),
                   hbm],
        out_shape=[SDS((s, D_MODEL), F32), SDS((8, D_MODEL), F32), SDS((3,) + pair.shape[1:], pair.dtype)],
        scratch_shapes=[pltpu.VMEM((tm, D_MODEL), F32), pltpu.VMEM((tm, D_MODEL), F32),
                        pltpu.SemaphoreType.DMA((2,)), *_scatter_scratch()],
        compiler_params=_params("arbitrary", "arbitrary"),
    )(dproj, wt, x, dx1, scale, norm_g, sums_o, pair)


def _sum_chips(own_ref, parts_ref):
    return ((own_ref[0].astype(F32) + parts_ref[0].astype(F32)) + parts_ref[1].astype(F32)) + parts_ref[2].astype(F32)


def _adam_rows(name, chip, pair, parts, w, m, v, tr):
    rows = w.shape[0]

    def body(chip_ref, own_ref, p_ref, w_ref, m_ref, v_ref, g_ref, d_ref, nm_ref, nv_ref):
        g = _sum_chips(own_ref, p_ref)
        g_ref[...] = g
        d_ref[...], nm_ref[...], nv_ref[...] = _adamw(w_ref[...], g, m_ref[...], v_ref[...])

    blk = pl.BlockSpec((tr, D_MODEL), lambda j, chip_ref: (j, 0))
    return pl.pallas_call(
        body, name=name,
        grid_spec=pltpu.PrefetchScalarGridSpec(
            num_scalar_prefetch=1, grid=(rows // tr,),
            in_specs=[pl.BlockSpec((1, tr, D_MODEL), lambda j, chip_ref: (chip_ref[0], j, 0)),
                      pl.BlockSpec((3, tr, D_MODEL), lambda j, chip_ref: (0, j, 0)), blk, blk, blk],
            out_specs=[blk] * 4),
        out_shape=[SDS(w.shape, F32)] * 4, compiler_params=_params("parallel"),
    )(chip, pair, parts, w, m, v)


def _adam_ada(name, cact, dmod, w, m, v):
    n = w.shape[1]
    tr = 512

    def body(c_ref, dm_ref, w_ref, m_ref, v_ref, g_ref, d_ref, nm_ref, nv_ref):
        pad_c = jnp.concatenate([c_ref[...], jnp.zeros_like(c_ref)], axis=0).astype(BF16)
        pad_d = jnp.concatenate([dm_ref[...], jnp.zeros_like(dm_ref)], axis=0).astype(BF16)
        g = lax.dot_general(pad_c, pad_d, TN, preferred_element_type=F32)
        g_ref[...] = g
        d_ref[...], nm_ref[...], nv_ref[...] = _adamw(w_ref[...], g, m_ref[...], v_ref[...])

    blk = pl.BlockSpec((tr, n), lambda j: (j, 0))
    return pl.pallas_call(
        body, name=name, grid=(D_MODEL // tr,),
        in_specs=[pl.BlockSpec((N_DEV, tr), lambda j: (0, j)), pl.BlockSpec((N_DEV, n), lambda j: (0, 0)),
                  blk, blk, blk],
        out_specs=[blk] * 4, out_shape=[SDS(w.shape, F32)] * 4,
        compiler_params=_params("parallel"),
    )(cact, dmod, w, m, v)


SMALL_PARAMS = ("w_spatial", "b_spatial", "sinks", "norm_g", "ln_v_g", "ln_v_b", "final_norm_g", "b_ada", "b_ada_final")


def _adam_small(d_wsp, misc, d_ln, sums, params):
    n_p = len(SMALL_PARAMS)

    def body(*refs):
        wsp_ref, misc_ref, ln_ref, sums_ref = refs[:4]
        wmv = [refs[4 + 3 * k:7 + 3 * k] for k in range(n_p)]
        loss_ref = refs[4 + 3 * n_p]
        outs = [refs[5 + 3 * n_p + 4 * k:9 + 3 * n_p + 4 * k] for k in range(n_p)]

        def column_sum(row):
            return total(sums_ref, (row, row + 1))

        def total(ref, rows=None):
            def part(j):
                return ref[j] if rows is None else ref[j, rows[0]:rows[1], :]
            acc = part(0)
            for j in range(1, N_DEV):
                acc = acc + part(j)
            return acc

        sink_rows = total(misc_ref, (ROW_DSINKS, ROW_DSINKS + 16))
        diag = (lax.broadcasted_iota(jnp.int32, (16, 128), 0) == lax.broadcasted_iota(jnp.int32, (16, 128), 1))
        grads = dict(
            w_spatial=total(wsp_ref), b_spatial=total(misc_ref, (ROW_DBSP, ROW_DBSP + A_GROUPS)),
            sinks=jnp.sum(jnp.where(diag, sink_rows, 0.0), axis=0, keepdims=True),
            norm_g=column_sum(SUM_NORM_G), ln_v_g=total(ln_ref, (0, 1)), ln_v_b=total(ln_ref, (1, 2)),
            final_norm_g=column_sum(SUM_FNG),
            b_ada=jnp.concatenate([column_sum(SUM_SHIFT), column_sum(SUM_SCALE), column_sum(SUM_GATE)], axis=1),
            b_ada_final=jnp.concatenate([column_sum(SUM_SHIFT_F), column_sum(SUM_SCALE_F)], axis=1))
        sq_err = jnp.sum(column_sum(SUM_SQ_ERR), axis=1, keepdims=True)
        loss_ref[...] = jnp.broadcast_to(sq_err * (0.5 / D_MODEL), (1, 128))
        for k, name in enumerate(SMALL_PARAMS):
            w_ref, m_ref, v_ref = wmv[k]
            g_ref, d_ref, nm_ref, nv_ref = outs[k]
            g_ref[...] = grads[name]
            d_ref[...], nm_ref[...], nv_ref[...] = _adamw(w_ref[...], grads[name], m_ref[...], v_ref[...])

    flat = [a for name in SMALL_PARAMS for a in params[name]]
    vmem = pl.BlockSpec(memory_space=pltpu.VMEM)
    out_shape = [SDS((1, 128), F32)] + [SDS(params[name][0].shape, F32) for name in SMALL_PARAMS for _ in range(4)]
    outs = pl.pallas_call(
        body, name="adam_small", in_specs=[vmem] * (4 + len(flat)), out_specs=[vmem] * len(out_shape),
        out_shape=out_shape, compiler_params=_params(),
    )(d_wsp, misc, d_ln, sums, *flat)
    return outs[0], {name: outs[1 + 4 * k:5 + 4 * k] for k, name in enumerate(SMALL_PARAMS)}


def kernel(x, c, w_ada, b_ada, norm_g, w_in, ln_v_g, ln_v_b, w_spatial, b_spatial, sinks, w_out, w_ada_final, b_ada_final, final_norm_g, loss_target, m_w_ada, m_b_ada, m_norm_g, m_w_in, m_ln_v_g, m_ln_v_b, m_w_spatial, m_b_spatial, m_sinks, m_w_out, m_w_ada_final, m_b_ada_final, m_final_norm_g, v_w_ada, v_b_ada, v_norm_g, v_w_in, v_ln_v_g, v_ln_v_b, v_w_spatial, v_b_spatial, v_sinks, v_w_out, v_w_ada_final, v_b_ada_final, v_final_norm_g):
    me = 4 * lax.axis_index("x") + 2 * lax.axis_index("y") + lax.axis_index("c")
    x2, tgt = x[0], loss_target[0]
    fng = final_norm_g.reshape(1, D_MODEL)

    n_ada, n_ada_f = w_ada.shape[2], w_ada_final.shape[1]
    cact, mod, mod_f = _ada_exchange(c, w_ada[0], b_ada.reshape(N_DEV, n_ada), w_ada_final,
                                     b_ada_final.reshape(N_DEV, n_ada_f))
    cact = cact.reshape(N_DEV, D_MODEL)
    mod, mod_f = mod.reshape(1, 3 * D_MODEL), mod_f.reshape(1, 2 * D_MODEL)
    shift, scale, gate = mod[:, :D_MODEL], mod[:, D_MODEL:2 * D_MODEL], mod[:, 2 * D_MODEL:]
    shift_f, scale_f = mod_f[:, :D_MODEL], mod_f[:, D_MODEL:]

    wt_f32, m_wt, v_wt = (jnp.swapaxes(a, 1, 2)[0] for a in (w_in, m_w_in, v_w_in))
    xi, yi = lax.axis_index("x"), lax.axis_index("y")
    chip_order = jnp.stack([2 * xi + yi, 2 * (1 - xi) + yi, 2 * xi + 1 - yi, 2 * (1 - xi) + 1 - yi]).astype(jnp.int32)
    wt_mine, wo_mine = _prep_weights(me.reshape(1), wt_f32, w_out[0])

    freqs = _rope_freqs()
    sinks_v = sinks.reshape(16)
    h, proj, wt = _gather_in_proj(chip_order, x2, shift, scale, norm_g, wt_mine)
    y, wo = _mixer_fwd(proj, freqs, ln_v_g, ln_v_b, w_spatial[0], b_spatial[0], sinks_v, wo_mine)
    dx1, do, dy, sums_o = _out_proj_loss(y, x2, tgt, wo, gate, shift_f, scale_f, fng)

    chip = (2 * lax.axis_index("x") + lax.axis_index("y")).reshape(1)
    pair_out, _ = _wgrad_pair("wgrad_out", y, do)
    dproj, d_ln, d_wsp, misc, parts_out = _mixer_bwd(
        me.reshape(1), proj, dy, freqs, ln_v_g, ln_v_b, w_spatial[0], b_spatial[0], sinks_v, pair_out)
    pair_in, (d_ln, d_wsp, misc) = _wgrad_pair(
        "wgrad_in", dproj, h, gathers=(d_ln, d_wsp.reshape(N_DEV * A_GROUPS * CHUNK, CHUNK), misc))
    grad_x, sums, parts_in = _in_proj_bwd(dproj, wt, x2, dx1, scale, norm_g, sums_o, pair_in)
    wt_leaves = [jnp.swapaxes(a[None], 1, 2)
                 for a in _adam_rows("adam_w_in", chip, pair_in, parts_in, wt_f32, m_wt, v_wt, 176)]
    w_out_leaves = [a[None] for a in _adam_rows("adam_w_out", chip, pair_out, parts_out, w_out[0], m_w_out[0], v_w_out[0], 64)]

    (sums,) = _all_gather("gather_sums", [sums], pltpu.VMEM)
    natural = dict(w_spatial=(A_GROUPS * CHUNK, CHUNK), b_spatial=(A_GROUPS, CHUNK), sinks=(1, 16), norm_g=(1, D_MODEL),
                   ln_v_g=(1, D_A), ln_v_b=(1, D_A), final_norm_g=(1, D_MODEL), b_ada=(1, 3 * D_MODEL),
                   b_ada_final=(1, 2 * D_MODEL))
    given = dict(
        w_spatial=(w_spatial, m_w_spatial, v_w_spatial), b_spatial=(b_spatial, m_b_spatial, v_b_spatial),
        sinks=(sinks, m_sinks, v_sinks), norm_g=(norm_g, m_norm_g, v_norm_g), ln_v_g=(ln_v_g, m_ln_v_g, v_ln_v_g),
        ln_v_b=(ln_v_b, m_ln_v_b, v_ln_v_b), final_norm_g=(final_norm_g, m_final_norm_g, v_final_norm_g),
        b_ada=(b_ada, m_b_ada, v_b_ada), b_ada_final=(b_ada_final, m_b_ada_final, v_b_ada_final))
    params = {name: tuple(a.reshape(natural[name]) for a in given[name]) for name in SMALL_PARAMS}
    params["sinks"] = tuple(jnp.pad(a, ((0, 0), (0, 128 - 16))) for a in params["sinks"])
    loss, small = _adam_small(d_wsp.reshape(N_DEV, A_GROUPS * CHUNK, CHUNK), misc.reshape(N_DEV, MISC_ROWS, 128),
                              d_ln.reshape(N_DEV, 8, D_A), sums, params)
    small["sinks"] = [a[:, :16] for a in small["sinks"]]
    small = {name: [a.reshape(given[name][0].shape) for a in small[name]] for name in SMALL_PARAMS}

    dmod_all = jnp.concatenate([sums[:, SUM_SHIFT], sums[:, SUM_SCALE], sums[:, SUM_GATE]], axis=1)
    dmod_f_all = jnp.concatenate([sums[:, SUM_SHIFT_F], sums[:, SUM_SCALE_F]], axis=1)
    dmod_mine = lax.dynamic_slice_in_dim(dmod_all, me * n_ada, n_ada, axis=1)
    dmod_f_mine = lax.dynamic_slice_in_dim(dmod_f_all, me * n_ada_f, n_ada_f, axis=1)
    ada = _adam_ada("adam_w_ada", cact, dmod_mine, w_ada[0], m_w_ada[0], v_w_ada[0])
    ada_f = _adam_ada("adam_w_ada_final", cact, dmod_f_mine, w_ada_final, m_w_ada_final, v_w_ada_final)

    def leaves(k):
        return (ada[k][None], small["b_ada"][k], small["norm_g"][k], wt_leaves[k], small["ln_v_g"][k],
                small["ln_v_b"][k], small["w_spatial"][k], small["b_spatial"][k], small["sinks"][k], w_out_leaves[k],
                ada_f[k], small["b_ada_final"][k], small["final_norm_g"][k])

    return (loss[0, 0], grad_x[None], *leaves(0), *leaves(1), *leaves(2), *leaves(3))
```

```python
import jax
import jax.numpy as jnp
from jax import lax
from jax.experimental import pallas as pl
from jax.experimental.pallas import tpu as pltpu

D_MODEL = 2048
D_IN = 5632
D_A = 1024
CHUNK = 128
A_GROUPS = 8
HEAD_DIM = 64
N_KV_HEADS = 4
N_DEV = 8
ROPE_THETA = 10000.0
NORM_EPS = 1e-5
ATTN_SCALE = HEAD_DIM ** -0.5

ADAM_LR = 0.001
ADAM_B1 = 0.9
ADAM_B2 = 0.999
ADAM_EPS = 1e-08
ADAM_WD = 0.01
ADAM_STEP = 10

OFF_U, OFF_VA, OFF_ZA, OFF_Q, OFF_K, OFF_V, OFF_ZB = 0, 1024, 2048, 3072, 4096, 4352, 4608

SUM_SHIFT, SUM_SCALE, SUM_NORM_G, SUM_GATE, SUM_SHIFT_F, SUM_SCALE_F, SUM_FNG, SUM_SQ_ERR = range(8)

V7X_VMEM_LIMIT_BYTES = 56 * 1024 * 1024

F32 = jnp.float32
BF16 = jnp.bfloat16
MESH = pl.DeviceIdType.MESH
SDS = jax.ShapeDtypeStruct
NT = (((1,), (1,)), ((), ()))
TN = (((0,), (0,)), ((), ()))


def _params(*semantics):
    return pltpu.CompilerParams(dimension_semantics=semantics or None, vmem_limit_bytes=V7X_VMEM_LIMIT_BYTES)


def _mesh_pos():
    return lax.axis_index("x"), lax.axis_index("y"), lax.axis_index("c")


def _sigmoid(z):
    return 1.0 / (1.0 + jnp.exp(-z))


def _adamw(w, g, m, v):
    m = ADAM_B1 * m + (1.0 - ADAM_B1) * g
    v = ADAM_B2 * v + (1.0 - ADAM_B2) * (g * g)
    m_hat = m / (1.0 - ADAM_B1 ** ADAM_STEP)
    v_hat = v / (1.0 - ADAM_B2 ** ADAM_STEP)
    delta = -ADAM_LR * (m_hat / (jnp.sqrt(v_hat) + ADAM_EPS) + ADAM_WD * w)
    return delta, m, v


def _all_gather(name, blocks, memory_space):
    n_arr = len(blocks)

    def body(*refs):
        ins, outs = refs[:n_arr], refs[n_arr:2 * n_arr]
        send_sems, recv_sems, local_sems = refs[2 * n_arr:]
        x, y, c = _mesh_pos()
        me, sibling = (x, y, c), (x, y, 1 - c)
        chips = [(1 - x, y), (x, 1 - y), (1 - x, 1 - y)]

        def slot(p):
            return 4 * p[0] + 2 * p[1] + p[2]

        def copy(a, k, block, to, src=None):
            dst = outs[a].at[slot(block)]
            return pltpu.make_async_remote_copy(
                src_ref=dst if src is None else src, dst_ref=dst,
                send_sem=send_sems.at[a, k], recv_sem=recv_sems.at[a, k],
                device_id=to, device_id_type=MESH)

        mine = [pltpu.make_async_copy(ins[a], outs[a].at[slot(me)], local_sems.at[a]) for a in range(n_arr)]
        for cp in mine:
            cp.start()
        first = []
        for a in range(n_arr):
            first.append(copy(a, 0, me, sibling, src=ins[a]))
            first += [copy(a, 1 + j, me, (*chip, c), src=ins[a]) for j, chip in enumerate(chips)]
        for cp in first:
            cp.start()
        passed = []
        for j, chip in enumerate(chips):
            for a in range(n_arr):
                copy(a, 1 + j, (*chip, c), me).wait_recv()
                fwd = copy(a, 4 + j, (*chip, c), sibling)
                fwd.start()
                passed.append(fwd)
        for a in range(n_arr):
            copy(a, 0, sibling, me).wait_recv()
            for j, chip in enumerate(chips):
                copy(a, 4 + j, (*chip, 1 - c), me).wait_recv()
        for cp in first + passed:
            cp.wait_send()
        for cp in mine:
            cp.wait()

    spec = pl.BlockSpec(memory_space=memory_space)
    return pl.pallas_call(
        body, name=name,
        out_shape=[SDS((N_DEV,) + b.shape, b.dtype) for b in blocks],
        in_specs=[spec] * n_arr, out_specs=[spec] * n_arr,
        scratch_shapes=[pltpu.SemaphoreType.DMA((n_arr, 7)), pltpu.SemaphoreType.DMA((n_arr, 7)),
                        pltpu.SemaphoreType.DMA((n_arr,))],
        compiler_params=_params(),
    )(*blocks)


def _ada_scratch(n1, n2):
    return [pltpu.VMEM((N_DEV, 1, D_MODEL), F32), pltpu.VMEM((N_DEV, 1, n1), F32), pltpu.VMEM((N_DEV, 1, n2), F32),
            pltpu.VMEM((1, D_MODEL), F32), pltpu.VMEM((N_DEV, n1), F32), pltpu.VMEM((N_DEV, n2), F32),
            pltpu.VMEM((N_DEV, 1, n1), F32), pltpu.VMEM((N_DEV, 1, n2), F32),
            pltpu.VMEM((D_MODEL, n1), BF16), pltpu.VMEM((D_MODEL, n2), BF16), pltpu.SemaphoreType.DMA((2,)),
            pltpu.SemaphoreType.DMA((3, 7)), pltpu.SemaphoreType.DMA((3, 7))]


def _ada_exchange(c_ref, w1_hbm, b1_ref, w2_hbm, b2_ref, scratch):
    cact_ref, mod_ref, modf_ref, cact_buf, res1, res2, send1, send2, w1_ref, w2_ref, load_sems, sems_s, sems_r = scratch
    x, y, c_pos = _mesh_pos()
    me = 4 * x + 2 * y + c_pos
    flips = [(k >> 2 & 1, k >> 1 & 1, k & 1) for k in range(1, N_DEV)]

    def peer(f):
        return (1 - x if f[0] else x, 1 - y if f[1] else y, 1 - c_pos if f[2] else c_pos)

    loads = [pltpu.make_async_copy(w1_hbm, w1_ref, load_sems.at[0]),
             pltpu.make_async_copy(w2_hbm, w2_ref, load_sems.at[1])]
    for cp in loads:
        cp.start()
    cv = c_ref[...]
    cact = cv * _sigmoid(cv)
    cact_buf[...] = cact
    cact_ref[me] = cact

    def rdma(phase, k, src, dst, f):
        return pltpu.make_async_remote_copy(src_ref=src, dst_ref=dst, send_sem=sems_s.at[phase, k],
                                            recv_sem=sems_r.at[phase, k], device_id=peer(f), device_id_type=MESH)

    gather = [rdma(0, k, cact_buf, cact_ref.at[me], f) for k, f in enumerate(flips)]
    for cp in gather:
        cp.start()
    for cp in gather:
        cp.wait_recv()
    for cp in gather:
        cp.wait_send()

    rid = lax.broadcasted_iota(jnp.int32, (N_DEV, D_MODEL), 0)
    rows = jnp.zeros((N_DEV, D_MODEL), F32)
    for j in range(N_DEV):
        rows = jnp.where(rid == j, jnp.broadcast_to(cact_ref[j], (N_DEV, D_MODEL)), rows)
    rows = rows.astype(BF16)
    for cp in loads:
        cp.wait()
    res1[...] = jnp.dot(rows, w1_ref[...], preferred_element_type=F32) + b1_ref[pl.ds(me, 1), :]
    res2[...] = jnp.dot(rows, w2_ref[...], preferred_element_type=F32) + b2_ref[pl.ds(me, 1), :]
    for j in range(N_DEV):
        send1[j] = res1[pl.ds(j, 1), :]
        send2[j] = res2[pl.ds(j, 1), :]
    mod_ref[me] = send1[me]
    modf_ref[me] = send2[me]
    scatter = []
    for k, f in enumerate(flips):
        to = me ^ (k + 1)
        scatter.append(rdma(1, k, send1.at[to], mod_ref.at[me], f))
        scatter.append(rdma(2, k, send2.at[to], modf_ref.at[me], f))
    for cp in scatter:
        cp.start()
    for cp in scatter:
        cp.wait_recv()
    for cp in scatter:
        cp.wait_send()


def _mod_columns(mod_ref, lo, hi):
    n = mod_ref.shape[2]
    pieces = [mod_ref[j][:, max(lo, j * n) - j * n:min(hi, (j + 1) * n) - j * n]
              for j in range(N_DEV) if max(lo, j * n) < min(hi, (j + 1) * n)]
    return jnp.concatenate(pieces, axis=1)


def _chip_scatter(pair_ref, parts_ref, send_sems, recv_sems):
    x, y, c = _mesh_pos()
    chips = [(1 - x, y), (x, 1 - y), (1 - x, 1 - y)]
    return [pltpu.make_async_remote_copy(
        src_ref=pair_ref.at[2 * cx + cy], dst_ref=parts_ref.at[j], send_sem=send_sems.at[j], recv_sem=recv_sems.at[j],
        device_id=(cx, cy, c), device_id_type=MESH) for j, (cx, cy) in enumerate(chips)]


def _scatter_scratch():
    return [pltpu.SemaphoreType.DMA((3,)), pltpu.SemaphoreType.DMA((3,))]


def _prep_weights(me, wt, w_out, w_ada, w_ada_f):
    steps = 4

    def body(me_ref, *refs):
        for src, dst in zip(refs[:4], refs[4:]):
            dst[...] = src[...].astype(BF16)

    def rows(a, mine):
        blk = (a.shape[0] // steps, a.shape[1])
        return pl.BlockSpec(blk, (lambda i, me_ref: (steps * me_ref[0] + i, 0)) if mine else (lambda i, me_ref: (i, 0)))

    ins = (wt, w_out, w_ada, w_ada_f)
    return pl.pallas_call(
        body, name="prep_weights",
        grid_spec=pltpu.PrefetchScalarGridSpec(
            num_scalar_prefetch=1, grid=(steps,),
            in_specs=[rows(a, False) for a in ins],
            out_specs=[rows(wt, True), rows(w_out, True), rows(w_ada, False), rows(w_ada_f, False)]),
        out_shape=[SDS((N_DEV * wt.shape[0], D_MODEL), BF16), SDS((N_DEV * w_out.shape[0], D_MODEL), BF16),
                   SDS(w_ada.shape, BF16), SDS(w_ada_f.shape, BF16)],
        compiler_params=_params("parallel"),
    )(me, *ins)


class _InPlaceGather:
    def __init__(self, buf_ref, send_sems, recv_sems, relay=False):
        self.buf, self.send_sems, self.recv_sems, self.relay = buf_ref, send_sems, recv_sems, relay
        self.n = buf_ref.shape[0] // N_DEV
        x, y, c = _mesh_pos()
        self.me, self.sibling, self.core = (x, y, c), (x, y, 1 - c), c
        self.chips = [(1 - x, y), (x, 1 - y), (1 - x, 1 - y)]
        self.relay_from = (jnp.where(c == 0, 1 - x, x), jnp.where(c == 0, y, 1 - y), c)
        self.relay_to = (jnp.where(c == 0, x, 1 - x), jnp.where(c == 0, 1 - y, y), c)

    def copy(self, k, block, to):
        start = pl.multiple_of((4 * block[0] + 2 * block[1] + block[2]) * self.n, self.n)
        rows = self.buf.at[pl.ds(start, self.n)]
        return pltpu.make_async_remote_copy(src_ref=rows, dst_ref=rows, send_sem=self.send_sems.at[k],
                                            recv_sem=self.recv_sems.at[k], device_id=to, device_id_type=MESH)

    def start(self):
        self.copy(0, self.me, self.sibling).start()
        for j, chip in enumerate(self.chips[:2] if self.relay else self.chips):
            self.copy(1 + j, self.me, (*chip, self.core)).start()

    def relay_diagonal(self):
        self.copy(3, self.relay_from, self.relay_to).start()

    def pass_on(self, j):
        self.copy(1 + j, (*self.chips[j], self.core), self.me).wait_recv()
        self.copy(4 + j, (*self.chips[j], self.core), self.sibling).start()

    def wait_sibling(self, k):
        self.copy(k, self.sibling, self.me).wait_recv()

    def wait_sends(self):
        for k in range(7):
            self.copy(k, self.me, self.sibling).wait_send()


def _gather_scratch():
    return [pltpu.SemaphoreType.DMA((7,)), pltpu.SemaphoreType.DMA((7,))]


def _gather_in_proj(order, x, norm_g, wt_all, c, w_ada, b_ada8, w_ada_f, b_ada_f8):
    s = x.shape[0]
    th, tm = min(512, s), min(512, s)
    nh, ni = s // th, s // tm
    tn = D_IN // 4
    steps = nh + 4 * ni
    n1, n2 = w_ada.shape[1], w_ada_f.shape[1]

    def body(order_ref, x_ref, g_ref, wt_in, c_ref, w1_hbm, b1_ref, w2_hbm, b2_ref,
             h_ref, proj_ref, wt_ref, cact_out, mod_out, modf_out,
             h_scr, w_buf, shift_scale, load_sems, send_sems, recv_sems, *ada):
        g = pl.program_id(0)
        gather = _InPlaceGather(wt_ref, send_sems, recv_sems, relay=True)

        def tile_load(slot, chip):
            return pltpu.make_async_copy(wt_ref.at[pl.ds(pl.multiple_of(chip * tn, tn), tn)], w_buf.at[slot],
                                         load_sems.at[slot])

        @pl.when(g == 0)
        def _():
            gather.start()
            _ada_exchange(c_ref, w1_hbm, b1_ref, w2_hbm, b2_ref, ada)
            cact_out[...], mod_out[...], modf_out[...] = ada[0][...], ada[1][...], ada[2][...]
            shift_scale[0:1, :] = _mod_columns(ada[1], 0, D_MODEL)
            shift_scale[1:2, :] = _mod_columns(ada[1], D_MODEL, 2 * D_MODEL)

        @pl.when(g < nh)
        def _():
            xv = x_ref[...]
            r = lax.rsqrt(jnp.mean(xv * xv, axis=-1, keepdims=True) + NORM_EPS)
            shift, scale = shift_scale[0:1, :], shift_scale[1:2, :]
            hb = (((xv * r) * g_ref[...]) * (1.0 + scale) + shift).astype(BF16)
            h_ref[...] = hb
            h_scr[pl.ds(pl.multiple_of(g * th, th), th), :] = hb

        @pl.when(g == nh - 1)
        def _():
            gather.wait_sibling(0)
            tile_load(0, order_ref[0]).start()

        @pl.when(g >= nh)
        def _():
            t, i = (g - nh) // ni, (g - nh) % ni

            @pl.when(i == 0)
            def _():
                tile_load(t % 2, order_ref[t]).wait()

            @pl.when((i == ni - 1) & (t == 0))
            def _():
                gather.pass_on(0)
                gather.pass_on(1)
                gather.relay_diagonal()

            @pl.when((i == ni // 2) & (t == 2))
            def _():
                gather.pass_on(2)

            for j in range(3):
                @pl.when((i == ni - 1) & (t == j))
                def _():
                    gather.wait_sibling(4 + j)
                    tile_load((j + 1) % 2, order_ref[j + 1]).start()

            lhs = h_scr[pl.ds(pl.multiple_of(i * tm, tm), tm), :]
            proj_ref[...] = lax.dot_general(lhs, w_buf[t % 2], NT, preferred_element_type=F32).astype(BF16)

        @pl.when(g == steps - 1)
        def _():
            gather.wait_sends()

    def h_tile(g, order_ref):
        return (jnp.minimum(g, nh - 1), 0)

    def proj_tile(g, order_ref):
        mm = jnp.maximum(g - nh, 0)
        return (mm % ni, order_ref[mm // ni])

    def whole(shape):
        return pl.BlockSpec(shape, lambda g, order_ref: (0,) * len(shape))

    hbm = pl.BlockSpec(memory_space=pl.ANY)
    ada_outs = [(N_DEV, 1, D_MODEL), (N_DEV, 1, n1), (N_DEV, 1, n2)]
    return pl.pallas_call(
        body, name="gather_in_proj",
        grid_spec=pltpu.PrefetchScalarGridSpec(
            num_scalar_prefetch=1, grid=(steps,),
            in_specs=[pl.BlockSpec((th, D_MODEL), h_tile), whole((1, D_MODEL)), hbm, whole((1, D_MODEL)), hbm,
                      whole((N_DEV, n1)), hbm, whole((N_DEV, n2))],
            out_specs=[pl.BlockSpec((th, D_MODEL), h_tile), pl.BlockSpec((tm, tn), proj_tile), hbm]
            + [whole(shape) for shape in ada_outs],
            scratch_shapes=[pltpu.VMEM((s, D_MODEL), BF16), pltpu.VMEM((2, tn, D_MODEL), BF16),
                            pltpu.VMEM((8, D_MODEL), F32), pltpu.SemaphoreType.DMA((2,)), *_gather_scratch(),
                            *_ada_scratch(n1, n2)]),
        out_shape=[SDS((s, D_MODEL), BF16), SDS((s, D_IN), BF16), SDS(wt_all.shape, BF16)]
        + [SDS(shape, F32) for shape in ada_outs],
        input_output_aliases={3: 2},
        compiler_params=_params("arbitrary"),
    )(order, x, norm_g, wt_all, c, w_ada, b_ada8, w_ada_f, b_ada_f8)


def _rope_freqs():
    inv_freq = ROPE_THETA ** (-jnp.arange(0, HEAD_DIM, 2, dtype=F32) / HEAD_DIM)
    return jnp.tile(inv_freq, 4).reshape(1, 128)


class _RopeTables:
    def __init__(self, freq_ref, rows_ref, state_ref, last_ref):
        self.freq, self.rows, self.state, self.last = freq_ref, rows_ref, state_ref, last_ref

    def start(self, block, direction):
        ang = lax.broadcasted_iota(jnp.int32, (CHUNK, 128), 0).astype(F32) * self.freq[...]
        self.rows[0] = jnp.cos(ang)
        self.rows[1] = jnp.sin(ang)
        base = jnp.asarray(block * CHUNK, dtype=F32) * self.freq[...]
        turn = float(direction * CHUNK) * self.freq[...]
        self.state[0:1, :] = jnp.cos(base)
        self.state[1:2, :] = jnp.sin(base)
        self.state[2:3, :] = jnp.cos(turn)
        self.state[3:4, :] = jnp.sin(turn)

    def step(self):
        c, s, ct, st = (self.state[k:k + 1, :] for k in range(4))
        self.state[0:1, :] = c * ct - s * st
        self.state[1:2, :] = s * ct + c * st

    def tables(self):
        c, s = self.state[0:1, :], self.state[1:2, :]
        cos = c * self.rows[0] - s * self.rows[1]
        sin = s * self.rows[0] + c * self.rows[1]
        first_half = (lax.broadcasted_iota(jnp.int32, (1, 128), 1) & (HEAD_DIM - 1)) < HEAD_DIM // 2
        return cos, jnp.where(first_half, -sin, 0.0), jnp.where(first_half, 0.0, sin)

    def keep(self, tabs):
        for k in range(3):
            self.last[k] = tabs[k]

    def kept(self):
        return tuple(self.last[k] for k in range(3))


def _rope_scratch():
    return [pltpu.VMEM((2, CHUNK, 128), F32), pltpu.VMEM((8, 128), F32), pltpu.VMEM((3, CHUNK, 128), F32)]


def _rope(v, cos, sin_lo, sin_hi):
    width = v.shape[1]
    rep = (1, width // 128)
    return (v * jnp.tile(cos, rep) + pltpu.roll(v, width - 32, 1) * jnp.tile(sin_lo, rep)
            + pltpu.roll(v, 32, 1) * jnp.tile(sin_hi, rep))


def _rope_bwd(d, cos, sin_lo, sin_hi):
    width = d.shape[1]
    rep = (1, width // 128)
    return (d * jnp.tile(cos, rep) + pltpu.roll(d * jnp.tile(sin_lo, rep), 32, 1)
            + pltpu.roll(d * jnp.tile(sin_hi, rep), width - 32, 1))


def _layer_norm(v, g, b):
    mu = jnp.mean(v, axis=-1, keepdims=True)
    vc = v - mu
    rstd = lax.rsqrt(jnp.mean(vc * vc, axis=-1, keepdims=True) + NORM_EPS)
    vhat = vc * rstd
    return vhat * g + b, vhat, rstd


def _tril_bf16(w_ref, g):
    t = lax.broadcasted_iota(jnp.int32, (CHUNK, CHUNK), 0)
    tp = lax.broadcasted_iota(jnp.int32, (CHUNK, CHUNK), 1)
    return jnp.where(tp <= t, w_ref[g], 0.0).astype(BF16)


def _bias_columns(b_ref, out_ref):
    for g in range(A_GROUPS):
        out_ref[g] = jnp.broadcast_to(b_ref[pl.ds(g, 1), :], (CHUNK, CHUNK)).T


def _from_prev():
    r = lax.broadcasted_iota(jnp.int32, (CHUNK, 4 * CHUNK), 0)
    i = lax.broadcasted_iota(jnp.int32, (CHUNK, 4 * CHUNK), 1) & (CHUNK - 1)
    return r > i


def _set_unfold_masks(mask_ref):
    prev = _from_prev()
    mask_ref[0] = jnp.where(prev, 1.0, 0.0).astype(BF16)
    mask_ref[1] = jnp.where(prev, 0.0, 1.0).astype(BF16)


def _fold_band(t, from_prev):
    return jnp.where(from_prev, t[:CHUNK], t[CHUNK:])


def _unfold_band(t, mask_ref):
    return jnp.concatenate([t * mask_ref[0], t * mask_ref[1]], axis=0)


def _low_lanes():
    return lax.broadcasted_iota(jnp.int32, (1, 128), 1) < HEAD_DIM


def _stack_heads(pair_a, pair_b):
    lo = _low_lanes()
    return jnp.concatenate([jnp.where(lo, pair_a, 0.0), jnp.where(lo, 0.0, pair_a),
                            jnp.where(lo, pair_b, 0.0), jnp.where(lo, 0.0, pair_b)], axis=0).astype(BF16)


def _heads_to_lanes(per_group):
    rows = [t[:, r * CHUNK:(r + 1) * CHUNK] for t in per_group for r in range(4)]
    return jnp.concatenate(rows, axis=0).T


def _dup_kv_head(band, gk):
    pair = band[:, (gk // 2) * 128:(gk // 2 + 1) * 128]
    lo = _low_lanes()
    one = jnp.where(lo if gk % 2 == 0 else jnp.logical_not(lo), pair, 0.0)
    return (one + pltpu.roll(one, HEAD_DIM, 1)).astype(BF16)


def _fold_kv_head(dup_grad, gk):
    both = dup_grad + pltpu.roll(dup_grad, HEAD_DIM, 1)
    lo = _low_lanes()
    return jnp.where(lo if gk % 2 == 0 else jnp.logical_not(lo), both, 0.0)


def _attn_probs(q_st, k_dup, sink_row, from_prev, first_block):
    s = lax.dot_general(k_dup, q_st, NT, preferred_element_type=F32)
    no_prev = jnp.where(first_block, -jnp.inf, 0.0)
    s = jnp.where(from_prev, s[:CHUNK] + no_prev, s[CHUNK:])
    m = jnp.maximum(jnp.max(s, axis=0, keepdims=True), sink_row)
    p = jnp.exp(s - m)
    e_sink = jnp.exp(sink_row - m)
    inv = 1.0 / (jnp.sum(p, axis=0, keepdims=True) + e_sink)
    return p * inv, e_sink * inv


def _sink_row(sinks_ref, gk):
    return jnp.concatenate([jnp.full((1, CHUNK), sinks_ref[4 * gk + r], F32) for r in range(4)], axis=1)


def _mixer_specs(nb, rev):
    def blk(i):
        return nb - 1 - i if rev else i

    def prev(i):
        return jnp.maximum(blk(i) - 1, 0)

    return dict(
        cur=pl.BlockSpec((CHUNK, D_IN), lambda i, *_: (blk(i), 0)),
        prev_kv=pl.BlockSpec((CHUNK, 2 * 256), lambda i, *_: (prev(i), OFF_K // 512)),
        freq=pl.BlockSpec((1, 128), lambda i, *_: (0, 0)),
        vec=pl.BlockSpec((1, D_A), lambda i, *_: (0, 0)),
        wsp=pl.BlockSpec((A_GROUPS, CHUNK, CHUNK), lambda i, *_: (0, 0, 0)),
        bsp=pl.BlockSpec((A_GROUPS, CHUNK), lambda i, *_: (0, 0)),
        smem=pl.BlockSpec(memory_space=pltpu.SMEM),
        blk=blk,
    )


def _mixer_fwd(proj, freqs, ln_g, ln_b, w_sp, b_sp, sinks, wo_all):
    s = proj.shape[0]
    nb = s // CHUNK
    sp = _mixer_specs(nb, rev=False)

    def body(cur_ref, pkv_ref, freq_ref, lg_ref, lb_ref, w_ref, b_ref, sinks_ref, wo_in, y_ref, wo_ref,
             bcol, mask, rope_rows, rope_state, rope_last, send_sems, recv_sems):
        i = pl.program_id(0)
        gather = _InPlaceGather(wo_ref, send_sems, recv_sems)
        rope = _RopeTables(freq_ref, rope_rows, rope_state, rope_last)

        @pl.when(i == 0)
        def _():
            gather.start()
            _bias_columns(b_ref, bcol)
            _set_unfold_masks(mask)
            rope.start(-1, 1)
            rope_last[...] = jnp.zeros_like(rope_last)

        @pl.when(i == (3 * nb) // 4)
        def _():
            for j in range(3):
                gather.pass_on(j)

        vln, _, _ = _layer_norm(cur_ref[:, OFF_VA:OFF_ZA].astype(F32), lg_ref[...], lb_ref[...])
        vln = vln.astype(BF16)
        for g in range(A_GROUPS):
            cols = slice(g * 128, (g + 1) * 128)
            sg = jnp.dot(_tril_bf16(w_ref, g), vln[:, cols], preferred_element_type=F32) + bcol[g]
            u = cur_ref[:, OFF_U + g * 128:OFF_U + (g + 1) * 128].astype(F32)
            z = cur_ref[:, OFF_ZA + g * 128:OFF_ZA + (g + 1) * 128].astype(F32)
            y_ref[:, cols] = (u * sg * (z * _sigmoid(z))).astype(BF16)

        rope.step()
        cur_t, prev_t = rope.tables(), rope.kept()
        rope.keep(cur_t)
        qr = _rope(cur_ref[:, OFF_Q:OFF_K].astype(F32), *cur_t) * ATTN_SCALE
        kr = jnp.concatenate([_rope(pkv_ref[:, 0:256].astype(F32), *prev_t),
                              _rope(cur_ref[:, OFF_K:OFF_V].astype(F32), *cur_t)], axis=0)
        v_t = jnp.concatenate([pkv_ref[:, 256:512], cur_ref[:, OFF_V:OFF_ZB]], axis=0).astype(F32).T.astype(BF16)
        outs = []
        from_prev = _from_prev()
        for gk in range(N_KV_HEADS):
            q_st = _stack_heads(qr[:, (2 * gk) * 128:(2 * gk + 1) * 128], qr[:, (2 * gk + 1) * 128:(2 * gk + 2) * 128])
            probs, _ = _attn_probs(q_st, _dup_kv_head(kr, gk), _sink_row(sinks_ref, gk), from_prev, i == 0)
            outs.append(jnp.dot(v_t[gk * HEAD_DIM:(gk + 1) * HEAD_DIM], _unfold_band(probs.astype(BF16), mask),
                                preferred_element_type=F32))
        zb = cur_ref[:, OFF_ZB:D_IN].astype(F32)
        y_ref[:, D_A:D_MODEL] = (_heads_to_lanes(outs) * (zb * _sigmoid(zb))).astype(BF16)

        @pl.when(i == nb - 1)
        def _():
            gather.wait_sibling(0)
            for j in range(3):
                gather.wait_sibling(4 + j)
            gather.wait_sends()

    hbm = pl.BlockSpec(memory_space=pl.ANY)
    return pl.pallas_call(
        body, name="mixer_fwd", grid=(nb,),
        in_specs=[sp["cur"], sp["prev_kv"], sp["freq"], sp["vec"], sp["vec"], sp["wsp"], sp["bsp"], sp["smem"], hbm],
        out_specs=[pl.BlockSpec((CHUNK, D_MODEL), lambda i: (i, 0)), hbm],
        out_shape=[SDS((s, D_MODEL), BF16), SDS(wo_all.shape, wo_all.dtype)],
        scratch_shapes=[pltpu.VMEM((A_GROUPS, CHUNK, CHUNK), F32), pltpu.VMEM((2, CHUNK, 4 * CHUNK), BF16),
                        *_rope_scratch(), *_gather_scratch()],
        input_output_aliases={8: 1},
        compiler_params=_params("arbitrary"),
    )(proj, proj, freqs, ln_g, ln_b, w_sp, b_sp, sinks, wo_all)


def _out_proj_loss(y, x, target, wo, gate, shift_f, scale_f, fng):
    s = y.shape[0]
    tm, tr = 256, 128
    nt = s // tm

    def body(y_ref, x_ref, t_ref, wo_ref, gate_ref, sh_ref, sc_ref, g_ref, dx1_ref, do_ref, dy_ref, sums_ref,
             do_last, do_work):
        i = pl.program_id(0)

        @pl.when(i == 0)
        def _():
            sums_ref[...] = jnp.zeros_like(sums_ref)
            do_last[...] = jnp.zeros_like(do_last)

        do_work[...] = do_last[...]
        o = jnp.dot(y_ref[...], wo_ref[...], preferred_element_type=F32)
        gate, g, sh = gate_ref[...], g_ref[...], sh_ref[...]
        one_sc = 1.0 + sc_ref[...]
        cs, inv_d = g * one_sc, 1.0 / D_MODEL

        def rowsum(v):
            return jnp.sum(v, axis=0, keepdims=True)

        sums = [jnp.zeros((1, D_MODEL), F32) for _ in range(4)]
        for c in range(tm // tr):
            rows = slice(c * tr, (c + 1) * tr)
            oc = o[rows]
            x1 = x_ref[rows, :] + gate * oc
            r = lax.rsqrt(jnp.sum(x1 * x1, axis=-1, keepdims=True) * inv_d + NORM_EPS)
            x1n = x1 * r
            diff = x1n * cs + sh - t_ref[rows, :]
            w = diff * x1n
            lane_sum = jnp.sum(w * cs, axis=-1, keepdims=True)
            dx1 = (diff * cs) * (r * inv_d) - x1n * (r * lane_sum * (inv_d * inv_d))
            dx1_ref[rows, :] = dx1
            do = (dx1 * gate).astype(BF16)
            do_ref[rows, :] = do
            do_last[rows, :] = do
            for k, v in enumerate((dx1 * oc, diff, w, diff * diff)):
                sums[k] = sums[k] + rowsum(v)
        live = jnp.where(i < nt, 1.0, 0.0)
        for row, v in ((SUM_GATE, sums[0]), (SUM_SHIFT_F, inv_d * sums[1]), (SUM_SCALE_F, inv_d * (sums[2] * g)),
                       (SUM_FNG, inv_d * (sums[2] * one_sc)), (SUM_SQ_ERR, sums[3])):
            sums_ref[row:row + 1, :] += live * v
        dy_ref[...] = lax.dot_general(do_work[...], wo_ref[...], NT, preferred_element_type=F32).astype(BF16)

    tile = pl.BlockSpec((tm, D_MODEL), lambda i: (jnp.minimum(i, nt - 1), 0))
    row = pl.BlockSpec((1, D_MODEL), lambda i: (0, 0))
    return pl.pallas_call(
        body, name="out_proj_loss", grid=(nt + 1,),
        in_specs=[tile, tile, tile, pl.BlockSpec((D_MODEL, D_MODEL), lambda i: (0, 0)), row, row, row, row],
        out_specs=[tile, tile, pl.BlockSpec((tm, D_MODEL), lambda i: (jnp.maximum(i - 1, 0), 0)),
                   pl.BlockSpec((8, D_MODEL), lambda i: (0, 0))],
        out_shape=[SDS((s, D_MODEL), F32), SDS((s, D_MODEL), BF16), SDS((s, D_MODEL), BF16), SDS((8, D_MODEL), F32)],
        scratch_shapes=[pltpu.VMEM((tm, D_MODEL), BF16), pltpu.VMEM((tm, D_MODEL), BF16)],
        compiler_params=_params("arbitrary"),
    )(y, x, target, wo, gate, shift_f, scale_f, fng)


ROW_DBSP, ROW_DSINKS, MISC_ROWS = 0, 8, 32


def _mixer_bwd(me, proj, dy, freqs, ln_g, ln_b, w_sp, b_sp, sinks, pair):
    s = proj.shape[0]
    nb = s // CHUNK
    sp = _mixer_specs(nb, rev=True)

    def body(me_ref, cur_ref, pkv_ref, dy_ref, freq_ref, lg_ref, lb_ref, w_ref, b_ref, sinks_ref, pair_ref,
             dproj_ref, dln_ref, dw_ref, misc_ref, parts_ref, bcol, dbcol, carry, mask, rope_rows, rope_state,
             rope_last, send_sems, recv_sems):
        i = pl.program_id(0)
        block = nb - 1 - i
        rope = _RopeTables(freq_ref, rope_rows, rope_state, rope_last)

        @pl.when(i == 0)
        def _():
            for cp in _chip_scatter(pair_ref, parts_ref, send_sems, recv_sems):
                cp.start()
            _bias_columns(b_ref, bcol)
            _set_unfold_masks(mask)
            rope.start(nb - 1, -1)
            rope.keep(rope.tables())
            dbcol[...] = jnp.zeros_like(dbcol)
            carry[...] = jnp.zeros_like(carry)
            dln_ref[...] = jnp.zeros_like(dln_ref)
            dw_ref[...] = jnp.zeros_like(dw_ref)
            misc_ref[...] = jnp.zeros_like(misc_ref)

        vln, vhat, rstd = _layer_norm(cur_ref[:, OFF_VA:OFF_ZA].astype(F32), lg_ref[...], lb_ref[...])
        vln = vln.astype(BF16)
        d_vln = []
        for g in range(A_GROUPS):
            cols = slice(g * 128, (g + 1) * 128)
            w_g = _tril_bf16(w_ref, g)
            sg = jnp.dot(w_g, vln[:, cols], preferred_element_type=F32) + bcol[g]
            u = cur_ref[:, OFF_U + g * 128:OFF_U + (g + 1) * 128].astype(F32)
            z = cur_ref[:, OFF_ZA + g * 128:OFF_ZA + (g + 1) * 128].astype(F32)
            dya = dy_ref[:, cols].astype(F32)
            sig = _sigmoid(z)
            d_ya = dya * (z * sig)
            dproj_ref[:, OFF_ZA + g * 128:OFF_ZA + (g + 1) * 128] = (
                dya * (u * sg) * (sig * (1.0 + z * (1.0 - sig)))).astype(BF16)
            dproj_ref[:, OFF_U + g * 128:OFF_U + (g + 1) * 128] = (d_ya * sg).astype(BF16)
            d_s = d_ya * u
            dbcol[g] += d_s
            d_sb = d_s.astype(BF16)
            dw_ref[g] += lax.dot_general(d_sb, vln[:, cols], NT, preferred_element_type=F32)
            d_vln.append(lax.dot_general(w_g, d_sb, TN, preferred_element_type=F32))
        d_vln = jnp.concatenate(d_vln, axis=1)
        dln_ref[0:1, :] += jnp.sum(d_vln * vhat, axis=0, keepdims=True)
        dln_ref[1:2, :] += jnp.sum(d_vln, axis=0, keepdims=True)
        d_vhat = d_vln * lg_ref[...]
        d_va = rstd * (d_vhat - jnp.mean(d_vhat, axis=-1, keepdims=True)
                       - vhat * jnp.mean(d_vhat * vhat, axis=-1, keepdims=True))
        dproj_ref[:, OFF_VA:OFF_ZA] = d_va.astype(BF16)

        cur_t = rope.kept()
        rope.step()
        prev_t = rope.tables()
        rope.keep(prev_t)
        band_t = tuple(jnp.concatenate([p, c], axis=0) for p, c in zip(prev_t, cur_t))
        qr = _rope(cur_ref[:, OFF_Q:OFF_K].astype(F32), *cur_t) * ATTN_SCALE
        kr = jnp.concatenate([_rope(pkv_ref[:, 0:256].astype(F32), *prev_t),
                              _rope(cur_ref[:, OFF_K:OFF_V].astype(F32), *cur_t)], axis=0)
        vb = jnp.concatenate([pkv_ref[:, 256:512], cur_ref[:, OFF_V:OFF_ZB]], axis=0).astype(F32)
        k_t, v_t = (kr.T * ATTN_SCALE).astype(BF16), vb.T.astype(BF16)
        zb = cur_ref[:, OFF_ZB:D_IN].astype(F32)
        dyb = dy_ref[:, D_A:D_MODEL].astype(F32)
        sig = _sigmoid(zb)
        d_yb = dyb * (zb * sig)
        outs, dqs = [], []
        dk_pairs = [jnp.zeros((2 * CHUNK, 128), F32) for _ in range(2)]
        dv_pairs = [jnp.zeros((2 * CHUNK, 128), F32) for _ in range(2)]
        from_prev = _from_prev()
        for gk in range(N_KV_HEADS):
            heads = slice(gk * HEAD_DIM, (gk + 1) * HEAD_DIM)
            q_st = _stack_heads(qr[:, (2 * gk) * 128:(2 * gk + 1) * 128], qr[:, (2 * gk + 1) * 128:(2 * gk + 2) * 128])
            k_dup, v_dup = _dup_kv_head(kr, gk), _dup_kv_head(vb, gk)
            probs, p_sink = _attn_probs(q_st, k_dup, _sink_row(sinks_ref, gk), from_prev, block == 0)
            probs_b = _unfold_band(probs.astype(BF16), mask)
            outs.append(jnp.dot(v_t[heads], probs_b, preferred_element_type=F32))
            do_st = _stack_heads(d_yb[:, (2 * gk) * 128:(2 * gk + 1) * 128], d_yb[:, (2 * gk + 1) * 128:(2 * gk + 2) * 128])
            dp = _fold_band(lax.dot_general(v_dup, do_st, NT, preferred_element_type=F32), from_prev)
            delta = jnp.sum(probs * dp, axis=0, keepdims=True)
            ds = _unfold_band((probs * (dp - delta)).astype(BF16), mask)
            d_sink = -p_sink * delta
            for r in range(4):
                row = ROW_DSINKS + 4 * gk + r
                misc_ref[row:row + 1, :] += jnp.broadcast_to(
                    jnp.sum(d_sink[:, r * CHUNK:(r + 1) * CHUNK], axis=1, keepdims=True), (1, 128))
            dqs.append(jnp.dot(k_t[heads], ds, preferred_element_type=F32))
            dk_pairs[gk // 2] += _fold_kv_head(jnp.dot(ds, q_st, preferred_element_type=F32), gk)
            dv_pairs[gk // 2] += _fold_kv_head(jnp.dot(probs_b, do_st, preferred_element_type=F32), gk)
        dproj_ref[:, OFF_ZB:D_IN] = (dyb * _heads_to_lanes(outs) * (sig * (1.0 + zb * (1.0 - sig)))).astype(BF16)
        dproj_ref[:, OFF_Q:OFF_K] = _rope_bwd(_heads_to_lanes(dqs), *cur_t).astype(BF16)
        dk_band = _rope_bwd(jnp.concatenate(dk_pairs, axis=1), *band_t)
        dv_band = jnp.concatenate(dv_pairs, axis=1)
        dproj_ref[:, OFF_K:OFF_V] = (dk_band[CHUNK:] + carry[:, 0:256]).astype(BF16)
        dproj_ref[:, OFF_V:OFF_ZB] = (dv_band[CHUNK:] + carry[:, 256:512]).astype(BF16)
        carry[:, 0:256] = dk_band[:CHUNK]
        carry[:, 256:512] = dv_band[:CHUNK]

        @pl.when(i == nb - 1)
        def _():
            t = lax.broadcasted_iota(jnp.int32, (CHUNK, CHUNK), 0)
            tp = lax.broadcasted_iota(jnp.int32, (CHUNK, CHUNK), 1)
            for g in range(A_GROUPS):
                dw_ref[g] = jnp.where(tp <= t, dw_ref[g], 0.0)
                misc_ref[pl.ds(ROW_DBSP + g, 1), :] = jnp.sum(dbcol[g].T, axis=0, keepdims=True)
            scatter = _chip_scatter(pair_ref, parts_ref, send_sems, recv_sems)
            for cp in scatter:
                cp.wait_recv()
            for cp in scatter:
                cp.wait_send()

    blk = sp["blk"]
    hbm = pl.BlockSpec(memory_space=pl.ANY)
    return pl.pallas_call(
        body, name="mixer_bwd",
        grid_spec=pltpu.PrefetchScalarGridSpec(
            num_scalar_prefetch=1, grid=(nb,),
            in_specs=[sp["cur"], sp["prev_kv"], pl.BlockSpec((CHUNK, D_MODEL), lambda i, me_ref: (blk(i), 0)),
                      sp["freq"], sp["vec"], sp["vec"], sp["wsp"], sp["bsp"], sp["smem"], hbm],
            out_specs=[pl.BlockSpec((CHUNK, D_IN), lambda i, me_ref: (blk(i), 0)),
                       pl.BlockSpec((8, D_A), lambda i, me_ref: (me_ref[0], 0)),
                       pl.BlockSpec((A_GROUPS, CHUNK, CHUNK), lambda i, me_ref: (me_ref[0], 0, 0)),
                       pl.BlockSpec((MISC_ROWS, 128), lambda i, me_ref: (me_ref[0], 0)), hbm],
            scratch_shapes=[pltpu.VMEM((A_GROUPS, CHUNK, CHUNK), F32), pltpu.VMEM((A_GROUPS, CHUNK, CHUNK), F32),
                            pltpu.VMEM((CHUNK, 512), F32), pltpu.VMEM((2, CHUNK, 4 * CHUNK), BF16),
                            *_rope_scratch(), *_scatter_scratch()]),
        out_shape=[SDS((s, D_IN), BF16), SDS((N_DEV * 8, D_A), F32), SDS((N_DEV * A_GROUPS, CHUNK, CHUNK), F32),
                   SDS((N_DEV * MISC_ROWS, 128), F32), SDS((3,) + pair.shape[1:], pair.dtype)],
        compiler_params=_params("arbitrary"),
    )(me, proj, proj, dy, freqs, ln_g, ln_b, w_sp, b_sp, sinks, pair)


def _wgrad_pair(name, a, b, gathers=()):
    s, m = a.shape
    n = b.shape[1]
    bm, half = m // 4, m // 8
    bt = min(1024, s)
    steps = s // bt
    last = 4 * steps
    n_g = len(gathers)

    def body(*refs):
        a_ref, b_ref = refs[:2]
        out_ref, bufs = refs[2 + n_g], refs[3 + n_g:3 + 2 * n_g]
        acc, kept, got, sent, send_sems, recv_sems = refs[3 + 2 * n_g:9 + 2 * n_g]
        sems = refs[9 + 2 * n_g:]
        g = pl.program_id(0)
        tile, t = g // steps, g % steps
        mx, my, mc = _mesh_pos()
        jobs = [_InPlaceGather(bufs[k], sems[2 * k], sems[2 * k + 1]) for k in range(n_g)]

        def exchange(q):
            return pltpu.make_async_remote_copy(src_ref=sent, dst_ref=got.at[q % 2], send_sem=send_sems.at[q],
                                                recv_sem=recv_sems.at[q], device_id=(mx, my, 1 - mc),
                                                device_id_type=MESH)

        @pl.when(g == 0)
        def _():
            for job in jobs:
                job.start()

        @pl.when(g == 2 * steps)
        def _():
            for job in jobs:
                for j in range(3):
                    job.pass_on(j)

        @pl.when(g < last)
        def _():
            prod = lax.dot_general(a_ref[...], b_ref[...], TN, preferred_element_type=F32)

            @pl.when(t == 0)
            def _():
                acc[...] = prod

            @pl.when(t > 0)
            def _():
                acc[...] += prod

            @pl.when(t == steps - 1)
            def _():
                @pl.when(tile > 0)
                def _():
                    exchange(tile - 1).wait_send()

                kept[tile % 2] = acc[pl.ds(pl.multiple_of(mc * half, 8), half), :].astype(BF16)
                sent[...] = acc[pl.ds(pl.multiple_of((1 - mc) * half, 8), half), :].astype(BF16)
                exchange(tile).start()

        @pl.when((t == 0) & (g > 0))
        def _():
            q = tile - 1
            exchange(q).wait_recv()
            out_ref[0] = (kept[q % 2].astype(F32) + got[q % 2].astype(F32)).astype(BF16)

        @pl.when(g == last)
        def _():
            exchange(3).wait_send()
            for job in jobs:
                job.wait_sibling(0)
                for j in range(3):
                    job.wait_sibling(4 + j)
                job.wait_sends()

    def a_tile(g):
        gg = jnp.minimum(g, last - 1)
        return (gg % steps, gg // steps)

    def b_tile(g):
        return (jnp.minimum(g, last - 1) % steps, 0)

    hbm = pl.BlockSpec(memory_space=pl.ANY)
    outs = pl.pallas_call(
        body, name=name, grid=(last + 1,),
        in_specs=[pl.BlockSpec((bt, bm), a_tile), pl.BlockSpec((bt, n), b_tile)] + [hbm] * n_g,
        out_specs=[pl.BlockSpec((1, half, n), lambda g: (jnp.maximum(g - 1, 0) // steps, 0, 0))] + [hbm] * n_g,
        out_shape=[SDS((4, half, n), BF16)] + [SDS(gb.shape, gb.dtype) for gb in gathers],
        scratch_shapes=[pltpu.VMEM((bm, n), F32), pltpu.VMEM((2, half, n), BF16), pltpu.VMEM((2, half, n), BF16),
                        pltpu.VMEM((half, n), BF16), pltpu.SemaphoreType.DMA((4,)), pltpu.SemaphoreType.DMA((4,))]
        + _gather_scratch() * n_g,
        input_output_aliases={2 + k: 1 + k for k in range(n_g)},
        compiler_params=_params("arbitrary"),
    )(a, b, *gathers)
    return outs[0], outs[1:]


def _in_proj_bwd(dproj, wt, x, dx1, scale, norm_g, sums_o, pair):
    s = x.shape[0]
    tm, tk, tr = min(1024, s), D_IN // 4, 64
    ksteps = D_IN // tk

    def body(dp_ref, wt_ref, x_hbm, dx1_hbm, sc_ref, g_ref, so_ref, pair_ref, gx_ref, sums_ref, parts_ref, x_buf,
             dx1_buf, tile_sems, send_sems, recv_sems):
        i, k = pl.program_id(0), pl.program_id(1)

        def tile_copies():
            rows = pl.ds(pl.multiple_of(i * tm, tm), tm)
            return (pltpu.make_async_copy(x_hbm.at[rows], x_buf, tile_sems.at[0]),
                    pltpu.make_async_copy(dx1_hbm.at[rows], dx1_buf, tile_sems.at[1]))

        @pl.when((i == 0) & (k == 0))
        def _():
            for cp in _chip_scatter(pair_ref, parts_ref, send_sems, recv_sems):
                cp.start()
            sums_ref[...] = so_ref[...]

        @pl.when(k == 0)
        def _():
            for cp in tile_copies():
                cp.start()
            gx_ref[...] = jnp.dot(dp_ref[...], wt_ref[...], preferred_element_type=F32)

        @pl.when(k > 0)
        def _():
            gx_ref[...] += jnp.dot(dp_ref[...], wt_ref[...], preferred_element_type=F32)

        @pl.when(k == ksteps - 1)
        def _():
            for cp in tile_copies():
                cp.wait()
            one_sc, g = 1.0 + sc_ref[...], g_ref[...]
            cs = one_sc * g

            def chunk(j, sums):
                rows = pl.ds(pl.multiple_of(j * tr, tr), tr)
                dh, xv = gx_ref[rows, :], x_buf[rows, :]
                dhx = dh * xv
                r = lax.rsqrt(jnp.sum(xv * xv, axis=-1, keepdims=True) * (1.0 / D_MODEL) + NORM_EPS)
                coef = (r * r * r) * (jnp.sum(dhx * cs, axis=-1, keepdims=True) * (1.0 / D_MODEL))
                gx_ref[rows, :] = dx1_buf[rows, :] + r * (dh * cs) - xv * coef
                return (sums[0] + jnp.sum(dh, axis=0, keepdims=True), sums[1] + jnp.sum(dhx * r, axis=0, keepdims=True))

            zero = jnp.zeros((1, D_MODEL), F32)
            sums = lax.fori_loop(0, tm // tr, chunk, (zero, zero))
            sums_ref[SUM_SHIFT:SUM_SHIFT + 1, :] += sums[0]
            sums_ref[SUM_SCALE:SUM_SCALE + 1, :] += sums[1] * g
            sums_ref[SUM_NORM_G:SUM_NORM_G + 1, :] += sums[1] * one_sc

        @pl.when((i == s // tm - 1) & (k == ksteps - 1))
        def _():
            scatter = _chip_scatter(pair_ref, parts_ref, send_sems, recv_sems)
            for cp in scatter:
                cp.wait_recv()
            for cp in scatter:
                cp.wait_send()

    row = pl.BlockSpec((1, D_MODEL), lambda i, k: (0, 0))
    hbm = pl.BlockSpec(memory_space=pl.ANY)
    return pl.pallas_call(
        body, name="in_proj_bwd", grid=(s // tm, ksteps),
        in_specs=[pl.BlockSpec((tm, tk), lambda i, k: (i, k)), pl.BlockSpec((tk, D_MODEL), lambda i, k: (k, 0)),
                  hbm, hbm, row, row, pl.BlockSpec((8, D_MODEL), lambda i, k: (0, 0)), hbm],
        out_specs=[pl.BlockSpec((tm, D_MODEL), lambda i, k: (i, 0)), pl.BlockSpec((8, D_MODEL), lambda i, k: (0, 0)),
                   hbm],
        out_shape=[SDS((s, D_MODEL), F32), SDS((8, D_MODEL), F32), SDS((3,) + pair.shape[1:], pair.dtype)],
        scratch_shapes=[pltpu.VMEM((tm, D_MODEL), F32), pltpu.VMEM((tm, D_MODEL), F32),
                        pltpu.SemaphoreType.DMA((2,)), *_scatter_scratch()],
        compiler_params=_params("arbitrary", "arbitrary"),
    )(dproj, wt, x, dx1, scale, norm_g, sums_o, pair)


def _sum_chips(own_ref, parts_ref):
    return ((own_ref[0].astype(F32) + parts_ref[0].astype(F32)) + parts_ref[1].astype(F32)) + parts_ref[2].astype(F32)


def _adam_rows(name, chip, pair, parts, w, m, v, tr):
    rows = w.shape[0]

    def body(chip_ref, own_ref, p_ref, w_ref, m_ref, v_ref, g_ref, d_ref, nm_ref, nv_ref):
        g = _sum_chips(own_ref, p_ref)
        g_ref[...] = g
        d_ref[...], nm_ref[...], nv_ref[...] = _adamw(w_ref[...], g, m_ref[...], v_ref[...])

    blk = pl.BlockSpec((tr, D_MODEL), lambda j, chip_ref: (j, 0))
    return pl.pallas_call(
        body, name=name,
        grid_spec=pltpu.PrefetchScalarGridSpec(
            num_scalar_prefetch=1, grid=(rows // tr,),
            in_specs=[pl.BlockSpec((1, tr, D_MODEL), lambda j, chip_ref: (chip_ref[0], j, 0)),
                      pl.BlockSpec((3, tr, D_MODEL), lambda j, chip_ref: (0, j, 0)), blk, blk, blk],
            out_specs=[blk] * 4),
        out_shape=[SDS(w.shape, F32)] * 4, compiler_params=_params("parallel"),
    )(chip, pair, parts, w, m, v)


def _adam_ada(name, cact, dmod, w, m, v):
    n = w.shape[1]
    tr = 512

    def body(c_ref, dm_ref, w_ref, m_ref, v_ref, g_ref, d_ref, nm_ref, nv_ref):
        pad_c = jnp.concatenate([c_ref[...], jnp.zeros_like(c_ref)], axis=0).astype(BF16)
        pad_d = jnp.concatenate([dm_ref[...], jnp.zeros_like(dm_ref)], axis=0).astype(BF16)
        g = lax.dot_general(pad_c, pad_d, TN, preferred_element_type=F32)
        g_ref[...] = g
        d_ref[...], nm_ref[...], nv_ref[...] = _adamw(w_ref[...], g, m_ref[...], v_ref[...])

    blk = pl.BlockSpec((tr, n), lambda j: (j, 0))
    return pl.pallas_call(
        body, name=name, grid=(D_MODEL // tr,),
        in_specs=[pl.BlockSpec((N_DEV, tr), lambda j: (0, j)), pl.BlockSpec((N_DEV, n), lambda j: (0, 0)),
                  blk, blk, blk],
        out_specs=[blk] * 4, out_shape=[SDS(w.shape, F32)] * 4,
        compiler_params=_params("parallel"),
    )(cact, dmod, w, m, v)


SMALL_PARAMS = ("w_spatial", "b_spatial", "sinks", "norm_g", "ln_v_g", "ln_v_b", "final_norm_g", "b_ada", "b_ada_final")


def _adam_small(d_wsp, misc, d_ln, sums, params):
    n_p = len(SMALL_PARAMS)

    def body(*refs):
        wsp_ref, misc_ref, ln_ref, sums_ref = refs[:4]
        wmv = [refs[4 + 3 * k:7 + 3 * k] for k in range(n_p)]
        loss_ref = refs[4 + 3 * n_p]
        outs = [refs[5 + 3 * n_p + 4 * k:9 + 3 * n_p + 4 * k] for k in range(n_p)]

        def column_sum(row):
            return total(sums_ref, (row, row + 1))

        def total(ref, rows=None):
            def part(j):
                return ref[j] if rows is None else ref[j, rows[0]:rows[1], :]
            acc = part(0)
            for j in range(1, N_DEV):
                acc = acc + part(j)
            return acc

        sink_rows = total(misc_ref, (ROW_DSINKS, ROW_DSINKS + 16))
        diag = (lax.broadcasted_iota(jnp.int32, (16, 128), 0) == lax.broadcasted_iota(jnp.int32, (16, 128), 1))
        grads = dict(
            w_spatial=total(wsp_ref), b_spatial=total(misc_ref, (ROW_DBSP, ROW_DBSP + A_GROUPS)),
            sinks=jnp.sum(jnp.where(diag, sink_rows, 0.0), axis=0, keepdims=True),
            norm_g=column_sum(SUM_NORM_G), ln_v_g=total(ln_ref, (0, 1)), ln_v_b=total(ln_ref, (1, 2)),
            final_norm_g=column_sum(SUM_FNG),
            b_ada=jnp.concatenate([column_sum(SUM_SHIFT), column_sum(SUM_SCALE), column_sum(SUM_GATE)], axis=1),
            b_ada_final=jnp.concatenate([column_sum(SUM_SHIFT_F), column_sum(SUM_SCALE_F)], axis=1))
        sq_err = jnp.sum(column_sum(SUM_SQ_ERR), axis=1, keepdims=True)
        loss_ref[...] = jnp.broadcast_to(sq_err * (0.5 / D_MODEL), (1, 128))
        for k, name in enumerate(SMALL_PARAMS):
            w_ref, m_ref, v_ref = wmv[k]
            g_ref, d_ref, nm_ref, nv_ref = outs[k]
            g_ref[...] = grads[name]
            d_ref[...], nm_ref[...], nv_ref[...] = _adamw(w_ref[...], grads[name], m_ref[...], v_ref[...])

    flat = [a for name in SMALL_PARAMS for a in params[name]]
    vmem = pl.BlockSpec(memory_space=pltpu.VMEM)
    out_shape = [SDS((1, 128), F32)] + [SDS(params[name][0].shape, F32) for name in SMALL_PARAMS for _ in range(4)]
    outs = pl.pallas_call(
        body, name="adam_small", in_specs=[vmem] * (4 + len(flat)), out_specs=[vmem] * len(out_shape),
        out_shape=out_shape, compiler_params=_params(),
    )(d_wsp, misc, d_ln, sums, *flat)
    return outs[0], {name: outs[1 + 4 * k:5 + 4 * k] for k, name in enumerate(SMALL_PARAMS)}


def kernel(x, c, w_ada, b_ada, norm_g, w_in, ln_v_g, ln_v_b, w_spatial, b_spatial, sinks, w_out, w_ada_final, b_ada_final, final_norm_g, loss_target, m_w_ada, m_b_ada, m_norm_g, m_w_in, m_ln_v_g, m_ln_v_b, m_w_spatial, m_b_spatial, m_sinks, m_w_out, m_w_ada_final, m_b_ada_final, m_final_norm_g, v_w_ada, v_b_ada, v_norm_g, v_w_in, v_ln_v_g, v_ln_v_b, v_w_spatial, v_b_spatial, v_sinks, v_w_out, v_w_ada_final, v_b_ada_final, v_final_norm_g):
    me = 4 * lax.axis_index("x") + 2 * lax.axis_index("y") + lax.axis_index("c")
    x2, tgt = x[0], loss_target[0]
    fng = final_norm_g.reshape(1, D_MODEL)

    n_ada, n_ada_f = w_ada.shape[2], w_ada_final.shape[1]
    wt_f32, m_wt, v_wt = (jnp.swapaxes(a, 1, 2)[0] for a in (w_in, m_w_in, v_w_in))
    xi, yi = lax.axis_index("x"), lax.axis_index("y")
    chip_order = jnp.stack([2 * xi + yi, 2 * (1 - xi) + yi, 2 * xi + 1 - yi, 2 * (1 - xi) + 1 - yi]).astype(jnp.int32)
    wt_mine, wo_mine, w_ada_b, w_ada_f_b = _prep_weights(me.reshape(1), wt_f32, w_out[0], w_ada[0], w_ada_final)

    freqs = _rope_freqs()
    sinks_v = sinks.reshape(16)
    h, proj, wt, cact, mod, mod_f = _gather_in_proj(
        chip_order, x2, norm_g, wt_mine, c, w_ada_b, b_ada.reshape(N_DEV, n_ada), w_ada_f_b,
        b_ada_final.reshape(N_DEV, n_ada_f))
    cact = cact.reshape(N_DEV, D_MODEL)
    mod, mod_f = mod.reshape(1, 3 * D_MODEL), mod_f.reshape(1, 2 * D_MODEL)
    scale, gate = mod[:, D_MODEL:2 * D_MODEL], mod[:, 2 * D_MODEL:]
    shift_f, scale_f = mod_f[:, :D_MODEL], mod_f[:, D_MODEL:]
    y, wo = _mixer_fwd(proj, freqs, ln_v_g, ln_v_b, w_spatial[0], b_spatial[0], sinks_v, wo_mine)
    dx1, do, dy, sums_o = _out_proj_loss(y, x2, tgt, wo, gate, shift_f, scale_f, fng)

    chip = (2 * lax.axis_index("x") + lax.axis_index("y")).reshape(1)
    pair_out, _ = _wgrad_pair("wgrad_out", y, do)
    dproj, d_ln, d_wsp, misc, parts_out = _mixer_bwd(
        me.reshape(1), proj, dy, freqs, ln_v_g, ln_v_b, w_spatial[0], b_spatial[0], sinks_v, pair_out)
    pair_in, (d_ln, d_wsp, misc) = _wgrad_pair(
        "wgrad_in", dproj, h, gathers=(d_ln, d_wsp.reshape(N_DEV * A_GROUPS * CHUNK, CHUNK), misc))
    grad_x, sums, parts_in = _in_proj_bwd(dproj, wt, x2, dx1, scale, norm_g, sums_o, pair_in)
    wt_leaves = [jnp.swapaxes(a[None], 1, 2)
                 for a in _adam_rows("adam_w_in", chip, pair_in, parts_in, wt_f32, m_wt, v_wt, 176)]
    w_out_leaves = [a[None] for a in _adam_rows("adam_w_out", chip, pair_out, parts_out, w_out[0], m_w_out[0], v_w_out[0], 64)]

    (sums,) = _all_gather("gather_sums", [sums], pltpu.VMEM)
    natural = dict(w_spatial=(A_GROUPS * CHUNK, CHUNK), b_spatial=(A_GROUPS, CHUNK), sinks=(1, 16), norm_g=(1, D_MODEL),
                   ln_v_g=(1, D_A), ln_v_b=(1, D_A), final_norm_g=(1, D_MODEL), b_ada=(1, 3 * D_MODEL),
                   b_ada_final=(1, 2 * D_MODEL))
    given = dict(
        w_spatial=(w_spatial, m_w_spatial, v_w_spatial), b_spatial=(b_spatial, m_b_spatial, v_b_spatial),
        sinks=(sinks, m_sinks, v_sinks), norm_g=(norm_g, m_norm_g, v_norm_g), ln_v_g=(ln_v_g, m_ln_v_g, v_ln_v_g),
        ln_v_b=(ln_v_b, m_ln_v_b, v_ln_v_b), final_norm_g=(final_norm_g, m_final_norm_g, v_final_norm_g),
        b_ada=(b_ada, m_b_ada, v_b_ada), b_ada_final=(b_ada_final, m_b_ada_final, v_b_ada_final))
    params = {name: tuple(a.reshape(natural[name]) for a in given[name]) for name in SMALL_PARAMS}
    params["sinks"] = tuple(jnp.pad(a, ((0, 0), (0, 128 - 16))) for a in params["sinks"])
    loss, small = _adam_small(d_wsp.reshape(N_DEV, A_GROUPS * CHUNK, CHUNK), misc.reshape(N_DEV, MISC_ROWS, 128),
                              d_ln.reshape(N_DEV, 8, D_A), sums, params)
    small["sinks"] = [a[:, :16] for a in small["sinks"]]
    small = {name: [a.reshape(given[name][0].shape) for a in small[name]] for name in SMALL_PARAMS}

    dmod_all = jnp.concatenate([sums[:, SUM_SHIFT], sums[:, SUM_SCALE], sums[:, SUM_GATE]], axis=1)
    dmod_f_all = jnp.concatenate([sums[:, SUM_SHIFT_F], sums[:, SUM_SCALE_F]], axis=1)
    dmod_mine = lax.dynamic_slice_in_dim(dmod_all, me * n_ada, n_ada, axis=1)
    dmod_f_mine = lax.dynamic_slice_in_dim(dmod_f_all, me * n_ada_f, n_ada_f, axis=1)
    ada = _adam_ada("adam_w_ada", cact, dmod_mine, w_ada[0], m_w_ada[0], v_w_ada[0])
    ada_f = _adam_ada("adam_w_ada_final", cact, dmod_f_mine, w_ada_final, m_w_ada_final, v_w_ada_final)

    def leaves(k):
        return (ada[k][None], small["b_ada"][k], small["norm_g"][k], wt_leaves[k], small["ln_v_g"][k],
                small["ln_v_b"][k], small["w_spatial"][k], small["b_spatial"][k], small["sinks"][k], w_out_leaves[k],
                ada_f[k], small["b_ada_final"][k], small["final_norm_g"][k])

    return (loss[0, 0], grad_x[None], *leaves(0), *leaves(1), *leaves(2), *leaves(3))
```

```python
import jax
import jax.numpy as jnp
from jax import lax
from jax.experimental import pallas as pl
from jax.experimental.pallas import tpu as pltpu

D_MODEL = 2048
D_IN = 5632
D_A = 1024
CHUNK = 128
A_GROUPS = 8
HEAD_DIM = 64
N_KV_HEADS = 4
N_DEV = 8
ROPE_THETA = 10000.0
NORM_EPS = 1e-5
ATTN_SCALE = HEAD_DIM ** -0.5

ADAM_LR = 0.001
ADAM_B1 = 0.9
ADAM_B2 = 0.999
ADAM_EPS = 1e-08
ADAM_WD = 0.01
ADAM_STEP = 10

OFF_U, OFF_VA, OFF_ZA, OFF_Q, OFF_K, OFF_V, OFF_ZB = 0, 1024, 2048, 3072, 4096, 4352, 4608

SUM_SHIFT, SUM_SCALE, SUM_NORM_G, SUM_GATE, SUM_SHIFT_F, SUM_SCALE_F, SUM_FNG, SUM_SQ_ERR = range(8)

V7X_VMEM_LIMIT_BYTES = 56 * 1024 * 1024

F32 = jnp.float32
BF16 = jnp.bfloat16
MESH = pl.DeviceIdType.MESH
SDS = jax.ShapeDtypeStruct
NT = (((1,), (1,)), ((), ()))
TN = (((0,), (0,)), ((), ()))


def _params(*semantics):
    return pltpu.CompilerParams(dimension_semantics=semantics or None, vmem_limit_bytes=V7X_VMEM_LIMIT_BYTES)


def _mesh_pos():
    return lax.axis_index("x"), lax.axis_index("y"), lax.axis_index("c")


def _sigmoid(z):
    return 1.0 / (1.0 + jnp.exp(-z))


def _adamw(w, g, m, v):
    m = ADAM_B1 * m + (1.0 - ADAM_B1) * g
    v = ADAM_B2 * v + (1.0 - ADAM_B2) * (g * g)
    m_hat = m / (1.0 - ADAM_B1 ** ADAM_STEP)
    v_hat = v / (1.0 - ADAM_B2 ** ADAM_STEP)
    delta = -ADAM_LR * (m_hat / (jnp.sqrt(v_hat) + ADAM_EPS) + ADAM_WD * w)
    return delta, m, v


def _all_gather(name, blocks, memory_space):
    n_arr = len(blocks)

    def body(*refs):
        ins, outs = refs[:n_arr], refs[n_arr:2 * n_arr]
        send_sems, recv_sems, local_sems = refs[2 * n_arr:]
        x, y, c = _mesh_pos()
        me, sibling = (x, y, c), (x, y, 1 - c)
        chips = [(1 - x, y), (x, 1 - y), (1 - x, 1 - y)]

        def slot(p):
            return 4 * p[0] + 2 * p[1] + p[2]

        def copy(a, k, block, to, src=None):
            dst = outs[a].at[slot(block)]
            return pltpu.make_async_remote_copy(
                src_ref=dst if src is None else src, dst_ref=dst,
                send_sem=send_sems.at[a, k], recv_sem=recv_sems.at[a, k],
                device_id=to, device_id_type=MESH)

        mine = [pltpu.make_async_copy(ins[a], outs[a].at[slot(me)], local_sems.at[a]) for a in range(n_arr)]
        for cp in mine:
            cp.start()
        first = []
        for a in range(n_arr):
            first.append(copy(a, 0, me, sibling, src=ins[a]))
            first += [copy(a, 1 + j, me, (*chip, c), src=ins[a]) for j, chip in enumerate(chips)]
        for cp in first:
            cp.start()
        passed = []
        for j, chip in enumerate(chips):
            for a in range(n_arr):
                copy(a, 1 + j, (*chip, c), me).wait_recv()
                fwd = copy(a, 4 + j, (*chip, c), sibling)
                fwd.start()
                passed.append(fwd)
        for a in range(n_arr):
            copy(a, 0, sibling, me).wait_recv()
            for j, chip in enumerate(chips):
                copy(a, 4 + j, (*chip, 1 - c), me).wait_recv()
        for cp in first + passed:
            cp.wait_send()
        for cp in mine:
            cp.wait()

    spec = pl.BlockSpec(memory_space=memory_space)
    return pl.pallas_call(
        body, name=name,
        out_shape=[SDS((N_DEV,) + b.shape, b.dtype) for b in blocks],
        in_specs=[spec] * n_arr, out_specs=[spec] * n_arr,
        scratch_shapes=[pltpu.SemaphoreType.DMA((n_arr, 7)), pltpu.SemaphoreType.DMA((n_arr, 7)),
                        pltpu.SemaphoreType.DMA((n_arr,))],
        compiler_params=_params(),
    )(*blocks)


def _ada_exchange(c, w_ada, b_ada8, w_ada_f, b_ada_f8):
    n1, n2 = w_ada.shape[1], w_ada_f.shape[1]

    def body(c_ref, w1_ref, b1_ref, w2_ref, b2_ref, cact_ref, mod_ref, modf_ref,
             cact_buf, res1, res2, send1, send2, sems_s, sems_r):
        x, y, c_pos = _mesh_pos()
        me = 4 * x + 2 * y + c_pos
        flips = [(k >> 2 & 1, k >> 1 & 1, k & 1) for k in range(1, N_DEV)]

        def peer(f):
            return (1 - x if f[0] else x, 1 - y if f[1] else y, 1 - c_pos if f[2] else c_pos)

        cv = c_ref[...]
        cact = cv * _sigmoid(cv)
        cact_buf[...] = cact
        cact_ref[me] = cact

        def rdma(phase, k, src, dst, f):
            return pltpu.make_async_remote_copy(src_ref=src, dst_ref=dst, send_sem=sems_s.at[phase, k],
                                                recv_sem=sems_r.at[phase, k], device_id=peer(f), device_id_type=MESH)

        gather = [rdma(0, k, cact_buf, cact_ref.at[me], f) for k, f in enumerate(flips)]
        for cp in gather:
            cp.start()
        for cp in gather:
            cp.wait_recv()
        for cp in gather:
            cp.wait_send()

        rid = lax.broadcasted_iota(jnp.int32, (N_DEV, D_MODEL), 0)
        rows = jnp.zeros((N_DEV, D_MODEL), F32)
        for j in range(N_DEV):
            rows = jnp.where(rid == j, jnp.broadcast_to(cact_ref[j], (N_DEV, D_MODEL)), rows)
        rows = rows.astype(BF16)
        res1[...] = jnp.dot(rows, w1_ref[...].astype(BF16), preferred_element_type=F32) + b1_ref[pl.ds(me, 1), :]
        res2[...] = jnp.dot(rows, w2_ref[...].astype(BF16), preferred_element_type=F32) + b2_ref[pl.ds(me, 1), :]
        for j in range(N_DEV):
            send1[j] = res1[pl.ds(j, 1), :]
            send2[j] = res2[pl.ds(j, 1), :]
        mod_ref[me] = send1[me]
        modf_ref[me] = send2[me]
        scatter = []
        for k, f in enumerate(flips):
            to = me ^ (k + 1)
            scatter.append(rdma(1, k, send1.at[to], mod_ref.at[me], f))
            scatter.append(rdma(2, k, send2.at[to], modf_ref.at[me], f))
        for cp in scatter:
            cp.start()
        for cp in scatter:
            cp.wait_recv()
        for cp in scatter:
            cp.wait_send()

    vmem = pl.BlockSpec(memory_space=pltpu.VMEM)
    return pl.pallas_call(
        body, name="ada_exchange",
        out_shape=[SDS((N_DEV, 1, D_MODEL), F32), SDS((N_DEV, 1, n1), F32), SDS((N_DEV, 1, n2), F32)],
        in_specs=[vmem] * 5, out_specs=[vmem] * 3,
        scratch_shapes=[pltpu.VMEM((1, D_MODEL), F32), pltpu.VMEM((N_DEV, n1), F32), pltpu.VMEM((N_DEV, n2), F32),
                        pltpu.VMEM((N_DEV, 1, n1), F32), pltpu.VMEM((N_DEV, 1, n2), F32),
                        pltpu.SemaphoreType.DMA((3, 7)), pltpu.SemaphoreType.DMA((3, 7))],
        compiler_params=_params(),
    )(c, w_ada, b_ada8, w_ada_f, b_ada_f8)


def _chip_scatter(pair_ref, parts_ref, send_sems, recv_sems):
    x, y, c = _mesh_pos()
    chips = [(1 - x, y), (x, 1 - y), (1 - x, 1 - y)]
    return [pltpu.make_async_remote_copy(
        src_ref=pair_ref.at[2 * cx + cy], dst_ref=parts_ref.at[j], send_sem=send_sems.at[j], recv_sem=recv_sems.at[j],
        device_id=(cx, cy, c), device_id_type=MESH) for j, (cx, cy) in enumerate(chips)]


def _scatter_scratch():
    return [pltpu.SemaphoreType.DMA((3,)), pltpu.SemaphoreType.DMA((3,))]


def _prep_weights(me, wt, w_out):
    steps = 4

    def body(me_ref, wt_ref, wo_ref, wtb_ref, wob_ref):
        wtb_ref[...] = wt_ref[...].astype(BF16)
        wob_ref[...] = wo_ref[...].astype(BF16)

    def rows(a, mine):
        blk = (a.shape[0] // steps, a.shape[1])
        return pl.BlockSpec(blk, (lambda i, me_ref: (steps * me_ref[0] + i, 0)) if mine else (lambda i, me_ref: (i, 0)))

    return pl.pallas_call(
        body, name="prep_weights",
        grid_spec=pltpu.PrefetchScalarGridSpec(
            num_scalar_prefetch=1, grid=(steps,),
            in_specs=[rows(wt, False), rows(w_out, False)], out_specs=[rows(wt, True), rows(w_out, True)]),
        out_shape=[SDS((N_DEV * wt.shape[0], D_MODEL), BF16), SDS((N_DEV * w_out.shape[0], D_MODEL), BF16)],
        compiler_params=_params("parallel"),
    )(me, wt, w_out)


class _InPlaceGather:
    def __init__(self, buf_ref, send_sems, recv_sems, relay=False):
        self.buf, self.send_sems, self.recv_sems, self.relay = buf_ref, send_sems, recv_sems, relay
        self.n = buf_ref.shape[0] // N_DEV
        x, y, c = _mesh_pos()
        self.me, self.sibling, self.core = (x, y, c), (x, y, 1 - c), c
        self.chips = [(1 - x, y), (x, 1 - y), (1 - x, 1 - y)]
        self.relay_from = (jnp.where(c == 0, 1 - x, x), jnp.where(c == 0, y, 1 - y), c)
        self.relay_to = (jnp.where(c == 0, x, 1 - x), jnp.where(c == 0, 1 - y, y), c)

    def copy(self, k, block, to):
        start = pl.multiple_of((4 * block[0] + 2 * block[1] + block[2]) * self.n, self.n)
        rows = self.buf.at[pl.ds(start, self.n)]
        return pltpu.make_async_remote_copy(src_ref=rows, dst_ref=rows, send_sem=self.send_sems.at[k],
                                            recv_sem=self.recv_sems.at[k], device_id=to, device_id_type=MESH)

    def start(self):
        self.copy(0, self.me, self.sibling).start()
        for j, chip in enumerate(self.chips[:2] if self.relay else self.chips):
            self.copy(1 + j, self.me, (*chip, self.core)).start()

    def relay_diagonal(self):
        self.copy(3, self.relay_from, self.relay_to).start()

    def pass_on(self, j):
        self.copy(1 + j, (*self.chips[j], self.core), self.me).wait_recv()
        self.copy(4 + j, (*self.chips[j], self.core), self.sibling).start()

    def wait_sibling(self, k):
        self.copy(k, self.sibling, self.me).wait_recv()

    def wait_sends(self):
        for k in range(7):
            self.copy(k, self.me, self.sibling).wait_send()


def _gather_scratch():
    return [pltpu.SemaphoreType.DMA((7,)), pltpu.SemaphoreType.DMA((7,))]


def _gather_in_proj(order, x, shift, scale, norm_g, wt_all):
    s = x.shape[0]
    th, tm = min(512, s), min(1024, s)
    nh, ni = s // th, s // tm
    tn = D_IN // 4
    steps = nh + 4 * ni

    def body(order_ref, x_ref, shift_ref, scale_ref, g_ref, wt_in, h_ref, proj_ref, wt_ref,
             h_scr, w_buf, load_sems, send_sems, recv_sems):
        g = pl.program_id(0)
        gather = _InPlaceGather(wt_ref, send_sems, recv_sems, relay=True)

        def tile_load(slot, chip):
            return pltpu.make_async_copy(wt_ref.at[pl.ds(pl.multiple_of(chip * tn, tn), tn)], w_buf.at[slot],
                                         load_sems.at[slot])

        @pl.when(g == 0)
        def _():
            gather.start()

        @pl.when(g < nh)
        def _():
            xv = x_ref[...]
            r = lax.rsqrt(jnp.mean(xv * xv, axis=-1, keepdims=True) + NORM_EPS)
            hb = (((xv * r) * g_ref[...]) * (1.0 + scale_ref[...]) + shift_ref[...]).astype(BF16)
            h_ref[...] = hb
            h_scr[pl.ds(pl.multiple_of(g * th, th), th), :] = hb

        @pl.when(g == nh - 1)
        def _():
            gather.wait_sibling(0)
            tile_load(0, order_ref[0]).start()

        @pl.when(g >= nh)
        def _():
            t, i = (g - nh) // ni, (g - nh) % ni

            @pl.when(i == 0)
            def _():
                tile_load(t % 2, order_ref[t]).wait()

            @pl.when((i == ni - 1) & (t == 0))
            def _():
                gather.pass_on(0)
                gather.pass_on(1)
                gather.relay_diagonal()

            @pl.when((i == ni // 2) & (t == 2))
            def _():
                gather.pass_on(2)

            for j in range(3):
                @pl.when((i == ni - 1) & (t == j))
                def _():
                    gather.wait_sibling(4 + j)
                    tile_load((j + 1) % 2, order_ref[j + 1]).start()

            lhs = h_scr[pl.ds(pl.multiple_of(i * tm, tm), tm), :]
            proj_ref[...] = lax.dot_general(lhs, w_buf[t % 2], NT, preferred_element_type=F32).astype(BF16)

        @pl.when(g == steps - 1)
        def _():
            gather.wait_sends()

    def h_tile(g, order_ref):
        return (jnp.minimum(g, nh - 1), 0)

    def proj_tile(g, order_ref):
        mm = jnp.maximum(g - nh, 0)
        return (mm % ni, order_ref[mm // ni])

    row = pl.BlockSpec((1, D_MODEL), lambda g, order_ref: (0, 0))
    hbm = pl.BlockSpec(memory_space=pl.ANY)
    return pl.pallas_call(
        body, name="gather_in_proj",
        grid_spec=pltpu.PrefetchScalarGridSpec(
            num_scalar_prefetch=1, grid=(steps,),
            in_specs=[pl.BlockSpec((th, D_MODEL), h_tile), row, row, row, hbm],
            out_specs=[pl.BlockSpec((th, D_MODEL), h_tile), pl.BlockSpec((tm, tn), proj_tile), hbm],
            scratch_shapes=[pltpu.VMEM((s, D_MODEL), BF16), pltpu.VMEM((2, tn, D_MODEL), BF16),
                            pltpu.SemaphoreType.DMA((2,)), *_gather_scratch()]),
        out_shape=[SDS((s, D_MODEL), BF16), SDS((s, D_IN), BF16), SDS(wt_all.shape, BF16)],
        input_output_aliases={5: 2},
        compiler_params=_params("arbitrary"),
    )(order, x, shift, scale, norm_g, wt_all)


def _rope_freqs():
    inv_freq = ROPE_THETA ** (-jnp.arange(0, HEAD_DIM, 2, dtype=F32) / HEAD_DIM)
    return jnp.tile(inv_freq, 4).reshape(1, 128)


class _RopeTables:
    def __init__(self, freq_ref, rows_ref, state_ref, last_ref):
        self.freq, self.rows, self.state, self.last = freq_ref, rows_ref, state_ref, last_ref

    def start(self, block, direction):
        ang = lax.broadcasted_iota(jnp.int32, (CHUNK, 128), 0).astype(F32) * self.freq[...]
        self.rows[0] = jnp.cos(ang)
        self.rows[1] = jnp.sin(ang)
        base = jnp.asarray(block * CHUNK, dtype=F32) * self.freq[...]
        turn = float(direction * CHUNK) * self.freq[...]
        self.state[0:1, :] = jnp.cos(base)
        self.state[1:2, :] = jnp.sin(base)
        self.state[2:3, :] = jnp.cos(turn)
        self.state[3:4, :] = jnp.sin(turn)

    def step(self):
        c, s, ct, st = (self.state[k:k + 1, :] for k in range(4))
        self.state[0:1, :] = c * ct - s * st
        self.state[1:2, :] = s * ct + c * st

    def tables(self):
        c, s = self.state[0:1, :], self.state[1:2, :]
        cos = c * self.rows[0] - s * self.rows[1]
        sin = s * self.rows[0] + c * self.rows[1]
        first_half = (lax.broadcasted_iota(jnp.int32, (1, 128), 1) & (HEAD_DIM - 1)) < HEAD_DIM // 2
        return cos, jnp.where(first_half, -sin, 0.0), jnp.where(first_half, 0.0, sin)

    def keep(self, tabs):
        for k in range(3):
            self.last[k] = tabs[k]

    def kept(self):
        return tuple(self.last[k] for k in range(3))


def _rope_scratch():
    return [pltpu.VMEM((2, CHUNK, 128), F32), pltpu.VMEM((8, 128), F32), pltpu.VMEM((3, CHUNK, 128), F32)]


def _rope(v, cos, sin_lo, sin_hi):
    width = v.shape[1]
    rep = (1, width // 128)
    return (v * jnp.tile(cos, rep) + pltpu.roll(v, width - 32, 1) * jnp.tile(sin_lo, rep)
            + pltpu.roll(v, 32, 1) * jnp.tile(sin_hi, rep))


def _rope_bwd(d, cos, sin_lo, sin_hi):
    width = d.shape[1]
    rep = (1, width // 128)
    return (d * jnp.tile(cos, rep) + pltpu.roll(d * jnp.tile(sin_lo, rep), 32, 1)
            + pltpu.roll(d * jnp.tile(sin_hi, rep), width - 32, 1))


def _layer_norm(v, g, b):
    mu = jnp.mean(v, axis=-1, keepdims=True)
    vc = v - mu
    rstd = lax.rsqrt(jnp.mean(vc * vc, axis=-1, keepdims=True) + NORM_EPS)
    vhat = vc * rstd
    return vhat * g + b, vhat, rstd


def _set_tril(w_ref, out_ref):
    t = lax.broadcasted_iota(jnp.int32, (CHUNK, CHUNK), 0)
    tp = lax.broadcasted_iota(jnp.int32, (CHUNK, CHUNK), 1)
    for g in range(A_GROUPS):
        out_ref[g] = jnp.where(tp <= t, w_ref[g], 0.0).astype(BF16)


def _bias_columns(b_ref, out_ref):
    for g in range(A_GROUPS):
        out_ref[g] = jnp.broadcast_to(b_ref[pl.ds(g, 1), :], (CHUNK, CHUNK)).T


def _from_prev():
    r = lax.broadcasted_iota(jnp.int32, (CHUNK, 4 * CHUNK), 0)
    i = lax.broadcasted_iota(jnp.int32, (CHUNK, 4 * CHUNK), 1) & (CHUNK - 1)
    return r > i


def _set_unfold_masks(mask_ref):
    prev = _from_prev()
    mask_ref[0] = jnp.where(prev, 1.0, 0.0).astype(BF16)
    mask_ref[1] = jnp.where(prev, 0.0, 1.0).astype(BF16)


def _fold_band(t, from_prev):
    return jnp.where(from_prev, t[:CHUNK], t[CHUNK:])


def _unfold_band(t, mask_ref):
    return jnp.concatenate([t * mask_ref[0], t * mask_ref[1]], axis=0)


def _low_lanes():
    return lax.broadcasted_iota(jnp.int32, (1, 128), 1) < HEAD_DIM


def _stack_heads(pair_a, pair_b):
    lo = _low_lanes()
    return jnp.concatenate([jnp.where(lo, pair_a, 0.0), jnp.where(lo, 0.0, pair_a),
                            jnp.where(lo, pair_b, 0.0), jnp.where(lo, 0.0, pair_b)], axis=0).astype(BF16)


def _heads_to_lanes(per_group):
    rows = [t[:, r * CHUNK:(r + 1) * CHUNK] for t in per_group for r in range(4)]
    return jnp.concatenate(rows, axis=0).T


def _dup_kv_head(band, gk):
    pair = band[:, (gk // 2) * 128:(gk // 2 + 1) * 128]
    lo = _low_lanes()
    one = jnp.where(lo if gk % 2 == 0 else jnp.logical_not(lo), pair, 0.0)
    return (one + pltpu.roll(one, HEAD_DIM, 1)).astype(BF16)


def _fold_kv_head(dup_grad, gk):
    both = dup_grad + pltpu.roll(dup_grad, HEAD_DIM, 1)
    lo = _low_lanes()
    return jnp.where(lo if gk % 2 == 0 else jnp.logical_not(lo), both, 0.0)


def _attn_probs(q_st, k_dup, sink_row, from_prev, first_block):
    s = lax.dot_general(k_dup, q_st, NT, preferred_element_type=F32)
    no_prev = jnp.where(first_block, -jnp.inf, 0.0)
    s = jnp.where(from_prev, s[:CHUNK] + no_prev, s[CHUNK:])
    m = jnp.maximum(jnp.max(s, axis=0, keepdims=True), sink_row)
    p = jnp.exp(s - m)
    e_sink = jnp.exp(sink_row - m)
    inv = 1.0 / (jnp.sum(p, axis=0, keepdims=True) + e_sink)
    return p * inv, e_sink * inv


def _sink_row(sinks_ref, gk):
    return jnp.concatenate([jnp.full((1, CHUNK), sinks_ref[4 * gk + r], F32) for r in range(4)], axis=1)


def _mixer_specs(nb, rev):
    def blk(i):
        return nb - 1 - i if rev else i

    def prev(i):
        return jnp.maximum(blk(i) - 1, 0)

    return dict(
        cur=pl.BlockSpec((CHUNK, D_IN), lambda i, *_: (blk(i), 0)),
        prev_kv=pl.BlockSpec((CHUNK, 2 * 256), lambda i, *_: (prev(i), OFF_K // 512)),
        freq=pl.BlockSpec((1, 128), lambda i, *_: (0, 0)),
        vec=pl.BlockSpec((1, D_A), lambda i, *_: (0, 0)),
        wsp=pl.BlockSpec((A_GROUPS, CHUNK, CHUNK), lambda i, *_: (0, 0, 0)),
        bsp=pl.BlockSpec((A_GROUPS, CHUNK), lambda i, *_: (0, 0)),
        smem=pl.BlockSpec(memory_space=pltpu.SMEM),
        blk=blk,
    )


def _mixer_fwd(proj, freqs, ln_g, ln_b, w_sp, b_sp, sinks, wo_all):
    s = proj.shape[0]
    nb = s // CHUNK
    sp = _mixer_specs(nb, rev=False)

    def body(cur_ref, pkv_ref, freq_ref, lg_ref, lb_ref, w_ref, b_ref, sinks_ref, wo_in, y_ref, wo_ref,
             bcol, wtril, mask, rope_rows, rope_state, rope_last, send_sems, recv_sems):
        i = pl.program_id(0)
        gather = _InPlaceGather(wo_ref, send_sems, recv_sems)
        rope = _RopeTables(freq_ref, rope_rows, rope_state, rope_last)

        @pl.when(i == 0)
        def _():
            gather.start()
            _bias_columns(b_ref, bcol)
            _set_tril(w_ref, wtril)
            _set_unfold_masks(mask)
            rope.start(-1, 1)
            rope_last[...] = jnp.zeros_like(rope_last)

        @pl.when(i == (3 * nb) // 4)
        def _():
            for j in range(3):
                gather.pass_on(j)

        vln, _, _ = _layer_norm(cur_ref[:, OFF_VA:OFF_ZA].astype(F32), lg_ref[...], lb_ref[...])
        vln = vln.astype(BF16)
        for g in range(A_GROUPS):
            cols = slice(g * 128, (g + 1) * 128)
            sg = jnp.dot(wtril[g], vln[:, cols], preferred_element_type=F32) + bcol[g]
            u = cur_ref[:, OFF_U + g * 128:OFF_U + (g + 1) * 128].astype(F32)
            z = cur_ref[:, OFF_ZA + g * 128:OFF_ZA + (g + 1) * 128].astype(F32)
            y_ref[:, cols] = (u * sg * (z * _sigmoid(z))).astype(BF16)

        rope.step()
        cur_t, prev_t = rope.tables(), rope.kept()
        rope.keep(cur_t)
        qr = _rope(cur_ref[:, OFF_Q:OFF_K].astype(F32), *cur_t) * ATTN_SCALE
        kr = jnp.concatenate([_rope(pkv_ref[:, 0:256].astype(F32), *prev_t),
                              _rope(cur_ref[:, OFF_K:OFF_V].astype(F32), *cur_t)], axis=0)
        v_t = jnp.concatenate([pkv_ref[:, 256:512], cur_ref[:, OFF_V:OFF_ZB]], axis=0).astype(F32).T.astype(BF16)
        outs = []
        from_prev = _from_prev()
        for gk in range(N_KV_HEADS):
            q_st = _stack_heads(qr[:, (2 * gk) * 128:(2 * gk + 1) * 128], qr[:, (2 * gk + 1) * 128:(2 * gk + 2) * 128])
            probs, _ = _attn_probs(q_st, _dup_kv_head(kr, gk), _sink_row(sinks_ref, gk), from_prev, i == 0)
            outs.append(jnp.dot(v_t[gk * HEAD_DIM:(gk + 1) * HEAD_DIM], _unfold_band(probs.astype(BF16), mask),
                                preferred_element_type=F32))
        zb = cur_ref[:, OFF_ZB:D_IN].astype(F32)
        y_ref[:, D_A:D_MODEL] = (_heads_to_lanes(outs) * (zb * _sigmoid(zb))).astype(BF16)

        @pl.when(i == nb - 1)
        def _():
            gather.wait_sibling(0)
            for j in range(3):
                gather.wait_sibling(4 + j)
            gather.wait_sends()

    hbm = pl.BlockSpec(memory_space=pl.ANY)
    return pl.pallas_call(
        body, name="mixer_fwd", grid=(nb,),
        in_specs=[sp["cur"], sp["prev_kv"], sp["freq"], sp["vec"], sp["vec"], sp["wsp"], sp["bsp"], sp["smem"], hbm],
        out_specs=[pl.BlockSpec((CHUNK, D_MODEL), lambda i: (i, 0)), hbm],
        out_shape=[SDS((s, D_MODEL), BF16), SDS(wo_all.shape, wo_all.dtype)],
        scratch_shapes=[pltpu.VMEM((A_GROUPS, CHUNK, CHUNK), F32), pltpu.VMEM((A_GROUPS, CHUNK, CHUNK), BF16),
                        pltpu.VMEM((2, CHUNK, 4 * CHUNK), BF16), *_rope_scratch(), *_gather_scratch()],
        input_output_aliases={8: 1},
        compiler_params=_params("arbitrary"),
    )(proj, proj, freqs, ln_g, ln_b, w_sp, b_sp, sinks, wo_all)


def _out_proj_loss(y, x, target, wo, gate, shift_f, scale_f, fng):
    s = y.shape[0]
    tm, tr = 256, 128
    nt = s // tm

    def body(y_ref, x_ref, t_ref, wo_ref, gate_ref, sh_ref, sc_ref, g_ref, dx1_ref, do_ref, dy_ref, sums_ref,
             do_last, do_work):
        i = pl.program_id(0)

        @pl.when(i == 0)
        def _():
            sums_ref[...] = jnp.zeros_like(sums_ref)
            do_last[...] = jnp.zeros_like(do_last)

        do_work[...] = do_last[...]
        o = jnp.dot(y_ref[...], wo_ref[...], preferred_element_type=F32)
        gate, g, sh = gate_ref[...], g_ref[...], sh_ref[...]
        one_sc = 1.0 + sc_ref[...]
        cs, inv_d = g * one_sc, 1.0 / D_MODEL

        def rowsum(v):
            return jnp.sum(v, axis=0, keepdims=True)

        sums = [jnp.zeros((1, D_MODEL), F32) for _ in range(4)]
        for c in range(tm // tr):
            rows = slice(c * tr, (c + 1) * tr)
            oc = o[rows]
            x1 = x_ref[rows, :] + gate * oc
            r = lax.rsqrt(jnp.sum(x1 * x1, axis=-1, keepdims=True) * inv_d + NORM_EPS)
            x1n = x1 * r
            diff = x1n * cs + sh - t_ref[rows, :]
            w = diff * x1n
            lane_sum = jnp.sum(w * cs, axis=-1, keepdims=True)
            dx1 = (diff * cs) * (r * inv_d) - x1n * (r * lane_sum * (inv_d * inv_d))
            dx1_ref[rows, :] = dx1
            do = (dx1 * gate).astype(BF16)
            do_ref[rows, :] = do
            do_last[rows, :] = do
            for k, v in enumerate((dx1 * oc, diff, w, diff * diff)):
                sums[k] = sums[k] + rowsum(v)
        live = jnp.where(i < nt, 1.0, 0.0)
        for row, v in ((SUM_GATE, sums[0]), (SUM_SHIFT_F, inv_d * sums[1]), (SUM_SCALE_F, inv_d * (sums[2] * g)),
                       (SUM_FNG, inv_d * (sums[2] * one_sc)), (SUM_SQ_ERR, sums[3])):
            sums_ref[row:row + 1, :] += live * v
        dy_ref[...] = lax.dot_general(do_work[...], wo_ref[...], NT, preferred_element_type=F32).astype(BF16)

    tile = pl.BlockSpec((tm, D_MODEL), lambda i: (jnp.minimum(i, nt - 1), 0))
    row = pl.BlockSpec((1, D_MODEL), lambda i: (0, 0))
    return pl.pallas_call(
        body, name="out_proj_loss", grid=(nt + 1,),
        in_specs=[tile, tile, tile, pl.BlockSpec((D_MODEL, D_MODEL), lambda i: (0, 0)), row, row, row, row],
        out_specs=[tile, tile, pl.BlockSpec((tm, D_MODEL), lambda i: (jnp.maximum(i - 1, 0), 0)),
                   pl.BlockSpec((8, D_MODEL), lambda i: (0, 0))],
        out_shape=[SDS((s, D_MODEL), F32), SDS((s, D_MODEL), BF16), SDS((s, D_MODEL), BF16), SDS((8, D_MODEL), F32)],
        scratch_shapes=[pltpu.VMEM((tm, D_MODEL), BF16), pltpu.VMEM((tm, D_MODEL), BF16)],
        compiler_params=_params("arbitrary"),
    )(y, x, target, wo, gate, shift_f, scale_f, fng)


ROW_DBSP, ROW_DSINKS, MISC_ROWS = 0, 8, 32


def _mixer_bwd(me, proj, dy, freqs, ln_g, ln_b, w_sp, b_sp, sinks, pair):
    s = proj.shape[0]
    nb = s // CHUNK
    sp = _mixer_specs(nb, rev=True)

    def body(me_ref, cur_ref, pkv_ref, dy_ref, freq_ref, lg_ref, lb_ref, w_ref, b_ref, sinks_ref, pair_ref,
             dproj_ref, dln_ref, dw_ref, misc_ref, parts_ref, bcol, wtril, dbcol, carry, mask, rope_rows, rope_state,
             rope_last, send_sems, recv_sems):
        i = pl.program_id(0)
        block = nb - 1 - i
        rope = _RopeTables(freq_ref, rope_rows, rope_state, rope_last)

        @pl.when(i == 0)
        def _():
            for cp in _chip_scatter(pair_ref, parts_ref, send_sems, recv_sems):
                cp.start()
            _bias_columns(b_ref, bcol)
            _set_tril(w_ref, wtril)
            _set_unfold_masks(mask)
            rope.start(nb - 1, -1)
            rope.keep(rope.tables())
            dbcol[...] = jnp.zeros_like(dbcol)
            carry[...] = jnp.zeros_like(carry)
            dln_ref[...] = jnp.zeros_like(dln_ref)
            dw_ref[...] = jnp.zeros_like(dw_ref)
            misc_ref[...] = jnp.zeros_like(misc_ref)

        vln, vhat, rstd = _layer_norm(cur_ref[:, OFF_VA:OFF_ZA].astype(F32), lg_ref[...], lb_ref[...])
        vln = vln.astype(BF16)
        d_vln = []
        for g in range(A_GROUPS):
            cols = slice(g * 128, (g + 1) * 128)
            w_g = wtril[g]
            sg = jnp.dot(w_g, vln[:, cols], preferred_element_type=F32) + bcol[g]
            u = cur_ref[:, OFF_U + g * 128:OFF_U + (g + 1) * 128].astype(F32)
            z = cur_ref[:, OFF_ZA + g * 128:OFF_ZA + (g + 1) * 128].astype(F32)
            dya = dy_ref[:, cols].astype(F32)
            sig = _sigmoid(z)
            d_ya = dya * (z * sig)
            dproj_ref[:, OFF_ZA + g * 128:OFF_ZA + (g + 1) * 128] = (
                dya * (u * sg) * (sig * (1.0 + z * (1.0 - sig)))).astype(BF16)
            dproj_ref[:, OFF_U + g * 128:OFF_U + (g + 1) * 128] = (d_ya * sg).astype(BF16)
            d_s = d_ya * u
            dbcol[g] += d_s
            d_sb = d_s.astype(BF16)
            dw_ref[g] += lax.dot_general(d_sb, vln[:, cols], NT, preferred_element_type=F32)
            d_vln.append(lax.dot_general(w_g, d_sb, TN, preferred_element_type=F32))
        d_vln = jnp.concatenate(d_vln, axis=1)
        dln_ref[0:1, :] += jnp.sum(d_vln * vhat, axis=0, keepdims=True)
        dln_ref[1:2, :] += jnp.sum(d_vln, axis=0, keepdims=True)
        d_vhat = d_vln * lg_ref[...]
        d_va = rstd * (d_vhat - jnp.mean(d_vhat, axis=-1, keepdims=True)
                       - vhat * jnp.mean(d_vhat * vhat, axis=-1, keepdims=True))
        dproj_ref[:, OFF_VA:OFF_ZA] = d_va.astype(BF16)

        cur_t = rope.kept()
        rope.step()
        prev_t = rope.tables()
        rope.keep(prev_t)
        band_t = tuple(jnp.concatenate([p, c], axis=0) for p, c in zip(prev_t, cur_t))
        qr = _rope(cur_ref[:, OFF_Q:OFF_K].astype(F32), *cur_t) * ATTN_SCALE
        kr = jnp.concatenate([_rope(pkv_ref[:, 0:256].astype(F32), *prev_t),
                              _rope(cur_ref[:, OFF_K:OFF_V].astype(F32), *cur_t)], axis=0)
        vb = jnp.concatenate([pkv_ref[:, 256:512], cur_ref[:, OFF_V:OFF_ZB]], axis=0).astype(F32)
        k_t, v_t = (kr.T * ATTN_SCALE).astype(BF16), vb.T.astype(BF16)
        zb = cur_ref[:, OFF_ZB:D_IN].astype(F32)
        dyb = dy_ref[:, D_A:D_MODEL].astype(F32)
        sig = _sigmoid(zb)
        d_yb = dyb * (zb * sig)
        outs, dqs = [], []
        dk_pairs = [jnp.zeros((2 * CHUNK, 128), F32) for _ in range(2)]
        dv_pairs = [jnp.zeros((2 * CHUNK, 128), F32) for _ in range(2)]
        from_prev = _from_prev()
        for gk in range(N_KV_HEADS):
            heads = slice(gk * HEAD_DIM, (gk + 1) * HEAD_DIM)
            q_st = _stack_heads(qr[:, (2 * gk) * 128:(2 * gk + 1) * 128], qr[:, (2 * gk + 1) * 128:(2 * gk + 2) * 128])
            k_dup, v_dup = _dup_kv_head(kr, gk), _dup_kv_head(vb, gk)
            probs, p_sink = _attn_probs(q_st, k_dup, _sink_row(sinks_ref, gk), from_prev, block == 0)
            probs_b = _unfold_band(probs.astype(BF16), mask)
            outs.append(jnp.dot(v_t[heads], probs_b, preferred_element_type=F32))
            do_st = _stack_heads(d_yb[:, (2 * gk) * 128:(2 * gk + 1) * 128], d_yb[:, (2 * gk + 1) * 128:(2 * gk + 2) * 128])
            dp = _fold_band(lax.dot_general(v_dup, do_st, NT, preferred_element_type=F32), from_prev)
            delta = jnp.sum(probs * dp, axis=0, keepdims=True)
            ds = _unfold_band((probs * (dp - delta)).astype(BF16), mask)
            d_sink = -p_sink * delta
            for r in range(4):
                row = ROW_DSINKS + 4 * gk + r
                misc_ref[row:row + 1, :] += jnp.broadcast_to(
                    jnp.sum(d_sink[:, r * CHUNK:(r + 1) * CHUNK], axis=1, keepdims=True), (1, 128))
            dqs.append(jnp.dot(k_t[heads], ds, preferred_element_type=F32))
            dk_pairs[gk // 2] += _fold_kv_head(jnp.dot(ds, q_st, preferred_element_type=F32), gk)
            dv_pairs[gk // 2] += _fold_kv_head(jnp.dot(probs_b, do_st, preferred_element_type=F32), gk)
        dproj_ref[:, OFF_ZB:D_IN] = (dyb * _heads_to_lanes(outs) * (sig * (1.0 + zb * (1.0 - sig)))).astype(BF16)
        dproj_ref[:, OFF_Q:OFF_K] = _rope_bwd(_heads_to_lanes(dqs), *cur_t).astype(BF16)
        dk_band = _rope_bwd(jnp.concatenate(dk_pairs, axis=1), *band_t)
        dv_band = jnp.concatenate(dv_pairs, axis=1)
        dproj_ref[:, OFF_K:OFF_V] = (dk_band[CHUNK:] + carry[:, 0:256]).astype(BF16)
        dproj_ref[:, OFF_V:OFF_ZB] = (dv_band[CHUNK:] + carry[:, 256:512]).astype(BF16)
        carry[:, 0:256] = dk_band[:CHUNK]
        carry[:, 256:512] = dv_band[:CHUNK]

        @pl.when(i == nb - 1)
        def _():
            t = lax.broadcasted_iota(jnp.int32, (CHUNK, CHUNK), 0)
            tp = lax.broadcasted_iota(jnp.int32, (CHUNK, CHUNK), 1)
            for g in range(A_GROUPS):
                dw_ref[g] = jnp.where(tp <= t, dw_ref[g], 0.0)
                misc_ref[pl.ds(ROW_DBSP + g, 1), :] = jnp.sum(dbcol[g].T, axis=0, keepdims=True)
            scatter = _chip_scatter(pair_ref, parts_ref, send_sems, recv_sems)
            for cp in scatter:
                cp.wait_recv()
            for cp in scatter:
                cp.wait_send()

    blk = sp["blk"]
    hbm = pl.BlockSpec(memory_space=pl.ANY)
    return pl.pallas_call(
        body, name="mixer_bwd",
        grid_spec=pltpu.PrefetchScalarGridSpec(
            num_scalar_prefetch=1, grid=(nb,),
            in_specs=[sp["cur"], sp["prev_kv"], pl.BlockSpec((CHUNK, D_MODEL), lambda i, me_ref: (blk(i), 0)),
                      sp["freq"], sp["vec"], sp["vec"], sp["wsp"], sp["bsp"], sp["smem"], hbm],
            out_specs=[pl.BlockSpec((CHUNK, D_IN), lambda i, me_ref: (blk(i), 0)),
                       pl.BlockSpec((8, D_A), lambda i, me_ref: (me_ref[0], 0)),
                       pl.BlockSpec((A_GROUPS, CHUNK, CHUNK), lambda i, me_ref: (me_ref[0], 0, 0)),
                       pl.BlockSpec((MISC_ROWS, 128), lambda i, me_ref: (me_ref[0], 0)), hbm],
            scratch_shapes=[pltpu.VMEM((A_GROUPS, CHUNK, CHUNK), F32), pltpu.VMEM((A_GROUPS, CHUNK, CHUNK), BF16),
                            pltpu.VMEM((A_GROUPS, CHUNK, CHUNK), F32), pltpu.VMEM((CHUNK, 512), F32), pltpu.VMEM((2, CHUNK, 4 * CHUNK), BF16),
                            *_rope_scratch(), *_scatter_scratch()]),
        out_shape=[SDS((s, D_IN), BF16), SDS((N_DEV * 8, D_A), F32), SDS((N_DEV * A_GROUPS, CHUNK, CHUNK), F32),
                   SDS((N_DEV * MISC_ROWS, 128), F32), SDS((3,) + pair.shape[1:], pair.dtype)],
        compiler_params=_params("arbitrary"),
    )(me, proj, proj, dy, freqs, ln_g, ln_b, w_sp, b_sp, sinks, pair)


def _wgrad_pair(name, a, b, gathers=()):
    s, m = a.shape
    n = b.shape[1]
    bm, half = m // 4, m // 8
    bt = min(1024, s)
    steps = s // bt
    last = 4 * steps
    n_g = len(gathers)

    def body(*refs):
        a_ref, b_ref = refs[:2]
        out_ref, bufs = refs[2 + n_g], refs[3 + n_g:3 + 2 * n_g]
        acc, kept, got, sent, send_sems, recv_sems = refs[3 + 2 * n_g:9 + 2 * n_g]
        sems = refs[9 + 2 * n_g:]
        g = pl.program_id(0)
        tile, t = g // steps, g % steps
        mx, my, mc = _mesh_pos()
        jobs = [_InPlaceGather(bufs[k], sems[2 * k], sems[2 * k + 1]) for k in range(n_g)]

        def exchange(q):
            return pltpu.make_async_remote_copy(src_ref=sent, dst_ref=got.at[q % 2], send_sem=send_sems.at[q],
                                                recv_sem=recv_sems.at[q], device_id=(mx, my, 1 - mc),
                                                device_id_type=MESH)

        @pl.when(g == 0)
        def _():
            for job in jobs:
                job.start()

        @pl.when(g == 2 * steps)
        def _():
            for job in jobs:
                for j in range(3):
                    job.pass_on(j)

        @pl.when(g < last)
        def _():
            prod = lax.dot_general(a_ref[...], b_ref[...], TN, preferred_element_type=F32)

            @pl.when(t == 0)
            def _():
                acc[...] = prod

            @pl.when(t > 0)
            def _():
                acc[...] += prod

            @pl.when(t == steps - 1)
            def _():
                @pl.when(tile > 0)
                def _():
                    exchange(tile - 1).wait_send()

                kept[tile % 2] = acc[pl.ds(pl.multiple_of(mc * half, 8), half), :].astype(BF16)
                sent[...] = acc[pl.ds(pl.multiple_of((1 - mc) * half, 8), half), :].astype(BF16)
                exchange(tile).start()

        @pl.when((t == 0) & (g > 0))
        def _():
            q = tile - 1
            exchange(q).wait_recv()
            out_ref[0] = (kept[q % 2].astype(F32) + got[q % 2].astype(F32)).astype(BF16)

        @pl.when(g == last)
        def _():
            exchange(3).wait_send()
            for job in jobs:
                job.wait_sibling(0)
                for j in range(3):
                    job.wait_sibling(4 + j)
                job.wait_sends()

    def a_tile(g):
        gg = jnp.minimum(g, last - 1)
        return (gg % steps, gg // steps)

    def b_tile(g):
        return (jnp.minimum(g, last - 1) % steps, 0)

    hbm = pl.BlockSpec(memory_space=pl.ANY)
    outs = pl.pallas_call(
        body, name=name, grid=(last + 1,),
        in_specs=[pl.BlockSpec((bt, bm), a_tile), pl.BlockSpec((bt, n), b_tile)] + [hbm] * n_g,
        out_specs=[pl.BlockSpec((1, half, n), lambda g: (jnp.maximum(g - 1, 0) // steps, 0, 0))] + [hbm] * n_g,
        out_shape=[SDS((4, half, n), BF16)] + [SDS(gb.shape, gb.dtype) for gb in gathers],
        scratch_shapes=[pltpu.VMEM((bm, n), F32), pltpu.VMEM((2, half, n), BF16), pltpu.VMEM((2, half, n), BF16),
                        pltpu.VMEM((half, n), BF16), pltpu.SemaphoreType.DMA((4,)), pltpu.SemaphoreType.DMA((4,))]
        + _gather_scratch() * n_g,
        input_output_aliases={2 + k: 1 + k for k in range(n_g)},
        compiler_params=_params("arbitrary"),
    )(a, b, *gathers)
    return outs[0], outs[1:]


def _in_proj_bwd(dproj, wt, x, dx1, scale, norm_g, sums_o, pair):
    s = x.shape[0]
    tm, tk, tr = min(1024, s), D_IN // 4, 64
    ksteps = D_IN // tk

    def body(dp_ref, wt_ref, x_hbm, dx1_hbm, sc_ref, g_ref, so_ref, pair_ref, gx_ref, sums_ref, parts_ref, x_buf,
             dx1_buf, tile_sems, send_sems, recv_sems):
        i, k = pl.program_id(0), pl.program_id(1)

        def tile_copies():
            rows = pl.ds(pl.multiple_of(i * tm, tm), tm)
            return (pltpu.make_async_copy(x_hbm.at[rows], x_buf, tile_sems.at[0]),
                    pltpu.make_async_copy(dx1_hbm.at[rows], dx1_buf, tile_sems.at[1]))

        @pl.when((i == 0) & (k == 0))
        def _():
            for cp in _chip_scatter(pair_ref, parts_ref, send_sems, recv_sems):
                cp.start()
            sums_ref[...] = so_ref[...]

        @pl.when(k == 0)
        def _():
            for cp in tile_copies():
                cp.start()
            gx_ref[...] = jnp.dot(dp_ref[...], wt_ref[...], preferred_element_type=F32)

        @pl.when(k > 0)
        def _():
            gx_ref[...] += jnp.dot(dp_ref[...], wt_ref[...], preferred_element_type=F32)

        @pl.when(k == ksteps - 1)
        def _():
            for cp in tile_copies():
                cp.wait()
            one_sc, g = 1.0 + sc_ref[...], g_ref[...]
            cs = one_sc * g

            def chunk(j, sums):
                rows = pl.ds(pl.multiple_of(j * tr, tr), tr)
                dh, xv = gx_ref[rows, :], x_buf[rows, :]
                dhx = dh * xv
                r = lax.rsqrt(jnp.sum(xv * xv, axis=-1, keepdims=True) * (1.0 / D_MODEL) + NORM_EPS)
                coef = (r * r * r) * (jnp.sum(dhx * cs, axis=-1, keepdims=True) * (1.0 / D_MODEL))
                gx_ref[rows, :] = dx1_buf[rows, :] + r * (dh * cs) - xv * coef
                return (sums[0] + jnp.sum(dh, axis=0, keepdims=True), sums[1] + jnp.sum(dhx * r, axis=0, keepdims=True))

            zero = jnp.zeros((1, D_MODEL), F32)
            sums = lax.fori_loop(0, tm // tr, chunk, (zero, zero))
            sums_ref[SUM_SHIFT:SUM_SHIFT + 1, :] += sums[0]
            sums_ref[SUM_SCALE:SUM_SCALE + 1, :] += sums[1] * g
            sums_ref[SUM_NORM_G:SUM_NORM_G + 1, :] += sums[1] * one_sc

        @pl.when((i == s // tm - 1) & (k == ksteps - 1))
        def _():
            scatter = _chip_scatter(pair_ref, parts_ref, send_sems, recv_sems)
            for cp in scatter:
                cp.wait_recv()
            for cp in scatter:
                cp.wait_send()

    row = pl.BlockSpec((1, D_MODEL), lambda i, k: (0, 0))
    hbm = pl.BlockSpec(memory_space=pl.ANY)
    return pl.pallas_call(
        body, name="in_proj_bwd", grid=(s // tm, ksteps),
        in_specs=[pl.BlockSpec((tm, tk), lambda i, k: (i, k)), pl.BlockSpec((tk, D_MODEL), lambda i, k: (k, 0)),
                  hbm, hbm, row, row, pl.BlockSpec((8, D_MODEL), lambda i, k: (0, 0)), hbm],
        out_specs=[pl.BlockSpec((tm, D_MODEL), lambda i, k: (i, 0)), pl.BlockSpec((8, D_MODEL), lambda i, k: (0, 0)),
                   hbm],
        out_shape=[SDS((s, D_MODEL), F32), SDS((8, D_MODEL), F32), SDS((3,) + pair.shape[1:], pair.dtype)],
        scratch_shapes=[pltpu.VMEM((tm, D_MODEL), F32), pltpu.VMEM((tm, D_MODEL), F32),
                        pltpu.SemaphoreType.DMA((2,)), *_scatter_scratch()],
        compiler_params=_params("arbitrary", "arbitrary"),
    )(dproj, wt, x, dx1, scale, norm_g, sums_o, pair)


def _sum_chips(own_ref, parts_ref):
    return ((own_ref[0].astype(F32) + parts_ref[0].astype(F32)) + parts_ref[1].astype(F32)) + parts_ref[2].astype(F32)


def _adam_rows(name, chip, pair, parts, w, m, v, tr):
    rows = w.shape[0]

    def body(chip_ref, own_ref, p_ref, w_ref, m_ref, v_ref, g_ref, d_ref, nm_ref, nv_ref):
        g = _sum_chips(own_ref, p_ref)
        g_ref[...] = g
        d_ref[...], nm_ref[...], nv_ref[...] = _adamw(w_ref[...], g, m_ref[...], v_ref[...])

    blk = pl.BlockSpec((tr, D_MODEL), lambda j, chip_ref: (j, 0))
    return pl.pallas_call(
        body, name=name,
        grid_spec=pltpu.PrefetchScalarGridSpec(
            num_scalar_prefetch=1, grid=(rows // tr,),
            in_specs=[pl.BlockSpec((1, tr, D_MODEL), lambda j, chip_ref: (chip_ref[0], j, 0)),
                      pl.BlockSpec((3, tr, D_MODEL), lambda j, chip_ref: (0, j, 0)), blk, blk, blk],
            out_specs=[blk] * 4),
        out_shape=[SDS(w.shape, F32)] * 4, compiler_params=_params("parallel"),
    )(chip, pair, parts, w, m, v)


def _adam_ada(name, cact, dmod, w, m, v):
    n = w.shape[1]
    tr = 512

    def body(c_ref, dm_ref, w_ref, m_ref, v_ref, g_ref, d_ref, nm_ref, nv_ref):
        pad_c = jnp.concatenate([c_ref[...], jnp.zeros_like(c_ref)], axis=0).astype(BF16)
        pad_d = jnp.concatenate([dm_ref[...], jnp.zeros_like(dm_ref)], axis=0).astype(BF16)
        g = lax.dot_general(pad_c, pad_d, TN, preferred_element_type=F32)
        g_ref[...] = g
        d_ref[...], nm_ref[...], nv_ref[...] = _adamw(w_ref[...], g, m_ref[...], v_ref[...])

    blk = pl.BlockSpec((tr, n), lambda j: (j, 0))
    return pl.pallas_call(
        body, name=name, grid=(D_MODEL // tr,),
        in_specs=[pl.BlockSpec((N_DEV, tr), lambda j: (0, j)), pl.BlockSpec((N_DEV, n), lambda j: (0, 0)),
                  blk, blk, blk],
        out_specs=[blk] * 4, out_shape=[SDS(w.shape, F32)] * 4,
        compiler_params=_params("parallel"),
    )(cact, dmod, w, m, v)


SMALL_PARAMS = ("w_spatial", "b_spatial", "sinks", "norm_g", "ln_v_g", "ln_v_b", "final_norm_g", "b_ada", "b_ada_final")


def _adam_small(d_wsp, misc, d_ln, sums, params):
    n_p = len(SMALL_PARAMS)

    def body(*refs):
        wsp_ref, misc_ref, ln_ref, sums_ref = refs[:4]
        wmv = [refs[4 + 3 * k:7 + 3 * k] for k in range(n_p)]
        loss_ref = refs[4 + 3 * n_p]
        outs = [refs[5 + 3 * n_p + 4 * k:9 + 3 * n_p + 4 * k] for k in range(n_p)]

        def column_sum(row):
            return total(sums_ref, (row, row + 1))

        def total(ref, rows=None):
            def part(j):
                return ref[j] if rows is None else ref[j, rows[0]:rows[1], :]
            acc = part(0)
            for j in range(1, N_DEV):
                acc = acc + part(j)
            return acc

        sink_rows = total(misc_ref, (ROW_DSINKS, ROW_DSINKS + 16))
        diag = (lax.broadcasted_iota(jnp.int32, (16, 128), 0) == lax.broadcasted_iota(jnp.int32, (16, 128), 1))
        grads = dict(
            w_spatial=total(wsp_ref), b_spatial=total(misc_ref, (ROW_DBSP, ROW_DBSP + A_GROUPS)),
            sinks=jnp.sum(jnp.where(diag, sink_rows, 0.0), axis=0, keepdims=True),
            norm_g=column_sum(SUM_NORM_G), ln_v_g=total(ln_ref, (0, 1)), ln_v_b=total(ln_ref, (1, 2)),
            final_norm_g=column_sum(SUM_FNG),
            b_ada=jnp.concatenate([column_sum(SUM_SHIFT), column_sum(SUM_SCALE), column_sum(SUM_GATE)], axis=1),
            b_ada_final=jnp.concatenate([column_sum(SUM_SHIFT_F), column_sum(SUM_SCALE_F)], axis=1))
        sq_err = jnp.sum(column_sum(SUM_SQ_ERR), axis=1, keepdims=True)
        loss_ref[...] = jnp.broadcast_to(sq_err * (0.5 / D_MODEL), (1, 128))
        for k, name in enumerate(SMALL_PARAMS):
            w_ref, m_ref, v_ref = wmv[k]
            g_ref, d_ref, nm_ref, nv_ref = outs[k]
            g_ref[...] = grads[name]
            d_ref[...], nm_ref[...], nv_ref[...] = _adamw(w_ref[...], grads[name], m_ref[...], v_ref[...])

    flat = [a for name in SMALL_PARAMS for a in params[name]]
    vmem = pl.BlockSpec(memory_space=pltpu.VMEM)
    out_shape = [SDS((1, 128), F32)] + [SDS(params[name][0].shape, F32) for name in SMALL_PARAMS for _ in range(4)]
    outs = pl.pallas_call(
        body, name="adam_small", in_specs=[vmem] * (4 + len(flat)), out_specs=[vmem] * len(out_shape),
        out_shape=out_shape, compiler_params=_params(),
    )(d_wsp, misc, d_ln, sums, *flat)
    return outs[0], {name: outs[1 + 4 * k:5 + 4 * k] for k, name in enumerate(SMALL_PARAMS)}


def kernel(x, c, w_ada, b_ada, norm_g, w_in, ln_v_g, ln_v_b, w_spatial, b_spatial, sinks, w_out, w_ada_final, b_ada_final, final_norm_g, loss_target, m_w_ada, m_b_ada, m_norm_g, m_w_in, m_ln_v_g, m_ln_v_b, m_w_spatial, m_b_spatial, m_sinks, m_w_out, m_w_ada_final, m_b_ada_final, m_final_norm_g, v_w_ada, v_b_ada, v_norm_g, v_w_in, v_ln_v_g, v_ln_v_b, v_w_spatial, v_b_spatial, v_sinks, v_w_out, v_w_ada_final, v_b_ada_final, v_final_norm_g):
    me = 4 * lax.axis_index("x") + 2 * lax.axis_index("y") + lax.axis_index("c")
    x2, tgt = x[0], loss_target[0]
    fng = final_norm_g.reshape(1, D_MODEL)

    n_ada, n_ada_f = w_ada.shape[2], w_ada_final.shape[1]
    cact, mod, mod_f = _ada_exchange(c, w_ada[0], b_ada.reshape(N_DEV, n_ada), w_ada_final,
                                     b_ada_final.reshape(N_DEV, n_ada_f))
    cact = cact.reshape(N_DEV, D_MODEL)
    mod, mod_f = mod.reshape(1, 3 * D_MODEL), mod_f.reshape(1, 2 * D_MODEL)
    shift, scale, gate = mod[:, :D_MODEL], mod[:, D_MODEL:2 * D_MODEL], mod[:, 2 * D_MODEL:]
    shift_f, scale_f = mod_f[:, :D_MODEL], mod_f[:, D_MODEL:]

    wt_f32, m_wt, v_wt = (jnp.swapaxes(a, 1, 2)[0] for a in (w_in, m_w_in, v_w_in))
    xi, yi = lax.axis_index("x"), lax.axis_index("y")
    chip_order = jnp.stack([2 * xi + yi, 2 * (1 - xi) + yi, 2 * xi + 1 - yi, 2 * (1 - xi) + 1 - yi]).astype(jnp.int32)
    wt_mine, wo_mine = _prep_weights(me.reshape(1), wt_f32, w_out[0])

    freqs = _rope_freqs()
    sinks_v = sinks.reshape(16)
    h, proj, wt = _gather_in_proj(chip_order, x2, shift, scale, norm_g, wt_mine)
    y, wo = _mixer_fwd(proj, freqs, ln_v_g, ln_v_b, w_spatial[0], b_spatial[0], sinks_v, wo_mine)
    dx1, do, dy, sums_o = _out_proj_loss(y, x2, tgt, wo, gate, shift_f, scale_f, fng)

    chip = (2 * lax.axis_index("x") + lax.axis_index("y")).reshape(1)
    pair_out, _ = _wgrad_pair("wgrad_out", y, do)
    dproj, d_ln, d_wsp, misc, parts_out = _mixer_bwd(
        me.reshape(1), proj, dy, freqs, ln_v_g, ln_v_b, w_spatial[0], b_spatial[0], sinks_v, pair_out)
    pair_in, (d_ln, d_wsp, misc) = _wgrad_pair(
        "wgrad_in", dproj, h, gathers=(d_ln, d_wsp.reshape(N_DEV * A_GROUPS * CHUNK, CHUNK), misc))
    grad_x, sums, parts_in = _in_proj_bwd(dproj, wt, x2, dx1, scale, norm_g, sums_o, pair_in)
    wt_leaves = [jnp.swapaxes(a[None], 1, 2)
                 for a in _adam_rows("adam_w_in", chip, pair_in, parts_in, wt_f32, m_wt, v_wt, 176)]
    w_out_leaves = [a[None] for a in _adam_rows("adam_w_out", chip, pair_out, parts_out, w_out[0], m_w_out[0], v_w_out[0], 64)]

    (sums,) = _all_gather("gather_sums", [sums], pltpu.VMEM)
    natural = dict(w_spatial=(A_GROUPS * CHUNK, CHUNK), b_spatial=(A_GROUPS, CHUNK), sinks=(1, 16), norm_g=(1, D_MODEL),
                   ln_v_g=(1, D_A), ln_v_b=(1, D_A), final_norm_g=(1, D_MODEL), b_ada=(1, 3 * D_MODEL),
                   b_ada_final=(1, 2 * D_MODEL))
    given = dict(
        w_spatial=(w_spatial, m_w_spatial, v_w_spatial), b_spatial=(b_spatial, m_b_spatial, v_b_spatial),
        sinks=(sinks, m_sinks, v_sinks), norm_g=(norm_g, m_norm_g, v_norm_g), ln_v_g=(ln_v_g, m_ln_v_g, v_ln_v_g),
        ln_v_b=(ln_v_b, m_ln_v_b, v_ln_v_b), final_norm_g=(final_norm_g, m_final_norm_g, v_final_norm_g),
        b_ada=(b_ada, m_b_ada, v_b_ada), b_ada_final=(b_ada_final, m_b_ada_final, v_b_ada_final))
    params = {name: tuple(a.reshape(natural[name]) for a in given[name]) for name in SMALL_PARAMS}
    params["sinks"] = tuple(jnp.pad(a, ((0, 0), (0, 128 - 16))) for a in params["sinks"])
    loss, small = _adam_small(d_wsp.reshape(N_DEV, A_GROUPS * CHUNK, CHUNK), misc.reshape(N_DEV, MISC_ROWS, 128),
                              d_ln.reshape(N_DEV, 8, D_A), sums, params)
    small["sinks"] = [a[:, :16] for a in small["sinks"]]
    small = {name: [a.reshape(given[name][0].shape) for a in small[name]] for name in SMALL_PARAMS}

    dmod_all = jnp.concatenate([sums[:, SUM_SHIFT], sums[:, SUM_SCALE], sums[:, SUM_GATE]], axis=1)
    dmod_f_all = jnp.concatenate([sums[:, SUM_SHIFT_F], sums[:, SUM_SCALE_F]], axis=1)
    dmod_mine = lax.dynamic_slice_in_dim(dmod_all, me * n_ada, n_ada, axis=1)
    dmod_f_mine = lax.dynamic_slice_in_dim(dmod_f_all, me * n_ada_f, n_ada_f, axis=1)
    ada = _adam_ada("adam_w_ada", cact, dmod_mine, w_ada[0], m_w_ada[0], v_w_ada[0])
    ada_f = _adam_ada("adam_w_ada_final", cact, dmod_f_mine, w_ada_final, m_w_ada_final, v_w_ada_final)

    def leaves(k):
        return (ada[k][None], small["b_ada"][k], small["norm_g"][k], wt_leaves[k], small["ln_v_g"][k],
                small["ln_v_b"][k], small["w_spatial"][k], small["b_spatial"][k], small["sinks"][k], w_out_leaves[k],
                ada_f[k], small["b_ada_final"][k], small["final_norm_g"][k])

    return (loss[0, 0], grad_x[None], *leaves(0), *leaves(1), *leaves(2), *leaves(3))
```

```python
import jax
import jax.numpy as jnp
from jax import lax
from jax.experimental import pallas as pl
from jax.experimental.pallas import tpu as pltpu

D_MODEL = 2048
D_IN = 5632
D_A = 1024
CHUNK = 128
A_GROUPS = 8
HEAD_DIM = 64
N_KV_HEADS = 4
N_DEV = 8
ROPE_THETA = 10000.0
NORM_EPS = 1e-5
ATTN_SCALE = HEAD_DIM ** -0.5

ADAM_LR = 0.001
ADAM_B1 = 0.9
ADAM_B2 = 0.999
ADAM_EPS = 1e-08
ADAM_WD = 0.01
ADAM_STEP = 10

OFF_U, OFF_VA, OFF_ZA, OFF_Q, OFF_K, OFF_V, OFF_ZB = 0, 1024, 2048, 3072, 4096, 4352, 4608

SUM_SHIFT, SUM_SCALE, SUM_NORM_G, SUM_GATE, SUM_SHIFT_F, SUM_SCALE_F, SUM_FNG, SUM_SQ_ERR = range(8)

V7X_VMEM_LIMIT_BYTES = 56 * 1024 * 1024

F32 = jnp.float32
BF16 = jnp.bfloat16
MESH = pl.DeviceIdType.MESH
SDS = jax.ShapeDtypeStruct
NT = (((1,), (1,)), ((), ()))
TN = (((0,), (0,)), ((), ()))


def _params(*semantics):
    return pltpu.CompilerParams(dimension_semantics=semantics or None, vmem_limit_bytes=V7X_VMEM_LIMIT_BYTES)


def _mesh_pos():
    return lax.axis_index("x"), lax.axis_index("y"), lax.axis_index("c")


def _sigmoid(z):
    return 1.0 / (1.0 + jnp.exp(-z))


def _adamw(w, g, m, v):
    m = ADAM_B1 * m + (1.0 - ADAM_B1) * g
    v = ADAM_B2 * v + (1.0 - ADAM_B2) * (g * g)
    m_hat = m / (1.0 - ADAM_B1 ** ADAM_STEP)
    v_hat = v / (1.0 - ADAM_B2 ** ADAM_STEP)
    delta = -ADAM_LR * (m_hat / (jnp.sqrt(v_hat) + ADAM_EPS) + ADAM_WD * w)
    return delta, m, v


def _all_gather(name, blocks, memory_space):
    n_arr = len(blocks)

    def body(*refs):
        ins, outs = refs[:n_arr], refs[n_arr:2 * n_arr]
        send_sems, recv_sems, local_sems = refs[2 * n_arr:]
        x, y, c = _mesh_pos()
        me, sibling = (x, y, c), (x, y, 1 - c)
        chips = [(1 - x, y), (x, 1 - y), (1 - x, 1 - y)]

        def slot(p):
            return 4 * p[0] + 2 * p[1] + p[2]

        def copy(a, k, block, to, src=None):
            dst = outs[a].at[slot(block)]
            return pltpu.make_async_remote_copy(
                src_ref=dst if src is None else src, dst_ref=dst,
                send_sem=send_sems.at[a, k], recv_sem=recv_sems.at[a, k],
                device_id=to, device_id_type=MESH)

        mine = [pltpu.make_async_copy(ins[a], outs[a].at[slot(me)], local_sems.at[a]) for a in range(n_arr)]
        for cp in mine:
            cp.start()
        first = []
        for a in range(n_arr):
            first.append(copy(a, 0, me, sibling, src=ins[a]))
            first += [copy(a, 1 + j, me, (*chip, c), src=ins[a]) for j, chip in enumerate(chips)]
        for cp in first:
            cp.start()
        passed = []
        for j, chip in enumerate(chips):
            for a in range(n_arr):
                copy(a, 1 + j, (*chip, c), me).wait_recv()
                fwd = copy(a, 4 + j, (*chip, c), sibling)
                fwd.start()
                passed.append(fwd)
        for a in range(n_arr):
            copy(a, 0, sibling, me).wait_recv()
            for j, chip in enumerate(chips):
                copy(a, 4 + j, (*chip, 1 - c), me).wait_recv()
        for cp in first + passed:
            cp.wait_send()
        for cp in mine:
            cp.wait()

    spec = pl.BlockSpec(memory_space=memory_space)
    return pl.pallas_call(
        body, name=name,
        out_shape=[SDS((N_DEV,) + b.shape, b.dtype) for b in blocks],
        in_specs=[spec] * n_arr, out_specs=[spec] * n_arr,
        scratch_shapes=[pltpu.SemaphoreType.DMA((n_arr, 7)), pltpu.SemaphoreType.DMA((n_arr, 7)),
                        pltpu.SemaphoreType.DMA((n_arr,))],
        compiler_params=_params(),
    )(*blocks)


def _ada_exchange(c, w_ada, b_ada8, w_ada_f, b_ada_f8):
    n1, n2 = w_ada.shape[1], w_ada_f.shape[1]

    def body(c_ref, w1_ref, b1_ref, w2_ref, b2_ref, cact_ref, mod_ref, modf_ref,
             cact_buf, res1, res2, send1, send2, sems_s, sems_r):
        x, y, c_pos = _mesh_pos()
        me = 4 * x + 2 * y + c_pos
        flips = [(k >> 2 & 1, k >> 1 & 1, k & 1) for k in range(1, N_DEV)]

        def peer(f):
            return (1 - x if f[0] else x, 1 - y if f[1] else y, 1 - c_pos if f[2] else c_pos)

        cv = c_ref[...]
        cact = cv * _sigmoid(cv)
        cact_buf[...] = cact
        cact_ref[me] = cact

        def rdma(phase, k, src, dst, f):
            return pltpu.make_async_remote_copy(src_ref=src, dst_ref=dst, send_sem=sems_s.at[phase, k],
                                                recv_sem=sems_r.at[phase, k], device_id=peer(f), device_id_type=MESH)

        gather = [rdma(0, k, cact_buf, cact_ref.at[me], f) for k, f in enumerate(flips)]
        for cp in gather:
            cp.start()
        for cp in gather:
            cp.wait_recv()
        for cp in gather:
            cp.wait_send()

        rid = lax.broadcasted_iota(jnp.int32, (N_DEV, D_MODEL), 0)
        rows = jnp.zeros((N_DEV, D_MODEL), F32)
        for j in range(N_DEV):
            rows = jnp.where(rid == j, jnp.broadcast_to(cact_ref[j], (N_DEV, D_MODEL)), rows)
        rows = rows.astype(BF16)
        res1[...] = jnp.dot(rows, w1_ref[...].astype(BF16), preferred_element_type=F32) + b1_ref[pl.ds(me, 1), :]
        res2[...] = jnp.dot(rows, w2_ref[...].astype(BF16), preferred_element_type=F32) + b2_ref[pl.ds(me, 1), :]
        for j in range(N_DEV):
            send1[j] = res1[pl.ds(j, 1), :]
            send2[j] = res2[pl.ds(j, 1), :]
        mod_ref[me] = send1[me]
        modf_ref[me] = send2[me]
        scatter = []
        for k, f in enumerate(flips):
            to = me ^ (k + 1)
            scatter.append(rdma(1, k, send1.at[to], mod_ref.at[me], f))
            scatter.append(rdma(2, k, send2.at[to], modf_ref.at[me], f))
        for cp in scatter:
            cp.start()
        for cp in scatter:
            cp.wait_recv()
        for cp in scatter:
            cp.wait_send()

    vmem = pl.BlockSpec(memory_space=pltpu.VMEM)
    return pl.pallas_call(
        body, name="ada_exchange",
        out_shape=[SDS((N_DEV, 1, D_MODEL), F32), SDS((N_DEV, 1, n1), F32), SDS((N_DEV, 1, n2), F32)],
        in_specs=[vmem] * 5, out_specs=[vmem] * 3,
        scratch_shapes=[pltpu.VMEM((1, D_MODEL), F32), pltpu.VMEM((N_DEV, n1), F32), pltpu.VMEM((N_DEV, n2), F32),
                        pltpu.VMEM((N_DEV, 1, n1), F32), pltpu.VMEM((N_DEV, 1, n2), F32),
                        pltpu.SemaphoreType.DMA((3, 7)), pltpu.SemaphoreType.DMA((3, 7))],
        compiler_params=_params(),
    )(c, w_ada, b_ada8, w_ada_f, b_ada_f8)


def _chip_scatter(pair_ref, parts_ref, send_sems, recv_sems):
    x, y, c = _mesh_pos()
    chips = [(1 - x, y), (x, 1 - y), (1 - x, 1 - y)]
    return [pltpu.make_async_remote_copy(
        src_ref=pair_ref.at[2 * cx + cy], dst_ref=parts_ref.at[j], send_sem=send_sems.at[j], recv_sem=recv_sems.at[j],
        device_id=(cx, cy, c), device_id_type=MESH) for j, (cx, cy) in enumerate(chips)]


def _scatter_scratch():
    return [pltpu.SemaphoreType.DMA((3,)), pltpu.SemaphoreType.DMA((3,))]


def _prep_weights(me, wt, w_out):
    steps = 4

    def body(me_ref, wt_ref, wo_ref, wtb_ref, wob_ref):
        wtb_ref[...] = wt_ref[...].astype(BF16)
        wob_ref[...] = wo_ref[...].astype(BF16)

    def rows(a, mine):
        blk = (a.shape[0] // steps, a.shape[1])
        return pl.BlockSpec(blk, (lambda i, me_ref: (steps * me_ref[0] + i, 0)) if mine else (lambda i, me_ref: (i, 0)))

    return pl.pallas_call(
        body, name="prep_weights",
        grid_spec=pltpu.PrefetchScalarGridSpec(
            num_scalar_prefetch=1, grid=(steps,),
            in_specs=[rows(wt, False), rows(w_out, False)], out_specs=[rows(wt, True), rows(w_out, True)]),
        out_shape=[SDS((N_DEV * wt.shape[0], D_MODEL), BF16), SDS((N_DEV * w_out.shape[0], D_MODEL), BF16)],
        compiler_params=_params("parallel"),
    )(me, wt, w_out)


class _InPlaceGather:
    def __init__(self, buf_ref, send_sems, recv_sems, relay=False):
        self.buf, self.send_sems, self.recv_sems, self.relay = buf_ref, send_sems, recv_sems, relay
        self.n = buf_ref.shape[0] // N_DEV
        x, y, c = _mesh_pos()
        self.me, self.sibling, self.core = (x, y, c), (x, y, 1 - c), c
        self.chips = [(1 - x, y), (x, 1 - y), (1 - x, 1 - y)]
        self.relay_from = (jnp.where(c == 0, 1 - x, x), jnp.where(c == 0, y, 1 - y), c)
        self.relay_to = (jnp.where(c == 0, x, 1 - x), jnp.where(c == 0, 1 - y, y), c)

    def copy(self, k, block, to):
        start = pl.multiple_of((4 * block[0] + 2 * block[1] + block[2]) * self.n, self.n)
        rows = self.buf.at[pl.ds(start, self.n)]
        return pltpu.make_async_remote_copy(src_ref=rows, dst_ref=rows, send_sem=self.send_sems.at[k],
                                            recv_sem=self.recv_sems.at[k], device_id=to, device_id_type=MESH)

    def start(self):
        self.copy(0, self.me, self.sibling).start()
        for j, chip in enumerate(self.chips[:2] if self.relay else self.chips):
            self.copy(1 + j, self.me, (*chip, self.core)).start()

    def relay_diagonal(self):
        self.copy(3, self.relay_from, self.relay_to).start()

    def pass_on(self, j):
        self.copy(1 + j, (*self.chips[j], self.core), self.me).wait_recv()
        self.copy(4 + j, (*self.chips[j], self.core), self.sibling).start()

    def wait_sibling(self, k):
        self.copy(k, self.sibling, self.me).wait_recv()

    def wait_sends(self):
        for k in range(7):
            self.copy(k, self.me, self.sibling).wait_send()


def _gather_scratch():
    return [pltpu.SemaphoreType.DMA((7,)), pltpu.SemaphoreType.DMA((7,))]


def _gather_in_proj(order, x, shift, scale, norm_g, wt_all):
    s = x.shape[0]
    th, tm = min(512, s), min(1024, s)
    nh, ni = s // th, s // tm
    tn = D_IN // 4
    steps = nh + 4 * ni

    def body(order_ref, x_ref, shift_ref, scale_ref, g_ref, wt_in, h_ref, proj_ref, wt_ref,
             h_scr, w_buf, load_sems, send_sems, recv_sems):
        g = pl.program_id(0)
        gather = _InPlaceGather(wt_ref, send_sems, recv_sems, relay=True)

        def tile_load(slot, chip):
            return pltpu.make_async_copy(wt_ref.at[pl.ds(pl.multiple_of(chip * tn, tn), tn)], w_buf.at[slot],
                                         load_sems.at[slot])

        @pl.when(g == 0)
        def _():
            gather.start()

        @pl.when(g < nh)
        def _():
            xv = x_ref[...]
            r = lax.rsqrt(jnp.mean(xv * xv, axis=-1, keepdims=True) + NORM_EPS)
            hb = (((xv * r) * g_ref[...]) * (1.0 + scale_ref[...]) + shift_ref[...]).astype(BF16)
            h_ref[...] = hb
            h_scr[pl.ds(pl.multiple_of(g * th, th), th), :] = hb

        @pl.when(g == nh - 1)
        def _():
            gather.wait_sibling(0)
            tile_load(0, order_ref[0]).start()

        @pl.when(g >= nh)
        def _():
            t, i = (g - nh) // ni, (g - nh) % ni

            @pl.when(i == 0)
            def _():
                tile_load(t % 2, order_ref[t]).wait()

            @pl.when((i == ni - 1) & (t == 0))
            def _():
                gather.pass_on(0)
                gather.pass_on(1)
                gather.relay_diagonal()

            @pl.when((i == ni // 2) & (t == 2))
            def _():
                gather.pass_on(2)

            for j in range(3):
                @pl.when((i == ni - 1) & (t == j))
                def _():
                    gather.wait_sibling(4 + j)
                    tile_load((j + 1) % 2, order_ref[j + 1]).start()

            lhs = h_scr[pl.ds(pl.multiple_of(i * tm, tm), tm), :]
            proj_ref[...] = lax.dot_general(lhs, w_buf[t % 2], NT, preferred_element_type=F32).astype(BF16)

        @pl.when(g == steps - 1)
        def _():
            gather.wait_sends()

    def h_tile(g, order_ref):
        return (jnp.minimum(g, nh - 1), 0)

    def proj_tile(g, order_ref):
        mm = jnp.maximum(g - nh, 0)
        return (mm % ni, order_ref[mm // ni])

    row = pl.BlockSpec((1, D_MODEL), lambda g, order_ref: (0, 0))
    hbm = pl.BlockSpec(memory_space=pl.ANY)
    return pl.pallas_call(
        body, name="gather_in_proj",
        grid_spec=pltpu.PrefetchScalarGridSpec(
            num_scalar_prefetch=1, grid=(steps,),
            in_specs=[pl.BlockSpec((th, D_MODEL), h_tile), row, row, row, hbm],
            out_specs=[pl.BlockSpec((th, D_MODEL), h_tile), pl.BlockSpec((tm, tn), proj_tile), hbm],
            scratch_shapes=[pltpu.VMEM((s, D_MODEL), BF16), pltpu.VMEM((2, tn, D_MODEL), BF16),
                            pltpu.SemaphoreType.DMA((2,)), *_gather_scratch()]),
        out_shape=[SDS((s, D_MODEL), BF16), SDS((s, D_IN), BF16), SDS(wt_all.shape, BF16)],
        input_output_aliases={5: 2},
        compiler_params=_params("arbitrary"),
    )(order, x, shift, scale, norm_g, wt_all)


def _rope_freqs():
    inv_freq = ROPE_THETA ** (-jnp.arange(0, HEAD_DIM, 2, dtype=F32) / HEAD_DIM)
    return jnp.tile(inv_freq, 4).reshape(1, 128)


class _RopeTables:
    def __init__(self, freq_ref, rows_ref, state_ref, last_ref):
        self.freq, self.rows, self.state, self.last = freq_ref, rows_ref, state_ref, last_ref

    def start(self, block, direction):
        ang = lax.broadcasted_iota(jnp.int32, (CHUNK, 128), 0).astype(F32) * self.freq[...]
        self.rows[0] = jnp.cos(ang)
        self.rows[1] = jnp.sin(ang)
        base = jnp.asarray(block * CHUNK, dtype=F32) * self.freq[...]
        turn = float(direction * CHUNK) * self.freq[...]
        self.state[0:1, :] = jnp.cos(base)
        self.state[1:2, :] = jnp.sin(base)
        self.state[2:3, :] = jnp.cos(turn)
        self.state[3:4, :] = jnp.sin(turn)

    def step(self):
        c, s, ct, st = (self.state[k:k + 1, :] for k in range(4))
        self.state[0:1, :] = c * ct - s * st
        self.state[1:2, :] = s * ct + c * st

    def tables(self):
        c, s = self.state[0:1, :], self.state[1:2, :]
        cos = c * self.rows[0] - s * self.rows[1]
        sin = s * self.rows[0] + c * self.rows[1]
        first_half = (lax.broadcasted_iota(jnp.int32, (1, 128), 1) & (HEAD_DIM - 1)) < HEAD_DIM // 2
        return cos, jnp.where(first_half, -sin, 0.0), jnp.where(first_half, 0.0, sin)

    def keep(self, tabs):
        for k in range(3):
            self.last[k] = tabs[k]

    def kept(self):
        return tuple(self.last[k] for k in range(3))


def _rope_scratch():
    return [pltpu.VMEM((2, CHUNK, 128), F32), pltpu.VMEM((8, 128), F32), pltpu.VMEM((3, CHUNK, 128), F32)]


def _rope(v, cos, sin_lo, sin_hi):
    width = v.shape[1]
    rep = (1, width // 128)
    return (v * jnp.tile(cos, rep) + pltpu.roll(v, width - 32, 1) * jnp.tile(sin_lo, rep)
            + pltpu.roll(v, 32, 1) * jnp.tile(sin_hi, rep))


def _rope_bwd(d, cos, sin_lo, sin_hi):
    width = d.shape[1]
    rep = (1, width // 128)
    return (d * jnp.tile(cos, rep) + pltpu.roll(d * jnp.tile(sin_lo, rep), 32, 1)
            + pltpu.roll(d * jnp.tile(sin_hi, rep), width - 32, 1))


def _layer_norm(v, g, b):
    mu = jnp.mean(v, axis=-1, keepdims=True)
    vc = v - mu
    rstd = lax.rsqrt(jnp.mean(vc * vc, axis=-1, keepdims=True) + NORM_EPS)
    vhat = vc * rstd
    return vhat * g + b, vhat, rstd


def _set_tril(w_ref, out_ref):
    t = lax.broadcasted_iota(jnp.int32, (CHUNK, CHUNK), 0)
    tp = lax.broadcasted_iota(jnp.int32, (CHUNK, CHUNK), 1)
    for g in range(A_GROUPS):
        out_ref[g] = jnp.where(tp <= t, w_ref[g], 0.0).astype(BF16)


def _bias_columns(b_ref, out_ref):
    for g in range(A_GROUPS):
        out_ref[g] = jnp.broadcast_to(b_ref[pl.ds(g, 1), :], (CHUNK, CHUNK)).T


def _from_prev():
    r = lax.broadcasted_iota(jnp.int32, (CHUNK, 4 * CHUNK), 0)
    i = lax.broadcasted_iota(jnp.int32, (CHUNK, 4 * CHUNK), 1) & (CHUNK - 1)
    return r > i


def _set_unfold_masks(mask_ref):
    prev = _from_prev()
    mask_ref[0] = jnp.where(prev, 1.0, 0.0).astype(BF16)
    mask_ref[1] = jnp.where(prev, 0.0, 1.0).astype(BF16)


def _fold_band(t, from_prev):
    return jnp.where(from_prev, t[:CHUNK], t[CHUNK:])


def _unfold_band(t, mask_ref):
    return jnp.concatenate([t * mask_ref[0], t * mask_ref[1]], axis=0)


def _low_lanes():
    return lax.broadcasted_iota(jnp.int32, (1, 128), 1) < HEAD_DIM


def _stack_heads(pair_a, pair_b):
    lo = _low_lanes()
    return jnp.concatenate([jnp.where(lo, pair_a, 0.0), jnp.where(lo, 0.0, pair_a),
                            jnp.where(lo, pair_b, 0.0), jnp.where(lo, 0.0, pair_b)], axis=0).astype(BF16)


def _heads_to_lanes(per_group):
    rows = [t[:, r * CHUNK:(r + 1) * CHUNK] for t in per_group for r in range(4)]
    return jnp.concatenate(rows, axis=0).T


def _dup_kv_head(band, gk):
    pair = band[:, (gk // 2) * 128:(gk // 2 + 1) * 128]
    lo = _low_lanes()
    one = jnp.where(lo if gk % 2 == 0 else jnp.logical_not(lo), pair, 0.0)
    return (one + pltpu.roll(one, HEAD_DIM, 1)).astype(BF16)


def _fold_kv_head(dup_grad, gk):
    both = dup_grad + pltpu.roll(dup_grad, HEAD_DIM, 1)
    lo = _low_lanes()
    return jnp.where(lo if gk % 2 == 0 else jnp.logical_not(lo), both, 0.0)


def _attn_probs(q_st, k_dup, sink_row, from_prev, first_block):
    s = lax.dot_general(k_dup, q_st, NT, preferred_element_type=F32)
    no_prev = jnp.where(first_block, -jnp.inf, 0.0)
    s = jnp.where(from_prev, s[:CHUNK] + no_prev, s[CHUNK:])
    m = jnp.maximum(jnp.max(s, axis=0, keepdims=True), sink_row)
    p = jnp.exp(s - m)
    e_sink = jnp.exp(sink_row - m)
    inv = 1.0 / (jnp.sum(p, axis=0, keepdims=True) + e_sink)
    return p * inv, e_sink * inv


def _sink_row(sinks_ref, gk):
    return jnp.concatenate([jnp.full((1, CHUNK), sinks_ref[4 * gk + r], F32) for r in range(4)], axis=1)


def _mixer_specs(nb, rev):
    def blk(i):
        return nb - 1 - i if rev else i

    def prev(i):
        return jnp.maximum(blk(i) - 1, 0)

    return dict(
        cur=pl.BlockSpec((CHUNK, D_IN), lambda i, *_: (blk(i), 0)),
        prev_kv=pl.BlockSpec((CHUNK, 2 * 256), lambda i, *_: (prev(i), OFF_K // 512)),
        freq=pl.BlockSpec((1, 128), lambda i, *_: (0, 0)),
        vec=pl.BlockSpec((1, D_A), lambda i, *_: (0, 0)),
        wsp=pl.BlockSpec((A_GROUPS, CHUNK, CHUNK), lambda i, *_: (0, 0, 0)),
        bsp=pl.BlockSpec((A_GROUPS, CHUNK), lambda i, *_: (0, 0)),
        smem=pl.BlockSpec(memory_space=pltpu.SMEM),
        blk=blk,
    )


def _mixer_fwd(proj, freqs, ln_g, ln_b, w_sp, b_sp, sinks, wo_all):
    s = proj.shape[0]
    nb = s // CHUNK
    sp = _mixer_specs(nb, rev=False)

    def body(cur_ref, pkv_ref, freq_ref, lg_ref, lb_ref, w_ref, b_ref, sinks_ref, wo_in, y_ref, wo_ref,
             bcol, wtril, mask, rope_rows, rope_state, rope_last, send_sems, recv_sems):
        i = pl.program_id(0)
        gather = _InPlaceGather(wo_ref, send_sems, recv_sems)
        rope = _RopeTables(freq_ref, rope_rows, rope_state, rope_last)

        @pl.when(i == 0)
        def _():
            gather.start()
            _bias_columns(b_ref, bcol)
            _set_tril(w_ref, wtril)
            _set_unfold_masks(mask)
            rope.start(-1, 1)
            rope_last[...] = jnp.zeros_like(rope_last)

        @pl.when(i == (3 * nb) // 4)
        def _():
            for j in range(3):
                gather.pass_on(j)

        vln, _, _ = _layer_norm(cur_ref[:, OFF_VA:OFF_ZA].astype(F32), lg_ref[...], lb_ref[...])
        vln = vln.astype(BF16)
        for g in range(A_GROUPS):
            cols = slice(g * 128, (g + 1) * 128)
            sg = jnp.dot(wtril[g], vln[:, cols], preferred_element_type=F32) + bcol[g]
            u = cur_ref[:, OFF_U + g * 128:OFF_U + (g + 1) * 128].astype(F32)
            z = cur_ref[:, OFF_ZA + g * 128:OFF_ZA + (g + 1) * 128].astype(F32)
            y_ref[:, cols] = (u * sg * (z * _sigmoid(z))).astype(BF16)

        rope.step()
        cur_t, prev_t = rope.tables(), rope.kept()
        rope.keep(cur_t)
        qr = _rope(cur_ref[:, OFF_Q:OFF_K].astype(F32), *cur_t) * ATTN_SCALE
        kr = jnp.concatenate([_rope(pkv_ref[:, 0:256].astype(F32), *prev_t),
                              _rope(cur_ref[:, OFF_K:OFF_V].astype(F32), *cur_t)], axis=0)
        v_t = jnp.concatenate([pkv_ref[:, 256:512], cur_ref[:, OFF_V:OFF_ZB]], axis=0).astype(F32).T.astype(BF16)
        outs = []
        from_prev = _from_prev()
        for gk in range(N_KV_HEADS):
            q_st = _stack_heads(qr[:, (2 * gk) * 128:(2 * gk + 1) * 128], qr[:, (2 * gk + 1) * 128:(2 * gk + 2) * 128])
            probs, _ = _attn_probs(q_st, _dup_kv_head(kr, gk), _sink_row(sinks_ref, gk), from_prev, i == 0)
            outs.append(jnp.dot(v_t[gk * HEAD_DIM:(gk + 1) * HEAD_DIM], _unfold_band(probs.astype(BF16), mask),
                                preferred_element_type=F32))
        zb = cur_ref[:, OFF_ZB:D_IN].astype(F32)
        y_ref[:, D_A:D_MODEL] = (_heads_to_lanes(outs) * (zb * _sigmoid(zb))).astype(BF16)

        @pl.when(i == nb - 1)
        def _():
            gather.wait_sibling(0)
            for j in range(3):
                gather.wait_sibling(4 + j)
            gather.wait_sends()

    hbm = pl.BlockSpec(memory_space=pl.ANY)
    return pl.pallas_call(
        body, name="mixer_fwd", grid=(nb,),
        in_specs=[sp["cur"], sp["prev_kv"], sp["freq"], sp["vec"], sp["vec"], sp["wsp"], sp["bsp"], sp["smem"], hbm],
        out_specs=[pl.BlockSpec((CHUNK, D_MODEL), lambda i: (i, 0)), hbm],
        out_shape=[SDS((s, D_MODEL), BF16), SDS(wo_all.shape, wo_all.dtype)],
        scratch_shapes=[pltpu.VMEM((A_GROUPS, CHUNK, CHUNK), F32), pltpu.VMEM((A_GROUPS, CHUNK, CHUNK), BF16),
                        pltpu.VMEM((2, CHUNK, 4 * CHUNK), BF16), *_rope_scratch(), *_gather_scratch()],
        input_output_aliases={8: 1},
        compiler_params=_params("arbitrary"),
    )(proj, proj, freqs, ln_g, ln_b, w_sp, b_sp, sinks, wo_all)


def _out_proj_loss(y, x, target, wo, gate, shift_f, scale_f, fng):
    s = y.shape[0]
    tm, tr = 256, 128
    nt = s // tm

    def body(y_ref, x_ref, t_ref, wo_ref, gate_ref, sh_ref, sc_ref, g_ref, dx1_ref, do_ref, dy_ref, sums_ref,
             do_last, do_work):
        i = pl.program_id(0)

        @pl.when(i == 0)
        def _():
            sums_ref[...] = jnp.zeros_like(sums_ref)
            do_last[...] = jnp.zeros_like(do_last)

        do_work[...] = do_last[...]
        o = jnp.dot(y_ref[...], wo_ref[...], preferred_element_type=F32)
        gate, g, sh = gate_ref[...], g_ref[...], sh_ref[...]
        one_sc = 1.0 + sc_ref[...]
        cs, inv_d = g * one_sc, 1.0 / D_MODEL

        def rowsum(v):
            return jnp.sum(v, axis=0, keepdims=True)

        sums = [jnp.zeros((1, D_MODEL), F32) for _ in range(4)]
        for c in range(tm // tr):
            rows = slice(c * tr, (c + 1) * tr)
            oc = o[rows]
            x1 = x_ref[rows, :] + gate * oc
            r = lax.rsqrt(jnp.sum(x1 * x1, axis=-1, keepdims=True) * inv_d + NORM_EPS)
            x1n = x1 * r
            diff = x1n * cs + sh - t_ref[rows, :]
            w = diff * x1n
            lane_sum = jnp.sum(w * cs, axis=-1, keepdims=True)
            dx1 = (diff * cs) * (r * inv_d) - x1n * (r * lane_sum * (inv_d * inv_d))
            dx1_ref[rows, :] = dx1
            do = (dx1 * gate).astype(BF16)
            do_ref[rows, :] = do
            do_last[rows, :] = do
            for k, v in enumerate((dx1 * oc, diff, w, diff * diff)):
                sums[k] = sums[k] + rowsum(v)
        live = jnp.where(i < nt, 1.0, 0.0)
        for row, v in ((SUM_GATE, sums[0]), (SUM_SHIFT_F, inv_d * sums[1]), (SUM_SCALE_F, inv_d * (sums[2] * g)),
                       (SUM_FNG, inv_d * (sums[2] * one_sc)), (SUM_SQ_ERR, sums[3])):
            sums_ref[row:row + 1, :] += live * v
        dy_ref[...] = lax.dot_general(do_work[...], wo_ref[...], NT, preferred_element_type=F32).astype(BF16)

    tile = pl.BlockSpec((tm, D_MODEL), lambda i: (jnp.minimum(i, nt - 1), 0))
    row = pl.BlockSpec((1, D_MODEL), lambda i: (0, 0))
    return pl.pallas_call(
        body, name="out_proj_loss", grid=(nt + 1,),
        in_specs=[tile, tile, tile, pl.BlockSpec((D_MODEL, D_MODEL), lambda i: (0, 0)), row, row, row, row],
        out_specs=[tile, tile, pl.BlockSpec((tm, D_MODEL), lambda i: (jnp.maximum(i - 1, 0), 0)),
                   pl.BlockSpec((8, D_MODEL), lambda i: (0, 0))],
        out_shape=[SDS((s, D_MODEL), F32), SDS((s, D_MODEL), BF16), SDS((s, D_MODEL), BF16), SDS((8, D_MODEL), F32)],
        scratch_shapes=[pltpu.VMEM((tm, D_MODEL), BF16), pltpu.VMEM((tm, D_MODEL), BF16)],
        compiler_params=_params("arbitrary"),
    )(y, x, target, wo, gate, shift_f, scale_f, fng)


ROW_DBSP, ROW_DSINKS, MISC_ROWS = 0, 8, 32


def _mixer_bwd(me, proj, dy, freqs, ln_g, ln_b, w_sp, b_sp, sinks, pair):
    s = proj.shape[0]
    nb = s // CHUNK
    sp = _mixer_specs(nb, rev=True)

    def body(me_ref, cur_ref, pkv_ref, dy_ref, freq_ref, lg_ref, lb_ref, w_ref, b_ref, sinks_ref, pair_ref,
             dproj_ref, dln_ref, dw_ref, misc_ref, parts_ref, bcol, wtril, dbcol, carry, mask, rope_rows, rope_state,
             rope_last, send_sems, recv_sems):
        i = pl.program_id(0)
        block = nb - 1 - i
        rope = _RopeTables(freq_ref, rope_rows, rope_state, rope_last)

        @pl.when(i == 0)
        def _():
            for cp in _chip_scatter(pair_ref, parts_ref, send_sems, recv_sems):
                cp.start()
            _bias_columns(b_ref, bcol)
            _set_tril(w_ref, wtril)
            _set_unfold_masks(mask)
            rope.start(nb - 1, -1)
            rope.keep(rope.tables())
            dbcol[...] = jnp.zeros_like(dbcol)
            carry[...] = jnp.zeros_like(carry)
            dln_ref[...] = jnp.zeros_like(dln_ref)
            dw_ref[...] = jnp.zeros_like(dw_ref)
            misc_ref[...] = jnp.zeros_like(misc_ref)

        vln, vhat, rstd = _layer_norm(cur_ref[:, OFF_VA:OFF_ZA].astype(F32), lg_ref[...], lb_ref[...])
        vln = vln.astype(BF16)
        d_vln = []
        for g in range(A_GROUPS):
            cols = slice(g * 128, (g + 1) * 128)
            w_g = wtril[g]
            sg = jnp.dot(w_g, vln[:, cols], preferred_element_type=F32) + bcol[g]
            u = cur_ref[:, OFF_U + g * 128:OFF_U + (g + 1) * 128].astype(F32)
            z = cur_ref[:, OFF_ZA + g * 128:OFF_ZA + (g + 1) * 128].astype(F32)
            dya = dy_ref[:, cols].astype(F32)
            sig = _sigmoid(z)
            d_ya = dya * (z * sig)
            dproj_ref[:, OFF_ZA + g * 128:OFF_ZA + (g + 1) * 128] = (
                dya * (u * sg) * (sig * (1.0 + z * (1.0 - sig)))).astype(BF16)
            dproj_ref[:, OFF_U + g * 128:OFF_U + (g + 1) * 128] = (d_ya * sg).astype(BF16)
            d_s = d_ya * u
            dbcol[g] += d_s
            d_sb = d_s.astype(BF16)
            dw_ref[g] += lax.dot_general(d_sb, vln[:, cols], NT, preferred_element_type=F32)
            d_vln.append(lax.dot_general(w_g, d_sb, TN, preferred_element_type=F32))
        d_vln = jnp.concatenate(d_vln, axis=1)
        dln_ref[0:1, :] += jnp.sum(d_vln * vhat, axis=0, keepdims=True)
        dln_ref[1:2, :] += jnp.sum(d_vln, axis=0, keepdims=True)
        d_vhat = d_vln * lg_ref[...]
        d_va = rstd * (d_vhat - jnp.mean(d_vhat, axis=-1, keepdims=True)
                       - vhat * jnp.mean(d_vhat * vhat, axis=-1, keepdims=True))
        dproj_ref[:, OFF_VA:OFF_ZA] = d_va.astype(BF16)

        cur_t = rope.kept()
        rope.step()
        prev_t = rope.tables()
        rope.keep(prev_t)
        band_t = tuple(jnp.concatenate([p, c], axis=0) for p, c in zip(prev_t, cur_t))
        qr = _rope(cur_ref[:, OFF_Q:OFF_K].astype(F32), *cur_t) * ATTN_SCALE
        kr = jnp.concatenate([_rope(pkv_ref[:, 0:256].astype(F32), *prev_t),
                              _rope(cur_ref[:, OFF_K:OFF_V].astype(F32), *cur_t)], axis=0)
        vb = jnp.concatenate([pkv_ref[:, 256:512], cur_ref[:, OFF_V:OFF_ZB]], axis=0).astype(F32)
        k_t, v_t = (kr.T * ATTN_SCALE).astype(BF16), vb.T.astype(BF16)
        zb = cur_ref[:, OFF_ZB:D_IN].astype(F32)
        dyb = dy_ref[:, D_A:D_MODEL].astype(F32)
        sig = _sigmoid(zb)
        d_yb = dyb * (zb * sig)
        outs, dqs = [], []
        dk_pairs = [jnp.zeros((2 * CHUNK, 128), F32) for _ in range(2)]
        dv_pairs = [jnp.zeros((2 * CHUNK, 128), F32) for _ in range(2)]
        from_prev = _from_prev()
        for gk in range(N_KV_HEADS):
            heads = slice(gk * HEAD_DIM, (gk + 1) * HEAD_DIM)
            q_st = _stack_heads(qr[:, (2 * gk) * 128:(2 * gk + 1) * 128], qr[:, (2 * gk + 1) * 128:(2 * gk + 2) * 128])
            k_dup, v_dup = _dup_kv_head(kr, gk), _dup_kv_head(vb, gk)
            probs, p_sink = _attn_probs(q_st, k_dup, _sink_row(sinks_ref, gk), from_prev, block == 0)
            probs_b = _unfold_band(probs.astype(BF16), mask)
            outs.append(jnp.dot(v_t[heads], probs_b, preferred_element_type=F32))
            do_st = _stack_heads(d_yb[:, (2 * gk) * 128:(2 * gk + 1) * 128], d_yb[:, (2 * gk + 1) * 128:(2 * gk + 2) * 128])
            dp = _fold_band(lax.dot_general(v_dup, do_st, NT, preferred_element_type=F32), from_prev)
            delta = jnp.sum(probs * dp, axis=0, keepdims=True)
            ds = _unfold_band((probs * (dp - delta)).astype(BF16), mask)
            d_sink = -p_sink * delta
            for r in range(4):
                row = ROW_DSINKS + 4 * gk + r
                misc_ref[row:row + 1, :] += jnp.broadcast_to(
                    jnp.sum(d_sink[:, r * CHUNK:(r + 1) * CHUNK], axis=1, keepdims=True), (1, 128))
            dqs.append(jnp.dot(k_t[heads], ds, preferred_element_type=F32))
            dk_pairs[gk // 2] += _fold_kv_head(jnp.dot(ds, q_st, preferred_element_type=F32), gk)
            dv_pairs[gk // 2] += _fold_kv_head(jnp.dot(probs_b, do_st, preferred_element_type=F32), gk)
        dproj_ref[:, OFF_ZB:D_IN] = (dyb * _heads_to_lanes(outs) * (sig * (1.0 + zb * (1.0 - sig)))).astype(BF16)
        dproj_ref[:, OFF_Q:OFF_K] = _rope_bwd(_heads_to_lanes(dqs), *cur_t).astype(BF16)
        dk_band = _rope_bwd(jnp.concatenate(dk_pairs, axis=1), *band_t)
        dv_band = jnp.concatenate(dv_pairs, axis=1)
        dproj_ref[:, OFF_K:OFF_V] = (dk_band[CHUNK:] + carry[:, 0:256]).astype(BF16)
        dproj_ref[:, OFF_V:OFF_ZB] = (dv_band[CHUNK:] + carry[:, 256:512]).astype(BF16)
        carry[:, 0:256] = dk_band[:CHUNK]
        carry[:, 256:512] = dv_band[:CHUNK]

        @pl.when(i == nb - 1)
        def _():
            t = lax.broadcasted_iota(jnp.int32, (CHUNK, CHUNK), 0)
            tp = lax.broadcasted_iota(jnp.int32, (CHUNK, CHUNK), 1)
            for g in range(A_GROUPS):
                dw_ref[g] = jnp.where(tp <= t, dw_ref[g], 0.0)
                misc_ref[pl.ds(ROW_DBSP + g, 1), :] = jnp.sum(dbcol[g].T, axis=0, keepdims=True)
            scatter = _chip_scatter(pair_ref, parts_ref, send_sems, recv_sems)
            for cp in scatter:
                cp.wait_recv()
            for cp in scatter:
                cp.wait_send()

    blk = sp["blk"]
    hbm = pl.BlockSpec(memory_space=pl.ANY)
    return pl.pallas_call(
        body, name="mixer_bwd",
        grid_spec=pltpu.PrefetchScalarGridSpec(
            num_scalar_prefetch=1, grid=(nb,),
            in_specs=[sp["cur"], sp["prev_kv"], pl.BlockSpec((CHUNK, D_MODEL), lambda i, me_ref: (blk(i), 0)),
                      sp["freq"], sp["vec"], sp["vec"], sp["wsp"], sp["bsp"], sp["smem"], hbm],
            out_specs=[pl.BlockSpec((CHUNK, D_IN), lambda i, me_ref: (blk(i), 0)),
                       pl.BlockSpec((8, D_A), lambda i, me_ref: (me_ref[0], 0)),
                       pl.BlockSpec((A_GROUPS, CHUNK, CHUNK), lambda i, me_ref: (me_ref[0], 0, 0)),
                       pl.BlockSpec((MISC_ROWS, 128), lambda i, me_ref: (me_ref[0], 0)), hbm],
            scratch_shapes=[pltpu.VMEM((A_GROUPS, CHUNK, CHUNK), F32), pltpu.VMEM((A_GROUPS, CHUNK, CHUNK), BF16),
                            pltpu.VMEM((A_GROUPS, CHUNK, CHUNK), F32), pltpu.VMEM((CHUNK, 512), F32), pltpu.VMEM((2, CHUNK, 4 * CHUNK), BF16),
                            *_rope_scratch(), *_scatter_scratch()]),
        out_shape=[SDS((s, D_IN), BF16), SDS((N_DEV * 8, D_A), F32), SDS((N_DEV * A_GROUPS, CHUNK, CHUNK), F32),
                   SDS((N_DEV * MISC_ROWS, 128), F32), SDS((3,) + pair.shape[1:], pair.dtype)],
        compiler_params=_params("arbitrary"),
    )(me, proj, proj, dy, freqs, ln_g, ln_b, w_sp, b_sp, sinks, pair)


def _wgrad_pair(name, a, b, bt, gathers=()):
    s, m = a.shape
    n = b.shape[1]
    bm, half = m // 4, m // 8
    bt = min(bt, s)
    steps = s // bt
    last = 4 * steps
    n_g = len(gathers)

    def body(*refs):
        a_ref, b_ref = refs[:2]
        out_ref, bufs = refs[2 + n_g], refs[3 + n_g:3 + 2 * n_g]
        acc, kept, got, sent, send_sems, recv_sems = refs[3 + 2 * n_g:9 + 2 * n_g]
        sems = refs[9 + 2 * n_g:]
        g = pl.program_id(0)
        tile, t = g // steps, g % steps
        mx, my, mc = _mesh_pos()
        jobs = [_InPlaceGather(bufs[k], sems[2 * k], sems[2 * k + 1]) for k in range(n_g)]

        def exchange(q):
            return pltpu.make_async_remote_copy(src_ref=sent, dst_ref=got.at[q % 2], send_sem=send_sems.at[q],
                                                recv_sem=recv_sems.at[q], device_id=(mx, my, 1 - mc),
                                                device_id_type=MESH)

        @pl.when(g == 0)
        def _():
            for job in jobs:
                job.start()

        @pl.when(g == 2 * steps)
        def _():
            for job in jobs:
                for j in range(3):
                    job.pass_on(j)

        @pl.when(g < last)
        def _():
            prod = lax.dot_general(a_ref[...], b_ref[...], TN, preferred_element_type=F32)

            @pl.when(t == 0)
            def _():
                acc[...] = prod

            @pl.when(t > 0)
            def _():
                acc[...] += prod

            @pl.when(t == steps - 1)
            def _():
                @pl.when(tile > 0)
                def _():
                    exchange(tile - 1).wait_send()

                kept[tile % 2] = acc[pl.ds(pl.multiple_of(mc * half, 8), half), :].astype(BF16)
                sent[...] = acc[pl.ds(pl.multiple_of((1 - mc) * half, 8), half), :].astype(BF16)
                exchange(tile).start()

        @pl.when((t == 0) & (g > 0))
        def _():
            q = tile - 1
            exchange(q).wait_recv()
            out_ref[0] = (kept[q % 2].astype(F32) + got[q % 2].astype(F32)).astype(BF16)

        @pl.when(g == last)
        def _():
            exchange(3).wait_send()
            for job in jobs:
                job.wait_sibling(0)
                for j in range(3):
                    job.wait_sibling(4 + j)
                job.wait_sends()

    def a_tile(g):
        gg = jnp.minimum(g, last - 1)
        return (gg % steps, gg // steps)

    def b_tile(g):
        return (jnp.minimum(g, last - 1) % steps, 0)

    hbm = pl.BlockSpec(memory_space=pl.ANY)
    outs = pl.pallas_call(
        body, name=name, grid=(last + 1,),
        in_specs=[pl.BlockSpec((bt, bm), a_tile), pl.BlockSpec((bt, n), b_tile)] + [hbm] * n_g,
        out_specs=[pl.BlockSpec((1, half, n), lambda g: (jnp.maximum(g - 1, 0) // steps, 0, 0))] + [hbm] * n_g,
        out_shape=[SDS((4, half, n), BF16)] + [SDS(gb.shape, gb.dtype) for gb in gathers],
        scratch_shapes=[pltpu.VMEM((bm, n), F32), pltpu.VMEM((2, half, n), BF16), pltpu.VMEM((2, half, n), BF16),
                        pltpu.VMEM((half, n), BF16), pltpu.SemaphoreType.DMA((4,)), pltpu.SemaphoreType.DMA((4,))]
        + _gather_scratch() * n_g,
        input_output_aliases={2 + k: 1 + k for k in range(n_g)},
        compiler_params=_params("arbitrary"),
    )(a, b, *gathers)
    return outs[0], outs[1:]


def _in_proj_bwd(dproj, wt, x, dx1, scale, norm_g, sums_o, pair):
    s = x.shape[0]
    tm, tk, tr = min(1024, s), D_IN // 4, 64
    ksteps = D_IN // tk

    def body(dp_ref, wt_ref, x_hbm, dx1_hbm, sc_ref, g_ref, so_ref, pair_ref, gx_ref, sums_ref, parts_ref, x_buf,
             dx1_buf, tile_sems, send_sems, recv_sems):
        i, k = pl.program_id(0), pl.program_id(1)

        def tile_copies():
            rows = pl.ds(pl.multiple_of(i * tm, tm), tm)
            return (pltpu.make_async_copy(x_hbm.at[rows], x_buf, tile_sems.at[0]),
                    pltpu.make_async_copy(dx1_hbm.at[rows], dx1_buf, tile_sems.at[1]))

        @pl.when((i == 0) & (k == 0))
        def _():
            for cp in _chip_scatter(pair_ref, parts_ref, send_sems, recv_sems):
                cp.start()
            sums_ref[...] = so_ref[...]

        @pl.when(k == 0)
        def _():
            for cp in tile_copies():
                cp.start()
            gx_ref[...] = jnp.dot(dp_ref[...], wt_ref[...], preferred_element_type=F32)

        @pl.when(k > 0)
        def _():
            gx_ref[...] += jnp.dot(dp_ref[...], wt_ref[...], preferred_element_type=F32)

        @pl.when(k == ksteps - 1)
        def _():
            for cp in tile_copies():
                cp.wait()
            one_sc, g = 1.0 + sc_ref[...], g_ref[...]
            cs = one_sc * g

            def chunk(j, sums):
                rows = pl.ds(pl.multiple_of(j * tr, tr), tr)
                dh, xv = gx_ref[rows, :], x_buf[rows, :]
                dhx = dh * xv
                r = lax.rsqrt(jnp.sum(xv * xv, axis=-1, keepdims=True) * (1.0 / D_MODEL) + NORM_EPS)
                coef = (r * r * r) * (jnp.sum(dhx * cs, axis=-1, keepdims=True) * (1.0 / D_MODEL))
                gx_ref[rows, :] = dx1_buf[rows, :] + r * (dh * cs) - xv * coef
                return (sums[0] + jnp.sum(dh, axis=0, keepdims=True), sums[1] + jnp.sum(dhx * r, axis=0, keepdims=True))

            zero = jnp.zeros((1, D_MODEL), F32)
            sums = lax.fori_loop(0, tm // tr, chunk, (zero, zero))
            sums_ref[SUM_SHIFT:SUM_SHIFT + 1, :] += sums[0]
            sums_ref[SUM_SCALE:SUM_SCALE + 1, :] += sums[1] * g
            sums_ref[SUM_NORM_G:SUM_NORM_G + 1, :] += sums[1] * one_sc

        @pl.when((i == s // tm - 1) & (k == ksteps - 1))
        def _():
            scatter = _chip_scatter(pair_ref, parts_ref, send_sems, recv_sems)
            for cp in scatter:
                cp.wait_recv()
            for cp in scatter:
                cp.wait_send()

    row = pl.BlockSpec((1, D_MODEL), lambda i, k: (0, 0))
    hbm = pl.BlockSpec(memory_space=pl.ANY)
    return pl.pallas_call(
        body, name="in_proj_bwd", grid=(s // tm, ksteps),
        in_specs=[pl.BlockSpec((tm, tk), lambda i, k: (i, k)), pl.BlockSpec((tk, D_MODEL), lambda i, k: (k, 0)),
                  hbm, hbm, row, row, pl.BlockSpec((8, D_MODEL), lambda i, k: (0, 0)), hbm],
        out_specs=[pl.BlockSpec((tm, D_MODEL), lambda i, k: (i, 0)), pl.BlockSpec((8, D_MODEL), lambda i, k: (0, 0)),
                   hbm],
        out_shape=[SDS((s, D_MODEL), F32), SDS((8, D_MODEL), F32), SDS((3,) + pair.shape[1:], pair.dtype)],
        scratch_shapes=[pltpu.VMEM((tm, D_MODEL), F32), pltpu.VMEM((tm, D_MODEL), F32),
                        pltpu.SemaphoreType.DMA((2,)), *_scatter_scratch()],
        compiler_params=_params("arbitrary", "arbitrary"),
    )(dproj, wt, x, dx1, scale, norm_g, sums_o, pair)


def _sum_chips(own_ref, parts_ref):
    return ((own_ref[0].astype(F32) + parts_ref[0].astype(F32)) + parts_ref[1].astype(F32)) + parts_ref[2].astype(F32)


def _adam_rows(name, chip, pair, parts, w, m, v, tr):
    rows = w.shape[0]

    def body(chip_ref, own_ref, p_ref, w_ref, m_ref, v_ref, g_ref, d_ref, nm_ref, nv_ref):
        g = _sum_chips(own_ref, p_ref)
        g_ref[...] = g
        d_ref[...], nm_ref[...], nv_ref[...] = _adamw(w_ref[...], g, m_ref[...], v_ref[...])

    blk = pl.BlockSpec((tr, D_MODEL), lambda j, chip_ref: (j, 0))
    return pl.pallas_call(
        body, name=name,
        grid_spec=pltpu.PrefetchScalarGridSpec(
            num_scalar_prefetch=1, grid=(rows // tr,),
            in_specs=[pl.BlockSpec((1, tr, D_MODEL), lambda j, chip_ref: (chip_ref[0], j, 0)),
                      pl.BlockSpec((3, tr, D_MODEL), lambda j, chip_ref: (0, j, 0)), blk, blk, blk],
            out_specs=[blk] * 4),
        out_shape=[SDS(w.shape, F32)] * 4, compiler_params=_params("parallel"),
    )(chip, pair, parts, w, m, v)


def _adam_ada(name, cact, dmod, w, m, v):
    n = w.shape[1]
    tr = 512

    def body(c_ref, dm_ref, w_ref, m_ref, v_ref, g_ref, d_ref, nm_ref, nv_ref):
        pad_c = jnp.concatenate([c_ref[...], jnp.zeros_like(c_ref)], axis=0).astype(BF16)
        pad_d = jnp.concatenate([dm_ref[...], jnp.zeros_like(dm_ref)], axis=0).astype(BF16)
        g = lax.dot_general(pad_c, pad_d, TN, preferred_element_type=F32)
        g_ref[...] = g
        d_ref[...], nm_ref[...], nv_ref[...] = _adamw(w_ref[...], g, m_ref[...], v_ref[...])

    blk = pl.BlockSpec((tr, n), lambda j: (j, 0))
    return pl.pallas_call(
        body, name=name, grid=(D_MODEL // tr,),
        in_specs=[pl.BlockSpec((N_DEV, tr), lambda j: (0, j)), pl.BlockSpec((N_DEV, n), lambda j: (0, 0)),
                  blk, blk, blk],
        out_specs=[blk] * 4, out_shape=[SDS(w.shape, F32)] * 4,
        compiler_params=_params("parallel"),
    )(cact, dmod, w, m, v)


SMALL_PARAMS = ("w_spatial", "b_spatial", "sinks", "norm_g", "ln_v_g", "ln_v_b", "final_norm_g", "b_ada", "b_ada_final")


def _adam_small(d_wsp, misc, d_ln, sums, params):
    n_p = len(SMALL_PARAMS)

    def body(*refs):
        wsp_ref, misc_ref, ln_ref, sums_ref = refs[:4]
        wmv = [refs[4 + 3 * k:7 + 3 * k] for k in range(n_p)]
        loss_ref = refs[4 + 3 * n_p]
        outs = [refs[5 + 3 * n_p + 4 * k:9 + 3 * n_p + 4 * k] for k in range(n_p)]

        def column_sum(row):
            return total(sums_ref, (row, row + 1))

        def total(ref, rows=None):
            def part(j):
                return ref[j] if rows is None else ref[j, rows[0]:rows[1], :]
            acc = part(0)
            for j in range(1, N_DEV):
                acc = acc + part(j)
            return acc

        sink_rows = total(misc_ref, (ROW_DSINKS, ROW_DSINKS + 16))
        diag = (lax.broadcasted_iota(jnp.int32, (16, 128), 0) == lax.broadcasted_iota(jnp.int32, (16, 128), 1))
        grads = dict(
            w_spatial=total(wsp_ref), b_spatial=total(misc_ref, (ROW_DBSP, ROW_DBSP + A_GROUPS)),
            sinks=jnp.sum(jnp.where(diag, sink_rows, 0.0), axis=0, keepdims=True),
            norm_g=column_sum(SUM_NORM_G), ln_v_g=total(ln_ref, (0, 1)), ln_v_b=total(ln_ref, (1, 2)),
            final_norm_g=column_sum(SUM_FNG),
            b_ada=jnp.concatenate([column_sum(SUM_SHIFT), column_sum(SUM_SCALE), column_sum(SUM_GATE)], axis=1),
            b_ada_final=jnp.concatenate([column_sum(SUM_SHIFT_F), column_sum(SUM_SCALE_F)], axis=1))
        sq_err = jnp.sum(column_sum(SUM_SQ_ERR), axis=1, keepdims=True)
        loss_ref[...] = jnp.broadcast_to(sq_err * (0.5 / D_MODEL), (1, 128))
        for k, name in enumerate(SMALL_PARAMS):
            w_ref, m_ref, v_ref = wmv[k]
            g_ref, d_ref, nm_ref, nv_ref = outs[k]
            g_ref[...] = grads[name]
            d_ref[...], nm_ref[...], nv_ref[...] = _adamw(w_ref[...], grads[name], m_ref[...], v_ref[...])

    flat = [a for name in SMALL_PARAMS for a in params[name]]
    vmem = pl.BlockSpec(memory_space=pltpu.VMEM)
    out_shape = [SDS((1, 128), F32)] + [SDS(params[name][0].shape, F32) for name in SMALL_PARAMS for _ in range(4)]
    outs = pl.pallas_call(
        body, name="adam_small", in_specs=[vmem] * (4 + len(flat)), out_specs=[vmem] * len(out_shape),
        out_shape=out_shape, compiler_params=_params(),
    )(d_wsp, misc, d_ln, sums, *flat)
    return outs[0], {name: outs[1 + 4 * k:5 + 4 * k] for k, name in enumerate(SMALL_PARAMS)}


def kernel(x, c, w_ada, b_ada, norm_g, w_in, ln_v_g, ln_v_b, w_spatial, b_spatial, sinks, w_out, w_ada_final, b_ada_final, final_norm_g, loss_target, m_w_ada, m_b_ada, m_norm_g, m_w_in, m_ln_v_g, m_ln_v_b, m_w_spatial, m_b_spatial, m_sinks, m_w_out, m_w_ada_final, m_b_ada_final, m_final_norm_g, v_w_ada, v_b_ada, v_norm_g, v_w_in, v_ln_v_g, v_ln_v_b, v_w_spatial, v_b_spatial, v_sinks, v_w_out, v_w_ada_final, v_b_ada_final, v_final_norm_g):
    me = 4 * lax.axis_index("x") + 2 * lax.axis_index("y") + lax.axis_index("c")
    x2, tgt = x[0], loss_target[0]
    fng = final_norm_g.reshape(1, D_MODEL)

    n_ada, n_ada_f = w_ada.shape[2], w_ada_final.shape[1]
    cact, mod, mod_f = _ada_exchange(c, w_ada[0], b_ada.reshape(N_DEV, n_ada), w_ada_final,
                                     b_ada_final.reshape(N_DEV, n_ada_f))
    cact = cact.reshape(N_DEV, D_MODEL)
    mod, mod_f = mod.reshape(1, 3 * D_MODEL), mod_f.reshape(1, 2 * D_MODEL)
    shift, scale, gate = mod[:, :D_MODEL], mod[:, D_MODEL:2 * D_MODEL], mod[:, 2 * D_MODEL:]
    shift_f, scale_f = mod_f[:, :D_MODEL], mod_f[:, D_MODEL:]

    wt_f32, m_wt, v_wt = (jnp.swapaxes(a, 1, 2)[0] for a in (w_in, m_w_in, v_w_in))
    xi, yi = lax.axis_index("x"), lax.axis_index("y")
    chip_order = jnp.stack([2 * xi + yi, 2 * (1 - xi) + yi, 2 * xi + 1 - yi, 2 * (1 - xi) + 1 - yi]).astype(jnp.int32)
    wt_mine, wo_mine = _prep_weights(me.reshape(1), wt_f32, w_out[0])

    freqs = _rope_freqs()
    sinks_v = sinks.reshape(16)
    h, proj, wt = _gather_in_proj(chip_order, x2, shift, scale, norm_g, wt_mine)
    y, wo = _mixer_fwd(proj, freqs, ln_v_g, ln_v_b, w_spatial[0], b_spatial[0], sinks_v, wo_mine)
    dx1, do, dy, sums_o = _out_proj_loss(y, x2, tgt, wo, gate, shift_f, scale_f, fng)

    chip = (2 * lax.axis_index("x") + lax.axis_index("y")).reshape(1)
    pair_out, _ = _wgrad_pair("wgrad_out", y, do, 2048)
    dproj, d_ln, d_wsp, misc, parts_out = _mixer_bwd(
        me.reshape(1), proj, dy, freqs, ln_v_g, ln_v_b, w_spatial[0], b_spatial[0], sinks_v, pair_out)
    pair_in, (d_ln, d_wsp, misc) = _wgrad_pair(
        "wgrad_in", dproj, h, 1024, gathers=(d_ln, d_wsp.reshape(N_DEV * A_GROUPS * CHUNK, CHUNK), misc))
    grad_x, sums, parts_in = _in_proj_bwd(dproj, wt, x2, dx1, scale, norm_g, sums_o, pair_in)
    wt_leaves = [jnp.swapaxes(a[None], 1, 2)
                 for a in _adam_rows("adam_w_in", chip, pair_in, parts_in, wt_f32, m_wt, v_wt, 176)]
    w_out_leaves = [a[None] for a in _adam_rows("adam_w_out", chip, pair_out, parts_out, w_out[0], m_w_out[0], v_w_out[0], 64)]

    (sums,) = _all_gather("gather_sums", [sums], pltpu.VMEM)
    natural = dict(w_spatial=(A_GROUPS * CHUNK, CHUNK), b_spatial=(A_GROUPS, CHUNK), sinks=(1, 16), norm_g=(1, D_MODEL),
                   ln_v_g=(1, D_A), ln_v_b=(1, D_A), final_norm_g=(1, D_MODEL), b_ada=(1, 3 * D_MODEL),
                   b_ada_final=(1, 2 * D_MODEL))
    given = dict(
        w_spatial=(w_spatial, m_w_spatial, v_w_spatial), b_spatial=(b_spatial, m_b_spatial, v_b_spatial),
        sinks=(sinks, m_sinks, v_sinks), norm_g=(norm_g, m_norm_g, v_norm_g), ln_v_g=(ln_v_g, m_ln_v_g, v_ln_v_g),
        ln_v_b=(ln_v_b, m_ln_v_b, v_ln_v_b), final_norm_g=(final_norm_g, m_final_norm_g, v_final_norm_g),
        b_ada=(b_ada, m_b_ada, v_b_ada), b_ada_final=(b_ada_final, m_b_ada_final, v_b_ada_final))
    params = {name: tuple(a.reshape(natural[name]) for a in given[name]) for name in SMALL_PARAMS}
    params["sinks"] = tuple(jnp.pad(a, ((0, 0), (0, 128 - 16))) for a in params["sinks"])
    loss, small = _adam_small(d_wsp.reshape(N_DEV, A_GROUPS * CHUNK, CHUNK), misc.reshape(N_DEV, MISC_ROWS, 128),
                              d_ln.reshape(N_DEV, 8, D_A), sums, params)
    small["sinks"] = [a[:, :16] for a in small["sinks"]]
    small = {name: [a.reshape(given[name][0].shape) for a in small[name]] for name in SMALL_PARAMS}

    dmod_all = jnp.concatenate([sums[:, SUM_SHIFT], sums[:, SUM_SCALE], sums[:, SUM_GATE]], axis=1)
    dmod_f_all = jnp.concatenate([sums[:, SUM_SHIFT_F], sums[:, SUM_SCALE_F]], axis=1)
    dmod_mine = lax.dynamic_slice_in_dim(dmod_all, me * n_ada, n_ada, axis=1)
    dmod_f_mine = lax.dynamic_slice_in_dim(dmod_f_all, me * n_ada_f, n_ada_f, axis=1)
    ada = _adam_ada("adam_w_ada", cact, dmod_mine, w_ada[0], m_w_ada[0], v_w_ada[0])
    ada_f = _adam_ada("adam_w_ada_final", cact, dmod_f_mine, w_ada_final, m_w_ada_final, v_w_ada_final)

    def leaves(k):
        return (ada[k][None], small["b_ada"][k], small["norm_g"][k], wt_leaves[k], small["ln_v_g"][k],
                small["ln_v_b"][k], small["w_spatial"][k], small["b_spatial"][k], small["sinks"][k], w_out_leaves[k],
                ada_f[k], small["b_ada_final"][k], small["final_norm_g"][k])

    return (loss[0, 0], grad_x[None], *leaves(0), *leaves(1), *leaves(2), *leaves(3))
```

```python
import jax
import jax.numpy as jnp
from jax import lax
from jax.experimental import pallas as pl
from jax.experimental.pallas import tpu as pltpu

D_MODEL = 2048
D_IN = 5632
D_A = 1024
CHUNK = 128
A_GROUPS = 8
HEAD_DIM = 64
N_KV_HEADS = 4
N_DEV = 8
ROPE_THETA = 10000.0
NORM_EPS = 1e-5
ATTN_SCALE = HEAD_DIM ** -0.5

ADAM_LR = 0.001
ADAM_B1 = 0.9
ADAM_B2 = 0.999
ADAM_EPS = 1e-08
ADAM_WD = 0.01
ADAM_STEP = 10

OFF_U, OFF_VA, OFF_ZA, OFF_Q, OFF_K, OFF_V, OFF_ZB = 0, 1024, 2048, 3072, 4096, 4352, 4608

SUM_SHIFT, SUM_SCALE, SUM_NORM_G, SUM_GATE, SUM_SHIFT_F, SUM_SCALE_F, SUM_FNG, SUM_SQ_ERR = range(8)

V7X_VMEM_LIMIT_BYTES = 56 * 1024 * 1024

F32 = jnp.float32
BF16 = jnp.bfloat16
MESH = pl.DeviceIdType.MESH
SDS = jax.ShapeDtypeStruct
NT = (((1,), (1,)), ((), ()))
TN = (((0,), (0,)), ((), ()))


def _params(*semantics):
    return pltpu.CompilerParams(dimension_semantics=semantics or None, vmem_limit_bytes=V7X_VMEM_LIMIT_BYTES)


def _mesh_pos():
    return lax.axis_index("x"), lax.axis_index("y"), lax.axis_index("c")


def _sigmoid(z):
    return 1.0 / (1.0 + jnp.exp(-z))


def _adamw(w, g, m, v):
    m = ADAM_B1 * m + (1.0 - ADAM_B1) * g
    v = ADAM_B2 * v + (1.0 - ADAM_B2) * (g * g)
    m_hat = m / (1.0 - ADAM_B1 ** ADAM_STEP)
    v_hat = v / (1.0 - ADAM_B2 ** ADAM_STEP)
    delta = -ADAM_LR * (m_hat / (jnp.sqrt(v_hat) + ADAM_EPS) + ADAM_WD * w)
    return delta, m, v


def _all_gather(name, blocks, memory_space):
    n_arr = len(blocks)

    def body(*refs):
        ins, outs = refs[:n_arr], refs[n_arr:2 * n_arr]
        send_sems, recv_sems, local_sems = refs[2 * n_arr:]
        x, y, c = _mesh_pos()
        me, sibling = (x, y, c), (x, y, 1 - c)
        chips = [(1 - x, y), (x, 1 - y), (1 - x, 1 - y)]

        def slot(p):
            return 4 * p[0] + 2 * p[1] + p[2]

        def copy(a, k, block, to, src=None):
            dst = outs[a].at[slot(block)]
            return pltpu.make_async_remote_copy(
                src_ref=dst if src is None else src, dst_ref=dst,
                send_sem=send_sems.at[a, k], recv_sem=recv_sems.at[a, k],
                device_id=to, device_id_type=MESH)

        mine = [pltpu.make_async_copy(ins[a], outs[a].at[slot(me)], local_sems.at[a]) for a in range(n_arr)]
        for cp in mine:
            cp.start()
        first = []
        for a in range(n_arr):
            first.append(copy(a, 0, me, sibling, src=ins[a]))
            first += [copy(a, 1 + j, me, (*chip, c), src=ins[a]) for j, chip in enumerate(chips)]
        for cp in first:
            cp.start()
        passed = []
        for j, chip in enumerate(chips):
            for a in range(n_arr):
                copy(a, 1 + j, (*chip, c), me).wait_recv()
                fwd = copy(a, 4 + j, (*chip, c), sibling)
                fwd.start()
                passed.append(fwd)
        for a in range(n_arr):
            copy(a, 0, sibling, me).wait_recv()
            for j, chip in enumerate(chips):
                copy(a, 4 + j, (*chip, 1 - c), me).wait_recv()
        for cp in first + passed:
            cp.wait_send()
        for cp in mine:
            cp.wait()

    spec = pl.BlockSpec(memory_space=memory_space)
    return pl.pallas_call(
        body, name=name,
        out_shape=[SDS((N_DEV,) + b.shape, b.dtype) for b in blocks],
        in_specs=[spec] * n_arr, out_specs=[spec] * n_arr,
        scratch_shapes=[pltpu.SemaphoreType.DMA((n_arr, 7)), pltpu.SemaphoreType.DMA((n_arr, 7)),
                        pltpu.SemaphoreType.DMA((n_arr,))],
        compiler_params=_params(),
    )(*blocks)


def _ada_exchange(c, w_ada, b_ada8, w_ada_f, b_ada_f8):
    n1, n2 = w_ada.shape[1], w_ada_f.shape[1]

    def body(c_ref, w1_ref, b1_ref, w2_ref, b2_ref, cact_ref, mod_ref, modf_ref,
             cact_buf, res1, res2, send1, send2, sems_s, sems_r):
        x, y, c_pos = _mesh_pos()
        me = 4 * x + 2 * y + c_pos
        flips = [(k >> 2 & 1, k >> 1 & 1, k & 1) for k in range(1, N_DEV)]

        def peer(f):
            return (1 - x if f[0] else x, 1 - y if f[1] else y, 1 - c_pos if f[2] else c_pos)

        cv = c_ref[...]
        cact = cv * _sigmoid(cv)
        cact_buf[...] = cact
        cact_ref[me] = cact

        def rdma(phase, k, src, dst, f):
            return pltpu.make_async_remote_copy(src_ref=src, dst_ref=dst, send_sem=sems_s.at[phase, k],
                                                recv_sem=sems_r.at[phase, k], device_id=peer(f), device_id_type=MESH)

        gather = [rdma(0, k, cact_buf, cact_ref.at[me], f) for k, f in enumerate(flips)]
        for cp in gather:
            cp.start()
        for cp in gather:
            cp.wait_recv()
        for cp in gather:
            cp.wait_send()

        rid = lax.broadcasted_iota(jnp.int32, (N_DEV, D_MODEL), 0)
        rows = jnp.zeros((N_DEV, D_MODEL), F32)
        for j in range(N_DEV):
            rows = jnp.where(rid == j, jnp.broadcast_to(cact_ref[j], (N_DEV, D_MODEL)), rows)
        rows = rows.astype(BF16)
        res1[...] = jnp.dot(rows, w1_ref[...].astype(BF16), preferred_element_type=F32) + b1_ref[pl.ds(me, 1), :]
        res2[...] = jnp.dot(rows, w2_ref[...].astype(BF16), preferred_element_type=F32) + b2_ref[pl.ds(me, 1), :]
        for j in range(N_DEV):
            send1[j] = res1[pl.ds(j, 1), :]
            send2[j] = res2[pl.ds(j, 1), :]
        mod_ref[me] = send1[me]
        modf_ref[me] = send2[me]
        scatter = []
        for k, f in enumerate(flips):
            to = me ^ (k + 1)
            scatter.append(rdma(1, k, send1.at[to], mod_ref.at[me], f))
            scatter.append(rdma(2, k, send2.at[to], modf_ref.at[me], f))
        for cp in scatter:
            cp.start()
        for cp in scatter:
            cp.wait_recv()
        for cp in scatter:
            cp.wait_send()

    vmem = pl.BlockSpec(memory_space=pltpu.VMEM)
    return pl.pallas_call(
        body, name="ada_exchange",
        out_shape=[SDS((N_DEV, 1, D_MODEL), F32), SDS((N_DEV, 1, n1), F32), SDS((N_DEV, 1, n2), F32)],
        in_specs=[vmem] * 5, out_specs=[vmem] * 3,
        scratch_shapes=[pltpu.VMEM((1, D_MODEL), F32), pltpu.VMEM((N_DEV, n1), F32), pltpu.VMEM((N_DEV, n2), F32),
                        pltpu.VMEM((N_DEV, 1, n1), F32), pltpu.VMEM((N_DEV, 1, n2), F32),
                        pltpu.SemaphoreType.DMA((3, 7)), pltpu.SemaphoreType.DMA((3, 7))],
        compiler_params=_params(),
    )(c, w_ada, b_ada8, w_ada_f, b_ada_f8)


def _chip_scatter(pair_ref, parts_ref, send_sems, recv_sems):
    x, y, c = _mesh_pos()
    chips = [(1 - x, y), (x, 1 - y), (1 - x, 1 - y)]
    return [pltpu.make_async_remote_copy(
        src_ref=pair_ref.at[2 * cx + cy], dst_ref=parts_ref.at[j], send_sem=send_sems.at[j], recv_sem=recv_sems.at[j],
        device_id=(cx, cy, c), device_id_type=MESH) for j, (cx, cy) in enumerate(chips)]


def _scatter_scratch():
    return [pltpu.SemaphoreType.DMA((3,)), pltpu.SemaphoreType.DMA((3,))]


def _prep_weights(me, wt, w_out):
    steps = 4

    def body(me_ref, wt_ref, wo_ref, wtb_ref, wob_ref):
        wtb_ref[...] = wt_ref[...].astype(BF16)
        wob_ref[...] = wo_ref[...].astype(BF16)

    def rows(a, mine):
        blk = (a.shape[0] // steps, a.shape[1])
        return pl.BlockSpec(blk, (lambda i, me_ref: (steps * me_ref[0] + i, 0)) if mine else (lambda i, me_ref: (i, 0)))

    return pl.pallas_call(
        body, name="prep_weights",
        grid_spec=pltpu.PrefetchScalarGridSpec(
            num_scalar_prefetch=1, grid=(steps,),
            in_specs=[rows(wt, False), rows(w_out, False)], out_specs=[rows(wt, True), rows(w_out, True)]),
        out_shape=[SDS((N_DEV * wt.shape[0], D_MODEL), BF16), SDS((N_DEV * w_out.shape[0], D_MODEL), BF16)],
        compiler_params=_params("parallel"),
    )(me, wt, w_out)


class _InPlaceGather:
    def __init__(self, buf_ref, send_sems, recv_sems, relay=False):
        self.buf, self.send_sems, self.recv_sems, self.relay = buf_ref, send_sems, recv_sems, relay
        self.n = buf_ref.shape[0] // N_DEV
        x, y, c = _mesh_pos()
        self.me, self.sibling, self.core = (x, y, c), (x, y, 1 - c), c
        self.chips = [(1 - x, y), (x, 1 - y), (1 - x, 1 - y)]
        self.relay_from = (jnp.where(c == 0, 1 - x, x), jnp.where(c == 0, y, 1 - y), c)
        self.relay_to = (jnp.where(c == 0, x, 1 - x), jnp.where(c == 0, 1 - y, y), c)

    def copy(self, k, block, to):
        start = pl.multiple_of((4 * block[0] + 2 * block[1] + block[2]) * self.n, self.n)
        rows = self.buf.at[pl.ds(start, self.n)]
        return pltpu.make_async_remote_copy(src_ref=rows, dst_ref=rows, send_sem=self.send_sems.at[k],
                                            recv_sem=self.recv_sems.at[k], device_id=to, device_id_type=MESH)

    def start(self):
        self.copy(0, self.me, self.sibling).start()
        for j, chip in enumerate(self.chips[:2] if self.relay else self.chips):
            self.copy(1 + j, self.me, (*chip, self.core)).start()

    def relay_diagonal(self):
        self.copy(3, self.relay_from, self.relay_to).start()

    def pass_on(self, j):
        self.copy(1 + j, (*self.chips[j], self.core), self.me).wait_recv()
        self.copy(4 + j, (*self.chips[j], self.core), self.sibling).start()

    def wait_sibling(self, k):
        self.copy(k, self.sibling, self.me).wait_recv()

    def wait_sends(self):
        for k in range(7):
            self.copy(k, self.me, self.sibling).wait_send()


def _gather_scratch():
    return [pltpu.SemaphoreType.DMA((7,)), pltpu.SemaphoreType.DMA((7,))]


def _gather_in_proj(order, x, shift, scale, norm_g, wt_all):
    s = x.shape[0]
    th, tm = min(512, s), min(1024, s)
    nh, ni = s // th, s // tm
    tn = D_IN // 4
    steps = nh + 4 * ni

    def body(order_ref, x_ref, shift_ref, scale_ref, g_ref, wt_in, h_ref, proj_ref, wt_ref,
             h_scr, w_buf, load_sems, send_sems, recv_sems):
        g = pl.program_id(0)
        gather = _InPlaceGather(wt_ref, send_sems, recv_sems, relay=True)

        def tile_load(slot, chip):
            return pltpu.make_async_copy(wt_ref.at[pl.ds(pl.multiple_of(chip * tn, tn), tn)], w_buf.at[slot],
                                         load_sems.at[slot])

        @pl.when(g == 0)
        def _():
            gather.start()

        @pl.when(g < nh)
        def _():
            xv = x_ref[...]
            r = lax.rsqrt(jnp.mean(xv * xv, axis=-1, keepdims=True) + NORM_EPS)
            hb = (((xv * r) * g_ref[...]) * (1.0 + scale_ref[...]) + shift_ref[...]).astype(BF16)
            h_ref[...] = hb
            h_scr[pl.ds(pl.multiple_of(g * th, th), th), :] = hb

        @pl.when(g == nh - 1)
        def _():
            gather.wait_sibling(0)
            tile_load(0, order_ref[0]).start()

        @pl.when(g >= nh)
        def _():
            t, i = (g - nh) // ni, (g - nh) % ni

            @pl.when(i == 0)
            def _():
                tile_load(t % 2, order_ref[t]).wait()

            @pl.when((i == ni - 1) & (t == 0))
            def _():
                gather.pass_on(0)
                gather.pass_on(1)
                gather.relay_diagonal()

            @pl.when((i == ni // 2) & (t == 2))
            def _():
                gather.pass_on(2)

            for j in range(3):
                @pl.when((i == ni - 1) & (t == j))
                def _():
                    gather.wait_sibling(4 + j)
                    tile_load((j + 1) % 2, order_ref[j + 1]).start()

            lhs = h_scr[pl.ds(pl.multiple_of(i * tm, tm), tm), :]
            proj_ref[...] = lax.dot_general(lhs, w_buf[t % 2], NT, preferred_element_type=F32).astype(BF16)

        @pl.when(g == steps - 1)
        def _():
            gather.wait_sends()

    def h_tile(g, order_ref):
        return (jnp.minimum(g, nh - 1), 0)

    def proj_tile(g, order_ref):
        mm = jnp.maximum(g - nh, 0)
        return (mm % ni, order_ref[mm // ni])

    row = pl.BlockSpec((1, D_MODEL), lambda g, order_ref: (0, 0))
    hbm = pl.BlockSpec(memory_space=pl.ANY)
    return pl.pallas_call(
        body, name="gather_in_proj",
        grid_spec=pltpu.PrefetchScalarGridSpec(
            num_scalar_prefetch=1, grid=(steps,),
            in_specs=[pl.BlockSpec((th, D_MODEL), h_tile), row, row, row, hbm],
            out_specs=[pl.BlockSpec((th, D_MODEL), h_tile), pl.BlockSpec((tm, tn), proj_tile), hbm],
            scratch_shapes=[pltpu.VMEM((s, D_MODEL), BF16), pltpu.VMEM((2, tn, D_MODEL), BF16),
                            pltpu.SemaphoreType.DMA((2,)), *_gather_scratch()]),
        out_shape=[SDS((s, D_MODEL), BF16), SDS((s, D_IN), BF16), SDS(wt_all.shape, BF16)],
        input_output_aliases={5: 2},
        compiler_params=_params("arbitrary"),
    )(order, x, shift, scale, norm_g, wt_all)


def _rope_freqs():
    inv_freq = ROPE_THETA ** (-jnp.arange(0, HEAD_DIM, 2, dtype=F32) / HEAD_DIM)
    return jnp.tile(inv_freq, 4).reshape(1, 128)


class _RopeTables:
    def __init__(self, freq_ref, rows_ref, state_ref, last_ref):
        self.freq, self.rows, self.state, self.last = freq_ref, rows_ref, state_ref, last_ref

    def start(self, block, direction):
        ang = lax.broadcasted_iota(jnp.int32, (CHUNK, 128), 0).astype(F32) * self.freq[...]
        self.rows[0] = jnp.cos(ang)
        self.rows[1] = jnp.sin(ang)
        base = jnp.asarray(block * CHUNK, dtype=F32) * self.freq[...]
        turn = float(direction * CHUNK) * self.freq[...]
        self.state[0:1, :] = jnp.cos(base)
        self.state[1:2, :] = jnp.sin(base)
        self.state[2:3, :] = jnp.cos(turn)
        self.state[3:4, :] = jnp.sin(turn)

    def step(self):
        c, s, ct, st = (self.state[k:k + 1, :] for k in range(4))
        self.state[0:1, :] = c * ct - s * st
        self.state[1:2, :] = s * ct + c * st

    def tables(self):
        c, s = self.state[0:1, :], self.state[1:2, :]
        cos = c * self.rows[0] - s * self.rows[1]
        sin = s * self.rows[0] + c * self.rows[1]
        first_half = (lax.broadcasted_iota(jnp.int32, (1, 128), 1) & (HEAD_DIM - 1)) < HEAD_DIM // 2
        return cos, jnp.where(first_half, -sin, 0.0), jnp.where(first_half, 0.0, sin)

    def keep(self, tabs):
        for k in range(3):
            self.last[k] = tabs[k]

    def kept(self):
        return tuple(self.last[k] for k in range(3))


def _rope_scratch():
    return [pltpu.VMEM((2, CHUNK, 128), F32), pltpu.VMEM((8, 128), F32), pltpu.VMEM((3, CHUNK, 128), F32)]


def _rope(v, cos, sin_lo, sin_hi):
    width = v.shape[1]
    rep = (1, width // 128)
    return (v * jnp.tile(cos, rep) + pltpu.roll(v, width - 32, 1) * jnp.tile(sin_lo, rep)
            + pltpu.roll(v, 32, 1) * jnp.tile(sin_hi, rep))


def _rope_bwd(d, cos, sin_lo, sin_hi):
    width = d.shape[1]
    rep = (1, width // 128)
    return (d * jnp.tile(cos, rep) + pltpu.roll(d * jnp.tile(sin_lo, rep), 32, 1)
            + pltpu.roll(d * jnp.tile(sin_hi, rep), width - 32, 1))


def _layer_norm(v, g, b):
    mu = jnp.mean(v, axis=-1, keepdims=True)
    vc = v - mu
    rstd = lax.rsqrt(jnp.mean(vc * vc, axis=-1, keepdims=True) + NORM_EPS)
    vhat = vc * rstd
    return vhat * g + b, vhat, rstd


def _set_tril(w_ref, out_ref):
    t = lax.broadcasted_iota(jnp.int32, (CHUNK, CHUNK), 0)
    tp = lax.broadcasted_iota(jnp.int32, (CHUNK, CHUNK), 1)
    for g in range(A_GROUPS):
        out_ref[g] = jnp.where(tp <= t, w_ref[g], 0.0).astype(BF16)


def _bias_columns(b_ref, out_ref):
    for g in range(A_GROUPS):
        out_ref[g] = jnp.broadcast_to(b_ref[pl.ds(g, 1), :], (CHUNK, CHUNK)).T


def _from_prev():
    r = lax.broadcasted_iota(jnp.int32, (CHUNK, 4 * CHUNK), 0)
    i = lax.broadcasted_iota(jnp.int32, (CHUNK, 4 * CHUNK), 1) & (CHUNK - 1)
    return r > i


def _set_unfold_masks(mask_ref):
    prev = _from_prev()
    mask_ref[0] = jnp.where(prev, 1.0, 0.0).astype(BF16)
    mask_ref[1] = jnp.where(prev, 0.0, 1.0).astype(BF16)


def _fold_band(t, from_prev):
    return jnp.where(from_prev, t[:CHUNK], t[CHUNK:])


def _unfold_band(t, mask_ref):
    return jnp.concatenate([t * mask_ref[0], t * mask_ref[1]], axis=0)


def _low_lanes():
    return lax.broadcasted_iota(jnp.int32, (1, 128), 1) < HEAD_DIM


def _stack_heads(pair_a, pair_b):
    lo = _low_lanes()
    return jnp.concatenate([jnp.where(lo, pair_a, 0.0), jnp.where(lo, 0.0, pair_a),
                            jnp.where(lo, pair_b, 0.0), jnp.where(lo, 0.0, pair_b)], axis=0).astype(BF16)


def _heads_to_lanes(per_group):
    rows = [t[:, r * CHUNK:(r + 1) * CHUNK] for t in per_group for r in range(4)]
    return jnp.concatenate(rows, axis=0).T


def _dup_kv_head(band, gk):
    pair = band[:, (gk // 2) * 128:(gk // 2 + 1) * 128]
    lo = _low_lanes()
    one = jnp.where(lo if gk % 2 == 0 else jnp.logical_not(lo), pair, 0.0)
    return (one + pltpu.roll(one, HEAD_DIM, 1)).astype(BF16)


def _fold_kv_head(dup_grad, gk):
    both = dup_grad + pltpu.roll(dup_grad, HEAD_DIM, 1)
    lo = _low_lanes()
    return jnp.where(lo if gk % 2 == 0 else jnp.logical_not(lo), both, 0.0)


def _attn_probs(q_st, k_dup, sink_row, from_prev, first_block):
    s = lax.dot_general(k_dup, q_st, NT, preferred_element_type=F32)
    no_prev = jnp.where(first_block, -jnp.inf, 0.0)
    s = jnp.where(from_prev, s[:CHUNK] + no_prev, s[CHUNK:])
    m = jnp.maximum(jnp.max(s, axis=0, keepdims=True), sink_row)
    p = jnp.exp(s - m)
    e_sink = jnp.exp(sink_row - m)
    inv = 1.0 / (jnp.sum(p, axis=0, keepdims=True) + e_sink)
    return p * inv, e_sink * inv


def _sink_row(sinks_ref, gk):
    return jnp.concatenate([jnp.full((1, CHUNK), sinks_ref[4 * gk + r], F32) for r in range(4)], axis=1)


def _mixer_specs(nb, rev):
    def blk(i):
        return nb - 1 - i if rev else i

    def prev(i):
        return jnp.maximum(blk(i) - 1, 0)

    return dict(
        cur=pl.BlockSpec((CHUNK, D_IN), lambda i, *_: (blk(i), 0)),
        prev_kv=pl.BlockSpec((CHUNK, 2 * 256), lambda i, *_: (prev(i), OFF_K // 512)),
        freq=pl.BlockSpec((1, 128), lambda i, *_: (0, 0)),
        vec=pl.BlockSpec((1, D_A), lambda i, *_: (0, 0)),
        wsp=pl.BlockSpec((A_GROUPS, CHUNK, CHUNK), lambda i, *_: (0, 0, 0)),
        bsp=pl.BlockSpec((A_GROUPS, CHUNK), lambda i, *_: (0, 0)),
        smem=pl.BlockSpec(memory_space=pltpu.SMEM),
        blk=blk,
    )


def _mixer_fwd(proj, freqs, ln_g, ln_b, w_sp, b_sp, sinks, wo_all):
    s = proj.shape[0]
    nb = s // CHUNK
    sp = _mixer_specs(nb, rev=False)

    def body(cur_ref, pkv_ref, freq_ref, lg_ref, lb_ref, w_ref, b_ref, sinks_ref, wo_in, y_ref, wo_ref,
             bcol, wtril, mask, rope_rows, rope_state, rope_last, send_sems, recv_sems):
        i = pl.program_id(0)
        gather = _InPlaceGather(wo_ref, send_sems, recv_sems)
        rope = _RopeTables(freq_ref, rope_rows, rope_state, rope_last)

        @pl.when(i == 0)
        def _():
            gather.start()
            _bias_columns(b_ref, bcol)
            _set_tril(w_ref, wtril)
            _set_unfold_masks(mask)
            rope.start(-1, 1)
            rope_last[...] = jnp.zeros_like(rope_last)

        @pl.when(i == (3 * nb) // 4)
        def _():
            for j in range(3):
                gather.pass_on(j)

        vln, _, _ = _layer_norm(cur_ref[:, OFF_VA:OFF_ZA].astype(F32), lg_ref[...], lb_ref[...])
        vln = vln.astype(BF16)

        def gating_group(g):
            cols = slice(g * 128, (g + 1) * 128)
            sg = jnp.dot(wtril[g], vln[:, cols], preferred_element_type=F32) + bcol[g]
            u = cur_ref[:, OFF_U + g * 128:OFF_U + (g + 1) * 128].astype(F32)
            z = cur_ref[:, OFF_ZA + g * 128:OFF_ZA + (g + 1) * 128].astype(F32)
            y_ref[:, cols] = (u * sg * (z * _sigmoid(z))).astype(BF16)

        rope.step()
        cur_t, prev_t = rope.tables(), rope.kept()
        rope.keep(cur_t)
        qr = _rope(cur_ref[:, OFF_Q:OFF_K].astype(F32), *cur_t) * ATTN_SCALE
        kr = jnp.concatenate([_rope(pkv_ref[:, 0:256].astype(F32), *prev_t),
                              _rope(cur_ref[:, OFF_K:OFF_V].astype(F32), *cur_t)], axis=0)
        v_t = jnp.concatenate([pkv_ref[:, 256:512], cur_ref[:, OFF_V:OFF_ZB]], axis=0).astype(F32).T.astype(BF16)
        outs = []
        from_prev = _from_prev()
        for gk in range(N_KV_HEADS):
            q_st = _stack_heads(qr[:, (2 * gk) * 128:(2 * gk + 1) * 128], qr[:, (2 * gk + 1) * 128:(2 * gk + 2) * 128])
            probs, _ = _attn_probs(q_st, _dup_kv_head(kr, gk), _sink_row(sinks_ref, gk), from_prev, i == 0)
            outs.append(jnp.dot(v_t[gk * HEAD_DIM:(gk + 1) * HEAD_DIM], _unfold_band(probs.astype(BF16), mask),
                                preferred_element_type=F32))
            gating_group(2 * gk)
            gating_group(2 * gk + 1)
        zb = cur_ref[:, OFF_ZB:D_IN].astype(F32)
        y_ref[:, D_A:D_MODEL] = (_heads_to_lanes(outs) * (zb * _sigmoid(zb))).astype(BF16)

        @pl.when(i == nb - 1)
        def _():
            gather.wait_sibling(0)
            for j in range(3):
                gather.wait_sibling(4 + j)
            gather.wait_sends()

    hbm = pl.BlockSpec(memory_space=pl.ANY)
    return pl.pallas_call(
        body, name="mixer_fwd", grid=(nb,),
        in_specs=[sp["cur"], sp["prev_kv"], sp["freq"], sp["vec"], sp["vec"], sp["wsp"], sp["bsp"], sp["smem"], hbm],
        out_specs=[pl.BlockSpec((CHUNK, D_MODEL), lambda i: (i, 0)), hbm],
        out_shape=[SDS((s, D_MODEL), BF16), SDS(wo_all.shape, wo_all.dtype)],
        scratch_shapes=[pltpu.VMEM((A_GROUPS, CHUNK, CHUNK), F32), pltpu.VMEM((A_GROUPS, CHUNK, CHUNK), BF16),
                        pltpu.VMEM((2, CHUNK, 4 * CHUNK), BF16), *_rope_scratch(), *_gather_scratch()],
        input_output_aliases={8: 1},
        compiler_params=_params("arbitrary"),
    )(proj, proj, freqs, ln_g, ln_b, w_sp, b_sp, sinks, wo_all)


def _out_proj_loss(y, x, target, wo, gate, shift_f, scale_f, fng):
    s = y.shape[0]
    tm, tr = 256, 128
    nt = s // tm

    def body(y_ref, x_ref, t_ref, wo_ref, gate_ref, sh_ref, sc_ref, g_ref, dx1_ref, do_ref, dy_ref, sums_ref,
             do_last, do_work):
        i = pl.program_id(0)

        @pl.when(i == 0)
        def _():
            sums_ref[...] = jnp.zeros_like(sums_ref)
            do_last[...] = jnp.zeros_like(do_last)

        do_work[...] = do_last[...]
        o = jnp.dot(y_ref[...], wo_ref[...], preferred_element_type=F32)
        gate, g, sh = gate_ref[...], g_ref[...], sh_ref[...]
        one_sc = 1.0 + sc_ref[...]
        cs, inv_d = g * one_sc, 1.0 / D_MODEL

        def rowsum(v):
            return jnp.sum(v, axis=0, keepdims=True)

        sums = [jnp.zeros((1, D_MODEL), F32) for _ in range(4)]
        for c in range(tm // tr):
            rows = slice(c * tr, (c + 1) * tr)
            oc = o[rows]
            x1 = x_ref[rows, :] + gate * oc
            r = lax.rsqrt(jnp.sum(x1 * x1, axis=-1, keepdims=True) * inv_d + NORM_EPS)
            x1n = x1 * r
            diff = x1n * cs + sh - t_ref[rows, :]
            w = diff * x1n
            lane_sum = jnp.sum(w * cs, axis=-1, keepdims=True)
            dx1 = (diff * cs) * (r * inv_d) - x1n * (r * lane_sum * (inv_d * inv_d))
            dx1_ref[rows, :] = dx1
            do = (dx1 * gate).astype(BF16)
            do_ref[rows, :] = do
            do_last[rows, :] = do
            for k, v in enumerate((dx1 * oc, diff, w, diff * diff)):
                sums[k] = sums[k] + rowsum(v)
        live = jnp.where(i < nt, 1.0, 0.0)
        for row, v in ((SUM_GATE, sums[0]), (SUM_SHIFT_F, inv_d * sums[1]), (SUM_SCALE_F, inv_d * (sums[2] * g)),
                       (SUM_FNG, inv_d * (sums[2] * one_sc)), (SUM_SQ_ERR, sums[3])):
            sums_ref[row:row + 1, :] += live * v
        dy_ref[...] = lax.dot_general(do_work[...], wo_ref[...], NT, preferred_element_type=F32).astype(BF16)

    tile = pl.BlockSpec((tm, D_MODEL), lambda i: (jnp.minimum(i, nt - 1), 0))
    row = pl.BlockSpec((1, D_MODEL), lambda i: (0, 0))
    return pl.pallas_call(
        body, name="out_proj_loss", grid=(nt + 1,),
        in_specs=[tile, tile, tile, pl.BlockSpec((D_MODEL, D_MODEL), lambda i: (0, 0)), row, row, row, row],
        out_specs=[tile, tile, pl.BlockSpec((tm, D_MODEL), lambda i: (jnp.maximum(i - 1, 0), 0)),
                   pl.BlockSpec((8, D_MODEL), lambda i: (0, 0))],
        out_shape=[SDS((s, D_MODEL), F32), SDS((s, D_MODEL), BF16), SDS((s, D_MODEL), BF16), SDS((8, D_MODEL), F32)],
        scratch_shapes=[pltpu.VMEM((tm, D_MODEL), BF16), pltpu.VMEM((tm, D_MODEL), BF16)],
        compiler_params=_params("arbitrary"),
    )(y, x, target, wo, gate, shift_f, scale_f, fng)


ROW_DBSP, ROW_DSINKS, MISC_ROWS = 0, 8, 32


def _mixer_bwd(me, proj, dy, freqs, ln_g, ln_b, w_sp, b_sp, sinks, pair):
    s = proj.shape[0]
    nb = s // CHUNK
    sp = _mixer_specs(nb, rev=True)

    def body(me_ref, cur_ref, pkv_ref, dy_ref, freq_ref, lg_ref, lb_ref, w_ref, b_ref, sinks_ref, pair_ref,
             dproj_ref, dln_ref, dw_ref, misc_ref, parts_ref, bcol, wtril, dbcol, carry, mask, rope_rows, rope_state,
             rope_last, send_sems, recv_sems):
        i = pl.program_id(0)
        block = nb - 1 - i
        rope = _RopeTables(freq_ref, rope_rows, rope_state, rope_last)

        @pl.when(i == 0)
        def _():
            for cp in _chip_scatter(pair_ref, parts_ref, send_sems, recv_sems):
                cp.start()
            _bias_columns(b_ref, bcol)
            _set_tril(w_ref, wtril)
            _set_unfold_masks(mask)
            rope.start(nb - 1, -1)
            rope.keep(rope.tables())
            dbcol[...] = jnp.zeros_like(dbcol)
            carry[...] = jnp.zeros_like(carry)
            dln_ref[...] = jnp.zeros_like(dln_ref)
            dw_ref[...] = jnp.zeros_like(dw_ref)
            misc_ref[...] = jnp.zeros_like(misc_ref)

        vln, vhat, rstd = _layer_norm(cur_ref[:, OFF_VA:OFF_ZA].astype(F32), lg_ref[...], lb_ref[...])
        vln = vln.astype(BF16)
        d_vln = []

        def gating_group(g):
            cols = slice(g * 128, (g + 1) * 128)
            w_g = wtril[g]
            sg = jnp.dot(w_g, vln[:, cols], preferred_element_type=F32) + bcol[g]
            u = cur_ref[:, OFF_U + g * 128:OFF_U + (g + 1) * 128].astype(F32)
            z = cur_ref[:, OFF_ZA + g * 128:OFF_ZA + (g + 1) * 128].astype(F32)
            dya = dy_ref[:, cols].astype(F32)
            sig = _sigmoid(z)
            d_ya = dya * (z * sig)
            dproj_ref[:, OFF_ZA + g * 128:OFF_ZA + (g + 1) * 128] = (
                dya * (u * sg) * (sig * (1.0 + z * (1.0 - sig)))).astype(BF16)
            dproj_ref[:, OFF_U + g * 128:OFF_U + (g + 1) * 128] = (d_ya * sg).astype(BF16)
            d_s = d_ya * u
            dbcol[g] += d_s
            d_sb = d_s.astype(BF16)
            dw_ref[g] += lax.dot_general(d_sb, vln[:, cols], NT, preferred_element_type=F32)
            d_vln.append(lax.dot_general(w_g, d_sb, TN, preferred_element_type=F32))

        cur_t = rope.kept()
        rope.step()
        prev_t = rope.tables()
        rope.keep(prev_t)
        band_t = tuple(jnp.concatenate([p, c], axis=0) for p, c in zip(prev_t, cur_t))
        qr = _rope(cur_ref[:, OFF_Q:OFF_K].astype(F32), *cur_t) * ATTN_SCALE
        kr = jnp.concatenate([_rope(pkv_ref[:, 0:256].astype(F32), *prev_t),
                              _rope(cur_ref[:, OFF_K:OFF_V].astype(F32), *cur_t)], axis=0)
        vb = jnp.concatenate([pkv_ref[:, 256:512], cur_ref[:, OFF_V:OFF_ZB]], axis=0).astype(F32)
        k_t, v_t = (kr.T * ATTN_SCALE).astype(BF16), vb.T.astype(BF16)
        zb = cur_ref[:, OFF_ZB:D_IN].astype(F32)
        dyb = dy_ref[:, D_A:D_MODEL].astype(F32)
        sig = _sigmoid(zb)
        d_yb = dyb * (zb * sig)
        outs, dqs = [], []
        dk_pairs = [jnp.zeros((2 * CHUNK, 128), F32) for _ in range(2)]
        dv_pairs = [jnp.zeros((2 * CHUNK, 128), F32) for _ in range(2)]
        from_prev = _from_prev()
        for gk in range(N_KV_HEADS):
            heads = slice(gk * HEAD_DIM, (gk + 1) * HEAD_DIM)
            q_st = _stack_heads(qr[:, (2 * gk) * 128:(2 * gk + 1) * 128], qr[:, (2 * gk + 1) * 128:(2 * gk + 2) * 128])
            k_dup, v_dup = _dup_kv_head(kr, gk), _dup_kv_head(vb, gk)
            probs, p_sink = _attn_probs(q_st, k_dup, _sink_row(sinks_ref, gk), from_prev, block == 0)
            probs_b = _unfold_band(probs.astype(BF16), mask)
            outs.append(jnp.dot(v_t[heads], probs_b, preferred_element_type=F32))
            do_st = _stack_heads(d_yb[:, (2 * gk) * 128:(2 * gk + 1) * 128], d_yb[:, (2 * gk + 1) * 128:(2 * gk + 2) * 128])
            dp = _fold_band(lax.dot_general(v_dup, do_st, NT, preferred_element_type=F32), from_prev)
            delta = jnp.sum(probs * dp, axis=0, keepdims=True)
            ds = _unfold_band((probs * (dp - delta)).astype(BF16), mask)
            d_sink = -p_sink * delta
            for r in range(4):
                row = ROW_DSINKS + 4 * gk + r
                misc_ref[row:row + 1, :] += jnp.broadcast_to(
                    jnp.sum(d_sink[:, r * CHUNK:(r + 1) * CHUNK], axis=1, keepdims=True), (1, 128))
            dqs.append(jnp.dot(k_t[heads], ds, preferred_element_type=F32))
            dk_pairs[gk // 2] += _fold_kv_head(jnp.dot(ds, q_st, preferred_element_type=F32), gk)
            dv_pairs[gk // 2] += _fold_kv_head(jnp.dot(probs_b, do_st, preferred_element_type=F32), gk)
            gating_group(2 * gk)
            gating_group(2 * gk + 1)
        d_vln = jnp.concatenate(d_vln, axis=1)
        dln_ref[0:1, :] += jnp.sum(d_vln * vhat, axis=0, keepdims=True)
        dln_ref[1:2, :] += jnp.sum(d_vln, axis=0, keepdims=True)
        d_vhat = d_vln * lg_ref[...]
        d_va = rstd * (d_vhat - jnp.mean(d_vhat, axis=-1, keepdims=True)
                       - vhat * jnp.mean(d_vhat * vhat, axis=-1, keepdims=True))
        dproj_ref[:, OFF_VA:OFF_ZA] = d_va.astype(BF16)
        dproj_ref[:, OFF_ZB:D_IN] = (dyb * _heads_to_lanes(outs) * (sig * (1.0 + zb * (1.0 - sig)))).astype(BF16)
        dproj_ref[:, OFF_Q:OFF_K] = _rope_bwd(_heads_to_lanes(dqs), *cur_t).astype(BF16)
        dk_band = _rope_bwd(jnp.concatenate(dk_pairs, axis=1), *band_t)
        dv_band = jnp.concatenate(dv_pairs, axis=1)
        dproj_ref[:, OFF_K:OFF_V] = (dk_band[CHUNK:] + carry[:, 0:256]).astype(BF16)
        dproj_ref[:, OFF_V:OFF_ZB] = (dv_band[CHUNK:] + carry[:, 256:512]).astype(BF16)
        carry[:, 0:256] = dk_band[:CHUNK]
        carry[:, 256:512] = dv_band[:CHUNK]

        @pl.when(i == nb - 1)
        def _():
            t = lax.broadcasted_iota(jnp.int32, (CHUNK, CHUNK), 0)
            tp = lax.broadcasted_iota(jnp.int32, (CHUNK, CHUNK), 1)
            for g in range(A_GROUPS):
                dw_ref[g] = jnp.where(tp <= t, dw_ref[g], 0.0)
                misc_ref[pl.ds(ROW_DBSP + g, 1), :] = jnp.sum(dbcol[g].T, axis=0, keepdims=True)
            scatter = _chip_scatter(pair_ref, parts_ref, send_sems, recv_sems)
            for cp in scatter:
                cp.wait_recv()
            for cp in scatter:
                cp.wait_send()

    blk = sp["blk"]
    hbm = pl.BlockSpec(memory_space=pl.ANY)
    return pl.pallas_call(
        body, name="mixer_bwd",
        grid_spec=pltpu.PrefetchScalarGridSpec(
            num_scalar_prefetch=1, grid=(nb,),
            in_specs=[sp["cur"], sp["prev_kv"], pl.BlockSpec((CHUNK, D_MODEL), lambda i, me_ref: (blk(i), 0)),
                      sp["freq"], sp["vec"], sp["vec"], sp["wsp"], sp["bsp"], sp["smem"], hbm],
            out_specs=[pl.BlockSpec((CHUNK, D_IN), lambda i, me_ref: (blk(i), 0)),
                       pl.BlockSpec((8, D_A), lambda i, me_ref: (me_ref[0], 0)),
                       pl.BlockSpec((A_GROUPS, CHUNK, CHUNK), lambda i, me_ref: (me_ref[0], 0, 0)),
                       pl.BlockSpec((MISC_ROWS, 128), lambda i, me_ref: (me_ref[0], 0)), hbm],
            scratch_shapes=[pltpu.VMEM((A_GROUPS, CHUNK, CHUNK), F32), pltpu.VMEM((A_GROUPS, CHUNK, CHUNK), BF16),
                            pltpu.VMEM((A_GROUPS, CHUNK, CHUNK), F32), pltpu.VMEM((CHUNK, 512), F32), pltpu.VMEM((2, CHUNK, 4 * CHUNK), BF16),
                            *_rope_scratch(), *_scatter_scratch()]),
        out_shape=[SDS((s, D_IN), BF16), SDS((N_DEV * 8, D_A), F32), SDS((N_DEV * A_GROUPS, CHUNK, CHUNK), F32),
                   SDS((N_DEV * MISC_ROWS, 128), F32), SDS((3,) + pair.shape[1:], pair.dtype)],
        compiler_params=_params("arbitrary"),
    )(me, proj, proj, dy, freqs, ln_g, ln_b, w_sp, b_sp, sinks, pair)


def _wgrad_pair(name, a, b, bt, gathers=()):
    s, m = a.shape
    n = b.shape[1]
    bm, half = m // 4, m // 8
    bt = min(bt, s)
    steps = s // bt
    last = 4 * steps
    n_g = len(gathers)

    def body(*refs):
        a_ref, b_ref = refs[:2]
        out_ref, bufs = refs[2 + n_g], refs[3 + n_g:3 + 2 * n_g]
        acc, kept, got, sent, send_sems, recv_sems = refs[3 + 2 * n_g:9 + 2 * n_g]
        sems = refs[9 + 2 * n_g:]
        g = pl.program_id(0)
        tile, t = g // steps, g % steps
        mx, my, mc = _mesh_pos()
        jobs = [_InPlaceGather(bufs[k], sems[2 * k], sems[2 * k + 1]) for k in range(n_g)]

        def exchange(q):
            return pltpu.make_async_remote_copy(src_ref=sent, dst_ref=got.at[q % 2], send_sem=send_sems.at[q],
                                                recv_sem=recv_sems.at[q], device_id=(mx, my, 1 - mc),
                                                device_id_type=MESH)

        @pl.when(g == 0)
        def _():
            for job in jobs:
                job.start()

        @pl.when(g == 2 * steps)
        def _():
            for job in jobs:
                for j in range(3):
                    job.pass_on(j)

        @pl.when(g < last)
        def _():
            prod = lax.dot_general(a_ref[...], b_ref[...], TN, preferred_element_type=F32)

            @pl.when(t == 0)
            def _():
                acc[...] = prod

            @pl.when(t > 0)
            def _():
                acc[...] += prod

            @pl.when(t == steps - 1)
            def _():
                @pl.when(tile > 0)
                def _():
                    exchange(tile - 1).wait_send()

                kept[tile % 2] = acc[pl.ds(pl.multiple_of(mc * half, 8), half), :].astype(BF16)
                sent[...] = acc[pl.ds(pl.multiple_of((1 - mc) * half, 8), half), :].astype(BF16)
                exchange(tile).start()

        @pl.when((t == 0) & (g > 0))
        def _():
            q = tile - 1
            exchange(q).wait_recv()
            out_ref[0] = (kept[q % 2].astype(F32) + got[q % 2].astype(F32)).astype(BF16)

        @pl.when(g == last)
        def _():
            exchange(3).wait_send()
            for job in jobs:
                job.wait_sibling(0)
                for j in range(3):
                    job.wait_sibling(4 + j)
                job.wait_sends()

    def a_tile(g):
        gg = jnp.minimum(g, last - 1)
        return (gg % steps, gg // steps)

    def b_tile(g):
        return (jnp.minimum(g, last - 1) % steps, 0)

    hbm = pl.BlockSpec(memory_space=pl.ANY)
    outs = pl.pallas_call(
        body, name=name, grid=(last + 1,),
        in_specs=[pl.BlockSpec((bt, bm), a_tile), pl.BlockSpec((bt, n), b_tile)] + [hbm] * n_g,
        out_specs=[pl.BlockSpec((1, half, n), lambda g: (jnp.maximum(g - 1, 0) // steps, 0, 0))] + [hbm] * n_g,
        out_shape=[SDS((4, half, n), BF16)] + [SDS(gb.shape, gb.dtype) for gb in gathers],
        scratch_shapes=[pltpu.VMEM((bm, n), F32), pltpu.VMEM((2, half, n), BF16), pltpu.VMEM((2, half, n), BF16),
                        pltpu.VMEM((half, n), BF16), pltpu.SemaphoreType.DMA((4,)), pltpu.SemaphoreType.DMA((4,))]
        + _gather_scratch() * n_g,
        input_output_aliases={2 + k: 1 + k for k in range(n_g)},
        compiler_params=_params("arbitrary"),
    )(a, b, *gathers)
    return outs[0], outs[1:]


def _in_proj_bwd(dproj, wt, x, dx1, scale, norm_g, sums_o, pair):
    s = x.shape[0]
    tm, tk, tr = min(1024, s), D_IN // 4, 64
    ksteps = D_IN // tk

    def body(dp_ref, wt_ref, x_hbm, dx1_hbm, sc_ref, g_ref, so_ref, pair_ref, gx_ref, sums_ref, parts_ref, x_buf,
             dx1_buf, tile_sems, send_sems, recv_sems):
        i, k = pl.program_id(0), pl.program_id(1)

        def tile_copies():
            rows = pl.ds(pl.multiple_of(i * tm, tm), tm)
            return (pltpu.make_async_copy(x_hbm.at[rows], x_buf, tile_sems.at[0]),
                    pltpu.make_async_copy(dx1_hbm.at[rows], dx1_buf, tile_sems.at[1]))

        @pl.when((i == 0) & (k == 0))
        def _():
            for cp in _chip_scatter(pair_ref, parts_ref, send_sems, recv_sems):
                cp.start()
            sums_ref[...] = so_ref[...]

        @pl.when(k == 0)
        def _():
            for cp in tile_copies():
                cp.start()
            gx_ref[...] = jnp.dot(dp_ref[...], wt_ref[...], preferred_element_type=F32)

        @pl.when(k > 0)
        def _():
            gx_ref[...] += jnp.dot(dp_ref[...], wt_ref[...], preferred_element_type=F32)

        @pl.when(k == ksteps - 1)
        def _():
            for cp in tile_copies():
                cp.wait()
            one_sc, g = 1.0 + sc_ref[...], g_ref[...]
            cs = one_sc * g

            def chunk(j, sums):
                rows = pl.ds(pl.multiple_of(j * tr, tr), tr)
                dh, xv = gx_ref[rows, :], x_buf[rows, :]
                dhx = dh * xv
                r = lax.rsqrt(jnp.sum(xv * xv, axis=-1, keepdims=True) * (1.0 / D_MODEL) + NORM_EPS)
                coef = (r * r * r) * (jnp.sum(dhx * cs, axis=-1, keepdims=True) * (1.0 / D_MODEL))
                gx_ref[rows, :] = dx1_buf[rows, :] + r * (dh * cs) - xv * coef
                return (sums[0] + jnp.sum(dh, axis=0, keepdims=True), sums[1] + jnp.sum(dhx * r, axis=0, keepdims=True))

            zero = jnp.zeros((1, D_MODEL), F32)
            sums = lax.fori_loop(0, tm // tr, chunk, (zero, zero))
            sums_ref[SUM_SHIFT:SUM_SHIFT + 1, :] += sums[0]
            sums_ref[SUM_SCALE:SUM_SCALE + 1, :] += sums[1] * g
            sums_ref[SUM_NORM_G:SUM_NORM_G + 1, :] += sums[1] * one_sc

        @pl.when((i == s // tm - 1) & (k == ksteps - 1))
        def _():
            scatter = _chip_scatter(pair_ref, parts_ref, send_sems, recv_sems)
            for cp in scatter:
                cp.wait_recv()
            for cp in scatter:
                cp.wait_send()

    row = pl.BlockSpec((1, D_MODEL), lambda i, k: (0, 0))
    hbm = pl.BlockSpec(memory_space=pl.ANY)
    return pl.pallas_call(
        body, name="in_proj_bwd", grid=(s // tm, ksteps),
        in_specs=[pl.BlockSpec((tm, tk), lambda i, k: (i, k)), pl.BlockSpec((tk, D_MODEL), lambda i, k: (k, 0)),
                  hbm, hbm, row, row, pl.BlockSpec((8, D_MODEL), lambda i, k: (0, 0)), hbm],
        out_specs=[pl.BlockSpec((tm, D_MODEL), lambda i, k: (i, 0)), pl.BlockSpec((8, D_MODEL), lambda i, k: (0, 0)),
                   hbm],
        out_shape=[SDS((s, D_MODEL), F32), SDS((8, D_MODEL), F32), SDS((3,) + pair.shape[1:], pair.dtype)],
        scratch_shapes=[pltpu.VMEM((tm, D_MODEL), F32), pltpu.VMEM((tm, D_MODEL), F32),
                        pltpu.SemaphoreType.DMA((2,)), *_scatter_scratch()],
        compiler_params=_params("arbitrary", "arbitrary"),
    )(dproj, wt, x, dx1, scale, norm_g, sums_o, pair)


def _sum_chips(own_ref, parts_ref):
    return ((own_ref[0].astype(F32) + parts_ref[0].astype(F32)) + parts_ref[1].astype(F32)) + parts_ref[2].astype(F32)


def _adam_rows(name, chip, pair, parts, w, m, v, tr):
    rows = w.shape[0]

    def body(chip_ref, own_ref, p_ref, w_ref, m_ref, v_ref, g_ref, d_ref, nm_ref, nv_ref):
        g = _sum_chips(own_ref, p_ref)
        g_ref[...] = g
        d_ref[...], nm_ref[...], nv_ref[...] = _adamw(w_ref[...], g, m_ref[...], v_ref[...])

    blk = pl.BlockSpec((tr, D_MODEL), lambda j, chip_ref: (j, 0))
    return pl.pallas_call(
        body, name=name,
        grid_spec=pltpu.PrefetchScalarGridSpec(
            num_scalar_prefetch=1, grid=(rows // tr,),
            in_specs=[pl.BlockSpec((1, tr, D_MODEL), lambda j, chip_ref: (chip_ref[0], j, 0)),
                      pl.BlockSpec((3, tr, D_MODEL), lambda j, chip_ref: (0, j, 0)), blk, blk, blk],
            out_specs=[blk] * 4),
        out_shape=[SDS(w.shape, F32)] * 4, compiler_params=_params("parallel"),
    )(chip, pair, parts, w, m, v)


def _adam_ada(name, cact, dmod, w, m, v):
    n = w.shape[1]
    tr = 512

    def body(c_ref, dm_ref, w_ref, m_ref, v_ref, g_ref, d_ref, nm_ref, nv_ref):
        pad_c = jnp.concatenate([c_ref[...], jnp.zeros_like(c_ref)], axis=0).astype(BF16)
        pad_d = jnp.concatenate([dm_ref[...], jnp.zeros_like(dm_ref)], axis=0).astype(BF16)
        g = lax.dot_general(pad_c, pad_d, TN, preferred_element_type=F32)
        g_ref[...] = g
        d_ref[...], nm_ref[...], nv_ref[...] = _adamw(w_ref[...], g, m_ref[...], v_ref[...])

    blk = pl.BlockSpec((tr, n), lambda j: (j, 0))
    return pl.pallas_call(
        body, name=name, grid=(D_MODEL // tr,),
        in_specs=[pl.BlockSpec((N_DEV, tr), lambda j: (0, j)), pl.BlockSpec((N_DEV, n), lambda j: (0, 0)),
                  blk, blk, blk],
        out_specs=[blk] * 4, out_shape=[SDS(w.shape, F32)] * 4,
        compiler_params=_params("parallel"),
    )(cact, dmod, w, m, v)


SMALL_PARAMS = ("w_spatial", "b_spatial", "sinks", "norm_g", "ln_v_g", "ln_v_b", "final_norm_g", "b_ada", "b_ada_final")


def _adam_small(d_wsp, misc, d_ln, sums, params):
    n_p = len(SMALL_PARAMS)

    def body(*refs):
        wsp_ref, misc_ref, ln_ref, sums_ref = refs[:4]
        wmv = [refs[4 + 3 * k:7 + 3 * k] for k in range(n_p)]
        loss_ref = refs[4 + 3 * n_p]
        outs = [refs[5 + 3 * n_p + 4 * k:9 + 3 * n_p + 4 * k] for k in range(n_p)]

        def column_sum(row):
            return total(sums_ref, (row, row + 1))

        def total(ref, rows=None):
            def part(j):
                return ref[j] if rows is None else ref[j, rows[0]:rows[1], :]
            acc = part(0)
            for j in range(1, N_DEV):
                acc = acc + part(j)
            return acc

        sink_rows = total(misc_ref, (ROW_DSINKS, ROW_DSINKS + 16))
        diag = (lax.broadcasted_iota(jnp.int32, (16, 128), 0) == lax.broadcasted_iota(jnp.int32, (16, 128), 1))
        grads = dict(
            w_spatial=total(wsp_ref), b_spatial=total(misc_ref, (ROW_DBSP, ROW_DBSP + A_GROUPS)),
            sinks=jnp.sum(jnp.where(diag, sink_rows, 0.0), axis=0, keepdims=True),
            norm_g=column_sum(SUM_NORM_G), ln_v_g=total(ln_ref, (0, 1)), ln_v_b=total(ln_ref, (1, 2)),
            final_norm_g=column_sum(SUM_FNG),
            b_ada=jnp.concatenate([column_sum(SUM_SHIFT), column_sum(SUM_SCALE), column_sum(SUM_GATE)], axis=1),
            b_ada_final=jnp.concatenate([column_sum(SUM_SHIFT_F), column_sum(SUM_SCALE_F)], axis=1))
        sq_err = jnp.sum(column_sum(SUM_SQ_ERR), axis=1, keepdims=True)
        loss_ref[...] = jnp.broadcast_to(sq_err * (0.5 / D_MODEL), (1, 128))
        for k, name in enumerate(SMALL_PARAMS):
            w_ref, m_ref, v_ref = wmv[k]
            g_ref, d_ref, nm_ref, nv_ref = outs[k]
            g_ref[...] = grads[name]
            d_ref[...], nm_ref[...], nv_ref[...] = _adamw(w_ref[...], grads[name], m_ref[...], v_ref[...])

    flat = [a for name in SMALL_PARAMS for a in params[name]]
    vmem = pl.BlockSpec(memory_space=pltpu.VMEM)
    out_shape = [SDS((1, 128), F32)] + [SDS(params[name][0].shape, F32) for name in SMALL_PARAMS for _ in range(4)]
    outs = pl.pallas_call(
        body, name="adam_small", in_specs=[vmem] * (4 + len(flat)), out_specs=[vmem] * len(out_shape),
        out_shape=out_shape, compiler_params=_params(),
    )(d_wsp, misc, d_ln, sums, *flat)
    return outs[0], {name: outs[1 + 4 * k:5 + 4 * k] for k, name in enumerate(SMALL_PARAMS)}


def kernel(x, c, w_ada, b_ada, norm_g, w_in, ln_v_g, ln_v_b, w_spatial, b_spatial, sinks, w_out, w_ada_final, b_ada_final, final_norm_g, loss_target, m_w_ada, m_b_ada, m_norm_g, m_w_in, m_ln_v_g, m_ln_v_b, m_w_spatial, m_b_spatial, m_sinks, m_w_out, m_w_ada_final, m_b_ada_final, m_final_norm_g, v_w_ada, v_b_ada, v_norm_g, v_w_in, v_ln_v_g, v_ln_v_b, v_w_spatial, v_b_spatial, v_sinks, v_w_out, v_w_ada_final, v_b_ada_final, v_final_norm_g):
    me = 4 * lax.axis_index("x") + 2 * lax.axis_index("y") + lax.axis_index("c")
    x2, tgt = x[0], loss_target[0]
    fng = final_norm_g.reshape(1, D_MODEL)

    n_ada, n_ada_f = w_ada.shape[2], w_ada_final.shape[1]
    cact, mod, mod_f = _ada_exchange(c, w_ada[0], b_ada.reshape(N_DEV, n_ada), w_ada_final,
                                     b_ada_final.reshape(N_DEV, n_ada_f))
    cact = cact.reshape(N_DEV, D_MODEL)
    mod, mod_f = mod.reshape(1, 3 * D_MODEL), mod_f.reshape(1, 2 * D_MODEL)
    shift, scale, gate = mod[:, :D_MODEL], mod[:, D_MODEL:2 * D_MODEL], mod[:, 2 * D_MODEL:]
    shift_f, scale_f = mod_f[:, :D_MODEL], mod_f[:, D_MODEL:]

    wt_f32, m_wt, v_wt = (jnp.swapaxes(a, 1, 2)[0] for a in (w_in, m_w_in, v_w_in))
    xi, yi = lax.axis_index("x"), lax.axis_index("y")
    chip_order = jnp.stack([2 * xi + yi, 2 * (1 - xi) + yi, 2 * xi + 1 - yi, 2 * (1 - xi) + 1 - yi]).astype(jnp.int32)
    wt_mine, wo_mine = _prep_weights(me.reshape(1), wt_f32, w_out[0])

    freqs = _rope_freqs()
    sinks_v = sinks.reshape(16)
    h, proj, wt = _gather_in_proj(chip_order, x2, shift, scale, norm_g, wt_mine)
    y, wo = _mixer_fwd(proj, freqs, ln_v_g, ln_v_b, w_spatial[0], b_spatial[0], sinks_v, wo_mine)
    dx1, do, dy, sums_o = _out_proj_loss(y, x2, tgt, wo, gate, shift_f, scale_f, fng)

    chip = (2 * lax.axis_index("x") + lax.axis_index("y")).reshape(1)
    pair_out, _ = _wgrad_pair("wgrad_out", y, do, 2048)
    dproj, d_ln, d_wsp, misc, parts_out = _mixer_bwd(
        me.reshape(1), proj, dy, freqs, ln_v_g, ln_v_b, w_spatial[0], b_spatial[0], sinks_v, pair_out)
    pair_in, (d_ln, d_wsp, misc) = _wgrad_pair(
        "wgrad_in", dproj, h, 1024, gathers=(d_ln, d_wsp.reshape(N_DEV * A_GROUPS * CHUNK, CHUNK), misc))
    grad_x, sums, parts_in = _in_proj_bwd(dproj, wt, x2, dx1, scale, norm_g, sums_o, pair_in)
    wt_leaves = [jnp.swapaxes(a[None], 1, 2)
                 for a in _adam_rows("adam_w_in", chip, pair_in, parts_in, wt_f32, m_wt, v_wt, 176)]
    w_out_leaves = [a[None] for a in _adam_rows("adam_w_out", chip, pair_out, parts_out, w_out[0], m_w_out[0], v_w_out[0], 64)]

    (sums,) = _all_gather("gather_sums", [sums], pltpu.VMEM)
    natural = dict(w_spatial=(A_GROUPS * CHUNK, CHUNK), b_spatial=(A_GROUPS, CHUNK), sinks=(1, 16), norm_g=(1, D_MODEL),
                   ln_v_g=(1, D_A), ln_v_b=(1, D_A), final_norm_g=(1, D_MODEL), b_ada=(1, 3 * D_MODEL),
                   b_ada_final=(1, 2 * D_MODEL))
    given = dict(
        w_spatial=(w_spatial, m_w_spatial, v_w_spatial), b_spatial=(b_spatial, m_b_spatial, v_b_spatial),
        sinks=(sinks, m_sinks, v_sinks), norm_g=(norm_g, m_norm_g, v_norm_g), ln_v_g=(ln_v_g, m_ln_v_g, v_ln_v_g),
        ln_v_b=(ln_v_b, m_ln_v_b, v_ln_v_b), final_norm_g=(final_norm_g, m_final_norm_g, v_final_norm_g),
        b_ada=(b_ada, m_b_ada, v_b_ada), b_ada_final=(b_ada_final, m_b_ada_final, v_b_ada_final))
    params = {name: tuple(a.reshape(natural[name]) for a in given[name]) for name in SMALL_PARAMS}
    params["sinks"] = tuple(jnp.pad(a, ((0, 0), (0, 128 - 16))) for a in params["sinks"])
    loss, small = _adam_small(d_wsp.reshape(N_DEV, A_GROUPS * CHUNK, CHUNK), misc.reshape(N_DEV, MISC_ROWS, 128),
                              d_ln.reshape(N_DEV, 8, D_A), sums, params)
    small["sinks"] = [a[:, :16] for a in small["sinks"]]
    small = {name: [a.reshape(given[name][0].shape) for a in small[name]] for name in SMALL_PARAMS}

    dmod_all = jnp.concatenate([sums[:, SUM_SHIFT], sums[:, SUM_SCALE], sums[:, SUM_GATE]], axis=1)
    dmod_f_all = jnp.concatenate([sums[:, SUM_SHIFT_F], sums[:, SUM_SCALE_F]], axis=1)
    dmod_mine = lax.dynamic_slice_in_dim(dmod_all, me * n_ada, n_ada, axis=1)
    dmod_f_mine = lax.dynamic_slice_in_dim(dmod_f_all, me * n_ada_f, n_ada_f, axis=1)
    ada = _adam_ada("adam_w_ada", cact, dmod_mine, w_ada[0], m_w_ada[0], v_w_ada[0])
    ada_f = _adam_ada("adam_w_ada_final", cact, dmod_f_mine, w_ada_final, m_w_ada_final, v_w_ada_final)

    def leaves(k):
        return (ada[k][None], small["b_ada"][k], small["norm_g"][k], wt_leaves[k], small["ln_v_g"][k],
                small["ln_v_b"][k], small["w_spatial"][k], small["b_spatial"][k], small["sinks"][k], w_out_leaves[k],
                ada_f[k], small["b_ada_final"][k], small["final_norm_g"][k])

    return (loss[0, 0], grad_x[None], *leaves(0), *leaves(1), *leaves(2), *leaves(3))
```

```python
import jax
import jax.numpy as jnp
from jax import lax
from jax.experimental import pallas as pl
from jax.experimental.pallas import tpu as pltpu

D_MODEL = 2048
D_IN = 5632
D_A = 1024
CHUNK = 128
A_GROUPS = 8
HEAD_DIM = 64
N_KV_HEADS = 4
N_DEV = 8
ROPE_THETA = 10000.0
NORM_EPS = 1e-5
ATTN_SCALE = HEAD_DIM ** -0.5

ADAM_LR = 0.001
ADAM_B1 = 0.9
ADAM_B2 = 0.999
ADAM_EPS = 1e-08
ADAM_WD = 0.01
ADAM_STEP = 10

OFF_U, OFF_VA, OFF_ZA, OFF_Q, OFF_K, OFF_V, OFF_ZB = 0, 1024, 2048, 3072, 4096, 4352, 4608

SUM_SHIFT, SUM_SCALE, SUM_NORM_G, SUM_GATE, SUM_SHIFT_F, SUM_SCALE_F, SUM_FNG, SUM_SQ_ERR = range(8)

V7X_VMEM_LIMIT_BYTES = 56 * 1024 * 1024

F32 = jnp.float32
BF16 = jnp.bfloat16
MESH = pl.DeviceIdType.MESH
SDS = jax.ShapeDtypeStruct
NT = (((1,), (1,)), ((), ()))
TN = (((0,), (0,)), ((), ()))


def _params(*semantics):
    return pltpu.CompilerParams(dimension_semantics=semantics or None, vmem_limit_bytes=V7X_VMEM_LIMIT_BYTES)


def _mesh_pos():
    return lax.axis_index("x"), lax.axis_index("y"), lax.axis_index("c")


def _sigmoid(z):
    return 1.0 / (1.0 + jnp.exp(-z))


def _adamw(w, g, m, v):
    m = ADAM_B1 * m + (1.0 - ADAM_B1) * g
    v = ADAM_B2 * v + (1.0 - ADAM_B2) * (g * g)
    m_hat = m / (1.0 - ADAM_B1 ** ADAM_STEP)
    v_hat = v / (1.0 - ADAM_B2 ** ADAM_STEP)
    delta = -ADAM_LR * (m_hat / (jnp.sqrt(v_hat) + ADAM_EPS) + ADAM_WD * w)
    return delta, m, v


def _all_gather(name, blocks, memory_space):
    n_arr = len(blocks)

    def body(*refs):
        ins, outs = refs[:n_arr], refs[n_arr:2 * n_arr]
        send_sems, recv_sems, local_sems = refs[2 * n_arr:]
        x, y, c = _mesh_pos()
        me, sibling = (x, y, c), (x, y, 1 - c)
        chips = [(1 - x, y), (x, 1 - y), (1 - x, 1 - y)]

        def slot(p):
            return 4 * p[0] + 2 * p[1] + p[2]

        def copy(a, k, block, to, src=None):
            dst = outs[a].at[slot(block)]
            return pltpu.make_async_remote_copy(
                src_ref=dst if src is None else src, dst_ref=dst,
                send_sem=send_sems.at[a, k], recv_sem=recv_sems.at[a, k],
                device_id=to, device_id_type=MESH)

        mine = [pltpu.make_async_copy(ins[a], outs[a].at[slot(me)], local_sems.at[a]) for a in range(n_arr)]
        for cp in mine:
            cp.start()
        first = []
        for a in range(n_arr):
            first.append(copy(a, 0, me, sibling, src=ins[a]))
            first += [copy(a, 1 + j, me, (*chip, c), src=ins[a]) for j, chip in enumerate(chips)]
        for cp in first:
            cp.start()
        passed = []
        for j, chip in enumerate(chips):
            for a in range(n_arr):
                copy(a, 1 + j, (*chip, c), me).wait_recv()
                fwd = copy(a, 4 + j, (*chip, c), sibling)
                fwd.start()
                passed.append(fwd)
        for a in range(n_arr):
            copy(a, 0, sibling, me).wait_recv()
            for j, chip in enumerate(chips):
                copy(a, 4 + j, (*chip, 1 - c), me).wait_recv()
        for cp in first + passed:
            cp.wait_send()
        for cp in mine:
            cp.wait()

    spec = pl.BlockSpec(memory_space=memory_space)
    return pl.pallas_call(
        body, name=name,
        out_shape=[SDS((N_DEV,) + b.shape, b.dtype) for b in blocks],
        in_specs=[spec] * n_arr, out_specs=[spec] * n_arr,
        scratch_shapes=[pltpu.SemaphoreType.DMA((n_arr, 7)), pltpu.SemaphoreType.DMA((n_arr, 7)),
                        pltpu.SemaphoreType.DMA((n_arr,))],
        compiler_params=_params(),
    )(*blocks)


def _ada_exchange(c, w_ada, b_ada8, w_ada_f, b_ada_f8):
    n1, n2 = w_ada.shape[1], w_ada_f.shape[1]

    def body(c_ref, w1_ref, b1_ref, w2_ref, b2_ref, cact_ref, mod_ref, modf_ref,
             cact_buf, res1, res2, send1, send2, sems_s, sems_r):
        x, y, c_pos = _mesh_pos()
        me = 4 * x + 2 * y + c_pos
        flips = [(k >> 2 & 1, k >> 1 & 1, k & 1) for k in range(1, N_DEV)]

        def peer(f):
            return (1 - x if f[0] else x, 1 - y if f[1] else y, 1 - c_pos if f[2] else c_pos)

        cv = c_ref[...]
        cact = cv * _sigmoid(cv)
        cact_buf[...] = cact
        cact_ref[me] = cact

        def rdma(phase, k, src, dst, f):
            return pltpu.make_async_remote_copy(src_ref=src, dst_ref=dst, send_sem=sems_s.at[phase, k],
                                                recv_sem=sems_r.at[phase, k], device_id=peer(f), device_id_type=MESH)

        gather = [rdma(0, k, cact_buf, cact_ref.at[me], f) for k, f in enumerate(flips)]
        for cp in gather:
            cp.start()
        for cp in gather:
            cp.wait_recv()
        for cp in gather:
            cp.wait_send()

        rid = lax.broadcasted_iota(jnp.int32, (N_DEV, D_MODEL), 0)
        rows = jnp.zeros((N_DEV, D_MODEL), F32)
        for j in range(N_DEV):
            rows = jnp.where(rid == j, jnp.broadcast_to(cact_ref[j], (N_DEV, D_MODEL)), rows)
        rows = rows.astype(BF16)
        res1[...] = jnp.dot(rows, w1_ref[...].astype(BF16), preferred_element_type=F32) + b1_ref[pl.ds(me, 1), :]
        res2[...] = jnp.dot(rows, w2_ref[...].astype(BF16), preferred_element_type=F32) + b2_ref[pl.ds(me, 1), :]
        for j in range(N_DEV):
            send1[j] = res1[pl.ds(j, 1), :]
            send2[j] = res2[pl.ds(j, 1), :]
        mod_ref[me] = send1[me]
        modf_ref[me] = send2[me]
        scatter = []
        for k, f in enumerate(flips):
            to = me ^ (k + 1)
            scatter.append(rdma(1, k, send1.at[to], mod_ref.at[me], f))
            scatter.append(rdma(2, k, send2.at[to], modf_ref.at[me], f))
        for cp in scatter:
            cp.start()
        for cp in scatter:
            cp.wait_recv()
        for cp in scatter:
            cp.wait_send()

    vmem = pl.BlockSpec(memory_space=pltpu.VMEM)
    return pl.pallas_call(
        body, name="ada_exchange",
        out_shape=[SDS((N_DEV, 1, D_MODEL), F32), SDS((N_DEV, 1, n1), F32), SDS((N_DEV, 1, n2), F32)],
        in_specs=[vmem] * 5, out_specs=[vmem] * 3,
        scratch_shapes=[pltpu.VMEM((1, D_MODEL), F32), pltpu.VMEM((N_DEV, n1), F32), pltpu.VMEM((N_DEV, n2), F32),
                        pltpu.VMEM((N_DEV, 1, n1), F32), pltpu.VMEM((N_DEV, 1, n2), F32),
                        pltpu.SemaphoreType.DMA((3, 7)), pltpu.SemaphoreType.DMA((3, 7))],
        compiler_params=_params(),
    )(c, w_ada, b_ada8, w_ada_f, b_ada_f8)


def _chip_scatter(pair_ref, parts_ref, send_sems, recv_sems):
    x, y, c = _mesh_pos()
    chips = [(1 - x, y), (x, 1 - y), (1 - x, 1 - y)]
    return [pltpu.make_async_remote_copy(
        src_ref=pair_ref.at[2 * cx + cy], dst_ref=parts_ref.at[j], send_sem=send_sems.at[j], recv_sem=recv_sems.at[j],
        device_id=(cx, cy, c), device_id_type=MESH) for j, (cx, cy) in enumerate(chips)]


def _scatter_scratch():
    return [pltpu.SemaphoreType.DMA((3,)), pltpu.SemaphoreType.DMA((3,))]


def _prep_weights(me, wt, w_out):
    steps = 4

    def body(me_ref, wt_ref, wo_ref, wtb_ref, wob_ref):
        wtb_ref[...] = wt_ref[...].astype(BF16)
        wob_ref[...] = wo_ref[...].astype(BF16)

    def rows(a, mine):
        blk = (a.shape[0] // steps, a.shape[1])
        return pl.BlockSpec(blk, (lambda i, me_ref: (steps * me_ref[0] + i, 0)) if mine else (lambda i, me_ref: (i, 0)))

    return pl.pallas_call(
        body, name="prep_weights",
        grid_spec=pltpu.PrefetchScalarGridSpec(
            num_scalar_prefetch=1, grid=(steps,),
            in_specs=[rows(wt, False), rows(w_out, False)], out_specs=[rows(wt, True), rows(w_out, True)]),
        out_shape=[SDS((N_DEV * wt.shape[0], D_MODEL), BF16), SDS((N_DEV * w_out.shape[0], D_MODEL), BF16)],
        compiler_params=_params("parallel"),
    )(me, wt, w_out)


class _InPlaceGather:
    def __init__(self, buf_ref, send_sems, recv_sems, relay=False):
        self.buf, self.send_sems, self.recv_sems, self.relay = buf_ref, send_sems, recv_sems, relay
        self.n = buf_ref.shape[0] // N_DEV
        x, y, c = _mesh_pos()
        self.me, self.sibling, self.core = (x, y, c), (x, y, 1 - c), c
        self.chips = [(1 - x, y), (x, 1 - y), (1 - x, 1 - y)]
        self.relay_from = (jnp.where(c == 0, 1 - x, x), jnp.where(c == 0, y, 1 - y), c)
        self.relay_to = (jnp.where(c == 0, x, 1 - x), jnp.where(c == 0, 1 - y, y), c)

    def copy(self, k, block, to):
        start = pl.multiple_of((4 * block[0] + 2 * block[1] + block[2]) * self.n, self.n)
        rows = self.buf.at[pl.ds(start, self.n)]
        return pltpu.make_async_remote_copy(src_ref=rows, dst_ref=rows, send_sem=self.send_sems.at[k],
                                            recv_sem=self.recv_sems.at[k], device_id=to, device_id_type=MESH)

    def start(self):
        self.copy(0, self.me, self.sibling).start()
        for j, chip in enumerate(self.chips[:2] if self.relay else self.chips):
            self.copy(1 + j, self.me, (*chip, self.core)).start()

    def relay_diagonal(self):
        self.copy(3, self.relay_from, self.relay_to).start()

    def pass_on(self, j):
        self.copy(1 + j, (*self.chips[j], self.core), self.me).wait_recv()
        self.copy(4 + j, (*self.chips[j], self.core), self.sibling).start()

    def wait_sibling(self, k):
        self.copy(k, self.sibling, self.me).wait_recv()

    def wait_sends(self):
        for k in range(7):
            self.copy(k, self.me, self.sibling).wait_send()


def _gather_scratch():
    return [pltpu.SemaphoreType.DMA((7,)), pltpu.SemaphoreType.DMA((7,))]


def _gather_in_proj(order, x, shift, scale, norm_g, wt_all):
    s = x.shape[0]
    th, tm = min(512, s), min(1024, s)
    nh, ni = s // th, s // tm
    tn = D_IN // 4
    steps = nh + 4 * ni

    def body(order_ref, x_ref, shift_ref, scale_ref, g_ref, wt_in, h_ref, proj_ref, wt_ref,
             h_scr, w_buf, load_sems, send_sems, recv_sems):
        g = pl.program_id(0)
        gather = _InPlaceGather(wt_ref, send_sems, recv_sems, relay=True)

        def tile_load(slot, chip):
            return pltpu.make_async_copy(wt_ref.at[pl.ds(pl.multiple_of(chip * tn, tn), tn)], w_buf.at[slot],
                                         load_sems.at[slot])

        @pl.when(g == 0)
        def _():
            gather.start()

        @pl.when(g < nh)
        def _():
            xv = x_ref[...]
            r = lax.rsqrt(jnp.mean(xv * xv, axis=-1, keepdims=True) + NORM_EPS)
            hb = (((xv * r) * g_ref[...]) * (1.0 + scale_ref[...]) + shift_ref[...]).astype(BF16)
            h_ref[...] = hb
            h_scr[pl.ds(pl.multiple_of(g * th, th), th), :] = hb

        @pl.when(g == nh - 1)
        def _():
            gather.wait_sibling(0)
            tile_load(0, order_ref[0]).start()

        @pl.when(g >= nh)
        def _():
            t, i = (g - nh) // ni, (g - nh) % ni

            @pl.when(i == 0)
            def _():
                tile_load(t % 2, order_ref[t]).wait()

            @pl.when((i == ni - 1) & (t == 0))
            def _():
                gather.pass_on(0)
                gather.pass_on(1)
                gather.relay_diagonal()

            @pl.when((i == ni // 2) & (t == 2))
            def _():
                gather.pass_on(2)

            for j in range(3):
                @pl.when((i == ni - 1) & (t == j))
                def _():
                    gather.wait_sibling(4 + j)
                    tile_load((j + 1) % 2, order_ref[j + 1]).start()

            lhs = h_scr[pl.ds(pl.multiple_of(i * tm, tm), tm), :]
            proj_ref[...] = lax.dot_general(lhs, w_buf[t % 2], NT, preferred_element_type=F32).astype(BF16)

        @pl.when(g == steps - 1)
        def _():
            gather.wait_sends()

    def h_tile(g, order_ref):
        return (jnp.minimum(g, nh - 1), 0)

    def proj_tile(g, order_ref):
        mm = jnp.maximum(g - nh, 0)
        return (mm % ni, order_ref[mm // ni])

    row = pl.BlockSpec((1, D_MODEL), lambda g, order_ref: (0, 0))
    hbm = pl.BlockSpec(memory_space=pl.ANY)
    return pl.pallas_call(
        body, name="gather_in_proj",
        grid_spec=pltpu.PrefetchScalarGridSpec(
            num_scalar_prefetch=1, grid=(steps,),
            in_specs=[pl.BlockSpec((th, D_MODEL), h_tile), row, row, row, hbm],
            out_specs=[pl.BlockSpec((th, D_MODEL), h_tile), pl.BlockSpec((tm, tn), proj_tile), hbm],
            scratch_shapes=[pltpu.VMEM((s, D_MODEL), BF16), pltpu.VMEM((2, tn, D_MODEL), BF16),
                            pltpu.SemaphoreType.DMA((2,)), *_gather_scratch()]),
        out_shape=[SDS((s, D_MODEL), BF16), SDS((s, D_IN), BF16), SDS(wt_all.shape, BF16)],
        input_output_aliases={5: 2},
        compiler_params=_params("arbitrary"),
    )(order, x, shift, scale, norm_g, wt_all)


def _rope_freqs():
    inv_freq = ROPE_THETA ** (-jnp.arange(0, HEAD_DIM, 2, dtype=F32) / HEAD_DIM)
    return jnp.tile(inv_freq, 4).reshape(1, 128)


class _RopeTables:
    def __init__(self, freq_ref, rows_ref, state_ref, last_ref):
        self.freq, self.rows, self.state, self.last = freq_ref, rows_ref, state_ref, last_ref

    def start(self, block, direction):
        ang = lax.broadcasted_iota(jnp.int32, (CHUNK, 128), 0).astype(F32) * self.freq[...]
        self.rows[0] = jnp.cos(ang)
        self.rows[1] = jnp.sin(ang)
        base = jnp.asarray(block * CHUNK, dtype=F32) * self.freq[...]
        turn = float(direction * CHUNK) * self.freq[...]
        self.state[0:1, :] = jnp.cos(base)
        self.state[1:2, :] = jnp.sin(base)
        self.state[2:3, :] = jnp.cos(turn)
        self.state[3:4, :] = jnp.sin(turn)

    def step(self):
        c, s, ct, st = (self.state[k:k + 1, :] for k in range(4))
        self.state[0:1, :] = c * ct - s * st
        self.state[1:2, :] = s * ct + c * st

    def tables(self):
        c, s = self.state[0:1, :], self.state[1:2, :]
        cos = c * self.rows[0] - s * self.rows[1]
        sin = s * self.rows[0] + c * self.rows[1]
        first_half = (lax.broadcasted_iota(jnp.int32, (1, 128), 1) & (HEAD_DIM - 1)) < HEAD_DIM // 2
        return cos, jnp.where(first_half, -sin, 0.0), jnp.where(first_half, 0.0, sin)

    def keep(self, tabs):
        for k in range(3):
            self.last[k] = tabs[k]

    def kept(self):
        return tuple(self.last[k] for k in range(3))


def _rope_scratch():
    return [pltpu.VMEM((2, CHUNK, 128), F32), pltpu.VMEM((8, 128), F32), pltpu.VMEM((3, CHUNK, 128), F32)]


def _rope(v, cos, sin_lo, sin_hi):
    width = v.shape[1]
    rep = (1, width // 128)
    return (v * jnp.tile(cos, rep) + pltpu.roll(v, width - 32, 1) * jnp.tile(sin_lo, rep)
            + pltpu.roll(v, 32, 1) * jnp.tile(sin_hi, rep))


def _rope_bwd(d, cos, sin_lo, sin_hi):
    width = d.shape[1]
    rep = (1, width // 128)
    return (d * jnp.tile(cos, rep) + pltpu.roll(d * jnp.tile(sin_lo, rep), 32, 1)
            + pltpu.roll(d * jnp.tile(sin_hi, rep), width - 32, 1))


def _layer_norm(v, g, b):
    mu = jnp.mean(v, axis=-1, keepdims=True)
    vc = v - mu
    rstd = lax.rsqrt(jnp.mean(vc * vc, axis=-1, keepdims=True) + NORM_EPS)
    vhat = vc * rstd
    return vhat * g + b, vhat, rstd


def _set_tril(w_ref, out_ref):
    t = lax.broadcasted_iota(jnp.int32, (CHUNK, CHUNK), 0)
    tp = lax.broadcasted_iota(jnp.int32, (CHUNK, CHUNK), 1)
    for g in range(A_GROUPS):
        out_ref[g] = jnp.where(tp <= t, w_ref[g], 0.0).astype(BF16)


def _bias_columns(b_ref, out_ref):
    for g in range(A_GROUPS):
        out_ref[g] = jnp.broadcast_to(b_ref[pl.ds(g, 1), :], (CHUNK, CHUNK)).T


def _from_prev():
    r = lax.broadcasted_iota(jnp.int32, (CHUNK, 4 * CHUNK), 0)
    i = lax.broadcasted_iota(jnp.int32, (CHUNK, 4 * CHUNK), 1) & (CHUNK - 1)
    return r > i


def _set_unfold_masks(mask_ref):
    prev = _from_prev()
    mask_ref[0] = jnp.where(prev, 1.0, 0.0).astype(BF16)
    mask_ref[1] = jnp.where(prev, 0.0, 1.0).astype(BF16)


def _fold_band(t, from_prev):
    return jnp.where(from_prev, t[:CHUNK], t[CHUNK:])


def _unfold_band(t, mask_ref):
    return jnp.concatenate([t * mask_ref[0], t * mask_ref[1]], axis=0)


def _low_lanes():
    return lax.broadcasted_iota(jnp.int32, (1, 128), 1) < HEAD_DIM


def _stack_heads(pair_a, pair_b):
    lo = _low_lanes()
    return jnp.concatenate([jnp.where(lo, pair_a, 0.0), jnp.where(lo, 0.0, pair_a),
                            jnp.where(lo, pair_b, 0.0), jnp.where(lo, 0.0, pair_b)], axis=0).astype(BF16)


def _heads_to_lanes(per_group):
    rows = [t[:, r * CHUNK:(r + 1) * CHUNK] for t in per_group for r in range(4)]
    return jnp.concatenate(rows, axis=0).T


def _dup_kv_head(band, gk):
    pair = band[:, (gk // 2) * 128:(gk // 2 + 1) * 128]
    lo = _low_lanes()
    one = jnp.where(lo if gk % 2 == 0 else jnp.logical_not(lo), pair, 0.0)
    return (one + pltpu.roll(one, HEAD_DIM, 1)).astype(BF16)


def _fold_kv_head(dup_grad, gk):
    both = dup_grad + pltpu.roll(dup_grad, HEAD_DIM, 1)
    lo = _low_lanes()
    return jnp.where(lo if gk % 2 == 0 else jnp.logical_not(lo), both, 0.0)


def _attn_probs(q_st, k_dup, sink_row, from_prev, first_block):
    s = lax.dot_general(k_dup, q_st, NT, preferred_element_type=F32)
    no_prev = jnp.where(first_block, -jnp.inf, 0.0)
    s = jnp.where(from_prev, s[:CHUNK] + no_prev, s[CHUNK:])
    m = jnp.maximum(jnp.max(s, axis=0, keepdims=True), sink_row)
    p = jnp.exp(s - m)
    e_sink = jnp.exp(sink_row - m)
    inv = 1.0 / (jnp.sum(p, axis=0, keepdims=True) + e_sink)
    return p * inv, e_sink * inv


def _sink_row(sinks_ref, gk):
    return jnp.concatenate([jnp.full((1, CHUNK), sinks_ref[4 * gk + r], F32) for r in range(4)], axis=1)


def _mixer_specs(nb, rev):
    def blk(i):
        return nb - 1 - i if rev else i

    def prev(i):
        return jnp.maximum(blk(i) - 1, 0)

    return dict(
        cur=pl.BlockSpec((CHUNK, D_IN), lambda i, *_: (blk(i), 0)),
        prev_kv=pl.BlockSpec((CHUNK, 2 * 256), lambda i, *_: (prev(i), OFF_K // 512)),
        freq=pl.BlockSpec((1, 128), lambda i, *_: (0, 0)),
        vec=pl.BlockSpec((1, D_A), lambda i, *_: (0, 0)),
        wsp=pl.BlockSpec((A_GROUPS, CHUNK, CHUNK), lambda i, *_: (0, 0, 0)),
        bsp=pl.BlockSpec((A_GROUPS, CHUNK), lambda i, *_: (0, 0)),
        smem=pl.BlockSpec(memory_space=pltpu.SMEM),
        blk=blk,
    )


def _mixer_fwd(proj, freqs, ln_g, ln_b, w_sp, b_sp, sinks, wo_all):
    s = proj.shape[0]
    nb = s // CHUNK
    sp = _mixer_specs(nb, rev=False)

    def body(cur_ref, pkv_ref, freq_ref, lg_ref, lb_ref, w_ref, b_ref, sinks_ref, wo_in, y_ref, wo_ref,
             bcol, wtril, mask, rope_rows, rope_state, rope_last, send_sems, recv_sems):
        i = pl.program_id(0)
        gather = _InPlaceGather(wo_ref, send_sems, recv_sems)
        rope = _RopeTables(freq_ref, rope_rows, rope_state, rope_last)

        @pl.when(i == 0)
        def _():
            gather.start()
            _bias_columns(b_ref, bcol)
            _set_tril(w_ref, wtril)
            _set_unfold_masks(mask)
            rope.start(-1, 1)
            rope_last[...] = jnp.zeros_like(rope_last)

        @pl.when(i == (7 * nb) // 8)
        def _():
            for j in range(3):
                gather.pass_on(j)

        vln, _, _ = _layer_norm(cur_ref[:, OFF_VA:OFF_ZA].astype(F32), lg_ref[...], lb_ref[...])
        vln = vln.astype(BF16)

        def gating_group(g):
            cols = slice(g * 128, (g + 1) * 128)
            sg = jnp.dot(wtril[g], vln[:, cols], preferred_element_type=F32) + bcol[g]
            u = cur_ref[:, OFF_U + g * 128:OFF_U + (g + 1) * 128].astype(F32)
            z = cur_ref[:, OFF_ZA + g * 128:OFF_ZA + (g + 1) * 128].astype(F32)
            y_ref[:, cols] = (u * sg * (z * _sigmoid(z))).astype(BF16)

        rope.step()
        cur_t, prev_t = rope.tables(), rope.kept()
        rope.keep(cur_t)
        qr = _rope(cur_ref[:, OFF_Q:OFF_K].astype(F32), *cur_t) * ATTN_SCALE
        kr = jnp.concatenate([_rope(pkv_ref[:, 0:256].astype(F32), *prev_t),
                              _rope(cur_ref[:, OFF_K:OFF_V].astype(F32), *cur_t)], axis=0)
        v_t = jnp.concatenate([pkv_ref[:, 256:512], cur_ref[:, OFF_V:OFF_ZB]], axis=0).astype(F32).T.astype(BF16)
        outs = []
        from_prev = _from_prev()
        for gk in range(N_KV_HEADS):
            q_st = _stack_heads(qr[:, (2 * gk) * 128:(2 * gk + 1) * 128], qr[:, (2 * gk + 1) * 128:(2 * gk + 2) * 128])
            probs, _ = _attn_probs(q_st, _dup_kv_head(kr, gk), _sink_row(sinks_ref, gk), from_prev, i == 0)
            gating_group(2 * gk)
            outs.append(jnp.dot(v_t[gk * HEAD_DIM:(gk + 1) * HEAD_DIM], _unfold_band(probs.astype(BF16), mask),
                                preferred_element_type=F32))
            gating_group(2 * gk + 1)
        zb = cur_ref[:, OFF_ZB:D_IN].astype(F32)
        y_ref[:, D_A:D_MODEL] = (_heads_to_lanes(outs) * (zb * _sigmoid(zb))).astype(BF16)

        @pl.when(i == nb - 1)
        def _():
            gather.wait_sibling(0)
            for j in range(3):
                gather.wait_sibling(4 + j)
            gather.wait_sends()

    hbm = pl.BlockSpec(memory_space=pl.ANY)
    return pl.pallas_call(
        body, name="mixer_fwd", grid=(nb,),
        in_specs=[sp["cur"], sp["prev_kv"], sp["freq"], sp["vec"], sp["vec"], sp["wsp"], sp["bsp"], sp["smem"], hbm],
        out_specs=[pl.BlockSpec((CHUNK, D_MODEL), lambda i: (i, 0)), hbm],
        out_shape=[SDS((s, D_MODEL), BF16), SDS(wo_all.shape, wo_all.dtype)],
        scratch_shapes=[pltpu.VMEM((A_GROUPS, CHUNK, CHUNK), F32), pltpu.VMEM((A_GROUPS, CHUNK, CHUNK), BF16),
                        pltpu.VMEM((2, CHUNK, 4 * CHUNK), BF16), *_rope_scratch(), *_gather_scratch()],
        input_output_aliases={8: 1},
        compiler_params=_params("arbitrary"),
    )(proj, proj, freqs, ln_g, ln_b, w_sp, b_sp, sinks, wo_all)


def _out_proj_loss(y, x, target, wo, gate, shift_f, scale_f, fng):
    s = y.shape[0]
    tm, tr = 256, 128
    nt = s // tm

    def body(y_ref, x_ref, t_ref, wo_ref, gate_ref, sh_ref, sc_ref, g_ref, dx1_ref, do_ref, dy_ref, sums_ref,
             do_last, do_work):
        i = pl.program_id(0)

        @pl.when(i == 0)
        def _():
            sums_ref[...] = jnp.zeros_like(sums_ref)
            do_last[...] = jnp.zeros_like(do_last)

        do_work[...] = do_last[...]
        o = jnp.dot(y_ref[...], wo_ref[...], preferred_element_type=F32)
        gate, g, sh = gate_ref[...], g_ref[...], sh_ref[...]
        one_sc = 1.0 + sc_ref[...]
        cs, inv_d = g * one_sc, 1.0 / D_MODEL

        def rowsum(v):
            return jnp.sum(v, axis=0, keepdims=True)

        sums = [jnp.zeros((1, D_MODEL), F32) for _ in range(4)]
        for c in range(tm // tr):
            rows = slice(c * tr, (c + 1) * tr)
            oc = o[rows]
            x1 = x_ref[rows, :] + gate * oc
            r = lax.rsqrt(jnp.sum(x1 * x1, axis=-1, keepdims=True) * inv_d + NORM_EPS)
            x1n = x1 * r
            diff = x1n * cs + sh - t_ref[rows, :]
            w = diff * x1n
            lane_sum = jnp.sum(w * cs, axis=-1, keepdims=True)
            dx1 = (diff * cs) * (r * inv_d) - x1n * (r * lane_sum * (inv_d * inv_d))
            dx1_ref[rows, :] = dx1
            do = (dx1 * gate).astype(BF16)
            do_ref[rows, :] = do
            do_last[rows, :] = do
            for k, v in enumerate((dx1 * oc, diff, w, diff * diff)):
                sums[k] = sums[k] + rowsum(v)
        live = jnp.where(i < nt, 1.0, 0.0)
        for row, v in ((SUM_GATE, sums[0]), (SUM_SHIFT_F, inv_d * sums[1]), (SUM_SCALE_F, inv_d * (sums[2] * g)),
                       (SUM_FNG, inv_d * (sums[2] * one_sc)), (SUM_SQ_ERR, sums[3])):
            sums_ref[row:row + 1, :] += live * v
        dy_ref[...] = lax.dot_general(do_work[...], wo_ref[...], NT, preferred_element_type=F32).astype(BF16)

    tile = pl.BlockSpec((tm, D_MODEL), lambda i: (jnp.minimum(i, nt - 1), 0))
    row = pl.BlockSpec((1, D_MODEL), lambda i: (0, 0))
    return pl.pallas_call(
        body, name="out_proj_loss", grid=(nt + 1,),
        in_specs=[tile, tile, tile, pl.BlockSpec((D_MODEL, D_MODEL), lambda i: (0, 0)), row, row, row, row],
        out_specs=[tile, tile, pl.BlockSpec((tm, D_MODEL), lambda i: (jnp.maximum(i - 1, 0), 0)),
                   pl.BlockSpec((8, D_MODEL), lambda i: (0, 0))],
        out_shape=[SDS((s, D_MODEL), F32), SDS((s, D_MODEL), BF16), SDS((s, D_MODEL), BF16), SDS((8, D_MODEL), F32)],
        scratch_shapes=[pltpu.VMEM((tm, D_MODEL), BF16), pltpu.VMEM((tm, D_MODEL), BF16)],
        compiler_params=_params("arbitrary"),
    )(y, x, target, wo, gate, shift_f, scale_f, fng)


ROW_DBSP, ROW_DSINKS, MISC_ROWS = 0, 8, 32


def _mixer_bwd(me, proj, dy, freqs, ln_g, ln_b, w_sp, b_sp, sinks, pair):
    s = proj.shape[0]
    nb = s // CHUNK
    sp = _mixer_specs(nb, rev=True)

    def body(me_ref, cur_ref, pkv_ref, dy_ref, freq_ref, lg_ref, lb_ref, w_ref, b_ref, sinks_ref, pair_ref,
             dproj_ref, dln_ref, dw_ref, misc_ref, parts_ref, bcol, wtril, dbcol, carry, mask, rope_rows, rope_state,
             rope_last, send_sems, recv_sems):
        i = pl.program_id(0)
        block = nb - 1 - i
        rope = _RopeTables(freq_ref, rope_rows, rope_state, rope_last)

        @pl.when(i == 0)
        def _():
            for cp in _chip_scatter(pair_ref, parts_ref, send_sems, recv_sems):
                cp.start()
            _bias_columns(b_ref, bcol)
            _set_tril(w_ref, wtril)
            _set_unfold_masks(mask)
            rope.start(nb - 1, -1)
            rope.keep(rope.tables())
            dbcol[...] = jnp.zeros_like(dbcol)
            carry[...] = jnp.zeros_like(carry)
            dln_ref[...] = jnp.zeros_like(dln_ref)
            dw_ref[...] = jnp.zeros_like(dw_ref)
            misc_ref[...] = jnp.zeros_like(misc_ref)

        vln, vhat, rstd = _layer_norm(cur_ref[:, OFF_VA:OFF_ZA].astype(F32), lg_ref[...], lb_ref[...])
        vln = vln.astype(BF16)
        d_vln = []

        def gating_group(g):
            cols = slice(g * 128, (g + 1) * 128)
            w_g = wtril[g]
            sg = jnp.dot(w_g, vln[:, cols], preferred_element_type=F32) + bcol[g]
            u = cur_ref[:, OFF_U + g * 128:OFF_U + (g + 1) * 128].astype(F32)
            z = cur_ref[:, OFF_ZA + g * 128:OFF_ZA + (g + 1) * 128].astype(F32)
            dya = dy_ref[:, cols].astype(F32)
            sig = _sigmoid(z)
            d_ya = dya * (z * sig)
            dproj_ref[:, OFF_ZA + g * 128:OFF_ZA + (g + 1) * 128] = (
                dya * (u * sg) * (sig * (1.0 + z * (1.0 - sig)))).astype(BF16)
            dproj_ref[:, OFF_U + g * 128:OFF_U + (g + 1) * 128] = (d_ya * sg).astype(BF16)
            d_s = d_ya * u
            dbcol[g] += d_s
            d_sb = d_s.astype(BF16)
            dw_ref[g] += lax.dot_general(d_sb, vln[:, cols], NT, preferred_element_type=F32)
            d_vln.append(lax.dot_general(w_g, d_sb, TN, preferred_element_type=F32))

        cur_t = rope.kept()
        rope.step()
        prev_t = rope.tables()
        rope.keep(prev_t)
        band_t = tuple(jnp.concatenate([p, c], axis=0) for p, c in zip(prev_t, cur_t))
        qr = _rope(cur_ref[:, OFF_Q:OFF_K].astype(F32), *cur_t) * ATTN_SCALE
        kr = jnp.concatenate([_rope(pkv_ref[:, 0:256].astype(F32), *prev_t),
                              _rope(cur_ref[:, OFF_K:OFF_V].astype(F32), *cur_t)], axis=0)
        vb = jnp.concatenate([pkv_ref[:, 256:512], cur_ref[:, OFF_V:OFF_ZB]], axis=0).astype(F32)
        k_t, v_t = (kr.T * ATTN_SCALE).astype(BF16), vb.T.astype(BF16)
        zb = cur_ref[:, OFF_ZB:D_IN].astype(F32)
        dyb = dy_ref[:, D_A:D_MODEL].astype(F32)
        sig = _sigmoid(zb)
        d_yb = dyb * (zb * sig)
        outs, dqs = [], []
        dk_pairs = [jnp.zeros((2 * CHUNK, 128), F32) for _ in range(2)]
        dv_pairs = [jnp.zeros((2 * CHUNK, 128), F32) for _ in range(2)]
        from_prev = _from_prev()
        for gk in range(N_KV_HEADS):
            heads = slice(gk * HEAD_DIM, (gk + 1) * HEAD_DIM)
            q_st = _stack_heads(qr[:, (2 * gk) * 128:(2 * gk + 1) * 128], qr[:, (2 * gk + 1) * 128:(2 * gk + 2) * 128])
            k_dup, v_dup = _dup_kv_head(kr, gk), _dup_kv_head(vb, gk)
            probs, p_sink = _attn_probs(q_st, k_dup, _sink_row(sinks_ref, gk), from_prev, block == 0)
            probs_b = _unfold_band(probs.astype(BF16), mask)
            outs.append(jnp.dot(v_t[heads], probs_b, preferred_element_type=F32))
            do_st = _stack_heads(d_yb[:, (2 * gk) * 128:(2 * gk + 1) * 128], d_yb[:, (2 * gk + 1) * 128:(2 * gk + 2) * 128])
            dp = _fold_band(lax.dot_general(v_dup, do_st, NT, preferred_element_type=F32), from_prev)
            delta = jnp.sum(probs * dp, axis=0, keepdims=True)
            ds = _unfold_band((probs * (dp - delta)).astype(BF16), mask)
            gating_group(2 * gk)
            d_sink = -p_sink * delta
            for r in range(4):
                row = ROW_DSINKS + 4 * gk + r
                misc_ref[row:row + 1, :] += jnp.broadcast_to(
                    jnp.sum(d_sink[:, r * CHUNK:(r + 1) * CHUNK], axis=1, keepdims=True), (1, 128))
            dqs.append(jnp.dot(k_t[heads], ds, preferred_element_type=F32))
            dk_pairs[gk // 2] += _fold_kv_head(jnp.dot(ds, q_st, preferred_element_type=F32), gk)
            dv_pairs[gk // 2] += _fold_kv_head(jnp.dot(probs_b, do_st, preferred_element_type=F32), gk)
            gating_group(2 * gk + 1)
        d_vln = jnp.concatenate(d_vln, axis=1)
        dln_ref[0:1, :] += jnp.sum(d_vln * vhat, axis=0, keepdims=True)
        dln_ref[1:2, :] += jnp.sum(d_vln, axis=0, keepdims=True)
        d_vhat = d_vln * lg_ref[...]
        d_va = rstd * (d_vhat - jnp.mean(d_vhat, axis=-1, keepdims=True)
                       - vhat * jnp.mean(d_vhat * vhat, axis=-1, keepdims=True))
        dproj_ref[:, OFF_VA:OFF_ZA] = d_va.astype(BF16)
        dproj_ref[:, OFF_ZB:D_IN] = (dyb * _heads_to_lanes(outs) * (sig * (1.0 + zb * (1.0 - sig)))).astype(BF16)
        dproj_ref[:, OFF_Q:OFF_K] = _rope_bwd(_heads_to_lanes(dqs), *cur_t).astype(BF16)
        dk_band = _rope_bwd(jnp.concatenate(dk_pairs, axis=1), *band_t)
        dv_band = jnp.concatenate(dv_pairs, axis=1)
        dproj_ref[:, OFF_K:OFF_V] = (dk_band[CHUNK:] + carry[:, 0:256]).astype(BF16)
        dproj_ref[:, OFF_V:OFF_ZB] = (dv_band[CHUNK:] + carry[:, 256:512]).astype(BF16)
        carry[:, 0:256] = dk_band[:CHUNK]
        carry[:, 256:512] = dv_band[:CHUNK]

        @pl.when(i == nb - 1)
        def _():
            t = lax.broadcasted_iota(jnp.int32, (CHUNK, CHUNK), 0)
            tp = lax.broadcasted_iota(jnp.int32, (CHUNK, CHUNK), 1)
            for g in range(A_GROUPS):
                dw_ref[g] = jnp.where(tp <= t, dw_ref[g], 0.0)
                misc_ref[pl.ds(ROW_DBSP + g, 1), :] = jnp.sum(dbcol[g].T, axis=0, keepdims=True)
            scatter = _chip_scatter(pair_ref, parts_ref, send_sems, recv_sems)
            for cp in scatter:
                cp.wait_recv()
            for cp in scatter:
                cp.wait_send()

    blk = sp["blk"]
    hbm = pl.BlockSpec(memory_space=pl.ANY)
    return pl.pallas_call(
        body, name="mixer_bwd",
        grid_spec=pltpu.PrefetchScalarGridSpec(
            num_scalar_prefetch=1, grid=(nb,),
            in_specs=[sp["cur"], sp["prev_kv"], pl.BlockSpec((CHUNK, D_MODEL), lambda i, me_ref: (blk(i), 0)),
                      sp["freq"], sp["vec"], sp["vec"], sp["wsp"], sp["bsp"], sp["smem"], hbm],
            out_specs=[pl.BlockSpec((CHUNK, D_IN), lambda i, me_ref: (blk(i), 0)),
                       pl.BlockSpec((8, D_A), lambda i, me_ref: (me_ref[0], 0)),
                       pl.BlockSpec((A_GROUPS, CHUNK, CHUNK), lambda i, me_ref: (me_ref[0], 0, 0)),
                       pl.BlockSpec((MISC_ROWS, 128), lambda i, me_ref: (me_ref[0], 0)), hbm],
            scratch_shapes=[pltpu.VMEM((A_GROUPS, CHUNK, CHUNK), F32), pltpu.VMEM((A_GROUPS, CHUNK, CHUNK), BF16),
                            pltpu.VMEM((A_GROUPS, CHUNK, CHUNK), F32), pltpu.VMEM((CHUNK, 512), F32), pltpu.VMEM((2, CHUNK, 4 * CHUNK), BF16),
                            *_rope_scratch(), *_scatter_scratch()]),
        out_shape=[SDS((s, D_IN), BF16), SDS((N_DEV * 8, D_A), F32), SDS((N_DEV * A_GROUPS, CHUNK, CHUNK), F32),
                   SDS((N_DEV * MISC_ROWS, 128), F32), SDS((3,) + pair.shape[1:], pair.dtype)],
        compiler_params=_params("arbitrary"),
    )(me, proj, proj, dy, freqs, ln_g, ln_b, w_sp, b_sp, sinks, pair)


def _wgrad_pair(name, a, b, bt, gathers=()):
    s, m = a.shape
    n = b.shape[1]
    bm, half = m // 4, m // 8
    bt = min(bt, s)
    steps = s // bt
    last = 4 * steps
    n_g = len(gathers)

    def body(*refs):
        a_ref, b_ref = refs[:2]
        out_ref, bufs = refs[2 + n_g], refs[3 + n_g:3 + 2 * n_g]
        acc, kept, got, sent, send_sems, recv_sems = refs[3 + 2 * n_g:9 + 2 * n_g]
        sems = refs[9 + 2 * n_g:]
        g = pl.program_id(0)
        tile, t = g // steps, g % steps
        mx, my, mc = _mesh_pos()
        jobs = [_InPlaceGather(bufs[k], sems[2 * k], sems[2 * k + 1]) for k in range(n_g)]

        def exchange(q):
            return pltpu.make_async_remote_copy(src_ref=sent, dst_ref=got.at[q % 2], send_sem=send_sems.at[q],
                                                recv_sem=recv_sems.at[q], device_id=(mx, my, 1 - mc),
                                                device_id_type=MESH)

        @pl.when(g == 0)
        def _():
            for job in jobs:
                job.start()

        @pl.when(g == 2 * steps)
        def _():
            for job in jobs:
                for j in range(3):
                    job.pass_on(j)

        @pl.when(g < last)
        def _():
            prod = lax.dot_general(a_ref[...], b_ref[...], TN, preferred_element_type=F32)

            @pl.when(t == 0)
            def _():
                acc[...] = prod

            @pl.when(t > 0)
            def _():
                acc[...] += prod

            @pl.when(t == steps - 1)
            def _():
                @pl.when(tile > 0)
                def _():
                    exchange(tile - 1).wait_send()

                kept[tile % 2] = acc[pl.ds(pl.multiple_of(mc * half, 8), half), :].astype(BF16)
                sent[...] = acc[pl.ds(pl.multiple_of((1 - mc) * half, 8), half), :].astype(BF16)
                exchange(tile).start()

        @pl.when((t == 0) & (g > 0))
        def _():
            q = tile - 1
            exchange(q).wait_recv()
            out_ref[0] = (kept[q % 2].astype(F32) + got[q % 2].astype(F32)).astype(BF16)

        @pl.when(g == last)
        def _():
            exchange(3).wait_send()
            for job in jobs:
                job.wait_sibling(0)
                for j in range(3):
                    job.wait_sibling(4 + j)
                job.wait_sends()

    def a_tile(g):
        gg = jnp.minimum(g, last - 1)
        return (gg % steps, gg // steps)

    def b_tile(g):
        return (jnp.minimum(g, last - 1) % steps, 0)

    hbm = pl.BlockSpec(memory_space=pl.ANY)
    outs = pl.pallas_call(
        body, name=name, grid=(last + 1,),
        in_specs=[pl.BlockSpec((bt, bm), a_tile), pl.BlockSpec((bt, n), b_tile)] + [hbm] * n_g,
        out_specs=[pl.BlockSpec((1, half, n), lambda g: (jnp.maximum(g - 1, 0) // steps, 0, 0))] + [hbm] * n_g,
        out_shape=[SDS((4, half, n), BF16)] + [SDS(gb.shape, gb.dtype) for gb in gathers],
        scratch_shapes=[pltpu.VMEM((bm, n), F32), pltpu.VMEM((2, half, n), BF16), pltpu.VMEM((2, half, n), BF16),
                        pltpu.VMEM((half, n), BF16), pltpu.SemaphoreType.DMA((4,)), pltpu.SemaphoreType.DMA((4,))]
        + _gather_scratch() * n_g,
        input_output_aliases={2 + k: 1 + k for k in range(n_g)},
        compiler_params=_params("arbitrary"),
    )(a, b, *gathers)
    return outs[0], outs[1:]


def _in_proj_bwd(dproj, wt, x, dx1, scale, norm_g, sums_o, pair):
    s = x.shape[0]
    tm, tk, tr = min(1024, s), D_IN // 4, 64
    ksteps = D_IN // tk

    def body(dp_ref, wt_ref, x_hbm, dx1_hbm, sc_ref, g_ref, so_ref, pair_ref, gx_ref, sums_ref, parts_ref, x_buf,
             dx1_buf, tile_sems, send_sems, recv_sems):
        i, k = pl.program_id(0), pl.program_id(1)

        def tile_copies():
            rows = pl.ds(pl.multiple_of(i * tm, tm), tm)
            return (pltpu.make_async_copy(x_hbm.at[rows], x_buf, tile_sems.at[0]),
                    pltpu.make_async_copy(dx1_hbm.at[rows], dx1_buf, tile_sems.at[1]))

        @pl.when((i == 0) & (k == 0))
        def _():
            for cp in _chip_scatter(pair_ref, parts_ref, send_sems, recv_sems):
                cp.start()
            sums_ref[...] = so_ref[...]

        @pl.when(k == 0)
        def _():
            for cp in tile_copies():
                cp.start()
            gx_ref[...] = jnp.dot(dp_ref[...], wt_ref[...], preferred_element_type=F32)

        @pl.when(k > 0)
        def _():
            gx_ref[...] += jnp.dot(dp_ref[...], wt_ref[...], preferred_element_type=F32)

        @pl.when(k == ksteps - 1)
        def _():
            for cp in tile_copies():
                cp.wait()
            one_sc, g = 1.0 + sc_ref[...], g_ref[...]
            cs = one_sc * g

            def chunk(j, sums):
                rows = pl.ds(pl.multiple_of(j * tr, tr), tr)
                dh, xv = gx_ref[rows, :], x_buf[rows, :]
                dhx = dh * xv
                r = lax.rsqrt(jnp.sum(xv * xv, axis=-1, keepdims=True) * (1.0 / D_MODEL) + NORM_EPS)
                coef = (r * r * r) * (jnp.sum(dhx * cs, axis=-1, keepdims=True) * (1.0 / D_MODEL))
                gx_ref[rows, :] = dx1_buf[rows, :] + r * (dh * cs) - xv * coef
                return (sums[0] + jnp.sum(dh, axis=0, keepdims=True), sums[1] + jnp.sum(dhx * r, axis=0, keepdims=True))

            zero = jnp.zeros((1, D_MODEL), F32)
            sums = lax.fori_loop(0, tm // tr, chunk, (zero, zero))
            sums_ref[SUM_SHIFT:SUM_SHIFT + 1, :] += sums[0]
            sums_ref[SUM_SCALE:SUM_SCALE + 1, :] += sums[1] * g
            sums_ref[SUM_NORM_G:SUM_NORM_G + 1, :] += sums[1] * one_sc

        @pl.when((i == s // tm - 1) & (k == ksteps - 1))
        def _():
            scatter = _chip_scatter(pair_ref, parts_ref, send_sems, recv_sems)
            for cp in scatter:
                cp.wait_recv()
            for cp in scatter:
                cp.wait_send()

    row = pl.BlockSpec((1, D_MODEL), lambda i, k: (0, 0))
    hbm = pl.BlockSpec(memory_space=pl.ANY)
    return pl.pallas_call(
        body, name="in_proj_bwd", grid=(s // tm, ksteps),
        in_specs=[pl.BlockSpec((tm, tk), lambda i, k: (i, k)), pl.BlockSpec((tk, D_MODEL), lambda i, k: (k, 0)),
                  hbm, hbm, row, row, pl.BlockSpec((8, D_MODEL), lambda i, k: (0, 0)), hbm],
        out_specs=[pl.BlockSpec((tm, D_MODEL), lambda i, k: (i, 0)), pl.BlockSpec((8, D_MODEL), lambda i, k: (0, 0)),
                   hbm],
        out_shape=[SDS((s, D_MODEL), F32), SDS((8, D_MODEL), F32), SDS((3,) + pair.shape[1:], pair.dtype)],
        scratch_shapes=[pltpu.VMEM((tm, D_MODEL), F32), pltpu.VMEM((tm, D_MODEL), F32),
                        pltpu.SemaphoreType.DMA((2,)), *_scatter_scratch()],
        compiler_params=_params("arbitrary", "arbitrary"),
    )(dproj, wt, x, dx1, scale, norm_g, sums_o, pair)


def _sum_chips(own_ref, parts_ref):
    return ((own_ref[0].astype(F32) + parts_ref[0].astype(F32)) + parts_ref[1].astype(F32)) + parts_ref[2].astype(F32)


def _adam_rows(name, chip, pair, parts, w, m, v, tr):
    rows = w.shape[0]

    def body(chip_ref, own_ref, p_ref, w_ref, m_ref, v_ref, g_ref, d_ref, nm_ref, nv_ref):
        g = _sum_chips(own_ref, p_ref)
        g_ref[...] = g
        d_ref[...], nm_ref[...], nv_ref[...] = _adamw(w_ref[...], g, m_ref[...], v_ref[...])

    blk = pl.BlockSpec((tr, D_MODEL), lambda j, chip_ref: (j, 0))
    return pl.pallas_call(
        body, name=name,
        grid_spec=pltpu.PrefetchScalarGridSpec(
            num_scalar_prefetch=1, grid=(rows // tr,),
            in_specs=[pl.BlockSpec((1, tr, D_MODEL), lambda j, chip_ref: (chip_ref[0], j, 0)),
                      pl.BlockSpec((3, tr, D_MODEL), lambda j, chip_ref: (0, j, 0)), blk, blk, blk],
            out_specs=[blk] * 4),
        out_shape=[SDS(w.shape, F32)] * 4, compiler_params=_params("parallel"),
    )(chip, pair, parts, w, m, v)


def _adam_ada(name, cact, dmod, w, m, v):
    n = w.shape[1]
    tr = 512

    def body(c_ref, dm_ref, w_ref, m_ref, v_ref, g_ref, d_ref, nm_ref, nv_ref):
        pad_c = jnp.concatenate([c_ref[...], jnp.zeros_like(c_ref)], axis=0).astype(BF16)
        pad_d = jnp.concatenate([dm_ref[...], jnp.zeros_like(dm_ref)], axis=0).astype(BF16)
        g = lax.dot_general(pad_c, pad_d, TN, preferred_element_type=F32)
        g_ref[...] = g
        d_ref[...], nm_ref[...], nv_ref[...] = _adamw(w_ref[...], g, m_ref[...], v_ref[...])

    blk = pl.BlockSpec((tr, n), lambda j: (j, 0))
    return pl.pallas_call(
        body, name=name, grid=(D_MODEL // tr,),
        in_specs=[pl.BlockSpec((N_DEV, tr), lambda j: (0, j)), pl.BlockSpec((N_DEV, n), lambda j: (0, 0)),
                  blk, blk, blk],
        out_specs=[blk] * 4, out_shape=[SDS(w.shape, F32)] * 4,
        compiler_params=_params("parallel"),
    )(cact, dmod, w, m, v)


SMALL_PARAMS = ("w_spatial", "b_spatial", "sinks", "norm_g", "ln_v_g", "ln_v_b", "final_norm_g", "b_ada", "b_ada_final")


def _adam_small(d_wsp, misc, d_ln, sums, params):
    n_p = len(SMALL_PARAMS)

    def body(*refs):
        wsp_ref, misc_ref, ln_ref, sums_ref = refs[:4]
        wmv = [refs[4 + 3 * k:7 + 3 * k] for k in range(n_p)]
        loss_ref = refs[4 + 3 * n_p]
        outs = [refs[5 + 3 * n_p + 4 * k:9 + 3 * n_p + 4 * k] for k in range(n_p)]

        def column_sum(row):
            return total(sums_ref, (row, row + 1))

        def total(ref, rows=None):
            def part(j):
                return ref[j] if rows is None else ref[j, rows[0]:rows[1], :]
            acc = part(0)
            for j in range(1, N_DEV):
                acc = acc + part(j)
            return acc

        sink_rows = total(misc_ref, (ROW_DSINKS, ROW_DSINKS + 16))
        diag = (lax.broadcasted_iota(jnp.int32, (16, 128), 0) == lax.broadcasted_iota(jnp.int32, (16, 128), 1))
        grads = dict(
            w_spatial=total(wsp_ref), b_spatial=total(misc_ref, (ROW_DBSP, ROW_DBSP + A_GROUPS)),
            sinks=jnp.sum(jnp.where(diag, sink_rows, 0.0), axis=0, keepdims=True),
            norm_g=column_sum(SUM_NORM_G), ln_v_g=total(ln_ref, (0, 1)), ln_v_b=total(ln_ref, (1, 2)),
            final_norm_g=column_sum(SUM_FNG),
            b_ada=jnp.concatenate([column_sum(SUM_SHIFT), column_sum(SUM_SCALE), column_sum(SUM_GATE)], axis=1),
            b_ada_final=jnp.concatenate([column_sum(SUM_SHIFT_F), column_sum(SUM_SCALE_F)], axis=1))
        sq_err = jnp.sum(column_sum(SUM_SQ_ERR), axis=1, keepdims=True)
        loss_ref[...] = jnp.broadcast_to(sq_err * (0.5 / D_MODEL), (1, 128))
        for k, name in enumerate(SMALL_PARAMS):
            w_ref, m_ref, v_ref = wmv[k]
            g_ref, d_ref, nm_ref, nv_ref = outs[k]
            g_ref[...] = grads[name]
            d_ref[...], nm_ref[...], nv_ref[...] = _adamw(w_ref[...], grads[name], m_ref[...], v_ref[...])

    flat = [a for name in SMALL_PARAMS for a in params[name]]
    vmem = pl.BlockSpec(memory_space=pltpu.VMEM)
    out_shape = [SDS((1, 128), F32)] + [SDS(params[name][0].shape, F32) for name in SMALL_PARAMS for _ in range(4)]
    outs = pl.pallas_call(
        body, name="adam_small", in_specs=[vmem] * (4 + len(flat)), out_specs=[vmem] * len(out_shape),
        out_shape=out_shape, compiler_params=_params(),
    )(d_wsp, misc, d_ln, sums, *flat)
    return outs[0], {name: outs[1 + 4 * k:5 + 4 * k] for k, name in enumerate(SMALL_PARAMS)}


def kernel(x, c, w_ada, b_ada, norm_g, w_in, ln_v_g, ln_v_b, w_spatial, b_spatial, sinks, w_out, w_ada_final, b_ada_final, final_norm_g, loss_target, m_w_ada, m_b_ada, m_norm_g, m_w_in, m_ln_v_g, m_ln_v_b, m_w_spatial, m_b_spatial, m_sinks, m_w_out, m_w_ada_final, m_b_ada_final, m_final_norm_g, v_w_ada, v_b_ada, v_norm_g, v_w_in, v_ln_v_g, v_ln_v_b, v_w_spatial, v_b_spatial, v_sinks, v_w_out, v_w_ada_final, v_b_ada_final, v_final_norm_g):
    me = 4 * lax.axis_index("x") + 2 * lax.axis_index("y") + lax.axis_index("c")
    x2, tgt = x[0], loss_target[0]
    fng = final_norm_g.reshape(1, D_MODEL)

    n_ada, n_ada_f = w_ada.shape[2], w_ada_final.shape[1]
    cact, mod, mod_f = _ada_exchange(c, w_ada[0], b_ada.reshape(N_DEV, n_ada), w_ada_final,
                                     b_ada_final.reshape(N_DEV, n_ada_f))
    cact = cact.reshape(N_DEV, D_MODEL)
    mod, mod_f = mod.reshape(1, 3 * D_MODEL), mod_f.reshape(1, 2 * D_MODEL)
    shift, scale, gate = mod[:, :D_MODEL], mod[:, D_MODEL:2 * D_MODEL], mod[:, 2 * D_MODEL:]
    shift_f, scale_f = mod_f[:, :D_MODEL], mod_f[:, D_MODEL:]

    wt_f32, m_wt, v_wt = (jnp.swapaxes(a, 1, 2)[0] for a in (w_in, m_w_in, v_w_in))
    xi, yi = lax.axis_index("x"), lax.axis_index("y")
    chip_order = jnp.stack([2 * xi + yi, 2 * (1 - xi) + yi, 2 * xi + 1 - yi, 2 * (1 - xi) + 1 - yi]).astype(jnp.int32)
    wt_mine, wo_mine = _prep_weights(me.reshape(1), wt_f32, w_out[0])

    freqs = _rope_freqs()
    sinks_v = sinks.reshape(16)
    h, proj, wt = _gather_in_proj(chip_order, x2, shift, scale, norm_g, wt_mine)
    y, wo = _mixer_fwd(proj, freqs, ln_v_g, ln_v_b, w_spatial[0], b_spatial[0], sinks_v, wo_mine)
    dx1, do, dy, sums_o = _out_proj_loss(y, x2, tgt, wo, gate, shift_f, scale_f, fng)

    chip = (2 * lax.axis_index("x") + lax.axis_index("y")).reshape(1)
    pair_out, _ = _wgrad_pair("wgrad_out", y, do, 2048)
    dproj, d_ln, d_wsp, misc, parts_out = _mixer_bwd(
        me.reshape(1), proj, dy, freqs, ln_v_g, ln_v_b, w_spatial[0], b_spatial[0], sinks_v, pair_out)
    pair_in, (d_ln, d_wsp, misc) = _wgrad_pair(
        "wgrad_in", dproj, h, 1024, gathers=(d_ln, d_wsp.reshape(N_DEV * A_GROUPS * CHUNK, CHUNK), misc))
    grad_x, sums, parts_in = _in_proj_bwd(dproj, wt, x2, dx1, scale, norm_g, sums_o, pair_in)
    wt_leaves = [jnp.swapaxes(a[None], 1, 2)
                 for a in _adam_rows("adam_w_in", chip, pair_in, parts_in, wt_f32, m_wt, v_wt, 176)]
    w_out_leaves = [a[None] for a in _adam_rows("adam_w_out", chip, pair_out, parts_out, w_out[0], m_w_out[0], v_w_out[0], 64)]

    (sums,) = _all_gather("gather_sums", [sums], pltpu.VMEM)
    natural = dict(w_spatial=(A_GROUPS * CHUNK, CHUNK), b_spatial=(A_GROUPS, CHUNK), sinks=(1, 16), norm_g=(1, D_MODEL),
                   ln_v_g=(1, D_A), ln_v_b=(1, D_A), final_norm_g=(1, D_MODEL), b_ada=(1, 3 * D_MODEL),
                   b_ada_final=(1, 2 * D_MODEL))
    given = dict(
        w_spatial=(w_spatial, m_w_spatial, v_w_spatial), b_spatial=(b_spatial, m_b_spatial, v_b_spatial),
        sinks=(sinks, m_sinks, v_sinks), norm_g=(norm_g, m_norm_g, v_norm_g), ln_v_g=(ln_v_g, m_ln_v_g, v_ln_v_g),
        ln_v_b=(ln_v_b, m_ln_v_b, v_ln_v_b), final_norm_g=(final_norm_g, m_final_norm_g, v_final_norm_g),
        b_ada=(b_ada, m_b_ada, v_b_ada), b_ada_final=(b_ada_final, m_b_ada_final, v_b_ada_final))
    params = {name: tuple(a.reshape(natural[name]) for a in given[name]) for name in SMALL_PARAMS}
    params["sinks"] = tuple(jnp.pad(a, ((0, 0), (0, 128 - 16))) for a in params["sinks"])
    loss, small = _adam_small(d_wsp.reshape(N_DEV, A_GROUPS * CHUNK, CHUNK), misc.reshape(N_DEV, MISC_ROWS, 128),
                              d_ln.reshape(N_DEV, 8, D_A), sums, params)
    small["sinks"] = [a[:, :16] for a in small["sinks"]]
    small = {name: [a.reshape(given[name][0].shape) for a in small[name]] for name in SMALL_PARAMS}

    dmod_all = jnp.concatenate([sums[:, SUM_SHIFT], sums[:, SUM_SCALE], sums[:, SUM_GATE]], axis=1)
    dmod_f_all = jnp.concatenate([sums[:, SUM_SHIFT_F], sums[:, SUM_SCALE_F]], axis=1)
    dmod_mine = lax.dynamic_slice_in_dim(dmod_all, me * n_ada, n_ada, axis=1)
    dmod_f_mine = lax.dynamic_slice_in_dim(dmod_f_all, me * n_ada_f, n_ada_f, axis=1)
    ada = _adam_ada("adam_w_ada", cact, dmod_mine, w_ada[0], m_w_ada[0], v_w_ada[0])
    ada_f = _adam_ada("adam_w_ada_final", cact, dmod_f_mine, w_ada_final, m_w_ada_final, v_w_ada_final)

    def leaves(k):
        return (ada[k][None], small["b_ada"][k], small["norm_g"][k], wt_leaves[k], small["ln_v_g"][k],
                small["ln_v_b"][k], small["w_spatial"][k], small["b_spatial"][k], small["sinks"][k], w_out_leaves[k],
                ada_f[k], small["b_ada_final"][k], small["final_norm_g"][k])

    return (loss[0, 0], grad_x[None], *leaves(0), *leaves(1), *leaves(2), *leaves(3))
```

```python
import jax
import jax.numpy as jnp
from jax import lax
from jax.experimental import pallas as pl
from jax.experimental.pallas import tpu as pltpu

D_MODEL = 2048
D_IN = 5632
D_A = 1024
CHUNK = 128
A_GROUPS = 8
HEAD_DIM = 64
N_KV_HEADS = 4
N_DEV = 8
ROPE_THETA = 10000.0
NORM_EPS = 1e-5
ATTN_SCALE = HEAD_DIM ** -0.5

ADAM_LR = 0.001
ADAM_B1 = 0.9
ADAM_B2 = 0.999
ADAM_EPS = 1e-08
ADAM_WD = 0.01
ADAM_STEP = 10

OFF_U, OFF_VA, OFF_ZA, OFF_Q, OFF_K, OFF_V, OFF_ZB = 0, 1024, 2048, 3072, 4096, 4352, 4608

SUM_SHIFT, SUM_SCALE, SUM_NORM_G, SUM_GATE, SUM_SHIFT_F, SUM_SCALE_F, SUM_FNG, SUM_SQ_ERR = range(8)

V7X_VMEM_LIMIT_BYTES = 56 * 1024 * 1024

F32 = jnp.float32
BF16 = jnp.bfloat16
MESH = pl.DeviceIdType.MESH
SDS = jax.ShapeDtypeStruct
NT = (((1,), (1,)), ((), ()))
TN = (((0,), (0,)), ((), ()))


def _params(*semantics):
    return pltpu.CompilerParams(dimension_semantics=semantics or None, vmem_limit_bytes=V7X_VMEM_LIMIT_BYTES)


def _mesh_pos():
    return lax.axis_index("x"), lax.axis_index("y"), lax.axis_index("c")


def _sigmoid(z):
    return 1.0 / (1.0 + jnp.exp(-z))


def _adamw(w, g, m, v):
    m = ADAM_B1 * m + (1.0 - ADAM_B1) * g
    v = ADAM_B2 * v + (1.0 - ADAM_B2) * (g * g)
    m_hat = m / (1.0 - ADAM_B1 ** ADAM_STEP)
    v_hat = v / (1.0 - ADAM_B2 ** ADAM_STEP)
    delta = -ADAM_LR * (m_hat / (jnp.sqrt(v_hat) + ADAM_EPS) + ADAM_WD * w)
    return delta, m, v


def _all_gather(name, blocks, memory_space):
    n_arr = len(blocks)

    def body(*refs):
        ins, outs = refs[:n_arr], refs[n_arr:2 * n_arr]
        send_sems, recv_sems, local_sems = refs[2 * n_arr:]
        x, y, c = _mesh_pos()
        me, sibling = (x, y, c), (x, y, 1 - c)
        chips = [(1 - x, y), (x, 1 - y), (1 - x, 1 - y)]

        def slot(p):
            return 4 * p[0] + 2 * p[1] + p[2]

        def copy(a, k, block, to, src=None):
            dst = outs[a].at[slot(block)]
            return pltpu.make_async_remote_copy(
                src_ref=dst if src is None else src, dst_ref=dst,
                send_sem=send_sems.at[a, k], recv_sem=recv_sems.at[a, k],
                device_id=to, device_id_type=MESH)

        mine = [pltpu.make_async_copy(ins[a], outs[a].at[slot(me)], local_sems.at[a]) for a in range(n_arr)]
        for cp in mine:
            cp.start()
        first = []
        for a in range(n_arr):
            first.append(copy(a, 0, me, sibling, src=ins[a]))
            first += [copy(a, 1 + j, me, (*chip, c), src=ins[a]) for j, chip in enumerate(chips)]
        for cp in first:
            cp.start()
        passed = []
        for j, chip in enumerate(chips):
            for a in range(n_arr):
                copy(a, 1 + j, (*chip, c), me).wait_recv()
                fwd = copy(a, 4 + j, (*chip, c), sibling)
                fwd.start()
                passed.append(fwd)
        for a in range(n_arr):
            copy(a, 0, sibling, me).wait_recv()
            for j, chip in enumerate(chips):
                copy(a, 4 + j, (*chip, 1 - c), me).wait_recv()
        for cp in first + passed:
            cp.wait_send()
        for cp in mine:
            cp.wait()

    spec = pl.BlockSpec(memory_space=memory_space)
    return pl.pallas_call(
        body, name=name,
        out_shape=[SDS((N_DEV,) + b.shape, b.dtype) for b in blocks],
        in_specs=[spec] * n_arr, out_specs=[spec] * n_arr,
        scratch_shapes=[pltpu.SemaphoreType.DMA((n_arr, 7)), pltpu.SemaphoreType.DMA((n_arr, 7)),
                        pltpu.SemaphoreType.DMA((n_arr,))],
        compiler_params=_params(),
    )(*blocks)


def _ada_exchange(c, w_ada, b_ada8, w_ada_f, b_ada_f8):
    n1, n2 = w_ada.shape[1], w_ada_f.shape[1]

    def body(c_ref, w1_ref, b1_ref, w2_ref, b2_ref, cact_ref, mod_ref, modf_ref,
             cact_buf, res1, res2, send1, send2, sems_s, sems_r):
        x, y, c_pos = _mesh_pos()
        me = 4 * x + 2 * y + c_pos
        flips = [(k >> 2 & 1, k >> 1 & 1, k & 1) for k in range(1, N_DEV)]

        def peer(f):
            return (1 - x if f[0] else x, 1 - y if f[1] else y, 1 - c_pos if f[2] else c_pos)

        cv = c_ref[...]
        cact = cv * _sigmoid(cv)
        cact_buf[...] = cact
        cact_ref[me] = cact

        def rdma(phase, k, src, dst, f):
            return pltpu.make_async_remote_copy(src_ref=src, dst_ref=dst, send_sem=sems_s.at[phase, k],
                                                recv_sem=sems_r.at[phase, k], device_id=peer(f), device_id_type=MESH)

        gather = [rdma(0, k, cact_buf, cact_ref.at[me], f) for k, f in enumerate(flips)]
        for cp in gather:
            cp.start()
        for cp in gather:
            cp.wait_recv()
        for cp in gather:
            cp.wait_send()

        rid = lax.broadcasted_iota(jnp.int32, (N_DEV, D_MODEL), 0)
        rows = jnp.zeros((N_DEV, D_MODEL), F32)
        for j in range(N_DEV):
            rows = jnp.where(rid == j, jnp.broadcast_to(cact_ref[j], (N_DEV, D_MODEL)), rows)
        rows = rows.astype(BF16)
        res1[...] = jnp.dot(rows, w1_ref[...].astype(BF16), preferred_element_type=F32) + b1_ref[pl.ds(me, 1), :]
        res2[...] = jnp.dot(rows, w2_ref[...].astype(BF16), preferred_element_type=F32) + b2_ref[pl.ds(me, 1), :]
        for j in range(N_DEV):
            send1[j] = res1[pl.ds(j, 1), :]
            send2[j] = res2[pl.ds(j, 1), :]
        mod_ref[me] = send1[me]
        modf_ref[me] = send2[me]
        scatter = []
        for k, f in enumerate(flips):
            to = me ^ (k + 1)
            scatter.append(rdma(1, k, send1.at[to], mod_ref.at[me], f))
            scatter.append(rdma(2, k, send2.at[to], modf_ref.at[me], f))
        for cp in scatter:
            cp.start()
        for cp in scatter:
            cp.wait_recv()
        for cp in scatter:
            cp.wait_send()

    vmem = pl.BlockSpec(memory_space=pltpu.VMEM)
    return pl.pallas_call(
        body, name="ada_exchange",
        out_shape=[SDS((N_DEV, 1, D_MODEL), F32), SDS((N_DEV, 1, n1), F32), SDS((N_DEV, 1, n2), F32)],
        in_specs=[vmem] * 5, out_specs=[vmem] * 3,
        scratch_shapes=[pltpu.VMEM((1, D_MODEL), F32), pltpu.VMEM((N_DEV, n1), F32), pltpu.VMEM((N_DEV, n2), F32),
                        pltpu.VMEM((N_DEV, 1, n1), F32), pltpu.VMEM((N_DEV, 1, n2), F32),
                        pltpu.SemaphoreType.DMA((3, 7)), pltpu.SemaphoreType.DMA((3, 7))],
        compiler_params=_params(),
    )(c, w_ada, b_ada8, w_ada_f, b_ada_f8)


def _chip_scatter(pair_ref, parts_ref, send_sems, recv_sems):
    x, y, c = _mesh_pos()
    chips = [(1 - x, y), (x, 1 - y), (1 - x, 1 - y)]
    return [pltpu.make_async_remote_copy(
        src_ref=pair_ref.at[2 * cx + cy], dst_ref=parts_ref.at[j], send_sem=send_sems.at[j], recv_sem=recv_sems.at[j],
        device_id=(cx, cy, c), device_id_type=MESH) for j, (cx, cy) in enumerate(chips)]


def _scatter_scratch():
    return [pltpu.SemaphoreType.DMA((3,)), pltpu.SemaphoreType.DMA((3,))]


def _prep_weights(me, wt, w_out):
    steps = 4

    def body(me_ref, wt_ref, wo_ref, wtb_ref, wob_ref):
        wtb_ref[...] = wt_ref[...].astype(BF16)
        wob_ref[...] = wo_ref[...].astype(BF16)

    def rows(a, mine):
        blk = (a.shape[0] // steps, a.shape[1])
        return pl.BlockSpec(blk, (lambda i, me_ref: (steps * me_ref[0] + i, 0)) if mine else (lambda i, me_ref: (i, 0)))

    return pl.pallas_call(
        body, name="prep_weights",
        grid_spec=pltpu.PrefetchScalarGridSpec(
            num_scalar_prefetch=1, grid=(steps,),
            in_specs=[rows(wt, False), rows(w_out, False)], out_specs=[rows(wt, True), rows(w_out, True)]),
        out_shape=[SDS((N_DEV * wt.shape[0], D_MODEL), BF16), SDS((N_DEV * w_out.shape[0], D_MODEL), BF16)],
        compiler_params=_params("parallel"),
    )(me, wt, w_out)


class _InPlaceGather:
    def __init__(self, buf_ref, send_sems, recv_sems, relay=False):
        self.buf, self.send_sems, self.recv_sems, self.relay = buf_ref, send_sems, recv_sems, relay
        self.n = buf_ref.shape[0] // N_DEV
        x, y, c = _mesh_pos()
        self.me, self.sibling, self.core = (x, y, c), (x, y, 1 - c), c
        self.chips = [(1 - x, y), (x, 1 - y), (1 - x, 1 - y)]
        self.relay_from = (jnp.where(c == 0, 1 - x, x), jnp.where(c == 0, y, 1 - y), c)
        self.relay_to = (jnp.where(c == 0, x, 1 - x), jnp.where(c == 0, 1 - y, y), c)

    def copy(self, k, block, to):
        start = pl.multiple_of((4 * block[0] + 2 * block[1] + block[2]) * self.n, self.n)
        rows = self.buf.at[pl.ds(start, self.n)]
        return pltpu.make_async_remote_copy(src_ref=rows, dst_ref=rows, send_sem=self.send_sems.at[k],
                                            recv_sem=self.recv_sems.at[k], device_id=to, device_id_type=MESH)

    def start(self):
        self.copy(0, self.me, self.sibling).start()
        for j, chip in enumerate(self.chips[:2] if self.relay else self.chips):
            self.copy(1 + j, self.me, (*chip, self.core)).start()

    def relay_diagonal(self):
        self.copy(3, self.relay_from, self.relay_to).start()

    def pass_on(self, j):
        self.copy(1 + j, (*self.chips[j], self.core), self.me).wait_recv()
        self.copy(4 + j, (*self.chips[j], self.core), self.sibling).start()

    def wait_sibling(self, k):
        self.copy(k, self.sibling, self.me).wait_recv()

    def wait_sends(self):
        for k in range(7):
            self.copy(k, self.me, self.sibling).wait_send()


def _gather_scratch():
    return [pltpu.SemaphoreType.DMA((7,)), pltpu.SemaphoreType.DMA((7,))]


def _gather_in_proj(order, x, shift, scale, norm_g, wt_all):
    s = x.shape[0]
    th, tm = min(512, s), min(1024, s)
    nh, ni = s // th, s // tm
    tn = D_IN // 4
    steps = nh + 4 * ni

    def body(order_ref, x_ref, shift_ref, scale_ref, g_ref, wt_in, h_ref, proj_ref, wt_ref,
             h_scr, w_buf, load_sems, send_sems, recv_sems):
        g = pl.program_id(0)
        gather = _InPlaceGather(wt_ref, send_sems, recv_sems, relay=True)

        def tile_load(slot, chip):
            return pltpu.make_async_copy(wt_ref.at[pl.ds(pl.multiple_of(chip * tn, tn), tn)], w_buf.at[slot],
                                         load_sems.at[slot])

        @pl.when(g == 0)
        def _():
            gather.start()

        @pl.when(g < nh)
        def _():
            xv = x_ref[...]
            r = lax.rsqrt(jnp.mean(xv * xv, axis=-1, keepdims=True) + NORM_EPS)
            hb = (((xv * r) * g_ref[...]) * (1.0 + scale_ref[...]) + shift_ref[...]).astype(BF16)
            h_ref[...] = hb
            h_scr[pl.ds(pl.multiple_of(g * th, th), th), :] = hb

        @pl.when(g == nh - 1)
        def _():
            gather.wait_sibling(0)
            tile_load(0, order_ref[0]).start()

        @pl.when(g >= nh)
        def _():
            t, i = (g - nh) // ni, (g - nh) % ni

            @pl.when(i == 0)
            def _():
                tile_load(t % 2, order_ref[t]).wait()

            @pl.when((i == ni - 1) & (t == 0))
            def _():
                gather.pass_on(0)
                gather.pass_on(1)
                gather.relay_diagonal()

            @pl.when((i == ni // 2) & (t == 2))
            def _():
                gather.pass_on(2)

            for j in range(3):
                @pl.when((i == ni - 1) & (t == j))
                def _():
                    gather.wait_sibling(4 + j)
                    tile_load((j + 1) % 2, order_ref[j + 1]).start()

            lhs = h_scr[pl.ds(pl.multiple_of(i * tm, tm), tm), :]
            proj_ref[...] = lax.dot_general(lhs, w_buf[t % 2], NT, preferred_element_type=F32).astype(BF16)

        @pl.when(g == steps - 1)
        def _():
            gather.wait_sends()

    def h_tile(g, order_ref):
        return (jnp.minimum(g, nh - 1), 0)

    def proj_tile(g, order_ref):
        mm = jnp.maximum(g - nh, 0)
        return (mm % ni, order_ref[mm // ni])

    row = pl.BlockSpec((1, D_MODEL), lambda g, order_ref: (0, 0))
    hbm = pl.BlockSpec(memory_space=pl.ANY)
    return pl.pallas_call(
        body, name="gather_in_proj",
        grid_spec=pltpu.PrefetchScalarGridSpec(
            num_scalar_prefetch=1, grid=(steps,),
            in_specs=[pl.BlockSpec((th, D_MODEL), h_tile), row, row, row, hbm],
            out_specs=[pl.BlockSpec((th, D_MODEL), h_tile), pl.BlockSpec((tm, tn), proj_tile), hbm],
            scratch_shapes=[pltpu.VMEM((s, D_MODEL), BF16), pltpu.VMEM((2, tn, D_MODEL), BF16),
                            pltpu.SemaphoreType.DMA((2,)), *_gather_scratch()]),
        out_shape=[SDS((s, D_MODEL), BF16), SDS((s, D_IN), BF16), SDS(wt_all.shape, BF16)],
        input_output_aliases={5: 2},
        compiler_params=_params("arbitrary"),
    )(order, x, shift, scale, norm_g, wt_all)


def _rope_freqs():
    inv_freq = ROPE_THETA ** (-jnp.arange(0, HEAD_DIM, 2, dtype=F32) / HEAD_DIM)
    return jnp.tile(inv_freq, 4).reshape(1, 128)


class _RopeTables:
    def __init__(self, freq_ref, rows_ref, state_ref, last_ref):
        self.freq, self.rows, self.state, self.last = freq_ref, rows_ref, state_ref, last_ref

    def start(self, block, direction):
        ang = lax.broadcasted_iota(jnp.int32, (CHUNK, 128), 0).astype(F32) * self.freq[...]
        self.rows[0] = jnp.cos(ang)
        self.rows[1] = jnp.sin(ang)
        base = jnp.asarray(block * CHUNK, dtype=F32) * self.freq[...]
        turn = float(direction * CHUNK) * self.freq[...]
        self.state[0:1, :] = jnp.cos(base)
        self.state[1:2, :] = jnp.sin(base)
        self.state[2:3, :] = jnp.cos(turn)
        self.state[3:4, :] = jnp.sin(turn)

    def step(self):
        c, s, ct, st = (self.state[k:k + 1, :] for k in range(4))
        self.state[0:1, :] = c * ct - s * st
        self.state[1:2, :] = s * ct + c * st

    def tables(self):
        c, s = self.state[0:1, :], self.state[1:2, :]
        cos = c * self.rows[0] - s * self.rows[1]
        sin = s * self.rows[0] + c * self.rows[1]
        first_half = (lax.broadcasted_iota(jnp.int32, (1, 128), 1) & (HEAD_DIM - 1)) < HEAD_DIM // 2
        return cos, jnp.where(first_half, -sin, 0.0), jnp.where(first_half, 0.0, sin)

    def keep(self, tabs):
        for k in range(3):
            self.last[k] = tabs[k]

    def kept(self):
        return tuple(self.last[k] for k in range(3))


def _rope_scratch():
    return [pltpu.VMEM((2, CHUNK, 128), F32), pltpu.VMEM((8, 128), F32), pltpu.VMEM((3, CHUNK, 128), F32)]


def _rope(v, cos, sin_lo, sin_hi):
    width = v.shape[1]
    rep = (1, width // 128)
    return (v * jnp.tile(cos, rep) + pltpu.roll(v, width - 32, 1) * jnp.tile(sin_lo, rep)
            + pltpu.roll(v, 32, 1) * jnp.tile(sin_hi, rep))


def _rope_bwd(d, cos, sin_lo, sin_hi):
    width = d.shape[1]
    rep = (1, width // 128)
    return (d * jnp.tile(cos, rep) + pltpu.roll(d * jnp.tile(sin_lo, rep), 32, 1)
            + pltpu.roll(d * jnp.tile(sin_hi, rep), width - 32, 1))


def _layer_norm(v, g, b):
    mu = jnp.mean(v, axis=-1, keepdims=True)
    vc = v - mu
    rstd = lax.rsqrt(jnp.mean(vc * vc, axis=-1, keepdims=True) + NORM_EPS)
    vhat = vc * rstd
    return vhat * g + b, vhat, rstd


def _set_tril(w_ref, out_ref):
    t = lax.broadcasted_iota(jnp.int32, (CHUNK, CHUNK), 0)
    tp = lax.broadcasted_iota(jnp.int32, (CHUNK, CHUNK), 1)
    for g in range(A_GROUPS):
        out_ref[g] = jnp.where(tp <= t, w_ref[g], 0.0).astype(BF16)


def _bias_columns(b_ref, out_ref):
    for g in range(A_GROUPS):
        out_ref[g] = jnp.broadcast_to(b_ref[pl.ds(g, 1), :], (CHUNK, CHUNK)).T


def _from_prev():
    r = lax.broadcasted_iota(jnp.int32, (CHUNK, 4 * CHUNK), 0)
    i = lax.broadcasted_iota(jnp.int32, (CHUNK, 4 * CHUNK), 1) & (CHUNK - 1)
    return r > i


def _set_unfold_masks(mask_ref):
    prev = _from_prev()
    mask_ref[0] = jnp.where(prev, 1.0, 0.0).astype(BF16)
    mask_ref[1] = jnp.where(prev, 0.0, 1.0).astype(BF16)


def _fold_band(t, from_prev):
    return jnp.where(from_prev, t[:CHUNK], t[CHUNK:])


def _unfold_band(t, mask_ref):
    return jnp.concatenate([t * mask_ref[0], t * mask_ref[1]], axis=0)


def _low_lanes():
    return lax.broadcasted_iota(jnp.int32, (1, 128), 1) < HEAD_DIM


def _stack_heads(pair_a, pair_b):
    lo = _low_lanes()
    return jnp.concatenate([jnp.where(lo, pair_a, 0.0), jnp.where(lo, 0.0, pair_a),
                            jnp.where(lo, pair_b, 0.0), jnp.where(lo, 0.0, pair_b)], axis=0).astype(BF16)


def _heads_to_lanes(per_group):
    rows = [t[:, r * CHUNK:(r + 1) * CHUNK] for t in per_group for r in range(4)]
    return jnp.concatenate(rows, axis=0).T


def _dup_kv_head(band, gk):
    pair = band[:, (gk // 2) * 128:(gk // 2 + 1) * 128]
    lo = _low_lanes()
    one = jnp.where(lo if gk % 2 == 0 else jnp.logical_not(lo), pair, 0.0)
    return (one + pltpu.roll(one, HEAD_DIM, 1)).astype(BF16)


def _fold_kv_head(dup_grad, gk):
    both = dup_grad + pltpu.roll(dup_grad, HEAD_DIM, 1)
    lo = _low_lanes()
    return jnp.where(lo if gk % 2 == 0 else jnp.logical_not(lo), both, 0.0)


def _attn_probs(q_st, k_dup, sink_row, from_prev, first_block):
    s = lax.dot_general(k_dup, q_st, NT, preferred_element_type=F32)
    no_prev = jnp.where(first_block, -jnp.inf, 0.0)
    s = jnp.where(from_prev, s[:CHUNK] + no_prev, s[CHUNK:])
    m = jnp.maximum(jnp.max(s, axis=0, keepdims=True), sink_row)
    p = jnp.exp(s - m)
    e_sink = jnp.exp(sink_row - m)
    inv = 1.0 / (jnp.sum(p, axis=0, keepdims=True) + e_sink)
    return p * inv, e_sink * inv


def _sink_row(sinks_ref, gk):
    return jnp.concatenate([jnp.full((1, CHUNK), sinks_ref[4 * gk + r], F32) for r in range(4)], axis=1)


def _mixer_specs(nb, rev):
    def blk(i):
        return nb - 1 - i if rev else i

    def prev(i):
        return jnp.maximum(blk(i) - 1, 0)

    return dict(
        cur=pl.BlockSpec((CHUNK, D_IN), lambda i, *_: (blk(i), 0)),
        prev_kv=pl.BlockSpec((CHUNK, 2 * 256), lambda i, *_: (prev(i), OFF_K // 512)),
        freq=pl.BlockSpec((1, 128), lambda i, *_: (0, 0)),
        vec=pl.BlockSpec((1, D_A), lambda i, *_: (0, 0)),
        wsp=pl.BlockSpec((A_GROUPS, CHUNK, CHUNK), lambda i, *_: (0, 0, 0)),
        bsp=pl.BlockSpec((A_GROUPS, CHUNK), lambda i, *_: (0, 0)),
        smem=pl.BlockSpec(memory_space=pltpu.SMEM),
        blk=blk,
    )


def _mixer_fwd(proj, freqs, ln_g, ln_b, w_sp, b_sp, sinks, wo_all):
    s = proj.shape[0]
    nb = s // CHUNK
    sp = _mixer_specs(nb, rev=False)

    def body(cur_ref, pkv_ref, freq_ref, lg_ref, lb_ref, w_ref, b_ref, sinks_ref, wo_in, y_ref, wo_ref,
             bcol, wtril, mask, rope_rows, rope_state, rope_last, send_sems, recv_sems):
        i = pl.program_id(0)
        gather = _InPlaceGather(wo_ref, send_sems, recv_sems)
        rope = _RopeTables(freq_ref, rope_rows, rope_state, rope_last)

        @pl.when(i == 0)
        def _():
            gather.start()
            _bias_columns(b_ref, bcol)
            _set_tril(w_ref, wtril)
            _set_unfold_masks(mask)
            rope.start(-1, 1)
            rope_last[...] = jnp.zeros_like(rope_last)

        @pl.when(i == (7 * nb) // 8)
        def _():
            for j in range(3):
                gather.pass_on(j)

        vln, _, _ = _layer_norm(cur_ref[:, OFF_VA:OFF_ZA].astype(F32), lg_ref[...], lb_ref[...])
        vln = vln.astype(BF16)

        def gating_group(g):
            cols = slice(g * 128, (g + 1) * 128)
            sg = jnp.dot(wtril[g], vln[:, cols], preferred_element_type=F32) + bcol[g]
            u = cur_ref[:, OFF_U + g * 128:OFF_U + (g + 1) * 128].astype(F32)
            z = cur_ref[:, OFF_ZA + g * 128:OFF_ZA + (g + 1) * 128].astype(F32)
            y_ref[:, cols] = (u * sg * (z * _sigmoid(z))).astype(BF16)

        rope.step()
        cur_t, prev_t = rope.tables(), rope.kept()
        rope.keep(cur_t)
        qr = _rope(cur_ref[:, OFF_Q:OFF_K].astype(F32), *cur_t) * ATTN_SCALE
        kr = jnp.concatenate([_rope(pkv_ref[:, 0:256].astype(F32), *prev_t),
                              _rope(cur_ref[:, OFF_K:OFF_V].astype(F32), *cur_t)], axis=0)
        v_t = jnp.concatenate([pkv_ref[:, 256:512], cur_ref[:, OFF_V:OFF_ZB]], axis=0).astype(F32).T.astype(BF16)
        outs = []
        from_prev = _from_prev()
        for gk in range(N_KV_HEADS):
            q_st = _stack_heads(qr[:, (2 * gk) * 128:(2 * gk + 1) * 128], qr[:, (2 * gk + 1) * 128:(2 * gk + 2) * 128])
            probs, _ = _attn_probs(q_st, _dup_kv_head(kr, gk), _sink_row(sinks_ref, gk), from_prev, i == 0)
            gating_group(2 * gk)
            outs.append(jnp.dot(v_t[gk * HEAD_DIM:(gk + 1) * HEAD_DIM], _unfold_band(probs.astype(BF16), mask),
                                preferred_element_type=F32))
            gating_group(2 * gk + 1)
        zb = cur_ref[:, OFF_ZB:D_IN].astype(F32)
        y_ref[:, D_A:D_MODEL] = (_heads_to_lanes(outs) * (zb * _sigmoid(zb))).astype(BF16)

        @pl.when(i == nb - 1)
        def _():
            gather.wait_sibling(0)
            for j in range(3):
                gather.wait_sibling(4 + j)
            gather.wait_sends()

    hbm = pl.BlockSpec(memory_space=pl.ANY)
    return pl.pallas_call(
        body, name="mixer_fwd", grid=(nb,),
        in_specs=[sp["cur"], sp["prev_kv"], sp["freq"], sp["vec"], sp["vec"], sp["wsp"], sp["bsp"], sp["smem"], hbm],
        out_specs=[pl.BlockSpec((CHUNK, D_MODEL), lambda i: (i, 0)), hbm],
        out_shape=[SDS((s, D_MODEL), BF16), SDS(wo_all.shape, wo_all.dtype)],
        scratch_shapes=[pltpu.VMEM((A_GROUPS, CHUNK, CHUNK), F32), pltpu.VMEM((A_GROUPS, CHUNK, CHUNK), BF16),
                        pltpu.VMEM((2, CHUNK, 4 * CHUNK), BF16), *_rope_scratch(), *_gather_scratch()],
        input_output_aliases={8: 1},
        compiler_params=_params("arbitrary"),
    )(proj, proj, freqs, ln_g, ln_b, w_sp, b_sp, sinks, wo_all)


def _out_proj_loss(y, x, target, wo, gate, shift_f, scale_f, fng):
    s = y.shape[0]
    tm, tr = 256, 128
    nt = s // tm

    def body(y_ref, x_ref, t_ref, wo_ref, gate_ref, sh_ref, sc_ref, g_ref, dx1_ref, do_ref, dy_ref, sums_ref,
             do_last, do_work):
        i = pl.program_id(0)

        @pl.when(i == 0)
        def _():
            sums_ref[...] = jnp.zeros_like(sums_ref)
            do_last[...] = jnp.zeros_like(do_last)

        do_work[...] = do_last[...]
        o = jnp.dot(y_ref[...], wo_ref[...], preferred_element_type=F32)
        dy_ref[...] = lax.dot_general(do_work[...], wo_ref[...], NT, preferred_element_type=F32).astype(BF16)
        gate, g, sh = gate_ref[...], g_ref[...], sh_ref[...]
        one_sc = 1.0 + sc_ref[...]
        cs, inv_d = g * one_sc, 1.0 / D_MODEL

        def rowsum(v):
            return jnp.sum(v, axis=0, keepdims=True)

        sums = [jnp.zeros((1, D_MODEL), F32) for _ in range(4)]
        for c in range(tm // tr):
            rows = slice(c * tr, (c + 1) * tr)
            oc = o[rows]
            x1 = x_ref[rows, :] + gate * oc
            r = lax.rsqrt(jnp.sum(x1 * x1, axis=-1, keepdims=True) * inv_d + NORM_EPS)
            x1n = x1 * r
            diff = x1n * cs + sh - t_ref[rows, :]
            w = diff * x1n
            lane_sum = jnp.sum(w * cs, axis=-1, keepdims=True)
            dx1 = (diff * cs) * (r * inv_d) - x1n * (r * lane_sum * (inv_d * inv_d))
            dx1_ref[rows, :] = dx1
            do = (dx1 * gate).astype(BF16)
            do_ref[rows, :] = do
            do_last[rows, :] = do
            for k, v in enumerate((dx1 * oc, diff, w, diff * diff)):
                sums[k] = sums[k] + rowsum(v)
        live = jnp.where(i < nt, 1.0, 0.0)
        for row, v in ((SUM_GATE, sums[0]), (SUM_SHIFT_F, inv_d * sums[1]), (SUM_SCALE_F, inv_d * (sums[2] * g)),
                       (SUM_FNG, inv_d * (sums[2] * one_sc)), (SUM_SQ_ERR, sums[3])):
            sums_ref[row:row + 1, :] += live * v

    tile = pl.BlockSpec((tm, D_MODEL), lambda i: (jnp.minimum(i, nt - 1), 0))
    row = pl.BlockSpec((1, D_MODEL), lambda i: (0, 0))
    return pl.pallas_call(
        body, name="out_proj_loss", grid=(nt + 1,),
        in_specs=[tile, tile, tile, pl.BlockSpec((D_MODEL, D_MODEL), lambda i: (0, 0)), row, row, row, row],
        out_specs=[tile, tile, pl.BlockSpec((tm, D_MODEL), lambda i: (jnp.maximum(i - 1, 0), 0)),
                   pl.BlockSpec((8, D_MODEL), lambda i: (0, 0))],
        out_shape=[SDS((s, D_MODEL), F32), SDS((s, D_MODEL), BF16), SDS((s, D_MODEL), BF16), SDS((8, D_MODEL), F32)],
        scratch_shapes=[pltpu.VMEM((tm, D_MODEL), BF16), pltpu.VMEM((tm, D_MODEL), BF16)],
        compiler_params=_params("arbitrary"),
    )(y, x, target, wo, gate, shift_f, scale_f, fng)


ROW_DBSP, ROW_DSINKS, MISC_ROWS = 0, 8, 32


def _mixer_bwd(me, proj, dy, freqs, ln_g, ln_b, w_sp, b_sp, sinks, pair):
    s = proj.shape[0]
    nb = s // CHUNK
    sp = _mixer_specs(nb, rev=True)

    def body(me_ref, cur_ref, pkv_ref, dy_ref, freq_ref, lg_ref, lb_ref, w_ref, b_ref, sinks_ref, pair_ref,
             dproj_ref, dln_ref, dw_ref, misc_ref, parts_ref, bcol, wtril, dbcol, carry, mask, rope_rows, rope_state,
             rope_last, send_sems, recv_sems):
        i = pl.program_id(0)
        block = nb - 1 - i
        rope = _RopeTables(freq_ref, rope_rows, rope_state, rope_last)

        @pl.when(i == 0)
        def _():
            for cp in _chip_scatter(pair_ref, parts_ref, send_sems, recv_sems):
                cp.start()
            _bias_columns(b_ref, bcol)
            _set_tril(w_ref, wtril)
            _set_unfold_masks(mask)
            rope.start(nb - 1, -1)
            rope.keep(rope.tables())
            dbcol[...] = jnp.zeros_like(dbcol)
            carry[...] = jnp.zeros_like(carry)
            dln_ref[...] = jnp.zeros_like(dln_ref)
            dw_ref[...] = jnp.zeros_like(dw_ref)
            misc_ref[...] = jnp.zeros_like(misc_ref)

        vln, vhat, rstd = _layer_norm(cur_ref[:, OFF_VA:OFF_ZA].astype(F32), lg_ref[...], lb_ref[...])
        vln = vln.astype(BF16)
        d_vln = []

        def gating_group(g):
            cols = slice(g * 128, (g + 1) * 128)
            w_g = wtril[g]
            sg = jnp.dot(w_g, vln[:, cols], preferred_element_type=F32) + bcol[g]
            u = cur_ref[:, OFF_U + g * 128:OFF_U + (g + 1) * 128].astype(F32)
            z = cur_ref[:, OFF_ZA + g * 128:OFF_ZA + (g + 1) * 128].astype(F32)
            dya = dy_ref[:, cols].astype(F32)
            sig = _sigmoid(z)
            d_ya = dya * (z * sig)
            dproj_ref[:, OFF_ZA + g * 128:OFF_ZA + (g + 1) * 128] = (
                dya * (u * sg) * (sig * (1.0 + z * (1.0 - sig)))).astype(BF16)
            dproj_ref[:, OFF_U + g * 128:OFF_U + (g + 1) * 128] = (d_ya * sg).astype(BF16)
            d_s = d_ya * u
            dbcol[g] += d_s
            d_sb = d_s.astype(BF16)
            dw_ref[g] += lax.dot_general(d_sb, vln[:, cols], NT, preferred_element_type=F32)
            d_vln.append(lax.dot_general(w_g, d_sb, TN, preferred_element_type=F32))

        cur_t = rope.kept()
        rope.step()
        prev_t = rope.tables()
        rope.keep(prev_t)
        band_t = tuple(jnp.concatenate([p, c], axis=0) for p, c in zip(prev_t, cur_t))
        qr = _rope(cur_ref[:, OFF_Q:OFF_K].astype(F32), *cur_t) * ATTN_SCALE
        kr = jnp.concatenate([_rope(pkv_ref[:, 0:256].astype(F32), *prev_t),
                              _rope(cur_ref[:, OFF_K:OFF_V].astype(F32), *cur_t)], axis=0)
        vb = jnp.concatenate([pkv_ref[:, 256:512], cur_ref[:, OFF_V:OFF_ZB]], axis=0).astype(F32)
        k_t, v_t = (kr.T * ATTN_SCALE).astype(BF16), vb.T.astype(BF16)
        zb = cur_ref[:, OFF_ZB:D_IN].astype(F32)
        dyb = dy_ref[:, D_A:D_MODEL].astype(F32)
        sig = _sigmoid(zb)
        d_yb = dyb * (zb * sig)
        outs, dqs = [], []
        dk_pairs = [jnp.zeros((2 * CHUNK, 128), F32) for _ in range(2)]
        dv_pairs = [jnp.zeros((2 * CHUNK, 128), F32) for _ in range(2)]
        from_prev = _from_prev()
        for gk in range(N_KV_HEADS):
            heads = slice(gk * HEAD_DIM, (gk + 1) * HEAD_DIM)
            q_st = _stack_heads(qr[:, (2 * gk) * 128:(2 * gk + 1) * 128], qr[:, (2 * gk + 1) * 128:(2 * gk + 2) * 128])
            k_dup, v_dup = _dup_kv_head(kr, gk), _dup_kv_head(vb, gk)
            probs, p_sink = _attn_probs(q_st, k_dup, _sink_row(sinks_ref, gk), from_prev, block == 0)
            probs_b = _unfold_band(probs.astype(BF16), mask)
            outs.append(jnp.dot(v_t[heads], probs_b, preferred_element_type=F32))
            do_st = _stack_heads(d_yb[:, (2 * gk) * 128:(2 * gk + 1) * 128], d_yb[:, (2 * gk + 1) * 128:(2 * gk + 2) * 128])
            dp = _fold_band(lax.dot_general(v_dup, do_st, NT, preferred_element_type=F32), from_prev)
            delta = jnp.sum(probs * dp, axis=0, keepdims=True)
            ds = _unfold_band((probs * (dp - delta)).astype(BF16), mask)
            gating_group(2 * gk)
            d_sink = -p_sink * delta
            for r in range(4):
                row = ROW_DSINKS + 4 * gk + r
                misc_ref[row:row + 1, :] += jnp.broadcast_to(
                    jnp.sum(d_sink[:, r * CHUNK:(r + 1) * CHUNK], axis=1, keepdims=True), (1, 128))
            dqs.append(jnp.dot(k_t[heads], ds, preferred_element_type=F32))
            dk_pairs[gk // 2] += _fold_kv_head(jnp.dot(ds, q_st, preferred_element_type=F32), gk)
            dv_pairs[gk // 2] += _fold_kv_head(jnp.dot(probs_b, do_st, preferred_element_type=F32), gk)
            gating_group(2 * gk + 1)
        d_vln = jnp.concatenate(d_vln, axis=1)
        dln_ref[0:1, :] += jnp.sum(d_vln * vhat, axis=0, keepdims=True)
        dln_ref[1:2, :] += jnp.sum(d_vln, axis=0, keepdims=True)
        d_vhat = d_vln * lg_ref[...]
        d_va = rstd * (d_vhat - jnp.mean(d_vhat, axis=-1, keepdims=True)
                       - vhat * jnp.mean(d_vhat * vhat, axis=-1, keepdims=True))
        dproj_ref[:, OFF_VA:OFF_ZA] = d_va.astype(BF16)
        dproj_ref[:, OFF_ZB:D_IN] = (dyb * _heads_to_lanes(outs) * (sig * (1.0 + zb * (1.0 - sig)))).astype(BF16)
        dproj_ref[:, OFF_Q:OFF_K] = _rope_bwd(_heads_to_lanes(dqs), *cur_t).astype(BF16)
        dk_band = _rope_bwd(jnp.concatenate(dk_pairs, axis=1), *band_t)
        dv_band = jnp.concatenate(dv_pairs, axis=1)
        dproj_ref[:, OFF_K:OFF_V] = (dk_band[CHUNK:] + carry[:, 0:256]).astype(BF16)
        dproj_ref[:, OFF_V:OFF_ZB] = (dv_band[CHUNK:] + carry[:, 256:512]).astype(BF16)
        carry[:, 0:256] = dk_band[:CHUNK]
        carry[:, 256:512] = dv_band[:CHUNK]

        @pl.when(i == nb - 1)
        def _():
            t = lax.broadcasted_iota(jnp.int32, (CHUNK, CHUNK), 0)
            tp = lax.broadcasted_iota(jnp.int32, (CHUNK, CHUNK), 1)
            for g in range(A_GROUPS):
                dw_ref[g] = jnp.where(tp <= t, dw_ref[g], 0.0)
                misc_ref[pl.ds(ROW_DBSP + g, 1), :] = jnp.sum(dbcol[g].T, axis=0, keepdims=True)
            scatter = _chip_scatter(pair_ref, parts_ref, send_sems, recv_sems)
            for cp in scatter:
                cp.wait_recv()
            for cp in scatter:
                cp.wait_send()

    blk = sp["blk"]
    hbm = pl.BlockSpec(memory_space=pl.ANY)
    return pl.pallas_call(
        body, name="mixer_bwd",
        grid_spec=pltpu.PrefetchScalarGridSpec(
            num_scalar_prefetch=1, grid=(nb,),
            in_specs=[sp["cur"], sp["prev_kv"], pl.BlockSpec((CHUNK, D_MODEL), lambda i, me_ref: (blk(i), 0)),
                      sp["freq"], sp["vec"], sp["vec"], sp["wsp"], sp["bsp"], sp["smem"], hbm],
            out_specs=[pl.BlockSpec((CHUNK, D_IN), lambda i, me_ref: (blk(i), 0)),
                       pl.BlockSpec((8, D_A), lambda i, me_ref: (me_ref[0], 0)),
                       pl.BlockSpec((A_GROUPS, CHUNK, CHUNK), lambda i, me_ref: (me_ref[0], 0, 0)),
                       pl.BlockSpec((MISC_ROWS, 128), lambda i, me_ref: (me_ref[0], 0)), hbm],
            scratch_shapes=[pltpu.VMEM((A_GROUPS, CHUNK, CHUNK), F32), pltpu.VMEM((A_GROUPS, CHUNK, CHUNK), BF16),
                            pltpu.VMEM((A_GROUPS, CHUNK, CHUNK), F32), pltpu.VMEM((CHUNK, 512), F32), pltpu.VMEM((2, CHUNK, 4 * CHUNK), BF16),
                            *_rope_scratch(), *_scatter_scratch()]),
        out_shape=[SDS((s, D_IN), BF16), SDS((N_DEV * 8, D_A), F32), SDS((N_DEV * A_GROUPS, CHUNK, CHUNK), F32),
                   SDS((N_DEV * MISC_ROWS, 128), F32), SDS((3,) + pair.shape[1:], pair.dtype)],
        compiler_params=_params("arbitrary"),
    )(me, proj, proj, dy, freqs, ln_g, ln_b, w_sp, b_sp, sinks, pair)


def _wgrad_pair(name, a, b, bt, gathers=()):
    s, m = a.shape
    n = b.shape[1]
    bm, half = m // 4, m // 8
    bt = min(bt, s)
    steps = s // bt
    last = 4 * steps
    n_g = len(gathers)

    def body(*refs):
        a_ref, b_ref = refs[:2]
        out_ref, bufs = refs[2 + n_g], refs[3 + n_g:3 + 2 * n_g]
        acc, kept, got, sent, send_sems, recv_sems = refs[3 + 2 * n_g:9 + 2 * n_g]
        sems = refs[9 + 2 * n_g:]
        g = pl.program_id(0)
        tile, t = g // steps, g % steps
        mx, my, mc = _mesh_pos()
        jobs = [_InPlaceGather(bufs[k], sems[2 * k], sems[2 * k + 1]) for k in range(n_g)]

        def exchange(q):
            return pltpu.make_async_remote_copy(src_ref=sent, dst_ref=got.at[q % 2], send_sem=send_sems.at[q],
                                                recv_sem=recv_sems.at[q], device_id=(mx, my, 1 - mc),
                                                device_id_type=MESH)

        @pl.when(g == 0)
        def _():
            for job in jobs:
                job.start()

        @pl.when(g == 2 * steps)
        def _():
            for job in jobs:
                for j in range(3):
                    job.pass_on(j)

        @pl.when(g < last)
        def _():
            prod = lax.dot_general(a_ref[...], b_ref[...], TN, preferred_element_type=F32)

            @pl.when(t == 0)
            def _():
                acc[...] = prod

            @pl.when(t > 0)
            def _():
                acc[...] += prod

            @pl.when(t == steps - 1)
            def _():
                @pl.when(tile > 0)
                def _():
                    exchange(tile - 1).wait_send()

                kept[tile % 2] = acc[pl.ds(pl.multiple_of(mc * half, 8), half), :].astype(BF16)
                sent[...] = acc[pl.ds(pl.multiple_of((1 - mc) * half, 8), half), :].astype(BF16)
                exchange(tile).start()

        @pl.when((t == 0) & (g > 0))
        def _():
            q = tile - 1
            exchange(q).wait_recv()
            out_ref[0] = (kept[q % 2].astype(F32) + got[q % 2].astype(F32)).astype(BF16)

        @pl.when(g == last)
        def _():
            exchange(3).wait_send()
            for job in jobs:
                job.wait_sibling(0)
                for j in range(3):
                    job.wait_sibling(4 + j)
                job.wait_sends()

    def a_tile(g):
        gg = jnp.minimum(g, last - 1)
        return (gg % steps, gg // steps)

    def b_tile(g):
        return (jnp.minimum(g, last - 1) % steps, 0)

    hbm = pl.BlockSpec(memory_space=pl.ANY)
    outs = pl.pallas_call(
        body, name=name, grid=(last + 1,),
        in_specs=[pl.BlockSpec((bt, bm), a_tile), pl.BlockSpec((bt, n), b_tile)] + [hbm] * n_g,
        out_specs=[pl.BlockSpec((1, half, n), lambda g: (jnp.maximum(g - 1, 0) // steps, 0, 0))] + [hbm] * n_g,
        out_shape=[SDS((4, half, n), BF16)] + [SDS(gb.shape, gb.dtype) for gb in gathers],
        scratch_shapes=[pltpu.VMEM((bm, n), F32), pltpu.VMEM((2, half, n), BF16), pltpu.VMEM((2, half, n), BF16),
                        pltpu.VMEM((half, n), BF16), pltpu.SemaphoreType.DMA((4,)), pltpu.SemaphoreType.DMA((4,))]
        + _gather_scratch() * n_g,
        input_output_aliases={2 + k: 1 + k for k in range(n_g)},
        compiler_params=_params("arbitrary"),
    )(a, b, *gathers)
    return outs[0], outs[1:]


def _in_proj_bwd(dproj, wt, x, dx1, scale, norm_g, sums_o, pair):
    s = x.shape[0]
    tm, tk, tr = min(1024, s), D_IN // 4, 64
    ksteps = D_IN // tk

    def body(dp_ref, wt_ref, x_hbm, dx1_hbm, sc_ref, g_ref, so_ref, pair_ref, gx_ref, sums_ref, parts_ref, x_buf,
             dx1_buf, tile_sems, send_sems, recv_sems):
        i, k = pl.program_id(0), pl.program_id(1)

        def tile_copies():
            rows = pl.ds(pl.multiple_of(i * tm, tm), tm)
            return (pltpu.make_async_copy(x_hbm.at[rows], x_buf, tile_sems.at[0]),
                    pltpu.make_async_copy(dx1_hbm.at[rows], dx1_buf, tile_sems.at[1]))

        @pl.when((i == 0) & (k == 0))
        def _():
            for cp in _chip_scatter(pair_ref, parts_ref, send_sems, recv_sems):
                cp.start()
            sums_ref[...] = so_ref[...]

        @pl.when(k == 0)
        def _():
            for cp in tile_copies():
                cp.start()
            gx_ref[...] = jnp.dot(dp_ref[...], wt_ref[...], preferred_element_type=F32)

        @pl.when(k > 0)
        def _():
            gx_ref[...] += jnp.dot(dp_ref[...], wt_ref[...], preferred_element_type=F32)

        @pl.when(k == ksteps - 1)
        def _():
            for cp in tile_copies():
                cp.wait()
            one_sc, g = 1.0 + sc_ref[...], g_ref[...]
            cs = one_sc * g

            def chunk(j, sums):
                rows = pl.ds(pl.multiple_of(j * tr, tr), tr)
                dh, xv = gx_ref[rows, :], x_buf[rows, :]
                dhx = dh * xv
                r = lax.rsqrt(jnp.sum(xv * xv, axis=-1, keepdims=True) * (1.0 / D_MODEL) + NORM_EPS)
                coef = (r * r * r) * (jnp.sum(dhx * cs, axis=-1, keepdims=True) * (1.0 / D_MODEL))
                gx_ref[rows, :] = dx1_buf[rows, :] + r * (dh * cs) - xv * coef
                return (sums[0] + jnp.sum(dh, axis=0, keepdims=True), sums[1] + jnp.sum(dhx * r, axis=0, keepdims=True))

            zero = jnp.zeros((1, D_MODEL), F32)
            sums = lax.fori_loop(0, tm // tr, chunk, (zero, zero))
            sums_ref[SUM_SHIFT:SUM_SHIFT + 1, :] += sums[0]
            sums_ref[SUM_SCALE:SUM_SCALE + 1, :] += sums[1] * g
            sums_ref[SUM_NORM_G:SUM_NORM_G + 1, :] += sums[1] * one_sc

        @pl.when((i == s // tm - 1) & (k == ksteps - 1))
        def _():
            scatter = _chip_scatter(pair_ref, parts_ref, send_sems, recv_sems)
            for cp in scatter:
                cp.wait_recv()
            for cp in scatter:
                cp.wait_send()

    row = pl.BlockSpec((1, D_MODEL), lambda i, k: (0, 0))
    hbm = pl.BlockSpec(memory_space=pl.ANY)
    return pl.pallas_call(
        body, name="in_proj_bwd", grid=(s // tm, ksteps),
        in_specs=[pl.BlockSpec((tm, tk), lambda i, k: (i, k)), pl.BlockSpec((tk, D_MODEL), lambda i, k: (k, 0)),
                  hbm, hbm, row, row, pl.BlockSpec((8, D_MODEL), lambda i, k: (0, 0)), hbm],
        out_specs=[pl.BlockSpec((tm, D_MODEL), lambda i, k: (i, 0)), pl.BlockSpec((8, D_MODEL), lambda i, k: (0, 0)),
                   hbm],
        out_shape=[SDS((s, D_MODEL), F32), SDS((8, D_MODEL), F32), SDS((3,) + pair.shape[1:], pair.dtype)],
        scratch_shapes=[pltpu.VMEM((tm, D_MODEL), F32), pltpu.VMEM((tm, D_MODEL), F32),
                        pltpu.SemaphoreType.DMA((2,)), *_scatter_scratch()],
        compiler_params=_params("arbitrary", "arbitrary"),
    )(dproj, wt, x, dx1, scale, norm_g, sums_o, pair)


def _hand_over(name, a):
    def body(a_ref, out_ref):
        del a_ref, out_ref

    anywhere = pl.BlockSpec(memory_space=pl.ANY)
    return pl.pallas_call(body, name=name, in_specs=[anywhere], out_specs=anywhere, out_shape=SDS(a.shape, a.dtype),
                          input_output_aliases={0: 0})(a)


def _sum_chips(own_ref, parts_ref):
    return ((own_ref[0].astype(F32) + parts_ref[0].astype(F32)) + parts_ref[1].astype(F32)) + parts_ref[2].astype(F32)


def _adam_rows(name, chip, pair, parts, w, m, v, tr):
    rows = w.shape[0]

    def body(chip_ref, own_ref, p_ref, w_ref, m_ref, v_ref, g_ref, d_ref, nm_ref, nv_ref):
        g = _sum_chips(own_ref, p_ref)
        g_ref[...] = g
        d_ref[...], nm_ref[...], nv_ref[...] = _adamw(w_ref[...], g, m_ref[...], v_ref[...])

    blk = pl.BlockSpec((tr, D_MODEL), lambda j, chip_ref: (j, 0))
    return pl.pallas_call(
        body, name=name,
        grid_spec=pltpu.PrefetchScalarGridSpec(
            num_scalar_prefetch=1, grid=(rows // tr,),
            in_specs=[pl.BlockSpec((1, tr, D_MODEL), lambda j, chip_ref: (chip_ref[0], j, 0)),
                      pl.BlockSpec((3, tr, D_MODEL), lambda j, chip_ref: (0, j, 0)), blk, blk, blk],
            out_specs=[blk] * 4),
        out_shape=[SDS(w.shape, F32)] * 4, compiler_params=_params("parallel"),
    )(chip, pair, parts, w, m, v)


def _adam_ada(name, cact, dmod, w, m, v):
    n = w.shape[1]
    tr = 512

    def body(c_ref, dm_ref, w_ref, m_ref, v_ref, g_ref, d_ref, nm_ref, nv_ref):
        pad_c = jnp.concatenate([c_ref[...], jnp.zeros_like(c_ref)], axis=0).astype(BF16)
        pad_d = jnp.concatenate([dm_ref[...], jnp.zeros_like(dm_ref)], axis=0).astype(BF16)
        g = lax.dot_general(pad_c, pad_d, TN, preferred_element_type=F32)
        g_ref[...] = g
        d_ref[...], nm_ref[...], nv_ref[...] = _adamw(w_ref[...], g, m_ref[...], v_ref[...])

    blk = pl.BlockSpec((tr, n), lambda j: (j, 0))
    return pl.pallas_call(
        body, name=name, grid=(D_MODEL // tr,),
        in_specs=[pl.BlockSpec((N_DEV, tr), lambda j: (0, j)), pl.BlockSpec((N_DEV, n), lambda j: (0, 0)),
                  blk, blk, blk],
        out_specs=[blk] * 4, out_shape=[SDS(w.shape, F32)] * 4,
        compiler_params=_params("parallel"),
    )(cact, dmod, w, m, v)


SMALL_PARAMS = ("w_spatial", "b_spatial", "sinks", "norm_g", "ln_v_g", "ln_v_b", "final_norm_g", "b_ada", "b_ada_final")


def _adam_small(d_wsp, misc, d_ln, sums, params):
    n_p = len(SMALL_PARAMS)

    def body(*refs):
        wsp_ref, misc_ref, ln_ref, sums_ref = refs[:4]
        wmv = [refs[4 + 3 * k:7 + 3 * k] for k in range(n_p)]
        loss_ref = refs[4 + 3 * n_p]
        outs = [refs[5 + 3 * n_p + 4 * k:9 + 3 * n_p + 4 * k] for k in range(n_p)]

        def column_sum(row):
            return total(sums_ref, (row, row + 1))

        def total(ref, rows=None):
            def part(j):
                return ref[j] if rows is None else ref[j, rows[0]:rows[1], :]
            acc = part(0)
            for j in range(1, N_DEV):
                acc = acc + part(j)
            return acc

        sink_rows = total(misc_ref, (ROW_DSINKS, ROW_DSINKS + 16))
        diag = (lax.broadcasted_iota(jnp.int32, (16, 128), 0) == lax.broadcasted_iota(jnp.int32, (16, 128), 1))
        grads = dict(
            w_spatial=total(wsp_ref), b_spatial=total(misc_ref, (ROW_DBSP, ROW_DBSP + A_GROUPS)),
            sinks=jnp.sum(jnp.where(diag, sink_rows, 0.0), axis=0, keepdims=True),
            norm_g=column_sum(SUM_NORM_G), ln_v_g=total(ln_ref, (0, 1)), ln_v_b=total(ln_ref, (1, 2)),
            final_norm_g=column_sum(SUM_FNG),
            b_ada=jnp.concatenate([column_sum(SUM_SHIFT), column_sum(SUM_SCALE), column_sum(SUM_GATE)], axis=1),
            b_ada_final=jnp.concatenate([column_sum(SUM_SHIFT_F), column_sum(SUM_SCALE_F)], axis=1))
        sq_err = jnp.sum(column_sum(SUM_SQ_ERR), axis=1, keepdims=True)
        loss_ref[...] = jnp.broadcast_to(sq_err * (0.5 / D_MODEL), (1, 128))
        for k, name in enumerate(SMALL_PARAMS):
            w_ref, m_ref, v_ref = wmv[k]
            g_ref, d_ref, nm_ref, nv_ref = outs[k]
            g_ref[...] = grads[name]
            d_ref[...], nm_ref[...], nv_ref[...] = _adamw(w_ref[...], grads[name], m_ref[...], v_ref[...])

    flat = [a for name in SMALL_PARAMS for a in params[name]]
    vmem = pl.BlockSpec(memory_space=pltpu.VMEM)
    out_shape = [SDS((1, 128), F32)] + [SDS(params[name][0].shape, F32) for name in SMALL_PARAMS for _ in range(4)]
    outs = pl.pallas_call(
        body, name="adam_small", in_specs=[vmem] * (4 + len(flat)), out_specs=[vmem] * len(out_shape),
        out_shape=out_shape, compiler_params=_params(),
    )(d_wsp, misc, d_ln, sums, *flat)
    return outs[0], {name: outs[1 + 4 * k:5 + 4 * k] for k, name in enumerate(SMALL_PARAMS)}


def kernel(x, c, w_ada, b_ada, norm_g, w_in, ln_v_g, ln_v_b, w_spatial, b_spatial, sinks, w_out, w_ada_final, b_ada_final, final_norm_g, loss_target, m_w_ada, m_b_ada, m_norm_g, m_w_in, m_ln_v_g, m_ln_v_b, m_w_spatial, m_b_spatial, m_sinks, m_w_out, m_w_ada_final, m_b_ada_final, m_final_norm_g, v_w_ada, v_b_ada, v_norm_g, v_w_in, v_ln_v_g, v_ln_v_b, v_w_spatial, v_b_spatial, v_sinks, v_w_out, v_w_ada_final, v_b_ada_final, v_final_norm_g):
    me = 4 * lax.axis_index("x") + 2 * lax.axis_index("y") + lax.axis_index("c")
    x2, tgt = x[0], loss_target[0]
    fng = final_norm_g.reshape(1, D_MODEL)

    n_ada, n_ada_f = w_ada.shape[2], w_ada_final.shape[1]
    cact, mod, mod_f = _ada_exchange(c, w_ada[0], b_ada.reshape(N_DEV, n_ada), w_ada_final,
                                     b_ada_final.reshape(N_DEV, n_ada_f))
    cact = cact.reshape(N_DEV, D_MODEL)
    mod, mod_f = mod.reshape(1, 3 * D_MODEL), mod_f.reshape(1, 2 * D_MODEL)
    shift, scale, gate = mod[:, :D_MODEL], mod[:, D_MODEL:2 * D_MODEL], mod[:, 2 * D_MODEL:]
    shift_f, scale_f = mod_f[:, :D_MODEL], mod_f[:, D_MODEL:]

    wt_f32, m_wt, v_wt = (jnp.swapaxes(a, 1, 2)[0] for a in (w_in, m_w_in, v_w_in))
    xi, yi = lax.axis_index("x"), lax.axis_index("y")
    chip_order = jnp.stack([2 * xi + yi, 2 * (1 - xi) + yi, 2 * xi + 1 - yi, 2 * (1 - xi) + 1 - yi]).astype(jnp.int32)
    wt_mine, wo_mine = _prep_weights(me.reshape(1), wt_f32, w_out[0])

    freqs = _rope_freqs()
    sinks_v = sinks.reshape(16)
    h, proj, wt = _gather_in_proj(chip_order, x2, shift, scale, norm_g, wt_mine)
    y, wo = _mixer_fwd(proj, freqs, ln_v_g, ln_v_b, w_spatial[0], b_spatial[0], sinks_v, wo_mine)
    dx1, do, dy, sums_o = _out_proj_loss(y, x2, tgt, wo, gate, shift_f, scale_f, fng)

    chip = (2 * lax.axis_index("x") + lax.axis_index("y")).reshape(1)
    pair_out, _ = _wgrad_pair("wgrad_out", y, do, 2048)
    dproj, d_ln, d_wsp, misc, parts_out = _mixer_bwd(
        me.reshape(1), proj, dy, freqs, ln_v_g, ln_v_b, w_spatial[0], b_spatial[0], sinks_v, pair_out)
    pair_in, (d_ln, d_wsp, misc) = _wgrad_pair(
        "wgrad_in", dproj, h, 1024, gathers=(d_ln, d_wsp.reshape(N_DEV * A_GROUPS * CHUNK, CHUNK), misc))
    grad_x, sums, parts_in = _in_proj_bwd(dproj, wt, x2, dx1, scale, norm_g, sums_o, pair_in)
    grad_x = _hand_over("hand_over_grad_x", grad_x)
    wt_leaves = [jnp.swapaxes(a[None], 1, 2)
                 for a in _adam_rows("adam_w_in", chip, pair_in, parts_in, wt_f32, m_wt, v_wt, 176)]
    w_out_leaves = [a[None] for a in _adam_rows("adam_w_out", chip, pair_out, parts_out, w_out[0], m_w_out[0], v_w_out[0], 64)]

    (sums,) = _all_gather("gather_sums", [sums], pltpu.VMEM)
    natural = dict(w_spatial=(A_GROUPS * CHUNK, CHUNK), b_spatial=(A_GROUPS, CHUNK), sinks=(1, 16), norm_g=(1, D_MODEL),
                   ln_v_g=(1, D_A), ln_v_b=(1, D_A), final_norm_g=(1, D_MODEL), b_ada=(1, 3 * D_MODEL),
                   b_ada_final=(1, 2 * D_MODEL))
    given = dict(
        w_spatial=(w_spatial, m_w_spatial, v_w_spatial), b_spatial=(b_spatial, m_b_spatial, v_b_spatial),
        sinks=(sinks, m_sinks, v_sinks), norm_g=(norm_g, m_norm_g, v_norm_g), ln_v_g=(ln_v_g, m_ln_v_g, v_ln_v_g),
        ln_v_b=(ln_v_b, m_ln_v_b, v_ln_v_b), final_norm_g=(final_norm_g, m_final_norm_g, v_final_norm_g),
        b_ada=(b_ada, m_b_ada, v_b_ada), b_ada_final=(b_ada_final, m_b_ada_final, v_b_ada_final))
    params = {name: tuple(a.reshape(natural[name]) for a in given[name]) for name in SMALL_PARAMS}
    params["sinks"] = tuple(jnp.pad(a, ((0, 0), (0, 128 - 16))) for a in params["sinks"])
    loss, small = _adam_small(d_wsp.reshape(N_DEV, A_GROUPS * CHUNK, CHUNK), misc.reshape(N_DEV, MISC_ROWS, 128),
                              d_ln.reshape(N_DEV, 8, D_A), sums, params)
    small["sinks"] = [a[:, :16] for a in small["sinks"]]
    small = {name: [a.reshape(given[name][0].shape) for a in small[name]] for name in SMALL_PARAMS}

    dmod_all = jnp.concatenate([sums[:, SUM_SHIFT], sums[:, SUM_SCALE], sums[:, SUM_GATE]], axis=1)
    dmod_f_all = jnp.concatenate([sums[:, SUM_SHIFT_F], sums[:, SUM_SCALE_F]], axis=1)
    dmod_mine = lax.dynamic_slice_in_dim(dmod_all, me * n_ada, n_ada, axis=1)
    dmod_f_mine = lax.dynamic_slice_in_dim(dmod_f_all, me * n_ada_f, n_ada_f, axis=1)
    ada = _adam_ada("adam_w_ada", cact, dmod_mine, w_ada[0], m_w_ada[0], v_w_ada[0])
    ada_f = _adam_ada("adam_w_ada_final", cact, dmod_f_mine, w_ada_final, m_w_ada_final, v_w_ada_final)

    def leaves(k):
        return (ada[k][None], small["b_ada"][k], small["norm_g"][k], wt_leaves[k], small["ln_v_g"][k],
                small["ln_v_b"][k], small["w_spatial"][k], small["b_spatial"][k], small["sinks"][k], w_out_leaves[k],
                ada_f[k], small["b_ada_final"][k], small["final_norm_g"][k])

    return (loss[0, 0], grad_x[None], *leaves(0), *leaves(1), *leaves(2), *leaves(3))
```

```python
import jax
import jax.numpy as jnp
from jax import lax
from jax.experimental import pallas as pl
from jax.experimental.pallas import tpu as pltpu

D_MODEL = 2048
D_IN = 5632
D_A = 1024
CHUNK = 128
A_GROUPS = 8
HEAD_DIM = 64
N_KV_HEADS = 4
N_DEV = 8
ROPE_THETA = 10000.0
NORM_EPS = 1e-5
ATTN_SCALE = HEAD_DIM ** -0.5

ADAM_LR = 0.001
ADAM_B1 = 0.9
ADAM_B2 = 0.999
ADAM_EPS = 1e-08
ADAM_WD = 0.01
ADAM_STEP = 10

OFF_U, OFF_VA, OFF_ZA, OFF_Q, OFF_K, OFF_V, OFF_ZB = 0, 1024, 2048, 3072, 4096, 4352, 4608

SUM_SHIFT, SUM_SCALE, SUM_NORM_G, SUM_GATE, SUM_SHIFT_F, SUM_SCALE_F, SUM_FNG, SUM_SQ_ERR = range(8)

V7X_VMEM_LIMIT_BYTES = 56 * 1024 * 1024

F32 = jnp.float32
BF16 = jnp.bfloat16
MESH = pl.DeviceIdType.MESH
SDS = jax.ShapeDtypeStruct
NT = (((1,), (1,)), ((), ()))
TN = (((0,), (0,)), ((), ()))


def _params(*semantics):
    return pltpu.CompilerParams(dimension_semantics=semantics or None, vmem_limit_bytes=V7X_VMEM_LIMIT_BYTES)


def _mesh_pos():
    return lax.axis_index("x"), lax.axis_index("y"), lax.axis_index("c")


def _sigmoid(z):
    return 1.0 / (1.0 + jnp.exp(-z))


def _adamw(w, g, m, v):
    m = ADAM_B1 * m + (1.0 - ADAM_B1) * g
    v = ADAM_B2 * v + (1.0 - ADAM_B2) * (g * g)
    m_hat = m / (1.0 - ADAM_B1 ** ADAM_STEP)
    v_hat = v / (1.0 - ADAM_B2 ** ADAM_STEP)
    delta = -ADAM_LR * (m_hat / (jnp.sqrt(v_hat) + ADAM_EPS) + ADAM_WD * w)
    return delta, m, v


def _all_gather(name, blocks, memory_space):
    n_arr = len(blocks)

    def body(*refs):
        ins, outs = refs[:n_arr], refs[n_arr:2 * n_arr]
        send_sems, recv_sems, local_sems = refs[2 * n_arr:]
        x, y, c = _mesh_pos()
        me, sibling = (x, y, c), (x, y, 1 - c)
        chips = [(1 - x, y), (x, 1 - y), (1 - x, 1 - y)]

        def slot(p):
            return 4 * p[0] + 2 * p[1] + p[2]

        def copy(a, k, block, to, src=None):
            dst = outs[a].at[slot(block)]
            return pltpu.make_async_remote_copy(
                src_ref=dst if src is None else src, dst_ref=dst,
                send_sem=send_sems.at[a, k], recv_sem=recv_sems.at[a, k],
                device_id=to, device_id_type=MESH)

        mine = [pltpu.make_async_copy(ins[a], outs[a].at[slot(me)], local_sems.at[a]) for a in range(n_arr)]
        for cp in mine:
            cp.start()
        first = []
        for a in range(n_arr):
            first.append(copy(a, 0, me, sibling, src=ins[a]))
            first += [copy(a, 1 + j, me, (*chip, c), src=ins[a]) for j, chip in enumerate(chips)]
        for cp in first:
            cp.start()
        passed = []
        for j, chip in enumerate(chips):
            for a in range(n_arr):
                copy(a, 1 + j, (*chip, c), me).wait_recv()
                fwd = copy(a, 4 + j, (*chip, c), sibling)
                fwd.start()
                passed.append(fwd)
        for a in range(n_arr):
            copy(a, 0, sibling, me).wait_recv()
            for j, chip in enumerate(chips):
                copy(a, 4 + j, (*chip, 1 - c), me).wait_recv()
        for cp in first + passed:
            cp.wait_send()
        for cp in mine:
            cp.wait()

    spec = pl.BlockSpec(memory_space=memory_space)
    return pl.pallas_call(
        body, name=name,
        out_shape=[SDS((N_DEV,) + b.shape, b.dtype) for b in blocks],
        in_specs=[spec] * n_arr, out_specs=[spec] * n_arr,
        scratch_shapes=[pltpu.SemaphoreType.DMA((n_arr, 7)), pltpu.SemaphoreType.DMA((n_arr, 7)),
                        pltpu.SemaphoreType.DMA((n_arr,))],
        compiler_params=_params(),
    )(*blocks)


def _ada_exchange(c, w_ada, b_ada8, w_ada_f, b_ada_f8):
    n1, n2 = w_ada.shape[1], w_ada_f.shape[1]

    def body(c_ref, w1_ref, b1_ref, w2_ref, b2_ref, cact_ref, mod_ref, modf_ref,
             cact_buf, res1, res2, send1, send2, sems_s, sems_r):
        x, y, c_pos = _mesh_pos()
        me = 4 * x + 2 * y + c_pos
        flips = [(k >> 2 & 1, k >> 1 & 1, k & 1) for k in range(1, N_DEV)]

        def peer(f):
            return (1 - x if f[0] else x, 1 - y if f[1] else y, 1 - c_pos if f[2] else c_pos)

        cv = c_ref[...]
        cact = cv * _sigmoid(cv)
        cact_buf[...] = cact
        cact_ref[me] = cact

        def rdma(phase, k, src, dst, f):
            return pltpu.make_async_remote_copy(src_ref=src, dst_ref=dst, send_sem=sems_s.at[phase, k],
                                                recv_sem=sems_r.at[phase, k], device_id=peer(f), device_id_type=MESH)

        gather = [rdma(0, k, cact_buf, cact_ref.at[me], f) for k, f in enumerate(flips)]
        for cp in gather:
            cp.start()
        for cp in gather:
            cp.wait_recv()
        for cp in gather:
            cp.wait_send()

        rid = lax.broadcasted_iota(jnp.int32, (N_DEV, D_MODEL), 0)
        rows = jnp.zeros((N_DEV, D_MODEL), F32)
        for j in range(N_DEV):
            rows = jnp.where(rid == j, jnp.broadcast_to(cact_ref[j], (N_DEV, D_MODEL)), rows)
        rows = rows.astype(BF16)
        res1[...] = jnp.dot(rows, w1_ref[...].astype(BF16), preferred_element_type=F32) + b1_ref[pl.ds(me, 1), :]
        res2[...] = jnp.dot(rows, w2_ref[...].astype(BF16), preferred_element_type=F32) + b2_ref[pl.ds(me, 1), :]
        for j in range(N_DEV):
            send1[j] = res1[pl.ds(j, 1), :]
            send2[j] = res2[pl.ds(j, 1), :]
        mod_ref[me] = send1[me]
        modf_ref[me] = send2[me]
        scatter = []
        for k, f in enumerate(flips):
            to = me ^ (k + 1)
            scatter.append(rdma(1, k, send1.at[to], mod_ref.at[me], f))
            scatter.append(rdma(2, k, send2.at[to], modf_ref.at[me], f))
        for cp in scatter:
            cp.start()
        for cp in scatter:
            cp.wait_recv()
        for cp in scatter:
            cp.wait_send()

    vmem = pl.BlockSpec(memory_space=pltpu.VMEM)
    return pl.pallas_call(
        body, name="ada_exchange",
        out_shape=[SDS((N_DEV, 1, D_MODEL), F32), SDS((N_DEV, 1, n1), F32), SDS((N_DEV, 1, n2), F32)],
        in_specs=[vmem] * 5, out_specs=[vmem] * 3,
        scratch_shapes=[pltpu.VMEM((1, D_MODEL), F32), pltpu.VMEM((N_DEV, n1), F32), pltpu.VMEM((N_DEV, n2), F32),
                        pltpu.VMEM((N_DEV, 1, n1), F32), pltpu.VMEM((N_DEV, 1, n2), F32),
                        pltpu.SemaphoreType.DMA((3, 7)), pltpu.SemaphoreType.DMA((3, 7))],
        compiler_params=_params(),
    )(c, w_ada, b_ada8, w_ada_f, b_ada_f8)


def _chip_scatter(pair_ref, parts_ref, send_sems, recv_sems):
    x, y, c = _mesh_pos()
    chips = [(1 - x, y), (x, 1 - y), (1 - x, 1 - y)]
    return [pltpu.make_async_remote_copy(
        src_ref=pair_ref.at[2 * cx + cy], dst_ref=parts_ref.at[j], send_sem=send_sems.at[j], recv_sem=recv_sems.at[j],
        device_id=(cx, cy, c), device_id_type=MESH) for j, (cx, cy) in enumerate(chips)]


def _scatter_scratch():
    return [pltpu.SemaphoreType.DMA((3,)), pltpu.SemaphoreType.DMA((3,))]


def _prep_weights(me, wt, w_out):
    steps = 4

    def body(me_ref, wt_ref, wo_ref, wtb_ref, wob_ref):
        wtb_ref[...] = wt_ref[...].astype(BF16)
        wob_ref[...] = wo_ref[...].astype(BF16)

    def rows(a, mine):
        blk = (a.shape[0] // steps, a.shape[1])
        return pl.BlockSpec(blk, (lambda i, me_ref: (steps * me_ref[0] + i, 0)) if mine else (lambda i, me_ref: (i, 0)))

    return pl.pallas_call(
        body, name="prep_weights",
        grid_spec=pltpu.PrefetchScalarGridSpec(
            num_scalar_prefetch=1, grid=(steps,),
            in_specs=[rows(wt, False), rows(w_out, False)], out_specs=[rows(wt, True), rows(w_out, True)]),
        out_shape=[SDS((N_DEV * wt.shape[0], D_MODEL), BF16), SDS((N_DEV * w_out.shape[0], D_MODEL), BF16)],
        compiler_params=_params("parallel"),
    )(me, wt, w_out)


class _InPlaceGather:
    def __init__(self, buf_ref, send_sems, recv_sems, relay=False):
        self.buf, self.send_sems, self.recv_sems, self.relay = buf_ref, send_sems, recv_sems, relay
        self.n = buf_ref.shape[0] // N_DEV
        x, y, c = _mesh_pos()
        self.me, self.sibling, self.core = (x, y, c), (x, y, 1 - c), c
        self.chips = [(1 - x, y), (x, 1 - y), (1 - x, 1 - y)]
        self.relay_from = (jnp.where(c == 0, 1 - x, x), jnp.where(c == 0, y, 1 - y), c)
        self.relay_to = (jnp.where(c == 0, x, 1 - x), jnp.where(c == 0, 1 - y, y), c)

    def copy(self, k, block, to):
        start = pl.multiple_of((4 * block[0] + 2 * block[1] + block[2]) * self.n, self.n)
        rows = self.buf.at[pl.ds(start, self.n)]
        return pltpu.make_async_remote_copy(src_ref=rows, dst_ref=rows, send_sem=self.send_sems.at[k],
                                            recv_sem=self.recv_sems.at[k], device_id=to, device_id_type=MESH)

    def start(self):
        self.copy(0, self.me, self.sibling).start()
        for j, chip in enumerate(self.chips[:2] if self.relay else self.chips):
            self.copy(1 + j, self.me, (*chip, self.core)).start()

    def relay_diagonal(self):
        self.copy(3, self.relay_from, self.relay_to).start()

    def pass_on(self, j):
        self.copy(1 + j, (*self.chips[j], self.core), self.me).wait_recv()
        self.copy(4 + j, (*self.chips[j], self.core), self.sibling).start()

    def wait_sibling(self, k):
        self.copy(k, self.sibling, self.me).wait_recv()

    def wait_sends(self):
        for k in range(7):
            self.copy(k, self.me, self.sibling).wait_send()


def _gather_scratch():
    return [pltpu.SemaphoreType.DMA((7,)), pltpu.SemaphoreType.DMA((7,))]


def _gather_in_proj(order, x, shift, scale, norm_g, wt_all):
    s = x.shape[0]
    th, tm = min(512, s), min(1024, s)
    nh, ni = s // th, s // tm
    tn = D_IN // 4
    steps = nh + 4 * ni

    def body(order_ref, x_ref, shift_ref, scale_ref, g_ref, wt_in, h_ref, proj_ref, wt_ref,
             h_scr, w_buf, load_sems, send_sems, recv_sems):
        g = pl.program_id(0)
        gather = _InPlaceGather(wt_ref, send_sems, recv_sems, relay=True)

        def tile_load(slot, chip):
            return pltpu.make_async_copy(wt_ref.at[pl.ds(pl.multiple_of(chip * tn, tn), tn)], w_buf.at[slot],
                                         load_sems.at[slot])

        @pl.when(g == 0)
        def _():
            gather.start()

        @pl.when(g < nh)
        def _():
            xv = x_ref[...]
            r = lax.rsqrt(jnp.mean(xv * xv, axis=-1, keepdims=True) + NORM_EPS)
            hb = (((xv * r) * g_ref[...]) * (1.0 + scale_ref[...]) + shift_ref[...]).astype(BF16)
            h_ref[...] = hb
            h_scr[pl.ds(pl.multiple_of(g * th, th), th), :] = hb

        @pl.when(g == nh - 1)
        def _():
            gather.wait_sibling(0)
            tile_load(0, order_ref[0]).start()

        @pl.when(g >= nh)
        def _():
            t, i = (g - nh) // ni, (g - nh) % ni

            @pl.when(i == 0)
            def _():
                tile_load(t % 2, order_ref[t]).wait()

            @pl.when((i == ni - 1) & (t == 0))
            def _():
                gather.pass_on(0)
                gather.pass_on(1)
                gather.relay_diagonal()

            @pl.when((i == ni // 2) & (t == 2))
            def _():
                gather.pass_on(2)

            for j in range(3):
                @pl.when((i == ni - 1) & (t == j))
                def _():
                    gather.wait_sibling(4 + j)
                    tile_load((j + 1) % 2, order_ref[j + 1]).start()

            lhs = h_scr[pl.ds(pl.multiple_of(i * tm, tm), tm), :]
            proj_ref[...] = lax.dot_general(lhs, w_buf[t % 2], NT, preferred_element_type=F32).astype(BF16)

        @pl.when(g == steps - 1)
        def _():
            gather.wait_sends()

    def h_tile(g, order_ref):
        return (jnp.minimum(g, nh - 1), 0)

    def proj_tile(g, order_ref):
        mm = jnp.maximum(g - nh, 0)
        return (mm % ni, order_ref[mm // ni])

    row = pl.BlockSpec((1, D_MODEL), lambda g, order_ref: (0, 0))
    hbm = pl.BlockSpec(memory_space=pl.ANY)
    return pl.pallas_call(
        body, name="gather_in_proj",
        grid_spec=pltpu.PrefetchScalarGridSpec(
            num_scalar_prefetch=1, grid=(steps,),
            in_specs=[pl.BlockSpec((th, D_MODEL), h_tile), row, row, row, hbm],
            out_specs=[pl.BlockSpec((th, D_MODEL), h_tile), pl.BlockSpec((tm, tn), proj_tile), hbm],
            scratch_shapes=[pltpu.VMEM((s, D_MODEL), BF16), pltpu.VMEM((2, tn, D_MODEL), BF16),
                            pltpu.SemaphoreType.DMA((2,)), *_gather_scratch()]),
        out_shape=[SDS((s, D_MODEL), BF16), SDS((s, D_IN), BF16), SDS(wt_all.shape, BF16)],
        input_output_aliases={5: 2},
        compiler_params=_params("arbitrary"),
    )(order, x, shift, scale, norm_g, wt_all)


def _rope_freqs():
    inv_freq = ROPE_THETA ** (-jnp.arange(0, HEAD_DIM, 2, dtype=F32) / HEAD_DIM)
    return jnp.tile(inv_freq, 4).reshape(1, 128)


class _RopeTables:
    def __init__(self, freq_ref, rows_ref, state_ref, last_ref):
        self.freq, self.rows, self.state, self.last = freq_ref, rows_ref, state_ref, last_ref

    def start(self, block, direction):
        ang = lax.broadcasted_iota(jnp.int32, (CHUNK, 128), 0).astype(F32) * self.freq[...]
        self.rows[0] = jnp.cos(ang)
        self.rows[1] = jnp.sin(ang)
        base = jnp.asarray(block * CHUNK, dtype=F32) * self.freq[...]
        turn = float(direction * CHUNK) * self.freq[...]
        self.state[0:1, :] = jnp.cos(base)
        self.state[1:2, :] = jnp.sin(base)
        self.state[2:3, :] = jnp.cos(turn)
        self.state[3:4, :] = jnp.sin(turn)

    def step(self):
        c, s, ct, st = (self.state[k:k + 1, :] for k in range(4))
        self.state[0:1, :] = c * ct - s * st
        self.state[1:2, :] = s * ct + c * st

    def tables(self):
        c, s = self.state[0:1, :], self.state[1:2, :]
        cos = c * self.rows[0] - s * self.rows[1]
        sin = s * self.rows[0] + c * self.rows[1]
        first_half = (lax.broadcasted_iota(jnp.int32, (1, 128), 1) & (HEAD_DIM - 1)) < HEAD_DIM // 2
        return cos, jnp.where(first_half, -sin, 0.0), jnp.where(first_half, 0.0, sin)

    def keep(self, tabs):
        for k in range(3):
            self.last[k] = tabs[k]

    def kept(self):
        return tuple(self.last[k] for k in range(3))


def _rope_scratch():
    return [pltpu.VMEM((2, CHUNK, 128), F32), pltpu.VMEM((8, 128), F32), pltpu.VMEM((3, CHUNK, 128), F32)]


def _rope(v, cos, sin_lo, sin_hi):
    width = v.shape[1]
    rep = (1, width // 128)
    return (v * jnp.tile(cos, rep) + pltpu.roll(v, width - 32, 1) * jnp.tile(sin_lo, rep)
            + pltpu.roll(v, 32, 1) * jnp.tile(sin_hi, rep))


def _rope_bwd(d, cos, sin_lo, sin_hi):
    width = d.shape[1]
    rep = (1, width // 128)
    return (d * jnp.tile(cos, rep) + pltpu.roll(d * jnp.tile(sin_lo, rep), 32, 1)
            + pltpu.roll(d * jnp.tile(sin_hi, rep), width - 32, 1))


def _layer_norm(v, g, b):
    mu = jnp.mean(v, axis=-1, keepdims=True)
    vc = v - mu
    rstd = lax.rsqrt(jnp.mean(vc * vc, axis=-1, keepdims=True) + NORM_EPS)
    vhat = vc * rstd
    return vhat * g + b, vhat, rstd


def _set_tril(w_ref, out_ref):
    t = lax.broadcasted_iota(jnp.int32, (CHUNK, CHUNK), 0)
    tp = lax.broadcasted_iota(jnp.int32, (CHUNK, CHUNK), 1)
    for g in range(A_GROUPS):
        out_ref[g] = jnp.where(tp <= t, w_ref[g], 0.0).astype(BF16)


def _bias_columns(b_ref, out_ref):
    for g in range(A_GROUPS):
        out_ref[g] = jnp.broadcast_to(b_ref[pl.ds(g, 1), :], (CHUNK, CHUNK)).T


def _from_prev():
    r = lax.broadcasted_iota(jnp.int32, (CHUNK, 4 * CHUNK), 0)
    i = lax.broadcasted_iota(jnp.int32, (CHUNK, 4 * CHUNK), 1) & (CHUNK - 1)
    return r > i


def _set_unfold_masks(mask_ref):
    prev = _from_prev()
    mask_ref[0] = jnp.where(prev, 1.0, 0.0).astype(BF16)
    mask_ref[1] = jnp.where(prev, 0.0, 1.0).astype(BF16)


def _fold_band(t, from_prev):
    return jnp.where(from_prev, t[:CHUNK], t[CHUNK:])


def _unfold_band(t, mask_ref):
    return jnp.concatenate([t * mask_ref[0], t * mask_ref[1]], axis=0)


def _low_lanes():
    return lax.broadcasted_iota(jnp.int32, (1, 128), 1) < HEAD_DIM


def _stack_heads(pair_a, pair_b):
    lo = _low_lanes()
    return jnp.concatenate([jnp.where(lo, pair_a, 0.0), jnp.where(lo, 0.0, pair_a),
                            jnp.where(lo, pair_b, 0.0), jnp.where(lo, 0.0, pair_b)], axis=0).astype(BF16)


def _heads_to_lanes(per_group):
    rows = [t[:, r * CHUNK:(r + 1) * CHUNK] for t in per_group for r in range(4)]
    return jnp.concatenate(rows, axis=0).T


def _dup_kv_head(band, gk):
    pair = band[:, (gk // 2) * 128:(gk // 2 + 1) * 128]
    lo = _low_lanes()
    one = jnp.where(lo if gk % 2 == 0 else jnp.logical_not(lo), pair, 0.0)
    return (one + pltpu.roll(one, HEAD_DIM, 1)).astype(BF16)


def _fold_kv_head(dup_grad, gk):
    both = dup_grad + pltpu.roll(dup_grad, HEAD_DIM, 1)
    lo = _low_lanes()
    return jnp.where(lo if gk % 2 == 0 else jnp.logical_not(lo), both, 0.0)


def _attn_probs(q_st, k_dup, sink_row, from_prev, first_block):
    s = lax.dot_general(k_dup, q_st, NT, preferred_element_type=F32)
    no_prev = jnp.where(first_block, -jnp.inf, 0.0)
    s = jnp.where(from_prev, s[:CHUNK] + no_prev, s[CHUNK:])
    m = jnp.maximum(jnp.max(s, axis=0, keepdims=True), sink_row)
    p = jnp.exp(s - m)
    e_sink = jnp.exp(sink_row - m)
    inv = 1.0 / (jnp.sum(p, axis=0, keepdims=True) + e_sink)
    return p * inv, e_sink * inv


def _sink_row(sinks_ref, gk):
    return jnp.concatenate([jnp.full((1, CHUNK), sinks_ref[4 * gk + r], F32) for r in range(4)], axis=1)


def _mixer_specs(nb, rev):
    def blk(i):
        return nb - 1 - i if rev else i

    def prev(i):
        return jnp.maximum(blk(i) - 1, 0)

    return dict(
        cur=pl.BlockSpec((CHUNK, D_IN), lambda i, *_: (blk(i), 0)),
        prev_kv=pl.BlockSpec((CHUNK, 2 * 256), lambda i, *_: (prev(i), OFF_K // 512)),
        freq=pl.BlockSpec((1, 128), lambda i, *_: (0, 0)),
        vec=pl.BlockSpec((1, D_A), lambda i, *_: (0, 0)),
        wsp=pl.BlockSpec((A_GROUPS, CHUNK, CHUNK), lambda i, *_: (0, 0, 0)),
        bsp=pl.BlockSpec((A_GROUPS, CHUNK), lambda i, *_: (0, 0)),
        smem=pl.BlockSpec(memory_space=pltpu.SMEM),
        blk=blk,
    )


def _mixer_fwd(proj, freqs, ln_g, ln_b, w_sp, b_sp, sinks, wo_all):
    s = proj.shape[0]
    nb = s // CHUNK
    sp = _mixer_specs(nb, rev=False)

    def body(cur_ref, pkv_ref, freq_ref, lg_ref, lb_ref, w_ref, b_ref, sinks_ref, wo_in, y_ref, wo_ref,
             bcol, wtril, mask, rope_rows, rope_state, rope_last, send_sems, recv_sems):
        i = pl.program_id(0)
        gather = _InPlaceGather(wo_ref, send_sems, recv_sems)
        rope = _RopeTables(freq_ref, rope_rows, rope_state, rope_last)

        @pl.when(i == 0)
        def _():
            gather.start()
            _bias_columns(b_ref, bcol)
            _set_tril(w_ref, wtril)
            _set_unfold_masks(mask)
            rope.start(-1, 1)
            rope_last[...] = jnp.zeros_like(rope_last)

        @pl.when(i == (7 * nb) // 8)
        def _():
            for j in range(3):
                gather.pass_on(j)

        vln, _, _ = _layer_norm(cur_ref[:, OFF_VA:OFF_ZA].astype(F32), lg_ref[...], lb_ref[...])
        vln = vln.astype(BF16)

        def gating_group(g):
            cols = slice(g * 128, (g + 1) * 128)
            sg = jnp.dot(wtril[g], vln[:, cols], preferred_element_type=F32) + bcol[g]
            u = cur_ref[:, OFF_U + g * 128:OFF_U + (g + 1) * 128].astype(F32)
            z = cur_ref[:, OFF_ZA + g * 128:OFF_ZA + (g + 1) * 128].astype(F32)
            y_ref[:, cols] = (u * sg * (z * _sigmoid(z))).astype(BF16)

        rope.step()
        cur_t, prev_t = rope.tables(), rope.kept()
        rope.keep(cur_t)
        qr = _rope(cur_ref[:, OFF_Q:OFF_K].astype(F32), *cur_t) * ATTN_SCALE
        kr = jnp.concatenate([_rope(pkv_ref[:, 0:256].astype(F32), *prev_t),
                              _rope(cur_ref[:, OFF_K:OFF_V].astype(F32), *cur_t)], axis=0)
        v_t = jnp.concatenate([pkv_ref[:, 256:512], cur_ref[:, OFF_V:OFF_ZB]], axis=0).astype(F32).T.astype(BF16)
        outs = []
        from_prev = _from_prev()
        for gk in range(N_KV_HEADS):
            q_st = _stack_heads(qr[:, (2 * gk) * 128:(2 * gk + 1) * 128], qr[:, (2 * gk + 1) * 128:(2 * gk + 2) * 128])
            probs, _ = _attn_probs(q_st, _dup_kv_head(kr, gk), _sink_row(sinks_ref, gk), from_prev, i == 0)
            gating_group(2 * gk)
            outs.append(jnp.dot(v_t[gk * HEAD_DIM:(gk + 1) * HEAD_DIM], _unfold_band(probs.astype(BF16), mask),
                                preferred_element_type=F32))
            gating_group(2 * gk + 1)
        zb = cur_ref[:, OFF_ZB:D_IN].astype(F32)
        y_ref[:, D_A:D_MODEL] = (_heads_to_lanes(outs) * (zb * _sigmoid(zb))).astype(BF16)

        @pl.when(i == nb - 1)
        def _():
            gather.wait_sibling(0)
            for j in range(3):
                gather.wait_sibling(4 + j)
            gather.wait_sends()

    hbm = pl.BlockSpec(memory_space=pl.ANY)
    return pl.pallas_call(
        body, name="mixer_fwd", grid=(nb,),
        in_specs=[sp["cur"], sp["prev_kv"], sp["freq"], sp["vec"], sp["vec"], sp["wsp"], sp["bsp"], sp["smem"], hbm],
        out_specs=[pl.BlockSpec((CHUNK, D_MODEL), lambda i: (i, 0)), hbm],
        out_shape=[SDS((s, D_MODEL), BF16), SDS(wo_all.shape, wo_all.dtype)],
        scratch_shapes=[pltpu.VMEM((A_GROUPS, CHUNK, CHUNK), F32), pltpu.VMEM((A_GROUPS, CHUNK, CHUNK), BF16),
                        pltpu.VMEM((2, CHUNK, 4 * CHUNK), BF16), *_rope_scratch(), *_gather_scratch()],
        input_output_aliases={8: 1},
        compiler_params=_params("arbitrary"),
    )(proj, proj, freqs, ln_g, ln_b, w_sp, b_sp, sinks, wo_all)


def _out_proj_loss(y, x, target, wo, gate, shift_f, scale_f, fng):
    s = y.shape[0]
    tm, tr = 256, 128
    nt = s // tm

    def body(y_ref, x_ref, t_ref, wo_ref, gate_ref, sh_ref, sc_ref, g_ref, dx1_ref, do_ref, dy_ref, sums_ref,
             do_last, do_work):
        i = pl.program_id(0)

        @pl.when(i == 0)
        def _():
            sums_ref[...] = jnp.zeros_like(sums_ref)
            do_last[...] = jnp.zeros_like(do_last)

        do_work[...] = do_last[...]
        o = jnp.dot(y_ref[...], wo_ref[...], preferred_element_type=F32)
        gate, g, sh = gate_ref[...], g_ref[...], sh_ref[...]
        one_sc = 1.0 + sc_ref[...]
        cs, inv_d = g * one_sc, 1.0 / D_MODEL

        def rowsum(v):
            return jnp.sum(v, axis=0, keepdims=True)

        sums = [jnp.zeros((1, D_MODEL), F32) for _ in range(4)]
        for c in range(tm // tr):
            rows = slice(c * tr, (c + 1) * tr)
            oc = o[rows]
            x1 = x_ref[rows, :] + gate * oc
            r = lax.rsqrt(jnp.sum(x1 * x1, axis=-1, keepdims=True) * inv_d + NORM_EPS)
            x1n = x1 * r
            diff = x1n * cs + sh - t_ref[rows, :]
            w = diff * x1n
            lane_sum = jnp.sum(w * cs, axis=-1, keepdims=True)
            dx1 = (diff * cs) * (r * inv_d) - x1n * (r * lane_sum * (inv_d * inv_d))
            dx1_ref[rows, :] = dx1
            do = (dx1 * gate).astype(BF16)
            do_ref[rows, :] = do
            do_last[rows, :] = do
            for k, v in enumerate((dx1 * oc, diff, w, diff * diff)):
                sums[k] = sums[k] + rowsum(v)
        live = jnp.where(i < nt, 1.0, 0.0)
        for row, v in ((SUM_GATE, sums[0]), (SUM_SHIFT_F, inv_d * sums[1]), (SUM_SCALE_F, inv_d * (sums[2] * g)),
                       (SUM_FNG, inv_d * (sums[2] * one_sc)), (SUM_SQ_ERR, sums[3])):
            sums_ref[row:row + 1, :] += live * v
        dy_ref[...] = lax.dot_general(do_work[...], wo_ref[...], NT, preferred_element_type=F32).astype(BF16)

    tile = pl.BlockSpec((tm, D_MODEL), lambda i: (jnp.minimum(i, nt - 1), 0))
    row = pl.BlockSpec((1, D_MODEL), lambda i: (0, 0))
    return pl.pallas_call(
        body, name="out_proj_loss", grid=(nt + 1,),
        in_specs=[tile, tile, tile, pl.BlockSpec((D_MODEL, D_MODEL), lambda i: (0, 0)), row, row, row, row],
        out_specs=[tile, tile, pl.BlockSpec((tm, D_MODEL), lambda i: (jnp.maximum(i - 1, 0), 0)),
                   pl.BlockSpec((8, D_MODEL), lambda i: (0, 0))],
        out_shape=[SDS((s, D_MODEL), F32), SDS((s, D_MODEL), BF16), SDS((s, D_MODEL), BF16), SDS((8, D_MODEL), F32)],
        scratch_shapes=[pltpu.VMEM((tm, D_MODEL), BF16), pltpu.VMEM((tm, D_MODEL), BF16)],
        compiler_params=_params("arbitrary"),
    )(y, x, target, wo, gate, shift_f, scale_f, fng)


ROW_DBSP, ROW_DSINKS, MISC_ROWS = 0, 8, 32


def _mixer_bwd(me, proj, dy, freqs, ln_g, ln_b, w_sp, b_sp, sinks, pair):
    s = proj.shape[0]
    nb = s // CHUNK
    sp = _mixer_specs(nb, rev=True)

    def body(me_ref, cur_ref, pkv_ref, dy_ref, freq_ref, lg_ref, lb_ref, w_ref, b_ref, sinks_ref, pair_ref,
             dproj_ref, dln_ref, dw_ref, misc_ref, parts_ref, bcol, wtril, dbcol, carry, mask, rope_rows, rope_state,
             rope_last, send_sems, recv_sems):
        i = pl.program_id(0)
        block = nb - 1 - i
        rope = _RopeTables(freq_ref, rope_rows, rope_state, rope_last)

        @pl.when(i == 0)
        def _():
            for cp in _chip_scatter(pair_ref, parts_ref, send_sems, recv_sems):
                cp.start()
            _bias_columns(b_ref, bcol)
            _set_tril(w_ref, wtril)
            _set_unfold_masks(mask)
            rope.start(nb - 1, -1)
            rope.keep(rope.tables())
            dbcol[...] = jnp.zeros_like(dbcol)
            carry[...] = jnp.zeros_like(carry)
            dln_ref[...] = jnp.zeros_like(dln_ref)
            dw_ref[...] = jnp.zeros_like(dw_ref)
            misc_ref[...] = jnp.zeros_like(misc_ref)

        vln, vhat, rstd = _layer_norm(cur_ref[:, OFF_VA:OFF_ZA].astype(F32), lg_ref[...], lb_ref[...])
        vln = vln.astype(BF16)
        d_vln = []

        def gating_group(g):
            cols = slice(g * 128, (g + 1) * 128)
            w_g = wtril[g]
            sg = jnp.dot(w_g, vln[:, cols], preferred_element_type=F32) + bcol[g]
            u = cur_ref[:, OFF_U + g * 128:OFF_U + (g + 1) * 128].astype(F32)
            z = cur_ref[:, OFF_ZA + g * 128:OFF_ZA + (g + 1) * 128].astype(F32)
            dya = dy_ref[:, cols].astype(F32)
            sig = _sigmoid(z)
            d_ya = dya * (z * sig)
            dproj_ref[:, OFF_ZA + g * 128:OFF_ZA + (g + 1) * 128] = (
                dya * (u * sg) * (sig * (1.0 + z * (1.0 - sig)))).astype(BF16)
            dproj_ref[:, OFF_U + g * 128:OFF_U + (g + 1) * 128] = (d_ya * sg).astype(BF16)
            d_s = d_ya * u
            dbcol[g] += d_s
            d_sb = d_s.astype(BF16)
            dw_ref[g] += lax.dot_general(d_sb, vln[:, cols], NT, preferred_element_type=F32)
            d_vln.append(lax.dot_general(w_g, d_sb, TN, preferred_element_type=F32))

        cur_t = rope.kept()
        rope.step()
        prev_t = rope.tables()
        rope.keep(prev_t)
        band_t = tuple(jnp.concatenate([p, c], axis=0) for p, c in zip(prev_t, cur_t))
        qr = _rope(cur_ref[:, OFF_Q:OFF_K].astype(F32), *cur_t) * ATTN_SCALE
        kr = jnp.concatenate([_rope(pkv_ref[:, 0:256].astype(F32), *prev_t),
                              _rope(cur_ref[:, OFF_K:OFF_V].astype(F32), *cur_t)], axis=0)
        vb = jnp.concatenate([pkv_ref[:, 256:512], cur_ref[:, OFF_V:OFF_ZB]], axis=0).astype(F32)
        k_t, v_t = (kr.T * ATTN_SCALE).astype(BF16), vb.T.astype(BF16)
        zb = cur_ref[:, OFF_ZB:D_IN].astype(F32)
        dyb = dy_ref[:, D_A:D_MODEL].astype(F32)
        sig = _sigmoid(zb)
        d_yb = dyb * (zb * sig)
        outs, dqs = [], []
        dk_pairs = [jnp.zeros((2 * CHUNK, 128), F32) for _ in range(2)]
        dv_pairs = [jnp.zeros((2 * CHUNK, 128), F32) for _ in range(2)]
        from_prev = _from_prev()
        for gk in range(N_KV_HEADS):
            heads = slice(gk * HEAD_DIM, (gk + 1) * HEAD_DIM)
            q_st = _stack_heads(qr[:, (2 * gk) * 128:(2 * gk + 1) * 128], qr[:, (2 * gk + 1) * 128:(2 * gk + 2) * 128])
            k_dup, v_dup = _dup_kv_head(kr, gk), _dup_kv_head(vb, gk)
            probs, p_sink = _attn_probs(q_st, k_dup, _sink_row(sinks_ref, gk), from_prev, block == 0)
            probs_b = _unfold_band(probs.astype(BF16), mask)
            outs.append(jnp.dot(v_t[heads], probs_b, preferred_element_type=F32))
            do_st = _stack_heads(d_yb[:, (2 * gk) * 128:(2 * gk + 1) * 128], d_yb[:, (2 * gk + 1) * 128:(2 * gk + 2) * 128])
            dp = _fold_band(lax.dot_general(v_dup, do_st, NT, preferred_element_type=F32), from_prev)
            delta = jnp.sum(probs * dp, axis=0, keepdims=True)
            ds = _unfold_band((probs * (dp - delta)).astype(BF16), mask)
            gating_group(2 * gk)
            d_sink = -p_sink * delta
            for r in range(4):
                row = ROW_DSINKS + 4 * gk + r
                misc_ref[row:row + 1, :] += jnp.broadcast_to(
                    jnp.sum(d_sink[:, r * CHUNK:(r + 1) * CHUNK], axis=1, keepdims=True), (1, 128))
            dqs.append(jnp.dot(k_t[heads], ds, preferred_element_type=F32))
            dk_pairs[gk // 2] += _fold_kv_head(jnp.dot(ds, q_st, preferred_element_type=F32), gk)
            dv_pairs[gk // 2] += _fold_kv_head(jnp.dot(probs_b, do_st, preferred_element_type=F32), gk)
            gating_group(2 * gk + 1)
        d_vln = jnp.concatenate(d_vln, axis=1)
        dln_ref[0:1, :] += jnp.sum(d_vln * vhat, axis=0, keepdims=True)
        dln_ref[1:2, :] += jnp.sum(d_vln, axis=0, keepdims=True)
        d_vhat = d_vln * lg_ref[...]
        d_va = rstd * (d_vhat - jnp.mean(d_vhat, axis=-1, keepdims=True)
                       - vhat * jnp.mean(d_vhat * vhat, axis=-1, keepdims=True))
        dproj_ref[:, OFF_VA:OFF_ZA] = d_va.astype(BF16)
        dproj_ref[:, OFF_ZB:D_IN] = (dyb * _heads_to_lanes(outs) * (sig * (1.0 + zb * (1.0 - sig)))).astype(BF16)
        dproj_ref[:, OFF_Q:OFF_K] = _rope_bwd(_heads_to_lanes(dqs), *cur_t).astype(BF16)
        dk_band = _rope_bwd(jnp.concatenate(dk_pairs, axis=1), *band_t)
        dv_band = jnp.concatenate(dv_pairs, axis=1)
        dproj_ref[:, OFF_K:OFF_V] = (dk_band[CHUNK:] + carry[:, 0:256]).astype(BF16)
        dproj_ref[:, OFF_V:OFF_ZB] = (dv_band[CHUNK:] + carry[:, 256:512]).astype(BF16)
        carry[:, 0:256] = dk_band[:CHUNK]
        carry[:, 256:512] = dv_band[:CHUNK]

        @pl.when(i == nb - 1)
        def _():
            t = lax.broadcasted_iota(jnp.int32, (CHUNK, CHUNK), 0)
            tp = lax.broadcasted_iota(jnp.int32, (CHUNK, CHUNK), 1)
            for g in range(A_GROUPS):
                dw_ref[g] = jnp.where(tp <= t, dw_ref[g], 0.0)
                misc_ref[pl.ds(ROW_DBSP + g, 1), :] = jnp.sum(dbcol[g].T, axis=0, keepdims=True)
            scatter = _chip_scatter(pair_ref, parts_ref, send_sems, recv_sems)
            for cp in scatter:
                cp.wait_recv()
            for cp in scatter:
                cp.wait_send()

    blk = sp["blk"]
    hbm = pl.BlockSpec(memory_space=pl.ANY)
    return pl.pallas_call(
        body, name="mixer_bwd",
        grid_spec=pltpu.PrefetchScalarGridSpec(
            num_scalar_prefetch=1, grid=(nb,),
            in_specs=[sp["cur"], sp["prev_kv"], pl.BlockSpec((CHUNK, D_MODEL), lambda i, me_ref: (blk(i), 0)),
                      sp["freq"], sp["vec"], sp["vec"], sp["wsp"], sp["bsp"], sp["smem"], hbm],
            out_specs=[pl.BlockSpec((CHUNK, D_IN), lambda i, me_ref: (blk(i), 0)),
                       pl.BlockSpec((8, D_A), lambda i, me_ref: (me_ref[0], 0)),
                       pl.BlockSpec((A_GROUPS, CHUNK, CHUNK), lambda i, me_ref: (me_ref[0], 0, 0)),
                       pl.BlockSpec((MISC_ROWS, 128), lambda i, me_ref: (me_ref[0], 0)), hbm],
            scratch_shapes=[pltpu.VMEM((A_GROUPS, CHUNK, CHUNK), F32), pltpu.VMEM((A_GROUPS, CHUNK, CHUNK), BF16),
                            pltpu.VMEM((A_GROUPS, CHUNK, CHUNK), F32), pltpu.VMEM((CHUNK, 512), F32), pltpu.VMEM((2, CHUNK, 4 * CHUNK), BF16),
                            *_rope_scratch(), *_scatter_scratch()]),
        out_shape=[SDS((s, D_IN), BF16), SDS((N_DEV * 8, D_A), F32), SDS((N_DEV * A_GROUPS, CHUNK, CHUNK), F32),
                   SDS((N_DEV * MISC_ROWS, 128), F32), SDS((3,) + pair.shape[1:], pair.dtype)],
        compiler_params=_params("arbitrary"),
    )(me, proj, proj, dy, freqs, ln_g, ln_b, w_sp, b_sp, sinks, pair)


def _wgrad_pair(name, a, b, bt, gathers=()):
    s, m = a.shape
    n = b.shape[1]
    bm, half = m // 4, m // 8
    bt = min(bt, s)
    steps = s // bt
    last = 4 * steps
    n_g = len(gathers)

    def body(*refs):
        a_ref, b_ref = refs[:2]
        out_ref, bufs = refs[2 + n_g], refs[3 + n_g:3 + 2 * n_g]
        acc, kept, got, sent, send_sems, recv_sems = refs[3 + 2 * n_g:9 + 2 * n_g]
        sems = refs[9 + 2 * n_g:]
        g = pl.program_id(0)
        tile, t = g // steps, g % steps
        mx, my, mc = _mesh_pos()
        jobs = [_InPlaceGather(bufs[k], sems[2 * k], sems[2 * k + 1]) for k in range(n_g)]

        def exchange(q):
            return pltpu.make_async_remote_copy(src_ref=sent, dst_ref=got.at[q % 2], send_sem=send_sems.at[q],
                                                recv_sem=recv_sems.at[q], device_id=(mx, my, 1 - mc),
                                                device_id_type=MESH)

        @pl.when(g == 0)
        def _():
            for job in jobs:
                job.start()

        @pl.when(g == 2 * steps)
        def _():
            for job in jobs:
                for j in range(3):
                    job.pass_on(j)

        @pl.when(g < last)
        def _():
            prod = lax.dot_general(a_ref[...], b_ref[...], TN, preferred_element_type=F32)

            @pl.when(t == 0)
            def _():
                acc[...] = prod

            @pl.when(t > 0)
            def _():
                acc[...] += prod

        @pl.when((t == 0) & (g > 0))
        def _():
            q = tile - 1
            exchange(q).wait_recv()
            out_ref[0] = (kept[q % 2].astype(F32) + got[q % 2].astype(F32)).astype(BF16)

        @pl.when((t == steps - 1) & (g < last))
        def _():
            @pl.when(tile > 0)
            def _():
                exchange(tile - 1).wait_send()

            kept[tile % 2] = acc[pl.ds(pl.multiple_of(mc * half, 8), half), :].astype(BF16)
            sent[...] = acc[pl.ds(pl.multiple_of((1 - mc) * half, 8), half), :].astype(BF16)
            exchange(tile).start()

        @pl.when(g == last)
        def _():
            exchange(3).wait_send()
            for job in jobs:
                job.wait_sibling(0)
                for j in range(3):
                    job.wait_sibling(4 + j)
                job.wait_sends()

    def a_tile(g):
        gg = jnp.minimum(g, last - 1)
        return (gg % steps, gg // steps)

    def b_tile(g):
        return (jnp.minimum(g, last - 1) % steps, 0)

    hbm = pl.BlockSpec(memory_space=pl.ANY)
    outs = pl.pallas_call(
        body, name=name, grid=(last + 1,),
        in_specs=[pl.BlockSpec((bt, bm), a_tile), pl.BlockSpec((bt, n), b_tile)] + [hbm] * n_g,
        out_specs=[pl.BlockSpec((1, half, n), lambda g: (jnp.maximum(g - 1, 0) // steps, 0, 0))] + [hbm] * n_g,
        out_shape=[SDS((4, half, n), BF16)] + [SDS(gb.shape, gb.dtype) for gb in gathers],
        scratch_shapes=[pltpu.VMEM((bm, n), F32), pltpu.VMEM((2, half, n), BF16), pltpu.VMEM((2, half, n), BF16),
                        pltpu.VMEM((half, n), BF16), pltpu.SemaphoreType.DMA((4,)), pltpu.SemaphoreType.DMA((4,))]
        + _gather_scratch() * n_g,
        input_output_aliases={2 + k: 1 + k for k in range(n_g)},
        compiler_params=_params("arbitrary"),
    )(a, b, *gathers)
    return outs[0], outs[1:]


def _in_proj_bwd(dproj, wt, x, dx1, scale, norm_g, sums_o, pair):
    s = x.shape[0]
    tm, tk, tr = min(1024, s), D_IN // 4, 64
    ksteps = D_IN // tk

    def body(dp_ref, wt_ref, x_hbm, dx1_hbm, sc_ref, g_ref, so_ref, pair_ref, gx_ref, sums_ref, parts_ref, x_buf,
             dx1_buf, tile_sems, send_sems, recv_sems):
        i, k = pl.program_id(0), pl.program_id(1)

        def tile_copies():
            rows = pl.ds(pl.multiple_of(i * tm, tm), tm)
            return (pltpu.make_async_copy(x_hbm.at[rows], x_buf, tile_sems.at[0]),
                    pltpu.make_async_copy(dx1_hbm.at[rows], dx1_buf, tile_sems.at[1]))

        @pl.when((i == 0) & (k == 0))
        def _():
            for cp in _chip_scatter(pair_ref, parts_ref, send_sems, recv_sems):
                cp.start()
            sums_ref[...] = so_ref[...]

        @pl.when(k == 0)
        def _():
            for cp in tile_copies():
                cp.start()
            gx_ref[...] = jnp.dot(dp_ref[...], wt_ref[...], preferred_element_type=F32)

        @pl.when(k > 0)
        def _():
            gx_ref[...] += jnp.dot(dp_ref[...], wt_ref[...], preferred_element_type=F32)

        @pl.when(k == ksteps - 1)
        def _():
            for cp in tile_copies():
                cp.wait()
            one_sc, g = 1.0 + sc_ref[...], g_ref[...]
            cs = one_sc * g

            def chunk(j, sums):
                rows = pl.ds(pl.multiple_of(j * tr, tr), tr)
                dh, xv = gx_ref[rows, :], x_buf[rows, :]
                dhx = dh * xv
                r = lax.rsqrt(jnp.sum(xv * xv, axis=-1, keepdims=True) * (1.0 / D_MODEL) + NORM_EPS)
                coef = (r * r * r) * (jnp.sum(dhx * cs, axis=-1, keepdims=True) * (1.0 / D_MODEL))
                gx_ref[rows, :] = dx1_buf[rows, :] + r * (dh * cs) - xv * coef
                return (sums[0] + jnp.sum(dh, axis=0, keepdims=True), sums[1] + jnp.sum(dhx * r, axis=0, keepdims=True))

            zero = jnp.zeros((1, D_MODEL), F32)
            sums = lax.fori_loop(0, tm // tr, chunk, (zero, zero))
            sums_ref[SUM_SHIFT:SUM_SHIFT + 1, :] += sums[0]
            sums_ref[SUM_SCALE:SUM_SCALE + 1, :] += sums[1] * g
            sums_ref[SUM_NORM_G:SUM_NORM_G + 1, :] += sums[1] * one_sc

        @pl.when((i == s // tm - 1) & (k == ksteps - 1))
        def _():
            scatter = _chip_scatter(pair_ref, parts_ref, send_sems, recv_sems)
            for cp in scatter:
                cp.wait_recv()
            for cp in scatter:
                cp.wait_send()

    row = pl.BlockSpec((1, D_MODEL), lambda i, k: (0, 0))
    hbm = pl.BlockSpec(memory_space=pl.ANY)
    return pl.pallas_call(
        body, name="in_proj_bwd", grid=(s // tm, ksteps),
        in_specs=[pl.BlockSpec((tm, tk), lambda i, k: (i, k)), pl.BlockSpec((tk, D_MODEL), lambda i, k: (k, 0)),
                  hbm, hbm, row, row, pl.BlockSpec((8, D_MODEL), lambda i, k: (0, 0)), hbm],
        out_specs=[pl.BlockSpec((tm, D_MODEL), lambda i, k: (i, 0)), pl.BlockSpec((8, D_MODEL), lambda i, k: (0, 0)),
                   hbm],
        out_shape=[SDS((s, D_MODEL), F32), SDS((8, D_MODEL), F32), SDS((3,) + pair.shape[1:], pair.dtype)],
        scratch_shapes=[pltpu.VMEM((tm, D_MODEL), F32), pltpu.VMEM((tm, D_MODEL), F32),
                        pltpu.SemaphoreType.DMA((2,)), *_scatter_scratch()],
        compiler_params=_params("arbitrary", "arbitrary"),
    )(dproj, wt, x, dx1, scale, norm_g, sums_o, pair)


def _sum_chips(own_ref, parts_ref):
    return ((own_ref[0].astype(F32) + parts_ref[0].astype(F32)) + parts_ref[1].astype(F32)) + parts_ref[2].astype(F32)


def _adam_rows(name, chip, pair, parts, w, m, v, tr):
    rows = w.shape[0]

    def body(chip_ref, own_ref, p_ref, w_ref, m_ref, v_ref, g_ref, d_ref, nm_ref, nv_ref):
        g = _sum_chips(own_ref, p_ref)
        g_ref[...] = g
        d_ref[...], nm_ref[...], nv_ref[...] = _adamw(w_ref[...], g, m_ref[...], v_ref[...])

    blk = pl.BlockSpec((tr, D_MODEL), lambda j, chip_ref: (j, 0))
    return pl.pallas_call(
        body, name=name,
        grid_spec=pltpu.PrefetchScalarGridSpec(
            num_scalar_prefetch=1, grid=(rows // tr,),
            in_specs=[pl.BlockSpec((1, tr, D_MODEL), lambda j, chip_ref: (chip_ref[0], j, 0)),
                      pl.BlockSpec((3, tr, D_MODEL), lambda j, chip_ref: (0, j, 0)), blk, blk, blk],
            out_specs=[blk] * 4),
        out_shape=[SDS(w.shape, F32)] * 4, compiler_params=_params("parallel"),
    )(chip, pair, parts, w, m, v)


def _adam_ada(name, cact, dmod, w, m, v):
    n = w.shape[1]
    tr = 512

    def body(c_ref, dm_ref, w_ref, m_ref, v_ref, g_ref, d_ref, nm_ref, nv_ref):
        pad_c = jnp.concatenate([c_ref[...], jnp.zeros_like(c_ref)], axis=0).astype(BF16)
        pad_d = jnp.concatenate([dm_ref[...], jnp.zeros_like(dm_ref)], axis=0).astype(BF16)
        g = lax.dot_general(pad_c, pad_d, TN, preferred_element_type=F32)
        g_ref[...] = g
        d_ref[...], nm_ref[...], nv_ref[...] = _adamw(w_ref[...], g, m_ref[...], v_ref[...])

    blk = pl.BlockSpec((tr, n), lambda j: (j, 0))
    return pl.pallas_call(
        body, name=name, grid=(D_MODEL // tr,),
        in_specs=[pl.BlockSpec((N_DEV, tr), lambda j: (0, j)), pl.BlockSpec((N_DEV, n), lambda j: (0, 0)),
                  blk, blk, blk],
        out_specs=[blk] * 4, out_shape=[SDS(w.shape, F32)] * 4,
        compiler_params=_params("parallel"),
    )(cact, dmod, w, m, v)


SMALL_PARAMS = ("w_spatial", "b_spatial", "sinks", "norm_g", "ln_v_g", "ln_v_b", "final_norm_g", "b_ada", "b_ada_final")


def _adam_small(d_wsp, misc, d_ln, sums, params):
    n_p = len(SMALL_PARAMS)

    def body(*refs):
        wsp_ref, misc_ref, ln_ref, sums_ref = refs[:4]
        wmv = [refs[4 + 3 * k:7 + 3 * k] for k in range(n_p)]
        loss_ref = refs[4 + 3 * n_p]
        outs = [refs[5 + 3 * n_p + 4 * k:9 + 3 * n_p + 4 * k] for k in range(n_p)]

        def column_sum(row):
            return total(sums_ref, (row, row + 1))

        def total(ref, rows=None):
            def part(j):
                return ref[j] if rows is None else ref[j, rows[0]:rows[1], :]
            acc = part(0)
            for j in range(1, N_DEV):
                acc = acc + part(j)
            return acc

        sink_rows = total(misc_ref, (ROW_DSINKS, ROW_DSINKS + 16))
        diag = (lax.broadcasted_iota(jnp.int32, (16, 128), 0) == lax.broadcasted_iota(jnp.int32, (16, 128), 1))
        grads = dict(
            w_spatial=total(wsp_ref), b_spatial=total(misc_ref, (ROW_DBSP, ROW_DBSP + A_GROUPS)),
            sinks=jnp.sum(jnp.where(diag, sink_rows, 0.0), axis=0, keepdims=True),
            norm_g=column_sum(SUM_NORM_G), ln_v_g=total(ln_ref, (0, 1)), ln_v_b=total(ln_ref, (1, 2)),
            final_norm_g=column_sum(SUM_FNG),
            b_ada=jnp.concatenate([column_sum(SUM_SHIFT), column_sum(SUM_SCALE), column_sum(SUM_GATE)], axis=1),
            b_ada_final=jnp.concatenate([column_sum(SUM_SHIFT_F), column_sum(SUM_SCALE_F)], axis=1))
        sq_err = jnp.sum(column_sum(SUM_SQ_ERR), axis=1, keepdims=True)
        loss_ref[...] = jnp.broadcast_to(sq_err * (0.5 / D_MODEL), (1, 128))
        for k, name in enumerate(SMALL_PARAMS):
            w_ref, m_ref, v_ref = wmv[k]
            g_ref, d_ref, nm_ref, nv_ref = outs[k]
            g_ref[...] = grads[name]
            d_ref[...], nm_ref[...], nv_ref[...] = _adamw(w_ref[...], grads[name], m_ref[...], v_ref[...])

    flat = [a for name in SMALL_PARAMS for a in params[name]]
    vmem = pl.BlockSpec(memory_space=pltpu.VMEM)
    out_shape = [SDS((1, 128), F32)] + [SDS(params[name][0].shape, F32) for name in SMALL_PARAMS for _ in range(4)]
    outs = pl.pallas_call(
        body, name="adam_small", in_specs=[vmem] * (4 + len(flat)), out_specs=[vmem] * len(out_shape),
        out_shape=out_shape, compiler_params=_params(),
    )(d_wsp, misc, d_ln, sums, *flat)
    return outs[0], {name: outs[1 + 4 * k:5 + 4 * k] for k, name in enumerate(SMALL_PARAMS)}


def kernel(x, c, w_ada, b_ada, norm_g, w_in, ln_v_g, ln_v_b, w_spatial, b_spatial, sinks, w_out, w_ada_final, b_ada_final, final_norm_g, loss_target, m_w_ada, m_b_ada, m_norm_g, m_w_in, m_ln_v_g, m_ln_v_b, m_w_spatial, m_b_spatial, m_sinks, m_w_out, m_w_ada_final, m_b_ada_final, m_final_norm_g, v_w_ada, v_b_ada, v_norm_g, v_w_in, v_ln_v_g, v_ln_v_b, v_w_spatial, v_b_spatial, v_sinks, v_w_out, v_w_ada_final, v_b_ada_final, v_final_norm_g):
    me = 4 * lax.axis_index("x") + 2 * lax.axis_index("y") + lax.axis_index("c")
    x2, tgt = x[0], loss_target[0]
    fng = final_norm_g.reshape(1, D_MODEL)

    n_ada, n_ada_f = w_ada.shape[2], w_ada_final.shape[1]
    cact, mod, mod_f = _ada_exchange(c, w_ada[0], b_ada.reshape(N_DEV, n_ada), w_ada_final,
                                     b_ada_final.reshape(N_DEV, n_ada_f))
    cact = cact.reshape(N_DEV, D_MODEL)
    mod, mod_f = mod.reshape(1, 3 * D_MODEL), mod_f.reshape(1, 2 * D_MODEL)
    shift, scale, gate = mod[:, :D_MODEL], mod[:, D_MODEL:2 * D_MODEL], mod[:, 2 * D_MODEL:]
    shift_f, scale_f = mod_f[:, :D_MODEL], mod_f[:, D_MODEL:]

    wt_f32, m_wt, v_wt = (jnp.swapaxes(a, 1, 2)[0] for a in (w_in, m_w_in, v_w_in))
    xi, yi = lax.axis_index("x"), lax.axis_index("y")
    chip_order = jnp.stack([2 * xi + yi, 2 * (1 - xi) + yi, 2 * xi + 1 - yi, 2 * (1 - xi) + 1 - yi]).astype(jnp.int32)
    wt_mine, wo_mine = _prep_weights(me.reshape(1), wt_f32, w_out[0])

    freqs = _rope_freqs()
    sinks_v = sinks.reshape(16)
    h, proj, wt = _gather_in_proj(chip_order, x2, shift, scale, norm_g, wt_mine)
    y, wo = _mixer_fwd(proj, freqs, ln_v_g, ln_v_b, w_spatial[0], b_spatial[0], sinks_v, wo_mine)
    dx1, do, dy, sums_o = _out_proj_loss(y, x2, tgt, wo, gate, shift_f, scale_f, fng)

    chip = (2 * lax.axis_index("x") + lax.axis_index("y")).reshape(1)
    pair_out, _ = _wgrad_pair("wgrad_out", y, do, 4096)
    dproj, d_ln, d_wsp, misc, parts_out = _mixer_bwd(
        me.reshape(1), proj, dy, freqs, ln_v_g, ln_v_b, w_spatial[0], b_spatial[0], sinks_v, pair_out)
    pair_in, (d_ln, d_wsp, misc) = _wgrad_pair(
        "wgrad_in", dproj, h, 1024, gathers=(d_ln, d_wsp.reshape(N_DEV * A_GROUPS * CHUNK, CHUNK), misc))
    grad_x, sums, parts_in = _in_proj_bwd(dproj, wt, x2, dx1, scale, norm_g, sums_o, pair_in)
    wt_leaves = [jnp.swapaxes(a[None], 1, 2)
                 for a in _adam_rows("adam_w_in", chip, pair_in, parts_in, wt_f32, m_wt, v_wt, 176)]
    w_out_leaves = [a[None] for a in _adam_rows("adam_w_out", chip, pair_out, parts_out, w_out[0], m_w_out[0], v_w_out[0], 64)]

    (sums,) = _all_gather("gather_sums", [sums], pltpu.VMEM)
    natural = dict(w_spatial=(A_GROUPS * CHUNK, CHUNK), b_spatial=(A_GROUPS, CHUNK), sinks=(1, 16), norm_g=(1, D_MODEL),
                   ln_v_g=(1, D_A), ln_v_b=(1, D_A), final_norm_g=(1, D_MODEL), b_ada=(1, 3 * D_MODEL),
                   b_ada_final=(1, 2 * D_MODEL))
    given = dict(
        w_spatial=(w_spatial, m_w_spatial, v_w_spatial), b_spatial=(b_spatial, m_b_spatial, v_b_spatial),
        sinks=(sinks, m_sinks, v_sinks), norm_g=(norm_g, m_norm_g, v_norm_g), ln_v_g=(ln_v_g, m_ln_v_g, v_ln_v_g),
        ln_v_b=(ln_v_b, m_ln_v_b, v_ln_v_b), final_norm_g=(final_norm_g, m_final_norm_g, v_final_norm_g),
        b_ada=(b_ada, m_b_ada, v_b_ada), b_ada_final=(b_ada_final, m_b_ada_final, v_b_ada_final))
    params = {name: tuple(a.reshape(natural[name]) for a in given[name]) for name in SMALL_PARAMS}
    params["sinks"] = tuple(jnp.pad(a, ((0, 0), (0, 128 - 16))) for a in params["sinks"])
    loss, small = _adam_small(d_wsp.reshape(N_DEV, A_GROUPS * CHUNK, CHUNK), misc.reshape(N_DEV, MISC_ROWS, 128),
                              d_ln.reshape(N_DEV, 8, D_A), sums, params)
    small["sinks"] = [a[:, :16] for a in small["sinks"]]
    small = {name: [a.reshape(given[name][0].shape) for a in small[name]] for name in SMALL_PARAMS}

    dmod_all = jnp.concatenate([sums[:, SUM_SHIFT], sums[:, SUM_SCALE], sums[:, SUM_GATE]], axis=1)
    dmod_f_all = jnp.concatenate([sums[:, SUM_SHIFT_F], sums[:, SUM_SCALE_F]], axis=1)
    dmod_mine = lax.dynamic_slice_in_dim(dmod_all, me * n_ada, n_ada, axis=1)
    dmod_f_mine = lax.dynamic_slice_in_dim(dmod_f_all, me * n_ada_f, n_ada_f, axis=1)
    ada = _adam_ada("adam_w_ada", cact, dmod_mine, w_ada[0], m_w_ada[0], v_w_ada[0])
    ada_f = _adam_ada("adam_w_ada_final", cact, dmod_f_mine, w_ada_final, m_w_ada_final, v_w_ada_final)

    def leaves(k):
        return (ada[k][None], small["b_ada"][k], small["norm_g"][k], wt_leaves[k], small["ln_v_g"][k],
                small["ln_v_b"][k], small["w_spatial"][k], small["b_spatial"][k], small["sinks"][k], w_out_leaves[k],
                ada_f[k], small["b_ada_final"][k], small["final_norm_g"][k])

    return (loss[0, 0], grad_x[None], *leaves(0), *leaves(1), *leaves(2), *leaves(3))
```

```python
import jax
import jax.numpy as jnp
from jax import lax
from jax.experimental import pallas as pl
from jax.experimental.pallas import tpu as pltpu

D_MODEL = 2048
D_IN = 5632
D_A = 1024
CHUNK = 128
A_GROUPS = 8
HEAD_DIM = 64
N_KV_HEADS = 4
N_DEV = 8
ROPE_THETA = 10000.0
NORM_EPS = 1e-5
ATTN_SCALE = HEAD_DIM ** -0.5

ADAM_LR = 0.001
ADAM_B1 = 0.9
ADAM_B2 = 0.999
ADAM_EPS = 1e-08
ADAM_WD = 0.01
ADAM_STEP = 10

OFF_U, OFF_VA, OFF_ZA, OFF_Q, OFF_K, OFF_V, OFF_ZB = 0, 1024, 2048, 3072, 4096, 4352, 4608

SUM_SHIFT, SUM_SCALE, SUM_NORM_G, SUM_GATE, SUM_SHIFT_F, SUM_SCALE_F, SUM_FNG, SUM_SQ_ERR = range(8)

V7X_VMEM_LIMIT_BYTES = 56 * 1024 * 1024

F32 = jnp.float32
BF16 = jnp.bfloat16
MESH = pl.DeviceIdType.MESH
SDS = jax.ShapeDtypeStruct
NT = (((1,), (1,)), ((), ()))
TN = (((0,), (0,)), ((), ()))


def _params(*semantics):
    return pltpu.CompilerParams(dimension_semantics=semantics or None, vmem_limit_bytes=V7X_VMEM_LIMIT_BYTES)


def _mesh_pos():
    return lax.axis_index("x"), lax.axis_index("y"), lax.axis_index("c")


def _sigmoid(z):
    return 1.0 / (1.0 + jnp.exp(-z))


def _adamw(w, g, m, v):
    m = ADAM_B1 * m + (1.0 - ADAM_B1) * g
    v = ADAM_B2 * v + (1.0 - ADAM_B2) * (g * g)
    m_hat = m / (1.0 - ADAM_B1 ** ADAM_STEP)
    v_hat = v / (1.0 - ADAM_B2 ** ADAM_STEP)
    delta = -ADAM_LR * (m_hat / (jnp.sqrt(v_hat) + ADAM_EPS) + ADAM_WD * w)
    return delta, m, v


def _all_gather(name, blocks, memory_space):
    n_arr = len(blocks)

    def body(*refs):
        ins, outs = refs[:n_arr], refs[n_arr:2 * n_arr]
        send_sems, recv_sems, local_sems = refs[2 * n_arr:]
        x, y, c = _mesh_pos()
        me, sibling = (x, y, c), (x, y, 1 - c)
        chips = [(1 - x, y), (x, 1 - y), (1 - x, 1 - y)]

        def slot(p):
            return 4 * p[0] + 2 * p[1] + p[2]

        def copy(a, k, block, to, src=None):
            dst = outs[a].at[slot(block)]
            return pltpu.make_async_remote_copy(
                src_ref=dst if src is None else src, dst_ref=dst,
                send_sem=send_sems.at[a, k], recv_sem=recv_sems.at[a, k],
                device_id=to, device_id_type=MESH)

        mine = [pltpu.make_async_copy(ins[a], outs[a].at[slot(me)], local_sems.at[a]) for a in range(n_arr)]
        for cp in mine:
            cp.start()
        first = []
        for a in range(n_arr):
            first.append(copy(a, 0, me, sibling, src=ins[a]))
            first += [copy(a, 1 + j, me, (*chip, c), src=ins[a]) for j, chip in enumerate(chips)]
        for cp in first:
            cp.start()
        passed = []
        for j, chip in enumerate(chips):
            for a in range(n_arr):
                copy(a, 1 + j, (*chip, c), me).wait_recv()
                fwd = copy(a, 4 + j, (*chip, c), sibling)
                fwd.start()
                passed.append(fwd)
        for a in range(n_arr):
            copy(a, 0, sibling, me).wait_recv()
            for j, chip in enumerate(chips):
                copy(a, 4 + j, (*chip, 1 - c), me).wait_recv()
        for cp in first + passed:
            cp.wait_send()
        for cp in mine:
            cp.wait()

    spec = pl.BlockSpec(memory_space=memory_space)
    return pl.pallas_call(
        body, name=name,
        out_shape=[SDS((N_DEV,) + b.shape, b.dtype) for b in blocks],
        in_specs=[spec] * n_arr, out_specs=[spec] * n_arr,
        scratch_shapes=[pltpu.SemaphoreType.DMA((n_arr, 7)), pltpu.SemaphoreType.DMA((n_arr, 7)),
                        pltpu.SemaphoreType.DMA((n_arr,))],
        compiler_params=_params(),
    )(*blocks)


def _ada_exchange(c, w_ada, b_ada8, w_ada_f, b_ada_f8):
    n1, n2 = w_ada.shape[1], w_ada_f.shape[1]

    def body(c_ref, w1_ref, b1_ref, w2_ref, b2_ref, cact_ref, mod_ref, modf_ref,
             cact_buf, res1, res2, send1, send2, sems_s, sems_r):
        x, y, c_pos = _mesh_pos()
        me = 4 * x + 2 * y + c_pos
        flips = [(k >> 2 & 1, k >> 1 & 1, k & 1) for k in range(1, N_DEV)]

        def peer(f):
            return (1 - x if f[0] else x, 1 - y if f[1] else y, 1 - c_pos if f[2] else c_pos)

        cv = c_ref[...]
        cact = cv * _sigmoid(cv)
        cact_buf[...] = cact
        cact_ref[me] = cact

        def rdma(phase, k, src, dst, f):
            return pltpu.make_async_remote_copy(src_ref=src, dst_ref=dst, send_sem=sems_s.at[phase, k],
                                                recv_sem=sems_r.at[phase, k], device_id=peer(f), device_id_type=MESH)

        gather = [rdma(0, k, cact_buf, cact_ref.at[me], f) for k, f in enumerate(flips)]
        for cp in gather:
            cp.start()
        for cp in gather:
            cp.wait_recv()
        for cp in gather:
            cp.wait_send()

        rid = lax.broadcasted_iota(jnp.int32, (N_DEV, D_MODEL), 0)
        rows = jnp.zeros((N_DEV, D_MODEL), F32)
        for j in range(N_DEV):
            rows = jnp.where(rid == j, jnp.broadcast_to(cact_ref[j], (N_DEV, D_MODEL)), rows)
        rows = rows.astype(BF16)
        res1[...] = jnp.dot(rows, w1_ref[...].astype(BF16), preferred_element_type=F32) + b1_ref[pl.ds(me, 1), :]
        res2[...] = jnp.dot(rows, w2_ref[...].astype(BF16), preferred_element_type=F32) + b2_ref[pl.ds(me, 1), :]
        for j in range(N_DEV):
            send1[j] = res1[pl.ds(j, 1), :]
            send2[j] = res2[pl.ds(j, 1), :]
        mod_ref[me] = send1[me]
        modf_ref[me] = send2[me]
        scatter = []
        for k, f in enumerate(flips):
            to = me ^ (k + 1)
            scatter.append(rdma(1, k, send1.at[to], mod_ref.at[me], f))
            scatter.append(rdma(2, k, send2.at[to], modf_ref.at[me], f))
        for cp in scatter:
            cp.start()
        for cp in scatter:
            cp.wait_recv()
        for cp in scatter:
            cp.wait_send()

    vmem = pl.BlockSpec(memory_space=pltpu.VMEM)
    return pl.pallas_call(
        body, name="ada_exchange",
        out_shape=[SDS((N_DEV, 1, D_MODEL), F32), SDS((N_DEV, 1, n1), F32), SDS((N_DEV, 1, n2), F32)],
        in_specs=[vmem] * 5, out_specs=[vmem] * 3,
        scratch_shapes=[pltpu.VMEM((1, D_MODEL), F32), pltpu.VMEM((N_DEV, n1), F32), pltpu.VMEM((N_DEV, n2), F32),
                        pltpu.VMEM((N_DEV, 1, n1), F32), pltpu.VMEM((N_DEV, 1, n2), F32),
                        pltpu.SemaphoreType.DMA((3, 7)), pltpu.SemaphoreType.DMA((3, 7))],
        compiler_params=_params(),
    )(c, w_ada, b_ada8, w_ada_f, b_ada_f8)


def _chip_scatter(pair_ref, parts_ref, send_sems, recv_sems):
    x, y, c = _mesh_pos()
    chips = [(1 - x, y), (x, 1 - y), (1 - x, 1 - y)]
    return [pltpu.make_async_remote_copy(
        src_ref=pair_ref.at[2 * cx + cy], dst_ref=parts_ref.at[j], send_sem=send_sems.at[j], recv_sem=recv_sems.at[j],
        device_id=(cx, cy, c), device_id_type=MESH) for j, (cx, cy) in enumerate(chips)]


def _scatter_scratch():
    return [pltpu.SemaphoreType.DMA((3,)), pltpu.SemaphoreType.DMA((3,))]


def _prep_weights(me, wt, w_out):
    steps = 4

    def body(me_ref, wt_ref, wo_ref, wtb_ref, wob_ref):
        wtb_ref[...] = wt_ref[...].astype(BF16)
        wob_ref[...] = wo_ref[...].astype(BF16)

    def rows(a, mine):
        blk = (a.shape[0] // steps, a.shape[1])
        return pl.BlockSpec(blk, (lambda i, me_ref: (steps * me_ref[0] + i, 0)) if mine else (lambda i, me_ref: (i, 0)))

    return pl.pallas_call(
        body, name="prep_weights",
        grid_spec=pltpu.PrefetchScalarGridSpec(
            num_scalar_prefetch=1, grid=(steps,),
            in_specs=[rows(wt, False), rows(w_out, False)], out_specs=[rows(wt, True), rows(w_out, True)]),
        out_shape=[SDS((N_DEV * wt.shape[0], D_MODEL), BF16), SDS((N_DEV * w_out.shape[0], D_MODEL), BF16)],
        compiler_params=_params("parallel"),
    )(me, wt, w_out)


class _InPlaceGather:
    def __init__(self, buf_ref, send_sems, recv_sems, relay=False):
        self.buf, self.send_sems, self.recv_sems, self.relay = buf_ref, send_sems, recv_sems, relay
        self.n = buf_ref.shape[0] // N_DEV
        x, y, c = _mesh_pos()
        self.me, self.sibling, self.core = (x, y, c), (x, y, 1 - c), c
        self.chips = [(1 - x, y), (x, 1 - y), (1 - x, 1 - y)]
        self.relay_from = (jnp.where(c == 0, 1 - x, x), jnp.where(c == 0, y, 1 - y), c)
        self.relay_to = (jnp.where(c == 0, x, 1 - x), jnp.where(c == 0, 1 - y, y), c)

    def copy(self, k, block, to):
        start = pl.multiple_of((4 * block[0] + 2 * block[1] + block[2]) * self.n, self.n)
        rows = self.buf.at[pl.ds(start, self.n)]
        return pltpu.make_async_remote_copy(src_ref=rows, dst_ref=rows, send_sem=self.send_sems.at[k],
                                            recv_sem=self.recv_sems.at[k], device_id=to, device_id_type=MESH)

    def start(self):
        self.copy(0, self.me, self.sibling).start()
        for j, chip in enumerate(self.chips[:2] if self.relay else self.chips):
            self.copy(1 + j, self.me, (*chip, self.core)).start()

    def relay_diagonal(self):
        self.copy(3, self.relay_from, self.relay_to).start()

    def pass_on(self, j):
        self.copy(1 + j, (*self.chips[j], self.core), self.me).wait_recv()
        self.copy(4 + j, (*self.chips[j], self.core), self.sibling).start()

    def wait_sibling(self, k):
        self.copy(k, self.sibling, self.me).wait_recv()

    def wait_sends(self):
        for k in range(7):
            self.copy(k, self.me, self.sibling).wait_send()


def _gather_scratch():
    return [pltpu.SemaphoreType.DMA((7,)), pltpu.SemaphoreType.DMA((7,))]


def _gather_in_proj(order, x, shift, scale, norm_g, wt_all):
    s = x.shape[0]
    th, tm = min(512, s), min(1024, s)
    nh, ni = s // th, s // tm
    tn = D_IN // 4
    steps = nh + 4 * ni

    def body(order_ref, x_ref, shift_ref, scale_ref, g_ref, wt_in, h_ref, proj_ref, wt_ref,
             h_scr, w_buf, load_sems, send_sems, recv_sems):
        g = pl.program_id(0)
        gather = _InPlaceGather(wt_ref, send_sems, recv_sems, relay=True)

        def tile_load(slot, chip):
            return pltpu.make_async_copy(wt_ref.at[pl.ds(pl.multiple_of(chip * tn, tn), tn)], w_buf.at[slot],
                                         load_sems.at[slot])

        @pl.when(g == 0)
        def _():
            gather.start()

        @pl.when(g < nh)
        def _():
            xv = x_ref[...]
            r = lax.rsqrt(jnp.mean(xv * xv, axis=-1, keepdims=True) + NORM_EPS)
            hb = (((xv * r) * g_ref[...]) * (1.0 + scale_ref[...]) + shift_ref[...]).astype(BF16)
            h_ref[...] = hb
            h_scr[pl.ds(pl.multiple_of(g * th, th), th), :] = hb

        @pl.when(g == nh - 1)
        def _():
            gather.wait_sibling(0)
            tile_load(0, order_ref[0]).start()

        @pl.when(g >= nh)
        def _():
            t, i = (g - nh) // ni, (g - nh) % ni

            @pl.when(i == 0)
            def _():
                tile_load(t % 2, order_ref[t]).wait()

            @pl.when((i == ni - 1) & (t == 0))
            def _():
                gather.pass_on(0)
                gather.pass_on(1)
                gather.relay_diagonal()

            @pl.when((i == ni // 2) & (t == 2))
            def _():
                gather.pass_on(2)

            for j in range(3):
                @pl.when((i == ni - 1) & (t == j))
                def _():
                    gather.wait_sibling(4 + j)
                    tile_load((j + 1) % 2, order_ref[j + 1]).start()

            lhs = h_scr[pl.ds(pl.multiple_of(i * tm, tm), tm), :]
            proj_ref[...] = lax.dot_general(lhs, w_buf[t % 2], NT, preferred_element_type=F32).astype(BF16)

        @pl.when(g == steps - 1)
        def _():
            gather.wait_sends()

    def h_tile(g, order_ref):
        return (jnp.minimum(g, nh - 1), 0)

    def proj_tile(g, order_ref):
        mm = jnp.maximum(g - nh, 0)
        return (mm % ni, order_ref[mm // ni])

    row = pl.BlockSpec((1, D_MODEL), lambda g, order_ref: (0, 0))
    hbm = pl.BlockSpec(memory_space=pl.ANY)
    return pl.pallas_call(
        body, name="gather_in_proj",
        grid_spec=pltpu.PrefetchScalarGridSpec(
            num_scalar_prefetch=1, grid=(steps,),
            in_specs=[pl.BlockSpec((th, D_MODEL), h_tile), row, row, row, hbm],
            out_specs=[pl.BlockSpec((th, D_MODEL), h_tile), pl.BlockSpec((tm, tn), proj_tile), hbm],
            scratch_shapes=[pltpu.VMEM((s, D_MODEL), BF16), pltpu.VMEM((2, tn, D_MODEL), BF16),
                            pltpu.SemaphoreType.DMA((2,)), *_gather_scratch()]),
        out_shape=[SDS((s, D_MODEL), BF16), SDS((s, D_IN), BF16), SDS(wt_all.shape, BF16)],
        input_output_aliases={5: 2},
        compiler_params=_params("arbitrary"),
    )(order, x, shift, scale, norm_g, wt_all)


def _rope_freqs():
    inv_freq = ROPE_THETA ** (-jnp.arange(0, HEAD_DIM, 2, dtype=F32) / HEAD_DIM)
    return jnp.tile(inv_freq, 4).reshape(1, 128)


class _RopeTables:
    def __init__(self, freq_ref, rows_ref, state_ref, last_ref):
        self.freq, self.rows, self.state, self.last = freq_ref, rows_ref, state_ref, last_ref

    def start(self, block, direction):
        ang = lax.broadcasted_iota(jnp.int32, (CHUNK, 128), 0).astype(F32) * self.freq[...]
        self.rows[0] = jnp.cos(ang)
        self.rows[1] = jnp.sin(ang)
        base = jnp.asarray(block * CHUNK, dtype=F32) * self.freq[...]
        turn = float(direction * CHUNK) * self.freq[...]
        self.state[0:1, :] = jnp.cos(base)
        self.state[1:2, :] = jnp.sin(base)
        self.state[2:3, :] = jnp.cos(turn)
        self.state[3:4, :] = jnp.sin(turn)

    def step(self):
        c, s, ct, st = (self.state[k:k + 1, :] for k in range(4))
        self.state[0:1, :] = c * ct - s * st
        self.state[1:2, :] = s * ct + c * st

    def tables(self):
        c, s = self.state[0:1, :], self.state[1:2, :]
        cos = c * self.rows[0] - s * self.rows[1]
        sin = s * self.rows[0] + c * self.rows[1]
        first_half = (lax.broadcasted_iota(jnp.int32, (1, 128), 1) & (HEAD_DIM - 1)) < HEAD_DIM // 2
        return cos, jnp.where(first_half, -sin, 0.0), jnp.where(first_half, 0.0, sin)

    def keep(self, tabs):
        for k in range(3):
            self.last[k] = tabs[k]

    def kept(self):
        return tuple(self.last[k] for k in range(3))


def _rope_scratch():
    return [pltpu.VMEM((2, CHUNK, 128), F32), pltpu.VMEM((8, 128), F32), pltpu.VMEM((3, CHUNK, 128), F32)]


def _rope(v, cos, sin_lo, sin_hi):
    width = v.shape[1]
    rep = (1, width // 128)
    return (v * jnp.tile(cos, rep) + pltpu.roll(v, width - 32, 1) * jnp.tile(sin_lo, rep)
            + pltpu.roll(v, 32, 1) * jnp.tile(sin_hi, rep))


def _rope_bwd(d, cos, sin_lo, sin_hi):
    width = d.shape[1]
    rep = (1, width // 128)
    return (d * jnp.tile(cos, rep) + pltpu.roll(d * jnp.tile(sin_lo, rep), 32, 1)
            + pltpu.roll(d * jnp.tile(sin_hi, rep), width - 32, 1))


def _layer_norm(v, g, b):
    mu = jnp.mean(v, axis=-1, keepdims=True)
    vc = v - mu
    rstd = lax.rsqrt(jnp.mean(vc * vc, axis=-1, keepdims=True) + NORM_EPS)
    vhat = vc * rstd
    return vhat * g + b, vhat, rstd


def _set_tril(w_ref, out_ref):
    t = lax.broadcasted_iota(jnp.int32, (CHUNK, CHUNK), 0)
    tp = lax.broadcasted_iota(jnp.int32, (CHUNK, CHUNK), 1)
    for g in range(A_GROUPS):
        out_ref[g] = jnp.where(tp <= t, w_ref[g], 0.0).astype(BF16)


def _bias_columns(b_ref, out_ref):
    for g in range(A_GROUPS):
        out_ref[g] = jnp.broadcast_to(b_ref[pl.ds(g, 1), :], (CHUNK, CHUNK)).T


def _from_prev():
    r = lax.broadcasted_iota(jnp.int32, (CHUNK, 4 * CHUNK), 0)
    i = lax.broadcasted_iota(jnp.int32, (CHUNK, 4 * CHUNK), 1) & (CHUNK - 1)
    return r > i


def _set_unfold_masks(mask_ref):
    prev = _from_prev()
    mask_ref[0] = jnp.where(prev, 1.0, 0.0).astype(BF16)
    mask_ref[1] = jnp.where(prev, 0.0, 1.0).astype(BF16)


def _fold_band(t, from_prev):
    return jnp.where(from_prev, t[:CHUNK], t[CHUNK:])


def _unfold_band(t, mask_ref):
    return jnp.concatenate([t * mask_ref[0], t * mask_ref[1]], axis=0)


def _low_lanes():
    return lax.broadcasted_iota(jnp.int32, (1, 128), 1) < HEAD_DIM


def _stack_heads(pair_a, pair_b):
    lo = _low_lanes()
    return jnp.concatenate([jnp.where(lo, pair_a, 0.0), jnp.where(lo, 0.0, pair_a),
                            jnp.where(lo, pair_b, 0.0), jnp.where(lo, 0.0, pair_b)], axis=0).astype(BF16)


def _heads_to_lanes(per_group):
    rows = [t[:, r * CHUNK:(r + 1) * CHUNK] for t in per_group for r in range(4)]
    return jnp.concatenate(rows, axis=0).T


def _dup_kv_head(band, gk):
    pair = band[:, (gk // 2) * 128:(gk // 2 + 1) * 128]
    lo = _low_lanes()
    one = jnp.where(lo if gk % 2 == 0 else jnp.logical_not(lo), pair, 0.0)
    return (one + pltpu.roll(one, HEAD_DIM, 1)).astype(BF16)


def _fold_kv_head(dup_grad, gk):
    both = dup_grad + pltpu.roll(dup_grad, HEAD_DIM, 1)
    lo = _low_lanes()
    return jnp.where(lo if gk % 2 == 0 else jnp.logical_not(lo), both, 0.0)


def _attn_probs(q_st, k_dup, sink_row, from_prev, first_block):
    s = lax.dot_general(k_dup, q_st, NT, preferred_element_type=F32)
    no_prev = jnp.where(first_block, -jnp.inf, 0.0)
    s = jnp.where(from_prev, s[:CHUNK] + no_prev, s[CHUNK:])
    m = jnp.maximum(jnp.max(s, axis=0, keepdims=True), sink_row)
    p = jnp.exp(s - m)
    e_sink = jnp.exp(sink_row - m)
    inv = 1.0 / (jnp.sum(p, axis=0, keepdims=True) + e_sink)
    return p * inv, e_sink * inv


def _sink_row(sinks_ref, gk):
    return jnp.concatenate([jnp.full((1, CHUNK), sinks_ref[4 * gk + r], F32) for r in range(4)], axis=1)


def _mixer_specs(nb, rev):
    def blk(i):
        return nb - 1 - i if rev else i

    def prev(i):
        return jnp.maximum(blk(i) - 1, 0)

    return dict(
        cur=pl.BlockSpec((CHUNK, D_IN), lambda i, *_: (blk(i), 0)),
        prev_kv=pl.BlockSpec((CHUNK, 2 * 256), lambda i, *_: (prev(i), OFF_K // 512)),
        freq=pl.BlockSpec((1, 128), lambda i, *_: (0, 0)),
        vec=pl.BlockSpec((1, D_A), lambda i, *_: (0, 0)),
        wsp=pl.BlockSpec((A_GROUPS, CHUNK, CHUNK), lambda i, *_: (0, 0, 0)),
        bsp=pl.BlockSpec((A_GROUPS, CHUNK), lambda i, *_: (0, 0)),
        smem=pl.BlockSpec(memory_space=pltpu.SMEM),
        blk=blk,
    )


def _mixer_fwd(proj, freqs, ln_g, ln_b, w_sp, b_sp, sinks, wo_all):
    s = proj.shape[0]
    nb = s // CHUNK
    sp = _mixer_specs(nb, rev=False)

    def body(cur_ref, pkv_ref, freq_ref, lg_ref, lb_ref, w_ref, b_ref, sinks_ref, wo_in, y_ref, wo_ref,
             bcol, wtril, mask, rope_rows, rope_state, rope_last, send_sems, recv_sems):
        i = pl.program_id(0)
        gather = _InPlaceGather(wo_ref, send_sems, recv_sems)
        rope = _RopeTables(freq_ref, rope_rows, rope_state, rope_last)

        @pl.when(i == 0)
        def _():
            gather.start()
            _bias_columns(b_ref, bcol)
            _set_tril(w_ref, wtril)
            _set_unfold_masks(mask)
            rope.start(-1, 1)
            rope_last[...] = jnp.zeros_like(rope_last)

        @pl.when(i == (7 * nb) // 8)
        def _():
            for j in range(3):
                gather.pass_on(j)

        vln, _, _ = _layer_norm(cur_ref[:, OFF_VA:OFF_ZA].astype(F32), lg_ref[...], lb_ref[...])
        vln = vln.astype(BF16)

        def gating_group(g):
            cols = slice(g * 128, (g + 1) * 128)
            sg = jnp.dot(wtril[g], vln[:, cols], preferred_element_type=F32) + bcol[g]
            u = cur_ref[:, OFF_U + g * 128:OFF_U + (g + 1) * 128].astype(F32)
            z = cur_ref[:, OFF_ZA + g * 128:OFF_ZA + (g + 1) * 128].astype(F32)
            y_ref[:, cols] = (u * sg * (z * _sigmoid(z))).astype(BF16)

        rope.step()
        cur_t, prev_t = rope.tables(), rope.kept()
        rope.keep(cur_t)
        qr = _rope(cur_ref[:, OFF_Q:OFF_K].astype(F32), *cur_t) * ATTN_SCALE
        kr = jnp.concatenate([_rope(pkv_ref[:, 0:256].astype(F32), *prev_t),
                              _rope(cur_ref[:, OFF_K:OFF_V].astype(F32), *cur_t)], axis=0)
        v_t = jnp.concatenate([pkv_ref[:, 256:512], cur_ref[:, OFF_V:OFF_ZB]], axis=0).astype(F32).T.astype(BF16)
        outs = []
        from_prev = _from_prev()
        for gk in range(N_KV_HEADS):
            q_st = _stack_heads(qr[:, (2 * gk) * 128:(2 * gk + 1) * 128], qr[:, (2 * gk + 1) * 128:(2 * gk + 2) * 128])
            probs, _ = _attn_probs(q_st, _dup_kv_head(kr, gk), _sink_row(sinks_ref, gk), from_prev, i == 0)
            gating_group(2 * gk)
            outs.append(jnp.dot(v_t[gk * HEAD_DIM:(gk + 1) * HEAD_DIM], _unfold_band(probs.astype(BF16), mask),
                                preferred_element_type=F32))
            gating_group(2 * gk + 1)
        zb = cur_ref[:, OFF_ZB:D_IN].astype(F32)
        y_ref[:, D_A:D_MODEL] = (_heads_to_lanes(outs) * (zb * _sigmoid(zb))).astype(BF16)

        @pl.when(i == nb - 1)
        def _():
            gather.wait_sibling(0)
            for j in range(3):
                gather.wait_sibling(4 + j)
            gather.wait_sends()

    hbm = pl.BlockSpec(memory_space=pl.ANY)
    return pl.pallas_call(
        body, name="mixer_fwd", grid=(nb,),
        in_specs=[sp["cur"], sp["prev_kv"], sp["freq"], sp["vec"], sp["vec"], sp["wsp"], sp["bsp"], sp["smem"], hbm],
        out_specs=[pl.BlockSpec((CHUNK, D_MODEL), lambda i: (i, 0)), hbm],
        out_shape=[SDS((s, D_MODEL), BF16), SDS(wo_all.shape, wo_all.dtype)],
        scratch_shapes=[pltpu.VMEM((A_GROUPS, CHUNK, CHUNK), F32), pltpu.VMEM((A_GROUPS, CHUNK, CHUNK), BF16),
                        pltpu.VMEM((2, CHUNK, 4 * CHUNK), BF16), *_rope_scratch(), *_gather_scratch()],
        input_output_aliases={8: 1},
        compiler_params=_params("arbitrary"),
    )(proj, proj, freqs, ln_g, ln_b, w_sp, b_sp, sinks, wo_all)


def _out_proj_loss(y, x, target, wo, gate, shift_f, scale_f, fng):
    s = y.shape[0]
    tm, tr = 256, 128
    nt = s // tm

    def body(y_ref, x_ref, t_ref, wo_ref, gate_ref, sh_ref, sc_ref, g_ref, dx1_ref, do_ref, dy_ref, sums_ref,
             room_ref, do_last, do_work):
        del room_ref
        i = pl.program_id(0)

        @pl.when(i == 0)
        def _():
            sums_ref[...] = jnp.zeros_like(sums_ref)
            do_last[...] = jnp.zeros_like(do_last)

        do_work[...] = do_last[...]
        o = jnp.dot(y_ref[...], wo_ref[...], preferred_element_type=F32)
        gate, g, sh = gate_ref[...], g_ref[...], sh_ref[...]
        one_sc = 1.0 + sc_ref[...]
        cs, inv_d = g * one_sc, 1.0 / D_MODEL

        def rowsum(v):
            return jnp.sum(v, axis=0, keepdims=True)

        sums = [jnp.zeros((1, D_MODEL), F32) for _ in range(4)]
        for c in range(tm // tr):
            rows = slice(c * tr, (c + 1) * tr)
            oc = o[rows]
            x1 = x_ref[rows, :] + gate * oc
            r = lax.rsqrt(jnp.sum(x1 * x1, axis=-1, keepdims=True) * inv_d + NORM_EPS)
            x1n = x1 * r
            diff = x1n * cs + sh - t_ref[rows, :]
            w = diff * x1n
            lane_sum = jnp.sum(w * cs, axis=-1, keepdims=True)
            dx1 = (diff * cs) * (r * inv_d) - x1n * (r * lane_sum * (inv_d * inv_d))
            dx1_ref[rows, :] = dx1
            do = (dx1 * gate).astype(BF16)
            do_ref[rows, :] = do
            do_last[rows, :] = do
            for k, v in enumerate((dx1 * oc, diff, w, diff * diff)):
                sums[k] = sums[k] + rowsum(v)
        live = jnp.where(i < nt, 1.0, 0.0)
        for row, v in ((SUM_GATE, sums[0]), (SUM_SHIFT_F, inv_d * sums[1]), (SUM_SCALE_F, inv_d * (sums[2] * g)),
                       (SUM_FNG, inv_d * (sums[2] * one_sc)), (SUM_SQ_ERR, sums[3])):
            sums_ref[row:row + 1, :] += live * v
        dy_ref[...] = lax.dot_general(do_work[...], wo_ref[...], NT, preferred_element_type=F32).astype(BF16)

    tile = pl.BlockSpec((tm, D_MODEL), lambda i: (jnp.minimum(i, nt - 1), 0))
    row = pl.BlockSpec((1, D_MODEL), lambda i: (0, 0))
    return pl.pallas_call(
        body, name="out_proj_loss", grid=(nt + 1,),
        in_specs=[tile, tile, tile, pl.BlockSpec((D_MODEL, D_MODEL), lambda i: (0, 0)), row, row, row, row],
        out_specs=[tile, tile, pl.BlockSpec((tm, D_MODEL), lambda i: (jnp.maximum(i - 1, 0), 0)),
                   pl.BlockSpec((8, D_MODEL), lambda i: (0, 0)), pl.BlockSpec(memory_space=pl.ANY)],
        out_shape=[SDS((s, D_MODEL), F32), SDS((s, D_MODEL), BF16), SDS((s, D_MODEL), BF16), SDS((8, D_MODEL), F32),
                   SDS((s, D_MODEL), F32)],
        scratch_shapes=[pltpu.VMEM((tm, D_MODEL), BF16), pltpu.VMEM((tm, D_MODEL), BF16)],
        compiler_params=_params("arbitrary"),
    )(y, x, target, wo, gate, shift_f, scale_f, fng)


ROW_DBSP, ROW_DSINKS, MISC_ROWS = 0, 8, 32


def _mixer_bwd(me, proj, dy, freqs, ln_g, ln_b, w_sp, b_sp, sinks, pair):
    s = proj.shape[0]
    nb = s // CHUNK
    sp = _mixer_specs(nb, rev=True)

    def body(me_ref, cur_ref, pkv_ref, dy_ref, freq_ref, lg_ref, lb_ref, w_ref, b_ref, sinks_ref, pair_ref,
             dproj_ref, dln_ref, dw_ref, misc_ref, parts_ref, bcol, wtril, dbcol, carry, mask, rope_rows, rope_state,
             rope_last, send_sems, recv_sems):
        i = pl.program_id(0)
        block = nb - 1 - i
        rope = _RopeTables(freq_ref, rope_rows, rope_state, rope_last)

        @pl.when(i == 0)
        def _():
            for cp in _chip_scatter(pair_ref, parts_ref, send_sems, recv_sems):
                cp.start()
            _bias_columns(b_ref, bcol)
            _set_tril(w_ref, wtril)
            _set_unfold_masks(mask)
            rope.start(nb - 1, -1)
            rope.keep(rope.tables())
            dbcol[...] = jnp.zeros_like(dbcol)
            carry[...] = jnp.zeros_like(carry)
            dln_ref[...] = jnp.zeros_like(dln_ref)
            dw_ref[...] = jnp.zeros_like(dw_ref)
            misc_ref[...] = jnp.zeros_like(misc_ref)

        vln, vhat, rstd = _layer_norm(cur_ref[:, OFF_VA:OFF_ZA].astype(F32), lg_ref[...], lb_ref[...])
        vln = vln.astype(BF16)
        d_vln = []

        def gating_group(g):
            cols = slice(g * 128, (g + 1) * 128)
            w_g = wtril[g]
            sg = jnp.dot(w_g, vln[:, cols], preferred_element_type=F32) + bcol[g]
            u = cur_ref[:, OFF_U + g * 128:OFF_U + (g + 1) * 128].astype(F32)
            z = cur_ref[:, OFF_ZA + g * 128:OFF_ZA + (g + 1) * 128].astype(F32)
            dya = dy_ref[:, cols].astype(F32)
            sig = _sigmoid(z)
            d_ya = dya * (z * sig)
            dproj_ref[:, OFF_ZA + g * 128:OFF_ZA + (g + 1) * 128] = (
                dya * (u * sg) * (sig * (1.0 + z * (1.0 - sig)))).astype(BF16)
            dproj_ref[:, OFF_U + g * 128:OFF_U + (g + 1) * 128] = (d_ya * sg).astype(BF16)
            d_s = d_ya * u
            dbcol[g] += d_s
            d_sb = d_s.astype(BF16)
            dw_ref[g] += lax.dot_general(d_sb, vln[:, cols], NT, preferred_element_type=F32)
            d_vln.append(lax.dot_general(w_g, d_sb, TN, preferred_element_type=F32))

        cur_t = rope.kept()
        rope.step()
        prev_t = rope.tables()
        rope.keep(prev_t)
        band_t = tuple(jnp.concatenate([p, c], axis=0) for p, c in zip(prev_t, cur_t))
        qr = _rope(cur_ref[:, OFF_Q:OFF_K].astype(F32), *cur_t) * ATTN_SCALE
        kr = jnp.concatenate([_rope(pkv_ref[:, 0:256].astype(F32), *prev_t),
                              _rope(cur_ref[:, OFF_K:OFF_V].astype(F32), *cur_t)], axis=0)
        vb = jnp.concatenate([pkv_ref[:, 256:512], cur_ref[:, OFF_V:OFF_ZB]], axis=0).astype(F32)
        k_t, v_t = (kr.T * ATTN_SCALE).astype(BF16), vb.T.astype(BF16)
        zb = cur_ref[:, OFF_ZB:D_IN].astype(F32)
        dyb = dy_ref[:, D_A:D_MODEL].astype(F32)
        sig = _sigmoid(zb)
        d_yb = dyb * (zb * sig)
        outs, dqs = [], []
        dk_pairs = [jnp.zeros((2 * CHUNK, 128), F32) for _ in range(2)]
        dv_pairs = [jnp.zeros((2 * CHUNK, 128), F32) for _ in range(2)]
        from_prev = _from_prev()
        for gk in range(N_KV_HEADS):
            heads = slice(gk * HEAD_DIM, (gk + 1) * HEAD_DIM)
            q_st = _stack_heads(qr[:, (2 * gk) * 128:(2 * gk + 1) * 128], qr[:, (2 * gk + 1) * 128:(2 * gk + 2) * 128])
            k_dup, v_dup = _dup_kv_head(kr, gk), _dup_kv_head(vb, gk)
            probs, p_sink = _attn_probs(q_st, k_dup, _sink_row(sinks_ref, gk), from_prev, block == 0)
            probs_b = _unfold_band(probs.astype(BF16), mask)
            outs.append(jnp.dot(v_t[heads], probs_b, preferred_element_type=F32))
            do_st = _stack_heads(d_yb[:, (2 * gk) * 128:(2 * gk + 1) * 128], d_yb[:, (2 * gk + 1) * 128:(2 * gk + 2) * 128])
            dp = _fold_band(lax.dot_general(v_dup, do_st, NT, preferred_element_type=F32), from_prev)
            delta = jnp.sum(probs * dp, axis=0, keepdims=True)
            ds = _unfold_band((probs * (dp - delta)).astype(BF16), mask)
            gating_group(2 * gk)
            d_sink = -p_sink * delta
            for r in range(4):
                row = ROW_DSINKS + 4 * gk + r
                misc_ref[row:row + 1, :] += jnp.broadcast_to(
                    jnp.sum(d_sink[:, r * CHUNK:(r + 1) * CHUNK], axis=1, keepdims=True), (1, 128))
            dqs.append(jnp.dot(k_t[heads], ds, preferred_element_type=F32))
            dk_pairs[gk // 2] += _fold_kv_head(jnp.dot(ds, q_st, preferred_element_type=F32), gk)
            dv_pairs[gk // 2] += _fold_kv_head(jnp.dot(probs_b, do_st, preferred_element_type=F32), gk)
            gating_group(2 * gk + 1)
        d_vln = jnp.concatenate(d_vln, axis=1)
        dln_ref[0:1, :] += jnp.sum(d_vln * vhat, axis=0, keepdims=True)
        dln_ref[1:2, :] += jnp.sum(d_vln, axis=0, keepdims=True)
        d_vhat = d_vln * lg_ref[...]
        d_va = rstd * (d_vhat - jnp.mean(d_vhat, axis=-1, keepdims=True)
                       - vhat * jnp.mean(d_vhat * vhat, axis=-1, keepdims=True))
        dproj_ref[:, OFF_VA:OFF_ZA] = d_va.astype(BF16)
        dproj_ref[:, OFF_ZB:D_IN] = (dyb * _heads_to_lanes(outs) * (sig * (1.0 + zb * (1.0 - sig)))).astype(BF16)
        dproj_ref[:, OFF_Q:OFF_K] = _rope_bwd(_heads_to_lanes(dqs), *cur_t).astype(BF16)
        dk_band = _rope_bwd(jnp.concatenate(dk_pairs, axis=1), *band_t)
        dv_band = jnp.concatenate(dv_pairs, axis=1)
        dproj_ref[:, OFF_K:OFF_V] = (dk_band[CHUNK:] + carry[:, 0:256]).astype(BF16)
        dproj_ref[:, OFF_V:OFF_ZB] = (dv_band[CHUNK:] + carry[:, 256:512]).astype(BF16)
        carry[:, 0:256] = dk_band[:CHUNK]
        carry[:, 256:512] = dv_band[:CHUNK]

        @pl.when(i == nb - 1)
        def _():
            t = lax.broadcasted_iota(jnp.int32, (CHUNK, CHUNK), 0)
            tp = lax.broadcasted_iota(jnp.int32, (CHUNK, CHUNK), 1)
            for g in range(A_GROUPS):
                dw_ref[g] = jnp.where(tp <= t, dw_ref[g], 0.0)
                misc_ref[pl.ds(ROW_DBSP + g, 1), :] = jnp.sum(dbcol[g].T, axis=0, keepdims=True)
            scatter = _chip_scatter(pair_ref, parts_ref, send_sems, recv_sems)
            for cp in scatter:
                cp.wait_recv()
            for cp in scatter:
                cp.wait_send()

    blk = sp["blk"]
    hbm = pl.BlockSpec(memory_space=pl.ANY)
    return pl.pallas_call(
        body, name="mixer_bwd",
        grid_spec=pltpu.PrefetchScalarGridSpec(
            num_scalar_prefetch=1, grid=(nb,),
            in_specs=[sp["cur"], sp["prev_kv"], pl.BlockSpec((CHUNK, D_MODEL), lambda i, me_ref: (blk(i), 0)),
                      sp["freq"], sp["vec"], sp["vec"], sp["wsp"], sp["bsp"], sp["smem"], hbm],
            out_specs=[pl.BlockSpec((CHUNK, D_IN), lambda i, me_ref: (blk(i), 0)),
                       pl.BlockSpec((8, D_A), lambda i, me_ref: (me_ref[0], 0)),
                       pl.BlockSpec((A_GROUPS, CHUNK, CHUNK), lambda i, me_ref: (me_ref[0], 0, 0)),
                       pl.BlockSpec((MISC_ROWS, 128), lambda i, me_ref: (me_ref[0], 0)), hbm],
            scratch_shapes=[pltpu.VMEM((A_GROUPS, CHUNK, CHUNK), F32), pltpu.VMEM((A_GROUPS, CHUNK, CHUNK), BF16),
                            pltpu.VMEM((A_GROUPS, CHUNK, CHUNK), F32), pltpu.VMEM((CHUNK, 512), F32), pltpu.VMEM((2, CHUNK, 4 * CHUNK), BF16),
                            *_rope_scratch(), *_scatter_scratch()]),
        out_shape=[SDS((s, D_IN), BF16), SDS((N_DEV * 8, D_A), F32), SDS((N_DEV * A_GROUPS, CHUNK, CHUNK), F32),
                   SDS((N_DEV * MISC_ROWS, 128), F32), SDS((3,) + pair.shape[1:], pair.dtype)],
        compiler_params=_params("arbitrary"),
    )(me, proj, proj, dy, freqs, ln_g, ln_b, w_sp, b_sp, sinks, pair)


def _wgrad_pair(name, a, b, bt, gathers=()):
    s, m = a.shape
    n = b.shape[1]
    bm, half = m // 4, m // 8
    bt = min(bt, s)
    steps = s // bt
    last = 4 * steps
    n_g = len(gathers)

    def body(*refs):
        a_ref, b_ref = refs[:2]
        out_ref, bufs = refs[2 + n_g], refs[3 + n_g:3 + 2 * n_g]
        acc, kept, got, sent, send_sems, recv_sems = refs[3 + 2 * n_g:9 + 2 * n_g]
        sems = refs[9 + 2 * n_g:]
        g = pl.program_id(0)
        tile, t = g // steps, g % steps
        mx, my, mc = _mesh_pos()
        jobs = [_InPlaceGather(bufs[k], sems[2 * k], sems[2 * k + 1]) for k in range(n_g)]

        def exchange(q):
            return pltpu.make_async_remote_copy(src_ref=sent, dst_ref=got.at[q % 2], send_sem=send_sems.at[q],
                                                recv_sem=recv_sems.at[q], device_id=(mx, my, 1 - mc),
                                                device_id_type=MESH)

        @pl.when(g == 0)
        def _():
            for job in jobs:
                job.start()

        @pl.when(g == 2 * steps)
        def _():
            for job in jobs:
                for j in range(3):
                    job.pass_on(j)

        @pl.when(g < last)
        def _():
            prod = lax.dot_general(a_ref[...], b_ref[...], TN, preferred_element_type=F32)

            @pl.when(t == 0)
            def _():
                acc[...] = prod

            @pl.when(t > 0)
            def _():
                acc[...] += prod

            @pl.when(t == steps - 1)
            def _():
                @pl.when(tile > 0)
                def _():
                    exchange(tile - 1).wait_send()

                kept[tile % 2] = acc[pl.ds(pl.multiple_of(mc * half, 8), half), :].astype(BF16)
                sent[...] = acc[pl.ds(pl.multiple_of((1 - mc) * half, 8), half), :].astype(BF16)
                exchange(tile).start()

        @pl.when((t == 0) & (g > 0))
        def _():
            q = tile - 1
            exchange(q).wait_recv()
            out_ref[0] = (kept[q % 2].astype(F32) + got[q % 2].astype(F32)).astype(BF16)

        @pl.when(g == last)
        def _():
            exchange(3).wait_send()
            for job in jobs:
                job.wait_sibling(0)
                for j in range(3):
                    job.wait_sibling(4 + j)
                job.wait_sends()

    def a_tile(g):
        gg = jnp.minimum(g, last - 1)
        return (gg % steps, gg // steps)

    def b_tile(g):
        return (jnp.minimum(g, last - 1) % steps, 0)

    hbm = pl.BlockSpec(memory_space=pl.ANY)
    outs = pl.pallas_call(
        body, name=name, grid=(last + 1,),
        in_specs=[pl.BlockSpec((bt, bm), a_tile), pl.BlockSpec((bt, n), b_tile)] + [hbm] * n_g,
        out_specs=[pl.BlockSpec((1, half, n), lambda g: (jnp.maximum(g - 1, 0) // steps, 0, 0))] + [hbm] * n_g,
        out_shape=[SDS((4, half, n), BF16)] + [SDS(gb.shape, gb.dtype) for gb in gathers],
        scratch_shapes=[pltpu.VMEM((bm, n), F32), pltpu.VMEM((2, half, n), BF16), pltpu.VMEM((2, half, n), BF16),
                        pltpu.VMEM((half, n), BF16), pltpu.SemaphoreType.DMA((4,)), pltpu.SemaphoreType.DMA((4,))]
        + _gather_scratch() * n_g,
        input_output_aliases={2 + k: 1 + k for k in range(n_g)},
        compiler_params=_params("arbitrary"),
    )(a, b, *gathers)
    return outs[0], outs[1:]


def _in_proj_bwd(dproj, wt, x, dx1, scale, norm_g, sums_o, pair, room):
    s = x.shape[0]
    tm, tk, tr = min(1024, s), D_IN // 4, 64
    ksteps = D_IN // tk

    def body(dp_ref, wt_ref, x_hbm, dx1_hbm, sc_ref, g_ref, so_ref, pair_ref, room_ref, gx_ref, sums_ref, parts_ref,
             x_buf, dx1_buf, tile_sems, send_sems, recv_sems):
        del room_ref
        i, k = pl.program_id(0), pl.program_id(1)

        def tile_copies():
            rows = pl.ds(pl.multiple_of(i * tm, tm), tm)
            return (pltpu.make_async_copy(x_hbm.at[rows], x_buf, tile_sems.at[0]),
                    pltpu.make_async_copy(dx1_hbm.at[rows], dx1_buf, tile_sems.at[1]))

        @pl.when((i == 0) & (k == 0))
        def _():
            for cp in _chip_scatter(pair_ref, parts_ref, send_sems, recv_sems):
                cp.start()
            sums_ref[...] = so_ref[...]

        @pl.when(k == 0)
        def _():
            for cp in tile_copies():
                cp.start()
            gx_ref[...] = jnp.dot(dp_ref[...], wt_ref[...], preferred_element_type=F32)

        @pl.when(k > 0)
        def _():
            gx_ref[...] += jnp.dot(dp_ref[...], wt_ref[...], preferred_element_type=F32)

        @pl.when(k == ksteps - 1)
        def _():
            for cp in tile_copies():
                cp.wait()
            one_sc, g = 1.0 + sc_ref[...], g_ref[...]
            cs = one_sc * g

            def chunk(j, sums):
                rows = pl.ds(pl.multiple_of(j * tr, tr), tr)
                dh, xv = gx_ref[rows, :], x_buf[rows, :]
                dhx = dh * xv
                r = lax.rsqrt(jnp.sum(xv * xv, axis=-1, keepdims=True) * (1.0 / D_MODEL) + NORM_EPS)
                coef = (r * r * r) * (jnp.sum(dhx * cs, axis=-1, keepdims=True) * (1.0 / D_MODEL))
                gx_ref[rows, :] = dx1_buf[rows, :] + r * (dh * cs) - xv * coef
                return (sums[0] + jnp.sum(dh, axis=0, keepdims=True), sums[1] + jnp.sum(dhx * r, axis=0, keepdims=True))

            zero = jnp.zeros((1, D_MODEL), F32)
            sums = lax.fori_loop(0, tm // tr, chunk, (zero, zero))
            sums_ref[SUM_SHIFT:SUM_SHIFT + 1, :] += sums[0]
            sums_ref[SUM_SCALE:SUM_SCALE + 1, :] += sums[1] * g
            sums_ref[SUM_NORM_G:SUM_NORM_G + 1, :] += sums[1] * one_sc

        @pl.when((i == s // tm - 1) & (k == ksteps - 1))
        def _():
            scatter = _chip_scatter(pair_ref, parts_ref, send_sems, recv_sems)
            for cp in scatter:
                cp.wait_recv()
            for cp in scatter:
                cp.wait_send()

    row = pl.BlockSpec((1, D_MODEL), lambda i, k: (0, 0))
    hbm = pl.BlockSpec(memory_space=pl.ANY)
    return pl.pallas_call(
        body, name="in_proj_bwd", grid=(s // tm, ksteps),
        in_specs=[pl.BlockSpec((tm, tk), lambda i, k: (i, k)), pl.BlockSpec((tk, D_MODEL), lambda i, k: (k, 0)),
                  hbm, hbm, row, row, pl.BlockSpec((8, D_MODEL), lambda i, k: (0, 0)), hbm, hbm],
        out_specs=[pl.BlockSpec((tm, D_MODEL), lambda i, k: (i, 0)), pl.BlockSpec((8, D_MODEL), lambda i, k: (0, 0)),
                   hbm],
        out_shape=[SDS((s, D_MODEL), F32), SDS((8, D_MODEL), F32), SDS((3,) + pair.shape[1:], pair.dtype)],
        scratch_shapes=[pltpu.VMEM((tm, D_MODEL), F32), pltpu.VMEM((tm, D_MODEL), F32),
                        pltpu.SemaphoreType.DMA((2,)), *_scatter_scratch()],
        input_output_aliases={8: 0},
        compiler_params=_params("arbitrary", "arbitrary"),
    )(dproj, wt, x, dx1, scale, norm_g, sums_o, pair, room)


def _sum_chips(own_ref, parts_ref):
    return ((own_ref[0].astype(F32) + parts_ref[0].astype(F32)) + parts_ref[1].astype(F32)) + parts_ref[2].astype(F32)


def _adam_rows(name, chip, pair, parts, w, m, v, tr):
    rows = w.shape[0]

    def body(chip_ref, own_ref, p_ref, w_ref, m_ref, v_ref, g_ref, d_ref, nm_ref, nv_ref):
        g = _sum_chips(own_ref, p_ref)
        g_ref[...] = g
        d_ref[...], nm_ref[...], nv_ref[...] = _adamw(w_ref[...], g, m_ref[...], v_ref[...])

    blk = pl.BlockSpec((tr, D_MODEL), lambda j, chip_ref: (j, 0))
    return pl.pallas_call(
        body, name=name,
        grid_spec=pltpu.PrefetchScalarGridSpec(
            num_scalar_prefetch=1, grid=(rows // tr,),
            in_specs=[pl.BlockSpec((1, tr, D_MODEL), lambda j, chip_ref: (chip_ref[0], j, 0)),
                      pl.BlockSpec((3, tr, D_MODEL), lambda j, chip_ref: (0, j, 0)), blk, blk, blk],
            out_specs=[blk] * 4),
        out_shape=[SDS(w.shape, F32)] * 4, compiler_params=_params("parallel"),
    )(chip, pair, parts, w, m, v)


def _adam_ada(name, cact, dmod, w, m, v):
    n = w.shape[1]
    tr = 512

    def body(c_ref, dm_ref, w_ref, m_ref, v_ref, g_ref, d_ref, nm_ref, nv_ref):
        pad_c = jnp.concatenate([c_ref[...], jnp.zeros_like(c_ref)], axis=0).astype(BF16)
        pad_d = jnp.concatenate([dm_ref[...], jnp.zeros_like(dm_ref)], axis=0).astype(BF16)
        g = lax.dot_general(pad_c, pad_d, TN, preferred_element_type=F32)
        g_ref[...] = g
        d_ref[...], nm_ref[...], nv_ref[...] = _adamw(w_ref[...], g, m_ref[...], v_ref[...])

    blk = pl.BlockSpec((tr, n), lambda j: (j, 0))
    return pl.pallas_call(
        body, name=name, grid=(D_MODEL // tr,),
        in_specs=[pl.BlockSpec((N_DEV, tr), lambda j: (0, j)), pl.BlockSpec((N_DEV, n), lambda j: (0, 0)),
                  blk, blk, blk],
        out_specs=[blk] * 4, out_shape=[SDS(w.shape, F32)] * 4,
        compiler_params=_params("parallel"),
    )(cact, dmod, w, m, v)


SMALL_PARAMS = ("w_spatial", "b_spatial", "sinks", "norm_g", "ln_v_g", "ln_v_b", "final_norm_g", "b_ada", "b_ada_final")


def _adam_small(d_wsp, misc, d_ln, sums, params):
    n_p = len(SMALL_PARAMS)

    def body(*refs):
        wsp_ref, misc_ref, ln_ref, sums_ref = refs[:4]
        wmv = [refs[4 + 3 * k:7 + 3 * k] for k in range(n_p)]
        loss_ref = refs[4 + 3 * n_p]
        outs = [refs[5 + 3 * n_p + 4 * k:9 + 3 * n_p + 4 * k] for k in range(n_p)]

        def column_sum(row):
            return total(sums_ref, (row, row + 1))

        def total(ref, rows=None):
            def part(j):
                return ref[j] if rows is None else ref[j, rows[0]:rows[1], :]
            acc = part(0)
            for j in range(1, N_DEV):
                acc = acc + part(j)
            return acc

        sink_rows = total(misc_ref, (ROW_DSINKS, ROW_DSINKS + 16))
        diag = (lax.broadcasted_iota(jnp.int32, (16, 128), 0) == lax.broadcasted_iota(jnp.int32, (16, 128), 1))
        grads = dict(
            w_spatial=total(wsp_ref), b_spatial=total(misc_ref, (ROW_DBSP, ROW_DBSP + A_GROUPS)),
            sinks=jnp.sum(jnp.where(diag, sink_rows, 0.0), axis=0, keepdims=True),
            norm_g=column_sum(SUM_NORM_G), ln_v_g=total(ln_ref, (0, 1)), ln_v_b=total(ln_ref, (1, 2)),
            final_norm_g=column_sum(SUM_FNG),
            b_ada=jnp.concatenate([column_sum(SUM_SHIFT), column_sum(SUM_SCALE), column_sum(SUM_GATE)], axis=1),
            b_ada_final=jnp.concatenate([column_sum(SUM_SHIFT_F), column_sum(SUM_SCALE_F)], axis=1))
        sq_err = jnp.sum(column_sum(SUM_SQ_ERR), axis=1, keepdims=True)
        loss_ref[...] = jnp.broadcast_to(sq_err * (0.5 / D_MODEL), (1, 128))
        for k, name in enumerate(SMALL_PARAMS):
            w_ref, m_ref, v_ref = wmv[k]
            g_ref, d_ref, nm_ref, nv_ref = outs[k]
            g_ref[...] = grads[name]
            d_ref[...], nm_ref[...], nv_ref[...] = _adamw(w_ref[...], grads[name], m_ref[...], v_ref[...])

    flat = [a for name in SMALL_PARAMS for a in params[name]]
    vmem = pl.BlockSpec(memory_space=pltpu.VMEM)
    out_shape = [SDS((1, 128), F32)] + [SDS(params[name][0].shape, F32) for name in SMALL_PARAMS for _ in range(4)]
    outs = pl.pallas_call(
        body, name="adam_small", in_specs=[vmem] * (4 + len(flat)), out_specs=[vmem] * len(out_shape),
        out_shape=out_shape, compiler_params=_params(),
    )(d_wsp, misc, d_ln, sums, *flat)
    return outs[0], {name: outs[1 + 4 * k:5 + 4 * k] for k, name in enumerate(SMALL_PARAMS)}


def kernel(x, c, w_ada, b_ada, norm_g, w_in, ln_v_g, ln_v_b, w_spatial, b_spatial, sinks, w_out, w_ada_final, b_ada_final, final_norm_g, loss_target, m_w_ada, m_b_ada, m_norm_g, m_w_in, m_ln_v_g, m_ln_v_b, m_w_spatial, m_b_spatial, m_sinks, m_w_out, m_w_ada_final, m_b_ada_final, m_final_norm_g, v_w_ada, v_b_ada, v_norm_g, v_w_in, v_ln_v_g, v_ln_v_b, v_w_spatial, v_b_spatial, v_sinks, v_w_out, v_w_ada_final, v_b_ada_final, v_final_norm_g):
    me = 4 * lax.axis_index("x") + 2 * lax.axis_index("y") + lax.axis_index("c")
    x2, tgt = x[0], loss_target[0]
    fng = final_norm_g.reshape(1, D_MODEL)

    n_ada, n_ada_f = w_ada.shape[2], w_ada_final.shape[1]
    cact, mod, mod_f = _ada_exchange(c, w_ada[0], b_ada.reshape(N_DEV, n_ada), w_ada_final,
                                     b_ada_final.reshape(N_DEV, n_ada_f))
    cact = cact.reshape(N_DEV, D_MODEL)
    mod, mod_f = mod.reshape(1, 3 * D_MODEL), mod_f.reshape(1, 2 * D_MODEL)
    shift, scale, gate = mod[:, :D_MODEL], mod[:, D_MODEL:2 * D_MODEL], mod[:, 2 * D_MODEL:]
    shift_f, scale_f = mod_f[:, :D_MODEL], mod_f[:, D_MODEL:]

    wt_f32, m_wt, v_wt = (jnp.swapaxes(a, 1, 2)[0] for a in (w_in, m_w_in, v_w_in))
    xi, yi = lax.axis_index("x"), lax.axis_index("y")
    chip_order = jnp.stack([2 * xi + yi, 2 * (1 - xi) + yi, 2 * xi + 1 - yi, 2 * (1 - xi) + 1 - yi]).astype(jnp.int32)
    wt_mine, wo_mine = _prep_weights(me.reshape(1), wt_f32, w_out[0])

    freqs = _rope_freqs()
    sinks_v = sinks.reshape(16)
    h, proj, wt = _gather_in_proj(chip_order, x2, shift, scale, norm_g, wt_mine)
    y, wo = _mixer_fwd(proj, freqs, ln_v_g, ln_v_b, w_spatial[0], b_spatial[0], sinks_v, wo_mine)
    dx1, do, dy, sums_o, room = _out_proj_loss(y, x2, tgt, wo, gate, shift_f, scale_f, fng)

    chip = (2 * lax.axis_index("x") + lax.axis_index("y")).reshape(1)
    pair_out, _ = _wgrad_pair("wgrad_out", y, do, 2048)
    dproj, d_ln, d_wsp, misc, parts_out = _mixer_bwd(
        me.reshape(1), proj, dy, freqs, ln_v_g, ln_v_b, w_spatial[0], b_spatial[0], sinks_v, pair_out)
    pair_in, (d_ln, d_wsp, misc) = _wgrad_pair(
        "wgrad_in", dproj, h, 1024, gathers=(d_ln, d_wsp.reshape(N_DEV * A_GROUPS * CHUNK, CHUNK), misc))
    grad_x, sums, parts_in = _in_proj_bwd(dproj, wt, x2, dx1, scale, norm_g, sums_o, pair_in, room)
    wt_leaves = [jnp.swapaxes(a[None], 1, 2)
                 for a in _adam_rows("adam_w_in", chip, pair_in, parts_in, wt_f32, m_wt, v_wt, 176)]
    w_out_leaves = [a[None] for a in _adam_rows("adam_w_out", chip, pair_out, parts_out, w_out[0], m_w_out[0], v_w_out[0], 64)]

    (sums,) = _all_gather("gather_sums", [sums], pltpu.VMEM)
    natural = dict(w_spatial=(A_GROUPS * CHUNK, CHUNK), b_spatial=(A_GROUPS, CHUNK), sinks=(1, 16), norm_g=(1, D_MODEL),
                   ln_v_g=(1, D_A), ln_v_b=(1, D_A), final_norm_g=(1, D_MODEL), b_ada=(1, 3 * D_MODEL),
                   b_ada_final=(1, 2 * D_MODEL))
    given = dict(
        w_spatial=(w_spatial, m_w_spatial, v_w_spatial), b_spatial=(b_spatial, m_b_spatial, v_b_spatial),
        sinks=(sinks, m_sinks, v_sinks), norm_g=(norm_g, m_norm_g, v_norm_g), ln_v_g=(ln_v_g, m_ln_v_g, v_ln_v_g),
        ln_v_b=(ln_v_b, m_ln_v_b, v_ln_v_b), final_norm_g=(final_norm_g, m_final_norm_g, v_final_norm_g),
        b_ada=(b_ada, m_b_ada, v_b_ada), b_ada_final=(b_ada_final, m_b_ada_final, v_b_ada_final))
    params = {name: tuple(a.reshape(natural[name]) for a in given[name]) for name in SMALL_PARAMS}
    params["sinks"] = tuple(jnp.pad(a, ((0, 0), (0, 128 - 16))) for a in params["sinks"])
    loss, small = _adam_small(d_wsp.reshape(N_DEV, A_GROUPS * CHUNK, CHUNK), misc.reshape(N_DEV, MISC_ROWS, 128),
                              d_ln.reshape(N_DEV, 8, D_A), sums, params)
    small["sinks"] = [a[:, :16] for a in small["sinks"]]
    small = {name: [a.reshape(given[name][0].shape) for a in small[name]] for name in SMALL_PARAMS}

    dmod_all = jnp.concatenate([sums[:, SUM_SHIFT], sums[:, SUM_SCALE], sums[:, SUM_GATE]], axis=1)
    dmod_f_all = jnp.concatenate([sums[:, SUM_SHIFT_F], sums[:, SUM_SCALE_F]], axis=1)
    dmod_mine = lax.dynamic_slice_in_dim(dmod_all, me * n_ada, n_ada, axis=1)
    dmod_f_mine = lax.dynamic_slice_in_dim(dmod_f_all, me * n_ada_f, n_ada_f, axis=1)
    ada = _adam_ada("adam_w_ada", cact, dmod_mine, w_ada[0], m_w_ada[0], v_w_ada[0])
    ada_f = _adam_ada("adam_w_ada_final", cact, dmod_f_mine, w_ada_final, m_w_ada_final, v_w_ada_final)

    def leaves(k):
        return (ada[k][None], small["b_ada"][k], small["norm_g"][k], wt_leaves[k], small["ln_v_g"][k],
                small["ln_v_b"][k], small["w_spatial"][k], small["b_spatial"][k], small["sinks"][k], w_out_leaves[k],
                ada_f[k], small["b_ada_final"][k], small["final_norm_g"][k])

    return (loss[0, 0], grad_x[None], *leaves(0), *leaves(1), *leaves(2), *leaves(3))
```

```python
import jax
import jax.numpy as jnp
from jax import lax
from jax.experimental import pallas as pl
from jax.experimental.pallas import tpu as pltpu

D_MODEL = 2048
D_IN = 5632
D_A = 1024
CHUNK = 128
A_GROUPS = 8
HEAD_DIM = 64
N_KV_HEADS = 4
N_DEV = 8
ROPE_THETA = 10000.0
NORM_EPS = 1e-5
ATTN_SCALE = HEAD_DIM ** -0.5

ADAM_LR = 0.001
ADAM_B1 = 0.9
ADAM_B2 = 0.999
ADAM_EPS = 1e-08
ADAM_WD = 0.01
ADAM_STEP = 10

OFF_U, OFF_VA, OFF_ZA, OFF_Q, OFF_K, OFF_V, OFF_ZB = 0, 1024, 2048, 3072, 4096, 4352, 4608

SUM_SHIFT, SUM_SCALE, SUM_NORM_G, SUM_GATE, SUM_SHIFT_F, SUM_SCALE_F, SUM_FNG, SUM_SQ_ERR = range(8)

V7X_VMEM_LIMIT_BYTES = 56 * 1024 * 1024

F32 = jnp.float32
BF16 = jnp.bfloat16
MESH = pl.DeviceIdType.MESH
SDS = jax.ShapeDtypeStruct
NT = (((1,), (1,)), ((), ()))
TN = (((0,), (0,)), ((), ()))


def _params(*semantics):
    return pltpu.CompilerParams(dimension_semantics=semantics or None, vmem_limit_bytes=V7X_VMEM_LIMIT_BYTES)


def _mesh_pos():
    return lax.axis_index("x"), lax.axis_index("y"), lax.axis_index("c")


def _sigmoid(z):
    return 1.0 / (1.0 + jnp.exp(-z))


def _adamw(w, g, m, v):
    m = ADAM_B1 * m + (1.0 - ADAM_B1) * g
    v = ADAM_B2 * v + (1.0 - ADAM_B2) * (g * g)
    m_hat = m / (1.0 - ADAM_B1 ** ADAM_STEP)
    v_hat = v / (1.0 - ADAM_B2 ** ADAM_STEP)
    delta = -ADAM_LR * (m_hat / (jnp.sqrt(v_hat) + ADAM_EPS) + ADAM_WD * w)
    return delta, m, v


def _all_gather(name, blocks, memory_space):
    n_arr = len(blocks)

    def body(*refs):
        ins, outs = refs[:n_arr], refs[n_arr:2 * n_arr]
        send_sems, recv_sems, local_sems = refs[2 * n_arr:]
        x, y, c = _mesh_pos()
        me, sibling = (x, y, c), (x, y, 1 - c)
        chips = [(1 - x, y), (x, 1 - y), (1 - x, 1 - y)]

        def slot(p):
            return 4 * p[0] + 2 * p[1] + p[2]

        def copy(a, k, block, to, src=None):
            dst = outs[a].at[slot(block)]
            return pltpu.make_async_remote_copy(
                src_ref=dst if src is None else src, dst_ref=dst,
                send_sem=send_sems.at[a, k], recv_sem=recv_sems.at[a, k],
                device_id=to, device_id_type=MESH)

        mine = [pltpu.make_async_copy(ins[a], outs[a].at[slot(me)], local_sems.at[a]) for a in range(n_arr)]
        for cp in mine:
            cp.start()
        first = []
        for a in range(n_arr):
            first.append(copy(a, 0, me, sibling, src=ins[a]))
            first += [copy(a, 1 + j, me, (*chip, c), src=ins[a]) for j, chip in enumerate(chips)]
        for cp in first:
            cp.start()
        passed = []
        for j, chip in enumerate(chips):
            for a in range(n_arr):
                copy(a, 1 + j, (*chip, c), me).wait_recv()
                fwd = copy(a, 4 + j, (*chip, c), sibling)
                fwd.start()
                passed.append(fwd)
        for a in range(n_arr):
            copy(a, 0, sibling, me).wait_recv()
            for j, chip in enumerate(chips):
                copy(a, 4 + j, (*chip, 1 - c), me).wait_recv()
        for cp in first + passed:
            cp.wait_send()
        for cp in mine:
            cp.wait()

    spec = pl.BlockSpec(memory_space=memory_space)
    return pl.pallas_call(
        body, name=name,
        out_shape=[SDS((N_DEV,) + b.shape, b.dtype) for b in blocks],
        in_specs=[spec] * n_arr, out_specs=[spec] * n_arr,
        scratch_shapes=[pltpu.SemaphoreType.DMA((n_arr, 7)), pltpu.SemaphoreType.DMA((n_arr, 7)),
                        pltpu.SemaphoreType.DMA((n_arr,))],
        compiler_params=_params(),
    )(*blocks)


def _ada_exchange(c, w_ada, b_ada8, w_ada_f, b_ada_f8):
    n1, n2 = w_ada.shape[1], w_ada_f.shape[1]

    def body(c_ref, w1_ref, b1_ref, w2_ref, b2_ref, cact_ref, mod_ref, modf_ref,
             cact_buf, res1, res2, send1, send2, sems_s, sems_r):
        x, y, c_pos = _mesh_pos()
        me = 4 * x + 2 * y + c_pos
        flips = [(k >> 2 & 1, k >> 1 & 1, k & 1) for k in range(1, N_DEV)]

        def peer(f):
            return (1 - x if f[0] else x, 1 - y if f[1] else y, 1 - c_pos if f[2] else c_pos)

        cv = c_ref[...]
        cact = cv * _sigmoid(cv)
        cact_buf[...] = cact
        cact_ref[me] = cact

        def rdma(phase, k, src, dst, f):
            return pltpu.make_async_remote_copy(src_ref=src, dst_ref=dst, send_sem=sems_s.at[phase, k],
                                                recv_sem=sems_r.at[phase, k], device_id=peer(f), device_id_type=MESH)

        gather = [rdma(0, k, cact_buf, cact_ref.at[me], f) for k, f in enumerate(flips)]
        for cp in gather:
            cp.start()
        for cp in gather:
            cp.wait_recv()
        for cp in gather:
            cp.wait_send()

        rid = lax.broadcasted_iota(jnp.int32, (N_DEV, D_MODEL), 0)
        rows = jnp.zeros((N_DEV, D_MODEL), F32)
        for j in range(N_DEV):
            rows = jnp.where(rid == j, jnp.broadcast_to(cact_ref[j], (N_DEV, D_MODEL)), rows)
        rows = rows.astype(BF16)
        res1[...] = jnp.dot(rows, w1_ref[...].astype(BF16), preferred_element_type=F32) + b1_ref[pl.ds(me, 1), :]
        res2[...] = jnp.dot(rows, w2_ref[...].astype(BF16), preferred_element_type=F32) + b2_ref[pl.ds(me, 1), :]
        for j in range(N_DEV):
            send1[j] = res1[pl.ds(j, 1), :]
            send2[j] = res2[pl.ds(j, 1), :]
        mod_ref[me] = send1[me]
        modf_ref[me] = send2[me]
        scatter = []
        for k, f in enumerate(flips):
            to = me ^ (k + 1)
            scatter.append(rdma(1, k, send1.at[to], mod_ref.at[me], f))
            scatter.append(rdma(2, k, send2.at[to], modf_ref.at[me], f))
        for cp in scatter:
            cp.start()
        for cp in scatter:
            cp.wait_recv()
        for cp in scatter:
            cp.wait_send()

    vmem = pl.BlockSpec(memory_space=pltpu.VMEM)
    return pl.pallas_call(
        body, name="ada_exchange",
        out_shape=[SDS((N_DEV, 1, D_MODEL), F32), SDS((N_DEV, 1, n1), F32), SDS((N_DEV, 1, n2), F32)],
        in_specs=[vmem] * 5, out_specs=[vmem] * 3,
        scratch_shapes=[pltpu.VMEM((1, D_MODEL), F32), pltpu.VMEM((N_DEV, n1), F32), pltpu.VMEM((N_DEV, n2), F32),
                        pltpu.VMEM((N_DEV, 1, n1), F32), pltpu.VMEM((N_DEV, 1, n2), F32),
                        pltpu.SemaphoreType.DMA((3, 7)), pltpu.SemaphoreType.DMA((3, 7))],
        compiler_params=_params(),
    )(c, w_ada, b_ada8, w_ada_f, b_ada_f8)


def _chip_scatter(pair_ref, parts_ref, send_sems, recv_sems):
    x, y, c = _mesh_pos()
    chips = [(1 - x, y), (x, 1 - y), (1 - x, 1 - y)]
    return [pltpu.make_async_remote_copy(
        src_ref=pair_ref.at[2 * cx + cy], dst_ref=parts_ref.at[j], send_sem=send_sems.at[j], recv_sem=recv_sems.at[j],
        device_id=(cx, cy, c), device_id_type=MESH) for j, (cx, cy) in enumerate(chips)]


def _scatter_scratch():
    return [pltpu.SemaphoreType.DMA((3,)), pltpu.SemaphoreType.DMA((3,))]


def _prep_weights(me, wt, w_out):
    steps = 4

    def body(me_ref, wt_ref, wo_ref, wtb_ref, wob_ref):
        wtb_ref[...] = wt_ref[...].astype(BF16)
        wob_ref[...] = wo_ref[...].astype(BF16)

    def rows(a, mine):
        blk = (a.shape[0] // steps, a.shape[1])
        return pl.BlockSpec(blk, (lambda i, me_ref: (steps * me_ref[0] + i, 0)) if mine else (lambda i, me_ref: (i, 0)))

    return pl.pallas_call(
        body, name="prep_weights",
        grid_spec=pltpu.PrefetchScalarGridSpec(
            num_scalar_prefetch=1, grid=(steps,),
            in_specs=[rows(wt, False), rows(w_out, False)], out_specs=[rows(wt, True), rows(w_out, True)]),
        out_shape=[SDS((N_DEV * wt.shape[0], D_MODEL), BF16), SDS((N_DEV * w_out.shape[0], D_MODEL), BF16)],
        compiler_params=_params("parallel"),
    )(me, wt, w_out)


class _InPlaceGather:
    def __init__(self, buf_ref, send_sems, recv_sems, relay=False):
        self.buf, self.send_sems, self.recv_sems, self.relay = buf_ref, send_sems, recv_sems, relay
        self.n = buf_ref.shape[0] // N_DEV
        x, y, c = _mesh_pos()
        self.me, self.sibling, self.core = (x, y, c), (x, y, 1 - c), c
        self.chips = [(1 - x, y), (x, 1 - y), (1 - x, 1 - y)]
        self.relay_from = (jnp.where(c == 0, 1 - x, x), jnp.where(c == 0, y, 1 - y), c)
        self.relay_to = (jnp.where(c == 0, x, 1 - x), jnp.where(c == 0, 1 - y, y), c)

    def copy(self, k, block, to):
        start = pl.multiple_of((4 * block[0] + 2 * block[1] + block[2]) * self.n, self.n)
        rows = self.buf.at[pl.ds(start, self.n)]
        return pltpu.make_async_remote_copy(src_ref=rows, dst_ref=rows, send_sem=self.send_sems.at[k],
                                            recv_sem=self.recv_sems.at[k], device_id=to, device_id_type=MESH)

    def start(self):
        self.copy(0, self.me, self.sibling).start()
        for j, chip in enumerate(self.chips[:2] if self.relay else self.chips):
            self.copy(1 + j, self.me, (*chip, self.core)).start()

    def relay_diagonal(self):
        self.copy(3, self.relay_from, self.relay_to).start()

    def pass_on(self, j):
        self.copy(1 + j, (*self.chips[j], self.core), self.me).wait_recv()
        self.copy(4 + j, (*self.chips[j], self.core), self.sibling).start()

    def wait_sibling(self, k):
        self.copy(k, self.sibling, self.me).wait_recv()

    def wait_sends(self):
        for k in range(7):
            self.copy(k, self.me, self.sibling).wait_send()


def _gather_scratch():
    return [pltpu.SemaphoreType.DMA((7,)), pltpu.SemaphoreType.DMA((7,))]


def _gather_in_proj(order, x, shift, scale, norm_g, wt_all):
    s = x.shape[0]
    th, tm = min(512, s), min(1024, s)
    nh, ni = s // th, s // tm
    tn = D_IN // 4
    steps = nh + 4 * ni

    def body(order_ref, x_ref, shift_ref, scale_ref, g_ref, wt_in, h_ref, proj_ref, wt_ref,
             h_scr, w_buf, load_sems, send_sems, recv_sems):
        g = pl.program_id(0)
        gather = _InPlaceGather(wt_ref, send_sems, recv_sems, relay=True)

        def tile_load(slot, chip):
            return pltpu.make_async_copy(wt_ref.at[pl.ds(pl.multiple_of(chip * tn, tn), tn)], w_buf.at[slot],
                                         load_sems.at[slot])

        @pl.when(g == 0)
        def _():
            gather.start()

        @pl.when(g < nh)
        def _():
            xv = x_ref[...]
            r = lax.rsqrt(jnp.mean(xv * xv, axis=-1, keepdims=True) + NORM_EPS)
            hb = (((xv * r) * g_ref[...]) * (1.0 + scale_ref[...]) + shift_ref[...]).astype(BF16)
            h_ref[...] = hb
            h_scr[pl.ds(pl.multiple_of(g * th, th), th), :] = hb

        @pl.when(g == nh - 1)
        def _():
            gather.wait_sibling(0)
            tile_load(0, order_ref[0]).start()

        @pl.when(g >= nh)
        def _():
            t, i = (g - nh) // ni, (g - nh) % ni

            @pl.when(i == 0)
            def _():
                tile_load(t % 2, order_ref[t]).wait()

            @pl.when((i == ni - 1) & (t == 0))
            def _():
                gather.pass_on(0)
                gather.pass_on(1)
                gather.relay_diagonal()

            @pl.when((i == ni // 2) & (t == 2))
            def _():
                gather.pass_on(2)

            for j in range(3):
                @pl.when((i == ni - 1) & (t == j))
                def _():
                    gather.wait_sibling(4 + j)
                    tile_load((j + 1) % 2, order_ref[j + 1]).start()

            lhs = h_scr[pl.ds(pl.multiple_of(i * tm, tm), tm), :]
            proj_ref[...] = lax.dot_general(lhs, w_buf[t % 2], NT, preferred_element_type=F32).astype(BF16)

        @pl.when(g == steps - 1)
        def _():
            gather.wait_sends()

    def h_tile(g, order_ref):
        return (jnp.minimum(g, nh - 1), 0)

    def proj_tile(g, order_ref):
        mm = jnp.maximum(g - nh, 0)
        return (mm % ni, order_ref[mm // ni])

    row = pl.BlockSpec((1, D_MODEL), lambda g, order_ref: (0, 0))
    hbm = pl.BlockSpec(memory_space=pl.ANY)
    return pl.pallas_call(
        body, name="gather_in_proj",
        grid_spec=pltpu.PrefetchScalarGridSpec(
            num_scalar_prefetch=1, grid=(steps,),
            in_specs=[pl.BlockSpec((th, D_MODEL), h_tile), row, row, row, hbm],
            out_specs=[pl.BlockSpec((th, D_MODEL), h_tile), pl.BlockSpec((tm, tn), proj_tile), hbm],
            scratch_shapes=[pltpu.VMEM((s, D_MODEL), BF16), pltpu.VMEM((2, tn, D_MODEL), BF16),
                            pltpu.SemaphoreType.DMA((2,)), *_gather_scratch()]),
        out_shape=[SDS((s, D_MODEL), BF16), SDS((s, D_IN), BF16), SDS(wt_all.shape, BF16)],
        input_output_aliases={5: 2},
        compiler_params=_params("arbitrary"),
    )(order, x, shift, scale, norm_g, wt_all)


def _rope_freqs():
    inv_freq = ROPE_THETA ** (-jnp.arange(0, HEAD_DIM, 2, dtype=F32) / HEAD_DIM)
    return jnp.tile(inv_freq, 4).reshape(1, 128)


class _RopeTables:
    def __init__(self, freq_ref, rows_ref, state_ref, last_ref):
        self.freq, self.rows, self.state, self.last = freq_ref, rows_ref, state_ref, last_ref

    def start(self, block, direction):
        ang = lax.broadcasted_iota(jnp.int32, (CHUNK, 128), 0).astype(F32) * self.freq[...]
        self.rows[0] = jnp.cos(ang)
        self.rows[1] = jnp.sin(ang)
        base = jnp.asarray(block * CHUNK, dtype=F32) * self.freq[...]
        turn = float(direction * CHUNK) * self.freq[...]
        self.state[0:1, :] = jnp.cos(base)
        self.state[1:2, :] = jnp.sin(base)
        self.state[2:3, :] = jnp.cos(turn)
        self.state[3:4, :] = jnp.sin(turn)

    def step(self):
        c, s, ct, st = (self.state[k:k + 1, :] for k in range(4))
        self.state[0:1, :] = c * ct - s * st
        self.state[1:2, :] = s * ct + c * st

    def tables(self):
        c, s = self.state[0:1, :], self.state[1:2, :]
        cos = c * self.rows[0] - s * self.rows[1]
        sin = s * self.rows[0] + c * self.rows[1]
        first_half = (lax.broadcasted_iota(jnp.int32, (1, 128), 1) & (HEAD_DIM - 1)) < HEAD_DIM // 2
        return cos, jnp.where(first_half, -sin, 0.0), jnp.where(first_half, 0.0, sin)

    def keep(self, tabs):
        for k in range(3):
            self.last[k] = tabs[k]

    def kept(self):
        return tuple(self.last[k] for k in range(3))


def _rope_scratch():
    return [pltpu.VMEM((2, CHUNK, 128), F32), pltpu.VMEM((8, 128), F32), pltpu.VMEM((3, CHUNK, 128), F32)]


def _rope(v, cos, sin_lo, sin_hi):
    width = v.shape[1]
    rep = (1, width // 128)
    return (v * jnp.tile(cos, rep) + pltpu.roll(v, width - 32, 1) * jnp.tile(sin_lo, rep)
            + pltpu.roll(v, 32, 1) * jnp.tile(sin_hi, rep))


def _rope_bwd(d, cos, sin_lo, sin_hi):
    width = d.shape[1]
    rep = (1, width // 128)
    return (d * jnp.tile(cos, rep) + pltpu.roll(d * jnp.tile(sin_lo, rep), 32, 1)
            + pltpu.roll(d * jnp.tile(sin_hi, rep), width - 32, 1))


def _layer_norm(v, g, b):
    mu = jnp.mean(v, axis=-1, keepdims=True)
    vc = v - mu
    rstd = lax.rsqrt(jnp.mean(vc * vc, axis=-1, keepdims=True) + NORM_EPS)
    vhat = vc * rstd
    return vhat * g + b, vhat, rstd


def _set_tril(w_ref, out_ref):
    t = lax.broadcasted_iota(jnp.int32, (CHUNK, CHUNK), 0)
    tp = lax.broadcasted_iota(jnp.int32, (CHUNK, CHUNK), 1)
    for g in range(A_GROUPS):
        out_ref[g] = jnp.where(tp <= t, w_ref[g], 0.0).astype(BF16)


def _bias_columns(b_ref, out_ref):
    for g in range(A_GROUPS):
        out_ref[g] = jnp.broadcast_to(b_ref[pl.ds(g, 1), :], (CHUNK, CHUNK)).T


def _from_prev():
    r = lax.broadcasted_iota(jnp.int32, (CHUNK, 4 * CHUNK), 0)
    i = lax.broadcasted_iota(jnp.int32, (CHUNK, 4 * CHUNK), 1) & (CHUNK - 1)
    return r > i


def _set_unfold_masks(mask_ref):
    prev = _from_prev()
    mask_ref[0] = jnp.where(prev, 1.0, 0.0).astype(BF16)
    mask_ref[1] = jnp.where(prev, 0.0, 1.0).astype(BF16)


def _fold_band(t, from_prev):
    return jnp.where(from_prev, t[:CHUNK], t[CHUNK:])


def _unfold_band(t, mask_ref):
    return jnp.concatenate([t * mask_ref[0], t * mask_ref[1]], axis=0)


def _low_lanes():
    return lax.broadcasted_iota(jnp.int32, (1, 128), 1) < HEAD_DIM


def _stack_heads(pair_a, pair_b):
    lo = _low_lanes()
    return jnp.concatenate([jnp.where(lo, pair_a, 0.0), jnp.where(lo, 0.0, pair_a),
                            jnp.where(lo, pair_b, 0.0), jnp.where(lo, 0.0, pair_b)], axis=0).astype(BF16)


def _heads_to_lanes(per_group):
    rows = [t[:, r * CHUNK:(r + 1) * CHUNK] for t in per_group for r in range(4)]
    return jnp.concatenate(rows, axis=0).T


def _dup_kv_head(band, gk):
    pair = band[:, (gk // 2) * 128:(gk // 2 + 1) * 128]
    lo = _low_lanes()
    one = jnp.where(lo if gk % 2 == 0 else jnp.logical_not(lo), pair, 0.0)
    return (one + pltpu.roll(one, HEAD_DIM, 1)).astype(BF16)


def _fold_kv_head(dup_grad, gk):
    both = dup_grad + pltpu.roll(dup_grad, HEAD_DIM, 1)
    lo = _low_lanes()
    return jnp.where(lo if gk % 2 == 0 else jnp.logical_not(lo), both, 0.0)


def _attn_probs(q_st, k_dup, sink_row, from_prev, first_block):
    s = lax.dot_general(k_dup, q_st, NT, preferred_element_type=F32)
    no_prev = jnp.where(first_block, -jnp.inf, 0.0)
    s = jnp.where(from_prev, s[:CHUNK] + no_prev, s[CHUNK:])
    m = jnp.maximum(jnp.max(s, axis=0, keepdims=True), sink_row)
    p = jnp.exp(s - m)
    e_sink = jnp.exp(sink_row - m)
    inv = 1.0 / (jnp.sum(p, axis=0, keepdims=True) + e_sink)
    return p * inv, e_sink * inv


def _sink_row(sinks_ref, gk):
    return jnp.concatenate([jnp.full((1, CHUNK), sinks_ref[4 * gk + r], F32) for r in range(4)], axis=1)


def _mixer_specs(nb, rev):
    def blk(i):
        return nb - 1 - i if rev else i

    def prev(i):
        return jnp.maximum(blk(i) - 1, 0)

    return dict(
        cur=pl.BlockSpec((CHUNK, D_IN), lambda i, *_: (blk(i), 0)),
        prev_kv=pl.BlockSpec((CHUNK, 2 * 256), lambda i, *_: (prev(i), OFF_K // 512)),
        freq=pl.BlockSpec((1, 128), lambda i, *_: (0, 0)),
        vec=pl.BlockSpec((1, D_A), lambda i, *_: (0, 0)),
        wsp=pl.BlockSpec((A_GROUPS, CHUNK, CHUNK), lambda i, *_: (0, 0, 0)),
        bsp=pl.BlockSpec((A_GROUPS, CHUNK), lambda i, *_: (0, 0)),
        smem=pl.BlockSpec(memory_space=pltpu.SMEM),
        blk=blk,
    )


def _mixer_fwd(proj, freqs, ln_g, ln_b, w_sp, b_sp, sinks, wo_all):
    s = proj.shape[0]
    nb = s // CHUNK
    sp = _mixer_specs(nb, rev=False)

    def body(cur_ref, pkv_ref, freq_ref, lg_ref, lb_ref, w_ref, b_ref, sinks_ref, wo_in, y_ref, wo_ref,
             bcol, wtril, mask, rope_rows, rope_state, rope_last, send_sems, recv_sems):
        i = pl.program_id(0)
        gather = _InPlaceGather(wo_ref, send_sems, recv_sems)
        rope = _RopeTables(freq_ref, rope_rows, rope_state, rope_last)

        @pl.when(i == 0)
        def _():
            gather.start()
            _bias_columns(b_ref, bcol)
            _set_tril(w_ref, wtril)
            _set_unfold_masks(mask)
            rope.start(-1, 1)
            rope_last[...] = jnp.zeros_like(rope_last)

        @pl.when(i == (7 * nb) // 8)
        def _():
            for j in range(3):
                gather.pass_on(j)

        vln, _, _ = _layer_norm(cur_ref[:, OFF_VA:OFF_ZA].astype(F32), lg_ref[...], lb_ref[...])
        vln = vln.astype(BF16)

        def gating_group(g):
            cols = slice(g * 128, (g + 1) * 128)
            sg = jnp.dot(wtril[g], vln[:, cols], preferred_element_type=F32) + bcol[g]
            u = cur_ref[:, OFF_U + g * 128:OFF_U + (g + 1) * 128].astype(F32)
            z = cur_ref[:, OFF_ZA + g * 128:OFF_ZA + (g + 1) * 128].astype(F32)
            y_ref[:, cols] = (u * sg * (z * _sigmoid(z))).astype(BF16)

        rope.step()
        cur_t, prev_t = rope.tables(), rope.kept()
        rope.keep(cur_t)
        qr = _rope(cur_ref[:, OFF_Q:OFF_K].astype(F32), *cur_t) * ATTN_SCALE
        kr = jnp.concatenate([_rope(pkv_ref[:, 0:256].astype(F32), *prev_t),
                              _rope(cur_ref[:, OFF_K:OFF_V].astype(F32), *cur_t)], axis=0)
        v_t = jnp.concatenate([pkv_ref[:, 256:512], cur_ref[:, OFF_V:OFF_ZB]], axis=0).astype(F32).T.astype(BF16)
        outs = []
        from_prev = _from_prev()
        for gk in range(N_KV_HEADS):
            q_st = _stack_heads(qr[:, (2 * gk) * 128:(2 * gk + 1) * 128], qr[:, (2 * gk + 1) * 128:(2 * gk + 2) * 128])
            probs, _ = _attn_probs(q_st, _dup_kv_head(kr, gk), _sink_row(sinks_ref, gk), from_prev, i == 0)
            gating_group(2 * gk)
            outs.append(jnp.dot(v_t[gk * HEAD_DIM:(gk + 1) * HEAD_DIM], _unfold_band(probs.astype(BF16), mask),
                                preferred_element_type=F32))
            gating_group(2 * gk + 1)
        zb = cur_ref[:, OFF_ZB:D_IN].astype(F32)
        y_ref[:, D_A:D_MODEL] = (_heads_to_lanes(outs) * (zb * _sigmoid(zb))).astype(BF16)

        @pl.when(i == nb - 1)
        def _():
            gather.wait_sibling(0)
            for j in range(3):
                gather.wait_sibling(4 + j)
            gather.wait_sends()

    hbm = pl.BlockSpec(memory_space=pl.ANY)
    return pl.pallas_call(
        body, name="mixer_fwd", grid=(nb,),
        in_specs=[sp["cur"], sp["prev_kv"], sp["freq"], sp["vec"], sp["vec"], sp["wsp"], sp["bsp"], sp["smem"], hbm],
        out_specs=[pl.BlockSpec((CHUNK, D_MODEL), lambda i: (i, 0)), hbm],
        out_shape=[SDS((s, D_MODEL), BF16), SDS(wo_all.shape, wo_all.dtype)],
        scratch_shapes=[pltpu.VMEM((A_GROUPS, CHUNK, CHUNK), F32), pltpu.VMEM((A_GROUPS, CHUNK, CHUNK), BF16),
                        pltpu.VMEM((2, CHUNK, 4 * CHUNK), BF16), *_rope_scratch(), *_gather_scratch()],
        input_output_aliases={8: 1},
        compiler_params=_params("arbitrary"),
    )(proj, proj, freqs, ln_g, ln_b, w_sp, b_sp, sinks, wo_all)


def _out_proj_loss(y, x, target, wo, gate, shift_f, scale_f, fng):
    s = y.shape[0]
    tm, tr = 256, 128
    nt = s // tm

    def body(y_ref, x_ref, t_ref, wo_ref, gate_ref, sh_ref, sc_ref, g_ref, dx1_ref, do_ref, dy_ref, sums_ref,
             do_last, do_work):
        i = pl.program_id(0)

        @pl.when(i == 0)
        def _():
            sums_ref[...] = jnp.zeros_like(sums_ref)
            do_last[...] = jnp.zeros_like(do_last)

        do_work[...] = do_last[...]
        o = jnp.dot(y_ref[...], wo_ref[...], preferred_element_type=F32)
        gate, g, sh = gate_ref[...], g_ref[...], sh_ref[...]
        one_sc = 1.0 + sc_ref[...]
        cs, inv_d = g * one_sc, 1.0 / D_MODEL

        def rowsum(v):
            return jnp.sum(v, axis=0, keepdims=True)

        sums = [jnp.zeros((1, D_MODEL), F32) for _ in range(4)]
        for c in range(tm // tr):
            rows = slice(c * tr, (c + 1) * tr)
            oc = o[rows]
            x1 = x_ref[rows, :] + gate * oc
            r = lax.rsqrt(jnp.sum(x1 * x1, axis=-1, keepdims=True) * inv_d + NORM_EPS)
            x1n = x1 * r
            diff = x1n * cs + sh - t_ref[rows, :]
            w = diff * x1n
            lane_sum = jnp.sum(w * cs, axis=-1, keepdims=True)
            dx1 = (diff * cs) * (r * inv_d) - x1n * (r * lane_sum * (inv_d * inv_d))
            dx1_ref[rows, :] = dx1
            do = (dx1 * gate).astype(BF16)
            do_ref[rows, :] = do
            do_last[rows, :] = do
            for k, v in enumerate((dx1 * oc, diff, w, diff * diff)):
                sums[k] = sums[k] + rowsum(v)
        live = jnp.where(i < nt, 1.0, 0.0)
        for row, v in ((SUM_GATE, sums[0]), (SUM_SHIFT_F, inv_d * sums[1]), (SUM_SCALE_F, inv_d * (sums[2] * g)),
                       (SUM_FNG, inv_d * (sums[2] * one_sc)), (SUM_SQ_ERR, sums[3])):
            sums_ref[row:row + 1, :] += live * v
        dy_ref[...] = lax.dot_general(do_work[...], wo_ref[...], NT, preferred_element_type=F32).astype(BF16)

    tile = pl.BlockSpec((tm, D_MODEL), lambda i: (jnp.minimum(i, nt - 1), 0))
    row = pl.BlockSpec((1, D_MODEL), lambda i: (0, 0))
    return pl.pallas_call(
        body, name="out_proj_loss", grid=(nt + 1,),
        in_specs=[tile, tile, tile, pl.BlockSpec((D_MODEL, D_MODEL), lambda i: (0, 0)), row, row, row, row],
        out_specs=[tile, tile, pl.BlockSpec((tm, D_MODEL), lambda i: (jnp.maximum(i - 1, 0), 0)),
                   pl.BlockSpec((8, D_MODEL), lambda i: (0, 0))],
        out_shape=[SDS((s, D_MODEL), F32), SDS((s, D_MODEL), BF16), SDS((s, D_MODEL), BF16), SDS((8, D_MODEL), F32)],
        scratch_shapes=[pltpu.VMEM((tm, D_MODEL), BF16), pltpu.VMEM((tm, D_MODEL), BF16)],
        compiler_params=_params("arbitrary"),
    )(y, x, target, wo, gate, shift_f, scale_f, fng)


ROW_DBSP, ROW_DSINKS, MISC_ROWS = 0, 8, 32


def _mixer_bwd(me, proj, dy, freqs, ln_g, ln_b, w_sp, b_sp, sinks, pair):
    s = proj.shape[0]
    nb = s // CHUNK
    sp = _mixer_specs(nb, rev=True)

    def body(me_ref, cur_ref, pkv_ref, dy_ref, freq_ref, lg_ref, lb_ref, w_ref, b_ref, sinks_ref, pair_ref,
             dproj_ref, dln_ref, dw_ref, misc_ref, parts_ref, bcol, wtril, dbcol, carry, mask, rope_rows, rope_state,
             rope_last, send_sems, recv_sems):
        i = pl.program_id(0)
        block = nb - 1 - i
        rope = _RopeTables(freq_ref, rope_rows, rope_state, rope_last)

        @pl.when(i == 0)
        def _():
            for cp in _chip_scatter(pair_ref, parts_ref, send_sems, recv_sems):
                cp.start()
            _bias_columns(b_ref, bcol)
            _set_tril(w_ref, wtril)
            _set_unfold_masks(mask)
            rope.start(nb - 1, -1)
            rope.keep(rope.tables())
            dbcol[...] = jnp.zeros_like(dbcol)
            carry[...] = jnp.zeros_like(carry)
            dln_ref[...] = jnp.zeros_like(dln_ref)
            dw_ref[...] = jnp.zeros_like(dw_ref)
            misc_ref[...] = jnp.zeros_like(misc_ref)

        vln, vhat, rstd = _layer_norm(cur_ref[:, OFF_VA:OFF_ZA].astype(F32), lg_ref[...], lb_ref[...])
        vln = vln.astype(BF16)
        d_vln = []

        def gating_group(g):
            cols = slice(g * 128, (g + 1) * 128)
            w_g = wtril[g]
            sg = jnp.dot(w_g, vln[:, cols], preferred_element_type=F32) + bcol[g]
            u = cur_ref[:, OFF_U + g * 128:OFF_U + (g + 1) * 128].astype(F32)
            z = cur_ref[:, OFF_ZA + g * 128:OFF_ZA + (g + 1) * 128].astype(F32)
            dya = dy_ref[:, cols].astype(F32)
            sig = _sigmoid(z)
            d_ya = dya * (z * sig)
            dproj_ref[:, OFF_ZA + g * 128:OFF_ZA + (g + 1) * 128] = (
                dya * (u * sg) * (sig * (1.0 + z * (1.0 - sig)))).astype(BF16)
            dproj_ref[:, OFF_U + g * 128:OFF_U + (g + 1) * 128] = (d_ya * sg).astype(BF16)
            d_s = d_ya * u
            dbcol[g] += d_s
            d_sb = d_s.astype(BF16)
            dw_ref[g] += lax.dot_general(d_sb, vln[:, cols], NT, preferred_element_type=F32)
            d_vln.append(lax.dot_general(w_g, d_sb, TN, preferred_element_type=F32))

        cur_t = rope.kept()
        rope.step()
        prev_t = rope.tables()
        rope.keep(prev_t)
        band_t = tuple(jnp.concatenate([p, c], axis=0) for p, c in zip(prev_t, cur_t))
        qr = _rope(cur_ref[:, OFF_Q:OFF_K].astype(F32), *cur_t) * ATTN_SCALE
        kr = jnp.concatenate([_rope(pkv_ref[:, 0:256].astype(F32), *prev_t),
                              _rope(cur_ref[:, OFF_K:OFF_V].astype(F32), *cur_t)], axis=0)
        vb = jnp.concatenate([pkv_ref[:, 256:512], cur_ref[:, OFF_V:OFF_ZB]], axis=0).astype(F32)
        k_t, v_t = (kr.T * ATTN_SCALE).astype(BF16), vb.T.astype(BF16)
        zb = cur_ref[:, OFF_ZB:D_IN].astype(F32)
        dyb = dy_ref[:, D_A:D_MODEL].astype(F32)
        sig = _sigmoid(zb)
        d_yb = dyb * (zb * sig)
        outs, dqs = [], []
        dk_pairs = [jnp.zeros((2 * CHUNK, 128), F32) for _ in range(2)]
        dv_pairs = [jnp.zeros((2 * CHUNK, 128), F32) for _ in range(2)]
        from_prev = _from_prev()
        for gk in range(N_KV_HEADS):
            heads = slice(gk * HEAD_DIM, (gk + 1) * HEAD_DIM)
            q_st = _stack_heads(qr[:, (2 * gk) * 128:(2 * gk + 1) * 128], qr[:, (2 * gk + 1) * 128:(2 * gk + 2) * 128])
            k_dup, v_dup = _dup_kv_head(kr, gk), _dup_kv_head(vb, gk)
            probs, p_sink = _attn_probs(q_st, k_dup, _sink_row(sinks_ref, gk), from_prev, block == 0)
            probs_b = _unfold_band(probs.astype(BF16), mask)
            outs.append(jnp.dot(v_t[heads], probs_b, preferred_element_type=F32))
            do_st = _stack_heads(d_yb[:, (2 * gk) * 128:(2 * gk + 1) * 128], d_yb[:, (2 * gk + 1) * 128:(2 * gk + 2) * 128])
            dp = _fold_band(lax.dot_general(v_dup, do_st, NT, preferred_element_type=F32), from_prev)
            delta = jnp.sum(probs * dp, axis=0, keepdims=True)
            ds = _unfold_band((probs * (dp - delta)).astype(BF16), mask)
            gating_group(2 * gk)
            d_sink = -p_sink * delta
            for r in range(4):
                row = ROW_DSINKS + 4 * gk + r
                misc_ref[row:row + 1, :] += jnp.broadcast_to(
                    jnp.sum(d_sink[:, r * CHUNK:(r + 1) * CHUNK], axis=1, keepdims=True), (1, 128))
            dqs.append(jnp.dot(k_t[heads], ds, preferred_element_type=F32))
            dk_pairs[gk // 2] += _fold_kv_head(jnp.dot(ds, q_st, preferred_element_type=F32), gk)
            dv_pairs[gk // 2] += _fold_kv_head(jnp.dot(probs_b, do_st, preferred_element_type=F32), gk)
            gating_group(2 * gk + 1)
        d_vln = jnp.concatenate(d_vln, axis=1)
        dln_ref[0:1, :] += jnp.sum(d_vln * vhat, axis=0, keepdims=True)
        dln_ref[1:2, :] += jnp.sum(d_vln, axis=0, keepdims=True)
        d_vhat = d_vln * lg_ref[...]
        d_va = rstd * (d_vhat - jnp.mean(d_vhat, axis=-1, keepdims=True)
                       - vhat * jnp.mean(d_vhat * vhat, axis=-1, keepdims=True))
        dproj_ref[:, OFF_VA:OFF_ZA] = d_va.astype(BF16)
        dproj_ref[:, OFF_ZB:D_IN] = (dyb * _heads_to_lanes(outs) * (sig * (1.0 + zb * (1.0 - sig)))).astype(BF16)
        dproj_ref[:, OFF_Q:OFF_K] = _rope_bwd(_heads_to_lanes(dqs), *cur_t).astype(BF16)
        dk_band = _rope_bwd(jnp.concatenate(dk_pairs, axis=1), *band_t)
        dv_band = jnp.concatenate(dv_pairs, axis=1)
        dproj_ref[:, OFF_K:OFF_V] = (dk_band[CHUNK:] + carry[:, 0:256]).astype(BF16)
        dproj_ref[:, OFF_V:OFF_ZB] = (dv_band[CHUNK:] + carry[:, 256:512]).astype(BF16)
        carry[:, 0:256] = dk_band[:CHUNK]
        carry[:, 256:512] = dv_band[:CHUNK]

        @pl.when(i == nb - 1)
        def _():
            t = lax.broadcasted_iota(jnp.int32, (CHUNK, CHUNK), 0)
            tp = lax.broadcasted_iota(jnp.int32, (CHUNK, CHUNK), 1)
            for g in range(A_GROUPS):
                dw_ref[g] = jnp.where(tp <= t, dw_ref[g], 0.0)
                misc_ref[pl.ds(ROW_DBSP + g, 1), :] = jnp.sum(dbcol[g].T, axis=0, keepdims=True)
            scatter = _chip_scatter(pair_ref, parts_ref, send_sems, recv_sems)
            for cp in scatter:
                cp.wait_recv()
            for cp in scatter:
                cp.wait_send()

    blk = sp["blk"]
    hbm = pl.BlockSpec(memory_space=pl.ANY)
    return pl.pallas_call(
        body, name="mixer_bwd",
        grid_spec=pltpu.PrefetchScalarGridSpec(
            num_scalar_prefetch=1, grid=(nb,),
            in_specs=[sp["cur"], sp["prev_kv"], pl.BlockSpec((CHUNK, D_MODEL), lambda i, me_ref: (blk(i), 0)),
                      sp["freq"], sp["vec"], sp["vec"], sp["wsp"], sp["bsp"], sp["smem"], hbm],
            out_specs=[pl.BlockSpec((CHUNK, D_IN), lambda i, me_ref: (blk(i), 0)),
                       pl.BlockSpec((8, D_A), lambda i, me_ref: (me_ref[0], 0)),
                       pl.BlockSpec((A_GROUPS, CHUNK, CHUNK), lambda i, me_ref: (me_ref[0], 0, 0)),
                       pl.BlockSpec((MISC_ROWS, 128), lambda i, me_ref: (me_ref[0], 0)), hbm],
            scratch_shapes=[pltpu.VMEM((A_GROUPS, CHUNK, CHUNK), F32), pltpu.VMEM((A_GROUPS, CHUNK, CHUNK), BF16),
                            pltpu.VMEM((A_GROUPS, CHUNK, CHUNK), F32), pltpu.VMEM((CHUNK, 512), F32), pltpu.VMEM((2, CHUNK, 4 * CHUNK), BF16),
                            *_rope_scratch(), *_scatter_scratch()]),
        out_shape=[SDS((s, D_IN), BF16), SDS((N_DEV * 8, D_A), F32), SDS((N_DEV * A_GROUPS, CHUNK, CHUNK), F32),
                   SDS((N_DEV * MISC_ROWS, 128), F32), SDS((3,) + pair.shape[1:], pair.dtype)],
        compiler_params=_params("arbitrary"),
    )(me, proj, proj, dy, freqs, ln_g, ln_b, w_sp, b_sp, sinks, pair)


def _wgrad_pair(name, a, b, bt, gathers=()):
    s, m = a.shape
    n = b.shape[1]
    bm, half = m // 4, m // 8
    bt = min(bt, s)
    steps = s // bt
    last = 4 * steps
    n_g = len(gathers)

    def body(*refs):
        a_ref, b_ref = refs[:2]
        out_ref, bufs = refs[2 + n_g], refs[3 + n_g:3 + 2 * n_g]
        acc, kept, got, sent, send_sems, recv_sems = refs[3 + 2 * n_g:9 + 2 * n_g]
        sems = refs[9 + 2 * n_g:]
        g = pl.program_id(0)
        tile, t = g // steps, g % steps
        mx, my, mc = _mesh_pos()
        jobs = [_InPlaceGather(bufs[k], sems[2 * k], sems[2 * k + 1]) for k in range(n_g)]

        def exchange(q):
            return pltpu.make_async_remote_copy(src_ref=sent, dst_ref=got.at[q % 2], send_sem=send_sems.at[q],
                                                recv_sem=recv_sems.at[q], device_id=(mx, my, 1 - mc),
                                                device_id_type=MESH)

        @pl.when(g == 0)
        def _():
            for job in jobs:
                job.start()

        @pl.when(g == 2 * steps)
        def _():
            for job in jobs:
                for j in range(3):
                    job.pass_on(j)

        @pl.when(g < last)
        def _():
            @pl.when(t == 0)
            def _():
                acc[...] = lax.dot_general(a_ref[...], b_ref[...], TN, preferred_element_type=F32)

            @pl.when(t > 0)
            def _():
                acc[...] += lax.dot_general(a_ref[...], b_ref[...], TN, preferred_element_type=F32)

            @pl.when(t == steps - 1)
            def _():
                @pl.when(tile > 0)
                def _():
                    exchange(tile - 1).wait_send()

                kept[tile % 2] = acc[pl.ds(pl.multiple_of(mc * half, 8), half), :].astype(BF16)
                sent[...] = acc[pl.ds(pl.multiple_of((1 - mc) * half, 8), half), :].astype(BF16)
                exchange(tile).start()

        @pl.when((t == 0) & (g > 0))
        def _():
            q = tile - 1
            exchange(q).wait_recv()
            out_ref[0] = (kept[q % 2].astype(F32) + got[q % 2].astype(F32)).astype(BF16)

        @pl.when(g == last)
        def _():
            exchange(3).wait_send()
            for job in jobs:
                job.wait_sibling(0)
                for j in range(3):
                    job.wait_sibling(4 + j)
                job.wait_sends()

    def a_tile(g):
        gg = jnp.minimum(g, last - 1)
        return (gg % steps, gg // steps)

    def b_tile(g):
        return (jnp.minimum(g, last - 1) % steps, 0)

    hbm = pl.BlockSpec(memory_space=pl.ANY)
    outs = pl.pallas_call(
        body, name=name, grid=(last + 1,),
        in_specs=[pl.BlockSpec((bt, bm), a_tile), pl.BlockSpec((bt, n), b_tile)] + [hbm] * n_g,
        out_specs=[pl.BlockSpec((1, half, n), lambda g: (jnp.maximum(g - 1, 0) // steps, 0, 0))] + [hbm] * n_g,
        out_shape=[SDS((4, half, n), BF16)] + [SDS(gb.shape, gb.dtype) for gb in gathers],
        scratch_shapes=[pltpu.VMEM((bm, n), F32), pltpu.VMEM((2, half, n), BF16), pltpu.VMEM((2, half, n), BF16),
                        pltpu.VMEM((half, n), BF16), pltpu.SemaphoreType.DMA((4,)), pltpu.SemaphoreType.DMA((4,))]
        + _gather_scratch() * n_g,
        input_output_aliases={2 + k: 1 + k for k in range(n_g)},
        compiler_params=_params("arbitrary"),
    )(a, b, *gathers)
    return outs[0], outs[1:]


def _in_proj_bwd(dproj, wt, x, dx1, scale, norm_g, sums_o, pair):
    s = x.shape[0]
    tm, tk, tr = min(1024, s), D_IN // 4, 64
    ksteps = D_IN // tk

    def body(dp_ref, wt_ref, x_hbm, dx1_hbm, sc_ref, g_ref, so_ref, pair_ref, gx_ref, sums_ref, parts_ref, x_buf,
             dx1_buf, tile_sems, send_sems, recv_sems):
        i, k = pl.program_id(0), pl.program_id(1)

        def tile_copies():
            rows = pl.ds(pl.multiple_of(i * tm, tm), tm)
            return (pltpu.make_async_copy(x_hbm.at[rows], x_buf, tile_sems.at[0]),
                    pltpu.make_async_copy(dx1_hbm.at[rows], dx1_buf, tile_sems.at[1]))

        @pl.when((i == 0) & (k == 0))
        def _():
            for cp in _chip_scatter(pair_ref, parts_ref, send_sems, recv_sems):
                cp.start()
            sums_ref[...] = so_ref[...]

        @pl.when(k == 0)
        def _():
            for cp in tile_copies():
                cp.start()
            gx_ref[...] = jnp.dot(dp_ref[...], wt_ref[...], preferred_element_type=F32)

        @pl.when(k > 0)
        def _():
            gx_ref[...] += jnp.dot(dp_ref[...], wt_ref[...], preferred_element_type=F32)

        @pl.when(k == ksteps - 1)
        def _():
            for cp in tile_copies():
                cp.wait()
            one_sc, g = 1.0 + sc_ref[...], g_ref[...]
            cs = one_sc * g

            def chunk(j, sums):
                rows = pl.ds(pl.multiple_of(j * tr, tr), tr)
                dh, xv = gx_ref[rows, :], x_buf[rows, :]
                dhx = dh * xv
                r = lax.rsqrt(jnp.sum(xv * xv, axis=-1, keepdims=True) * (1.0 / D_MODEL) + NORM_EPS)
                coef = (r * r * r) * (jnp.sum(dhx * cs, axis=-1, keepdims=True) * (1.0 / D_MODEL))
                gx_ref[rows, :] = dx1_buf[rows, :] + r * (dh * cs) - xv * coef
                return (sums[0] + jnp.sum(dh, axis=0, keepdims=True), sums[1] + jnp.sum(dhx * r, axis=0, keepdims=True))

            zero = jnp.zeros((1, D_MODEL), F32)
            sums = lax.fori_loop(0, tm // tr, chunk, (zero, zero))
            sums_ref[SUM_SHIFT:SUM_SHIFT + 1, :] += sums[0]
            sums_ref[SUM_SCALE:SUM_SCALE + 1, :] += sums[1] * g
            sums_ref[SUM_NORM_G:SUM_NORM_G + 1, :] += sums[1] * one_sc

        @pl.when((i == s // tm - 1) & (k == ksteps - 1))
        def _():
            scatter = _chip_scatter(pair_ref, parts_ref, send_sems, recv_sems)
            for cp in scatter:
                cp.wait_recv()
            for cp in scatter:
                cp.wait_send()

    row = pl.BlockSpec((1, D_MODEL), lambda i, k: (0, 0))
    hbm = pl.BlockSpec(memory_space=pl.ANY)
    return pl.pallas_call(
        body, name="in_proj_bwd", grid=(s // tm, ksteps),
        in_specs=[pl.BlockSpec((tm, tk), lambda i, k: (i, k)), pl.BlockSpec((tk, D_MODEL), lambda i, k: (k, 0)),
                  hbm, hbm, row, row, pl.BlockSpec((8, D_MODEL), lambda i, k: (0, 0)), hbm],
        out_specs=[pl.BlockSpec((tm, D_MODEL), lambda i, k: (i, 0)), pl.BlockSpec((8, D_MODEL), lambda i, k: (0, 0)),
                   hbm],
        out_shape=[SDS((s, D_MODEL), F32), SDS((8, D_MODEL), F32), SDS((3,) + pair.shape[1:], pair.dtype)],
        scratch_shapes=[pltpu.VMEM((tm, D_MODEL), F32), pltpu.VMEM((tm, D_MODEL), F32),
                        pltpu.SemaphoreType.DMA((2,)), *_scatter_scratch()],
        compiler_params=_params("arbitrary", "arbitrary"),
    )(dproj, wt, x, dx1, scale, norm_g, sums_o, pair)


def _sum_chips(own_ref, parts_ref):
    return ((own_ref[0].astype(F32) + parts_ref[0].astype(F32)) + parts_ref[1].astype(F32)) + parts_ref[2].astype(F32)


def _adam_rows(name, chip, pair, parts, w, m, v, tr):
    rows = w.shape[0]

    def body(chip_ref, own_ref, p_ref, w_ref, m_ref, v_ref, g_ref, d_ref, nm_ref, nv_ref):
        g = _sum_chips(own_ref, p_ref)
        g_ref[...] = g
        d_ref[...], nm_ref[...], nv_ref[...] = _adamw(w_ref[...], g, m_ref[...], v_ref[...])

    blk = pl.BlockSpec((tr, D_MODEL), lambda j, chip_ref: (j, 0))
    return pl.pallas_call(
        body, name=name,
        grid_spec=pltpu.PrefetchScalarGridSpec(
            num_scalar_prefetch=1, grid=(rows // tr,),
            in_specs=[pl.BlockSpec((1, tr, D_MODEL), lambda j, chip_ref: (chip_ref[0], j, 0)),
                      pl.BlockSpec((3, tr, D_MODEL), lambda j, chip_ref: (0, j, 0)), blk, blk, blk],
            out_specs=[blk] * 4),
        out_shape=[SDS(w.shape, F32)] * 4, compiler_params=_params("parallel"),
    )(chip, pair, parts, w, m, v)


def _adam_ada(name, cact, dmod, w, m, v):
    n = w.shape[1]
    tr = 512

    def body(c_ref, dm_ref, w_ref, m_ref, v_ref, g_ref, d_ref, nm_ref, nv_ref):
        pad_c = jnp.concatenate([c_ref[...], jnp.zeros_like(c_ref)], axis=0).astype(BF16)
        pad_d = jnp.concatenate([dm_ref[...], jnp.zeros_like(dm_ref)], axis=0).astype(BF16)
        g = lax.dot_general(pad_c, pad_d, TN, preferred_element_type=F32)
        g_ref[...] = g
        d_ref[...], nm_ref[...], nv_ref[...] = _adamw(w_ref[...], g, m_ref[...], v_ref[...])

    blk = pl.BlockSpec((tr, n), lambda j: (j, 0))
    return pl.pallas_call(
        body, name=name, grid=(D_MODEL // tr,),
        in_specs=[pl.BlockSpec((N_DEV, tr), lambda j: (0, j)), pl.BlockSpec((N_DEV, n), lambda j: (0, 0)),
                  blk, blk, blk],
        out_specs=[blk] * 4, out_shape=[SDS(w.shape, F32)] * 4,
        compiler_params=_params("parallel"),
    )(cact, dmod, w, m, v)


SMALL_PARAMS = ("w_spatial", "b_spatial", "sinks", "norm_g", "ln_v_g", "ln_v_b", "final_norm_g", "b_ada", "b_ada_final")


def _adam_small(d_wsp, misc, d_ln, sums, params):
    n_p = len(SMALL_PARAMS)

    def body(*refs):
        wsp_ref, misc_ref, ln_ref, sums_ref = refs[:4]
        wmv = [refs[4 + 3 * k:7 + 3 * k] for k in range(n_p)]
        loss_ref = refs[4 + 3 * n_p]
        outs = [refs[5 + 3 * n_p + 4 * k:9 + 3 * n_p + 4 * k] for k in range(n_p)]

        def column_sum(row):
            return total(sums_ref, (row, row + 1))

        def total(ref, rows=None):
            def part(j):
                return ref[j] if rows is None else ref[j, rows[0]:rows[1], :]
            acc = part(0)
            for j in range(1, N_DEV):
                acc = acc + part(j)
            return acc

        sink_rows = total(misc_ref, (ROW_DSINKS, ROW_DSINKS + 16))
        diag = (lax.broadcasted_iota(jnp.int32, (16, 128), 0) == lax.broadcasted_iota(jnp.int32, (16, 128), 1))
        grads = dict(
            w_spatial=total(wsp_ref), b_spatial=total(misc_ref, (ROW_DBSP, ROW_DBSP + A_GROUPS)),
            sinks=jnp.sum(jnp.where(diag, sink_rows, 0.0), axis=0, keepdims=True),
            norm_g=column_sum(SUM_NORM_G), ln_v_g=total(ln_ref, (0, 1)), ln_v_b=total(ln_ref, (1, 2)),
            final_norm_g=column_sum(SUM_FNG),
            b_ada=jnp.concatenate([column_sum(SUM_SHIFT), column_sum(SUM_SCALE), column_sum(SUM_GATE)], axis=1),
            b_ada_final=jnp.concatenate([column_sum(SUM_SHIFT_F), column_sum(SUM_SCALE_F)], axis=1))
        sq_err = jnp.sum(column_sum(SUM_SQ_ERR), axis=1, keepdims=True)
        loss_ref[...] = jnp.broadcast_to(sq_err * (0.5 / D_MODEL), (1, 128))
        for k, name in enumerate(SMALL_PARAMS):
            w_ref, m_ref, v_ref = wmv[k]
            g_ref, d_ref, nm_ref, nv_ref = outs[k]
            g_ref[...] = grads[name]
            d_ref[...], nm_ref[...], nv_ref[...] = _adamw(w_ref[...], grads[name], m_ref[...], v_ref[...])

    flat = [a for name in SMALL_PARAMS for a in params[name]]
    vmem = pl.BlockSpec(memory_space=pltpu.VMEM)
    out_shape = [SDS((1, 128), F32)] + [SDS(params[name][0].shape, F32) for name in SMALL_PARAMS for _ in range(4)]
    outs = pl.pallas_call(
        body, name="adam_small", in_specs=[vmem] * (4 + len(flat)), out_specs=[vmem] * len(out_shape),
        out_shape=out_shape, compiler_params=_params(),
    )(d_wsp, misc, d_ln, sums, *flat)
    return outs[0], {name: outs[1 + 4 * k:5 + 4 * k] for k, name in enumerate(SMALL_PARAMS)}


def kernel(x, c, w_ada, b_ada, norm_g, w_in, ln_v_g, ln_v_b, w_spatial, b_spatial, sinks, w_out, w_ada_final, b_ada_final, final_norm_g, loss_target, m_w_ada, m_b_ada, m_norm_g, m_w_in, m_ln_v_g, m_ln_v_b, m_w_spatial, m_b_spatial, m_sinks, m_w_out, m_w_ada_final, m_b_ada_final, m_final_norm_g, v_w_ada, v_b_ada, v_norm_g, v_w_in, v_ln_v_g, v_ln_v_b, v_w_spatial, v_b_spatial, v_sinks, v_w_out, v_w_ada_final, v_b_ada_final, v_final_norm_g):
    me = 4 * lax.axis_index("x") + 2 * lax.axis_index("y") + lax.axis_index("c")
    x2, tgt = x[0], loss_target[0]
    fng = final_norm_g.reshape(1, D_MODEL)

    n_ada, n_ada_f = w_ada.shape[2], w_ada_final.shape[1]
    cact, mod, mod_f = _ada_exchange(c, w_ada[0], b_ada.reshape(N_DEV, n_ada), w_ada_final,
                                     b_ada_final.reshape(N_DEV, n_ada_f))
    cact = cact.reshape(N_DEV, D_MODEL)
    mod, mod_f = mod.reshape(1, 3 * D_MODEL), mod_f.reshape(1, 2 * D_MODEL)
    shift, scale, gate = mod[:, :D_MODEL], mod[:, D_MODEL:2 * D_MODEL], mod[:, 2 * D_MODEL:]
    shift_f, scale_f = mod_f[:, :D_MODEL], mod_f[:, D_MODEL:]

    wt_f32, m_wt, v_wt = (jnp.swapaxes(a, 1, 2)[0] for a in (w_in, m_w_in, v_w_in))
    xi, yi = lax.axis_index("x"), lax.axis_index("y")
    chip_order = jnp.stack([2 * xi + yi, 2 * (1 - xi) + yi, 2 * xi + 1 - yi, 2 * (1 - xi) + 1 - yi]).astype(jnp.int32)
    wt_mine, wo_mine = _prep_weights(me.reshape(1), wt_f32, w_out[0])

    freqs = _rope_freqs()
    sinks_v = sinks.reshape(16)
    h, proj, wt = _gather_in_proj(chip_order, x2, shift, scale, norm_g, wt_mine)
    y, wo = _mixer_fwd(proj, freqs, ln_v_g, ln_v_b, w_spatial[0], b_spatial[0], sinks_v, wo_mine)
    dx1, do, dy, sums_o = _out_proj_loss(y, x2, tgt, wo, gate, shift_f, scale_f, fng)

    chip = (2 * lax.axis_index("x") + lax.axis_index("y")).reshape(1)
    pair_out, _ = _wgrad_pair("wgrad_out", y, do, 2048)
    dproj, d_ln, d_wsp, misc, parts_out = _mixer_bwd(
        me.reshape(1), proj, dy, freqs, ln_v_g, ln_v_b, w_spatial[0], b_spatial[0], sinks_v, pair_out)
    pair_in, (d_ln, d_wsp, misc) = _wgrad_pair(
        "wgrad_in", dproj, h, 1024, gathers=(d_ln, d_wsp.reshape(N_DEV * A_GROUPS * CHUNK, CHUNK), misc))
    grad_x, sums, parts_in = _in_proj_bwd(dproj, wt, x2, dx1, scale, norm_g, sums_o, pair_in)
    wt_leaves = [jnp.swapaxes(a[None], 1, 2)
                 for a in _adam_rows("adam_w_in", chip, pair_in, parts_in, wt_f32, m_wt, v_wt, 176)]
    w_out_leaves = [a[None] for a in _adam_rows("adam_w_out", chip, pair_out, parts_out, w_out[0], m_w_out[0], v_w_out[0], 64)]

    (sums,) = _all_gather("gather_sums", [sums], pltpu.VMEM)
    natural = dict(w_spatial=(A_GROUPS * CHUNK, CHUNK), b_spatial=(A_GROUPS, CHUNK), sinks=(1, 16), norm_g=(1, D_MODEL),
                   ln_v_g=(1, D_A), ln_v_b=(1, D_A), final_norm_g=(1, D_MODEL), b_ada=(1, 3 * D_MODEL),
                   b_ada_final=(1, 2 * D_MODEL))
    given = dict(
        w_spatial=(w_spatial, m_w_spatial, v_w_spatial), b_spatial=(b_spatial, m_b_spatial, v_b_spatial),
        sinks=(sinks, m_sinks, v_sinks), norm_g=(norm_g, m_norm_g, v_norm_g), ln_v_g=(ln_v_g, m_ln_v_g, v_ln_v_g),
        ln_v_b=(ln_v_b, m_ln_v_b, v_ln_v_b), final_norm_g=(final_norm_g, m_final_norm_g, v_final_norm_g),
        b_ada=(b_ada, m_b_ada, v_b_ada), b_ada_final=(b_ada_final, m_b_ada_final, v_b_ada_final))
    params = {name: tuple(a.reshape(natural[name]) for a in given[name]) for name in SMALL_PARAMS}
    params["sinks"] = tuple(jnp.pad(a, ((0, 0), (0, 128 - 16))) for a in params["sinks"])
    loss, small = _adam_small(d_wsp.reshape(N_DEV, A_GROUPS * CHUNK, CHUNK), misc.reshape(N_DEV, MISC_ROWS, 128),
                              d_ln.reshape(N_DEV, 8, D_A), sums, params)
    small["sinks"] = [a[:, :16] for a in small["sinks"]]
    small = {name: [a.reshape(given[name][0].shape) for a in small[name]] for name in SMALL_PARAMS}

    dmod_all = jnp.concatenate([sums[:, SUM_SHIFT], sums[:, SUM_SCALE], sums[:, SUM_GATE]], axis=1)
    dmod_f_all = jnp.concatenate([sums[:, SUM_SHIFT_F], sums[:, SUM_SCALE_F]], axis=1)
    dmod_mine = lax.dynamic_slice_in_dim(dmod_all, me * n_ada, n_ada, axis=1)
    dmod_f_mine = lax.dynamic_slice_in_dim(dmod_f_all, me * n_ada_f, n_ada_f, axis=1)
    ada = _adam_ada("adam_w_ada", cact, dmod_mine, w_ada[0], m_w_ada[0], v_w_ada[0])
    ada_f = _adam_ada("adam_w_ada_final", cact, dmod_f_mine, w_ada_final, m_w_ada_final, v_w_ada_final)

    def leaves(k):
        return (ada[k][None], small["b_ada"][k], small["norm_g"][k], wt_leaves[k], small["ln_v_g"][k],
                small["ln_v_b"][k], small["w_spatial"][k], small["b_spatial"][k], small["sinks"][k], w_out_leaves[k],
                ada_f[k], small["b_ada_final"][k], small["final_norm_g"][k])

    return (loss[0, 0], grad_x[None], *leaves(0), *leaves(1), *leaves(2), *leaves(3))
```

```python
import jax
import jax.numpy as jnp
from jax import lax
from jax.experimental import pallas as pl
from jax.experimental.pallas import tpu as pltpu

D_MODEL = 2048
D_IN = 5632
D_A = 1024
CHUNK = 128
A_GROUPS = 8
HEAD_DIM = 64
N_KV_HEADS = 4
N_DEV = 8
ROPE_THETA = 10000.0
NORM_EPS = 1e-5
ATTN_SCALE = HEAD_DIM ** -0.5

ADAM_LR = 0.001
ADAM_B1 = 0.9
ADAM_B2 = 0.999
ADAM_EPS = 1e-08
ADAM_WD = 0.01
ADAM_STEP = 10

OFF_U, OFF_VA, OFF_ZA, OFF_Q, OFF_K, OFF_V, OFF_ZB = 0, 1024, 2048, 3072, 4096, 4352, 4608

SUM_SHIFT, SUM_SCALE, SUM_NORM_G, SUM_GATE, SUM_SHIFT_F, SUM_SCALE_F, SUM_FNG, SUM_SQ_ERR = range(8)

V7X_VMEM_LIMIT_BYTES = 56 * 1024 * 1024

F32 = jnp.float32
BF16 = jnp.bfloat16
MESH = pl.DeviceIdType.MESH
SDS = jax.ShapeDtypeStruct
NT = (((1,), (1,)), ((), ()))
TN = (((0,), (0,)), ((), ()))


def _params(*semantics):
    return pltpu.CompilerParams(dimension_semantics=semantics or None, vmem_limit_bytes=V7X_VMEM_LIMIT_BYTES)


def _mesh_pos():
    return lax.axis_index("x"), lax.axis_index("y"), lax.axis_index("c")


def _sigmoid(z):
    return 1.0 / (1.0 + jnp.exp(-z))


def _adamw(w, g, m, v):
    m = ADAM_B1 * m + (1.0 - ADAM_B1) * g
    v = ADAM_B2 * v + (1.0 - ADAM_B2) * (g * g)
    m_hat = m / (1.0 - ADAM_B1 ** ADAM_STEP)
    v_hat = v / (1.0 - ADAM_B2 ** ADAM_STEP)
    delta = -ADAM_LR * (m_hat / (jnp.sqrt(v_hat) + ADAM_EPS) + ADAM_WD * w)
    return delta, m, v


def _all_gather(name, blocks, memory_space):
    n_arr = len(blocks)

    def body(*refs):
        ins, outs = refs[:n_arr], refs[n_arr:2 * n_arr]
        send_sems, recv_sems, local_sems = refs[2 * n_arr:]
        x, y, c = _mesh_pos()
        me, sibling = (x, y, c), (x, y, 1 - c)
        chips = [(1 - x, y), (x, 1 - y), (1 - x, 1 - y)]

        def slot(p):
            return 4 * p[0] + 2 * p[1] + p[2]

        def copy(a, k, block, to, src=None):
            dst = outs[a].at[slot(block)]
            return pltpu.make_async_remote_copy(
                src_ref=dst if src is None else src, dst_ref=dst,
                send_sem=send_sems.at[a, k], recv_sem=recv_sems.at[a, k],
                device_id=to, device_id_type=MESH)

        mine = [pltpu.make_async_copy(ins[a], outs[a].at[slot(me)], local_sems.at[a]) for a in range(n_arr)]
        for cp in mine:
            cp.start()
        first = []
        for a in range(n_arr):
            first.append(copy(a, 0, me, sibling, src=ins[a]))
            first += [copy(a, 1 + j, me, (*chip, c), src=ins[a]) for j, chip in enumerate(chips)]
        for cp in first:
            cp.start()
        passed = []
        for j, chip in enumerate(chips):
            for a in range(n_arr):
                copy(a, 1 + j, (*chip, c), me).wait_recv()
                fwd = copy(a, 4 + j, (*chip, c), sibling)
                fwd.start()
                passed.append(fwd)
        for a in range(n_arr):
            copy(a, 0, sibling, me).wait_recv()
            for j, chip in enumerate(chips):
                copy(a, 4 + j, (*chip, 1 - c), me).wait_recv()
        for cp in first + passed:
            cp.wait_send()
        for cp in mine:
            cp.wait()

    spec = pl.BlockSpec(memory_space=memory_space)
    return pl.pallas_call(
        body, name=name,
        out_shape=[SDS((N_DEV,) + b.shape, b.dtype) for b in blocks],
        in_specs=[spec] * n_arr, out_specs=[spec] * n_arr,
        scratch_shapes=[pltpu.SemaphoreType.DMA((n_arr, 7)), pltpu.SemaphoreType.DMA((n_arr, 7)),
                        pltpu.SemaphoreType.DMA((n_arr,))],
        compiler_params=_params(),
    )(*blocks)


def _ada_exchange(c, w_ada, b_ada8, w_ada_f, b_ada_f8):
    n1, n2 = w_ada.shape[1], w_ada_f.shape[1]

    def body(c_ref, w1_ref, b1_ref, w2_ref, b2_ref, cact_ref, mod_ref, modf_ref,
             cact_buf, res1, res2, send1, send2, sems_s, sems_r):
        x, y, c_pos = _mesh_pos()
        me = 4 * x + 2 * y + c_pos
        flips = [(k >> 2 & 1, k >> 1 & 1, k & 1) for k in range(1, N_DEV)]

        def peer(f):
            return (1 - x if f[0] else x, 1 - y if f[1] else y, 1 - c_pos if f[2] else c_pos)

        cv = c_ref[...]
        cact = cv * _sigmoid(cv)
        cact_buf[...] = cact
        cact_ref[me] = cact

        def rdma(phase, k, src, dst, f):
            return pltpu.make_async_remote_copy(src_ref=src, dst_ref=dst, send_sem=sems_s.at[phase, k],
                                                recv_sem=sems_r.at[phase, k], device_id=peer(f), device_id_type=MESH)

        gather = [rdma(0, k, cact_buf, cact_ref.at[me], f) for k, f in enumerate(flips)]
        for cp in gather:
            cp.start()
        for cp in gather:
            cp.wait_recv()
        for cp in gather:
            cp.wait_send()

        rid = lax.broadcasted_iota(jnp.int32, (N_DEV, D_MODEL), 0)
        rows = jnp.zeros((N_DEV, D_MODEL), F32)
        for j in range(N_DEV):
            rows = jnp.where(rid == j, jnp.broadcast_to(cact_ref[j], (N_DEV, D_MODEL)), rows)
        rows = rows.astype(BF16)
        res1[...] = jnp.dot(rows, w1_ref[...].astype(BF16), preferred_element_type=F32) + b1_ref[pl.ds(me, 1), :]
        res2[...] = jnp.dot(rows, w2_ref[...].astype(BF16), preferred_element_type=F32) + b2_ref[pl.ds(me, 1), :]
        for j in range(N_DEV):
            send1[j] = res1[pl.ds(j, 1), :]
            send2[j] = res2[pl.ds(j, 1), :]
        mod_ref[me] = send1[me]
        modf_ref[me] = send2[me]
        scatter = []
        for k, f in enumerate(flips):
            to = me ^ (k + 1)
            scatter.append(rdma(1, k, send1.at[to], mod_ref.at[me], f))
            scatter.append(rdma(2, k, send2.at[to], modf_ref.at[me], f))
        for cp in scatter:
            cp.start()
        for cp in scatter:
            cp.wait_recv()
        for cp in scatter:
            cp.wait_send()

    vmem = pl.BlockSpec(memory_space=pltpu.VMEM)
    return pl.pallas_call(
        body, name="ada_exchange",
        out_shape=[SDS((N_DEV, 1, D_MODEL), F32), SDS((N_DEV, 1, n1), F32), SDS((N_DEV, 1, n2), F32)],
        in_specs=[vmem] * 5, out_specs=[vmem] * 3,
        scratch_shapes=[pltpu.VMEM((1, D_MODEL), F32), pltpu.VMEM((N_DEV, n1), F32), pltpu.VMEM((N_DEV, n2), F32),
                        pltpu.VMEM((N_DEV, 1, n1), F32), pltpu.VMEM((N_DEV, 1, n2), F32),
                        pltpu.SemaphoreType.DMA((3, 7)), pltpu.SemaphoreType.DMA((3, 7))],
        compiler_params=_params(),
    )(c, w_ada, b_ada8, w_ada_f, b_ada_f8)


def _chip_scatter(pair_ref, parts_ref, send_sems, recv_sems):
    x, y, c = _mesh_pos()
    chips = [(1 - x, y), (x, 1 - y), (1 - x, 1 - y)]
    return [pltpu.make_async_remote_copy(
        src_ref=pair_ref.at[2 * cx + cy], dst_ref=parts_ref.at[j], send_sem=send_sems.at[j], recv_sem=recv_sems.at[j],
        device_id=(cx, cy, c), device_id_type=MESH) for j, (cx, cy) in enumerate(chips)]


def _scatter_scratch():
    return [pltpu.SemaphoreType.DMA((3,)), pltpu.SemaphoreType.DMA((3,))]


def _prep_weights(me, wt, w_out):
    steps = 4

    def body(me_ref, wt_ref, wo_ref, wtb_ref, wob_ref):
        wtb_ref[...] = wt_ref[...].astype(BF16)
        wob_ref[...] = wo_ref[...].astype(BF16)

    def rows(a, mine):
        blk = (a.shape[0] // steps, a.shape[1])
        return pl.BlockSpec(blk, (lambda i, me_ref: (steps * me_ref[0] + i, 0)) if mine else (lambda i, me_ref: (i, 0)))

    return pl.pallas_call(
        body, name="prep_weights",
        grid_spec=pltpu.PrefetchScalarGridSpec(
            num_scalar_prefetch=1, grid=(steps,),
            in_specs=[rows(wt, False), rows(w_out, False)], out_specs=[rows(wt, True), rows(w_out, True)]),
        out_shape=[SDS((N_DEV * wt.shape[0], D_MODEL), BF16), SDS((N_DEV * w_out.shape[0], D_MODEL), BF16)],
        compiler_params=_params("parallel"),
    )(me, wt, w_out)


class _InPlaceGather:
    def __init__(self, buf_ref, send_sems, recv_sems, relay=False):
        self.buf, self.send_sems, self.recv_sems, self.relay = buf_ref, send_sems, recv_sems, relay
        self.n = buf_ref.shape[0] // N_DEV
        x, y, c = _mesh_pos()
        self.me, self.sibling, self.core = (x, y, c), (x, y, 1 - c), c
        self.chips = [(1 - x, y), (x, 1 - y), (1 - x, 1 - y)]
        self.relay_from = (jnp.where(c == 0, 1 - x, x), jnp.where(c == 0, y, 1 - y), c)
        self.relay_to = (jnp.where(c == 0, x, 1 - x), jnp.where(c == 0, 1 - y, y), c)

    def copy(self, k, block, to):
        start = pl.multiple_of((4 * block[0] + 2 * block[1] + block[2]) * self.n, self.n)
        rows = self.buf.at[pl.ds(start, self.n)]
        return pltpu.make_async_remote_copy(src_ref=rows, dst_ref=rows, send_sem=self.send_sems.at[k],
                                            recv_sem=self.recv_sems.at[k], device_id=to, device_id_type=MESH)

    def start(self):
        self.copy(0, self.me, self.sibling).start()
        for j, chip in enumerate(self.chips[:2] if self.relay else self.chips):
            self.copy(1 + j, self.me, (*chip, self.core)).start()

    def relay_diagonal(self):
        self.copy(3, self.relay_from, self.relay_to).start()

    def pass_on(self, j):
        self.copy(1 + j, (*self.chips[j], self.core), self.me).wait_recv()
        self.copy(4 + j, (*self.chips[j], self.core), self.sibling).start()

    def wait_sibling(self, k):
        self.copy(k, self.sibling, self.me).wait_recv()

    def wait_sends(self):
        for k in range(7):
            self.copy(k, self.me, self.sibling).wait_send()


def _gather_scratch():
    return [pltpu.SemaphoreType.DMA((7,)), pltpu.SemaphoreType.DMA((7,))]


def _gather_in_proj(order, x, shift, scale, norm_g, wt_all):
    s = x.shape[0]
    th, tm = min(512, s), min(1024, s)
    nh, ni = s // th, s // tm
    tn = D_IN // 4
    steps = nh + 4 * ni

    def body(order_ref, x_ref, shift_ref, scale_ref, g_ref, wt_in, h_ref, proj_ref, wt_ref,
             h_scr, w_buf, load_sems, send_sems, recv_sems):
        g = pl.program_id(0)
        gather = _InPlaceGather(wt_ref, send_sems, recv_sems, relay=True)

        def tile_load(slot, chip):
            return pltpu.make_async_copy(wt_ref.at[pl.ds(pl.multiple_of(chip * tn, tn), tn)], w_buf.at[slot],
                                         load_sems.at[slot])

        @pl.when(g == 0)
        def _():
            gather.start()

        @pl.when(g < nh)
        def _():
            xv = x_ref[...]
            r = lax.rsqrt(jnp.mean(xv * xv, axis=-1, keepdims=True) + NORM_EPS)
            hb = (((xv * r) * g_ref[...]) * (1.0 + scale_ref[...]) + shift_ref[...]).astype(BF16)
            h_ref[...] = hb
            h_scr[pl.ds(pl.multiple_of(g * th, th), th), :] = hb

        @pl.when(g == nh - 1)
        def _():
            gather.wait_sibling(0)
            tile_load(0, order_ref[0]).start()

        @pl.when(g >= nh)
        def _():
            t, i = (g - nh) // ni, (g - nh) % ni

            @pl.when(i == 0)
            def _():
                tile_load(t % 2, order_ref[t]).wait()

            @pl.when((i == ni - 1) & (t == 0))
            def _():
                gather.pass_on(0)
                gather.pass_on(1)
                gather.relay_diagonal()

            @pl.when((i == ni // 2) & (t == 2))
            def _():
                gather.pass_on(2)

            for j in range(3):
                @pl.when((i == ni - 1) & (t == j))
                def _():
                    gather.wait_sibling(4 + j)
                    tile_load((j + 1) % 2, order_ref[j + 1]).start()

            lhs = h_scr[pl.ds(pl.multiple_of(i * tm, tm), tm), :]
            proj_ref[...] = lax.dot_general(lhs, w_buf[t % 2], NT, preferred_element_type=F32).astype(BF16)

        @pl.when(g == steps - 1)
        def _():
            gather.wait_sends()

    def h_tile(g, order_ref):
        return (jnp.minimum(g, nh - 1), 0)

    def proj_tile(g, order_ref):
        mm = jnp.maximum(g - nh, 0)
        return (mm % ni, order_ref[mm // ni])

    row = pl.BlockSpec((1, D_MODEL), lambda g, order_ref: (0, 0))
    hbm = pl.BlockSpec(memory_space=pl.ANY)
    return pl.pallas_call(
        body, name="gather_in_proj",
        grid_spec=pltpu.PrefetchScalarGridSpec(
            num_scalar_prefetch=1, grid=(steps,),
            in_specs=[pl.BlockSpec((th, D_MODEL), h_tile), row, row, row, hbm],
            out_specs=[pl.BlockSpec((th, D_MODEL), h_tile), pl.BlockSpec((tm, tn), proj_tile), hbm],
            scratch_shapes=[pltpu.VMEM((s, D_MODEL), BF16), pltpu.VMEM((2, tn, D_MODEL), BF16),
                            pltpu.SemaphoreType.DMA((2,)), *_gather_scratch()]),
        out_shape=[SDS((s, D_MODEL), BF16), SDS((s, D_IN), BF16), SDS(wt_all.shape, BF16)],
        input_output_aliases={5: 2},
        compiler_params=_params("arbitrary"),
    )(order, x, shift, scale, norm_g, wt_all)


def _rope_freqs():
    inv_freq = ROPE_THETA ** (-jnp.arange(0, HEAD_DIM, 2, dtype=F32) / HEAD_DIM)
    return jnp.tile(inv_freq, 4).reshape(1, 128)


class _RopeTables:
    def __init__(self, freq_ref, rows_ref, state_ref, last_ref):
        self.freq, self.rows, self.state, self.last = freq_ref, rows_ref, state_ref, last_ref

    def start(self, block, direction):
        ang = lax.broadcasted_iota(jnp.int32, (CHUNK, 128), 0).astype(F32) * self.freq[...]
        self.rows[0] = jnp.cos(ang)
        self.rows[1] = jnp.sin(ang)
        base = jnp.asarray(block * CHUNK, dtype=F32) * self.freq[...]
        turn = float(direction * CHUNK) * self.freq[...]
        self.state[0:1, :] = jnp.cos(base)
        self.state[1:2, :] = jnp.sin(base)
        self.state[2:3, :] = jnp.cos(turn)
        self.state[3:4, :] = jnp.sin(turn)

    def step(self):
        c, s, ct, st = (self.state[k:k + 1, :] for k in range(4))
        self.state[0:1, :] = c * ct - s * st
        self.state[1:2, :] = s * ct + c * st

    def tables(self):
        c, s = self.state[0:1, :], self.state[1:2, :]
        cos = c * self.rows[0] - s * self.rows[1]
        sin = s * self.rows[0] + c * self.rows[1]
        first_half = (lax.broadcasted_iota(jnp.int32, (1, 128), 1) & (HEAD_DIM - 1)) < HEAD_DIM // 2
        return cos, jnp.where(first_half, -sin, 0.0), jnp.where(first_half, 0.0, sin)

    def keep(self, tabs):
        for k in range(3):
            self.last[k] = tabs[k]

    def kept(self):
        return tuple(self.last[k] for k in range(3))


def _rope_scratch():
    return [pltpu.VMEM((2, CHUNK, 128), F32), pltpu.VMEM((8, 128), F32), pltpu.VMEM((3, CHUNK, 128), F32)]


def _rope(v, cos, sin_lo, sin_hi):
    width = v.shape[1]
    rep = (1, width // 128)
    return (v * jnp.tile(cos, rep) + pltpu.roll(v, width - 32, 1) * jnp.tile(sin_lo, rep)
            + pltpu.roll(v, 32, 1) * jnp.tile(sin_hi, rep))


def _rope_bwd(d, cos, sin_lo, sin_hi):
    width = d.shape[1]
    rep = (1, width // 128)
    return (d * jnp.tile(cos, rep) + pltpu.roll(d * jnp.tile(sin_lo, rep), 32, 1)
            + pltpu.roll(d * jnp.tile(sin_hi, rep), width - 32, 1))


def _layer_norm(v, g, b):
    mu = jnp.mean(v, axis=-1, keepdims=True)
    vc = v - mu
    rstd = lax.rsqrt(jnp.mean(vc * vc, axis=-1, keepdims=True) + NORM_EPS)
    vhat = vc * rstd
    return vhat * g + b, vhat, rstd


def _set_tril(w_ref, out_ref):
    t = lax.broadcasted_iota(jnp.int32, (CHUNK, CHUNK), 0)
    tp = lax.broadcasted_iota(jnp.int32, (CHUNK, CHUNK), 1)
    for g in range(A_GROUPS):
        out_ref[g] = jnp.where(tp <= t, w_ref[g], 0.0).astype(BF16)


def _bias_columns(b_ref, out_ref):
    for g in range(A_GROUPS):
        out_ref[g] = jnp.broadcast_to(b_ref[pl.ds(g, 1), :], (CHUNK, CHUNK)).T


def _from_prev():
    r = lax.broadcasted_iota(jnp.int32, (CHUNK, 4 * CHUNK), 0)
    i = lax.broadcasted_iota(jnp.int32, (CHUNK, 4 * CHUNK), 1) & (CHUNK - 1)
    return r > i


def _set_unfold_masks(mask_ref):
    prev = _from_prev()
    mask_ref[0] = jnp.where(prev, 1.0, 0.0).astype(BF16)
    mask_ref[1] = jnp.where(prev, 0.0, 1.0).astype(BF16)


def _fold_band(t, from_prev):
    return jnp.where(from_prev, t[:CHUNK], t[CHUNK:])


def _unfold_band(t, mask_ref):
    return jnp.concatenate([t * mask_ref[0], t * mask_ref[1]], axis=0)


def _low_lanes():
    return lax.broadcasted_iota(jnp.int32, (1, 128), 1) < HEAD_DIM


def _stack_heads(pair_a, pair_b):
    lo = _low_lanes()
    return jnp.concatenate([jnp.where(lo, pair_a, 0.0), jnp.where(lo, 0.0, pair_a),
                            jnp.where(lo, pair_b, 0.0), jnp.where(lo, 0.0, pair_b)], axis=0).astype(BF16)


def _heads_to_lanes(per_group):
    rows = [t[:, r * CHUNK:(r + 1) * CHUNK] for t in per_group for r in range(4)]
    return jnp.concatenate(rows, axis=0).T


def _dup_kv_head(band, gk):
    pair = band[:, (gk // 2) * 128:(gk // 2 + 1) * 128]
    lo = _low_lanes()
    one = jnp.where(lo if gk % 2 == 0 else jnp.logical_not(lo), pair, 0.0)
    return (one + pltpu.roll(one, HEAD_DIM, 1)).astype(BF16)


def _fold_kv_head(dup_grad, gk):
    both = dup_grad + pltpu.roll(dup_grad, HEAD_DIM, 1)
    lo = _low_lanes()
    return jnp.where(lo if gk % 2 == 0 else jnp.logical_not(lo), both, 0.0)


def _attn_probs(q_st, k_dup, sink_row, from_prev, first_block):
    s = lax.dot_general(k_dup, q_st, NT, preferred_element_type=F32)
    no_prev = jnp.where(first_block, -jnp.inf, 0.0)
    s = jnp.where(from_prev, s[:CHUNK] + no_prev, s[CHUNK:])
    m = jnp.maximum(jnp.max(s, axis=0, keepdims=True), sink_row)
    p = jnp.exp(s - m)
    e_sink = jnp.exp(sink_row - m)
    inv = 1.0 / (jnp.sum(p, axis=0, keepdims=True) + e_sink)
    return p * inv, e_sink * inv


def _sink_row(sinks_ref, gk):
    return jnp.concatenate([jnp.full((1, CHUNK), sinks_ref[4 * gk + r], F32) for r in range(4)], axis=1)


def _mixer_specs(nb, rev):
    def blk(i):
        return nb - 1 - i if rev else i

    def prev(i):
        return jnp.maximum(blk(i) - 1, 0)

    return dict(
        cur=pl.BlockSpec((CHUNK, D_IN), lambda i, *_: (blk(i), 0)),
        prev_kv=pl.BlockSpec((CHUNK, 2 * 256), lambda i, *_: (prev(i), OFF_K // 512)),
        freq=pl.BlockSpec((1, 128), lambda i, *_: (0, 0)),
        vec=pl.BlockSpec((1, D_A), lambda i, *_: (0, 0)),
        wsp=pl.BlockSpec((A_GROUPS, CHUNK, CHUNK), lambda i, *_: (0, 0, 0)),
        bsp=pl.BlockSpec((A_GROUPS, CHUNK), lambda i, *_: (0, 0)),
        smem=pl.BlockSpec(memory_space=pltpu.SMEM),
        blk=blk,
    )


def _mixer_fwd(proj, freqs, ln_g, ln_b, w_sp, b_sp, sinks, wo_all):
    s = proj.shape[0]
    nb = s // CHUNK
    sp = _mixer_specs(nb, rev=False)

    def body(cur_ref, pkv_ref, freq_ref, lg_ref, lb_ref, w_ref, b_ref, sinks_ref, wo_in, y_ref, wo_ref,
             bcol, wtril, mask, rope_rows, rope_state, rope_last, send_sems, recv_sems):
        i = pl.program_id(0)
        gather = _InPlaceGather(wo_ref, send_sems, recv_sems)
        rope = _RopeTables(freq_ref, rope_rows, rope_state, rope_last)

        @pl.when(i == 0)
        def _():
            gather.start()
            _bias_columns(b_ref, bcol)
            _set_tril(w_ref, wtril)
            _set_unfold_masks(mask)
            rope.start(-1, 1)
            rope_last[...] = jnp.zeros_like(rope_last)

        @pl.when(i == (7 * nb) // 8)
        def _():
            for j in range(3):
                gather.pass_on(j)

        vln, _, _ = _layer_norm(cur_ref[:, OFF_VA:OFF_ZA].astype(F32), lg_ref[...], lb_ref[...])
        vln = vln.astype(BF16)

        def gating_group(g):
            cols = slice(g * 128, (g + 1) * 128)
            sg = jnp.dot(wtril[g], vln[:, cols], preferred_element_type=F32) + bcol[g]
            u = cur_ref[:, OFF_U + g * 128:OFF_U + (g + 1) * 128].astype(F32)
            z = cur_ref[:, OFF_ZA + g * 128:OFF_ZA + (g + 1) * 128].astype(F32)
            y_ref[:, cols] = (u * sg * (z * _sigmoid(z))).astype(BF16)

        rope.step()
        cur_t, prev_t = rope.tables(), rope.kept()
        rope.keep(cur_t)
        qr = _rope(cur_ref[:, OFF_Q:OFF_K].astype(F32), *cur_t) * ATTN_SCALE
        kr = jnp.concatenate([_rope(pkv_ref[:, 0:256].astype(F32), *prev_t),
                              _rope(cur_ref[:, OFF_K:OFF_V].astype(F32), *cur_t)], axis=0)
        v_t = jnp.concatenate([pkv_ref[:, 256:512], cur_ref[:, OFF_V:OFF_ZB]], axis=0).astype(F32).T.astype(BF16)
        outs = []
        from_prev = _from_prev()
        for gk in range(N_KV_HEADS):
            q_st = _stack_heads(qr[:, (2 * gk) * 128:(2 * gk + 1) * 128], qr[:, (2 * gk + 1) * 128:(2 * gk + 2) * 128])
            probs, _ = _attn_probs(q_st, _dup_kv_head(kr, gk), _sink_row(sinks_ref, gk), from_prev, i == 0)
            gating_group(2 * gk)
            outs.append(jnp.dot(v_t[gk * HEAD_DIM:(gk + 1) * HEAD_DIM], _unfold_band(probs.astype(BF16), mask),
                                preferred_element_type=F32))
            gating_group(2 * gk + 1)
        zb = cur_ref[:, OFF_ZB:D_IN].astype(F32)
        y_ref[:, D_A:D_MODEL] = (_heads_to_lanes(outs) * (zb * _sigmoid(zb))).astype(BF16)

        @pl.when(i == nb - 1)
        def _():
            gather.wait_sibling(0)
            for j in range(3):
                gather.wait_sibling(4 + j)
            gather.wait_sends()

    hbm = pl.BlockSpec(memory_space=pl.ANY)
    return pl.pallas_call(
        body, name="mixer_fwd", grid=(nb,),
        in_specs=[sp["cur"], sp["prev_kv"], sp["freq"], sp["vec"], sp["vec"], sp["wsp"], sp["bsp"], sp["smem"], hbm],
        out_specs=[pl.BlockSpec((CHUNK, D_MODEL), lambda i: (i, 0)), hbm],
        out_shape=[SDS((s, D_MODEL), BF16), SDS(wo_all.shape, wo_all.dtype)],
        scratch_shapes=[pltpu.VMEM((A_GROUPS, CHUNK, CHUNK), F32), pltpu.VMEM((A_GROUPS, CHUNK, CHUNK), BF16),
                        pltpu.VMEM((2, CHUNK, 4 * CHUNK), BF16), *_rope_scratch(), *_gather_scratch()],
        input_output_aliases={8: 1},
        compiler_params=_params("arbitrary"),
    )(proj, proj, freqs, ln_g, ln_b, w_sp, b_sp, sinks, wo_all)


def _out_proj_loss(y, x, target, wo, gate, shift_f, scale_f, fng):
    s = y.shape[0]
    tm, tr = 256, 128
    nt = s // tm

    def body(y_ref, x_ref, t_ref, wo_ref, gate_ref, sh_ref, sc_ref, g_ref, dx1_ref, do_ref, dy_ref, sums_ref,
             do_last, do_work):
        i = pl.program_id(0)

        @pl.when(i == 0)
        def _():
            sums_ref[...] = jnp.zeros_like(sums_ref)
            do_last[...] = jnp.zeros_like(do_last)

        do_work[...] = do_last[...]
        o = jnp.dot(y_ref[...], wo_ref[...], preferred_element_type=F32)
        gate, g, sh = gate_ref[...], g_ref[...], sh_ref[...]
        one_sc = 1.0 + sc_ref[...]
        cs, inv_d = g * one_sc, 1.0 / D_MODEL

        def rowsum(v):
            return jnp.sum(v, axis=0, keepdims=True)

        sums = [jnp.zeros((1, D_MODEL), F32) for _ in range(4)]
        for c in range(tm // tr):
            rows = slice(c * tr, (c + 1) * tr)
            oc = o[rows]
            x1 = x_ref[rows, :] + gate * oc
            r = lax.rsqrt(jnp.sum(x1 * x1, axis=-1, keepdims=True) * inv_d + NORM_EPS)
            x1n = x1 * r
            diff = x1n * cs + sh - t_ref[rows, :]
            w = diff * x1n
            lane_sum = jnp.sum(w * cs, axis=-1, keepdims=True)
            dx1 = (diff * cs) * (r * inv_d) - x1n * (r * lane_sum * (inv_d * inv_d))
            dx1_ref[rows, :] = dx1
            do = (dx1 * gate).astype(BF16)
            do_ref[rows, :] = do
            do_last[rows, :] = do
            for k, v in enumerate((dx1 * oc, diff, w, diff * diff)):
                sums[k] = sums[k] + rowsum(v)
        live = jnp.where(i < nt, 1.0, 0.0)
        for row, v in ((SUM_GATE, sums[0]), (SUM_SHIFT_F, inv_d * sums[1]), (SUM_SCALE_F, inv_d * (sums[2] * g)),
                       (SUM_FNG, inv_d * (sums[2] * one_sc)), (SUM_SQ_ERR, sums[3])):
            sums_ref[row:row + 1, :] += live * v
        dy_ref[...] = lax.dot_general(do_work[...], wo_ref[...], NT, preferred_element_type=F32).astype(BF16)

    tile = pl.BlockSpec((tm, D_MODEL), lambda i: (jnp.minimum(i, nt - 1), 0))
    row = pl.BlockSpec((1, D_MODEL), lambda i: (0, 0))
    return pl.pallas_call(
        body, name="out_proj_loss", grid=(nt + 1,),
        in_specs=[tile, tile, tile, pl.BlockSpec((D_MODEL, D_MODEL), lambda i: (0, 0)), row, row, row, row],
        out_specs=[tile, tile, pl.BlockSpec((tm, D_MODEL), lambda i: (jnp.maximum(i - 1, 0), 0)),
                   pl.BlockSpec((8, D_MODEL), lambda i: (0, 0))],
        out_shape=[SDS((s, D_MODEL), F32), SDS((s, D_MODEL), BF16), SDS((s, D_MODEL), BF16), SDS((8, D_MODEL), F32)],
        scratch_shapes=[pltpu.VMEM((tm, D_MODEL), BF16), pltpu.VMEM((tm, D_MODEL), BF16)],
        compiler_params=_params("arbitrary"),
    )(y, x, target, wo, gate, shift_f, scale_f, fng)


ROW_DBSP, ROW_DSINKS, MISC_ROWS = 0, 8, 32


def _mixer_bwd(me, proj, dy, freqs, ln_g, ln_b, w_sp, b_sp, sinks, pair):
    s = proj.shape[0]
    nb = s // CHUNK
    sp = _mixer_specs(nb, rev=True)

    def body(me_ref, cur_ref, pkv_ref, dy_ref, freq_ref, lg_ref, lb_ref, w_ref, b_ref, sinks_ref, pair_ref,
             dproj_ref, dln_ref, dw_ref, misc_ref, parts_ref, bcol, wtril, dbcol, carry, mask, rope_rows, rope_state,
             rope_last, send_sems, recv_sems):
        i = pl.program_id(0)
        block = nb - 1 - i
        rope = _RopeTables(freq_ref, rope_rows, rope_state, rope_last)

        @pl.when(i == 0)
        def _():
            for cp in _chip_scatter(pair_ref, parts_ref, send_sems, recv_sems):
                cp.start()
            _bias_columns(b_ref, bcol)
            _set_tril(w_ref, wtril)
            _set_unfold_masks(mask)
            rope.start(nb - 1, -1)
            rope.keep(rope.tables())
            dbcol[...] = jnp.zeros_like(dbcol)
            carry[...] = jnp.zeros_like(carry)
            dln_ref[...] = jnp.zeros_like(dln_ref)
            dw_ref[...] = jnp.zeros_like(dw_ref)
            misc_ref[...] = jnp.zeros_like(misc_ref)

        vln, vhat, rstd = _layer_norm(cur_ref[:, OFF_VA:OFF_ZA].astype(F32), lg_ref[...], lb_ref[...])
        vln = vln.astype(BF16)
        d_vln = []

        def gating_group(g):
            cols = slice(g * 128, (g + 1) * 128)
            w_g = wtril[g]
            sg = jnp.dot(w_g, vln[:, cols], preferred_element_type=F32) + bcol[g]
            u = cur_ref[:, OFF_U + g * 128:OFF_U + (g + 1) * 128].astype(F32)
            z = cur_ref[:, OFF_ZA + g * 128:OFF_ZA + (g + 1) * 128].astype(F32)
            dya = dy_ref[:, cols].astype(F32)
            sig = _sigmoid(z)
            d_ya = dya * (z * sig)
            dproj_ref[:, OFF_ZA + g * 128:OFF_ZA + (g + 1) * 128] = (
                dya * (u * sg) * (sig * (1.0 + z * (1.0 - sig)))).astype(BF16)
            dproj_ref[:, OFF_U + g * 128:OFF_U + (g + 1) * 128] = (d_ya * sg).astype(BF16)
            d_s = d_ya * u
            dbcol[g] += d_s
            d_sb = d_s.astype(BF16)
            dw_ref[g] += lax.dot_general(d_sb, vln[:, cols], NT, preferred_element_type=F32)
            d_vln.append(lax.dot_general(w_g, d_sb, TN, preferred_element_type=F32))

        cur_t = rope.kept()
        rope.step()
        prev_t = rope.tables()
        rope.keep(prev_t)
        band_t = tuple(jnp.concatenate([p, c], axis=0) for p, c in zip(prev_t, cur_t))
        qr = _rope(cur_ref[:, OFF_Q:OFF_K].astype(F32), *cur_t) * ATTN_SCALE
        kr = jnp.concatenate([_rope(pkv_ref[:, 0:256].astype(F32), *prev_t),
                              _rope(cur_ref[:, OFF_K:OFF_V].astype(F32), *cur_t)], axis=0)
        vb = jnp.concatenate([pkv_ref[:, 256:512], cur_ref[:, OFF_V:OFF_ZB]], axis=0).astype(F32)
        k_t, v_t = (kr.T * ATTN_SCALE).astype(BF16), vb.T.astype(BF16)
        zb = cur_ref[:, OFF_ZB:D_IN].astype(F32)
        dyb = dy_ref[:, D_A:D_MODEL].astype(F32)
        sig = _sigmoid(zb)
        d_yb = dyb * (zb * sig)
        outs, dqs = [], []
        dk_pairs = [jnp.zeros((2 * CHUNK, 128), F32) for _ in range(2)]
        dv_pairs = [jnp.zeros((2 * CHUNK, 128), F32) for _ in range(2)]
        from_prev = _from_prev()
        for gk in range(N_KV_HEADS):
            heads = slice(gk * HEAD_DIM, (gk + 1) * HEAD_DIM)
            q_st = _stack_heads(qr[:, (2 * gk) * 128:(2 * gk + 1) * 128], qr[:, (2 * gk + 1) * 128:(2 * gk + 2) * 128])
            k_dup, v_dup = _dup_kv_head(kr, gk), _dup_kv_head(vb, gk)
            probs, p_sink = _attn_probs(q_st, k_dup, _sink_row(sinks_ref, gk), from_prev, block == 0)
            probs_b = _unfold_band(probs.astype(BF16), mask)
            outs.append(jnp.dot(v_t[heads], probs_b, preferred_element_type=F32))
            do_st = _stack_heads(d_yb[:, (2 * gk) * 128:(2 * gk + 1) * 128], d_yb[:, (2 * gk + 1) * 128:(2 * gk + 2) * 128])
            dp = _fold_band(lax.dot_general(v_dup, do_st, NT, preferred_element_type=F32), from_prev)
            delta = jnp.sum(probs * dp, axis=0, keepdims=True)
            ds = _unfold_band((probs * (dp - delta)).astype(BF16), mask)
            gating_group(2 * gk)
            d_sink = -p_sink * delta
            for r in range(4):
                row = ROW_DSINKS + 4 * gk + r
                misc_ref[row:row + 1, :] += jnp.broadcast_to(
                    jnp.sum(d_sink[:, r * CHUNK:(r + 1) * CHUNK], axis=1, keepdims=True), (1, 128))
            dqs.append(jnp.dot(k_t[heads], ds, preferred_element_type=F32))
            dk_pairs[gk // 2] += _fold_kv_head(jnp.dot(ds, q_st, preferred_element_type=F32), gk)
            dv_pairs[gk // 2] += _fold_kv_head(jnp.dot(probs_b, do_st, preferred_element_type=F32), gk)
            gating_group(2 * gk + 1)
        d_vln = jnp.concatenate(d_vln, axis=1)
        dln_ref[0:1, :] += jnp.sum(d_vln * vhat, axis=0, keepdims=True)
        dln_ref[1:2, :] += jnp.sum(d_vln, axis=0, keepdims=True)
        d_vhat = d_vln * lg_ref[...]
        d_va = rstd * (d_vhat - jnp.mean(d_vhat, axis=-1, keepdims=True)
                       - vhat * jnp.mean(d_vhat * vhat, axis=-1, keepdims=True))
        dproj_ref[:, OFF_VA:OFF_ZA] = d_va.astype(BF16)
        dproj_ref[:, OFF_ZB:D_IN] = (dyb * _heads_to_lanes(outs) * (sig * (1.0 + zb * (1.0 - sig)))).astype(BF16)
        dproj_ref[:, OFF_Q:OFF_K] = _rope_bwd(_heads_to_lanes(dqs), *cur_t).astype(BF16)
        dk_band = _rope_bwd(jnp.concatenate(dk_pairs, axis=1), *band_t)
        dv_band = jnp.concatenate(dv_pairs, axis=1)
        dproj_ref[:, OFF_K:OFF_V] = (dk_band[CHUNK:] + carry[:, 0:256]).astype(BF16)
        dproj_ref[:, OFF_V:OFF_ZB] = (dv_band[CHUNK:] + carry[:, 256:512]).astype(BF16)
        carry[:, 0:256] = dk_band[:CHUNK]
        carry[:, 256:512] = dv_band[:CHUNK]

        @pl.when(i == nb - 1)
        def _():
            t = lax.broadcasted_iota(jnp.int32, (CHUNK, CHUNK), 0)
            tp = lax.broadcasted_iota(jnp.int32, (CHUNK, CHUNK), 1)
            for g in range(A_GROUPS):
                dw_ref[g] = jnp.where(tp <= t, dw_ref[g], 0.0)
                misc_ref[pl.ds(ROW_DBSP + g, 1), :] = jnp.sum(dbcol[g].T, axis=0, keepdims=True)
            scatter = _chip_scatter(pair_ref, parts_ref, send_sems, recv_sems)
            for cp in scatter:
                cp.wait_recv()
            for cp in scatter:
                cp.wait_send()

    blk = sp["blk"]
    hbm = pl.BlockSpec(memory_space=pl.ANY)
    return pl.pallas_call(
        body, name="mixer_bwd",
        grid_spec=pltpu.PrefetchScalarGridSpec(
            num_scalar_prefetch=1, grid=(nb,),
            in_specs=[sp["cur"], sp["prev_kv"], pl.BlockSpec((CHUNK, D_MODEL), lambda i, me_ref: (blk(i), 0)),
                      sp["freq"], sp["vec"], sp["vec"], sp["wsp"], sp["bsp"], sp["smem"], hbm],
            out_specs=[pl.BlockSpec((CHUNK, D_IN), lambda i, me_ref: (blk(i), 0)),
                       pl.BlockSpec((8, D_A), lambda i, me_ref: (me_ref[0], 0)),
                       pl.BlockSpec((A_GROUPS, CHUNK, CHUNK), lambda i, me_ref: (me_ref[0], 0, 0)),
                       pl.BlockSpec((MISC_ROWS, 128), lambda i, me_ref: (me_ref[0], 0)), hbm],
            scratch_shapes=[pltpu.VMEM((A_GROUPS, CHUNK, CHUNK), F32), pltpu.VMEM((A_GROUPS, CHUNK, CHUNK), BF16),
                            pltpu.VMEM((A_GROUPS, CHUNK, CHUNK), F32), pltpu.VMEM((CHUNK, 512), F32), pltpu.VMEM((2, CHUNK, 4 * CHUNK), BF16),
                            *_rope_scratch(), *_scatter_scratch()]),
        out_shape=[SDS((s, D_IN), BF16), SDS((N_DEV * 8, D_A), F32), SDS((N_DEV * A_GROUPS, CHUNK, CHUNK), F32),
                   SDS((N_DEV * MISC_ROWS, 128), F32), SDS((3,) + pair.shape[1:], pair.dtype)],
        compiler_params=_params("arbitrary"),
    )(me, proj, proj, dy, freqs, ln_g, ln_b, w_sp, b_sp, sinks, pair)


def _wgrad_pair(name, a, b, bt, gathers=()):
    s, m = a.shape
    n = b.shape[1]
    bm, half = m // 4, m // 8
    bt = min(bt, s)
    steps = s // bt
    last = 4 * steps
    n_g = len(gathers)

    def body(*refs):
        a_ref, b_ref = refs[:2]
        out_ref, bufs = refs[2 + n_g], refs[3 + n_g:3 + 2 * n_g]
        acc, kept, got, sent, send_sems, recv_sems = refs[3 + 2 * n_g:9 + 2 * n_g]
        sems = refs[9 + 2 * n_g:]
        g = pl.program_id(0)
        tile, t = g // steps, g % steps
        mx, my, mc = _mesh_pos()
        jobs = [_InPlaceGather(bufs[k], sems[2 * k], sems[2 * k + 1]) for k in range(n_g)]

        def exchange(q):
            return pltpu.make_async_remote_copy(src_ref=sent, dst_ref=got.at[q % 2], send_sem=send_sems.at[q],
                                                recv_sem=recv_sems.at[q], device_id=(mx, my, 1 - mc),
                                                device_id_type=MESH)

        @pl.when(g == 0)
        def _():
            for job in jobs:
                job.start()

        @pl.when(g == 2 * steps)
        def _():
            for job in jobs:
                for j in range(3):
                    job.pass_on(j)

        @pl.when(g < last)
        def _():
            @pl.when(t == 0)
            def _():
                acc[...] = lax.dot_general(a_ref[...], b_ref[...], TN, preferred_element_type=F32)

            @pl.when(t > 0)
            def _():
                acc[...] += lax.dot_general(a_ref[...], b_ref[...], TN, preferred_element_type=F32)

            @pl.when(t == steps - 1)
            def _():
                @pl.when(tile > 0)
                def _():
                    exchange(tile - 1).wait_send()

                kept[tile % 2] = acc[pl.ds(pl.multiple_of(mc * half, 8), half), :].astype(BF16)
                sent[...] = acc[pl.ds(pl.multiple_of((1 - mc) * half, 8), half), :].astype(BF16)
                exchange(tile).start()

        @pl.when((t == 0) & (g > 0))
        def _():
            q = tile - 1
            exchange(q).wait_recv()
            out_ref[0] = (kept[q % 2].astype(F32) + got[q % 2].astype(F32)).astype(BF16)

        @pl.when(g == last)
        def _():
            exchange(3).wait_send()
            for job in jobs:
                job.wait_sibling(0)
                for j in range(3):
                    job.wait_sibling(4 + j)
                job.wait_sends()

    def a_tile(g):
        gg = jnp.minimum(g, last - 1)
        return (gg % steps, gg // steps)

    def b_tile(g):
        return (jnp.minimum(g, last - 1) % steps, 0)

    hbm = pl.BlockSpec(memory_space=pl.ANY)
    outs = pl.pallas_call(
        body, name=name, grid=(last + 1,),
        in_specs=[pl.BlockSpec((bt, bm), a_tile), pl.BlockSpec((bt, n), b_tile)] + [hbm] * n_g,
        out_specs=[pl.BlockSpec((1, half, n), lambda g: (jnp.maximum(g - 1, 0) // steps, 0, 0))] + [hbm] * n_g,
        out_shape=[SDS((4, half, n), BF16)] + [SDS(gb.shape, gb.dtype) for gb in gathers],
        scratch_shapes=[pltpu.VMEM((bm, n), F32), pltpu.VMEM((2, half, n), BF16), pltpu.VMEM((2, half, n), BF16),
                        pltpu.VMEM((half, n), BF16), pltpu.SemaphoreType.DMA((4,)), pltpu.SemaphoreType.DMA((4,))]
        + _gather_scratch() * n_g,
        input_output_aliases={2 + k: 1 + k for k in range(n_g)},
        compiler_params=_params("arbitrary"),
    )(a, b, *gathers)
    return outs[0], outs[1:]


def _in_proj_bwd(dproj, wt, x, dx1, scale, norm_g, sums_o, pair):
    s = x.shape[0]
    tm, tk, tr = min(1024, s), 1536, 64
    ksteps = pl.cdiv(D_IN, tk)
    k_last = D_IN - (ksteps - 1) * tk

    def body(dp_ref, wt_ref, x_hbm, dx1_hbm, sc_ref, g_ref, so_ref, pair_ref, gx_ref, sums_ref, parts_ref, x_buf,
             dx1_buf, tile_sems, send_sems, recv_sems):
        i, k = pl.program_id(0), pl.program_id(1)

        def tile_copies():
            rows = pl.ds(pl.multiple_of(i * tm, tm), tm)
            return (pltpu.make_async_copy(x_hbm.at[rows], x_buf, tile_sems.at[0]),
                    pltpu.make_async_copy(dx1_hbm.at[rows], dx1_buf, tile_sems.at[1]))

        @pl.when((i == 0) & (k == 0))
        def _():
            for cp in _chip_scatter(pair_ref, parts_ref, send_sems, recv_sems):
                cp.start()
            sums_ref[...] = so_ref[...]

        @pl.when(k == 0)
        def _():
            for cp in tile_copies():
                cp.start()
            gx_ref[...] = jnp.dot(dp_ref[...], wt_ref[...], preferred_element_type=F32)

        @pl.when((k > 0) & (k < ksteps - 1))
        def _():
            gx_ref[...] += jnp.dot(dp_ref[...], wt_ref[...], preferred_element_type=F32)

        @pl.when(k == ksteps - 1)
        def _():
            gx_ref[...] += jnp.dot(dp_ref[:, :k_last], wt_ref[:k_last, :], preferred_element_type=F32)

        @pl.when(k == ksteps - 1)
        def _():
            for cp in tile_copies():
                cp.wait()
            one_sc, g = 1.0 + sc_ref[...], g_ref[...]
            cs = one_sc * g

            def chunk(j, sums):
                rows = pl.ds(pl.multiple_of(j * tr, tr), tr)
                dh, xv = gx_ref[rows, :], x_buf[rows, :]
                dhx = dh * xv
                r = lax.rsqrt(jnp.sum(xv * xv, axis=-1, keepdims=True) * (1.0 / D_MODEL) + NORM_EPS)
                coef = (r * r * r) * (jnp.sum(dhx * cs, axis=-1, keepdims=True) * (1.0 / D_MODEL))
                gx_ref[rows, :] = dx1_buf[rows, :] + r * (dh * cs) - xv * coef
                return (sums[0] + jnp.sum(dh, axis=0, keepdims=True), sums[1] + jnp.sum(dhx * r, axis=0, keepdims=True))

            zero = jnp.zeros((1, D_MODEL), F32)
            sums = lax.fori_loop(0, tm // tr, chunk, (zero, zero))
            sums_ref[SUM_SHIFT:SUM_SHIFT + 1, :] += sums[0]
            sums_ref[SUM_SCALE:SUM_SCALE + 1, :] += sums[1] * g
            sums_ref[SUM_NORM_G:SUM_NORM_G + 1, :] += sums[1] * one_sc

        @pl.when((i == s // tm - 1) & (k == ksteps - 1))
        def _():
            scatter = _chip_scatter(pair_ref, parts_ref, send_sems, recv_sems)
            for cp in scatter:
                cp.wait_recv()
            for cp in scatter:
                cp.wait_send()

    row = pl.BlockSpec((1, D_MODEL), lambda i, k: (0, 0))
    hbm = pl.BlockSpec(memory_space=pl.ANY)
    return pl.pallas_call(
        body, name="in_proj_bwd", grid=(s // tm, ksteps),
        in_specs=[pl.BlockSpec((tm, tk), lambda i, k: (i, k)), pl.BlockSpec((tk, D_MODEL), lambda i, k: (k, 0)),
                  hbm, hbm, row, row, pl.BlockSpec((8, D_MODEL), lambda i, k: (0, 0)), hbm],
        out_specs=[pl.BlockSpec((tm, D_MODEL), lambda i, k: (i, 0)), pl.BlockSpec((8, D_MODEL), lambda i, k: (0, 0)),
                   hbm],
        out_shape=[SDS((s, D_MODEL), F32), SDS((8, D_MODEL), F32), SDS((3,) + pair.shape[1:], pair.dtype)],
        scratch_shapes=[pltpu.VMEM((tm, D_MODEL), F32), pltpu.VMEM((tm, D_MODEL), F32),
                        pltpu.SemaphoreType.DMA((2,)), *_scatter_scratch()],
        compiler_params=_params("arbitrary", "arbitrary"),
    )(dproj, wt, x, dx1, scale, norm_g, sums_o, pair)


def _sum_chips(own_ref, parts_ref):
    return ((own_ref[0].astype(F32) + parts_ref[0].astype(F32)) + parts_ref[1].astype(F32)) + parts_ref[2].astype(F32)


def _adam_rows(name, chip, pair, parts, w, m, v, tr):
    rows = w.shape[0]

    def body(chip_ref, own_ref, p_ref, w_ref, m_ref, v_ref, g_ref, d_ref, nm_ref, nv_ref):
        g = _sum_chips(own_ref, p_ref)
        g_ref[...] = g
        d_ref[...], nm_ref[...], nv_ref[...] = _adamw(w_ref[...], g, m_ref[...], v_ref[...])

    blk = pl.BlockSpec((tr, D_MODEL), lambda j, chip_ref: (j, 0))
    return pl.pallas_call(
        body, name=name,
        grid_spec=pltpu.PrefetchScalarGridSpec(
            num_scalar_prefetch=1, grid=(rows // tr,),
            in_specs=[pl.BlockSpec((1, tr, D_MODEL), lambda j, chip_ref: (chip_ref[0], j, 0)),
                      pl.BlockSpec((3, tr, D_MODEL), lambda j, chip_ref: (0, j, 0)), blk, blk, blk],
            out_specs=[blk] * 4),
        out_shape=[SDS(w.shape, F32)] * 4, compiler_params=_params("parallel"),
    )(chip, pair, parts, w, m, v)


def _adam_ada(name, cact, dmod, w, m, v):
    n = w.shape[1]
    tr = 512

    def body(c_ref, dm_ref, w_ref, m_ref, v_ref, g_ref, d_ref, nm_ref, nv_ref):
        pad_c = jnp.concatenate([c_ref[...], jnp.zeros_like(c_ref)], axis=0).astype(BF16)
        pad_d = jnp.concatenate([dm_ref[...], jnp.zeros_like(dm_ref)], axis=0).astype(BF16)
        g = lax.dot_general(pad_c, pad_d, TN, preferred_element_type=F32)
        g_ref[...] = g
        d_ref[...], nm_ref[...], nv_ref[...] = _adamw(w_ref[...], g, m_ref[...], v_ref[...])

    blk = pl.BlockSpec((tr, n), lambda j: (j, 0))
    return pl.pallas_call(
        body, name=name, grid=(D_MODEL // tr,),
        in_specs=[pl.BlockSpec((N_DEV, tr), lambda j: (0, j)), pl.BlockSpec((N_DEV, n), lambda j: (0, 0)),
                  blk, blk, blk],
        out_specs=[blk] * 4, out_shape=[SDS(w.shape, F32)] * 4,
        compiler_params=_params("parallel"),
    )(cact, dmod, w, m, v)


SMALL_PARAMS = ("w_spatial", "b_spatial", "sinks", "norm_g", "ln_v_g", "ln_v_b", "final_norm_g", "b_ada", "b_ada_final")


def _adam_small(d_wsp, misc, d_ln, sums, params):
    n_p = len(SMALL_PARAMS)

    def body(*refs):
        wsp_ref, misc_ref, ln_ref, sums_ref = refs[:4]
        wmv = [refs[4 + 3 * k:7 + 3 * k] for k in range(n_p)]
        loss_ref = refs[4 + 3 * n_p]
        outs = [refs[5 + 3 * n_p + 4 * k:9 + 3 * n_p + 4 * k] for k in range(n_p)]

        def column_sum(row):
            return total(sums_ref, (row, row + 1))

        def total(ref, rows=None):
            def part(j):
                return ref[j] if rows is None else ref[j, rows[0]:rows[1], :]
            acc = part(0)
            for j in range(1, N_DEV):
                acc = acc + part(j)
            return acc

        sink_rows = total(misc_ref, (ROW_DSINKS, ROW_DSINKS + 16))
        diag = (lax.broadcasted_iota(jnp.int32, (16, 128), 0) == lax.broadcasted_iota(jnp.int32, (16, 128), 1))
        grads = dict(
            w_spatial=total(wsp_ref), b_spatial=total(misc_ref, (ROW_DBSP, ROW_DBSP + A_GROUPS)),
            sinks=jnp.sum(jnp.where(diag, sink_rows, 0.0), axis=0, keepdims=True),
            norm_g=column_sum(SUM_NORM_G), ln_v_g=total(ln_ref, (0, 1)), ln_v_b=total(ln_ref, (1, 2)),
            final_norm_g=column_sum(SUM_FNG),
            b_ada=jnp.concatenate([column_sum(SUM_SHIFT), column_sum(SUM_SCALE), column_sum(SUM_GATE)], axis=1),
            b_ada_final=jnp.concatenate([column_sum(SUM_SHIFT_F), column_sum(SUM_SCALE_F)], axis=1))
        sq_err = jnp.sum(column_sum(SUM_SQ_ERR), axis=1, keepdims=True)
        loss_ref[...] = jnp.broadcast_to(sq_err * (0.5 / D_MODEL), (1, 128))
        for k, name in enumerate(SMALL_PARAMS):
            w_ref, m_ref, v_ref = wmv[k]
            g_ref, d_ref, nm_ref, nv_ref = outs[k]
            g_ref[...] = grads[name]
            d_ref[...], nm_ref[...], nv_ref[...] = _adamw(w_ref[...], grads[name], m_ref[...], v_ref[...])

    flat = [a for name in SMALL_PARAMS for a in params[name]]
    vmem = pl.BlockSpec(memory_space=pltpu.VMEM)
    out_shape = [SDS((1, 128), F32)] + [SDS(params[name][0].shape, F32) for name in SMALL_PARAMS for _ in range(4)]
    outs = pl.pallas_call(
        body, name="adam_small", in_specs=[vmem] * (4 + len(flat)), out_specs=[vmem] * len(out_shape),
        out_shape=out_shape, compiler_params=_params(),
    )(d_wsp, misc, d_ln, sums, *flat)
    return outs[0], {name: outs[1 + 4 * k:5 + 4 * k] for k, name in enumerate(SMALL_PARAMS)}


def kernel(x, c, w_ada, b_ada, norm_g, w_in, ln_v_g, ln_v_b, w_spatial, b_spatial, sinks, w_out, w_ada_final, b_ada_final, final_norm_g, loss_target, m_w_ada, m_b_ada, m_norm_g, m_w_in, m_ln_v_g, m_ln_v_b, m_w_spatial, m_b_spatial, m_sinks, m_w_out, m_w_ada_final, m_b_ada_final, m_final_norm_g, v_w_ada, v_b_ada, v_norm_g, v_w_in, v_ln_v_g, v_ln_v_b, v_w_spatial, v_b_spatial, v_sinks, v_w_out, v_w_ada_final, v_b_ada_final, v_final_norm_g):
    me = 4 * lax.axis_index("x") + 2 * lax.axis_index("y") + lax.axis_index("c")
    x2, tgt = x[0], loss_target[0]
    fng = final_norm_g.reshape(1, D_MODEL)

    n_ada, n_ada_f = w_ada.shape[2], w_ada_final.shape[1]
    cact, mod, mod_f = _ada_exchange(c, w_ada[0], b_ada.reshape(N_DEV, n_ada), w_ada_final,
                                     b_ada_final.reshape(N_DEV, n_ada_f))
    cact = cact.reshape(N_DEV, D_MODEL)
    mod, mod_f = mod.reshape(1, 3 * D_MODEL), mod_f.reshape(1, 2 * D_MODEL)
    shift, scale, gate = mod[:, :D_MODEL], mod[:, D_MODEL:2 * D_MODEL], mod[:, 2 * D_MODEL:]
    shift_f, scale_f = mod_f[:, :D_MODEL], mod_f[:, D_MODEL:]

    wt_f32, m_wt, v_wt = (jnp.swapaxes(a, 1, 2)[0] for a in (w_in, m_w_in, v_w_in))
    xi, yi = lax.axis_index("x"), lax.axis_index("y")
    chip_order = jnp.stack([2 * xi + yi, 2 * (1 - xi) + yi, 2 * xi + 1 - yi, 2 * (1 - xi) + 1 - yi]).astype(jnp.int32)
    wt_mine, wo_mine = _prep_weights(me.reshape(1), wt_f32, w_out[0])

    freqs = _rope_freqs()
    sinks_v = sinks.reshape(16)
    h, proj, wt = _gather_in_proj(chip_order, x2, shift, scale, norm_g, wt_mine)
    y, wo = _mixer_fwd(proj, freqs, ln_v_g, ln_v_b, w_spatial[0], b_spatial[0], sinks_v, wo_mine)
    dx1, do, dy, sums_o = _out_proj_loss(y, x2, tgt, wo, gate, shift_f, scale_f, fng)

    chip = (2 * lax.axis_index("x") + lax.axis_index("y")).reshape(1)
    pair_out, _ = _wgrad_pair("wgrad_out", y, do, 2048)
    dproj, d_ln, d_wsp, misc, parts_out = _mixer_bwd(
        me.reshape(1), proj, dy, freqs, ln_v_g, ln_v_b, w_spatial[0], b_spatial[0], sinks_v, pair_out)
    pair_in, (d_ln, d_wsp, misc) = _wgrad_pair(
        "wgrad_in", dproj, h, 1024, gathers=(d_ln, d_wsp.reshape(N_DEV * A_GROUPS * CHUNK, CHUNK), misc))
    grad_x, sums, parts_in = _in_proj_bwd(dproj, wt, x2, dx1, scale, norm_g, sums_o, pair_in)
    wt_leaves = [jnp.swapaxes(a[None], 1, 2)
                 for a in _adam_rows("adam_w_in", chip, pair_in, parts_in, wt_f32, m_wt, v_wt, 176)]
    w_out_leaves = [a[None] for a in _adam_rows("adam_w_out", chip, pair_out, parts_out, w_out[0], m_w_out[0], v_w_out[0], 64)]

    (sums,) = _all_gather("gather_sums", [sums], pltpu.VMEM)
    natural = dict(w_spatial=(A_GROUPS * CHUNK, CHUNK), b_spatial=(A_GROUPS, CHUNK), sinks=(1, 16), norm_g=(1, D_MODEL),
                   ln_v_g=(1, D_A), ln_v_b=(1, D_A), final_norm_g=(1, D_MODEL), b_ada=(1, 3 * D_MODEL),
                   b_ada_final=(1, 2 * D_MODEL))
    given = dict(
        w_spatial=(w_spatial, m_w_spatial, v_w_spatial), b_spatial=(b_spatial, m_b_spatial, v_b_spatial),
        sinks=(sinks, m_sinks, v_sinks), norm_g=(norm_g, m_norm_g, v_norm_g), ln_v_g=(ln_v_g, m_ln_v_g, v_ln_v_g),
        ln_v_b=(ln_v_b, m_ln_v_b, v_ln_v_b), final_norm_g=(final_norm_g, m_final_norm_g, v_final_norm_g),
        b_ada=(b_ada, m_b_ada, v_b_ada), b_ada_final=(b_ada_final, m_b_ada_final, v_b_ada_final))
    params = {name: tuple(a.reshape(natural[name]) for a in given[name]) for name in SMALL_PARAMS}
    params["sinks"] = tuple(jnp.pad(a, ((0, 0), (0, 128 - 16))) for a in params["sinks"])
    loss, small = _adam_small(d_wsp.reshape(N_DEV, A_GROUPS * CHUNK, CHUNK), misc.reshape(N_DEV, MISC_ROWS, 128),
                              d_ln.reshape(N_DEV, 8, D_A), sums, params)
    small["sinks"] = [a[:, :16] for a in small["sinks"]]
    small = {name: [a.reshape(given[name][0].shape) for a in small[name]] for name in SMALL_PARAMS}

    dmod_all = jnp.concatenate([sums[:, SUM_SHIFT], sums[:, SUM_SCALE], sums[:, SUM_GATE]], axis=1)
    dmod_f_all = jnp.concatenate([sums[:, SUM_SHIFT_F], sums[:, SUM_SCALE_F]], axis=1)
    dmod_mine = lax.dynamic_slice_in_dim(dmod_all, me * n_ada, n_ada, axis=1)
    dmod_f_mine = lax.dynamic_slice_in_dim(dmod_f_all, me * n_ada_f, n_ada_f, axis=1)
    ada = _adam_ada("adam_w_ada", cact, dmod_mine, w_ada[0], m_w_ada[0], v_w_ada[0])
    ada_f = _adam_ada("adam_w_ada_final", cact, dmod_f_mine, w_ada_final, m_w_ada_final, v_w_ada_final)

    def leaves(k):
        return (ada[k][None], small["b_ada"][k], small["norm_g"][k], wt_leaves[k], small["ln_v_g"][k],
                small["ln_v_b"][k], small["w_spatial"][k], small["b_spatial"][k], small["sinks"][k], w_out_leaves[k],
                ada_f[k], small["b_ada_final"][k], small["final_norm_g"][k])

    return (loss[0, 0], grad_x[None], *leaves(0), *leaves(1), *leaves(2), *leaves(3))
```

```python
import jax
import jax.numpy as jnp
from jax import lax
from jax.experimental import pallas as pl
from jax.experimental.pallas import tpu as pltpu

D_MODEL = 2048
D_IN = 5632
D_A = 1024
CHUNK = 128
A_GROUPS = 8
HEAD_DIM = 64
N_KV_HEADS = 4
N_DEV = 8
ROPE_THETA = 10000.0
NORM_EPS = 1e-5
ATTN_SCALE = HEAD_DIM ** -0.5

ADAM_LR = 0.001
ADAM_B1 = 0.9
ADAM_B2 = 0.999
ADAM_EPS = 1e-08
ADAM_WD = 0.01
ADAM_STEP = 10

OFF_U, OFF_VA, OFF_ZA, OFF_Q, OFF_K, OFF_V, OFF_ZB = 0, 1024, 2048, 3072, 4096, 4352, 4608

SUM_SHIFT, SUM_SCALE, SUM_NORM_G, SUM_GATE, SUM_SHIFT_F, SUM_SCALE_F, SUM_FNG, SUM_SQ_ERR = range(8)

V7X_VMEM_LIMIT_BYTES = 56 * 1024 * 1024

F32 = jnp.float32
BF16 = jnp.bfloat16
MESH = pl.DeviceIdType.MESH
SDS = jax.ShapeDtypeStruct
NT = (((1,), (1,)), ((), ()))
TN = (((0,), (0,)), ((), ()))


def _params(*semantics):
    return pltpu.CompilerParams(dimension_semantics=semantics or None, vmem_limit_bytes=V7X_VMEM_LIMIT_BYTES)


def _mesh_pos():
    return lax.axis_index("x"), lax.axis_index("y"), lax.axis_index("c")


def _sigmoid(z):
    return 1.0 / (1.0 + jnp.exp(-z))


def _adamw(w, g, m, v):
    m = ADAM_B1 * m + (1.0 - ADAM_B1) * g
    v = ADAM_B2 * v + (1.0 - ADAM_B2) * (g * g)
    m_hat = m / (1.0 - ADAM_B1 ** ADAM_STEP)
    v_hat = v / (1.0 - ADAM_B2 ** ADAM_STEP)
    delta = -ADAM_LR * (m_hat / (jnp.sqrt(v_hat) + ADAM_EPS) + ADAM_WD * w)
    return delta, m, v


def _all_gather(name, blocks, memory_space):
    n_arr = len(blocks)

    def body(*refs):
        ins, outs = refs[:n_arr], refs[n_arr:2 * n_arr]
        send_sems, recv_sems, local_sems = refs[2 * n_arr:]
        x, y, c = _mesh_pos()
        me, sibling = (x, y, c), (x, y, 1 - c)
        chips = [(1 - x, y), (x, 1 - y), (1 - x, 1 - y)]

        def slot(p):
            return 4 * p[0] + 2 * p[1] + p[2]

        def copy(a, k, block, to, src=None):
            dst = outs[a].at[slot(block)]
            return pltpu.make_async_remote_copy(
                src_ref=dst if src is None else src, dst_ref=dst,
                send_sem=send_sems.at[a, k], recv_sem=recv_sems.at[a, k],
                device_id=to, device_id_type=MESH)

        mine = [pltpu.make_async_copy(ins[a], outs[a].at[slot(me)], local_sems.at[a]) for a in range(n_arr)]
        for cp in mine:
            cp.start()
        first = []
        for a in range(n_arr):
            first.append(copy(a, 0, me, sibling, src=ins[a]))
            first += [copy(a, 1 + j, me, (*chip, c), src=ins[a]) for j, chip in enumerate(chips)]
        for cp in first:
            cp.start()
        passed = []
        for j, chip in enumerate(chips):
            for a in range(n_arr):
                copy(a, 1 + j, (*chip, c), me).wait_recv()
                fwd = copy(a, 4 + j, (*chip, c), sibling)
                fwd.start()
                passed.append(fwd)
        for a in range(n_arr):
            copy(a, 0, sibling, me).wait_recv()
            for j, chip in enumerate(chips):
                copy(a, 4 + j, (*chip, 1 - c), me).wait_recv()
        for cp in first + passed:
            cp.wait_send()
        for cp in mine:
            cp.wait()

    spec = pl.BlockSpec(memory_space=memory_space)
    return pl.pallas_call(
        body, name=name,
        out_shape=[SDS((N_DEV,) + b.shape, b.dtype) for b in blocks],
        in_specs=[spec] * n_arr, out_specs=[spec] * n_arr,
        scratch_shapes=[pltpu.SemaphoreType.DMA((n_arr, 7)), pltpu.SemaphoreType.DMA((n_arr, 7)),
                        pltpu.SemaphoreType.DMA((n_arr,))],
        compiler_params=_params(),
    )(*blocks)


def _ada_exchange(c, w_ada, b_ada8, w_ada_f, b_ada_f8):
    n1, n2 = w_ada.shape[1], w_ada_f.shape[1]

    def body(c_ref, w1_ref, b1_ref, w2_ref, b2_ref, cact_ref, mod_ref, modf_ref,
             cact_buf, res1, res2, send1, send2, sems_s, sems_r):
        x, y, c_pos = _mesh_pos()
        me = 4 * x + 2 * y + c_pos
        flips = [(k >> 2 & 1, k >> 1 & 1, k & 1) for k in range(1, N_DEV)]

        def peer(f):
            return (1 - x if f[0] else x, 1 - y if f[1] else y, 1 - c_pos if f[2] else c_pos)

        cv = c_ref[...]
        cact = cv * _sigmoid(cv)
        cact_buf[...] = cact
        cact_ref[me] = cact

        def rdma(phase, k, src, dst, f):
            return pltpu.make_async_remote_copy(src_ref=src, dst_ref=dst, send_sem=sems_s.at[phase, k],
                                                recv_sem=sems_r.at[phase, k], device_id=peer(f), device_id_type=MESH)

        gather = [rdma(0, k, cact_buf, cact_ref.at[me], f) for k, f in enumerate(flips)]
        for cp in gather:
            cp.start()
        for cp in gather:
            cp.wait_recv()
        for cp in gather:
            cp.wait_send()

        rid = lax.broadcasted_iota(jnp.int32, (N_DEV, D_MODEL), 0)
        rows = jnp.zeros((N_DEV, D_MODEL), F32)
        for j in range(N_DEV):
            rows = jnp.where(rid == j, jnp.broadcast_to(cact_ref[j], (N_DEV, D_MODEL)), rows)
        rows = rows.astype(BF16)
        res1[...] = jnp.dot(rows, w1_ref[...].astype(BF16), preferred_element_type=F32) + b1_ref[pl.ds(me, 1), :]
        res2[...] = jnp.dot(rows, w2_ref[...].astype(BF16), preferred_element_type=F32) + b2_ref[pl.ds(me, 1), :]
        for j in range(N_DEV):
            send1[j] = res1[pl.ds(j, 1), :]
            send2[j] = res2[pl.ds(j, 1), :]
        mod_ref[me] = send1[me]
        modf_ref[me] = send2[me]
        scatter = []
        for k, f in enumerate(flips):
            to = me ^ (k + 1)
            scatter.append(rdma(1, k, send1.at[to], mod_ref.at[me], f))
            scatter.append(rdma(2, k, send2.at[to], modf_ref.at[me], f))
        for cp in scatter:
            cp.start()
        for cp in scatter:
            cp.wait_recv()
        for cp in scatter:
            cp.wait_send()

    vmem = pl.BlockSpec(memory_space=pltpu.VMEM)
    return pl.pallas_call(
        body, name="ada_exchange",
        out_shape=[SDS((N_DEV, 1, D_MODEL), F32), SDS((N_DEV, 1, n1), F32), SDS((N_DEV, 1, n2), F32)],
        in_specs=[vmem] * 5, out_specs=[vmem] * 3,
        scratch_shapes=[pltpu.VMEM((1, D_MODEL), F32), pltpu.VMEM((N_DEV, n1), F32), pltpu.VMEM((N_DEV, n2), F32),
                        pltpu.VMEM((N_DEV, 1, n1), F32), pltpu.VMEM((N_DEV, 1, n2), F32),
                        pltpu.SemaphoreType.DMA((3, 7)), pltpu.SemaphoreType.DMA((3, 7))],
        compiler_params=_params(),
    )(c, w_ada, b_ada8, w_ada_f, b_ada_f8)


def _chip_scatter(pair_ref, parts_ref, send_sems, recv_sems):
    x, y, c = _mesh_pos()
    chips = [(1 - x, y), (x, 1 - y), (1 - x, 1 - y)]
    return [pltpu.make_async_remote_copy(
        src_ref=pair_ref.at[2 * cx + cy], dst_ref=parts_ref.at[j], send_sem=send_sems.at[j], recv_sem=recv_sems.at[j],
        device_id=(cx, cy, c), device_id_type=MESH) for j, (cx, cy) in enumerate(chips)]


def _scatter_scratch():
    return [pltpu.SemaphoreType.DMA((3,)), pltpu.SemaphoreType.DMA((3,))]


def _prep_weights(me, wt, w_out):
    steps = 4

    def body(me_ref, wt_ref, wo_ref, wtb_ref, wob_ref):
        wtb_ref[...] = wt_ref[...].astype(BF16)
        wob_ref[...] = wo_ref[...].astype(BF16)

    def rows(a, mine):
        blk = (a.shape[0] // steps, a.shape[1])
        return pl.BlockSpec(blk, (lambda i, me_ref: (steps * me_ref[0] + i, 0)) if mine else (lambda i, me_ref: (i, 0)))

    return pl.pallas_call(
        body, name="prep_weights",
        grid_spec=pltpu.PrefetchScalarGridSpec(
            num_scalar_prefetch=1, grid=(steps,),
            in_specs=[rows(wt, False), rows(w_out, False)], out_specs=[rows(wt, True), rows(w_out, True)]),
        out_shape=[SDS((N_DEV * wt.shape[0], D_MODEL), BF16), SDS((N_DEV * w_out.shape[0], D_MODEL), BF16)],
        compiler_params=_params("parallel"),
    )(me, wt, w_out)


class _InPlaceGather:
    def __init__(self, buf_ref, send_sems, recv_sems, relay=False):
        self.buf, self.send_sems, self.recv_sems, self.relay = buf_ref, send_sems, recv_sems, relay
        self.n = buf_ref.shape[0] // N_DEV
        x, y, c = _mesh_pos()
        self.me, self.sibling, self.core = (x, y, c), (x, y, 1 - c), c
        self.chips = [(1 - x, y), (x, 1 - y), (1 - x, 1 - y)]
        self.relay_from = (jnp.where(c == 0, 1 - x, x), jnp.where(c == 0, y, 1 - y), c)
        self.relay_to = (jnp.where(c == 0, x, 1 - x), jnp.where(c == 0, 1 - y, y), c)

    def copy(self, k, block, to):
        start = pl.multiple_of((4 * block[0] + 2 * block[1] + block[2]) * self.n, self.n)
        rows = self.buf.at[pl.ds(start, self.n)]
        return pltpu.make_async_remote_copy(src_ref=rows, dst_ref=rows, send_sem=self.send_sems.at[k],
                                            recv_sem=self.recv_sems.at[k], device_id=to, device_id_type=MESH)

    def start(self):
        self.copy(0, self.me, self.sibling).start()
        for j, chip in enumerate(self.chips[:2] if self.relay else self.chips):
            self.copy(1 + j, self.me, (*chip, self.core)).start()

    def relay_diagonal(self):
        self.copy(3, self.relay_from, self.relay_to).start()

    def pass_on(self, j):
        self.copy(1 + j, (*self.chips[j], self.core), self.me).wait_recv()
        self.copy(4 + j, (*self.chips[j], self.core), self.sibling).start()

    def wait_sibling(self, k):
        self.copy(k, self.sibling, self.me).wait_recv()

    def wait_sends(self):
        for k in range(7):
            self.copy(k, self.me, self.sibling).wait_send()


def _gather_scratch():
    return [pltpu.SemaphoreType.DMA((7,)), pltpu.SemaphoreType.DMA((7,))]


def _gather_in_proj(order, x, shift, scale, norm_g, wt_all):
    s = x.shape[0]
    th, tm = min(512, s), min(1024, s)
    nh, ni = s // th, s // tm
    tn = D_IN // 4
    steps = nh + 4 * ni

    def body(order_ref, x_ref, shift_ref, scale_ref, g_ref, wt_in, h_ref, proj_ref, wt_ref,
             h_scr, w_buf, load_sems, send_sems, recv_sems):
        g = pl.program_id(0)
        gather = _InPlaceGather(wt_ref, send_sems, recv_sems, relay=True)

        def tile_load(slot, chip):
            return pltpu.make_async_copy(wt_ref.at[pl.ds(pl.multiple_of(chip * tn, tn), tn)], w_buf.at[slot],
                                         load_sems.at[slot])

        @pl.when(g == 0)
        def _():
            gather.start()

        @pl.when(g < nh)
        def _():
            xv = x_ref[...]
            r = lax.rsqrt(jnp.mean(xv * xv, axis=-1, keepdims=True) + NORM_EPS)
            hb = (((xv * r) * g_ref[...]) * (1.0 + scale_ref[...]) + shift_ref[...]).astype(BF16)
            h_ref[...] = hb
            h_scr[pl.ds(pl.multiple_of(g * th, th), th), :] = hb

        @pl.when(g == nh - 1)
        def _():
            gather.wait_sibling(0)
            tile_load(0, order_ref[0]).start()

        @pl.when(g >= nh)
        def _():
            t, i = (g - nh) // ni, (g - nh) % ni

            @pl.when(i == 0)
            def _():
                tile_load(t % 2, order_ref[t]).wait()

            @pl.when((i == ni - 1) & (t == 0))
            def _():
                gather.pass_on(0)
                gather.pass_on(1)
                gather.relay_diagonal()

            @pl.when((i == ni // 2) & (t == 2))
            def _():
                gather.pass_on(2)

            for j in range(3):
                @pl.when((i == ni - 1) & (t == j))
                def _():
                    gather.wait_sibling(4 + j)
                    tile_load((j + 1) % 2, order_ref[j + 1]).start()

            lhs = h_scr[pl.ds(pl.multiple_of(i * tm, tm), tm), :]
            proj_ref[...] = lax.dot_general(w_buf[t % 2], lhs, NT, preferred_element_type=F32).T.astype(BF16)

        @pl.when(g == steps - 1)
        def _():
            gather.wait_sends()

    def h_tile(g, order_ref):
        return (jnp.minimum(g, nh - 1), 0)

    def proj_tile(g, order_ref):
        mm = jnp.maximum(g - nh, 0)
        return (mm % ni, order_ref[mm // ni])

    row = pl.BlockSpec((1, D_MODEL), lambda g, order_ref: (0, 0))
    hbm = pl.BlockSpec(memory_space=pl.ANY)
    return pl.pallas_call(
        body, name="gather_in_proj",
        grid_spec=pltpu.PrefetchScalarGridSpec(
            num_scalar_prefetch=1, grid=(steps,),
            in_specs=[pl.BlockSpec((th, D_MODEL), h_tile), row, row, row, hbm],
            out_specs=[pl.BlockSpec((th, D_MODEL), h_tile), pl.BlockSpec((tm, tn), proj_tile), hbm],
            scratch_shapes=[pltpu.VMEM((s, D_MODEL), BF16), pltpu.VMEM((2, tn, D_MODEL), BF16),
                            pltpu.SemaphoreType.DMA((2,)), *_gather_scratch()]),
        out_shape=[SDS((s, D_MODEL), BF16), SDS((s, D_IN), BF16), SDS(wt_all.shape, BF16)],
        input_output_aliases={5: 2},
        compiler_params=_params("arbitrary"),
    )(order, x, shift, scale, norm_g, wt_all)


def _rope_freqs():
    inv_freq = ROPE_THETA ** (-jnp.arange(0, HEAD_DIM, 2, dtype=F32) / HEAD_DIM)
    return jnp.tile(inv_freq, 4).reshape(1, 128)


class _RopeTables:
    def __init__(self, freq_ref, rows_ref, state_ref, last_ref):
        self.freq, self.rows, self.state, self.last = freq_ref, rows_ref, state_ref, last_ref

    def start(self, block, direction):
        ang = lax.broadcasted_iota(jnp.int32, (CHUNK, 128), 0).astype(F32) * self.freq[...]
        self.rows[0] = jnp.cos(ang)
        self.rows[1] = jnp.sin(ang)
        base = jnp.asarray(block * CHUNK, dtype=F32) * self.freq[...]
        turn = float(direction * CHUNK) * self.freq[...]
        self.state[0:1, :] = jnp.cos(base)
        self.state[1:2, :] = jnp.sin(base)
        self.state[2:3, :] = jnp.cos(turn)
        self.state[3:4, :] = jnp.sin(turn)

    def step(self):
        c, s, ct, st = (self.state[k:k + 1, :] for k in range(4))
        self.state[0:1, :] = c * ct - s * st
        self.state[1:2, :] = s * ct + c * st

    def tables(self):
        c, s = self.state[0:1, :], self.state[1:2, :]
        cos = c * self.rows[0] - s * self.rows[1]
        sin = s * self.rows[0] + c * self.rows[1]
        first_half = (lax.broadcasted_iota(jnp.int32, (1, 128), 1) & (HEAD_DIM - 1)) < HEAD_DIM // 2
        return cos, jnp.where(first_half, -sin, 0.0), jnp.where(first_half, 0.0, sin)

    def keep(self, tabs):
        for k in range(3):
            self.last[k] = tabs[k]

    def kept(self):
        return tuple(self.last[k] for k in range(3))


def _rope_scratch():
    return [pltpu.VMEM((2, CHUNK, 128), F32), pltpu.VMEM((8, 128), F32), pltpu.VMEM((3, CHUNK, 128), F32)]


def _rope(v, cos, sin_lo, sin_hi):
    width = v.shape[1]
    rep = (1, width // 128)
    return (v * jnp.tile(cos, rep) + pltpu.roll(v, width - 32, 1) * jnp.tile(sin_lo, rep)
            + pltpu.roll(v, 32, 1) * jnp.tile(sin_hi, rep))


def _rope_bwd(d, cos, sin_lo, sin_hi):
    width = d.shape[1]
    rep = (1, width // 128)
    return (d * jnp.tile(cos, rep) + pltpu.roll(d * jnp.tile(sin_lo, rep), 32, 1)
            + pltpu.roll(d * jnp.tile(sin_hi, rep), width - 32, 1))


def _layer_norm(v, g, b):
    mu = jnp.mean(v, axis=-1, keepdims=True)
    vc = v - mu
    rstd = lax.rsqrt(jnp.mean(vc * vc, axis=-1, keepdims=True) + NORM_EPS)
    vhat = vc * rstd
    return vhat * g + b, vhat, rstd


def _set_tril(w_ref, out_ref):
    t = lax.broadcasted_iota(jnp.int32, (CHUNK, CHUNK), 0)
    tp = lax.broadcasted_iota(jnp.int32, (CHUNK, CHUNK), 1)
    for g in range(A_GROUPS):
        out_ref[g] = jnp.where(tp <= t, w_ref[g], 0.0).astype(BF16)


def _bias_columns(b_ref, out_ref):
    for g in range(A_GROUPS):
        out_ref[g] = jnp.broadcast_to(b_ref[pl.ds(g, 1), :], (CHUNK, CHUNK)).T


def _from_prev():
    r = lax.broadcasted_iota(jnp.int32, (CHUNK, 4 * CHUNK), 0)
    i = lax.broadcasted_iota(jnp.int32, (CHUNK, 4 * CHUNK), 1) & (CHUNK - 1)
    return r > i


def _set_unfold_masks(mask_ref):
    prev = _from_prev()
    mask_ref[0] = jnp.where(prev, 1.0, 0.0).astype(BF16)
    mask_ref[1] = jnp.where(prev, 0.0, 1.0).astype(BF16)


def _fold_band(t, from_prev):
    return jnp.where(from_prev, t[:CHUNK], t[CHUNK:])


def _unfold_band(t, mask_ref):
    return jnp.concatenate([t * mask_ref[0], t * mask_ref[1]], axis=0)


def _low_lanes():
    return lax.broadcasted_iota(jnp.int32, (1, 128), 1) < HEAD_DIM


def _stack_heads(pair_a, pair_b):
    lo = _low_lanes()
    return jnp.concatenate([jnp.where(lo, pair_a, 0.0), jnp.where(lo, 0.0, pair_a),
                            jnp.where(lo, pair_b, 0.0), jnp.where(lo, 0.0, pair_b)], axis=0).astype(BF16)


def _heads_to_lanes(per_group):
    rows = [t[:, r * CHUNK:(r + 1) * CHUNK] for t in per_group for r in range(4)]
    return jnp.concatenate(rows, axis=0).T


def _dup_kv_head(band, gk):
    pair = band[:, (gk // 2) * 128:(gk // 2 + 1) * 128]
    lo = _low_lanes()
    one = jnp.where(lo if gk % 2 == 0 else jnp.logical_not(lo), pair, 0.0)
    return (one + pltpu.roll(one, HEAD_DIM, 1)).astype(BF16)


def _fold_kv_head(dup_grad, gk):
    both = dup_grad + pltpu.roll(dup_grad, HEAD_DIM, 1)
    lo = _low_lanes()
    return jnp.where(lo if gk % 2 == 0 else jnp.logical_not(lo), both, 0.0)


def _attn_probs(q_st, k_dup, sink_row, from_prev, first_block):
    s = lax.dot_general(k_dup, q_st, NT, preferred_element_type=F32)
    no_prev = jnp.where(first_block, -jnp.inf, 0.0)
    s = jnp.where(from_prev, s[:CHUNK] + no_prev, s[CHUNK:])
    m = jnp.maximum(jnp.max(s, axis=0, keepdims=True), sink_row)
    p = jnp.exp(s - m)
    e_sink = jnp.exp(sink_row - m)
    inv = 1.0 / (jnp.sum(p, axis=0, keepdims=True) + e_sink)
    return p * inv, e_sink * inv


def _sink_row(sinks_ref, gk):
    return jnp.concatenate([jnp.full((1, CHUNK), sinks_ref[4 * gk + r], F32) for r in range(4)], axis=1)


def _mixer_specs(nb, rev):
    def blk(i):
        return nb - 1 - i if rev else i

    def prev(i):
        return jnp.maximum(blk(i) - 1, 0)

    return dict(
        cur=pl.BlockSpec((CHUNK, D_IN), lambda i, *_: (blk(i), 0)),
        prev_kv=pl.BlockSpec((CHUNK, 2 * 256), lambda i, *_: (prev(i), OFF_K // 512)),
        freq=pl.BlockSpec((1, 128), lambda i, *_: (0, 0)),
        vec=pl.BlockSpec((1, D_A), lambda i, *_: (0, 0)),
        wsp=pl.BlockSpec((A_GROUPS, CHUNK, CHUNK), lambda i, *_: (0, 0, 0)),
        bsp=pl.BlockSpec((A_GROUPS, CHUNK), lambda i, *_: (0, 0)),
        smem=pl.BlockSpec(memory_space=pltpu.SMEM),
        blk=blk,
    )


def _mixer_fwd(proj, freqs, ln_g, ln_b, w_sp, b_sp, sinks, wo_all):
    s = proj.shape[0]
    nb = s // CHUNK
    sp = _mixer_specs(nb, rev=False)

    def body(cur_ref, pkv_ref, freq_ref, lg_ref, lb_ref, w_ref, b_ref, sinks_ref, wo_in, y_ref, wo_ref,
             bcol, wtril, mask, rope_rows, rope_state, rope_last, send_sems, recv_sems):
        i = pl.program_id(0)
        gather = _InPlaceGather(wo_ref, send_sems, recv_sems)
        rope = _RopeTables(freq_ref, rope_rows, rope_state, rope_last)

        @pl.when(i == 0)
        def _():
            gather.start()
            _bias_columns(b_ref, bcol)
            _set_tril(w_ref, wtril)
            _set_unfold_masks(mask)
            rope.start(-1, 1)
            rope_last[...] = jnp.zeros_like(rope_last)

        @pl.when(i == (7 * nb) // 8)
        def _():
            for j in range(3):
                gather.pass_on(j)

        vln, _, _ = _layer_norm(cur_ref[:, OFF_VA:OFF_ZA].astype(F32), lg_ref[...], lb_ref[...])
        vln = vln.astype(BF16)

        def gating_group(g):
            cols = slice(g * 128, (g + 1) * 128)
            sg = jnp.dot(wtril[g], vln[:, cols], preferred_element_type=F32) + bcol[g]
            u = cur_ref[:, OFF_U + g * 128:OFF_U + (g + 1) * 128].astype(F32)
            z = cur_ref[:, OFF_ZA + g * 128:OFF_ZA + (g + 1) * 128].astype(F32)
            y_ref[:, cols] = (u * sg * (z * _sigmoid(z))).astype(BF16)

        rope.step()
        cur_t, prev_t = rope.tables(), rope.kept()
        rope.keep(cur_t)
        qr = _rope(cur_ref[:, OFF_Q:OFF_K].astype(F32), *cur_t) * ATTN_SCALE
        kr = jnp.concatenate([_rope(pkv_ref[:, 0:256].astype(F32), *prev_t),
                              _rope(cur_ref[:, OFF_K:OFF_V].astype(F32), *cur_t)], axis=0)
        v_t = jnp.concatenate([pkv_ref[:, 256:512], cur_ref[:, OFF_V:OFF_ZB]], axis=0).astype(F32).T.astype(BF16)
        outs = []
        from_prev = _from_prev()
        for gk in range(N_KV_HEADS):
            q_st = _stack_heads(qr[:, (2 * gk) * 128:(2 * gk + 1) * 128], qr[:, (2 * gk + 1) * 128:(2 * gk + 2) * 128])
            probs, _ = _attn_probs(q_st, _dup_kv_head(kr, gk), _sink_row(sinks_ref, gk), from_prev, i == 0)
            gating_group(2 * gk)
            outs.append(jnp.dot(v_t[gk * HEAD_DIM:(gk + 1) * HEAD_DIM], _unfold_band(probs.astype(BF16), mask),
                                preferred_element_type=F32))
            gating_group(2 * gk + 1)
        zb = cur_ref[:, OFF_ZB:D_IN].astype(F32)
        y_ref[:, D_A:D_MODEL] = (_heads_to_lanes(outs) * (zb * _sigmoid(zb))).astype(BF16)

        @pl.when(i == nb - 1)
        def _():
            gather.wait_sibling(0)
            for j in range(3):
                gather.wait_sibling(4 + j)
            gather.wait_sends()

    hbm = pl.BlockSpec(memory_space=pl.ANY)
    return pl.pallas_call(
        body, name="mixer_fwd", grid=(nb,),
        in_specs=[sp["cur"], sp["prev_kv"], sp["freq"], sp["vec"], sp["vec"], sp["wsp"], sp["bsp"], sp["smem"], hbm],
        out_specs=[pl.BlockSpec((CHUNK, D_MODEL), lambda i: (i, 0)), hbm],
        out_shape=[SDS((s, D_MODEL), BF16), SDS(wo_all.shape, wo_all.dtype)],
        scratch_shapes=[pltpu.VMEM((A_GROUPS, CHUNK, CHUNK), F32), pltpu.VMEM((A_GROUPS, CHUNK, CHUNK), BF16),
                        pltpu.VMEM((2, CHUNK, 4 * CHUNK), BF16), *_rope_scratch(), *_gather_scratch()],
        input_output_aliases={8: 1},
        compiler_params=_params("arbitrary"),
    )(proj, proj, freqs, ln_g, ln_b, w_sp, b_sp, sinks, wo_all)


def _out_proj_loss(y, x, target, wo, gate, shift_f, scale_f, fng):
    s = y.shape[0]
    tm, tr = 256, 128
    nt = s // tm

    def body(y_ref, x_ref, t_ref, wo_ref, gate_ref, sh_ref, sc_ref, g_ref, dx1_ref, do_ref, dy_ref, sums_ref,
             do_last, do_work):
        i = pl.program_id(0)

        @pl.when(i == 0)
        def _():
            sums_ref[...] = jnp.zeros_like(sums_ref)
            do_last[...] = jnp.zeros_like(do_last)

        do_work[...] = do_last[...]
        o = jnp.dot(y_ref[...], wo_ref[...], preferred_element_type=F32)
        gate, g, sh = gate_ref[...], g_ref[...], sh_ref[...]
        one_sc = 1.0 + sc_ref[...]
        cs, inv_d = g * one_sc, 1.0 / D_MODEL

        def rowsum(v):
            return jnp.sum(v, axis=0, keepdims=True)

        sums = [jnp.zeros((1, D_MODEL), F32) for _ in range(4)]
        for c in range(tm // tr):
            rows = slice(c * tr, (c + 1) * tr)
            oc = o[rows]
            x1 = x_ref[rows, :] + gate * oc
            r = lax.rsqrt(jnp.sum(x1 * x1, axis=-1, keepdims=True) * inv_d + NORM_EPS)
            x1n = x1 * r
            diff = x1n * cs + sh - t_ref[rows, :]
            w = diff * x1n
            lane_sum = jnp.sum(w * cs, axis=-1, keepdims=True)
            dx1 = (diff * cs) * (r * inv_d) - x1n * (r * lane_sum * (inv_d * inv_d))
            dx1_ref[rows, :] = dx1
            do = (dx1 * gate).astype(BF16)
            do_ref[rows, :] = do
            do_last[rows, :] = do
            for k, v in enumerate((dx1 * oc, diff, w, diff * diff)):
                sums[k] = sums[k] + rowsum(v)
        live = jnp.where(i < nt, 1.0, 0.0)
        for row, v in ((SUM_GATE, sums[0]), (SUM_SHIFT_F, inv_d * sums[1]), (SUM_SCALE_F, inv_d * (sums[2] * g)),
                       (SUM_FNG, inv_d * (sums[2] * one_sc)), (SUM_SQ_ERR, sums[3])):
            sums_ref[row:row + 1, :] += live * v
        dy_ref[...] = lax.dot_general(do_work[...], wo_ref[...], NT, preferred_element_type=F32).astype(BF16)

    tile = pl.BlockSpec((tm, D_MODEL), lambda i: (jnp.minimum(i, nt - 1), 0))
    row = pl.BlockSpec((1, D_MODEL), lambda i: (0, 0))
    return pl.pallas_call(
        body, name="out_proj_loss", grid=(nt + 1,),
        in_specs=[tile, tile, tile, pl.BlockSpec((D_MODEL, D_MODEL), lambda i: (0, 0)), row, row, row, row],
        out_specs=[tile, tile, pl.BlockSpec((tm, D_MODEL), lambda i: (jnp.maximum(i - 1, 0), 0)),
                   pl.BlockSpec((8, D_MODEL), lambda i: (0, 0))],
        out_shape=[SDS((s, D_MODEL), F32), SDS((s, D_MODEL), BF16), SDS((s, D_MODEL), BF16), SDS((8, D_MODEL), F32)],
        scratch_shapes=[pltpu.VMEM((tm, D_MODEL), BF16), pltpu.VMEM((tm, D_MODEL), BF16)],
        compiler_params=_params("arbitrary"),
    )(y, x, target, wo, gate, shift_f, scale_f, fng)


ROW_DBSP, ROW_DSINKS, MISC_ROWS = 0, 8, 32


def _mixer_bwd(me, proj, dy, freqs, ln_g, ln_b, w_sp, b_sp, sinks, pair):
    s = proj.shape[0]
    nb = s // CHUNK
    sp = _mixer_specs(nb, rev=True)

    def body(me_ref, cur_ref, pkv_ref, dy_ref, freq_ref, lg_ref, lb_ref, w_ref, b_ref, sinks_ref, pair_ref,
             dproj_ref, dln_ref, dw_ref, misc_ref, parts_ref, bcol, wtril, dbcol, carry, mask, rope_rows, rope_state,
             rope_last, send_sems, recv_sems):
        i = pl.program_id(0)
        block = nb - 1 - i
        rope = _RopeTables(freq_ref, rope_rows, rope_state, rope_last)

        @pl.when(i == 0)
        def _():
            for cp in _chip_scatter(pair_ref, parts_ref, send_sems, recv_sems):
                cp.start()
            _bias_columns(b_ref, bcol)
            _set_tril(w_ref, wtril)
            _set_unfold_masks(mask)
            rope.start(nb - 1, -1)
            rope.keep(rope.tables())
            dbcol[...] = jnp.zeros_like(dbcol)
            carry[...] = jnp.zeros_like(carry)
            dln_ref[...] = jnp.zeros_like(dln_ref)
            dw_ref[...] = jnp.zeros_like(dw_ref)
            misc_ref[...] = jnp.zeros_like(misc_ref)

        vln, vhat, rstd = _layer_norm(cur_ref[:, OFF_VA:OFF_ZA].astype(F32), lg_ref[...], lb_ref[...])
        vln = vln.astype(BF16)
        d_vln = []

        def gating_group(g):
            cols = slice(g * 128, (g + 1) * 128)
            w_g = wtril[g]
            sg = jnp.dot(w_g, vln[:, cols], preferred_element_type=F32) + bcol[g]
            u = cur_ref[:, OFF_U + g * 128:OFF_U + (g + 1) * 128].astype(F32)
            z = cur_ref[:, OFF_ZA + g * 128:OFF_ZA + (g + 1) * 128].astype(F32)
            dya = dy_ref[:, cols].astype(F32)
            sig = _sigmoid(z)
            d_ya = dya * (z * sig)
            dproj_ref[:, OFF_ZA + g * 128:OFF_ZA + (g + 1) * 128] = (
                dya * (u * sg) * (sig * (1.0 + z * (1.0 - sig)))).astype(BF16)
            dproj_ref[:, OFF_U + g * 128:OFF_U + (g + 1) * 128] = (d_ya * sg).astype(BF16)
            d_s = d_ya * u
            dbcol[g] += d_s
            d_sb = d_s.astype(BF16)
            dw_ref[g] += lax.dot_general(d_sb, vln[:, cols], NT, preferred_element_type=F32)
            d_vln.append(lax.dot_general(w_g, d_sb, TN, preferred_element_type=F32))

        cur_t = rope.kept()
        rope.step()
        prev_t = rope.tables()
        rope.keep(prev_t)
        band_t = tuple(jnp.concatenate([p, c], axis=0) for p, c in zip(prev_t, cur_t))
        qr = _rope(cur_ref[:, OFF_Q:OFF_K].astype(F32), *cur_t) * ATTN_SCALE
        kr = jnp.concatenate([_rope(pkv_ref[:, 0:256].astype(F32), *prev_t),
                              _rope(cur_ref[:, OFF_K:OFF_V].astype(F32), *cur_t)], axis=0)
        vb = jnp.concatenate([pkv_ref[:, 256:512], cur_ref[:, OFF_V:OFF_ZB]], axis=0).astype(F32)
        k_t, v_t = (kr.T * ATTN_SCALE).astype(BF16), vb.T.astype(BF16)
        zb = cur_ref[:, OFF_ZB:D_IN].astype(F32)
        dyb = dy_ref[:, D_A:D_MODEL].astype(F32)
        sig = _sigmoid(zb)
        d_yb = dyb * (zb * sig)
        outs, dqs = [], []
        dk_pairs = [jnp.zeros((2 * CHUNK, 128), F32) for _ in range(2)]
        dv_pairs = [jnp.zeros((2 * CHUNK, 128), F32) for _ in range(2)]
        from_prev = _from_prev()
        for gk in range(N_KV_HEADS):
            heads = slice(gk * HEAD_DIM, (gk + 1) * HEAD_DIM)
            q_st = _stack_heads(qr[:, (2 * gk) * 128:(2 * gk + 1) * 128], qr[:, (2 * gk + 1) * 128:(2 * gk + 2) * 128])
            k_dup, v_dup = _dup_kv_head(kr, gk), _dup_kv_head(vb, gk)
            probs, p_sink = _attn_probs(q_st, k_dup, _sink_row(sinks_ref, gk), from_prev, block == 0)
            probs_b = _unfold_band(probs.astype(BF16), mask)
            outs.append(jnp.dot(v_t[heads], probs_b, preferred_element_type=F32))
            do_st = _stack_heads(d_yb[:, (2 * gk) * 128:(2 * gk + 1) * 128], d_yb[:, (2 * gk + 1) * 128:(2 * gk + 2) * 128])
            dp = _fold_band(lax.dot_general(v_dup, do_st, NT, preferred_element_type=F32), from_prev)
            delta = jnp.sum(probs * dp, axis=0, keepdims=True)
            ds = _unfold_band((probs * (dp - delta)).astype(BF16), mask)
            gating_group(2 * gk)
            d_sink = -p_sink * delta
            for r in range(4):
                row = ROW_DSINKS + 4 * gk + r
                misc_ref[row:row + 1, :] += jnp.broadcast_to(
                    jnp.sum(d_sink[:, r * CHUNK:(r + 1) * CHUNK], axis=1, keepdims=True), (1, 128))
            dqs.append(jnp.dot(k_t[heads], ds, preferred_element_type=F32))
            dk_pairs[gk // 2] += _fold_kv_head(jnp.dot(ds, q_st, preferred_element_type=F32), gk)
            dv_pairs[gk // 2] += _fold_kv_head(jnp.dot(probs_b, do_st, preferred_element_type=F32), gk)
            gating_group(2 * gk + 1)
        d_vln = jnp.concatenate(d_vln, axis=1)
        dln_ref[0:1, :] += jnp.sum(d_vln * vhat, axis=0, keepdims=True)
        dln_ref[1:2, :] += jnp.sum(d_vln, axis=0, keepdims=True)
        d_vhat = d_vln * lg_ref[...]
        d_va = rstd * (d_vhat - jnp.mean(d_vhat, axis=-1, keepdims=True)
                       - vhat * jnp.mean(d_vhat * vhat, axis=-1, keepdims=True))
        dproj_ref[:, OFF_VA:OFF_ZA] = d_va.astype(BF16)
        dproj_ref[:, OFF_ZB:D_IN] = (dyb * _heads_to_lanes(outs) * (sig * (1.0 + zb * (1.0 - sig)))).astype(BF16)
        dproj_ref[:, OFF_Q:OFF_K] = _rope_bwd(_heads_to_lanes(dqs), *cur_t).astype(BF16)
        dk_band = _rope_bwd(jnp.concatenate(dk_pairs, axis=1), *band_t)
        dv_band = jnp.concatenate(dv_pairs, axis=1)
        dproj_ref[:, OFF_K:OFF_V] = (dk_band[CHUNK:] + carry[:, 0:256]).astype(BF16)
        dproj_ref[:, OFF_V:OFF_ZB] = (dv_band[CHUNK:] + carry[:, 256:512]).astype(BF16)
        carry[:, 0:256] = dk_band[:CHUNK]
        carry[:, 256:512] = dv_band[:CHUNK]

        @pl.when(i == nb - 1)
        def _():
            t = lax.broadcasted_iota(jnp.int32, (CHUNK, CHUNK), 0)
            tp = lax.broadcasted_iota(jnp.int32, (CHUNK, CHUNK), 1)
            for g in range(A_GROUPS):
                dw_ref[g] = jnp.where(tp <= t, dw_ref[g], 0.0)
                misc_ref[pl.ds(ROW_DBSP + g, 1), :] = jnp.sum(dbcol[g].T, axis=0, keepdims=True)
            scatter = _chip_scatter(pair_ref, parts_ref, send_sems, recv_sems)
            for cp in scatter:
                cp.wait_recv()
            for cp in scatter:
                cp.wait_send()

    blk = sp["blk"]
    hbm = pl.BlockSpec(memory_space=pl.ANY)
    return pl.pallas_call(
        body, name="mixer_bwd",
        grid_spec=pltpu.PrefetchScalarGridSpec(
            num_scalar_prefetch=1, grid=(nb,),
            in_specs=[sp["cur"], sp["prev_kv"], pl.BlockSpec((CHUNK, D_MODEL), lambda i, me_ref: (blk(i), 0)),
                      sp["freq"], sp["vec"], sp["vec"], sp["wsp"], sp["bsp"], sp["smem"], hbm],
            out_specs=[pl.BlockSpec((CHUNK, D_IN), lambda i, me_ref: (blk(i), 0)),
                       pl.BlockSpec((8, D_A), lambda i, me_ref: (me_ref[0], 0)),
                       pl.BlockSpec((A_GROUPS, CHUNK, CHUNK), lambda i, me_ref: (me_ref[0], 0, 0)),
                       pl.BlockSpec((MISC_ROWS, 128), lambda i, me_ref: (me_ref[0], 0)), hbm],
            scratch_shapes=[pltpu.VMEM((A_GROUPS, CHUNK, CHUNK), F32), pltpu.VMEM((A_GROUPS, CHUNK, CHUNK), BF16),
                            pltpu.VMEM((A_GROUPS, CHUNK, CHUNK), F32), pltpu.VMEM((CHUNK, 512), F32), pltpu.VMEM((2, CHUNK, 4 * CHUNK), BF16),
                            *_rope_scratch(), *_scatter_scratch()]),
        out_shape=[SDS((s, D_IN), BF16), SDS((N_DEV * 8, D_A), F32), SDS((N_DEV * A_GROUPS, CHUNK, CHUNK), F32),
                   SDS((N_DEV * MISC_ROWS, 128), F32), SDS((3,) + pair.shape[1:], pair.dtype)],
        compiler_params=_params("arbitrary"),
    )(me, proj, proj, dy, freqs, ln_g, ln_b, w_sp, b_sp, sinks, pair)


def _wgrad_pair(name, a, b, bt, gathers=()):
    s, m = a.shape
    n = b.shape[1]
    bm, half = m // 4, m // 8
    bt = min(bt, s)
    steps = s // bt
    last = 4 * steps
    n_g = len(gathers)

    def body(*refs):
        a_ref, b_ref = refs[:2]
        out_ref, bufs = refs[2 + n_g], refs[3 + n_g:3 + 2 * n_g]
        acc, kept, got, sent, send_sems, recv_sems = refs[3 + 2 * n_g:9 + 2 * n_g]
        sems = refs[9 + 2 * n_g:]
        g = pl.program_id(0)
        tile, t = g // steps, g % steps
        mx, my, mc = _mesh_pos()
        jobs = [_InPlaceGather(bufs[k], sems[2 * k], sems[2 * k + 1]) for k in range(n_g)]

        def exchange(q):
            return pltpu.make_async_remote_copy(src_ref=sent, dst_ref=got.at[q % 2], send_sem=send_sems.at[q],
                                                recv_sem=recv_sems.at[q], device_id=(mx, my, 1 - mc),
                                                device_id_type=MESH)

        @pl.when(g == 0)
        def _():
            for job in jobs:
                job.start()

        @pl.when(g == 2 * steps)
        def _():
            for job in jobs:
                for j in range(3):
                    job.pass_on(j)

        @pl.when(g < last)
        def _():
            @pl.when(t == 0)
            def _():
                acc[...] = lax.dot_general(a_ref[...], b_ref[...], TN, preferred_element_type=F32)

            @pl.when(t > 0)
            def _():
                acc[...] += lax.dot_general(a_ref[...], b_ref[...], TN, preferred_element_type=F32)

            @pl.when(t == steps - 1)
            def _():
                @pl.when(tile > 0)
                def _():
                    exchange(tile - 1).wait_send()

                kept[tile % 2] = acc[pl.ds(pl.multiple_of(mc * half, 8), half), :].astype(BF16)
                sent[...] = acc[pl.ds(pl.multiple_of((1 - mc) * half, 8), half), :].astype(BF16)
                exchange(tile).start()

        @pl.when((t == 0) & (g > 0))
        def _():
            q = tile - 1
            exchange(q).wait_recv()
            out_ref[0] = (kept[q % 2].astype(F32) + got[q % 2].astype(F32)).astype(BF16)

        @pl.when(g == last)
        def _():
            exchange(3).wait_send()
            for job in jobs:
                job.wait_sibling(0)
                for j in range(3):
                    job.wait_sibling(4 + j)
                job.wait_sends()

    def a_tile(g):
        gg = jnp.minimum(g, last - 1)
        return (gg % steps, gg // steps)

    def b_tile(g):
        return (jnp.minimum(g, last - 1) % steps, 0)

    hbm = pl.BlockSpec(memory_space=pl.ANY)
    outs = pl.pallas_call(
        body, name=name, grid=(last + 1,),
        in_specs=[pl.BlockSpec((bt, bm), a_tile), pl.BlockSpec((bt, n), b_tile)] + [hbm] * n_g,
        out_specs=[pl.BlockSpec((1, half, n), lambda g: (jnp.maximum(g - 1, 0) // steps, 0, 0))] + [hbm] * n_g,
        out_shape=[SDS((4, half, n), BF16)] + [SDS(gb.shape, gb.dtype) for gb in gathers],
        scratch_shapes=[pltpu.VMEM((bm, n), F32), pltpu.VMEM((2, half, n), BF16), pltpu.VMEM((2, half, n), BF16),
                        pltpu.VMEM((half, n), BF16), pltpu.SemaphoreType.DMA((4,)), pltpu.SemaphoreType.DMA((4,))]
        + _gather_scratch() * n_g,
        input_output_aliases={2 + k: 1 + k for k in range(n_g)},
        compiler_params=_params("arbitrary"),
    )(a, b, *gathers)
    return outs[0], outs[1:]


def _in_proj_bwd(dproj, wt, x, dx1, scale, norm_g, sums_o, pair):
    s = x.shape[0]
    tm, tk, tr = min(1024, s), 1536, 64
    ksteps = pl.cdiv(D_IN, tk)
    k_last = D_IN - (ksteps - 1) * tk

    def body(dp_ref, wt_ref, x_hbm, dx1_hbm, sc_ref, g_ref, so_ref, pair_ref, gx_ref, sums_ref, parts_ref, x_buf,
             dx1_buf, tile_sems, send_sems, recv_sems):
        i, k = pl.program_id(0), pl.program_id(1)

        def tile_copies():
            rows = pl.ds(pl.multiple_of(i * tm, tm), tm)
            return (pltpu.make_async_copy(x_hbm.at[rows], x_buf, tile_sems.at[0]),
                    pltpu.make_async_copy(dx1_hbm.at[rows], dx1_buf, tile_sems.at[1]))

        @pl.when((i == 0) & (k == 0))
        def _():
            for cp in _chip_scatter(pair_ref, parts_ref, send_sems, recv_sems):
                cp.start()
            sums_ref[...] = so_ref[...]

        @pl.when(k == 0)
        def _():
            for cp in tile_copies():
                cp.start()
            gx_ref[...] = jnp.dot(dp_ref[...], wt_ref[...], preferred_element_type=F32)

        @pl.when((k > 0) & (k < ksteps - 1))
        def _():
            gx_ref[...] += jnp.dot(dp_ref[...], wt_ref[...], preferred_element_type=F32)

        @pl.when(k == ksteps - 1)
        def _():
            gx_ref[...] += jnp.dot(dp_ref[:, :k_last], wt_ref[:k_last, :], preferred_element_type=F32)

        @pl.when(k == ksteps - 1)
        def _():
            for cp in tile_copies():
                cp.wait()
            one_sc, g = 1.0 + sc_ref[...], g_ref[...]
            cs = one_sc * g

            def chunk(j, sums):
                rows = pl.ds(pl.multiple_of(j * tr, tr), tr)
                dh, xv = gx_ref[rows, :], x_buf[rows, :]
                dhx = dh * xv
                r = lax.rsqrt(jnp.sum(xv * xv, axis=-1, keepdims=True) * (1.0 / D_MODEL) + NORM_EPS)
                coef = (r * r * r) * (jnp.sum(dhx * cs, axis=-1, keepdims=True) * (1.0 / D_MODEL))
                gx_ref[rows, :] = dx1_buf[rows, :] + r * (dh * cs) - xv * coef
                return (sums[0] + jnp.sum(dh, axis=0, keepdims=True), sums[1] + jnp.sum(dhx * r, axis=0, keepdims=True))

            zero = jnp.zeros((1, D_MODEL), F32)
            sums = lax.fori_loop(0, tm // tr, chunk, (zero, zero))
            sums_ref[SUM_SHIFT:SUM_SHIFT + 1, :] += sums[0]
            sums_ref[SUM_SCALE:SUM_SCALE + 1, :] += sums[1] * g
            sums_ref[SUM_NORM_G:SUM_NORM_G + 1, :] += sums[1] * one_sc

        @pl.when((i == s // tm - 1) & (k == ksteps - 1))
        def _():
            scatter = _chip_scatter(pair_ref, parts_ref, send_sems, recv_sems)
            for cp in scatter:
                cp.wait_recv()
            for cp in scatter:
                cp.wait_send()

    row = pl.BlockSpec((1, D_MODEL), lambda i, k: (0, 0))
    hbm = pl.BlockSpec(memory_space=pl.ANY)
    return pl.pallas_call(
        body, name="in_proj_bwd", grid=(s // tm, ksteps),
        in_specs=[pl.BlockSpec((tm, tk), lambda i, k: (i, k)), pl.BlockSpec((tk, D_MODEL), lambda i, k: (k, 0)),
                  hbm, hbm, row, row, pl.BlockSpec((8, D_MODEL), lambda i, k: (0, 0)), hbm],
        out_specs=[pl.BlockSpec((tm, D_MODEL), lambda i, k: (i, 0)), pl.BlockSpec((8, D_MODEL), lambda i, k: (0, 0)),
                   hbm],
        out_shape=[SDS((s, D_MODEL), F32), SDS((8, D_MODEL), F32), SDS((3,) + pair.shape[1:], pair.dtype)],
        scratch_shapes=[pltpu.VMEM((tm, D_MODEL), F32), pltpu.VMEM((tm, D_MODEL), F32),
                        pltpu.SemaphoreType.DMA((2,)), *_scatter_scratch()],
        compiler_params=_params("arbitrary", "arbitrary"),
    )(dproj, wt, x, dx1, scale, norm_g, sums_o, pair)


def _sum_chips(own_ref, parts_ref):
    return ((own_ref[0].astype(F32) + parts_ref[0].astype(F32)) + parts_ref[1].astype(F32)) + parts_ref[2].astype(F32)


def _adam_rows(name, chip, pair, parts, w, m, v, tr):
    rows = w.shape[0]

    def body(chip_ref, own_ref, p_ref, w_ref, m_ref, v_ref, g_ref, d_ref, nm_ref, nv_ref):
        g = _sum_chips(own_ref, p_ref)
        g_ref[...] = g
        d_ref[...], nm_ref[...], nv_ref[...] = _adamw(w_ref[...], g, m_ref[...], v_ref[...])

    blk = pl.BlockSpec((tr, D_MODEL), lambda j, chip_ref: (j, 0))
    return pl.pallas_call(
        body, name=name,
        grid_spec=pltpu.PrefetchScalarGridSpec(
            num_scalar_prefetch=1, grid=(rows // tr,),
            in_specs=[pl.BlockSpec((1, tr, D_MODEL), lambda j, chip_ref: (chip_ref[0], j, 0)),
                      pl.BlockSpec((3, tr, D_MODEL), lambda j, chip_ref: (0, j, 0)), blk, blk, blk],
            out_specs=[blk] * 4),
        out_shape=[SDS(w.shape, F32)] * 4, compiler_params=_params("parallel"),
    )(chip, pair, parts, w, m, v)


def _adam_ada(name, cact, dmod, w, m, v):
    n = w.shape[1]
    tr = 512

    def body(c_ref, dm_ref, w_ref, m_ref, v_ref, g_ref, d_ref, nm_ref, nv_ref):
        pad_c = jnp.concatenate([c_ref[...], jnp.zeros_like(c_ref)], axis=0).astype(BF16)
        pad_d = jnp.concatenate([dm_ref[...], jnp.zeros_like(dm_ref)], axis=0).astype(BF16)
        g = lax.dot_general(pad_c, pad_d, TN, preferred_element_type=F32)
        g_ref[...] = g
        d_ref[...], nm_ref[...], nv_ref[...] = _adamw(w_ref[...], g, m_ref[...], v_ref[...])

    blk = pl.BlockSpec((tr, n), lambda j: (j, 0))
    return pl.pallas_call(
        body, name=name, grid=(D_MODEL // tr,),
        in_specs=[pl.BlockSpec((N_DEV, tr), lambda j: (0, j)), pl.BlockSpec((N_DEV, n), lambda j: (0, 0)),
                  blk, blk, blk],
        out_specs=[blk] * 4, out_shape=[SDS(w.shape, F32)] * 4,
        compiler_params=_params("parallel"),
    )(cact, dmod, w, m, v)


SMALL_PARAMS = ("w_spatial", "b_spatial", "sinks", "norm_g", "ln_v_g", "ln_v_b", "final_norm_g", "b_ada", "b_ada_final")


def _adam_small(d_wsp, misc, d_ln, sums, params):
    n_p = len(SMALL_PARAMS)

    def body(*refs):
        wsp_ref, misc_ref, ln_ref, sums_ref = refs[:4]
        wmv = [refs[4 + 3 * k:7 + 3 * k] for k in range(n_p)]
        loss_ref = refs[4 + 3 * n_p]
        outs = [refs[5 + 3 * n_p + 4 * k:9 + 3 * n_p + 4 * k] for k in range(n_p)]

        def column_sum(row):
            return total(sums_ref, (row, row + 1))

        def total(ref, rows=None):
            def part(j):
                return ref[j] if rows is None else ref[j, rows[0]:rows[1], :]
            acc = part(0)
            for j in range(1, N_DEV):
                acc = acc + part(j)
            return acc

        sink_rows = total(misc_ref, (ROW_DSINKS, ROW_DSINKS + 16))
        diag = (lax.broadcasted_iota(jnp.int32, (16, 128), 0) == lax.broadcasted_iota(jnp.int32, (16, 128), 1))
        grads = dict(
            w_spatial=total(wsp_ref), b_spatial=total(misc_ref, (ROW_DBSP, ROW_DBSP + A_GROUPS)),
            sinks=jnp.sum(jnp.where(diag, sink_rows, 0.0), axis=0, keepdims=True),
            norm_g=column_sum(SUM_NORM_G), ln_v_g=total(ln_ref, (0, 1)), ln_v_b=total(ln_ref, (1, 2)),
            final_norm_g=column_sum(SUM_FNG),
            b_ada=jnp.concatenate([column_sum(SUM_SHIFT), column_sum(SUM_SCALE), column_sum(SUM_GATE)], axis=1),
            b_ada_final=jnp.concatenate([column_sum(SUM_SHIFT_F), column_sum(SUM_SCALE_F)], axis=1))
        sq_err = jnp.sum(column_sum(SUM_SQ_ERR), axis=1, keepdims=True)
        loss_ref[...] = jnp.broadcast_to(sq_err * (0.5 / D_MODEL), (1, 128))
        for k, name in enumerate(SMALL_PARAMS):
            w_ref, m_ref, v_ref = wmv[k]
            g_ref, d_ref, nm_ref, nv_ref = outs[k]
            g_ref[...] = grads[name]
            d_ref[...], nm_ref[...], nv_ref[...] = _adamw(w_ref[...], grads[name], m_ref[...], v_ref[...])

    flat = [a for name in SMALL_PARAMS for a in params[name]]
    vmem = pl.BlockSpec(memory_space=pltpu.VMEM)
    out_shape = [SDS((1, 128), F32)] + [SDS(params[name][0].shape, F32) for name in SMALL_PARAMS for _ in range(4)]
    outs = pl.pallas_call(
        body, name="adam_small", in_specs=[vmem] * (4 + len(flat)), out_specs=[vmem] * len(out_shape),
        out_shape=out_shape, compiler_params=_params(),
    )(d_wsp, misc, d_ln, sums, *flat)
    return outs[0], {name: outs[1 + 4 * k:5 + 4 * k] for k, name in enumerate(SMALL_PARAMS)}


def kernel(x, c, w_ada, b_ada, norm_g, w_in, ln_v_g, ln_v_b, w_spatial, b_spatial, sinks, w_out, w_ada_final, b_ada_final, final_norm_g, loss_target, m_w_ada, m_b_ada, m_norm_g, m_w_in, m_ln_v_g, m_ln_v_b, m_w_spatial, m_b_spatial, m_sinks, m_w_out, m_w_ada_final, m_b_ada_final, m_final_norm_g, v_w_ada, v_b_ada, v_norm_g, v_w_in, v_ln_v_g, v_ln_v_b, v_w_spatial, v_b_spatial, v_sinks, v_w_out, v_w_ada_final, v_b_ada_final, v_final_norm_g):
    me = 4 * lax.axis_index("x") + 2 * lax.axis_index("y") + lax.axis_index("c")
    x2, tgt = x[0], loss_target[0]
    fng = final_norm_g.reshape(1, D_MODEL)

    n_ada, n_ada_f = w_ada.shape[2], w_ada_final.shape[1]
    cact, mod, mod_f = _ada_exchange(c, w_ada[0], b_ada.reshape(N_DEV, n_ada), w_ada_final,
                                     b_ada_final.reshape(N_DEV, n_ada_f))
    cact = cact.reshape(N_DEV, D_MODEL)
    mod, mod_f = mod.reshape(1, 3 * D_MODEL), mod_f.reshape(1, 2 * D_MODEL)
    shift, scale, gate = mod[:, :D_MODEL], mod[:, D_MODEL:2 * D_MODEL], mod[:, 2 * D_MODEL:]
    shift_f, scale_f = mod_f[:, :D_MODEL], mod_f[:, D_MODEL:]

    wt_f32, m_wt, v_wt = (jnp.swapaxes(a, 1, 2)[0] for a in (w_in, m_w_in, v_w_in))
    xi, yi = lax.axis_index("x"), lax.axis_index("y")
    chip_order = jnp.stack([2 * xi + yi, 2 * (1 - xi) + yi, 2 * xi + 1 - yi, 2 * (1 - xi) + 1 - yi]).astype(jnp.int32)
    wt_mine, wo_mine = _prep_weights(me.reshape(1), wt_f32, w_out[0])

    freqs = _rope_freqs()
    sinks_v = sinks.reshape(16)
    h, proj, wt = _gather_in_proj(chip_order, x2, shift, scale, norm_g, wt_mine)
    y, wo = _mixer_fwd(proj, freqs, ln_v_g, ln_v_b, w_spatial[0], b_spatial[0], sinks_v, wo_mine)
    dx1, do, dy, sums_o = _out_proj_loss(y, x2, tgt, wo, gate, shift_f, scale_f, fng)

    chip = (2 * lax.axis_index("x") + lax.axis_index("y")).reshape(1)
    pair_out, _ = _wgrad_pair("wgrad_out", y, do, 2048)
    dproj, d_ln, d_wsp, misc, parts_out = _mixer_bwd(
        me.reshape(1), proj, dy, freqs, ln_v_g, ln_v_b, w_spatial[0], b_spatial[0], sinks_v, pair_out)
    pair_in, (d_ln, d_wsp, misc) = _wgrad_pair(
        "wgrad_in", dproj, h, 1024, gathers=(d_ln, d_wsp.reshape(N_DEV * A_GROUPS * CHUNK, CHUNK), misc))
    grad_x, sums, parts_in = _in_proj_bwd(dproj, wt, x2, dx1, scale, norm_g, sums_o, pair_in)
    wt_leaves = [jnp.swapaxes(a[None], 1, 2)
                 for a in _adam_rows("adam_w_in", chip, pair_in, parts_in, wt_f32, m_wt, v_wt, 176)]
    w_out_leaves = [a[None] for a in _adam_rows("adam_w_out", chip, pair_out, parts_out, w_out[0], m_w_out[0], v_w_out[0], 64)]

    (sums,) = _all_gather("gather_sums", [sums], pltpu.VMEM)
    natural = dict(w_spatial=(A_GROUPS * CHUNK, CHUNK), b_spatial=(A_GROUPS, CHUNK), sinks=(1, 16), norm_g=(1, D_MODEL),
                   ln_v_g=(1, D_A), ln_v_b=(1, D_A), final_norm_g=(1, D_MODEL), b_ada=(1, 3 * D_MODEL),
                   b_ada_final=(1, 2 * D_MODEL))
    given = dict(
        w_spatial=(w_spatial, m_w_spatial, v_w_spatial), b_spatial=(b_spatial, m_b_spatial, v_b_spatial),
        sinks=(sinks, m_sinks, v_sinks), norm_g=(norm_g, m_norm_g, v_norm_g), ln_v_g=(ln_v_g, m_ln_v_g, v_ln_v_g),
        ln_v_b=(ln_v_b, m_ln_v_b, v_ln_v_b), final_norm_g=(final_norm_g, m_final_norm_g, v_final_norm_g),
        b_ada=(b_ada, m_b_ada, v_b_ada), b_ada_final=(b_ada_final, m_b_ada_final, v_b_ada_final))
    params = {name: tuple(a.reshape(natural[name]) for a in given[name]) for name in SMALL_PARAMS}
    params["sinks"] = tuple(jnp.pad(a, ((0, 0), (0, 128 - 16))) for a in params["sinks"])
    loss, small = _adam_small(d_wsp.reshape(N_DEV, A_GROUPS * CHUNK, CHUNK), misc.reshape(N_DEV, MISC_ROWS, 128),
                              d_ln.reshape(N_DEV, 8, D_A), sums, params)
    small["sinks"] = [a[:, :16] for a in small["sinks"]]
    small = {name: [a.reshape(given[name][0].shape) for a in small[name]] for name in SMALL_PARAMS}

    dmod_all = jnp.concatenate([sums[:, SUM_SHIFT], sums[:, SUM_SCALE], sums[:, SUM_GATE]], axis=1)
    dmod_f_all = jnp.concatenate([sums[:, SUM_SHIFT_F], sums[:, SUM_SCALE_F]], axis=1)
    dmod_mine = lax.dynamic_slice_in_dim(dmod_all, me * n_ada, n_ada, axis=1)
    dmod_f_mine = lax.dynamic_slice_in_dim(dmod_f_all, me * n_ada_f, n_ada_f, axis=1)
    ada = _adam_ada("adam_w_ada", cact, dmod_mine, w_ada[0], m_w_ada[0], v_w_ada[0])
    ada_f = _adam_ada("adam_w_ada_final", cact, dmod_f_mine, w_ada_final, m_w_ada_final, v_w_ada_final)

    def leaves(k):
        return (ada[k][None], small["b_ada"][k], small["norm_g"][k], wt_leaves[k], small["ln_v_g"][k],
                small["ln_v_b"][k], small["w_spatial"][k], small["b_spatial"][k], small["sinks"][k], w_out_leaves[k],
                ada_f[k], small["b_ada_final"][k], small["final_norm_g"][k])

    return (loss[0, 0], grad_x[None], *leaves(0), *leaves(1), *leaves(2), *leaves(3))
```
